```python
import math
import jax, jax.numpy as jnp
from jax import lax
import numpy as np

D_MODEL = 1024
BATCH = 16
SEQ = 2048
DEPTH = 1

N_ATTN_HEADS = 8
HEAD_DIM = 64
ATTN_WIDTH = N_ATTN_HEADS * HEAD_DIM
N_CONV_GROUPS = 8
CONV_WIDTH = D_MODEL // 2
CONV_K = 3
D_FF = 2816
Q_BLOCK = 128
RMS_EPS = 1e-6
FFN_RESIDUAL_WEIGHT = 0.5
FORGET_BIAS_MEAN = 3.0

IN_SPLITS = (
    ATTN_WIDTH,
    ATTN_WIDTH,
    ATTN_WIDTH,
    N_ATTN_HEADS,
    CONV_WIDTH,
    CONV_WIDTH,
    CONV_WIDTH,
    D_MODEL,
    D_MODEL,
)
IN_COLS = sum(IN_SPLITS)

kernel_name = "fox_shortconv_gated_macaron_layer"


def rms_norm(x, g):
    xf = x.astype(jnp.float32)
    inv = lax.rsqrt(jnp.mean(xf * xf, axis=-1, keepdims=True) + RMS_EPS)
    return (xf * inv).astype(x.dtype) * g


def swiglu(x, w_gate, w_up, w_down):
    return (jax.nn.silu(x @ w_gate) * (x @ w_up)) @ w_down


def forgetting_attention(q, k, v, f_logits, b_forget):
    seq = q.shape[1]
    scale = 1.0 / math.sqrt(HEAD_DIM)
    log_f = jax.nn.log_sigmoid(f_logits.astype(jnp.float32) + b_forget.astype(jnp.float32))
    cum = jnp.transpose(jnp.cumsum(log_f, axis=1), (0, 2, 1))
    outs = []
    n_blocks = seq // Q_BLOCK
    for i in range(n_blocks):
        q0, q1 = i * Q_BLOCK, (i + 1) * Q_BLOCK
        kv_len = q1
        q_blk = q[:, q0:q1]
        k_pre = k[:, :kv_len]
        v_pre = v[:, :kv_len]
        s = jnp.einsum('bqhd,bkhd->bhqk', q_blk, k_pre).astype(jnp.float32) * scale
        s = s + cum[:, :, q0:q1, None] - cum[:, :, None, :kv_len]
        q_pos = jnp.arange(q0, q1)[:, None]
        k_pos = jnp.arange(kv_len)[None, :]
        s = jnp.where(q_pos >= k_pos, s, -jnp.inf)
        p = jax.nn.softmax(s, axis=-1).astype(v.dtype)
        outs.append(jnp.einsum('bhqk,bkhd->bqhd', p, v_pre))
    return jnp.concatenate(outs, axis=1)


def short_conv_mixer(xin, gate_b, gate_c, conv_w):
    seq = xin.shape[1]
    u = gate_c * xin
    up = jnp.pad(u, ((0, 0), (CONV_K - 1, 0), (0, 0)))
    conv = (conv_w[0] * up[:, 0:seq] + conv_w[1] * up[:, 1:seq + 1]
            + conv_w[2] * up[:, 2:seq + 2])
    return gate_b * conv


def _fwd_setup_inputs(seed: int = 0) -> dict:
    key = jax.random.key(seed)
    ks = jax.random.split(key, 20)
    f32 = jnp.float32

    def lin(k, fan_in, fan_out):
        return jax.random.normal(k, (fan_in, fan_out), f32) * fan_in ** -0.5

    def gain(k, n):
        return jnp.ones((n,), f32) + 0.02 * jax.random.normal(k, (n,), f32)

    return {
        "x": jax.random.normal(ks[0], (BATCH, SEQ, D_MODEL), f32),
        "ffn1_norm": gain(ks[1], D_MODEL),
        "ffn1_gate": lin(ks[2], D_MODEL, D_FF),
        "ffn1_up": lin(ks[3], D_MODEL, D_FF),
        "ffn1_down": lin(ks[4], D_FF, D_MODEL),
        "mix_norm": gain(ks[5], D_MODEL),
        "w_in": lin(ks[6], D_MODEL, IN_COLS),
        "b_forget": FORGET_BIAS_MEAN + 0.5 * jax.random.normal(ks[7], (N_ATTN_HEADS,), f32),
        "conv_w": 0.5 * jax.random.normal(ks[8], (CONV_K, CONV_WIDTH), f32),
        "w_o_attn": lin(ks[9], ATTN_WIDTH, D_MODEL),
        "w_o_conv": lin(ks[10], CONV_WIDTH, D_MODEL),
        "w_out": lin(ks[11], D_MODEL, D_MODEL),
        "ffn2_norm": gain(ks[12], D_MODEL),
        "ffn2_gate": lin(ks[13], D_MODEL, D_FF),
        "ffn2_up": lin(ks[14], D_MODEL, D_FF),
        "ffn2_down": lin(ks[15], D_FF, D_MODEL),
        "final_norm": gain(ks[16], D_MODEL),
    }


def _fwd_reference(x, ffn1_norm, ffn1_gate, ffn1_up, ffn1_down, mix_norm, w_in,
              b_forget, conv_w, w_o_attn, w_o_conv, w_out, ffn2_norm,
              ffn2_gate, ffn2_up, ffn2_down, final_norm):
    bsz, seq, _ = x.shape
    for _layer in range(DEPTH):
        x = x + FFN_RESIDUAL_WEIGHT * swiglu(rms_norm(x, ffn1_norm), ffn1_gate, ffn1_up, ffn1_down)

        h = rms_norm(x, mix_norm)
        proj = h @ w_in
        offsets = list(np.cumsum(IN_SPLITS)[:-1])
        q, k, v, f_log, c_b, c_c, c_x, g_attn, g_conv = jnp.split(proj, offsets, axis=-1)

        heads = lambda t: t.reshape(bsz, seq, N_ATTN_HEADS, HEAD_DIM)
        y_attn = forgetting_attention(heads(q), heads(k), heads(v), f_log, b_forget)
        y_attn = y_attn.reshape(bsz, seq, ATTN_WIDTH) @ w_o_attn

        y_conv = short_conv_mixer(c_x, c_b, c_c, conv_w) @ w_o_conv

        merged = jax.nn.sigmoid(g_attn) * y_attn + jax.nn.sigmoid(g_conv) * y_conv
        x = x + merged @ w_out

        x = x + FFN_RESIDUAL_WEIGHT * swiglu(rms_norm(x, ffn2_norm), ffn2_gate, ffn2_up, ffn2_down)
    return rms_norm(x, final_norm)


import jax as _jax
import jax.numpy as _jnp

TWIN_FORMAT = 'train_step'
FWD_PARAMS = ['x', 'ffn1_norm', 'ffn1_gate', 'ffn1_up', 'ffn1_down', 'mix_norm', 'w_in', 'b_forget', 'conv_w', 'w_o_attn', 'w_o_conv', 'w_out', 'ffn2_norm', 'ffn2_gate', 'ffn2_up', 'ffn2_down', 'final_norm']
TWIN_WEIGHTS = ['ffn1_norm', 'ffn1_gate', 'ffn1_up', 'ffn1_down', 'mix_norm', 'w_in', 'b_forget', 'conv_w', 'w_o_attn', 'w_o_conv', 'w_out', 'ffn2_norm', 'ffn2_gate', 'ffn2_up', 'ffn2_down', 'final_norm']
TWIN_DIFF_INPUT = 'x'
TWIN_INPUTS = ['x', 'ffn1_norm', 'ffn1_gate', 'ffn1_up', 'ffn1_down', 'mix_norm', 'w_in', 'b_forget', 'conv_w', 'w_o_attn', 'w_o_conv', 'w_out', 'ffn2_norm', 'ffn2_gate', 'ffn2_up', 'ffn2_down', 'final_norm', 'loss_target', 'm_ffn1_norm', 'm_ffn1_gate', 'm_ffn1_up', 'm_ffn1_down', 'm_mix_norm', 'm_w_in', 'm_b_forget', 'm_conv_w', 'm_w_o_attn', 'm_w_o_conv', 'm_w_out', 'm_ffn2_norm', 'm_ffn2_gate', 'm_ffn2_up', 'm_ffn2_down', 'm_final_norm', 'v_ffn1_norm', 'v_ffn1_gate', 'v_ffn1_up', 'v_ffn1_down', 'v_mix_norm', 'v_w_in', 'v_b_forget', 'v_conv_w', 'v_w_o_attn', 'v_w_o_conv', 'v_w_out', 'v_ffn2_norm', 'v_ffn2_gate', 'v_ffn2_up', 'v_ffn2_down', 'v_final_norm']
TWIN_OUTPUTS = ['loss', 'grad_x', 'grad_ffn1_norm', 'grad_ffn1_gate', 'grad_ffn1_up', 'grad_ffn1_down', 'grad_mix_norm', 'grad_w_in', 'grad_b_forget', 'grad_conv_w', 'grad_w_o_attn', 'grad_w_o_conv', 'grad_w_out', 'grad_ffn2_norm', 'grad_ffn2_gate', 'grad_ffn2_up', 'grad_ffn2_down', 'grad_final_norm', 'delta_ffn1_norm', 'delta_ffn1_gate', 'delta_ffn1_up', 'delta_ffn1_down', 'delta_mix_norm', 'delta_w_in', 'delta_b_forget', 'delta_conv_w', 'delta_w_o_attn', 'delta_w_o_conv', 'delta_w_out', 'delta_ffn2_norm', 'delta_ffn2_gate', 'delta_ffn2_up', 'delta_ffn2_down', 'delta_final_norm', 'new_m_ffn1_norm', 'new_m_ffn1_gate', 'new_m_ffn1_up', 'new_m_ffn1_down', 'new_m_mix_norm', 'new_m_w_in', 'new_m_b_forget', 'new_m_conv_w', 'new_m_w_o_attn', 'new_m_w_o_conv', 'new_m_w_out', 'new_m_ffn2_norm', 'new_m_ffn2_gate', 'new_m_ffn2_up', 'new_m_ffn2_down', 'new_m_final_norm', 'new_v_ffn1_norm', 'new_v_ffn1_gate', 'new_v_ffn1_up', 'new_v_ffn1_down', 'new_v_mix_norm', 'new_v_w_in', 'new_v_b_forget', 'new_v_conv_w', 'new_v_w_o_attn', 'new_v_w_o_conv', 'new_v_w_out', 'new_v_ffn2_norm', 'new_v_ffn2_gate', 'new_v_ffn2_up', 'new_v_ffn2_down', 'new_v_final_norm']
TWIN_LEAF_KINDS = {'loss': 'loss', 'grad_x': 'grad_x', 'grad_ffn1_norm': 'grad_w', 'grad_ffn1_gate': 'grad_w', 'grad_ffn1_up': 'grad_w', 'grad_ffn1_down': 'grad_w', 'grad_mix_norm': 'grad_w', 'grad_w_in': 'grad_w', 'grad_b_forget': 'grad_w', 'grad_conv_w': 'grad_w', 'grad_w_o_attn': 'grad_w', 'grad_w_o_conv': 'grad_w', 'grad_w_out': 'grad_w', 'grad_ffn2_norm': 'grad_w', 'grad_ffn2_gate': 'grad_w', 'grad_ffn2_up': 'grad_w', 'grad_ffn2_down': 'grad_w', 'grad_final_norm': 'grad_w', 'delta_ffn1_norm': 'delta_w', 'delta_ffn1_gate': 'delta_w', 'delta_ffn1_up': 'delta_w', 'delta_ffn1_down': 'delta_w', 'delta_mix_norm': 'delta_w', 'delta_w_in': 'delta_w', 'delta_b_forget': 'delta_w', 'delta_conv_w': 'delta_w', 'delta_w_o_attn': 'delta_w', 'delta_w_o_conv': 'delta_w', 'delta_w_out': 'delta_w', 'delta_ffn2_norm': 'delta_w', 'delta_ffn2_gate': 'delta_w', 'delta_ffn2_up': 'delta_w', 'delta_ffn2_down': 'delta_w', 'delta_final_norm': 'delta_w', 'new_m_ffn1_norm': 'new_m', 'new_m_ffn1_gate': 'new_m', 'new_m_ffn1_up': 'new_m', 'new_m_ffn1_down': 'new_m', 'new_m_mix_norm': 'new_m', 'new_m_w_in': 'new_m', 'new_m_b_forget': 'new_m', 'new_m_conv_w': 'new_m', 'new_m_w_o_attn': 'new_m', 'new_m_w_o_conv': 'new_m', 'new_m_w_out': 'new_m', 'new_m_ffn2_norm': 'new_m', 'new_m_ffn2_gate': 'new_m', 'new_m_ffn2_up': 'new_m', 'new_m_ffn2_down': 'new_m', 'new_m_final_norm': 'new_m', 'new_v_ffn1_norm': 'new_v', 'new_v_ffn1_gate': 'new_v', 'new_v_ffn1_up': 'new_v', 'new_v_ffn1_down': 'new_v', 'new_v_mix_norm': 'new_v', 'new_v_w_in': 'new_v', 'new_v_b_forget': 'new_v', 'new_v_conv_w': 'new_v', 'new_v_w_o_attn': 'new_v', 'new_v_w_o_conv': 'new_v', 'new_v_w_out': 'new_v', 'new_v_ffn2_norm': 'new_v', 'new_v_ffn2_gate': 'new_v', 'new_v_ffn2_up': 'new_v', 'new_v_ffn2_down': 'new_v', 'new_v_final_norm': 'new_v'}


def _forward(args):
    return _fwd_reference(*[args[k] for k in FWD_PARAMS])


def _output_shape():
    out = _jax.eval_shape(lambda: _forward(_fwd_setup_inputs(0)))
    return out.shape, out.dtype

N_MICROBATCH = 1
ADAM_LR = 0.001
ADAM_B1 = 0.9
ADAM_B2 = 0.999
ADAM_EPS = 1e-08
ADAM_WD = 0.01
ADAM_STEP = 10
PER_EXAMPLE_BATCH_AXIS = {'x': 0, 'loss_target': 0}
SHARED_INPUTS = []
_WEIGHT_DTYPES = {'ffn1_norm': _jnp.float32, 'ffn1_gate': _jnp.float32, 'ffn1_up': _jnp.float32, 'ffn1_down': _jnp.float32, 'mix_norm': _jnp.float32, 'w_in': _jnp.float32, 'b_forget': _jnp.float32, 'conv_w': _jnp.float32, 'w_o_attn': _jnp.float32, 'w_o_conv': _jnp.float32, 'w_out': _jnp.float32, 'ffn2_norm': _jnp.float32, 'ffn2_gate': _jnp.float32, 'ffn2_up': _jnp.float32, 'ffn2_down': _jnp.float32, 'final_norm': _jnp.float32}
MOMENT_SCALE = {'ffn1_norm': 9.157550e-02, 'ffn1_gate': 3.825172e-02, 'ffn1_up': 3.702969e-02, 'ffn1_down': 6.139400e-02, 'mix_norm': 1.408329e-01, 'w_in': 6.098350e-02, 'b_forget': 2.253998e-01, 'conv_w': 1.224026e-01, 'w_o_attn': 2.981281e-02, 'w_o_conv': 7.373733e-02, 'w_out': 7.779008e-02, 'ffn2_norm': 7.110593e-02, 'ffn2_gate': 2.775271e-02, 'ffn2_up': 2.690523e-02, 'ffn2_down': 4.457628e-02, 'final_norm': 3.202901e+01}


def _to_microbatches(a, axis):
    t = _jnp.moveaxis(a, axis, 0)
    t = t.reshape((N_MICROBATCH, t.shape[0] // N_MICROBATCH) + t.shape[1:])
    return _jnp.moveaxis(t, 1, axis + 1)


def setup_inputs(seed: int = 0) -> dict:
    inp = _fwd_setup_inputs(seed)
    key = _jax.random.fold_in(_jax.random.key(seed), 7919)
    shape, _ = _output_shape()
    out = dict(inp)
    out["loss_target"] = _jax.random.normal(_jax.random.fold_in(key, 0), shape, _jnp.float32)
    for i, name in enumerate(TWIN_WEIGHTS):
        w = inp[name].astype(_jnp.float32)
        if MOMENT_SCALE is None:
            s = _jnp.sqrt(_jnp.mean(_jnp.square(w)) + 1e-30)
        else:
            s = MOMENT_SCALE[name]
        km, kv = _jax.random.split(_jax.random.fold_in(key, i + 1))
        out[name] = w
        out["m_" + name] = s * _jax.random.normal(km, w.shape, _jnp.float32)
        out["v_" + name] = (s * s) * _jax.random.uniform(kv, w.shape, _jnp.float32, 0.5, 1.5)
    if N_MICROBATCH > 1:
        for name, axis in PER_EXAMPLE_BATCH_AXIS.items():
            out[name] = _to_microbatches(out[name], axis)
    return {'x': out['x'], 'ffn1_norm': out['ffn1_norm'], 'ffn1_gate': out['ffn1_gate'], 'ffn1_up': out['ffn1_up'], 'ffn1_down': out['ffn1_down'], 'mix_norm': out['mix_norm'], 'w_in': out['w_in'], 'b_forget': out['b_forget'], 'conv_w': out['conv_w'], 'w_o_attn': out['w_o_attn'], 'w_o_conv': out['w_o_conv'], 'w_out': out['w_out'], 'ffn2_norm': out['ffn2_norm'], 'ffn2_gate': out['ffn2_gate'], 'ffn2_up': out['ffn2_up'], 'ffn2_down': out['ffn2_down'], 'final_norm': out['final_norm'], 'loss_target': out['loss_target'], 'm_ffn1_norm': out['m_ffn1_norm'], 'm_ffn1_gate': out['m_ffn1_gate'], 'm_ffn1_up': out['m_ffn1_up'], 'm_ffn1_down': out['m_ffn1_down'], 'm_mix_norm': out['m_mix_norm'], 'm_w_in': out['m_w_in'], 'm_b_forget': out['m_b_forget'], 'm_conv_w': out['m_conv_w'], 'm_w_o_attn': out['m_w_o_attn'], 'm_w_o_conv': out['m_w_o_conv'], 'm_w_out': out['m_w_out'], 'm_ffn2_norm': out['m_ffn2_norm'], 'm_ffn2_gate': out['m_ffn2_gate'], 'm_ffn2_up': out['m_ffn2_up'], 'm_ffn2_down': out['m_ffn2_down'], 'm_final_norm': out['m_final_norm'], 'v_ffn1_norm': out['v_ffn1_norm'], 'v_ffn1_gate': out['v_ffn1_gate'], 'v_ffn1_up': out['v_ffn1_up'], 'v_ffn1_down': out['v_ffn1_down'], 'v_mix_norm': out['v_mix_norm'], 'v_w_in': out['v_w_in'], 'v_b_forget': out['v_b_forget'], 'v_conv_w': out['v_conv_w'], 'v_w_o_attn': out['v_w_o_attn'], 'v_w_o_conv': out['v_w_o_conv'], 'v_w_out': out['v_w_out'], 'v_ffn2_norm': out['v_ffn2_norm'], 'v_ffn2_gate': out['v_ffn2_gate'], 'v_ffn2_up': out['v_ffn2_up'], 'v_ffn2_down': out['v_ffn2_down'], 'v_final_norm': out['v_final_norm']}


def _loss(weights, diff, rest, loss_target):
    with _jax.named_scope("forward"):
        args = {**rest, TWIN_DIFF_INPUT: diff, **{k: w.astype(_WEIGHT_DTYPES[k]) for k, w in weights.items()}}
        y = _forward(args)
    with _jax.named_scope("loss_head"):
        err = _jnp.square(y.astype(_jnp.float32) - loss_target)
        return 0.5 * _jnp.sum(_jnp.mean(err, axis=-1)) if err.ndim else 0.5 * err


def _adamw(w, g, m, v):
    m = ADAM_B1 * m + (1.0 - ADAM_B1) * g
    v = ADAM_B2 * v + (1.0 - ADAM_B2) * _jnp.square(g)
    m_hat = m / (1.0 - ADAM_B1 ** ADAM_STEP)
    v_hat = v / (1.0 - ADAM_B2 ** ADAM_STEP)
    delta = -ADAM_LR * (m_hat / (_jnp.sqrt(v_hat) + ADAM_EPS) + ADAM_WD * w)
    return delta, m, v


def reference(x, ffn1_norm, ffn1_gate, ffn1_up, ffn1_down, mix_norm, w_in, b_forget, conv_w, w_o_attn, w_o_conv, w_out, ffn2_norm, ffn2_gate, ffn2_up, ffn2_down, final_norm, loss_target, m_ffn1_norm, m_ffn1_gate, m_ffn1_up, m_ffn1_down, m_mix_norm, m_w_in, m_b_forget, m_conv_w, m_w_o_attn, m_w_o_conv, m_w_out, m_ffn2_norm, m_ffn2_gate, m_ffn2_up, m_ffn2_down, m_final_norm, v_ffn1_norm, v_ffn1_gate, v_ffn1_up, v_ffn1_down, v_mix_norm, v_w_in, v_b_forget, v_conv_w, v_w_o_attn, v_w_o_conv, v_w_out, v_ffn2_norm, v_ffn2_gate, v_ffn2_up, v_ffn2_down, v_final_norm):
    given = dict(x=x, ffn1_norm=ffn1_norm, ffn1_gate=ffn1_gate, ffn1_up=ffn1_up, ffn1_down=ffn1_down, mix_norm=mix_norm, w_in=w_in, b_forget=b_forget, conv_w=conv_w, w_o_attn=w_o_attn, w_o_conv=w_o_conv, w_out=w_out, ffn2_norm=ffn2_norm, ffn2_gate=ffn2_gate, ffn2_up=ffn2_up, ffn2_down=ffn2_down, final_norm=final_norm, loss_target=loss_target, m_ffn1_norm=m_ffn1_norm, m_ffn1_gate=m_ffn1_gate, m_ffn1_up=m_ffn1_up, m_ffn1_down=m_ffn1_down, m_mix_norm=m_mix_norm, m_w_in=m_w_in, m_b_forget=m_b_forget, m_conv_w=m_conv_w, m_w_o_attn=m_w_o_attn, m_w_o_conv=m_w_o_conv, m_w_out=m_w_out, m_ffn2_norm=m_ffn2_norm, m_ffn2_gate=m_ffn2_gate, m_ffn2_up=m_ffn2_up, m_ffn2_down=m_ffn2_down, m_final_norm=m_final_norm, v_ffn1_norm=v_ffn1_norm, v_ffn1_gate=v_ffn1_gate, v_ffn1_up=v_ffn1_up, v_ffn1_down=v_ffn1_down, v_mix_norm=v_mix_norm, v_w_in=v_w_in, v_b_forget=v_b_forget, v_conv_w=v_conv_w, v_w_o_attn=v_w_o_attn, v_w_o_conv=v_w_o_conv, v_w_out=v_w_out, v_ffn2_norm=v_ffn2_norm, v_ffn2_gate=v_ffn2_gate, v_ffn2_up=v_ffn2_up, v_ffn2_down=v_ffn2_down, v_final_norm=v_final_norm)
    weights = {n: given[n] for n in TWIN_WEIGHTS}
    shared = {n: given[n] for n in SHARED_INPUTS}
    per_example = {n: given[n] for n in ['x']}
    grad_fn = _jax.value_and_grad(_loss, argnums=(0, 1))

    def one_microbatch(ex, loss_target):
        ex = dict(ex)
        diff = ex.pop(TWIN_DIFF_INPUT)
        return grad_fn(weights, diff, {**shared, **ex}, loss_target)

    if N_MICROBATCH == 1:
        loss, (grad_w, grad_x) = one_microbatch(per_example, given["loss_target"])
    else:
        def body(carry, xs):
            loss_sum, grad_sum = carry
            l_k, (gw_k, gx_k) = one_microbatch(xs[0], xs[1])
            with _jax.named_scope("update"):
                return (loss_sum + l_k, _jax.tree.map(_jnp.add, grad_sum, gw_k)), gx_k

        init = (_jnp.zeros((), _jnp.float32), _jax.tree.map(_jnp.zeros_like, weights))
        (loss, grad_w), grad_x = _jax.lax.scan(body, init, (per_example, given["loss_target"]))
    with _jax.named_scope("update"):
        delta_w, new_m, new_v = {}, {}, {}
        for n in TWIN_WEIGHTS:
            delta_w[n], new_m[n], new_v[n] = _adamw(weights[n], grad_w[n], given["m_" + n], given["v_" + n])
    return (loss, grad_x, *[grad_w[n] for n in TWIN_WEIGHTS], *[delta_w[n] for n in TWIN_WEIGHTS],
            *[new_m[n] for n in TWIN_WEIGHTS], *[new_v[n] for n in TWIN_WEIGHTS])
```

```python
import functools
import math

import jax
import jax.numpy as jnp
from jax import lax
from jax.experimental import pallas as pl
from jax.experimental.pallas import tpu as pltpu

F32 = jnp.float32
BF16 = jnp.bfloat16
MESH = pl.DeviceIdType.MESH

N_CHIPS = 4
N_DEV = 8
N_HEADS = 8
HEAD_DIM = 64
ATTN_W = N_HEADS * HEAD_DIM
CONV_W = 512
RMS_EPS = 1e-6
FFN_RES = 0.5
LANES = 128
VMEM_LIMIT = 56 * 1024 * 1024

ADAM_LR = 0.001
ADAM_B1 = 0.9
ADAM_B2 = 0.999
ADAM_EPS = 1e-08
ADAM_WD = 0.01
ADAM_STEP = 10

PROJ_W = 3 * ATTN_W + 3 * CONV_W + 2 * 1024
COL_CB, COL_CC, COL_CX = 3 * ATTN_W, 3 * ATTN_W + CONV_W, 3 * ATTN_W + 2 * CONV_W
COL_GATES = 3 * ATTN_W + 3 * CONV_W


def _params(sem=None, vmem=VMEM_LIMIT):
    return pltpu.CompilerParams(dimension_semantics=sem, vmem_limit_bytes=vmem)


def _dot(a, b):
    return lax.dot_general(a, b, (((1,), (0,)), ((), ())), preferred_element_type=F32)


def _dot_nt(a, b):
    return lax.dot_general(a, b, (((1,), (1,)), ((), ())), preferred_element_type=F32)


def _dot_tn(a, b):
    return lax.dot_general(a, b, (((0,), (0,)), ((), ())), preferred_element_type=F32)


def _sigmoid(x):
    return 1.0 / (1.0 + jnp.exp(-x))


def _rms(xv):
    inv = lax.rsqrt(jnp.mean(xv * xv, axis=-1, keepdims=True) + RMS_EPS)
    return xv * inv, inv


def _rms_bwd(dn, xhat, inv, g):
    dxhat = dn * g
    dx = inv * (dxhat - xhat * jnp.mean(dxhat * xhat, axis=-1, keepdims=True))
    return dx, jnp.sum(dn * xhat, axis=0, keepdims=True)


def _ffn_fwd(name, x, g, wg, wu, wd, tm):
    T, D = x.shape
    K, _, Fs = wg.shape

    def body(x_ref, g_ref, wg_ref, wu_ref, wd_ref, out_ref, hg_ref, hu_ref, n_scr, acc_scr):
        k = pl.program_id(1)

        @pl.when(k == 0)
        def _():
            xhat, _ = _rms(x_ref[...])
            n_scr[...] = (xhat * g_ref[...]).astype(BF16)
            acc_scr[...] = jnp.zeros_like(acc_scr)

        n = n_scr[...]
        hg = _dot(n, wg_ref[...])
        hu = _dot(n, wu_ref[...])
        hg_ref[...] = hg.astype(BF16)
        hu_ref[...] = hu.astype(BF16)
        act = (hg * _sigmoid(hg) * hu).astype(BF16)
        acc_scr[...] += _dot(act, wd_ref[...])

        @pl.when(k == K - 1)
        def _():
            out_ref[...] = x_ref[...] + FFN_RES * acc_scr[...]

    return pl.pallas_call(
        body, name=name, grid=(T // tm, K),
        in_specs=[pl.BlockSpec((tm, D), lambda i, k: (i, 0)),
                  pl.BlockSpec((1, D), lambda i, k: (0, 0)),
                  pl.BlockSpec((None, D, Fs), lambda i, k: (k, 0, 0)),
                  pl.BlockSpec((None, D, Fs), lambda i, k: (k, 0, 0)),
                  pl.BlockSpec((None, Fs, D), lambda i, k: (k, 0, 0))],
        out_specs=[pl.BlockSpec((tm, D), lambda i, k: (i, 0)),
                   pl.BlockSpec((None, tm, Fs), lambda i, k: (k, i, 0)),
                   pl.BlockSpec((None, tm, Fs), lambda i, k: (k, i, 0))],
        out_shape=[jax.ShapeDtypeStruct((T, D), F32),
                   jax.ShapeDtypeStruct((K, T, Fs), BF16),
                   jax.ShapeDtypeStruct((K, T, Fs), BF16)],
        scratch_shapes=[pltpu.VMEM((tm, D), BF16), pltpu.VMEM((tm, D), F32)],
        compiler_params=_params(("arbitrary", "arbitrary")),
    )(x, g, wg, wu, wd)


def _ffn_bwd_dx(name, dout, x, g, hg, hu, wg, wu, wd, tm):
    T, D = x.shape
    K, _, Fs = wg.shape

    def body(dout_ref, x_ref, g_ref, hg_ref, hu_ref, wg_ref, wu_ref, wd_ref,
             dx_ref, dhg_ref, dhu_ref, dg_ref, df_scr, dn_scr):
        i, k = pl.program_id(0), pl.program_id(1)

        @pl.when(k == 0)
        def _():
            df_scr[...] = (FFN_RES * dout_ref[...]).astype(BF16)
            dn_scr[...] = jnp.zeros_like(dn_scr)

        @pl.when((k == 0) & (i == 0))
        def _():
            dg_ref[...] = jnp.zeros_like(dg_ref)

        dact = _dot_nt(df_scr[...], wd_ref[...])
        hgv = hg_ref[...].astype(F32)
        huv = hu_ref[...].astype(F32)
        s = _sigmoid(hgv)
        dhu = (dact * (hgv * s)).astype(BF16)
        dhg = (dact * huv * (s * (1.0 + hgv * (1.0 - s)))).astype(BF16)
        dhg_ref[...] = dhg
        dhu_ref[...] = dhu
        dn_scr[...] += _dot_nt(dhg, wg_ref[...]) + _dot_nt(dhu, wu_ref[...])

        @pl.when(k == K - 1)
        def _():
            xhat, inv = _rms(x_ref[...])
            dx, dg = _rms_bwd(dn_scr[...], xhat, inv, g_ref[...])
            dx_ref[...] = dout_ref[...] + dx
            dg_ref[...] += dg

    return pl.pallas_call(
        body, name=name, grid=(T // tm, K),
        in_specs=[pl.BlockSpec((tm, D), lambda i, k: (i, 0)),
                  pl.BlockSpec((tm, D), lambda i, k: (i, 0)),
                  pl.BlockSpec((1, D), lambda i, k: (0, 0)),
                  pl.BlockSpec((None, tm, Fs), lambda i, k: (k, i, 0)),
                  pl.BlockSpec((None, tm, Fs), lambda i, k: (k, i, 0)),
                  pl.BlockSpec((None, D, Fs), lambda i, k: (k, 0, 0)),
                  pl.BlockSpec((None, D, Fs), lambda i, k: (k, 0, 0)),
                  pl.BlockSpec((None, Fs, D), lambda i, k: (k, 0, 0))],
        out_specs=[pl.BlockSpec((tm, D), lambda i, k: (i, 0)),
                   pl.BlockSpec((None, tm, Fs), lambda i, k: (k, i, 0)),
                   pl.BlockSpec((None, tm, Fs), lambda i, k: (k, i, 0)),
                   pl.BlockSpec((1, D), lambda i, k: (0, 0))],
        out_shape=[jax.ShapeDtypeStruct((T, D), F32),
                   jax.ShapeDtypeStruct((K, T, Fs), BF16),
                   jax.ShapeDtypeStruct((K, T, Fs), BF16),
                   jax.ShapeDtypeStruct((1, D), F32)],
        scratch_shapes=[pltpu.VMEM((tm, D), BF16), pltpu.VMEM((tm, D), F32)],
        compiler_params=_params(("arbitrary", "arbitrary")),
    )(dout, x, g, hg, hu, wg, wu, wd)


def _ffn_bwd_dw(name, dout, x, g, hg, hu, dhg, dhu, tk):
    T, D = x.shape
    K, _, Fs = hg.shape
    nt = T // tk

    def body(dout_ref, x_ref, g_ref, hg_ref, hu_ref, dhg_ref, dhu_ref,
             dwg_ref, dwu_ref, dwd_ref, accg, accu, accd):
        t = pl.program_id(1)

        @pl.when(t == 0)
        def _():
            accg[...] = jnp.zeros_like(accg)
            accu[...] = jnp.zeros_like(accu)
            accd[...] = jnp.zeros_like(accd)

        xhat, _ = _rms(x_ref[...])
        n = (xhat * g_ref[...]).astype(BF16)
        df = (FFN_RES * dout_ref[...]).astype(BF16)
        hgv = hg_ref[...].astype(F32)
        act = (hgv * _sigmoid(hgv) * hu_ref[...].astype(F32)).astype(BF16)
        accg[...] += _dot_tn(n, dhg_ref[...])
        accu[...] += _dot_tn(n, dhu_ref[...])
        accd[...] += _dot_tn(act, df)

        @pl.when(t == nt - 1)
        def _():
            dwg_ref[...] = accg[...].astype(BF16)
            dwu_ref[...] = accu[...].astype(BF16)
            dwd_ref[...] = accd[...].astype(BF16)

    act_spec = pl.BlockSpec((None, tk, Fs), lambda k, t: (k, t, 0))
    return pl.pallas_call(
        body, name=name, grid=(K, nt),
        in_specs=[pl.BlockSpec((tk, D), lambda k, t: (t, 0)),
                  pl.BlockSpec((tk, D), lambda k, t: (t, 0)),
                  pl.BlockSpec((1, D), lambda k, t: (0, 0)),
                  act_spec, act_spec, act_spec, act_spec],
        out_specs=[pl.BlockSpec((None, D, Fs), lambda k, t: (k, 0, 0)),
                   pl.BlockSpec((None, D, Fs), lambda k, t: (k, 0, 0)),
                   pl.BlockSpec((None, Fs, D), lambda k, t: (k, 0, 0))],
        out_shape=[jax.ShapeDtypeStruct((K, D, Fs), BF16),
                   jax.ShapeDtypeStruct((K, D, Fs), BF16),
                   jax.ShapeDtypeStruct((K, Fs, D), BF16)],
        scratch_shapes=[pltpu.VMEM((D, Fs), F32), pltpu.VMEM((D, Fs), F32), pltpu.VMEM((Fs, D), F32)],
        compiler_params=_params(("arbitrary", "arbitrary")),
    )(dout, x, g, hg, hu, dhg, dhu)


def _mix_proj_fwd(x, g, wproj, wf, tm, tn):
    T, D = x.shape
    N = wproj.shape[1]

    def body(x_ref, g_ref, w_ref, wf_ref, h_ref, proj_ref, flog_ref, h_scr):
        @pl.when(pl.program_id(1) == 0)
        def _():
            xhat, _ = _rms(x_ref[...])
            h = (xhat * g_ref[...]).astype(BF16)
            h_scr[...] = h
            h_ref[...] = h
            flog_ref[...] = _dot(h, wf_ref[...])

        proj_ref[...] = _dot(h_scr[...], w_ref[...]).astype(BF16)

    return pl.pallas_call(
        body, name="mix_proj_fwd", grid=(T // tm, N // tn),
        in_specs=[pl.BlockSpec((tm, D), lambda i, n: (i, 0)),
                  pl.BlockSpec((1, D), lambda i, n: (0, 0)),
                  pl.BlockSpec((D, tn), lambda i, n: (0, n)),
                  pl.BlockSpec((D, LANES), lambda i, n: (0, 0))],
        out_specs=[pl.BlockSpec((tm, D), lambda i, n: (i, 0)),
                   pl.BlockSpec((tm, tn), lambda i, n: (i, n)),
                   pl.BlockSpec((tm, LANES), lambda i, n: (i, 0))],
        out_shape=[jax.ShapeDtypeStruct((T, D), BF16),
                   jax.ShapeDtypeStruct((T, N), BF16),
                   jax.ShapeDtypeStruct((T, LANES), F32)],
        scratch_shapes=[pltpu.VMEM((tm, D), BF16)],
        compiler_params=_params(("arbitrary", "arbitrary")),
    )(x, g, wproj, wf)


def _log_sigmoid(z):
    return -(jnp.maximum(-z, 0.0) + jnp.log(1.0 + jnp.exp(-jnp.abs(z))))


def _tri(n, lower):
    r = lax.broadcasted_iota(jnp.int32, (n, n), 0)
    c = lax.broadcasted_iota(jnp.int32, (n, n), 1)
    return jnp.where((r >= c) if lower else (r <= c), 1.0, 0.0).astype(F32)


def _dot_f32(a, b):
    return lax.dot_general(a, b, (((1,), (0,)), ((), ())), preferred_element_type=F32,
                           precision=lax.Precision.HIGHEST)


def _fgate_fwd(flog, bias, B, S, ch):
    def body(flog_ref, b_ref, cum_ref):
        tri = _tri(ch, True)
        carry = jnp.zeros((1, LANES), F32)
        for c0 in range(0, S, ch):
            lf = _log_sigmoid(flog_ref[c0:c0 + ch, :] + b_ref[...])
            cs = _dot_f32(tri, lf) + carry
            cum_ref[c0:c0 + ch, :] = cs
            carry = cs[ch - 1:ch, :]

    return pl.pallas_call(
        body, name="fgate_fwd", grid=(B,),
        in_specs=[pl.BlockSpec((S, LANES), lambda b: (b, 0)),
                  pl.BlockSpec((1, LANES), lambda b: (0, 0))],
        out_specs=pl.BlockSpec((S, LANES), lambda b: (b, 0)),
        out_shape=jax.ShapeDtypeStruct((B * S, LANES), F32),
        compiler_params=_params(("arbitrary",)),
    )(flog, bias)


def _fgate_bwd(dcum, flog, bias, B, S, ch):
    def body(dcum_ref, flog_ref, b_ref, dflog_ref, db_ref):
        @pl.when(pl.program_id(0) == 0)
        def _():
            db_ref[...] = jnp.zeros_like(db_ref)

        tri = _tri(ch, False)
        carry = jnp.zeros((1, LANES), F32)
        db = jnp.zeros((1, LANES), F32)
        for c0 in range(S - ch, -1, -ch):
            dlf = _dot_f32(tri, dcum_ref[c0:c0 + ch, :]) + carry
            carry = dlf[0:1, :]
            z = flog_ref[c0:c0 + ch, :] + b_ref[...]
            dz = dlf * _sigmoid(-z)
            dflog_ref[c0:c0 + ch, :] = dz
            db = db + jnp.sum(dz, axis=0, keepdims=True)
        db_ref[...] += db

    return pl.pallas_call(
        body, name="fgate_bwd", grid=(B,),
        in_specs=[pl.BlockSpec((S, LANES), lambda b: (b, 0)),
                  pl.BlockSpec((S, LANES), lambda b: (b, 0)),
                  pl.BlockSpec((1, LANES), lambda b: (0, 0))],
        out_specs=[pl.BlockSpec((S, LANES), lambda b: (b, 0)),
                   pl.BlockSpec((1, LANES), lambda b: (0, 0))],
        out_shape=[jax.ShapeDtypeStruct((B * S, LANES), F32),
                   jax.ShapeDtypeStruct((1, LANES), F32)],
        compiler_params=_params(("arbitrary",)),
    )(dcum, flog, bias)


def _pick_lane(tile, h):
    lane = lax.broadcasted_iota(jnp.int32, tile.shape, 1)
    return jnp.sum(jnp.where(lane == h, tile, 0.0), axis=1, keepdims=True)


def _pick_row(tile, h):
    row = lax.broadcasted_iota(jnp.int32, tile.shape, 0)
    return jnp.sum(jnp.where(row == h, tile, 0.0), axis=0, keepdims=True)


def _causal(tq):
    r = lax.broadcasted_iota(jnp.int32, (tq, tq), 0)
    c = lax.broadcasted_iota(jnp.int32, (tq, tq), 1)
    return r >= c


NEG = -1e30


def _attn_fwd(q, k, v, cum, cum_t, tq):
    B, H, S, hd = q.shape
    scale = 1.0 / math.sqrt(hd)

    def body(q_ref, k_ref, v_ref, cum_ref, cumt_ref, o_ref, lse_ref):
        h, qi = pl.program_id(1), pl.program_id(2)
        qv = q_ref[...]
        cq = _pick_lane(cum_ref[...], h)

        def tile(j, carry, masked):
            m, l, acc = carry
            off = pl.multiple_of(j * tq, tq)
            kj = k_ref[pl.ds(off, tq), :]
            vj = v_ref[pl.ds(off, tq), :]
            ck = _pick_row(cumt_ref[j], h)
            s = _dot_nt(qv, kj) * scale + (cq - ck)
            if masked:
                s = jnp.where(_causal(tq), s, NEG)
            m_new = jnp.maximum(m, jnp.max(s, axis=1, keepdims=True))
            p = jnp.exp(s - m_new)
            alpha = jnp.exp(m - m_new)
            l = alpha * l + jnp.sum(p, axis=1, keepdims=True)
            acc = alpha * acc + _dot(p.astype(BF16), vj)
            return m_new, l, acc

        init = (jnp.full((tq, 1), NEG, F32), jnp.zeros((tq, 1), F32), jnp.zeros((tq, hd), F32))
        carry = lax.fori_loop(0, qi, lambda j, c: tile(j, c, False), init)
        m, l, acc = tile(qi, carry, True)
        o_ref[...] = (acc / l).astype(BF16)
        lse_ref[...] = m + jnp.log(l)

    return pl.pallas_call(
        body, name="attn_fwd", grid=(B, H, S // tq),
        in_specs=[pl.BlockSpec((None, None, tq, hd), lambda b, h, i: (b, h, i, 0)),
                  pl.BlockSpec((None, None, S, hd), lambda b, h, i: (b, h, 0, 0)),
                  pl.BlockSpec((None, None, S, hd), lambda b, h, i: (b, h, 0, 0)),
                  pl.BlockSpec((None, tq, LANES), lambda b, h, i: (b, i, 0)),
                  pl.BlockSpec((None, S // tq, 8, tq), lambda b, h, i: (b, 0, 0, 0))],
        out_specs=[pl.BlockSpec((None, None, tq, hd), lambda b, h, i: (b, h, i, 0)),
                   pl.BlockSpec((None, None, tq, 1), lambda b, h, i: (b, h, i, 0))],
        out_shape=[jax.ShapeDtypeStruct((B, H, S, hd), BF16),
                   jax.ShapeDtypeStruct((B, H, S, 1), F32)],
        compiler_params=_params(("arbitrary", "arbitrary", "arbitrary")),
    )(q, k, v, cum, cum_t)


def _attn_bwd(q, k, v, o, do, lse, cum, cum_t, tq):
    B, H, S, hd = q.shape
    nq = S // tq
    scale = 1.0 / math.sqrt(hd)

    def body(q_ref, k_ref, v_ref, o_ref, do_ref, lse_ref, cum_ref, cumt_ref,
             dq_ref, dk_ref, dv_ref, dcq_ref, dck_ref, dq_scr, dcq_scr):
        h, kj = pl.program_id(1), pl.program_id(2)

        @pl.when(kj == 0)
        def _():
            dq_scr[...] = jnp.zeros_like(dq_scr)
            dcq_scr[...] = jnp.zeros_like(dcq_scr)

        kv = k_ref[...]
        vv = v_ref[...]
        ck = _pick_row(cumt_ref[...], h)

        def tile(i, carry, masked):
            dk, dv, dcol = carry
            off = pl.multiple_of(i * tq, tq)
            qi = q_ref[pl.ds(off, tq), :]
            doi = do_ref[pl.ds(off, tq), :]
            delta = jnp.sum(doi.astype(F32) * o_ref[pl.ds(off, tq), :].astype(F32), axis=1, keepdims=True)
            cq = _pick_lane(cum_ref[pl.ds(off, tq), :], h)
            s = _dot_nt(qi, kv) * scale + (cq - ck)
            p = jnp.exp(s - lse_ref[pl.ds(off, tq), :])
            if masked:
                p = jnp.where(_causal(tq), p, 0.0)
            dv = dv + _dot_tn(p.astype(BF16), doi)
            ds = p * (_dot_nt(doi, vv) - delta)
            dcol = dcol + jnp.sum(ds, axis=0, keepdims=True)
            dcq_scr[pl.ds(off, tq), :] += jnp.sum(ds, axis=1, keepdims=True)
            dsb = ds.astype(BF16)
            dk = dk + _dot_tn(dsb, qi) * scale
            dq_scr[pl.ds(off, tq), :] += _dot(dsb, kv) * scale
            return dk, dv, dcol

        init = (jnp.zeros((tq, hd), F32), jnp.zeros((tq, hd), F32), jnp.zeros((1, tq), F32))
        carry = tile(kj, init, True)
        dk, dv, dcol = lax.fori_loop(kj + 1, nq, lambda i, c: tile(i, c, False), carry)
        dk_ref[...] = dk.astype(BF16)
        dv_ref[...] = dv.astype(BF16)
        dck_ref[...] = -dcol

        @pl.when(kj == nq - 1)
        def _():
            dq_ref[...] = dq_scr[...].astype(BF16)
            dcq_ref[...] = dcq_scr[...]

    full = pl.BlockSpec((None, None, S, hd), lambda b, h, j: (b, h, 0, 0))
    tile_spec = pl.BlockSpec((None, None, tq, hd), lambda b, h, j: (b, h, j, 0))
    return pl.pallas_call(
        body, name="attn_bwd", grid=(B, H, nq),
        in_specs=[full, tile_spec, tile_spec, full, full,
                  pl.BlockSpec((None, None, S, 1), lambda b, h, j: (b, h, 0, 0)),
                  pl.BlockSpec((None, S, LANES), lambda b, h, j: (b, 0, 0)),
                  pl.BlockSpec((None, None, 8, tq), lambda b, h, j: (b, j, 0, 0))],
        out_specs=[full, tile_spec, tile_spec,
                   pl.BlockSpec((None, None, S, 1), lambda b, h, j: (b, h, 0, 0)),
                   pl.BlockSpec((None, None, 1, tq), lambda b, h, j: (b, h, 0, j))],
        out_shape=[jax.ShapeDtypeStruct((B, H, S, hd), BF16),
                   jax.ShapeDtypeStruct((B, H, S, hd), BF16),
                   jax.ShapeDtypeStruct((B, H, S, hd), BF16),
                   jax.ShapeDtypeStruct((B, H, S, 1), F32),
                   jax.ShapeDtypeStruct((B, H, 1, S), F32)],
        scratch_shapes=[pltpu.VMEM((S, hd), F32), pltpu.VMEM((S, 1), F32)],
        compiler_params=_params(("arbitrary", "arbitrary", "arbitrary")),
    )(q, k, v, o, do, lse, cum, cum_t)


def _shift_down(u, n):
    row = lax.broadcasted_iota(jnp.int32, u.shape, 0)
    return jnp.where(row >= n, pltpu.roll(u, n, 0), 0.0)


def _shift_up(u, n):
    rows = u.shape[0]
    row = lax.broadcasted_iota(jnp.int32, u.shape, 0)
    return jnp.where(row < rows - n, pltpu.roll(u, rows - n, 0), 0.0)


def _conv_specs(S):
    cb = pl.BlockSpec((S, LANES), lambda g, b: (b, COL_CB // LANES + g))
    cc = pl.BlockSpec((S, LANES), lambda g, b: (b, COL_CC // LANES + g))
    cx = pl.BlockSpec((S, LANES), lambda g, b: (b, COL_CX // LANES + g))
    w = pl.BlockSpec((8, LANES), lambda g, b: (0, g))
    return cb, cc, cx, w


def _conv_fwd(proj, conv_w, B, S):
    def body(cb_ref, cc_ref, cx_ref, w_ref, y_ref):
        u = cc_ref[...].astype(F32) * cx_ref[...].astype(F32)
        w = w_ref[...]
        conv = w[0:1, :] * _shift_down(u, 2) + w[1:2, :] * _shift_down(u, 1) + w[2:3, :] * u
        y_ref[...] = (cb_ref[...].astype(F32) * conv).astype(BF16)

    cb, cc, cx, w = _conv_specs(S)
    return pl.pallas_call(
        body, name="conv_fwd", grid=(CONV_W // LANES, B),
        in_specs=[cb, cc, cx, w],
        out_specs=pl.BlockSpec((S, LANES), lambda g, b: (b, g)),
        out_shape=jax.ShapeDtypeStruct((B * S, CONV_W), BF16),
        compiler_params=_params(("arbitrary", "arbitrary")),
    )(proj, proj, proj, conv_w)


def _conv_bwd(dy, proj, conv_w, B, S):
    def body(dy_ref, cb_ref, cc_ref, cx_ref, w_ref, dcb_ref, dcc_ref, dcx_ref, dw_ref):
        @pl.when(pl.program_id(1) == 0)
        def _():
            dw_ref[...] = jnp.zeros_like(dw_ref)

        ccv = cc_ref[...].astype(F32)
        cxv = cx_ref[...].astype(F32)
        u = ccv * cxv
        u1 = _shift_down(u, 1)
        u2 = _shift_down(u, 2)
        w = w_ref[...]
        conv = w[0:1, :] * u2 + w[1:2, :] * u1 + w[2:3, :] * u
        dyv = dy_ref[...].astype(F32)
        dcb_ref[...] = (dyv * conv).astype(BF16)
        dconv = dyv * cb_ref[...].astype(F32)
        du = w[2:3, :] * dconv + w[1:2, :] * _shift_up(dconv, 1) + w[0:1, :] * _shift_up(dconv, 2)
        dcc_ref[...] = (du * cxv).astype(BF16)
        dcx_ref[...] = (du * ccv).astype(BF16)
        row = lax.broadcasted_iota(jnp.int32, (8, LANES), 0)
        dw = jnp.where(row == 0, jnp.sum(dconv * u2, axis=0, keepdims=True),
                       jnp.where(row == 1, jnp.sum(dconv * u1, axis=0, keepdims=True),
                                 jnp.where(row == 2, jnp.sum(dconv * u, axis=0, keepdims=True), 0.0)))
        dw_ref[...] += dw

    cb, cc, cx, w = _conv_specs(S)
    out = pl.BlockSpec((S, LANES), lambda g, b: (b, g))
    return pl.pallas_call(
        body, name="conv_bwd", grid=(CONV_W // LANES, B),
        in_specs=[out, cb, cc, cx, w],
        out_specs=[out, out, out, w],
        out_shape=[jax.ShapeDtypeStruct((B * S, CONV_W), BF16)] * 3 + [jax.ShapeDtypeStruct((8, CONV_W), F32)],
        compiler_params=_params(("arbitrary", "arbitrary")),
    )(dy, proj, proj, proj, conv_w)


def _gate_specs(tm, D):
    ga = pl.BlockSpec((tm, D), lambda i: (i, COL_GATES // D))
    gc = pl.BlockSpec((tm, D), lambda i: (i, COL_GATES // D + 1))
    return ga, gc


def _mix_out_fwd(x, o, yc, proj, woa, woc, wout, tm):
    T, D = x.shape

    def body(x_ref, o_ref, yc_ref, ga_ref, gc_ref, woa_ref, woc_ref, wout_ref, out_ref):
        ya = _dot(o_ref[...], woa_ref[...])
        yp = _dot(yc_ref[...], woc_ref[...])
        merged = _sigmoid(ga_ref[...].astype(F32)) * ya + _sigmoid(gc_ref[...].astype(F32)) * yp
        out_ref[...] = x_ref[...] + _dot(merged.astype(BF16), wout_ref[...])

    ga, gc = _gate_specs(tm, D)
    row = lambda w: pl.BlockSpec((tm, w), lambda i: (i, 0))
    whole = lambda a: pl.BlockSpec(a.shape, lambda i: (0, 0))
    return pl.pallas_call(
        body, name="mix_out_fwd", grid=(T // tm,),
        in_specs=[row(D), row(ATTN_W), row(CONV_W), ga, gc, whole(woa), whole(woc), whole(wout)],
        out_specs=row(D),
        out_shape=jax.ShapeDtypeStruct((T, D), F32),
        compiler_params=_params(("arbitrary",)),
    )(x, o, yc, proj, proj, woa, woc, wout)


def _mix_out_bwd(dx, o, yc, proj, woa, woc, wout, tm):
    T, D = dx.shape
    nt = T // tm

    def body(dx_ref, o_ref, yc_ref, ga_ref, gc_ref, woa_ref, woc_ref, wout_ref,
             do_ref, dyc_ref, dg_ref, dwoa_ref, dwoc_ref, dwout_ref, acca, accc, acco):
        t = pl.program_id(0)

        @pl.when(t == 0)
        def _():
            acca[...] = jnp.zeros_like(acca)
            accc[...] = jnp.zeros_like(accc)
            acco[...] = jnp.zeros_like(acco)

        dxb = dx_ref[...].astype(BF16)
        ov, ycv = o_ref[...], yc_ref[...]
        ya = _dot(ov, woa_ref[...])
        yp = _dot(ycv, woc_ref[...])
        sa = _sigmoid(ga_ref[...].astype(F32))
        sc = _sigmoid(gc_ref[...].astype(F32))
        merged = (sa * ya + sc * yp).astype(BF16)
        dm = _dot_nt(dxb, wout_ref[...])
        dya = (dm * sa).astype(BF16)
        dyp = (dm * sc).astype(BF16)
        dg_ref[:, :D] = (dm * ya * sa * (1.0 - sa)).astype(BF16)
        dg_ref[:, D:] = (dm * yp * sc * (1.0 - sc)).astype(BF16)
        do_ref[...] = _dot_nt(dya, woa_ref[...]).astype(BF16)
        dyc_ref[...] = _dot_nt(dyp, woc_ref[...]).astype(BF16)
        acca[...] += _dot_tn(ov, dya)
        accc[...] += _dot_tn(ycv, dyp)
        acco[...] += _dot_tn(merged, dxb)

        @pl.when(t == nt - 1)
        def _():
            dwoa_ref[...] = acca[...].astype(BF16)
            dwoc_ref[...] = accc[...].astype(BF16)
            dwout_ref[...] = acco[...].astype(BF16)

    ga, gc = _gate_specs(tm, D)
    row = lambda w: pl.BlockSpec((tm, w), lambda i: (i, 0))
    whole = lambda a: pl.BlockSpec(a.shape, lambda i: (0, 0))
    return pl.pallas_call(
        body, name="mix_out_bwd", grid=(nt,),
        in_specs=[row(D), row(ATTN_W), row(CONV_W), ga, gc, whole(woa), whole(woc), whole(wout)],
        out_specs=[row(ATTN_W), row(CONV_W), row(2 * D), whole(woa), whole(woc), whole(wout)],
        out_shape=[jax.ShapeDtypeStruct((T, ATTN_W), BF16), jax.ShapeDtypeStruct((T, CONV_W), BF16),
                   jax.ShapeDtypeStruct((T, 2 * D), BF16),
                   jax.ShapeDtypeStruct(woa.shape, BF16), jax.ShapeDtypeStruct(woc.shape, BF16),
                   jax.ShapeDtypeStruct(wout.shape, BF16)],
        scratch_shapes=[pltpu.VMEM(woa.shape, F32), pltpu.VMEM(woc.shape, F32), pltpu.VMEM(wout.shape, F32)],
        compiler_params=_params(("arbitrary",)),
    )(dx, o, yc, proj, proj, woa, woc, wout)


def _mix_proj_bwd_dx(dres, x, g, pieces, weights, tm):
    T, D = x.shape
    n = len(pieces)

    def body(*refs):
        dres_ref, x_ref, g_ref = refs[:3]
        p_refs, w_refs = refs[3:3 + n], refs[3 + n:3 + 2 * n]
        dx_ref, dg_ref = refs[3 + 2 * n:]

        @pl.when(pl.program_id(0) == 0)
        def _():
            dg_ref[...] = jnp.zeros_like(dg_ref)

        dh = _dot_nt(p_refs[0][...].astype(BF16), w_refs[0][...])
        for p_ref, w_ref in zip(p_refs[1:], w_refs[1:]):
            dh = dh + _dot_nt(p_ref[...].astype(BF16), w_ref[...])
        xhat, inv = _rms(x_ref[...])
        dx, dg = _rms_bwd(dh, xhat, inv, g_ref[...])
        dx_ref[...] = dres_ref[...] + dx
        dg_ref[...] += dg

    row = lambda w: pl.BlockSpec((tm, w), lambda i: (i, 0))
    whole = lambda a: pl.BlockSpec(a.shape, lambda i: (0, 0))
    return pl.pallas_call(
        body, name="mix_proj_bwd_dx", grid=(T // tm,),
        in_specs=[row(D), row(D), whole(g)] + [row(p.shape[1]) for p in pieces] + [whole(w) for w in weights],
        out_specs=[row(D), whole(g)],
        out_shape=[jax.ShapeDtypeStruct((T, D), F32), jax.ShapeDtypeStruct((1, D), F32)],
        compiler_params=_params(("arbitrary",)),
    )(dres, x, g, *pieces, *weights)


def _matmul_tn(name, a, b, tn, tk):
    T, M = a.shape
    N = b.shape[1]
    nt = T // tk

    def body(a_ref, b_ref, out_ref, acc):
        t = pl.program_id(1)

        @pl.when(t == 0)
        def _():
            acc[...] = jnp.zeros_like(acc)

        acc[...] += _dot_tn(a_ref[...], b_ref[...].astype(BF16))

        @pl.when(t == nt - 1)
        def _():
            out_ref[...] = acc[...].astype(BF16)

    return pl.pallas_call(
        body, name=name, grid=(N // tn, nt),
        in_specs=[pl.BlockSpec((tk, M), lambda n, t: (t, 0)),
                  pl.BlockSpec((tk, tn), lambda n, t: (t, n))],
        out_specs=pl.BlockSpec((M, tn), lambda n, t: (0, n)),
        out_shape=jax.ShapeDtypeStruct((M, N), BF16),
        scratch_shapes=[pltpu.VMEM((M, tn), F32)],
        compiler_params=_params(("arbitrary", "arbitrary")),
    )(a, b)


def _final_loss(x, target, g, tm):
    T, D = x.shape

    def body(x_ref, t_ref, g_ref, dx_ref, loss_ref, dg_ref):
        @pl.when(pl.program_id(0) == 0)
        def _():
            loss_ref[...] = jnp.zeros_like(loss_ref)
            dg_ref[...] = jnp.zeros_like(dg_ref)

        xhat, inv = _rms(x_ref[...])
        err = xhat * g_ref[...] - t_ref[...]
        loss_ref[...] += 0.5 * jnp.sum(jnp.sum(err * err, axis=1, keepdims=True), axis=0, keepdims=True) / D
        dx, dg = _rms_bwd(err * (1.0 / D), xhat, inv, g_ref[...])
        dx_ref[...] = dx
        dg_ref[...] += dg

    row = pl.BlockSpec((tm, D), lambda i: (i, 0))
    return pl.pallas_call(
        body, name="final_loss", grid=(T // tm,),
        in_specs=[row, row, pl.BlockSpec((1, D), lambda i: (0, 0))],
        out_specs=[row, pl.BlockSpec((1, LANES), lambda i: (0, 0)), pl.BlockSpec((1, D), lambda i: (0, 0))],
        out_shape=[jax.ShapeDtypeStruct((T, D), F32), jax.ShapeDtypeStruct((1, LANES), F32),
                   jax.ShapeDtypeStruct((1, D), F32)],
        compiler_params=_params(("arbitrary",)),
    )(x, target, g)


def _heads(t, B, S):
    return jnp.transpose(t.reshape(B, S, N_HEADS, HEAD_DIM), (0, 2, 1, 3))


def _unheads(t, B, S):
    return jnp.transpose(t, (0, 2, 1, 3)).reshape(B * S, ATTN_W)


def _local_step(x, target, w, B, S):
    T, D = x.shape
    tm = min(512, T)
    tq = min(512, S)
    ch = min(256, S)

    x1, hg1, hu1 = _ffn_fwd("ffn1_fwd", x, w["ffn1_norm"], w["ffn1_gate"], w["ffn1_up"], w["ffn1_down"], tm)
    h, proj, flog = _mix_proj_fwd(x1, w["mix_norm"], w["w_proj"], w["w_f"], tm, 1280)
    cum = _fgate_fwd(flog, w["b_forget"], B, S, ch)
    cum3 = cum.reshape(B, S, LANES)
    cum_t = jnp.transpose(cum3[:, :, :8].reshape(B, S // tq, tq, 8), (0, 1, 3, 2))
    q = _heads(proj[:, 0:ATTN_W], B, S)
    k = _heads(proj[:, ATTN_W:2 * ATTN_W], B, S)
    v = _heads(proj[:, 2 * ATTN_W:3 * ATTN_W], B, S)
    o4, lse = _attn_fwd(q, k, v, cum3, cum_t, tq)
    o = _unheads(o4, B, S)
    yc = _conv_fwd(proj, w["conv_w"], B, S)
    x2 = _mix_out_fwd(x1, o, yc, proj, w["w_o_attn"], w["w_o_conv"], w["w_out"], tm)
    x3, hg2, hu2 = _ffn_fwd("ffn2_fwd", x2, w["ffn2_norm"], w["ffn2_gate"], w["ffn2_up"], w["ffn2_down"], tm)
    dx3, loss, d_final_norm = _final_loss(x3, target, w["final_norm"], tm)

    g = {"final_norm": d_final_norm}
    dx2, dhg2, dhu2, g["ffn2_norm"] = _ffn_bwd_dx("ffn2_bwd_dx", dx3, x2, w["ffn2_norm"], hg2, hu2,
                                                  w["ffn2_gate"], w["ffn2_up"], w["ffn2_down"], tm)
    g["ffn2_gate"], g["ffn2_up"], g["ffn2_down"] = _ffn_bwd_dw("ffn2_bwd_dw", dx3, x2, w["ffn2_norm"],
                                                               hg2, hu2, dhg2, dhu2, tm)
    do, dyc, dgates, g["w_o_attn"], g["w_o_conv"], g["w_out"] = _mix_out_bwd(
        dx2, o, yc, proj, w["w_o_attn"], w["w_o_conv"], w["w_out"], tm)
    dq4, dk4, dv4, dcq, dck = _attn_bwd(q, k, v, o4, _heads(do, B, S), lse, cum3, cum_t, tq)
    dcum = jnp.pad(jnp.transpose(dcq.reshape(B, N_HEADS, S) + dck.reshape(B, N_HEADS, S), (0, 2, 1)).reshape(T, N_HEADS),
                   ((0, 0), (0, LANES - N_HEADS)))
    dflog, g["b_forget"] = _fgate_bwd(dcum, flog, w["b_forget"], B, S, ch)
    dcb, dcc, dcx, g["conv_w"] = _conv_bwd(dyc, proj, w["conv_w"], B, S)
    dqkv = jnp.concatenate([_unheads(dq4, B, S), _unheads(dk4, B, S), _unheads(dv4, B, S)], axis=1)
    wp = w["w_proj"]
    pieces = [dqkv, dcb, dcc, dcx, dgates, dflog]
    weights = [wp[:, 0:COL_CB], wp[:, COL_CB:COL_CC], wp[:, COL_CC:COL_CX], wp[:, COL_CX:COL_GATES],
               wp[:, COL_GATES:], w["w_f"]]
    tmx = min(256, T)
    dx1, g["mix_norm"] = _mix_proj_bwd_dx(dx2, x1, w["mix_norm"], pieces, weights, tmx)
    g["w_proj"] = [_matmul_tn("mix_dw_%d" % j, h, p, min(512, p.shape[1]), tm) for j, p in enumerate(pieces)]
    grad_x, dhg1, dhu1, g["ffn1_norm"] = _ffn_bwd_dx("ffn1_bwd_dx", dx1, x, w["ffn1_norm"], hg1, hu1,
                                                     w["ffn1_gate"], w["ffn1_up"], w["ffn1_down"], tm)
    g["ffn1_gate"], g["ffn1_up"], g["ffn1_down"] = _ffn_bwd_dw("ffn1_bwd_dw", dx1, x, w["ffn1_norm"],
                                                               hg1, hu1, dhg1, dhu1, tm)
    return loss, grad_x, g


N_FORGET_COL = 3 * ATTN_W
COLUMN_SHARDED = ("ffn1_gate", "ffn1_up", "ffn2_gate", "ffn2_up", "w_in", "w_o_attn", "w_o_conv")
ROW_SHARDED = ("ffn1_down", "ffn2_down", "w_out")
NORMS = ("ffn1_norm", "mix_norm", "ffn2_norm", "final_norm")


def _unshard_cols(a):
    return jnp.transpose(a, (1, 0, 2)).reshape(a.shape[1], N_CHIPS * a.shape[2])


def _shard_cols(a):
    return jnp.transpose(a.reshape(a.shape[0], N_CHIPS, a.shape[1] // N_CHIPS), (1, 0, 2))


def _compute_layout(gw, small):
    w = {n: gw[n] for n in ("ffn1_gate", "ffn1_up", "ffn1_down", "ffn2_gate", "ffn2_up", "ffn2_down")}
    win = _unshard_cols(gw["w_in"])
    w["w_proj"] = jnp.concatenate([win[:, :N_FORGET_COL], win[:, N_FORGET_COL + N_HEADS:]], axis=1)
    w["w_f"] = jnp.pad(win[:, N_FORGET_COL:N_FORGET_COL + N_HEADS], ((0, 0), (0, LANES - N_HEADS)))
    w["w_o_attn"] = _unshard_cols(gw["w_o_attn"])
    w["w_o_conv"] = _unshard_cols(gw["w_o_conv"])
    w["w_out"] = gw["w_out"].reshape(-1, gw["w_out"].shape[2])
    w["conv_w"] = _unshard_cols(gw["conv_w"])
    for n in NORMS:
        w[n] = small[n].reshape(1, -1)
    w["b_forget"] = jnp.pad(small["b_forget"].reshape(1, -1), ((0, 0), (0, LANES - N_HEADS)))
    return w


def _grads_to_shard_major(g):
    out = {n: g[n] for n in ("ffn1_gate", "ffn1_up", "ffn1_down", "ffn2_gate", "ffn2_up", "ffn2_down")}
    dqkv, dcb, dcc, dcx, dgates, df = g["w_proj"]
    out["w_in"] = _shard_cols(jnp.concatenate([dqkv, df[:, :N_HEADS], dcb, dcc, dcx, dgates], axis=1))
    out["w_o_attn"] = _shard_cols(g["w_o_attn"])
    out["w_o_conv"] = _shard_cols(g["w_o_conv"])
    out["w_out"] = g["w_out"].reshape(N_CHIPS, -1, g["w_out"].shape[1])
    for n in NORMS + ("b_forget", "conv_w"):
        out[n] = g[n]
    return out


ANY = pl.BlockSpec(memory_space=pl.ANY)
BIG = ("ffn1_gate", "ffn1_up", "ffn1_down", "w_in", "w_o_attn", "w_o_conv", "w_out",
       "ffn2_gate", "ffn2_up", "ffn2_down")


def _place():
    x, y, c = lax.axis_index("x"), lax.axis_index("y"), lax.axis_index("c")
    others = [(1 - x, y), (x, 1 - y), (1 - x, 1 - y)]
    return x, y, c, others


def _gather_weights(shards, conv_shard):
    n = len(shards)

    def body(*refs):
        srcs, conv_src = refs[:n], refs[n]
        dsts, conv_dst = refs[n + 1:2 * n + 1], refs[2 * n + 1]
        send_sems, recv_sems, pass_send, pass_recv, own_sems, conv_send, conv_recv = refs[2 * n + 2:]
        x, y, c, others = _place()
        me = 2 * x + y

        def halves(a):
            hr = srcs[a].shape[0] // 2
            return pl.ds(c * hr, hr), pl.ds((1 - c) * hr, hr)

        def chip_copy(a, j, chip):
            mine, _ = halves(a)
            return pltpu.make_async_remote_copy(
                src_ref=srcs[a].at[mine], dst_ref=dsts[a].at[chip, mine],
                send_sem=send_sems.at[3 * a + j], recv_sem=recv_sems.at[3 * a + j],
                device_id=(*others[j], c), device_id_type=MESH)

        def pass_copy(a, j, chip, half):
            return pltpu.make_async_remote_copy(
                src_ref=dsts[a].at[chip, half], dst_ref=dsts[a].at[chip, half],
                send_sem=pass_send.at[3 * a + j], recv_sem=pass_recv.at[3 * a + j],
                device_id=(x, y, 1 - c), device_id_type=MESH)

        def conv_copy(j, chip):
            return pltpu.make_async_remote_copy(
                src_ref=conv_src, dst_ref=conv_dst.at[chip],
                send_sem=conv_send.at[j], recv_sem=conv_recv.at[j],
                device_id=(*others[j], c), device_id_type=MESH)

        own = [pltpu.make_async_copy(srcs[a], dsts[a].at[me], own_sems.at[a]) for a in range(n)]
        own.append(pltpu.make_async_copy(conv_src, conv_dst.at[me], own_sems.at[n]))
        for cp in own:
            cp.start()
        sends = [conv_copy(j, me) for j in range(3)] + [chip_copy(a, j, me) for a in range(n) for j in range(3)]
        for cp in sends:
            cp.start()
        passed = []
        for a in range(n):
            mine, _ = halves(a)
            for j, (ox, oy) in enumerate(others):
                chip_copy(a, j, 2 * ox + oy).wait_recv()
                passed.append(pass_copy(a, j, 2 * ox + oy, mine))
                passed[-1].start()
        for a in range(n):
            _, theirs = halves(a)
            for j, (ox, oy) in enumerate(others):
                pass_copy(a, j, 2 * ox + oy, theirs).wait_recv()
        for j, (ox, oy) in enumerate(others):
            conv_copy(j, 2 * ox + oy).wait_recv()
        for cp in sends + passed:
            cp.wait_send()
        for cp in own:
            cp.wait()

    stack = lambda s: jax.ShapeDtypeStruct((N_CHIPS,) + s.shape, s.dtype)
    return pl.pallas_call(
        body, name="gather_weights",
        in_specs=[ANY] * (n + 1), out_specs=[ANY] * (n + 1),
        out_shape=[stack(s) for s in shards] + [stack(conv_shard)],
        scratch_shapes=[pltpu.SemaphoreType.DMA((3 * n,)), pltpu.SemaphoreType.DMA((3 * n,)),
                        pltpu.SemaphoreType.DMA((3 * n,)), pltpu.SemaphoreType.DMA((3 * n,)),
                        pltpu.SemaphoreType.DMA((n + 1,)),
                        pltpu.SemaphoreType.DMA((3,)), pltpu.SemaphoreType.DMA((3,))],
    )(*shards, conv_shard)


def _sibling_exchange(grads):
    n = len(grads)

    def body(*refs):
        srcs, dsts = refs[:n], refs[n:2 * n]
        send_sems, recv_sems = refs[2 * n:]
        x, y, c, _ = _place()
        copies = []
        for a in range(n):
            hr = srcs[a].shape[1] // 2
            copies.append(pltpu.make_async_remote_copy(
                src_ref=srcs[a].at[:, pl.ds((1 - c) * hr, hr)], dst_ref=dsts[a],
                send_sem=send_sems.at[a], recv_sem=recv_sems.at[a],
                device_id=(x, y, 1 - c), device_id_type=MESH))
        for cp in copies:
            cp.start()
        for cp in copies:
            cp.wait()

    half = lambda s: jax.ShapeDtypeStruct((s.shape[0], s.shape[1] // 2, s.shape[2]), s.dtype)
    return pl.pallas_call(
        body, name="sibling_exchange",
        in_specs=[ANY] * n, out_specs=[ANY] * n, out_shape=[half(s) for s in grads],
        scratch_shapes=[pltpu.SemaphoreType.DMA((n,)), pltpu.SemaphoreType.DMA((n,))],
    )(*grads)


def _add_halves(name, grad, recv, core):
    K, r, cols = grad.shape
    hr = r // 2

    def body(core_ref, g_ref, r_ref, out_ref):
        out_ref[...] = (g_ref[...].astype(F32) + r_ref[...].astype(F32)).astype(BF16)

    return pl.pallas_call(
        body, name=name,
        grid_spec=pltpu.PrefetchScalarGridSpec(
            num_scalar_prefetch=1, grid=(K,),
            in_specs=[pl.BlockSpec((None, hr, cols), lambda k, core_ref: (k, core_ref[0], 0)),
                      pl.BlockSpec((None, hr, cols), lambda k, core_ref: (k, 0, 0))],
            out_specs=pl.BlockSpec((None, hr, cols), lambda k, core_ref: (k, 0, 0))),
        out_shape=jax.ShapeDtypeStruct((K, hr, cols), BF16),
        compiler_params=_params(("arbitrary",)),
    )(core, grad, recv)


def _chip_exchange(parts):
    n = len(parts)

    def body(*refs):
        srcs, dsts = refs[:n], refs[n:2 * n]
        send_sems, recv_sems, own_sems = refs[2 * n:]
        x, y, c, others = _place()
        me = 2 * x + y
        own = [pltpu.make_async_copy(srcs[a].at[me], dsts[a].at[me], own_sems.at[a]) for a in range(n)]
        for cp in own:
            cp.start()
        copies = []
        for a in range(n):
            for j, (ox, oy) in enumerate(others):
                copies.append(pltpu.make_async_remote_copy(
                    src_ref=srcs[a].at[2 * ox + oy], dst_ref=dsts[a].at[me],
                    send_sem=send_sems.at[3 * a + j], recv_sem=recv_sems.at[3 * a + j],
                    device_id=(ox, oy, c), device_id_type=MESH))
        for cp in copies:
            cp.start()
        for a in range(n):
            for j, (ox, oy) in enumerate(others):
                pltpu.make_async_remote_copy(
                    src_ref=srcs[a].at[me], dst_ref=dsts[a].at[2 * ox + oy],
                    send_sem=send_sems.at[3 * a + j], recv_sem=recv_sems.at[3 * a + j],
                    device_id=(ox, oy, c), device_id_type=MESH).wait_recv()
        for cp in copies:
            cp.wait_send()
        for cp in own:
            cp.wait()

    return pl.pallas_call(
        body, name="chip_exchange",
        in_specs=[ANY] * n, out_specs=[ANY] * n,
        out_shape=[jax.ShapeDtypeStruct(s.shape, s.dtype) for s in parts],
        scratch_shapes=[pltpu.SemaphoreType.DMA((3 * n,)), pltpu.SemaphoreType.DMA((3 * n,)),
                        pltpu.SemaphoreType.DMA((n,))],
    )(*parts)


def _sum_chips(name, parts):
    K, hr, cols = parts.shape
    tr = hr // 2

    def body(p_ref, out_ref):
        acc = p_ref[0].astype(F32)
        for k in range(1, K):
            acc = acc + p_ref[k].astype(F32)
        out_ref[...] = acc

    return pl.pallas_call(
        body, name=name, grid=(hr // tr,),
        in_specs=[pl.BlockSpec((K, tr, cols), lambda i: (0, i, 0))],
        out_specs=pl.BlockSpec((tr, cols), lambda i: (i, 0)),
        out_shape=jax.ShapeDtypeStruct((hr, cols), F32),
        compiler_params=_params(("arbitrary",)),
    )(parts)


def _share_halves(halves):
    n = len(halves)

    def body(*refs):
        srcs, dsts = refs[:n], refs[n:2 * n]
        send_sems, recv_sems, own_sems = refs[2 * n:]
        x, y, c, _ = _place()
        own, copies = [], []
        for a in range(n):
            hr = srcs[a].shape[0]
            mine = pl.ds(c * hr, hr)
            own.append(pltpu.make_async_copy(srcs[a], dsts[a].at[mine], own_sems.at[a]))
            copies.append(pltpu.make_async_remote_copy(
                src_ref=srcs[a], dst_ref=dsts[a].at[mine],
                send_sem=send_sems.at[a], recv_sem=recv_sems.at[a],
                device_id=(x, y, 1 - c), device_id_type=MESH))
        for cp in own + copies:
            cp.start()
        for a in range(n):
            hr = srcs[a].shape[0]
            pltpu.make_async_remote_copy(
                src_ref=srcs[a], dst_ref=dsts[a].at[pl.ds((1 - c) * hr, hr)],
                send_sem=send_sems.at[a], recv_sem=recv_sems.at[a],
                device_id=(x, y, 1 - c), device_id_type=MESH).wait_recv()
        for cp in copies:
            cp.wait_send()
        for cp in own:
            cp.wait()

    return pl.pallas_call(
        body, name="share_halves",
        in_specs=[ANY] * n, out_specs=[ANY] * n,
        out_shape=[jax.ShapeDtypeStruct((2 * s.shape[0], s.shape[1]), s.dtype) for s in halves],
        scratch_shapes=[pltpu.SemaphoreType.DMA((n,)), pltpu.SemaphoreType.DMA((n,)),
                        pltpu.SemaphoreType.DMA((n,))],
    )(*halves)


def _allreduce_small(part):
    rows = part.shape[0]

    def body(in_ref, out_ref, land, send_sems, recv_sems):
        x, y, c, _ = _place()
        me = 4 * x + 2 * y + c
        land[me] = in_ref[...]
        copies = []
        for d in range(1, N_DEV):
            peer = (1 - x if d & 4 else x, 1 - y if d & 2 else y, 1 - c if d & 1 else c)
            copies.append(pltpu.make_async_remote_copy(
                src_ref=in_ref, dst_ref=land.at[me],
                send_sem=send_sems.at[d - 1], recv_sem=recv_sems.at[d - 1],
                device_id=peer, device_id_type=MESH))
        for cp in copies:
            cp.start()
        for d in range(1, N_DEV):
            px, py, pc = (1 - x if d & 4 else x, 1 - y if d & 2 else y, 1 - c if d & 1 else c)
            pltpu.make_async_remote_copy(
                src_ref=in_ref, dst_ref=land.at[4 * px + 2 * py + pc],
                send_sem=send_sems.at[d - 1], recv_sem=recv_sems.at[d - 1],
                device_id=(px, py, pc), device_id_type=MESH).wait_recv()
        for cp in copies:
            cp.wait_send()
        acc = land[0]
        for k in range(1, N_DEV):
            acc = acc + land[k]
        out_ref[...] = acc

    vmem = pl.BlockSpec(memory_space=pltpu.VMEM)
    return pl.pallas_call(
        body, name="allreduce_small",
        in_specs=[vmem], out_specs=vmem,
        out_shape=jax.ShapeDtypeStruct(part.shape, F32),
        scratch_shapes=[pltpu.VMEM((N_DEV, rows, LANES), F32),
                        pltpu.SemaphoreType.DMA((N_DEV - 1,)), pltpu.SemaphoreType.DMA((N_DEV - 1,))],
    )(part)


def _adamw(name, w, g, m, v, tr):
    rows, cols = w.shape
    c1 = 1.0 / (1.0 - ADAM_B1 ** ADAM_STEP)
    c2 = 1.0 / (1.0 - ADAM_B2 ** ADAM_STEP)

    def body(w_ref, g_ref, m_ref, v_ref, d_ref, nm_ref, nv_ref):
        gv = g_ref[...]
        nm = ADAM_B1 * m_ref[...] + (1.0 - ADAM_B1) * gv
        nv = ADAM_B2 * v_ref[...] + (1.0 - ADAM_B2) * (gv * gv)
        nm_ref[...] = nm
        nv_ref[...] = nv
        d_ref[...] = -ADAM_LR * ((nm * c1) / (jnp.sqrt(nv * c2) + ADAM_EPS) + ADAM_WD * w_ref[...])

    spec = pl.BlockSpec((tr, cols), lambda i: (i, 0))
    out = jax.ShapeDtypeStruct((rows, cols), F32)
    return pl.pallas_call(
        body, name=name, grid=(rows // tr,),
        in_specs=[spec] * 4, out_specs=[spec] * 3, out_shape=[out] * 3,
        compiler_params=_params(("arbitrary",)),
    )(w, g, m, v)


WEIGHTS = ("ffn1_norm", "ffn1_gate", "ffn1_up", "ffn1_down", "mix_norm", "w_in", "b_forget", "conv_w",
           "w_o_attn", "w_o_conv", "w_out", "ffn2_norm", "ffn2_gate", "ffn2_up", "ffn2_down", "final_norm")
VEC_ROWS = 8


def _pack_small(t, conv_rows):
    conv = t["conv_w"]
    parts = [t[n].reshape(VEC_ROWS, LANES) for n in NORMS]
    parts.append(jnp.pad(conv, ((0, conv_rows - conv.shape[0]), (0, 0))))
    parts.append(jnp.pad(t["b_forget"].reshape(1, N_HEADS), ((0, 7), (0, LANES - N_HEADS))))
    return jnp.concatenate(parts, axis=0)


def _unpack_small(p, conv_rows):
    out = {n: p[VEC_ROWS * i:VEC_ROWS * (i + 1)].reshape(-1) for i, n in enumerate(NORMS)}
    base = VEC_ROWS * len(NORMS)
    out["conv_w"] = p[base:base + 3]
    out["b_forget"] = p[base + conv_rows, :N_HEADS]
    return out


def kernel(x, ffn1_norm, ffn1_gate, ffn1_up, ffn1_down, mix_norm, w_in, b_forget, conv_w, w_o_attn, w_o_conv, w_out, ffn2_norm, ffn2_gate, ffn2_up, ffn2_down, final_norm, loss_target, m_ffn1_norm, m_ffn1_gate, m_ffn1_up, m_ffn1_down, m_mix_norm, m_w_in, m_b_forget, m_conv_w, m_w_o_attn, m_w_o_conv, m_w_out, m_ffn2_norm, m_ffn2_gate, m_ffn2_up, m_ffn2_down, m_final_norm, v_ffn1_norm, v_ffn1_gate, v_ffn1_up, v_ffn1_down, v_mix_norm, v_w_in, v_b_forget, v_conv_w, v_w_o_attn, v_w_o_conv, v_w_out, v_ffn2_norm, v_ffn2_gate, v_ffn2_up, v_ffn2_down, v_final_norm):
    given = dict(locals())
    wts = {n: given[n] for n in WEIGHTS}
    mom = {n: given["m_" + n] for n in WEIGHTS}
    var = {n: given["v_" + n] for n in WEIGHTS}
    B, S, D = x.shape
    chip = 2 * lax.axis_index("x") + lax.axis_index("y")
    core = lax.axis_index("c").astype(jnp.int32).reshape(1)

    conv_shard = jnp.pad(conv_w, ((0, 8 - conv_w.shape[0]), (0, 0)))
    gathered = _gather_weights([wts[n].astype(BF16) for n in BIG], conv_shard)
    gw = dict(zip(BIG + ("conv_w",), gathered))
    w = _compute_layout(gw, wts)

    loss, grad_x, g = _local_step(x.reshape(B * S, D), loss_target.reshape(B * S, D), w, B, S)
    gs = _grads_to_shard_major(g)

    partial = [gs[n] for n in BIG]
    from_sibling = _sibling_exchange(partial)
    chip_part = [_add_halves("add_halves_" + n, p, r, core) for n, p, r in zip(BIG, partial, from_sibling)]
    by_chip = _chip_exchange(chip_part)
    halves = [_sum_chips("sum_chips_" + n, p) for n, p in zip(BIG, by_chip)]
    grads = dict(zip(BIG, _share_halves(halves)))

    conv_all = _shard_cols(gs["conv_w"]).reshape(N_CHIPS * 8, LANES)
    small_part = _pack_small({**{n: gs[n] for n in NORMS}, "conv_w": conv_all, "b_forget": gs["b_forget"][0, :N_HEADS]},
                             N_CHIPS * 8)
    base = VEC_ROWS * len(NORMS)
    small_sum = _allreduce_small(small_part)
    small = _unpack_small(small_sum, N_CHIPS * 8)
    small["conv_w"] = lax.dynamic_slice_in_dim(small_sum[base:base + N_CHIPS * 8], chip * 8, 8, axis=0)[:3]
    grads.update(small)

    delta, new_m, new_v = {}, {}, {}
    for n in BIG:
        delta[n], new_m[n], new_v[n] = _adamw("adamw_" + n, wts[n], grads[n], mom[n], var[n], wts[n].shape[0] // 4)
    packs = [_pack_small(t, 8) for t in (wts, grads, mom, var)]
    for out, p in zip((delta, new_m, new_v), _adamw("adamw_small", *packs, packs[0].shape[0])):
        out.update(_unpack_small(p, 8))

    total = lax.psum(loss[0, 0], ("x", "y", "c"))
    return (total, grad_x.reshape(B, S, D), *[grads[n] for n in WEIGHTS], *[delta[n] for n in WEIGHTS],
            *[new_m[n] for n in WEIGHTS], *[new_v[n] for n in WEIGHTS])
```

```python
import functools
import math

import jax
import jax.numpy as jnp
from jax import lax
from jax.experimental import pallas as pl
from jax.experimental.pallas import tpu as pltpu

F32 = jnp.float32
BF16 = jnp.bfloat16
MESH = pl.DeviceIdType.MESH

N_CHIPS = 4
N_DEV = 8
N_HEADS = 8
HEAD_DIM = 64
ATTN_W = N_HEADS * HEAD_DIM
CONV_W = 512
RMS_EPS = 1e-6
FFN_RES = 0.5
LANES = 128
VMEM_LIMIT = 56 * 1024 * 1024

ADAM_LR = 0.001
ADAM_B1 = 0.9
ADAM_B2 = 0.999
ADAM_EPS = 1e-08
ADAM_WD = 0.01
ADAM_STEP = 10

PROJ_W = 3 * ATTN_W + 3 * CONV_W + 2 * 1024
COL_CB, COL_CC, COL_CX = 3 * ATTN_W, 3 * ATTN_W + CONV_W, 3 * ATTN_W + 2 * CONV_W
COL_GATES = 3 * ATTN_W + 3 * CONV_W


def _params(sem=None, vmem=VMEM_LIMIT):
    return pltpu.CompilerParams(dimension_semantics=sem, vmem_limit_bytes=vmem)


def _dot(a, b):
    return lax.dot_general(a, b, (((1,), (0,)), ((), ())), preferred_element_type=F32)


def _dot_nt(a, b):
    return lax.dot_general(a, b, (((1,), (1,)), ((), ())), preferred_element_type=F32)


def _dot_tn(a, b):
    return lax.dot_general(a, b, (((0,), (0,)), ((), ())), preferred_element_type=F32)


def _sigmoid(x):
    return 1.0 / (1.0 + jnp.exp(-x))


def _rms(xv):
    inv = lax.rsqrt(jnp.mean(xv * xv, axis=-1, keepdims=True) + RMS_EPS)
    return xv * inv, inv


def _rms_bwd(dn, xhat, inv, g):
    dxhat = dn * g
    dx = inv * (dxhat - xhat * jnp.mean(dxhat * xhat, axis=-1, keepdims=True))
    return dx, jnp.sum(dn * xhat, axis=0, keepdims=True)


def _ffn_fwd(name, x, g, wg, wu, wd, tm):
    T, D = x.shape
    K, _, Fs = wg.shape

    def body(x_ref, g_ref, wg_ref, wu_ref, wd_ref, out_ref, hg_ref, hu_ref, n_scr, acc_scr):
        k = pl.program_id(1)

        @pl.when(k == 0)
        def _():
            xhat, _ = _rms(x_ref[...])
            n_scr[...] = (xhat * g_ref[...]).astype(BF16)
            acc_scr[...] = jnp.zeros_like(acc_scr)

        n = n_scr[...]
        hg = _dot(n, wg_ref[...])
        hu = _dot(n, wu_ref[...])
        hg_ref[...] = hg.astype(BF16)
        hu_ref[...] = hu.astype(BF16)
        act = (hg * _sigmoid(hg) * hu).astype(BF16)
        acc_scr[...] += _dot(act, wd_ref[...])

        @pl.when(k == K - 1)
        def _():
            out_ref[...] = x_ref[...] + FFN_RES * acc_scr[...]

    return pl.pallas_call(
        body, name=name, grid=(T // tm, K),
        in_specs=[pl.BlockSpec((tm, D), lambda i, k: (i, 0)),
                  pl.BlockSpec((1, D), lambda i, k: (0, 0)),
                  pl.BlockSpec((None, D, Fs), lambda i, k: (k, 0, 0)),
                  pl.BlockSpec((None, D, Fs), lambda i, k: (k, 0, 0)),
                  pl.BlockSpec((None, Fs, D), lambda i, k: (k, 0, 0))],
        out_specs=[pl.BlockSpec((tm, D), lambda i, k: (i, 0)),
                   pl.BlockSpec((None, tm, Fs), lambda i, k: (k, i, 0)),
                   pl.BlockSpec((None, tm, Fs), lambda i, k: (k, i, 0))],
        out_shape=[jax.ShapeDtypeStruct((T, D), F32),
                   jax.ShapeDtypeStruct((K, T, Fs), BF16),
                   jax.ShapeDtypeStruct((K, T, Fs), BF16)],
        scratch_shapes=[pltpu.VMEM((tm, D), BF16), pltpu.VMEM((tm, D), F32)],
        compiler_params=_params(("arbitrary", "arbitrary")),
    )(x, g, wg, wu, wd)


def _ffn_bwd_dx(name, dout, x, g, hg, hu, wg, wu, wd, tm):
    T, D = x.shape
    K, _, Fs = wg.shape

    def body(dout_ref, x_ref, g_ref, hg_ref, hu_ref, wg_ref, wu_ref, wd_ref,
             dx_ref, dhg_ref, dhu_ref, dg_ref, df_scr, dn_scr):
        i, k = pl.program_id(0), pl.program_id(1)

        @pl.when(k == 0)
        def _():
            df_scr[...] = (FFN_RES * dout_ref[...]).astype(BF16)
            dn_scr[...] = jnp.zeros_like(dn_scr)

        @pl.when((k == 0) & (i == 0))
        def _():
            dg_ref[...] = jnp.zeros_like(dg_ref)

        dact = _dot_nt(df_scr[...], wd_ref[...])
        hgv = hg_ref[...].astype(F32)
        huv = hu_ref[...].astype(F32)
        s = _sigmoid(hgv)
        dhu = (dact * (hgv * s)).astype(BF16)
        dhg = (dact * huv * (s * (1.0 + hgv * (1.0 - s)))).astype(BF16)
        dhg_ref[...] = dhg
        dhu_ref[...] = dhu
        dn_scr[...] += _dot_nt(dhg, wg_ref[...]) + _dot_nt(dhu, wu_ref[...])

        @pl.when(k == K - 1)
        def _():
            xhat, inv = _rms(x_ref[...])
            dx, dg = _rms_bwd(dn_scr[...], xhat, inv, g_ref[...])
            dx_ref[...] = dout_ref[...] + dx
            dg_ref[...] += dg

    return pl.pallas_call(
        body, name=name, grid=(T // tm, K),
        in_specs=[pl.BlockSpec((tm, D), lambda i, k: (i, 0)),
                  pl.BlockSpec((tm, D), lambda i, k: (i, 0)),
                  pl.BlockSpec((1, D), lambda i, k: (0, 0)),
                  pl.BlockSpec((None, tm, Fs), lambda i, k: (k, i, 0)),
                  pl.BlockSpec((None, tm, Fs), lambda i, k: (k, i, 0)),
                  pl.BlockSpec((None, D, Fs), lambda i, k: (k, 0, 0)),
                  pl.BlockSpec((None, D, Fs), lambda i, k: (k, 0, 0)),
                  pl.BlockSpec((None, Fs, D), lambda i, k: (k, 0, 0))],
        out_specs=[pl.BlockSpec((tm, D), lambda i, k: (i, 0)),
                   pl.BlockSpec((None, tm, Fs), lambda i, k: (k, i, 0)),
                   pl.BlockSpec((None, tm, Fs), lambda i, k: (k, i, 0)),
                   pl.BlockSpec((1, D), lambda i, k: (0, 0))],
        out_shape=[jax.ShapeDtypeStruct((T, D), F32),
                   jax.ShapeDtypeStruct((K, T, Fs), BF16),
                   jax.ShapeDtypeStruct((K, T, Fs), BF16),
                   jax.ShapeDtypeStruct((1, D), F32)],
        scratch_shapes=[pltpu.VMEM((tm, D), BF16), pltpu.VMEM((tm, D), F32)],
        compiler_params=_params(("arbitrary", "arbitrary")),
    )(dout, x, g, hg, hu, wg, wu, wd)


def _ffn_bwd_dw(name, dout, x, g, hg, hu, dhg, dhu, tk):
    T, D = x.shape
    K, _, Fs = hg.shape
    nt = T // tk

    def body(dout_ref, x_ref, g_ref, hg_ref, hu_ref, dhg_ref, dhu_ref,
             dwg_ref, dwu_ref, dwd_ref, accg, accu, accd):
        t = pl.program_id(1)

        @pl.when(t == 0)
        def _():
            accg[...] = jnp.zeros_like(accg)
            accu[...] = jnp.zeros_like(accu)
            accd[...] = jnp.zeros_like(accd)

        xhat, _ = _rms(x_ref[...])
        n = (xhat * g_ref[...]).astype(BF16)
        df = (FFN_RES * dout_ref[...]).astype(BF16)
        hgv = hg_ref[...].astype(F32)
        act = (hgv * _sigmoid(hgv) * hu_ref[...].astype(F32)).astype(BF16)
        accg[...] += _dot_tn(n, dhg_ref[...])
        accu[...] += _dot_tn(n, dhu_ref[...])
        accd[...] += _dot_tn(act, df)

        @pl.when(t == nt - 1)
        def _():
            dwg_ref[...] = accg[...].astype(BF16)
            dwu_ref[...] = accu[...].astype(BF16)
            dwd_ref[...] = accd[...].astype(BF16)

    act_spec = pl.BlockSpec((None, tk, Fs), lambda k, t: (k, t, 0))
    return pl.pallas_call(
        body, name=name, grid=(K, nt),
        in_specs=[pl.BlockSpec((tk, D), lambda k, t: (t, 0)),
                  pl.BlockSpec((tk, D), lambda k, t: (t, 0)),
                  pl.BlockSpec((1, D), lambda k, t: (0, 0)),
                  act_spec, act_spec, act_spec, act_spec],
        out_specs=[pl.BlockSpec((None, D, Fs), lambda k, t: (k, 0, 0)),
                   pl.BlockSpec((None, D, Fs), lambda k, t: (k, 0, 0)),
                   pl.BlockSpec((None, Fs, D), lambda k, t: (k, 0, 0))],
        out_shape=[jax.ShapeDtypeStruct((K, D, Fs), BF16),
                   jax.ShapeDtypeStruct((K, D, Fs), BF16),
                   jax.ShapeDtypeStruct((K, Fs, D), BF16)],
        scratch_shapes=[pltpu.VMEM((D, Fs), F32), pltpu.VMEM((D, Fs), F32), pltpu.VMEM((Fs, D), F32)],
        compiler_params=_params(("arbitrary", "arbitrary")),
    )(dout, x, g, hg, hu, dhg, dhu)


def _mix_proj_fwd(x, g, wproj, wf, tm, tn):
    T, D = x.shape
    N = wproj.shape[1]

    def body(x_ref, g_ref, w_ref, wf_ref, h_ref, proj_ref, flog_ref, h_scr):
        @pl.when(pl.program_id(1) == 0)
        def _():
            xhat, _ = _rms(x_ref[...])
            h = (xhat * g_ref[...]).astype(BF16)
            h_scr[...] = h
            h_ref[...] = h
            flog_ref[...] = _dot(h, wf_ref[...])

        proj_ref[...] = _dot(h_scr[...], w_ref[...]).astype(BF16)

    return pl.pallas_call(
        body, name="mix_proj_fwd", grid=(T // tm, N // tn),
        in_specs=[pl.BlockSpec((tm, D), lambda i, n: (i, 0)),
                  pl.BlockSpec((1, D), lambda i, n: (0, 0)),
                  pl.BlockSpec((D, tn), lambda i, n: (0, n)),
                  pl.BlockSpec((D, LANES), lambda i, n: (0, 0))],
        out_specs=[pl.BlockSpec((tm, D), lambda i, n: (i, 0)),
                   pl.BlockSpec((tm, tn), lambda i, n: (i, n)),
                   pl.BlockSpec((tm, LANES), lambda i, n: (i, 0))],
        out_shape=[jax.ShapeDtypeStruct((T, D), BF16),
                   jax.ShapeDtypeStruct((T, N), BF16),
                   jax.ShapeDtypeStruct((T, LANES), F32)],
        scratch_shapes=[pltpu.VMEM((tm, D), BF16)],
        compiler_params=_params(("arbitrary", "arbitrary")),
    )(x, g, wproj, wf)


def _log_sigmoid(z):
    return -(jnp.maximum(-z, 0.0) + jnp.log(1.0 + jnp.exp(-jnp.abs(z))))


def _tri(n, lower):
    r = lax.broadcasted_iota(jnp.int32, (n, n), 0)
    c = lax.broadcasted_iota(jnp.int32, (n, n), 1)
    return jnp.where((r >= c) if lower else (r <= c), 1.0, 0.0).astype(F32)


def _dot_f32(a, b):
    return lax.dot_general(a, b, (((1,), (0,)), ((), ())), preferred_element_type=F32,
                           precision=lax.Precision.HIGHEST)


def _fgate_fwd(flog, bias, B, S, ch):
    def body(flog_ref, b_ref, cum_ref):
        tri = _tri(ch, True)
        carry = jnp.zeros((1, LANES), F32)
        for c0 in range(0, S, ch):
            lf = _log_sigmoid(flog_ref[c0:c0 + ch, :] + b_ref[...])
            cs = _dot_f32(tri, lf) + carry
            cum_ref[c0:c0 + ch, :] = cs
            carry = cs[ch - 1:ch, :]

    return pl.pallas_call(
        body, name="fgate_fwd", grid=(B,),
        in_specs=[pl.BlockSpec((S, LANES), lambda b: (b, 0)),
                  pl.BlockSpec((1, LANES), lambda b: (0, 0))],
        out_specs=pl.BlockSpec((S, LANES), lambda b: (b, 0)),
        out_shape=jax.ShapeDtypeStruct((B * S, LANES), F32),
        compiler_params=_params(("arbitrary",)),
    )(flog, bias)


def _fgate_bwd(dcum, flog, bias, B, S, ch):
    def body(dcum_ref, flog_ref, b_ref, dflog_ref, db_ref):
        @pl.when(pl.program_id(0) == 0)
        def _():
            db_ref[...] = jnp.zeros_like(db_ref)

        tri = _tri(ch, False)
        carry = jnp.zeros((1, LANES), F32)
        db = jnp.zeros((1, LANES), F32)
        for c0 in range(S - ch, -1, -ch):
            dlf = _dot_f32(tri, dcum_ref[c0:c0 + ch, :]) + carry
            carry = dlf[0:1, :]
            z = flog_ref[c0:c0 + ch, :] + b_ref[...]
            dz = dlf * _sigmoid(-z)
            dflog_ref[c0:c0 + ch, :] = dz
            db = db + jnp.sum(dz, axis=0, keepdims=True)
        db_ref[...] += db

    return pl.pallas_call(
        body, name="fgate_bwd", grid=(B,),
        in_specs=[pl.BlockSpec((S, LANES), lambda b: (b, 0)),
                  pl.BlockSpec((S, LANES), lambda b: (b, 0)),
                  pl.BlockSpec((1, LANES), lambda b: (0, 0))],
        out_specs=[pl.BlockSpec((S, LANES), lambda b: (b, 0)),
                   pl.BlockSpec((1, LANES), lambda b: (0, 0))],
        out_shape=[jax.ShapeDtypeStruct((B * S, LANES), F32),
                   jax.ShapeDtypeStruct((1, LANES), F32)],
        compiler_params=_params(("arbitrary",)),
    )(dcum, flog, bias)


def _pick_lane(tile, h):
    lane = lax.broadcasted_iota(jnp.int32, tile.shape, 1)
    return jnp.sum(jnp.where(lane == h, tile, 0.0), axis=1, keepdims=True)


def _pick_row(tile, h):
    row = lax.broadcasted_iota(jnp.int32, tile.shape, 0)
    return jnp.sum(jnp.where(row == h, tile, 0.0), axis=0, keepdims=True)


def _causal(tq):
    r = lax.broadcasted_iota(jnp.int32, (tq, tq), 0)
    c = lax.broadcasted_iota(jnp.int32, (tq, tq), 1)
    return r >= c


NEG = -1e30


def _attn_fwd(q, k, v, cum, cum_t, tq):
    B, H, S, hd = q.shape
    scale = 1.0 / math.sqrt(hd)

    def body(q_ref, k_ref, v_ref, cum_ref, cumt_ref, o_ref, lse_ref):
        h, qi = pl.program_id(1), pl.program_id(2)
        qv = q_ref[...]
        cq = _pick_lane(cum_ref[...], h)

        def tile(j, carry, masked):
            m, l, acc = carry
            off = pl.multiple_of(j * tq, tq)
            kj = k_ref[pl.ds(off, tq), :]
            vj = v_ref[pl.ds(off, tq), :]
            ck = _pick_row(cumt_ref[j], h)
            s = _dot_nt(qv, kj) * scale + (cq - ck)
            if masked:
                s = jnp.where(_causal(tq), s, NEG)
            m_new = jnp.maximum(m, jnp.max(s, axis=1, keepdims=True))
            p = jnp.exp(s - m_new)
            alpha = jnp.exp(m - m_new)
            l = alpha * l + jnp.sum(p, axis=1, keepdims=True)
            acc = alpha * acc + _dot(p.astype(BF16), vj)
            return m_new, l, acc

        init = (jnp.full((tq, 1), NEG, F32), jnp.zeros((tq, 1), F32), jnp.zeros((tq, hd), F32))
        carry = lax.fori_loop(0, qi, lambda j, c: tile(j, c, False), init)
        m, l, acc = tile(qi, carry, True)
        o_ref[...] = (acc / l).astype(BF16)
        lse_ref[...] = m + jnp.log(l)

    return pl.pallas_call(
        body, name="attn_fwd", grid=(B, H, S // tq),
        in_specs=[pl.BlockSpec((None, None, tq, hd), lambda b, h, i: (b, h, i, 0)),
                  pl.BlockSpec((None, None, S, hd), lambda b, h, i: (b, h, 0, 0)),
                  pl.BlockSpec((None, None, S, hd), lambda b, h, i: (b, h, 0, 0)),
                  pl.BlockSpec((None, tq, LANES), lambda b, h, i: (b, i, 0)),
                  pl.BlockSpec((None, S // tq, 8, tq), lambda b, h, i: (b, 0, 0, 0))],
        out_specs=[pl.BlockSpec((None, None, tq, hd), lambda b, h, i: (b, h, i, 0)),
                   pl.BlockSpec((None, None, tq, 1), lambda b, h, i: (b, h, i, 0))],
        out_shape=[jax.ShapeDtypeStruct((B, H, S, hd), BF16),
                   jax.ShapeDtypeStruct((B, H, S, 1), F32)],
        compiler_params=_params(("arbitrary", "arbitrary", "arbitrary")),
    )(q, k, v, cum, cum_t)


def _attn_bwd(q, k, v, o, do, lse, cum, cum_t, tq):
    B, H, S, hd = q.shape
    nq = S // tq
    scale = 1.0 / math.sqrt(hd)

    def body(q_ref, k_ref, v_ref, o_ref, do_ref, lse_ref, cum_ref, cumt_ref,
             dq_ref, dk_ref, dv_ref, dcq_ref, dck_ref, dq_scr, dcq_scr):
        h, kj = pl.program_id(1), pl.program_id(2)

        @pl.when(kj == 0)
        def _():
            dq_scr[...] = jnp.zeros_like(dq_scr)
            dcq_scr[...] = jnp.zeros_like(dcq_scr)

        kv = k_ref[...]
        vv = v_ref[...]
        ck = _pick_row(cumt_ref[...], h)

        def tile(i, carry, masked):
            dk, dv, dcol = carry
            off = pl.multiple_of(i * tq, tq)
            qi = q_ref[pl.ds(off, tq), :]
            doi = do_ref[pl.ds(off, tq), :]
            delta = jnp.sum(doi.astype(F32) * o_ref[pl.ds(off, tq), :].astype(F32), axis=1, keepdims=True)
            cq = _pick_lane(cum_ref[pl.ds(off, tq), :], h)
            s = _dot_nt(qi, kv) * scale + (cq - ck)
            p = jnp.exp(s - lse_ref[pl.ds(off, tq), :])
            if masked:
                p = jnp.where(_causal(tq), p, 0.0)
            dv = dv + _dot_tn(p.astype(BF16), doi)
            ds = p * (_dot_nt(doi, vv) - delta)
            dcol = dcol + jnp.sum(ds, axis=0, keepdims=True)
            dcq_scr[pl.ds(off, tq), :] += jnp.sum(ds, axis=1, keepdims=True)
            dsb = ds.astype(BF16)
            dk = dk + _dot_tn(dsb, qi) * scale
            dq_scr[pl.ds(off, tq), :] += _dot(dsb, kv) * scale
            return dk, dv, dcol

        init = (jnp.zeros((tq, hd), F32), jnp.zeros((tq, hd), F32), jnp.zeros((1, tq), F32))
        carry = tile(kj, init, True)
        dk, dv, dcol = lax.fori_loop(kj + 1, nq, lambda i, c: tile(i, c, False), carry)
        dk_ref[...] = dk.astype(BF16)
        dv_ref[...] = dv.astype(BF16)
        dck_ref[...] = -dcol

        @pl.when(kj == nq - 1)
        def _():
            dq_ref[...] = dq_scr[...].astype(BF16)
            dcq_ref[...] = dcq_scr[...]

    full = pl.BlockSpec((None, None, S, hd), lambda b, h, j: (b, h, 0, 0))
    tile_spec = pl.BlockSpec((None, None, tq, hd), lambda b, h, j: (b, h, j, 0))
    return pl.pallas_call(
        body, name="attn_bwd", grid=(B, H, nq),
        in_specs=[full, tile_spec, tile_spec, full, full,
                  pl.BlockSpec((None, None, S, 1), lambda b, h, j: (b, h, 0, 0)),
                  pl.BlockSpec((None, S, LANES), lambda b, h, j: (b, 0, 0)),
                  pl.BlockSpec((None, None, 8, tq), lambda b, h, j: (b, j, 0, 0))],
        out_specs=[full, tile_spec, tile_spec,
                   pl.BlockSpec((None, None, S, 1), lambda b, h, j: (b, h, 0, 0)),
                   pl.BlockSpec((None, None, 1, tq), lambda b, h, j: (b, h, 0, j))],
        out_shape=[jax.ShapeDtypeStruct((B, H, S, hd), BF16),
                   jax.ShapeDtypeStruct((B, H, S, hd), BF16),
                   jax.ShapeDtypeStruct((B, H, S, hd), BF16),
                   jax.ShapeDtypeStruct((B, H, S, 1), F32),
                   jax.ShapeDtypeStruct((B, H, 1, S), F32)],
        scratch_shapes=[pltpu.VMEM((S, hd), F32), pltpu.VMEM((S, 1), F32)],
        compiler_params=_params(("arbitrary", "arbitrary", "arbitrary")),
    )(q, k, v, o, do, lse, cum, cum_t)


def _shift_down(u, n):
    row = lax.broadcasted_iota(jnp.int32, u.shape, 0)
    return jnp.where(row >= n, pltpu.roll(u, n, 0), 0.0)


def _shift_up(u, n):
    rows = u.shape[0]
    row = lax.broadcasted_iota(jnp.int32, u.shape, 0)
    return jnp.where(row < rows - n, pltpu.roll(u, rows - n, 0), 0.0)


def _conv_specs(S):
    cb = pl.BlockSpec((S, LANES), lambda g, b: (b, COL_CB // LANES + g))
    cc = pl.BlockSpec((S, LANES), lambda g, b: (b, COL_CC // LANES + g))
    cx = pl.BlockSpec((S, LANES), lambda g, b: (b, COL_CX // LANES + g))
    w = pl.BlockSpec((8, LANES), lambda g, b: (0, g))
    return cb, cc, cx, w


def _conv_fwd(proj, conv_w, B, S):
    def body(cb_ref, cc_ref, cx_ref, w_ref, y_ref):
        u = cc_ref[...].astype(F32) * cx_ref[...].astype(F32)
        w = w_ref[...]
        conv = w[0:1, :] * _shift_down(u, 2) + w[1:2, :] * _shift_down(u, 1) + w[2:3, :] * u
        y_ref[...] = (cb_ref[...].astype(F32) * conv).astype(BF16)

    cb, cc, cx, w = _conv_specs(S)
    return pl.pallas_call(
        body, name="conv_fwd", grid=(CONV_W // LANES, B),
        in_specs=[cb, cc, cx, w],
        out_specs=pl.BlockSpec((S, LANES), lambda g, b: (b, g)),
        out_shape=jax.ShapeDtypeStruct((B * S, CONV_W), BF16),
        compiler_params=_params(("arbitrary", "arbitrary")),
    )(proj, proj, proj, conv_w)


def _conv_bwd(dy, proj, conv_w, B, S):
    def body(dy_ref, cb_ref, cc_ref, cx_ref, w_ref, dcb_ref, dcc_ref, dcx_ref, dw_ref):
        @pl.when(pl.program_id(1) == 0)
        def _():
            dw_ref[...] = jnp.zeros_like(dw_ref)

        ccv = cc_ref[...].astype(F32)
        cxv = cx_ref[...].astype(F32)
        u = ccv * cxv
        u1 = _shift_down(u, 1)
        u2 = _shift_down(u, 2)
        w = w_ref[...]
        conv = w[0:1, :] * u2 + w[1:2, :] * u1 + w[2:3, :] * u
        dyv = dy_ref[...].astype(F32)
        dcb_ref[...] = (dyv * conv).astype(BF16)
        dconv = dyv * cb_ref[...].astype(F32)
        du = w[2:3, :] * dconv + w[1:2, :] * _shift_up(dconv, 1) + w[0:1, :] * _shift_up(dconv, 2)
        dcc_ref[...] = (du * cxv).astype(BF16)
        dcx_ref[...] = (du * ccv).astype(BF16)
        row = lax.broadcasted_iota(jnp.int32, (8, LANES), 0)
        dw = jnp.where(row == 0, jnp.sum(dconv * u2, axis=0, keepdims=True),
                       jnp.where(row == 1, jnp.sum(dconv * u1, axis=0, keepdims=True),
                                 jnp.where(row == 2, jnp.sum(dconv * u, axis=0, keepdims=True), 0.0)))
        dw_ref[...] += dw

    cb, cc, cx, w = _conv_specs(S)
    out = pl.BlockSpec((S, LANES), lambda g, b: (b, g))
    return pl.pallas_call(
        body, name="conv_bwd", grid=(CONV_W // LANES, B),
        in_specs=[out, cb, cc, cx, w],
        out_specs=[out, out, out, w],
        out_shape=[jax.ShapeDtypeStruct((B * S, CONV_W), BF16)] * 3 + [jax.ShapeDtypeStruct((8, CONV_W), F32)],
        compiler_params=_params(("arbitrary", "arbitrary")),
    )(dy, proj, proj, proj, conv_w)


def _gate_specs(tm, D):
    ga = pl.BlockSpec((tm, D), lambda i: (i, COL_GATES // D))
    gc = pl.BlockSpec((tm, D), lambda i: (i, COL_GATES // D + 1))
    return ga, gc


def _mix_out_fwd(x, o, yc, proj, woa, woc, wout, tm):
    T, D = x.shape

    def body(x_ref, o_ref, yc_ref, ga_ref, gc_ref, woa_ref, woc_ref, wout_ref, out_ref):
        ya = _dot(o_ref[...], woa_ref[...])
        yp = _dot(yc_ref[...], woc_ref[...])
        merged = _sigmoid(ga_ref[...].astype(F32)) * ya + _sigmoid(gc_ref[...].astype(F32)) * yp
        out_ref[...] = x_ref[...] + _dot(merged.astype(BF16), wout_ref[...])

    ga, gc = _gate_specs(tm, D)
    row = lambda w: pl.BlockSpec((tm, w), lambda i: (i, 0))
    whole = lambda a: pl.BlockSpec(a.shape, lambda i: (0, 0))
    return pl.pallas_call(
        body, name="mix_out_fwd", grid=(T // tm,),
        in_specs=[row(D), row(ATTN_W), row(CONV_W), ga, gc, whole(woa), whole(woc), whole(wout)],
        out_specs=row(D),
        out_shape=jax.ShapeDtypeStruct((T, D), F32),
        compiler_params=_params(("arbitrary",)),
    )(x, o, yc, proj, proj, woa, woc, wout)


def _mix_out_bwd(dx, o, yc, proj, woa, woc, wout, tm):
    T, D = dx.shape
    nt = T // tm

    def body(dx_ref, o_ref, yc_ref, ga_ref, gc_ref, woa_ref, woc_ref, wout_ref,
             do_ref, dyc_ref, dg_ref, dwoa_ref, dwoc_ref, dwout_ref, acca, accc, acco):
        t = pl.program_id(0)

        @pl.when(t == 0)
        def _():
            acca[...] = jnp.zeros_like(acca)
            accc[...] = jnp.zeros_like(accc)
            acco[...] = jnp.zeros_like(acco)

        dxb = dx_ref[...].astype(BF16)
        ov, ycv = o_ref[...], yc_ref[...]
        ya = _dot(ov, woa_ref[...])
        yp = _dot(ycv, woc_ref[...])
        sa = _sigmoid(ga_ref[...].astype(F32))
        sc = _sigmoid(gc_ref[...].astype(F32))
        merged = (sa * ya + sc * yp).astype(BF16)
        dm = _dot_nt(dxb, wout_ref[...])
        dya = (dm * sa).astype(BF16)
        dyp = (dm * sc).astype(BF16)
        dg_ref[:, :D] = (dm * ya * sa * (1.0 - sa)).astype(BF16)
        dg_ref[:, D:] = (dm * yp * sc * (1.0 - sc)).astype(BF16)
        do_ref[...] = _dot_nt(dya, woa_ref[...]).astype(BF16)
        dyc_ref[...] = _dot_nt(dyp, woc_ref[...]).astype(BF16)
        acca[...] += _dot_tn(ov, dya)
        accc[...] += _dot_tn(ycv, dyp)
        acco[...] += _dot_tn(merged, dxb)

        @pl.when(t == nt - 1)
        def _():
            dwoa_ref[...] = acca[...].astype(BF16)
            dwoc_ref[...] = accc[...].astype(BF16)
            dwout_ref[...] = acco[...].astype(BF16)

    ga, gc = _gate_specs(tm, D)
    row = lambda w: pl.BlockSpec((tm, w), lambda i: (i, 0))
    whole = lambda a: pl.BlockSpec(a.shape, lambda i: (0, 0))
    return pl.pallas_call(
        body, name="mix_out_bwd", grid=(nt,),
        in_specs=[row(D), row(ATTN_W), row(CONV_W), ga, gc, whole(woa), whole(woc), whole(wout)],
        out_specs=[row(ATTN_W), row(CONV_W), row(2 * D), whole(woa), whole(woc), whole(wout)],
        out_shape=[jax.ShapeDtypeStruct((T, ATTN_W), BF16), jax.ShapeDtypeStruct((T, CONV_W), BF16),
                   jax.ShapeDtypeStruct((T, 2 * D), BF16),
                   jax.ShapeDtypeStruct(woa.shape, BF16), jax.ShapeDtypeStruct(woc.shape, BF16),
                   jax.ShapeDtypeStruct(wout.shape, BF16)],
        scratch_shapes=[pltpu.VMEM(woa.shape, F32), pltpu.VMEM(woc.shape, F32), pltpu.VMEM(wout.shape, F32)],
        compiler_params=_params(("arbitrary",)),
    )(dx, o, yc, proj, proj, woa, woc, wout)


def _mix_proj_bwd_dx(dres, x, g, pieces, weights, tm):
    T, D = x.shape
    n = len(pieces)

    def body(*refs):
        dres_ref, x_ref, g_ref = refs[:3]
        p_refs, w_refs = refs[3:3 + n], refs[3 + n:3 + 2 * n]
        dx_ref, dg_ref = refs[3 + 2 * n:]

        @pl.when(pl.program_id(0) == 0)
        def _():
            dg_ref[...] = jnp.zeros_like(dg_ref)

        dh = _dot_nt(p_refs[0][...].astype(BF16), w_refs[0][...])
        for p_ref, w_ref in zip(p_refs[1:], w_refs[1:]):
            dh = dh + _dot_nt(p_ref[...].astype(BF16), w_ref[...])
        xhat, inv = _rms(x_ref[...])
        dx, dg = _rms_bwd(dh, xhat, inv, g_ref[...])
        dx_ref[...] = dres_ref[...] + dx
        dg_ref[...] += dg

    row = lambda w: pl.BlockSpec((tm, w), lambda i: (i, 0))
    whole = lambda a: pl.BlockSpec(a.shape, lambda i: (0, 0))
    return pl.pallas_call(
        body, name="mix_proj_bwd_dx", grid=(T // tm,),
        in_specs=[row(D), row(D), whole(g)] + [row(p.shape[1]) for p in pieces] + [whole(w) for w in weights],
        out_specs=[row(D), whole(g)],
        out_shape=[jax.ShapeDtypeStruct((T, D), F32), jax.ShapeDtypeStruct((1, D), F32)],
        compiler_params=_params(("arbitrary",)),
    )(dres, x, g, *pieces, *weights)


def _matmul_tn(name, a, b, tn, tk):
    T, M = a.shape
    N = b.shape[1]
    nt = T // tk

    def body(a_ref, b_ref, out_ref, acc):
        t = pl.program_id(1)

        @pl.when(t == 0)
        def _():
            acc[...] = jnp.zeros_like(acc)

        acc[...] += _dot_tn(a_ref[...], b_ref[...].astype(BF16))

        @pl.when(t == nt - 1)
        def _():
            out_ref[...] = acc[...].astype(BF16)

    return pl.pallas_call(
        body, name=name, grid=(N // tn, nt),
        in_specs=[pl.BlockSpec((tk, M), lambda n, t: (t, 0)),
                  pl.BlockSpec((tk, tn), lambda n, t: (t, n))],
        out_specs=pl.BlockSpec((M, tn), lambda n, t: (0, n)),
        out_shape=jax.ShapeDtypeStruct((M, N), BF16),
        scratch_shapes=[pltpu.VMEM((M, tn), F32)],
        compiler_params=_params(("arbitrary", "arbitrary")),
    )(a, b)


def _final_loss(x, target, g, tm):
    T, D = x.shape

    def body(x_ref, t_ref, g_ref, dx_ref, loss_ref, dg_ref):
        @pl.when(pl.program_id(0) == 0)
        def _():
            loss_ref[...] = jnp.zeros_like(loss_ref)
            dg_ref[...] = jnp.zeros_like(dg_ref)

        xhat, inv = _rms(x_ref[...])
        err = xhat * g_ref[...] - t_ref[...]
        loss_ref[...] += 0.5 * jnp.sum(jnp.sum(err * err, axis=1, keepdims=True), axis=0, keepdims=True) / D
        dx, dg = _rms_bwd(err * (1.0 / D), xhat, inv, g_ref[...])
        dx_ref[...] = dx
        dg_ref[...] += dg

    row = pl.BlockSpec((tm, D), lambda i: (i, 0))
    return pl.pallas_call(
        body, name="final_loss", grid=(T // tm,),
        in_specs=[row, row, pl.BlockSpec((1, D), lambda i: (0, 0))],
        out_specs=[row, pl.BlockSpec((1, LANES), lambda i: (0, 0)), pl.BlockSpec((1, D), lambda i: (0, 0))],
        out_shape=[jax.ShapeDtypeStruct((T, D), F32), jax.ShapeDtypeStruct((1, LANES), F32),
                   jax.ShapeDtypeStruct((1, D), F32)],
        compiler_params=_params(("arbitrary",)),
    )(x, target, g)


def _heads(t, B, S):
    return jnp.transpose(t.reshape(B, S, N_HEADS, HEAD_DIM), (0, 2, 1, 3))


def _unheads(t, B, S):
    return jnp.transpose(t, (0, 2, 1, 3)).reshape(B * S, ATTN_W)


def _local_step(x, target, w, B, S):
    T, D = x.shape
    tm = min(512, T)
    tq = min(512, S)
    ch = min(256, S)

    x1, hg1, hu1 = _ffn_fwd("ffn1_fwd", x, w["ffn1_norm"], w["ffn1_gate"], w["ffn1_up"], w["ffn1_down"], tm)
    h, proj, flog = _mix_proj_fwd(x1, w["mix_norm"], w["w_proj"], w["w_f"], tm, 1280)
    cum = _fgate_fwd(flog, w["b_forget"], B, S, ch)
    cum3 = cum.reshape(B, S, LANES)
    cum_t = jnp.transpose(cum3[:, :, :8].reshape(B, S // tq, tq, 8), (0, 1, 3, 2))
    q = _heads(proj[:, 0:ATTN_W], B, S)
    k = _heads(proj[:, ATTN_W:2 * ATTN_W], B, S)
    v = _heads(proj[:, 2 * ATTN_W:3 * ATTN_W], B, S)
    o4, lse = _attn_fwd(q, k, v, cum3, cum_t, tq)
    o = _unheads(o4, B, S)
    yc = _conv_fwd(proj, w["conv_w"], B, S)
    x2 = _mix_out_fwd(x1, o, yc, proj, w["w_o_attn"], w["w_o_conv"], w["w_out"], tm)
    x3, hg2, hu2 = _ffn_fwd("ffn2_fwd", x2, w["ffn2_norm"], w["ffn2_gate"], w["ffn2_up"], w["ffn2_down"], tm)
    dx3, loss, d_final_norm = _final_loss(x3, target, w["final_norm"], tm)

    g = {"final_norm": d_final_norm}
    dx2, dhg2, dhu2, g["ffn2_norm"] = _ffn_bwd_dx("ffn2_bwd_dx", dx3, x2, w["ffn2_norm"], hg2, hu2,
                                                  w["ffn2_gate"], w["ffn2_up"], w["ffn2_down"], tm)
    g["ffn2_gate"], g["ffn2_up"], g["ffn2_down"] = _ffn_bwd_dw("ffn2_bwd_dw", dx3, x2, w["ffn2_norm"],
                                                               hg2, hu2, dhg2, dhu2, tm)
    do, dyc, dgates, g["w_o_attn"], g["w_o_conv"], g["w_out"] = _mix_out_bwd(
        dx2, o, yc, proj, w["w_o_attn"], w["w_o_conv"], w["w_out"], tm)
    dq4, dk4, dv4, dcq, dck = _attn_bwd(q, k, v, o4, _heads(do, B, S), lse, cum3, cum_t, tq)
    dcum = jnp.pad(jnp.transpose(dcq.reshape(B, N_HEADS, S) + dck.reshape(B, N_HEADS, S), (0, 2, 1)).reshape(T, N_HEADS),
                   ((0, 0), (0, LANES - N_HEADS)))
    dflog, g["b_forget"] = _fgate_bwd(dcum, flog, w["b_forget"], B, S, ch)
    dcb, dcc, dcx, g["conv_w"] = _conv_bwd(dyc, proj, w["conv_w"], B, S)
    dqkv = jnp.concatenate([_unheads(dq4, B, S), _unheads(dk4, B, S), _unheads(dv4, B, S)], axis=1)
    wp = w["w_proj"]
    pieces = [dqkv, dcb, dcc, dcx, dgates, dflog]
    weights = [wp[:, 0:COL_CB], wp[:, COL_CB:COL_CC], wp[:, COL_CC:COL_CX], wp[:, COL_CX:COL_GATES],
               wp[:, COL_GATES:], w["w_f"]]
    tmx = min(256, T)
    dx1, g["mix_norm"] = _mix_proj_bwd_dx(dx2, x1, w["mix_norm"], pieces, weights, tmx)
    g["w_proj"] = [_matmul_tn("mix_dw_%d" % j, h, p, min(512, p.shape[1]), tm) for j, p in enumerate(pieces)]
    grad_x, dhg1, dhu1, g["ffn1_norm"] = _ffn_bwd_dx("ffn1_bwd_dx", dx1, x, w["ffn1_norm"], hg1, hu1,
                                                     w["ffn1_gate"], w["ffn1_up"], w["ffn1_down"], tm)
    g["ffn1_gate"], g["ffn1_up"], g["ffn1_down"] = _ffn_bwd_dw("ffn1_bwd_dw", dx1, x, w["ffn1_norm"],
                                                               hg1, hu1, dhg1, dhu1, tm)
    return loss, grad_x, g


N_FORGET_COL = 3 * ATTN_W
COLUMN_SHARDED = ("ffn1_gate", "ffn1_up", "ffn2_gate", "ffn2_up", "w_in", "w_o_attn", "w_o_conv")
ROW_SHARDED = ("ffn1_down", "ffn2_down", "w_out")
NORMS = ("ffn1_norm", "mix_norm", "ffn2_norm", "final_norm")


def _unshard_cols(a):
    return jnp.transpose(a, (1, 0, 2)).reshape(a.shape[1], N_CHIPS * a.shape[2])


def _shard_cols(a):
    return jnp.transpose(a.reshape(a.shape[0], N_CHIPS, a.shape[1] // N_CHIPS), (1, 0, 2))


def _compute_layout(gw, small):
    w = {n: gw[n] for n in ("ffn1_gate", "ffn1_up", "ffn1_down", "ffn2_gate", "ffn2_up", "ffn2_down")}
    win = _unshard_cols(gw["w_in"])
    w["w_proj"] = jnp.concatenate([win[:, :N_FORGET_COL], win[:, N_FORGET_COL + N_HEADS:]], axis=1)
    w["w_f"] = jnp.pad(win[:, N_FORGET_COL:N_FORGET_COL + N_HEADS], ((0, 0), (0, LANES - N_HEADS)))
    w["w_o_attn"] = _unshard_cols(gw["w_o_attn"])
    w["w_o_conv"] = _unshard_cols(gw["w_o_conv"])
    w["w_out"] = gw["w_out"].reshape(-1, gw["w_out"].shape[2])
    w["conv_w"] = _unshard_cols(gw["conv_w"])
    for n in NORMS:
        w[n] = small[n].reshape(1, -1)
    w["b_forget"] = jnp.pad(small["b_forget"].reshape(1, -1), ((0, 0), (0, LANES - N_HEADS)))
    return w


def _grads_to_shard_major(g):
    out = {n: g[n] for n in ("ffn1_gate", "ffn1_up", "ffn1_down", "ffn2_gate", "ffn2_up", "ffn2_down")}
    dqkv, dcb, dcc, dcx, dgates, df = g["w_proj"]
    out["w_in"] = _shard_cols(jnp.concatenate([dqkv, df[:, :N_HEADS], dcb, dcc, dcx, dgates], axis=1))
    out["w_o_attn"] = _shard_cols(g["w_o_attn"])
    out["w_o_conv"] = _shard_cols(g["w_o_conv"])
    out["w_out"] = g["w_out"].reshape(N_CHIPS, -1, g["w_out"].shape[1])
    for n in NORMS + ("b_forget", "conv_w"):
        out[n] = g[n]
    return out


ANY = pl.BlockSpec(memory_space=pl.ANY)
BIG = ("ffn1_gate", "ffn1_up", "ffn1_down", "w_in", "w_o_attn", "w_o_conv", "w_out",
       "ffn2_gate", "ffn2_up", "ffn2_down")


def _place():
    x, y, c = lax.axis_index("x"), lax.axis_index("y"), lax.axis_index("c")
    others = [(1 - x, y), (x, 1 - y), (1 - x, 1 - y)]
    return x, y, c, others


def _gather_weights(shards, conv_shard):
    n = len(shards)

    def body(*refs):
        srcs, conv_src = refs[:n], refs[n]
        dsts, conv_dst = refs[n + 1:2 * n + 1], refs[2 * n + 1]
        send_sems, recv_sems, pass_send, pass_recv, conv_send, conv_recv = refs[2 * n + 2:]
        x, y, c, others = _place()
        me = 2 * x + y

        def halves(a):
            hr = srcs[a].shape[0] // 2
            return pl.ds(c * hr, hr), pl.ds((1 - c) * hr, hr)

        def chip_copy(a, j, chip):
            mine, _ = halves(a)
            return pltpu.make_async_remote_copy(
                src_ref=srcs[a].at[mine], dst_ref=dsts[a].at[chip, mine],
                send_sem=send_sems.at[3 * a + j], recv_sem=recv_sems.at[3 * a + j],
                device_id=(*others[j], c), device_id_type=MESH)

        def pass_copy(a, j, chip, half):
            return pltpu.make_async_remote_copy(
                src_ref=dsts[a].at[chip, half], dst_ref=dsts[a].at[chip, half],
                send_sem=pass_send.at[3 * a + j], recv_sem=pass_recv.at[3 * a + j],
                device_id=(x, y, 1 - c), device_id_type=MESH)

        def conv_copy(j, chip):
            return pltpu.make_async_remote_copy(
                src_ref=conv_src, dst_ref=conv_dst.at[chip],
                send_sem=conv_send.at[j], recv_sem=conv_recv.at[j],
                device_id=(*others[j], c), device_id_type=MESH)

        sends = [conv_copy(j, me) for j in range(3)] + [chip_copy(a, j, me) for a in range(n) for j in range(3)]
        for cp in sends:
            cp.start()
        passed = []
        for a in range(n):
            mine, _ = halves(a)
            for j, (ox, oy) in enumerate(others):
                chip_copy(a, j, 2 * ox + oy).wait_recv()
                passed.append(pass_copy(a, j, 2 * ox + oy, mine))
                passed[-1].start()
        for a in range(n):
            _, theirs = halves(a)
            for j, (ox, oy) in enumerate(others):
                pass_copy(a, j, 2 * ox + oy, theirs).wait_recv()
        for j, (ox, oy) in enumerate(others):
            conv_copy(j, 2 * ox + oy).wait_recv()
        for cp in sends + passed:
            cp.wait_send()

    stack = lambda s: jax.ShapeDtypeStruct((N_CHIPS,) + s.shape, s.dtype)
    stacks = pl.pallas_call(
        body, name="gather_weights",
        in_specs=[ANY] * (n + 1), out_specs=[ANY] * (n + 1),
        out_shape=[stack(s) for s in shards] + [stack(conv_shard)],
        scratch_shapes=[pltpu.SemaphoreType.DMA((3 * n,)), pltpu.SemaphoreType.DMA((3 * n,)),
                        pltpu.SemaphoreType.DMA((3 * n,)), pltpu.SemaphoreType.DMA((3 * n,)),
                        pltpu.SemaphoreType.DMA((3,)), pltpu.SemaphoreType.DMA((3,))],
    )(*shards, conv_shard)
    chip = 2 * lax.axis_index("x") + lax.axis_index("y")
    return [lax.dynamic_update_index_in_dim(st, s, chip, 0) for st, s in zip(stacks, list(shards) + [conv_shard])]


def _sibling_exchange(grads):
    n = len(grads)

    def body(*refs):
        srcs, dsts = refs[:n], refs[n:2 * n]
        send_sems, recv_sems = refs[2 * n:]
        x, y, c, _ = _place()
        copies = []
        for a in range(n):
            hr = srcs[a].shape[1] // 2
            copies.append(pltpu.make_async_remote_copy(
                src_ref=srcs[a].at[:, pl.ds((1 - c) * hr, hr)], dst_ref=dsts[a],
                send_sem=send_sems.at[a], recv_sem=recv_sems.at[a],
                device_id=(x, y, 1 - c), device_id_type=MESH))
        for cp in copies:
            cp.start()
        for cp in copies:
            cp.wait()

    half = lambda s: jax.ShapeDtypeStruct((s.shape[0], s.shape[1] // 2, s.shape[2]), s.dtype)
    return pl.pallas_call(
        body, name="sibling_exchange",
        in_specs=[ANY] * n, out_specs=[ANY] * n, out_shape=[half(s) for s in grads],
        scratch_shapes=[pltpu.SemaphoreType.DMA((n,)), pltpu.SemaphoreType.DMA((n,))],
    )(*grads)


def _add_halves(name, grad, recv, core):
    K, r, cols = grad.shape
    hr = r // 2

    def body(core_ref, g_ref, r_ref, out_ref):
        out_ref[...] = (g_ref[...].astype(F32) + r_ref[...].astype(F32)).astype(BF16)

    return pl.pallas_call(
        body, name=name,
        grid_spec=pltpu.PrefetchScalarGridSpec(
            num_scalar_prefetch=1, grid=(K,),
            in_specs=[pl.BlockSpec((None, hr, cols), lambda k, core_ref: (k, core_ref[0], 0)),
                      pl.BlockSpec((None, hr, cols), lambda k, core_ref: (k, 0, 0))],
            out_specs=pl.BlockSpec((None, hr, cols), lambda k, core_ref: (k, 0, 0))),
        out_shape=jax.ShapeDtypeStruct((K, hr, cols), BF16),
        compiler_params=_params(("arbitrary",)),
    )(core, grad, recv)


def _chip_exchange(parts):
    n = len(parts)

    def body(*refs):
        srcs, dsts = refs[:n], refs[n:2 * n]
        send_sems, recv_sems = refs[2 * n:]
        x, y, c, others = _place()

        def copy(a, j):
            ox, oy = others[j]
            return pltpu.make_async_remote_copy(
                src_ref=srcs[a].at[2 * ox + oy], dst_ref=dsts[a].at[j],
                send_sem=send_sems.at[3 * a + j], recv_sem=recv_sems.at[3 * a + j],
                device_id=(ox, oy, c), device_id_type=MESH)

        copies = [copy(a, j) for a in range(n) for j in range(3)]
        for cp in copies:
            cp.start()
        for cp in copies:
            cp.wait()

    return pl.pallas_call(
        body, name="chip_exchange",
        in_specs=[ANY] * n, out_specs=[ANY] * n,
        out_shape=[jax.ShapeDtypeStruct((3,) + s.shape[1:], s.dtype) for s in parts],
        scratch_shapes=[pltpu.SemaphoreType.DMA((3 * n,)), pltpu.SemaphoreType.DMA((3 * n,))],
    )(*parts)


def _sum_chips(name, own, recv, chip):
    _, hr, cols = own.shape
    tr = hr // 2

    def body(chip_ref, own_ref, recv_ref, out_ref):
        acc = own_ref[...].astype(F32)
        for j in range(3):
            acc = acc + recv_ref[j].astype(F32)
        out_ref[...] = acc

    return pl.pallas_call(
        body, name=name,
        grid_spec=pltpu.PrefetchScalarGridSpec(
            num_scalar_prefetch=1, grid=(hr // tr,),
            in_specs=[pl.BlockSpec((None, tr, cols), lambda i, chip_ref: (chip_ref[0], i, 0)),
                      pl.BlockSpec((3, tr, cols), lambda i, chip_ref: (0, i, 0))],
            out_specs=pl.BlockSpec((tr, cols), lambda i, chip_ref: (i, 0))),
        out_shape=jax.ShapeDtypeStruct((hr, cols), F32),
        compiler_params=_params(("arbitrary",)),
    )(chip, own, recv)


def _share_halves(halves):
    n = len(halves)

    def body(*refs):
        srcs, dsts = refs[:n], refs[n:2 * n]
        send_sems, recv_sems = refs[2 * n:]
        x, y, c, _ = _place()
        copies = [pltpu.make_async_remote_copy(
            src_ref=srcs[a], dst_ref=dsts[a], send_sem=send_sems.at[a], recv_sem=recv_sems.at[a],
            device_id=(x, y, 1 - c), device_id_type=MESH) for a in range(n)]
        for cp in copies:
            cp.start()
        for cp in copies:
            cp.wait()

    return pl.pallas_call(
        body, name="share_halves",
        in_specs=[ANY] * n, out_specs=[ANY] * n,
        out_shape=[jax.ShapeDtypeStruct(s.shape, s.dtype) for s in halves],
        scratch_shapes=[pltpu.SemaphoreType.DMA((n,)), pltpu.SemaphoreType.DMA((n,))],
    )(*halves)


def _allreduce_small(part):
    rows = part.shape[0]

    def body(in_ref, out_ref, land, send_sems, recv_sems):
        x, y, c, _ = _place()
        me = 4 * x + 2 * y + c
        land[me] = in_ref[...]
        copies = []
        for d in range(1, N_DEV):
            peer = (1 - x if d & 4 else x, 1 - y if d & 2 else y, 1 - c if d & 1 else c)
            copies.append(pltpu.make_async_remote_copy(
                src_ref=in_ref, dst_ref=land.at[me],
                send_sem=send_sems.at[d - 1], recv_sem=recv_sems.at[d - 1],
                device_id=peer, device_id_type=MESH))
        for cp in copies:
            cp.start()
        for d in range(1, N_DEV):
            px, py, pc = (1 - x if d & 4 else x, 1 - y if d & 2 else y, 1 - c if d & 1 else c)
            pltpu.make_async_remote_copy(
                src_ref=in_ref, dst_ref=land.at[4 * px + 2 * py + pc],
                send_sem=send_sems.at[d - 1], recv_sem=recv_sems.at[d - 1],
                device_id=(px, py, pc), device_id_type=MESH).wait_recv()
        for cp in copies:
            cp.wait_send()
        acc = land[0]
        for k in range(1, N_DEV):
            acc = acc + land[k]
        out_ref[...] = acc

    vmem = pl.BlockSpec(memory_space=pltpu.VMEM)
    return pl.pallas_call(
        body, name="allreduce_small",
        in_specs=[vmem], out_specs=vmem,
        out_shape=jax.ShapeDtypeStruct(part.shape, F32),
        scratch_shapes=[pltpu.VMEM((N_DEV, rows, LANES), F32),
                        pltpu.SemaphoreType.DMA((N_DEV - 1,)), pltpu.SemaphoreType.DMA((N_DEV - 1,))],
    )(part)


def _adamw(name, w, g, m, v, tr):
    rows, cols = w.shape
    c1 = 1.0 / (1.0 - ADAM_B1 ** ADAM_STEP)
    c2 = 1.0 / (1.0 - ADAM_B2 ** ADAM_STEP)

    def body(w_ref, g_ref, m_ref, v_ref, d_ref, nm_ref, nv_ref):
        gv = g_ref[...]
        nm = ADAM_B1 * m_ref[...] + (1.0 - ADAM_B1) * gv
        nv = ADAM_B2 * v_ref[...] + (1.0 - ADAM_B2) * (gv * gv)
        nm_ref[...] = nm
        nv_ref[...] = nv
        d_ref[...] = -ADAM_LR * ((nm * c1) / (jnp.sqrt(nv * c2) + ADAM_EPS) + ADAM_WD * w_ref[...])

    spec = pl.BlockSpec((tr, cols), lambda i: (i, 0))
    out = jax.ShapeDtypeStruct((rows, cols), F32)
    return pl.pallas_call(
        body, name=name, grid=(rows // tr,),
        in_specs=[spec] * 4, out_specs=[spec] * 3, out_shape=[out] * 3,
        compiler_params=_params(("arbitrary",)),
    )(w, g, m, v)


def _adamw_halves(name, w, mine, theirs, m, v, core):
    rows, cols = w.shape
    hr = rows // 2
    tr = hr // 2
    nt = hr // tr
    c1 = 1.0 / (1.0 - ADAM_B1 ** ADAM_STEP)
    c2 = 1.0 / (1.0 - ADAM_B2 ** ADAM_STEP)

    def body(core_ref, w_ref, mine_ref, theirs_ref, m_ref, v_ref, g_ref, d_ref, nm_ref, nv_ref):
        gv = jnp.where(pl.program_id(0) == core_ref[0], mine_ref[...], theirs_ref[...])
        nm = ADAM_B1 * m_ref[...] + (1.0 - ADAM_B1) * gv
        nv = ADAM_B2 * v_ref[...] + (1.0 - ADAM_B2) * (gv * gv)
        g_ref[...] = gv
        nm_ref[...] = nm
        nv_ref[...] = nv
        d_ref[...] = -ADAM_LR * ((nm * c1) / (jnp.sqrt(nv * c2) + ADAM_EPS) + ADAM_WD * w_ref[...])

    whole = pl.BlockSpec((tr, cols), lambda h, i, core_ref: (h * nt + i, 0))
    half = pl.BlockSpec((tr, cols), lambda h, i, core_ref: (i, 0))
    out = jax.ShapeDtypeStruct((rows, cols), F32)
    return pl.pallas_call(
        body, name=name,
        grid_spec=pltpu.PrefetchScalarGridSpec(
            num_scalar_prefetch=1, grid=(2, nt),
            in_specs=[whole, half, half, whole, whole], out_specs=[whole] * 4),
        out_shape=[out] * 4,
        compiler_params=_params(("arbitrary", "arbitrary")),
    )(core, w, mine, theirs, m, v)


WEIGHTS = ("ffn1_norm", "ffn1_gate", "ffn1_up", "ffn1_down", "mix_norm", "w_in", "b_forget", "conv_w",
           "w_o_attn", "w_o_conv", "w_out", "ffn2_norm", "ffn2_gate", "ffn2_up", "ffn2_down", "final_norm")
VEC_ROWS = 8


def _pack_small(t, conv_rows):
    conv = t["conv_w"]
    parts = [t[n].reshape(VEC_ROWS, LANES) for n in NORMS]
    parts.append(jnp.pad(conv, ((0, conv_rows - conv.shape[0]), (0, 0))))
    parts.append(jnp.pad(t["b_forget"].reshape(1, N_HEADS), ((0, 7), (0, LANES - N_HEADS))))
    return jnp.concatenate(parts, axis=0)


def _unpack_small(p, conv_rows):
    out = {n: p[VEC_ROWS * i:VEC_ROWS * (i + 1)].reshape(-1) for i, n in enumerate(NORMS)}
    base = VEC_ROWS * len(NORMS)
    out["conv_w"] = p[base:base + 3]
    out["b_forget"] = p[base + conv_rows, :N_HEADS]
    return out


def kernel(x, ffn1_norm, ffn1_gate, ffn1_up, ffn1_down, mix_norm, w_in, b_forget, conv_w, w_o_attn, w_o_conv, w_out, ffn2_norm, ffn2_gate, ffn2_up, ffn2_down, final_norm, loss_target, m_ffn1_norm, m_ffn1_gate, m_ffn1_up, m_ffn1_down, m_mix_norm, m_w_in, m_b_forget, m_conv_w, m_w_o_attn, m_w_o_conv, m_w_out, m_ffn2_norm, m_ffn2_gate, m_ffn2_up, m_ffn2_down, m_final_norm, v_ffn1_norm, v_ffn1_gate, v_ffn1_up, v_ffn1_down, v_mix_norm, v_w_in, v_b_forget, v_conv_w, v_w_o_attn, v_w_o_conv, v_w_out, v_ffn2_norm, v_ffn2_gate, v_ffn2_up, v_ffn2_down, v_final_norm):
    given = dict(locals())
    wts = {n: given[n] for n in WEIGHTS}
    mom = {n: given["m_" + n] for n in WEIGHTS}
    var = {n: given["v_" + n] for n in WEIGHTS}
    B, S, D = x.shape
    chip = 2 * lax.axis_index("x") + lax.axis_index("y")
    chip1 = chip.astype(jnp.int32).reshape(1)
    core = lax.axis_index("c").astype(jnp.int32).reshape(1)

    conv_shard = jnp.pad(conv_w, ((0, 8 - conv_w.shape[0]), (0, 0)))
    gathered = _gather_weights([wts[n].astype(BF16) for n in BIG], conv_shard)
    gw = dict(zip(BIG + ("conv_w",), gathered))
    w = _compute_layout(gw, wts)

    loss, grad_x, g = _local_step(x.reshape(B * S, D), loss_target.reshape(B * S, D), w, B, S)
    gs = _grads_to_shard_major(g)

    partial = [gs[n] for n in BIG]
    from_sibling = _sibling_exchange(partial)
    chip_part = [_add_halves("add_halves_" + n, p, r, core) for n, p, r in zip(BIG, partial, from_sibling)]
    from_chips = _chip_exchange(chip_part)
    mine = [_sum_chips("sum_chips_" + n, p, r, chip1) for n, p, r in zip(BIG, chip_part, from_chips)]
    theirs = _share_halves(mine)

    conv_all = _shard_cols(gs["conv_w"]).reshape(N_CHIPS * 8, LANES)
    small_part = _pack_small({**{n: gs[n] for n in NORMS}, "conv_w": conv_all, "b_forget": gs["b_forget"][0, :N_HEADS]},
                             N_CHIPS * 8)
    base = VEC_ROWS * len(NORMS)
    small_sum = _allreduce_small(small_part)
    small = _unpack_small(small_sum, N_CHIPS * 8)
    small["conv_w"] = lax.dynamic_slice_in_dim(small_sum[base:base + N_CHIPS * 8], chip * 8, 8, axis=0)[:3]
    grads = dict(small)

    delta, new_m, new_v = {}, {}, {}
    for n, gm, gt in zip(BIG, mine, theirs):
        grads[n], delta[n], new_m[n], new_v[n] = _adamw_halves("adamw_" + n, wts[n], gm, gt, mom[n], var[n], core)
    packs = [_pack_small(t, 8) for t in (wts, grads, mom, var)]
    for out, p in zip((delta, new_m, new_v), _adamw("adamw_small", *packs, packs[0].shape[0])):
        out.update(_unpack_small(p, 8))

    total = lax.psum(loss[0, 0], ("x", "y", "c"))
    return (total, grad_x.reshape(B, S, D), *[grads[n] for n in WEIGHTS], *[delta[n] for n in WEIGHTS],
            *[new_m[n] for n in WEIGHTS], *[new_v[n] for n in WEIGHTS])
```

```python
import math

import jax
import jax.numpy as jnp
from jax import lax
from jax.experimental import pallas as pl
from jax.experimental.pallas import tpu as pltpu

F32 = jnp.float32
BF16 = jnp.bfloat16
MESH = pl.DeviceIdType.MESH

N_CHIPS = 4
N_DEV = 8
N_HEADS = 8
HEAD_DIM = 64
HEAD_PAIRS = N_HEADS // 2
ATTN_W = N_HEADS * HEAD_DIM
CONV_W = 512
RMS_EPS = 1e-6
FFN_RES = 0.5
LANES = 128
VMEM_LIMIT = 56 * 1024 * 1024

ADAM_LR = 0.001
ADAM_B1 = 0.9
ADAM_B2 = 0.999
ADAM_EPS = 1e-08
ADAM_WD = 0.01
ADAM_STEP = 10

PROJ_W = 3 * ATTN_W + 3 * CONV_W + 2 * 1024
COL_CB, COL_CC, COL_CX = 3 * ATTN_W, 3 * ATTN_W + CONV_W, 3 * ATTN_W + 2 * CONV_W
COL_GATES = 3 * ATTN_W + 3 * CONV_W
N_FORGET_COL = 3 * ATTN_W


def _params(sem=None, vmem=VMEM_LIMIT):
    return pltpu.CompilerParams(dimension_semantics=sem, vmem_limit_bytes=vmem)


def _dot(a, b):
    return lax.dot_general(a, b, (((1,), (0,)), ((), ())), preferred_element_type=F32)


def _dot_nt(a, b):
    return lax.dot_general(a, b, (((1,), (1,)), ((), ())), preferred_element_type=F32)


def _dot_tn(a, b):
    return lax.dot_general(a, b, (((0,), (0,)), ((), ())), preferred_element_type=F32)


def _sigmoid(x):
    return 1.0 / (1.0 + jnp.exp(-x))


def _rms(xv):
    inv = lax.rsqrt(jnp.mean(xv * xv, axis=-1, keepdims=True) + RMS_EPS)
    return xv * inv, inv


def _rms_bwd(dn, xhat, inv, g):
    dxhat = dn * g
    dx = inv * (dxhat - xhat * jnp.mean(dxhat * xhat, axis=-1, keepdims=True))
    return dx, jnp.sum(dn * xhat, axis=0, keepdims=True)


def _ffn_fwd(name, x, g, wgt, wut, wd, tm):
    T, D = x.shape
    K, Fs, _ = wgt.shape

    def body(x_ref, g_ref, wg_ref, wu_ref, wd_ref, out_ref, hg_ref, hu_ref, n_scr, acc_scr):
        k = pl.program_id(1)

        @pl.when(k == 0)
        def _():
            xhat, _ = _rms(x_ref[...])
            n_scr[...] = (xhat * g_ref[...]).astype(BF16)
            acc_scr[...] = jnp.zeros_like(acc_scr)

        n = n_scr[...]
        hg = _dot_nt(n, wg_ref[...])
        hu = _dot_nt(n, wu_ref[...])
        hg_ref[...] = hg.astype(BF16)
        hu_ref[...] = hu.astype(BF16)
        act = (hg * _sigmoid(hg) * hu).astype(BF16)
        acc_scr[...] += _dot(act, wd_ref[...])

        @pl.when(k == K - 1)
        def _():
            out_ref[...] = x_ref[...] + FFN_RES * acc_scr[...]

    w_spec = pl.BlockSpec((None, Fs, D), lambda i, k: (k, 0, 0))
    act_spec = pl.BlockSpec((None, tm, Fs), lambda i, k: (k, i, 0))
    return pl.pallas_call(
        body, name=name, grid=(T // tm, K),
        in_specs=[pl.BlockSpec((tm, D), lambda i, k: (i, 0)),
                  pl.BlockSpec((1, D), lambda i, k: (0, 0)),
                  w_spec, w_spec, w_spec],
        out_specs=[pl.BlockSpec((tm, D), lambda i, k: (i, 0)), act_spec, act_spec],
        out_shape=[jax.ShapeDtypeStruct((T, D), F32),
                   jax.ShapeDtypeStruct((K, T, Fs), BF16),
                   jax.ShapeDtypeStruct((K, T, Fs), BF16)],
        scratch_shapes=[pltpu.VMEM((tm, D), BF16), pltpu.VMEM((tm, D), F32)],
        compiler_params=_params(("arbitrary", "arbitrary")),
    )(x, g, wgt, wut, wd)


def _ffn_bwd_dx(name, dout, x, g, hg, hu, wgt, wut, wd, tm):
    T, D = x.shape
    K, Fs, _ = wgt.shape

    def body(dout_ref, x_ref, g_ref, hg_ref, hu_ref, wg_ref, wu_ref, wd_ref,
             dx_ref, dhg_ref, dhu_ref, dg_ref, df_scr, dn_scr):
        i, k = pl.program_id(0), pl.program_id(1)

        @pl.when(k == 0)
        def _():
            df_scr[...] = (FFN_RES * dout_ref[...]).astype(BF16)
            dn_scr[...] = jnp.zeros_like(dn_scr)

        @pl.when((k == 0) & (i == 0))
        def _():
            dg_ref[...] = jnp.zeros_like(dg_ref)

        dact = _dot_nt(df_scr[...], wd_ref[...])
        hgv = hg_ref[...].astype(F32)
        huv = hu_ref[...].astype(F32)
        s = _sigmoid(hgv)
        dhu = (dact * (hgv * s)).astype(BF16)
        dhg = (dact * huv * (s * (1.0 + hgv * (1.0 - s)))).astype(BF16)
        dhg_ref[...] = dhg
        dhu_ref[...] = dhu
        dn_scr[...] += _dot(dhg, wg_ref[...]) + _dot(dhu, wu_ref[...])

        @pl.when(k == K - 1)
        def _():
            xhat, inv = _rms(x_ref[...])
            dx, dg = _rms_bwd(dn_scr[...], xhat, inv, g_ref[...])
            dx_ref[...] = dout_ref[...] + dx
            dg_ref[...] += dg

    w_spec = pl.BlockSpec((None, Fs, D), lambda i, k: (k, 0, 0))
    act_spec = pl.BlockSpec((None, tm, Fs), lambda i, k: (k, i, 0))
    row = pl.BlockSpec((tm, D), lambda i, k: (i, 0))
    vec = pl.BlockSpec((1, D), lambda i, k: (0, 0))
    return pl.pallas_call(
        body, name=name, grid=(T // tm, K),
        in_specs=[row, row, vec, act_spec, act_spec, w_spec, w_spec, w_spec],
        out_specs=[row, act_spec, act_spec, vec],
        out_shape=[jax.ShapeDtypeStruct((T, D), F32),
                   jax.ShapeDtypeStruct((K, T, Fs), BF16),
                   jax.ShapeDtypeStruct((K, T, Fs), BF16),
                   jax.ShapeDtypeStruct((1, D), F32)],
        scratch_shapes=[pltpu.VMEM((tm, D), BF16), pltpu.VMEM((tm, D), F32)],
        compiler_params=_params(("arbitrary", "arbitrary")),
    )(dout, x, g, hg, hu, wgt, wut, wd)


def _ffn_bwd_dw(name, dout, x, g, hg, hu, dhg, dhu, tk):
    T, D = x.shape
    K, _, Fs = hg.shape
    nt = T // tk

    def body(dout_ref, x_ref, g_ref, hg_ref, hu_ref, dhg_ref, dhu_ref,
             dwg_ref, dwu_ref, dwd_ref, accg, accu, accd):
        t = pl.program_id(1)

        @pl.when(t == 0)
        def _():
            accg[...] = jnp.zeros_like(accg)
            accu[...] = jnp.zeros_like(accu)
            accd[...] = jnp.zeros_like(accd)

        xhat, _ = _rms(x_ref[...])
        n = (xhat * g_ref[...]).astype(BF16)
        df = (FFN_RES * dout_ref[...]).astype(BF16)
        hgv = hg_ref[...].astype(F32)
        act = (hgv * _sigmoid(hgv) * hu_ref[...].astype(F32)).astype(BF16)
        accg[...] += _dot_tn(dhg_ref[...], n)
        accu[...] += _dot_tn(dhu_ref[...], n)
        accd[...] += _dot_tn(act, df)

        @pl.when(t == nt - 1)
        def _():
            dwg_ref[...] = accg[...].astype(BF16)
            dwu_ref[...] = accu[...].astype(BF16)
            dwd_ref[...] = accd[...].astype(BF16)

    act_spec = pl.BlockSpec((None, tk, Fs), lambda k, t: (k, t, 0))
    w_spec = pl.BlockSpec((None, Fs, D), lambda k, t: (k, 0, 0))
    return pl.pallas_call(
        body, name=name, grid=(K, nt),
        in_specs=[pl.BlockSpec((tk, D), lambda k, t: (t, 0)),
                  pl.BlockSpec((tk, D), lambda k, t: (t, 0)),
                  pl.BlockSpec((1, D), lambda k, t: (0, 0)),
                  act_spec, act_spec, act_spec, act_spec],
        out_specs=[w_spec, w_spec, w_spec],
        out_shape=[jax.ShapeDtypeStruct((K, Fs, D), BF16)] * 3,
        scratch_shapes=[pltpu.VMEM((Fs, D), F32)] * 3,
        compiler_params=_params(("arbitrary", "arbitrary")),
    )(dout, x, g, hg, hu, dhg, dhu)


def _mix_proj_fwd(x, g, wproj_t, wf_t, tm, tn):
    T, D = x.shape
    N = wproj_t.shape[0]

    def body(x_ref, g_ref, w_ref, wf_ref, h_ref, proj_ref, flog_ref, h_scr):
        @pl.when(pl.program_id(1) == 0)
        def _():
            xhat, _ = _rms(x_ref[...])
            h = (xhat * g_ref[...]).astype(BF16)
            h_scr[...] = h
            h_ref[...] = h
            flog_ref[...] = _dot_nt(h, wf_ref[...])

        proj_ref[...] = _dot_nt(h_scr[...], w_ref[...]).astype(BF16)

    return pl.pallas_call(
        body, name="mix_proj_fwd", grid=(T // tm, N // tn),
        in_specs=[pl.BlockSpec((tm, D), lambda i, n: (i, 0)),
                  pl.BlockSpec((1, D), lambda i, n: (0, 0)),
                  pl.BlockSpec((tn, D), lambda i, n: (n, 0)),
                  pl.BlockSpec((LANES, D), lambda i, n: (0, 0))],
        out_specs=[pl.BlockSpec((tm, D), lambda i, n: (i, 0)),
                   pl.BlockSpec((tm, tn), lambda i, n: (i, n)),
                   pl.BlockSpec((tm, LANES), lambda i, n: (i, 0))],
        out_shape=[jax.ShapeDtypeStruct((T, D), BF16),
                   jax.ShapeDtypeStruct((T, N), BF16),
                   jax.ShapeDtypeStruct((T, LANES), F32)],
        scratch_shapes=[pltpu.VMEM((tm, D), BF16)],
        compiler_params=_params(("arbitrary", "arbitrary")),
    )(x, g, wproj_t, wf_t)


def _log_sigmoid(z):
    return -(jnp.maximum(-z, 0.0) + jnp.log(1.0 + jnp.exp(-jnp.abs(z))))


def _tri(n, lower):
    r = lax.broadcasted_iota(jnp.int32, (n, n), 0)
    c = lax.broadcasted_iota(jnp.int32, (n, n), 1)
    return jnp.where((r >= c) if lower else (r <= c), 1.0, 0.0).astype(F32)


def _dot_f32(a, b):
    return lax.dot_general(a, b, (((1,), (0,)), ((), ())), preferred_element_type=F32,
                           precision=lax.Precision.HIGHEST)


def _fgate_fwd(flog, bias, B, S, ch):
    def body(flog_ref, b_ref, cum_ref):
        tri = _tri(ch, True)
        carry = jnp.zeros((1, LANES), F32)
        for c0 in range(0, S, ch):
            lf = _log_sigmoid(flog_ref[c0:c0 + ch, :] + b_ref[...])
            cs = _dot_f32(tri, lf) + carry
            cum_ref[c0:c0 + ch, :] = cs
            carry = cs[ch - 1:ch, :]

    return pl.pallas_call(
        body, name="fgate_fwd", grid=(B,),
        in_specs=[pl.BlockSpec((S, LANES), lambda b: (b, 0)),
                  pl.BlockSpec((1, LANES), lambda b: (0, 0))],
        out_specs=pl.BlockSpec((S, LANES), lambda b: (b, 0)),
        out_shape=jax.ShapeDtypeStruct((B * S, LANES), F32),
        compiler_params=_params(("arbitrary",)),
    )(flog, bias)


def _fgate_bwd(dcum, flog, bias, B, S, ch):
    def body(dcum_ref, flog_ref, b_ref, dflog_ref, db_ref):
        @pl.when(pl.program_id(0) == 0)
        def _():
            db_ref[...] = jnp.zeros_like(db_ref)

        tri = _tri(ch, False)
        carry = jnp.zeros((1, LANES), F32)
        db = jnp.zeros((1, LANES), F32)
        for c0 in range(S - ch, -1, -ch):
            dlf = _dot_f32(tri, dcum_ref[c0:c0 + ch, :]) + carry
            carry = dlf[0:1, :]
            z = flog_ref[c0:c0 + ch, :] + b_ref[...]
            dz = dlf * _sigmoid(-z)
            dflog_ref[c0:c0 + ch, :] = dz
            db = db + jnp.sum(dz, axis=0, keepdims=True)
        db_ref[...] += db

    return pl.pallas_call(
        body, name="fgate_bwd", grid=(B,),
        in_specs=[pl.BlockSpec((S, LANES), lambda b: (b, 0)),
                  pl.BlockSpec((S, LANES), lambda b: (b, 0)),
                  pl.BlockSpec((1, LANES), lambda b: (0, 0))],
        out_specs=[pl.BlockSpec((S, LANES), lambda b: (b, 0)),
                   pl.BlockSpec((1, LANES), lambda b: (0, 0))],
        out_shape=[jax.ShapeDtypeStruct((B * S, LANES), F32),
                   jax.ShapeDtypeStruct((1, LANES), F32)],
        compiler_params=_params(("arbitrary",)),
    )(dcum, flog, bias)


def _pick_lane(tile, h):
    lane = lax.broadcasted_iota(jnp.int32, tile.shape, 1)
    return jnp.sum(jnp.where(lane == h, tile, 0.0), axis=1, keepdims=True)


def _put_lane(col, h, width=LANES):
    lane = lax.broadcasted_iota(jnp.int32, (col.shape[0], width), 1)
    return jnp.where(lane == h, col, 0.0)


def _pick_row(tile, h):
    row = lax.broadcasted_iota(jnp.int32, tile.shape, 0)
    return jnp.sum(jnp.where(row == h, tile, 0.0), axis=0, keepdims=True)


def _put_row(vec, h):
    row = lax.broadcasted_iota(jnp.int32, (8, vec.shape[1]), 0)
    return jnp.where(row == h, vec, 0.0)


def _causal(tq):
    r = lax.broadcasted_iota(jnp.int32, (tq, tq), 0)
    c = lax.broadcasted_iota(jnp.int32, (tq, tq), 1)
    return r >= c


def _head_halves(t):
    lo = lax.broadcasted_iota(jnp.int32, t.shape, 1) < HEAD_DIM
    zero = jnp.zeros_like(t)
    return jnp.where(lo, t, zero), jnp.where(lo, zero, t)


NEG = -1e30


def _attn_fwd(proj, cum, cum_t, B, S, tq):
    nq = S // tq
    scale = 1.0 / math.sqrt(HEAD_DIM)

    def body(q_ref, k_ref, v_ref, cum_ref, cumt_ref, o_ref, lse_ref):
        qi, hp = pl.program_id(1), pl.program_id(2)
        qm = _head_halves(q_ref[...])
        cumv = cum_ref[...]
        cq = [_pick_lane(cumv, 2 * hp + e) for e in range(2)]

        def tile(j, carry, masked):
            off = pl.multiple_of(j * tq, tq)
            kj = k_ref[pl.ds(off, tq), :]
            vj = v_ref[pl.ds(off, tq), :]
            ct = cumt_ref[j]
            new = []
            for e in range(2):
                m, l, acc = carry[e]
                s = _dot_nt(qm[e], kj) * scale + (cq[e] - _pick_row(ct, 2 * hp + e))
                if masked:
                    s = jnp.where(_causal(tq), s, NEG)
                m_new = jnp.maximum(m, jnp.max(s, axis=1, keepdims=True))
                p = jnp.exp(s - m_new)
                alpha = jnp.exp(m - m_new)
                l = alpha * l + jnp.sum(p, axis=1, keepdims=True)
                acc = alpha * acc + _dot(p.astype(BF16), vj)
                new.append((m_new, l, acc))
            return tuple(new)

        one = (jnp.full((tq, 1), NEG, F32), jnp.zeros((tq, 1), F32), jnp.zeros((tq, LANES), F32))
        carry = lax.fori_loop(0, qi, lambda j, c: tile(j, c, False), (one, one))
        (ma, la, acca), (mb, lb, accb) = tile(qi, carry, True)
        lo = lax.broadcasted_iota(jnp.int32, (tq, LANES), 1) < HEAD_DIM
        o_ref[...] = jnp.where(lo, acca / la, accb / lb).astype(BF16)

        @pl.when(hp == 0)
        def _():
            lse_ref[...] = jnp.zeros_like(lse_ref)

        lse_ref[...] += _put_lane(ma + jnp.log(la), 2 * hp) + _put_lane(mb + jnp.log(lb), 2 * hp + 1)

    kv = lambda first: pl.BlockSpec((S, LANES), lambda b, i, hp: (b, first + hp))
    return pl.pallas_call(
        body, name="attn_fwd", grid=(B, nq, HEAD_PAIRS),
        in_specs=[pl.BlockSpec((tq, LANES), lambda b, i, hp: (b * nq + i, hp)),
                  kv(ATTN_W // LANES), kv(2 * ATTN_W // LANES),
                  pl.BlockSpec((tq, LANES), lambda b, i, hp: (b * nq + i, 0)),
                  pl.BlockSpec((None, nq, 8, tq), lambda b, i, hp: (b, 0, 0, 0))],
        out_specs=[pl.BlockSpec((tq, LANES), lambda b, i, hp: (b * nq + i, hp)),
                   pl.BlockSpec((tq, LANES), lambda b, i, hp: (b * nq + i, 0))],
        out_shape=[jax.ShapeDtypeStruct((B * S, ATTN_W), BF16),
                   jax.ShapeDtypeStruct((B * S, LANES), F32)],
        compiler_params=_params(("arbitrary", "arbitrary", "arbitrary")),
    )(proj, proj, proj, cum, cum_t)


def _attn_bwd(proj, o, do, lse, cum, cum_t, B, S, tq):
    nq = S // tq
    scale = 1.0 / math.sqrt(HEAD_DIM)

    def body(q_ref, k_ref, v_ref, o_ref, do_ref, lse_ref, cum_ref, cumt_ref,
             dq_ref, dk_ref, dv_ref, dcq_ref, dck_ref, dq_scr):
        hp, kj = pl.program_id(1), pl.program_id(2)

        @pl.when(kj == 0)
        def _():
            dq_scr[...] = jnp.zeros_like(dq_scr)

        @pl.when((kj == 0) & (hp == 0))
        def _():
            dcq_ref[...] = jnp.zeros_like(dcq_ref)
            dck_ref[...] = jnp.zeros_like(dck_ref)

        kv = k_ref[...]
        vv = v_ref[...]
        km = _head_halves(kv)
        ct = cumt_ref[...]
        ck = [_pick_row(ct, 2 * hp + e) for e in range(2)]

        def tile(i, carry, masked):
            dk, dv, dcol = carry
            off = pl.multiple_of(i * tq, tq)
            qi = q_ref[pl.ds(off, tq), :]
            ov = o_ref[pl.ds(off, tq), :].astype(F32)
            qm = _head_halves(qi)
            dom = _head_halves(do_ref[pl.ds(off, tq), :])
            cumv = cum_ref[pl.ds(off, tq), :]
            lsev = lse_ref[pl.ds(off, tq), :]
            dcq = jnp.zeros((tq, LANES), F32)
            dq = jnp.zeros((tq, LANES), F32)
            dcol_new = []
            for e in range(2):
                delta = jnp.sum(dom[e].astype(F32) * ov, axis=1, keepdims=True)
                s = _dot_nt(qm[e], kv) * scale + (_pick_lane(cumv, 2 * hp + e) - ck[e])
                p = jnp.exp(s - _pick_lane(lsev, 2 * hp + e))
                if masked:
                    p = jnp.where(_causal(tq), p, 0.0)
                dv = dv + _dot_tn(p.astype(BF16), dom[e])
                ds = p * (_dot_nt(dom[e], vv) - delta)
                dcol_new.append(dcol[e] + jnp.sum(ds, axis=0, keepdims=True))
                dcq = dcq + _put_lane(jnp.sum(ds, axis=1, keepdims=True), 2 * hp + e)
                dsb = ds.astype(BF16)
                dk = dk + _dot_tn(dsb, qm[e]) * scale
                dq = dq + _dot(dsb, km[e]) * scale
            dq_scr[pl.ds(off, tq), :] += dq
            dcq_ref[pl.ds(off, tq), :] += dcq
            return dk, dv, tuple(dcol_new)

        zero_row = jnp.zeros((1, tq), F32)
        init = (jnp.zeros((tq, LANES), F32), jnp.zeros((tq, LANES), F32), (zero_row, zero_row))
        carry = tile(kj, init, True)
        dk, dv, dcol = lax.fori_loop(kj + 1, nq, lambda i, c: tile(i, c, False), carry)
        dk_ref[...] = dk.astype(BF16)
        dv_ref[...] = dv.astype(BF16)
        dck_ref[kj] += -(_put_row(dcol[0], 2 * hp) + _put_row(dcol[1], 2 * hp + 1))

        @pl.when(kj == nq - 1)
        def _():
            dq_ref[...] = dq_scr[...].astype(BF16)

    seq = lambda first: pl.BlockSpec((S, LANES), lambda b, hp, j: (b, first + hp))
    tile_in = lambda first: pl.BlockSpec((tq, LANES), lambda b, hp, j: (b * nq + j, first + hp))
    lanes0 = pl.BlockSpec((S, LANES), lambda b, hp, j: (b, 0))
    out = jax.ShapeDtypeStruct((B * S, ATTN_W), BF16)
    return pl.pallas_call(
        body, name="attn_bwd", grid=(B, HEAD_PAIRS, nq),
        in_specs=[seq(0), tile_in(ATTN_W // LANES), tile_in(2 * ATTN_W // LANES), seq(0), seq(0), lanes0, lanes0,
                  pl.BlockSpec((None, None, 8, tq), lambda b, hp, j: (b, j, 0, 0))],
        out_specs=[seq(0), tile_in(0), tile_in(0), lanes0,
                   pl.BlockSpec((None, nq, 8, tq), lambda b, hp, j: (b, 0, 0, 0))],
        out_shape=[out, out, out,
                   jax.ShapeDtypeStruct((B * S, LANES), F32),
                   jax.ShapeDtypeStruct((B, nq, 8, tq), F32)],
        scratch_shapes=[pltpu.VMEM((S, LANES), F32)],
        compiler_params=_params(("arbitrary", "arbitrary", "arbitrary")),
    )(proj, proj, proj, o, do, lse, cum, cum_t)


def _shift_down(u, n):
    row = lax.broadcasted_iota(jnp.int32, u.shape, 0)
    return jnp.where(row >= n, pltpu.roll(u, n, 0), 0.0)


def _shift_up(u, n):
    rows = u.shape[0]
    row = lax.broadcasted_iota(jnp.int32, u.shape, 0)
    return jnp.where(row < rows - n, pltpu.roll(u, rows - n, 0), 0.0)


def _conv_specs(S):
    cb = pl.BlockSpec((S, LANES), lambda g, b: (b, COL_CB // LANES + g))
    cc = pl.BlockSpec((S, LANES), lambda g, b: (b, COL_CC // LANES + g))
    cx = pl.BlockSpec((S, LANES), lambda g, b: (b, COL_CX // LANES + g))
    w = pl.BlockSpec((8, LANES), lambda g, b: (0, g))
    return cb, cc, cx, w


def _conv_fwd(proj, conv_w, B, S):
    def body(cb_ref, cc_ref, cx_ref, w_ref, y_ref):
        u = cc_ref[...].astype(F32) * cx_ref[...].astype(F32)
        w = w_ref[...]
        conv = w[0:1, :] * _shift_down(u, 2) + w[1:2, :] * _shift_down(u, 1) + w[2:3, :] * u
        y_ref[...] = (cb_ref[...].astype(F32) * conv).astype(BF16)

    cb, cc, cx, w = _conv_specs(S)
    return pl.pallas_call(
        body, name="conv_fwd", grid=(CONV_W // LANES, B),
        in_specs=[cb, cc, cx, w],
        out_specs=pl.BlockSpec((S, LANES), lambda g, b: (b, g)),
        out_shape=jax.ShapeDtypeStruct((B * S, CONV_W), BF16),
        compiler_params=_params(("arbitrary", "arbitrary")),
    )(proj, proj, proj, conv_w)


def _conv_bwd(dy, proj, conv_w, B, S):
    def body(dy_ref, cb_ref, cc_ref, cx_ref, w_ref, dcb_ref, dcc_ref, dcx_ref, dw_ref):
        @pl.when(pl.program_id(1) == 0)
        def _():
            dw_ref[...] = jnp.zeros_like(dw_ref)

        ccv = cc_ref[...].astype(F32)
        cxv = cx_ref[...].astype(F32)
        u = ccv * cxv
        u1 = _shift_down(u, 1)
        u2 = _shift_down(u, 2)
        w = w_ref[...]
        conv = w[0:1, :] * u2 + w[1:2, :] * u1 + w[2:3, :] * u
        dyv = dy_ref[...].astype(F32)
        dcb_ref[...] = (dyv * conv).astype(BF16)
        dconv = dyv * cb_ref[...].astype(F32)
        du = w[2:3, :] * dconv + w[1:2, :] * _shift_up(dconv, 1) + w[0:1, :] * _shift_up(dconv, 2)
        dcc_ref[...] = (du * cxv).astype(BF16)
        dcx_ref[...] = (du * ccv).astype(BF16)
        row = lax.broadcasted_iota(jnp.int32, (8, LANES), 0)
        dw = jnp.where(row == 0, jnp.sum(dconv * u2, axis=0, keepdims=True),
                       jnp.where(row == 1, jnp.sum(dconv * u1, axis=0, keepdims=True),
                                 jnp.where(row == 2, jnp.sum(dconv * u, axis=0, keepdims=True), 0.0)))
        dw_ref[...] += dw

    cb, cc, cx, w = _conv_specs(S)
    out = pl.BlockSpec((S, LANES), lambda g, b: (b, g))
    return pl.pallas_call(
        body, name="conv_bwd", grid=(CONV_W // LANES, B),
        in_specs=[out, cb, cc, cx, w],
        out_specs=[out, out, out, w],
        out_shape=[jax.ShapeDtypeStruct((B * S, CONV_W), BF16)] * 3 + [jax.ShapeDtypeStruct((8, CONV_W), F32)],
        compiler_params=_params(("arbitrary", "arbitrary")),
    )(dy, proj, proj, proj, conv_w)


def _gate_specs(tm, D):
    ga = pl.BlockSpec((tm, D), lambda i: (i, COL_GATES // D))
    gc = pl.BlockSpec((tm, D), lambda i: (i, COL_GATES // D + 1))
    return ga, gc


def _mix_out_fwd(x, o, yc, proj, woa, woc, wout, tm):
    T, D = x.shape

    def body(x_ref, o_ref, yc_ref, ga_ref, gc_ref, woa_ref, woc_ref, wout_ref, out_ref):
        ya = _dot(o_ref[...], woa_ref[...])
        yp = _dot(yc_ref[...], woc_ref[...])
        merged = _sigmoid(ga_ref[...].astype(F32)) * ya + _sigmoid(gc_ref[...].astype(F32)) * yp
        out_ref[...] = x_ref[...] + _dot(merged.astype(BF16), wout_ref[...])

    ga, gc = _gate_specs(tm, D)
    row = lambda w: pl.BlockSpec((tm, w), lambda i: (i, 0))
    whole = lambda a: pl.BlockSpec(a.shape, lambda i: (0, 0))
    return pl.pallas_call(
        body, name="mix_out_fwd", grid=(T // tm,),
        in_specs=[row(D), row(ATTN_W), row(CONV_W), ga, gc, whole(woa), whole(woc), whole(wout)],
        out_specs=row(D),
        out_shape=jax.ShapeDtypeStruct((T, D), F32),
        compiler_params=_params(("arbitrary",)),
    )(x, o, yc, proj, proj, woa, woc, wout)


def _mix_out_bwd(dx, o, yc, proj, woa, woc, wout, tm):
    T, D = dx.shape
    nt = T // tm

    def body(dx_ref, o_ref, yc_ref, ga_ref, gc_ref, woa_ref, woc_ref, wout_ref,
             do_ref, dyc_ref, dg_ref, dwoa_ref, dwoc_ref, dwout_ref, acca, accc, acco):
        t = pl.program_id(0)

        @pl.when(t == 0)
        def _():
            acca[...] = jnp.zeros_like(acca)
            accc[...] = jnp.zeros_like(accc)
            acco[...] = jnp.zeros_like(acco)

        dxb = dx_ref[...].astype(BF16)
        ov, ycv = o_ref[...], yc_ref[...]
        ya = _dot(ov, woa_ref[...])
        yp = _dot(ycv, woc_ref[...])
        sa = _sigmoid(ga_ref[...].astype(F32))
        sc = _sigmoid(gc_ref[...].astype(F32))
        merged = (sa * ya + sc * yp).astype(BF16)
        dm = _dot_nt(dxb, wout_ref[...])
        dya = (dm * sa).astype(BF16)
        dyp = (dm * sc).astype(BF16)
        dg_ref[:, :D] = (dm * ya * sa * (1.0 - sa)).astype(BF16)
        dg_ref[:, D:] = (dm * yp * sc * (1.0 - sc)).astype(BF16)
        do_ref[...] = _dot_nt(dya, woa_ref[...]).astype(BF16)
        dyc_ref[...] = _dot_nt(dyp, woc_ref[...]).astype(BF16)
        acca[...] += _dot_tn(ov, dya)
        accc[...] += _dot_tn(ycv, dyp)
        acco[...] += _dot_tn(merged, dxb)

        @pl.when(t == nt - 1)
        def _():
            dwoa_ref[...] = acca[...].astype(BF16)
            dwoc_ref[...] = accc[...].astype(BF16)
            dwout_ref[...] = acco[...].astype(BF16)

    ga, gc = _gate_specs(tm, D)
    row = lambda w: pl.BlockSpec((tm, w), lambda i: (i, 0))
    whole = lambda a: pl.BlockSpec(a.shape, lambda i: (0, 0))
    return pl.pallas_call(
        body, name="mix_out_bwd", grid=(nt,),
        in_specs=[row(D), row(ATTN_W), row(CONV_W), ga, gc, whole(woa), whole(woc), whole(wout)],
        out_specs=[row(ATTN_W), row(CONV_W), row(2 * D), whole(woa), whole(woc), whole(wout)],
        out_shape=[jax.ShapeDtypeStruct((T, ATTN_W), BF16), jax.ShapeDtypeStruct((T, CONV_W), BF16),
                   jax.ShapeDtypeStruct((T, 2 * D), BF16),
                   jax.ShapeDtypeStruct(woa.shape, BF16), jax.ShapeDtypeStruct(woc.shape, BF16),
                   jax.ShapeDtypeStruct(wout.shape, BF16)],
        scratch_shapes=[pltpu.VMEM(woa.shape, F32), pltpu.VMEM(woc.shape, F32), pltpu.VMEM(wout.shape, F32)],
        compiler_params=_params(("arbitrary",)),
    )(dx, o, yc, proj, proj, woa, woc, wout)


def _proj_pieces(dq, dk, dv, dcb, dcc, dcx, dgates, dflog):
    D = dgates.shape[1] // 2
    return [(dq, ATTN_W, 0), (dk, ATTN_W, 0), (dv, ATTN_W, 0), (dcb, CONV_W, 0), (dcc, CONV_W, 0), (dcx, CONV_W, 0),
            (dgates, D, 0), (dgates, D, 1), (dflog, LANES, 0)]


def _mix_proj_bwd_dx(dres, x, g, pieces, wproj_t, wf_t, tm):
    T, D = x.shape
    n = len(pieces)
    w_blocks = [(ATTN_W, 0), (ATTN_W, 1), (ATTN_W, 2), (CONV_W, 3), (CONV_W, 4), (CONV_W, 5),
                (D, COL_GATES // D), (D, COL_GATES // D + 1)]

    def body(*refs):
        dres_ref, x_ref, g_ref = refs[:3]
        p_refs, w_refs = refs[3:3 + n], refs[3 + n:3 + 2 * n]
        dx_ref, dg_ref = refs[3 + 2 * n:]

        @pl.when(pl.program_id(0) == 0)
        def _():
            dg_ref[...] = jnp.zeros_like(dg_ref)

        dh = _dot(p_refs[0][...].astype(BF16), w_refs[0][...])
        for p_ref, w_ref in zip(p_refs[1:], w_refs[1:]):
            dh = dh + _dot(p_ref[...].astype(BF16), w_ref[...])
        xhat, inv = _rms(x_ref[...])
        dx, dg = _rms_bwd(dh, xhat, inv, g_ref[...])
        dx_ref[...] = dres_ref[...] + dx
        dg_ref[...] += dg

    row = pl.BlockSpec((tm, D), lambda i: (i, 0))
    vec = pl.BlockSpec((1, D), lambda i: (0, 0))
    p_specs = [pl.BlockSpec((tm, w), lambda i, cb=cb: (i, cb)) for _, w, cb in pieces]
    w_specs = [pl.BlockSpec((r, D), lambda i, rb=rb: (rb, 0)) for r, rb in w_blocks]
    w_specs.append(pl.BlockSpec((LANES, D), lambda i: (0, 0)))
    return pl.pallas_call(
        body, name="mix_proj_bwd_dx", grid=(T // tm,),
        in_specs=[row, row, vec] + p_specs + w_specs,
        out_specs=[row, vec],
        out_shape=[jax.ShapeDtypeStruct((T, D), F32), jax.ShapeDtypeStruct((1, D), F32)],
        compiler_params=_params(("arbitrary",)),
    )(dres, x, g, *[p for p, _, _ in pieces], *([wproj_t] * len(w_blocks)), wf_t)


def _matmul_tn(name, a, width, col_block, b, tk):
    T = a.shape[0]
    N = b.shape[1]
    nt = T // tk

    def body(a_ref, b_ref, out_ref, acc):
        t = pl.program_id(0)

        @pl.when(t == 0)
        def _():
            acc[...] = jnp.zeros_like(acc)

        acc[...] += _dot_tn(a_ref[...].astype(BF16), b_ref[...])

        @pl.when(t == nt - 1)
        def _():
            out_ref[...] = acc[...].astype(BF16)

    return pl.pallas_call(
        body, name=name, grid=(nt,),
        in_specs=[pl.BlockSpec((tk, width), lambda t: (t, col_block)),
                  pl.BlockSpec((tk, N), lambda t: (t, 0))],
        out_specs=pl.BlockSpec((width, N), lambda t: (0, 0)),
        out_shape=jax.ShapeDtypeStruct((width, N), BF16),
        scratch_shapes=[pltpu.VMEM((width, N), F32)],
        compiler_params=_params(("arbitrary",)),
    )(a, b)


def _final_loss(x, target, g, tm):
    T, D = x.shape

    def body(x_ref, t_ref, g_ref, dx_ref, loss_ref, dg_ref):
        @pl.when(pl.program_id(0) == 0)
        def _():
            loss_ref[...] = jnp.zeros_like(loss_ref)
            dg_ref[...] = jnp.zeros_like(dg_ref)

        xhat, inv = _rms(x_ref[...])
        err = xhat * g_ref[...] - t_ref[...]
        loss_ref[...] += 0.5 * jnp.sum(jnp.sum(err * err, axis=1, keepdims=True), axis=0, keepdims=True) / D
        dx, dg = _rms_bwd(err * (1.0 / D), xhat, inv, g_ref[...])
        dx_ref[...] = dx
        dg_ref[...] += dg

    row = pl.BlockSpec((tm, D), lambda i: (i, 0))
    return pl.pallas_call(
        body, name="final_loss", grid=(T // tm,),
        in_specs=[row, row, pl.BlockSpec((1, D), lambda i: (0, 0))],
        out_specs=[row, pl.BlockSpec((1, LANES), lambda i: (0, 0)), pl.BlockSpec((1, D), lambda i: (0, 0))],
        out_shape=[jax.ShapeDtypeStruct((T, D), F32), jax.ShapeDtypeStruct((1, LANES), F32),
                   jax.ShapeDtypeStruct((1, D), F32)],
        compiler_params=_params(("arbitrary",)),
    )(x, target, g)


def _local_step(x, target, w, B, S):
    T, D = x.shape
    tm = min(512, T)
    tq = min(512, S)
    nq = S // tq
    ch = min(256, S)

    x1, hg1, hu1 = _ffn_fwd("ffn1_fwd", x, w["ffn1_norm"], w["ffn1_gate"], w["ffn1_up"], w["ffn1_down"], tm)
    h, proj, flog = _mix_proj_fwd(x1, w["mix_norm"], w["w_proj"], w["w_f"], tm, 1280)
    cum = _fgate_fwd(flog, w["b_forget"], B, S, ch)
    cum_t = jnp.transpose(cum[:, :N_HEADS].reshape(B, nq, tq, N_HEADS), (0, 1, 3, 2))
    o, lse = _attn_fwd(proj, cum, cum_t, B, S, tq)
    yc = _conv_fwd(proj, w["conv_w"], B, S)
    x2 = _mix_out_fwd(x1, o, yc, proj, w["w_o_attn"], w["w_o_conv"], w["w_out"], tm)
    x3, hg2, hu2 = _ffn_fwd("ffn2_fwd", x2, w["ffn2_norm"], w["ffn2_gate"], w["ffn2_up"], w["ffn2_down"], tm)
    dx3, loss, d_final_norm = _final_loss(x3, target, w["final_norm"], tm)

    g = {"final_norm": d_final_norm}
    dx2, dhg2, dhu2, g["ffn2_norm"] = _ffn_bwd_dx("ffn2_bwd_dx", dx3, x2, w["ffn2_norm"], hg2, hu2,
                                                  w["ffn2_gate"], w["ffn2_up"], w["ffn2_down"], tm)
    g["ffn2_gate"], g["ffn2_up"], g["ffn2_down"] = _ffn_bwd_dw("ffn2_bwd_dw", dx3, x2, w["ffn2_norm"],
                                                               hg2, hu2, dhg2, dhu2, tm)
    do, dyc, dgates, g["w_o_attn"], g["w_o_conv"], g["w_out"] = _mix_out_bwd(
        dx2, o, yc, proj, w["w_o_attn"], w["w_o_conv"], w["w_out"], tm)
    dq, dk, dv, dcq, dck = _attn_bwd(proj, o, do, lse, cum, cum_t, B, S, tq)
    dcum = dcq + jnp.pad(jnp.transpose(dck, (0, 1, 3, 2)).reshape(T, N_HEADS), ((0, 0), (0, LANES - N_HEADS)))
    dflog, g["b_forget"] = _fgate_bwd(dcum, flog, w["b_forget"], B, S, ch)
    dcb, dcc, dcx, g["conv_w"] = _conv_bwd(dyc, proj, w["conv_w"], B, S)
    pieces = _proj_pieces(dq, dk, dv, dcb, dcc, dcx, dgates, dflog)
    dx1, g["mix_norm"] = _mix_proj_bwd_dx(dx2, x1, w["mix_norm"], pieces, w["w_proj"], w["w_f"], min(256, T))
    g["w_proj"] = [_matmul_tn("mix_dw_%d" % j, p, width, cb, h, tm) for j, (p, width, cb) in enumerate(pieces)]
    grad_x, dhg1, dhu1, g["ffn1_norm"] = _ffn_bwd_dx("ffn1_bwd_dx", dx1, x, w["ffn1_norm"], hg1, hu1,
                                                     w["ffn1_gate"], w["ffn1_up"], w["ffn1_down"], tm)
    g["ffn1_gate"], g["ffn1_up"], g["ffn1_down"] = _ffn_bwd_dw("ffn1_bwd_dw", dx1, x, w["ffn1_norm"],
                                                               hg1, hu1, dhg1, dhu1, tm)
    return loss, grad_x, g


TRANSPOSED = ("ffn1_gate", "ffn1_up", "ffn2_gate", "ffn2_up", "w_in")
NORMS = ("ffn1_norm", "mix_norm", "ffn2_norm", "final_norm")
FFN = ("ffn1_gate", "ffn1_up", "ffn1_down", "ffn2_gate", "ffn2_up", "ffn2_down")


def _unshard_cols(a):
    return jnp.transpose(a, (1, 0, 2)).reshape(a.shape[1], N_CHIPS * a.shape[2])


def _shard_cols(a):
    return jnp.transpose(a.reshape(a.shape[0], N_CHIPS, a.shape[1] // N_CHIPS), (1, 0, 2))


def _compute_layout(gw, small):
    w = {n: gw[n] for n in FFN}
    win_t = gw["w_in"].reshape(-1, gw["w_in"].shape[2])
    w["w_proj"] = jnp.concatenate([win_t[:N_FORGET_COL], win_t[N_FORGET_COL + N_HEADS:]], axis=0)
    w["w_f"] = jnp.pad(win_t[N_FORGET_COL:N_FORGET_COL + N_HEADS], ((0, LANES - N_HEADS), (0, 0)))
    w["w_o_attn"] = _unshard_cols(gw["w_o_attn"])
    w["w_o_conv"] = _unshard_cols(gw["w_o_conv"])
    w["w_out"] = gw["w_out"].reshape(-1, gw["w_out"].shape[2])
    w["conv_w"] = _unshard_cols(gw["conv_w"])
    for n in NORMS:
        w[n] = small[n].reshape(1, -1)
    w["b_forget"] = jnp.pad(small["b_forget"].reshape(1, -1), ((0, 0), (0, LANES - N_HEADS)))
    return w


def _grads_to_travel(g):
    out = {n: g[n] for n in FFN}
    dq, dk, dv, dcb, dcc, dcx, dga, dgc, df = g["w_proj"]
    win_t = jnp.concatenate([dq, dk, dv, df[:N_HEADS], dcb, dcc, dcx, dga, dgc], axis=0)
    out["w_in"] = win_t.reshape(N_CHIPS, -1, win_t.shape[1])
    out["w_o_attn"] = _shard_cols(g["w_o_attn"])
    out["w_o_conv"] = _shard_cols(g["w_o_conv"])
    out["w_out"] = g["w_out"].reshape(N_CHIPS, -1, g["w_out"].shape[1])
    for n in NORMS + ("b_forget", "conv_w"):
        out[n] = g[n]
    return out


ANY = pl.BlockSpec(memory_space=pl.ANY)
BIG = ("ffn1_gate", "ffn1_up", "ffn1_down", "w_in", "w_o_attn", "w_o_conv", "w_out",
       "ffn2_gate", "ffn2_up", "ffn2_down")


def _place():
    x, y, c = lax.axis_index("x"), lax.axis_index("y"), lax.axis_index("c")
    others = [(1 - x, y), (x, 1 - y), (1 - x, 1 - y)]
    return x, y, c, others


def _col_halves(cols, c):
    hc = cols // 2
    return pl.ds(pl.multiple_of(c * hc, LANES), hc), pl.ds(pl.multiple_of((1 - c) * hc, LANES), hc)


def _gather_weights(shards, conv_shard):
    n = len(shards)

    def body(*refs):
        srcs, conv_src = refs[:n], refs[n]
        dsts, conv_dst = refs[n + 1:2 * n + 1], refs[2 * n + 1]
        send_sems, recv_sems, pass_send, pass_recv, conv_send, conv_recv = refs[2 * n + 2:]
        x, y, c, others = _place()
        me = 2 * x + y

        def chip_copy(a, j, chip):
            mine, _ = _col_halves(srcs[a].shape[1], c)
            return pltpu.make_async_remote_copy(
                src_ref=srcs[a].at[:, mine], dst_ref=dsts[a].at[chip, :, mine],
                send_sem=send_sems.at[3 * a + j], recv_sem=recv_sems.at[3 * a + j],
                device_id=(*others[j], c), device_id_type=MESH)

        def pass_copy(a, j, chip, half):
            return pltpu.make_async_remote_copy(
                src_ref=dsts[a].at[chip, :, half], dst_ref=dsts[a].at[chip, :, half],
                send_sem=pass_send.at[3 * a + j], recv_sem=pass_recv.at[3 * a + j],
                device_id=(x, y, 1 - c), device_id_type=MESH)

        def conv_copy(j, chip):
            return pltpu.make_async_remote_copy(
                src_ref=conv_src, dst_ref=conv_dst.at[chip],
                send_sem=conv_send.at[j], recv_sem=conv_recv.at[j],
                device_id=(*others[j], c), device_id_type=MESH)

        sends = [conv_copy(j, me) for j in range(3)] + [chip_copy(a, j, me) for a in range(n) for j in range(3)]
        for cp in sends:
            cp.start()
        passed = []
        for a in range(n):
            mine, _ = _col_halves(srcs[a].shape[1], c)
            for j, (ox, oy) in enumerate(others):
                chip_copy(a, j, 2 * ox + oy).wait_recv()
                passed.append(pass_copy(a, j, 2 * ox + oy, mine))
                passed[-1].start()
        for a in range(n):
            _, theirs = _col_halves(srcs[a].shape[1], c)
            for j, (ox, oy) in enumerate(others):
                pass_copy(a, j, 2 * ox + oy, theirs).wait_recv()
        for j, (ox, oy) in enumerate(others):
            conv_copy(j, 2 * ox + oy).wait_recv()
        for cp in sends + passed:
            cp.wait_send()

    stack = lambda s: jax.ShapeDtypeStruct((N_CHIPS,) + s.shape, s.dtype)
    stacks = pl.pallas_call(
        body, name="gather_weights",
        in_specs=[ANY] * (n + 1), out_specs=[ANY] * (n + 1),
        out_shape=[stack(s) for s in shards] + [stack(conv_shard)],
        scratch_shapes=[pltpu.SemaphoreType.DMA((3 * n,)), pltpu.SemaphoreType.DMA((3 * n,)),
                        pltpu.SemaphoreType.DMA((3 * n,)), pltpu.SemaphoreType.DMA((3 * n,)),
                        pltpu.SemaphoreType.DMA((3,)), pltpu.SemaphoreType.DMA((3,))],
    )(*shards, conv_shard)
    chip = 2 * lax.axis_index("x") + lax.axis_index("y")
    return [lax.dynamic_update_index_in_dim(st, s, chip, 0) for st, s in zip(stacks, list(shards) + [conv_shard])]


def _sibling_exchange(grads):
    n = len(grads)

    def body(*refs):
        srcs, dsts = refs[:n], refs[n:2 * n]
        send_sems, recv_sems = refs[2 * n:]
        x, y, c, _ = _place()
        copies = []
        for a in range(n):
            _, theirs = _col_halves(srcs[a].shape[2], c)
            copies.append(pltpu.make_async_remote_copy(
                src_ref=srcs[a].at[:, :, theirs], dst_ref=dsts[a],
                send_sem=send_sems.at[a], recv_sem=recv_sems.at[a],
                device_id=(x, y, 1 - c), device_id_type=MESH))
        for cp in copies:
            cp.start()
        for cp in copies:
            cp.wait()

    half = lambda s: jax.ShapeDtypeStruct((s.shape[0], s.shape[1], s.shape[2] // 2), s.dtype)
    return pl.pallas_call(
        body, name="sibling_exchange",
        in_specs=[ANY] * n, out_specs=[ANY] * n, out_shape=[half(s) for s in grads],
        scratch_shapes=[pltpu.SemaphoreType.DMA((n,)), pltpu.SemaphoreType.DMA((n,))],
    )(*grads)


def _add_halves(name, grad, recv, core):
    K, r, cols = grad.shape
    hc = cols // 2

    def body(core_ref, g_ref, r_ref, out_ref):
        out_ref[...] = (g_ref[...].astype(F32) + r_ref[...].astype(F32)).astype(BF16)

    return pl.pallas_call(
        body, name=name,
        grid_spec=pltpu.PrefetchScalarGridSpec(
            num_scalar_prefetch=1, grid=(K,),
            in_specs=[pl.BlockSpec((None, r, hc), lambda k, core_ref: (k, 0, core_ref[0])),
                      pl.BlockSpec((None, r, hc), lambda k, core_ref: (k, 0, 0))],
            out_specs=pl.BlockSpec((None, r, hc), lambda k, core_ref: (k, 0, 0))),
        out_shape=jax.ShapeDtypeStruct((K, r, hc), BF16),
        compiler_params=_params(("arbitrary",)),
    )(core, grad, recv)


def _chip_exchange(parts):
    n = len(parts)

    def body(*refs):
        srcs, dsts = refs[:n], refs[n:2 * n]
        send_sems, recv_sems = refs[2 * n:]
        x, y, c, others = _place()

        def copy(a, j):
            ox, oy = others[j]
            return pltpu.make_async_remote_copy(
                src_ref=srcs[a].at[2 * ox + oy], dst_ref=dsts[a].at[j],
                send_sem=send_sems.at[3 * a + j], recv_sem=recv_sems.at[3 * a + j],
                device_id=(ox, oy, c), device_id_type=MESH)

        copies = [copy(a, j) for a in range(n) for j in range(3)]
        for cp in copies:
            cp.start()
        for cp in copies:
            cp.wait()

    return pl.pallas_call(
        body, name="chip_exchange",
        in_specs=[ANY] * n, out_specs=[ANY] * n,
        out_shape=[jax.ShapeDtypeStruct((3,) + s.shape[1:], s.dtype) for s in parts],
        scratch_shapes=[pltpu.SemaphoreType.DMA((3 * n,)), pltpu.SemaphoreType.DMA((3 * n,))],
    )(*parts)


def _sum_chips(name, own, recv, chip):
    _, r, hc = own.shape

    def body(chip_ref, own_ref, recv_ref, out_ref):
        acc = own_ref[...].astype(F32)
        for j in range(3):
            acc = acc + recv_ref[j].astype(F32)
        out_ref[...] = acc

    return pl.pallas_call(
        body, name=name,
        grid_spec=pltpu.PrefetchScalarGridSpec(
            num_scalar_prefetch=1, grid=(hc // LANES,),
            in_specs=[pl.BlockSpec((None, r, LANES), lambda i, chip_ref: (chip_ref[0], 0, i)),
                      pl.BlockSpec((3, r, LANES), lambda i, chip_ref: (0, 0, i))],
            out_specs=pl.BlockSpec((r, LANES), lambda i, chip_ref: (0, i))),
        out_shape=jax.ShapeDtypeStruct((r, hc), F32),
        compiler_params=_params(("arbitrary",)),
    )(chip, own, recv)


def _share_halves(halves):
    n = len(halves)

    def body(*refs):
        srcs, dsts = refs[:n], refs[n:2 * n]
        send_sems, recv_sems = refs[2 * n:]
        x, y, c, _ = _place()
        copies = [pltpu.make_async_remote_copy(
            src_ref=srcs[a], dst_ref=dsts[a], send_sem=send_sems.at[a], recv_sem=recv_sems.at[a],
            device_id=(x, y, 1 - c), device_id_type=MESH) for a in range(n)]
        for cp in copies:
            cp.start()
        for cp in copies:
            cp.wait()

    return pl.pallas_call(
        body, name="share_halves",
        in_specs=[ANY] * n, out_specs=[ANY] * n,
        out_shape=[jax.ShapeDtypeStruct(s.shape, s.dtype) for s in halves],
        scratch_shapes=[pltpu.SemaphoreType.DMA((n,)), pltpu.SemaphoreType.DMA((n,))],
    )(*halves)


def _allreduce_small(part):
    rows = part.shape[0]

    def body(in_ref, out_ref, land, send_sems, recv_sems):
        x, y, c, _ = _place()
        me = 4 * x + 2 * y + c
        land[me] = in_ref[...]
        copies = []
        for d in range(1, N_DEV):
            peer = (1 - x if d & 4 else x, 1 - y if d & 2 else y, 1 - c if d & 1 else c)
            copies.append(pltpu.make_async_remote_copy(
                src_ref=in_ref, dst_ref=land.at[me],
                send_sem=send_sems.at[d - 1], recv_sem=recv_sems.at[d - 1],
                device_id=peer, device_id_type=MESH))
        for cp in copies:
            cp.start()
        for d in range(1, N_DEV):
            px, py, pc = (1 - x if d & 4 else x, 1 - y if d & 2 else y, 1 - c if d & 1 else c)
            pltpu.make_async_remote_copy(
                src_ref=in_ref, dst_ref=land.at[4 * px + 2 * py + pc],
                send_sem=send_sems.at[d - 1], recv_sem=recv_sems.at[d - 1],
                device_id=(px, py, pc), device_id_type=MESH).wait_recv()
        for cp in copies:
            cp.wait_send()
        acc = land[0]
        for k in range(1, N_DEV):
            acc = acc + land[k]
        out_ref[...] = acc

    vmem = pl.BlockSpec(memory_space=pltpu.VMEM)
    return pl.pallas_call(
        body, name="allreduce_small",
        in_specs=[vmem], out_specs=vmem,
        out_shape=jax.ShapeDtypeStruct(part.shape, F32),
        scratch_shapes=[pltpu.VMEM((N_DEV, rows, LANES), F32),
                        pltpu.SemaphoreType.DMA((N_DEV - 1,)), pltpu.SemaphoreType.DMA((N_DEV - 1,))],
    )(part)


def _adam_update(w, g, m, v):
    nm = ADAM_B1 * m + (1.0 - ADAM_B1) * g
    nv = ADAM_B2 * v + (1.0 - ADAM_B2) * (g * g)
    m_hat = nm * (1.0 / (1.0 - ADAM_B1 ** ADAM_STEP))
    v_hat = nv * (1.0 / (1.0 - ADAM_B2 ** ADAM_STEP))
    return -ADAM_LR * (m_hat / (jnp.sqrt(v_hat) + ADAM_EPS) + ADAM_WD * w), nm, nv


def _adamw(name, w, g, m, v):
    def body(w_ref, g_ref, m_ref, v_ref, d_ref, nm_ref, nv_ref):
        d_ref[...], nm_ref[...], nv_ref[...] = _adam_update(w_ref[...], g_ref[...], m_ref[...], v_ref[...])

    spec = pl.BlockSpec(w.shape, lambda i: (0, 0))
    out = jax.ShapeDtypeStruct(w.shape, F32)
    return pl.pallas_call(
        body, name=name, grid=(1,),
        in_specs=[spec] * 4, out_specs=[spec] * 3, out_shape=[out] * 3,
        compiler_params=_params(("arbitrary",)),
    )(w, g, m, v)


def _adamw_halves(name, w, mine, theirs, m, v, core):
    rows, cols = w.shape
    hc = cols // 2
    tc = min(256, hc)
    nt = hc // tc

    def body(core_ref, w_ref, mine_ref, theirs_ref, m_ref, v_ref, g_ref, d_ref, nm_ref, nv_ref):
        gv = jnp.where(pl.program_id(0) == core_ref[0], mine_ref[...], theirs_ref[...])
        g_ref[...] = gv
        d_ref[...], nm_ref[...], nv_ref[...] = _adam_update(w_ref[...], gv, m_ref[...], v_ref[...])

    whole = pl.BlockSpec((rows, tc), lambda h, i, core_ref: (0, h * nt + i))
    half = pl.BlockSpec((rows, tc), lambda h, i, core_ref: (0, i))
    out = jax.ShapeDtypeStruct((rows, cols), F32)
    return pl.pallas_call(
        body, name=name,
        grid_spec=pltpu.PrefetchScalarGridSpec(
            num_scalar_prefetch=1, grid=(2, nt),
            in_specs=[whole, half, half, whole, whole], out_specs=[whole] * 4),
        out_shape=[out] * 4,
        compiler_params=_params(("arbitrary", "arbitrary")),
    )(core, w, mine, theirs, m, v)


WEIGHTS = ("ffn1_norm", "ffn1_gate", "ffn1_up", "ffn1_down", "mix_norm", "w_in", "b_forget", "conv_w",
           "w_o_attn", "w_o_conv", "w_out", "ffn2_norm", "ffn2_gate", "ffn2_up", "ffn2_down", "final_norm")
VEC_ROWS = 8


def _pack_small(t, conv_rows):
    conv = t["conv_w"]
    parts = [t[n].reshape(VEC_ROWS, LANES) for n in NORMS]
    parts.append(jnp.pad(conv, ((0, conv_rows - conv.shape[0]), (0, 0))))
    parts.append(jnp.pad(t["b_forget"].reshape(1, N_HEADS), ((0, 7), (0, LANES - N_HEADS))))
    return jnp.concatenate(parts, axis=0)


def _unpack_small(p, conv_rows):
    out = {n: p[VEC_ROWS * i:VEC_ROWS * (i + 1)].reshape(-1) for i, n in enumerate(NORMS)}
    base = VEC_ROWS * len(NORMS)
    out["conv_w"] = p[base:base + 3]
    out["b_forget"] = p[base + conv_rows, :N_HEADS]
    return out


def _travel(name, a):
    return a.T if name in TRANSPOSED else a


def kernel(x, ffn1_norm, ffn1_gate, ffn1_up, ffn1_down, mix_norm, w_in, b_forget, conv_w, w_o_attn, w_o_conv, w_out, ffn2_norm, ffn2_gate, ffn2_up, ffn2_down, final_norm, loss_target, m_ffn1_norm, m_ffn1_gate, m_ffn1_up, m_ffn1_down, m_mix_norm, m_w_in, m_b_forget, m_conv_w, m_w_o_attn, m_w_o_conv, m_w_out, m_ffn2_norm, m_ffn2_gate, m_ffn2_up, m_ffn2_down, m_final_norm, v_ffn1_norm, v_ffn1_gate, v_ffn1_up, v_ffn1_down, v_mix_norm, v_w_in, v_b_forget, v_conv_w, v_w_o_attn, v_w_o_conv, v_w_out, v_ffn2_norm, v_ffn2_gate, v_ffn2_up, v_ffn2_down, v_final_norm):
    given = dict(locals())
    wts = {n: _travel(n, given[n]) for n in WEIGHTS}
    mom = {n: _travel(n, given["m_" + n]) for n in WEIGHTS}
    var = {n: _travel(n, given["v_" + n]) for n in WEIGHTS}
    B, S, D = x.shape
    chip = 2 * lax.axis_index("x") + lax.axis_index("y")
    chip1 = chip.astype(jnp.int32).reshape(1)
    core = lax.axis_index("c").astype(jnp.int32).reshape(1)

    conv_shard = jnp.pad(conv_w, ((0, 8 - conv_w.shape[0]), (0, 0)))
    gathered = _gather_weights([wts[n].astype(BF16) for n in BIG], conv_shard)
    gw = dict(zip(BIG + ("conv_w",), gathered))
    w = _compute_layout(gw, wts)

    loss, grad_x, g = _local_step(x.reshape(B * S, D), loss_target.reshape(B * S, D), w, B, S)
    gs = _grads_to_travel(g)

    partial = [gs[n] for n in BIG]
    from_sibling = _sibling_exchange(partial)
    chip_part = [_add_halves("add_halves_" + n, p, r, core) for n, p, r in zip(BIG, partial, from_sibling)]
    from_chips = _chip_exchange(chip_part)
    mine = [_sum_chips("sum_chips_" + n, p, r, chip1) for n, p, r in zip(BIG, chip_part, from_chips)]
    theirs = _share_halves(mine)

    conv_all = _shard_cols(gs["conv_w"]).reshape(N_CHIPS * 8, LANES)
    small_part = _pack_small({**{n: gs[n] for n in NORMS}, "conv_w": conv_all, "b_forget": gs["b_forget"][0, :N_HEADS]},
                             N_CHIPS * 8)
    base = VEC_ROWS * len(NORMS)
    small_sum = _allreduce_small(small_part)
    grads = _unpack_small(small_sum, N_CHIPS * 8)
    grads["conv_w"] = lax.dynamic_slice_in_dim(small_sum[base:base + N_CHIPS * 8], chip * 8, 8, axis=0)[:3]

    delta, new_m, new_v = {}, {}, {}
    for n, gm, gt in zip(BIG, mine, theirs):
        outs = _adamw_halves("adamw_" + n, wts[n], gm, gt, mom[n], var[n], core)
        grads[n], delta[n], new_m[n], new_v[n] = [_travel(n, o) for o in outs]
    packs = [_pack_small(t, 8) for t in (wts, grads, mom, var)]
    for out, p in zip((delta, new_m, new_v), _adamw("adamw_small", *packs)):
        out.update(_unpack_small(p, 8))

    total = lax.psum(loss[0, 0], ("x", "y", "c"))
    return (total, grad_x.reshape(B, S, D), *[grads[n] for n in WEIGHTS], *[delta[n] for n in WEIGHTS],
            *[new_m[n] for n in WEIGHTS], *[new_v[n] for n in WEIGHTS])
```

```python
import functools
import math

import jax
import jax.numpy as jnp
from jax import lax
from jax.experimental import pallas as pl
from jax.experimental.pallas import tpu as pltpu

F32 = jnp.float32
BF16 = jnp.bfloat16
MESH = pl.DeviceIdType.MESH

N_CHIPS = 4
N_DEV = 8
N_HEADS = 8
HEAD_DIM = 64
HEAD_PAIRS = N_HEADS // 2
ATTN_W = N_HEADS * HEAD_DIM
CONV_W = 512
RMS_EPS = 1e-6
FFN_RES = 0.5
LANES = 128
VMEM_LIMIT = 56 * 1024 * 1024

ADAM_LR = 0.001
ADAM_B1 = 0.9
ADAM_B2 = 0.999
ADAM_EPS = 1e-08
ADAM_WD = 0.01
ADAM_STEP = 10

PROJ_W = 3 * ATTN_W + 3 * CONV_W + 2 * 1024
COL_CB, COL_CC, COL_CX = 3 * ATTN_W, 3 * ATTN_W + CONV_W, 3 * ATTN_W + 2 * CONV_W
COL_GATES = 3 * ATTN_W + 3 * CONV_W
N_FORGET_COL = 3 * ATTN_W


def _params(sem=None, vmem=VMEM_LIMIT):
    return pltpu.CompilerParams(dimension_semantics=sem, vmem_limit_bytes=vmem)


def _dot(a, b):
    return lax.dot_general(a, b, (((1,), (0,)), ((), ())), preferred_element_type=F32)


def _dot_nt(a, b):
    return lax.dot_general(a, b, (((1,), (1,)), ((), ())), preferred_element_type=F32)


def _dot_tn(a, b):
    return lax.dot_general(a, b, (((0,), (0,)), ((), ())), preferred_element_type=F32)


def _sigmoid(x):
    return 1.0 / (1.0 + jnp.exp(-x))


def _rms(xv):
    inv = lax.rsqrt(jnp.mean(xv * xv, axis=-1, keepdims=True) + RMS_EPS)
    return xv * inv, inv


class _Comm:
    def __init__(self, inputs, out_shape, scratch, start, finish):
        self.inputs, self.out_shape, self.scratch = list(inputs), list(out_shape), list(scratch)
        self.start, self.finish = start, finish


def _pallas(body, name, grid, in_specs, out_specs, out_shape, scratch, args, comm=None):
    sem = ("arbitrary",) * len(grid)
    if comm is None:
        outs = pl.pallas_call(body, name=name, grid=grid, in_specs=in_specs, out_specs=out_specs,
                              out_shape=out_shape, scratch_shapes=scratch, compiler_params=_params(sem))(*args)
        return list(outs), []
    n_in, n_out, n_scr = len(in_specs), len(out_specs), len(scratch)
    ci, co = len(comm.inputs), len(comm.out_shape)

    def riding(*refs):
        ins, refs = refs[:n_in], refs[n_in:]
        cins, refs = refs[:ci], refs[ci:]
        outs, refs = refs[:n_out], refs[n_out:]
        couts, refs = refs[:co], refs[co:]
        scr, sems = refs[:n_scr], refs[n_scr:]
        ids = [pl.program_id(d) for d in range(len(grid))]
        first = functools.reduce(lambda a, b: a & b, [i == 0 for i in ids])
        last = functools.reduce(lambda a, b: a & b, [i == g - 1 for i, g in zip(ids, grid)])

        @pl.when(first)
        def _():
            comm.start(cins, couts, sems)

        body(*ins, *outs, *scr)

        @pl.when(last)
        def _():
            comm.finish(cins, couts, sems)

    any_spec = pl.BlockSpec(memory_space=pl.ANY)
    outs = pl.pallas_call(
        riding, name=name, grid=grid,
        in_specs=list(in_specs) + [any_spec] * ci, out_specs=list(out_specs) + [any_spec] * co,
        out_shape=list(out_shape) + comm.out_shape, scratch_shapes=list(scratch) + comm.scratch,
        compiler_params=_params(sem))(*args, *comm.inputs)
    return list(outs[:n_out]), list(outs[n_out:])


def _rms_bwd(dn, xhat, inv, g):
    dxhat = dn * g
    dx = inv * (dxhat - xhat * jnp.mean(dxhat * xhat, axis=-1, keepdims=True))
    return dx, jnp.sum(dn * xhat, axis=0, keepdims=True)


def _ffn_fwd(name, x, g, wgt, wut, wd, tm, comm=None):
    T, D = x.shape
    K, Fs, _ = wgt.shape

    def body(x_ref, g_ref, wg_ref, wu_ref, wd_ref, out_ref, hg_ref, hu_ref, n_scr, acc_scr):
        k = pl.program_id(1)

        @pl.when(k == 0)
        def _():
            xhat, _ = _rms(x_ref[...])
            n_scr[...] = (xhat * g_ref[...]).astype(BF16)
            acc_scr[...] = jnp.zeros_like(acc_scr)

        n = n_scr[...]
        hg = _dot_nt(n, wg_ref[...])
        hu = _dot_nt(n, wu_ref[...])
        hg_ref[...] = hg.astype(BF16)
        hu_ref[...] = hu.astype(BF16)
        act = (hg * _sigmoid(hg) * hu).astype(BF16)
        acc_scr[...] += _dot(act, wd_ref[...])

        @pl.when(k == K - 1)
        def _():
            out_ref[...] = x_ref[...] + FFN_RES * acc_scr[...]

    w_spec = pl.BlockSpec((None, Fs, D), lambda i, k: (k, 0, 0))
    act_spec = pl.BlockSpec((None, tm, Fs), lambda i, k: (k, i, 0))
    return _pallas(
        body, name, (T // tm, K),
        [pl.BlockSpec((tm, D), lambda i, k: (i, 0)), pl.BlockSpec((1, D), lambda i, k: (0, 0)),
         w_spec, w_spec, w_spec],
        [pl.BlockSpec((tm, D), lambda i, k: (i, 0)), act_spec, act_spec],
        [jax.ShapeDtypeStruct((T, D), F32), jax.ShapeDtypeStruct((K, T, Fs), BF16),
         jax.ShapeDtypeStruct((K, T, Fs), BF16)],
        [pltpu.VMEM((tm, D), BF16), pltpu.VMEM((tm, D), F32)],
        (x, g, wgt, wut, wd), comm)


def _ffn_bwd_dx(name, dout, x, g, hg, hu, wgt, wut, wd, tm, comm=None):
    T, D = x.shape
    K, Fs, _ = wgt.shape

    def body(dout_ref, x_ref, g_ref, hg_ref, hu_ref, wg_ref, wu_ref, wd_ref,
             dx_ref, dhg_ref, dhu_ref, dg_ref, df_scr, dn_scr):
        i, k = pl.program_id(0), pl.program_id(1)

        @pl.when(k == 0)
        def _():
            df_scr[...] = (FFN_RES * dout_ref[...]).astype(BF16)
            dn_scr[...] = jnp.zeros_like(dn_scr)

        @pl.when((k == 0) & (i == 0))
        def _():
            dg_ref[...] = jnp.zeros_like(dg_ref)

        dact = _dot_nt(df_scr[...], wd_ref[...])
        hgv = hg_ref[...].astype(F32)
        huv = hu_ref[...].astype(F32)
        s = _sigmoid(hgv)
        dhu = (dact * (hgv * s)).astype(BF16)
        dhg = (dact * huv * (s * (1.0 + hgv * (1.0 - s)))).astype(BF16)
        dhg_ref[...] = dhg
        dhu_ref[...] = dhu
        dn_scr[...] += _dot(dhg, wg_ref[...]) + _dot(dhu, wu_ref[...])

        @pl.when(k == K - 1)
        def _():
            xhat, inv = _rms(x_ref[...])
            dx, dg = _rms_bwd(dn_scr[...], xhat, inv, g_ref[...])
            dx_ref[...] = dout_ref[...] + dx
            dg_ref[...] += dg

    w_spec = pl.BlockSpec((None, Fs, D), lambda i, k: (k, 0, 0))
    act_spec = pl.BlockSpec((None, tm, Fs), lambda i, k: (k, i, 0))
    row = pl.BlockSpec((tm, D), lambda i, k: (i, 0))
    vec = pl.BlockSpec((1, D), lambda i, k: (0, 0))
    return _pallas(
        body, name, (T // tm, K),
        [row, row, vec, act_spec, act_spec, w_spec, w_spec, w_spec],
        [row, act_spec, act_spec, vec],
        [jax.ShapeDtypeStruct((T, D), F32), jax.ShapeDtypeStruct((K, T, Fs), BF16),
         jax.ShapeDtypeStruct((K, T, Fs), BF16), jax.ShapeDtypeStruct((1, D), F32)],
        [pltpu.VMEM((tm, D), BF16), pltpu.VMEM((tm, D), F32)],
        (dout, x, g, hg, hu, wgt, wut, wd), comm)


def _ffn_bwd_dw(name, dout, x, g, hg, hu, dhg, dhu, tk):
    T, D = x.shape
    K, _, Fs = hg.shape
    nt = T // tk

    def body(dout_ref, x_ref, g_ref, hg_ref, hu_ref, dhg_ref, dhu_ref,
             dwg_ref, dwu_ref, dwd_ref, accg, accu, accd):
        t = pl.program_id(1)

        @pl.when(t == 0)
        def _():
            accg[...] = jnp.zeros_like(accg)
            accu[...] = jnp.zeros_like(accu)
            accd[...] = jnp.zeros_like(accd)

        xhat, _ = _rms(x_ref[...])
        n = (xhat * g_ref[...]).astype(BF16)
        df = (FFN_RES * dout_ref[...]).astype(BF16)
        hgv = hg_ref[...].astype(F32)
        act = (hgv * _sigmoid(hgv) * hu_ref[...].astype(F32)).astype(BF16)
        accg[...] += _dot_tn(dhg_ref[...], n)
        accu[...] += _dot_tn(dhu_ref[...], n)
        accd[...] += _dot_tn(act, df)

        @pl.when(t == nt - 1)
        def _():
            dwg_ref[...] = accg[...].astype(BF16)
            dwu_ref[...] = accu[...].astype(BF16)
            dwd_ref[...] = accd[...].astype(BF16)

    act_spec = pl.BlockSpec((None, tk, Fs), lambda k, t: (k, t, 0))
    w_spec = pl.BlockSpec((None, Fs, D), lambda k, t: (k, 0, 0))
    return pl.pallas_call(
        body, name=name, grid=(K, nt),
        in_specs=[pl.BlockSpec((tk, D), lambda k, t: (t, 0)),
                  pl.BlockSpec((tk, D), lambda k, t: (t, 0)),
                  pl.BlockSpec((1, D), lambda k, t: (0, 0)),
                  act_spec, act_spec, act_spec, act_spec],
        out_specs=[w_spec, w_spec, w_spec],
        out_shape=[jax.ShapeDtypeStruct((K, Fs, D), BF16)] * 3,
        scratch_shapes=[pltpu.VMEM((Fs, D), F32)] * 3,
        compiler_params=_params(("arbitrary", "arbitrary")),
    )(dout, x, g, hg, hu, dhg, dhu)


def _mix_proj_fwd(x, g, wproj_t, wf_t, tm, tn):
    T, D = x.shape
    N = wproj_t.shape[0]

    def body(x_ref, g_ref, w_ref, wf_ref, h_ref, proj_ref, flog_ref, h_scr):
        @pl.when(pl.program_id(1) == 0)
        def _():
            xhat, _ = _rms(x_ref[...])
            h = (xhat * g_ref[...]).astype(BF16)
            h_scr[...] = h
            h_ref[...] = h
            flog_ref[...] = _dot_nt(h, wf_ref[...])

        proj_ref[...] = _dot_nt(h_scr[...], w_ref[...]).astype(BF16)

    return pl.pallas_call(
        body, name="mix_proj_fwd", grid=(T // tm, N // tn),
        in_specs=[pl.BlockSpec((tm, D), lambda i, n: (i, 0)),
                  pl.BlockSpec((1, D), lambda i, n: (0, 0)),
                  pl.BlockSpec((tn, D), lambda i, n: (n, 0)),
                  pl.BlockSpec((LANES, D), lambda i, n: (0, 0))],
        out_specs=[pl.BlockSpec((tm, D), lambda i, n: (i, 0)),
                   pl.BlockSpec((tm, tn), lambda i, n: (i, n)),
                   pl.BlockSpec((tm, LANES), lambda i, n: (i, 0))],
        out_shape=[jax.ShapeDtypeStruct((T, D), BF16),
                   jax.ShapeDtypeStruct((T, N), BF16),
                   jax.ShapeDtypeStruct((T, LANES), F32)],
        scratch_shapes=[pltpu.VMEM((tm, D), BF16)],
        compiler_params=_params(("arbitrary", "arbitrary")),
    )(x, g, wproj_t, wf_t)


def _log_sigmoid(z):
    return -(jnp.maximum(-z, 0.0) + jnp.log(1.0 + jnp.exp(-jnp.abs(z))))


def _tri(n, lower):
    r = lax.broadcasted_iota(jnp.int32, (n, n), 0)
    c = lax.broadcasted_iota(jnp.int32, (n, n), 1)
    return jnp.where((r >= c) if lower else (r <= c), 1.0, 0.0).astype(F32)


def _dot_f32(a, b):
    return lax.dot_general(a, b, (((1,), (0,)), ((), ())), preferred_element_type=F32,
                           precision=lax.Precision.HIGHEST)


def _fgate_fwd(flog, bias, B, S, ch):
    def body(flog_ref, b_ref, cum_ref):
        tri = _tri(ch, True)
        carry = jnp.zeros((1, LANES), F32)
        for c0 in range(0, S, ch):
            lf = _log_sigmoid(flog_ref[c0:c0 + ch, :] + b_ref[...])
            cs = _dot_f32(tri, lf) + carry
            cum_ref[c0:c0 + ch, :] = cs
            carry = cs[ch - 1:ch, :]

    return pl.pallas_call(
        body, name="fgate_fwd", grid=(B,),
        in_specs=[pl.BlockSpec((S, LANES), lambda b: (b, 0)),
                  pl.BlockSpec((1, LANES), lambda b: (0, 0))],
        out_specs=pl.BlockSpec((S, LANES), lambda b: (b, 0)),
        out_shape=jax.ShapeDtypeStruct((B * S, LANES), F32),
        compiler_params=_params(("arbitrary",)),
    )(flog, bias)


def _fgate_bwd(dcum, flog, bias, B, S, ch):
    def body(dcum_ref, flog_ref, b_ref, dflog_ref, db_ref):
        @pl.when(pl.program_id(0) == 0)
        def _():
            db_ref[...] = jnp.zeros_like(db_ref)

        tri = _tri(ch, False)
        carry = jnp.zeros((1, LANES), F32)
        db = jnp.zeros((1, LANES), F32)
        for c0 in range(S - ch, -1, -ch):
            dlf = _dot_f32(tri, dcum_ref[c0:c0 + ch, :]) + carry
            carry = dlf[0:1, :]
            z = flog_ref[c0:c0 + ch, :] + b_ref[...]
            dz = dlf * _sigmoid(-z)
            dflog_ref[c0:c0 + ch, :] = dz
            db = db + jnp.sum(dz, axis=0, keepdims=True)
        db_ref[...] += db

    return pl.pallas_call(
        body, name="fgate_bwd", grid=(B,),
        in_specs=[pl.BlockSpec((S, LANES), lambda b: (b, 0)),
                  pl.BlockSpec((S, LANES), lambda b: (b, 0)),
                  pl.BlockSpec((1, LANES), lambda b: (0, 0))],
        out_specs=[pl.BlockSpec((S, LANES), lambda b: (b, 0)),
                   pl.BlockSpec((1, LANES), lambda b: (0, 0))],
        out_shape=[jax.ShapeDtypeStruct((B * S, LANES), F32),
                   jax.ShapeDtypeStruct((1, LANES), F32)],
        compiler_params=_params(("arbitrary",)),
    )(dcum, flog, bias)


def _pick_lane(tile, h):
    lane = lax.broadcasted_iota(jnp.int32, tile.shape, 1)
    return jnp.sum(jnp.where(lane == h, tile, 0.0), axis=1, keepdims=True)


def _put_lane(col, h, width=LANES):
    lane = lax.broadcasted_iota(jnp.int32, (col.shape[0], width), 1)
    return jnp.where(lane == h, col, 0.0)


def _pick_row(tile, h):
    row = lax.broadcasted_iota(jnp.int32, tile.shape, 0)
    return jnp.sum(jnp.where(row == h, tile, 0.0), axis=0, keepdims=True)


def _put_row(vec, h):
    row = lax.broadcasted_iota(jnp.int32, (8, vec.shape[1]), 0)
    return jnp.where(row == h, vec, 0.0)


def _causal(tq):
    r = lax.broadcasted_iota(jnp.int32, (tq, tq), 0)
    c = lax.broadcasted_iota(jnp.int32, (tq, tq), 1)
    return r >= c


def _head_halves(t):
    lo = lax.broadcasted_iota(jnp.int32, t.shape, 1) < HEAD_DIM
    zero = jnp.zeros_like(t)
    return jnp.where(lo, t, zero), jnp.where(lo, zero, t)


NEG = -1e30


def _attn_fwd(proj, cum, cum_t, B, S, tq, comm=None):
    nq = S // tq
    scale = 1.0 / math.sqrt(HEAD_DIM)

    def body(q_ref, k_ref, v_ref, cum_ref, cumt_ref, o_ref, lse_ref):
        qi, hp = pl.program_id(1), pl.program_id(2)
        qm = _head_halves(q_ref[...])
        cumv = cum_ref[...]
        cq = [_pick_lane(cumv, 2 * hp + e) for e in range(2)]

        def tile(j, carry, masked):
            off = pl.multiple_of(j * tq, tq)
            kj = k_ref[pl.ds(off, tq), :]
            vj = v_ref[pl.ds(off, tq), :]
            ct = cumt_ref[j]
            new = []
            for e in range(2):
                m, l, acc = carry[e]
                s = _dot_nt(qm[e], kj) * scale + (cq[e] - _pick_row(ct, 2 * hp + e))
                if masked:
                    s = jnp.where(_causal(tq), s, NEG)
                m_new = jnp.maximum(m, jnp.max(s, axis=1, keepdims=True))
                p = jnp.exp(s - m_new)
                alpha = jnp.exp(m - m_new)
                l = alpha * l + jnp.sum(p, axis=1, keepdims=True)
                acc = alpha * acc + _dot(p.astype(BF16), vj)
                new.append((m_new, l, acc))
            return tuple(new)

        one = (jnp.full((tq, 1), NEG, F32), jnp.zeros((tq, 1), F32), jnp.zeros((tq, LANES), F32))
        carry = lax.fori_loop(0, qi, lambda j, c: tile(j, c, False), (one, one))
        (ma, la, acca), (mb, lb, accb) = tile(qi, carry, True)
        lo = lax.broadcasted_iota(jnp.int32, (tq, LANES), 1) < HEAD_DIM
        o_ref[...] = jnp.where(lo, acca / la, accb / lb).astype(BF16)

        @pl.when(hp == 0)
        def _():
            lse_ref[...] = jnp.zeros_like(lse_ref)

        lse_ref[...] += _put_lane(ma + jnp.log(la), 2 * hp) + _put_lane(mb + jnp.log(lb), 2 * hp + 1)

    kv = lambda first: pl.BlockSpec((S, LANES), lambda b, i, hp: (b, first + hp))
    return _pallas(
        body, "attn_fwd", (B, nq, HEAD_PAIRS),
        [pl.BlockSpec((tq, LANES), lambda b, i, hp: (b * nq + i, hp)),
         kv(ATTN_W // LANES), kv(2 * ATTN_W // LANES),
         pl.BlockSpec((tq, LANES), lambda b, i, hp: (b * nq + i, 0)),
         pl.BlockSpec((None, nq, 8, tq), lambda b, i, hp: (b, 0, 0, 0))],
        [pl.BlockSpec((tq, LANES), lambda b, i, hp: (b * nq + i, hp)),
         pl.BlockSpec((tq, LANES), lambda b, i, hp: (b * nq + i, 0))],
        [jax.ShapeDtypeStruct((B * S, ATTN_W), BF16), jax.ShapeDtypeStruct((B * S, LANES), F32)],
        [], (proj, proj, proj, cum, cum_t), comm)


def _attn_bwd(proj, o, do, lse, cum, cum_t, B, S, tq, comm=None):
    nq = S // tq
    scale = 1.0 / math.sqrt(HEAD_DIM)

    def body(q_ref, k_ref, v_ref, o_ref, do_ref, lse_ref, cum_ref, cumt_ref,
             dq_ref, dk_ref, dv_ref, dcq_ref, dck_ref, dq_scr):
        hp, kj = pl.program_id(1), pl.program_id(2)

        @pl.when(kj == 0)
        def _():
            dq_scr[...] = jnp.zeros_like(dq_scr)

        @pl.when((kj == 0) & (hp == 0))
        def _():
            dcq_ref[...] = jnp.zeros_like(dcq_ref)
            dck_ref[...] = jnp.zeros_like(dck_ref)

        kv = k_ref[...]
        vv = v_ref[...]
        km = _head_halves(kv)
        ct = cumt_ref[...]
        ck = [_pick_row(ct, 2 * hp + e) for e in range(2)]

        def tile(i, carry, masked):
            dk, dv, dcol = carry
            off = pl.multiple_of(i * tq, tq)
            qi = q_ref[pl.ds(off, tq), :]
            ov = o_ref[pl.ds(off, tq), :].astype(F32)
            qm = _head_halves(qi)
            dom = _head_halves(do_ref[pl.ds(off, tq), :])
            cumv = cum_ref[pl.ds(off, tq), :]
            lsev = lse_ref[pl.ds(off, tq), :]
            dcq = jnp.zeros((tq, LANES), F32)
            dq = jnp.zeros((tq, LANES), F32)
            dcol_new = []
            for e in range(2):
                delta = jnp.sum(dom[e].astype(F32) * ov, axis=1, keepdims=True)
                s = _dot_nt(qm[e], kv) * scale + (_pick_lane(cumv, 2 * hp + e) - ck[e])
                p = jnp.exp(s - _pick_lane(lsev, 2 * hp + e))
                if masked:
                    p = jnp.where(_causal(tq), p, 0.0)
                dv = dv + _dot_tn(p.astype(BF16), dom[e])
                ds = p * (_dot_nt(dom[e], vv) - delta)
                dcol_new.append(dcol[e] + jnp.sum(ds, axis=0, keepdims=True))
                dcq = dcq + _put_lane(jnp.sum(ds, axis=1, keepdims=True), 2 * hp + e)
                dsb = ds.astype(BF16)
                dk = dk + _dot_tn(dsb, qm[e]) * scale
                dq = dq + _dot(dsb, km[e]) * scale
            dq_scr[pl.ds(off, tq), :] += dq
            dcq_ref[pl.ds(off, tq), :] += dcq
            return dk, dv, tuple(dcol_new)

        zero_row = jnp.zeros((1, tq), F32)
        init = (jnp.zeros((tq, LANES), F32), jnp.zeros((tq, LANES), F32), (zero_row, zero_row))
        carry = tile(kj, init, True)
        dk, dv, dcol = lax.fori_loop(kj + 1, nq, lambda i, c: tile(i, c, False), carry)
        dk_ref[...] = dk.astype(BF16)
        dv_ref[...] = dv.astype(BF16)
        dck_ref[kj] += -(_put_row(dcol[0], 2 * hp) + _put_row(dcol[1], 2 * hp + 1))

        @pl.when(kj == nq - 1)
        def _():
            dq_ref[...] = dq_scr[...].astype(BF16)

    seq = lambda first: pl.BlockSpec((S, LANES), lambda b, hp, j: (b, first + hp))
    tile_in = lambda first: pl.BlockSpec((tq, LANES), lambda b, hp, j: (b * nq + j, first + hp))
    lanes0 = pl.BlockSpec((S, LANES), lambda b, hp, j: (b, 0))
    out = jax.ShapeDtypeStruct((B * S, ATTN_W), BF16)
    return _pallas(
        body, "attn_bwd", (B, HEAD_PAIRS, nq),
        [seq(0), tile_in(ATTN_W // LANES), tile_in(2 * ATTN_W // LANES), seq(0), seq(0), lanes0, lanes0,
         pl.BlockSpec((None, None, 8, tq), lambda b, hp, j: (b, j, 0, 0))],
        [seq(0), tile_in(0), tile_in(0), lanes0,
         pl.BlockSpec((None, nq, 8, tq), lambda b, hp, j: (b, 0, 0, 0))],
        [out, out, out, jax.ShapeDtypeStruct((B * S, LANES), F32), jax.ShapeDtypeStruct((B, nq, 8, tq), F32)],
        [pltpu.VMEM((S, LANES), F32)],
        (proj, proj, proj, o, do, lse, cum, cum_t), comm)


def _shift_down(u, n):
    row = lax.broadcasted_iota(jnp.int32, u.shape, 0)
    return jnp.where(row >= n, pltpu.roll(u, n, 0), 0.0)


def _shift_up(u, n):
    rows = u.shape[0]
    row = lax.broadcasted_iota(jnp.int32, u.shape, 0)
    return jnp.where(row < rows - n, pltpu.roll(u, rows - n, 0), 0.0)


def _conv_specs(S):
    cb = pl.BlockSpec((S, LANES), lambda g, b: (b, COL_CB // LANES + g))
    cc = pl.BlockSpec((S, LANES), lambda g, b: (b, COL_CC // LANES + g))
    cx = pl.BlockSpec((S, LANES), lambda g, b: (b, COL_CX // LANES + g))
    w = pl.BlockSpec((8, LANES), lambda g, b: (0, g))
    return cb, cc, cx, w


def _conv_fwd(proj, conv_w, B, S):
    def body(cb_ref, cc_ref, cx_ref, w_ref, y_ref):
        u = cc_ref[...].astype(F32) * cx_ref[...].astype(F32)
        w = w_ref[...]
        conv = w[0:1, :] * _shift_down(u, 2) + w[1:2, :] * _shift_down(u, 1) + w[2:3, :] * u
        y_ref[...] = (cb_ref[...].astype(F32) * conv).astype(BF16)

    cb, cc, cx, w = _conv_specs(S)
    return pl.pallas_call(
        body, name="conv_fwd", grid=(CONV_W // LANES, B),
        in_specs=[cb, cc, cx, w],
        out_specs=pl.BlockSpec((S, LANES), lambda g, b: (b, g)),
        out_shape=jax.ShapeDtypeStruct((B * S, CONV_W), BF16),
        compiler_params=_params(("arbitrary", "arbitrary")),
    )(proj, proj, proj, conv_w)


def _conv_bwd(dy, proj, conv_w, B, S):
    def body(dy_ref, cb_ref, cc_ref, cx_ref, w_ref, dcb_ref, dcc_ref, dcx_ref, dw_ref):
        @pl.when(pl.program_id(1) == 0)
        def _():
            dw_ref[...] = jnp.zeros_like(dw_ref)

        ccv = cc_ref[...].astype(F32)
        cxv = cx_ref[...].astype(F32)
        u = ccv * cxv
        u1 = _shift_down(u, 1)
        u2 = _shift_down(u, 2)
        w = w_ref[...]
        conv = w[0:1, :] * u2 + w[1:2, :] * u1 + w[2:3, :] * u
        dyv = dy_ref[...].astype(F32)
        dcb_ref[...] = (dyv * conv).astype(BF16)
        dconv = dyv * cb_ref[...].astype(F32)
        du = w[2:3, :] * dconv + w[1:2, :] * _shift_up(dconv, 1) + w[0:1, :] * _shift_up(dconv, 2)
        dcc_ref[...] = (du * cxv).astype(BF16)
        dcx_ref[...] = (du * ccv).astype(BF16)
        row = lax.broadcasted_iota(jnp.int32, (8, LANES), 0)
        dw = jnp.where(row == 0, jnp.sum(dconv * u2, axis=0, keepdims=True),
                       jnp.where(row == 1, jnp.sum(dconv * u1, axis=0, keepdims=True),
                                 jnp.where(row == 2, jnp.sum(dconv * u, axis=0, keepdims=True), 0.0)))
        dw_ref[...] += dw

    cb, cc, cx, w = _conv_specs(S)
    out = pl.BlockSpec((S, LANES), lambda g, b: (b, g))
    return pl.pallas_call(
        body, name="conv_bwd", grid=(CONV_W // LANES, B),
        in_specs=[out, cb, cc, cx, w],
        out_specs=[out, out, out, w],
        out_shape=[jax.ShapeDtypeStruct((B * S, CONV_W), BF16)] * 3 + [jax.ShapeDtypeStruct((8, CONV_W), F32)],
        compiler_params=_params(("arbitrary", "arbitrary")),
    )(dy, proj, proj, proj, conv_w)


def _gate_specs(tm, D):
    ga = pl.BlockSpec((tm, D), lambda i: (i, COL_GATES // D))
    gc = pl.BlockSpec((tm, D), lambda i: (i, COL_GATES // D + 1))
    return ga, gc


def _mix_out_fwd(x, o, yc, proj, woa, woc, wout, tm):
    T, D = x.shape

    def body(x_ref, o_ref, yc_ref, ga_ref, gc_ref, woa_ref, woc_ref, wout_ref, out_ref):
        ya = _dot(o_ref[...], woa_ref[...])
        yp = _dot(yc_ref[...], woc_ref[...])
        merged = _sigmoid(ga_ref[...].astype(F32)) * ya + _sigmoid(gc_ref[...].astype(F32)) * yp
        out_ref[...] = x_ref[...] + _dot(merged.astype(BF16), wout_ref[...])

    ga, gc = _gate_specs(tm, D)
    row = lambda w: pl.BlockSpec((tm, w), lambda i: (i, 0))
    whole = lambda a: pl.BlockSpec(a.shape, lambda i: (0, 0))
    return pl.pallas_call(
        body, name="mix_out_fwd", grid=(T // tm,),
        in_specs=[row(D), row(ATTN_W), row(CONV_W), ga, gc, whole(woa), whole(woc), whole(wout)],
        out_specs=row(D),
        out_shape=jax.ShapeDtypeStruct((T, D), F32),
        compiler_params=_params(("arbitrary",)),
    )(x, o, yc, proj, proj, woa, woc, wout)


def _mix_out_bwd(dx, o, yc, proj, woa, woc, wout, tm):
    T, D = dx.shape
    nt = T // tm

    def body(dx_ref, o_ref, yc_ref, ga_ref, gc_ref, woa_ref, woc_ref, wout_ref,
             do_ref, dyc_ref, dg_ref, dwoa_ref, dwoc_ref, dwout_ref, acca, accc, acco):
        t = pl.program_id(0)

        @pl.when(t == 0)
        def _():
            acca[...] = jnp.zeros_like(acca)
            accc[...] = jnp.zeros_like(accc)
            acco[...] = jnp.zeros_like(acco)

        dxb = dx_ref[...].astype(BF16)
        ov, ycv = o_ref[...], yc_ref[...]
        ya = _dot(ov, woa_ref[...])
        yp = _dot(ycv, woc_ref[...])
        sa = _sigmoid(ga_ref[...].astype(F32))
        sc = _sigmoid(gc_ref[...].astype(F32))
        merged = (sa * ya + sc * yp).astype(BF16)
        dm = _dot_nt(dxb, wout_ref[...])
        dya = (dm * sa).astype(BF16)
        dyp = (dm * sc).astype(BF16)
        dg_ref[:, :D] = (dm * ya * sa * (1.0 - sa)).astype(BF16)
        dg_ref[:, D:] = (dm * yp * sc * (1.0 - sc)).astype(BF16)
        do_ref[...] = _dot_nt(dya, woa_ref[...]).astype(BF16)
        dyc_ref[...] = _dot_nt(dyp, woc_ref[...]).astype(BF16)
        acca[...] += _dot_tn(ov, dya)
        accc[...] += _dot_tn(ycv, dyp)
        acco[...] += _dot_tn(merged, dxb)

        @pl.when(t == nt - 1)
        def _():
            dwoa_ref[...] = acca[...].astype(BF16)
            dwoc_ref[...] = accc[...].astype(BF16)
            dwout_ref[...] = acco[...].astype(BF16)

    ga, gc = _gate_specs(tm, D)
    row = lambda w: pl.BlockSpec((tm, w), lambda i: (i, 0))
    whole = lambda a: pl.BlockSpec(a.shape, lambda i: (0, 0))
    return pl.pallas_call(
        body, name="mix_out_bwd", grid=(nt,),
        in_specs=[row(D), row(ATTN_W), row(CONV_W), ga, gc, whole(woa), whole(woc), whole(wout)],
        out_specs=[row(ATTN_W), row(CONV_W), row(2 * D), whole(woa), whole(woc), whole(wout)],
        out_shape=[jax.ShapeDtypeStruct((T, ATTN_W), BF16), jax.ShapeDtypeStruct((T, CONV_W), BF16),
                   jax.ShapeDtypeStruct((T, 2 * D), BF16),
                   jax.ShapeDtypeStruct(woa.shape, BF16), jax.ShapeDtypeStruct(woc.shape, BF16),
                   jax.ShapeDtypeStruct(wout.shape, BF16)],
        scratch_shapes=[pltpu.VMEM(woa.shape, F32), pltpu.VMEM(woc.shape, F32), pltpu.VMEM(wout.shape, F32)],
        compiler_params=_params(("arbitrary",)),
    )(dx, o, yc, proj, proj, woa, woc, wout)


def _proj_pieces(dq, dk, dv, dcb, dcc, dcx, dgates, dflog):
    D = dgates.shape[1] // 2
    return [(dq, ATTN_W, 0), (dk, ATTN_W, 0), (dv, ATTN_W, 0), (dcb, CONV_W, 0), (dcc, CONV_W, 0), (dcx, CONV_W, 0),
            (dgates, D, 0), (dgates, D, 1), (dflog, LANES, 0)]


def _mix_proj_bwd_dx(dres, x, g, pieces, wproj_t, wf_t, tm):
    T, D = x.shape
    n = len(pieces)
    w_blocks = [(ATTN_W, 0), (ATTN_W, 1), (ATTN_W, 2), (CONV_W, 3), (CONV_W, 4), (CONV_W, 5),
                (D, COL_GATES // D), (D, COL_GATES // D + 1)]

    def body(*refs):
        dres_ref, x_ref, g_ref = refs[:3]
        p_refs, w_refs = refs[3:3 + n], refs[3 + n:3 + 2 * n]
        dx_ref, dg_ref = refs[3 + 2 * n:]

        @pl.when(pl.program_id(0) == 0)
        def _():
            dg_ref[...] = jnp.zeros_like(dg_ref)

        dh = _dot(p_refs[0][...].astype(BF16), w_refs[0][...])
        for p_ref, w_ref in zip(p_refs[1:], w_refs[1:]):
            dh = dh + _dot(p_ref[...].astype(BF16), w_ref[...])
        xhat, inv = _rms(x_ref[...])
        dx, dg = _rms_bwd(dh, xhat, inv, g_ref[...])
        dx_ref[...] = dres_ref[...] + dx
        dg_ref[...] += dg

    row = pl.BlockSpec((tm, D), lambda i: (i, 0))
    vec = pl.BlockSpec((1, D), lambda i: (0, 0))
    p_specs = [pl.BlockSpec((tm, w), lambda i, cb=cb: (i, cb)) for _, w, cb in pieces]
    w_specs = [pl.BlockSpec((r, D), lambda i, rb=rb: (rb, 0)) for r, rb in w_blocks]
    w_specs.append(pl.BlockSpec((LANES, D), lambda i: (0, 0)))
    return pl.pallas_call(
        body, name="mix_proj_bwd_dx", grid=(T // tm,),
        in_specs=[row, row, vec] + p_specs + w_specs,
        out_specs=[row, vec],
        out_shape=[jax.ShapeDtypeStruct((T, D), F32), jax.ShapeDtypeStruct((1, D), F32)],
        compiler_params=_params(("arbitrary",)),
    )(dres, x, g, *[p for p, _, _ in pieces], *([wproj_t] * len(w_blocks)), wf_t)


def _matmul_tn(name, a, width, col_block, b, tk):
    T = a.shape[0]
    N = b.shape[1]
    nt = T // tk

    def body(a_ref, b_ref, out_ref, acc):
        t = pl.program_id(0)

        @pl.when(t == 0)
        def _():
            acc[...] = jnp.zeros_like(acc)

        acc[...] += _dot_tn(a_ref[...].astype(BF16), b_ref[...])

        @pl.when(t == nt - 1)
        def _():
            out_ref[...] = acc[...].astype(BF16)

    return pl.pallas_call(
        body, name=name, grid=(nt,),
        in_specs=[pl.BlockSpec((tk, width), lambda t: (t, col_block)),
                  pl.BlockSpec((tk, N), lambda t: (t, 0))],
        out_specs=pl.BlockSpec((width, N), lambda t: (0, 0)),
        out_shape=jax.ShapeDtypeStruct((width, N), BF16),
        scratch_shapes=[pltpu.VMEM((width, N), F32)],
        compiler_params=_params(("arbitrary",)),
    )(a, b)


def _final_loss(x, target, g, tm):
    T, D = x.shape

    def body(x_ref, t_ref, g_ref, dx_ref, loss_ref, dg_ref):
        @pl.when(pl.program_id(0) == 0)
        def _():
            loss_ref[...] = jnp.zeros_like(loss_ref)
            dg_ref[...] = jnp.zeros_like(dg_ref)

        xhat, inv = _rms(x_ref[...])
        err = xhat * g_ref[...] - t_ref[...]
        loss_ref[...] += 0.5 * jnp.sum(jnp.sum(err * err, axis=1, keepdims=True), axis=0, keepdims=True) / D
        dx, dg = _rms_bwd(err * (1.0 / D), xhat, inv, g_ref[...])
        dx_ref[...] = dx
        dg_ref[...] += dg

    row = pl.BlockSpec((tm, D), lambda i: (i, 0))
    return pl.pallas_call(
        body, name="final_loss", grid=(T // tm,),
        in_specs=[row, row, pl.BlockSpec((1, D), lambda i: (0, 0))],
        out_specs=[row, pl.BlockSpec((1, LANES), lambda i: (0, 0)), pl.BlockSpec((1, D), lambda i: (0, 0))],
        out_shape=[jax.ShapeDtypeStruct((T, D), F32), jax.ShapeDtypeStruct((1, LANES), F32),
                   jax.ShapeDtypeStruct((1, D), F32)],
        compiler_params=_params(("arbitrary",)),
    )(x, target, g)


class _LocalPlan:
    def __init__(self, stacks, small):
        self.stacks, self.small, self.grads = stacks, small, {}

    def weights(self, group):
        return _LAYOUTS[group](self.stacks, self.small)

    def rider(self, kernel_name):
        return None

    def arrived(self, kernel_name, results):
        pass

    def reduce(self, group, grads):
        self.grads.update(grads)


def _local_step(x, target, plan, B, S):
    T, D = x.shape
    tm = min(512, T)
    tq = min(512, S)
    nq = S // tq
    ch = min(256, S)

    def riding(kernel_name, build):
        results, brought = build(plan.rider(kernel_name))
        plan.arrived(kernel_name, brought)
        return results

    w1 = plan.weights("ffn1")
    x1, hg1, hu1 = riding("ffn1_fwd", lambda comm: _ffn_fwd(
        "ffn1_fwd", x, w1["ffn1_norm"], w1["ffn1_gate"], w1["ffn1_up"], w1["ffn1_down"], tm, comm))
    wm = plan.weights("mix")
    h, proj, flog = _mix_proj_fwd(x1, wm["mix_norm"], wm["w_proj"], wm["w_f"], tm, 1280)
    cum = _fgate_fwd(flog, wm["b_forget"], B, S, ch)
    cum_t = jnp.transpose(cum[:, :N_HEADS].reshape(B, nq, tq, N_HEADS), (0, 1, 3, 2))
    o, lse = riding("attn_fwd", lambda comm: _attn_fwd(proj, cum, cum_t, B, S, tq, comm))
    yc = _conv_fwd(proj, wm["conv_w"], B, S)
    x2 = _mix_out_fwd(x1, o, yc, proj, wm["w_o_attn"], wm["w_o_conv"], wm["w_out"], tm)
    w2 = plan.weights("ffn2")
    x3, hg2, hu2 = _ffn_fwd("ffn2_fwd", x2, w2["ffn2_norm"], w2["ffn2_gate"], w2["ffn2_up"], w2["ffn2_down"], tm)[0]
    dx3, loss, d_final_norm = _final_loss(x3, target, w2["final_norm"], tm)

    g = {"final_norm": d_final_norm}
    dx2, dhg2, dhu2, g["ffn2_norm"] = _ffn_bwd_dx("ffn2_bwd_dx", dx3, x2, w2["ffn2_norm"], hg2, hu2,
                                                  w2["ffn2_gate"], w2["ffn2_up"], w2["ffn2_down"], tm)[0]
    late = dict(zip(("ffn2_gate", "ffn2_up", "ffn2_down"),
                    _ffn_bwd_dw("ffn2_bwd_dw", dx3, x2, w2["ffn2_norm"], hg2, hu2, dhg2, dhu2, tm)))
    do, dyc, dgates, dwoa, dwoc, dwout = _mix_out_bwd(
        dx2, o, yc, proj, wm["w_o_attn"], wm["w_o_conv"], wm["w_out"], tm)
    late.update(w_o_attn=_shard_cols(dwoa), w_o_conv=_shard_cols(dwoc), w_out=dwout.reshape(N_CHIPS, -1, D))
    plan.reduce("late", late)
    dq, dk, dv, dcq, dck = riding("attn_bwd", lambda comm: _attn_bwd(proj, o, do, lse, cum, cum_t, B, S, tq, comm))
    dcum = dcq + jnp.pad(jnp.transpose(dck, (0, 1, 3, 2)).reshape(T, N_HEADS), ((0, 0), (0, LANES - N_HEADS)))
    dflog, g["b_forget"] = _fgate_bwd(dcum, flog, wm["b_forget"], B, S, ch)
    dcb, dcc, dcx, g["conv_w"] = _conv_bwd(dyc, proj, wm["conv_w"], B, S)
    pieces = _proj_pieces(dq, dk, dv, dcb, dcc, dcx, dgates, dflog)
    dx1, g["mix_norm"] = _mix_proj_bwd_dx(dx2, x1, wm["mix_norm"], pieces, wm["w_proj"], wm["w_f"], min(256, T))
    dwq, dwk, dwv, dwcb, dwcc, dwcx, dwga, dwgc, dwf = [
        _matmul_tn("mix_dw_%d" % j, p, width, cb, h, tm) for j, (p, width, cb) in enumerate(pieces)]
    dwin_t = jnp.concatenate([dwq, dwk, dwv, dwf[:N_HEADS], dwcb, dwcc, dwcx, dwga, dwgc], axis=0)
    plan.reduce("w_in", {"w_in": dwin_t.reshape(N_CHIPS, -1, D)})
    grad_x, dhg1, dhu1, g["ffn1_norm"] = riding("ffn1_bwd_dx", lambda comm: _ffn_bwd_dx(
        "ffn1_bwd_dx", dx1, x, w1["ffn1_norm"], hg1, hu1, w1["ffn1_gate"], w1["ffn1_up"], w1["ffn1_down"], tm, comm))
    plan.reduce("ffn1", dict(zip(("ffn1_gate", "ffn1_up", "ffn1_down"),
                                 _ffn_bwd_dw("ffn1_bwd_dw", dx1, x, w1["ffn1_norm"], hg1, hu1, dhg1, dhu1, tm))))
    return loss, grad_x, g


TRANSPOSED = ("ffn1_gate", "ffn1_up", "ffn2_gate", "ffn2_up", "w_in")
NORMS = ("ffn1_norm", "mix_norm", "ffn2_norm", "final_norm")


def _unshard_cols(a):
    return jnp.transpose(a, (1, 0, 2)).reshape(a.shape[1], N_CHIPS * a.shape[2])


def _shard_cols(a):
    return jnp.transpose(a.reshape(a.shape[0], N_CHIPS, a.shape[1] // N_CHIPS), (1, 0, 2))


def _layout_ffn(which):
    def layout(st, small):
        w = {n: st[n] for n in (which + "_gate", which + "_up", which + "_down")}
        w[which + "_norm"] = small[which + "_norm"].reshape(1, -1)
        if which == "ffn2":
            w["final_norm"] = small["final_norm"].reshape(1, -1)
        return w
    return layout


def _layout_mix(st, small):
    win_t = st["w_in"].reshape(-1, st["w_in"].shape[2])
    return {
        "w_proj": jnp.concatenate([win_t[:N_FORGET_COL], win_t[N_FORGET_COL + N_HEADS:]], axis=0),
        "w_f": jnp.pad(win_t[N_FORGET_COL:N_FORGET_COL + N_HEADS], ((0, LANES - N_HEADS), (0, 0))),
        "w_o_attn": _unshard_cols(st["w_o_attn"]),
        "w_o_conv": _unshard_cols(st["w_o_conv"]),
        "w_out": st["w_out"].reshape(-1, st["w_out"].shape[2]),
        "conv_w": _unshard_cols(st["conv_w"]),
        "mix_norm": small["mix_norm"].reshape(1, -1),
        "b_forget": jnp.pad(small["b_forget"].reshape(1, -1), ((0, 0), (0, LANES - N_HEADS))),
    }


_LAYOUTS = {"ffn1": _layout_ffn("ffn1"), "mix": _layout_mix, "ffn2": _layout_ffn("ffn2")}


ANY = pl.BlockSpec(memory_space=pl.ANY)
BIG = ("ffn1_gate", "ffn1_up", "ffn1_down", "w_in", "w_o_attn", "w_o_conv", "w_out",
       "ffn2_gate", "ffn2_up", "ffn2_down")


def _place():
    x, y, c = lax.axis_index("x"), lax.axis_index("y"), lax.axis_index("c")
    others = [(1 - x, y), (x, 1 - y), (1 - x, 1 - y)]
    return x, y, c, others


def _col_halves(cols, c):
    hc = cols // 2
    return pl.ds(pl.multiple_of(c * hc, LANES), hc), pl.ds(pl.multiple_of((1 - c) * hc, LANES), hc)


def _gather_comm(shards, conv_shard=None):
    n = len(shards)
    inputs = list(shards) + ([] if conv_shard is None else [conv_shard])

    def copies(ins, outs, sems):
        send_sems, recv_sems, pass_send, pass_recv = sems[:4]
        x, y, c, others = _place()

        def chip_copy(a, j, chip):
            mine, _ = _col_halves(ins[a].shape[1], c)
            return pltpu.make_async_remote_copy(
                src_ref=ins[a].at[:, mine], dst_ref=outs[a].at[chip, :, mine],
                send_sem=send_sems.at[3 * a + j], recv_sem=recv_sems.at[3 * a + j],
                device_id=(*others[j], c), device_id_type=MESH)

        def pass_copy(a, j, chip, half):
            return pltpu.make_async_remote_copy(
                src_ref=outs[a].at[chip, :, half], dst_ref=outs[a].at[chip, :, half],
                send_sem=pass_send.at[3 * a + j], recv_sem=pass_recv.at[3 * a + j],
                device_id=(x, y, 1 - c), device_id_type=MESH)

        def conv_copy(j, chip):
            return pltpu.make_async_remote_copy(
                src_ref=ins[n], dst_ref=outs[n].at[chip],
                send_sem=sems[4].at[j], recv_sem=sems[5].at[j],
                device_id=(*others[j], c), device_id_type=MESH)

        me = 2 * x + y
        sends = [chip_copy(a, j, me) for a in range(n) for j in range(3)]
        if conv_shard is not None:
            sends += [conv_copy(j, me) for j in range(3)]
        return c, others, sends, chip_copy, pass_copy, conv_copy

    def start(ins, outs, sems):
        for cp in copies(ins, outs, sems)[2]:
            cp.start()

    def finish(ins, outs, sems):
        c, others, sends, chip_copy, pass_copy, conv_copy = copies(ins, outs, sems)
        passed = []
        for a in range(n):
            mine, _ = _col_halves(ins[a].shape[1], c)
            for j, (ox, oy) in enumerate(others):
                chip_copy(a, j, 2 * ox + oy).wait_recv()
                passed.append(pass_copy(a, j, 2 * ox + oy, mine))
                passed[-1].start()
        for a in range(n):
            _, theirs = _col_halves(ins[a].shape[1], c)
            for j, (ox, oy) in enumerate(others):
                pass_copy(a, j, 2 * ox + oy, theirs).wait_recv()
        if conv_shard is not None:
            for j, (ox, oy) in enumerate(others):
                conv_copy(j, 2 * ox + oy).wait_recv()
        for cp in sends + passed:
            cp.wait_send()

    scratch = [pltpu.SemaphoreType.DMA((3 * n,))] * 4
    if conv_shard is not None:
        scratch += [pltpu.SemaphoreType.DMA((3,))] * 2
    return _Comm(inputs, [jax.ShapeDtypeStruct((N_CHIPS,) + s.shape, s.dtype) for s in inputs], scratch, start, finish)


def _fill_own(stacks, shards):
    chip = 2 * lax.axis_index("x") + lax.axis_index("y")
    return [lax.dynamic_update_index_in_dim(st, s, chip, 0) for st, s in zip(stacks, shards)]


def _run_comm(name, comm):
    ci, co = len(comm.inputs), len(comm.out_shape)

    def body(*refs):
        comm.start(refs[:ci], refs[ci:ci + co], refs[ci + co:])
        comm.finish(refs[:ci], refs[ci:ci + co], refs[ci + co:])

    return pl.pallas_call(body, name=name, in_specs=[ANY] * ci, out_specs=[ANY] * co, out_shape=comm.out_shape,
                          scratch_shapes=comm.scratch)(*comm.inputs)


def _sibling_exchange(name, grads):
    n = len(grads)

    def body(*refs):
        srcs, dsts = refs[:n], refs[n:2 * n]
        send_sems, recv_sems = refs[2 * n:]
        x, y, c, _ = _place()
        copies = []
        for a in range(n):
            _, theirs = _col_halves(srcs[a].shape[2], c)
            copies.append(pltpu.make_async_remote_copy(
                src_ref=srcs[a].at[:, :, theirs], dst_ref=dsts[a],
                send_sem=send_sems.at[a], recv_sem=recv_sems.at[a],
                device_id=(x, y, 1 - c), device_id_type=MESH))
        for cp in copies:
            cp.start()
        for cp in copies:
            cp.wait()

    half = lambda s: jax.ShapeDtypeStruct((s.shape[0], s.shape[1], s.shape[2] // 2), s.dtype)
    return pl.pallas_call(
        body, name=name,
        in_specs=[ANY] * n, out_specs=[ANY] * n, out_shape=[half(s) for s in grads],
        scratch_shapes=[pltpu.SemaphoreType.DMA((n,)), pltpu.SemaphoreType.DMA((n,))],
    )(*grads)


def _add_halves(name, grad, recv, core):
    K, r, cols = grad.shape
    hc = cols // 2

    def body(core_ref, g_ref, r_ref, out_ref):
        out_ref[...] = (g_ref[...].astype(F32) + r_ref[...].astype(F32)).astype(BF16)

    return pl.pallas_call(
        body, name=name,
        grid_spec=pltpu.PrefetchScalarGridSpec(
            num_scalar_prefetch=1, grid=(K,),
            in_specs=[pl.BlockSpec((None, r, hc), lambda k, core_ref: (k, 0, core_ref[0])),
                      pl.BlockSpec((None, r, hc), lambda k, core_ref: (k, 0, 0))],
            out_specs=pl.BlockSpec((None, r, hc), lambda k, core_ref: (k, 0, 0))),
        out_shape=jax.ShapeDtypeStruct((K, r, hc), BF16),
        compiler_params=_params(("arbitrary",)),
    )(core, grad, recv)


def _chip_exchange_comm(parts):
    n = len(parts)

    def copies(ins, outs, sems):
        x, y, c, others = _place()
        return [pltpu.make_async_remote_copy(
            src_ref=ins[a].at[2 * ox + oy], dst_ref=outs[a].at[j],
            send_sem=sems[0].at[3 * a + j], recv_sem=sems[1].at[3 * a + j],
            device_id=(ox, oy, c), device_id_type=MESH) for a in range(n) for j, (ox, oy) in enumerate(others)]

    def start(ins, outs, sems):
        for cp in copies(ins, outs, sems):
            cp.start()

    def finish(ins, outs, sems):
        for cp in copies(ins, outs, sems):
            cp.wait()

    return _Comm(parts, [jax.ShapeDtypeStruct((3,) + s.shape[1:], s.dtype) for s in parts],
                 [pltpu.SemaphoreType.DMA((3 * n,))] * 2, start, finish)


def _sum_chips(name, own, recv, chip):
    _, r, hc = own.shape

    def body(chip_ref, own_ref, recv_ref, out_ref):
        acc = own_ref[...].astype(F32)
        for j in range(3):
            acc = acc + recv_ref[j].astype(F32)
        out_ref[...] = acc

    return pl.pallas_call(
        body, name=name,
        grid_spec=pltpu.PrefetchScalarGridSpec(
            num_scalar_prefetch=1, grid=(hc // LANES,),
            in_specs=[pl.BlockSpec((None, r, LANES), lambda i, chip_ref: (chip_ref[0], 0, i)),
                      pl.BlockSpec((3, r, LANES), lambda i, chip_ref: (0, 0, i))],
            out_specs=pl.BlockSpec((r, LANES), lambda i, chip_ref: (0, i))),
        out_shape=jax.ShapeDtypeStruct((r, hc), F32),
        compiler_params=_params(("arbitrary",)),
    )(chip, own, recv)


def _share_halves(halves):
    n = len(halves)

    def body(*refs):
        srcs, dsts = refs[:n], refs[n:2 * n]
        send_sems, recv_sems = refs[2 * n:]
        x, y, c, _ = _place()
        copies = [pltpu.make_async_remote_copy(
            src_ref=srcs[a], dst_ref=dsts[a], send_sem=send_sems.at[a], recv_sem=recv_sems.at[a],
            device_id=(x, y, 1 - c), device_id_type=MESH) for a in range(n)]
        for cp in copies:
            cp.start()
        for cp in copies:
            cp.wait()

    return pl.pallas_call(
        body, name="share_halves",
        in_specs=[ANY] * n, out_specs=[ANY] * n,
        out_shape=[jax.ShapeDtypeStruct(s.shape, s.dtype) for s in halves],
        scratch_shapes=[pltpu.SemaphoreType.DMA((n,)), pltpu.SemaphoreType.DMA((n,))],
    )(*halves)


def _allreduce_small(part):
    rows = part.shape[0]

    def body(in_ref, out_ref, land, send_sems, recv_sems):
        x, y, c, _ = _place()
        me = 4 * x + 2 * y + c
        land[me] = in_ref[...]
        copies = []
        for d in range(1, N_DEV):
            peer = (1 - x if d & 4 else x, 1 - y if d & 2 else y, 1 - c if d & 1 else c)
            copies.append(pltpu.make_async_remote_copy(
                src_ref=in_ref, dst_ref=land.at[me],
                send_sem=send_sems.at[d - 1], recv_sem=recv_sems.at[d - 1],
                device_id=peer, device_id_type=MESH))
        for cp in copies:
            cp.start()
        for d in range(1, N_DEV):
            px, py, pc = (1 - x if d & 4 else x, 1 - y if d & 2 else y, 1 - c if d & 1 else c)
            pltpu.make_async_remote_copy(
                src_ref=in_ref, dst_ref=land.at[4 * px + 2 * py + pc],
                send_sem=send_sems.at[d - 1], recv_sem=recv_sems.at[d - 1],
                device_id=(px, py, pc), device_id_type=MESH).wait_recv()
        for cp in copies:
            cp.wait_send()
        acc = land[0]
        for k in range(1, N_DEV):
            acc = acc + land[k]
        out_ref[...] = acc

    vmem = pl.BlockSpec(memory_space=pltpu.VMEM)
    return pl.pallas_call(
        body, name="allreduce_small",
        in_specs=[vmem], out_specs=vmem,
        out_shape=jax.ShapeDtypeStruct(part.shape, F32),
        scratch_shapes=[pltpu.VMEM((N_DEV, rows, LANES), F32),
                        pltpu.SemaphoreType.DMA((N_DEV - 1,)), pltpu.SemaphoreType.DMA((N_DEV - 1,))],
    )(part)


def _adam_update(w, g, m, v):
    nm = ADAM_B1 * m + (1.0 - ADAM_B1) * g
    nv = ADAM_B2 * v + (1.0 - ADAM_B2) * (g * g)
    m_hat = nm * (1.0 / (1.0 - ADAM_B1 ** ADAM_STEP))
    v_hat = nv * (1.0 / (1.0 - ADAM_B2 ** ADAM_STEP))
    return -ADAM_LR * (m_hat / (jnp.sqrt(v_hat) + ADAM_EPS) + ADAM_WD * w), nm, nv


def _adamw(name, w, g, m, v):
    def body(w_ref, g_ref, m_ref, v_ref, d_ref, nm_ref, nv_ref):
        d_ref[...], nm_ref[...], nv_ref[...] = _adam_update(w_ref[...], g_ref[...], m_ref[...], v_ref[...])

    spec = pl.BlockSpec(w.shape, lambda i: (0, 0))
    out = jax.ShapeDtypeStruct(w.shape, F32)
    return pl.pallas_call(
        body, name=name, grid=(1,),
        in_specs=[spec] * 4, out_specs=[spec] * 3, out_shape=[out] * 3,
        compiler_params=_params(("arbitrary",)),
    )(w, g, m, v)


def _adamw_halves(name, w, mine, theirs, m, v, core):
    rows, cols = w.shape
    hc = cols // 2
    tc = min(256, hc)
    nt = hc // tc

    def body(core_ref, w_ref, mine_ref, theirs_ref, m_ref, v_ref, g_ref, d_ref, nm_ref, nv_ref):
        gv = jnp.where(pl.program_id(0) == core_ref[0], mine_ref[...], theirs_ref[...])
        g_ref[...] = gv
        d_ref[...], nm_ref[...], nv_ref[...] = _adam_update(w_ref[...], gv, m_ref[...], v_ref[...])

    whole = pl.BlockSpec((rows, tc), lambda h, i, core_ref: (0, h * nt + i))
    half = pl.BlockSpec((rows, tc), lambda h, i, core_ref: (0, i))
    out = jax.ShapeDtypeStruct((rows, cols), F32)
    return pl.pallas_call(
        body, name=name,
        grid_spec=pltpu.PrefetchScalarGridSpec(
            num_scalar_prefetch=1, grid=(2, nt),
            in_specs=[whole, half, half, whole, whole], out_specs=[whole] * 4),
        out_shape=[out] * 4,
        compiler_params=_params(("arbitrary", "arbitrary")),
    )(core, w, mine, theirs, m, v)


WEIGHTS = ("ffn1_norm", "ffn1_gate", "ffn1_up", "ffn1_down", "mix_norm", "w_in", "b_forget", "conv_w",
           "w_o_attn", "w_o_conv", "w_out", "ffn2_norm", "ffn2_gate", "ffn2_up", "ffn2_down", "final_norm")
VEC_ROWS = 8


def _pack_small(t, conv_rows):
    conv = t["conv_w"]
    parts = [t[n].reshape(VEC_ROWS, LANES) for n in NORMS]
    parts.append(jnp.pad(conv, ((0, conv_rows - conv.shape[0]), (0, 0))))
    parts.append(jnp.pad(t["b_forget"].reshape(1, N_HEADS), ((0, 7), (0, LANES - N_HEADS))))
    return jnp.concatenate(parts, axis=0)


def _unpack_small(p, conv_rows):
    out = {n: p[VEC_ROWS * i:VEC_ROWS * (i + 1)].reshape(-1) for i, n in enumerate(NORMS)}
    base = VEC_ROWS * len(NORMS)
    out["conv_w"] = p[base:base + 3]
    out["b_forget"] = p[base + conv_rows, :N_HEADS]
    return out


def _travel(name, a):
    return a.T if name in TRANSPOSED else a


GATHER_RIDES = {"ffn1_fwd": ("w_in", "w_o_attn", "w_o_conv", "w_out"), "attn_fwd": ("ffn2_gate", "ffn2_up", "ffn2_down")}
REDUCE_RIDES = {"late": "attn_bwd", "w_in": "ffn1_bwd_dx", "ffn1": None}


class _MeshPlan(_LocalPlan):
    def __init__(self, wts, core):
        self.small, self.core = wts, core
        self.shards = {n: wts[n].astype(BF16) for n in BIG}
        self.chip_part, self.from_chips, self.pending = {}, {}, {}
        first = ("ffn1_gate", "ffn1_up", "ffn1_down")
        conv_shard = jnp.pad(wts["conv_w"], ((0, 8 - wts["conv_w"].shape[0]), (0, 0)))
        own = [self.shards[n] for n in first] + [conv_shard]
        got = _run_comm("gather_ffn1", _gather_comm(own[:-1], conv_shard))
        self.stacks = dict(zip(first + ("conv_w",), _fill_own(got, own)))

    def rider(self, kernel_name):
        if kernel_name in GATHER_RIDES:
            return _gather_comm([self.shards[n] for n in GATHER_RIDES[kernel_name]])
        if kernel_name in self.pending:
            return _chip_exchange_comm([self.chip_part[n] for n in self.pending[kernel_name]])
        return None

    def arrived(self, kernel_name, results):
        if kernel_name in GATHER_RIDES:
            names = GATHER_RIDES[kernel_name]
            self.stacks.update(zip(names, _fill_own(results, [self.shards[n] for n in names])))
        elif kernel_name in self.pending:
            self.from_chips.update(zip(self.pending[kernel_name], results))

    def reduce(self, group, grads):
        names = tuple(grads)
        from_sibling = _sibling_exchange("sibling_exchange_" + group, [grads[n] for n in names])
        self.chip_part.update({n: _add_halves("add_halves_" + n, grads[n], r, self.core)
                               for n, r in zip(names, from_sibling)})
        if REDUCE_RIDES[group] is None:
            got = _run_comm("chip_exchange_" + group, _chip_exchange_comm([self.chip_part[n] for n in names]))
            self.from_chips.update(zip(names, got))
        else:
            self.pending[REDUCE_RIDES[group]] = names


def kernel(x, ffn1_norm, ffn1_gate, ffn1_up, ffn1_down, mix_norm, w_in, b_forget, conv_w, w_o_attn, w_o_conv, w_out, ffn2_norm, ffn2_gate, ffn2_up, ffn2_down, final_norm, loss_target, m_ffn1_norm, m_ffn1_gate, m_ffn1_up, m_ffn1_down, m_mix_norm, m_w_in, m_b_forget, m_conv_w, m_w_o_attn, m_w_o_conv, m_w_out, m_ffn2_norm, m_ffn2_gate, m_ffn2_up, m_ffn2_down, m_final_norm, v_ffn1_norm, v_ffn1_gate, v_ffn1_up, v_ffn1_down, v_mix_norm, v_w_in, v_b_forget, v_conv_w, v_w_o_attn, v_w_o_conv, v_w_out, v_ffn2_norm, v_ffn2_gate, v_ffn2_up, v_ffn2_down, v_final_norm):
    given = dict(locals())
    wts = {n: _travel(n, given[n]) for n in WEIGHTS}
    mom = {n: _travel(n, given["m_" + n]) for n in WEIGHTS}
    var = {n: _travel(n, given["v_" + n]) for n in WEIGHTS}
    B, S, D = x.shape
    chip = 2 * lax.axis_index("x") + lax.axis_index("y")
    chip1 = chip.astype(jnp.int32).reshape(1)
    core = lax.axis_index("c").astype(jnp.int32).reshape(1)

    plan = _MeshPlan(wts, core)
    loss, grad_x, gs = _local_step(x.reshape(B * S, D), loss_target.reshape(B * S, D), plan, B, S)

    mine = [_sum_chips("sum_chips_" + n, plan.chip_part[n], plan.from_chips[n], chip1) for n in BIG]
    theirs = _share_halves(mine)

    conv_all = _shard_cols(gs["conv_w"]).reshape(N_CHIPS * 8, LANES)
    small_part = _pack_small({**{n: gs[n] for n in NORMS}, "conv_w": conv_all, "b_forget": gs["b_forget"][0, :N_HEADS]},
                             N_CHIPS * 8)
    base = VEC_ROWS * len(NORMS)
    small_sum = _allreduce_small(small_part)
    grads = _unpack_small(small_sum, N_CHIPS * 8)
    grads["conv_w"] = lax.dynamic_slice_in_dim(small_sum[base:base + N_CHIPS * 8], chip * 8, 8, axis=0)[:3]

    delta, new_m, new_v = {}, {}, {}
    for n, gm, gt in zip(BIG, mine, theirs):
        outs = _adamw_halves("adamw_" + n, wts[n], gm, gt, mom[n], var[n], core)
        grads[n], delta[n], new_m[n], new_v[n] = [_travel(n, o) for o in outs]
    packs = [_pack_small(t, 8) for t in (wts, grads, mom, var)]
    for out, p in zip((delta, new_m, new_v), _adamw("adamw_small", *packs)):
        out.update(_unpack_small(p, 8))

    total = lax.psum(loss[0, 0], ("x", "y", "c"))
    return (total, grad_x.reshape(B, S, D), *[grads[n] for n in WEIGHTS], *[delta[n] for n in WEIGHTS],
            *[new_m[n] for n in WEIGHTS], *[new_v[n] for n in WEIGHTS])
```

```python
import functools
import math

import jax
import jax.numpy as jnp
from jax import lax
from jax.experimental import pallas as pl
from jax.experimental.pallas import tpu as pltpu

F32 = jnp.float32
BF16 = jnp.bfloat16
MESH = pl.DeviceIdType.MESH

N_CHIPS = 4
N_DEV = 8
N_HEADS = 8
HEAD_DIM = 64
HEAD_PAIRS = N_HEADS // 2
ATTN_W = N_HEADS * HEAD_DIM
CONV_W = 512
RMS_EPS = 1e-6
FFN_RES = 0.5
LANES = 128
VMEM_LIMIT = 56 * 1024 * 1024
ROW_BLOCK = 256

ADAM_LR = 0.001
ADAM_B1 = 0.9
ADAM_B2 = 0.999
ADAM_EPS = 1e-08
ADAM_WD = 0.01
ADAM_STEP = 10

PROJ_W = 3 * ATTN_W + 3 * CONV_W + 2 * 1024
COL_CB, COL_CC, COL_CX = 3 * ATTN_W, 3 * ATTN_W + CONV_W, 3 * ATTN_W + 2 * CONV_W
COL_GATES = 3 * ATTN_W + 3 * CONV_W
N_FORGET_COL = 3 * ATTN_W


def _params(sem=None, vmem=VMEM_LIMIT):
    return pltpu.CompilerParams(dimension_semantics=sem, vmem_limit_bytes=vmem)


def _dot(a, b):
    return lax.dot_general(a, b, (((1,), (0,)), ((), ())), preferred_element_type=F32)


def _dot_nt(a, b):
    return lax.dot_general(a, b, (((1,), (1,)), ((), ())), preferred_element_type=F32)


def _dot_tn(a, b):
    return lax.dot_general(a, b, (((0,), (0,)), ((), ())), preferred_element_type=F32)


def _sigmoid(x):
    return 1.0 / (1.0 + jnp.exp(-x))


def _rms(xv):
    inv = lax.rsqrt(jnp.mean(xv * xv, axis=-1, keepdims=True) + RMS_EPS)
    return xv * inv, inv


class _Comm:
    def __init__(self, inputs, out_shape, scratch, start, finish):
        self.inputs, self.out_shape, self.scratch = list(inputs), list(out_shape), list(scratch)
        self.start, self.finish = start, finish


def _pallas(body, name, grid, in_specs, out_specs, out_shape, scratch, args, comm=None):
    sem = ("arbitrary",) * len(grid)
    if comm is None:
        outs = pl.pallas_call(body, name=name, grid=grid, in_specs=in_specs, out_specs=out_specs,
                              out_shape=out_shape, scratch_shapes=scratch, compiler_params=_params(sem))(*args)
        return list(outs), []
    n_in, n_out, n_scr = len(in_specs), len(out_specs), len(scratch)
    ci, co = len(comm.inputs), len(comm.out_shape)

    def riding(*refs):
        ins, refs = refs[:n_in], refs[n_in:]
        cins, refs = refs[:ci], refs[ci:]
        outs, refs = refs[:n_out], refs[n_out:]
        couts, refs = refs[:co], refs[co:]
        scr, sems = refs[:n_scr], refs[n_scr:]
        ids = [pl.program_id(d) for d in range(len(grid))]
        first = functools.reduce(lambda a, b: a & b, [i == 0 for i in ids])
        last = functools.reduce(lambda a, b: a & b, [i == g - 1 for i, g in zip(ids, grid)])

        @pl.when(first)
        def _():
            comm.start(cins, couts, sems)

        body(*ins, *outs, *scr)

        @pl.when(last)
        def _():
            comm.finish(cins, couts, sems)

    any_spec = pl.BlockSpec(memory_space=pl.ANY)
    outs = pl.pallas_call(
        riding, name=name, grid=grid,
        in_specs=list(in_specs) + [any_spec] * ci, out_specs=list(out_specs) + [any_spec] * co,
        out_shape=list(out_shape) + comm.out_shape, scratch_shapes=list(scratch) + comm.scratch,
        compiler_params=_params(sem))(*args, *comm.inputs)
    return list(outs[:n_out]), list(outs[n_out:])


def _rms_bwd(dn, xhat, inv, g):
    dxhat = dn * g
    dx = inv * (dxhat - xhat * jnp.mean(dxhat * xhat, axis=-1, keepdims=True))
    return dx, jnp.sum(dn * xhat, axis=0, keepdims=True)


def _ffn_fwd(name, x, g, wgt, wut, wd, tm, comm=None):
    T, D = x.shape
    K, Fs, _ = wgt.shape

    def body(x_ref, g_ref, wg_ref, wu_ref, wd_ref, out_ref, hg_ref, hu_ref, n_scr, acc_scr):
        k = pl.program_id(1)

        @pl.when(k == 0)
        def _():
            xhat, _ = _rms(x_ref[...])
            n_scr[...] = (xhat * g_ref[...]).astype(BF16)
            acc_scr[...] = jnp.zeros_like(acc_scr)

        n = n_scr[...]
        hg = _dot_nt(n, wg_ref[...])
        hu = _dot_nt(n, wu_ref[...])
        hg_ref[...] = hg.astype(BF16)
        hu_ref[...] = hu.astype(BF16)
        act = (hg * _sigmoid(hg) * hu).astype(BF16)
        acc_scr[...] += _dot(act, wd_ref[...])

        @pl.when(k == K - 1)
        def _():
            out_ref[...] = x_ref[...] + FFN_RES * acc_scr[...]

    w_spec = pl.BlockSpec((None, Fs, D), lambda i, k: (k, 0, 0))
    act_spec = pl.BlockSpec((None, tm, Fs), lambda i, k: (k, i, 0))
    return _pallas(
        body, name, (T // tm, K),
        [pl.BlockSpec((tm, D), lambda i, k: (i, 0)), pl.BlockSpec((1, D), lambda i, k: (0, 0)),
         w_spec, w_spec, w_spec],
        [pl.BlockSpec((tm, D), lambda i, k: (i, 0)), act_spec, act_spec],
        [jax.ShapeDtypeStruct((T, D), F32), jax.ShapeDtypeStruct((K, T, Fs), BF16),
         jax.ShapeDtypeStruct((K, T, Fs), BF16)],
        [pltpu.VMEM((tm, D), BF16), pltpu.VMEM((tm, D), F32)],
        (x, g, wgt, wut, wd), comm)


def _ffn_bwd_dx(name, dout, x, g, hg, hu, wgt, wut, wd, tm, comm=None):
    T, D = x.shape
    K, Fs, _ = wgt.shape

    def body(dout_ref, x_ref, g_ref, hg_ref, hu_ref, wg_ref, wu_ref, wd_ref,
             dx_ref, dhg_ref, dhu_ref, dg_ref, df_scr, dn_scr):
        i, k = pl.program_id(0), pl.program_id(1)

        @pl.when(k == 0)
        def _():
            df_scr[...] = (FFN_RES * dout_ref[...]).astype(BF16)
            dn_scr[...] = jnp.zeros_like(dn_scr)

        @pl.when((k == 0) & (i == 0))
        def _():
            dg_ref[...] = jnp.zeros_like(dg_ref)

        for r0 in range(0, tm, ROW_BLOCK):
            rows = slice(r0, r0 + ROW_BLOCK)
            dact = _dot_nt(df_scr[rows, :], wd_ref[...])
            hgv = hg_ref[rows, :].astype(F32)
            huv = hu_ref[rows, :].astype(F32)
            s = _sigmoid(hgv)
            dhu = (dact * (hgv * s)).astype(BF16)
            dhg = (dact * huv * (s * (1.0 + hgv * (1.0 - s)))).astype(BF16)
            dhg_ref[rows, :] = dhg
            dhu_ref[rows, :] = dhu
            dn_scr[rows, :] += _dot(dhg, wg_ref[...]) + _dot(dhu, wu_ref[...])

        @pl.when(k == K - 1)
        def _():
            xhat, inv = _rms(x_ref[...])
            dx, dg = _rms_bwd(dn_scr[...], xhat, inv, g_ref[...])
            dx_ref[...] = dout_ref[...] + dx
            dg_ref[...] += dg

    w_spec = pl.BlockSpec((None, Fs, D), lambda i, k: (k, 0, 0))
    act_spec = pl.BlockSpec((None, tm, Fs), lambda i, k: (k, i, 0))
    row = pl.BlockSpec((tm, D), lambda i, k: (i, 0))
    vec = pl.BlockSpec((1, D), lambda i, k: (0, 0))
    return _pallas(
        body, name, (T // tm, K),
        [row, row, vec, act_spec, act_spec, w_spec, w_spec, w_spec],
        [row, act_spec, act_spec, vec],
        [jax.ShapeDtypeStruct((T, D), F32), jax.ShapeDtypeStruct((K, T, Fs), BF16),
         jax.ShapeDtypeStruct((K, T, Fs), BF16), jax.ShapeDtypeStruct((1, D), F32)],
        [pltpu.VMEM((tm, D), BF16), pltpu.VMEM((tm, D), F32)],
        (dout, x, g, hg, hu, wgt, wut, wd), comm)


def _ffn_bwd_dw(name, dout, x, g, hg, hu, dhg, dhu, tk):
    T, D = x.shape
    K, _, Fs = hg.shape
    nt = T // tk

    def body(dout_ref, x_ref, g_ref, hg_ref, hu_ref, dhg_ref, dhu_ref,
             dwg_ref, dwu_ref, dwd_ref, accg, accu, accd):
        t = pl.program_id(1)

        @pl.when(t == 0)
        def _():
            accg[...] = jnp.zeros_like(accg)
            accu[...] = jnp.zeros_like(accu)
            accd[...] = jnp.zeros_like(accd)

        xhat, _ = _rms(x_ref[...])
        n = (xhat * g_ref[...]).astype(BF16)
        df = (FFN_RES * dout_ref[...]).astype(BF16)
        hgv = hg_ref[...].astype(F32)
        act = (hgv * _sigmoid(hgv) * hu_ref[...].astype(F32)).astype(BF16)
        accg[...] += _dot_tn(dhg_ref[...], n)
        accu[...] += _dot_tn(dhu_ref[...], n)
        accd[...] += _dot_tn(act, df)

        @pl.when(t == nt - 1)
        def _():
            dwg_ref[...] = accg[...].astype(BF16)
            dwu_ref[...] = accu[...].astype(BF16)
            dwd_ref[...] = accd[...].astype(BF16)

    act_spec = pl.BlockSpec((None, tk, Fs), lambda k, t: (k, t, 0))
    w_spec = pl.BlockSpec((None, Fs, D), lambda k, t: (k, 0, 0))
    return pl.pallas_call(
        body, name=name, grid=(K, nt),
        in_specs=[pl.BlockSpec((tk, D), lambda k, t: (t, 0)),
                  pl.BlockSpec((tk, D), lambda k, t: (t, 0)),
                  pl.BlockSpec((1, D), lambda k, t: (0, 0)),
                  act_spec, act_spec, act_spec, act_spec],
        out_specs=[w_spec, w_spec, w_spec],
        out_shape=[jax.ShapeDtypeStruct((K, Fs, D), BF16)] * 3,
        scratch_shapes=[pltpu.VMEM((Fs, D), F32)] * 3,
        compiler_params=_params(("arbitrary", "arbitrary")),
    )(dout, x, g, hg, hu, dhg, dhu)


def _mix_proj_fwd(x, g, wproj_t, wf_t, tm, tn):
    T, D = x.shape
    N = wproj_t.shape[0]

    def body(x_ref, g_ref, w_ref, wf_ref, h_ref, proj_ref, flog_ref, h_scr):
        @pl.when(pl.program_id(1) == 0)
        def _():
            xhat, _ = _rms(x_ref[...])
            h = (xhat * g_ref[...]).astype(BF16)
            h_scr[...] = h
            h_ref[...] = h
            flog_ref[...] = _dot_nt(h, wf_ref[...])

        proj_ref[...] = _dot_nt(h_scr[...], w_ref[...]).astype(BF16)

    return pl.pallas_call(
        body, name="mix_proj_fwd", grid=(T // tm, N // tn),
        in_specs=[pl.BlockSpec((tm, D), lambda i, n: (i, 0)),
                  pl.BlockSpec((1, D), lambda i, n: (0, 0)),
                  pl.BlockSpec((tn, D), lambda i, n: (n, 0)),
                  pl.BlockSpec((LANES, D), lambda i, n: (0, 0))],
        out_specs=[pl.BlockSpec((tm, D), lambda i, n: (i, 0)),
                   pl.BlockSpec((tm, tn), lambda i, n: (i, n)),
                   pl.BlockSpec((tm, LANES), lambda i, n: (i, 0))],
        out_shape=[jax.ShapeDtypeStruct((T, D), BF16),
                   jax.ShapeDtypeStruct((T, N), BF16),
                   jax.ShapeDtypeStruct((T, LANES), F32)],
        scratch_shapes=[pltpu.VMEM((tm, D), BF16)],
        compiler_params=_params(("arbitrary", "arbitrary")),
    )(x, g, wproj_t, wf_t)


def _log_sigmoid(z):
    return -(jnp.maximum(-z, 0.0) + jnp.log(1.0 + jnp.exp(-jnp.abs(z))))


def _tri(n, lower):
    r = lax.broadcasted_iota(jnp.int32, (n, n), 0)
    c = lax.broadcasted_iota(jnp.int32, (n, n), 1)
    return jnp.where((r >= c) if lower else (r <= c), 1.0, 0.0).astype(F32)


def _dot_f32(a, b):
    return lax.dot_general(a, b, (((1,), (0,)), ((), ())), preferred_element_type=F32,
                           precision=lax.Precision.HIGHEST)


def _fgate_fwd(flog, bias, B, S, ch):
    def body(flog_ref, b_ref, cum_ref):
        tri = _tri(ch, True)
        carry = jnp.zeros((1, LANES), F32)
        for c0 in range(0, S, ch):
            lf = _log_sigmoid(flog_ref[c0:c0 + ch, :] + b_ref[...])
            cs = _dot_f32(tri, lf) + carry
            cum_ref[c0:c0 + ch, :] = cs
            carry = cs[ch - 1:ch, :]

    return pl.pallas_call(
        body, name="fgate_fwd", grid=(B,),
        in_specs=[pl.BlockSpec((S, LANES), lambda b: (b, 0)),
                  pl.BlockSpec((1, LANES), lambda b: (0, 0))],
        out_specs=pl.BlockSpec((S, LANES), lambda b: (b, 0)),
        out_shape=jax.ShapeDtypeStruct((B * S, LANES), F32),
        compiler_params=_params(("arbitrary",)),
    )(flog, bias)


def _fgate_bwd(dcum, flog, bias, B, S, ch):
    def body(dcum_ref, flog_ref, b_ref, dflog_ref, db_ref):
        @pl.when(pl.program_id(0) == 0)
        def _():
            db_ref[...] = jnp.zeros_like(db_ref)

        tri = _tri(ch, False)
        carry = jnp.zeros((1, LANES), F32)
        db = jnp.zeros((1, LANES), F32)
        for c0 in range(S - ch, -1, -ch):
            dlf = _dot_f32(tri, dcum_ref[c0:c0 + ch, :]) + carry
            carry = dlf[0:1, :]
            z = flog_ref[c0:c0 + ch, :] + b_ref[...]
            dz = dlf * _sigmoid(-z)
            dflog_ref[c0:c0 + ch, :] = dz
            db = db + jnp.sum(dz, axis=0, keepdims=True)
        db_ref[...] += db

    return pl.pallas_call(
        body, name="fgate_bwd", grid=(B,),
        in_specs=[pl.BlockSpec((S, LANES), lambda b: (b, 0)),
                  pl.BlockSpec((S, LANES), lambda b: (b, 0)),
                  pl.BlockSpec((1, LANES), lambda b: (0, 0))],
        out_specs=[pl.BlockSpec((S, LANES), lambda b: (b, 0)),
                   pl.BlockSpec((1, LANES), lambda b: (0, 0))],
        out_shape=[jax.ShapeDtypeStruct((B * S, LANES), F32),
                   jax.ShapeDtypeStruct((1, LANES), F32)],
        compiler_params=_params(("arbitrary",)),
    )(dcum, flog, bias)


def _pick_lane(tile, h):
    lane = lax.broadcasted_iota(jnp.int32, tile.shape, 1)
    return jnp.sum(jnp.where(lane == h, tile, 0.0), axis=1, keepdims=True)


def _put_lane(col, h, width=LANES):
    lane = lax.broadcasted_iota(jnp.int32, (col.shape[0], width), 1)
    return jnp.where(lane == h, col, 0.0)


def _pick_row(tile, h):
    row = lax.broadcasted_iota(jnp.int32, tile.shape, 0)
    return jnp.sum(jnp.where(row == h, tile, 0.0), axis=0, keepdims=True)


def _put_row(vec, h):
    row = lax.broadcasted_iota(jnp.int32, (8, vec.shape[1]), 0)
    return jnp.where(row == h, vec, 0.0)


def _causal(tq):
    r = lax.broadcasted_iota(jnp.int32, (tq, tq), 0)
    c = lax.broadcasted_iota(jnp.int32, (tq, tq), 1)
    return r >= c


def _head_halves(t):
    lo = lax.broadcasted_iota(jnp.int32, t.shape, 1) < HEAD_DIM
    zero = jnp.zeros_like(t)
    return jnp.where(lo, t, zero), jnp.where(lo, zero, t)


NEG = -1e30
ATTN_SCALE = 1.0 / math.sqrt(HEAD_DIM)


def _scaled(q):
    return (q.astype(F32) * ATTN_SCALE).astype(q.dtype)


def _attn_fwd(proj, cum, cum_t, B, S, tq, comm=None):
    nq = S // tq

    def body(q_ref, k_ref, v_ref, cum_ref, cumt_ref, o_ref, lse_ref):
        qi, hp = pl.program_id(1), pl.program_id(2)
        qm = _head_halves(_scaled(q_ref[...]))
        cumv = cum_ref[...]
        cq = [_pick_lane(cumv, 2 * hp + e) for e in range(2)]

        def tile(j, carry, masked):
            off = pl.multiple_of(j * tq, tq)
            kj = k_ref[pl.ds(off, tq), :]
            vj = v_ref[pl.ds(off, tq), :]
            ct = cumt_ref[j]
            new = []
            for e in range(2):
                m, l, acc = carry[e]
                s = _dot_nt(qm[e], kj) + (cq[e] - _pick_row(ct, 2 * hp + e))
                if masked:
                    s = jnp.where(_causal(tq), s, NEG)
                m_new = jnp.maximum(m, jnp.max(s, axis=1, keepdims=True))
                p = jnp.exp(s - m_new)
                alpha = jnp.exp(m - m_new)
                l = alpha * l + jnp.sum(p, axis=1, keepdims=True)
                acc = alpha * acc + _dot(p.astype(BF16), vj)
                new.append((m_new, l, acc))
            return tuple(new)

        one = (jnp.full((tq, 1), NEG, F32), jnp.zeros((tq, 1), F32), jnp.zeros((tq, LANES), F32))
        carry = lax.fori_loop(0, qi, lambda j, c: tile(j, c, False), (one, one))
        (ma, la, acca), (mb, lb, accb) = tile(qi, carry, True)
        lo = lax.broadcasted_iota(jnp.int32, (tq, LANES), 1) < HEAD_DIM
        o_ref[...] = jnp.where(lo, acca / la, accb / lb).astype(BF16)

        @pl.when(hp == 0)
        def _():
            lse_ref[...] = jnp.zeros_like(lse_ref)

        lse_ref[...] += _put_lane(ma + jnp.log(la), 2 * hp) + _put_lane(mb + jnp.log(lb), 2 * hp + 1)

    kv = lambda first: pl.BlockSpec((S, LANES), lambda b, i, hp: (b, first + hp))
    return _pallas(
        body, "attn_fwd", (B, nq, HEAD_PAIRS),
        [pl.BlockSpec((tq, LANES), lambda b, i, hp: (b * nq + i, hp)),
         kv(ATTN_W // LANES), kv(2 * ATTN_W // LANES),
         pl.BlockSpec((tq, LANES), lambda b, i, hp: (b * nq + i, 0)),
         pl.BlockSpec((None, nq, 8, tq), lambda b, i, hp: (b, 0, 0, 0))],
        [pl.BlockSpec((tq, LANES), lambda b, i, hp: (b * nq + i, hp)),
         pl.BlockSpec((tq, LANES), lambda b, i, hp: (b * nq + i, 0))],
        [jax.ShapeDtypeStruct((B * S, ATTN_W), BF16), jax.ShapeDtypeStruct((B * S, LANES), F32)],
        [], (proj, proj, proj, cum, cum_t), comm)


def _attn_bwd(proj, o, do, lse, cum, cum_t, B, S, tq, comm=None):
    nq = S // tq

    def body(q_ref, k_ref, v_ref, o_ref, do_ref, lse_ref, cum_ref, cumt_ref,
             dq_ref, dk_ref, dv_ref, dcq_ref, dck_ref, dq_scr):
        hp, kj = pl.program_id(1), pl.program_id(2)

        @pl.when(kj == 0)
        def _():
            dq_scr[...] = jnp.zeros_like(dq_scr)

        @pl.when((kj == 0) & (hp == 0))
        def _():
            dcq_ref[...] = jnp.zeros_like(dcq_ref)
            dck_ref[...] = jnp.zeros_like(dck_ref)

        kv = k_ref[...]
        vv = v_ref[...]
        km = _head_halves(kv)
        ct = cumt_ref[...]
        ck = [_pick_row(ct, 2 * hp + e) for e in range(2)]

        def tile(i, carry, masked):
            dk, dv, dcol = carry
            off = pl.multiple_of(i * tq, tq)
            qi = q_ref[pl.ds(off, tq), :]
            ov = o_ref[pl.ds(off, tq), :].astype(F32)
            qm = _head_halves(_scaled(qi))
            dom = _head_halves(do_ref[pl.ds(off, tq), :])
            cumv = cum_ref[pl.ds(off, tq), :]
            lsev = lse_ref[pl.ds(off, tq), :]
            dcq = jnp.zeros((tq, LANES), F32)
            dq = jnp.zeros((tq, LANES), F32)
            dcol_new = []
            for e in range(2):
                delta = jnp.sum(dom[e].astype(F32) * ov, axis=1, keepdims=True)
                row_term = _pick_lane(cumv, 2 * hp + e) - _pick_lane(lsev, 2 * hp + e)
                p = jnp.exp(_dot_nt(qm[e], kv) + row_term - ck[e])
                if masked:
                    p = jnp.where(_causal(tq), p, 0.0)
                dv = dv + _dot_tn(p.astype(BF16), dom[e])
                ds = p * (_dot_nt(dom[e], vv) - delta)
                dcol_new.append(dcol[e] + jnp.sum(ds, axis=0, keepdims=True))
                dcq = dcq + _put_lane(jnp.sum(ds, axis=1, keepdims=True), 2 * hp + e)
                dsb = ds.astype(BF16)
                dk = dk + _dot_tn(dsb, qm[e])
                dq = dq + _dot(dsb, km[e]) * ATTN_SCALE
            dq_scr[pl.ds(off, tq), :] += dq
            dcq_ref[pl.ds(off, tq), :] += dcq
            return dk, dv, tuple(dcol_new)

        zero_row = jnp.zeros((1, tq), F32)
        init = (jnp.zeros((tq, LANES), F32), jnp.zeros((tq, LANES), F32), (zero_row, zero_row))
        carry = tile(kj, init, True)
        dk, dv, dcol = lax.fori_loop(kj + 1, nq, lambda i, c: tile(i, c, False), carry)
        dk_ref[...] = dk.astype(BF16)
        dv_ref[...] = dv.astype(BF16)
        dck_ref[kj] += -(_put_row(dcol[0], 2 * hp) + _put_row(dcol[1], 2 * hp + 1))

        @pl.when(kj == nq - 1)
        def _():
            dq_ref[...] = dq_scr[...].astype(BF16)

    seq = lambda first: pl.BlockSpec((S, LANES), lambda b, hp, j: (b, first + hp))
    tile_in = lambda first: pl.BlockSpec((tq, LANES), lambda b, hp, j: (b * nq + j, first + hp))
    lanes0 = pl.BlockSpec((S, LANES), lambda b, hp, j: (b, 0))
    out = jax.ShapeDtypeStruct((B * S, ATTN_W), BF16)
    return _pallas(
        body, "attn_bwd", (B, HEAD_PAIRS, nq),
        [seq(0), tile_in(ATTN_W // LANES), tile_in(2 * ATTN_W // LANES), seq(0), seq(0), lanes0, lanes0,
         pl.BlockSpec((None, None, 8, tq), lambda b, hp, j: (b, j, 0, 0))],
        [seq(0), tile_in(0), tile_in(0), lanes0,
         pl.BlockSpec((None, nq, 8, tq), lambda b, hp, j: (b, 0, 0, 0))],
        [out, out, out, jax.ShapeDtypeStruct((B * S, LANES), F32), jax.ShapeDtypeStruct((B, nq, 8, tq), F32)],
        [pltpu.VMEM((S, LANES), F32)],
        (proj, proj, proj, o, do, lse, cum, cum_t), comm)


def _shift_down(u, n):
    row = lax.broadcasted_iota(jnp.int32, u.shape, 0)
    return jnp.where(row >= n, pltpu.roll(u, n, 0), 0.0)


def _shift_up(u, n):
    rows = u.shape[0]
    row = lax.broadcasted_iota(jnp.int32, u.shape, 0)
    return jnp.where(row < rows - n, pltpu.roll(u, rows - n, 0), 0.0)


def _conv_specs(S):
    cb = pl.BlockSpec((S, LANES), lambda g, b: (b, COL_CB // LANES + g))
    cc = pl.BlockSpec((S, LANES), lambda g, b: (b, COL_CC // LANES + g))
    cx = pl.BlockSpec((S, LANES), lambda g, b: (b, COL_CX // LANES + g))
    w = pl.BlockSpec((8, LANES), lambda g, b: (0, g))
    return cb, cc, cx, w


def _conv_fwd(proj, conv_w, B, S):
    def body(cb_ref, cc_ref, cx_ref, w_ref, y_ref):
        u = cc_ref[...].astype(F32) * cx_ref[...].astype(F32)
        w = w_ref[...]
        conv = w[0:1, :] * _shift_down(u, 2) + w[1:2, :] * _shift_down(u, 1) + w[2:3, :] * u
        y_ref[...] = (cb_ref[...].astype(F32) * conv).astype(BF16)

    cb, cc, cx, w = _conv_specs(S)
    return pl.pallas_call(
        body, name="conv_fwd", grid=(CONV_W // LANES, B),
        in_specs=[cb, cc, cx, w],
        out_specs=pl.BlockSpec((S, LANES), lambda g, b: (b, g)),
        out_shape=jax.ShapeDtypeStruct((B * S, CONV_W), BF16),
        compiler_params=_params(("arbitrary", "arbitrary")),
    )(proj, proj, proj, conv_w)


def _conv_bwd(dy, proj, conv_w, B, S):
    def body(dy_ref, cb_ref, cc_ref, cx_ref, w_ref, dcb_ref, dcc_ref, dcx_ref, dw_ref):
        @pl.when(pl.program_id(1) == 0)
        def _():
            dw_ref[...] = jnp.zeros_like(dw_ref)

        ccv = cc_ref[...].astype(F32)
        cxv = cx_ref[...].astype(F32)
        u = ccv * cxv
        u1 = _shift_down(u, 1)
        u2 = _shift_down(u, 2)
        w = w_ref[...]
        conv = w[0:1, :] * u2 + w[1:2, :] * u1 + w[2:3, :] * u
        dyv = dy_ref[...].astype(F32)
        dcb_ref[...] = (dyv * conv).astype(BF16)
        dconv = dyv * cb_ref[...].astype(F32)
        du = w[2:3, :] * dconv + w[1:2, :] * _shift_up(dconv, 1) + w[0:1, :] * _shift_up(dconv, 2)
        dcc_ref[...] = (du * cxv).astype(BF16)
        dcx_ref[...] = (du * ccv).astype(BF16)
        row = lax.broadcasted_iota(jnp.int32, (8, LANES), 0)
        dw = jnp.where(row == 0, jnp.sum(dconv * u2, axis=0, keepdims=True),
                       jnp.where(row == 1, jnp.sum(dconv * u1, axis=0, keepdims=True),
                                 jnp.where(row == 2, jnp.sum(dconv * u, axis=0, keepdims=True), 0.0)))
        dw_ref[...] += dw

    cb, cc, cx, w = _conv_specs(S)
    out = pl.BlockSpec((S, LANES), lambda g, b: (b, g))
    return pl.pallas_call(
        body, name="conv_bwd", grid=(CONV_W // LANES, B),
        in_specs=[out, cb, cc, cx, w],
        out_specs=[out, out, out, w],
        out_shape=[jax.ShapeDtypeStruct((B * S, CONV_W), BF16)] * 3 + [jax.ShapeDtypeStruct((8, CONV_W), F32)],
        compiler_params=_params(("arbitrary", "arbitrary")),
    )(dy, proj, proj, proj, conv_w)


def _gate_specs(tm, D):
    ga = pl.BlockSpec((tm, D), lambda i: (i, COL_GATES // D))
    gc = pl.BlockSpec((tm, D), lambda i: (i, COL_GATES // D + 1))
    return ga, gc


def _mix_out_fwd(x, o, yc, proj, woa, woc, wout, tm):
    T, D = x.shape

    def body(x_ref, o_ref, yc_ref, ga_ref, gc_ref, woa_ref, woc_ref, wout_ref, out_ref):
        ya = _dot(o_ref[...], woa_ref[...])
        yp = _dot(yc_ref[...], woc_ref[...])
        merged = _sigmoid(ga_ref[...].astype(F32)) * ya + _sigmoid(gc_ref[...].astype(F32)) * yp
        out_ref[...] = x_ref[...] + _dot(merged.astype(BF16), wout_ref[...])

    ga, gc = _gate_specs(tm, D)
    row = lambda w: pl.BlockSpec((tm, w), lambda i: (i, 0))
    whole = lambda a: pl.BlockSpec(a.shape, lambda i: (0, 0))
    return pl.pallas_call(
        body, name="mix_out_fwd", grid=(T // tm,),
        in_specs=[row(D), row(ATTN_W), row(CONV_W), ga, gc, whole(woa), whole(woc), whole(wout)],
        out_specs=row(D),
        out_shape=jax.ShapeDtypeStruct((T, D), F32),
        compiler_params=_params(("arbitrary",)),
    )(x, o, yc, proj, proj, woa, woc, wout)


def _mix_out_bwd(dx, o, yc, proj, woa, woc, wout, tm):
    T, D = dx.shape
    nt = T // tm

    def body(dx_ref, o_ref, yc_ref, ga_ref, gc_ref, woa_ref, woc_ref, wout_ref,
             do_ref, dyc_ref, dg_ref, dwoa_ref, dwoc_ref, dwout_ref, acca, accc, acco):
        t = pl.program_id(0)

        @pl.when(t == 0)
        def _():
            acca[...] = jnp.zeros_like(acca)
            accc[...] = jnp.zeros_like(accc)
            acco[...] = jnp.zeros_like(acco)

        dxb = dx_ref[...].astype(BF16)
        ov, ycv = o_ref[...], yc_ref[...]
        ya = _dot(ov, woa_ref[...])
        yp = _dot(ycv, woc_ref[...])
        sa = _sigmoid(ga_ref[...].astype(F32))
        sc = _sigmoid(gc_ref[...].astype(F32))
        merged = (sa * ya + sc * yp).astype(BF16)
        dm = _dot_nt(dxb, wout_ref[...])
        dya = (dm * sa).astype(BF16)
        dyp = (dm * sc).astype(BF16)
        dg_ref[:, :D] = (dm * ya * sa * (1.0 - sa)).astype(BF16)
        dg_ref[:, D:] = (dm * yp * sc * (1.0 - sc)).astype(BF16)
        do_ref[...] = _dot_nt(dya, woa_ref[...]).astype(BF16)
        dyc_ref[...] = _dot_nt(dyp, woc_ref[...]).astype(BF16)
        acca[...] += _dot_tn(ov, dya)
        accc[...] += _dot_tn(ycv, dyp)
        acco[...] += _dot_tn(merged, dxb)

        @pl.when(t == nt - 1)
        def _():
            dwoa_ref[...] = acca[...].astype(BF16)
            dwoc_ref[...] = accc[...].astype(BF16)
            dwout_ref[...] = acco[...].astype(BF16)

    ga, gc = _gate_specs(tm, D)
    row = lambda w: pl.BlockSpec((tm, w), lambda i: (i, 0))
    whole = lambda a: pl.BlockSpec(a.shape, lambda i: (0, 0))
    return pl.pallas_call(
        body, name="mix_out_bwd", grid=(nt,),
        in_specs=[row(D), row(ATTN_W), row(CONV_W), ga, gc, whole(woa), whole(woc), whole(wout)],
        out_specs=[row(ATTN_W), row(CONV_W), row(2 * D), whole(woa), whole(woc), whole(wout)],
        out_shape=[jax.ShapeDtypeStruct((T, ATTN_W), BF16), jax.ShapeDtypeStruct((T, CONV_W), BF16),
                   jax.ShapeDtypeStruct((T, 2 * D), BF16),
                   jax.ShapeDtypeStruct(woa.shape, BF16), jax.ShapeDtypeStruct(woc.shape, BF16),
                   jax.ShapeDtypeStruct(wout.shape, BF16)],
        scratch_shapes=[pltpu.VMEM(woa.shape, F32), pltpu.VMEM(woc.shape, F32), pltpu.VMEM(wout.shape, F32)],
        compiler_params=_params(("arbitrary",)),
    )(dx, o, yc, proj, proj, woa, woc, wout)


def _proj_pieces(dq, dk, dv, dcb, dcc, dcx, dgates, dflog):
    D = dgates.shape[1] // 2
    return [(dq, ATTN_W, 0), (dk, ATTN_W, 0), (dv, ATTN_W, 0), (dcb, CONV_W, 0), (dcc, CONV_W, 0), (dcx, CONV_W, 0),
            (dgates, D, 0), (dgates, D, 1), (dflog, LANES, 0)]


def _mix_proj_bwd_dx(dres, x, g, pieces, wproj_t, wf_t, tm):
    T, D = x.shape
    n = len(pieces)
    w_blocks = [(ATTN_W, 0), (ATTN_W, 1), (ATTN_W, 2), (CONV_W, 3), (CONV_W, 4), (CONV_W, 5),
                (D, COL_GATES // D), (D, COL_GATES // D + 1)]

    def body(*refs):
        dres_ref, x_ref, g_ref = refs[:3]
        p_refs, w_refs = refs[3:3 + n], refs[3 + n:3 + 2 * n]
        dx_ref, dg_ref = refs[3 + 2 * n:]

        @pl.when(pl.program_id(0) == 0)
        def _():
            dg_ref[...] = jnp.zeros_like(dg_ref)

        dh = _dot(p_refs[0][...].astype(BF16), w_refs[0][...])
        for p_ref, w_ref in zip(p_refs[1:], w_refs[1:]):
            dh = dh + _dot(p_ref[...].astype(BF16), w_ref[...])
        xhat, inv = _rms(x_ref[...])
        dx, dg = _rms_bwd(dh, xhat, inv, g_ref[...])
        dx_ref[...] = dres_ref[...] + dx
        dg_ref[...] += dg

    row = pl.BlockSpec((tm, D), lambda i: (i, 0))
    vec = pl.BlockSpec((1, D), lambda i: (0, 0))
    p_specs = [pl.BlockSpec((tm, w), lambda i, cb=cb: (i, cb)) for _, w, cb in pieces]
    w_specs = [pl.BlockSpec((r, D), lambda i, rb=rb: (rb, 0)) for r, rb in w_blocks]
    w_specs.append(pl.BlockSpec((LANES, D), lambda i: (0, 0)))
    return pl.pallas_call(
        body, name="mix_proj_bwd_dx", grid=(T // tm,),
        in_specs=[row, row, vec] + p_specs + w_specs,
        out_specs=[row, vec],
        out_shape=[jax.ShapeDtypeStruct((T, D), F32), jax.ShapeDtypeStruct((1, D), F32)],
        compiler_params=_params(("arbitrary",)),
    )(dres, x, g, *[p for p, _, _ in pieces], *([wproj_t] * len(w_blocks)), wf_t)


def _matmuls_tn(name, pieces, b, tk):
    T, N = b.shape
    nt = T // tk
    n = len(pieces)

    def body(*refs):
        a_refs, b_ref, out_refs, accs = refs[:n], refs[n], refs[n + 1:2 * n + 1], refs[2 * n + 1:]
        t = pl.program_id(0)

        @pl.when(t == 0)
        def _():
            for acc in accs:
                acc[...] = jnp.zeros_like(acc)

        bv = b_ref[...]
        for a_ref, acc in zip(a_refs, accs):
            acc[...] += _dot_tn(a_ref[...].astype(BF16), bv)

        @pl.when(t == nt - 1)
        def _():
            for out_ref, acc in zip(out_refs, accs):
                out_ref[...] = acc[...].astype(BF16)

    return pl.pallas_call(
        body, name=name, grid=(nt,),
        in_specs=[pl.BlockSpec((tk, w), lambda t, cb=cb: (t, cb)) for _, w, cb in pieces]
        + [pl.BlockSpec((tk, N), lambda t: (t, 0))],
        out_specs=[pl.BlockSpec((w, N), lambda t: (0, 0)) for _, w, _ in pieces],
        out_shape=[jax.ShapeDtypeStruct((w, N), BF16) for _, w, _ in pieces],
        scratch_shapes=[pltpu.VMEM((w, N), F32) for _, w, _ in pieces],
        compiler_params=_params(("arbitrary",)),
    )(*[a for a, _, _ in pieces], b)


def _final_loss(x, target, g, tm):
    T, D = x.shape

    def body(x_ref, t_ref, g_ref, dx_ref, loss_ref, dg_ref):
        @pl.when(pl.program_id(0) == 0)
        def _():
            loss_ref[...] = jnp.zeros_like(loss_ref)
            dg_ref[...] = jnp.zeros_like(dg_ref)

        xhat, inv = _rms(x_ref[...])
        err = xhat * g_ref[...] - t_ref[...]
        loss_ref[...] += 0.5 * jnp.sum(jnp.sum(err * err, axis=1, keepdims=True), axis=0, keepdims=True) / D
        dx, dg = _rms_bwd(err * (1.0 / D), xhat, inv, g_ref[...])
        dx_ref[...] = dx
        dg_ref[...] += dg

    row = pl.BlockSpec((tm, D), lambda i: (i, 0))
    return pl.pallas_call(
        body, name="final_loss", grid=(T // tm,),
        in_specs=[row, row, pl.BlockSpec((1, D), lambda i: (0, 0))],
        out_specs=[row, pl.BlockSpec((1, LANES), lambda i: (0, 0)), pl.BlockSpec((1, D), lambda i: (0, 0))],
        out_shape=[jax.ShapeDtypeStruct((T, D), F32), jax.ShapeDtypeStruct((1, LANES), F32),
                   jax.ShapeDtypeStruct((1, D), F32)],
        compiler_params=_params(("arbitrary",)),
    )(x, target, g)


class _LocalPlan:
    def __init__(self, stacks, small):
        self.stacks, self.small, self.grads = stacks, small, {}

    def weights(self, group):
        return _LAYOUTS[group](self.stacks, self.small)

    def rider(self, kernel_name):
        return None

    def arrived(self, kernel_name, results):
        pass

    def reduce(self, group, grads):
        self.grads.update(grads)


def _local_step(x, target, plan, B, S):
    T, D = x.shape
    tm = min(512, T)
    tq = min(512, S)
    nq = S // tq
    ch = min(256, S)

    def riding(kernel_name, build):
        results, brought = build(plan.rider(kernel_name))
        plan.arrived(kernel_name, brought)
        return results

    w1 = plan.weights("ffn1")
    x1, hg1, hu1 = riding("ffn1_fwd", lambda comm: _ffn_fwd(
        "ffn1_fwd", x, w1["ffn1_norm"], w1["ffn1_gate"], w1["ffn1_up"], w1["ffn1_down"], tm, comm))
    wm = plan.weights("mix")
    h, proj, flog = _mix_proj_fwd(x1, wm["mix_norm"], wm["w_proj"], wm["w_f"], tm, 1280)
    cum = _fgate_fwd(flog, wm["b_forget"], B, S, ch)
    cum_t = jnp.transpose(cum[:, :N_HEADS].reshape(B, nq, tq, N_HEADS), (0, 1, 3, 2))
    o, lse = riding("attn_fwd", lambda comm: _attn_fwd(proj, cum, cum_t, B, S, tq, comm))
    yc = _conv_fwd(proj, wm["conv_w"], B, S)
    x2 = _mix_out_fwd(x1, o, yc, proj, wm["w_o_attn"], wm["w_o_conv"], wm["w_out"], tm)
    w2 = plan.weights("ffn2")
    x3, hg2, hu2 = _ffn_fwd("ffn2_fwd", x2, w2["ffn2_norm"], w2["ffn2_gate"], w2["ffn2_up"], w2["ffn2_down"], tm)[0]
    dx3, loss, d_final_norm = _final_loss(x3, target, w2["final_norm"], tm)

    g = {"final_norm": d_final_norm}
    dx2, dhg2, dhu2, g["ffn2_norm"] = _ffn_bwd_dx("ffn2_bwd_dx", dx3, x2, w2["ffn2_norm"], hg2, hu2,
                                                  w2["ffn2_gate"], w2["ffn2_up"], w2["ffn2_down"], tm)[0]
    late = dict(zip(("ffn2_gate", "ffn2_up", "ffn2_down"),
                    _ffn_bwd_dw("ffn2_bwd_dw", dx3, x2, w2["ffn2_norm"], hg2, hu2, dhg2, dhu2, tm)))
    do, dyc, dgates, dwoa, dwoc, dwout = _mix_out_bwd(
        dx2, o, yc, proj, wm["w_o_attn"], wm["w_o_conv"], wm["w_out"], tm)
    late.update(w_o_attn=_shard_cols(dwoa), w_o_conv=_shard_cols(dwoc), w_out=dwout.reshape(N_CHIPS, -1, D))
    plan.reduce("late", late)
    dq, dk, dv, dcq, dck = riding("attn_bwd", lambda comm: _attn_bwd(proj, o, do, lse, cum, cum_t, B, S, tq, comm))
    dcum = dcq + jnp.pad(jnp.transpose(dck, (0, 1, 3, 2)).reshape(T, N_HEADS), ((0, 0), (0, LANES - N_HEADS)))
    dflog, g["b_forget"] = _fgate_bwd(dcum, flog, wm["b_forget"], B, S, ch)
    dcb, dcc, dcx, g["conv_w"] = _conv_bwd(dyc, proj, wm["conv_w"], B, S)
    pieces = _proj_pieces(dq, dk, dv, dcb, dcc, dcx, dgates, dflog)
    dx1, g["mix_norm"] = _mix_proj_bwd_dx(dx2, x1, wm["mix_norm"], pieces, wm["w_proj"], wm["w_f"], min(256, T))
    dwq, dwk, dwv, dwcb, dwcc, dwcx = _matmuls_tn("mix_dw_a", pieces[:6], h, tm)
    dwga, dwgc, dwf = _matmuls_tn("mix_dw_b", pieces[6:], h, tm)
    dwin_t = jnp.concatenate([dwq, dwk, dwv, dwf[:N_HEADS], dwcb, dwcc, dwcx, dwga, dwgc], axis=0)
    plan.reduce("w_in", {"w_in": dwin_t.reshape(N_CHIPS, -1, D)})
    grad_x, dhg1, dhu1, g["ffn1_norm"] = riding("ffn1_bwd_dx", lambda comm: _ffn_bwd_dx(
        "ffn1_bwd_dx", dx1, x, w1["ffn1_norm"], hg1, hu1, w1["ffn1_gate"], w1["ffn1_up"], w1["ffn1_down"], tm, comm))
    plan.reduce("ffn1", dict(zip(("ffn1_gate", "ffn1_up", "ffn1_down"),
                                 _ffn_bwd_dw("ffn1_bwd_dw", dx1, x, w1["ffn1_norm"], hg1, hu1, dhg1, dhu1, tm))))
    return loss, grad_x, g


TRANSPOSED = ("ffn1_gate", "ffn1_up", "ffn2_gate", "ffn2_up", "w_in")
NORMS = ("ffn1_norm", "mix_norm", "ffn2_norm", "final_norm")


def _unshard_cols(a):
    return jnp.transpose(a, (1, 0, 2)).reshape(a.shape[1], N_CHIPS * a.shape[2])


def _shard_cols(a):
    return jnp.transpose(a.reshape(a.shape[0], N_CHIPS, a.shape[1] // N_CHIPS), (1, 0, 2))


def _layout_ffn(which):
    def layout(st, small):
        w = {n: st[n] for n in (which + "_gate", which + "_up", which + "_down")}
        w[which + "_norm"] = small[which + "_norm"].reshape(1, -1)
        if which == "ffn2":
            w["final_norm"] = small["final_norm"].reshape(1, -1)
        return w
    return layout


def _layout_mix(st, small):
    win_t = st["w_in"].reshape(-1, st["w_in"].shape[2])
    return {
        "w_proj": jnp.concatenate([win_t[:N_FORGET_COL], win_t[N_FORGET_COL + N_HEADS:]], axis=0),
        "w_f": jnp.pad(win_t[N_FORGET_COL:N_FORGET_COL + N_HEADS], ((0, LANES - N_HEADS), (0, 0))),
        "w_o_attn": _unshard_cols(st["w_o_attn"]),
        "w_o_conv": _unshard_cols(st["w_o_conv"]),
        "w_out": st["w_out"].reshape(-1, st["w_out"].shape[2]),
        "conv_w": _unshard_cols(st["conv_w"]),
        "mix_norm": small["mix_norm"].reshape(1, -1),
        "b_forget": jnp.pad(small["b_forget"].reshape(1, -1), ((0, 0), (0, LANES - N_HEADS))),
    }


_LAYOUTS = {"ffn1": _layout_ffn("ffn1"), "mix": _layout_mix, "ffn2": _layout_ffn("ffn2")}


ANY = pl.BlockSpec(memory_space=pl.ANY)
BIG = ("ffn1_gate", "ffn1_up", "ffn1_down", "w_in", "w_o_attn", "w_o_conv", "w_out",
       "ffn2_gate", "ffn2_up", "ffn2_down")


def _place():
    x, y, c = lax.axis_index("x"), lax.axis_index("y"), lax.axis_index("c")
    others = [(1 - x, y), (x, 1 - y), (1 - x, 1 - y)]
    return x, y, c, others


def _col_halves(cols, c):
    hc = cols // 2
    return pl.ds(pl.multiple_of(c * hc, LANES), hc), pl.ds(pl.multiple_of((1 - c) * hc, LANES), hc)


def _gather_comm(shards, conv_shard=None):
    n = len(shards)
    inputs = list(shards) + ([] if conv_shard is None else [conv_shard])

    def copies(ins, outs, sems):
        send_sems, recv_sems, pass_send, pass_recv = sems[:4]
        x, y, c, others = _place()

        def chip_copy(a, j, chip):
            mine, _ = _col_halves(ins[a].shape[1], c)
            return pltpu.make_async_remote_copy(
                src_ref=ins[a].at[:, mine], dst_ref=outs[a].at[chip, :, mine],
                send_sem=send_sems.at[3 * a + j], recv_sem=recv_sems.at[3 * a + j],
                device_id=(*others[j], c), device_id_type=MESH)

        def pass_copy(a, j, chip, half):
            return pltpu.make_async_remote_copy(
                src_ref=outs[a].at[chip, :, half], dst_ref=outs[a].at[chip, :, half],
                send_sem=pass_send.at[3 * a + j], recv_sem=pass_recv.at[3 * a + j],
                device_id=(x, y, 1 - c), device_id_type=MESH)

        def conv_copy(j, chip):
            return pltpu.make_async_remote_copy(
                src_ref=ins[n], dst_ref=outs[n].at[chip],
                send_sem=sems[4].at[j], recv_sem=sems[5].at[j],
                device_id=(*others[j], c), device_id_type=MESH)

        me = 2 * x + y
        sends = [chip_copy(a, j, me) for a in range(n) for j in range(3)]
        if conv_shard is not None:
            sends += [conv_copy(j, me) for j in range(3)]
        return c, others, sends, chip_copy, pass_copy, conv_copy

    def start(ins, outs, sems):
        for cp in copies(ins, outs, sems)[2]:
            cp.start()

    def finish(ins, outs, sems):
        c, others, sends, chip_copy, pass_copy, conv_copy = copies(ins, outs, sems)
        passed = []
        for a in range(n):
            mine, _ = _col_halves(ins[a].shape[1], c)
            for j, (ox, oy) in enumerate(others):
                chip_copy(a, j, 2 * ox + oy).wait_recv()
                passed.append(pass_copy(a, j, 2 * ox + oy, mine))
                passed[-1].start()
        for a in range(n):
            _, theirs = _col_halves(ins[a].shape[1], c)
            for j, (ox, oy) in enumerate(others):
                pass_copy(a, j, 2 * ox + oy, theirs).wait_recv()
        if conv_shard is not None:
            for j, (ox, oy) in enumerate(others):
                conv_copy(j, 2 * ox + oy).wait_recv()
        for cp in sends + passed:
            cp.wait_send()

    scratch = [pltpu.SemaphoreType.DMA((3 * n,))] * 4
    if conv_shard is not None:
        scratch += [pltpu.SemaphoreType.DMA((3,))] * 2
    return _Comm(inputs, [jax.ShapeDtypeStruct((N_CHIPS,) + s.shape, s.dtype) for s in inputs], scratch, start, finish)


def _fill_own(stacks, shards):
    chip = 2 * lax.axis_index("x") + lax.axis_index("y")
    return [lax.dynamic_update_index_in_dim(st, s, chip, 0) for st, s in zip(stacks, shards)]


def _run_comm(name, comm):
    ci, co = len(comm.inputs), len(comm.out_shape)

    def body(*refs):
        comm.start(refs[:ci], refs[ci:ci + co], refs[ci + co:])
        comm.finish(refs[:ci], refs[ci:ci + co], refs[ci + co:])

    return pl.pallas_call(body, name=name, in_specs=[ANY] * ci, out_specs=[ANY] * co, out_shape=comm.out_shape,
                          scratch_shapes=comm.scratch)(*comm.inputs)


def _sibling_exchange(name, grads):
    n = len(grads)

    def body(*refs):
        srcs, dsts = refs[:n], refs[n:2 * n]
        send_sems, recv_sems = refs[2 * n:]
        x, y, c, _ = _place()
        copies = []
        for a in range(n):
            _, theirs = _col_halves(srcs[a].shape[2], c)
            copies.append(pltpu.make_async_remote_copy(
                src_ref=srcs[a].at[:, :, theirs], dst_ref=dsts[a],
                send_sem=send_sems.at[a], recv_sem=recv_sems.at[a],
                device_id=(x, y, 1 - c), device_id_type=MESH))
        for cp in copies:
            cp.start()
        for cp in copies:
            cp.wait()

    half = lambda s: jax.ShapeDtypeStruct((s.shape[0], s.shape[1], s.shape[2] // 2), s.dtype)
    return pl.pallas_call(
        body, name=name,
        in_specs=[ANY] * n, out_specs=[ANY] * n, out_shape=[half(s) for s in grads],
        scratch_shapes=[pltpu.SemaphoreType.DMA((n,)), pltpu.SemaphoreType.DMA((n,))],
    )(*grads)


def _add_halves(name, grad, recv, core):
    K, r, cols = grad.shape
    hc = cols // 2

    def body(core_ref, g_ref, r_ref, out_ref):
        out_ref[...] = (g_ref[...].astype(F32) + r_ref[...].astype(F32)).astype(BF16)

    return pl.pallas_call(
        body, name=name,
        grid_spec=pltpu.PrefetchScalarGridSpec(
            num_scalar_prefetch=1, grid=(K,),
            in_specs=[pl.BlockSpec((None, r, hc), lambda k, core_ref: (k, 0, core_ref[0])),
                      pl.BlockSpec((None, r, hc), lambda k, core_ref: (k, 0, 0))],
            out_specs=pl.BlockSpec((None, r, hc), lambda k, core_ref: (k, 0, 0))),
        out_shape=jax.ShapeDtypeStruct((K, r, hc), BF16),
        compiler_params=_params(("arbitrary",)),
    )(core, grad, recv)


def _chip_exchange_comm(parts):
    n = len(parts)

    def copies(ins, outs, sems):
        x, y, c, others = _place()
        return [pltpu.make_async_remote_copy(
            src_ref=ins[a].at[2 * ox + oy], dst_ref=outs[a].at[j],
            send_sem=sems[0].at[3 * a + j], recv_sem=sems[1].at[3 * a + j],
            device_id=(ox, oy, c), device_id_type=MESH) for a in range(n) for j, (ox, oy) in enumerate(others)]

    def start(ins, outs, sems):
        for cp in copies(ins, outs, sems):
            cp.start()

    def finish(ins, outs, sems):
        for cp in copies(ins, outs, sems):
            cp.wait()

    return _Comm(parts, [jax.ShapeDtypeStruct((3,) + s.shape[1:], s.dtype) for s in parts],
                 [pltpu.SemaphoreType.DMA((3 * n,))] * 2, start, finish)


def _sum_chips(name, own, recv, chip):
    _, r, hc = own.shape

    def body(chip_ref, own_ref, recv_ref, out_ref):
        acc = own_ref[...].astype(F32)
        for j in range(3):
            acc = acc + recv_ref[j].astype(F32)
        out_ref[...] = acc

    return pl.pallas_call(
        body, name=name,
        grid_spec=pltpu.PrefetchScalarGridSpec(
            num_scalar_prefetch=1, grid=(hc // LANES,),
            in_specs=[pl.BlockSpec((None, r, LANES), lambda i, chip_ref: (chip_ref[0], 0, i)),
                      pl.BlockSpec((3, r, LANES), lambda i, chip_ref: (0, 0, i))],
            out_specs=pl.BlockSpec((r, LANES), lambda i, chip_ref: (0, i))),
        out_shape=jax.ShapeDtypeStruct((r, hc), F32),
        compiler_params=_params(("arbitrary",)),
    )(chip, own, recv)


def _share_halves(halves):
    n = len(halves)

    def body(*refs):
        srcs, dsts = refs[:n], refs[n:2 * n]
        send_sems, recv_sems = refs[2 * n:]
        x, y, c, _ = _place()
        copies = [pltpu.make_async_remote_copy(
            src_ref=srcs[a], dst_ref=dsts[a], send_sem=send_sems.at[a], recv_sem=recv_sems.at[a],
            device_id=(x, y, 1 - c), device_id_type=MESH) for a in range(n)]
        for cp in copies:
            cp.start()
        for cp in copies:
            cp.wait()

    return pl.pallas_call(
        body, name="share_halves",
        in_specs=[ANY] * n, out_specs=[ANY] * n,
        out_shape=[jax.ShapeDtypeStruct(s.shape, s.dtype) for s in halves],
        scratch_shapes=[pltpu.SemaphoreType.DMA((n,)), pltpu.SemaphoreType.DMA((n,))],
    )(*halves)


def _allreduce_small(part):
    rows = part.shape[0]

    def body(in_ref, out_ref, land, send_sems, recv_sems):
        x, y, c, _ = _place()
        me = 4 * x + 2 * y + c
        land[me] = in_ref[...]
        copies = []
        for d in range(1, N_DEV):
            peer = (1 - x if d & 4 else x, 1 - y if d & 2 else y, 1 - c if d & 1 else c)
            copies.append(pltpu.make_async_remote_copy(
                src_ref=in_ref, dst_ref=land.at[me],
                send_sem=send_sems.at[d - 1], recv_sem=recv_sems.at[d - 1],
                device_id=peer, device_id_type=MESH))
        for cp in copies:
            cp.start()
        for d in range(1, N_DEV):
            px, py, pc = (1 - x if d & 4 else x, 1 - y if d & 2 else y, 1 - c if d & 1 else c)
            pltpu.make_async_remote_copy(
                src_ref=in_ref, dst_ref=land.at[4 * px + 2 * py + pc],
                send_sem=send_sems.at[d - 1], recv_sem=recv_sems.at[d - 1],
                device_id=(px, py, pc), device_id_type=MESH).wait_recv()
        for cp in copies:
            cp.wait_send()
        acc = land[0]
        for k in range(1, N_DEV):
            acc = acc + land[k]
        out_ref[...] = acc

    vmem = pl.BlockSpec(memory_space=pltpu.VMEM)
    return pl.pallas_call(
        body, name="allreduce_small",
        in_specs=[vmem], out_specs=vmem,
        out_shape=jax.ShapeDtypeStruct(part.shape, F32),
        scratch_shapes=[pltpu.VMEM((N_DEV, rows, LANES), F32),
                        pltpu.SemaphoreType.DMA((N_DEV - 1,)), pltpu.SemaphoreType.DMA((N_DEV - 1,))],
    )(part)


def _adam_update(w, g, m, v):
    nm = ADAM_B1 * m + (1.0 - ADAM_B1) * g
    nv = ADAM_B2 * v + (1.0 - ADAM_B2) * (g * g)
    m_hat = nm * (1.0 / (1.0 - ADAM_B1 ** ADAM_STEP))
    v_hat = nv * (1.0 / (1.0 - ADAM_B2 ** ADAM_STEP))
    return -ADAM_LR * (m_hat / (jnp.sqrt(v_hat) + ADAM_EPS) + ADAM_WD * w), nm, nv


def _adamw(name, w, g, m, v):
    def body(w_ref, g_ref, m_ref, v_ref, d_ref, nm_ref, nv_ref):
        d_ref[...], nm_ref[...], nv_ref[...] = _adam_update(w_ref[...], g_ref[...], m_ref[...], v_ref[...])

    spec = pl.BlockSpec(w.shape, lambda i: (0, 0))
    out = jax.ShapeDtypeStruct(w.shape, F32)
    return pl.pallas_call(
        body, name=name, grid=(1,),
        in_specs=[spec] * 4, out_specs=[spec] * 3, out_shape=[out] * 3,
        compiler_params=_params(("arbitrary",)),
    )(w, g, m, v)


def _adamw_halves(name, w, mine, theirs, m, v, core):
    rows, cols = w.shape
    hc = cols // 2
    tc = min(256, hc)
    nt = hc // tc

    def body(core_ref, w_ref, mine_ref, theirs_ref, m_ref, v_ref, g_ref, d_ref, nm_ref, nv_ref):
        gv = jnp.where(pl.program_id(0) == core_ref[0], mine_ref[...], theirs_ref[...])
        g_ref[...] = gv
        d_ref[...], nm_ref[...], nv_ref[...] = _adam_update(w_ref[...], gv, m_ref[...], v_ref[...])

    whole = pl.BlockSpec((rows, tc), lambda h, i, core_ref: (0, h * nt + i))
    mine_spec = pl.BlockSpec((rows, tc), lambda h, i, core_ref: (0, jnp.where(h == core_ref[0], i, 0)))
    theirs_spec = pl.BlockSpec((rows, tc), lambda h, i, core_ref: (0, jnp.where(h == core_ref[0], 0, i)))
    out = jax.ShapeDtypeStruct((rows, cols), F32)
    return pl.pallas_call(
        body, name=name,
        grid_spec=pltpu.PrefetchScalarGridSpec(
            num_scalar_prefetch=1, grid=(2, nt),
            in_specs=[whole, mine_spec, theirs_spec, whole, whole], out_specs=[whole] * 4),
        out_shape=[out] * 4,
        compiler_params=_params(("arbitrary", "arbitrary")),
    )(core, w, mine, theirs, m, v)


WEIGHTS = ("ffn1_norm", "ffn1_gate", "ffn1_up", "ffn1_down", "mix_norm", "w_in", "b_forget", "conv_w",
           "w_o_attn", "w_o_conv", "w_out", "ffn2_norm", "ffn2_gate", "ffn2_up", "ffn2_down", "final_norm")
VEC_ROWS = 8


def _pack_small(t, conv_rows):
    conv = t["conv_w"]
    parts = [t[n].reshape(VEC_ROWS, LANES) for n in NORMS]
    parts.append(jnp.pad(conv, ((0, conv_rows - conv.shape[0]), (0, 0))))
    parts.append(jnp.pad(t["b_forget"].reshape(1, N_HEADS), ((0, 7), (0, LANES - N_HEADS))))
    return jnp.concatenate(parts, axis=0)


def _unpack_small(p, conv_rows):
    out = {n: p[VEC_ROWS * i:VEC_ROWS * (i + 1)].reshape(-1) for i, n in enumerate(NORMS)}
    base = VEC_ROWS * len(NORMS)
    out["conv_w"] = p[base:base + 3]
    out["b_forget"] = p[base + conv_rows, :N_HEADS]
    return out


def _travel(name, a):
    return a.T if name in TRANSPOSED else a


GATHER_RIDES = {"ffn1_fwd": ("w_in", "w_o_attn", "w_o_conv", "w_out"), "attn_fwd": ("ffn2_gate", "ffn2_up", "ffn2_down")}
REDUCE_RIDES = {"late": "attn_bwd", "w_in": "ffn1_bwd_dx", "ffn1": None}


class _MeshPlan(_LocalPlan):
    def __init__(self, wts, core):
        self.small, self.core = wts, core
        self.shards = {n: wts[n].astype(BF16) for n in BIG}
        self.chip_part, self.from_chips, self.pending = {}, {}, {}
        first = ("ffn1_gate", "ffn1_up", "ffn1_down")
        conv_shard = jnp.pad(wts["conv_w"], ((0, 8 - wts["conv_w"].shape[0]), (0, 0)))
        own = [self.shards[n] for n in first] + [conv_shard]
        got = _run_comm("gather_ffn1", _gather_comm(own[:-1], conv_shard))
        self.stacks = dict(zip(first + ("conv_w",), _fill_own(got, own)))

    def rider(self, kernel_name):
        if kernel_name in GATHER_RIDES:
            return _gather_comm([self.shards[n] for n in GATHER_RIDES[kernel_name]])
        if kernel_name in self.pending:
            return _chip_exchange_comm([self.chip_part[n] for n in self.pending[kernel_name]])
        return None

    def arrived(self, kernel_name, results):
        if kernel_name in GATHER_RIDES:
            names = GATHER_RIDES[kernel_name]
            self.stacks.update(zip(names, _fill_own(results, [self.shards[n] for n in names])))
        elif kernel_name in self.pending:
            self.from_chips.update(zip(self.pending[kernel_name], results))

    def reduce(self, group, grads):
        names = tuple(grads)
        from_sibling = _sibling_exchange("sibling_exchange_" + group, [grads[n] for n in names])
        self.chip_part.update({n: _add_halves("add_halves_" + n, grads[n], r, self.core)
                               for n, r in zip(names, from_sibling)})
        if REDUCE_RIDES[group] is None:
            got = _run_comm("chip_exchange_" + group, _chip_exchange_comm([self.chip_part[n] for n in names]))
            self.from_chips.update(zip(names, got))
        else:
            self.pending[REDUCE_RIDES[group]] = names


def kernel(x, ffn1_norm, ffn1_gate, ffn1_up, ffn1_down, mix_norm, w_in, b_forget, conv_w, w_o_attn, w_o_conv, w_out, ffn2_norm, ffn2_gate, ffn2_up, ffn2_down, final_norm, loss_target, m_ffn1_norm, m_ffn1_gate, m_ffn1_up, m_ffn1_down, m_mix_norm, m_w_in, m_b_forget, m_conv_w, m_w_o_attn, m_w_o_conv, m_w_out, m_ffn2_norm, m_ffn2_gate, m_ffn2_up, m_ffn2_down, m_final_norm, v_ffn1_norm, v_ffn1_gate, v_ffn1_up, v_ffn1_down, v_mix_norm, v_w_in, v_b_forget, v_conv_w, v_w_o_attn, v_w_o_conv, v_w_out, v_ffn2_norm, v_ffn2_gate, v_ffn2_up, v_ffn2_down, v_final_norm):
    given = dict(locals())
    wts = {n: _travel(n, given[n]) for n in WEIGHTS}
    mom = {n: _travel(n, given["m_" + n]) for n in WEIGHTS}
    var = {n: _travel(n, given["v_" + n]) for n in WEIGHTS}
    B, S, D = x.shape
    chip = 2 * lax.axis_index("x") + lax.axis_index("y")
    chip1 = chip.astype(jnp.int32).reshape(1)
    core = lax.axis_index("c").astype(jnp.int32).reshape(1)

    plan = _MeshPlan(wts, core)
    loss, grad_x, gs = _local_step(x.reshape(B * S, D), loss_target.reshape(B * S, D), plan, B, S)

    mine = [_sum_chips("sum_chips_" + n, plan.chip_part[n], plan.from_chips[n], chip1) for n in BIG]
    theirs = _share_halves(mine)

    conv_all = _shard_cols(gs["conv_w"]).reshape(N_CHIPS * 8, LANES)
    small_part = _pack_small({**{n: gs[n] for n in NORMS}, "conv_w": conv_all, "b_forget": gs["b_forget"][0, :N_HEADS]},
                             N_CHIPS * 8)
    base = VEC_ROWS * len(NORMS)
    small_sum = _allreduce_small(small_part)
    grads = _unpack_small(small_sum, N_CHIPS * 8)
    grads["conv_w"] = lax.dynamic_slice_in_dim(small_sum[base:base + N_CHIPS * 8], chip * 8, 8, axis=0)[:3]

    delta, new_m, new_v = {}, {}, {}
    for n, gm, gt in zip(BIG, mine, theirs):
        outs = _adamw_halves("adamw_" + n, wts[n], gm, gt, mom[n], var[n], core)
        grads[n], delta[n], new_m[n], new_v[n] = [_travel(n, o) for o in outs]
    packs = [_pack_small(t, 8) for t in (wts, grads, mom, var)]
    for out, p in zip((delta, new_m, new_v), _adamw("adamw_small", *packs)):
        out.update(_unpack_small(p, 8))

    total = lax.psum(loss[0, 0], ("x", "y", "c"))
    return (total, grad_x.reshape(B, S, D), *[grads[n] for n in WEIGHTS], *[delta[n] for n in WEIGHTS],
            *[new_m[n] for n in WEIGHTS], *[new_v[n] for n in WEIGHTS])
```

```python
import functools
import math

import jax
import jax.numpy as jnp
from jax import lax
from jax.experimental import pallas as pl
from jax.experimental.pallas import tpu as pltpu

F32 = jnp.float32
BF16 = jnp.bfloat16
MESH = pl.DeviceIdType.MESH

N_CHIPS = 4
N_DEV = 8
N_HEADS = 8
HEAD_DIM = 64
HEAD_PAIRS = N_HEADS // 2
ATTN_W = N_HEADS * HEAD_DIM
CONV_W = 512
RMS_EPS = 1e-6
FFN_RES = 0.5
LANES = 128
VMEM_LIMIT = 56 * 1024 * 1024
ROW_BLOCK = 256

ADAM_LR = 0.001
ADAM_B1 = 0.9
ADAM_B2 = 0.999
ADAM_EPS = 1e-08
ADAM_WD = 0.01
ADAM_STEP = 10

PROJ_W = 3 * ATTN_W + 3 * CONV_W + 2 * 1024
COL_CB, COL_CC, COL_CX = 3 * ATTN_W, 3 * ATTN_W + CONV_W, 3 * ATTN_W + 2 * CONV_W
COL_GATES = 3 * ATTN_W + 3 * CONV_W
N_FORGET_COL = 3 * ATTN_W


def _params(sem=None, vmem=VMEM_LIMIT):
    return pltpu.CompilerParams(dimension_semantics=sem, vmem_limit_bytes=vmem)


def _dot(a, b):
    return lax.dot_general(a, b, (((1,), (0,)), ((), ())), preferred_element_type=F32)


def _dot_nt(a, b):
    return lax.dot_general(a, b, (((1,), (1,)), ((), ())), preferred_element_type=F32)


def _dot_tn(a, b):
    return lax.dot_general(a, b, (((0,), (0,)), ((), ())), preferred_element_type=F32)


def _sigmoid(x):
    return 1.0 / (1.0 + jnp.exp(-x))


def _rms(xv):
    inv = lax.rsqrt(jnp.mean(xv * xv, axis=-1, keepdims=True) + RMS_EPS)
    return xv * inv, inv


class _Comm:
    def __init__(self, inputs, out_shape, scratch, start, finish):
        self.inputs, self.out_shape, self.scratch = list(inputs), list(out_shape), list(scratch)
        self.start, self.finish = start, finish


def _pallas(body, name, grid, in_specs, out_specs, out_shape, scratch, args, comm=None):
    sem = ("arbitrary",) * len(grid)
    if comm is None:
        outs = pl.pallas_call(body, name=name, grid=grid, in_specs=in_specs, out_specs=out_specs,
                              out_shape=out_shape, scratch_shapes=scratch, compiler_params=_params(sem))(*args)
        return list(outs), []
    n_in, n_out, n_scr = len(in_specs), len(out_specs), len(scratch)
    ci, co = len(comm.inputs), len(comm.out_shape)

    def riding(*refs):
        ins, refs = refs[:n_in], refs[n_in:]
        cins, refs = refs[:ci], refs[ci:]
        outs, refs = refs[:n_out], refs[n_out:]
        couts, refs = refs[:co], refs[co:]
        scr, sems = refs[:n_scr], refs[n_scr:]
        ids = [pl.program_id(d) for d in range(len(grid))]
        first = functools.reduce(lambda a, b: a & b, [i == 0 for i in ids])
        last = functools.reduce(lambda a, b: a & b, [i == g - 1 for i, g in zip(ids, grid)])

        @pl.when(first)
        def _():
            comm.start(cins, couts, sems)

        body(*ins, *outs, *scr)

        @pl.when(last)
        def _():
            comm.finish(cins, couts, sems)

    any_spec = pl.BlockSpec(memory_space=pl.ANY)
    outs = pl.pallas_call(
        riding, name=name, grid=grid,
        in_specs=list(in_specs) + [any_spec] * ci, out_specs=list(out_specs) + [any_spec] * co,
        out_shape=list(out_shape) + comm.out_shape, scratch_shapes=list(scratch) + comm.scratch,
        compiler_params=_params(sem))(*args, *comm.inputs)
    return list(outs[:n_out]), list(outs[n_out:])


def _rms_bwd(dn, xhat, inv, g):
    dxhat = dn * g
    dx = inv * (dxhat - xhat * jnp.mean(dxhat * xhat, axis=-1, keepdims=True))
    return dx, jnp.sum(dn * xhat, axis=0, keepdims=True)


def _ffn_fwd(name, x, g, wgt, wut, wd, tm, comm=None):
    T, D = x.shape
    K, Fs, _ = wgt.shape

    def body(x_ref, g_ref, wg_ref, wu_ref, wd_ref, out_ref, hg_ref, hu_ref, n_scr, acc_scr):
        k = pl.program_id(1)

        @pl.when(k == 0)
        def _():
            xhat, _ = _rms(x_ref[...])
            n_scr[...] = (xhat * g_ref[...]).astype(BF16)
            acc_scr[...] = jnp.zeros_like(acc_scr)

        n = n_scr[...]
        hg = _dot_nt(n, wg_ref[...])
        hu = _dot_nt(n, wu_ref[...])
        hg_ref[...] = hg.astype(BF16)
        hu_ref[...] = hu.astype(BF16)
        act = (hg * _sigmoid(hg) * hu).astype(BF16)
        acc_scr[...] += _dot(act, wd_ref[...])

        @pl.when(k == K - 1)
        def _():
            out_ref[...] = x_ref[...] + FFN_RES * acc_scr[...]

    w_spec = pl.BlockSpec((None, Fs, D), lambda i, k: (k, 0, 0))
    act_spec = pl.BlockSpec((None, tm, Fs), lambda i, k: (k, i, 0))
    return _pallas(
        body, name, (T // tm, K),
        [pl.BlockSpec((tm, D), lambda i, k: (i, 0)), pl.BlockSpec((1, D), lambda i, k: (0, 0)),
         w_spec, w_spec, w_spec],
        [pl.BlockSpec((tm, D), lambda i, k: (i, 0)), act_spec, act_spec],
        [jax.ShapeDtypeStruct((T, D), F32), jax.ShapeDtypeStruct((K, T, Fs), BF16),
         jax.ShapeDtypeStruct((K, T, Fs), BF16)],
        [pltpu.VMEM((tm, D), BF16), pltpu.VMEM((tm, D), F32)],
        (x, g, wgt, wut, wd), comm)


def _ffn_bwd_dx(name, dout, x, g, hg, hu, wgt, wut, wd, tm, comm=None):
    T, D = x.shape
    K, Fs, _ = wgt.shape

    def body(dout_ref, x_ref, g_ref, hg_ref, hu_ref, wg_ref, wu_ref, wd_ref,
             dx_ref, dhg_ref, dhu_ref, dg_ref, df_scr, dn_scr):
        i, k = pl.program_id(0), pl.program_id(1)

        @pl.when(k == 0)
        def _():
            df_scr[...] = (FFN_RES * dout_ref[...]).astype(BF16)
            dn_scr[...] = jnp.zeros_like(dn_scr)

        @pl.when((k == 0) & (i == 0))
        def _():
            dg_ref[...] = jnp.zeros_like(dg_ref)

        for r0 in range(0, tm, ROW_BLOCK):
            rows = slice(r0, r0 + ROW_BLOCK)
            dact = _dot_nt(df_scr[rows, :], wd_ref[...])
            hgv = hg_ref[rows, :].astype(F32)
            huv = hu_ref[rows, :].astype(F32)
            s = _sigmoid(hgv)
            dhu = (dact * (hgv * s)).astype(BF16)
            dhg = (dact * huv * (s * (1.0 + hgv * (1.0 - s)))).astype(BF16)
            dhg_ref[rows, :] = dhg
            dhu_ref[rows, :] = dhu
            dn_scr[rows, :] += _dot(dhg, wg_ref[...]) + _dot(dhu, wu_ref[...])

        @pl.when(k == K - 1)
        def _():
            xhat, inv = _rms(x_ref[...])
            dx, dg = _rms_bwd(dn_scr[...], xhat, inv, g_ref[...])
            dx_ref[...] = dout_ref[...] + dx
            dg_ref[...] += dg

    w_spec = pl.BlockSpec((None, Fs, D), lambda i, k: (k, 0, 0))
    act_spec = pl.BlockSpec((None, tm, Fs), lambda i, k: (k, i, 0))
    row = pl.BlockSpec((tm, D), lambda i, k: (i, 0))
    vec = pl.BlockSpec((1, D), lambda i, k: (0, 0))
    return _pallas(
        body, name, (T // tm, K),
        [row, row, vec, act_spec, act_spec, w_spec, w_spec, w_spec],
        [row, act_spec, act_spec, vec],
        [jax.ShapeDtypeStruct((T, D), F32), jax.ShapeDtypeStruct((K, T, Fs), BF16),
         jax.ShapeDtypeStruct((K, T, Fs), BF16), jax.ShapeDtypeStruct((1, D), F32)],
        [pltpu.VMEM((tm, D), BF16), pltpu.VMEM((tm, D), F32)],
        (dout, x, g, hg, hu, wgt, wut, wd), comm)


def _ffn_bwd_dw(name, dout, x, g, hg, hu, dhg, dhu, tk):
    T, D = x.shape
    K, _, Fs = hg.shape
    nt = T // tk

    def body(dout_ref, x_ref, g_ref, hg_ref, hu_ref, dhg_ref, dhu_ref,
             dwg_ref, dwu_ref, dwd_ref, accg, accu, accd):
        t = pl.program_id(1)

        @pl.when(t == 0)
        def _():
            accg[...] = jnp.zeros_like(accg)
            accu[...] = jnp.zeros_like(accu)
            accd[...] = jnp.zeros_like(accd)

        xhat, _ = _rms(x_ref[...])
        n = (xhat * g_ref[...]).astype(BF16)
        df = (FFN_RES * dout_ref[...]).astype(BF16)
        hgv = hg_ref[...].astype(F32)
        act = (hgv * _sigmoid(hgv) * hu_ref[...].astype(F32)).astype(BF16)
        accg[...] += _dot_tn(dhg_ref[...], n)
        accu[...] += _dot_tn(dhu_ref[...], n)
        accd[...] += _dot_tn(act, df)

        @pl.when(t == nt - 1)
        def _():
            dwg_ref[...] = accg[...].astype(BF16)
            dwu_ref[...] = accu[...].astype(BF16)
            dwd_ref[...] = accd[...].astype(BF16)

    act_spec = pl.BlockSpec((None, tk, Fs), lambda k, t: (k, t, 0))
    w_spec = pl.BlockSpec((None, Fs, D), lambda k, t: (k, 0, 0))
    return pl.pallas_call(
        body, name=name, grid=(K, nt),
        in_specs=[pl.BlockSpec((tk, D), lambda k, t: (t, 0)),
                  pl.BlockSpec((tk, D), lambda k, t: (t, 0)),
                  pl.BlockSpec((1, D), lambda k, t: (0, 0)),
                  act_spec, act_spec, act_spec, act_spec],
        out_specs=[w_spec, w_spec, w_spec],
        out_shape=[jax.ShapeDtypeStruct((K, Fs, D), BF16)] * 3,
        scratch_shapes=[pltpu.VMEM((Fs, D), F32)] * 3,
        compiler_params=_params(("arbitrary", "arbitrary")),
    )(dout, x, g, hg, hu, dhg, dhu)


def _mix_proj_fwd(x, g, wproj_t, wf_t, tm, tn):
    T, D = x.shape
    N = wproj_t.shape[0]

    def body(x_ref, g_ref, w_ref, wf_ref, h_ref, proj_ref, flog_ref, h_scr):
        @pl.when(pl.program_id(1) == 0)
        def _():
            xhat, _ = _rms(x_ref[...])
            h = (xhat * g_ref[...]).astype(BF16)
            h_scr[...] = h
            h_ref[...] = h
            flog_ref[...] = _dot_nt(h, wf_ref[...])

        proj_ref[...] = _dot_nt(h_scr[...], w_ref[...]).astype(BF16)

    return pl.pallas_call(
        body, name="mix_proj_fwd", grid=(T // tm, N // tn),
        in_specs=[pl.BlockSpec((tm, D), lambda i, n: (i, 0)),
                  pl.BlockSpec((1, D), lambda i, n: (0, 0)),
                  pl.BlockSpec((tn, D), lambda i, n: (n, 0)),
                  pl.BlockSpec((LANES, D), lambda i, n: (0, 0))],
        out_specs=[pl.BlockSpec((tm, D), lambda i, n: (i, 0)),
                   pl.BlockSpec((tm, tn), lambda i, n: (i, n)),
                   pl.BlockSpec((tm, LANES), lambda i, n: (i, 0))],
        out_shape=[jax.ShapeDtypeStruct((T, D), BF16),
                   jax.ShapeDtypeStruct((T, N), BF16),
                   jax.ShapeDtypeStruct((T, LANES), F32)],
        scratch_shapes=[pltpu.VMEM((tm, D), BF16)],
        compiler_params=_params(("arbitrary", "arbitrary")),
    )(x, g, wproj_t, wf_t)


def _log_sigmoid(z):
    return -(jnp.maximum(-z, 0.0) + jnp.log(1.0 + jnp.exp(-jnp.abs(z))))


def _tri(n, lower):
    r = lax.broadcasted_iota(jnp.int32, (n, n), 0)
    c = lax.broadcasted_iota(jnp.int32, (n, n), 1)
    return jnp.where((r >= c) if lower else (r <= c), 1.0, 0.0).astype(F32)


def _dot_f32(a, b):
    return lax.dot_general(a, b, (((1,), (0,)), ((), ())), preferred_element_type=F32,
                           precision=lax.Precision.HIGHEST)


def _fgate_fwd(flog, bias, B, S, ch):
    def body(flog_ref, b_ref, cum_ref):
        tri = _tri(ch, True)
        carry = jnp.zeros((1, LANES), F32)
        for c0 in range(0, S, ch):
            lf = _log_sigmoid(flog_ref[c0:c0 + ch, :] + b_ref[...])
            cs = _dot_f32(tri, lf) + carry
            cum_ref[c0:c0 + ch, :] = cs
            carry = cs[ch - 1:ch, :]

    return pl.pallas_call(
        body, name="fgate_fwd", grid=(B,),
        in_specs=[pl.BlockSpec((S, LANES), lambda b: (b, 0)),
                  pl.BlockSpec((1, LANES), lambda b: (0, 0))],
        out_specs=pl.BlockSpec((S, LANES), lambda b: (b, 0)),
        out_shape=jax.ShapeDtypeStruct((B * S, LANES), F32),
        compiler_params=_params(("arbitrary",)),
    )(flog, bias)


def _fgate_bwd(dcum, flog, bias, B, S, ch):
    def body(dcum_ref, flog_ref, b_ref, dflog_ref, db_ref):
        @pl.when(pl.program_id(0) == 0)
        def _():
            db_ref[...] = jnp.zeros_like(db_ref)

        tri = _tri(ch, False)
        carry = jnp.zeros((1, LANES), F32)
        db = jnp.zeros((1, LANES), F32)
        for c0 in range(S - ch, -1, -ch):
            dlf = _dot_f32(tri, dcum_ref[c0:c0 + ch, :]) + carry
            carry = dlf[0:1, :]
            z = flog_ref[c0:c0 + ch, :] + b_ref[...]
            dz = dlf * _sigmoid(-z)
            dflog_ref[c0:c0 + ch, :] = dz
            db = db + jnp.sum(dz, axis=0, keepdims=True)
        db_ref[...] += db

    return pl.pallas_call(
        body, name="fgate_bwd", grid=(B,),
        in_specs=[pl.BlockSpec((S, LANES), lambda b: (b, 0)),
                  pl.BlockSpec((S, LANES), lambda b: (b, 0)),
                  pl.BlockSpec((1, LANES), lambda b: (0, 0))],
        out_specs=[pl.BlockSpec((S, LANES), lambda b: (b, 0)),
                   pl.BlockSpec((1, LANES), lambda b: (0, 0))],
        out_shape=[jax.ShapeDtypeStruct((B * S, LANES), F32),
                   jax.ShapeDtypeStruct((1, LANES), F32)],
        compiler_params=_params(("arbitrary",)),
    )(dcum, flog, bias)


def _pick_lane(tile, h):
    lane = lax.broadcasted_iota(jnp.int32, tile.shape, 1)
    return jnp.sum(jnp.where(lane == h, tile, 0.0), axis=1, keepdims=True)


def _put_lane(col, h, width=LANES):
    lane = lax.broadcasted_iota(jnp.int32, (col.shape[0], width), 1)
    return jnp.where(lane == h, col, 0.0)


def _pick_row(tile, h):
    row = lax.broadcasted_iota(jnp.int32, tile.shape, 0)
    return jnp.sum(jnp.where(row == h, tile, 0.0), axis=0, keepdims=True)


def _put_row(vec, h):
    row = lax.broadcasted_iota(jnp.int32, (8, vec.shape[1]), 0)
    return jnp.where(row == h, vec, 0.0)


def _causal(tq):
    r = lax.broadcasted_iota(jnp.int32, (tq, tq), 0)
    c = lax.broadcasted_iota(jnp.int32, (tq, tq), 1)
    return r >= c


def _head_halves(t):
    lo = lax.broadcasted_iota(jnp.int32, t.shape, 1) < HEAD_DIM
    zero = jnp.zeros_like(t)
    return jnp.where(lo, t, zero), jnp.where(lo, zero, t)


NEG = -1e30
ATTN_SCALE = 1.0 / math.sqrt(HEAD_DIM)


def _scaled(q):
    return (q.astype(F32) * ATTN_SCALE).astype(q.dtype)


def _attn_fwd(proj, cum, cum_t, B, S, tq, comm=None):
    nq = S // tq

    def body(q_ref, k_ref, v_ref, cum_ref, cumt_ref, o_ref, lse_ref):
        qi, hp = pl.program_id(1), pl.program_id(2)
        qm = _head_halves(_scaled(q_ref[...]))
        cumv = cum_ref[...]
        cq = [_pick_lane(cumv, 2 * hp + e) for e in range(2)]

        def tile(j, carry, masked):
            off = pl.multiple_of(j * tq, tq)
            kj = k_ref[pl.ds(off, tq), :]
            vj = v_ref[pl.ds(off, tq), :]
            ct = cumt_ref[j]
            new = []
            for e in range(2):
                m, l, acc = carry[e]
                s = _dot_nt(qm[e], kj) + (cq[e] - _pick_row(ct, 2 * hp + e))
                if masked:
                    s = jnp.where(_causal(tq), s, NEG)
                m_new = jnp.maximum(m, jnp.max(s, axis=1, keepdims=True))
                p = jnp.exp(s - m_new)
                alpha = jnp.exp(m - m_new)
                l = alpha * l + jnp.sum(p, axis=1, keepdims=True)
                acc = alpha * acc + _dot(p.astype(BF16), vj)
                new.append((m_new, l, acc))
            return tuple(new)

        one = (jnp.full((tq, 1), NEG, F32), jnp.zeros((tq, 1), F32), jnp.zeros((tq, LANES), F32))
        carry = lax.fori_loop(0, qi, lambda j, c: tile(j, c, False), (one, one))
        (ma, la, acca), (mb, lb, accb) = tile(qi, carry, True)
        lo = lax.broadcasted_iota(jnp.int32, (tq, LANES), 1) < HEAD_DIM
        o_ref[...] = jnp.where(lo, acca / la, accb / lb).astype(BF16)

        @pl.when(hp == 0)
        def _():
            lse_ref[...] = jnp.zeros_like(lse_ref)

        lse_ref[...] += _put_lane(ma + jnp.log(la), 2 * hp) + _put_lane(mb + jnp.log(lb), 2 * hp + 1)

    kv = lambda first: pl.BlockSpec((S, LANES), lambda b, i, hp: (b, first + hp))
    return _pallas(
        body, "attn_fwd", (B, nq, HEAD_PAIRS),
        [pl.BlockSpec((tq, LANES), lambda b, i, hp: (b * nq + i, hp)),
         kv(ATTN_W // LANES), kv(2 * ATTN_W // LANES),
         pl.BlockSpec((tq, LANES), lambda b, i, hp: (b * nq + i, 0)),
         pl.BlockSpec((None, nq, 8, tq), lambda b, i, hp: (b, 0, 0, 0))],
        [pl.BlockSpec((tq, LANES), lambda b, i, hp: (b * nq + i, hp)),
         pl.BlockSpec((tq, LANES), lambda b, i, hp: (b * nq + i, 0))],
        [jax.ShapeDtypeStruct((B * S, ATTN_W), BF16), jax.ShapeDtypeStruct((B * S, LANES), F32)],
        [], (proj, proj, proj, cum, cum_t), comm)


def _attn_bwd(proj, o, do, lse, cum, cum_t, B, S, tq, comm=None):
    nq = S // tq

    def body(q_ref, k_ref, v_ref, o_ref, do_ref, lse_ref, cum_ref, cumt_ref,
             dq_ref, dk_ref, dv_ref, dcq_ref, dck_ref, dq_scr):
        hp, kj = pl.program_id(1), pl.program_id(2)

        @pl.when(kj == 0)
        def _():
            dq_scr[...] = jnp.zeros_like(dq_scr)

        @pl.when((kj == 0) & (hp == 0))
        def _():
            dcq_ref[...] = jnp.zeros_like(dcq_ref)
            dck_ref[...] = jnp.zeros_like(dck_ref)

        kv = k_ref[...]
        vv = v_ref[...]
        km = _head_halves(kv)
        ct = cumt_ref[...]
        ck = [_pick_row(ct, 2 * hp + e) for e in range(2)]

        def tile(i, carry, masked):
            dk, dv, dcol = carry
            off = pl.multiple_of(i * tq, tq)
            qi = q_ref[pl.ds(off, tq), :]
            ov = o_ref[pl.ds(off, tq), :].astype(F32)
            qm = _head_halves(_scaled(qi))
            dom = _head_halves(do_ref[pl.ds(off, tq), :])
            cumv = cum_ref[pl.ds(off, tq), :]
            lsev = lse_ref[pl.ds(off, tq), :]
            dcq = jnp.zeros((tq, LANES), F32)
            dq = jnp.zeros((tq, LANES), F32)
            dcol_new = []
            for e in range(2):
                delta = jnp.sum(dom[e].astype(F32) * ov, axis=1, keepdims=True)
                row_term = _pick_lane(cumv, 2 * hp + e) - _pick_lane(lsev, 2 * hp + e)
                p = jnp.exp(_dot_nt(qm[e], kv) + row_term - ck[e])
                if masked:
                    p = jnp.where(_causal(tq), p, 0.0)
                dv = dv + _dot_tn(p.astype(BF16), dom[e])
                ds = p * (_dot_nt(dom[e], vv) - delta)
                dcol_new.append(dcol[e] + jnp.sum(ds, axis=0, keepdims=True))
                dcq = dcq + _put_lane(jnp.sum(ds, axis=1, keepdims=True), 2 * hp + e)
                dsb = ds.astype(BF16)
                dk = dk + _dot_tn(dsb, qm[e])
                dq = dq + _dot(dsb, km[e]) * ATTN_SCALE
            dq_scr[pl.ds(off, tq), :] += dq
            dcq_ref[pl.ds(off, tq), :] += dcq
            return dk, dv, tuple(dcol_new)

        zero_row = jnp.zeros((1, tq), F32)
        init = (jnp.zeros((tq, LANES), F32), jnp.zeros((tq, LANES), F32), (zero_row, zero_row))
        carry = tile(kj, init, True)
        dk, dv, dcol = lax.fori_loop(kj + 1, nq, lambda i, c: tile(i, c, False), carry)
        dk_ref[...] = dk.astype(BF16)
        dv_ref[...] = dv.astype(BF16)
        dck_ref[kj] += -(_put_row(dcol[0], 2 * hp) + _put_row(dcol[1], 2 * hp + 1))

        @pl.when(kj == nq - 1)
        def _():
            dq_ref[...] = dq_scr[...].astype(BF16)

    seq = lambda first: pl.BlockSpec((S, LANES), lambda b, hp, j: (b, first + hp))
    tile_in = lambda first: pl.BlockSpec((tq, LANES), lambda b, hp, j: (b * nq + j, first + hp))
    lanes0 = pl.BlockSpec((S, LANES), lambda b, hp, j: (b, 0))
    out = jax.ShapeDtypeStruct((B * S, ATTN_W), BF16)
    return _pallas(
        body, "attn_bwd", (B, HEAD_PAIRS, nq),
        [seq(0), tile_in(ATTN_W // LANES), tile_in(2 * ATTN_W // LANES), seq(0), seq(0), lanes0, lanes0,
         pl.BlockSpec((None, None, 8, tq), lambda b, hp, j: (b, j, 0, 0))],
        [seq(0), tile_in(0), tile_in(0), lanes0,
         pl.BlockSpec((None, nq, 8, tq), lambda b, hp, j: (b, 0, 0, 0))],
        [out, out, out, jax.ShapeDtypeStruct((B * S, LANES), F32), jax.ShapeDtypeStruct((B, nq, 8, tq), F32)],
        [pltpu.VMEM((S, LANES), F32)],
        (proj, proj, proj, o, do, lse, cum, cum_t), comm)


def _shift_down(u, n):
    row = lax.broadcasted_iota(jnp.int32, u.shape, 0)
    return jnp.where(row >= n, pltpu.roll(u, n, 0), 0.0)


def _shift_up(u, n):
    rows = u.shape[0]
    row = lax.broadcasted_iota(jnp.int32, u.shape, 0)
    return jnp.where(row < rows - n, pltpu.roll(u, rows - n, 0), 0.0)


def _conv_specs(S):
    cb = pl.BlockSpec((S, LANES), lambda g, b: (b, COL_CB // LANES + g))
    cc = pl.BlockSpec((S, LANES), lambda g, b: (b, COL_CC // LANES + g))
    cx = pl.BlockSpec((S, LANES), lambda g, b: (b, COL_CX // LANES + g))
    w = pl.BlockSpec((8, LANES), lambda g, b: (0, g))
    return cb, cc, cx, w


def _conv_fwd(proj, conv_w, B, S):
    def body(cb_ref, cc_ref, cx_ref, w_ref, y_ref):
        u = cc_ref[...].astype(F32) * cx_ref[...].astype(F32)
        w = w_ref[...]
        conv = w[0:1, :] * _shift_down(u, 2) + w[1:2, :] * _shift_down(u, 1) + w[2:3, :] * u
        y_ref[...] = (cb_ref[...].astype(F32) * conv).astype(BF16)

    cb, cc, cx, w = _conv_specs(S)
    return pl.pallas_call(
        body, name="conv_fwd", grid=(CONV_W // LANES, B),
        in_specs=[cb, cc, cx, w],
        out_specs=pl.BlockSpec((S, LANES), lambda g, b: (b, g)),
        out_shape=jax.ShapeDtypeStruct((B * S, CONV_W), BF16),
        compiler_params=_params(("arbitrary", "arbitrary")),
    )(proj, proj, proj, conv_w)


def _conv_bwd(dy, proj, conv_w, B, S):
    def body(dy_ref, cb_ref, cc_ref, cx_ref, w_ref, dcb_ref, dcc_ref, dcx_ref, dw_ref):
        @pl.when(pl.program_id(1) == 0)
        def _():
            dw_ref[...] = jnp.zeros_like(dw_ref)

        ccv = cc_ref[...].astype(F32)
        cxv = cx_ref[...].astype(F32)
        u = ccv * cxv
        u1 = _shift_down(u, 1)
        u2 = _shift_down(u, 2)
        w = w_ref[...]
        conv = w[0:1, :] * u2 + w[1:2, :] * u1 + w[2:3, :] * u
        dyv = dy_ref[...].astype(F32)
        dcb_ref[...] = (dyv * conv).astype(BF16)
        dconv = dyv * cb_ref[...].astype(F32)
        du = w[2:3, :] * dconv + w[1:2, :] * _shift_up(dconv, 1) + w[0:1, :] * _shift_up(dconv, 2)
        dcc_ref[...] = (du * cxv).astype(BF16)
        dcx_ref[...] = (du * ccv).astype(BF16)
        row = lax.broadcasted_iota(jnp.int32, (8, LANES), 0)
        dw = jnp.where(row == 0, jnp.sum(dconv * u2, axis=0, keepdims=True),
                       jnp.where(row == 1, jnp.sum(dconv * u1, axis=0, keepdims=True),
                                 jnp.where(row == 2, jnp.sum(dconv * u, axis=0, keepdims=True), 0.0)))
        dw_ref[...] += dw

    cb, cc, cx, w = _conv_specs(S)
    out = pl.BlockSpec((S, LANES), lambda g, b: (b, g))
    return pl.pallas_call(
        body, name="conv_bwd", grid=(CONV_W // LANES, B),
        in_specs=[out, cb, cc, cx, w],
        out_specs=[out, out, out, w],
        out_shape=[jax.ShapeDtypeStruct((B * S, CONV_W), BF16)] * 3 + [jax.ShapeDtypeStruct((8, CONV_W), F32)],
        compiler_params=_params(("arbitrary", "arbitrary")),
    )(dy, proj, proj, proj, conv_w)


def _gate_specs(tm, D):
    ga = pl.BlockSpec((tm, D), lambda i: (i, COL_GATES // D))
    gc = pl.BlockSpec((tm, D), lambda i: (i, COL_GATES // D + 1))
    return ga, gc


def _mix_out_fwd(x, o, yc, proj, woa, woc, wout, tm):
    T, D = x.shape

    def body(x_ref, o_ref, yc_ref, ga_ref, gc_ref, woa_ref, woc_ref, wout_ref, out_ref):
        ya = _dot(o_ref[...], woa_ref[...])
        yp = _dot(yc_ref[...], woc_ref[...])
        merged = _sigmoid(ga_ref[...].astype(F32)) * ya + _sigmoid(gc_ref[...].astype(F32)) * yp
        out_ref[...] = x_ref[...] + _dot(merged.astype(BF16), wout_ref[...])

    ga, gc = _gate_specs(tm, D)
    row = lambda w: pl.BlockSpec((tm, w), lambda i: (i, 0))
    whole = lambda a: pl.BlockSpec(a.shape, lambda i: (0, 0))
    return pl.pallas_call(
        body, name="mix_out_fwd", grid=(T // tm,),
        in_specs=[row(D), row(ATTN_W), row(CONV_W), ga, gc, whole(woa), whole(woc), whole(wout)],
        out_specs=row(D),
        out_shape=jax.ShapeDtypeStruct((T, D), F32),
        compiler_params=_params(("arbitrary",)),
    )(x, o, yc, proj, proj, woa, woc, wout)


def _mix_out_bwd(dx, o, yc, proj, woa, woc, wout, tm):
    T, D = dx.shape
    nt = T // tm

    def body(dx_ref, o_ref, yc_ref, ga_ref, gc_ref, woa_ref, woc_ref, wout_ref,
             do_ref, dyc_ref, dg_ref, dwoa_ref, dwoc_ref, dwout_ref, acca, accc, acco):
        t = pl.program_id(0)

        @pl.when(t == 0)
        def _():
            acca[...] = jnp.zeros_like(acca)
            accc[...] = jnp.zeros_like(accc)
            acco[...] = jnp.zeros_like(acco)

        dxb = dx_ref[...].astype(BF16)
        ov, ycv = o_ref[...], yc_ref[...]
        ya = _dot(ov, woa_ref[...])
        yp = _dot(ycv, woc_ref[...])
        sa = _sigmoid(ga_ref[...].astype(F32))
        sc = _sigmoid(gc_ref[...].astype(F32))
        merged = (sa * ya + sc * yp).astype(BF16)
        dm = _dot_nt(dxb, wout_ref[...])
        dya = (dm * sa).astype(BF16)
        dyp = (dm * sc).astype(BF16)
        dg_ref[:, :D] = (dm * ya * sa * (1.0 - sa)).astype(BF16)
        dg_ref[:, D:] = (dm * yp * sc * (1.0 - sc)).astype(BF16)
        do_ref[...] = _dot_nt(dya, woa_ref[...]).astype(BF16)
        dyc_ref[...] = _dot_nt(dyp, woc_ref[...]).astype(BF16)
        acca[...] += _dot_tn(ov, dya)
        accc[...] += _dot_tn(ycv, dyp)
        acco[...] += _dot_tn(merged, dxb)

        @pl.when(t == nt - 1)
        def _():
            dwoa_ref[...] = acca[...].astype(BF16)
            dwoc_ref[...] = accc[...].astype(BF16)
            dwout_ref[...] = acco[...].astype(BF16)

    ga, gc = _gate_specs(tm, D)
    row = lambda w: pl.BlockSpec((tm, w), lambda i: (i, 0))
    whole = lambda a: pl.BlockSpec(a.shape, lambda i: (0, 0))
    return pl.pallas_call(
        body, name="mix_out_bwd", grid=(nt,),
        in_specs=[row(D), row(ATTN_W), row(CONV_W), ga, gc, whole(woa), whole(woc), whole(wout)],
        out_specs=[row(ATTN_W), row(CONV_W), row(2 * D), whole(woa), whole(woc), whole(wout)],
        out_shape=[jax.ShapeDtypeStruct((T, ATTN_W), BF16), jax.ShapeDtypeStruct((T, CONV_W), BF16),
                   jax.ShapeDtypeStruct((T, 2 * D), BF16),
                   jax.ShapeDtypeStruct(woa.shape, BF16), jax.ShapeDtypeStruct(woc.shape, BF16),
                   jax.ShapeDtypeStruct(wout.shape, BF16)],
        scratch_shapes=[pltpu.VMEM(woa.shape, F32), pltpu.VMEM(woc.shape, F32), pltpu.VMEM(wout.shape, F32)],
        compiler_params=_params(("arbitrary",)),
    )(dx, o, yc, proj, proj, woa, woc, wout)


def _proj_pieces(dq, dk, dv, dcb, dcc, dcx, dgates, dflog):
    D = dgates.shape[1] // 2
    return [(dq, ATTN_W, 0), (dk, ATTN_W, 0), (dv, ATTN_W, 0), (dcb, CONV_W, 0), (dcc, CONV_W, 0), (dcx, CONV_W, 0),
            (dgates, D, 0), (dgates, D, 1), (dflog, LANES, 0)]


def _mix_proj_bwd_dx(dres, x, g, pieces, wproj_t, wf_t, tm):
    T, D = x.shape
    n = len(pieces)
    w_blocks = [(ATTN_W, 0), (ATTN_W, 1), (ATTN_W, 2), (CONV_W, 3), (CONV_W, 4), (CONV_W, 5),
                (D, COL_GATES // D), (D, COL_GATES // D + 1)]

    def body(*refs):
        dres_ref, x_ref, g_ref = refs[:3]
        p_refs, w_refs = refs[3:3 + n], refs[3 + n:3 + 2 * n]
        dx_ref, dg_ref = refs[3 + 2 * n:]

        @pl.when(pl.program_id(0) == 0)
        def _():
            dg_ref[...] = jnp.zeros_like(dg_ref)

        dh = _dot(p_refs[0][...].astype(BF16), w_refs[0][...])
        for p_ref, w_ref in zip(p_refs[1:], w_refs[1:]):
            dh = dh + _dot(p_ref[...].astype(BF16), w_ref[...])
        xhat, inv = _rms(x_ref[...])
        dx, dg = _rms_bwd(dh, xhat, inv, g_ref[...])
        dx_ref[...] = dres_ref[...] + dx
        dg_ref[...] += dg

    row = pl.BlockSpec((tm, D), lambda i: (i, 0))
    vec = pl.BlockSpec((1, D), lambda i: (0, 0))
    p_specs = [pl.BlockSpec((tm, w), lambda i, cb=cb: (i, cb)) for _, w, cb in pieces]
    w_specs = [pl.BlockSpec((r, D), lambda i, rb=rb: (rb, 0)) for r, rb in w_blocks]
    w_specs.append(pl.BlockSpec((LANES, D), lambda i: (0, 0)))
    return pl.pallas_call(
        body, name="mix_proj_bwd_dx", grid=(T // tm,),
        in_specs=[row, row, vec] + p_specs + w_specs,
        out_specs=[row, vec],
        out_shape=[jax.ShapeDtypeStruct((T, D), F32), jax.ShapeDtypeStruct((1, D), F32)],
        compiler_params=_params(("arbitrary",)),
    )(dres, x, g, *[p for p, _, _ in pieces], *([wproj_t] * len(w_blocks)), wf_t)


def _matmuls_tn(name, pieces, b, tk):
    T, N = b.shape
    nt = T // tk
    n = len(pieces)

    def body(*refs):
        a_refs, b_ref, out_refs, accs = refs[:n], refs[n], refs[n + 1:2 * n + 1], refs[2 * n + 1:]
        t = pl.program_id(0)

        @pl.when(t == 0)
        def _():
            for acc in accs:
                acc[...] = jnp.zeros_like(acc)

        bv = b_ref[...]
        for a_ref, acc in zip(a_refs, accs):
            acc[...] += _dot_tn(a_ref[...].astype(BF16), bv)

        @pl.when(t == nt - 1)
        def _():
            for out_ref, acc in zip(out_refs, accs):
                out_ref[...] = acc[...].astype(BF16)

    return pl.pallas_call(
        body, name=name, grid=(nt,),
        in_specs=[pl.BlockSpec((tk, w), lambda t, cb=cb: (t, cb)) for _, w, cb in pieces]
        + [pl.BlockSpec((tk, N), lambda t: (t, 0))],
        out_specs=[pl.BlockSpec((w, N), lambda t: (0, 0)) for _, w, _ in pieces],
        out_shape=[jax.ShapeDtypeStruct((w, N), BF16) for _, w, _ in pieces],
        scratch_shapes=[pltpu.VMEM((w, N), F32) for _, w, _ in pieces],
        compiler_params=_params(("arbitrary",)),
    )(*[a for a, _, _ in pieces], b)


def _final_loss(x, target, g, tm):
    T, D = x.shape

    def body(x_ref, t_ref, g_ref, dx_ref, loss_ref, dg_ref):
        @pl.when(pl.program_id(0) == 0)
        def _():
            loss_ref[...] = jnp.zeros_like(loss_ref)
            dg_ref[...] = jnp.zeros_like(dg_ref)

        xhat, inv = _rms(x_ref[...])
        err = xhat * g_ref[...] - t_ref[...]
        loss_ref[...] += 0.5 * jnp.sum(jnp.sum(err * err, axis=1, keepdims=True), axis=0, keepdims=True) / D
        dx, dg = _rms_bwd(err * (1.0 / D), xhat, inv, g_ref[...])
        dx_ref[...] = dx
        dg_ref[...] += dg

    row = pl.BlockSpec((tm, D), lambda i: (i, 0))
    return pl.pallas_call(
        body, name="final_loss", grid=(T // tm,),
        in_specs=[row, row, pl.BlockSpec((1, D), lambda i: (0, 0))],
        out_specs=[row, pl.BlockSpec((1, LANES), lambda i: (0, 0)), pl.BlockSpec((1, D), lambda i: (0, 0))],
        out_shape=[jax.ShapeDtypeStruct((T, D), F32), jax.ShapeDtypeStruct((1, LANES), F32),
                   jax.ShapeDtypeStruct((1, D), F32)],
        compiler_params=_params(("arbitrary",)),
    )(x, target, g)


class _LocalPlan:
    def __init__(self, stacks, small):
        self.stacks, self.small, self.grads = stacks, small, {}

    def weights(self, group):
        return _LAYOUTS[group](self.stacks, self.small)

    def rider(self, kernel_name):
        return None

    def arrived(self, kernel_name, results):
        pass

    def reduce(self, group, grads):
        self.grads.update(grads)


def _local_step(x, target, plan, B, S):
    T, D = x.shape
    tm = min(512, T)
    tq = min(512, S)
    nq = S // tq
    ch = min(256, S)

    def riding(kernel_name, build):
        results, brought = build(plan.rider(kernel_name))
        plan.arrived(kernel_name, brought)
        return results

    w1 = plan.weights("ffn1")
    x1, hg1, hu1 = riding("ffn1_fwd", lambda comm: _ffn_fwd(
        "ffn1_fwd", x, w1["ffn1_norm"], w1["ffn1_gate"], w1["ffn1_up"], w1["ffn1_down"], tm, comm))
    wm = plan.weights("mix")
    h, proj, flog = _mix_proj_fwd(x1, wm["mix_norm"], wm["w_proj"], wm["w_f"], tm, 1280)
    cum = _fgate_fwd(flog, wm["b_forget"], B, S, ch)
    cum_t = jnp.transpose(cum[:, :N_HEADS].reshape(B, nq, tq, N_HEADS), (0, 1, 3, 2))
    o, lse = riding("attn_fwd", lambda comm: _attn_fwd(proj, cum, cum_t, B, S, tq, comm))
    yc = _conv_fwd(proj, wm["conv_w"], B, S)
    x2 = _mix_out_fwd(x1, o, yc, proj, wm["w_o_attn"], wm["w_o_conv"], wm["w_out"], tm)
    w2 = plan.weights("ffn2")
    x3, hg2, hu2 = _ffn_fwd("ffn2_fwd", x2, w2["ffn2_norm"], w2["ffn2_gate"], w2["ffn2_up"], w2["ffn2_down"], tm)[0]
    dx3, loss, d_final_norm = _final_loss(x3, target, w2["final_norm"], tm)

    g = {"final_norm": d_final_norm}
    dx2, dhg2, dhu2, g["ffn2_norm"] = _ffn_bwd_dx("ffn2_bwd_dx", dx3, x2, w2["ffn2_norm"], hg2, hu2,
                                                  w2["ffn2_gate"], w2["ffn2_up"], w2["ffn2_down"], tm)[0]
    late = dict(zip(("ffn2_gate", "ffn2_up", "ffn2_down"),
                    _ffn_bwd_dw("ffn2_bwd_dw", dx3, x2, w2["ffn2_norm"], hg2, hu2, dhg2, dhu2, tm)))
    do, dyc, dgates, dwoa, dwoc, dwout = _mix_out_bwd(
        dx2, o, yc, proj, wm["w_o_attn"], wm["w_o_conv"], wm["w_out"], tm)
    late.update(w_o_attn=_shard_cols(dwoa), w_o_conv=_shard_cols(dwoc), w_out=dwout.reshape(N_CHIPS, -1, D))
    plan.reduce("late", late)
    dq, dk, dv, dcq, dck = riding("attn_bwd", lambda comm: _attn_bwd(proj, o, do, lse, cum, cum_t, B, S, tq, comm))
    dcum = dcq + jnp.pad(jnp.transpose(dck, (0, 1, 3, 2)).reshape(T, N_HEADS), ((0, 0), (0, LANES - N_HEADS)))
    dflog, g["b_forget"] = _fgate_bwd(dcum, flog, wm["b_forget"], B, S, ch)
    dcb, dcc, dcx, g["conv_w"] = _conv_bwd(dyc, proj, wm["conv_w"], B, S)
    pieces = _proj_pieces(dq, dk, dv, dcb, dcc, dcx, dgates, dflog)
    dx1, g["mix_norm"] = _mix_proj_bwd_dx(dx2, x1, wm["mix_norm"], pieces, wm["w_proj"], wm["w_f"], min(256, T))
    dwq, dwk, dwv, dwcb, dwcc, dwcx = _matmuls_tn("mix_dw_a", pieces[:6], h, tm)
    dwga, dwgc, dwf = _matmuls_tn("mix_dw_b", pieces[6:], h, tm)
    dwin_t = jnp.concatenate([dwq, dwk, dwv, dwf[:N_HEADS], dwcb, dwcc, dwcx, dwga, dwgc], axis=0)
    plan.reduce("w_in", {"w_in": dwin_t.reshape(N_CHIPS, -1, D)})
    grad_x, dhg1, dhu1, g["ffn1_norm"] = riding("ffn1_bwd_dx", lambda comm: _ffn_bwd_dx(
        "ffn1_bwd_dx", dx1, x, w1["ffn1_norm"], hg1, hu1, w1["ffn1_gate"], w1["ffn1_up"], w1["ffn1_down"], tm, comm))
    plan.reduce("ffn1", dict(zip(("ffn1_gate", "ffn1_up", "ffn1_down"),
                                 _ffn_bwd_dw("ffn1_bwd_dw", dx1, x, w1["ffn1_norm"], hg1, hu1, dhg1, dhu1, tm))))
    return loss, grad_x, g


TRANSPOSED = ("ffn1_gate", "ffn1_up", "ffn2_gate", "ffn2_up", "w_in")
NORMS = ("ffn1_norm", "mix_norm", "ffn2_norm", "final_norm")


def _unshard_cols(a):
    return jnp.transpose(a, (1, 0, 2)).reshape(a.shape[1], N_CHIPS * a.shape[2])


def _shard_cols(a):
    return jnp.transpose(a.reshape(a.shape[0], N_CHIPS, a.shape[1] // N_CHIPS), (1, 0, 2))


def _layout_ffn(which):
    def layout(st, small):
        w = {n: st[n] for n in (which + "_gate", which + "_up", which + "_down")}
        w[which + "_norm"] = small[which + "_norm"].reshape(1, -1)
        if which == "ffn2":
            w["final_norm"] = small["final_norm"].reshape(1, -1)
        return w
    return layout


def _layout_mix(st, small):
    win_t = st["w_in"].reshape(-1, st["w_in"].shape[2])
    return {
        "w_proj": jnp.concatenate([win_t[:N_FORGET_COL], win_t[N_FORGET_COL + N_HEADS:]], axis=0),
        "w_f": jnp.pad(win_t[N_FORGET_COL:N_FORGET_COL + N_HEADS], ((0, LANES - N_HEADS), (0, 0))),
        "w_o_attn": _unshard_cols(st["w_o_attn"]),
        "w_o_conv": _unshard_cols(st["w_o_conv"]),
        "w_out": st["w_out"].reshape(-1, st["w_out"].shape[2]),
        "conv_w": _unshard_cols(st["conv_w"]),
        "mix_norm": small["mix_norm"].reshape(1, -1),
        "b_forget": jnp.pad(small["b_forget"].reshape(1, -1), ((0, 0), (0, LANES - N_HEADS))),
    }


_LAYOUTS = {"ffn1": _layout_ffn("ffn1"), "mix": _layout_mix, "ffn2": _layout_ffn("ffn2")}


ANY = pl.BlockSpec(memory_space=pl.ANY)
BIG = ("ffn1_gate", "ffn1_up", "ffn1_down", "w_in", "w_o_attn", "w_o_conv", "w_out",
       "ffn2_gate", "ffn2_up", "ffn2_down")


def _place():
    x, y, c = lax.axis_index("x"), lax.axis_index("y"), lax.axis_index("c")
    others = [(1 - x, y), (x, 1 - y), (1 - x, 1 - y)]
    return x, y, c, others


def _col_halves(cols, c):
    hc = cols // 2
    return pl.ds(pl.multiple_of(c * hc, LANES), hc), pl.ds(pl.multiple_of((1 - c) * hc, LANES), hc)


def _gather_comm(shards, conv_shard=None):
    n = len(shards)
    inputs = list(shards) + ([] if conv_shard is None else [conv_shard])

    def copies(ins, outs, sems):
        send_sems, recv_sems, pass_send, pass_recv = sems[:4]
        x, y, c, others = _place()

        def chip_copy(a, j, chip):
            mine, _ = _col_halves(ins[a].shape[1], c)
            return pltpu.make_async_remote_copy(
                src_ref=ins[a].at[:, mine], dst_ref=outs[a].at[chip, :, mine],
                send_sem=send_sems.at[3 * a + j], recv_sem=recv_sems.at[3 * a + j],
                device_id=(*others[j], c), device_id_type=MESH)

        def pass_copy(a, j, chip, half):
            return pltpu.make_async_remote_copy(
                src_ref=outs[a].at[chip, :, half], dst_ref=outs[a].at[chip, :, half],
                send_sem=pass_send.at[3 * a + j], recv_sem=pass_recv.at[3 * a + j],
                device_id=(x, y, 1 - c), device_id_type=MESH)

        def conv_copy(j, chip):
            return pltpu.make_async_remote_copy(
                src_ref=ins[n], dst_ref=outs[n].at[chip],
                send_sem=sems[4].at[j], recv_sem=sems[5].at[j],
                device_id=(*others[j], c), device_id_type=MESH)

        me = 2 * x + y
        sends = [chip_copy(a, j, me) for a in range(n) for j in range(3)]
        if conv_shard is not None:
            sends += [conv_copy(j, me) for j in range(3)]
        return c, others, sends, chip_copy, pass_copy, conv_copy

    def start(ins, outs, sems):
        for cp in copies(ins, outs, sems)[2]:
            cp.start()

    def finish(ins, outs, sems):
        c, others, sends, chip_copy, pass_copy, conv_copy = copies(ins, outs, sems)
        passed = []
        for a in range(n):
            mine, _ = _col_halves(ins[a].shape[1], c)
            for j, (ox, oy) in enumerate(others):
                chip_copy(a, j, 2 * ox + oy).wait_recv()
                passed.append(pass_copy(a, j, 2 * ox + oy, mine))
                passed[-1].start()
        for a in range(n):
            _, theirs = _col_halves(ins[a].shape[1], c)
            for j, (ox, oy) in enumerate(others):
                pass_copy(a, j, 2 * ox + oy, theirs).wait_recv()
        if conv_shard is not None:
            for j, (ox, oy) in enumerate(others):
                conv_copy(j, 2 * ox + oy).wait_recv()
        for cp in sends + passed:
            cp.wait_send()

    scratch = [pltpu.SemaphoreType.DMA((3 * n,))] * 4
    if conv_shard is not None:
        scratch += [pltpu.SemaphoreType.DMA((3,))] * 2
    return _Comm(inputs, [jax.ShapeDtypeStruct((N_CHIPS,) + s.shape, s.dtype) for s in inputs], scratch, start, finish)


def _fill_own(stacks, shards):
    chip = 2 * lax.axis_index("x") + lax.axis_index("y")
    return [lax.dynamic_update_index_in_dim(st, s, chip, 0) for st, s in zip(stacks, shards)]


def _run_comm(name, comm):
    ci, co = len(comm.inputs), len(comm.out_shape)

    def body(*refs):
        comm.start(refs[:ci], refs[ci:ci + co], refs[ci + co:])
        comm.finish(refs[:ci], refs[ci:ci + co], refs[ci + co:])

    return pl.pallas_call(body, name=name, in_specs=[ANY] * ci, out_specs=[ANY] * co, out_shape=comm.out_shape,
                          scratch_shapes=comm.scratch)(*comm.inputs)


def _sibling_exchange(name, grads):
    n = len(grads)

    def body(*refs):
        srcs, dsts = refs[:n], refs[n:2 * n]
        send_sems, recv_sems = refs[2 * n:]
        x, y, c, _ = _place()
        copies = []
        for a in range(n):
            _, theirs = _col_halves(srcs[a].shape[2], c)
            copies.append(pltpu.make_async_remote_copy(
                src_ref=srcs[a].at[:, :, theirs], dst_ref=dsts[a],
                send_sem=send_sems.at[a], recv_sem=recv_sems.at[a],
                device_id=(x, y, 1 - c), device_id_type=MESH))
        for cp in copies:
            cp.start()
        for cp in copies:
            cp.wait()

    half = lambda s: jax.ShapeDtypeStruct((s.shape[0], s.shape[1], s.shape[2] // 2), s.dtype)
    return pl.pallas_call(
        body, name=name,
        in_specs=[ANY] * n, out_specs=[ANY] * n, out_shape=[half(s) for s in grads],
        scratch_shapes=[pltpu.SemaphoreType.DMA((n,)), pltpu.SemaphoreType.DMA((n,))],
    )(*grads)


def _add_halves(name, grad, recv, core):
    K, r, cols = grad.shape
    hc = cols // 2

    def body(core_ref, g_ref, r_ref, out_ref):
        out_ref[...] = (g_ref[...].astype(F32) + r_ref[...].astype(F32)).astype(BF16)

    return pl.pallas_call(
        body, name=name,
        grid_spec=pltpu.PrefetchScalarGridSpec(
            num_scalar_prefetch=1, grid=(K,),
            in_specs=[pl.BlockSpec((None, r, hc), lambda k, core_ref: (k, 0, core_ref[0])),
                      pl.BlockSpec((None, r, hc), lambda k, core_ref: (k, 0, 0))],
            out_specs=pl.BlockSpec((None, r, hc), lambda k, core_ref: (k, 0, 0))),
        out_shape=jax.ShapeDtypeStruct((K, r, hc), BF16),
        compiler_params=_params(("arbitrary",)),
    )(core, grad, recv)


def _chip_exchange_comm(parts):
    n = len(parts)

    def copies(ins, outs, sems):
        x, y, c, others = _place()
        return [pltpu.make_async_remote_copy(
            src_ref=ins[a].at[2 * ox + oy], dst_ref=outs[a].at[j],
            send_sem=sems[0].at[3 * a + j], recv_sem=sems[1].at[3 * a + j],
            device_id=(ox, oy, c), device_id_type=MESH) for a in range(n) for j, (ox, oy) in enumerate(others)]

    def start(ins, outs, sems):
        for cp in copies(ins, outs, sems):
            cp.start()

    def finish(ins, outs, sems):
        for cp in copies(ins, outs, sems):
            cp.wait()

    return _Comm(parts, [jax.ShapeDtypeStruct((3,) + s.shape[1:], s.dtype) for s in parts],
                 [pltpu.SemaphoreType.DMA((3 * n,))] * 2, start, finish)


HBM = pl.BlockSpec(memory_space=pltpu.HBM)
SEM = pl.BlockSpec(memory_space=pltpu.SEMAPHORE)


def _split_exchange_copies(parts, lands, send_sems, recv_sems):
    x, y, c, others = _place()
    return [pltpu.make_async_remote_copy(
        src_ref=parts[a].at[2 * ox + oy], dst_ref=lands[a].at[j],
        send_sem=send_sems.at[3 * a + j], recv_sem=recv_sems.at[3 * a + j],
        device_id=(ox, oy, c), device_id_type=MESH) for a in range(len(parts)) for j, (ox, oy) in enumerate(others)]


def _exchange_start(name, parts):
    n = len(parts)

    def body(*refs):
        ins, lands = refs[:n], refs[n:2 * n]
        send_sems, recv_sems, token = refs[2 * n], refs[2 * n + 1], refs[-1]
        for cp in _split_exchange_copies(ins, lands, send_sems, recv_sems):
            cp.start()
        token[...] = jnp.zeros_like(token)

    land_shape = [(3,) + p.shape[1:] for p in parts]
    outs = pl.pallas_call(
        body, name=name,
        out_shape=[pltpu.SemaphoreType.DMA((3 * n,)), pltpu.SemaphoreType.DMA((3 * n,))]
        + [pltpu.HBM(p.shape, p.dtype) for p in parts] + [pltpu.HBM(s, p.dtype) for s, p in zip(land_shape, parts)]
        + [jax.ShapeDtypeStruct((8, LANES), F32)],
        in_specs=[HBM] * (2 * n), out_specs=[SEM, SEM] + [HBM] * (2 * n) + [pl.BlockSpec(memory_space=pltpu.VMEM)],
        input_output_aliases={i: 2 + i for i in range(2 * n)},
        compiler_params=pltpu.CompilerParams(has_side_effects=pltpu.SideEffectType.DATAFLOW_SIDE_EFFECTING),
    )(*[pltpu.with_memory_space_constraint(p, pltpu.HBM) for p in parts],
      *[pltpu.with_memory_space_constraint(lax.empty(s, p.dtype), pltpu.HBM) for s, p in zip(land_shape, parts)])
    return outs[0], outs[1], list(outs[2:2 + n]), list(outs[2 + n:2 + 2 * n]), outs[-1]


def _exchange_wait(name, send_sems, recv_sems, parts, lands, after):
    n = len(parts)

    def body(*refs):
        ins, zones = refs[:n], refs[n:2 * n]
        for cp in _split_exchange_copies(ins, zones, refs[2 * n], refs[2 * n + 1]):
            cp.wait_send()
            cp.wait_recv()

    outs = pl.pallas_call(
        body, name=name,
        out_shape=[pltpu.HBM(p.shape, p.dtype) for p in parts] + [pltpu.HBM(z.shape, z.dtype) for z in lands],
        in_specs=[HBM] * (2 * n) + [SEM, SEM] + [ANY] * len(after), out_specs=[HBM] * (2 * n),
        input_output_aliases={i: i for i in range(2 * n)},
        compiler_params=pltpu.CompilerParams(has_side_effects=pltpu.SideEffectType.DATAFLOW_SIDE_EFFECTING),
    )(*parts, *lands, send_sems, recv_sems, *after)
    return list(outs[n:])


def _sum_chips(name, own, recv, chip, after):
    _, r, hc = own.shape

    def body(chip_ref, own_ref, recv_ref, after_ref, out_ref):
        acc = own_ref[...].astype(F32)
        for j in range(3):
            acc = acc + recv_ref[j].astype(F32)
        out_ref[...] = acc

    return pl.pallas_call(
        body, name=name,
        grid_spec=pltpu.PrefetchScalarGridSpec(
            num_scalar_prefetch=1, grid=(hc // LANES,),
            in_specs=[pl.BlockSpec((None, r, LANES), lambda i, chip_ref: (chip_ref[0], 0, i)),
                      pl.BlockSpec((3, r, LANES), lambda i, chip_ref: (0, 0, i)),
                      pl.BlockSpec((8, LANES), lambda i, chip_ref: (0, 0))],
            out_specs=pl.BlockSpec((r, LANES), lambda i, chip_ref: (0, i))),
        out_shape=jax.ShapeDtypeStruct((r, hc), F32),
        compiler_params=_params(("arbitrary",)),
    )(chip, own, recv, after)


def _share_halves(name, halves):
    n = len(halves)

    def body(*refs):
        srcs, dsts = refs[:n], refs[n:2 * n]
        send_sems, recv_sems = refs[2 * n:]
        x, y, c, _ = _place()
        copies = [pltpu.make_async_remote_copy(
            src_ref=srcs[a], dst_ref=dsts[a], send_sem=send_sems.at[a], recv_sem=recv_sems.at[a],
            device_id=(x, y, 1 - c), device_id_type=MESH) for a in range(n)]
        for cp in copies:
            cp.start()
        for cp in copies:
            cp.wait()

    return pl.pallas_call(
        body, name=name,
        in_specs=[ANY] * n, out_specs=[ANY] * n,
        out_shape=[jax.ShapeDtypeStruct(s.shape, s.dtype) for s in halves],
        scratch_shapes=[pltpu.SemaphoreType.DMA((n,)), pltpu.SemaphoreType.DMA((n,))],
    )(*halves)


def _allreduce_small(part):
    rows = part.shape[0]

    def body(in_ref, out_ref, land, send_sems, recv_sems):
        x, y, c, _ = _place()
        me = 4 * x + 2 * y + c
        land[me] = in_ref[...]
        copies = []
        for d in range(1, N_DEV):
            peer = (1 - x if d & 4 else x, 1 - y if d & 2 else y, 1 - c if d & 1 else c)
            copies.append(pltpu.make_async_remote_copy(
                src_ref=in_ref, dst_ref=land.at[me],
                send_sem=send_sems.at[d - 1], recv_sem=recv_sems.at[d - 1],
                device_id=peer, device_id_type=MESH))
        for cp in copies:
            cp.start()
        for d in range(1, N_DEV):
            px, py, pc = (1 - x if d & 4 else x, 1 - y if d & 2 else y, 1 - c if d & 1 else c)
            pltpu.make_async_remote_copy(
                src_ref=in_ref, dst_ref=land.at[4 * px + 2 * py + pc],
                send_sem=send_sems.at[d - 1], recv_sem=recv_sems.at[d - 1],
                device_id=(px, py, pc), device_id_type=MESH).wait_recv()
        for cp in copies:
            cp.wait_send()
        acc = land[0]
        for k in range(1, N_DEV):
            acc = acc + land[k]
        out_ref[...] = acc

    vmem = pl.BlockSpec(memory_space=pltpu.VMEM)
    return pl.pallas_call(
        body, name="allreduce_small",
        in_specs=[vmem], out_specs=vmem,
        out_shape=jax.ShapeDtypeStruct(part.shape, F32),
        scratch_shapes=[pltpu.VMEM((N_DEV, rows, LANES), F32),
                        pltpu.SemaphoreType.DMA((N_DEV - 1,)), pltpu.SemaphoreType.DMA((N_DEV - 1,))],
    )(part)


def _adam_update(w, g, m, v):
    nm = ADAM_B1 * m + (1.0 - ADAM_B1) * g
    nv = ADAM_B2 * v + (1.0 - ADAM_B2) * (g * g)
    m_hat = nm * (1.0 / (1.0 - ADAM_B1 ** ADAM_STEP))
    v_hat = nv * (1.0 / (1.0 - ADAM_B2 ** ADAM_STEP))
    return -ADAM_LR * (m_hat / (jnp.sqrt(v_hat) + ADAM_EPS) + ADAM_WD * w), nm, nv


def _adamw(name, w, g, m, v):
    def body(w_ref, g_ref, m_ref, v_ref, d_ref, nm_ref, nv_ref):
        d_ref[...], nm_ref[...], nv_ref[...] = _adam_update(w_ref[...], g_ref[...], m_ref[...], v_ref[...])

    spec = pl.BlockSpec(w.shape, lambda i: (0, 0))
    out = jax.ShapeDtypeStruct(w.shape, F32)
    return pl.pallas_call(
        body, name=name, grid=(1,),
        in_specs=[spec] * 4, out_specs=[spec] * 3, out_shape=[out] * 3,
        compiler_params=_params(("arbitrary",)),
    )(w, g, m, v)


def _adamw_halves(name, w, mine, theirs, m, v, core):
    rows, cols = w.shape
    hc = cols // 2
    tc = min(256, hc)
    nt = hc // tc

    def body(core_ref, w_ref, mine_ref, theirs_ref, m_ref, v_ref, g_ref, d_ref, nm_ref, nv_ref):
        gv = jnp.where(pl.program_id(0) == core_ref[0], mine_ref[...], theirs_ref[...])
        g_ref[...] = gv
        d_ref[...], nm_ref[...], nv_ref[...] = _adam_update(w_ref[...], gv, m_ref[...], v_ref[...])

    whole = pl.BlockSpec((rows, tc), lambda h, i, core_ref: (0, h * nt + i))
    mine_spec = pl.BlockSpec((rows, tc), lambda h, i, core_ref: (0, jnp.where(h == core_ref[0], i, 0)))
    theirs_spec = pl.BlockSpec((rows, tc), lambda h, i, core_ref: (0, jnp.where(h == core_ref[0], 0, i)))
    out = jax.ShapeDtypeStruct((rows, cols), F32)
    return pl.pallas_call(
        body, name=name,
        grid_spec=pltpu.PrefetchScalarGridSpec(
            num_scalar_prefetch=1, grid=(2, nt),
            in_specs=[whole, mine_spec, theirs_spec, whole, whole], out_specs=[whole] * 4),
        out_shape=[out] * 4,
        compiler_params=_params(("arbitrary", "arbitrary")),
    )(core, w, mine, theirs, m, v)


WEIGHTS = ("ffn1_norm", "ffn1_gate", "ffn1_up", "ffn1_down", "mix_norm", "w_in", "b_forget", "conv_w",
           "w_o_attn", "w_o_conv", "w_out", "ffn2_norm", "ffn2_gate", "ffn2_up", "ffn2_down", "final_norm")
VEC_ROWS = 8


def _pack_small(t, conv_rows):
    conv = t["conv_w"]
    parts = [t[n].reshape(VEC_ROWS, LANES) for n in NORMS]
    parts.append(jnp.pad(conv, ((0, conv_rows - conv.shape[0]), (0, 0))))
    parts.append(jnp.pad(t["b_forget"].reshape(1, N_HEADS), ((0, 7), (0, LANES - N_HEADS))))
    return jnp.concatenate(parts, axis=0)


def _unpack_small(p, conv_rows):
    out = {n: p[VEC_ROWS * i:VEC_ROWS * (i + 1)].reshape(-1) for i, n in enumerate(NORMS)}
    base = VEC_ROWS * len(NORMS)
    out["conv_w"] = p[base:base + 3]
    out["b_forget"] = p[base + conv_rows, :N_HEADS]
    return out


def _travel(name, a):
    return a.T if name in TRANSPOSED else a


GATHER_RIDES = {"ffn1_fwd": ("w_in", "w_o_attn", "w_o_conv", "w_out"), "attn_fwd": ("ffn2_gate", "ffn2_up", "ffn2_down")}
REDUCE_RIDES = {"late": "attn_bwd", "w_in": "ffn1_bwd_dx", "ffn1": None}


class _MeshPlan(_LocalPlan):
    def __init__(self, wts, core):
        self.small, self.core = wts, core
        self.shards = {n: wts[n].astype(BF16) for n in BIG}
        self.chip_part, self.from_chips, self.pending = {}, {}, {}
        first = ("ffn1_gate", "ffn1_up", "ffn1_down")
        conv_shard = jnp.pad(wts["conv_w"], ((0, 8 - wts["conv_w"].shape[0]), (0, 0)))
        own = [self.shards[n] for n in first] + [conv_shard]
        got = _run_comm("gather_ffn1", _gather_comm(own[:-1], conv_shard))
        self.stacks = dict(zip(first + ("conv_w",), _fill_own(got, own)))

    def rider(self, kernel_name):
        if kernel_name in GATHER_RIDES:
            return _gather_comm([self.shards[n] for n in GATHER_RIDES[kernel_name]])
        if kernel_name in self.pending:
            return _chip_exchange_comm([self.chip_part[n] for n in self.pending[kernel_name]])
        return None

    def arrived(self, kernel_name, results):
        if kernel_name in GATHER_RIDES:
            names = GATHER_RIDES[kernel_name]
            self.stacks.update(zip(names, _fill_own(results, [self.shards[n] for n in names])))
        elif kernel_name in self.pending:
            self.from_chips.update(zip(self.pending[kernel_name], results))

    def reduce(self, group, grads):
        names = tuple(grads)
        from_sibling = _sibling_exchange("sibling_exchange_" + group, [grads[n] for n in names])
        self.chip_part.update({n: _add_halves("add_halves_" + n, grads[n], r, self.core)
                               for n, r in zip(names, from_sibling)})
        if REDUCE_RIDES[group] is None:
            self.last = (names, _exchange_start("exchange_start_" + group, [self.chip_part[n] for n in names]))
        else:
            self.pending[REDUCE_RIDES[group]] = names


def kernel(x, ffn1_norm, ffn1_gate, ffn1_up, ffn1_down, mix_norm, w_in, b_forget, conv_w, w_o_attn, w_o_conv, w_out, ffn2_norm, ffn2_gate, ffn2_up, ffn2_down, final_norm, loss_target, m_ffn1_norm, m_ffn1_gate, m_ffn1_up, m_ffn1_down, m_mix_norm, m_w_in, m_b_forget, m_conv_w, m_w_o_attn, m_w_o_conv, m_w_out, m_ffn2_norm, m_ffn2_gate, m_ffn2_up, m_ffn2_down, m_final_norm, v_ffn1_norm, v_ffn1_gate, v_ffn1_up, v_ffn1_down, v_mix_norm, v_w_in, v_b_forget, v_conv_w, v_w_o_attn, v_w_o_conv, v_w_out, v_ffn2_norm, v_ffn2_gate, v_ffn2_up, v_ffn2_down, v_final_norm):
    given = dict(locals())
    wts = {n: _travel(n, given[n]) for n in WEIGHTS}
    mom = {n: _travel(n, given["m_" + n]) for n in WEIGHTS}
    var = {n: _travel(n, given["v_" + n]) for n in WEIGHTS}
    B, S, D = x.shape
    chip = 2 * lax.axis_index("x") + lax.axis_index("y")
    chip1 = chip.astype(jnp.int32).reshape(1)
    core = lax.axis_index("c").astype(jnp.int32).reshape(1)

    plan = _MeshPlan(wts, core)
    loss, grad_x, gs = _local_step(x.reshape(B * S, D), loss_target.reshape(B * S, D), plan, B, S)

    last_names, (send_sems, recv_sems, parts_thru, lands, token) = plan.last
    delta, new_m, new_v, grads = {}, {}, {}, {}

    def finish(tag, names):
        mine = [_sum_chips("sum_chips_" + n, plan.chip_part[n], plan.from_chips[n], chip1, token) for n in names]
        theirs = _share_halves("share_halves_" + tag, mine)
        raw = []
        for n, gm, gt in zip(names, mine, theirs):
            outs = _adamw_halves("adamw_" + n, wts[n], gm, gt, mom[n], var[n], core)
            raw.append(outs[-1])
            grads[n], delta[n], new_m[n], new_v[n] = [_travel(n, o) for o in outs]
        return raw

    done = finish("early", [n for n in BIG if n not in last_names])
    plan.from_chips.update(zip(last_names, _exchange_wait("exchange_wait", send_sems, recv_sems, parts_thru, lands, done)))
    finish("last", last_names)

    conv_all = _shard_cols(gs["conv_w"]).reshape(N_CHIPS * 8, LANES)
    small_part = _pack_small({**{n: gs[n] for n in NORMS}, "conv_w": conv_all, "b_forget": gs["b_forget"][0, :N_HEADS]},
                             N_CHIPS * 8)
    base = VEC_ROWS * len(NORMS)
    small_sum = _allreduce_small(small_part)
    grads.update(_unpack_small(small_sum, N_CHIPS * 8))
    grads["conv_w"] = lax.dynamic_slice_in_dim(small_sum[base:base + N_CHIPS * 8], chip * 8, 8, axis=0)[:3]
    packs = [_pack_small(t, 8) for t in (wts, grads, mom, var)]
    for out, p in zip((delta, new_m, new_v), _adamw("adamw_small", *packs)):
        out.update(_unpack_small(p, 8))

    total = lax.psum(loss[0, 0], ("x", "y", "c"))
    return (total, grad_x.reshape(B, S, D), *[grads[n] for n in WEIGHTS], *[delta[n] for n in WEIGHTS],
            *[new_m[n] for n in WEIGHTS], *[new_v[n] for n in WEIGHTS])
```

```python
import functools
import math

import jax
import jax.numpy as jnp
from jax import lax
from jax.experimental import pallas as pl
from jax.experimental.pallas import tpu as pltpu

F32 = jnp.float32
BF16 = jnp.bfloat16
MESH = pl.DeviceIdType.MESH

N_CHIPS = 4
N_DEV = 8
N_HEADS = 8
HEAD_DIM = 64
HEAD_PAIRS = N_HEADS // 2
ATTN_W = N_HEADS * HEAD_DIM
CONV_W = 512
RMS_EPS = 1e-6
FFN_RES = 0.5
LANES = 128
VMEM_LIMIT = 56 * 1024 * 1024
ROW_BLOCK = 256

ADAM_LR = 0.001
ADAM_B1 = 0.9
ADAM_B2 = 0.999
ADAM_EPS = 1e-08
ADAM_WD = 0.01
ADAM_STEP = 10

PROJ_W = 3 * ATTN_W + 3 * CONV_W + 2 * 1024
COL_CB, COL_CC, COL_CX = 3 * ATTN_W, 3 * ATTN_W + CONV_W, 3 * ATTN_W + 2 * CONV_W
COL_GATES = 3 * ATTN_W + 3 * CONV_W
N_FORGET_COL = 3 * ATTN_W


def _params(sem=None, vmem=VMEM_LIMIT):
    return pltpu.CompilerParams(dimension_semantics=sem, vmem_limit_bytes=vmem)


def _dot(a, b):
    return lax.dot_general(a, b, (((1,), (0,)), ((), ())), preferred_element_type=F32)


def _dot_nt(a, b):
    return lax.dot_general(a, b, (((1,), (1,)), ((), ())), preferred_element_type=F32)


def _dot_tn(a, b):
    return lax.dot_general(a, b, (((0,), (0,)), ((), ())), preferred_element_type=F32)


def _sigmoid(x):
    return 1.0 / (1.0 + jnp.exp(-x))


def _rms(xv):
    inv = lax.rsqrt(jnp.mean(xv * xv, axis=-1, keepdims=True) + RMS_EPS)
    return xv * inv, inv


class _Comm:
    def __init__(self, inputs, out_shape, scratch, start, finish):
        self.inputs, self.out_shape, self.scratch = list(inputs), list(out_shape), list(scratch)
        self.start, self.finish = start, finish


def _pallas(body, name, grid, in_specs, out_specs, out_shape, scratch, args, comm=None):
    sem = ("arbitrary",) * len(grid)
    if comm is None:
        outs = pl.pallas_call(body, name=name, grid=grid, in_specs=in_specs, out_specs=out_specs,
                              out_shape=out_shape, scratch_shapes=scratch, compiler_params=_params(sem))(*args)
        return list(outs), []
    n_in, n_out, n_scr = len(in_specs), len(out_specs), len(scratch)
    ci, co = len(comm.inputs), len(comm.out_shape)

    def riding(*refs):
        ins, refs = refs[:n_in], refs[n_in:]
        cins, refs = refs[:ci], refs[ci:]
        outs, refs = refs[:n_out], refs[n_out:]
        couts, refs = refs[:co], refs[co:]
        scr, sems = refs[:n_scr], refs[n_scr:]
        ids = [pl.program_id(d) for d in range(len(grid))]
        first = functools.reduce(lambda a, b: a & b, [i == 0 for i in ids])
        last = functools.reduce(lambda a, b: a & b, [i == g - 1 for i, g in zip(ids, grid)])

        @pl.when(first)
        def _():
            comm.start(cins, couts, sems)

        body(*ins, *outs, *scr)

        @pl.when(last)
        def _():
            comm.finish(cins, couts, sems)

    any_spec = pl.BlockSpec(memory_space=pl.ANY)
    outs = pl.pallas_call(
        riding, name=name, grid=grid,
        in_specs=list(in_specs) + [any_spec] * ci, out_specs=list(out_specs) + [any_spec] * co,
        out_shape=list(out_shape) + comm.out_shape, scratch_shapes=list(scratch) + comm.scratch,
        compiler_params=_params(sem))(*args, *comm.inputs)
    return list(outs[:n_out]), list(outs[n_out:])


def _rms_bwd(dn, xhat, inv, g):
    dxhat = dn * g
    dx = inv * (dxhat - xhat * jnp.mean(dxhat * xhat, axis=-1, keepdims=True))
    return dx, jnp.sum(dn * xhat, axis=0, keepdims=True)


def _ffn_fwd(name, x, g, wgt, wut, wd, tm, comm=None):
    T, D = x.shape
    K, Fs, _ = wgt.shape

    def body(x_ref, g_ref, wg_ref, wu_ref, wd_ref, out_ref, hg_ref, hu_ref, n_scr, acc_scr):
        k = pl.program_id(1)

        @pl.when(k == 0)
        def _():
            xhat, _ = _rms(x_ref[...])
            n_scr[...] = (xhat * g_ref[...]).astype(BF16)
            acc_scr[...] = jnp.zeros_like(acc_scr)

        n = n_scr[...]
        hg = _dot_nt(n, wg_ref[...])
        hu = _dot_nt(n, wu_ref[...])
        hg_ref[...] = hg.astype(BF16)
        hu_ref[...] = hu.astype(BF16)
        act = (hg * _sigmoid(hg) * hu).astype(BF16)
        acc_scr[...] += _dot(act, wd_ref[...])

        @pl.when(k == K - 1)
        def _():
            out_ref[...] = x_ref[...] + FFN_RES * acc_scr[...]

    w_spec = pl.BlockSpec((None, Fs, D), lambda i, k: (k, 0, 0))
    act_spec = pl.BlockSpec((None, tm, Fs), lambda i, k: (k, i, 0))
    return _pallas(
        body, name, (T // tm, K),
        [pl.BlockSpec((tm, D), lambda i, k: (i, 0)), pl.BlockSpec((1, D), lambda i, k: (0, 0)),
         w_spec, w_spec, w_spec],
        [pl.BlockSpec((tm, D), lambda i, k: (i, 0)), act_spec, act_spec],
        [jax.ShapeDtypeStruct((T, D), F32), jax.ShapeDtypeStruct((K, T, Fs), BF16),
         jax.ShapeDtypeStruct((K, T, Fs), BF16)],
        [pltpu.VMEM((tm, D), BF16), pltpu.VMEM((tm, D), F32)],
        (x, g, wgt, wut, wd), comm)


def _ffn_bwd_dx(name, dout, x, g, hg, hu, wgt, wut, wd, tm, comm=None):
    T, D = x.shape
    K, Fs, _ = wgt.shape

    def body(dout_ref, x_ref, g_ref, hg_ref, hu_ref, wg_ref, wu_ref, wd_ref,
             dx_ref, dhg_ref, dhu_ref, dg_ref, df_scr, dn_scr):
        i, k = pl.program_id(0), pl.program_id(1)

        @pl.when(k == 0)
        def _():
            df_scr[...] = (FFN_RES * dout_ref[...]).astype(BF16)
            dn_scr[...] = jnp.zeros_like(dn_scr)

        @pl.when((k == 0) & (i == 0))
        def _():
            dg_ref[...] = jnp.zeros_like(dg_ref)

        for r0 in range(0, tm, ROW_BLOCK):
            rows = slice(r0, r0 + ROW_BLOCK)
            dact = _dot_nt(df_scr[rows, :], wd_ref[...])
            hgv = hg_ref[rows, :].astype(F32)
            huv = hu_ref[rows, :].astype(F32)
            s = _sigmoid(hgv)
            dhu = (dact * (hgv * s)).astype(BF16)
            dhg = (dact * huv * (s * (1.0 + hgv * (1.0 - s)))).astype(BF16)
            dhg_ref[rows, :] = dhg
            dhu_ref[rows, :] = dhu
            dn_scr[rows, :] += _dot(dhg, wg_ref[...]) + _dot(dhu, wu_ref[...])

        @pl.when(k == K - 1)
        def _():
            xhat, inv = _rms(x_ref[...])
            dx, dg = _rms_bwd(dn_scr[...], xhat, inv, g_ref[...])
            dx_ref[...] = dout_ref[...] + dx
            dg_ref[...] += dg

    w_spec = pl.BlockSpec((None, Fs, D), lambda i, k: (k, 0, 0))
    act_spec = pl.BlockSpec((None, tm, Fs), lambda i, k: (k, i, 0))
    row = pl.BlockSpec((tm, D), lambda i, k: (i, 0))
    vec = pl.BlockSpec((1, D), lambda i, k: (0, 0))
    return _pallas(
        body, name, (T // tm, K),
        [row, row, vec, act_spec, act_spec, w_spec, w_spec, w_spec],
        [row, act_spec, act_spec, vec],
        [jax.ShapeDtypeStruct((T, D), F32), jax.ShapeDtypeStruct((K, T, Fs), BF16),
         jax.ShapeDtypeStruct((K, T, Fs), BF16), jax.ShapeDtypeStruct((1, D), F32)],
        [pltpu.VMEM((tm, D), BF16), pltpu.VMEM((tm, D), F32)],
        (dout, x, g, hg, hu, wgt, wut, wd), comm)


def _ffn_bwd_dw(name, dout, x, g, hg, hu, dhg, dhu, tk, comm=None):
    T, D = x.shape
    K, _, Fs = hg.shape
    nt = T // tk

    def body(dout_ref, x_ref, g_ref, hg_ref, hu_ref, dhg_ref, dhu_ref,
             dwg_ref, dwu_ref, dwd_ref, accg, accu, accd):
        t = pl.program_id(1)

        @pl.when(t == 0)
        def _():
            accg[...] = jnp.zeros_like(accg)
            accu[...] = jnp.zeros_like(accu)
            accd[...] = jnp.zeros_like(accd)

        xhat, _ = _rms(x_ref[...])
        n = (xhat * g_ref[...]).astype(BF16)
        df = (FFN_RES * dout_ref[...]).astype(BF16)
        hgv = hg_ref[...].astype(F32)
        act = (hgv * _sigmoid(hgv) * hu_ref[...].astype(F32)).astype(BF16)
        accg[...] += _dot_tn(dhg_ref[...], n)
        accu[...] += _dot_tn(dhu_ref[...], n)
        accd[...] += _dot_tn(act, df)

        @pl.when(t == nt - 1)
        def _():
            dwg_ref[...] = accg[...].astype(BF16)
            dwu_ref[...] = accu[...].astype(BF16)
            dwd_ref[...] = accd[...].astype(BF16)

    act_spec = pl.BlockSpec((None, tk, Fs), lambda k, t: (k, t, 0))
    w_spec = pl.BlockSpec((None, Fs, D), lambda k, t: (k, 0, 0))
    return _pallas(
        body, name, (K, nt),
        [pl.BlockSpec((tk, D), lambda k, t: (t, 0)), pl.BlockSpec((tk, D), lambda k, t: (t, 0)),
         pl.BlockSpec((1, D), lambda k, t: (0, 0)), act_spec, act_spec, act_spec, act_spec],
        [w_spec, w_spec, w_spec],
        [jax.ShapeDtypeStruct((K, Fs, D), BF16)] * 3,
        [pltpu.VMEM((Fs, D), F32)] * 3,
        (dout, x, g, hg, hu, dhg, dhu), comm)


def _mix_proj_fwd(x, g, wproj_t, wf_t, tm, tn):
    T, D = x.shape
    N = wproj_t.shape[0]

    def body(x_ref, g_ref, w_ref, wf_ref, h_ref, proj_ref, flog_ref, h_scr):
        @pl.when(pl.program_id(1) == 0)
        def _():
            xhat, _ = _rms(x_ref[...])
            h = (xhat * g_ref[...]).astype(BF16)
            h_scr[...] = h
            h_ref[...] = h
            flog_ref[...] = _dot_nt(h, wf_ref[...])

        proj_ref[...] = _dot_nt(h_scr[...], w_ref[...]).astype(BF16)

    return pl.pallas_call(
        body, name="mix_proj_fwd", grid=(T // tm, N // tn),
        in_specs=[pl.BlockSpec((tm, D), lambda i, n: (i, 0)),
                  pl.BlockSpec((1, D), lambda i, n: (0, 0)),
                  pl.BlockSpec((tn, D), lambda i, n: (n, 0)),
                  pl.BlockSpec((LANES, D), lambda i, n: (0, 0))],
        out_specs=[pl.BlockSpec((tm, D), lambda i, n: (i, 0)),
                   pl.BlockSpec((tm, tn), lambda i, n: (i, n)),
                   pl.BlockSpec((tm, LANES), lambda i, n: (i, 0))],
        out_shape=[jax.ShapeDtypeStruct((T, D), BF16),
                   jax.ShapeDtypeStruct((T, N), BF16),
                   jax.ShapeDtypeStruct((T, LANES), F32)],
        scratch_shapes=[pltpu.VMEM((tm, D), BF16)],
        compiler_params=_params(("arbitrary", "arbitrary")),
    )(x, g, wproj_t, wf_t)


def _log_sigmoid(z):
    return -(jnp.maximum(-z, 0.0) + jnp.log(1.0 + jnp.exp(-jnp.abs(z))))


def _tri(n, lower):
    r = lax.broadcasted_iota(jnp.int32, (n, n), 0)
    c = lax.broadcasted_iota(jnp.int32, (n, n), 1)
    return jnp.where((r >= c) if lower else (r <= c), 1.0, 0.0).astype(F32)


def _dot_f32(a, b):
    return lax.dot_general(a, b, (((1,), (0,)), ((), ())), preferred_element_type=F32,
                           precision=lax.Precision.HIGHEST)


def _fgate_fwd(flog, bias, B, S, ch):
    def body(flog_ref, b_ref, cum_ref):
        tri = _tri(ch, True)
        carry = jnp.zeros((1, LANES), F32)
        for c0 in range(0, S, ch):
            lf = _log_sigmoid(flog_ref[c0:c0 + ch, :] + b_ref[...])
            cs = _dot_f32(tri, lf) + carry
            cum_ref[c0:c0 + ch, :] = cs
            carry = cs[ch - 1:ch, :]

    return pl.pallas_call(
        body, name="fgate_fwd", grid=(B,),
        in_specs=[pl.BlockSpec((S, LANES), lambda b: (b, 0)),
                  pl.BlockSpec((1, LANES), lambda b: (0, 0))],
        out_specs=pl.BlockSpec((S, LANES), lambda b: (b, 0)),
        out_shape=jax.ShapeDtypeStruct((B * S, LANES), F32),
        compiler_params=_params(("arbitrary",)),
    )(flog, bias)


def _fgate_bwd(dcum, flog, bias, B, S, ch):
    def body(dcum_ref, flog_ref, b_ref, dflog_ref, db_ref):
        @pl.when(pl.program_id(0) == 0)
        def _():
            db_ref[...] = jnp.zeros_like(db_ref)

        tri = _tri(ch, False)
        carry = jnp.zeros((1, LANES), F32)
        db = jnp.zeros((1, LANES), F32)
        for c0 in range(S - ch, -1, -ch):
            dlf = _dot_f32(tri, dcum_ref[c0:c0 + ch, :]) + carry
            carry = dlf[0:1, :]
            z = flog_ref[c0:c0 + ch, :] + b_ref[...]
            dz = dlf * _sigmoid(-z)
            dflog_ref[c0:c0 + ch, :] = dz
            db = db + jnp.sum(dz, axis=0, keepdims=True)
        db_ref[...] += db

    return pl.pallas_call(
        body, name="fgate_bwd", grid=(B,),
        in_specs=[pl.BlockSpec((S, LANES), lambda b: (b, 0)),
                  pl.BlockSpec((S, LANES), lambda b: (b, 0)),
                  pl.BlockSpec((1, LANES), lambda b: (0, 0))],
        out_specs=[pl.BlockSpec((S, LANES), lambda b: (b, 0)),
                   pl.BlockSpec((1, LANES), lambda b: (0, 0))],
        out_shape=[jax.ShapeDtypeStruct((B * S, LANES), F32),
                   jax.ShapeDtypeStruct((1, LANES), F32)],
        compiler_params=_params(("arbitrary",)),
    )(dcum, flog, bias)


def _pick_lane(tile, h):
    lane = lax.broadcasted_iota(jnp.int32, tile.shape, 1)
    return jnp.sum(jnp.where(lane == h, tile, 0.0), axis=1, keepdims=True)


def _put_lane(col, h, width=LANES):
    lane = lax.broadcasted_iota(jnp.int32, (col.shape[0], width), 1)
    return jnp.where(lane == h, col, 0.0)


def _pick_row(tile, h):
    row = lax.broadcasted_iota(jnp.int32, tile.shape, 0)
    return jnp.sum(jnp.where(row == h, tile, 0.0), axis=0, keepdims=True)


def _put_row(vec, h):
    row = lax.broadcasted_iota(jnp.int32, (8, vec.shape[1]), 0)
    return jnp.where(row == h, vec, 0.0)


def _causal(tq):
    r = lax.broadcasted_iota(jnp.int32, (tq, tq), 0)
    c = lax.broadcasted_iota(jnp.int32, (tq, tq), 1)
    return r >= c


def _head_halves(t):
    lo = lax.broadcasted_iota(jnp.int32, t.shape, 1) < HEAD_DIM
    zero = jnp.zeros_like(t)
    return jnp.where(lo, t, zero), jnp.where(lo, zero, t)


NEG = -1e30
ATTN_SCALE = 1.0 / math.sqrt(HEAD_DIM)


def _scaled(q):
    return (q.astype(F32) * ATTN_SCALE).astype(q.dtype)


def _attn_fwd(proj, cum, cum_t, B, S, tq, comm=None):
    nq = S // tq

    def body(q_ref, k_ref, v_ref, cum_ref, cumt_ref, o_ref, lse_ref):
        qi, hp = pl.program_id(1), pl.program_id(2)
        qm = _head_halves(_scaled(q_ref[...]))
        cumv = cum_ref[...]
        cq = [_pick_lane(cumv, 2 * hp + e) for e in range(2)]

        def tile(j, carry, masked):
            off = pl.multiple_of(j * tq, tq)
            kj = k_ref[pl.ds(off, tq), :]
            vj = v_ref[pl.ds(off, tq), :]
            ct = cumt_ref[j]
            new = []
            for e in range(2):
                m, l, acc = carry[e]
                s = _dot_nt(qm[e], kj) + (cq[e] - _pick_row(ct, 2 * hp + e))
                if masked:
                    s = jnp.where(_causal(tq), s, NEG)
                m_new = jnp.maximum(m, jnp.max(s, axis=1, keepdims=True))
                p = jnp.exp(s - m_new)
                alpha = jnp.exp(m - m_new)
                l = alpha * l + jnp.sum(p, axis=1, keepdims=True)
                acc = alpha * acc + _dot(p.astype(BF16), vj)
                new.append((m_new, l, acc))
            return tuple(new)

        one = (jnp.full((tq, 1), NEG, F32), jnp.zeros((tq, 1), F32), jnp.zeros((tq, LANES), F32))
        carry = lax.fori_loop(0, qi, lambda j, c: tile(j, c, False), (one, one))
        (ma, la, acca), (mb, lb, accb) = tile(qi, carry, True)
        lo = lax.broadcasted_iota(jnp.int32, (tq, LANES), 1) < HEAD_DIM
        o_ref[...] = jnp.where(lo, acca / la, accb / lb).astype(BF16)

        @pl.when(hp == 0)
        def _():
            lse_ref[...] = jnp.zeros_like(lse_ref)

        lse_ref[...] += _put_lane(ma + jnp.log(la), 2 * hp) + _put_lane(mb + jnp.log(lb), 2 * hp + 1)

    kv = lambda first: pl.BlockSpec((S, LANES), lambda b, i, hp: (b, first + hp))
    return _pallas(
        body, "attn_fwd", (B, nq, HEAD_PAIRS),
        [pl.BlockSpec((tq, LANES), lambda b, i, hp: (b * nq + i, hp)),
         kv(ATTN_W // LANES), kv(2 * ATTN_W // LANES),
         pl.BlockSpec((tq, LANES), lambda b, i, hp: (b * nq + i, 0)),
         pl.BlockSpec((None, nq, 8, tq), lambda b, i, hp: (b, 0, 0, 0))],
        [pl.BlockSpec((tq, LANES), lambda b, i, hp: (b * nq + i, hp)),
         pl.BlockSpec((tq, LANES), lambda b, i, hp: (b * nq + i, 0))],
        [jax.ShapeDtypeStruct((B * S, ATTN_W), BF16), jax.ShapeDtypeStruct((B * S, LANES), F32)],
        [], (proj, proj, proj, cum, cum_t), comm)


def _attn_bwd(proj, o, do, lse, cum, cum_t, B, S, tq, comm=None):
    nq = S // tq

    def body(q_ref, k_ref, v_ref, o_ref, do_ref, lse_ref, cum_ref, cumt_ref,
             dq_ref, dk_ref, dv_ref, dcq_ref, dck_ref, dq_scr):
        hp, kj = pl.program_id(1), pl.program_id(2)

        @pl.when(kj == 0)
        def _():
            dq_scr[...] = jnp.zeros_like(dq_scr)

        @pl.when((kj == 0) & (hp == 0))
        def _():
            dcq_ref[...] = jnp.zeros_like(dcq_ref)
            dck_ref[...] = jnp.zeros_like(dck_ref)

        kv = k_ref[...]
        vv = v_ref[...]
        km = _head_halves(kv)
        ct = cumt_ref[...]
        ck = [_pick_row(ct, 2 * hp + e) for e in range(2)]

        def tile(i, carry, masked):
            dk, dv, dcol = carry
            off = pl.multiple_of(i * tq, tq)
            qi = q_ref[pl.ds(off, tq), :]
            ov = o_ref[pl.ds(off, tq), :].astype(F32)
            qm = _head_halves(_scaled(qi))
            dom = _head_halves(do_ref[pl.ds(off, tq), :])
            cumv = cum_ref[pl.ds(off, tq), :]
            lsev = lse_ref[pl.ds(off, tq), :]
            dcq = jnp.zeros((tq, LANES), F32)
            dq = jnp.zeros((tq, LANES), F32)
            dcol_new = []
            for e in range(2):
                delta = jnp.sum(dom[e].astype(F32) * ov, axis=1, keepdims=True)
                row_term = _pick_lane(cumv, 2 * hp + e) - _pick_lane(lsev, 2 * hp + e)
                p = jnp.exp(_dot_nt(qm[e], kv) + row_term - ck[e])
                if masked:
                    p = jnp.where(_causal(tq), p, 0.0)
                dv = dv + _dot_tn(p.astype(BF16), dom[e])
                ds = p * (_dot_nt(dom[e], vv) - delta)
                dcol_new.append(dcol[e] + jnp.sum(ds, axis=0, keepdims=True))
                dcq = dcq + _put_lane(jnp.sum(ds, axis=1, keepdims=True), 2 * hp + e)
                dsb = ds.astype(BF16)
                dk = dk + _dot_tn(dsb, qm[e])
                dq = dq + _dot(dsb, km[e]) * ATTN_SCALE
            dq_scr[pl.ds(off, tq), :] += dq
            dcq_ref[pl.ds(off, tq), :] += dcq
            return dk, dv, tuple(dcol_new)

        zero_row = jnp.zeros((1, tq), F32)
        init = (jnp.zeros((tq, LANES), F32), jnp.zeros((tq, LANES), F32), (zero_row, zero_row))
        carry = tile(kj, init, True)
        dk, dv, dcol = lax.fori_loop(kj + 1, nq, lambda i, c: tile(i, c, False), carry)
        dk_ref[...] = dk.astype(BF16)
        dv_ref[...] = dv.astype(BF16)
        dck_ref[kj] += -(_put_row(dcol[0], 2 * hp) + _put_row(dcol[1], 2 * hp + 1))

        @pl.when(kj == nq - 1)
        def _():
            dq_ref[...] = dq_scr[...].astype(BF16)

    seq = lambda first: pl.BlockSpec((S, LANES), lambda b, hp, j: (b, first + hp))
    tile_in = lambda first: pl.BlockSpec((tq, LANES), lambda b, hp, j: (b * nq + j, first + hp))
    lanes0 = pl.BlockSpec((S, LANES), lambda b, hp, j: (b, 0))
    out = jax.ShapeDtypeStruct((B * S, ATTN_W), BF16)
    return _pallas(
        body, "attn_bwd", (B, HEAD_PAIRS, nq),
        [seq(0), tile_in(ATTN_W // LANES), tile_in(2 * ATTN_W // LANES), seq(0), seq(0), lanes0, lanes0,
         pl.BlockSpec((None, None, 8, tq), lambda b, hp, j: (b, j, 0, 0))],
        [seq(0), tile_in(0), tile_in(0), lanes0,
         pl.BlockSpec((None, nq, 8, tq), lambda b, hp, j: (b, 0, 0, 0))],
        [out, out, out, jax.ShapeDtypeStruct((B * S, LANES), F32), jax.ShapeDtypeStruct((B, nq, 8, tq), F32)],
        [pltpu.VMEM((S, LANES), F32)],
        (proj, proj, proj, o, do, lse, cum, cum_t), comm)


def _shift_down(u, n):
    row = lax.broadcasted_iota(jnp.int32, u.shape, 0)
    return jnp.where(row >= n, pltpu.roll(u, n, 0), 0.0)


def _shift_up(u, n):
    rows = u.shape[0]
    row = lax.broadcasted_iota(jnp.int32, u.shape, 0)
    return jnp.where(row < rows - n, pltpu.roll(u, rows - n, 0), 0.0)


def _conv_specs(S):
    cb = pl.BlockSpec((S, LANES), lambda g, b: (b, COL_CB // LANES + g))
    cc = pl.BlockSpec((S, LANES), lambda g, b: (b, COL_CC // LANES + g))
    cx = pl.BlockSpec((S, LANES), lambda g, b: (b, COL_CX // LANES + g))
    w = pl.BlockSpec((8, LANES), lambda g, b: (0, g))
    return cb, cc, cx, w


def _conv_fwd(proj, conv_w, B, S):
    def body(cb_ref, cc_ref, cx_ref, w_ref, y_ref):
        u = cc_ref[...].astype(F32) * cx_ref[...].astype(F32)
        w = w_ref[...]
        conv = w[0:1, :] * _shift_down(u, 2) + w[1:2, :] * _shift_down(u, 1) + w[2:3, :] * u
        y_ref[...] = (cb_ref[...].astype(F32) * conv).astype(BF16)

    cb, cc, cx, w = _conv_specs(S)
    return pl.pallas_call(
        body, name="conv_fwd", grid=(CONV_W // LANES, B),
        in_specs=[cb, cc, cx, w],
        out_specs=pl.BlockSpec((S, LANES), lambda g, b: (b, g)),
        out_shape=jax.ShapeDtypeStruct((B * S, CONV_W), BF16),
        compiler_params=_params(("arbitrary", "arbitrary")),
    )(proj, proj, proj, conv_w)


def _conv_bwd(dy, proj, conv_w, B, S):
    def body(dy_ref, cb_ref, cc_ref, cx_ref, w_ref, dcb_ref, dcc_ref, dcx_ref, dw_ref):
        @pl.when(pl.program_id(1) == 0)
        def _():
            dw_ref[...] = jnp.zeros_like(dw_ref)

        ccv = cc_ref[...].astype(F32)
        cxv = cx_ref[...].astype(F32)
        u = ccv * cxv
        u1 = _shift_down(u, 1)
        u2 = _shift_down(u, 2)
        w = w_ref[...]
        conv = w[0:1, :] * u2 + w[1:2, :] * u1 + w[2:3, :] * u
        dyv = dy_ref[...].astype(F32)
        dcb_ref[...] = (dyv * conv).astype(BF16)
        dconv = dyv * cb_ref[...].astype(F32)
        du = w[2:3, :] * dconv + w[1:2, :] * _shift_up(dconv, 1) + w[0:1, :] * _shift_up(dconv, 2)
        dcc_ref[...] = (du * cxv).astype(BF16)
        dcx_ref[...] = (du * ccv).astype(BF16)
        row = lax.broadcasted_iota(jnp.int32, (8, LANES), 0)
        dw = jnp.where(row == 0, jnp.sum(dconv * u2, axis=0, keepdims=True),
                       jnp.where(row == 1, jnp.sum(dconv * u1, axis=0, keepdims=True),
                                 jnp.where(row == 2, jnp.sum(dconv * u, axis=0, keepdims=True), 0.0)))
        dw_ref[...] += dw

    cb, cc, cx, w = _conv_specs(S)
    out = pl.BlockSpec((S, LANES), lambda g, b: (b, g))
    return pl.pallas_call(
        body, name="conv_bwd", grid=(CONV_W // LANES, B),
        in_specs=[out, cb, cc, cx, w],
        out_specs=[out, out, out, w],
        out_shape=[jax.ShapeDtypeStruct((B * S, CONV_W), BF16)] * 3 + [jax.ShapeDtypeStruct((8, CONV_W), F32)],
        compiler_params=_params(("arbitrary", "arbitrary")),
    )(dy, proj, proj, proj, conv_w)


def _gate_specs(tm, D):
    ga = pl.BlockSpec((tm, D), lambda i: (i, COL_GATES // D))
    gc = pl.BlockSpec((tm, D), lambda i: (i, COL_GATES // D + 1))
    return ga, gc


def _mix_out_fwd(x, o, yc, proj, woa, woc, wout, tm):
    T, D = x.shape

    def body(x_ref, o_ref, yc_ref, ga_ref, gc_ref, woa_ref, woc_ref, wout_ref, out_ref):
        ya = _dot(o_ref[...], woa_ref[...])
        yp = _dot(yc_ref[...], woc_ref[...])
        merged = _sigmoid(ga_ref[...].astype(F32)) * ya + _sigmoid(gc_ref[...].astype(F32)) * yp
        out_ref[...] = x_ref[...] + _dot(merged.astype(BF16), wout_ref[...])

    ga, gc = _gate_specs(tm, D)
    row = lambda w: pl.BlockSpec((tm, w), lambda i: (i, 0))
    whole = lambda a: pl.BlockSpec(a.shape, lambda i: (0, 0))
    return pl.pallas_call(
        body, name="mix_out_fwd", grid=(T // tm,),
        in_specs=[row(D), row(ATTN_W), row(CONV_W), ga, gc, whole(woa), whole(woc), whole(wout)],
        out_specs=row(D),
        out_shape=jax.ShapeDtypeStruct((T, D), F32),
        compiler_params=_params(("arbitrary",)),
    )(x, o, yc, proj, proj, woa, woc, wout)


def _mix_out_bwd(dx, o, yc, proj, woa, woc, wout, tm, comm=None):
    T, D = dx.shape
    nt = T // tm

    def body(dx_ref, o_ref, yc_ref, ga_ref, gc_ref, woa_ref, woc_ref, wout_ref,
             do_ref, dyc_ref, dg_ref, dwoa_ref, dwoc_ref, dwout_ref, acca, accc, acco):
        t = pl.program_id(0)

        @pl.when(t == 0)
        def _():
            acca[...] = jnp.zeros_like(acca)
            accc[...] = jnp.zeros_like(accc)
            acco[...] = jnp.zeros_like(acco)

        dxb = dx_ref[...].astype(BF16)
        ov, ycv = o_ref[...], yc_ref[...]
        ya = _dot(ov, woa_ref[...])
        yp = _dot(ycv, woc_ref[...])
        sa = _sigmoid(ga_ref[...].astype(F32))
        sc = _sigmoid(gc_ref[...].astype(F32))
        merged = (sa * ya + sc * yp).astype(BF16)
        dm = _dot_nt(dxb, wout_ref[...])
        dya = (dm * sa).astype(BF16)
        dyp = (dm * sc).astype(BF16)
        dg_ref[:, :D] = (dm * ya * sa * (1.0 - sa)).astype(BF16)
        dg_ref[:, D:] = (dm * yp * sc * (1.0 - sc)).astype(BF16)
        do_ref[...] = _dot_nt(dya, woa_ref[...]).astype(BF16)
        dyc_ref[...] = _dot_nt(dyp, woc_ref[...]).astype(BF16)
        acca[...] += _dot_tn(ov, dya)
        accc[...] += _dot_tn(ycv, dyp)
        acco[...] += _dot_tn(merged, dxb)

        @pl.when(t == nt - 1)
        def _():
            dwoa_ref[...] = acca[...].astype(BF16)
            dwoc_ref[...] = accc[...].astype(BF16)
            dwout_ref[...] = acco[...].astype(BF16)

    ga, gc = _gate_specs(tm, D)
    row = lambda w: pl.BlockSpec((tm, w), lambda i: (i, 0))
    whole = lambda a: pl.BlockSpec(a.shape, lambda i: (0, 0))
    return _pallas(
        body, "mix_out_bwd", (nt,),
        [row(D), row(ATTN_W), row(CONV_W), ga, gc, whole(woa), whole(woc), whole(wout)],
        [row(ATTN_W), row(CONV_W), row(2 * D), whole(woa), whole(woc), whole(wout)],
        [jax.ShapeDtypeStruct((T, ATTN_W), BF16), jax.ShapeDtypeStruct((T, CONV_W), BF16),
         jax.ShapeDtypeStruct((T, 2 * D), BF16),
         jax.ShapeDtypeStruct(woa.shape, BF16), jax.ShapeDtypeStruct(woc.shape, BF16),
         jax.ShapeDtypeStruct(wout.shape, BF16)],
        [pltpu.VMEM(woa.shape, F32), pltpu.VMEM(woc.shape, F32), pltpu.VMEM(wout.shape, F32)],
        (dx, o, yc, proj, proj, woa, woc, wout), comm)


def _proj_pieces(dq, dk, dv, dcb, dcc, dcx, dgates, dflog):
    D = dgates.shape[1] // 2
    return [(dq, ATTN_W, 0), (dk, ATTN_W, 0), (dv, ATTN_W, 0), (dcb, CONV_W, 0), (dcc, CONV_W, 0), (dcx, CONV_W, 0),
            (dgates, D, 0), (dgates, D, 1), (dflog, LANES, 0)]


def _mix_proj_bwd_dx(dres, x, g, pieces, wproj_t, wf_t, tm, comm=None):
    T, D = x.shape
    n = len(pieces)
    w_blocks = [(ATTN_W, 0), (ATTN_W, 1), (ATTN_W, 2), (CONV_W, 3), (CONV_W, 4), (CONV_W, 5),
                (D, COL_GATES // D), (D, COL_GATES // D + 1)]

    def body(*refs):
        dres_ref, x_ref, g_ref = refs[:3]
        p_refs, w_refs = refs[3:3 + n], refs[3 + n:3 + 2 * n]
        dx_ref, dg_ref = refs[3 + 2 * n:]

        @pl.when(pl.program_id(0) == 0)
        def _():
            dg_ref[...] = jnp.zeros_like(dg_ref)

        dh = _dot(p_refs[0][...].astype(BF16), w_refs[0][...])
        for p_ref, w_ref in zip(p_refs[1:], w_refs[1:]):
            dh = dh + _dot(p_ref[...].astype(BF16), w_ref[...])
        xhat, inv = _rms(x_ref[...])
        dx, dg = _rms_bwd(dh, xhat, inv, g_ref[...])
        dx_ref[...] = dres_ref[...] + dx
        dg_ref[...] += dg

    row = pl.BlockSpec((tm, D), lambda i: (i, 0))
    vec = pl.BlockSpec((1, D), lambda i: (0, 0))
    p_specs = [pl.BlockSpec((tm, w), lambda i, cb=cb: (i, cb)) for _, w, cb in pieces]
    w_specs = [pl.BlockSpec((r, D), lambda i, rb=rb: (rb, 0)) for r, rb in w_blocks]
    w_specs.append(pl.BlockSpec((LANES, D), lambda i: (0, 0)))
    return _pallas(
        body, "mix_proj_bwd_dx", (T // tm,),
        [row, row, vec] + p_specs + w_specs, [row, vec],
        [jax.ShapeDtypeStruct((T, D), F32), jax.ShapeDtypeStruct((1, D), F32)], [],
        (dres, x, g, *[p for p, _, _ in pieces], *([wproj_t] * len(w_blocks)), wf_t), comm)


def _matmuls_tn(name, pieces, b, tk):
    T, N = b.shape
    nt = T // tk
    n = len(pieces)

    def body(*refs):
        a_refs, b_ref, out_refs, accs = refs[:n], refs[n], refs[n + 1:2 * n + 1], refs[2 * n + 1:]
        t = pl.program_id(0)

        @pl.when(t == 0)
        def _():
            for acc in accs:
                acc[...] = jnp.zeros_like(acc)

        bv = b_ref[...]
        for a_ref, acc in zip(a_refs, accs):
            acc[...] += _dot_tn(a_ref[...].astype(BF16), bv)

        @pl.when(t == nt - 1)
        def _():
            for out_ref, acc in zip(out_refs, accs):
                out_ref[...] = acc[...].astype(BF16)

    return pl.pallas_call(
        body, name=name, grid=(nt,),
        in_specs=[pl.BlockSpec((tk, w), lambda t, cb=cb: (t, cb)) for _, w, cb in pieces]
        + [pl.BlockSpec((tk, N), lambda t: (t, 0))],
        out_specs=[pl.BlockSpec((w, N), lambda t: (0, 0)) for _, w, _ in pieces],
        out_shape=[jax.ShapeDtypeStruct((w, N), BF16) for _, w, _ in pieces],
        scratch_shapes=[pltpu.VMEM((w, N), F32) for _, w, _ in pieces],
        compiler_params=_params(("arbitrary",)),
    )(*[a for a, _, _ in pieces], b)


def _final_loss(x, target, g, tm):
    T, D = x.shape

    def body(x_ref, t_ref, g_ref, dx_ref, loss_ref, dg_ref):
        @pl.when(pl.program_id(0) == 0)
        def _():
            loss_ref[...] = jnp.zeros_like(loss_ref)
            dg_ref[...] = jnp.zeros_like(dg_ref)

        xhat, inv = _rms(x_ref[...])
        err = xhat * g_ref[...] - t_ref[...]
        loss_ref[...] += 0.5 * jnp.sum(jnp.sum(err * err, axis=1, keepdims=True), axis=0, keepdims=True) / D
        dx, dg = _rms_bwd(err * (1.0 / D), xhat, inv, g_ref[...])
        dx_ref[...] = dx
        dg_ref[...] += dg

    row = pl.BlockSpec((tm, D), lambda i: (i, 0))
    return pl.pallas_call(
        body, name="final_loss", grid=(T // tm,),
        in_specs=[row, row, pl.BlockSpec((1, D), lambda i: (0, 0))],
        out_specs=[row, pl.BlockSpec((1, LANES), lambda i: (0, 0)), pl.BlockSpec((1, D), lambda i: (0, 0))],
        out_shape=[jax.ShapeDtypeStruct((T, D), F32), jax.ShapeDtypeStruct((1, LANES), F32),
                   jax.ShapeDtypeStruct((1, D), F32)],
        compiler_params=_params(("arbitrary",)),
    )(x, target, g)


class _LocalPlan:
    def __init__(self, stacks, small):
        self.stacks, self.small, self.grads = stacks, small, {}

    def weights(self, group):
        return _LAYOUTS[group](self.stacks, self.small)

    def rider(self, kernel_name):
        return None

    def arrived(self, kernel_name, results):
        pass

    def reduce(self, group, grads):
        self.grads.update(grads)


def _local_step(x, target, plan, B, S):
    T, D = x.shape
    tm = min(512, T)
    tq = min(512, S)
    nq = S // tq
    ch = min(256, S)

    def riding(kernel_name, build):
        results, brought = build(plan.rider(kernel_name))
        plan.arrived(kernel_name, brought)
        return results

    w1 = plan.weights("ffn1")
    x1, hg1, hu1 = riding("ffn1_fwd", lambda comm: _ffn_fwd(
        "ffn1_fwd", x, w1["ffn1_norm"], w1["ffn1_gate"], w1["ffn1_up"], w1["ffn1_down"], tm, comm))
    wm = plan.weights("mix")
    h, proj, flog = _mix_proj_fwd(x1, wm["mix_norm"], wm["w_proj"], wm["w_f"], tm, 1280)
    cum = _fgate_fwd(flog, wm["b_forget"], B, S, ch)
    cum_t = jnp.transpose(cum[:, :N_HEADS].reshape(B, nq, tq, N_HEADS), (0, 1, 3, 2))
    o, lse = riding("attn_fwd", lambda comm: _attn_fwd(proj, cum, cum_t, B, S, tq, comm))
    yc = _conv_fwd(proj, wm["conv_w"], B, S)
    x2 = _mix_out_fwd(x1, o, yc, proj, wm["w_o_attn"], wm["w_o_conv"], wm["w_out"], tm)
    w2 = plan.weights("ffn2")
    x3, hg2, hu2 = _ffn_fwd("ffn2_fwd", x2, w2["ffn2_norm"], w2["ffn2_gate"], w2["ffn2_up"], w2["ffn2_down"], tm)[0]
    dx3, loss, d_final_norm = _final_loss(x3, target, w2["final_norm"], tm)

    g = {"final_norm": d_final_norm}
    dx2, dhg2, dhu2, g["ffn2_norm"] = _ffn_bwd_dx("ffn2_bwd_dx", dx3, x2, w2["ffn2_norm"], hg2, hu2,
                                                  w2["ffn2_gate"], w2["ffn2_up"], w2["ffn2_down"], tm)[0]
    plan.reduce("ffn2", dict(zip(("ffn2_gate", "ffn2_up", "ffn2_down"),
                                 _ffn_bwd_dw("ffn2_bwd_dw", dx3, x2, w2["ffn2_norm"], hg2, hu2, dhg2, dhu2, tm)[0])))
    do, dyc, dgates, dwoa, dwoc, dwout = riding("mix_out_bwd", lambda comm: _mix_out_bwd(
        dx2, o, yc, proj, wm["w_o_attn"], wm["w_o_conv"], wm["w_out"], tm, comm))
    plan.reduce("out", dict(w_o_attn=_shard_cols(dwoa), w_o_conv=_shard_cols(dwoc), w_out=dwout.reshape(N_CHIPS, -1, D)))
    dq, dk, dv, dcq, dck = riding("attn_bwd", lambda comm: _attn_bwd(proj, o, do, lse, cum, cum_t, B, S, tq, comm))
    dcum = dcq + jnp.pad(jnp.transpose(dck, (0, 1, 3, 2)).reshape(T, N_HEADS), ((0, 0), (0, LANES - N_HEADS)))
    dflog, g["b_forget"] = _fgate_bwd(dcum, flog, wm["b_forget"], B, S, ch)
    dcb, dcc, dcx, g["conv_w"] = _conv_bwd(dyc, proj, wm["conv_w"], B, S)
    pieces = _proj_pieces(dq, dk, dv, dcb, dcc, dcx, dgates, dflog)
    dwq, dwk, dwv, dwcb, dwcc, dwcx = _matmuls_tn("mix_dw_a", pieces[:6], h, tm)
    dwga, dwgc, dwf = _matmuls_tn("mix_dw_b", pieces[6:], h, tm)
    dwin_t = jnp.concatenate([dwq, dwk, dwv, dwf[:N_HEADS], dwcb, dwcc, dwcx, dwga, dwgc], axis=0)
    plan.reduce("w_in", {"w_in": dwin_t.reshape(N_CHIPS, -1, D)})
    dx1, g["mix_norm"] = riding("mix_proj_bwd_dx", lambda comm: _mix_proj_bwd_dx(
        dx2, x1, wm["mix_norm"], pieces, wm["w_proj"], wm["w_f"], min(256, T), comm))
    grad_x, dhg1, dhu1, g["ffn1_norm"] = _ffn_bwd_dx(
        "ffn1_bwd_dx", dx1, x, w1["ffn1_norm"], hg1, hu1, w1["ffn1_gate"], w1["ffn1_up"], w1["ffn1_down"], tm)[0]
    plan.reduce("ffn1", dict(zip(("ffn1_gate", "ffn1_up", "ffn1_down"), riding("ffn1_bwd_dw", lambda comm: _ffn_bwd_dw(
        "ffn1_bwd_dw", dx1, x, w1["ffn1_norm"], hg1, hu1, dhg1, dhu1, tm, comm)))))
    return loss, grad_x, g


TRANSPOSED = ("ffn1_gate", "ffn1_up", "ffn2_gate", "ffn2_up", "w_in")
NORMS = ("ffn1_norm", "mix_norm", "ffn2_norm", "final_norm")


def _unshard_cols(a):
    return jnp.transpose(a, (1, 0, 2)).reshape(a.shape[1], N_CHIPS * a.shape[2])


def _shard_cols(a):
    return jnp.transpose(a.reshape(a.shape[0], N_CHIPS, a.shape[1] // N_CHIPS), (1, 0, 2))


def _layout_ffn(which):
    def layout(st, small):
        w = {n: st[n] for n in (which + "_gate", which + "_up", which + "_down")}
        w[which + "_norm"] = small[which + "_norm"].reshape(1, -1)
        if which == "ffn2":
            w["final_norm"] = small["final_norm"].reshape(1, -1)
        return w
    return layout


def _layout_mix(st, small):
    win_t = st["w_in"].reshape(-1, st["w_in"].shape[2])
    return {
        "w_proj": jnp.concatenate([win_t[:N_FORGET_COL], win_t[N_FORGET_COL + N_HEADS:]], axis=0),
        "w_f": jnp.pad(win_t[N_FORGET_COL:N_FORGET_COL + N_HEADS], ((0, LANES - N_HEADS), (0, 0))),
        "w_o_attn": _unshard_cols(st["w_o_attn"]),
        "w_o_conv": _unshard_cols(st["w_o_conv"]),
        "w_out": st["w_out"].reshape(-1, st["w_out"].shape[2]),
        "conv_w": _unshard_cols(st["conv_w"]),
        "mix_norm": small["mix_norm"].reshape(1, -1),
        "b_forget": jnp.pad(small["b_forget"].reshape(1, -1), ((0, 0), (0, LANES - N_HEADS))),
    }


_LAYOUTS = {"ffn1": _layout_ffn("ffn1"), "mix": _layout_mix, "ffn2": _layout_ffn("ffn2")}


ANY = pl.BlockSpec(memory_space=pl.ANY)
BIG = ("ffn1_gate", "ffn1_up", "ffn1_down", "w_in", "w_o_attn", "w_o_conv", "w_out",
       "ffn2_gate", "ffn2_up", "ffn2_down")


def _place():
    x, y, c = lax.axis_index("x"), lax.axis_index("y"), lax.axis_index("c")
    others = [(1 - x, y), (x, 1 - y), (1 - x, 1 - y)]
    return x, y, c, others


def _col_halves(cols, c):
    hc = cols // 2
    return pl.ds(pl.multiple_of(c * hc, LANES), hc), pl.ds(pl.multiple_of((1 - c) * hc, LANES), hc)


def _gather_comm(shards, conv_shard=None):
    n = len(shards)
    inputs = list(shards) + ([] if conv_shard is None else [conv_shard])

    def copies(ins, outs, sems):
        send_sems, recv_sems, pass_send, pass_recv = sems[:4]
        x, y, c, others = _place()

        def chip_copy(a, j, chip):
            mine, _ = _col_halves(ins[a].shape[1], c)
            return pltpu.make_async_remote_copy(
                src_ref=ins[a].at[:, mine], dst_ref=outs[a].at[chip, :, mine],
                send_sem=send_sems.at[3 * a + j], recv_sem=recv_sems.at[3 * a + j],
                device_id=(*others[j], c), device_id_type=MESH)

        def pass_copy(a, j, chip, half):
            return pltpu.make_async_remote_copy(
                src_ref=outs[a].at[chip, :, half], dst_ref=outs[a].at[chip, :, half],
                send_sem=pass_send.at[3 * a + j], recv_sem=pass_recv.at[3 * a + j],
                device_id=(x, y, 1 - c), device_id_type=MESH)

        def conv_copy(j, chip):
            return pltpu.make_async_remote_copy(
                src_ref=ins[n], dst_ref=outs[n].at[chip],
                send_sem=sems[4].at[j], recv_sem=sems[5].at[j],
                device_id=(*others[j], c), device_id_type=MESH)

        me = 2 * x + y
        sends = [chip_copy(a, j, me) for a in range(n) for j in range(3)]
        if conv_shard is not None:
            sends += [conv_copy(j, me) for j in range(3)]
        return c, others, sends, chip_copy, pass_copy, conv_copy

    def start(ins, outs, sems):
        for cp in copies(ins, outs, sems)[2]:
            cp.start()

    def finish(ins, outs, sems):
        c, others, sends, chip_copy, pass_copy, conv_copy = copies(ins, outs, sems)
        passed = []
        for a in range(n):
            mine, _ = _col_halves(ins[a].shape[1], c)
            for j, (ox, oy) in enumerate(others):
                chip_copy(a, j, 2 * ox + oy).wait_recv()
                passed.append(pass_copy(a, j, 2 * ox + oy, mine))
                passed[-1].start()
        for a in range(n):
            _, theirs = _col_halves(ins[a].shape[1], c)
            for j, (ox, oy) in enumerate(others):
                pass_copy(a, j, 2 * ox + oy, theirs).wait_recv()
        if conv_shard is not None:
            for j, (ox, oy) in enumerate(others):
                conv_copy(j, 2 * ox + oy).wait_recv()
        for cp in sends + passed:
            cp.wait_send()

    scratch = [pltpu.SemaphoreType.DMA((3 * n,))] * 4
    if conv_shard is not None:
        scratch += [pltpu.SemaphoreType.DMA((3,))] * 2
    return _Comm(inputs, [jax.ShapeDtypeStruct((N_CHIPS,) + s.shape, s.dtype) for s in inputs], scratch, start, finish)


def _fill_own(stacks, shards):
    chip = 2 * lax.axis_index("x") + lax.axis_index("y")
    return [lax.dynamic_update_index_in_dim(st, s, chip, 0) for st, s in zip(stacks, shards)]


def _run_comm(name, comm):
    ci, co = len(comm.inputs), len(comm.out_shape)

    def body(*refs):
        comm.start(refs[:ci], refs[ci:ci + co], refs[ci + co:])
        comm.finish(refs[:ci], refs[ci:ci + co], refs[ci + co:])

    return pl.pallas_call(body, name=name, in_specs=[ANY] * ci, out_specs=[ANY] * co, out_shape=comm.out_shape,
                          scratch_shapes=comm.scratch)(*comm.inputs)


def _sibling_exchange_comm(grads):
    n = len(grads)

    def copies(ins, outs, sems):
        x, y, c, _ = _place()
        return [pltpu.make_async_remote_copy(
            src_ref=ins[a].at[:, :, _col_halves(ins[a].shape[2], c)[1]], dst_ref=outs[a],
            send_sem=sems[0].at[a], recv_sem=sems[1].at[a],
            device_id=(x, y, 1 - c), device_id_type=MESH) for a in range(n)]

    def start(ins, outs, sems):
        for cp in copies(ins, outs, sems):
            cp.start()

    def finish(ins, outs, sems):
        for cp in copies(ins, outs, sems):
            cp.wait()

    half = lambda s: jax.ShapeDtypeStruct((s.shape[0], s.shape[1], s.shape[2] // 2), s.dtype)
    return _Comm(grads, [half(s) for s in grads], [pltpu.SemaphoreType.DMA((n,))] * 2, start, finish)


def _merge_comms(comms):
    def split(refs, count):
        out, at = [], 0
        for cm in comms:
            out.append(refs[at:at + count(cm)])
            at += count(cm)
        return out

    def parts(ins, outs, sems):
        return zip(comms, split(ins, lambda cm: len(cm.inputs)), split(outs, lambda cm: len(cm.out_shape)),
                   split(sems, lambda cm: len(cm.scratch)))

    def start(ins, outs, sems):
        for cm, i, o, s in parts(ins, outs, sems):
            cm.start(i, o, s)

    def finish(ins, outs, sems):
        for cm, i, o, s in parts(ins, outs, sems):
            cm.finish(i, o, s)

    return _Comm(sum([cm.inputs for cm in comms], []), sum([cm.out_shape for cm in comms], []),
                 sum([cm.scratch for cm in comms], []), start, finish)


def _add_halves(name, grad, recv, core):
    K, r, cols = grad.shape
    hc = cols // 2

    def body(core_ref, g_ref, r_ref, out_ref):
        out_ref[...] = (g_ref[...].astype(F32) + r_ref[...].astype(F32)).astype(BF16)

    return pl.pallas_call(
        body, name=name,
        grid_spec=pltpu.PrefetchScalarGridSpec(
            num_scalar_prefetch=1, grid=(K,),
            in_specs=[pl.BlockSpec((None, r, hc), lambda k, core_ref: (k, 0, core_ref[0])),
                      pl.BlockSpec((None, r, hc), lambda k, core_ref: (k, 0, 0))],
            out_specs=pl.BlockSpec((None, r, hc), lambda k, core_ref: (k, 0, 0))),
        out_shape=jax.ShapeDtypeStruct((K, r, hc), BF16),
        compiler_params=_params(("arbitrary",)),
    )(core, grad, recv)


def _chip_exchange_comm(parts):
    n = len(parts)

    def copies(ins, outs, sems):
        x, y, c, others = _place()
        return [pltpu.make_async_remote_copy(
            src_ref=ins[a].at[2 * ox + oy], dst_ref=outs[a].at[j],
            send_sem=sems[0].at[3 * a + j], recv_sem=sems[1].at[3 * a + j],
            device_id=(ox, oy, c), device_id_type=MESH) for a in range(n) for j, (ox, oy) in enumerate(others)]

    def start(ins, outs, sems):
        for cp in copies(ins, outs, sems):
            cp.start()

    def finish(ins, outs, sems):
        for cp in copies(ins, outs, sems):
            cp.wait()

    return _Comm(parts, [jax.ShapeDtypeStruct((3,) + s.shape[1:], s.dtype) for s in parts],
                 [pltpu.SemaphoreType.DMA((3 * n,))] * 2, start, finish)


HBM = pl.BlockSpec(memory_space=pltpu.HBM)
SEM = pl.BlockSpec(memory_space=pltpu.SEMAPHORE)


def _split_exchange_copies(parts, lands, send_sems, recv_sems):
    x, y, c, others = _place()
    return [pltpu.make_async_remote_copy(
        src_ref=parts[a].at[2 * ox + oy], dst_ref=lands[a].at[j],
        send_sem=send_sems.at[3 * a + j], recv_sem=recv_sems.at[3 * a + j],
        device_id=(ox, oy, c), device_id_type=MESH) for a in range(len(parts)) for j, (ox, oy) in enumerate(others)]


def _exchange_start(name, parts):
    n = len(parts)

    def body(*refs):
        ins, lands = refs[:n], refs[n:2 * n]
        send_sems, recv_sems, token = refs[2 * n], refs[2 * n + 1], refs[-1]
        for cp in _split_exchange_copies(ins, lands, send_sems, recv_sems):
            cp.start()
        token[...] = jnp.zeros_like(token)

    land_shape = [(3,) + p.shape[1:] for p in parts]
    outs = pl.pallas_call(
        body, name=name,
        out_shape=[pltpu.SemaphoreType.DMA((3 * n,)), pltpu.SemaphoreType.DMA((3 * n,))]
        + [pltpu.HBM(p.shape, p.dtype) for p in parts] + [pltpu.HBM(s, p.dtype) for s, p in zip(land_shape, parts)]
        + [jax.ShapeDtypeStruct((8, LANES), F32)],
        in_specs=[HBM] * (2 * n), out_specs=[SEM, SEM] + [HBM] * (2 * n) + [pl.BlockSpec(memory_space=pltpu.VMEM)],
        input_output_aliases={i: 2 + i for i in range(2 * n)},
        compiler_params=pltpu.CompilerParams(has_side_effects=pltpu.SideEffectType.DATAFLOW_SIDE_EFFECTING),
    )(*[pltpu.with_memory_space_constraint(p, pltpu.HBM) for p in parts],
      *[pltpu.with_memory_space_constraint(lax.empty(s, p.dtype), pltpu.HBM) for s, p in zip(land_shape, parts)])
    return outs[0], outs[1], list(outs[2:2 + n]), list(outs[2 + n:2 + 2 * n]), outs[-1]


def _exchange_wait(name, send_sems, recv_sems, parts, lands, after):
    n = len(parts)

    def body(*refs):
        ins, zones = refs[:n], refs[n:2 * n]
        for cp in _split_exchange_copies(ins, zones, refs[2 * n], refs[2 * n + 1]):
            cp.wait_send()
            cp.wait_recv()

    outs = pl.pallas_call(
        body, name=name,
        out_shape=[pltpu.HBM(p.shape, p.dtype) for p in parts] + [pltpu.HBM(z.shape, z.dtype) for z in lands],
        in_specs=[HBM] * (2 * n) + [SEM, SEM] + [ANY] * len(after), out_specs=[HBM] * (2 * n),
        input_output_aliases={i: i for i in range(2 * n)},
        compiler_params=pltpu.CompilerParams(has_side_effects=pltpu.SideEffectType.DATAFLOW_SIDE_EFFECTING),
    )(*parts, *lands, send_sems, recv_sems, *after)
    return list(outs[:n]), list(outs[n:])


def _sum_chips(name, own, recv, chip, after):
    _, r, hc = own.shape

    def body(chip_ref, own_ref, recv_ref, after_ref, out_ref):
        acc = own_ref[...].astype(F32)
        for j in range(3):
            acc = acc + recv_ref[j].astype(F32)
        out_ref[...] = acc

    return pl.pallas_call(
        body, name=name,
        grid_spec=pltpu.PrefetchScalarGridSpec(
            num_scalar_prefetch=1, grid=(hc // LANES,),
            in_specs=[pl.BlockSpec((None, r, LANES), lambda i, chip_ref: (chip_ref[0], 0, i)),
                      pl.BlockSpec((3, r, LANES), lambda i, chip_ref: (0, 0, i)),
                      pl.BlockSpec((8, LANES), lambda i, chip_ref: (0, 0))],
            out_specs=pl.BlockSpec((r, LANES), lambda i, chip_ref: (0, i))),
        out_shape=jax.ShapeDtypeStruct((r, hc), F32),
        compiler_params=_params(("arbitrary",)),
    )(chip, own, recv, after)


def _share_halves(name, halves):
    n = len(halves)

    def body(*refs):
        srcs, dsts = refs[:n], refs[n:2 * n]
        send_sems, recv_sems = refs[2 * n:]
        x, y, c, _ = _place()
        copies = [pltpu.make_async_remote_copy(
            src_ref=srcs[a], dst_ref=dsts[a], send_sem=send_sems.at[a], recv_sem=recv_sems.at[a],
            device_id=(x, y, 1 - c), device_id_type=MESH) for a in range(n)]
        for cp in copies:
            cp.start()
        for cp in copies:
            cp.wait()

    return pl.pallas_call(
        body, name=name,
        in_specs=[ANY] * n, out_specs=[ANY] * n,
        out_shape=[jax.ShapeDtypeStruct(s.shape, s.dtype) for s in halves],
        scratch_shapes=[pltpu.SemaphoreType.DMA((n,)), pltpu.SemaphoreType.DMA((n,))],
    )(*halves)


def _allreduce_small(part):
    rows = part.shape[0]

    def body(in_ref, out_ref, land, send_sems, recv_sems):
        x, y, c, _ = _place()
        me = 4 * x + 2 * y + c
        land[me] = in_ref[...]
        copies = []
        for d in range(1, N_DEV):
            peer = (1 - x if d & 4 else x, 1 - y if d & 2 else y, 1 - c if d & 1 else c)
            copies.append(pltpu.make_async_remote_copy(
                src_ref=in_ref, dst_ref=land.at[me],
                send_sem=send_sems.at[d - 1], recv_sem=recv_sems.at[d - 1],
                device_id=peer, device_id_type=MESH))
        for cp in copies:
            cp.start()
        for d in range(1, N_DEV):
            px, py, pc = (1 - x if d & 4 else x, 1 - y if d & 2 else y, 1 - c if d & 1 else c)
            pltpu.make_async_remote_copy(
                src_ref=in_ref, dst_ref=land.at[4 * px + 2 * py + pc],
                send_sem=send_sems.at[d - 1], recv_sem=recv_sems.at[d - 1],
                device_id=(px, py, pc), device_id_type=MESH).wait_recv()
        for cp in copies:
            cp.wait_send()
        acc = land[0]
        for k in range(1, N_DEV):
            acc = acc + land[k]
        out_ref[...] = acc

    vmem = pl.BlockSpec(memory_space=pltpu.VMEM)
    return pl.pallas_call(
        body, name="allreduce_small",
        in_specs=[vmem], out_specs=vmem,
        out_shape=jax.ShapeDtypeStruct(part.shape, F32),
        scratch_shapes=[pltpu.VMEM((N_DEV, rows, LANES), F32),
                        pltpu.SemaphoreType.DMA((N_DEV - 1,)), pltpu.SemaphoreType.DMA((N_DEV - 1,))],
    )(part)


def _adam_update(w, g, m, v):
    nm = ADAM_B1 * m + (1.0 - ADAM_B1) * g
    nv = ADAM_B2 * v + (1.0 - ADAM_B2) * (g * g)
    m_hat = nm * (1.0 / (1.0 - ADAM_B1 ** ADAM_STEP))
    v_hat = nv * (1.0 / (1.0 - ADAM_B2 ** ADAM_STEP))
    return -ADAM_LR * (m_hat / (jnp.sqrt(v_hat) + ADAM_EPS) + ADAM_WD * w), nm, nv


def _adamw(name, w, g, m, v):
    def body(w_ref, g_ref, m_ref, v_ref, d_ref, nm_ref, nv_ref):
        d_ref[...], nm_ref[...], nv_ref[...] = _adam_update(w_ref[...], g_ref[...], m_ref[...], v_ref[...])

    spec = pl.BlockSpec(w.shape, lambda i: (0, 0))
    out = jax.ShapeDtypeStruct(w.shape, F32)
    return pl.pallas_call(
        body, name=name, grid=(1,),
        in_specs=[spec] * 4, out_specs=[spec] * 3, out_shape=[out] * 3,
        compiler_params=_params(("arbitrary",)),
    )(w, g, m, v)


def _adamw_halves(name, w, mine, theirs, m, v, core):
    rows, cols = w.shape
    hc = cols // 2
    tc = min(256, hc)
    nt = hc // tc

    def body(core_ref, w_ref, mine_ref, theirs_ref, m_ref, v_ref, g_ref, d_ref, nm_ref, nv_ref):
        gv = jnp.where(pl.program_id(0) == core_ref[0], mine_ref[...], theirs_ref[...])
        g_ref[...] = gv
        d_ref[...], nm_ref[...], nv_ref[...] = _adam_update(w_ref[...], gv, m_ref[...], v_ref[...])

    whole = pl.BlockSpec((rows, tc), lambda h, i, core_ref: (0, h * nt + i))
    mine_spec = pl.BlockSpec((rows, tc), lambda h, i, core_ref: (0, jnp.where(h == core_ref[0], i, 0)))
    theirs_spec = pl.BlockSpec((rows, tc), lambda h, i, core_ref: (0, jnp.where(h == core_ref[0], 0, i)))
    out = jax.ShapeDtypeStruct((rows, cols), F32)
    return pl.pallas_call(
        body, name=name,
        grid_spec=pltpu.PrefetchScalarGridSpec(
            num_scalar_prefetch=1, grid=(2, nt),
            in_specs=[whole, mine_spec, theirs_spec, whole, whole], out_specs=[whole] * 4),
        out_shape=[out] * 4,
        compiler_params=_params(("arbitrary", "arbitrary")),
    )(core, w, mine, theirs, m, v)


WEIGHTS = ("ffn1_norm", "ffn1_gate", "ffn1_up", "ffn1_down", "mix_norm", "w_in", "b_forget", "conv_w",
           "w_o_attn", "w_o_conv", "w_out", "ffn2_norm", "ffn2_gate", "ffn2_up", "ffn2_down", "final_norm")
VEC_ROWS = 8


def _pack_small(t, conv_rows):
    conv = t["conv_w"]
    parts = [t[n].reshape(VEC_ROWS, LANES) for n in NORMS]
    parts.append(jnp.pad(conv, ((0, conv_rows - conv.shape[0]), (0, 0))))
    parts.append(jnp.pad(t["b_forget"].reshape(1, N_HEADS), ((0, 7), (0, LANES - N_HEADS))))
    return jnp.concatenate(parts, axis=0)


def _unpack_small(p, conv_rows):
    out = {n: p[VEC_ROWS * i:VEC_ROWS * (i + 1)].reshape(-1) for i, n in enumerate(NORMS)}
    base = VEC_ROWS * len(NORMS)
    out["conv_w"] = p[base:base + 3]
    out["b_forget"] = p[base + conv_rows, :N_HEADS]
    return out


def _travel(name, a):
    return a.T if name in TRANSPOSED else a


GATHER_RIDES = {"ffn1_fwd": ("w_in", "w_o_attn", "w_o_conv", "w_out"), "attn_fwd": ("ffn2_gate", "ffn2_up", "ffn2_down")}
SIBLING_RIDES = {"ffn2": "mix_out_bwd", "out": None, "w_in": "mix_proj_bwd_dx", "ffn1": None}
CHIP_RIDES = {"ffn2": "attn_bwd", "out": "attn_bwd", "w_in": "ffn1_bwd_dw", "ffn1": None}


class _MeshPlan(_LocalPlan):
    def __init__(self, wts, core):
        self.small, self.core = wts, core
        self.shards = {n: wts[n].astype(BF16) for n in BIG}
        self.chip_part, self.from_chips, self.rides = {}, {}, {}
        first = ("ffn1_gate", "ffn1_up", "ffn1_down")
        conv_shard = jnp.pad(wts["conv_w"], ((0, 8 - wts["conv_w"].shape[0]), (0, 0)))
        own = [self.shards[n] for n in first] + [conv_shard]
        got = _run_comm("gather_ffn1", _gather_comm(own[:-1], conv_shard))
        self.stacks = dict(zip(first + ("conv_w",), _fill_own(got, own)))
        for kernel_name, names in GATHER_RIDES.items():
            mine = [self.shards[n] for n in names]
            self._ride(kernel_name, _gather_comm(mine),
                       lambda got, names=names, mine=mine: self.stacks.update(zip(names, _fill_own(got, mine))))

    def _ride(self, kernel_name, comm, then):
        self.rides.setdefault(kernel_name, []).append((comm, then))

    def rider(self, kernel_name):
        comms = [comm for comm, _ in self.rides.get(kernel_name, [])]
        return _merge_comms(comms) if comms else None

    def arrived(self, kernel_name, results):
        for comm, then in self.rides.pop(kernel_name, []):
            then(results[:len(comm.out_shape)])
            results = results[len(comm.out_shape):]

    def reduce(self, group, grads):
        names = tuple(grads)
        mine = [grads[n] for n in names]

        def with_sibling(from_sibling):
            parts = [_add_halves("add_halves_" + n, g, r, self.core) for n, g, r in zip(names, mine, from_sibling)]
            self.chip_part.update(zip(names, parts))
            if CHIP_RIDES[group] is None:
                self.last = (names, _exchange_start("exchange_start_" + group, parts))
            else:
                self._ride(CHIP_RIDES[group], _chip_exchange_comm(parts),
                           lambda got: self.from_chips.update(zip(names, got)))

        if SIBLING_RIDES[group] is None:
            with_sibling(_run_comm("sibling_exchange_" + group, _sibling_exchange_comm(mine)))
        else:
            self._ride(SIBLING_RIDES[group], _sibling_exchange_comm(mine), with_sibling)


def kernel(x, ffn1_norm, ffn1_gate, ffn1_up, ffn1_down, mix_norm, w_in, b_forget, conv_w, w_o_attn, w_o_conv, w_out, ffn2_norm, ffn2_gate, ffn2_up, ffn2_down, final_norm, loss_target, m_ffn1_norm, m_ffn1_gate, m_ffn1_up, m_ffn1_down, m_mix_norm, m_w_in, m_b_forget, m_conv_w, m_w_o_attn, m_w_o_conv, m_w_out, m_ffn2_norm, m_ffn2_gate, m_ffn2_up, m_ffn2_down, m_final_norm, v_ffn1_norm, v_ffn1_gate, v_ffn1_up, v_ffn1_down, v_mix_norm, v_w_in, v_b_forget, v_conv_w, v_w_o_attn, v_w_o_conv, v_w_out, v_ffn2_norm, v_ffn2_gate, v_ffn2_up, v_ffn2_down, v_final_norm):
    given = dict(locals())
    wts = {n: _travel(n, given[n]) for n in WEIGHTS}
    mom = {n: _travel(n, given["m_" + n]) for n in WEIGHTS}
    var = {n: _travel(n, given["v_" + n]) for n in WEIGHTS}
    B, S, D = x.shape
    chip = 2 * lax.axis_index("x") + lax.axis_index("y")
    chip1 = chip.astype(jnp.int32).reshape(1)
    core = lax.axis_index("c").astype(jnp.int32).reshape(1)

    plan = _MeshPlan(wts, core)
    loss, grad_x, gs = _local_step(x.reshape(B * S, D), loss_target.reshape(B * S, D), plan, B, S)

    last_names, (send_sems, recv_sems, parts_thru, lands, token) = plan.last
    delta, new_m, new_v, grads = {}, {}, {}, {}

    def finish(tag, names):
        mine = [_sum_chips("sum_chips_" + n, plan.chip_part[n], plan.from_chips[n], chip1, token) for n in names]
        theirs = _share_halves("share_halves_" + tag, mine)
        raw = []
        for n, gm, gt in zip(names, mine, theirs):
            outs = _adamw_halves("adamw_" + n, wts[n], gm, gt, mom[n], var[n], core)
            raw.append(outs[-1])
            grads[n], delta[n], new_m[n], new_v[n] = [_travel(n, o) for o in outs]
        return raw

    done = finish("early", [n for n in BIG if n not in last_names])
    parts_back, got = _exchange_wait("exchange_wait", send_sems, recv_sems, parts_thru, lands, done)
    plan.chip_part.update(zip(last_names, parts_back))
    plan.from_chips.update(zip(last_names, got))
    finish("last", last_names)

    conv_all = _shard_cols(gs["conv_w"]).reshape(N_CHIPS * 8, LANES)
    small_part = _pack_small({**{n: gs[n] for n in NORMS}, "conv_w": conv_all, "b_forget": gs["b_forget"][0, :N_HEADS]},
                             N_CHIPS * 8)
    base = VEC_ROWS * len(NORMS)
    small_sum = _allreduce_small(small_part)
    grads.update(_unpack_small(small_sum, N_CHIPS * 8))
    grads["conv_w"] = lax.dynamic_slice_in_dim(small_sum[base:base + N_CHIPS * 8], chip * 8, 8, axis=0)[:3]
    packs = [_pack_small(t, 8) for t in (wts, grads, mom, var)]
    for out, p in zip((delta, new_m, new_v), _adamw("adamw_small", *packs)):
        out.update(_unpack_small(p, 8))

    total = lax.psum(loss[0, 0], ("x", "y", "c"))
    return (total, grad_x.reshape(B, S, D), *[grads[n] for n in WEIGHTS], *[delta[n] for n in WEIGHTS],
            *[new_m[n] for n in WEIGHTS], *[new_v[n] for n in WEIGHTS])
```

```python
import functools
import math

import jax
import jax.numpy as jnp
from jax import lax
from jax.experimental import pallas as pl
from jax.experimental.pallas import tpu as pltpu

F32 = jnp.float32
BF16 = jnp.bfloat16
MESH = pl.DeviceIdType.MESH

N_CHIPS = 4
N_DEV = 8
N_HEADS = 8
HEAD_DIM = 64
HEAD_PAIRS = N_HEADS // 2
ATTN_W = N_HEADS * HEAD_DIM
CONV_W = 512
RMS_EPS = 1e-6
FFN_RES = 0.5
LANES = 128
VMEM_LIMIT = 56 * 1024 * 1024
ROW_BLOCK = 256

ADAM_LR = 0.001
ADAM_B1 = 0.9
ADAM_B2 = 0.999
ADAM_EPS = 1e-08
ADAM_WD = 0.01
ADAM_STEP = 10

PROJ_W = 3 * ATTN_W + 3 * CONV_W + 2 * 1024
COL_CB, COL_CC, COL_CX = 3 * ATTN_W, 3 * ATTN_W + CONV_W, 3 * ATTN_W + 2 * CONV_W
COL_GATES = 3 * ATTN_W + 3 * CONV_W
N_FORGET_COL = 3 * ATTN_W


def _params(sem=None, vmem=VMEM_LIMIT):
    return pltpu.CompilerParams(dimension_semantics=sem, vmem_limit_bytes=vmem)


def _dot(a, b):
    return lax.dot_general(a, b, (((1,), (0,)), ((), ())), preferred_element_type=F32)


def _dot_nt(a, b):
    return lax.dot_general(a, b, (((1,), (1,)), ((), ())), preferred_element_type=F32)


def _dot_tn(a, b):
    return lax.dot_general(a, b, (((0,), (0,)), ((), ())), preferred_element_type=F32)


def _sigmoid(x):
    return 1.0 / (1.0 + jnp.exp(-x))


def _rms(xv):
    inv = lax.rsqrt(jnp.mean(xv * xv, axis=-1, keepdims=True) + RMS_EPS)
    return xv * inv, inv


class _Comm:
    def __init__(self, inputs, out_shape, scratch, start, finish):
        self.inputs, self.out_shape, self.scratch = list(inputs), list(out_shape), list(scratch)
        self.start, self.finish = start, finish


def _pallas(body, name, grid, in_specs, out_specs, out_shape, scratch, args, comm=None):
    sem = ("arbitrary",) * len(grid)
    if comm is None:
        outs = pl.pallas_call(body, name=name, grid=grid, in_specs=in_specs, out_specs=out_specs,
                              out_shape=out_shape, scratch_shapes=scratch, compiler_params=_params(sem))(*args)
        return list(outs), []
    n_in, n_out, n_scr = len(in_specs), len(out_specs), len(scratch)
    ci, co = len(comm.inputs), len(comm.out_shape)

    def riding(*refs):
        ins, refs = refs[:n_in], refs[n_in:]
        cins, refs = refs[:ci], refs[ci:]
        outs, refs = refs[:n_out], refs[n_out:]
        couts, refs = refs[:co], refs[co:]
        scr, sems = refs[:n_scr], refs[n_scr:]
        ids = [pl.program_id(d) for d in range(len(grid))]
        first = functools.reduce(lambda a, b: a & b, [i == 0 for i in ids])
        last = functools.reduce(lambda a, b: a & b, [i == g - 1 for i, g in zip(ids, grid)])

        @pl.when(first)
        def _():
            comm.start(cins, couts, sems)

        body(*ins, *outs, *scr)

        @pl.when(last)
        def _():
            comm.finish(cins, couts, sems)

    any_spec = pl.BlockSpec(memory_space=pl.ANY)
    outs = pl.pallas_call(
        riding, name=name, grid=grid,
        in_specs=list(in_specs) + [any_spec] * ci, out_specs=list(out_specs) + [any_spec] * co,
        out_shape=list(out_shape) + comm.out_shape, scratch_shapes=list(scratch) + comm.scratch,
        compiler_params=_params(sem))(*args, *comm.inputs)
    return list(outs[:n_out]), list(outs[n_out:])


def _rms_bwd(dn, xhat, inv, g):
    dxhat = dn * g
    dx = inv * (dxhat - xhat * jnp.mean(dxhat * xhat, axis=-1, keepdims=True))
    return dx, jnp.sum(dn * xhat, axis=0, keepdims=True)


def _ffn_fwd(name, x, g, wgt, wut, wd, tm, comm=None):
    T, D = x.shape
    K, Fs, _ = wgt.shape

    def body(x_ref, g_ref, wg_ref, wu_ref, wd_ref, out_ref, hg_ref, hu_ref, n_scr, acc_scr):
        k = pl.program_id(1)

        @pl.when(k == 0)
        def _():
            xhat, _ = _rms(x_ref[...])
            n_scr[...] = (xhat * g_ref[...]).astype(BF16)
            acc_scr[...] = jnp.zeros_like(acc_scr)

        n = n_scr[...]
        hg = _dot_nt(n, wg_ref[...])
        hu = _dot_nt(n, wu_ref[...])
        hg_ref[...] = hg.astype(BF16)
        hu_ref[...] = hu.astype(BF16)
        act = (hg * _sigmoid(hg) * hu).astype(BF16)
        acc_scr[...] += _dot(act, wd_ref[...])

        @pl.when(k == K - 1)
        def _():
            out_ref[...] = x_ref[...] + FFN_RES * acc_scr[...]

    w_spec = pl.BlockSpec((None, Fs, D), lambda i, k: (k, 0, 0))
    act_spec = pl.BlockSpec((None, tm, Fs), lambda i, k: (k, i, 0))
    return _pallas(
        body, name, (T // tm, K),
        [pl.BlockSpec((tm, D), lambda i, k: (i, 0)), pl.BlockSpec((1, D), lambda i, k: (0, 0)),
         w_spec, w_spec, w_spec],
        [pl.BlockSpec((tm, D), lambda i, k: (i, 0)), act_spec, act_spec],
        [jax.ShapeDtypeStruct((T, D), F32), jax.ShapeDtypeStruct((K, T, Fs), BF16),
         jax.ShapeDtypeStruct((K, T, Fs), BF16)],
        [pltpu.VMEM((tm, D), BF16), pltpu.VMEM((tm, D), F32)],
        (x, g, wgt, wut, wd), comm)


def _ffn_up(name, x, g, wgt, wut, tm, comm=None):
    T, D = x.shape
    K, Fs, _ = wgt.shape

    def body(x_ref, g_ref, wg_ref, wu_ref, hg_ref, hu_ref, n_scr):
        @pl.when(pl.program_id(1) == 0)
        def _():
            xhat, _ = _rms(x_ref[...])
            n_scr[...] = (xhat * g_ref[...]).astype(BF16)

        n = n_scr[...]
        hg_ref[...] = _dot_nt(n, wg_ref[...]).astype(BF16)
        hu_ref[...] = _dot_nt(n, wu_ref[...]).astype(BF16)

    w_spec = pl.BlockSpec((None, Fs, D), lambda i, k: (k, 0, 0))
    act_spec = pl.BlockSpec((None, tm, Fs), lambda i, k: (k, i, 0))
    return _pallas(
        body, name, (T // tm, K),
        [pl.BlockSpec((tm, D), lambda i, k: (i, 0)), pl.BlockSpec((1, D), lambda i, k: (0, 0)), w_spec, w_spec],
        [act_spec, act_spec],
        [jax.ShapeDtypeStruct((K, T, Fs), BF16), jax.ShapeDtypeStruct((K, T, Fs), BF16)],
        [pltpu.VMEM((tm, D), BF16)],
        (x, g, wgt, wut), comm)


def _ffn_down(name, x, hg, hu, wd, tm, comm=None):
    T, D = x.shape
    K, Fs, _ = wd.shape

    def body(x_ref, hg_ref, hu_ref, wd_ref, out_ref, acc_scr):
        k = pl.program_id(1)

        @pl.when(k == 0)
        def _():
            acc_scr[...] = jnp.zeros_like(acc_scr)

        hgv = hg_ref[...].astype(F32)
        act = (hgv * _sigmoid(hgv) * hu_ref[...].astype(F32)).astype(BF16)
        acc_scr[...] += _dot(act, wd_ref[...])

        @pl.when(k == K - 1)
        def _():
            out_ref[...] = x_ref[...] + FFN_RES * acc_scr[...]

    act_spec = pl.BlockSpec((None, tm, Fs), lambda i, k: (k, i, 0))
    row = pl.BlockSpec((tm, D), lambda i, k: (i, 0))
    return _pallas(
        body, name, (T // tm, K),
        [row, act_spec, act_spec, pl.BlockSpec((None, Fs, D), lambda i, k: (k, 0, 0))],
        [row], [jax.ShapeDtypeStruct((T, D), F32)], [pltpu.VMEM((tm, D), F32)],
        (x, hg, hu, wd), comm)


def _ffn_bwd_dx(name, dout, x, g, hg, hu, wgt, wut, wd, tm, comm=None):
    T, D = x.shape
    K, Fs, _ = wgt.shape

    def body(dout_ref, x_ref, g_ref, hg_ref, hu_ref, wg_ref, wu_ref, wd_ref,
             dx_ref, dhg_ref, dhu_ref, dg_ref, df_scr, dn_scr):
        i, k = pl.program_id(0), pl.program_id(1)

        @pl.when(k == 0)
        def _():
            df_scr[...] = (FFN_RES * dout_ref[...]).astype(BF16)
            dn_scr[...] = jnp.zeros_like(dn_scr)

        @pl.when((k == 0) & (i == 0))
        def _():
            dg_ref[...] = jnp.zeros_like(dg_ref)

        for r0 in range(0, tm, ROW_BLOCK):
            rows = slice(r0, r0 + ROW_BLOCK)
            dact = _dot_nt(df_scr[rows, :], wd_ref[...])
            hgv = hg_ref[rows, :].astype(F32)
            huv = hu_ref[rows, :].astype(F32)
            s = _sigmoid(hgv)
            dhu = (dact * (hgv * s)).astype(BF16)
            dhg = (dact * huv * (s * (1.0 + hgv * (1.0 - s)))).astype(BF16)
            dhg_ref[rows, :] = dhg
            dhu_ref[rows, :] = dhu
            dn_scr[rows, :] += _dot(dhg, wg_ref[...]) + _dot(dhu, wu_ref[...])

        @pl.when(k == K - 1)
        def _():
            xhat, inv = _rms(x_ref[...])
            dx, dg = _rms_bwd(dn_scr[...], xhat, inv, g_ref[...])
            dx_ref[...] = dout_ref[...] + dx
            dg_ref[...] += dg

    w_spec = pl.BlockSpec((None, Fs, D), lambda i, k: (k, 0, 0))
    act_spec = pl.BlockSpec((None, tm, Fs), lambda i, k: (k, i, 0))
    row = pl.BlockSpec((tm, D), lambda i, k: (i, 0))
    vec = pl.BlockSpec((1, D), lambda i, k: (0, 0))
    return _pallas(
        body, name, (T // tm, K),
        [row, row, vec, act_spec, act_spec, w_spec, w_spec, w_spec],
        [row, act_spec, act_spec, vec],
        [jax.ShapeDtypeStruct((T, D), F32), jax.ShapeDtypeStruct((K, T, Fs), BF16),
         jax.ShapeDtypeStruct((K, T, Fs), BF16), jax.ShapeDtypeStruct((1, D), F32)],
        [pltpu.VMEM((tm, D), BF16), pltpu.VMEM((tm, D), F32)],
        (dout, x, g, hg, hu, wgt, wut, wd), comm)


def _ffn_bwd_dw(name, dout, x, g, hg, hu, dhg, dhu, tk, comm=None):
    T, D = x.shape
    K, _, Fs = hg.shape
    nt = T // tk

    def body(dout_ref, x_ref, g_ref, hg_ref, hu_ref, dhg_ref, dhu_ref,
             dwg_ref, dwu_ref, dwd_ref, accg, accu, accd):
        t = pl.program_id(1)

        @pl.when(t == 0)
        def _():
            accg[...] = jnp.zeros_like(accg)
            accu[...] = jnp.zeros_like(accu)
            accd[...] = jnp.zeros_like(accd)

        xhat, _ = _rms(x_ref[...])
        n = (xhat * g_ref[...]).astype(BF16)
        df = (FFN_RES * dout_ref[...]).astype(BF16)
        hgv = hg_ref[...].astype(F32)
        act = (hgv * _sigmoid(hgv) * hu_ref[...].astype(F32)).astype(BF16)
        accg[...] += _dot_tn(dhg_ref[...], n)
        accu[...] += _dot_tn(dhu_ref[...], n)
        accd[...] += _dot_tn(act, df)

        @pl.when(t == nt - 1)
        def _():
            dwg_ref[...] = accg[...].astype(BF16)
            dwu_ref[...] = accu[...].astype(BF16)
            dwd_ref[...] = accd[...].astype(BF16)

    act_spec = pl.BlockSpec((None, tk, Fs), lambda k, t: (k, t, 0))
    w_spec = pl.BlockSpec((None, Fs, D), lambda k, t: (k, 0, 0))
    return _pallas(
        body, name, (K, nt),
        [pl.BlockSpec((tk, D), lambda k, t: (t, 0)), pl.BlockSpec((tk, D), lambda k, t: (t, 0)),
         pl.BlockSpec((1, D), lambda k, t: (0, 0)), act_spec, act_spec, act_spec, act_spec],
        [w_spec, w_spec, w_spec],
        [jax.ShapeDtypeStruct((K, Fs, D), BF16)] * 3,
        [pltpu.VMEM((Fs, D), F32)] * 3,
        (dout, x, g, hg, hu, dhg, dhu), comm)


def _mix_proj_fwd(x, g, wproj_t, wf_t, tm, tn):
    T, D = x.shape
    N = wproj_t.shape[0]

    def body(x_ref, g_ref, w_ref, wf_ref, h_ref, proj_ref, flog_ref, h_scr):
        @pl.when(pl.program_id(1) == 0)
        def _():
            xhat, _ = _rms(x_ref[...])
            h = (xhat * g_ref[...]).astype(BF16)
            h_scr[...] = h
            h_ref[...] = h
            flog_ref[...] = _dot_nt(h, wf_ref[...])

        proj_ref[...] = _dot_nt(h_scr[...], w_ref[...]).astype(BF16)

    return pl.pallas_call(
        body, name="mix_proj_fwd", grid=(T // tm, N // tn),
        in_specs=[pl.BlockSpec((tm, D), lambda i, n: (i, 0)),
                  pl.BlockSpec((1, D), lambda i, n: (0, 0)),
                  pl.BlockSpec((tn, D), lambda i, n: (n, 0)),
                  pl.BlockSpec((LANES, D), lambda i, n: (0, 0))],
        out_specs=[pl.BlockSpec((tm, D), lambda i, n: (i, 0)),
                   pl.BlockSpec((tm, tn), lambda i, n: (i, n)),
                   pl.BlockSpec((tm, LANES), lambda i, n: (i, 0))],
        out_shape=[jax.ShapeDtypeStruct((T, D), BF16),
                   jax.ShapeDtypeStruct((T, N), BF16),
                   jax.ShapeDtypeStruct((T, LANES), F32)],
        scratch_shapes=[pltpu.VMEM((tm, D), BF16)],
        compiler_params=_params(("arbitrary", "arbitrary")),
    )(x, g, wproj_t, wf_t)


def _log_sigmoid(z):
    return -(jnp.maximum(-z, 0.0) + jnp.log(1.0 + jnp.exp(-jnp.abs(z))))


def _tri(n, lower):
    r = lax.broadcasted_iota(jnp.int32, (n, n), 0)
    c = lax.broadcasted_iota(jnp.int32, (n, n), 1)
    return jnp.where((r >= c) if lower else (r <= c), 1.0, 0.0).astype(F32)


def _dot_f32(a, b):
    return lax.dot_general(a, b, (((1,), (0,)), ((), ())), preferred_element_type=F32,
                           precision=lax.Precision.HIGHEST)


def _fgate_fwd(flog, bias, B, S, ch):
    def body(flog_ref, b_ref, cum_ref):
        tri = _tri(ch, True)
        carry = jnp.zeros((1, LANES), F32)
        for c0 in range(0, S, ch):
            lf = _log_sigmoid(flog_ref[c0:c0 + ch, :] + b_ref[...])
            cs = _dot_f32(tri, lf) + carry
            cum_ref[c0:c0 + ch, :] = cs
            carry = cs[ch - 1:ch, :]

    return pl.pallas_call(
        body, name="fgate_fwd", grid=(B,),
        in_specs=[pl.BlockSpec((S, LANES), lambda b: (b, 0)),
                  pl.BlockSpec((1, LANES), lambda b: (0, 0))],
        out_specs=pl.BlockSpec((S, LANES), lambda b: (b, 0)),
        out_shape=jax.ShapeDtypeStruct((B * S, LANES), F32),
        compiler_params=_params(("arbitrary",)),
    )(flog, bias)


def _fgate_bwd(dcum, flog, bias, B, S, ch):
    def body(dcum_ref, flog_ref, b_ref, dflog_ref, db_ref):
        @pl.when(pl.program_id(0) == 0)
        def _():
            db_ref[...] = jnp.zeros_like(db_ref)

        tri = _tri(ch, False)
        carry = jnp.zeros((1, LANES), F32)
        db = jnp.zeros((1, LANES), F32)
        for c0 in range(S - ch, -1, -ch):
            dlf = _dot_f32(tri, dcum_ref[c0:c0 + ch, :]) + carry
            carry = dlf[0:1, :]
            z = flog_ref[c0:c0 + ch, :] + b_ref[...]
            dz = dlf * _sigmoid(-z)
            dflog_ref[c0:c0 + ch, :] = dz
            db = db + jnp.sum(dz, axis=0, keepdims=True)
        db_ref[...] += db

    return pl.pallas_call(
        body, name="fgate_bwd", grid=(B,),
        in_specs=[pl.BlockSpec((S, LANES), lambda b: (b, 0)),
                  pl.BlockSpec((S, LANES), lambda b: (b, 0)),
                  pl.BlockSpec((1, LANES), lambda b: (0, 0))],
        out_specs=[pl.BlockSpec((S, LANES), lambda b: (b, 0)),
                   pl.BlockSpec((1, LANES), lambda b: (0, 0))],
        out_shape=[jax.ShapeDtypeStruct((B * S, LANES), F32),
                   jax.ShapeDtypeStruct((1, LANES), F32)],
        compiler_params=_params(("arbitrary",)),
    )(dcum, flog, bias)


def _pick_lane(tile, h):
    lane = lax.broadcasted_iota(jnp.int32, tile.shape, 1)
    return jnp.sum(jnp.where(lane == h, tile, 0.0), axis=1, keepdims=True)


def _put_lane(col, h, width=LANES):
    lane = lax.broadcasted_iota(jnp.int32, (col.shape[0], width), 1)
    return jnp.where(lane == h, col, 0.0)


def _pick_row(tile, h):
    row = lax.broadcasted_iota(jnp.int32, tile.shape, 0)
    return jnp.sum(jnp.where(row == h, tile, 0.0), axis=0, keepdims=True)


def _put_row(vec, h):
    row = lax.broadcasted_iota(jnp.int32, (8, vec.shape[1]), 0)
    return jnp.where(row == h, vec, 0.0)


def _causal(tq):
    r = lax.broadcasted_iota(jnp.int32, (tq, tq), 0)
    c = lax.broadcasted_iota(jnp.int32, (tq, tq), 1)
    return r >= c


def _head_halves(t):
    lo = lax.broadcasted_iota(jnp.int32, t.shape, 1) < HEAD_DIM
    zero = jnp.zeros_like(t)
    return jnp.where(lo, t, zero), jnp.where(lo, zero, t)


NEG = -1e30
ATTN_SCALE = 1.0 / math.sqrt(HEAD_DIM)


def _scaled(q):
    return (q.astype(F32) * ATTN_SCALE).astype(q.dtype)


def _attn_fwd(proj, cum, cum_t, B, S, tq, comm=None):
    nq = S // tq

    def body(q_ref, k_ref, v_ref, cum_ref, cumt_ref, o_ref, lse_ref):
        qi, hp = pl.program_id(1), pl.program_id(2)
        qm = _head_halves(_scaled(q_ref[...]))
        cumv = cum_ref[...]
        cq = [_pick_lane(cumv, 2 * hp + e) for e in range(2)]

        def tile(j, carry, masked):
            off = pl.multiple_of(j * tq, tq)
            kj = k_ref[pl.ds(off, tq), :]
            vj = v_ref[pl.ds(off, tq), :]
            ct = cumt_ref[j]
            new = []
            for e in range(2):
                m, l, acc = carry[e]
                s = _dot_nt(qm[e], kj) + (cq[e] - _pick_row(ct, 2 * hp + e))
                if masked:
                    s = jnp.where(_causal(tq), s, NEG)
                m_new = jnp.maximum(m, jnp.max(s, axis=1, keepdims=True))
                p = jnp.exp(s - m_new)
                alpha = jnp.exp(m - m_new)
                l = alpha * l + jnp.sum(p, axis=1, keepdims=True)
                acc = alpha * acc + _dot(p.astype(BF16), vj)
                new.append((m_new, l, acc))
            return tuple(new)

        one = (jnp.full((tq, 1), NEG, F32), jnp.zeros((tq, 1), F32), jnp.zeros((tq, LANES), F32))
        carry = lax.fori_loop(0, qi, lambda j, c: tile(j, c, False), (one, one))
        (ma, la, acca), (mb, lb, accb) = tile(qi, carry, True)
        lo = lax.broadcasted_iota(jnp.int32, (tq, LANES), 1) < HEAD_DIM
        o_ref[...] = jnp.where(lo, acca / la, accb / lb).astype(BF16)

        @pl.when(hp == 0)
        def _():
            lse_ref[...] = jnp.zeros_like(lse_ref)

        lse_ref[...] += _put_lane(ma + jnp.log(la), 2 * hp) + _put_lane(mb + jnp.log(lb), 2 * hp + 1)

    kv = lambda first: pl.BlockSpec((S, LANES), lambda b, i, hp: (b, first + hp))
    return _pallas(
        body, "attn_fwd", (B, nq, HEAD_PAIRS),
        [pl.BlockSpec((tq, LANES), lambda b, i, hp: (b * nq + i, hp)),
         kv(ATTN_W // LANES), kv(2 * ATTN_W // LANES),
         pl.BlockSpec((tq, LANES), lambda b, i, hp: (b * nq + i, 0)),
         pl.BlockSpec((None, nq, 8, tq), lambda b, i, hp: (b, 0, 0, 0))],
        [pl.BlockSpec((tq, LANES), lambda b, i, hp: (b * nq + i, hp)),
         pl.BlockSpec((tq, LANES), lambda b, i, hp: (b * nq + i, 0))],
        [jax.ShapeDtypeStruct((B * S, ATTN_W), BF16), jax.ShapeDtypeStruct((B * S, LANES), F32)],
        [], (proj, proj, proj, cum, cum_t), comm)


def _attn_bwd(proj, o, do, lse, cum, cum_t, B, S, tq, comm=None):
    nq = S // tq

    def body(q_ref, k_ref, v_ref, o_ref, do_ref, lse_ref, cum_ref, cumt_ref,
             dq_ref, dk_ref, dv_ref, dcq_ref, dck_ref, dq_scr):
        hp, kj = pl.program_id(1), pl.program_id(2)

        @pl.when(kj == 0)
        def _():
            dq_scr[...] = jnp.zeros_like(dq_scr)

        @pl.when((kj == 0) & (hp == 0))
        def _():
            dcq_ref[...] = jnp.zeros_like(dcq_ref)
            dck_ref[...] = jnp.zeros_like(dck_ref)

        kv = k_ref[...]
        vv = v_ref[...]
        km = _head_halves(kv)
        ct = cumt_ref[...]
        ck = [_pick_row(ct, 2 * hp + e) for e in range(2)]

        def tile(i, carry, masked):
            dk, dv, dcol = carry
            off = pl.multiple_of(i * tq, tq)
            qi = q_ref[pl.ds(off, tq), :]
            ov = o_ref[pl.ds(off, tq), :].astype(F32)
            qm = _head_halves(_scaled(qi))
            dom = _head_halves(do_ref[pl.ds(off, tq), :])
            cumv = cum_ref[pl.ds(off, tq), :]
            lsev = lse_ref[pl.ds(off, tq), :]
            dcq = jnp.zeros((tq, LANES), F32)
            dq = jnp.zeros((tq, LANES), F32)
            dcol_new = []
            for e in range(2):
                delta = jnp.sum(dom[e].astype(F32) * ov, axis=1, keepdims=True)
                row_term = _pick_lane(cumv, 2 * hp + e) - _pick_lane(lsev, 2 * hp + e)
                p = jnp.exp(_dot_nt(qm[e], kv) + row_term - ck[e])
                if masked:
                    p = jnp.where(_causal(tq), p, 0.0)
                dv = dv + _dot_tn(p.astype(BF16), dom[e])
                ds = p * (_dot_nt(dom[e], vv) - delta)
                dcol_new.append(dcol[e] + jnp.sum(ds, axis=0, keepdims=True))
                dcq = dcq + _put_lane(jnp.sum(ds, axis=1, keepdims=True), 2 * hp + e)
                dsb = ds.astype(BF16)
                dk = dk + _dot_tn(dsb, qm[e])
                dq = dq + _dot(dsb, km[e]) * ATTN_SCALE
            dq_scr[pl.ds(off, tq), :] += dq
            dcq_ref[pl.ds(off, tq), :] += dcq
            return dk, dv, tuple(dcol_new)

        zero_row = jnp.zeros((1, tq), F32)
        init = (jnp.zeros((tq, LANES), F32), jnp.zeros((tq, LANES), F32), (zero_row, zero_row))
        carry = tile(kj, init, True)
        dk, dv, dcol = lax.fori_loop(kj + 1, nq, lambda i, c: tile(i, c, False), carry)
        dk_ref[...] = dk.astype(BF16)
        dv_ref[...] = dv.astype(BF16)
        dck_ref[kj] += -(_put_row(dcol[0], 2 * hp) + _put_row(dcol[1], 2 * hp + 1))

        @pl.when(kj == nq - 1)
        def _():
            dq_ref[...] = dq_scr[...].astype(BF16)

    seq = lambda first: pl.BlockSpec((S, LANES), lambda b, hp, j: (b, first + hp))
    tile_in = lambda first: pl.BlockSpec((tq, LANES), lambda b, hp, j: (b * nq + j, first + hp))
    lanes0 = pl.BlockSpec((S, LANES), lambda b, hp, j: (b, 0))
    out = jax.ShapeDtypeStruct((B * S, ATTN_W), BF16)
    return _pallas(
        body, "attn_bwd", (B, HEAD_PAIRS, nq),
        [seq(0), tile_in(ATTN_W // LANES), tile_in(2 * ATTN_W // LANES), seq(0), seq(0), lanes0, lanes0,
         pl.BlockSpec((None, None, 8, tq), lambda b, hp, j: (b, j, 0, 0))],
        [seq(0), tile_in(0), tile_in(0), lanes0,
         pl.BlockSpec((None, nq, 8, tq), lambda b, hp, j: (b, 0, 0, 0))],
        [out, out, out, jax.ShapeDtypeStruct((B * S, LANES), F32), jax.ShapeDtypeStruct((B, nq, 8, tq), F32)],
        [pltpu.VMEM((S, LANES), F32)],
        (proj, proj, proj, o, do, lse, cum, cum_t), comm)


def _shift_down(u, n):
    row = lax.broadcasted_iota(jnp.int32, u.shape, 0)
    return jnp.where(row >= n, pltpu.roll(u, n, 0), 0.0)


def _shift_up(u, n):
    rows = u.shape[0]
    row = lax.broadcasted_iota(jnp.int32, u.shape, 0)
    return jnp.where(row < rows - n, pltpu.roll(u, rows - n, 0), 0.0)


def _conv_specs(S):
    cb = pl.BlockSpec((S, LANES), lambda g, b: (b, COL_CB // LANES + g))
    cc = pl.BlockSpec((S, LANES), lambda g, b: (b, COL_CC // LANES + g))
    cx = pl.BlockSpec((S, LANES), lambda g, b: (b, COL_CX // LANES + g))
    w = pl.BlockSpec((8, LANES), lambda g, b: (0, g))
    return cb, cc, cx, w


def _conv_fwd(proj, conv_w, B, S):
    def body(cb_ref, cc_ref, cx_ref, w_ref, y_ref):
        u = cc_ref[...].astype(F32) * cx_ref[...].astype(F32)
        w = w_ref[...]
        conv = w[0:1, :] * _shift_down(u, 2) + w[1:2, :] * _shift_down(u, 1) + w[2:3, :] * u
        y_ref[...] = (cb_ref[...].astype(F32) * conv).astype(BF16)

    cb, cc, cx, w = _conv_specs(S)
    return pl.pallas_call(
        body, name="conv_fwd", grid=(CONV_W // LANES, B),
        in_specs=[cb, cc, cx, w],
        out_specs=pl.BlockSpec((S, LANES), lambda g, b: (b, g)),
        out_shape=jax.ShapeDtypeStruct((B * S, CONV_W), BF16),
        compiler_params=_params(("arbitrary", "arbitrary")),
    )(proj, proj, proj, conv_w)


def _conv_bwd(dy, proj, conv_w, B, S):
    def body(dy_ref, cb_ref, cc_ref, cx_ref, w_ref, dcb_ref, dcc_ref, dcx_ref, dw_ref):
        @pl.when(pl.program_id(1) == 0)
        def _():
            dw_ref[...] = jnp.zeros_like(dw_ref)

        ccv = cc_ref[...].astype(F32)
        cxv = cx_ref[...].astype(F32)
        u = ccv * cxv
        u1 = _shift_down(u, 1)
        u2 = _shift_down(u, 2)
        w = w_ref[...]
        conv = w[0:1, :] * u2 + w[1:2, :] * u1 + w[2:3, :] * u
        dyv = dy_ref[...].astype(F32)
        dcb_ref[...] = (dyv * conv).astype(BF16)
        dconv = dyv * cb_ref[...].astype(F32)
        du = w[2:3, :] * dconv + w[1:2, :] * _shift_up(dconv, 1) + w[0:1, :] * _shift_up(dconv, 2)
        dcc_ref[...] = (du * cxv).astype(BF16)
        dcx_ref[...] = (du * ccv).astype(BF16)
        row = lax.broadcasted_iota(jnp.int32, (8, LANES), 0)
        dw = jnp.where(row == 0, jnp.sum(dconv * u2, axis=0, keepdims=True),
                       jnp.where(row == 1, jnp.sum(dconv * u1, axis=0, keepdims=True),
                                 jnp.where(row == 2, jnp.sum(dconv * u, axis=0, keepdims=True), 0.0)))
        dw_ref[...] += dw

    cb, cc, cx, w = _conv_specs(S)
    out = pl.BlockSpec((S, LANES), lambda g, b: (b, g))
    return pl.pallas_call(
        body, name="conv_bwd", grid=(CONV_W // LANES, B),
        in_specs=[out, cb, cc, cx, w],
        out_specs=[out, out, out, w],
        out_shape=[jax.ShapeDtypeStruct((B * S, CONV_W), BF16)] * 3 + [jax.ShapeDtypeStruct((8, CONV_W), F32)],
        compiler_params=_params(("arbitrary", "arbitrary")),
    )(dy, proj, proj, proj, conv_w)


def _gate_specs(tm, D):
    ga = pl.BlockSpec((tm, D), lambda i: (i, COL_GATES // D))
    gc = pl.BlockSpec((tm, D), lambda i: (i, COL_GATES // D + 1))
    return ga, gc


def _mix_out_fwd(x, o, yc, proj, woa, woc, wout, tm):
    T, D = x.shape

    def body(x_ref, o_ref, yc_ref, ga_ref, gc_ref, woa_ref, woc_ref, wout_ref, out_ref):
        ya = _dot(o_ref[...], woa_ref[...])
        yp = _dot(yc_ref[...], woc_ref[...])
        merged = _sigmoid(ga_ref[...].astype(F32)) * ya + _sigmoid(gc_ref[...].astype(F32)) * yp
        out_ref[...] = x_ref[...] + _dot(merged.astype(BF16), wout_ref[...])

    ga, gc = _gate_specs(tm, D)
    row = lambda w: pl.BlockSpec((tm, w), lambda i: (i, 0))
    whole = lambda a: pl.BlockSpec(a.shape, lambda i: (0, 0))
    return pl.pallas_call(
        body, name="mix_out_fwd", grid=(T // tm,),
        in_specs=[row(D), row(ATTN_W), row(CONV_W), ga, gc, whole(woa), whole(woc), whole(wout)],
        out_specs=row(D),
        out_shape=jax.ShapeDtypeStruct((T, D), F32),
        compiler_params=_params(("arbitrary",)),
    )(x, o, yc, proj, proj, woa, woc, wout)


def _mix_out_bwd(dx, o, yc, proj, woa, woc, wout, tm, comm=None):
    T, D = dx.shape
    nt = T // tm

    def body(dx_ref, o_ref, yc_ref, ga_ref, gc_ref, woa_ref, woc_ref, wout_ref,
             do_ref, dyc_ref, dg_ref, dwoa_ref, dwoc_ref, dwout_ref, acca, accc, acco):
        t = pl.program_id(0)

        @pl.when(t == 0)
        def _():
            acca[...] = jnp.zeros_like(acca)
            accc[...] = jnp.zeros_like(accc)
            acco[...] = jnp.zeros_like(acco)

        dxb = dx_ref[...].astype(BF16)
        ov, ycv = o_ref[...], yc_ref[...]
        ya = _dot(ov, woa_ref[...])
        yp = _dot(ycv, woc_ref[...])
        sa = _sigmoid(ga_ref[...].astype(F32))
        sc = _sigmoid(gc_ref[...].astype(F32))
        merged = (sa * ya + sc * yp).astype(BF16)
        dm = _dot_nt(dxb, wout_ref[...])
        dya = (dm * sa).astype(BF16)
        dyp = (dm * sc).astype(BF16)
        dg_ref[:, :D] = (dm * ya * sa * (1.0 - sa)).astype(BF16)
        dg_ref[:, D:] = (dm * yp * sc * (1.0 - sc)).astype(BF16)
        do_ref[...] = _dot_nt(dya, woa_ref[...]).astype(BF16)
        dyc_ref[...] = _dot_nt(dyp, woc_ref[...]).astype(BF16)
        acca[...] += _dot_tn(ov, dya)
        accc[...] += _dot_tn(ycv, dyp)
        acco[...] += _dot_tn(merged, dxb)

        @pl.when(t == nt - 1)
        def _():
            dwoa_ref[...] = acca[...].astype(BF16)
            dwoc_ref[...] = accc[...].astype(BF16)
            dwout_ref[...] = acco[...].astype(BF16)

    ga, gc = _gate_specs(tm, D)
    row = lambda w: pl.BlockSpec((tm, w), lambda i: (i, 0))
    whole = lambda a: pl.BlockSpec(a.shape, lambda i: (0, 0))
    return _pallas(
        body, "mix_out_bwd", (nt,),
        [row(D), row(ATTN_W), row(CONV_W), ga, gc, whole(woa), whole(woc), whole(wout)],
        [row(ATTN_W), row(CONV_W), row(2 * D), whole(woa), whole(woc), whole(wout)],
        [jax.ShapeDtypeStruct((T, ATTN_W), BF16), jax.ShapeDtypeStruct((T, CONV_W), BF16),
         jax.ShapeDtypeStruct((T, 2 * D), BF16),
         jax.ShapeDtypeStruct(woa.shape, BF16), jax.ShapeDtypeStruct(woc.shape, BF16),
         jax.ShapeDtypeStruct(wout.shape, BF16)],
        [pltpu.VMEM(woa.shape, F32), pltpu.VMEM(woc.shape, F32), pltpu.VMEM(wout.shape, F32)],
        (dx, o, yc, proj, proj, woa, woc, wout), comm)


def _proj_pieces(dq, dk, dv, dcb, dcc, dcx, dgates, dflog):
    D = dgates.shape[1] // 2
    return [(dq, ATTN_W, 0), (dk, ATTN_W, 0), (dv, ATTN_W, 0), (dcb, CONV_W, 0), (dcc, CONV_W, 0), (dcx, CONV_W, 0),
            (dgates, D, 0), (dgates, D, 1), (dflog, LANES, 0)]


def _mix_proj_bwd_dx(dres, x, g, pieces, wproj_t, wf_t, tm, comm=None):
    T, D = x.shape
    n = len(pieces)
    w_blocks = [(ATTN_W, 0), (ATTN_W, 1), (ATTN_W, 2), (CONV_W, 3), (CONV_W, 4), (CONV_W, 5),
                (D, COL_GATES // D), (D, COL_GATES // D + 1)]

    def body(*refs):
        dres_ref, x_ref, g_ref = refs[:3]
        p_refs, w_refs = refs[3:3 + n], refs[3 + n:3 + 2 * n]
        dx_ref, dg_ref = refs[3 + 2 * n:]

        @pl.when(pl.program_id(0) == 0)
        def _():
            dg_ref[...] = jnp.zeros_like(dg_ref)

        dh = _dot(p_refs[0][...].astype(BF16), w_refs[0][...])
        for p_ref, w_ref in zip(p_refs[1:], w_refs[1:]):
            dh = dh + _dot(p_ref[...].astype(BF16), w_ref[...])
        xhat, inv = _rms(x_ref[...])
        dx, dg = _rms_bwd(dh, xhat, inv, g_ref[...])
        dx_ref[...] = dres_ref[...] + dx
        dg_ref[...] += dg

    row = pl.BlockSpec((tm, D), lambda i: (i, 0))
    vec = pl.BlockSpec((1, D), lambda i: (0, 0))
    p_specs = [pl.BlockSpec((tm, w), lambda i, cb=cb: (i, cb)) for _, w, cb in pieces]
    w_specs = [pl.BlockSpec((r, D), lambda i, rb=rb: (rb, 0)) for r, rb in w_blocks]
    w_specs.append(pl.BlockSpec((LANES, D), lambda i: (0, 0)))
    return _pallas(
        body, "mix_proj_bwd_dx", (T // tm,),
        [row, row, vec] + p_specs + w_specs, [row, vec],
        [jax.ShapeDtypeStruct((T, D), F32), jax.ShapeDtypeStruct((1, D), F32)], [],
        (dres, x, g, *[p for p, _, _ in pieces], *([wproj_t] * len(w_blocks)), wf_t), comm)


def _matmuls_tn(name, pieces, b, tk):
    T, N = b.shape
    nt = T // tk
    n = len(pieces)

    def body(*refs):
        a_refs, b_ref, out_refs, accs = refs[:n], refs[n], refs[n + 1:2 * n + 1], refs[2 * n + 1:]
        t = pl.program_id(0)

        @pl.when(t == 0)
        def _():
            for acc in accs:
                acc[...] = jnp.zeros_like(acc)

        bv = b_ref[...]
        for a_ref, acc in zip(a_refs, accs):
            acc[...] += _dot_tn(a_ref[...].astype(BF16), bv)

        @pl.when(t == nt - 1)
        def _():
            for out_ref, acc in zip(out_refs, accs):
                out_ref[...] = acc[...].astype(BF16)

    return pl.pallas_call(
        body, name=name, grid=(nt,),
        in_specs=[pl.BlockSpec((tk, w), lambda t, cb=cb: (t, cb)) for _, w, cb in pieces]
        + [pl.BlockSpec((tk, N), lambda t: (t, 0))],
        out_specs=[pl.BlockSpec((w, N), lambda t: (0, 0)) for _, w, _ in pieces],
        out_shape=[jax.ShapeDtypeStruct((w, N), BF16) for _, w, _ in pieces],
        scratch_shapes=[pltpu.VMEM((w, N), F32) for _, w, _ in pieces],
        compiler_params=_params(("arbitrary",)),
    )(*[a for a, _, _ in pieces], b)


def _final_loss(x, target, g, tm):
    T, D = x.shape

    def body(x_ref, t_ref, g_ref, dx_ref, loss_ref, dg_ref):
        @pl.when(pl.program_id(0) == 0)
        def _():
            loss_ref[...] = jnp.zeros_like(loss_ref)
            dg_ref[...] = jnp.zeros_like(dg_ref)

        xhat, inv = _rms(x_ref[...])
        err = xhat * g_ref[...] - t_ref[...]
        loss_ref[...] += 0.5 * jnp.sum(jnp.sum(err * err, axis=1, keepdims=True), axis=0, keepdims=True) / D
        dx, dg = _rms_bwd(err * (1.0 / D), xhat, inv, g_ref[...])
        dx_ref[...] = dx
        dg_ref[...] += dg

    row = pl.BlockSpec((tm, D), lambda i: (i, 0))
    return pl.pallas_call(
        body, name="final_loss", grid=(T // tm,),
        in_specs=[row, row, pl.BlockSpec((1, D), lambda i: (0, 0))],
        out_specs=[row, pl.BlockSpec((1, LANES), lambda i: (0, 0)), pl.BlockSpec((1, D), lambda i: (0, 0))],
        out_shape=[jax.ShapeDtypeStruct((T, D), F32), jax.ShapeDtypeStruct((1, LANES), F32),
                   jax.ShapeDtypeStruct((1, D), F32)],
        compiler_params=_params(("arbitrary",)),
    )(x, target, g)


class _LocalPlan:
    def __init__(self, stacks, small):
        self.stacks, self.small, self.grads = stacks, small, {}

    def weights(self, group):
        return _LAYOUTS[group](self.stacks, self.small)

    def rider(self, kernel_name):
        return None

    def arrived(self, kernel_name, results):
        pass

    def reduce(self, group, grads):
        self.grads.update(grads)


def _local_step(x, target, plan, B, S):
    T, D = x.shape
    tm = min(512, T)
    tm_fwd = min(1024, T)
    tq = min(512, S)
    nq = S // tq
    ch = min(256, S)

    def riding(kernel_name, build):
        results, brought = build(plan.rider(kernel_name))
        plan.arrived(kernel_name, brought)
        return results

    w1 = plan.weights("ffn1_in")
    hg1, hu1 = riding("ffn1_up", lambda comm: _ffn_up(
        "ffn1_up", x, w1["ffn1_norm"], w1["ffn1_gate"], w1["ffn1_up"], tm_fwd, comm))
    w1 = plan.weights("ffn1")
    x1, = riding("ffn1_down", lambda comm: _ffn_down("ffn1_down", x, hg1, hu1, w1["ffn1_down"], tm_fwd, comm))
    wm = plan.weights("mix")
    h, proj, flog = _mix_proj_fwd(x1, wm["mix_norm"], wm["w_proj"], wm["w_f"], tm_fwd, 1280)
    cum = _fgate_fwd(flog, wm["b_forget"], B, S, ch)
    cum_t = jnp.transpose(cum[:, :N_HEADS].reshape(B, nq, tq, N_HEADS), (0, 1, 3, 2))
    o, lse = riding("attn_fwd", lambda comm: _attn_fwd(proj, cum, cum_t, B, S, tq, comm))
    yc = _conv_fwd(proj, wm["conv_w"], B, S)
    x2 = _mix_out_fwd(x1, o, yc, proj, wm["w_o_attn"], wm["w_o_conv"], wm["w_out"], tm)
    w2 = plan.weights("ffn2")
    x3, hg2, hu2 = _ffn_fwd("ffn2_fwd", x2, w2["ffn2_norm"], w2["ffn2_gate"], w2["ffn2_up"], w2["ffn2_down"], tm_fwd)[0]
    dx3, loss, d_final_norm = _final_loss(x3, target, w2["final_norm"], tm)

    g = {"final_norm": d_final_norm}
    dx2, dhg2, dhu2, g["ffn2_norm"] = _ffn_bwd_dx("ffn2_bwd_dx", dx3, x2, w2["ffn2_norm"], hg2, hu2,
                                                  w2["ffn2_gate"], w2["ffn2_up"], w2["ffn2_down"], tm)[0]
    plan.reduce("ffn2", dict(zip(("ffn2_gate", "ffn2_up", "ffn2_down"),
                                 _ffn_bwd_dw("ffn2_bwd_dw", dx3, x2, w2["ffn2_norm"], hg2, hu2, dhg2, dhu2, tm)[0])))
    do, dyc, dgates, dwoa, dwoc, dwout = riding("mix_out_bwd", lambda comm: _mix_out_bwd(
        dx2, o, yc, proj, wm["w_o_attn"], wm["w_o_conv"], wm["w_out"], tm, comm))
    plan.reduce("out", dict(w_o_attn=_shard_cols(dwoa), w_o_conv=_shard_cols(dwoc), w_out=dwout.reshape(N_CHIPS, -1, D)))
    dq, dk, dv, dcq, dck = riding("attn_bwd", lambda comm: _attn_bwd(proj, o, do, lse, cum, cum_t, B, S, tq, comm))
    dcum = dcq + jnp.pad(jnp.transpose(dck, (0, 1, 3, 2)).reshape(T, N_HEADS), ((0, 0), (0, LANES - N_HEADS)))
    dflog, g["b_forget"] = _fgate_bwd(dcum, flog, wm["b_forget"], B, S, ch)
    dcb, dcc, dcx, g["conv_w"] = _conv_bwd(dyc, proj, wm["conv_w"], B, S)
    pieces = _proj_pieces(dq, dk, dv, dcb, dcc, dcx, dgates, dflog)
    dwq, dwk, dwv, dwcb, dwcc, dwcx = _matmuls_tn("mix_dw_a", pieces[:6], h, tm)
    dwga, dwgc, dwf = _matmuls_tn("mix_dw_b", pieces[6:], h, tm)
    dwin_t = jnp.concatenate([dwq, dwk, dwv, dwf[:N_HEADS], dwcb, dwcc, dwcx, dwga, dwgc], axis=0)
    plan.reduce("w_in", {"w_in": dwin_t.reshape(N_CHIPS, -1, D)})
    dx1, g["mix_norm"] = riding("mix_proj_bwd_dx", lambda comm: _mix_proj_bwd_dx(
        dx2, x1, wm["mix_norm"], pieces, wm["w_proj"], wm["w_f"], min(256, T), comm))
    grad_x, dhg1, dhu1, g["ffn1_norm"] = _ffn_bwd_dx(
        "ffn1_bwd_dx", dx1, x, w1["ffn1_norm"], hg1, hu1, w1["ffn1_gate"], w1["ffn1_up"], w1["ffn1_down"], tm)[0]
    plan.reduce("ffn1", dict(zip(("ffn1_gate", "ffn1_up", "ffn1_down"), riding("ffn1_bwd_dw", lambda comm: _ffn_bwd_dw(
        "ffn1_bwd_dw", dx1, x, w1["ffn1_norm"], hg1, hu1, dhg1, dhu1, tm, comm)))))
    return loss, grad_x, g


TRANSPOSED = ("ffn1_gate", "ffn1_up", "ffn2_gate", "ffn2_up", "w_in")
NORMS = ("ffn1_norm", "mix_norm", "ffn2_norm", "final_norm")


def _unshard_cols(a):
    return jnp.transpose(a, (1, 0, 2)).reshape(a.shape[1], N_CHIPS * a.shape[2])


def _shard_cols(a):
    return jnp.transpose(a.reshape(a.shape[0], N_CHIPS, a.shape[1] // N_CHIPS), (1, 0, 2))


def _layout_ffn(which):
    def layout(st, small):
        w = {n: st[n] for n in (which + "_gate", which + "_up", which + "_down")}
        w[which + "_norm"] = small[which + "_norm"].reshape(1, -1)
        if which == "ffn2":
            w["final_norm"] = small["final_norm"].reshape(1, -1)
        return w
    return layout


def _layout_mix(st, small):
    win_t = st["w_in"].reshape(-1, st["w_in"].shape[2])
    return {
        "w_proj": jnp.concatenate([win_t[:N_FORGET_COL], win_t[N_FORGET_COL + N_HEADS:]], axis=0),
        "w_f": jnp.pad(win_t[N_FORGET_COL:N_FORGET_COL + N_HEADS], ((0, LANES - N_HEADS), (0, 0))),
        "w_o_attn": _unshard_cols(st["w_o_attn"]),
        "w_o_conv": _unshard_cols(st["w_o_conv"]),
        "w_out": st["w_out"].reshape(-1, st["w_out"].shape[2]),
        "conv_w": _unshard_cols(st["conv_w"]),
        "mix_norm": small["mix_norm"].reshape(1, -1),
        "b_forget": jnp.pad(small["b_forget"].reshape(1, -1), ((0, 0), (0, LANES - N_HEADS))),
    }


def _layout_ffn1_in(st, small):
    return {"ffn1_gate": st["ffn1_gate"], "ffn1_up": st["ffn1_up"], "ffn1_norm": small["ffn1_norm"].reshape(1, -1)}


_LAYOUTS = {"ffn1_in": _layout_ffn1_in, "ffn1": _layout_ffn("ffn1"), "mix": _layout_mix, "ffn2": _layout_ffn("ffn2")}


ANY = pl.BlockSpec(memory_space=pl.ANY)
BIG = ("ffn1_gate", "ffn1_up", "ffn1_down", "w_in", "w_o_attn", "w_o_conv", "w_out",
       "ffn2_gate", "ffn2_up", "ffn2_down")


def _place():
    x, y, c = lax.axis_index("x"), lax.axis_index("y"), lax.axis_index("c")
    others = [(1 - x, y), (x, 1 - y), (1 - x, 1 - y)]
    return x, y, c, others


def _col_halves(cols, c):
    hc = cols // 2
    return pl.ds(pl.multiple_of(c * hc, LANES), hc), pl.ds(pl.multiple_of((1 - c) * hc, LANES), hc)


def _gather_comm(shards, conv_shard=None):
    n = len(shards)
    inputs = list(shards) + ([] if conv_shard is None else [conv_shard])

    def copies(ins, outs, sems):
        send_sems, recv_sems, pass_send, pass_recv = sems[:4]
        x, y, c, others = _place()

        def chip_copy(a, j, chip):
            mine, _ = _col_halves(ins[a].shape[1], c)
            return pltpu.make_async_remote_copy(
                src_ref=ins[a].at[:, mine], dst_ref=outs[a].at[chip, :, mine],
                send_sem=send_sems.at[3 * a + j], recv_sem=recv_sems.at[3 * a + j],
                device_id=(*others[j], c), device_id_type=MESH)

        def pass_copy(a, j, chip, half):
            return pltpu.make_async_remote_copy(
                src_ref=outs[a].at[chip, :, half], dst_ref=outs[a].at[chip, :, half],
                send_sem=pass_send.at[3 * a + j], recv_sem=pass_recv.at[3 * a + j],
                device_id=(x, y, 1 - c), device_id_type=MESH)

        def conv_copy(j, chip):
            return pltpu.make_async_remote_copy(
                src_ref=ins[n], dst_ref=outs[n].at[chip],
                send_sem=sems[4].at[j], recv_sem=sems[5].at[j],
                device_id=(*others[j], c), device_id_type=MESH)

        me = 2 * x + y
        sends = [chip_copy(a, j, me) for a in range(n) for j in range(3)]
        if conv_shard is not None:
            sends += [conv_copy(j, me) for j in range(3)]
        return c, others, sends, chip_copy, pass_copy, conv_copy

    def start(ins, outs, sems):
        for cp in copies(ins, outs, sems)[2]:
            cp.start()

    def finish(ins, outs, sems):
        c, others, sends, chip_copy, pass_copy, conv_copy = copies(ins, outs, sems)
        passed = []
        for a in range(n):
            mine, _ = _col_halves(ins[a].shape[1], c)
            for j, (ox, oy) in enumerate(others):
                chip_copy(a, j, 2 * ox + oy).wait_recv()
                passed.append(pass_copy(a, j, 2 * ox + oy, mine))
                passed[-1].start()
        for a in range(n):
            _, theirs = _col_halves(ins[a].shape[1], c)
            for j, (ox, oy) in enumerate(others):
                pass_copy(a, j, 2 * ox + oy, theirs).wait_recv()
        if conv_shard is not None:
            for j, (ox, oy) in enumerate(others):
                conv_copy(j, 2 * ox + oy).wait_recv()
        for cp in sends + passed:
            cp.wait_send()

    scratch = [pltpu.SemaphoreType.DMA((3 * n,))] * 4
    if conv_shard is not None:
        scratch += [pltpu.SemaphoreType.DMA((3,))] * 2
    return _Comm(inputs, [jax.ShapeDtypeStruct((N_CHIPS,) + s.shape, s.dtype) for s in inputs], scratch, start, finish)


def _fill_own(stacks, shards):
    chip = 2 * lax.axis_index("x") + lax.axis_index("y")
    return [lax.dynamic_update_index_in_dim(st, s, chip, 0) for st, s in zip(stacks, shards)]


def _run_comm(name, comm):
    ci, co = len(comm.inputs), len(comm.out_shape)

    def body(*refs):
        comm.start(refs[:ci], refs[ci:ci + co], refs[ci + co:])
        comm.finish(refs[:ci], refs[ci:ci + co], refs[ci + co:])

    return pl.pallas_call(body, name=name, in_specs=[ANY] * ci, out_specs=[ANY] * co, out_shape=comm.out_shape,
                          scratch_shapes=comm.scratch)(*comm.inputs)


def _sibling_exchange_comm(grads):
    n = len(grads)

    def copies(ins, outs, sems):
        x, y, c, _ = _place()
        return [pltpu.make_async_remote_copy(
            src_ref=ins[a].at[:, :, _col_halves(ins[a].shape[2], c)[1]], dst_ref=outs[a],
            send_sem=sems[0].at[a], recv_sem=sems[1].at[a],
            device_id=(x, y, 1 - c), device_id_type=MESH) for a in range(n)]

    def start(ins, outs, sems):
        for cp in copies(ins, outs, sems):
            cp.start()

    def finish(ins, outs, sems):
        for cp in copies(ins, outs, sems):
            cp.wait()

    half = lambda s: jax.ShapeDtypeStruct((s.shape[0], s.shape[1], s.shape[2] // 2), s.dtype)
    return _Comm(grads, [half(s) for s in grads], [pltpu.SemaphoreType.DMA((n,))] * 2, start, finish)


def _merge_comms(comms):
    def split(refs, count):
        out, at = [], 0
        for cm in comms:
            out.append(refs[at:at + count(cm)])
            at += count(cm)
        return out

    def parts(ins, outs, sems):
        return zip(comms, split(ins, lambda cm: len(cm.inputs)), split(outs, lambda cm: len(cm.out_shape)),
                   split(sems, lambda cm: len(cm.scratch)))

    def start(ins, outs, sems):
        for cm, i, o, s in parts(ins, outs, sems):
            cm.start(i, o, s)

    def finish(ins, outs, sems):
        for cm, i, o, s in parts(ins, outs, sems):
            cm.finish(i, o, s)

    return _Comm(sum([cm.inputs for cm in comms], []), sum([cm.out_shape for cm in comms], []),
                 sum([cm.scratch for cm in comms], []), start, finish)


def _add_halves(name, grad, recv, core):
    K, r, cols = grad.shape
    hc = cols // 2

    def body(core_ref, g_ref, r_ref, out_ref):
        out_ref[...] = (g_ref[...].astype(F32) + r_ref[...].astype(F32)).astype(BF16)

    return pl.pallas_call(
        body, name=name,
        grid_spec=pltpu.PrefetchScalarGridSpec(
            num_scalar_prefetch=1, grid=(K,),
            in_specs=[pl.BlockSpec((None, r, hc), lambda k, core_ref: (k, 0, core_ref[0])),
                      pl.BlockSpec((None, r, hc), lambda k, core_ref: (k, 0, 0))],
            out_specs=pl.BlockSpec((None, r, hc), lambda k, core_ref: (k, 0, 0))),
        out_shape=jax.ShapeDtypeStruct((K, r, hc), BF16),
        compiler_params=_params(("arbitrary",)),
    )(core, grad, recv)


def _chip_exchange_comm(parts):
    n = len(parts)

    def copies(ins, outs, sems):
        x, y, c, others = _place()
        return [pltpu.make_async_remote_copy(
            src_ref=ins[a].at[2 * ox + oy], dst_ref=outs[a].at[j],
            send_sem=sems[0].at[3 * a + j], recv_sem=sems[1].at[3 * a + j],
            device_id=(ox, oy, c), device_id_type=MESH) for a in range(n) for j, (ox, oy) in enumerate(others)]

    def start(ins, outs, sems):
        for cp in copies(ins, outs, sems):
            cp.start()

    def finish(ins, outs, sems):
        for cp in copies(ins, outs, sems):
            cp.wait()

    return _Comm(parts, [jax.ShapeDtypeStruct((3,) + s.shape[1:], s.dtype) for s in parts],
                 [pltpu.SemaphoreType.DMA((3 * n,))] * 2, start, finish)


HBM = pl.BlockSpec(memory_space=pltpu.HBM)
SEM = pl.BlockSpec(memory_space=pltpu.SEMAPHORE)


def _split_exchange_copies(parts, lands, send_sems, recv_sems):
    x, y, c, others = _place()
    return [pltpu.make_async_remote_copy(
        src_ref=parts[a].at[2 * ox + oy], dst_ref=lands[a].at[j],
        send_sem=send_sems.at[3 * a + j], recv_sem=recv_sems.at[3 * a + j],
        device_id=(ox, oy, c), device_id_type=MESH) for a in range(len(parts)) for j, (ox, oy) in enumerate(others)]


def _exchange_start(name, parts):
    n = len(parts)

    def body(*refs):
        ins, lands = refs[:n], refs[n:2 * n]
        send_sems, recv_sems, token = refs[2 * n], refs[2 * n + 1], refs[-1]
        for cp in _split_exchange_copies(ins, lands, send_sems, recv_sems):
            cp.start()
        token[...] = jnp.zeros_like(token)

    land_shape = [(3,) + p.shape[1:] for p in parts]
    outs = pl.pallas_call(
        body, name=name,
        out_shape=[pltpu.SemaphoreType.DMA((3 * n,)), pltpu.SemaphoreType.DMA((3 * n,))]
        + [pltpu.HBM(p.shape, p.dtype) for p in parts] + [pltpu.HBM(s, p.dtype) for s, p in zip(land_shape, parts)]
        + [jax.ShapeDtypeStruct((8, LANES), F32)],
        in_specs=[HBM] * (2 * n), out_specs=[SEM, SEM] + [HBM] * (2 * n) + [pl.BlockSpec(memory_space=pltpu.VMEM)],
        input_output_aliases={i: 2 + i for i in range(2 * n)},
        compiler_params=pltpu.CompilerParams(has_side_effects=pltpu.SideEffectType.DATAFLOW_SIDE_EFFECTING),
    )(*[pltpu.with_memory_space_constraint(p, pltpu.HBM) for p in parts],
      *[pltpu.with_memory_space_constraint(lax.empty(s, p.dtype), pltpu.HBM) for s, p in zip(land_shape, parts)])
    return outs[0], outs[1], list(outs[2:2 + n]), list(outs[2 + n:2 + 2 * n]), outs[-1]


def _exchange_wait(name, send_sems, recv_sems, parts, lands, after):
    n = len(parts)

    def body(*refs):
        ins, zones = refs[:n], refs[n:2 * n]
        for cp in _split_exchange_copies(ins, zones, refs[2 * n], refs[2 * n + 1]):
            cp.wait_send()
            cp.wait_recv()

    outs = pl.pallas_call(
        body, name=name,
        out_shape=[pltpu.HBM(p.shape, p.dtype) for p in parts] + [pltpu.HBM(z.shape, z.dtype) for z in lands],
        in_specs=[HBM] * (2 * n) + [SEM, SEM] + [ANY] * len(after), out_specs=[HBM] * (2 * n),
        input_output_aliases={i: i for i in range(2 * n)},
        compiler_params=pltpu.CompilerParams(has_side_effects=pltpu.SideEffectType.DATAFLOW_SIDE_EFFECTING),
    )(*parts, *lands, send_sems, recv_sems, *after)
    return list(outs[:n]), list(outs[n:])


def _sum_chips(name, own, recv, chip, after):
    _, r, hc = own.shape

    def body(chip_ref, own_ref, recv_ref, after_ref, out_ref):
        acc = own_ref[...].astype(F32)
        for j in range(3):
            acc = acc + recv_ref[j].astype(F32)
        out_ref[...] = acc

    return pl.pallas_call(
        body, name=name,
        grid_spec=pltpu.PrefetchScalarGridSpec(
            num_scalar_prefetch=1, grid=(hc // LANES,),
            in_specs=[pl.BlockSpec((None, r, LANES), lambda i, chip_ref: (chip_ref[0], 0, i)),
                      pl.BlockSpec((3, r, LANES), lambda i, chip_ref: (0, 0, i)),
                      pl.BlockSpec((8, LANES), lambda i, chip_ref: (0, 0))],
            out_specs=pl.BlockSpec((r, LANES), lambda i, chip_ref: (0, i))),
        out_shape=jax.ShapeDtypeStruct((r, hc), F32),
        compiler_params=_params(("arbitrary",)),
    )(chip, own, recv, after)


def _share_halves(name, halves):
    n = len(halves)

    def body(*refs):
        srcs, dsts = refs[:n], refs[n:2 * n]
        send_sems, recv_sems = refs[2 * n:]
        x, y, c, _ = _place()
        copies = [pltpu.make_async_remote_copy(
            src_ref=srcs[a], dst_ref=dsts[a], send_sem=send_sems.at[a], recv_sem=recv_sems.at[a],
            device_id=(x, y, 1 - c), device_id_type=MESH) for a in range(n)]
        for cp in copies:
            cp.start()
        for cp in copies:
            cp.wait()

    return pl.pallas_call(
        body, name=name,
        in_specs=[ANY] * n, out_specs=[ANY] * n,
        out_shape=[jax.ShapeDtypeStruct(s.shape, s.dtype) for s in halves],
        scratch_shapes=[pltpu.SemaphoreType.DMA((n,)), pltpu.SemaphoreType.DMA((n,))],
    )(*halves)


def _allreduce_small(part):
    rows = part.shape[0]

    def body(in_ref, out_ref, land, send_sems, recv_sems):
        x, y, c, _ = _place()
        me = 4 * x + 2 * y + c
        land[me] = in_ref[...]
        copies = []
        for d in range(1, N_DEV):
            peer = (1 - x if d & 4 else x, 1 - y if d & 2 else y, 1 - c if d & 1 else c)
            copies.append(pltpu.make_async_remote_copy(
                src_ref=in_ref, dst_ref=land.at[me],
                send_sem=send_sems.at[d - 1], recv_sem=recv_sems.at[d - 1],
                device_id=peer, device_id_type=MESH))
        for cp in copies:
            cp.start()
        for d in range(1, N_DEV):
            px, py, pc = (1 - x if d & 4 else x, 1 - y if d & 2 else y, 1 - c if d & 1 else c)
            pltpu.make_async_remote_copy(
                src_ref=in_ref, dst_ref=land.at[4 * px + 2 * py + pc],
                send_sem=send_sems.at[d - 1], recv_sem=recv_sems.at[d - 1],
                device_id=(px, py, pc), device_id_type=MESH).wait_recv()
        for cp in copies:
            cp.wait_send()
        acc = land[0]
        for k in range(1, N_DEV):
            acc = acc + land[k]
        out_ref[...] = acc

    vmem = pl.BlockSpec(memory_space=pltpu.VMEM)
    return pl.pallas_call(
        body, name="allreduce_small",
        in_specs=[vmem], out_specs=vmem,
        out_shape=jax.ShapeDtypeStruct(part.shape, F32),
        scratch_shapes=[pltpu.VMEM((N_DEV, rows, LANES), F32),
                        pltpu.SemaphoreType.DMA((N_DEV - 1,)), pltpu.SemaphoreType.DMA((N_DEV - 1,))],
    )(part)


def _adam_update(w, g, m, v):
    nm = ADAM_B1 * m + (1.0 - ADAM_B1) * g
    nv = ADAM_B2 * v + (1.0 - ADAM_B2) * (g * g)
    m_hat = nm * (1.0 / (1.0 - ADAM_B1 ** ADAM_STEP))
    v_hat = nv * (1.0 / (1.0 - ADAM_B2 ** ADAM_STEP))
    return -ADAM_LR * (m_hat / (jnp.sqrt(v_hat) + ADAM_EPS) + ADAM_WD * w), nm, nv


def _adamw(name, w, g, m, v):
    def body(w_ref, g_ref, m_ref, v_ref, d_ref, nm_ref, nv_ref):
        d_ref[...], nm_ref[...], nv_ref[...] = _adam_update(w_ref[...], g_ref[...], m_ref[...], v_ref[...])

    spec = pl.BlockSpec(w.shape, lambda i: (0, 0))
    out = jax.ShapeDtypeStruct(w.shape, F32)
    return pl.pallas_call(
        body, name=name, grid=(1,),
        in_specs=[spec] * 4, out_specs=[spec] * 3, out_shape=[out] * 3,
        compiler_params=_params(("arbitrary",)),
    )(w, g, m, v)


def _adamw_halves(name, w, mine, theirs, m, v, core):
    rows, cols = w.shape
    hc = cols // 2
    tc = min(256, hc)
    nt = hc // tc

    def body(core_ref, w_ref, mine_ref, theirs_ref, m_ref, v_ref, g_ref, d_ref, nm_ref, nv_ref):
        gv = jnp.where(pl.program_id(0) == core_ref[0], mine_ref[...], theirs_ref[...])
        g_ref[...] = gv
        d_ref[...], nm_ref[...], nv_ref[...] = _adam_update(w_ref[...], gv, m_ref[...], v_ref[...])

    whole = pl.BlockSpec((rows, tc), lambda h, i, core_ref: (0, h * nt + i))
    mine_spec = pl.BlockSpec((rows, tc), lambda h, i, core_ref: (0, jnp.where(h == core_ref[0], i, 0)))
    theirs_spec = pl.BlockSpec((rows, tc), lambda h, i, core_ref: (0, jnp.where(h == core_ref[0], 0, i)))
    out = jax.ShapeDtypeStruct((rows, cols), F32)
    return pl.pallas_call(
        body, name=name,
        grid_spec=pltpu.PrefetchScalarGridSpec(
            num_scalar_prefetch=1, grid=(2, nt),
            in_specs=[whole, mine_spec, theirs_spec, whole, whole], out_specs=[whole] * 4),
        out_shape=[out] * 4,
        compiler_params=_params(("arbitrary", "arbitrary")),
    )(core, w, mine, theirs, m, v)


WEIGHTS = ("ffn1_norm", "ffn1_gate", "ffn1_up", "ffn1_down", "mix_norm", "w_in", "b_forget", "conv_w",
           "w_o_attn", "w_o_conv", "w_out", "ffn2_norm", "ffn2_gate", "ffn2_up", "ffn2_down", "final_norm")
VEC_ROWS = 8


def _pack_small(t, conv_rows):
    conv = t["conv_w"]
    parts = [t[n].reshape(VEC_ROWS, LANES) for n in NORMS]
    parts.append(jnp.pad(conv, ((0, conv_rows - conv.shape[0]), (0, 0))))
    parts.append(jnp.pad(t["b_forget"].reshape(1, N_HEADS), ((0, 7), (0, LANES - N_HEADS))))
    return jnp.concatenate(parts, axis=0)


def _unpack_small(p, conv_rows):
    out = {n: p[VEC_ROWS * i:VEC_ROWS * (i + 1)].reshape(-1) for i, n in enumerate(NORMS)}
    base = VEC_ROWS * len(NORMS)
    out["conv_w"] = p[base:base + 3]
    out["b_forget"] = p[base + conv_rows, :N_HEADS]
    return out


def _travel(name, a):
    return a.T if name in TRANSPOSED else a


GATHER_FIRST = ("ffn1_gate", "ffn1_up")
GATHER_RIDES = {"ffn1_up": ("ffn1_down",), "ffn1_down": ("w_in", "w_o_attn", "w_o_conv", "w_out"),
                "attn_fwd": ("ffn2_gate", "ffn2_up", "ffn2_down")}
SIBLING_RIDES = {"ffn2": "mix_out_bwd", "out": None, "w_in": "mix_proj_bwd_dx", "ffn1": None}
CHIP_RIDES = {"ffn2": "attn_bwd", "out": "attn_bwd", "w_in": "ffn1_bwd_dw", "ffn1": None}


class _MeshPlan(_LocalPlan):
    def __init__(self, wts, core):
        self.small, self.core = wts, core
        self.shards = {n: wts[n].astype(BF16) for n in BIG}
        self.chip_part, self.from_chips, self.rides = {}, {}, {}
        conv_shard = jnp.pad(wts["conv_w"], ((0, 8 - wts["conv_w"].shape[0]), (0, 0)))
        own = [self.shards[n] for n in GATHER_FIRST] + [conv_shard]
        got = _run_comm("gather_first", _gather_comm(own[:-1], conv_shard))
        self.stacks = dict(zip(GATHER_FIRST + ("conv_w",), _fill_own(got, own)))
        for kernel_name, names in GATHER_RIDES.items():
            mine = [self.shards[n] for n in names]
            self._ride(kernel_name, _gather_comm(mine),
                       lambda got, names=names, mine=mine: self.stacks.update(zip(names, _fill_own(got, mine))))

    def _ride(self, kernel_name, comm, then):
        self.rides.setdefault(kernel_name, []).append((comm, then))

    def rider(self, kernel_name):
        comms = [comm for comm, _ in self.rides.get(kernel_name, [])]
        return _merge_comms(comms) if comms else None

    def arrived(self, kernel_name, results):
        for comm, then in self.rides.pop(kernel_name, []):
            then(results[:len(comm.out_shape)])
            results = results[len(comm.out_shape):]

    def reduce(self, group, grads):
        names = tuple(grads)
        mine = [grads[n] for n in names]

        def with_sibling(from_sibling):
            parts = [_add_halves("add_halves_" + n, g, r, self.core) for n, g, r in zip(names, mine, from_sibling)]
            self.chip_part.update(zip(names, parts))
            if CHIP_RIDES[group] is None:
                self.last = (names, _exchange_start("exchange_start_" + group, parts))
            else:
                self._ride(CHIP_RIDES[group], _chip_exchange_comm(parts),
                           lambda got: self.from_chips.update(zip(names, got)))

        if SIBLING_RIDES[group] is None:
            with_sibling(_run_comm("sibling_exchange_" + group, _sibling_exchange_comm(mine)))
        else:
            self._ride(SIBLING_RIDES[group], _sibling_exchange_comm(mine), with_sibling)


def kernel(x, ffn1_norm, ffn1_gate, ffn1_up, ffn1_down, mix_norm, w_in, b_forget, conv_w, w_o_attn, w_o_conv, w_out, ffn2_norm, ffn2_gate, ffn2_up, ffn2_down, final_norm, loss_target, m_ffn1_norm, m_ffn1_gate, m_ffn1_up, m_ffn1_down, m_mix_norm, m_w_in, m_b_forget, m_conv_w, m_w_o_attn, m_w_o_conv, m_w_out, m_ffn2_norm, m_ffn2_gate, m_ffn2_up, m_ffn2_down, m_final_norm, v_ffn1_norm, v_ffn1_gate, v_ffn1_up, v_ffn1_down, v_mix_norm, v_w_in, v_b_forget, v_conv_w, v_w_o_attn, v_w_o_conv, v_w_out, v_ffn2_norm, v_ffn2_gate, v_ffn2_up, v_ffn2_down, v_final_norm):
    given = dict(locals())
    wts = {n: _travel(n, given[n]) for n in WEIGHTS}
    mom = {n: _travel(n, given["m_" + n]) for n in WEIGHTS}
    var = {n: _travel(n, given["v_" + n]) for n in WEIGHTS}
    B, S, D = x.shape
    chip = 2 * lax.axis_index("x") + lax.axis_index("y")
    chip1 = chip.astype(jnp.int32).reshape(1)
    core = lax.axis_index("c").astype(jnp.int32).reshape(1)

    plan = _MeshPlan(wts, core)
    loss, grad_x, gs = _local_step(x.reshape(B * S, D), loss_target.reshape(B * S, D), plan, B, S)

    last_names, (send_sems, recv_sems, parts_thru, lands, token) = plan.last
    delta, new_m, new_v, grads = {}, {}, {}, {}

    def finish(tag, names):
        mine = [_sum_chips("sum_chips_" + n, plan.chip_part[n], plan.from_chips[n], chip1, token) for n in names]
        theirs = _share_halves("share_halves_" + tag, mine)
        raw = []
        for n, gm, gt in zip(names, mine, theirs):
            outs = _adamw_halves("adamw_" + n, wts[n], gm, gt, mom[n], var[n], core)
            raw.append(outs[-1])
            grads[n], delta[n], new_m[n], new_v[n] = [_travel(n, o) for o in outs]
        return raw

    done = finish("early", [n for n in BIG if n not in last_names])
    parts_back, got = _exchange_wait("exchange_wait", send_sems, recv_sems, parts_thru, lands, done)
    plan.chip_part.update(zip(last_names, parts_back))
    plan.from_chips.update(zip(last_names, got))
    finish("last", last_names)

    conv_all = _shard_cols(gs["conv_w"]).reshape(N_CHIPS * 8, LANES)
    small_part = _pack_small({**{n: gs[n] for n in NORMS}, "conv_w": conv_all, "b_forget": gs["b_forget"][0, :N_HEADS]},
                             N_CHIPS * 8)
    base = VEC_ROWS * len(NORMS)
    small_sum = _allreduce_small(small_part)
    grads.update(_unpack_small(small_sum, N_CHIPS * 8))
    grads["conv_w"] = lax.dynamic_slice_in_dim(small_sum[base:base + N_CHIPS * 8], chip * 8, 8, axis=0)[:3]
    packs = [_pack_small(t, 8) for t in (wts, grads, mom, var)]
    for out, p in zip((delta, new_m, new_v), _adamw("adamw_small", *packs)):
        out.update(_unpack_small(p, 8))

    total = lax.psum(loss[0, 0], ("x", "y", "c"))
    return (total, grad_x.reshape(B, S, D), *[grads[n] for n in WEIGHTS], *[delta[n] for n in WEIGHTS],
            *[new_m[n] for n in WEIGHTS], *[new_v[n] for n in WEIGHTS])
```

```python
import functools
import math

import jax
import jax.numpy as jnp
from jax import lax
from jax.experimental import pallas as pl
from jax.experimental.pallas import tpu as pltpu

F32 = jnp.float32
BF16 = jnp.bfloat16
MESH = pl.DeviceIdType.MESH

N_CHIPS = 4
N_DEV = 8
N_HEADS = 8
HEAD_DIM = 64
HEAD_PAIRS = N_HEADS // 2
ATTN_W = N_HEADS * HEAD_DIM
CONV_W = 512
RMS_EPS = 1e-6
FFN_RES = 0.5
LANES = 128
VMEM_LIMIT = 56 * 1024 * 1024
ROW_BLOCK = 256

ADAM_LR = 0.001
ADAM_B1 = 0.9
ADAM_B2 = 0.999
ADAM_EPS = 1e-08
ADAM_WD = 0.01
ADAM_STEP = 10

PROJ_W = 3 * ATTN_W + 3 * CONV_W + 2 * 1024
COL_CB, COL_CC, COL_CX = 3 * ATTN_W, 3 * ATTN_W + CONV_W, 3 * ATTN_W + 2 * CONV_W
COL_GATES = 3 * ATTN_W + 3 * CONV_W
N_FORGET_COL = 3 * ATTN_W


def _params(sem=None, vmem=VMEM_LIMIT):
    return pltpu.CompilerParams(dimension_semantics=sem, vmem_limit_bytes=vmem)


def _dot(a, b):
    return lax.dot_general(a, b, (((1,), (0,)), ((), ())), preferred_element_type=F32)


def _dot_nt(a, b):
    return lax.dot_general(a, b, (((1,), (1,)), ((), ())), preferred_element_type=F32)


def _dot_tn(a, b):
    return lax.dot_general(a, b, (((0,), (0,)), ((), ())), preferred_element_type=F32)


def _sigmoid(x):
    return 1.0 / (1.0 + jnp.exp(-x))


def _rms(xv):
    inv = lax.rsqrt(jnp.mean(xv * xv, axis=-1, keepdims=True) + RMS_EPS)
    return xv * inv, inv


class _Comm:
    def __init__(self, inputs, out_shape, scratch, start, finish):
        self.inputs, self.out_shape, self.scratch = list(inputs), list(out_shape), list(scratch)
        self.start, self.finish = start, finish


def _pallas(body, name, grid, in_specs, out_specs, out_shape, scratch, args, comm=None):
    sem = ("arbitrary",) * len(grid)
    if comm is None:
        outs = pl.pallas_call(body, name=name, grid=grid, in_specs=in_specs, out_specs=out_specs,
                              out_shape=out_shape, scratch_shapes=scratch, compiler_params=_params(sem))(*args)
        return list(outs), []
    n_in, n_out, n_scr = len(in_specs), len(out_specs), len(scratch)
    ci, co = len(comm.inputs), len(comm.out_shape)

    def riding(*refs):
        ins, refs = refs[:n_in], refs[n_in:]
        cins, refs = refs[:ci], refs[ci:]
        outs, refs = refs[:n_out], refs[n_out:]
        couts, refs = refs[:co], refs[co:]
        scr, sems = refs[:n_scr], refs[n_scr:]
        ids = [pl.program_id(d) for d in range(len(grid))]
        first = functools.reduce(lambda a, b: a & b, [i == 0 for i in ids])
        last = functools.reduce(lambda a, b: a & b, [i == g - 1 for i, g in zip(ids, grid)])

        @pl.when(first)
        def _():
            comm.start(cins, couts, sems)

        body(*ins, *outs, *scr)

        @pl.when(last)
        def _():
            comm.finish(cins, couts, sems)

    any_spec = pl.BlockSpec(memory_space=pl.ANY)
    outs = pl.pallas_call(
        riding, name=name, grid=grid,
        in_specs=list(in_specs) + [any_spec] * ci, out_specs=list(out_specs) + [any_spec] * co,
        out_shape=list(out_shape) + comm.out_shape, scratch_shapes=list(scratch) + comm.scratch,
        compiler_params=_params(sem))(*args, *comm.inputs)
    return list(outs[:n_out]), list(outs[n_out:])


def _rms_bwd(dn, xhat, inv, g):
    dxhat = dn * g
    dx = inv * (dxhat - xhat * jnp.mean(dxhat * xhat, axis=-1, keepdims=True))
    return dx, jnp.sum(dn * xhat, axis=0, keepdims=True)


def _ffn_fwd(name, x, g, wgt, wut, wd, tm, comm=None):
    T, D = x.shape
    K, Fs, _ = wgt.shape

    def body(x_ref, g_ref, wg_ref, wu_ref, wd_ref, out_ref, hg_ref, hu_ref, n_scr, acc_scr):
        k = pl.program_id(1)

        @pl.when(k == 0)
        def _():
            xhat, _ = _rms(x_ref[...])
            n_scr[...] = (xhat * g_ref[...]).astype(BF16)
            acc_scr[...] = jnp.zeros_like(acc_scr)

        n = n_scr[...]
        hg = _dot_nt(n, wg_ref[...])
        hu = _dot_nt(n, wu_ref[...])
        hg_ref[...] = hg.astype(BF16)
        hu_ref[...] = hu.astype(BF16)
        act = (hg * _sigmoid(hg) * hu).astype(BF16)
        acc_scr[...] += _dot(act, wd_ref[...])

        @pl.when(k == K - 1)
        def _():
            out_ref[...] = x_ref[...] + FFN_RES * acc_scr[...]

    w_spec = pl.BlockSpec((None, Fs, D), lambda i, k: (k, 0, 0))
    act_spec = pl.BlockSpec((None, tm, Fs), lambda i, k: (k, i, 0))
    return _pallas(
        body, name, (T // tm, K),
        [pl.BlockSpec((tm, D), lambda i, k: (i, 0)), pl.BlockSpec((1, D), lambda i, k: (0, 0)),
         w_spec, w_spec, w_spec],
        [pl.BlockSpec((tm, D), lambda i, k: (i, 0)), act_spec, act_spec],
        [jax.ShapeDtypeStruct((T, D), F32), jax.ShapeDtypeStruct((K, T, Fs), BF16),
         jax.ShapeDtypeStruct((K, T, Fs), BF16)],
        [pltpu.VMEM((tm, D), BF16), pltpu.VMEM((tm, D), F32)],
        (x, g, wgt, wut, wd), comm)


def _ffn_up(name, x, g, wgt, wut, tm, comm=None):
    T, D = x.shape
    K, Fs, _ = wgt.shape

    def body(x_ref, g_ref, wg_ref, wu_ref, hg_ref, hu_ref, n_scr):
        @pl.when(pl.program_id(1) == 0)
        def _():
            xhat, _ = _rms(x_ref[...])
            n_scr[...] = (xhat * g_ref[...]).astype(BF16)

        n = n_scr[...]
        hg_ref[...] = _dot_nt(n, wg_ref[...]).astype(BF16)
        hu_ref[...] = _dot_nt(n, wu_ref[...]).astype(BF16)

    w_spec = pl.BlockSpec((None, Fs, D), lambda i, k: (k, 0, 0))
    act_spec = pl.BlockSpec((None, tm, Fs), lambda i, k: (k, i, 0))
    return _pallas(
        body, name, (T // tm, K),
        [pl.BlockSpec((tm, D), lambda i, k: (i, 0)), pl.BlockSpec((1, D), lambda i, k: (0, 0)), w_spec, w_spec],
        [act_spec, act_spec],
        [jax.ShapeDtypeStruct((K, T, Fs), BF16), jax.ShapeDtypeStruct((K, T, Fs), BF16)],
        [pltpu.VMEM((tm, D), BF16)],
        (x, g, wgt, wut), comm)


def _ffn_down(name, x, hg, hu, wd, tm, comm=None):
    T, D = x.shape
    K, Fs, _ = wd.shape

    def body(x_ref, hg_ref, hu_ref, wd_ref, out_ref, acc_scr):
        k = pl.program_id(1)

        @pl.when(k == 0)
        def _():
            acc_scr[...] = jnp.zeros_like(acc_scr)

        hgv = hg_ref[...].astype(F32)
        act = (hgv * _sigmoid(hgv) * hu_ref[...].astype(F32)).astype(BF16)
        acc_scr[...] += _dot(act, wd_ref[...])

        @pl.when(k == K - 1)
        def _():
            out_ref[...] = x_ref[...] + FFN_RES * acc_scr[...]

    act_spec = pl.BlockSpec((None, tm, Fs), lambda i, k: (k, i, 0))
    row = pl.BlockSpec((tm, D), lambda i, k: (i, 0))
    return _pallas(
        body, name, (T // tm, K),
        [row, act_spec, act_spec, pl.BlockSpec((None, Fs, D), lambda i, k: (k, 0, 0))],
        [row], [jax.ShapeDtypeStruct((T, D), F32)], [pltpu.VMEM((tm, D), F32)],
        (x, hg, hu, wd), comm)


def _ffn_bwd_dx(name, dout, x, g, hg, hu, wgt, wut, wd, tm, comm=None):
    T, D = x.shape
    K, Fs, _ = wgt.shape

    def body(dout_ref, x_ref, g_ref, hg_ref, hu_ref, wg_ref, wu_ref, wd_ref,
             dx_ref, dhg_ref, dhu_ref, dg_ref, df_scr, dn_scr):
        i, k = pl.program_id(0), pl.program_id(1)

        @pl.when(k == 0)
        def _():
            df_scr[...] = (FFN_RES * dout_ref[...]).astype(BF16)
            dn_scr[...] = jnp.zeros_like(dn_scr)

        @pl.when((k == 0) & (i == 0))
        def _():
            dg_ref[...] = jnp.zeros_like(dg_ref)

        for r0 in range(0, tm, ROW_BLOCK):
            rows = slice(r0, r0 + ROW_BLOCK)
            dact = _dot_nt(df_scr[rows, :], wd_ref[...])
            hgv = hg_ref[rows, :].astype(F32)
            huv = hu_ref[rows, :].astype(F32)
            s = _sigmoid(hgv)
            dhu = (dact * (hgv * s)).astype(BF16)
            dhg = (dact * huv * (s * (1.0 + hgv * (1.0 - s)))).astype(BF16)
            dhg_ref[rows, :] = dhg
            dhu_ref[rows, :] = dhu
            dn_scr[rows, :] += _dot(dhg, wg_ref[...]) + _dot(dhu, wu_ref[...])

        @pl.when(k == K - 1)
        def _():
            xhat, inv = _rms(x_ref[...])
            dx, dg = _rms_bwd(dn_scr[...], xhat, inv, g_ref[...])
            dx_ref[...] = dout_ref[...] + dx
            dg_ref[...] += dg

    w_spec = pl.BlockSpec((None, Fs, D), lambda i, k: (k, 0, 0))
    act_spec = pl.BlockSpec((None, tm, Fs), lambda i, k: (k, i, 0))
    row = pl.BlockSpec((tm, D), lambda i, k: (i, 0))
    vec = pl.BlockSpec((1, D), lambda i, k: (0, 0))
    return _pallas(
        body, name, (T // tm, K),
        [row, row, vec, act_spec, act_spec, w_spec, w_spec, w_spec],
        [row, act_spec, act_spec, vec],
        [jax.ShapeDtypeStruct((T, D), F32), jax.ShapeDtypeStruct((K, T, Fs), BF16),
         jax.ShapeDtypeStruct((K, T, Fs), BF16), jax.ShapeDtypeStruct((1, D), F32)],
        [pltpu.VMEM((tm, D), BF16), pltpu.VMEM((tm, D), F32)],
        (dout, x, g, hg, hu, wgt, wut, wd), comm)


def _ffn_bwd_dw(name, dout, x, g, hg, hu, dhg, dhu, tk, comm=None):
    T, D = x.shape
    K, _, Fs = hg.shape
    nt = T // tk

    def body(dout_ref, x_ref, g_ref, hg_ref, hu_ref, dhg_ref, dhu_ref,
             dwg_ref, dwu_ref, dwd_ref, accg, accu, accd):
        t = pl.program_id(1)

        @pl.when(t == 0)
        def _():
            accg[...] = jnp.zeros_like(accg)
            accu[...] = jnp.zeros_like(accu)
            accd[...] = jnp.zeros_like(accd)

        xhat, _ = _rms(x_ref[...])
        n = (xhat * g_ref[...]).astype(BF16)
        df = (FFN_RES * dout_ref[...]).astype(BF16)
        hgv = hg_ref[...].astype(F32)
        act = (hgv * _sigmoid(hgv) * hu_ref[...].astype(F32)).astype(BF16)
        accg[...] += _dot_tn(dhg_ref[...], n)
        accu[...] += _dot_tn(dhu_ref[...], n)
        accd[...] += _dot_tn(act, df)

        @pl.when(t == nt - 1)
        def _():
            dwg_ref[...] = accg[...].astype(BF16)
            dwu_ref[...] = accu[...].astype(BF16)
            dwd_ref[...] = accd[...].astype(BF16)

    act_spec = pl.BlockSpec((None, tk, Fs), lambda k, t: (k, t, 0))
    w_spec = pl.BlockSpec((None, Fs, D), lambda k, t: (k, 0, 0))
    return _pallas(
        body, name, (K, nt),
        [pl.BlockSpec((tk, D), lambda k, t: (t, 0)), pl.BlockSpec((tk, D), lambda k, t: (t, 0)),
         pl.BlockSpec((1, D), lambda k, t: (0, 0)), act_spec, act_spec, act_spec, act_spec],
        [w_spec, w_spec, w_spec],
        [jax.ShapeDtypeStruct((K, Fs, D), BF16)] * 3,
        [pltpu.VMEM((Fs, D), F32)] * 3,
        (dout, x, g, hg, hu, dhg, dhu), comm)


def _mix_proj_fwd(x, g, wproj_t, wf_t, tm, tn):
    T, D = x.shape
    N = wproj_t.shape[0]

    def body(x_ref, g_ref, w_ref, wf_ref, h_ref, proj_ref, flog_ref, h_scr):
        @pl.when(pl.program_id(1) == 0)
        def _():
            xhat, _ = _rms(x_ref[...])
            h = (xhat * g_ref[...]).astype(BF16)
            h_scr[...] = h
            h_ref[...] = h
            flog_ref[...] = _dot_nt(h, wf_ref[...])

        proj_ref[...] = _dot_nt(h_scr[...], w_ref[...]).astype(BF16)

    return pl.pallas_call(
        body, name="mix_proj_fwd", grid=(T // tm, N // tn),
        in_specs=[pl.BlockSpec((tm, D), lambda i, n: (i, 0)),
                  pl.BlockSpec((1, D), lambda i, n: (0, 0)),
                  pl.BlockSpec((tn, D), lambda i, n: (n, 0)),
                  pl.BlockSpec((LANES, D), lambda i, n: (0, 0))],
        out_specs=[pl.BlockSpec((tm, D), lambda i, n: (i, 0)),
                   pl.BlockSpec((tm, tn), lambda i, n: (i, n)),
                   pl.BlockSpec((tm, LANES), lambda i, n: (i, 0))],
        out_shape=[jax.ShapeDtypeStruct((T, D), BF16),
                   jax.ShapeDtypeStruct((T, N), BF16),
                   jax.ShapeDtypeStruct((T, LANES), F32)],
        scratch_shapes=[pltpu.VMEM((tm, D), BF16)],
        compiler_params=_params(("arbitrary", "arbitrary")),
    )(x, g, wproj_t, wf_t)


def _log_sigmoid(z):
    return -(jnp.maximum(-z, 0.0) + jnp.log(1.0 + jnp.exp(-jnp.abs(z))))


def _tri(n, lower):
    r = lax.broadcasted_iota(jnp.int32, (n, n), 0)
    c = lax.broadcasted_iota(jnp.int32, (n, n), 1)
    return jnp.where((r >= c) if lower else (r <= c), 1.0, 0.0).astype(F32)


def _dot_f32(a, b):
    return lax.dot_general(a, b, (((1,), (0,)), ((), ())), preferred_element_type=F32,
                           precision=lax.Precision.HIGHEST)


def _fgate_fwd(flog, bias, B, S, ch):
    def body(flog_ref, b_ref, cum_ref):
        tri = _tri(ch, True)
        carry = jnp.zeros((1, LANES), F32)
        for c0 in range(0, S, ch):
            lf = _log_sigmoid(flog_ref[c0:c0 + ch, :] + b_ref[...])
            cs = _dot_f32(tri, lf) + carry
            cum_ref[c0:c0 + ch, :] = cs
            carry = cs[ch - 1:ch, :]

    return pl.pallas_call(
        body, name="fgate_fwd", grid=(B,),
        in_specs=[pl.BlockSpec((S, LANES), lambda b: (b, 0)),
                  pl.BlockSpec((1, LANES), lambda b: (0, 0))],
        out_specs=pl.BlockSpec((S, LANES), lambda b: (b, 0)),
        out_shape=jax.ShapeDtypeStruct((B * S, LANES), F32),
        compiler_params=_params(("arbitrary",)),
    )(flog, bias)


def _fgate_bwd(dcum, flog, bias, B, S, ch):
    def body(dcum_ref, flog_ref, b_ref, dflog_ref, db_ref):
        @pl.when(pl.program_id(0) == 0)
        def _():
            db_ref[...] = jnp.zeros_like(db_ref)

        tri = _tri(ch, False)
        carry = jnp.zeros((1, LANES), F32)
        db = jnp.zeros((1, LANES), F32)
        for c0 in range(S - ch, -1, -ch):
            dlf = _dot_f32(tri, dcum_ref[c0:c0 + ch, :]) + carry
            carry = dlf[0:1, :]
            z = flog_ref[c0:c0 + ch, :] + b_ref[...]
            dz = dlf * _sigmoid(-z)
            dflog_ref[c0:c0 + ch, :] = dz
            db = db + jnp.sum(dz, axis=0, keepdims=True)
        db_ref[...] += db

    return pl.pallas_call(
        body, name="fgate_bwd", grid=(B,),
        in_specs=[pl.BlockSpec((S, LANES), lambda b: (b, 0)),
                  pl.BlockSpec((S, LANES), lambda b: (b, 0)),
                  pl.BlockSpec((1, LANES), lambda b: (0, 0))],
        out_specs=[pl.BlockSpec((S, LANES), lambda b: (b, 0)),
                   pl.BlockSpec((1, LANES), lambda b: (0, 0))],
        out_shape=[jax.ShapeDtypeStruct((B * S, LANES), F32),
                   jax.ShapeDtypeStruct((1, LANES), F32)],
        compiler_params=_params(("arbitrary",)),
    )(dcum, flog, bias)


def _pick_lane(tile, h):
    lane = lax.broadcasted_iota(jnp.int32, tile.shape, 1)
    return jnp.sum(jnp.where(lane == h, tile, 0.0), axis=1, keepdims=True)


def _put_lane(col, h, width=LANES):
    lane = lax.broadcasted_iota(jnp.int32, (col.shape[0], width), 1)
    return jnp.where(lane == h, col, 0.0)


def _pick_row(tile, h):
    row = lax.broadcasted_iota(jnp.int32, tile.shape, 0)
    return jnp.sum(jnp.where(row == h, tile, 0.0), axis=0, keepdims=True)


def _put_row(vec, h):
    row = lax.broadcasted_iota(jnp.int32, (8, vec.shape[1]), 0)
    return jnp.where(row == h, vec, 0.0)


def _causal(tq):
    r = lax.broadcasted_iota(jnp.int32, (tq, tq), 0)
    c = lax.broadcasted_iota(jnp.int32, (tq, tq), 1)
    return r >= c


def _head_halves(t):
    lo = lax.broadcasted_iota(jnp.int32, t.shape, 1) < HEAD_DIM
    zero = jnp.zeros_like(t)
    return jnp.where(lo, t, zero), jnp.where(lo, zero, t)


NEG = -1e30
ATTN_SCALE = 1.0 / math.sqrt(HEAD_DIM)


def _scaled(q):
    return (q.astype(F32) * ATTN_SCALE).astype(q.dtype)


def _attn_fwd(proj, cum, cum_t, B, S, tq, comm=None):
    nq = S // tq

    def body(q_ref, k_ref, v_ref, cum_ref, cumt_ref, o_ref, lse_ref):
        qi, hp = pl.program_id(1), pl.program_id(2)
        qm = _head_halves(_scaled(q_ref[...]))
        cumv = cum_ref[...]
        cq = [_pick_lane(cumv, 2 * hp + e) for e in range(2)]

        def tile(j, carry, masked):
            off = pl.multiple_of(j * tq, tq)
            kj = k_ref[pl.ds(off, tq), :]
            vj = v_ref[pl.ds(off, tq), :]
            ct = cumt_ref[j]
            new = []
            for e in range(2):
                m, l, acc = carry[e]
                s = _dot_nt(qm[e], kj) + (cq[e] - _pick_row(ct, 2 * hp + e))
                if masked:
                    s = jnp.where(_causal(tq), s, NEG)
                m_new = jnp.maximum(m, jnp.max(s, axis=1, keepdims=True))
                p = jnp.exp(s - m_new)
                alpha = jnp.exp(m - m_new)
                l = alpha * l + jnp.sum(p, axis=1, keepdims=True)
                acc = alpha * acc + _dot(p.astype(BF16), vj)
                new.append((m_new, l, acc))
            return tuple(new)

        one = (jnp.full((tq, 1), NEG, F32), jnp.zeros((tq, 1), F32), jnp.zeros((tq, LANES), F32))
        carry = lax.fori_loop(0, qi, lambda j, c: tile(j, c, False), (one, one))
        (ma, la, acca), (mb, lb, accb) = tile(qi, carry, True)
        lo = lax.broadcasted_iota(jnp.int32, (tq, LANES), 1) < HEAD_DIM
        o_ref[...] = jnp.where(lo, acca / la, accb / lb).astype(BF16)

        @pl.when(hp == 0)
        def _():
            lse_ref[...] = jnp.zeros_like(lse_ref)

        lse_ref[...] += _put_lane(ma + jnp.log(la), 2 * hp) + _put_lane(mb + jnp.log(lb), 2 * hp + 1)

    kv = lambda first: pl.BlockSpec((S, LANES), lambda b, i, hp: (b, first + hp))
    return _pallas(
        body, "attn_fwd", (B, nq, HEAD_PAIRS),
        [pl.BlockSpec((tq, LANES), lambda b, i, hp: (b * nq + i, hp)),
         kv(ATTN_W // LANES), kv(2 * ATTN_W // LANES),
         pl.BlockSpec((tq, LANES), lambda b, i, hp: (b * nq + i, 0)),
         pl.BlockSpec((None, nq, 8, tq), lambda b, i, hp: (b, 0, 0, 0))],
        [pl.BlockSpec((tq, LANES), lambda b, i, hp: (b * nq + i, hp)),
         pl.BlockSpec((tq, LANES), lambda b, i, hp: (b * nq + i, 0))],
        [jax.ShapeDtypeStruct((B * S, ATTN_W), BF16), jax.ShapeDtypeStruct((B * S, LANES), F32)],
        [], (proj, proj, proj, cum, cum_t), comm)


def _attn_bwd(proj, o, do, lse, cum, cum_t, B, S, tq, comm=None):
    nq = S // tq

    def body(q_ref, k_ref, v_ref, o_ref, do_ref, lse_ref, cum_ref, cumt_ref,
             dq_ref, dk_ref, dv_ref, dcq_ref, dck_ref, dq_scr):
        hp, kj = pl.program_id(1), pl.program_id(2)

        @pl.when(kj == 0)
        def _():
            dq_scr[...] = jnp.zeros_like(dq_scr)

        @pl.when((kj == 0) & (hp == 0))
        def _():
            dcq_ref[...] = jnp.zeros_like(dcq_ref)
            dck_ref[...] = jnp.zeros_like(dck_ref)

        kv = k_ref[...]
        vv = v_ref[...]
        km = _head_halves(kv)
        ct = cumt_ref[...]
        ck = [_pick_row(ct, 2 * hp + e) for e in range(2)]

        def tile(i, carry, masked):
            dk, dv, dcol = carry
            off = pl.multiple_of(i * tq, tq)
            qi = q_ref[pl.ds(off, tq), :]
            ov = o_ref[pl.ds(off, tq), :].astype(F32)
            qm = _head_halves(_scaled(qi))
            dom = _head_halves(do_ref[pl.ds(off, tq), :])
            cumv = cum_ref[pl.ds(off, tq), :]
            lsev = lse_ref[pl.ds(off, tq), :]
            dcq = jnp.zeros((tq, LANES), F32)
            dq = jnp.zeros((tq, LANES), F32)
            dcol_new = []
            for e in range(2):
                delta = jnp.sum(dom[e].astype(F32) * ov, axis=1, keepdims=True)
                row_term = _pick_lane(cumv, 2 * hp + e) - _pick_lane(lsev, 2 * hp + e)
                p = jnp.exp(_dot_nt(qm[e], kv) + row_term - ck[e])
                if masked:
                    p = jnp.where(_causal(tq), p, 0.0)
                dv = dv + _dot_tn(p.astype(BF16), dom[e])
                ds = p * (_dot_nt(dom[e], vv) - delta)
                dcol_new.append(dcol[e] + jnp.sum(ds, axis=0, keepdims=True))
                dcq = dcq + _put_lane(jnp.sum(ds, axis=1, keepdims=True), 2 * hp + e)
                dsb = ds.astype(BF16)
                dk = dk + _dot_tn(dsb, qm[e])
                dq = dq + _dot(dsb, km[e]) * ATTN_SCALE
            dq_scr[pl.ds(off, tq), :] += dq
            dcq_ref[pl.ds(off, tq), :] += dcq
            return dk, dv, tuple(dcol_new)

        zero_row = jnp.zeros((1, tq), F32)
        init = (jnp.zeros((tq, LANES), F32), jnp.zeros((tq, LANES), F32), (zero_row, zero_row))
        carry = tile(kj, init, True)
        dk, dv, dcol = lax.fori_loop(kj + 1, nq, lambda i, c: tile(i, c, False), carry)
        dk_ref[...] = dk.astype(BF16)
        dv_ref[...] = dv.astype(BF16)
        dck_ref[kj] += -(_put_row(dcol[0], 2 * hp) + _put_row(dcol[1], 2 * hp + 1))

        @pl.when(kj == nq - 1)
        def _():
            dq_ref[...] = dq_scr[...].astype(BF16)

    seq = lambda first: pl.BlockSpec((S, LANES), lambda b, hp, j: (b, first + hp))
    tile_in = lambda first: pl.BlockSpec((tq, LANES), lambda b, hp, j: (b * nq + j, first + hp))
    lanes0 = pl.BlockSpec((S, LANES), lambda b, hp, j: (b, 0))
    out = jax.ShapeDtypeStruct((B * S, ATTN_W), BF16)
    return _pallas(
        body, "attn_bwd", (B, HEAD_PAIRS, nq),
        [seq(0), tile_in(ATTN_W // LANES), tile_in(2 * ATTN_W // LANES), seq(0), seq(0), lanes0, lanes0,
         pl.BlockSpec((None, None, 8, tq), lambda b, hp, j: (b, j, 0, 0))],
        [seq(0), tile_in(0), tile_in(0), lanes0,
         pl.BlockSpec((None, nq, 8, tq), lambda b, hp, j: (b, 0, 0, 0))],
        [out, out, out, jax.ShapeDtypeStruct((B * S, LANES), F32), jax.ShapeDtypeStruct((B, nq, 8, tq), F32)],
        [pltpu.VMEM((S, LANES), F32)],
        (proj, proj, proj, o, do, lse, cum, cum_t), comm)


def _shift_down(u, n):
    row = lax.broadcasted_iota(jnp.int32, u.shape, 0)
    return jnp.where(row >= n, pltpu.roll(u, n, 0), 0.0)


def _shift_up(u, n):
    rows = u.shape[0]
    row = lax.broadcasted_iota(jnp.int32, u.shape, 0)
    return jnp.where(row < rows - n, pltpu.roll(u, rows - n, 0), 0.0)


def _conv_specs(S):
    cb = pl.BlockSpec((S, LANES), lambda g, b: (b, COL_CB // LANES + g))
    cc = pl.BlockSpec((S, LANES), lambda g, b: (b, COL_CC // LANES + g))
    cx = pl.BlockSpec((S, LANES), lambda g, b: (b, COL_CX // LANES + g))
    w = pl.BlockSpec((8, LANES), lambda g, b: (0, g))
    return cb, cc, cx, w


def _conv_fwd(proj, conv_w, B, S):
    def body(cb_ref, cc_ref, cx_ref, w_ref, y_ref):
        u = cc_ref[...].astype(F32) * cx_ref[...].astype(F32)
        w = w_ref[...]
        conv = w[0:1, :] * _shift_down(u, 2) + w[1:2, :] * _shift_down(u, 1) + w[2:3, :] * u
        y_ref[...] = (cb_ref[...].astype(F32) * conv).astype(BF16)

    cb, cc, cx, w = _conv_specs(S)
    return pl.pallas_call(
        body, name="conv_fwd", grid=(CONV_W // LANES, B),
        in_specs=[cb, cc, cx, w],
        out_specs=pl.BlockSpec((S, LANES), lambda g, b: (b, g)),
        out_shape=jax.ShapeDtypeStruct((B * S, CONV_W), BF16),
        compiler_params=_params(("arbitrary", "arbitrary")),
    )(proj, proj, proj, conv_w)


def _conv_bwd(dy, proj, conv_w, B, S):
    def body(dy_ref, cb_ref, cc_ref, cx_ref, w_ref, dcb_ref, dcc_ref, dcx_ref, dw_ref):
        @pl.when(pl.program_id(1) == 0)
        def _():
            dw_ref[...] = jnp.zeros_like(dw_ref)

        ccv = cc_ref[...].astype(F32)
        cxv = cx_ref[...].astype(F32)
        u = ccv * cxv
        u1 = _shift_down(u, 1)
        u2 = _shift_down(u, 2)
        w = w_ref[...]
        conv = w[0:1, :] * u2 + w[1:2, :] * u1 + w[2:3, :] * u
        dyv = dy_ref[...].astype(F32)
        dcb_ref[...] = (dyv * conv).astype(BF16)
        dconv = dyv * cb_ref[...].astype(F32)
        du = w[2:3, :] * dconv + w[1:2, :] * _shift_up(dconv, 1) + w[0:1, :] * _shift_up(dconv, 2)
        dcc_ref[...] = (du * cxv).astype(BF16)
        dcx_ref[...] = (du * ccv).astype(BF16)
        row = lax.broadcasted_iota(jnp.int32, (8, LANES), 0)
        dw = jnp.where(row == 0, jnp.sum(dconv * u2, axis=0, keepdims=True),
                       jnp.where(row == 1, jnp.sum(dconv * u1, axis=0, keepdims=True),
                                 jnp.where(row == 2, jnp.sum(dconv * u, axis=0, keepdims=True), 0.0)))
        dw_ref[...] += dw

    cb, cc, cx, w = _conv_specs(S)
    out = pl.BlockSpec((S, LANES), lambda g, b: (b, g))
    return pl.pallas_call(
        body, name="conv_bwd", grid=(CONV_W // LANES, B),
        in_specs=[out, cb, cc, cx, w],
        out_specs=[out, out, out, w],
        out_shape=[jax.ShapeDtypeStruct((B * S, CONV_W), BF16)] * 3 + [jax.ShapeDtypeStruct((8, CONV_W), F32)],
        compiler_params=_params(("arbitrary", "arbitrary")),
    )(dy, proj, proj, proj, conv_w)


def _gate_specs(tm, D):
    ga = pl.BlockSpec((tm, D), lambda i: (i, COL_GATES // D))
    gc = pl.BlockSpec((tm, D), lambda i: (i, COL_GATES // D + 1))
    return ga, gc


def _mix_out_fwd(x, o, yc, proj, woa, woc, wout, tm):
    T, D = x.shape

    def body(x_ref, o_ref, yc_ref, ga_ref, gc_ref, woa_ref, woc_ref, wout_ref, out_ref):
        ya = _dot(o_ref[...], woa_ref[...])
        yp = _dot(yc_ref[...], woc_ref[...])
        merged = _sigmoid(ga_ref[...].astype(F32)) * ya + _sigmoid(gc_ref[...].astype(F32)) * yp
        out_ref[...] = x_ref[...] + _dot(merged.astype(BF16), wout_ref[...])

    ga, gc = _gate_specs(tm, D)
    row = lambda w: pl.BlockSpec((tm, w), lambda i: (i, 0))
    whole = lambda a: pl.BlockSpec(a.shape, lambda i: (0, 0))
    return pl.pallas_call(
        body, name="mix_out_fwd", grid=(T // tm,),
        in_specs=[row(D), row(ATTN_W), row(CONV_W), ga, gc, whole(woa), whole(woc), whole(wout)],
        out_specs=row(D),
        out_shape=jax.ShapeDtypeStruct((T, D), F32),
        compiler_params=_params(("arbitrary",)),
    )(x, o, yc, proj, proj, woa, woc, wout)


def _mix_out_bwd(dx, o, yc, proj, woa, woc, wout, tm, comm=None):
    T, D = dx.shape
    nt = T // tm

    def body(dx_ref, o_ref, yc_ref, ga_ref, gc_ref, woa_ref, woc_ref, wout_ref,
             do_ref, dyc_ref, dg_ref, dwoa_ref, dwoc_ref, dwout_ref, acca, accc, acco):
        t = pl.program_id(0)

        @pl.when(t == 0)
        def _():
            acca[...] = jnp.zeros_like(acca)
            accc[...] = jnp.zeros_like(accc)
            acco[...] = jnp.zeros_like(acco)

        dxb = dx_ref[...].astype(BF16)
        ov, ycv = o_ref[...], yc_ref[...]
        ya = _dot(ov, woa_ref[...])
        yp = _dot(ycv, woc_ref[...])
        sa = _sigmoid(ga_ref[...].astype(F32))
        sc = _sigmoid(gc_ref[...].astype(F32))
        merged = (sa * ya + sc * yp).astype(BF16)
        dm = _dot_nt(dxb, wout_ref[...])
        dya = (dm * sa).astype(BF16)
        dyp = (dm * sc).astype(BF16)
        dg_ref[:, :D] = (dm * ya * sa * (1.0 - sa)).astype(BF16)
        dg_ref[:, D:] = (dm * yp * sc * (1.0 - sc)).astype(BF16)
        do_ref[...] = _dot_nt(dya, woa_ref[...]).astype(BF16)
        dyc_ref[...] = _dot_nt(dyp, woc_ref[...]).astype(BF16)
        acca[...] += _dot_tn(ov, dya)
        accc[...] += _dot_tn(ycv, dyp)
        acco[...] += _dot_tn(merged, dxb)

        @pl.when(t == nt - 1)
        def _():
            dwoa_ref[...] = acca[...].astype(BF16)
            dwoc_ref[...] = accc[...].astype(BF16)
            dwout_ref[...] = acco[...].astype(BF16)

    ga, gc = _gate_specs(tm, D)
    row = lambda w: pl.BlockSpec((tm, w), lambda i: (i, 0))
    whole = lambda a: pl.BlockSpec(a.shape, lambda i: (0, 0))
    return _pallas(
        body, "mix_out_bwd", (nt,),
        [row(D), row(ATTN_W), row(CONV_W), ga, gc, whole(woa), whole(woc), whole(wout)],
        [row(ATTN_W), row(CONV_W), row(2 * D), whole(woa), whole(woc), whole(wout)],
        [jax.ShapeDtypeStruct((T, ATTN_W), BF16), jax.ShapeDtypeStruct((T, CONV_W), BF16),
         jax.ShapeDtypeStruct((T, 2 * D), BF16),
         jax.ShapeDtypeStruct(woa.shape, BF16), jax.ShapeDtypeStruct(woc.shape, BF16),
         jax.ShapeDtypeStruct(wout.shape, BF16)],
        [pltpu.VMEM(woa.shape, F32), pltpu.VMEM(woc.shape, F32), pltpu.VMEM(wout.shape, F32)],
        (dx, o, yc, proj, proj, woa, woc, wout), comm)


def _proj_pieces(dq, dk, dv, dcb, dcc, dcx, dgates, dflog):
    D = dgates.shape[1] // 2
    return [(dq, ATTN_W, 0), (dk, ATTN_W, 0), (dv, ATTN_W, 0), (dcb, CONV_W, 0), (dcc, CONV_W, 0), (dcx, CONV_W, 0),
            (dgates, D, 0), (dgates, D, 1), (dflog, LANES, 0)]


def _mix_proj_bwd_dx(dres, x, g, pieces, wproj_t, wf_t, tm, comm=None):
    T, D = x.shape
    n = len(pieces)
    w_blocks = [(ATTN_W, 0), (ATTN_W, 1), (ATTN_W, 2), (CONV_W, 3), (CONV_W, 4), (CONV_W, 5),
                (D, COL_GATES // D), (D, COL_GATES // D + 1)]

    def body(*refs):
        dres_ref, x_ref, g_ref = refs[:3]
        p_refs, w_refs = refs[3:3 + n], refs[3 + n:3 + 2 * n]
        dx_ref, dg_ref = refs[3 + 2 * n:]

        @pl.when(pl.program_id(0) == 0)
        def _():
            dg_ref[...] = jnp.zeros_like(dg_ref)

        dh = _dot(p_refs[0][...].astype(BF16), w_refs[0][...])
        for p_ref, w_ref in zip(p_refs[1:], w_refs[1:]):
            dh = dh + _dot(p_ref[...].astype(BF16), w_ref[...])
        xhat, inv = _rms(x_ref[...])
        dx, dg = _rms_bwd(dh, xhat, inv, g_ref[...])
        dx_ref[...] = dres_ref[...] + dx
        dg_ref[...] += dg

    row = pl.BlockSpec((tm, D), lambda i: (i, 0))
    vec = pl.BlockSpec((1, D), lambda i: (0, 0))
    p_specs = [pl.BlockSpec((tm, w), lambda i, cb=cb: (i, cb)) for _, w, cb in pieces]
    w_specs = [pl.BlockSpec((r, D), lambda i, rb=rb: (rb, 0)) for r, rb in w_blocks]
    w_specs.append(pl.BlockSpec((LANES, D), lambda i: (0, 0)))
    return _pallas(
        body, "mix_proj_bwd_dx", (T // tm,),
        [row, row, vec] + p_specs + w_specs, [row, vec],
        [jax.ShapeDtypeStruct((T, D), F32), jax.ShapeDtypeStruct((1, D), F32)], [],
        (dres, x, g, *[p for p, _, _ in pieces], *([wproj_t] * len(w_blocks)), wf_t), comm)


def _matmuls_tn(name, pieces, b, tk):
    T, N = b.shape
    nt = T // tk
    n = len(pieces)

    def body(*refs):
        a_refs, b_ref, out_refs, accs = refs[:n], refs[n], refs[n + 1:2 * n + 1], refs[2 * n + 1:]
        t = pl.program_id(0)

        @pl.when(t == 0)
        def _():
            for acc in accs:
                acc[...] = jnp.zeros_like(acc)

        bv = b_ref[...]
        for a_ref, acc in zip(a_refs, accs):
            acc[...] += _dot_tn(a_ref[...].astype(BF16), bv)

        @pl.when(t == nt - 1)
        def _():
            for out_ref, acc in zip(out_refs, accs):
                out_ref[...] = acc[...].astype(BF16)

    return pl.pallas_call(
        body, name=name, grid=(nt,),
        in_specs=[pl.BlockSpec((tk, w), lambda t, cb=cb: (t, cb)) for _, w, cb in pieces]
        + [pl.BlockSpec((tk, N), lambda t: (t, 0))],
        out_specs=[pl.BlockSpec((w, N), lambda t: (0, 0)) for _, w, _ in pieces],
        out_shape=[jax.ShapeDtypeStruct((w, N), BF16) for _, w, _ in pieces],
        scratch_shapes=[pltpu.VMEM((w, N), F32) for _, w, _ in pieces],
        compiler_params=_params(("arbitrary",)),
    )(*[a for a, _, _ in pieces], b)


def _final_loss(x, target, g, tm):
    T, D = x.shape

    def body(x_ref, t_ref, g_ref, dx_ref, loss_ref, dg_ref):
        @pl.when(pl.program_id(0) == 0)
        def _():
            loss_ref[...] = jnp.zeros_like(loss_ref)
            dg_ref[...] = jnp.zeros_like(dg_ref)

        xhat, inv = _rms(x_ref[...])
        err = xhat * g_ref[...] - t_ref[...]
        loss_ref[...] += 0.5 * jnp.sum(jnp.sum(err * err, axis=1, keepdims=True), axis=0, keepdims=True) / D
        dx, dg = _rms_bwd(err * (1.0 / D), xhat, inv, g_ref[...])
        dx_ref[...] = dx
        dg_ref[...] += dg

    row = pl.BlockSpec((tm, D), lambda i: (i, 0))
    return pl.pallas_call(
        body, name="final_loss", grid=(T // tm,),
        in_specs=[row, row, pl.BlockSpec((1, D), lambda i: (0, 0))],
        out_specs=[row, pl.BlockSpec((1, LANES), lambda i: (0, 0)), pl.BlockSpec((1, D), lambda i: (0, 0))],
        out_shape=[jax.ShapeDtypeStruct((T, D), F32), jax.ShapeDtypeStruct((1, LANES), F32),
                   jax.ShapeDtypeStruct((1, D), F32)],
        compiler_params=_params(("arbitrary",)),
    )(x, target, g)


class _LocalPlan:
    def __init__(self, stacks, small):
        self.stacks, self.small, self.grads = stacks, small, {}

    def weights(self, group):
        return _LAYOUTS[group](self.stacks, self.small)

    def rider(self, kernel_name):
        return None

    def arrived(self, kernel_name, results):
        pass

    def reduce(self, group, grads):
        self.grads.update(grads)


def _local_step(x, target, plan, B, S):
    T, D = x.shape
    tm = min(512, T)
    tm_fwd = min(1024, T)
    tq = min(512, S)
    nq = S // tq
    ch = min(256, S)

    def riding(kernel_name, build):
        results, brought = build(plan.rider(kernel_name))
        plan.arrived(kernel_name, brought)
        return results

    w1 = plan.weights("ffn1_in")
    hg1, hu1 = riding("ffn1_up", lambda comm: _ffn_up(
        "ffn1_up", x, w1["ffn1_norm"], w1["ffn1_gate"], w1["ffn1_up"], tm_fwd, comm))
    w1 = plan.weights("ffn1")
    x1, = riding("ffn1_down", lambda comm: _ffn_down("ffn1_down", x, hg1, hu1, w1["ffn1_down"], tm_fwd, comm))
    wm = plan.weights("mix")
    h, proj, flog = _mix_proj_fwd(x1, wm["mix_norm"], wm["w_proj"], wm["w_f"], tm_fwd, 1280)
    cum = _fgate_fwd(flog, wm["b_forget"], B, S, ch)
    cum_t = jnp.transpose(cum[:, :N_HEADS].reshape(B, nq, tq, N_HEADS), (0, 1, 3, 2))
    o, lse = riding("attn_fwd", lambda comm: _attn_fwd(proj, cum, cum_t, B, S, tq, comm))
    yc = _conv_fwd(proj, wm["conv_w"], B, S)
    x2 = _mix_out_fwd(x1, o, yc, proj, wm["w_o_attn"], wm["w_o_conv"], wm["w_out"], tm)
    w2 = plan.weights("ffn2")
    x3, hg2, hu2 = _ffn_fwd("ffn2_fwd", x2, w2["ffn2_norm"], w2["ffn2_gate"], w2["ffn2_up"], w2["ffn2_down"], tm_fwd)[0]
    dx3, loss, d_final_norm = _final_loss(x3, target, w2["final_norm"], tm)

    g = {"final_norm": d_final_norm}
    dx2, dhg2, dhu2, g["ffn2_norm"] = _ffn_bwd_dx("ffn2_bwd_dx", dx3, x2, w2["ffn2_norm"], hg2, hu2,
                                                  w2["ffn2_gate"], w2["ffn2_up"], w2["ffn2_down"], tm)[0]
    plan.reduce("ffn2", dict(zip(("ffn2_gate", "ffn2_up", "ffn2_down"),
                                 _ffn_bwd_dw("ffn2_bwd_dw", dx3, x2, w2["ffn2_norm"], hg2, hu2, dhg2, dhu2, tm)[0])))
    do, dyc, dgates, dwoa, dwoc, dwout = riding("mix_out_bwd", lambda comm: _mix_out_bwd(
        dx2, o, yc, proj, wm["w_o_attn"], wm["w_o_conv"], wm["w_out"], tm, comm))
    plan.reduce("out", dict(w_o_attn=_shard_cols(dwoa), w_o_conv=_shard_cols(dwoc), w_out=dwout.reshape(N_CHIPS, -1, D)))
    dq, dk, dv, dcq, dck = riding("attn_bwd", lambda comm: _attn_bwd(proj, o, do, lse, cum, cum_t, B, S, tq, comm))
    dcum = dcq + jnp.pad(jnp.transpose(dck, (0, 1, 3, 2)).reshape(T, N_HEADS), ((0, 0), (0, LANES - N_HEADS)))
    dflog, g["b_forget"] = _fgate_bwd(dcum, flog, wm["b_forget"], B, S, ch)
    dcb, dcc, dcx, g["conv_w"] = _conv_bwd(dyc, proj, wm["conv_w"], B, S)
    pieces = _proj_pieces(dq, dk, dv, dcb, dcc, dcx, dgates, dflog)
    dwq, dwk, dwv, dwcb, dwcc, dwcx = _matmuls_tn("mix_dw_a", pieces[:6], h, tm)
    dwga, dwgc, dwf = _matmuls_tn("mix_dw_b", pieces[6:], h, tm)
    dwin_t = jnp.concatenate([dwq, dwk, dwv, dwf[:N_HEADS], dwcb, dwcc, dwcx, dwga, dwgc], axis=0)
    plan.reduce("w_in", {"w_in": dwin_t.reshape(N_CHIPS, -1, D)})
    dx1, g["mix_norm"] = riding("mix_proj_bwd_dx", lambda comm: _mix_proj_bwd_dx(
        dx2, x1, wm["mix_norm"], pieces, wm["w_proj"], wm["w_f"], min(256, T), comm))
    grad_x, dhg1, dhu1, g["ffn1_norm"] = _ffn_bwd_dx(
        "ffn1_bwd_dx", dx1, x, w1["ffn1_norm"], hg1, hu1, w1["ffn1_gate"], w1["ffn1_up"], w1["ffn1_down"], tm)[0]
    plan.reduce("ffn1", dict(zip(("ffn1_gate", "ffn1_up", "ffn1_down"), riding("ffn1_bwd_dw", lambda comm: _ffn_bwd_dw(
        "ffn1_bwd_dw", dx1, x, w1["ffn1_norm"], hg1, hu1, dhg1, dhu1, tm, comm)))))
    return loss, grad_x, g


TRANSPOSED = ("ffn1_gate", "ffn1_up", "ffn2_gate", "ffn2_up", "w_in")
NORMS = ("ffn1_norm", "mix_norm", "ffn2_norm", "final_norm")


def _unshard_cols(a):
    return jnp.transpose(a, (1, 0, 2)).reshape(a.shape[1], N_CHIPS * a.shape[2])


def _shard_cols(a):
    return jnp.transpose(a.reshape(a.shape[0], N_CHIPS, a.shape[1] // N_CHIPS), (1, 0, 2))


def _layout_ffn(which):
    def layout(st, small):
        w = {n: st[n] for n in (which + "_gate", which + "_up", which + "_down")}
        w[which + "_norm"] = small[which + "_norm"].reshape(1, -1)
        if which == "ffn2":
            w["final_norm"] = small["final_norm"].reshape(1, -1)
        return w
    return layout


def _layout_mix(st, small):
    win_t = st["w_in"].reshape(-1, st["w_in"].shape[2])
    return {
        "w_proj": jnp.concatenate([win_t[:N_FORGET_COL], win_t[N_FORGET_COL + N_HEADS:]], axis=0),
        "w_f": jnp.pad(win_t[N_FORGET_COL:N_FORGET_COL + N_HEADS], ((0, LANES - N_HEADS), (0, 0))),
        "w_o_attn": _unshard_cols(st["w_o_attn"]),
        "w_o_conv": _unshard_cols(st["w_o_conv"]),
        "w_out": st["w_out"].reshape(-1, st["w_out"].shape[2]),
        "conv_w": _unshard_cols(st["conv_w"]),
        "mix_norm": small["mix_norm"].reshape(1, -1),
        "b_forget": jnp.pad(small["b_forget"].reshape(1, -1), ((0, 0), (0, LANES - N_HEADS))),
    }


def _layout_ffn1_in(st, small):
    return {"ffn1_gate": st["ffn1_gate"], "ffn1_up": st["ffn1_up"], "ffn1_norm": small["ffn1_norm"].reshape(1, -1)}


_LAYOUTS = {"ffn1_in": _layout_ffn1_in, "ffn1": _layout_ffn("ffn1"), "mix": _layout_mix, "ffn2": _layout_ffn("ffn2")}


ANY = pl.BlockSpec(memory_space=pl.ANY)
BIG = ("ffn1_gate", "ffn1_up", "ffn1_down", "w_in", "w_o_attn", "w_o_conv", "w_out",
       "ffn2_gate", "ffn2_up", "ffn2_down")


def _place():
    x, y, c = lax.axis_index("x"), lax.axis_index("y"), lax.axis_index("c")
    others = [(1 - x, y), (x, 1 - y), (1 - x, 1 - y)]
    return x, y, c, others


def _col_halves(cols, c):
    hc = cols // 2
    return pl.ds(pl.multiple_of(c * hc, LANES), hc), pl.ds(pl.multiple_of((1 - c) * hc, LANES), hc)


def _gather_comm(shards, conv_shard=None):
    n = len(shards)
    inputs = list(shards) + ([] if conv_shard is None else [conv_shard])

    def copies(ins, outs, sems):
        send_sems, recv_sems, pass_send, pass_recv = sems[:4]
        x, y, c, others = _place()

        def chip_copy(a, j, chip):
            mine, _ = _col_halves(ins[a].shape[1], c)
            return pltpu.make_async_remote_copy(
                src_ref=ins[a].at[:, mine], dst_ref=outs[a].at[chip, :, mine],
                send_sem=send_sems.at[3 * a + j], recv_sem=recv_sems.at[3 * a + j],
                device_id=(*others[j], c), device_id_type=MESH)

        def pass_copy(a, j, chip, half):
            return pltpu.make_async_remote_copy(
                src_ref=outs[a].at[chip, :, half], dst_ref=outs[a].at[chip, :, half],
                send_sem=pass_send.at[3 * a + j], recv_sem=pass_recv.at[3 * a + j],
                device_id=(x, y, 1 - c), device_id_type=MESH)

        def conv_copy(j, chip):
            return pltpu.make_async_remote_copy(
                src_ref=ins[n], dst_ref=outs[n].at[chip],
                send_sem=sems[4].at[j], recv_sem=sems[5].at[j],
                device_id=(*others[j], c), device_id_type=MESH)

        me = 2 * x + y
        sends = [chip_copy(a, j, me) for a in range(n) for j in range(3)]
        if conv_shard is not None:
            sends += [conv_copy(j, me) for j in range(3)]
        return c, others, sends, chip_copy, pass_copy, conv_copy

    def start(ins, outs, sems):
        for cp in copies(ins, outs, sems)[2]:
            cp.start()

    def finish(ins, outs, sems):
        c, others, sends, chip_copy, pass_copy, conv_copy = copies(ins, outs, sems)
        passed = []
        for a in range(n):
            mine, _ = _col_halves(ins[a].shape[1], c)
            for j, (ox, oy) in enumerate(others):
                chip_copy(a, j, 2 * ox + oy).wait_recv()
                passed.append(pass_copy(a, j, 2 * ox + oy, mine))
                passed[-1].start()
        for a in range(n):
            _, theirs = _col_halves(ins[a].shape[1], c)
            for j, (ox, oy) in enumerate(others):
                pass_copy(a, j, 2 * ox + oy, theirs).wait_recv()
        if conv_shard is not None:
            for j, (ox, oy) in enumerate(others):
                conv_copy(j, 2 * ox + oy).wait_recv()
        for cp in sends + passed:
            cp.wait_send()

    scratch = [pltpu.SemaphoreType.DMA((3 * n,))] * 4
    if conv_shard is not None:
        scratch += [pltpu.SemaphoreType.DMA((3,))] * 2
    return _Comm(inputs, [jax.ShapeDtypeStruct((N_CHIPS,) + s.shape, s.dtype) for s in inputs], scratch, start, finish)


def _fill_own(stacks, shards):
    chip = 2 * lax.axis_index("x") + lax.axis_index("y")
    return [lax.dynamic_update_index_in_dim(st, s, chip, 0) for st, s in zip(stacks, shards)]


def _run_comm(name, comm):
    ci, co = len(comm.inputs), len(comm.out_shape)

    def body(*refs):
        comm.start(refs[:ci], refs[ci:ci + co], refs[ci + co:])
        comm.finish(refs[:ci], refs[ci:ci + co], refs[ci + co:])

    return pl.pallas_call(body, name=name, in_specs=[ANY] * ci, out_specs=[ANY] * co, out_shape=comm.out_shape,
                          scratch_shapes=comm.scratch)(*comm.inputs)


def _sibling_exchange_comm(grads):
    n = len(grads)

    def copies(ins, outs, sems):
        x, y, c, _ = _place()
        return [pltpu.make_async_remote_copy(
            src_ref=ins[a].at[:, :, _col_halves(ins[a].shape[2], c)[1]], dst_ref=outs[a],
            send_sem=sems[0].at[a], recv_sem=sems[1].at[a],
            device_id=(x, y, 1 - c), device_id_type=MESH) for a in range(n)]

    def start(ins, outs, sems):
        for cp in copies(ins, outs, sems):
            cp.start()

    def finish(ins, outs, sems):
        for cp in copies(ins, outs, sems):
            cp.wait()

    half = lambda s: jax.ShapeDtypeStruct((s.shape[0], s.shape[1], s.shape[2] // 2), s.dtype)
    return _Comm(grads, [half(s) for s in grads], [pltpu.SemaphoreType.DMA((n,))] * 2, start, finish)


def _merge_comms(comms):
    def split(refs, count):
        out, at = [], 0
        for cm in comms:
            out.append(refs[at:at + count(cm)])
            at += count(cm)
        return out

    def parts(ins, outs, sems):
        return zip(comms, split(ins, lambda cm: len(cm.inputs)), split(outs, lambda cm: len(cm.out_shape)),
                   split(sems, lambda cm: len(cm.scratch)))

    def start(ins, outs, sems):
        for cm, i, o, s in parts(ins, outs, sems):
            cm.start(i, o, s)

    def finish(ins, outs, sems):
        for cm, i, o, s in parts(ins, outs, sems):
            cm.finish(i, o, s)

    return _Comm(sum([cm.inputs for cm in comms], []), sum([cm.out_shape for cm in comms], []),
                 sum([cm.scratch for cm in comms], []), start, finish)


def _add_halves(name, grads, recvs, core):
    n = len(grads)

    def body(core_ref, *refs):
        for g_ref, r_ref, out_ref in zip(refs[:n], refs[n:2 * n], refs[2 * n:]):
            out_ref[...] = (g_ref[...].astype(F32) + r_ref[...].astype(F32)).astype(BF16)

    half = lambda g: pl.BlockSpec((None, g.shape[1], g.shape[2] // 2), lambda k, core_ref: (k, 0, 0))
    mine = lambda g: pl.BlockSpec((None, g.shape[1], g.shape[2] // 2), lambda k, core_ref: (k, 0, core_ref[0]))
    return pl.pallas_call(
        body, name=name,
        grid_spec=pltpu.PrefetchScalarGridSpec(
            num_scalar_prefetch=1, grid=(N_CHIPS,),
            in_specs=[mine(g) for g in grads] + [half(g) for g in grads],
            out_specs=[half(g) for g in grads]),
        out_shape=[jax.ShapeDtypeStruct(r.shape, BF16) for r in recvs],
        compiler_params=_params(("arbitrary",)),
    )(core, *grads, *recvs)


def _chip_exchange_comm(parts):
    n = len(parts)

    def copies(ins, outs, sems):
        x, y, c, others = _place()
        return [pltpu.make_async_remote_copy(
            src_ref=ins[a].at[2 * ox + oy], dst_ref=outs[a].at[j],
            send_sem=sems[0].at[3 * a + j], recv_sem=sems[1].at[3 * a + j],
            device_id=(ox, oy, c), device_id_type=MESH) for a in range(n) for j, (ox, oy) in enumerate(others)]

    def start(ins, outs, sems):
        for cp in copies(ins, outs, sems):
            cp.start()

    def finish(ins, outs, sems):
        for cp in copies(ins, outs, sems):
            cp.wait()

    return _Comm(parts, [jax.ShapeDtypeStruct((3,) + s.shape[1:], s.dtype) for s in parts],
                 [pltpu.SemaphoreType.DMA((3 * n,))] * 2, start, finish)


HBM = pl.BlockSpec(memory_space=pltpu.HBM)
SEM = pl.BlockSpec(memory_space=pltpu.SEMAPHORE)


def _split_exchange_copies(parts, lands, send_sems, recv_sems):
    x, y, c, others = _place()
    return [pltpu.make_async_remote_copy(
        src_ref=parts[a].at[2 * ox + oy], dst_ref=lands[a].at[j],
        send_sem=send_sems.at[3 * a + j], recv_sem=recv_sems.at[3 * a + j],
        device_id=(ox, oy, c), device_id_type=MESH) for a in range(len(parts)) for j, (ox, oy) in enumerate(others)]


def _exchange_start(name, parts):
    n = len(parts)

    def body(*refs):
        ins, lands = refs[:n], refs[n:2 * n]
        send_sems, recv_sems, token = refs[2 * n], refs[2 * n + 1], refs[-1]
        for cp in _split_exchange_copies(ins, lands, send_sems, recv_sems):
            cp.start()
        token[...] = jnp.zeros_like(token)

    land_shape = [(3,) + p.shape[1:] for p in parts]
    outs = pl.pallas_call(
        body, name=name,
        out_shape=[pltpu.SemaphoreType.DMA((3 * n,)), pltpu.SemaphoreType.DMA((3 * n,))]
        + [pltpu.HBM(p.shape, p.dtype) for p in parts] + [pltpu.HBM(s, p.dtype) for s, p in zip(land_shape, parts)]
        + [jax.ShapeDtypeStruct((8, LANES), F32)],
        in_specs=[HBM] * (2 * n), out_specs=[SEM, SEM] + [HBM] * (2 * n) + [pl.BlockSpec(memory_space=pltpu.VMEM)],
        input_output_aliases={i: 2 + i for i in range(2 * n)},
        compiler_params=pltpu.CompilerParams(has_side_effects=pltpu.SideEffectType.DATAFLOW_SIDE_EFFECTING),
    )(*[pltpu.with_memory_space_constraint(p, pltpu.HBM) for p in parts],
      *[pltpu.with_memory_space_constraint(lax.empty(s, p.dtype), pltpu.HBM) for s, p in zip(land_shape, parts)])
    return outs[0], outs[1], list(outs[2:2 + n]), list(outs[2 + n:2 + 2 * n]), outs[-1]


def _exchange_wait(name, send_sems, recv_sems, parts, lands, after):
    n = len(parts)

    def body(*refs):
        ins, zones = refs[:n], refs[n:2 * n]
        for cp in _split_exchange_copies(ins, zones, refs[2 * n], refs[2 * n + 1]):
            cp.wait_send()
            cp.wait_recv()

    outs = pl.pallas_call(
        body, name=name,
        out_shape=[pltpu.HBM(p.shape, p.dtype) for p in parts] + [pltpu.HBM(z.shape, z.dtype) for z in lands],
        in_specs=[HBM] * (2 * n) + [SEM, SEM] + [ANY] * len(after), out_specs=[HBM] * (2 * n),
        input_output_aliases={i: i for i in range(2 * n)},
        compiler_params=pltpu.CompilerParams(has_side_effects=pltpu.SideEffectType.DATAFLOW_SIDE_EFFECTING),
    )(*parts, *lands, send_sems, recv_sems, *after)
    return list(outs[:n]), list(outs[n:])


def _sum_chips(name, owns, recvs, chip, after):
    n = len(owns)
    hc = owns[0].shape[2]
    assert all(o.shape[2] == hc for o in owns)

    def body(chip_ref, *refs):
        for own_ref, recv_ref, out_ref in zip(refs[:n], refs[n:2 * n], refs[2 * n + 1:]):
            acc = own_ref[...].astype(F32)
            for j in range(3):
                acc = acc + recv_ref[j].astype(F32)
            out_ref[...] = acc

    return pl.pallas_call(
        body, name=name,
        grid_spec=pltpu.PrefetchScalarGridSpec(
            num_scalar_prefetch=1, grid=(hc // LANES,),
            in_specs=[pl.BlockSpec((None, o.shape[1], LANES), lambda i, chip_ref: (chip_ref[0], 0, i)) for o in owns]
            + [pl.BlockSpec((3, o.shape[1], LANES), lambda i, chip_ref: (0, 0, i)) for o in owns]
            + [pl.BlockSpec((8, LANES), lambda i, chip_ref: (0, 0))],
            out_specs=[pl.BlockSpec((o.shape[1], LANES), lambda i, chip_ref: (0, i)) for o in owns]),
        out_shape=[jax.ShapeDtypeStruct((o.shape[1], hc), F32) for o in owns],
        compiler_params=_params(("arbitrary",)),
    )(chip, *owns, *recvs, after)


def _share_halves(name, halves):
    n = len(halves)

    def body(*refs):
        srcs, dsts = refs[:n], refs[n:2 * n]
        send_sems, recv_sems = refs[2 * n:]
        x, y, c, _ = _place()
        copies = [pltpu.make_async_remote_copy(
            src_ref=srcs[a], dst_ref=dsts[a], send_sem=send_sems.at[a], recv_sem=recv_sems.at[a],
            device_id=(x, y, 1 - c), device_id_type=MESH) for a in range(n)]
        for cp in copies:
            cp.start()
        for cp in copies:
            cp.wait()

    return pl.pallas_call(
        body, name=name,
        in_specs=[ANY] * n, out_specs=[ANY] * n,
        out_shape=[jax.ShapeDtypeStruct(s.shape, s.dtype) for s in halves],
        scratch_shapes=[pltpu.SemaphoreType.DMA((n,)), pltpu.SemaphoreType.DMA((n,))],
    )(*halves)


def _allreduce_small(part):
    rows = part.shape[0]

    def body(in_ref, out_ref, land, send_sems, recv_sems):
        x, y, c, _ = _place()
        me = 4 * x + 2 * y + c
        land[me] = in_ref[...]
        copies = []
        for d in range(1, N_DEV):
            peer = (1 - x if d & 4 else x, 1 - y if d & 2 else y, 1 - c if d & 1 else c)
            copies.append(pltpu.make_async_remote_copy(
                src_ref=in_ref, dst_ref=land.at[me],
                send_sem=send_sems.at[d - 1], recv_sem=recv_sems.at[d - 1],
                device_id=peer, device_id_type=MESH))
        for cp in copies:
            cp.start()
        for d in range(1, N_DEV):
            px, py, pc = (1 - x if d & 4 else x, 1 - y if d & 2 else y, 1 - c if d & 1 else c)
            pltpu.make_async_remote_copy(
                src_ref=in_ref, dst_ref=land.at[4 * px + 2 * py + pc],
                send_sem=send_sems.at[d - 1], recv_sem=recv_sems.at[d - 1],
                device_id=(px, py, pc), device_id_type=MESH).wait_recv()
        for cp in copies:
            cp.wait_send()
        acc = land[0]
        for k in range(1, N_DEV):
            acc = acc + land[k]
        out_ref[...] = acc

    vmem = pl.BlockSpec(memory_space=pltpu.VMEM)
    return pl.pallas_call(
        body, name="allreduce_small",
        in_specs=[vmem], out_specs=vmem,
        out_shape=jax.ShapeDtypeStruct(part.shape, F32),
        scratch_shapes=[pltpu.VMEM((N_DEV, rows, LANES), F32),
                        pltpu.SemaphoreType.DMA((N_DEV - 1,)), pltpu.SemaphoreType.DMA((N_DEV - 1,))],
    )(part)


def _adam_update(w, g, m, v):
    nm = ADAM_B1 * m + (1.0 - ADAM_B1) * g
    nv = ADAM_B2 * v + (1.0 - ADAM_B2) * (g * g)
    m_hat = nm * (1.0 / (1.0 - ADAM_B1 ** ADAM_STEP))
    v_hat = nv * (1.0 / (1.0 - ADAM_B2 ** ADAM_STEP))
    return -ADAM_LR * (m_hat / (jnp.sqrt(v_hat) + ADAM_EPS) + ADAM_WD * w), nm, nv


def _adamw(name, w, g, m, v):
    def body(w_ref, g_ref, m_ref, v_ref, d_ref, nm_ref, nv_ref):
        d_ref[...], nm_ref[...], nv_ref[...] = _adam_update(w_ref[...], g_ref[...], m_ref[...], v_ref[...])

    spec = pl.BlockSpec(w.shape, lambda i: (0, 0))
    out = jax.ShapeDtypeStruct(w.shape, F32)
    return pl.pallas_call(
        body, name=name, grid=(1,),
        in_specs=[spec] * 4, out_specs=[spec] * 3, out_shape=[out] * 3,
        compiler_params=_params(("arbitrary",)),
    )(w, g, m, v)


def _adamw_halves(name, ws, mines, theirs, ms, vs, core):
    n = len(ws)
    cols = ws[0].shape[1]
    assert all(w.shape[1] == cols for w in ws)
    hc = cols // 2
    tc = LANES if n > 1 else min(256, hc)
    nt = hc // tc

    def body(core_ref, *refs):
        ins, outs = refs[:5 * n], refs[5 * n:]
        for a in range(n):
            w_ref, mine_ref, theirs_ref, m_ref, v_ref = [ins[j * n + a] for j in range(5)]
            g_ref, d_ref, nm_ref, nv_ref = outs[4 * a:4 * a + 4]
            gv = jnp.where(pl.program_id(0) == core_ref[0], mine_ref[...], theirs_ref[...])
            g_ref[...] = gv
            d_ref[...], nm_ref[...], nv_ref[...] = _adam_update(w_ref[...], gv, m_ref[...], v_ref[...])

    whole = lambda w: pl.BlockSpec((w.shape[0], tc), lambda h, i, core_ref: (0, h * nt + i))
    mine_spec = lambda w: pl.BlockSpec((w.shape[0], tc), lambda h, i, core_ref: (0, jnp.where(h == core_ref[0], i, 0)))
    theirs_spec = lambda w: pl.BlockSpec((w.shape[0], tc), lambda h, i, core_ref: (0, jnp.where(h == core_ref[0], 0, i)))
    outs = pl.pallas_call(
        body, name=name,
        grid_spec=pltpu.PrefetchScalarGridSpec(
            num_scalar_prefetch=1, grid=(2, nt),
            in_specs=[whole(w) for w in ws] + [mine_spec(w) for w in ws] + [theirs_spec(w) for w in ws]
            + [whole(w) for w in ws] * 2,
            out_specs=[whole(w) for w in ws for _ in range(4)]),
        out_shape=[jax.ShapeDtypeStruct(w.shape, F32) for w in ws for _ in range(4)],
        compiler_params=_params(("arbitrary", "arbitrary")),
    )(core, *ws, *mines, *theirs, *ms, *vs)
    return [outs[4 * a:4 * a + 4] for a in range(n)]


WEIGHTS = ("ffn1_norm", "ffn1_gate", "ffn1_up", "ffn1_down", "mix_norm", "w_in", "b_forget", "conv_w",
           "w_o_attn", "w_o_conv", "w_out", "ffn2_norm", "ffn2_gate", "ffn2_up", "ffn2_down", "final_norm")
VEC_ROWS = 8


def _pack_small(t, conv_rows):
    conv = t["conv_w"]
    parts = [t[n].reshape(VEC_ROWS, LANES) for n in NORMS]
    parts.append(jnp.pad(conv, ((0, conv_rows - conv.shape[0]), (0, 0))))
    parts.append(jnp.pad(t["b_forget"].reshape(1, N_HEADS), ((0, 7), (0, LANES - N_HEADS))))
    return jnp.concatenate(parts, axis=0)


def _unpack_small(p, conv_rows):
    out = {n: p[VEC_ROWS * i:VEC_ROWS * (i + 1)].reshape(-1) for i, n in enumerate(NORMS)}
    base = VEC_ROWS * len(NORMS)
    out["conv_w"] = p[base:base + 3]
    out["b_forget"] = p[base + conv_rows, :N_HEADS]
    return out


def _travel(name, a):
    return a.T if name in TRANSPOSED else a


GATHER_FIRST = ("ffn1_gate", "ffn1_up")
GATHER_RIDES = {"ffn1_up": ("ffn1_down", "w_in"), "ffn1_down": ("w_o_attn", "w_o_conv", "w_out"),
                "attn_fwd": ("ffn2_gate", "ffn2_up", "ffn2_down")}
SIBLING_RIDES = {"ffn2": "mix_out_bwd", "out": None, "w_in": "mix_proj_bwd_dx", "ffn1": None}
CHIP_RIDES = {"ffn2": "attn_bwd", "out": "attn_bwd", "w_in": "ffn1_bwd_dw", "ffn1": None}


class _MeshPlan(_LocalPlan):
    def __init__(self, wts, core):
        self.small, self.core = wts, core
        self.shards = {n: wts[n].astype(BF16) for n in BIG}
        self.chip_part, self.from_chips, self.rides = {}, {}, {}
        conv_shard = jnp.pad(wts["conv_w"], ((0, 8 - wts["conv_w"].shape[0]), (0, 0)))
        own = [self.shards[n] for n in GATHER_FIRST] + [conv_shard]
        got = _run_comm("gather_first", _gather_comm(own[:-1], conv_shard))
        self.stacks = dict(zip(GATHER_FIRST + ("conv_w",), _fill_own(got, own)))
        for kernel_name, names in GATHER_RIDES.items():
            mine = [self.shards[n] for n in names]
            self._ride(kernel_name, _gather_comm(mine),
                       lambda got, names=names, mine=mine: self.stacks.update(zip(names, _fill_own(got, mine))))

    def _ride(self, kernel_name, comm, then):
        self.rides.setdefault(kernel_name, []).append((comm, then))

    def rider(self, kernel_name):
        comms = [comm for comm, _ in self.rides.get(kernel_name, [])]
        return _merge_comms(comms) if comms else None

    def arrived(self, kernel_name, results):
        for comm, then in self.rides.pop(kernel_name, []):
            then(results[:len(comm.out_shape)])
            results = results[len(comm.out_shape):]

    def reduce(self, group, grads):
        names = tuple(grads)
        mine = [grads[n] for n in names]

        def with_sibling(from_sibling):
            parts = _add_halves("add_halves_" + group, mine, list(from_sibling), self.core)
            self.chip_part.update(zip(names, parts))
            if CHIP_RIDES[group] is None:
                self.last = (names, _exchange_start("exchange_start_" + group, parts))
            else:
                self._ride(CHIP_RIDES[group], _chip_exchange_comm(parts),
                           lambda got: self.from_chips.update(zip(names, got)))

        if SIBLING_RIDES[group] is None:
            with_sibling(_run_comm("sibling_exchange_" + group, _sibling_exchange_comm(mine)))
        else:
            self._ride(SIBLING_RIDES[group], _sibling_exchange_comm(mine), with_sibling)


def kernel(x, ffn1_norm, ffn1_gate, ffn1_up, ffn1_down, mix_norm, w_in, b_forget, conv_w, w_o_attn, w_o_conv, w_out, ffn2_norm, ffn2_gate, ffn2_up, ffn2_down, final_norm, loss_target, m_ffn1_norm, m_ffn1_gate, m_ffn1_up, m_ffn1_down, m_mix_norm, m_w_in, m_b_forget, m_conv_w, m_w_o_attn, m_w_o_conv, m_w_out, m_ffn2_norm, m_ffn2_gate, m_ffn2_up, m_ffn2_down, m_final_norm, v_ffn1_norm, v_ffn1_gate, v_ffn1_up, v_ffn1_down, v_mix_norm, v_w_in, v_b_forget, v_conv_w, v_w_o_attn, v_w_o_conv, v_w_out, v_ffn2_norm, v_ffn2_gate, v_ffn2_up, v_ffn2_down, v_final_norm):
    given = dict(locals())
    wts = {n: _travel(n, given[n]) for n in WEIGHTS}
    mom = {n: _travel(n, given["m_" + n]) for n in WEIGHTS}
    var = {n: _travel(n, given["v_" + n]) for n in WEIGHTS}
    B, S, D = x.shape
    chip = 2 * lax.axis_index("x") + lax.axis_index("y")
    chip1 = chip.astype(jnp.int32).reshape(1)
    core = lax.axis_index("c").astype(jnp.int32).reshape(1)

    plan = _MeshPlan(wts, core)
    loss, grad_x, gs = _local_step(x.reshape(B * S, D), loss_target.reshape(B * S, D), plan, B, S)

    last_names, (send_sems, recv_sems, parts_thru, lands, token) = plan.last
    delta, new_m, new_v, grads = {}, {}, {}, {}

    def finish(tag, names):
        by_cols = {}
        for n in names:
            by_cols.setdefault(wts[n].shape[1], []).append(n)
        mine = {}
        for cols, ns in by_cols.items():
            mine.update(zip(ns, _sum_chips("sum_chips_%s_%d" % (tag, cols), [plan.chip_part[n] for n in ns],
                                           [plan.from_chips[n] for n in ns], chip1, token)))
        theirs = dict(zip(names, _share_halves("share_halves_" + tag, [mine[n] for n in names])))
        raw = []
        for cols, ns in by_cols.items():
            outs = _adamw_halves("adamw_%s_%d" % (tag, cols), [wts[n] for n in ns], [mine[n] for n in ns],
                                 [theirs[n] for n in ns], [mom[n] for n in ns], [var[n] for n in ns], core)
            for n, per in zip(ns, outs):
                raw.append(per[-1])
                grads[n], delta[n], new_m[n], new_v[n] = [_travel(n, o) for o in per]
        return raw

    done = finish("early", [n for n in BIG if n not in last_names])
    parts_back, got = _exchange_wait("exchange_wait", send_sems, recv_sems, parts_thru, lands, done)
    plan.chip_part.update(zip(last_names, parts_back))
    plan.from_chips.update(zip(last_names, got))
    finish("last", last_names)

    conv_all = _shard_cols(gs["conv_w"]).reshape(N_CHIPS * 8, LANES)
    small_part = _pack_small({**{n: gs[n] for n in NORMS}, "conv_w": conv_all, "b_forget": gs["b_forget"][0, :N_HEADS]},
                             N_CHIPS * 8)
    base = VEC_ROWS * len(NORMS)
    small_sum = _allreduce_small(small_part)
    grads.update(_unpack_small(small_sum, N_CHIPS * 8))
    grads["conv_w"] = lax.dynamic_slice_in_dim(small_sum[base:base + N_CHIPS * 8], chip * 8, 8, axis=0)[:3]
    packs = [_pack_small(t, 8) for t in (wts, grads, mom, var)]
    for out, p in zip((delta, new_m, new_v), _adamw("adamw_small", *packs)):
        out.update(_unpack_small(p, 8))

    total = lax.psum(loss[0, 0], ("x", "y", "c"))
    return (total, grad_x.reshape(B, S, D), *[grads[n] for n in WEIGHTS], *[delta[n] for n in WEIGHTS],
            *[new_m[n] for n in WEIGHTS], *[new_v[n] for n in WEIGHTS])
```

```python
import functools
import math

import jax
import jax.numpy as jnp
from jax import lax
from jax.experimental import pallas as pl
from jax.experimental.pallas import tpu as pltpu

F32 = jnp.float32
BF16 = jnp.bfloat16
MESH = pl.DeviceIdType.MESH

N_CHIPS = 4
N_DEV = 8
N_HEADS = 8
HEAD_DIM = 64
HEAD_PAIRS = N_HEADS // 2
ATTN_W = N_HEADS * HEAD_DIM
CONV_W = 512
RMS_EPS = 1e-6
FFN_RES = 0.5
LANES = 128
VMEM_LIMIT = 56 * 1024 * 1024
ROW_BLOCK = 256

ADAM_LR = 0.001
ADAM_B1 = 0.9
ADAM_B2 = 0.999
ADAM_EPS = 1e-08
ADAM_WD = 0.01
ADAM_STEP = 10

PROJ_W = 3 * ATTN_W + 3 * CONV_W + 2 * 1024
COL_CB, COL_CC, COL_CX = 3 * ATTN_W, 3 * ATTN_W + CONV_W, 3 * ATTN_W + 2 * CONV_W
COL_GATES = 3 * ATTN_W + 3 * CONV_W
N_FORGET_COL = 3 * ATTN_W


def _params(sem=None, vmem=VMEM_LIMIT):
    return pltpu.CompilerParams(dimension_semantics=sem, vmem_limit_bytes=vmem)


def _dot(a, b):
    return lax.dot_general(a, b, (((1,), (0,)), ((), ())), preferred_element_type=F32)


def _dot_nt(a, b):
    return lax.dot_general(a, b, (((1,), (1,)), ((), ())), preferred_element_type=F32)


def _dot_tn(a, b):
    return lax.dot_general(a, b, (((0,), (0,)), ((), ())), preferred_element_type=F32)


def _sigmoid(x):
    return 1.0 / (1.0 + jnp.exp(-x))


def _rms(xv):
    inv = lax.rsqrt(jnp.mean(xv * xv, axis=-1, keepdims=True) + RMS_EPS)
    return xv * inv, inv


class _Comm:
    def __init__(self, inputs, out_shape, scratch, start, finish):
        self.inputs, self.out_shape, self.scratch = list(inputs), list(out_shape), list(scratch)
        self.start, self.finish = start, finish


def _pallas(body, name, grid, in_specs, out_specs, out_shape, scratch, args, comm=None):
    sem = ("arbitrary",) * len(grid)
    if comm is None:
        outs = pl.pallas_call(body, name=name, grid=grid, in_specs=in_specs, out_specs=out_specs,
                              out_shape=out_shape, scratch_shapes=scratch, compiler_params=_params(sem))(*args)
        return list(outs), []
    n_in, n_out, n_scr = len(in_specs), len(out_specs), len(scratch)
    ci, co = len(comm.inputs), len(comm.out_shape)

    def riding(*refs):
        ins, refs = refs[:n_in], refs[n_in:]
        cins, refs = refs[:ci], refs[ci:]
        outs, refs = refs[:n_out], refs[n_out:]
        couts, refs = refs[:co], refs[co:]
        scr, sems = refs[:n_scr], refs[n_scr:]
        ids = [pl.program_id(d) for d in range(len(grid))]
        first = functools.reduce(lambda a, b: a & b, [i == 0 for i in ids])
        last = functools.reduce(lambda a, b: a & b, [i == g - 1 for i, g in zip(ids, grid)])

        @pl.when(first)
        def _():
            comm.start(cins, couts, sems)

        body(*ins, *outs, *scr)

        @pl.when(last)
        def _():
            comm.finish(cins, couts, sems)

    any_spec = pl.BlockSpec(memory_space=pl.ANY)
    outs = pl.pallas_call(
        riding, name=name, grid=grid,
        in_specs=list(in_specs) + [any_spec] * ci, out_specs=list(out_specs) + [any_spec] * co,
        out_shape=list(out_shape) + comm.out_shape, scratch_shapes=list(scratch) + comm.scratch,
        compiler_params=_params(sem))(*args, *comm.inputs)
    return list(outs[:n_out]), list(outs[n_out:])


def _rms_bwd(dn, xhat, inv, g):
    dxhat = dn * g
    dx = inv * (dxhat - xhat * jnp.mean(dxhat * xhat, axis=-1, keepdims=True))
    return dx, jnp.sum(dn * xhat, axis=0, keepdims=True)


def _ffn_fwd(name, x, g, wgt, wut, wd, tm, comm=None):
    T, D = x.shape
    K, Fs, _ = wgt.shape

    def body(x_ref, g_ref, wg_ref, wu_ref, wd_ref, out_ref, hg_ref, hu_ref, n_scr, acc_scr):
        k = pl.program_id(1)

        @pl.when(k == 0)
        def _():
            xhat, _ = _rms(x_ref[...])
            n_scr[...] = (xhat * g_ref[...]).astype(BF16)
            acc_scr[...] = jnp.zeros_like(acc_scr)

        n = n_scr[...]
        hg = _dot_nt(n, wg_ref[...])
        hu = _dot_nt(n, wu_ref[...])
        hg_ref[...] = hg.astype(BF16)
        hu_ref[...] = hu.astype(BF16)
        act = (hg * _sigmoid(hg) * hu).astype(BF16)
        acc_scr[...] += _dot(act, wd_ref[...])

        @pl.when(k == K - 1)
        def _():
            out_ref[...] = x_ref[...] + FFN_RES * acc_scr[...]

    w_spec = pl.BlockSpec((None, Fs, D), lambda i, k: (k, 0, 0))
    act_spec = pl.BlockSpec((None, tm, Fs), lambda i, k: (k, i, 0))
    return _pallas(
        body, name, (T // tm, K),
        [pl.BlockSpec((tm, D), lambda i, k: (i, 0)), pl.BlockSpec((1, D), lambda i, k: (0, 0)),
         w_spec, w_spec, w_spec],
        [pl.BlockSpec((tm, D), lambda i, k: (i, 0)), act_spec, act_spec],
        [jax.ShapeDtypeStruct((T, D), F32), jax.ShapeDtypeStruct((K, T, Fs), BF16),
         jax.ShapeDtypeStruct((K, T, Fs), BF16)],
        [pltpu.VMEM((tm, D), BF16), pltpu.VMEM((tm, D), F32)],
        (x, g, wgt, wut, wd), comm)


def _ffn_up(name, x, g, wgt, wut, tm, comm=None):
    T, D = x.shape
    K, Fs, _ = wgt.shape

    def body(x_ref, g_ref, wg_ref, wu_ref, hg_ref, hu_ref, n_scr):
        @pl.when(pl.program_id(1) == 0)
        def _():
            xhat, _ = _rms(x_ref[...])
            n_scr[...] = (xhat * g_ref[...]).astype(BF16)

        n = n_scr[...]
        hg_ref[...] = _dot_nt(n, wg_ref[...]).astype(BF16)
        hu_ref[...] = _dot_nt(n, wu_ref[...]).astype(BF16)

    w_spec = pl.BlockSpec((None, Fs, D), lambda i, k: (k, 0, 0))
    act_spec = pl.BlockSpec((None, tm, Fs), lambda i, k: (k, i, 0))
    return _pallas(
        body, name, (T // tm, K),
        [pl.BlockSpec((tm, D), lambda i, k: (i, 0)), pl.BlockSpec((1, D), lambda i, k: (0, 0)), w_spec, w_spec],
        [act_spec, act_spec],
        [jax.ShapeDtypeStruct((K, T, Fs), BF16), jax.ShapeDtypeStruct((K, T, Fs), BF16)],
        [pltpu.VMEM((tm, D), BF16)],
        (x, g, wgt, wut), comm)


def _ffn_down(name, x, hg, hu, wd, tm, comm=None):
    T, D = x.shape
    K, Fs, _ = wd.shape

    def body(x_ref, hg_ref, hu_ref, wd_ref, out_ref, acc_scr):
        k = pl.program_id(1)

        @pl.when(k == 0)
        def _():
            acc_scr[...] = jnp.zeros_like(acc_scr)

        hgv = hg_ref[...].astype(F32)
        act = (hgv * _sigmoid(hgv) * hu_ref[...].astype(F32)).astype(BF16)
        acc_scr[...] += _dot(act, wd_ref[...])

        @pl.when(k == K - 1)
        def _():
            out_ref[...] = x_ref[...] + FFN_RES * acc_scr[...]

    act_spec = pl.BlockSpec((None, tm, Fs), lambda i, k: (k, i, 0))
    row = pl.BlockSpec((tm, D), lambda i, k: (i, 0))
    return _pallas(
        body, name, (T // tm, K),
        [row, act_spec, act_spec, pl.BlockSpec((None, Fs, D), lambda i, k: (k, 0, 0))],
        [row], [jax.ShapeDtypeStruct((T, D), F32)], [pltpu.VMEM((tm, D), F32)],
        (x, hg, hu, wd), comm)


def _ffn_bwd_dx(name, dout, x, g, hg, hu, wgt, wut, wd, tm, comm=None):
    T, D = x.shape
    K, Fs, _ = wgt.shape

    def body(dout_ref, x_ref, g_ref, hg_ref, hu_ref, wg_ref, wu_ref, wd_ref,
             dx_ref, dhg_ref, dhu_ref, dg_ref, df_scr, dn_scr):
        i, k = pl.program_id(0), pl.program_id(1)

        @pl.when(k == 0)
        def _():
            df_scr[...] = (FFN_RES * dout_ref[...]).astype(BF16)
            dn_scr[...] = jnp.zeros_like(dn_scr)

        @pl.when((k == 0) & (i == 0))
        def _():
            dg_ref[...] = jnp.zeros_like(dg_ref)

        for r0 in range(0, tm, ROW_BLOCK):
            rows = slice(r0, r0 + ROW_BLOCK)
            dact = _dot_nt(df_scr[rows, :], wd_ref[...])
            hgv = hg_ref[rows, :].astype(F32)
            huv = hu_ref[rows, :].astype(F32)
            s = _sigmoid(hgv)
            dhu = (dact * (hgv * s)).astype(BF16)
            dhg = (dact * huv * (s * (1.0 + hgv * (1.0 - s)))).astype(BF16)
            dhg_ref[rows, :] = dhg
            dhu_ref[rows, :] = dhu
            dn_scr[rows, :] += _dot(dhg, wg_ref[...]) + _dot(dhu, wu_ref[...])

        @pl.when(k == K - 1)
        def _():
            xhat, inv = _rms(x_ref[...])
            dx, dg = _rms_bwd(dn_scr[...], xhat, inv, g_ref[...])
            dx_ref[...] = dout_ref[...] + dx
            dg_ref[...] += dg

    w_spec = pl.BlockSpec((None, Fs, D), lambda i, k: (k, 0, 0))
    act_spec = pl.BlockSpec((None, tm, Fs), lambda i, k: (k, i, 0))
    row = pl.BlockSpec((tm, D), lambda i, k: (i, 0))
    vec = pl.BlockSpec((1, D), lambda i, k: (0, 0))
    return _pallas(
        body, name, (T // tm, K),
        [row, row, vec, act_spec, act_spec, w_spec, w_spec, w_spec],
        [row, act_spec, act_spec, vec],
        [jax.ShapeDtypeStruct((T, D), F32), jax.ShapeDtypeStruct((K, T, Fs), BF16),
         jax.ShapeDtypeStruct((K, T, Fs), BF16), jax.ShapeDtypeStruct((1, D), F32)],
        [pltpu.VMEM((tm, D), BF16), pltpu.VMEM((tm, D), F32)],
        (dout, x, g, hg, hu, wgt, wut, wd), comm)


def _ffn_bwd_dw(name, dout, x, g, hg, hu, dhg, dhu, tk, comm=None):
    T, D = x.shape
    K, _, Fs = hg.shape
    nt = T // tk

    def body(dout_ref, x_ref, g_ref, hg_ref, hu_ref, dhg_ref, dhu_ref,
             dwg_ref, dwu_ref, dwd_ref, accg, accu, accd):
        t = pl.program_id(1)

        @pl.when(t == 0)
        def _():
            accg[...] = jnp.zeros_like(accg)
            accu[...] = jnp.zeros_like(accu)
            accd[...] = jnp.zeros_like(accd)

        xhat, _ = _rms(x_ref[...])
        n = (xhat * g_ref[...]).astype(BF16)
        df = (FFN_RES * dout_ref[...]).astype(BF16)
        hgv = hg_ref[...].astype(F32)
        act = (hgv * _sigmoid(hgv) * hu_ref[...].astype(F32)).astype(BF16)
        accg[...] += _dot_tn(dhg_ref[...], n)
        accu[...] += _dot_tn(dhu_ref[...], n)
        accd[...] += _dot_tn(act, df)

        @pl.when(t == nt - 1)
        def _():
            dwg_ref[...] = accg[...].astype(BF16)
            dwu_ref[...] = accu[...].astype(BF16)
            dwd_ref[...] = accd[...].astype(BF16)

    act_spec = pl.BlockSpec((None, tk, Fs), lambda k, t: (k, t, 0))
    w_spec = pl.BlockSpec((None, Fs, D), lambda k, t: (k, 0, 0))
    return _pallas(
        body, name, (K, nt),
        [pl.BlockSpec((tk, D), lambda k, t: (t, 0)), pl.BlockSpec((tk, D), lambda k, t: (t, 0)),
         pl.BlockSpec((1, D), lambda k, t: (0, 0)), act_spec, act_spec, act_spec, act_spec],
        [w_spec, w_spec, w_spec],
        [jax.ShapeDtypeStruct((K, Fs, D), BF16)] * 3,
        [pltpu.VMEM((Fs, D), F32)] * 3,
        (dout, x, g, hg, hu, dhg, dhu), comm)


def _mix_proj_fwd(x, g, wproj_t, wf_t, tm, tn):
    T, D = x.shape
    N = wproj_t.shape[0]

    def body(x_ref, g_ref, w_ref, wf_ref, h_ref, proj_ref, flog_ref, h_scr):
        @pl.when(pl.program_id(1) == 0)
        def _():
            xhat, _ = _rms(x_ref[...])
            h = (xhat * g_ref[...]).astype(BF16)
            h_scr[...] = h
            h_ref[...] = h
            flog_ref[...] = _dot_nt(h, wf_ref[...])

        proj_ref[...] = _dot_nt(h_scr[...], w_ref[...]).astype(BF16)

    return pl.pallas_call(
        body, name="mix_proj_fwd", grid=(T // tm, N // tn),
        in_specs=[pl.BlockSpec((tm, D), lambda i, n: (i, 0)),
                  pl.BlockSpec((1, D), lambda i, n: (0, 0)),
                  pl.BlockSpec((tn, D), lambda i, n: (n, 0)),
                  pl.BlockSpec((LANES, D), lambda i, n: (0, 0))],
        out_specs=[pl.BlockSpec((tm, D), lambda i, n: (i, 0)),
                   pl.BlockSpec((tm, tn), lambda i, n: (i, n)),
                   pl.BlockSpec((tm, LANES), lambda i, n: (i, 0))],
        out_shape=[jax.ShapeDtypeStruct((T, D), BF16),
                   jax.ShapeDtypeStruct((T, N), BF16),
                   jax.ShapeDtypeStruct((T, LANES), F32)],
        scratch_shapes=[pltpu.VMEM((tm, D), BF16)],
        compiler_params=_params(("arbitrary", "arbitrary")),
    )(x, g, wproj_t, wf_t)


def _log_sigmoid(z):
    return -(jnp.maximum(-z, 0.0) + jnp.log(1.0 + jnp.exp(-jnp.abs(z))))


def _tri(n, lower):
    r = lax.broadcasted_iota(jnp.int32, (n, n), 0)
    c = lax.broadcasted_iota(jnp.int32, (n, n), 1)
    return jnp.where((r >= c) if lower else (r <= c), 1.0, 0.0).astype(F32)


def _dot_f32(a, b):
    return lax.dot_general(a, b, (((1,), (0,)), ((), ())), preferred_element_type=F32,
                           precision=lax.Precision.HIGHEST)


def _fgate_fwd(flog, bias, B, S, ch):
    def body(flog_ref, b_ref, cum_ref):
        tri = _tri(ch, True)
        carry = jnp.zeros((1, LANES), F32)
        for c0 in range(0, S, ch):
            lf = _log_sigmoid(flog_ref[c0:c0 + ch, :] + b_ref[...])
            cs = _dot_f32(tri, lf) + carry
            cum_ref[c0:c0 + ch, :] = cs
            carry = cs[ch - 1:ch, :]

    return pl.pallas_call(
        body, name="fgate_fwd", grid=(B,),
        in_specs=[pl.BlockSpec((S, LANES), lambda b: (b, 0)),
                  pl.BlockSpec((1, LANES), lambda b: (0, 0))],
        out_specs=pl.BlockSpec((S, LANES), lambda b: (b, 0)),
        out_shape=jax.ShapeDtypeStruct((B * S, LANES), F32),
        compiler_params=_params(("arbitrary",)),
    )(flog, bias)


def _fgate_bwd(dcum, flog, bias, B, S, ch):
    def body(dcum_ref, flog_ref, b_ref, dflog_ref, db_ref):
        @pl.when(pl.program_id(0) == 0)
        def _():
            db_ref[...] = jnp.zeros_like(db_ref)

        tri = _tri(ch, False)
        carry = jnp.zeros((1, LANES), F32)
        db = jnp.zeros((1, LANES), F32)
        for c0 in range(S - ch, -1, -ch):
            dlf = _dot_f32(tri, dcum_ref[c0:c0 + ch, :]) + carry
            carry = dlf[0:1, :]
            z = flog_ref[c0:c0 + ch, :] + b_ref[...]
            dz = dlf * _sigmoid(-z)
            dflog_ref[c0:c0 + ch, :] = dz
            db = db + jnp.sum(dz, axis=0, keepdims=True)
        db_ref[...] += db

    return pl.pallas_call(
        body, name="fgate_bwd", grid=(B,),
        in_specs=[pl.BlockSpec((S, LANES), lambda b: (b, 0)),
                  pl.BlockSpec((S, LANES), lambda b: (b, 0)),
                  pl.BlockSpec((1, LANES), lambda b: (0, 0))],
        out_specs=[pl.BlockSpec((S, LANES), lambda b: (b, 0)),
                   pl.BlockSpec((1, LANES), lambda b: (0, 0))],
        out_shape=[jax.ShapeDtypeStruct((B * S, LANES), F32),
                   jax.ShapeDtypeStruct((1, LANES), F32)],
        compiler_params=_params(("arbitrary",)),
    )(dcum, flog, bias)


def _pick_lane(tile, h):
    lane = lax.broadcasted_iota(jnp.int32, tile.shape, 1)
    return jnp.sum(jnp.where(lane == h, tile, 0.0), axis=1, keepdims=True)


def _put_lane(col, h, width=LANES):
    lane = lax.broadcasted_iota(jnp.int32, (col.shape[0], width), 1)
    return jnp.where(lane == h, col, 0.0)


def _pick_row(tile, h):
    row = lax.broadcasted_iota(jnp.int32, tile.shape, 0)
    return jnp.sum(jnp.where(row == h, tile, 0.0), axis=0, keepdims=True)


def _put_row(vec, h):
    row = lax.broadcasted_iota(jnp.int32, (8, vec.shape[1]), 0)
    return jnp.where(row == h, vec, 0.0)


def _causal(tq):
    r = lax.broadcasted_iota(jnp.int32, (tq, tq), 0)
    c = lax.broadcasted_iota(jnp.int32, (tq, tq), 1)
    return r >= c


def _head_halves(t):
    lo = lax.broadcasted_iota(jnp.int32, t.shape, 1) < HEAD_DIM
    zero = jnp.zeros_like(t)
    return jnp.where(lo, t, zero), jnp.where(lo, zero, t)


NEG = -1e30
ATTN_SCALE = 1.0 / math.sqrt(HEAD_DIM)


def _scaled(q):
    return (q.astype(F32) * ATTN_SCALE).astype(q.dtype)


def _attn_fwd(proj, cum, cum_t, B, S, tq, comm=None):
    nq = S // tq

    def body(q_ref, k_ref, v_ref, cum_ref, cumt_ref, o_ref, lse_ref):
        qi, hp = pl.program_id(1), pl.program_id(2)
        qm = _head_halves(_scaled(q_ref[...]))
        cumv = cum_ref[...]
        cq = [_pick_lane(cumv, 2 * hp + e) for e in range(2)]

        def tile(j, carry, masked):
            off = pl.multiple_of(j * tq, tq)
            kj = k_ref[pl.ds(off, tq), :]
            vj = v_ref[pl.ds(off, tq), :]
            ct = cumt_ref[j]
            new = []
            for e in range(2):
                m, l, acc = carry[e]
                s = _dot_nt(qm[e], kj) - _pick_row(ct, 2 * hp + e)
                if masked:
                    s = jnp.where(_causal(tq), s, NEG)
                m_new = jnp.maximum(m, jnp.max(s, axis=1, keepdims=True))
                p = jnp.exp(s - m_new)
                alpha = jnp.exp(m - m_new)
                l = alpha * l + jnp.sum(p, axis=1, keepdims=True)
                acc = alpha * acc + _dot(p.astype(BF16), vj)
                new.append((m_new, l, acc))
            return tuple(new)

        one = (jnp.full((tq, 1), NEG, F32), jnp.zeros((tq, 1), F32), jnp.zeros((tq, LANES), F32))
        carry = lax.fori_loop(0, qi, lambda j, c: tile(j, c, False), (one, one))
        (ma, la, acca), (mb, lb, accb) = tile(qi, carry, True)
        lo = lax.broadcasted_iota(jnp.int32, (tq, LANES), 1) < HEAD_DIM
        o_ref[...] = jnp.where(lo, acca / la, accb / lb).astype(BF16)

        @pl.when(hp == 0)
        def _():
            lse_ref[...] = jnp.zeros_like(lse_ref)

        lse_ref[...] += (_put_lane(ma + jnp.log(la) + cq[0], 2 * hp) + _put_lane(mb + jnp.log(lb) + cq[1], 2 * hp + 1))

    kv = lambda first: pl.BlockSpec((S, LANES), lambda b, i, hp: (b, first + hp))
    return _pallas(
        body, "attn_fwd", (B, nq, HEAD_PAIRS),
        [pl.BlockSpec((tq, LANES), lambda b, i, hp: (b * nq + i, hp)),
         kv(ATTN_W // LANES), kv(2 * ATTN_W // LANES),
         pl.BlockSpec((tq, LANES), lambda b, i, hp: (b * nq + i, 0)),
         pl.BlockSpec((None, nq, 8, tq), lambda b, i, hp: (b, 0, 0, 0))],
        [pl.BlockSpec((tq, LANES), lambda b, i, hp: (b * nq + i, hp)),
         pl.BlockSpec((tq, LANES), lambda b, i, hp: (b * nq + i, 0))],
        [jax.ShapeDtypeStruct((B * S, ATTN_W), BF16), jax.ShapeDtypeStruct((B * S, LANES), F32)],
        [], (proj, proj, proj, cum, cum_t), comm)


def _attn_bwd(proj, o, do, lse, cum, cum_t, B, S, tq, comm=None):
    nq = S // tq

    def body(q_ref, k_ref, v_ref, o_ref, do_ref, lse_ref, cum_ref, cumt_ref,
             dq_ref, dk_ref, dv_ref, dcq_ref, dck_ref, dq_scr):
        hp, kj = pl.program_id(1), pl.program_id(2)

        @pl.when(kj == 0)
        def _():
            dq_scr[...] = jnp.zeros_like(dq_scr)

        @pl.when((kj == 0) & (hp == 0))
        def _():
            dcq_ref[...] = jnp.zeros_like(dcq_ref)
            dck_ref[...] = jnp.zeros_like(dck_ref)

        kv = k_ref[...]
        vv = v_ref[...]
        km = _head_halves(kv)
        ct = cumt_ref[...]
        ck = [_pick_row(ct, 2 * hp + e) for e in range(2)]

        def tile(i, carry, masked):
            dk, dv, dcol = carry
            off = pl.multiple_of(i * tq, tq)
            qi = q_ref[pl.ds(off, tq), :]
            ov = o_ref[pl.ds(off, tq), :].astype(F32)
            qm = _head_halves(_scaled(qi))
            dom = _head_halves(do_ref[pl.ds(off, tq), :])
            cumv = cum_ref[pl.ds(off, tq), :]
            lsev = lse_ref[pl.ds(off, tq), :]
            dcq = jnp.zeros((tq, LANES), F32)
            dq = jnp.zeros((tq, LANES), F32)
            dcol_new = []
            for e in range(2):
                delta = jnp.sum(dom[e].astype(F32) * ov, axis=1, keepdims=True)
                row_term = _pick_lane(cumv, 2 * hp + e) - _pick_lane(lsev, 2 * hp + e)
                p = jnp.exp(_dot_nt(qm[e], kv) + row_term - ck[e])
                if masked:
                    p = jnp.where(_causal(tq), p, 0.0)
                dv = dv + _dot_tn(p.astype(BF16), dom[e])
                ds = p * (_dot_nt(dom[e], vv) - delta)
                dcol_new.append(dcol[e] + jnp.sum(ds, axis=0, keepdims=True))
                dcq = dcq + _put_lane(jnp.sum(ds, axis=1, keepdims=True), 2 * hp + e)
                dsb = ds.astype(BF16)
                dk = dk + _dot_tn(dsb, qm[e])
                dq = dq + _dot(dsb, km[e]) * ATTN_SCALE
            dq_scr[pl.ds(off, tq), :] += dq
            dcq_ref[pl.ds(off, tq), :] += dcq
            return dk, dv, tuple(dcol_new)

        zero_row = jnp.zeros((1, tq), F32)
        init = (jnp.zeros((tq, LANES), F32), jnp.zeros((tq, LANES), F32), (zero_row, zero_row))
        carry = tile(kj, init, True)
        dk, dv, dcol = lax.fori_loop(kj + 1, nq, lambda i, c: tile(i, c, False), carry)
        dk_ref[...] = dk.astype(BF16)
        dv_ref[...] = dv.astype(BF16)
        dck_ref[kj] += -(_put_row(dcol[0], 2 * hp) + _put_row(dcol[1], 2 * hp + 1))

        @pl.when(kj == nq - 1)
        def _():
            dq_ref[...] = dq_scr[...].astype(BF16)

    seq = lambda first: pl.BlockSpec((S, LANES), lambda b, hp, j: (b, first + hp))
    tile_in = lambda first: pl.BlockSpec((tq, LANES), lambda b, hp, j: (b * nq + j, first + hp))
    lanes0 = pl.BlockSpec((S, LANES), lambda b, hp, j: (b, 0))
    out = jax.ShapeDtypeStruct((B * S, ATTN_W), BF16)
    return _pallas(
        body, "attn_bwd", (B, HEAD_PAIRS, nq),
        [seq(0), tile_in(ATTN_W // LANES), tile_in(2 * ATTN_W // LANES), seq(0), seq(0), lanes0, lanes0,
         pl.BlockSpec((None, None, 8, tq), lambda b, hp, j: (b, j, 0, 0))],
        [seq(0), tile_in(0), tile_in(0), lanes0,
         pl.BlockSpec((None, nq, 8, tq), lambda b, hp, j: (b, 0, 0, 0))],
        [out, out, out, jax.ShapeDtypeStruct((B * S, LANES), F32), jax.ShapeDtypeStruct((B, nq, 8, tq), F32)],
        [pltpu.VMEM((S, LANES), F32)],
        (proj, proj, proj, o, do, lse, cum, cum_t), comm)


def _shift_down(u, n):
    row = lax.broadcasted_iota(jnp.int32, u.shape, 0)
    return jnp.where(row >= n, pltpu.roll(u, n, 0), 0.0)


def _shift_up(u, n):
    rows = u.shape[0]
    row = lax.broadcasted_iota(jnp.int32, u.shape, 0)
    return jnp.where(row < rows - n, pltpu.roll(u, rows - n, 0), 0.0)


def _conv_specs(S):
    cb = pl.BlockSpec((S, LANES), lambda g, b: (b, COL_CB // LANES + g))
    cc = pl.BlockSpec((S, LANES), lambda g, b: (b, COL_CC // LANES + g))
    cx = pl.BlockSpec((S, LANES), lambda g, b: (b, COL_CX // LANES + g))
    w = pl.BlockSpec((8, LANES), lambda g, b: (0, g))
    return cb, cc, cx, w


def _conv_fwd(proj, conv_w, B, S):
    def body(cb_ref, cc_ref, cx_ref, w_ref, y_ref):
        u = cc_ref[...].astype(F32) * cx_ref[...].astype(F32)
        w = w_ref[...]
        conv = w[0:1, :] * _shift_down(u, 2) + w[1:2, :] * _shift_down(u, 1) + w[2:3, :] * u
        y_ref[...] = (cb_ref[...].astype(F32) * conv).astype(BF16)

    cb, cc, cx, w = _conv_specs(S)
    return pl.pallas_call(
        body, name="conv_fwd", grid=(CONV_W // LANES, B),
        in_specs=[cb, cc, cx, w],
        out_specs=pl.BlockSpec((S, LANES), lambda g, b: (b, g)),
        out_shape=jax.ShapeDtypeStruct((B * S, CONV_W), BF16),
        compiler_params=_params(("arbitrary", "arbitrary")),
    )(proj, proj, proj, conv_w)


def _conv_bwd(dy, proj, conv_w, B, S):
    def body(dy_ref, cb_ref, cc_ref, cx_ref, w_ref, dcb_ref, dcc_ref, dcx_ref, dw_ref):
        @pl.when(pl.program_id(1) == 0)
        def _():
            dw_ref[...] = jnp.zeros_like(dw_ref)

        ccv = cc_ref[...].astype(F32)
        cxv = cx_ref[...].astype(F32)
        u = ccv * cxv
        u1 = _shift_down(u, 1)
        u2 = _shift_down(u, 2)
        w = w_ref[...]
        conv = w[0:1, :] * u2 + w[1:2, :] * u1 + w[2:3, :] * u
        dyv = dy_ref[...].astype(F32)
        dcb_ref[...] = (dyv * conv).astype(BF16)
        dconv = dyv * cb_ref[...].astype(F32)
        du = w[2:3, :] * dconv + w[1:2, :] * _shift_up(dconv, 1) + w[0:1, :] * _shift_up(dconv, 2)
        dcc_ref[...] = (du * cxv).astype(BF16)
        dcx_ref[...] = (du * ccv).astype(BF16)
        row = lax.broadcasted_iota(jnp.int32, (8, LANES), 0)
        dw = jnp.where(row == 0, jnp.sum(dconv * u2, axis=0, keepdims=True),
                       jnp.where(row == 1, jnp.sum(dconv * u1, axis=0, keepdims=True),
                                 jnp.where(row == 2, jnp.sum(dconv * u, axis=0, keepdims=True), 0.0)))
        dw_ref[...] += dw

    cb, cc, cx, w = _conv_specs(S)
    out = pl.BlockSpec((S, LANES), lambda g, b: (b, g))
    return pl.pallas_call(
        body, name="conv_bwd", grid=(CONV_W // LANES, B),
        in_specs=[out, cb, cc, cx, w],
        out_specs=[out, out, out, w],
        out_shape=[jax.ShapeDtypeStruct((B * S, CONV_W), BF16)] * 3 + [jax.ShapeDtypeStruct((8, CONV_W), F32)],
        compiler_params=_params(("arbitrary", "arbitrary")),
    )(dy, proj, proj, proj, conv_w)


def _gate_specs(tm, D):
    ga = pl.BlockSpec((tm, D), lambda i: (i, COL_GATES // D))
    gc = pl.BlockSpec((tm, D), lambda i: (i, COL_GATES // D + 1))
    return ga, gc


def _mix_out_fwd(x, o, yc, proj, woa, woc, wout, tm):
    T, D = x.shape

    def body(x_ref, o_ref, yc_ref, ga_ref, gc_ref, woa_ref, woc_ref, wout_ref, out_ref):
        ya = _dot(o_ref[...], woa_ref[...])
        yp = _dot(yc_ref[...], woc_ref[...])
        merged = _sigmoid(ga_ref[...].astype(F32)) * ya + _sigmoid(gc_ref[...].astype(F32)) * yp
        out_ref[...] = x_ref[...] + _dot(merged.astype(BF16), wout_ref[...])

    ga, gc = _gate_specs(tm, D)
    row = lambda w: pl.BlockSpec((tm, w), lambda i: (i, 0))
    whole = lambda a: pl.BlockSpec(a.shape, lambda i: (0, 0))
    return pl.pallas_call(
        body, name="mix_out_fwd", grid=(T // tm,),
        in_specs=[row(D), row(ATTN_W), row(CONV_W), ga, gc, whole(woa), whole(woc), whole(wout)],
        out_specs=row(D),
        out_shape=jax.ShapeDtypeStruct((T, D), F32),
        compiler_params=_params(("arbitrary",)),
    )(x, o, yc, proj, proj, woa, woc, wout)


def _mix_out_bwd(dx, o, yc, proj, woa, woc, wout, tm, comm=None):
    T, D = dx.shape
    nt = T // tm

    def body(dx_ref, o_ref, yc_ref, ga_ref, gc_ref, woa_ref, woc_ref, wout_ref,
             do_ref, dyc_ref, dg_ref, dwoa_ref, dwoc_ref, dwout_ref, acca, accc, acco):
        t = pl.program_id(0)

        @pl.when(t == 0)
        def _():
            acca[...] = jnp.zeros_like(acca)
            accc[...] = jnp.zeros_like(accc)
            acco[...] = jnp.zeros_like(acco)

        dxb = dx_ref[...].astype(BF16)
        ov, ycv = o_ref[...], yc_ref[...]
        ya = _dot(ov, woa_ref[...])
        yp = _dot(ycv, woc_ref[...])
        sa = _sigmoid(ga_ref[...].astype(F32))
        sc = _sigmoid(gc_ref[...].astype(F32))
        merged = (sa * ya + sc * yp).astype(BF16)
        dm = _dot_nt(dxb, wout_ref[...])
        dya = (dm * sa).astype(BF16)
        dyp = (dm * sc).astype(BF16)
        dg_ref[:, :D] = (dm * ya * sa * (1.0 - sa)).astype(BF16)
        dg_ref[:, D:] = (dm * yp * sc * (1.0 - sc)).astype(BF16)
        do_ref[...] = _dot_nt(dya, woa_ref[...]).astype(BF16)
        dyc_ref[...] = _dot_nt(dyp, woc_ref[...]).astype(BF16)
        acca[...] += _dot_tn(ov, dya)
        accc[...] += _dot_tn(ycv, dyp)
        acco[...] += _dot_tn(merged, dxb)

        @pl.when(t == nt - 1)
        def _():
            dwoa_ref[...] = acca[...].astype(BF16)
            dwoc_ref[...] = accc[...].astype(BF16)
            dwout_ref[...] = acco[...].astype(BF16)

    ga, gc = _gate_specs(tm, D)
    row = lambda w: pl.BlockSpec((tm, w), lambda i: (i, 0))
    whole = lambda a: pl.BlockSpec(a.shape, lambda i: (0, 0))
    return _pallas(
        body, "mix_out_bwd", (nt,),
        [row(D), row(ATTN_W), row(CONV_W), ga, gc, whole(woa), whole(woc), whole(wout)],
        [row(ATTN_W), row(CONV_W), row(2 * D), whole(woa), whole(woc), whole(wout)],
        [jax.ShapeDtypeStruct((T, ATTN_W), BF16), jax.ShapeDtypeStruct((T, CONV_W), BF16),
         jax.ShapeDtypeStruct((T, 2 * D), BF16),
         jax.ShapeDtypeStruct(woa.shape, BF16), jax.ShapeDtypeStruct(woc.shape, BF16),
         jax.ShapeDtypeStruct(wout.shape, BF16)],
        [pltpu.VMEM(woa.shape, F32), pltpu.VMEM(woc.shape, F32), pltpu.VMEM(wout.shape, F32)],
        (dx, o, yc, proj, proj, woa, woc, wout), comm)


def _proj_pieces(dq, dk, dv, dcb, dcc, dcx, dgates, dflog):
    D = dgates.shape[1] // 2
    return [(dq, ATTN_W, 0), (dk, ATTN_W, 0), (dv, ATTN_W, 0), (dcb, CONV_W, 0), (dcc, CONV_W, 0), (dcx, CONV_W, 0),
            (dgates, D, 0), (dgates, D, 1), (dflog, LANES, 0)]


def _mix_proj_bwd_dx(dres, x, g, pieces, wproj_t, wf_t, tm, comm=None):
    T, D = x.shape
    n = len(pieces)
    w_blocks = [(ATTN_W, 0), (ATTN_W, 1), (ATTN_W, 2), (CONV_W, 3), (CONV_W, 4), (CONV_W, 5),
                (D, COL_GATES // D), (D, COL_GATES // D + 1)]

    def body(*refs):
        dres_ref, x_ref, g_ref = refs[:3]
        p_refs, w_refs = refs[3:3 + n], refs[3 + n:3 + 2 * n]
        dx_ref, dg_ref = refs[3 + 2 * n:]

        @pl.when(pl.program_id(0) == 0)
        def _():
            dg_ref[...] = jnp.zeros_like(dg_ref)

        dh = _dot(p_refs[0][...].astype(BF16), w_refs[0][...])
        for p_ref, w_ref in zip(p_refs[1:], w_refs[1:]):
            dh = dh + _dot(p_ref[...].astype(BF16), w_ref[...])
        xhat, inv = _rms(x_ref[...])
        dx, dg = _rms_bwd(dh, xhat, inv, g_ref[...])
        dx_ref[...] = dres_ref[...] + dx
        dg_ref[...] += dg

    row = pl.BlockSpec((tm, D), lambda i: (i, 0))
    vec = pl.BlockSpec((1, D), lambda i: (0, 0))
    p_specs = [pl.BlockSpec((tm, w), lambda i, cb=cb: (i, cb)) for _, w, cb in pieces]
    w_specs = [pl.BlockSpec((r, D), lambda i, rb=rb: (rb, 0)) for r, rb in w_blocks]
    w_specs.append(pl.BlockSpec((LANES, D), lambda i: (0, 0)))
    return _pallas(
        body, "mix_proj_bwd_dx", (T // tm,),
        [row, row, vec] + p_specs + w_specs, [row, vec],
        [jax.ShapeDtypeStruct((T, D), F32), jax.ShapeDtypeStruct((1, D), F32)], [],
        (dres, x, g, *[p for p, _, _ in pieces], *([wproj_t] * len(w_blocks)), wf_t), comm)


def _matmuls_tn(name, pieces, b, tk):
    T, N = b.shape
    nt = T // tk
    n = len(pieces)

    def body(*refs):
        a_refs, b_ref, out_refs, accs = refs[:n], refs[n], refs[n + 1:2 * n + 1], refs[2 * n + 1:]
        t = pl.program_id(0)

        @pl.when(t == 0)
        def _():
            for acc in accs:
                acc[...] = jnp.zeros_like(acc)

        bv = b_ref[...]
        for a_ref, acc in zip(a_refs, accs):
            acc[...] += _dot_tn(a_ref[...].astype(BF16), bv)

        @pl.when(t == nt - 1)
        def _():
            for out_ref, acc in zip(out_refs, accs):
                out_ref[...] = acc[...].astype(BF16)

    return pl.pallas_call(
        body, name=name, grid=(nt,),
        in_specs=[pl.BlockSpec((tk, w), lambda t, cb=cb: (t, cb)) for _, w, cb in pieces]
        + [pl.BlockSpec((tk, N), lambda t: (t, 0))],
        out_specs=[pl.BlockSpec((w, N), lambda t: (0, 0)) for _, w, _ in pieces],
        out_shape=[jax.ShapeDtypeStruct((w, N), BF16) for _, w, _ in pieces],
        scratch_shapes=[pltpu.VMEM((w, N), F32) for _, w, _ in pieces],
        compiler_params=_params(("arbitrary",)),
    )(*[a for a, _, _ in pieces], b)


def _final_loss(x, target, g, tm):
    T, D = x.shape

    def body(x_ref, t_ref, g_ref, dx_ref, loss_ref, dg_ref):
        @pl.when(pl.program_id(0) == 0)
        def _():
            loss_ref[...] = jnp.zeros_like(loss_ref)
            dg_ref[...] = jnp.zeros_like(dg_ref)

        xhat, inv = _rms(x_ref[...])
        err = xhat * g_ref[...] - t_ref[...]
        loss_ref[...] += 0.5 * jnp.sum(jnp.sum(err * err, axis=1, keepdims=True), axis=0, keepdims=True) / D
        dx, dg = _rms_bwd(err * (1.0 / D), xhat, inv, g_ref[...])
        dx_ref[...] = dx
        dg_ref[...] += dg

    row = pl.BlockSpec((tm, D), lambda i: (i, 0))
    return pl.pallas_call(
        body, name="final_loss", grid=(T // tm,),
        in_specs=[row, row, pl.BlockSpec((1, D), lambda i: (0, 0))],
        out_specs=[row, pl.BlockSpec((1, LANES), lambda i: (0, 0)), pl.BlockSpec((1, D), lambda i: (0, 0))],
        out_shape=[jax.ShapeDtypeStruct((T, D), F32), jax.ShapeDtypeStruct((1, LANES), F32),
                   jax.ShapeDtypeStruct((1, D), F32)],
        compiler_params=_params(("arbitrary",)),
    )(x, target, g)


class _LocalPlan:
    def __init__(self, stacks, small):
        self.stacks, self.small, self.grads = stacks, small, {}

    def weights(self, group):
        return _LAYOUTS[group](self.stacks, self.small)

    def rider(self, kernel_name):
        return None

    def arrived(self, kernel_name, results):
        pass

    def reduce(self, group, grads):
        self.grads.update(grads)


def _local_step(x, target, plan, B, S):
    T, D = x.shape
    tm = min(512, T)
    tm_fwd = min(1024, T)
    tq = min(512, S)
    nq = S // tq
    ch = min(256, S)

    def riding(kernel_name, build):
        results, brought = build(plan.rider(kernel_name))
        plan.arrived(kernel_name, brought)
        return results

    w1 = plan.weights("ffn1_in")
    hg1, hu1 = riding("ffn1_up", lambda comm: _ffn_up(
        "ffn1_up", x, w1["ffn1_norm"], w1["ffn1_gate"], w1["ffn1_up"], tm_fwd, comm))
    w1 = plan.weights("ffn1")
    x1, = riding("ffn1_down", lambda comm: _ffn_down("ffn1_down", x, hg1, hu1, w1["ffn1_down"], tm_fwd, comm))
    wm = plan.weights("mix")
    h, proj, flog = _mix_proj_fwd(x1, wm["mix_norm"], wm["w_proj"], wm["w_f"], tm_fwd, 1280)
    cum = _fgate_fwd(flog, wm["b_forget"], B, S, ch)
    cum_t = jnp.transpose(cum[:, :N_HEADS].reshape(B, nq, tq, N_HEADS), (0, 1, 3, 2))
    o, lse = riding("attn_fwd", lambda comm: _attn_fwd(proj, cum, cum_t, B, S, tq, comm))
    yc = _conv_fwd(proj, wm["conv_w"], B, S)
    x2 = _mix_out_fwd(x1, o, yc, proj, wm["w_o_attn"], wm["w_o_conv"], wm["w_out"], tm)
    w2 = plan.weights("ffn2")
    x3, hg2, hu2 = _ffn_fwd("ffn2_fwd", x2, w2["ffn2_norm"], w2["ffn2_gate"], w2["ffn2_up"], w2["ffn2_down"], tm_fwd)[0]
    dx3, loss, d_final_norm = _final_loss(x3, target, w2["final_norm"], tm)

    g = {"final_norm": d_final_norm}
    dx2, dhg2, dhu2, g["ffn2_norm"] = _ffn_bwd_dx("ffn2_bwd_dx", dx3, x2, w2["ffn2_norm"], hg2, hu2,
                                                  w2["ffn2_gate"], w2["ffn2_up"], w2["ffn2_down"], tm)[0]
    plan.reduce("ffn2", dict(zip(("ffn2_gate", "ffn2_up", "ffn2_down"),
                                 _ffn_bwd_dw("ffn2_bwd_dw", dx3, x2, w2["ffn2_norm"], hg2, hu2, dhg2, dhu2, tm)[0])))
    do, dyc, dgates, dwoa, dwoc, dwout = riding("mix_out_bwd", lambda comm: _mix_out_bwd(
        dx2, o, yc, proj, wm["w_o_attn"], wm["w_o_conv"], wm["w_out"], tm, comm))
    plan.reduce("out", dict(w_o_attn=_shard_cols(dwoa), w_o_conv=_shard_cols(dwoc), w_out=dwout.reshape(N_CHIPS, -1, D)))
    dq, dk, dv, dcq, dck = riding("attn_bwd", lambda comm: _attn_bwd(proj, o, do, lse, cum, cum_t, B, S, tq, comm))
    dcum = dcq + jnp.pad(jnp.transpose(dck, (0, 1, 3, 2)).reshape(T, N_HEADS), ((0, 0), (0, LANES - N_HEADS)))
    dflog, g["b_forget"] = _fgate_bwd(dcum, flog, wm["b_forget"], B, S, ch)
    dcb, dcc, dcx, g["conv_w"] = _conv_bwd(dyc, proj, wm["conv_w"], B, S)
    pieces = _proj_pieces(dq, dk, dv, dcb, dcc, dcx, dgates, dflog)
    dwq, dwk, dwv, dwcb, dwcc, dwcx = _matmuls_tn("mix_dw_a", pieces[:6], h, tm)
    dwga, dwgc, dwf = _matmuls_tn("mix_dw_b", pieces[6:], h, tm)
    dwin_t = jnp.concatenate([dwq, dwk, dwv, dwf[:N_HEADS], dwcb, dwcc, dwcx, dwga, dwgc], axis=0)
    plan.reduce("w_in", {"w_in": dwin_t.reshape(N_CHIPS, -1, D)})
    dx1, g["mix_norm"] = riding("mix_proj_bwd_dx", lambda comm: _mix_proj_bwd_dx(
        dx2, x1, wm["mix_norm"], pieces, wm["w_proj"], wm["w_f"], min(256, T), comm))
    grad_x, dhg1, dhu1, g["ffn1_norm"] = _ffn_bwd_dx(
        "ffn1_bwd_dx", dx1, x, w1["ffn1_norm"], hg1, hu1, w1["ffn1_gate"], w1["ffn1_up"], w1["ffn1_down"], tm)[0]
    plan.reduce("ffn1", dict(zip(("ffn1_gate", "ffn1_up", "ffn1_down"), riding("ffn1_bwd_dw", lambda comm: _ffn_bwd_dw(
        "ffn1_bwd_dw", dx1, x, w1["ffn1_norm"], hg1, hu1, dhg1, dhu1, tm, comm)))))
    return loss, grad_x, g


TRANSPOSED = ("ffn1_gate", "ffn1_up", "ffn2_gate", "ffn2_up", "w_in")
NORMS = ("ffn1_norm", "mix_norm", "ffn2_norm", "final_norm")


def _unshard_cols(a):
    return jnp.transpose(a, (1, 0, 2)).reshape(a.shape[1], N_CHIPS * a.shape[2])


def _shard_cols(a):
    return jnp.transpose(a.reshape(a.shape[0], N_CHIPS, a.shape[1] // N_CHIPS), (1, 0, 2))


def _layout_ffn(which):
    def layout(st, small):
        w = {n: st[n] for n in (which + "_gate", which + "_up", which + "_down")}
        w[which + "_norm"] = small[which + "_norm"].reshape(1, -1)
        if which == "ffn2":
            w["final_norm"] = small["final_norm"].reshape(1, -1)
        return w
    return layout


def _layout_mix(st, small):
    win_t = st["w_in"].reshape(-1, st["w_in"].shape[2])
    return {
        "w_proj": jnp.concatenate([win_t[:N_FORGET_COL], win_t[N_FORGET_COL + N_HEADS:]], axis=0),
        "w_f": jnp.pad(win_t[N_FORGET_COL:N_FORGET_COL + N_HEADS], ((0, LANES - N_HEADS), (0, 0))),
        "w_o_attn": _unshard_cols(st["w_o_attn"]),
        "w_o_conv": _unshard_cols(st["w_o_conv"]),
        "w_out": st["w_out"].reshape(-1, st["w_out"].shape[2]),
        "conv_w": _unshard_cols(st["conv_w"]),
        "mix_norm": small["mix_norm"].reshape(1, -1),
        "b_forget": jnp.pad(small["b_forget"].reshape(1, -1), ((0, 0), (0, LANES - N_HEADS))),
    }


def _layout_ffn1_in(st, small):
    return {"ffn1_gate": st["ffn1_gate"], "ffn1_up": st["ffn1_up"], "ffn1_norm": small["ffn1_norm"].reshape(1, -1)}


_LAYOUTS = {"ffn1_in": _layout_ffn1_in, "ffn1": _layout_ffn("ffn1"), "mix": _layout_mix, "ffn2": _layout_ffn("ffn2")}


ANY = pl.BlockSpec(memory_space=pl.ANY)
BIG = ("ffn1_gate", "ffn1_up", "ffn1_down", "w_in", "w_o_attn", "w_o_conv", "w_out",
       "ffn2_gate", "ffn2_up", "ffn2_down")


def _place():
    x, y, c = lax.axis_index("x"), lax.axis_index("y"), lax.axis_index("c")
    others = [(1 - x, y), (x, 1 - y), (1 - x, 1 - y)]
    return x, y, c, others


def _col_halves(cols, c):
    hc = cols // 2
    return pl.ds(pl.multiple_of(c * hc, LANES), hc), pl.ds(pl.multiple_of((1 - c) * hc, LANES), hc)


def _gather_comm(shards, conv_shard=None):
    n = len(shards)
    inputs = list(shards) + ([] if conv_shard is None else [conv_shard])

    def copies(ins, outs, sems):
        send_sems, recv_sems, pass_send, pass_recv = sems[:4]
        x, y, c, others = _place()

        def chip_copy(a, j, chip):
            mine, _ = _col_halves(ins[a].shape[1], c)
            return pltpu.make_async_remote_copy(
                src_ref=ins[a].at[:, mine], dst_ref=outs[a].at[chip, :, mine],
                send_sem=send_sems.at[3 * a + j], recv_sem=recv_sems.at[3 * a + j],
                device_id=(*others[j], c), device_id_type=MESH)

        def pass_copy(a, j, chip, half):
            return pltpu.make_async_remote_copy(
                src_ref=outs[a].at[chip, :, half], dst_ref=outs[a].at[chip, :, half],
                send_sem=pass_send.at[3 * a + j], recv_sem=pass_recv.at[3 * a + j],
                device_id=(x, y, 1 - c), device_id_type=MESH)

        def conv_copy(j, chip):
            return pltpu.make_async_remote_copy(
                src_ref=ins[n], dst_ref=outs[n].at[chip],
                send_sem=sems[4].at[j], recv_sem=sems[5].at[j],
                device_id=(*others[j], c), device_id_type=MESH)

        me = 2 * x + y
        sends = [chip_copy(a, j, me) for a in range(n) for j in range(3)]
        if conv_shard is not None:
            sends += [conv_copy(j, me) for j in range(3)]
        return c, others, sends, chip_copy, pass_copy, conv_copy

    def start(ins, outs, sems):
        for cp in copies(ins, outs, sems)[2]:
            cp.start()

    def finish(ins, outs, sems):
        c, others, sends, chip_copy, pass_copy, conv_copy = copies(ins, outs, sems)
        passed = []
        for a in range(n):
            mine, _ = _col_halves(ins[a].shape[1], c)
            for j, (ox, oy) in enumerate(others):
                chip_copy(a, j, 2 * ox + oy).wait_recv()
                passed.append(pass_copy(a, j, 2 * ox + oy, mine))
                passed[-1].start()
        for a in range(n):
            _, theirs = _col_halves(ins[a].shape[1], c)
            for j, (ox, oy) in enumerate(others):
                pass_copy(a, j, 2 * ox + oy, theirs).wait_recv()
        if conv_shard is not None:
            for j, (ox, oy) in enumerate(others):
                conv_copy(j, 2 * ox + oy).wait_recv()
        for cp in sends + passed:
            cp.wait_send()

    scratch = [pltpu.SemaphoreType.DMA((3 * n,))] * 4
    if conv_shard is not None:
        scratch += [pltpu.SemaphoreType.DMA((3,))] * 2
    return _Comm(inputs, [jax.ShapeDtypeStruct((N_CHIPS,) + s.shape, s.dtype) for s in inputs], scratch, start, finish)


def _fill_own(stacks, shards):
    chip = 2 * lax.axis_index("x") + lax.axis_index("y")
    return [lax.dynamic_update_index_in_dim(st, s, chip, 0) for st, s in zip(stacks, shards)]


def _run_comm(name, comm):
    ci, co = len(comm.inputs), len(comm.out_shape)

    def body(*refs):
        comm.start(refs[:ci], refs[ci:ci + co], refs[ci + co:])
        comm.finish(refs[:ci], refs[ci:ci + co], refs[ci + co:])

    return pl.pallas_call(body, name=name, in_specs=[ANY] * ci, out_specs=[ANY] * co, out_shape=comm.out_shape,
                          scratch_shapes=comm.scratch)(*comm.inputs)


def _sibling_exchange_comm(grads):
    n = len(grads)

    def copies(ins, outs, sems):
        x, y, c, _ = _place()
        return [pltpu.make_async_remote_copy(
            src_ref=ins[a].at[:, :, _col_halves(ins[a].shape[2], c)[1]], dst_ref=outs[a],
            send_sem=sems[0].at[a], recv_sem=sems[1].at[a],
            device_id=(x, y, 1 - c), device_id_type=MESH) for a in range(n)]

    def start(ins, outs, sems):
        for cp in copies(ins, outs, sems):
            cp.start()

    def finish(ins, outs, sems):
        for cp in copies(ins, outs, sems):
            cp.wait()

    half = lambda s: jax.ShapeDtypeStruct((s.shape[0], s.shape[1], s.shape[2] // 2), s.dtype)
    return _Comm(grads, [half(s) for s in grads], [pltpu.SemaphoreType.DMA((n,))] * 2, start, finish)


def _merge_comms(comms):
    def split(refs, count):
        out, at = [], 0
        for cm in comms:
            out.append(refs[at:at + count(cm)])
            at += count(cm)
        return out

    def parts(ins, outs, sems):
        return zip(comms, split(ins, lambda cm: len(cm.inputs)), split(outs, lambda cm: len(cm.out_shape)),
                   split(sems, lambda cm: len(cm.scratch)))

    def start(ins, outs, sems):
        for cm, i, o, s in parts(ins, outs, sems):
            cm.start(i, o, s)

    def finish(ins, outs, sems):
        for cm, i, o, s in parts(ins, outs, sems):
            cm.finish(i, o, s)

    return _Comm(sum([cm.inputs for cm in comms], []), sum([cm.out_shape for cm in comms], []),
                 sum([cm.scratch for cm in comms], []), start, finish)


def _add_halves(name, grads, recvs, core):
    n = len(grads)

    def body(core_ref, *refs):
        for g_ref, r_ref, out_ref in zip(refs[:n], refs[n:2 * n], refs[2 * n:]):
            out_ref[...] = (g_ref[...].astype(F32) + r_ref[...].astype(F32)).astype(BF16)

    half = lambda g: pl.BlockSpec((None, g.shape[1], g.shape[2] // 2), lambda k, core_ref: (k, 0, 0))
    mine = lambda g: pl.BlockSpec((None, g.shape[1], g.shape[2] // 2), lambda k, core_ref: (k, 0, core_ref[0]))
    return pl.pallas_call(
        body, name=name,
        grid_spec=pltpu.PrefetchScalarGridSpec(
            num_scalar_prefetch=1, grid=(N_CHIPS,),
            in_specs=[mine(g) for g in grads] + [half(g) for g in grads],
            out_specs=[half(g) for g in grads]),
        out_shape=[jax.ShapeDtypeStruct(r.shape, BF16) for r in recvs],
        compiler_params=_params(("arbitrary",)),
    )(core, *grads, *recvs)


def _chip_exchange_comm(parts):
    n = len(parts)

    def copies(ins, outs, sems):
        x, y, c, others = _place()
        return [pltpu.make_async_remote_copy(
            src_ref=ins[a].at[2 * ox + oy], dst_ref=outs[a].at[j],
            send_sem=sems[0].at[3 * a + j], recv_sem=sems[1].at[3 * a + j],
            device_id=(ox, oy, c), device_id_type=MESH) for a in range(n) for j, (ox, oy) in enumerate(others)]

    def start(ins, outs, sems):
        for cp in copies(ins, outs, sems):
            cp.start()

    def finish(ins, outs, sems):
        for cp in copies(ins, outs, sems):
            cp.wait()

    return _Comm(parts, [jax.ShapeDtypeStruct((3,) + s.shape[1:], s.dtype) for s in parts],
                 [pltpu.SemaphoreType.DMA((3 * n,))] * 2, start, finish)


HBM = pl.BlockSpec(memory_space=pltpu.HBM)
SEM = pl.BlockSpec(memory_space=pltpu.SEMAPHORE)


def _split_exchange_copies(parts, lands, send_sems, recv_sems):
    x, y, c, others = _place()
    return [pltpu.make_async_remote_copy(
        src_ref=parts[a].at[2 * ox + oy], dst_ref=lands[a].at[j],
        send_sem=send_sems.at[3 * a + j], recv_sem=recv_sems.at[3 * a + j],
        device_id=(ox, oy, c), device_id_type=MESH) for a in range(len(parts)) for j, (ox, oy) in enumerate(others)]


def _exchange_start(name, parts):
    n = len(parts)

    def body(*refs):
        ins, lands = refs[:n], refs[n:2 * n]
        send_sems, recv_sems, token = refs[2 * n], refs[2 * n + 1], refs[-1]
        for cp in _split_exchange_copies(ins, lands, send_sems, recv_sems):
            cp.start()
        token[...] = jnp.zeros_like(token)

    land_shape = [(3,) + p.shape[1:] for p in parts]
    outs = pl.pallas_call(
        body, name=name,
        out_shape=[pltpu.SemaphoreType.DMA((3 * n,)), pltpu.SemaphoreType.DMA((3 * n,))]
        + [pltpu.HBM(p.shape, p.dtype) for p in parts] + [pltpu.HBM(s, p.dtype) for s, p in zip(land_shape, parts)]
        + [jax.ShapeDtypeStruct((8, LANES), F32)],
        in_specs=[HBM] * (2 * n), out_specs=[SEM, SEM] + [HBM] * (2 * n) + [pl.BlockSpec(memory_space=pltpu.VMEM)],
        input_output_aliases={i: 2 + i for i in range(2 * n)},
        compiler_params=pltpu.CompilerParams(has_side_effects=pltpu.SideEffectType.DATAFLOW_SIDE_EFFECTING),
    )(*[pltpu.with_memory_space_constraint(p, pltpu.HBM) for p in parts],
      *[pltpu.with_memory_space_constraint(lax.empty(s, p.dtype), pltpu.HBM) for s, p in zip(land_shape, parts)])
    return outs[0], outs[1], list(outs[2:2 + n]), list(outs[2 + n:2 + 2 * n]), outs[-1]


def _exchange_wait(name, send_sems, recv_sems, parts, lands, after):
    n = len(parts)

    def body(*refs):
        ins, zones = refs[:n], refs[n:2 * n]
        for cp in _split_exchange_copies(ins, zones, refs[2 * n], refs[2 * n + 1]):
            cp.wait_send()
            cp.wait_recv()

    outs = pl.pallas_call(
        body, name=name,
        out_shape=[pltpu.HBM(p.shape, p.dtype) for p in parts] + [pltpu.HBM(z.shape, z.dtype) for z in lands],
        in_specs=[HBM] * (2 * n) + [SEM, SEM] + [ANY] * len(after), out_specs=[HBM] * (2 * n),
        input_output_aliases={i: i for i in range(2 * n)},
        compiler_params=pltpu.CompilerParams(has_side_effects=pltpu.SideEffectType.DATAFLOW_SIDE_EFFECTING),
    )(*parts, *lands, send_sems, recv_sems, *after)
    return list(outs[:n]), list(outs[n:])


def _sum_chips(name, owns, recvs, chip, after):
    n = len(owns)
    hc = owns[0].shape[2]
    assert all(o.shape[2] == hc for o in owns)

    def body(chip_ref, *refs):
        for own_ref, recv_ref, out_ref in zip(refs[:n], refs[n:2 * n], refs[2 * n + 1:]):
            acc = own_ref[...].astype(F32)
            for j in range(3):
                acc = acc + recv_ref[j].astype(F32)
            out_ref[...] = acc

    return pl.pallas_call(
        body, name=name,
        grid_spec=pltpu.PrefetchScalarGridSpec(
            num_scalar_prefetch=1, grid=(hc // LANES,),
            in_specs=[pl.BlockSpec((None, o.shape[1], LANES), lambda i, chip_ref: (chip_ref[0], 0, i)) for o in owns]
            + [pl.BlockSpec((3, o.shape[1], LANES), lambda i, chip_ref: (0, 0, i)) for o in owns]
            + [pl.BlockSpec((8, LANES), lambda i, chip_ref: (0, 0))],
            out_specs=[pl.BlockSpec((o.shape[1], LANES), lambda i, chip_ref: (0, i)) for o in owns]),
        out_shape=[jax.ShapeDtypeStruct((o.shape[1], hc), F32) for o in owns],
        compiler_params=_params(("arbitrary",)),
    )(chip, *owns, *recvs, after)


def _share_halves(name, halves):
    n = len(halves)

    def body(*refs):
        srcs, dsts = refs[:n], refs[n:2 * n]
        send_sems, recv_sems = refs[2 * n:]
        x, y, c, _ = _place()
        copies = [pltpu.make_async_remote_copy(
            src_ref=srcs[a], dst_ref=dsts[a], send_sem=send_sems.at[a], recv_sem=recv_sems.at[a],
            device_id=(x, y, 1 - c), device_id_type=MESH) for a in range(n)]
        for cp in copies:
            cp.start()
        for cp in copies:
            cp.wait()

    return pl.pallas_call(
        body, name=name,
        in_specs=[ANY] * n, out_specs=[ANY] * n,
        out_shape=[jax.ShapeDtypeStruct(s.shape, s.dtype) for s in halves],
        scratch_shapes=[pltpu.SemaphoreType.DMA((n,)), pltpu.SemaphoreType.DMA((n,))],
    )(*halves)


def _allreduce_small(part):
    rows = part.shape[0]

    def body(in_ref, out_ref, land, send_sems, recv_sems):
        x, y, c, _ = _place()
        me = 4 * x + 2 * y + c
        land[me] = in_ref[...]
        copies = []
        for d in range(1, N_DEV):
            peer = (1 - x if d & 4 else x, 1 - y if d & 2 else y, 1 - c if d & 1 else c)
            copies.append(pltpu.make_async_remote_copy(
                src_ref=in_ref, dst_ref=land.at[me],
                send_sem=send_sems.at[d - 1], recv_sem=recv_sems.at[d - 1],
                device_id=peer, device_id_type=MESH))
        for cp in copies:
            cp.start()
        for d in range(1, N_DEV):
            px, py, pc = (1 - x if d & 4 else x, 1 - y if d & 2 else y, 1 - c if d & 1 else c)
            pltpu.make_async_remote_copy(
                src_ref=in_ref, dst_ref=land.at[4 * px + 2 * py + pc],
                send_sem=send_sems.at[d - 1], recv_sem=recv_sems.at[d - 1],
                device_id=(px, py, pc), device_id_type=MESH).wait_recv()
        for cp in copies:
            cp.wait_send()
        acc = land[0]
        for k in range(1, N_DEV):
            acc = acc + land[k]
        out_ref[...] = acc

    vmem = pl.BlockSpec(memory_space=pltpu.VMEM)
    return pl.pallas_call(
        body, name="allreduce_small",
        in_specs=[vmem], out_specs=vmem,
        out_shape=jax.ShapeDtypeStruct(part.shape, F32),
        scratch_shapes=[pltpu.VMEM((N_DEV, rows, LANES), F32),
                        pltpu.SemaphoreType.DMA((N_DEV - 1,)), pltpu.SemaphoreType.DMA((N_DEV - 1,))],
    )(part)


def _adam_update(w, g, m, v):
    nm = ADAM_B1 * m + (1.0 - ADAM_B1) * g
    nv = ADAM_B2 * v + (1.0 - ADAM_B2) * (g * g)
    m_hat = nm * (1.0 / (1.0 - ADAM_B1 ** ADAM_STEP))
    v_hat = nv * (1.0 / (1.0 - ADAM_B2 ** ADAM_STEP))
    return -ADAM_LR * (m_hat / (jnp.sqrt(v_hat) + ADAM_EPS) + ADAM_WD * w), nm, nv


def _adamw(name, w, g, m, v):
    def body(w_ref, g_ref, m_ref, v_ref, d_ref, nm_ref, nv_ref):
        d_ref[...], nm_ref[...], nv_ref[...] = _adam_update(w_ref[...], g_ref[...], m_ref[...], v_ref[...])

    spec = pl.BlockSpec(w.shape, lambda i: (0, 0))
    out = jax.ShapeDtypeStruct(w.shape, F32)
    return pl.pallas_call(
        body, name=name, grid=(1,),
        in_specs=[spec] * 4, out_specs=[spec] * 3, out_shape=[out] * 3,
        compiler_params=_params(("arbitrary",)),
    )(w, g, m, v)


def _adamw_halves(name, ws, mines, theirs, ms, vs, core):
    n = len(ws)
    cols = ws[0].shape[1]
    assert all(w.shape[1] == cols for w in ws)
    hc = cols // 2
    tc = LANES if n > 1 else min(256, hc)
    nt = hc // tc

    def body(core_ref, *refs):
        ins, outs = refs[:5 * n], refs[5 * n:]
        for a in range(n):
            w_ref, mine_ref, theirs_ref, m_ref, v_ref = [ins[j * n + a] for j in range(5)]
            g_ref, d_ref, nm_ref, nv_ref = outs[4 * a:4 * a + 4]
            gv = jnp.where(pl.program_id(0) == core_ref[0], mine_ref[...], theirs_ref[...])
            g_ref[...] = gv
            d_ref[...], nm_ref[...], nv_ref[...] = _adam_update(w_ref[...], gv, m_ref[...], v_ref[...])

    whole = lambda w: pl.BlockSpec((w.shape[0], tc), lambda h, i, core_ref: (0, h * nt + i))
    mine_spec = lambda w: pl.BlockSpec((w.shape[0], tc), lambda h, i, core_ref: (0, jnp.where(h == core_ref[0], i, 0)))
    theirs_spec = lambda w: pl.BlockSpec((w.shape[0], tc), lambda h, i, core_ref: (0, jnp.where(h == core_ref[0], 0, i)))
    outs = pl.pallas_call(
        body, name=name,
        grid_spec=pltpu.PrefetchScalarGridSpec(
            num_scalar_prefetch=1, grid=(2, nt),
            in_specs=[whole(w) for w in ws] + [mine_spec(w) for w in ws] + [theirs_spec(w) for w in ws]
            + [whole(w) for w in ws] * 2,
            out_specs=[whole(w) for w in ws for _ in range(4)]),
        out_shape=[jax.ShapeDtypeStruct(w.shape, F32) for w in ws for _ in range(4)],
        compiler_params=_params(("arbitrary", "arbitrary")),
    )(core, *ws, *mines, *theirs, *ms, *vs)
    return [outs[4 * a:4 * a + 4] for a in range(n)]


WEIGHTS = ("ffn1_norm", "ffn1_gate", "ffn1_up", "ffn1_down", "mix_norm", "w_in", "b_forget", "conv_w",
           "w_o_attn", "w_o_conv", "w_out", "ffn2_norm", "ffn2_gate", "ffn2_up", "ffn2_down", "final_norm")
VEC_ROWS = 8


def _pack_small(t, conv_rows):
    conv = t["conv_w"]
    parts = [t[n].reshape(VEC_ROWS, LANES) for n in NORMS]
    parts.append(jnp.pad(conv, ((0, conv_rows - conv.shape[0]), (0, 0))))
    parts.append(jnp.pad(t["b_forget"].reshape(1, N_HEADS), ((0, 7), (0, LANES - N_HEADS))))
    return jnp.concatenate(parts, axis=0)


def _unpack_small(p, conv_rows):
    out = {n: p[VEC_ROWS * i:VEC_ROWS * (i + 1)].reshape(-1) for i, n in enumerate(NORMS)}
    base = VEC_ROWS * len(NORMS)
    out["conv_w"] = p[base:base + 3]
    out["b_forget"] = p[base + conv_rows, :N_HEADS]
    return out


def _travel(name, a):
    return a.T if name in TRANSPOSED else a


GATHER_FIRST = ("ffn1_gate", "ffn1_up")
GATHER_RIDES = {"ffn1_up": ("ffn1_down", "w_in"), "ffn1_down": ("w_o_attn", "w_o_conv", "w_out"),
                "attn_fwd": ("ffn2_gate", "ffn2_up", "ffn2_down")}
SIBLING_RIDES = {"ffn2": "mix_out_bwd", "out": None, "w_in": "mix_proj_bwd_dx", "ffn1": None}
CHIP_RIDES = {"ffn2": "attn_bwd", "out": "attn_bwd", "w_in": "ffn1_bwd_dw", "ffn1": None}


class _MeshPlan(_LocalPlan):
    def __init__(self, wts, core):
        self.small, self.core = wts, core
        self.shards = {n: wts[n].astype(BF16) for n in BIG}
        self.chip_part, self.from_chips, self.rides = {}, {}, {}
        conv_shard = jnp.pad(wts["conv_w"], ((0, 8 - wts["conv_w"].shape[0]), (0, 0)))
        own = [self.shards[n] for n in GATHER_FIRST] + [conv_shard]
        got = _run_comm("gather_first", _gather_comm(own[:-1], conv_shard))
        self.stacks = dict(zip(GATHER_FIRST + ("conv_w",), _fill_own(got, own)))
        for kernel_name, names in GATHER_RIDES.items():
            mine = [self.shards[n] for n in names]
            self._ride(kernel_name, _gather_comm(mine),
                       lambda got, names=names, mine=mine: self.stacks.update(zip(names, _fill_own(got, mine))))

    def _ride(self, kernel_name, comm, then):
        self.rides.setdefault(kernel_name, []).append((comm, then))

    def rider(self, kernel_name):
        comms = [comm for comm, _ in self.rides.get(kernel_name, [])]
        return _merge_comms(comms) if comms else None

    def arrived(self, kernel_name, results):
        for comm, then in self.rides.pop(kernel_name, []):
            then(results[:len(comm.out_shape)])
            results = results[len(comm.out_shape):]

    def reduce(self, group, grads):
        names = tuple(grads)
        mine = [grads[n] for n in names]

        def with_sibling(from_sibling):
            parts = _add_halves("add_halves_" + group, mine, list(from_sibling), self.core)
            self.chip_part.update(zip(names, parts))
            if CHIP_RIDES[group] is None:
                self.last = (names, _exchange_start("exchange_start_" + group, parts))
            else:
                self._ride(CHIP_RIDES[group], _chip_exchange_comm(parts),
                           lambda got: self.from_chips.update(zip(names, got)))

        if SIBLING_RIDES[group] is None:
            with_sibling(_run_comm("sibling_exchange_" + group, _sibling_exchange_comm(mine)))
        else:
            self._ride(SIBLING_RIDES[group], _sibling_exchange_comm(mine), with_sibling)


def kernel(x, ffn1_norm, ffn1_gate, ffn1_up, ffn1_down, mix_norm, w_in, b_forget, conv_w, w_o_attn, w_o_conv, w_out, ffn2_norm, ffn2_gate, ffn2_up, ffn2_down, final_norm, loss_target, m_ffn1_norm, m_ffn1_gate, m_ffn1_up, m_ffn1_down, m_mix_norm, m_w_in, m_b_forget, m_conv_w, m_w_o_attn, m_w_o_conv, m_w_out, m_ffn2_norm, m_ffn2_gate, m_ffn2_up, m_ffn2_down, m_final_norm, v_ffn1_norm, v_ffn1_gate, v_ffn1_up, v_ffn1_down, v_mix_norm, v_w_in, v_b_forget, v_conv_w, v_w_o_attn, v_w_o_conv, v_w_out, v_ffn2_norm, v_ffn2_gate, v_ffn2_up, v_ffn2_down, v_final_norm):
    given = dict(locals())
    wts = {n: _travel(n, given[n]) for n in WEIGHTS}
    mom = {n: _travel(n, given["m_" + n]) for n in WEIGHTS}
    var = {n: _travel(n, given["v_" + n]) for n in WEIGHTS}
    B, S, D = x.shape
    chip = 2 * lax.axis_index("x") + lax.axis_index("y")
    chip1 = chip.astype(jnp.int32).reshape(1)
    core = lax.axis_index("c").astype(jnp.int32).reshape(1)

    plan = _MeshPlan(wts, core)
    loss, grad_x, gs = _local_step(x.reshape(B * S, D), loss_target.reshape(B * S, D), plan, B, S)

    last_names, (send_sems, recv_sems, parts_thru, lands, token) = plan.last
    delta, new_m, new_v, grads = {}, {}, {}, {}

    def finish(tag, names):
        by_cols = {}
        for n in names:
            by_cols.setdefault(wts[n].shape[1], []).append(n)
        mine = {}
        for cols, ns in by_cols.items():
            mine.update(zip(ns, _sum_chips("sum_chips_%s_%d" % (tag, cols), [plan.chip_part[n] for n in ns],
                                           [plan.from_chips[n] for n in ns], chip1, token)))
        theirs = dict(zip(names, _share_halves("share_halves_" + tag, [mine[n] for n in names])))
        raw = []
        for cols, ns in by_cols.items():
            outs = _adamw_halves("adamw_%s_%d" % (tag, cols), [wts[n] for n in ns], [mine[n] for n in ns],
                                 [theirs[n] for n in ns], [mom[n] for n in ns], [var[n] for n in ns], core)
            for n, per in zip(ns, outs):
                raw.append(per[-1])
                grads[n], delta[n], new_m[n], new_v[n] = [_travel(n, o) for o in per]
        return raw

    conv_all = _shard_cols(gs["conv_w"]).reshape(N_CHIPS * 8, LANES)
    small_part = _pack_small({**{n: gs[n] for n in NORMS}, "conv_w": conv_all, "b_forget": gs["b_forget"][0, :N_HEADS]},
                             N_CHIPS * 8)
    base = VEC_ROWS * len(NORMS)
    loss_row = small_part.shape[0]
    small_part = jnp.concatenate([small_part, jnp.broadcast_to(loss, (8, LANES))], axis=0) + token[0, 0]
    small_sum = _allreduce_small(small_part)
    small_grads = _unpack_small(small_sum, N_CHIPS * 8)
    small_grads["conv_w"] = lax.dynamic_slice_in_dim(small_sum[base:base + N_CHIPS * 8], chip * 8, 8, axis=0)[:3]
    packs = [_pack_small(t, 8) for t in (wts, small_grads, mom, var)]
    small_out = _adamw("adamw_small", *packs)

    done = finish("early", [n for n in BIG if n not in last_names])
    parts_back, got = _exchange_wait("exchange_wait", send_sems, recv_sems, parts_thru, lands, done + list(small_out))
    plan.chip_part.update(zip(last_names, parts_back))
    plan.from_chips.update(zip(last_names, got))
    finish("last", last_names)
    grads.update(small_grads)
    for out, p in zip((delta, new_m, new_v), small_out):
        out.update(_unpack_small(p, 8))

    return (small_sum[loss_row, 0], grad_x.reshape(B, S, D), *[grads[n] for n in WEIGHTS], *[delta[n] for n in WEIGHTS],
            *[new_m[n] for n in WEIGHTS], *[new_v[n] for n in WEIGHTS])
```

```python
import functools
import math

import jax
import jax.numpy as jnp
from jax import lax
from jax.experimental import pallas as pl
from jax.experimental.pallas import tpu as pltpu

F32 = jnp.float32
BF16 = jnp.bfloat16
MESH = pl.DeviceIdType.MESH

N_CHIPS = 4
N_DEV = 8
N_HEADS = 8
HEAD_DIM = 64
HEAD_PAIRS = N_HEADS // 2
ATTN_W = N_HEADS * HEAD_DIM
CONV_W = 512
RMS_EPS = 1e-6
FFN_RES = 0.5
LANES = 128
VMEM_LIMIT = 56 * 1024 * 1024
ROW_BLOCK = 256

ADAM_LR = 0.001
ADAM_B1 = 0.9
ADAM_B2 = 0.999
ADAM_EPS = 1e-08
ADAM_WD = 0.01
ADAM_STEP = 10

PROJ_W = 3 * ATTN_W + 3 * CONV_W + 2 * 1024
COL_CB, COL_CC, COL_CX = 3 * ATTN_W, 3 * ATTN_W + CONV_W, 3 * ATTN_W + 2 * CONV_W
COL_GATES = 3 * ATTN_W + 3 * CONV_W
N_FORGET_COL = 3 * ATTN_W


def _params(sem=None, vmem=VMEM_LIMIT):
    return pltpu.CompilerParams(dimension_semantics=sem, vmem_limit_bytes=vmem)


def _dot(a, b):
    return lax.dot_general(a, b, (((1,), (0,)), ((), ())), preferred_element_type=F32)


def _dot_nt(a, b):
    return lax.dot_general(a, b, (((1,), (1,)), ((), ())), preferred_element_type=F32)


def _dot_tn(a, b):
    return lax.dot_general(a, b, (((0,), (0,)), ((), ())), preferred_element_type=F32)


def _sigmoid(x):
    return 1.0 / (1.0 + jnp.exp(-x))


def _rms(xv):
    inv = lax.rsqrt(jnp.mean(xv * xv, axis=-1, keepdims=True) + RMS_EPS)
    return xv * inv, inv


class _Comm:
    def __init__(self, inputs, out_shape, scratch, start, finish):
        self.inputs, self.out_shape, self.scratch = list(inputs), list(out_shape), list(scratch)
        self.start, self.finish = start, finish


def _pallas(body, name, grid, in_specs, out_specs, out_shape, scratch, args, comm=None):
    sem = ("arbitrary",) * len(grid)
    if comm is None:
        outs = pl.pallas_call(body, name=name, grid=grid, in_specs=in_specs, out_specs=out_specs,
                              out_shape=out_shape, scratch_shapes=scratch, compiler_params=_params(sem))(*args)
        return list(outs), []
    n_in, n_out, n_scr = len(in_specs), len(out_specs), len(scratch)
    ci, co = len(comm.inputs), len(comm.out_shape)

    def riding(*refs):
        ins, refs = refs[:n_in], refs[n_in:]
        cins, refs = refs[:ci], refs[ci:]
        outs, refs = refs[:n_out], refs[n_out:]
        couts, refs = refs[:co], refs[co:]
        scr, sems = refs[:n_scr], refs[n_scr:]
        ids = [pl.program_id(d) for d in range(len(grid))]
        first = functools.reduce(lambda a, b: a & b, [i == 0 for i in ids])
        last = functools.reduce(lambda a, b: a & b, [i == g - 1 for i, g in zip(ids, grid)])

        @pl.when(first)
        def _():
            comm.start(cins, couts, sems)

        body(*ins, *outs, *scr)

        @pl.when(last)
        def _():
            comm.finish(cins, couts, sems)

    any_spec = pl.BlockSpec(memory_space=pl.ANY)
    outs = pl.pallas_call(
        riding, name=name, grid=grid,
        in_specs=list(in_specs) + [any_spec] * ci, out_specs=list(out_specs) + [any_spec] * co,
        out_shape=list(out_shape) + comm.out_shape, scratch_shapes=list(scratch) + comm.scratch,
        compiler_params=_params(sem))(*args, *comm.inputs)
    return list(outs[:n_out]), list(outs[n_out:])


def _rms_bwd(dn, xhat, inv, g):
    dxhat = dn * g
    dx = inv * (dxhat - xhat * jnp.mean(dxhat * xhat, axis=-1, keepdims=True))
    return dx, jnp.sum(dn * xhat, axis=0, keepdims=True)


def _ffn_fwd(name, x, g, wgt, wut, wd, tm, comm=None):
    T, D = x.shape
    K, Fs, _ = wgt.shape

    def body(x_ref, g_ref, wg_ref, wu_ref, wd_ref, out_ref, hg_ref, hu_ref, n_scr, acc_scr):
        k = pl.program_id(1)

        @pl.when(k == 0)
        def _():
            xhat, _ = _rms(x_ref[...])
            n_scr[...] = (xhat * g_ref[...]).astype(BF16)
            acc_scr[...] = jnp.zeros_like(acc_scr)

        n = n_scr[...]
        hg = _dot_nt(n, wg_ref[...])
        hu = _dot_nt(n, wu_ref[...])
        hg_ref[...] = hg.astype(BF16)
        hu_ref[...] = hu.astype(BF16)
        act = (hg * _sigmoid(hg) * hu).astype(BF16)
        acc_scr[...] += _dot(act, wd_ref[...])

        @pl.when(k == K - 1)
        def _():
            out_ref[...] = x_ref[...] + FFN_RES * acc_scr[...]

    w_spec = pl.BlockSpec((None, Fs, D), lambda i, k: (k, 0, 0))
    act_spec = pl.BlockSpec((None, tm, Fs), lambda i, k: (k, i, 0))
    return _pallas(
        body, name, (T // tm, K),
        [pl.BlockSpec((tm, D), lambda i, k: (i, 0)), pl.BlockSpec((1, D), lambda i, k: (0, 0)),
         w_spec, w_spec, w_spec],
        [pl.BlockSpec((tm, D), lambda i, k: (i, 0)), act_spec, act_spec],
        [jax.ShapeDtypeStruct((T, D), F32), jax.ShapeDtypeStruct((K, T, Fs), BF16),
         jax.ShapeDtypeStruct((K, T, Fs), BF16)],
        [pltpu.VMEM((tm, D), BF16), pltpu.VMEM((tm, D), F32)],
        (x, g, wgt, wut, wd), comm)


def _ffn_up(name, x, g, wgt, wut, tm, comm=None):
    T, D = x.shape
    K, Fs, _ = wgt.shape

    def body(x_ref, g_ref, wg_ref, wu_ref, hg_ref, hu_ref, n_scr):
        @pl.when(pl.program_id(1) == 0)
        def _():
            xhat, _ = _rms(x_ref[...])
            n_scr[...] = (xhat * g_ref[...]).astype(BF16)

        n = n_scr[...]
        hg_ref[...] = _dot_nt(n, wg_ref[...]).astype(BF16)
        hu_ref[...] = _dot_nt(n, wu_ref[...]).astype(BF16)

    w_spec = pl.BlockSpec((None, Fs, D), lambda i, k: (k, 0, 0))
    act_spec = pl.BlockSpec((None, tm, Fs), lambda i, k: (k, i, 0))
    return _pallas(
        body, name, (T // tm, K),
        [pl.BlockSpec((tm, D), lambda i, k: (i, 0)), pl.BlockSpec((1, D), lambda i, k: (0, 0)), w_spec, w_spec],
        [act_spec, act_spec],
        [jax.ShapeDtypeStruct((K, T, Fs), BF16), jax.ShapeDtypeStruct((K, T, Fs), BF16)],
        [pltpu.VMEM((tm, D), BF16)],
        (x, g, wgt, wut), comm)


def _ffn_down(name, x, hg, hu, wd, tm, comm=None):
    T, D = x.shape
    K, Fs, _ = wd.shape

    def body(x_ref, hg_ref, hu_ref, wd_ref, out_ref, acc_scr):
        k = pl.program_id(1)

        @pl.when(k == 0)
        def _():
            acc_scr[...] = jnp.zeros_like(acc_scr)

        hgv = hg_ref[...].astype(F32)
        act = (hgv * _sigmoid(hgv) * hu_ref[...].astype(F32)).astype(BF16)
        acc_scr[...] += _dot(act, wd_ref[...])

        @pl.when(k == K - 1)
        def _():
            out_ref[...] = x_ref[...] + FFN_RES * acc_scr[...]

    act_spec = pl.BlockSpec((None, tm, Fs), lambda i, k: (k, i, 0))
    row = pl.BlockSpec((tm, D), lambda i, k: (i, 0))
    return _pallas(
        body, name, (T // tm, K),
        [row, act_spec, act_spec, pl.BlockSpec((None, Fs, D), lambda i, k: (k, 0, 0))],
        [row], [jax.ShapeDtypeStruct((T, D), F32)], [pltpu.VMEM((tm, D), F32)],
        (x, hg, hu, wd), comm)


def _ffn_bwd_dx(name, dout, x, g, hg, hu, wgt, wut, wd, tm, comm=None):
    T, D = x.shape
    K, Fs, _ = wgt.shape

    def body(dout_ref, x_ref, g_ref, hg_ref, hu_ref, wg_ref, wu_ref, wd_ref,
             dx_ref, dhg_ref, dhu_ref, dg_ref, df_scr, dn_scr):
        i, k = pl.program_id(0), pl.program_id(1)

        @pl.when(k == 0)
        def _():
            df_scr[...] = (FFN_RES * dout_ref[...]).astype(BF16)
            dn_scr[...] = jnp.zeros_like(dn_scr)

        @pl.when((k == 0) & (i == 0))
        def _():
            dg_ref[...] = jnp.zeros_like(dg_ref)

        for r0 in range(0, tm, ROW_BLOCK):
            rows = slice(r0, r0 + ROW_BLOCK)
            dact = _dot_nt(df_scr[rows, :], wd_ref[...])
            hgv = hg_ref[rows, :].astype(F32)
            huv = hu_ref[rows, :].astype(F32)
            s = _sigmoid(hgv)
            dhu = (dact * (hgv * s)).astype(BF16)
            dhg = (dact * huv * (s * (1.0 + hgv * (1.0 - s)))).astype(BF16)
            dhg_ref[rows, :] = dhg
            dhu_ref[rows, :] = dhu
            dn_scr[rows, :] += _dot(dhg, wg_ref[...]) + _dot(dhu, wu_ref[...])

        @pl.when(k == K - 1)
        def _():
            xhat, inv = _rms(x_ref[...])
            dx, dg = _rms_bwd(dn_scr[...], xhat, inv, g_ref[...])
            dx_ref[...] = dout_ref[...] + dx
            dg_ref[...] += dg

    w_spec = pl.BlockSpec((None, Fs, D), lambda i, k: (k, 0, 0))
    act_spec = pl.BlockSpec((None, tm, Fs), lambda i, k: (k, i, 0))
    row = pl.BlockSpec((tm, D), lambda i, k: (i, 0))
    vec = pl.BlockSpec((1, D), lambda i, k: (0, 0))
    return _pallas(
        body, name, (T // tm, K),
        [row, row, vec, act_spec, act_spec, w_spec, w_spec, w_spec],
        [row, act_spec, act_spec, vec],
        [jax.ShapeDtypeStruct((T, D), F32), jax.ShapeDtypeStruct((K, T, Fs), BF16),
         jax.ShapeDtypeStruct((K, T, Fs), BF16), jax.ShapeDtypeStruct((1, D), F32)],
        [pltpu.VMEM((tm, D), BF16), pltpu.VMEM((tm, D), F32)],
        (dout, x, g, hg, hu, wgt, wut, wd), comm)


def _ffn_bwd_dw(name, dout, x, g, hg, hu, dhg, dhu, tk, comm=None):
    T, D = x.shape
    K, _, Fs = hg.shape
    nt = T // tk

    def body(dout_ref, x_ref, g_ref, hg_ref, hu_ref, dhg_ref, dhu_ref,
             dwg_ref, dwu_ref, dwd_ref, accg, accu, accd):
        t = pl.program_id(1)

        @pl.when(t == 0)
        def _():
            accg[...] = jnp.zeros_like(accg)
            accu[...] = jnp.zeros_like(accu)
            accd[...] = jnp.zeros_like(accd)

        xhat, _ = _rms(x_ref[...])
        n = (xhat * g_ref[...]).astype(BF16)
        df = (FFN_RES * dout_ref[...]).astype(BF16)
        hgv = hg_ref[...].astype(F32)
        act = (hgv * _sigmoid(hgv) * hu_ref[...].astype(F32)).astype(BF16)
        accg[...] += _dot_tn(dhg_ref[...], n)
        accu[...] += _dot_tn(dhu_ref[...], n)
        accd[...] += _dot_tn(act, df)

        @pl.when(t == nt - 1)
        def _():
            dwg_ref[...] = accg[...].astype(BF16)
            dwu_ref[...] = accu[...].astype(BF16)
            dwd_ref[...] = accd[...].astype(BF16)

    act_spec = pl.BlockSpec((None, tk, Fs), lambda k, t: (k, t, 0))
    w_spec = pl.BlockSpec((None, Fs, D), lambda k, t: (k, 0, 0))
    return _pallas(
        body, name, (K, nt),
        [pl.BlockSpec((tk, D), lambda k, t: (t, 0)), pl.BlockSpec((tk, D), lambda k, t: (t, 0)),
         pl.BlockSpec((1, D), lambda k, t: (0, 0)), act_spec, act_spec, act_spec, act_spec],
        [w_spec, w_spec, w_spec],
        [jax.ShapeDtypeStruct((K, Fs, D), BF16)] * 3,
        [pltpu.VMEM((Fs, D), F32)] * 3,
        (dout, x, g, hg, hu, dhg, dhu), comm)


def _mix_proj_fwd(x, g, wproj_t, wf_t, tm, tn):
    T, D = x.shape
    N = wproj_t.shape[0]

    def body(x_ref, g_ref, w_ref, wf_ref, h_ref, proj_ref, flog_ref, h_scr):
        @pl.when(pl.program_id(1) == 0)
        def _():
            xhat, _ = _rms(x_ref[...])
            h = (xhat * g_ref[...]).astype(BF16)
            h_scr[...] = h
            h_ref[...] = h
            flog_ref[...] = _dot_nt(h, wf_ref[...])

        proj_ref[...] = _dot_nt(h_scr[...], w_ref[...]).astype(BF16)

    return pl.pallas_call(
        body, name="mix_proj_fwd", grid=(T // tm, N // tn),
        in_specs=[pl.BlockSpec((tm, D), lambda i, n: (i, 0)),
                  pl.BlockSpec((1, D), lambda i, n: (0, 0)),
                  pl.BlockSpec((tn, D), lambda i, n: (n, 0)),
                  pl.BlockSpec((LANES, D), lambda i, n: (0, 0))],
        out_specs=[pl.BlockSpec((tm, D), lambda i, n: (i, 0)),
                   pl.BlockSpec((tm, tn), lambda i, n: (i, n)),
                   pl.BlockSpec((tm, LANES), lambda i, n: (i, 0))],
        out_shape=[jax.ShapeDtypeStruct((T, D), BF16),
                   jax.ShapeDtypeStruct((T, N), BF16),
                   jax.ShapeDtypeStruct((T, LANES), F32)],
        scratch_shapes=[pltpu.VMEM((tm, D), BF16)],
        compiler_params=_params(("arbitrary", "arbitrary")),
    )(x, g, wproj_t, wf_t)


def _log_sigmoid(z):
    return -(jnp.maximum(-z, 0.0) + jnp.log(1.0 + jnp.exp(-jnp.abs(z))))


def _tri(n, lower):
    r = lax.broadcasted_iota(jnp.int32, (n, n), 0)
    c = lax.broadcasted_iota(jnp.int32, (n, n), 1)
    return jnp.where((r >= c) if lower else (r <= c), 1.0, 0.0).astype(F32)


def _dot_f32(a, b):
    return lax.dot_general(a, b, (((1,), (0,)), ((), ())), preferred_element_type=F32,
                           precision=lax.Precision.HIGHEST)


def _fgate_fwd(flog, bias, B, S, ch):
    def body(flog_ref, b_ref, cum_ref):
        tri = _tri(ch, True)
        carry = jnp.zeros((1, LANES), F32)
        for c0 in range(0, S, ch):
            lf = _log_sigmoid(flog_ref[c0:c0 + ch, :] + b_ref[...])
            cs = _dot_f32(tri, lf) + carry
            cum_ref[c0:c0 + ch, :] = cs
            carry = cs[ch - 1:ch, :]

    return pl.pallas_call(
        body, name="fgate_fwd", grid=(B,),
        in_specs=[pl.BlockSpec((S, LANES), lambda b: (b, 0)),
                  pl.BlockSpec((1, LANES), lambda b: (0, 0))],
        out_specs=pl.BlockSpec((S, LANES), lambda b: (b, 0)),
        out_shape=jax.ShapeDtypeStruct((B * S, LANES), F32),
        compiler_params=_params(("arbitrary",)),
    )(flog, bias)


def _fgate_bwd(dcum, flog, bias, B, S, ch):
    def body(dcum_ref, flog_ref, b_ref, dflog_ref, db_ref):
        @pl.when(pl.program_id(0) == 0)
        def _():
            db_ref[...] = jnp.zeros_like(db_ref)

        tri = _tri(ch, False)
        carry = jnp.zeros((1, LANES), F32)
        db = jnp.zeros((1, LANES), F32)
        for c0 in range(S - ch, -1, -ch):
            dlf = _dot_f32(tri, dcum_ref[c0:c0 + ch, :]) + carry
            carry = dlf[0:1, :]
            z = flog_ref[c0:c0 + ch, :] + b_ref[...]
            dz = dlf * _sigmoid(-z)
            dflog_ref[c0:c0 + ch, :] = dz
            db = db + jnp.sum(dz, axis=0, keepdims=True)
        db_ref[...] += db

    return pl.pallas_call(
        body, name="fgate_bwd", grid=(B,),
        in_specs=[pl.BlockSpec((S, LANES), lambda b: (b, 0)),
                  pl.BlockSpec((S, LANES), lambda b: (b, 0)),
                  pl.BlockSpec((1, LANES), lambda b: (0, 0))],
        out_specs=[pl.BlockSpec((S, LANES), lambda b: (b, 0)),
                   pl.BlockSpec((1, LANES), lambda b: (0, 0))],
        out_shape=[jax.ShapeDtypeStruct((B * S, LANES), F32),
                   jax.ShapeDtypeStruct((1, LANES), F32)],
        compiler_params=_params(("arbitrary",)),
    )(dcum, flog, bias)


def _pick_lane(tile, h):
    lane = lax.broadcasted_iota(jnp.int32, tile.shape, 1)
    return jnp.sum(jnp.where(lane == h, tile, 0.0), axis=1, keepdims=True)


def _put_lane(col, h, width=LANES):
    lane = lax.broadcasted_iota(jnp.int32, (col.shape[0], width), 1)
    return jnp.where(lane == h, col, 0.0)


def _pick_row(tile, h):
    row = lax.broadcasted_iota(jnp.int32, tile.shape, 0)
    return jnp.sum(jnp.where(row == h, tile, 0.0), axis=0, keepdims=True)


def _put_row(vec, h):
    row = lax.broadcasted_iota(jnp.int32, (8, vec.shape[1]), 0)
    return jnp.where(row == h, vec, 0.0)


def _causal(tq):
    r = lax.broadcasted_iota(jnp.int32, (tq, tq), 0)
    c = lax.broadcasted_iota(jnp.int32, (tq, tq), 1)
    return r >= c


def _head_halves(t):
    lo = lax.broadcasted_iota(jnp.int32, t.shape, 1) < HEAD_DIM
    zero = jnp.zeros_like(t)
    return jnp.where(lo, t, zero), jnp.where(lo, zero, t)


NEG = -1e30
ATTN_SCALE = 1.0 / math.sqrt(HEAD_DIM)


def _scaled(q):
    return (q.astype(F32) * ATTN_SCALE).astype(q.dtype)


def _attn_fwd(proj, cum, cum_t, B, S, tq, comm=None):
    nq = S // tq

    def body(q_ref, k_ref, v_ref, cum_ref, cumt_ref, o_ref, lse_ref):
        qi, hp = pl.program_id(1), pl.program_id(2)
        qm = _head_halves(_scaled(q_ref[...]))
        cumv = cum_ref[...]
        cq = [_pick_lane(cumv, 2 * hp + e) for e in range(2)]

        def tile(j, carry, masked):
            off = pl.multiple_of(j * tq, tq)
            kj = k_ref[pl.ds(off, tq), :]
            vj = v_ref[pl.ds(off, tq), :]
            ct = cumt_ref[j]
            new = []
            for e in range(2):
                m, l, acc = carry[e]
                s = _dot_nt(qm[e], kj) - _pick_row(ct, 2 * hp + e)
                if masked:
                    s = jnp.where(_causal(tq), s, NEG)
                m_new = jnp.maximum(m, jnp.max(s, axis=1, keepdims=True))
                p = jnp.exp(s - m_new)
                alpha = jnp.exp(m - m_new)
                l = alpha * l + jnp.sum(p, axis=1, keepdims=True)
                acc = alpha * acc + _dot(p.astype(BF16), vj)
                new.append((m_new, l, acc))
            return tuple(new)

        one = (jnp.full((tq, 1), NEG, F32), jnp.zeros((tq, 1), F32), jnp.zeros((tq, LANES), F32))
        carry = lax.fori_loop(0, qi, lambda j, c: tile(j, c, False), (one, one))
        (ma, la, acca), (mb, lb, accb) = tile(qi, carry, True)
        lo = lax.broadcasted_iota(jnp.int32, (tq, LANES), 1) < HEAD_DIM
        o_ref[...] = jnp.where(lo, acca / la, accb / lb).astype(BF16)

        @pl.when(hp == 0)
        def _():
            lse_ref[...] = jnp.zeros_like(lse_ref)

        lse_ref[...] += (_put_lane(ma + jnp.log(la) + cq[0], 2 * hp) + _put_lane(mb + jnp.log(lb) + cq[1], 2 * hp + 1))

    kv = lambda first: pl.BlockSpec((S, LANES), lambda b, i, hp: (b, first + hp))
    return _pallas(
        body, "attn_fwd", (B, nq, HEAD_PAIRS),
        [pl.BlockSpec((tq, LANES), lambda b, i, hp: (b * nq + i, hp)),
         kv(ATTN_W // LANES), kv(2 * ATTN_W // LANES),
         pl.BlockSpec((tq, LANES), lambda b, i, hp: (b * nq + i, 0)),
         pl.BlockSpec((None, nq, 8, tq), lambda b, i, hp: (b, 0, 0, 0))],
        [pl.BlockSpec((tq, LANES), lambda b, i, hp: (b * nq + i, hp)),
         pl.BlockSpec((tq, LANES), lambda b, i, hp: (b * nq + i, 0))],
        [jax.ShapeDtypeStruct((B * S, ATTN_W), BF16), jax.ShapeDtypeStruct((B * S, LANES), F32)],
        [], (proj, proj, proj, cum, cum_t), comm)


def _attn_bwd(proj, o, do, lse, cum, cum_t, B, S, tq, comm=None):
    nq = S // tq

    def body(q_ref, k_ref, v_ref, o_ref, do_ref, lse_ref, cum_ref, cumt_ref,
             dq_ref, dk_ref, dv_ref, dcq_ref, dck_ref, dq_scr):
        hp, kj = pl.program_id(1), pl.program_id(2)

        @pl.when(kj == 0)
        def _():
            dq_scr[...] = jnp.zeros_like(dq_scr)

        @pl.when((kj == 0) & (hp == 0))
        def _():
            dcq_ref[...] = jnp.zeros_like(dcq_ref)
            dck_ref[...] = jnp.zeros_like(dck_ref)

        kv = k_ref[...]
        vv = v_ref[...]
        km = _head_halves(kv)
        ct = cumt_ref[...]
        ck = [_pick_row(ct, 2 * hp + e) for e in range(2)]

        def tile(i, carry, masked):
            dk, dv, dcol = carry
            off = pl.multiple_of(i * tq, tq)
            qi = q_ref[pl.ds(off, tq), :]
            ov = o_ref[pl.ds(off, tq), :].astype(F32)
            qm = _head_halves(_scaled(qi))
            dom = _head_halves(do_ref[pl.ds(off, tq), :])
            cumv = cum_ref[pl.ds(off, tq), :]
            lsev = lse_ref[pl.ds(off, tq), :]
            dcq = jnp.zeros((tq, LANES), F32)
            dq = jnp.zeros((tq, LANES), F32)
            dcol_new = []
            for e in range(2):
                delta = jnp.sum(dom[e].astype(F32) * ov, axis=1, keepdims=True)
                row_term = _pick_lane(cumv, 2 * hp + e) - _pick_lane(lsev, 2 * hp + e)
                p = jnp.exp(_dot_nt(qm[e], kv) + row_term - ck[e])
                if masked:
                    p = jnp.where(_causal(tq), p, 0.0)
                dv = dv + _dot_tn(p.astype(BF16), dom[e])
                ds = p * (_dot_nt(dom[e], vv) - delta)
                dcol_new.append(dcol[e] + jnp.sum(ds, axis=0, keepdims=True))
                dcq = dcq + _put_lane(jnp.sum(ds, axis=1, keepdims=True), 2 * hp + e)
                dsb = ds.astype(BF16)
                dk = dk + _dot_tn(dsb, qm[e])
                dq = dq + _dot(dsb, km[e]) * ATTN_SCALE
            dq_scr[pl.ds(off, tq), :] += dq
            dcq_ref[pl.ds(off, tq), :] += dcq
            return dk, dv, tuple(dcol_new)

        zero_row = jnp.zeros((1, tq), F32)
        init = (jnp.zeros((tq, LANES), F32), jnp.zeros((tq, LANES), F32), (zero_row, zero_row))
        carry = tile(kj, init, True)
        dk, dv, dcol = lax.fori_loop(kj + 1, nq, lambda i, c: tile(i, c, False), carry)
        dk_ref[...] = dk.astype(BF16)
        dv_ref[...] = dv.astype(BF16)
        dck_ref[kj] += -(_put_row(dcol[0], 2 * hp) + _put_row(dcol[1], 2 * hp + 1))

        @pl.when(kj == nq - 1)
        def _():
            dq_ref[...] = dq_scr[...].astype(BF16)

    seq = lambda first: pl.BlockSpec((S, LANES), lambda b, hp, j: (b, first + hp))
    tile_in = lambda first: pl.BlockSpec((tq, LANES), lambda b, hp, j: (b * nq + j, first + hp))
    lanes0 = pl.BlockSpec((S, LANES), lambda b, hp, j: (b, 0))
    out = jax.ShapeDtypeStruct((B * S, ATTN_W), BF16)
    return _pallas(
        body, "attn_bwd", (B, HEAD_PAIRS, nq),
        [seq(0), tile_in(ATTN_W // LANES), tile_in(2 * ATTN_W // LANES), seq(0), seq(0), lanes0, lanes0,
         pl.BlockSpec((None, None, 8, tq), lambda b, hp, j: (b, j, 0, 0))],
        [seq(0), tile_in(0), tile_in(0), lanes0,
         pl.BlockSpec((None, nq, 8, tq), lambda b, hp, j: (b, 0, 0, 0))],
        [out, out, out, jax.ShapeDtypeStruct((B * S, LANES), F32), jax.ShapeDtypeStruct((B, nq, 8, tq), F32)],
        [pltpu.VMEM((S, LANES), F32)],
        (proj, proj, proj, o, do, lse, cum, cum_t), comm)


def _shift_down(u, n):
    row = lax.broadcasted_iota(jnp.int32, u.shape, 0)
    return jnp.where(row >= n, pltpu.roll(u, n, 0), 0.0)


def _shift_up(u, n):
    rows = u.shape[0]
    row = lax.broadcasted_iota(jnp.int32, u.shape, 0)
    return jnp.where(row < rows - n, pltpu.roll(u, rows - n, 0), 0.0)


def _conv_specs(S):
    cb = pl.BlockSpec((S, LANES), lambda g, b: (b, COL_CB // LANES + g))
    cc = pl.BlockSpec((S, LANES), lambda g, b: (b, COL_CC // LANES + g))
    cx = pl.BlockSpec((S, LANES), lambda g, b: (b, COL_CX // LANES + g))
    w = pl.BlockSpec((8, LANES), lambda g, b: (0, g))
    return cb, cc, cx, w


def _conv_fwd(proj, conv_w, B, S):
    def body(cb_ref, cc_ref, cx_ref, w_ref, y_ref):
        u = cc_ref[...].astype(F32) * cx_ref[...].astype(F32)
        w = w_ref[...]
        conv = w[0:1, :] * _shift_down(u, 2) + w[1:2, :] * _shift_down(u, 1) + w[2:3, :] * u
        y_ref[...] = (cb_ref[...].astype(F32) * conv).astype(BF16)

    cb, cc, cx, w = _conv_specs(S)
    return pl.pallas_call(
        body, name="conv_fwd", grid=(CONV_W // LANES, B),
        in_specs=[cb, cc, cx, w],
        out_specs=pl.BlockSpec((S, LANES), lambda g, b: (b, g)),
        out_shape=jax.ShapeDtypeStruct((B * S, CONV_W), BF16),
        compiler_params=_params(("arbitrary", "arbitrary")),
    )(proj, proj, proj, conv_w)


def _conv_bwd(dy, proj, conv_w, B, S):
    def body(dy_ref, cb_ref, cc_ref, cx_ref, w_ref, dcb_ref, dcc_ref, dcx_ref, dw_ref):
        @pl.when(pl.program_id(1) == 0)
        def _():
            dw_ref[...] = jnp.zeros_like(dw_ref)

        ccv = cc_ref[...].astype(F32)
        cxv = cx_ref[...].astype(F32)
        u = ccv * cxv
        u1 = _shift_down(u, 1)
        u2 = _shift_down(u, 2)
        w = w_ref[...]
        conv = w[0:1, :] * u2 + w[1:2, :] * u1 + w[2:3, :] * u
        dyv = dy_ref[...].astype(F32)
        dcb_ref[...] = (dyv * conv).astype(BF16)
        dconv = dyv * cb_ref[...].astype(F32)
        du = w[2:3, :] * dconv + w[1:2, :] * _shift_up(dconv, 1) + w[0:1, :] * _shift_up(dconv, 2)
        dcc_ref[...] = (du * cxv).astype(BF16)
        dcx_ref[...] = (du * ccv).astype(BF16)
        row = lax.broadcasted_iota(jnp.int32, (8, LANES), 0)
        dw = jnp.where(row == 0, jnp.sum(dconv * u2, axis=0, keepdims=True),
                       jnp.where(row == 1, jnp.sum(dconv * u1, axis=0, keepdims=True),
                                 jnp.where(row == 2, jnp.sum(dconv * u, axis=0, keepdims=True), 0.0)))
        dw_ref[...] += dw

    cb, cc, cx, w = _conv_specs(S)
    out = pl.BlockSpec((S, LANES), lambda g, b: (b, g))
    return pl.pallas_call(
        body, name="conv_bwd", grid=(CONV_W // LANES, B),
        in_specs=[out, cb, cc, cx, w],
        out_specs=[out, out, out, w],
        out_shape=[jax.ShapeDtypeStruct((B * S, CONV_W), BF16)] * 3 + [jax.ShapeDtypeStruct((8, CONV_W), F32)],
        compiler_params=_params(("arbitrary", "arbitrary")),
    )(dy, proj, proj, proj, conv_w)


def _gate_specs(tm, D):
    ga = pl.BlockSpec((tm, D), lambda i: (i, COL_GATES // D))
    gc = pl.BlockSpec((tm, D), lambda i: (i, COL_GATES // D + 1))
    return ga, gc


def _mix_out_fwd(x, o, yc, proj, woa, woc, wout, tm):
    T, D = x.shape

    def body(x_ref, o_ref, yc_ref, ga_ref, gc_ref, woa_ref, woc_ref, wout_ref, out_ref):
        ya = _dot(o_ref[...], woa_ref[...])
        yp = _dot(yc_ref[...], woc_ref[...])
        merged = _sigmoid(ga_ref[...].astype(F32)) * ya + _sigmoid(gc_ref[...].astype(F32)) * yp
        out_ref[...] = x_ref[...] + _dot(merged.astype(BF16), wout_ref[...])

    ga, gc = _gate_specs(tm, D)
    row = lambda w: pl.BlockSpec((tm, w), lambda i: (i, 0))
    whole = lambda a: pl.BlockSpec(a.shape, lambda i: (0, 0))
    return pl.pallas_call(
        body, name="mix_out_fwd", grid=(T // tm,),
        in_specs=[row(D), row(ATTN_W), row(CONV_W), ga, gc, whole(woa), whole(woc), whole(wout)],
        out_specs=row(D),
        out_shape=jax.ShapeDtypeStruct((T, D), F32),
        compiler_params=_params(("arbitrary",)),
    )(x, o, yc, proj, proj, woa, woc, wout)


def _mix_out_bwd(dx, o, yc, proj, woa, woc, wout, tm, comm=None):
    T, D = dx.shape
    nt = T // tm

    def body(dx_ref, o_ref, yc_ref, ga_ref, gc_ref, woa_ref, woc_ref, wout_ref,
             do_ref, dyc_ref, dg_ref, dwoa_ref, dwoc_ref, dwout_ref, acca, accc, acco):
        t = pl.program_id(0)

        @pl.when(t == 0)
        def _():
            acca[...] = jnp.zeros_like(acca)
            accc[...] = jnp.zeros_like(accc)
            acco[...] = jnp.zeros_like(acco)

        dxb = dx_ref[...].astype(BF16)
        ov, ycv = o_ref[...], yc_ref[...]
        ya = _dot(ov, woa_ref[...])
        yp = _dot(ycv, woc_ref[...])
        sa = _sigmoid(ga_ref[...].astype(F32))
        sc = _sigmoid(gc_ref[...].astype(F32))
        merged = (sa * ya + sc * yp).astype(BF16)
        dm = _dot_nt(dxb, wout_ref[...])
        dya = (dm * sa).astype(BF16)
        dyp = (dm * sc).astype(BF16)
        dg_ref[:, :D] = (dm * ya * sa * (1.0 - sa)).astype(BF16)
        dg_ref[:, D:] = (dm * yp * sc * (1.0 - sc)).astype(BF16)
        do_ref[...] = _dot_nt(dya, woa_ref[...]).astype(BF16)
        dyc_ref[...] = _dot_nt(dyp, woc_ref[...]).astype(BF16)
        acca[...] += _dot_tn(ov, dya)
        accc[...] += _dot_tn(ycv, dyp)
        acco[...] += _dot_tn(merged, dxb)

        @pl.when(t == nt - 1)
        def _():
            dwoa_ref[...] = acca[...].astype(BF16)
            dwoc_ref[...] = accc[...].astype(BF16)
            dwout_ref[...] = acco[...].astype(BF16)

    ga, gc = _gate_specs(tm, D)
    row = lambda w: pl.BlockSpec((tm, w), lambda i: (i, 0))
    whole = lambda a: pl.BlockSpec(a.shape, lambda i: (0, 0))
    return _pallas(
        body, "mix_out_bwd", (nt,),
        [row(D), row(ATTN_W), row(CONV_W), ga, gc, whole(woa), whole(woc), whole(wout)],
        [row(ATTN_W), row(CONV_W), row(2 * D), whole(woa), whole(woc), whole(wout)],
        [jax.ShapeDtypeStruct((T, ATTN_W), BF16), jax.ShapeDtypeStruct((T, CONV_W), BF16),
         jax.ShapeDtypeStruct((T, 2 * D), BF16),
         jax.ShapeDtypeStruct(woa.shape, BF16), jax.ShapeDtypeStruct(woc.shape, BF16),
         jax.ShapeDtypeStruct(wout.shape, BF16)],
        [pltpu.VMEM(woa.shape, F32), pltpu.VMEM(woc.shape, F32), pltpu.VMEM(wout.shape, F32)],
        (dx, o, yc, proj, proj, woa, woc, wout), comm)


def _proj_pieces(dq, dk, dv, dcb, dcc, dcx, dgates, dflog):
    D = dgates.shape[1] // 2
    return [(dq, ATTN_W, 0), (dk, ATTN_W, 0), (dv, ATTN_W, 0), (dcb, CONV_W, 0), (dcc, CONV_W, 0), (dcx, CONV_W, 0),
            (dgates, D, 0), (dgates, D, 1), (dflog, LANES, 0)]


def _mix_proj_bwd_dx(dres, x, g, pieces, wproj_t, wf_t, tm, comm=None):
    T, D = x.shape
    n = len(pieces)
    w_blocks = [(ATTN_W, 0), (ATTN_W, 1), (ATTN_W, 2), (CONV_W, 3), (CONV_W, 4), (CONV_W, 5),
                (D, COL_GATES // D), (D, COL_GATES // D + 1)]

    def body(*refs):
        dres_ref, x_ref, g_ref = refs[:3]
        p_refs, w_refs = refs[3:3 + n], refs[3 + n:3 + 2 * n]
        dx_ref, dg_ref = refs[3 + 2 * n:]

        @pl.when(pl.program_id(0) == 0)
        def _():
            dg_ref[...] = jnp.zeros_like(dg_ref)

        dh = _dot(p_refs[0][...].astype(BF16), w_refs[0][...])
        for p_ref, w_ref in zip(p_refs[1:], w_refs[1:]):
            dh = dh + _dot(p_ref[...].astype(BF16), w_ref[...])
        xhat, inv = _rms(x_ref[...])
        dx, dg = _rms_bwd(dh, xhat, inv, g_ref[...])
        dx_ref[...] = dres_ref[...] + dx
        dg_ref[...] += dg

    row = pl.BlockSpec((tm, D), lambda i: (i, 0))
    vec = pl.BlockSpec((1, D), lambda i: (0, 0))
    p_specs = [pl.BlockSpec((tm, w), lambda i, cb=cb: (i, cb)) for _, w, cb in pieces]
    w_specs = [pl.BlockSpec((r, D), lambda i, rb=rb: (rb, 0)) for r, rb in w_blocks]
    w_specs.append(pl.BlockSpec((LANES, D), lambda i: (0, 0)))
    return _pallas(
        body, "mix_proj_bwd_dx", (T // tm,),
        [row, row, vec] + p_specs + w_specs, [row, vec],
        [jax.ShapeDtypeStruct((T, D), F32), jax.ShapeDtypeStruct((1, D), F32)], [],
        (dres, x, g, *[p for p, _, _ in pieces], *([wproj_t] * len(w_blocks)), wf_t), comm)


def _matmuls_tn(name, pieces, b, tk):
    T, N = b.shape
    nt = T // tk
    n = len(pieces)

    def body(*refs):
        a_refs, b_ref, out_refs, accs = refs[:n], refs[n], refs[n + 1:2 * n + 1], refs[2 * n + 1:]
        t = pl.program_id(0)

        @pl.when(t == 0)
        def _():
            for acc in accs:
                acc[...] = jnp.zeros_like(acc)

        bv = b_ref[...]
        for a_ref, acc in zip(a_refs, accs):
            acc[...] += _dot_tn(a_ref[...].astype(BF16), bv)

        @pl.when(t == nt - 1)
        def _():
            for out_ref, acc in zip(out_refs, accs):
                out_ref[...] = acc[...].astype(BF16)

    return pl.pallas_call(
        body, name=name, grid=(nt,),
        in_specs=[pl.BlockSpec((tk, w), lambda t, cb=cb: (t, cb)) for _, w, cb in pieces]
        + [pl.BlockSpec((tk, N), lambda t: (t, 0))],
        out_specs=[pl.BlockSpec((w, N), lambda t: (0, 0)) for _, w, _ in pieces],
        out_shape=[jax.ShapeDtypeStruct((w, N), BF16) for _, w, _ in pieces],
        scratch_shapes=[pltpu.VMEM((w, N), F32) for _, w, _ in pieces],
        compiler_params=_params(("arbitrary",)),
    )(*[a for a, _, _ in pieces], b)


def _final_loss(x, target, g, tm):
    T, D = x.shape

    def body(x_ref, t_ref, g_ref, dx_ref, loss_ref, dg_ref):
        @pl.when(pl.program_id(0) == 0)
        def _():
            loss_ref[...] = jnp.zeros_like(loss_ref)
            dg_ref[...] = jnp.zeros_like(dg_ref)

        xhat, inv = _rms(x_ref[...])
        err = xhat * g_ref[...] - t_ref[...]
        loss_ref[...] += 0.5 * jnp.sum(jnp.sum(err * err, axis=1, keepdims=True), axis=0, keepdims=True) / D
        dx, dg = _rms_bwd(err * (1.0 / D), xhat, inv, g_ref[...])
        dx_ref[...] = dx
        dg_ref[...] += dg

    row = pl.BlockSpec((tm, D), lambda i: (i, 0))
    return pl.pallas_call(
        body, name="final_loss", grid=(T // tm,),
        in_specs=[row, row, pl.BlockSpec((1, D), lambda i: (0, 0))],
        out_specs=[row, pl.BlockSpec((1, LANES), lambda i: (0, 0)), pl.BlockSpec((1, D), lambda i: (0, 0))],
        out_shape=[jax.ShapeDtypeStruct((T, D), F32), jax.ShapeDtypeStruct((1, LANES), F32),
                   jax.ShapeDtypeStruct((1, D), F32)],
        compiler_params=_params(("arbitrary",)),
    )(x, target, g)


class _LocalPlan:
    def __init__(self, stacks, small):
        self.stacks, self.small, self.grads = stacks, small, {}

    def weights(self, group):
        return _LAYOUTS[group](self.stacks, self.small)

    def rider(self, kernel_name):
        return None

    def arrived(self, kernel_name, results):
        pass

    def reduce(self, group, grads):
        self.grads.update(grads)

    def reduce_small(self, small_grads, loss):
        pass


def _local_step(x, target, plan, B, S):
    T, D = x.shape
    tm = min(512, T)
    tm_fwd = min(1024, T)
    tq = min(512, S)
    nq = S // tq
    ch = min(256, S)

    def riding(kernel_name, build):
        results, brought = build(plan.rider(kernel_name))
        plan.arrived(kernel_name, brought)
        return results

    w1 = plan.weights("ffn1_in")
    hg1, hu1 = riding("ffn1_up", lambda comm: _ffn_up(
        "ffn1_up", x, w1["ffn1_norm"], w1["ffn1_gate"], w1["ffn1_up"], tm_fwd, comm))
    w1 = plan.weights("ffn1")
    x1, = riding("ffn1_down", lambda comm: _ffn_down("ffn1_down", x, hg1, hu1, w1["ffn1_down"], tm_fwd, comm))
    wm = plan.weights("mix")
    h, proj, flog = _mix_proj_fwd(x1, wm["mix_norm"], wm["w_proj"], wm["w_f"], tm_fwd, 1280)
    cum = _fgate_fwd(flog, wm["b_forget"], B, S, ch)
    cum_t = jnp.transpose(cum[:, :N_HEADS].reshape(B, nq, tq, N_HEADS), (0, 1, 3, 2))
    o, lse = riding("attn_fwd", lambda comm: _attn_fwd(proj, cum, cum_t, B, S, tq, comm))
    yc = _conv_fwd(proj, wm["conv_w"], B, S)
    x2 = _mix_out_fwd(x1, o, yc, proj, wm["w_o_attn"], wm["w_o_conv"], wm["w_out"], tm)
    w2 = plan.weights("ffn2")
    x3, hg2, hu2 = _ffn_fwd("ffn2_fwd", x2, w2["ffn2_norm"], w2["ffn2_gate"], w2["ffn2_up"], w2["ffn2_down"], tm_fwd)[0]
    dx3, loss, d_final_norm = _final_loss(x3, target, w2["final_norm"], tm)

    g = {"final_norm": d_final_norm}
    dx2, dhg2, dhu2, g["ffn2_norm"] = _ffn_bwd_dx("ffn2_bwd_dx", dx3, x2, w2["ffn2_norm"], hg2, hu2,
                                                  w2["ffn2_gate"], w2["ffn2_up"], w2["ffn2_down"], tm)[0]
    plan.reduce("ffn2", dict(zip(("ffn2_gate", "ffn2_up", "ffn2_down"),
                                 _ffn_bwd_dw("ffn2_bwd_dw", dx3, x2, w2["ffn2_norm"], hg2, hu2, dhg2, dhu2, tm)[0])))
    do, dyc, dgates, dwoa, dwoc, dwout = riding("mix_out_bwd", lambda comm: _mix_out_bwd(
        dx2, o, yc, proj, wm["w_o_attn"], wm["w_o_conv"], wm["w_out"], tm, comm))
    plan.reduce("out", dict(w_o_attn=_shard_cols(dwoa), w_o_conv=_shard_cols(dwoc), w_out=dwout.reshape(N_CHIPS, -1, D)))
    dq, dk, dv, dcq, dck = riding("attn_bwd", lambda comm: _attn_bwd(proj, o, do, lse, cum, cum_t, B, S, tq, comm))
    dcum = dcq + jnp.pad(jnp.transpose(dck, (0, 1, 3, 2)).reshape(T, N_HEADS), ((0, 0), (0, LANES - N_HEADS)))
    dflog, g["b_forget"] = _fgate_bwd(dcum, flog, wm["b_forget"], B, S, ch)
    dcb, dcc, dcx, g["conv_w"] = _conv_bwd(dyc, proj, wm["conv_w"], B, S)
    pieces = _proj_pieces(dq, dk, dv, dcb, dcc, dcx, dgates, dflog)
    dwq, dwk, dwv, dwcb, dwcc, dwcx = _matmuls_tn("mix_dw_a", pieces[:6], h, tm)
    dwga, dwgc, dwf = _matmuls_tn("mix_dw_b", pieces[6:], h, tm)
    dwin_t = jnp.concatenate([dwq, dwk, dwv, dwf[:N_HEADS], dwcb, dwcc, dwcx, dwga, dwgc], axis=0)
    plan.reduce("w_in", {"w_in": dwin_t.reshape(N_CHIPS, -1, D)})
    dx1, g["mix_norm"] = riding("mix_proj_bwd_dx", lambda comm: _mix_proj_bwd_dx(
        dx2, x1, wm["mix_norm"], pieces, wm["w_proj"], wm["w_f"], min(256, T), comm))
    grad_x, dhg1, dhu1, g["ffn1_norm"] = _ffn_bwd_dx(
        "ffn1_bwd_dx", dx1, x, w1["ffn1_norm"], hg1, hu1, w1["ffn1_gate"], w1["ffn1_up"], w1["ffn1_down"], tm)[0]
    plan.reduce_small(g, loss)
    plan.reduce("ffn1", dict(zip(("ffn1_gate", "ffn1_up", "ffn1_down"), riding("ffn1_bwd_dw", lambda comm: _ffn_bwd_dw(
        "ffn1_bwd_dw", dx1, x, w1["ffn1_norm"], hg1, hu1, dhg1, dhu1, tm, comm)))))
    return loss, grad_x, g


TRANSPOSED = ("ffn1_gate", "ffn1_up", "ffn2_gate", "ffn2_up", "w_in")
NORMS = ("ffn1_norm", "mix_norm", "ffn2_norm", "final_norm")


def _unshard_cols(a):
    return jnp.transpose(a, (1, 0, 2)).reshape(a.shape[1], N_CHIPS * a.shape[2])


def _shard_cols(a):
    return jnp.transpose(a.reshape(a.shape[0], N_CHIPS, a.shape[1] // N_CHIPS), (1, 0, 2))


def _layout_ffn(which):
    def layout(st, small):
        w = {n: st[n] for n in (which + "_gate", which + "_up", which + "_down")}
        w[which + "_norm"] = small[which + "_norm"].reshape(1, -1)
        if which == "ffn2":
            w["final_norm"] = small["final_norm"].reshape(1, -1)
        return w
    return layout


def _layout_mix(st, small):
    win_t = st["w_in"].reshape(-1, st["w_in"].shape[2])
    return {
        "w_proj": jnp.concatenate([win_t[:N_FORGET_COL], win_t[N_FORGET_COL + N_HEADS:]], axis=0),
        "w_f": jnp.pad(win_t[N_FORGET_COL:N_FORGET_COL + N_HEADS], ((0, LANES - N_HEADS), (0, 0))),
        "w_o_attn": _unshard_cols(st["w_o_attn"]),
        "w_o_conv": _unshard_cols(st["w_o_conv"]),
        "w_out": st["w_out"].reshape(-1, st["w_out"].shape[2]),
        "conv_w": _unshard_cols(st["conv_w"]),
        "mix_norm": small["mix_norm"].reshape(1, -1),
        "b_forget": jnp.pad(small["b_forget"].reshape(1, -1), ((0, 0), (0, LANES - N_HEADS))),
    }


def _layout_ffn1_in(st, small):
    return {"ffn1_gate": st["ffn1_gate"], "ffn1_up": st["ffn1_up"], "ffn1_norm": small["ffn1_norm"].reshape(1, -1)}


_LAYOUTS = {"ffn1_in": _layout_ffn1_in, "ffn1": _layout_ffn("ffn1"), "mix": _layout_mix, "ffn2": _layout_ffn("ffn2")}


ANY = pl.BlockSpec(memory_space=pl.ANY)
BIG = ("ffn1_gate", "ffn1_up", "ffn1_down", "w_in", "w_o_attn", "w_o_conv", "w_out",
       "ffn2_gate", "ffn2_up", "ffn2_down")


def _place():
    x, y, c = lax.axis_index("x"), lax.axis_index("y"), lax.axis_index("c")
    others = [(1 - x, y), (x, 1 - y), (1 - x, 1 - y)]
    return x, y, c, others


def _col_halves(cols, c):
    hc = cols // 2
    return pl.ds(pl.multiple_of(c * hc, LANES), hc), pl.ds(pl.multiple_of((1 - c) * hc, LANES), hc)


def _gather_comm(shards, conv_shard=None):
    n = len(shards)
    inputs = list(shards) + ([] if conv_shard is None else [conv_shard])

    def copies(ins, outs, sems):
        send_sems, recv_sems, pass_send, pass_recv = sems[:4]
        x, y, c, others = _place()

        def chip_copy(a, j, chip):
            mine, _ = _col_halves(ins[a].shape[1], c)
            return pltpu.make_async_remote_copy(
                src_ref=ins[a].at[:, mine], dst_ref=outs[a].at[chip, :, mine],
                send_sem=send_sems.at[3 * a + j], recv_sem=recv_sems.at[3 * a + j],
                device_id=(*others[j], c), device_id_type=MESH)

        def pass_copy(a, j, chip, half):
            return pltpu.make_async_remote_copy(
                src_ref=outs[a].at[chip, :, half], dst_ref=outs[a].at[chip, :, half],
                send_sem=pass_send.at[3 * a + j], recv_sem=pass_recv.at[3 * a + j],
                device_id=(x, y, 1 - c), device_id_type=MESH)

        def conv_copy(j, chip):
            return pltpu.make_async_remote_copy(
                src_ref=ins[n], dst_ref=outs[n].at[chip],
                send_sem=sems[4].at[j], recv_sem=sems[5].at[j],
                device_id=(*others[j], c), device_id_type=MESH)

        me = 2 * x + y
        sends = [chip_copy(a, j, me) for a in range(n) for j in range(3)]
        if conv_shard is not None:
            sends += [conv_copy(j, me) for j in range(3)]
        return c, others, sends, chip_copy, pass_copy, conv_copy

    def start(ins, outs, sems):
        for cp in copies(ins, outs, sems)[2]:
            cp.start()

    def finish(ins, outs, sems):
        c, others, sends, chip_copy, pass_copy, conv_copy = copies(ins, outs, sems)
        passed = []
        for a in range(n):
            mine, _ = _col_halves(ins[a].shape[1], c)
            for j, (ox, oy) in enumerate(others):
                chip_copy(a, j, 2 * ox + oy).wait_recv()
                passed.append(pass_copy(a, j, 2 * ox + oy, mine))
                passed[-1].start()
        for a in range(n):
            _, theirs = _col_halves(ins[a].shape[1], c)
            for j, (ox, oy) in enumerate(others):
                pass_copy(a, j, 2 * ox + oy, theirs).wait_recv()
        if conv_shard is not None:
            for j, (ox, oy) in enumerate(others):
                conv_copy(j, 2 * ox + oy).wait_recv()
        for cp in sends + passed:
            cp.wait_send()

    scratch = [pltpu.SemaphoreType.DMA((3 * n,))] * 4
    if conv_shard is not None:
        scratch += [pltpu.SemaphoreType.DMA((3,))] * 2
    return _Comm(inputs, [jax.ShapeDtypeStruct((N_CHIPS,) + s.shape, s.dtype) for s in inputs], scratch, start, finish)


def _fill_own(stacks, shards):
    chip = 2 * lax.axis_index("x") + lax.axis_index("y")
    return [lax.dynamic_update_index_in_dim(st, s, chip, 0) for st, s in zip(stacks, shards)]


def _run_comm(name, comm):
    ci, co = len(comm.inputs), len(comm.out_shape)

    def body(*refs):
        comm.start(refs[:ci], refs[ci:ci + co], refs[ci + co:])
        comm.finish(refs[:ci], refs[ci:ci + co], refs[ci + co:])

    return pl.pallas_call(body, name=name, in_specs=[ANY] * ci, out_specs=[ANY] * co, out_shape=comm.out_shape,
                          scratch_shapes=comm.scratch)(*comm.inputs)


def _sibling_exchange_comm(grads):
    n = len(grads)

    def copies(ins, outs, sems):
        x, y, c, _ = _place()
        return [pltpu.make_async_remote_copy(
            src_ref=ins[a].at[:, :, _col_halves(ins[a].shape[2], c)[1]], dst_ref=outs[a],
            send_sem=sems[0].at[a], recv_sem=sems[1].at[a],
            device_id=(x, y, 1 - c), device_id_type=MESH) for a in range(n)]

    def start(ins, outs, sems):
        for cp in copies(ins, outs, sems):
            cp.start()

    def finish(ins, outs, sems):
        for cp in copies(ins, outs, sems):
            cp.wait()

    half = lambda s: jax.ShapeDtypeStruct((s.shape[0], s.shape[1], s.shape[2] // 2), s.dtype)
    return _Comm(grads, [half(s) for s in grads], [pltpu.SemaphoreType.DMA((n,))] * 2, start, finish)


def _merge_comms(comms):
    def split(refs, count):
        out, at = [], 0
        for cm in comms:
            out.append(refs[at:at + count(cm)])
            at += count(cm)
        return out

    def parts(ins, outs, sems):
        return zip(comms, split(ins, lambda cm: len(cm.inputs)), split(outs, lambda cm: len(cm.out_shape)),
                   split(sems, lambda cm: len(cm.scratch)))

    def start(ins, outs, sems):
        for cm, i, o, s in parts(ins, outs, sems):
            cm.start(i, o, s)

    def finish(ins, outs, sems):
        for cm, i, o, s in parts(ins, outs, sems):
            cm.finish(i, o, s)

    return _Comm(sum([cm.inputs for cm in comms], []), sum([cm.out_shape for cm in comms], []),
                 sum([cm.scratch for cm in comms], []), start, finish)


def _add_halves(name, grads, recvs, core):
    n = len(grads)

    def body(core_ref, *refs):
        for g_ref, r_ref, out_ref in zip(refs[:n], refs[n:2 * n], refs[2 * n:]):
            out_ref[...] = (g_ref[...].astype(F32) + r_ref[...].astype(F32)).astype(BF16)

    half = lambda g: pl.BlockSpec((None, g.shape[1], g.shape[2] // 2), lambda k, core_ref: (k, 0, 0))
    mine = lambda g: pl.BlockSpec((None, g.shape[1], g.shape[2] // 2), lambda k, core_ref: (k, 0, core_ref[0]))
    return pl.pallas_call(
        body, name=name,
        grid_spec=pltpu.PrefetchScalarGridSpec(
            num_scalar_prefetch=1, grid=(N_CHIPS,),
            in_specs=[mine(g) for g in grads] + [half(g) for g in grads],
            out_specs=[half(g) for g in grads]),
        out_shape=[jax.ShapeDtypeStruct(r.shape, BF16) for r in recvs],
        compiler_params=_params(("arbitrary",)),
    )(core, *grads, *recvs)


def _chip_exchange_comm(parts):
    n = len(parts)

    def copies(ins, outs, sems):
        x, y, c, others = _place()
        return [pltpu.make_async_remote_copy(
            src_ref=ins[a].at[2 * ox + oy], dst_ref=outs[a].at[j],
            send_sem=sems[0].at[3 * a + j], recv_sem=sems[1].at[3 * a + j],
            device_id=(ox, oy, c), device_id_type=MESH) for a in range(n) for j, (ox, oy) in enumerate(others)]

    def start(ins, outs, sems):
        for cp in copies(ins, outs, sems):
            cp.start()

    def finish(ins, outs, sems):
        for cp in copies(ins, outs, sems):
            cp.wait()

    return _Comm(parts, [jax.ShapeDtypeStruct((3,) + s.shape[1:], s.dtype) for s in parts],
                 [pltpu.SemaphoreType.DMA((3 * n,))] * 2, start, finish)


HBM = pl.BlockSpec(memory_space=pltpu.HBM)
SEM = pl.BlockSpec(memory_space=pltpu.SEMAPHORE)


def _split_exchange_copies(parts, lands, send_sems, recv_sems):
    x, y, c, others = _place()
    return [pltpu.make_async_remote_copy(
        src_ref=parts[a].at[2 * ox + oy], dst_ref=lands[a].at[j],
        send_sem=send_sems.at[3 * a + j], recv_sem=recv_sems.at[3 * a + j],
        device_id=(ox, oy, c), device_id_type=MESH) for a in range(len(parts)) for j, (ox, oy) in enumerate(others)]


def _exchange_start(name, parts):
    n = len(parts)

    def body(*refs):
        ins, lands = refs[:n], refs[n:2 * n]
        send_sems, recv_sems, token = refs[2 * n], refs[2 * n + 1], refs[-1]
        for cp in _split_exchange_copies(ins, lands, send_sems, recv_sems):
            cp.start()
        token[...] = jnp.zeros_like(token)

    land_shape = [(3,) + p.shape[1:] for p in parts]
    outs = pl.pallas_call(
        body, name=name,
        out_shape=[pltpu.SemaphoreType.DMA((3 * n,)), pltpu.SemaphoreType.DMA((3 * n,))]
        + [pltpu.HBM(p.shape, p.dtype) for p in parts] + [pltpu.HBM(s, p.dtype) for s, p in zip(land_shape, parts)]
        + [jax.ShapeDtypeStruct((8, LANES), F32)],
        in_specs=[HBM] * (2 * n), out_specs=[SEM, SEM] + [HBM] * (2 * n) + [pl.BlockSpec(memory_space=pltpu.VMEM)],
        input_output_aliases={i: 2 + i for i in range(2 * n)},
        compiler_params=pltpu.CompilerParams(has_side_effects=pltpu.SideEffectType.DATAFLOW_SIDE_EFFECTING),
    )(*[pltpu.with_memory_space_constraint(p, pltpu.HBM) for p in parts],
      *[pltpu.with_memory_space_constraint(lax.empty(s, p.dtype), pltpu.HBM) for s, p in zip(land_shape, parts)])
    return outs[0], outs[1], list(outs[2:2 + n]), list(outs[2 + n:2 + 2 * n]), outs[-1]


def _exchange_wait(name, send_sems, recv_sems, parts, lands, after):
    n = len(parts)

    def body(*refs):
        ins, zones = refs[:n], refs[n:2 * n]
        for cp in _split_exchange_copies(ins, zones, refs[2 * n], refs[2 * n + 1]):
            cp.wait_send()
            cp.wait_recv()

    outs = pl.pallas_call(
        body, name=name,
        out_shape=[pltpu.HBM(p.shape, p.dtype) for p in parts] + [pltpu.HBM(z.shape, z.dtype) for z in lands],
        in_specs=[HBM] * (2 * n) + [SEM, SEM] + [ANY] * len(after), out_specs=[HBM] * (2 * n),
        input_output_aliases={i: i for i in range(2 * n)},
        compiler_params=pltpu.CompilerParams(has_side_effects=pltpu.SideEffectType.DATAFLOW_SIDE_EFFECTING),
    )(*parts, *lands, send_sems, recv_sems, *after)
    return list(outs[:n]), list(outs[n:])


def _sum_chips(name, owns, recvs, chip, after):
    n = len(owns)
    hc = owns[0].shape[2]
    assert all(o.shape[2] == hc for o in owns)

    def body(chip_ref, *refs):
        for own_ref, recv_ref, out_ref in zip(refs[:n], refs[n:2 * n], refs[2 * n + 1:]):
            acc = own_ref[...].astype(F32)
            for j in range(3):
                acc = acc + recv_ref[j].astype(F32)
            out_ref[...] = acc

    return pl.pallas_call(
        body, name=name,
        grid_spec=pltpu.PrefetchScalarGridSpec(
            num_scalar_prefetch=1, grid=(hc // LANES,),
            in_specs=[pl.BlockSpec((None, o.shape[1], LANES), lambda i, chip_ref: (chip_ref[0], 0, i)) for o in owns]
            + [pl.BlockSpec((3, o.shape[1], LANES), lambda i, chip_ref: (0, 0, i)) for o in owns]
            + [pl.BlockSpec((8, LANES), lambda i, chip_ref: (0, 0))],
            out_specs=[pl.BlockSpec((o.shape[1], LANES), lambda i, chip_ref: (0, i)) for o in owns]),
        out_shape=[jax.ShapeDtypeStruct((o.shape[1], hc), F32) for o in owns],
        compiler_params=_params(("arbitrary",)),
    )(chip, *owns, *recvs, after)


def _share_halves(name, halves):
    n = len(halves)

    def body(*refs):
        srcs, dsts = refs[:n], refs[n:2 * n]
        send_sems, recv_sems = refs[2 * n:]
        x, y, c, _ = _place()
        copies = [pltpu.make_async_remote_copy(
            src_ref=srcs[a], dst_ref=dsts[a], send_sem=send_sems.at[a], recv_sem=recv_sems.at[a],
            device_id=(x, y, 1 - c), device_id_type=MESH) for a in range(n)]
        for cp in copies:
            cp.start()
        for cp in copies:
            cp.wait()

    return pl.pallas_call(
        body, name=name,
        in_specs=[ANY] * n, out_specs=[ANY] * n,
        out_shape=[jax.ShapeDtypeStruct(s.shape, s.dtype) for s in halves],
        scratch_shapes=[pltpu.SemaphoreType.DMA((n,)), pltpu.SemaphoreType.DMA((n,))],
    )(*halves)


def _small_gather_comm(part):
    def copies(ins, outs, sems):
        x, y, c, _ = _place()
        me = 4 * x + 2 * y + c
        both = []
        for d in range(1, N_DEV):
            px, py, pc = (1 - x if d & 4 else x, 1 - y if d & 2 else y, 1 - c if d & 1 else c)
            send = pltpu.make_async_remote_copy(
                src_ref=ins[0], dst_ref=outs[0].at[me], send_sem=sems[0].at[d - 1], recv_sem=sems[1].at[d - 1],
                device_id=(px, py, pc), device_id_type=MESH)
            recv = pltpu.make_async_remote_copy(
                src_ref=ins[0], dst_ref=outs[0].at[4 * px + 2 * py + pc], send_sem=sems[0].at[d - 1],
                recv_sem=sems[1].at[d - 1], device_id=(px, py, pc), device_id_type=MESH)
            both.append((send, recv))
        return both

    def start(ins, outs, sems):
        for send, _ in copies(ins, outs, sems):
            send.start()

    def finish(ins, outs, sems):
        for send, recv in copies(ins, outs, sems):
            recv.wait_recv()
            send.wait_send()

    return _Comm([part], [jax.ShapeDtypeStruct((N_DEV,) + part.shape, F32)],
                 [pltpu.SemaphoreType.DMA((N_DEV - 1,))] * 2, start, finish)


def _sum_devices(parts):
    def body(p_ref, out_ref):
        acc = p_ref[0]
        for k in range(1, N_DEV):
            acc = acc + p_ref[k]
        out_ref[...] = acc

    return pl.pallas_call(
        body, name="sum_devices", grid=(1,),
        in_specs=[pl.BlockSpec(parts.shape, lambda i: (0, 0, 0))],
        out_specs=pl.BlockSpec(parts.shape[1:], lambda i: (0, 0)),
        out_shape=jax.ShapeDtypeStruct(parts.shape[1:], F32),
        compiler_params=_params(("arbitrary",)),
    )(parts)


def _adam_update(w, g, m, v):
    nm = ADAM_B1 * m + (1.0 - ADAM_B1) * g
    nv = ADAM_B2 * v + (1.0 - ADAM_B2) * (g * g)
    m_hat = nm * (1.0 / (1.0 - ADAM_B1 ** ADAM_STEP))
    v_hat = nv * (1.0 / (1.0 - ADAM_B2 ** ADAM_STEP))
    return -ADAM_LR * (m_hat / (jnp.sqrt(v_hat) + ADAM_EPS) + ADAM_WD * w), nm, nv


def _adamw(name, w, g, m, v):
    def body(w_ref, g_ref, m_ref, v_ref, d_ref, nm_ref, nv_ref):
        d_ref[...], nm_ref[...], nv_ref[...] = _adam_update(w_ref[...], g_ref[...], m_ref[...], v_ref[...])

    spec = pl.BlockSpec(w.shape, lambda i: (0, 0))
    out = jax.ShapeDtypeStruct(w.shape, F32)
    return pl.pallas_call(
        body, name=name, grid=(1,),
        in_specs=[spec] * 4, out_specs=[spec] * 3, out_shape=[out] * 3,
        compiler_params=_params(("arbitrary",)),
    )(w, g, m, v)


def _adamw_halves(name, ws, mines, theirs, ms, vs, core):
    n = len(ws)
    cols = ws[0].shape[1]
    assert all(w.shape[1] == cols for w in ws)
    hc = cols // 2
    tc = LANES if n > 1 else min(256, hc)
    nt = hc // tc

    def body(core_ref, *refs):
        ins, outs = refs[:5 * n], refs[5 * n:]
        for a in range(n):
            w_ref, mine_ref, theirs_ref, m_ref, v_ref = [ins[j * n + a] for j in range(5)]
            g_ref, d_ref, nm_ref, nv_ref = outs[4 * a:4 * a + 4]
            gv = jnp.where(pl.program_id(0) == core_ref[0], mine_ref[...], theirs_ref[...])
            g_ref[...] = gv
            d_ref[...], nm_ref[...], nv_ref[...] = _adam_update(w_ref[...], gv, m_ref[...], v_ref[...])

    whole = lambda w: pl.BlockSpec((w.shape[0], tc), lambda h, i, core_ref: (0, h * nt + i))
    mine_spec = lambda w: pl.BlockSpec((w.shape[0], tc), lambda h, i, core_ref: (0, jnp.where(h == core_ref[0], i, 0)))
    theirs_spec = lambda w: pl.BlockSpec((w.shape[0], tc), lambda h, i, core_ref: (0, jnp.where(h == core_ref[0], 0, i)))
    outs = pl.pallas_call(
        body, name=name,
        grid_spec=pltpu.PrefetchScalarGridSpec(
            num_scalar_prefetch=1, grid=(2, nt),
            in_specs=[whole(w) for w in ws] + [mine_spec(w) for w in ws] + [theirs_spec(w) for w in ws]
            + [whole(w) for w in ws] * 2,
            out_specs=[whole(w) for w in ws for _ in range(4)]),
        out_shape=[jax.ShapeDtypeStruct(w.shape, F32) for w in ws for _ in range(4)],
        compiler_params=_params(("arbitrary", "arbitrary")),
    )(core, *ws, *mines, *theirs, *ms, *vs)
    return [outs[4 * a:4 * a + 4] for a in range(n)]


WEIGHTS = ("ffn1_norm", "ffn1_gate", "ffn1_up", "ffn1_down", "mix_norm", "w_in", "b_forget", "conv_w",
           "w_o_attn", "w_o_conv", "w_out", "ffn2_norm", "ffn2_gate", "ffn2_up", "ffn2_down", "final_norm")
VEC_ROWS = 8


def _pack_small(t, conv_rows):
    conv = t["conv_w"]
    parts = [t[n].reshape(VEC_ROWS, LANES) for n in NORMS]
    parts.append(jnp.pad(conv, ((0, conv_rows - conv.shape[0]), (0, 0))))
    parts.append(jnp.pad(t["b_forget"].reshape(1, N_HEADS), ((0, 7), (0, LANES - N_HEADS))))
    return jnp.concatenate(parts, axis=0)


def _unpack_small(p, conv_rows):
    out = {n: p[VEC_ROWS * i:VEC_ROWS * (i + 1)].reshape(-1) for i, n in enumerate(NORMS)}
    base = VEC_ROWS * len(NORMS)
    out["conv_w"] = p[base:base + 3]
    out["b_forget"] = p[base + conv_rows, :N_HEADS]
    return out


def _travel(name, a):
    return a.T if name in TRANSPOSED else a


GATHER_FIRST = ("ffn1_gate", "ffn1_up")
GATHER_RIDES = {"ffn1_up": ("ffn1_down", "w_in"), "ffn1_down": ("w_o_attn", "w_o_conv", "w_out"),
                "attn_fwd": ("ffn2_gate", "ffn2_up", "ffn2_down")}
SIBLING_RIDES = {"ffn2": "mix_out_bwd", "out": None, "w_in": "mix_proj_bwd_dx", "ffn1": None}
CHIP_RIDES = {"ffn2": "attn_bwd", "out": "attn_bwd", "w_in": "ffn1_bwd_dw", "ffn1": None}
SMALL_RIDE = "ffn1_bwd_dw"


class _MeshPlan(_LocalPlan):
    def __init__(self, wts, core):
        self.small, self.core = wts, core
        self.shards = {n: wts[n].astype(BF16) for n in BIG}
        self.chip_part, self.from_chips, self.rides = {}, {}, {}
        conv_shard = jnp.pad(wts["conv_w"], ((0, 8 - wts["conv_w"].shape[0]), (0, 0)))
        own = [self.shards[n] for n in GATHER_FIRST] + [conv_shard]
        got = _run_comm("gather_first", _gather_comm(own[:-1], conv_shard))
        self.stacks = dict(zip(GATHER_FIRST + ("conv_w",), _fill_own(got, own)))
        for kernel_name, names in GATHER_RIDES.items():
            mine = [self.shards[n] for n in names]
            self._ride(kernel_name, _gather_comm(mine),
                       lambda got, names=names, mine=mine: self.stacks.update(zip(names, _fill_own(got, mine))))

    def _ride(self, kernel_name, comm, then):
        self.rides.setdefault(kernel_name, []).append((comm, then))

    def rider(self, kernel_name):
        comms = [comm for comm, _ in self.rides.get(kernel_name, [])]
        return _merge_comms(comms) if comms else None

    def arrived(self, kernel_name, results):
        for comm, then in self.rides.pop(kernel_name, []):
            then(results[:len(comm.out_shape)])
            results = results[len(comm.out_shape):]

    def reduce(self, group, grads):
        names = tuple(grads)
        mine = [grads[n] for n in names]

        def with_sibling(from_sibling):
            parts = _add_halves("add_halves_" + group, mine, list(from_sibling), self.core)
            self.chip_part.update(zip(names, parts))
            if CHIP_RIDES[group] is None:
                self.last = (names, _exchange_start("exchange_start_" + group, parts))
            else:
                self._ride(CHIP_RIDES[group], _chip_exchange_comm(parts),
                           lambda got: self.from_chips.update(zip(names, got)))

        if SIBLING_RIDES[group] is None:
            with_sibling(_run_comm("sibling_exchange_" + group, _sibling_exchange_comm(mine)))
        else:
            self._ride(SIBLING_RIDES[group], _sibling_exchange_comm(mine), with_sibling)

    def reduce_small(self, gs, loss):
        conv_all = _shard_cols(gs["conv_w"]).reshape(N_CHIPS * 8, LANES)
        part = _pack_small({**{n: gs[n] for n in NORMS}, "conv_w": conv_all, "b_forget": gs["b_forget"][0, :N_HEADS]},
                           N_CHIPS * 8)
        part = jnp.concatenate([part, jnp.broadcast_to(loss, (8, LANES))], axis=0)
        me = 4 * lax.axis_index("x") + 2 * lax.axis_index("y") + lax.axis_index("c")

        def landed(got):
            self.small_parts = lax.dynamic_update_index_in_dim(got[0], part, me, 0)

        self._ride(SMALL_RIDE, _small_gather_comm(part), landed)


def kernel(x, ffn1_norm, ffn1_gate, ffn1_up, ffn1_down, mix_norm, w_in, b_forget, conv_w, w_o_attn, w_o_conv, w_out, ffn2_norm, ffn2_gate, ffn2_up, ffn2_down, final_norm, loss_target, m_ffn1_norm, m_ffn1_gate, m_ffn1_up, m_ffn1_down, m_mix_norm, m_w_in, m_b_forget, m_conv_w, m_w_o_attn, m_w_o_conv, m_w_out, m_ffn2_norm, m_ffn2_gate, m_ffn2_up, m_ffn2_down, m_final_norm, v_ffn1_norm, v_ffn1_gate, v_ffn1_up, v_ffn1_down, v_mix_norm, v_w_in, v_b_forget, v_conv_w, v_w_o_attn, v_w_o_conv, v_w_out, v_ffn2_norm, v_ffn2_gate, v_ffn2_up, v_ffn2_down, v_final_norm):
    given = dict(locals())
    wts = {n: _travel(n, given[n]) for n in WEIGHTS}
    mom = {n: _travel(n, given["m_" + n]) for n in WEIGHTS}
    var = {n: _travel(n, given["v_" + n]) for n in WEIGHTS}
    B, S, D = x.shape
    chip = 2 * lax.axis_index("x") + lax.axis_index("y")
    chip1 = chip.astype(jnp.int32).reshape(1)
    core = lax.axis_index("c").astype(jnp.int32).reshape(1)

    plan = _MeshPlan(wts, core)
    loss, grad_x, gs = _local_step(x.reshape(B * S, D), loss_target.reshape(B * S, D), plan, B, S)

    last_names, (send_sems, recv_sems, parts_thru, lands, token) = plan.last
    delta, new_m, new_v, grads = {}, {}, {}, {}

    def finish(tag, names):
        by_cols = {}
        for n in names:
            by_cols.setdefault(wts[n].shape[1], []).append(n)
        mine = {}
        for cols, ns in by_cols.items():
            mine.update(zip(ns, _sum_chips("sum_chips_%s_%d" % (tag, cols), [plan.chip_part[n] for n in ns],
                                           [plan.from_chips[n] for n in ns], chip1, token)))
        theirs = dict(zip(names, _share_halves("share_halves_" + tag, [mine[n] for n in names])))
        raw = []
        for cols, ns in by_cols.items():
            outs = _adamw_halves("adamw_%s_%d" % (tag, cols), [wts[n] for n in ns], [mine[n] for n in ns],
                                 [theirs[n] for n in ns], [mom[n] for n in ns], [var[n] for n in ns], core)
            for n, per in zip(ns, outs):
                raw.append(per[-1])
                grads[n], delta[n], new_m[n], new_v[n] = [_travel(n, o) for o in per]
        return raw

    small_sum = _sum_devices(plan.small_parts)
    base = VEC_ROWS * len(NORMS)
    loss_row = small_sum.shape[0] - 8
    small_grads = _unpack_small(small_sum, N_CHIPS * 8)
    small_grads["conv_w"] = lax.dynamic_slice_in_dim(small_sum[base:base + N_CHIPS * 8], chip * 8, 8, axis=0)[:3]
    packs = [_pack_small(t, 8) for t in (wts, small_grads, mom, var)]
    small_out = _adamw("adamw_small", *packs)

    done = finish("early", [n for n in BIG if n not in last_names])
    parts_back, got = _exchange_wait("exchange_wait", send_sems, recv_sems, parts_thru, lands, done + list(small_out))
    plan.chip_part.update(zip(last_names, parts_back))
    plan.from_chips.update(zip(last_names, got))
    finish("last", last_names)
    grads.update(small_grads)
    for out, p in zip((delta, new_m, new_v), small_out):
        out.update(_unpack_small(p, 8))

    return (small_sum[loss_row, 0], grad_x.reshape(B, S, D), *[grads[n] for n in WEIGHTS], *[delta[n] for n in WEIGHTS],
            *[new_m[n] for n in WEIGHTS], *[new_v[n] for n in WEIGHTS])
```

```python
import functools
import math

import jax
import jax.numpy as jnp
from jax import lax
from jax.experimental import pallas as pl
from jax.experimental.pallas import tpu as pltpu

F32 = jnp.float32
BF16 = jnp.bfloat16
MESH = pl.DeviceIdType.MESH

N_CHIPS = 4
N_DEV = 8
N_HEADS = 8
HEAD_DIM = 64
HEAD_PAIRS = N_HEADS // 2
ATTN_W = N_HEADS * HEAD_DIM
CONV_W = 512
RMS_EPS = 1e-6
FFN_RES = 0.5
LANES = 128
VMEM_LIMIT = 56 * 1024 * 1024
ROW_BLOCK = 256

ADAM_LR = 0.001
ADAM_B1 = 0.9
ADAM_B2 = 0.999
ADAM_EPS = 1e-08
ADAM_WD = 0.01
ADAM_STEP = 10

PROJ_W = 3 * ATTN_W + 3 * CONV_W + 2 * 1024
COL_CB, COL_CC, COL_CX = 3 * ATTN_W, 3 * ATTN_W + CONV_W, 3 * ATTN_W + 2 * CONV_W
COL_GATES = 3 * ATTN_W + 3 * CONV_W
N_FORGET_COL = 3 * ATTN_W


def _params(sem=None, vmem=VMEM_LIMIT):
    return pltpu.CompilerParams(dimension_semantics=sem, vmem_limit_bytes=vmem)


def _dot(a, b):
    return lax.dot_general(a, b, (((1,), (0,)), ((), ())), preferred_element_type=F32)


def _dot_nt(a, b):
    return lax.dot_general(a, b, (((1,), (1,)), ((), ())), preferred_element_type=F32)


def _dot_tn(a, b):
    return lax.dot_general(a, b, (((0,), (0,)), ((), ())), preferred_element_type=F32)


def _sigmoid(x):
    return 1.0 / (1.0 + jnp.exp(-x))


def _rms(xv):
    inv = lax.rsqrt(jnp.mean(xv * xv, axis=-1, keepdims=True) + RMS_EPS)
    return xv * inv, inv


class _Comm:
    def __init__(self, inputs, out_shape, scratch, start, finish):
        self.inputs, self.out_shape, self.scratch = list(inputs), list(out_shape), list(scratch)
        self.start, self.finish = start, finish


def _pallas(body, name, grid, in_specs, out_specs, out_shape, scratch, args, comm=None):
    sem = ("arbitrary",) * len(grid)
    if comm is None:
        outs = pl.pallas_call(body, name=name, grid=grid, in_specs=in_specs, out_specs=out_specs,
                              out_shape=out_shape, scratch_shapes=scratch, compiler_params=_params(sem))(*args)
        return list(outs), []
    n_in, n_out, n_scr = len(in_specs), len(out_specs), len(scratch)
    ci, co = len(comm.inputs), len(comm.out_shape)

    def riding(*refs):
        ins, refs = refs[:n_in], refs[n_in:]
        cins, refs = refs[:ci], refs[ci:]
        outs, refs = refs[:n_out], refs[n_out:]
        couts, refs = refs[:co], refs[co:]
        scr, sems = refs[:n_scr], refs[n_scr:]
        ids = [pl.program_id(d) for d in range(len(grid))]
        first = functools.reduce(lambda a, b: a & b, [i == 0 for i in ids])
        last = functools.reduce(lambda a, b: a & b, [i == g - 1 for i, g in zip(ids, grid)])

        @pl.when(first)
        def _():
            comm.start(cins, couts, sems)

        body(*ins, *outs, *scr)

        @pl.when(last)
        def _():
            comm.finish(cins, couts, sems)

    any_spec = pl.BlockSpec(memory_space=pl.ANY)
    outs = pl.pallas_call(
        riding, name=name, grid=grid,
        in_specs=list(in_specs) + [any_spec] * ci, out_specs=list(out_specs) + [any_spec] * co,
        out_shape=list(out_shape) + comm.out_shape, scratch_shapes=list(scratch) + comm.scratch,
        compiler_params=_params(sem))(*args, *comm.inputs)
    return list(outs[:n_out]), list(outs[n_out:])


def _rms_bwd(dn, xhat, inv, g):
    dxhat = dn * g
    dx = inv * (dxhat - xhat * jnp.mean(dxhat * xhat, axis=-1, keepdims=True))
    return dx, jnp.sum(dn * xhat, axis=0, keepdims=True)


def _ffn_fwd(name, x, g, wgt, wut, wd, tm, comm=None):
    T, D = x.shape
    K, Fs, _ = wgt.shape

    def body(x_ref, g_ref, wg_ref, wu_ref, wd_ref, out_ref, hg_ref, hu_ref, n_ref, acc_scr):
        k = pl.program_id(1)

        @pl.when(k == 0)
        def _():
            xhat, _ = _rms(x_ref[...])
            n_ref[...] = (xhat * g_ref[...]).astype(BF16)
            acc_scr[...] = jnp.zeros_like(acc_scr)

        n = n_ref[...]
        hg = _dot_nt(n, wg_ref[...])
        hu = _dot_nt(n, wu_ref[...])
        hg_ref[...] = hg.astype(BF16)
        hu_ref[...] = hu.astype(BF16)
        act = (hg * _sigmoid(hg) * hu).astype(BF16)
        acc_scr[...] += _dot(act, wd_ref[...])

        @pl.when(k == K - 1)
        def _():
            out_ref[...] = x_ref[...] + FFN_RES * acc_scr[...]

    w_spec = pl.BlockSpec((None, Fs, D), lambda i, k: (k, 0, 0))
    act_spec = pl.BlockSpec((None, tm, Fs), lambda i, k: (k, i, 0))
    return _pallas(
        body, name, (T // tm, K),
        [pl.BlockSpec((tm, D), lambda i, k: (i, 0)), pl.BlockSpec((1, D), lambda i, k: (0, 0)),
         w_spec, w_spec, w_spec],
        [pl.BlockSpec((tm, D), lambda i, k: (i, 0)), act_spec, act_spec, pl.BlockSpec((tm, D), lambda i, k: (i, 0))],
        [jax.ShapeDtypeStruct((T, D), F32), jax.ShapeDtypeStruct((K, T, Fs), BF16),
         jax.ShapeDtypeStruct((K, T, Fs), BF16), jax.ShapeDtypeStruct((T, D), BF16)],
        [pltpu.VMEM((tm, D), F32)],
        (x, g, wgt, wut, wd), comm)


def _ffn_up(name, x, g, wgt, wut, tm, comm=None):
    T, D = x.shape
    K, Fs, _ = wgt.shape

    def body(x_ref, g_ref, wg_ref, wu_ref, hg_ref, hu_ref, n_ref):
        @pl.when(pl.program_id(1) == 0)
        def _():
            xhat, _ = _rms(x_ref[...])
            n_ref[...] = (xhat * g_ref[...]).astype(BF16)

        n = n_ref[...]
        hg_ref[...] = _dot_nt(n, wg_ref[...]).astype(BF16)
        hu_ref[...] = _dot_nt(n, wu_ref[...]).astype(BF16)

    w_spec = pl.BlockSpec((None, Fs, D), lambda i, k: (k, 0, 0))
    act_spec = pl.BlockSpec((None, tm, Fs), lambda i, k: (k, i, 0))
    return _pallas(
        body, name, (T // tm, K),
        [pl.BlockSpec((tm, D), lambda i, k: (i, 0)), pl.BlockSpec((1, D), lambda i, k: (0, 0)), w_spec, w_spec],
        [act_spec, act_spec, pl.BlockSpec((tm, D), lambda i, k: (i, 0))],
        [jax.ShapeDtypeStruct((K, T, Fs), BF16), jax.ShapeDtypeStruct((K, T, Fs), BF16),
         jax.ShapeDtypeStruct((T, D), BF16)],
        [],
        (x, g, wgt, wut), comm)


def _ffn_down(name, x, hg, hu, wd, tm, comm=None):
    T, D = x.shape
    K, Fs, _ = wd.shape

    def body(x_ref, hg_ref, hu_ref, wd_ref, out_ref, acc_scr):
        k = pl.program_id(1)

        @pl.when(k == 0)
        def _():
            acc_scr[...] = jnp.zeros_like(acc_scr)

        hgv = hg_ref[...].astype(F32)
        act = (hgv * _sigmoid(hgv) * hu_ref[...].astype(F32)).astype(BF16)
        acc_scr[...] += _dot(act, wd_ref[...])

        @pl.when(k == K - 1)
        def _():
            out_ref[...] = x_ref[...] + FFN_RES * acc_scr[...]

    act_spec = pl.BlockSpec((None, tm, Fs), lambda i, k: (k, i, 0))
    row = pl.BlockSpec((tm, D), lambda i, k: (i, 0))
    return _pallas(
        body, name, (T // tm, K),
        [row, act_spec, act_spec, pl.BlockSpec((None, Fs, D), lambda i, k: (k, 0, 0))],
        [row], [jax.ShapeDtypeStruct((T, D), F32)], [pltpu.VMEM((tm, D), F32)],
        (x, hg, hu, wd), comm)


def _ffn_bwd_dx(name, dout, x, g, hg, hu, wgt, wut, wd, tm, comm=None):
    T, D = x.shape
    K, Fs, _ = wgt.shape

    def body(dout_ref, x_ref, g_ref, hg_ref, hu_ref, wg_ref, wu_ref, wd_ref,
             dx_ref, dhg_ref, dhu_ref, dg_ref, df_ref, dn_scr):
        i, k = pl.program_id(0), pl.program_id(1)

        @pl.when(k == 0)
        def _():
            df_ref[...] = (FFN_RES * dout_ref[...]).astype(BF16)
            dn_scr[...] = jnp.zeros_like(dn_scr)

        @pl.when((k == 0) & (i == 0))
        def _():
            dg_ref[...] = jnp.zeros_like(dg_ref)

        for r0 in range(0, tm, ROW_BLOCK):
            rows = slice(r0, r0 + ROW_BLOCK)
            dact = _dot_nt(df_ref[rows, :], wd_ref[...])
            hgv = hg_ref[rows, :].astype(F32)
            huv = hu_ref[rows, :].astype(F32)
            s = _sigmoid(hgv)
            dhu = (dact * (hgv * s)).astype(BF16)
            dhg = (dact * huv * (s * (1.0 + hgv * (1.0 - s)))).astype(BF16)
            dhg_ref[rows, :] = dhg
            dhu_ref[rows, :] = dhu
            dn_scr[rows, :] += _dot(dhg, wg_ref[...]) + _dot(dhu, wu_ref[...])

        @pl.when(k == K - 1)
        def _():
            xhat, inv = _rms(x_ref[...])
            dx, dg = _rms_bwd(dn_scr[...], xhat, inv, g_ref[...])
            dx_ref[...] = dout_ref[...] + dx
            dg_ref[...] += dg

    w_spec = pl.BlockSpec((None, Fs, D), lambda i, k: (k, 0, 0))
    act_spec = pl.BlockSpec((None, tm, Fs), lambda i, k: (k, i, 0))
    row = pl.BlockSpec((tm, D), lambda i, k: (i, 0))
    vec = pl.BlockSpec((1, D), lambda i, k: (0, 0))
    return _pallas(
        body, name, (T // tm, K),
        [row, row, vec, act_spec, act_spec, w_spec, w_spec, w_spec],
        [row, act_spec, act_spec, vec, row],
        [jax.ShapeDtypeStruct((T, D), F32), jax.ShapeDtypeStruct((K, T, Fs), BF16),
         jax.ShapeDtypeStruct((K, T, Fs), BF16), jax.ShapeDtypeStruct((1, D), F32),
         jax.ShapeDtypeStruct((T, D), BF16)],
        [pltpu.VMEM((tm, D), F32)],
        (dout, x, g, hg, hu, wgt, wut, wd), comm)


def _ffn_bwd_dw(name, n, df, hg, hu, dhg, dhu, tk, comm=None):
    T, D = n.shape
    K, _, Fs = hg.shape
    nt = T // tk

    def body(n_ref, df_ref, hg_ref, hu_ref, dhg_ref, dhu_ref, dwg_ref, dwu_ref, dwd_ref, accg, accu, accd):
        t = pl.program_id(1)

        @pl.when(t == 0)
        def _():
            accg[...] = jnp.zeros_like(accg)
            accu[...] = jnp.zeros_like(accu)
            accd[...] = jnp.zeros_like(accd)

        nv = n_ref[...]
        hgv = hg_ref[...].astype(F32)
        act = (hgv * _sigmoid(hgv) * hu_ref[...].astype(F32)).astype(BF16)
        accg[...] += _dot_tn(dhg_ref[...], nv)
        accu[...] += _dot_tn(dhu_ref[...], nv)
        accd[...] += _dot_tn(act, df_ref[...])

        @pl.when(t == nt - 1)
        def _():
            dwg_ref[...] = accg[...].astype(BF16)
            dwu_ref[...] = accu[...].astype(BF16)
            dwd_ref[...] = accd[...].astype(BF16)

    act_spec = pl.BlockSpec((None, tk, Fs), lambda k, t: (k, t, 0))
    w_spec = pl.BlockSpec((None, Fs, D), lambda k, t: (k, 0, 0))
    row = pl.BlockSpec((tk, D), lambda k, t: (t, 0))
    return _pallas(
        body, name, (K, nt),
        [row, row, act_spec, act_spec, act_spec, act_spec],
        [w_spec, w_spec, w_spec],
        [jax.ShapeDtypeStruct((K, Fs, D), BF16)] * 3,
        [pltpu.VMEM((Fs, D), F32)] * 3,
        (n, df, hg, hu, dhg, dhu), comm)


def _mix_proj_fwd(x, g, wproj_t, wf_t, tm, tn):
    T, D = x.shape
    N = wproj_t.shape[0]

    def body(x_ref, g_ref, w_ref, wf_ref, h_ref, proj_ref, flog_ref, h_scr):
        @pl.when(pl.program_id(1) == 0)
        def _():
            xhat, _ = _rms(x_ref[...])
            h = (xhat * g_ref[...]).astype(BF16)
            h_scr[...] = h
            h_ref[...] = h
            flog_ref[...] = _dot_nt(h, wf_ref[...])

        proj_ref[...] = _dot_nt(h_scr[...], w_ref[...]).astype(BF16)

    return pl.pallas_call(
        body, name="mix_proj_fwd", grid=(T // tm, N // tn),
        in_specs=[pl.BlockSpec((tm, D), lambda i, n: (i, 0)),
                  pl.BlockSpec((1, D), lambda i, n: (0, 0)),
                  pl.BlockSpec((tn, D), lambda i, n: (n, 0)),
                  pl.BlockSpec((LANES, D), lambda i, n: (0, 0))],
        out_specs=[pl.BlockSpec((tm, D), lambda i, n: (i, 0)),
                   pl.BlockSpec((tm, tn), lambda i, n: (i, n)),
                   pl.BlockSpec((tm, LANES), lambda i, n: (i, 0))],
        out_shape=[jax.ShapeDtypeStruct((T, D), BF16),
                   jax.ShapeDtypeStruct((T, N), BF16),
                   jax.ShapeDtypeStruct((T, LANES), F32)],
        scratch_shapes=[pltpu.VMEM((tm, D), BF16)],
        compiler_params=_params(("arbitrary", "arbitrary")),
    )(x, g, wproj_t, wf_t)


def _log_sigmoid(z):
    return -(jnp.maximum(-z, 0.0) + jnp.log(1.0 + jnp.exp(-jnp.abs(z))))


def _tri(n, lower):
    r = lax.broadcasted_iota(jnp.int32, (n, n), 0)
    c = lax.broadcasted_iota(jnp.int32, (n, n), 1)
    return jnp.where((r >= c) if lower else (r <= c), 1.0, 0.0).astype(F32)


def _dot_f32(a, b):
    return lax.dot_general(a, b, (((1,), (0,)), ((), ())), preferred_element_type=F32,
                           precision=lax.Precision.HIGHEST)


def _fgate_fwd(flog, bias, B, S, ch):
    def body(flog_ref, b_ref, cum_ref):
        tri = _tri(ch, True)
        carry = jnp.zeros((1, LANES), F32)
        for c0 in range(0, S, ch):
            lf = _log_sigmoid(flog_ref[c0:c0 + ch, :] + b_ref[...])
            cs = _dot_f32(tri, lf) + carry
            cum_ref[c0:c0 + ch, :] = cs
            carry = cs[ch - 1:ch, :]

    return pl.pallas_call(
        body, name="fgate_fwd", grid=(B,),
        in_specs=[pl.BlockSpec((S, LANES), lambda b: (b, 0)),
                  pl.BlockSpec((1, LANES), lambda b: (0, 0))],
        out_specs=pl.BlockSpec((S, LANES), lambda b: (b, 0)),
        out_shape=jax.ShapeDtypeStruct((B * S, LANES), F32),
        compiler_params=_params(("arbitrary",)),
    )(flog, bias)


def _fgate_bwd(dcum, flog, bias, B, S, ch):
    def body(dcum_ref, flog_ref, b_ref, dflog_ref, db_ref):
        @pl.when(pl.program_id(0) == 0)
        def _():
            db_ref[...] = jnp.zeros_like(db_ref)

        tri = _tri(ch, False)
        carry = jnp.zeros((1, LANES), F32)
        db = jnp.zeros((1, LANES), F32)
        for c0 in range(S - ch, -1, -ch):
            dlf = _dot_f32(tri, dcum_ref[c0:c0 + ch, :]) + carry
            carry = dlf[0:1, :]
            z = flog_ref[c0:c0 + ch, :] + b_ref[...]
            dz = dlf * _sigmoid(-z)
            dflog_ref[c0:c0 + ch, :] = dz
            db = db + jnp.sum(dz, axis=0, keepdims=True)
        db_ref[...] += db

    return pl.pallas_call(
        body, name="fgate_bwd", grid=(B,),
        in_specs=[pl.BlockSpec((S, LANES), lambda b: (b, 0)),
                  pl.BlockSpec((S, LANES), lambda b: (b, 0)),
                  pl.BlockSpec((1, LANES), lambda b: (0, 0))],
        out_specs=[pl.BlockSpec((S, LANES), lambda b: (b, 0)),
                   pl.BlockSpec((1, LANES), lambda b: (0, 0))],
        out_shape=[jax.ShapeDtypeStruct((B * S, LANES), F32),
                   jax.ShapeDtypeStruct((1, LANES), F32)],
        compiler_params=_params(("arbitrary",)),
    )(dcum, flog, bias)


def _pick_lane(tile, h):
    lane = lax.broadcasted_iota(jnp.int32, tile.shape, 1)
    return jnp.sum(jnp.where(lane == h, tile, 0.0), axis=1, keepdims=True)


def _put_lane(col, h, width=LANES):
    lane = lax.broadcasted_iota(jnp.int32, (col.shape[0], width), 1)
    return jnp.where(lane == h, col, 0.0)


def _pick_row(tile, h):
    row = lax.broadcasted_iota(jnp.int32, tile.shape, 0)
    return jnp.sum(jnp.where(row == h, tile, 0.0), axis=0, keepdims=True)


def _put_row(vec, h):
    row = lax.broadcasted_iota(jnp.int32, (8, vec.shape[1]), 0)
    return jnp.where(row == h, vec, 0.0)


def _causal(tq):
    r = lax.broadcasted_iota(jnp.int32, (tq, tq), 0)
    c = lax.broadcasted_iota(jnp.int32, (tq, tq), 1)
    return r >= c


def _head_halves(t):
    lo = lax.broadcasted_iota(jnp.int32, t.shape, 1) < HEAD_DIM
    zero = jnp.zeros_like(t)
    return jnp.where(lo, t, zero), jnp.where(lo, zero, t)


NEG = -1e30
ATTN_SCALE = 1.0 / math.sqrt(HEAD_DIM)


def _scaled(q):
    return (q.astype(F32) * ATTN_SCALE).astype(q.dtype)


def _attn_fwd(proj, cum, cum_t, B, S, tq, comm=None):
    nq = S // tq

    def body(q_ref, k_ref, v_ref, cum_ref, cumt_ref, o_ref, lse_ref):
        qi, hp = pl.program_id(1), pl.program_id(2)
        qm = _head_halves(_scaled(q_ref[...]))
        cumv = cum_ref[...]
        cq = [_pick_lane(cumv, 2 * hp + e) for e in range(2)]

        def tile(j, carry, masked):
            off = pl.multiple_of(j * tq, tq)
            kj = k_ref[pl.ds(off, tq), :]
            vj = v_ref[pl.ds(off, tq), :]
            ct = cumt_ref[j]
            new = []
            for e in range(2):
                m, l, acc = carry[e]
                s = _dot_nt(qm[e], kj) - _pick_row(ct, 2 * hp + e)
                if masked:
                    s = jnp.where(_causal(tq), s, NEG)
                m_new = jnp.maximum(m, jnp.max(s, axis=1, keepdims=True))
                p = jnp.exp(s - m_new)
                alpha = jnp.exp(m - m_new)
                l = alpha * l + jnp.sum(p, axis=1, keepdims=True)
                acc = alpha * acc + _dot(p.astype(BF16), vj)
                new.append((m_new, l, acc))
            return tuple(new)

        one = (jnp.full((tq, 1), NEG, F32), jnp.zeros((tq, 1), F32), jnp.zeros((tq, LANES), F32))
        carry = lax.fori_loop(0, qi, lambda j, c: tile(j, c, False), (one, one))
        (ma, la, acca), (mb, lb, accb) = tile(qi, carry, True)
        lo = lax.broadcasted_iota(jnp.int32, (tq, LANES), 1) < HEAD_DIM
        o_ref[...] = jnp.where(lo, acca / la, accb / lb).astype(BF16)

        @pl.when(hp == 0)
        def _():
            lse_ref[...] = jnp.zeros_like(lse_ref)

        lse_ref[...] += (_put_lane(ma + jnp.log(la) + cq[0], 2 * hp) + _put_lane(mb + jnp.log(lb) + cq[1], 2 * hp + 1))

    kv = lambda first: pl.BlockSpec((S, LANES), lambda b, i, hp: (b, first + hp))
    return _pallas(
        body, "attn_fwd", (B, nq, HEAD_PAIRS),
        [pl.BlockSpec((tq, LANES), lambda b, i, hp: (b * nq + i, hp)),
         kv(ATTN_W // LANES), kv(2 * ATTN_W // LANES),
         pl.BlockSpec((tq, LANES), lambda b, i, hp: (b * nq + i, 0)),
         pl.BlockSpec((None, nq, 8, tq), lambda b, i, hp: (b, 0, 0, 0))],
        [pl.BlockSpec((tq, LANES), lambda b, i, hp: (b * nq + i, hp)),
         pl.BlockSpec((tq, LANES), lambda b, i, hp: (b * nq + i, 0))],
        [jax.ShapeDtypeStruct((B * S, ATTN_W), BF16), jax.ShapeDtypeStruct((B * S, LANES), F32)],
        [], (proj, proj, proj, cum, cum_t), comm)


def _attn_bwd(proj, o, do, lse, cum, cum_t, B, S, tq, comm=None):
    nq = S // tq

    def body(q_ref, k_ref, v_ref, o_ref, do_ref, lse_ref, cum_ref, cumt_ref,
             dq_ref, dk_ref, dv_ref, dcq_ref, dck_ref, dq_scr):
        hp, kj = pl.program_id(1), pl.program_id(2)

        @pl.when(kj == 0)
        def _():
            dq_scr[...] = jnp.zeros_like(dq_scr)

        @pl.when((kj == 0) & (hp == 0))
        def _():
            dcq_ref[...] = jnp.zeros_like(dcq_ref)
            dck_ref[...] = jnp.zeros_like(dck_ref)

        kv = k_ref[...]
        vv = v_ref[...]
        km = _head_halves(kv)
        ct = cumt_ref[...]
        ck = [_pick_row(ct, 2 * hp + e) for e in range(2)]

        def tile(i, carry, masked):
            dk, dv, dcol = carry
            off = pl.multiple_of(i * tq, tq)
            qi = q_ref[pl.ds(off, tq), :]
            ov = o_ref[pl.ds(off, tq), :].astype(F32)
            qm = _head_halves(_scaled(qi))
            dom = _head_halves(do_ref[pl.ds(off, tq), :])
            cumv = cum_ref[pl.ds(off, tq), :]
            lsev = lse_ref[pl.ds(off, tq), :]
            dcq = jnp.zeros((tq, LANES), F32)
            dq = jnp.zeros((tq, LANES), F32)
            dcol_new = []
            for e in range(2):
                delta = jnp.sum(dom[e].astype(F32) * ov, axis=1, keepdims=True)
                row_term = _pick_lane(cumv, 2 * hp + e) - _pick_lane(lsev, 2 * hp + e)
                p = jnp.exp(_dot_nt(qm[e], kv) + row_term - ck[e])
                if masked:
                    p = jnp.where(_causal(tq), p, 0.0)
                dv = dv + _dot_tn(dom[e], p.astype(BF16))
                ds = p * (_dot_nt(dom[e], vv) - delta)
                dcol_new.append(dcol[e] + jnp.sum(ds, axis=0, keepdims=True))
                dcq = dcq + _put_lane(jnp.sum(ds, axis=1, keepdims=True), 2 * hp + e)
                dsb = ds.astype(BF16)
                dk = dk + _dot_tn(qm[e], dsb)
                dq = dq + _dot(dsb, km[e]) * ATTN_SCALE
            dq_scr[pl.ds(off, tq), :] += dq
            dcq_ref[pl.ds(off, tq), :] += dcq
            return dk, dv, tuple(dcol_new)

        zero_row = jnp.zeros((1, tq), F32)
        init = (jnp.zeros((LANES, tq), F32), jnp.zeros((LANES, tq), F32), (zero_row, zero_row))
        carry = tile(kj, init, True)
        dk, dv, dcol = lax.fori_loop(kj + 1, nq, lambda i, c: tile(i, c, False), carry)
        dk_ref[...] = dk.T.astype(BF16)
        dv_ref[...] = dv.T.astype(BF16)
        dck_ref[kj] += -(_put_row(dcol[0], 2 * hp) + _put_row(dcol[1], 2 * hp + 1))

        @pl.when(kj == nq - 1)
        def _():
            dq_ref[...] = dq_scr[...].astype(BF16)

    seq = lambda first: pl.BlockSpec((S, LANES), lambda b, hp, j: (b, first + hp))
    tile_in = lambda first: pl.BlockSpec((tq, LANES), lambda b, hp, j: (b * nq + j, first + hp))
    lanes0 = pl.BlockSpec((S, LANES), lambda b, hp, j: (b, 0))
    out = jax.ShapeDtypeStruct((B * S, ATTN_W), BF16)
    return _pallas(
        body, "attn_bwd", (B, HEAD_PAIRS, nq),
        [seq(0), tile_in(ATTN_W // LANES), tile_in(2 * ATTN_W // LANES), seq(0), seq(0), lanes0, lanes0,
         pl.BlockSpec((None, None, 8, tq), lambda b, hp, j: (b, j, 0, 0))],
        [seq(0), tile_in(0), tile_in(0), lanes0,
         pl.BlockSpec((None, nq, 8, tq), lambda b, hp, j: (b, 0, 0, 0))],
        [out, out, out, jax.ShapeDtypeStruct((B * S, LANES), F32), jax.ShapeDtypeStruct((B, nq, 8, tq), F32)],
        [pltpu.VMEM((S, LANES), F32)],
        (proj, proj, proj, o, do, lse, cum, cum_t), comm)


def _shift_down(u, n):
    row = lax.broadcasted_iota(jnp.int32, u.shape, 0)
    return jnp.where(row >= n, pltpu.roll(u, n, 0), 0.0)


def _shift_up(u, n):
    rows = u.shape[0]
    row = lax.broadcasted_iota(jnp.int32, u.shape, 0)
    return jnp.where(row < rows - n, pltpu.roll(u, rows - n, 0), 0.0)


def _conv_specs(S):
    cb = pl.BlockSpec((S, LANES), lambda g, b: (b, COL_CB // LANES + g))
    cc = pl.BlockSpec((S, LANES), lambda g, b: (b, COL_CC // LANES + g))
    cx = pl.BlockSpec((S, LANES), lambda g, b: (b, COL_CX // LANES + g))
    w = pl.BlockSpec((8, LANES), lambda g, b: (0, g))
    return cb, cc, cx, w


def _conv_fwd(proj, conv_w, B, S):
    def body(cb_ref, cc_ref, cx_ref, w_ref, y_ref):
        u = cc_ref[...].astype(F32) * cx_ref[...].astype(F32)
        w = w_ref[...]
        conv = w[0:1, :] * _shift_down(u, 2) + w[1:2, :] * _shift_down(u, 1) + w[2:3, :] * u
        y_ref[...] = (cb_ref[...].astype(F32) * conv).astype(BF16)

    cb, cc, cx, w = _conv_specs(S)
    return pl.pallas_call(
        body, name="conv_fwd", grid=(CONV_W // LANES, B),
        in_specs=[cb, cc, cx, w],
        out_specs=pl.BlockSpec((S, LANES), lambda g, b: (b, g)),
        out_shape=jax.ShapeDtypeStruct((B * S, CONV_W), BF16),
        compiler_params=_params(("arbitrary", "arbitrary")),
    )(proj, proj, proj, conv_w)


def _conv_bwd(dy, proj, conv_w, B, S):
    def body(dy_ref, cb_ref, cc_ref, cx_ref, w_ref, dcb_ref, dcc_ref, dcx_ref, dw_ref):
        @pl.when(pl.program_id(1) == 0)
        def _():
            dw_ref[...] = jnp.zeros_like(dw_ref)

        ccv = cc_ref[...].astype(F32)
        cxv = cx_ref[...].astype(F32)
        u = ccv * cxv
        u1 = _shift_down(u, 1)
        u2 = _shift_down(u, 2)
        w = w_ref[...]
        conv = w[0:1, :] * u2 + w[1:2, :] * u1 + w[2:3, :] * u
        dyv = dy_ref[...].astype(F32)
        dcb_ref[...] = (dyv * conv).astype(BF16)
        dconv = dyv * cb_ref[...].astype(F32)
        du = w[2:3, :] * dconv + w[1:2, :] * _shift_up(dconv, 1) + w[0:1, :] * _shift_up(dconv, 2)
        dcc_ref[...] = (du * cxv).astype(BF16)
        dcx_ref[...] = (du * ccv).astype(BF16)
        row = lax.broadcasted_iota(jnp.int32, (8, LANES), 0)
        dw = jnp.where(row == 0, jnp.sum(dconv * u2, axis=0, keepdims=True),
                       jnp.where(row == 1, jnp.sum(dconv * u1, axis=0, keepdims=True),
                                 jnp.where(row == 2, jnp.sum(dconv * u, axis=0, keepdims=True), 0.0)))
        dw_ref[...] += dw

    cb, cc, cx, w = _conv_specs(S)
    out = pl.BlockSpec((S, LANES), lambda g, b: (b, g))
    return pl.pallas_call(
        body, name="conv_bwd", grid=(CONV_W // LANES, B),
        in_specs=[out, cb, cc, cx, w],
        out_specs=[out, out, out, w],
        out_shape=[jax.ShapeDtypeStruct((B * S, CONV_W), BF16)] * 3 + [jax.ShapeDtypeStruct((8, CONV_W), F32)],
        compiler_params=_params(("arbitrary", "arbitrary")),
    )(dy, proj, proj, proj, conv_w)


def _gate_specs(tm, D):
    ga = pl.BlockSpec((tm, D), lambda i: (i, COL_GATES // D))
    gc = pl.BlockSpec((tm, D), lambda i: (i, COL_GATES // D + 1))
    return ga, gc


def _mix_out_fwd(x, o, yc, proj, woa, woc, wout, tm):
    T, D = x.shape

    def body(x_ref, o_ref, yc_ref, ga_ref, gc_ref, woa_ref, woc_ref, wout_ref, out_ref):
        ya = _dot(o_ref[...], woa_ref[...])
        yp = _dot(yc_ref[...], woc_ref[...])
        merged = _sigmoid(ga_ref[...].astype(F32)) * ya + _sigmoid(gc_ref[...].astype(F32)) * yp
        out_ref[...] = x_ref[...] + _dot(merged.astype(BF16), wout_ref[...])

    ga, gc = _gate_specs(tm, D)
    row = lambda w: pl.BlockSpec((tm, w), lambda i: (i, 0))
    whole = lambda a: pl.BlockSpec(a.shape, lambda i: (0, 0))
    return pl.pallas_call(
        body, name="mix_out_fwd", grid=(T // tm,),
        in_specs=[row(D), row(ATTN_W), row(CONV_W), ga, gc, whole(woa), whole(woc), whole(wout)],
        out_specs=row(D),
        out_shape=jax.ShapeDtypeStruct((T, D), F32),
        compiler_params=_params(("arbitrary",)),
    )(x, o, yc, proj, proj, woa, woc, wout)


def _mix_out_bwd(dx, o, yc, proj, woa, woc, wout, tm, comm=None):
    T, D = dx.shape
    nt = T // tm

    def body(dx_ref, o_ref, yc_ref, ga_ref, gc_ref, woa_ref, woc_ref, wout_ref,
             do_ref, dyc_ref, dg_ref, dwoa_ref, dwoc_ref, dwout_ref, acca, accc, acco):
        t = pl.program_id(0)

        @pl.when(t == 0)
        def _():
            acca[...] = jnp.zeros_like(acca)
            accc[...] = jnp.zeros_like(accc)
            acco[...] = jnp.zeros_like(acco)

        dxb = dx_ref[...].astype(BF16)
        ov, ycv = o_ref[...], yc_ref[...]
        ya = _dot(ov, woa_ref[...])
        yp = _dot(ycv, woc_ref[...])
        sa = _sigmoid(ga_ref[...].astype(F32))
        sc = _sigmoid(gc_ref[...].astype(F32))
        merged = (sa * ya + sc * yp).astype(BF16)
        dm = _dot_nt(dxb, wout_ref[...])
        dya = (dm * sa).astype(BF16)
        dyp = (dm * sc).astype(BF16)
        dg_ref[:, :D] = (dm * ya * sa * (1.0 - sa)).astype(BF16)
        dg_ref[:, D:] = (dm * yp * sc * (1.0 - sc)).astype(BF16)
        do_ref[...] = _dot_nt(dya, woa_ref[...]).astype(BF16)
        dyc_ref[...] = _dot_nt(dyp, woc_ref[...]).astype(BF16)
        acca[...] += _dot_tn(ov, dya)
        accc[...] += _dot_tn(ycv, dyp)
        acco[...] += _dot_tn(merged, dxb)

        @pl.when(t == nt - 1)
        def _():
            dwoa_ref[...] = acca[...].astype(BF16)
            dwoc_ref[...] = accc[...].astype(BF16)
            dwout_ref[...] = acco[...].astype(BF16)

    ga, gc = _gate_specs(tm, D)
    row = lambda w: pl.BlockSpec((tm, w), lambda i: (i, 0))
    whole = lambda a: pl.BlockSpec(a.shape, lambda i: (0, 0))
    return _pallas(
        body, "mix_out_bwd", (nt,),
        [row(D), row(ATTN_W), row(CONV_W), ga, gc, whole(woa), whole(woc), whole(wout)],
        [row(ATTN_W), row(CONV_W), row(2 * D), whole(woa), whole(woc), whole(wout)],
        [jax.ShapeDtypeStruct((T, ATTN_W), BF16), jax.ShapeDtypeStruct((T, CONV_W), BF16),
         jax.ShapeDtypeStruct((T, 2 * D), BF16),
         jax.ShapeDtypeStruct(woa.shape, BF16), jax.ShapeDtypeStruct(woc.shape, BF16),
         jax.ShapeDtypeStruct(wout.shape, BF16)],
        [pltpu.VMEM(woa.shape, F32), pltpu.VMEM(woc.shape, F32), pltpu.VMEM(wout.shape, F32)],
        (dx, o, yc, proj, proj, woa, woc, wout), comm)


def _proj_pieces(dq, dk, dv, dcb, dcc, dcx, dgates, dflog):
    D = dgates.shape[1] // 2
    return [(dq, ATTN_W, 0), (dk, ATTN_W, 0), (dv, ATTN_W, 0), (dcb, CONV_W, 0), (dcc, CONV_W, 0), (dcx, CONV_W, 0),
            (dgates, D, 0), (dgates, D, 1), (dflog, LANES, 0)]


def _mix_proj_bwd_dx(dres, x, g, pieces, wproj_t, wf_t, tm, comm=None):
    T, D = x.shape
    n = len(pieces)
    w_blocks = [(ATTN_W, 0), (ATTN_W, 1), (ATTN_W, 2), (CONV_W, 3), (CONV_W, 4), (CONV_W, 5),
                (D, COL_GATES // D), (D, COL_GATES // D + 1)]

    def body(*refs):
        dres_ref, x_ref, g_ref = refs[:3]
        p_refs, w_refs = refs[3:3 + n], refs[3 + n:3 + 2 * n]
        dx_ref, dg_ref = refs[3 + 2 * n:]

        @pl.when(pl.program_id(0) == 0)
        def _():
            dg_ref[...] = jnp.zeros_like(dg_ref)

        dh = _dot(p_refs[0][...].astype(BF16), w_refs[0][...])
        for p_ref, w_ref in zip(p_refs[1:], w_refs[1:]):
            dh = dh + _dot(p_ref[...].astype(BF16), w_ref[...])
        xhat, inv = _rms(x_ref[...])
        dx, dg = _rms_bwd(dh, xhat, inv, g_ref[...])
        dx_ref[...] = dres_ref[...] + dx
        dg_ref[...] += dg

    row = pl.BlockSpec((tm, D), lambda i: (i, 0))
    vec = pl.BlockSpec((1, D), lambda i: (0, 0))
    p_specs = [pl.BlockSpec((tm, w), lambda i, cb=cb: (i, cb)) for _, w, cb in pieces]
    w_specs = [pl.BlockSpec((r, D), lambda i, rb=rb: (rb, 0)) for r, rb in w_blocks]
    w_specs.append(pl.BlockSpec((LANES, D), lambda i: (0, 0)))
    return _pallas(
        body, "mix_proj_bwd_dx", (T // tm,),
        [row, row, vec] + p_specs + w_specs, [row, vec],
        [jax.ShapeDtypeStruct((T, D), F32), jax.ShapeDtypeStruct((1, D), F32)], [],
        (dres, x, g, *[p for p, _, _ in pieces], *([wproj_t] * len(w_blocks)), wf_t), comm)


def _matmuls_tn(name, pieces, b, tk):
    T, N = b.shape
    nt = T // tk
    n = len(pieces)

    def body(*refs):
        a_refs, b_ref, out_refs, accs = refs[:n], refs[n], refs[n + 1:2 * n + 1], refs[2 * n + 1:]
        t = pl.program_id(0)

        @pl.when(t == 0)
        def _():
            for acc in accs:
                acc[...] = jnp.zeros_like(acc)

        bv = b_ref[...]
        for a_ref, acc in zip(a_refs, accs):
            acc[...] += _dot_tn(a_ref[...].astype(BF16), bv)

        @pl.when(t == nt - 1)
        def _():
            for out_ref, acc in zip(out_refs, accs):
                out_ref[...] = acc[...].astype(BF16)

    return pl.pallas_call(
        body, name=name, grid=(nt,),
        in_specs=[pl.BlockSpec((tk, w), lambda t, cb=cb: (t, cb)) for _, w, cb in pieces]
        + [pl.BlockSpec((tk, N), lambda t: (t, 0))],
        out_specs=[pl.BlockSpec((w, N), lambda t: (0, 0)) for _, w, _ in pieces],
        out_shape=[jax.ShapeDtypeStruct((w, N), BF16) for _, w, _ in pieces],
        scratch_shapes=[pltpu.VMEM((w, N), F32) for _, w, _ in pieces],
        compiler_params=_params(("arbitrary",)),
    )(*[a for a, _, _ in pieces], b)


def _final_loss(x, target, g, tm):
    T, D = x.shape

    def body(x_ref, t_ref, g_ref, dx_ref, loss_ref, dg_ref):
        @pl.when(pl.program_id(0) == 0)
        def _():
            loss_ref[...] = jnp.zeros_like(loss_ref)
            dg_ref[...] = jnp.zeros_like(dg_ref)

        xhat, inv = _rms(x_ref[...])
        err = xhat * g_ref[...] - t_ref[...]
        loss_ref[...] += 0.5 * jnp.sum(jnp.sum(err * err, axis=1, keepdims=True), axis=0, keepdims=True) / D
        dx, dg = _rms_bwd(err * (1.0 / D), xhat, inv, g_ref[...])
        dx_ref[...] = dx
        dg_ref[...] += dg

    row = pl.BlockSpec((tm, D), lambda i: (i, 0))
    return pl.pallas_call(
        body, name="final_loss", grid=(T // tm,),
        in_specs=[row, row, pl.BlockSpec((1, D), lambda i: (0, 0))],
        out_specs=[row, pl.BlockSpec((1, LANES), lambda i: (0, 0)), pl.BlockSpec((1, D), lambda i: (0, 0))],
        out_shape=[jax.ShapeDtypeStruct((T, D), F32), jax.ShapeDtypeStruct((1, LANES), F32),
                   jax.ShapeDtypeStruct((1, D), F32)],
        compiler_params=_params(("arbitrary",)),
    )(x, target, g)


class _LocalPlan:
    def __init__(self, stacks, small):
        self.stacks, self.small, self.grads = stacks, small, {}

    def weights(self, group):
        return _LAYOUTS[group](self.stacks, self.small)

    def rider(self, kernel_name):
        return None

    def arrived(self, kernel_name, results):
        pass

    def reduce(self, group, grads):
        self.grads.update(grads)

    def reduce_small(self, small_grads, loss):
        pass


def _local_step(x, target, plan, B, S):
    T, D = x.shape
    tm = min(512, T)
    tm_fwd = min(1024, T)
    tq = min(512, S)
    nq = S // tq
    ch = min(256, S)

    def riding(kernel_name, build):
        results, brought = build(plan.rider(kernel_name))
        plan.arrived(kernel_name, brought)
        return results

    w1 = plan.weights("ffn1_in")
    hg1, hu1, n1 = riding("ffn1_up", lambda comm: _ffn_up(
        "ffn1_up", x, w1["ffn1_norm"], w1["ffn1_gate"], w1["ffn1_up"], tm_fwd, comm))
    w1 = plan.weights("ffn1")
    x1, = riding("ffn1_down", lambda comm: _ffn_down("ffn1_down", x, hg1, hu1, w1["ffn1_down"], tm_fwd, comm))
    wm = plan.weights("mix")
    h, proj, flog = _mix_proj_fwd(x1, wm["mix_norm"], wm["w_proj"], wm["w_f"], tm_fwd, 1280)
    cum = _fgate_fwd(flog, wm["b_forget"], B, S, ch)
    cum_t = jnp.transpose(cum[:, :N_HEADS].reshape(B, nq, tq, N_HEADS), (0, 1, 3, 2))
    o, lse = riding("attn_fwd", lambda comm: _attn_fwd(proj, cum, cum_t, B, S, tq, comm))
    yc = _conv_fwd(proj, wm["conv_w"], B, S)
    x2 = _mix_out_fwd(x1, o, yc, proj, wm["w_o_attn"], wm["w_o_conv"], wm["w_out"], tm)
    w2 = plan.weights("ffn2")
    x3, hg2, hu2, n2 = _ffn_fwd("ffn2_fwd", x2, w2["ffn2_norm"], w2["ffn2_gate"], w2["ffn2_up"], w2["ffn2_down"], tm_fwd)[0]
    dx3, loss, d_final_norm = _final_loss(x3, target, w2["final_norm"], tm)

    g = {"final_norm": d_final_norm}
    dx2, dhg2, dhu2, g["ffn2_norm"], df2 = _ffn_bwd_dx("ffn2_bwd_dx", dx3, x2, w2["ffn2_norm"], hg2, hu2,
                                                  w2["ffn2_gate"], w2["ffn2_up"], w2["ffn2_down"], tm)[0]
    plan.reduce("ffn2", dict(zip(("ffn2_gate", "ffn2_up", "ffn2_down"),
                                 _ffn_bwd_dw("ffn2_bwd_dw", n2, df2, hg2, hu2, dhg2, dhu2, tm)[0])))
    do, dyc, dgates, dwoa, dwoc, dwout = riding("mix_out_bwd", lambda comm: _mix_out_bwd(
        dx2, o, yc, proj, wm["w_o_attn"], wm["w_o_conv"], wm["w_out"], tm, comm))
    plan.reduce("out", dict(w_o_attn=_shard_cols(dwoa), w_o_conv=_shard_cols(dwoc), w_out=dwout.reshape(N_CHIPS, -1, D)))
    dq, dk, dv, dcq, dck = riding("attn_bwd", lambda comm: _attn_bwd(proj, o, do, lse, cum, cum_t, B, S, tq, comm))
    dcum = dcq + jnp.pad(jnp.transpose(dck, (0, 1, 3, 2)).reshape(T, N_HEADS), ((0, 0), (0, LANES - N_HEADS)))
    dflog, g["b_forget"] = _fgate_bwd(dcum, flog, wm["b_forget"], B, S, ch)
    dcb, dcc, dcx, g["conv_w"] = _conv_bwd(dyc, proj, wm["conv_w"], B, S)
    pieces = _proj_pieces(dq, dk, dv, dcb, dcc, dcx, dgates, dflog)
    dwq, dwk, dwv, dwcb, dwcc, dwcx = _matmuls_tn("mix_dw_a", pieces[:6], h, tm)
    dwga, dwgc, dwf = _matmuls_tn("mix_dw_b", pieces[6:], h, tm)
    dwin_t = jnp.concatenate([dwq, dwk, dwv, dwf[:N_HEADS], dwcb, dwcc, dwcx, dwga, dwgc], axis=0)
    plan.reduce("w_in", {"w_in": dwin_t.reshape(N_CHIPS, -1, D)})
    dx1, g["mix_norm"] = riding("mix_proj_bwd_dx", lambda comm: _mix_proj_bwd_dx(
        dx2, x1, wm["mix_norm"], pieces, wm["w_proj"], wm["w_f"], min(256, T), comm))
    grad_x, dhg1, dhu1, g["ffn1_norm"], df1 = _ffn_bwd_dx(
        "ffn1_bwd_dx", dx1, x, w1["ffn1_norm"], hg1, hu1, w1["ffn1_gate"], w1["ffn1_up"], w1["ffn1_down"], tm)[0]
    plan.reduce_small(g, loss)
    plan.reduce("ffn1", dict(zip(("ffn1_gate", "ffn1_up", "ffn1_down"), riding("ffn1_bwd_dw", lambda comm: _ffn_bwd_dw(
        "ffn1_bwd_dw", n1, df1, hg1, hu1, dhg1, dhu1, tm, comm)))))
    return loss, grad_x, g


TRANSPOSED = ("ffn1_gate", "ffn1_up", "ffn2_gate", "ffn2_up", "w_in")
NORMS = ("ffn1_norm", "mix_norm", "ffn2_norm", "final_norm")


def _unshard_cols(a):
    return jnp.transpose(a, (1, 0, 2)).reshape(a.shape[1], N_CHIPS * a.shape[2])


def _shard_cols(a):
    return jnp.transpose(a.reshape(a.shape[0], N_CHIPS, a.shape[1] // N_CHIPS), (1, 0, 2))


def _layout_ffn(which):
    def layout(st, small):
        w = {n: st[n] for n in (which + "_gate", which + "_up", which + "_down")}
        w[which + "_norm"] = small[which + "_norm"].reshape(1, -1)
        if which == "ffn2":
            w["final_norm"] = small["final_norm"].reshape(1, -1)
        return w
    return layout


def _layout_mix(st, small):
    win_t = st["w_in"].reshape(-1, st["w_in"].shape[2])
    return {
        "w_proj": jnp.concatenate([win_t[:N_FORGET_COL], win_t[N_FORGET_COL + N_HEADS:]], axis=0),
        "w_f": jnp.pad(win_t[N_FORGET_COL:N_FORGET_COL + N_HEADS], ((0, LANES - N_HEADS), (0, 0))),
        "w_o_attn": _unshard_cols(st["w_o_attn"]),
        "w_o_conv": _unshard_cols(st["w_o_conv"]),
        "w_out": st["w_out"].reshape(-1, st["w_out"].shape[2]),
        "conv_w": _unshard_cols(st["conv_w"]),
        "mix_norm": small["mix_norm"].reshape(1, -1),
        "b_forget": jnp.pad(small["b_forget"].reshape(1, -1), ((0, 0), (0, LANES - N_HEADS))),
    }


def _layout_ffn1_in(st, small):
    return {"ffn1_gate": st["ffn1_gate"], "ffn1_up": st["ffn1_up"], "ffn1_norm": small["ffn1_norm"].reshape(1, -1)}


_LAYOUTS = {"ffn1_in": _layout_ffn1_in, "ffn1": _layout_ffn("ffn1"), "mix": _layout_mix, "ffn2": _layout_ffn("ffn2")}


ANY = pl.BlockSpec(memory_space=pl.ANY)
BIG = ("ffn1_gate", "ffn1_up", "ffn1_down", "w_in", "w_o_attn", "w_o_conv", "w_out",
       "ffn2_gate", "ffn2_up", "ffn2_down")


def _place():
    x, y, c = lax.axis_index("x"), lax.axis_index("y"), lax.axis_index("c")
    others = [(1 - x, y), (x, 1 - y), (1 - x, 1 - y)]
    return x, y, c, others


def _col_halves(cols, c):
    hc = cols // 2
    return pl.ds(pl.multiple_of(c * hc, LANES), hc), pl.ds(pl.multiple_of((1 - c) * hc, LANES), hc)


def _gather_comm(shards, conv_shard=None):
    n = len(shards)
    inputs = list(shards) + ([] if conv_shard is None else [conv_shard])

    def copies(ins, outs, sems):
        send_sems, recv_sems, pass_send, pass_recv = sems[:4]
        x, y, c, others = _place()

        def chip_copy(a, j, chip):
            mine, _ = _col_halves(ins[a].shape[1], c)
            return pltpu.make_async_remote_copy(
                src_ref=ins[a].at[:, mine], dst_ref=outs[a].at[chip, :, mine],
                send_sem=send_sems.at[3 * a + j], recv_sem=recv_sems.at[3 * a + j],
                device_id=(*others[j], c), device_id_type=MESH)

        def pass_copy(a, j, chip, half):
            return pltpu.make_async_remote_copy(
                src_ref=outs[a].at[chip, :, half], dst_ref=outs[a].at[chip, :, half],
                send_sem=pass_send.at[3 * a + j], recv_sem=pass_recv.at[3 * a + j],
                device_id=(x, y, 1 - c), device_id_type=MESH)

        def conv_copy(j, chip):
            return pltpu.make_async_remote_copy(
                src_ref=ins[n], dst_ref=outs[n].at[chip],
                send_sem=sems[4].at[j], recv_sem=sems[5].at[j],
                device_id=(*others[j], c), device_id_type=MESH)

        me = 2 * x + y
        sends = [chip_copy(a, j, me) for a in range(n) for j in range(3)]
        if conv_shard is not None:
            sends += [conv_copy(j, me) for j in range(3)]
        return c, others, sends, chip_copy, pass_copy, conv_copy

    def start(ins, outs, sems):
        for cp in copies(ins, outs, sems)[2]:
            cp.start()

    def finish(ins, outs, sems):
        c, others, sends, chip_copy, pass_copy, conv_copy = copies(ins, outs, sems)
        passed = []
        for a in range(n):
            mine, _ = _col_halves(ins[a].shape[1], c)
            for j, (ox, oy) in enumerate(others):
                chip_copy(a, j, 2 * ox + oy).wait_recv()
                passed.append(pass_copy(a, j, 2 * ox + oy, mine))
                passed[-1].start()
        for a in range(n):
            _, theirs = _col_halves(ins[a].shape[1], c)
            for j, (ox, oy) in enumerate(others):
                pass_copy(a, j, 2 * ox + oy, theirs).wait_recv()
        if conv_shard is not None:
            for j, (ox, oy) in enumerate(others):
                conv_copy(j, 2 * ox + oy).wait_recv()
        for cp in sends + passed:
            cp.wait_send()

    scratch = [pltpu.SemaphoreType.DMA((3 * n,))] * 4
    if conv_shard is not None:
        scratch += [pltpu.SemaphoreType.DMA((3,))] * 2
    return _Comm(inputs, [jax.ShapeDtypeStruct((N_CHIPS,) + s.shape, s.dtype) for s in inputs], scratch, start, finish)


def _fill_own(stacks, shards):
    chip = 2 * lax.axis_index("x") + lax.axis_index("y")
    return [lax.dynamic_update_index_in_dim(st, s, chip, 0) for st, s in zip(stacks, shards)]


def _run_comm(name, comm):
    ci, co = len(comm.inputs), len(comm.out_shape)

    def body(*refs):
        comm.start(refs[:ci], refs[ci:ci + co], refs[ci + co:])
        comm.finish(refs[:ci], refs[ci:ci + co], refs[ci + co:])

    return pl.pallas_call(body, name=name, in_specs=[ANY] * ci, out_specs=[ANY] * co, out_shape=comm.out_shape,
                          scratch_shapes=comm.scratch)(*comm.inputs)


def _sibling_exchange_comm(grads):
    n = len(grads)

    def copies(ins, outs, sems):
        x, y, c, _ = _place()
        return [pltpu.make_async_remote_copy(
            src_ref=ins[a].at[:, :, _col_halves(ins[a].shape[2], c)[1]], dst_ref=outs[a],
            send_sem=sems[0].at[a], recv_sem=sems[1].at[a],
            device_id=(x, y, 1 - c), device_id_type=MESH) for a in range(n)]

    def start(ins, outs, sems):
        for cp in copies(ins, outs, sems):
            cp.start()

    def finish(ins, outs, sems):
        for cp in copies(ins, outs, sems):
            cp.wait()

    half = lambda s: jax.ShapeDtypeStruct((s.shape[0], s.shape[1], s.shape[2] // 2), s.dtype)
    return _Comm(grads, [half(s) for s in grads], [pltpu.SemaphoreType.DMA((n,))] * 2, start, finish)


def _merge_comms(comms):
    def split(refs, count):
        out, at = [], 0
        for cm in comms:
            out.append(refs[at:at + count(cm)])
            at += count(cm)
        return out

    def parts(ins, outs, sems):
        return zip(comms, split(ins, lambda cm: len(cm.inputs)), split(outs, lambda cm: len(cm.out_shape)),
                   split(sems, lambda cm: len(cm.scratch)))

    def start(ins, outs, sems):
        for cm, i, o, s in parts(ins, outs, sems):
            cm.start(i, o, s)

    def finish(ins, outs, sems):
        for cm, i, o, s in parts(ins, outs, sems):
            cm.finish(i, o, s)

    return _Comm(sum([cm.inputs for cm in comms], []), sum([cm.out_shape for cm in comms], []),
                 sum([cm.scratch for cm in comms], []), start, finish)


def _add_halves(name, grads, recvs, core):
    n = len(grads)

    def body(core_ref, *refs):
        for g_ref, r_ref, out_ref in zip(refs[:n], refs[n:2 * n], refs[2 * n:]):
            out_ref[...] = (g_ref[...].astype(F32) + r_ref[...].astype(F32)).astype(BF16)

    half = lambda g: pl.BlockSpec((None, g.shape[1], g.shape[2] // 2), lambda k, core_ref: (k, 0, 0))
    mine = lambda g: pl.BlockSpec((None, g.shape[1], g.shape[2] // 2), lambda k, core_ref: (k, 0, core_ref[0]))
    return pl.pallas_call(
        body, name=name,
        grid_spec=pltpu.PrefetchScalarGridSpec(
            num_scalar_prefetch=1, grid=(N_CHIPS,),
            in_specs=[mine(g) for g in grads] + [half(g) for g in grads],
            out_specs=[half(g) for g in grads]),
        out_shape=[jax.ShapeDtypeStruct(r.shape, BF16) for r in recvs],
        compiler_params=_params(("arbitrary",)),
    )(core, *grads, *recvs)


def _chip_exchange_comm(parts):
    n = len(parts)

    def copies(ins, outs, sems):
        x, y, c, others = _place()
        return [pltpu.make_async_remote_copy(
            src_ref=ins[a].at[2 * ox + oy], dst_ref=outs[a].at[j],
            send_sem=sems[0].at[3 * a + j], recv_sem=sems[1].at[3 * a + j],
            device_id=(ox, oy, c), device_id_type=MESH) for a in range(n) for j, (ox, oy) in enumerate(others)]

    def start(ins, outs, sems):
        for cp in copies(ins, outs, sems):
            cp.start()

    def finish(ins, outs, sems):
        for cp in copies(ins, outs, sems):
            cp.wait()

    return _Comm(parts, [jax.ShapeDtypeStruct((3,) + s.shape[1:], s.dtype) for s in parts],
                 [pltpu.SemaphoreType.DMA((3 * n,))] * 2, start, finish)


HBM = pl.BlockSpec(memory_space=pltpu.HBM)
SEM = pl.BlockSpec(memory_space=pltpu.SEMAPHORE)


def _split_exchange_copies(parts, lands, send_sems, recv_sems):
    x, y, c, others = _place()
    return [pltpu.make_async_remote_copy(
        src_ref=parts[a].at[2 * ox + oy], dst_ref=lands[a].at[j],
        send_sem=send_sems.at[3 * a + j], recv_sem=recv_sems.at[3 * a + j],
        device_id=(ox, oy, c), device_id_type=MESH) for a in range(len(parts)) for j, (ox, oy) in enumerate(others)]


def _exchange_start(name, parts):
    n = len(parts)

    def body(*refs):
        ins, lands = refs[:n], refs[n:2 * n]
        send_sems, recv_sems, token = refs[2 * n], refs[2 * n + 1], refs[-1]
        for cp in _split_exchange_copies(ins, lands, send_sems, recv_sems):
            cp.start()
        token[...] = jnp.zeros_like(token)

    land_shape = [(3,) + p.shape[1:] for p in parts]
    outs = pl.pallas_call(
        body, name=name,
        out_shape=[pltpu.SemaphoreType.DMA((3 * n,)), pltpu.SemaphoreType.DMA((3 * n,))]
        + [pltpu.HBM(p.shape, p.dtype) for p in parts] + [pltpu.HBM(s, p.dtype) for s, p in zip(land_shape, parts)]
        + [jax.ShapeDtypeStruct((8, LANES), F32)],
        in_specs=[HBM] * (2 * n), out_specs=[SEM, SEM] + [HBM] * (2 * n) + [pl.BlockSpec(memory_space=pltpu.VMEM)],
        input_output_aliases={i: 2 + i for i in range(2 * n)},
        compiler_params=pltpu.CompilerParams(has_side_effects=pltpu.SideEffectType.DATAFLOW_SIDE_EFFECTING),
    )(*[pltpu.with_memory_space_constraint(p, pltpu.HBM) for p in parts],
      *[pltpu.with_memory_space_constraint(lax.empty(s, p.dtype), pltpu.HBM) for s, p in zip(land_shape, parts)])
    return outs[0], outs[1], list(outs[2:2 + n]), list(outs[2 + n:2 + 2 * n]), outs[-1]


def _exchange_wait(name, send_sems, recv_sems, parts, lands, after):
    n = len(parts)

    def body(*refs):
        ins, zones = refs[:n], refs[n:2 * n]
        for cp in _split_exchange_copies(ins, zones, refs[2 * n], refs[2 * n + 1]):
            cp.wait_send()
            cp.wait_recv()

    outs = pl.pallas_call(
        body, name=name,
        out_shape=[pltpu.HBM(p.shape, p.dtype) for p in parts] + [pltpu.HBM(z.shape, z.dtype) for z in lands],
        in_specs=[HBM] * (2 * n) + [SEM, SEM] + [ANY] * len(after), out_specs=[HBM] * (2 * n),
        input_output_aliases={i: i for i in range(2 * n)},
        compiler_params=pltpu.CompilerParams(has_side_effects=pltpu.SideEffectType.DATAFLOW_SIDE_EFFECTING),
    )(*parts, *lands, send_sems, recv_sems, *after)
    return list(outs[:n]), list(outs[n:])


def _sum_chips(name, owns, recvs, chip, after):
    n = len(owns)
    hc = owns[0].shape[2]
    assert all(o.shape[2] == hc for o in owns)

    def body(chip_ref, *refs):
        for own_ref, recv_ref, out_ref in zip(refs[:n], refs[n:2 * n], refs[2 * n + 1:]):
            acc = own_ref[...].astype(F32)
            for j in range(3):
                acc = acc + recv_ref[j].astype(F32)
            out_ref[...] = acc

    return pl.pallas_call(
        body, name=name,
        grid_spec=pltpu.PrefetchScalarGridSpec(
            num_scalar_prefetch=1, grid=(hc // LANES,),
            in_specs=[pl.BlockSpec((None, o.shape[1], LANES), lambda i, chip_ref: (chip_ref[0], 0, i)) for o in owns]
            + [pl.BlockSpec((3, o.shape[1], LANES), lambda i, chip_ref: (0, 0, i)) for o in owns]
            + [pl.BlockSpec((8, LANES), lambda i, chip_ref: (0, 0))],
            out_specs=[pl.BlockSpec((o.shape[1], LANES), lambda i, chip_ref: (0, i)) for o in owns]),
        out_shape=[jax.ShapeDtypeStruct((o.shape[1], hc), F32) for o in owns],
        compiler_params=_params(("arbitrary",)),
    )(chip, *owns, *recvs, after)


def _share_halves(name, halves):
    n = len(halves)

    def body(*refs):
        srcs, dsts = refs[:n], refs[n:2 * n]
        send_sems, recv_sems = refs[2 * n:]
        x, y, c, _ = _place()
        copies = [pltpu.make_async_remote_copy(
            src_ref=srcs[a], dst_ref=dsts[a], send_sem=send_sems.at[a], recv_sem=recv_sems.at[a],
            device_id=(x, y, 1 - c), device_id_type=MESH) for a in range(n)]
        for cp in copies:
            cp.start()
        for cp in copies:
            cp.wait()

    return pl.pallas_call(
        body, name=name,
        in_specs=[ANY] * n, out_specs=[ANY] * n,
        out_shape=[jax.ShapeDtypeStruct(s.shape, s.dtype) for s in halves],
        scratch_shapes=[pltpu.SemaphoreType.DMA((n,)), pltpu.SemaphoreType.DMA((n,))],
    )(*halves)


def _small_gather_comm(part):
    def copies(ins, outs, sems):
        x, y, c, _ = _place()
        me = 4 * x + 2 * y + c
        both = []
        for d in range(1, N_DEV):
            px, py, pc = (1 - x if d & 4 else x, 1 - y if d & 2 else y, 1 - c if d & 1 else c)
            send = pltpu.make_async_remote_copy(
                src_ref=ins[0], dst_ref=outs[0].at[me], send_sem=sems[0].at[d - 1], recv_sem=sems[1].at[d - 1],
                device_id=(px, py, pc), device_id_type=MESH)
            recv = pltpu.make_async_remote_copy(
                src_ref=ins[0], dst_ref=outs[0].at[4 * px + 2 * py + pc], send_sem=sems[0].at[d - 1],
                recv_sem=sems[1].at[d - 1], device_id=(px, py, pc), device_id_type=MESH)
            both.append((send, recv))
        return both

    def start(ins, outs, sems):
        for send, _ in copies(ins, outs, sems):
            send.start()

    def finish(ins, outs, sems):
        for send, recv in copies(ins, outs, sems):
            recv.wait_recv()
            send.wait_send()

    return _Comm([part], [jax.ShapeDtypeStruct((N_DEV,) + part.shape, F32)],
                 [pltpu.SemaphoreType.DMA((N_DEV - 1,))] * 2, start, finish)


def _sum_devices(parts):
    def body(p_ref, out_ref):
        acc = p_ref[0]
        for k in range(1, N_DEV):
            acc = acc + p_ref[k]
        out_ref[...] = acc

    return pl.pallas_call(
        body, name="sum_devices", grid=(1,),
        in_specs=[pl.BlockSpec(parts.shape, lambda i: (0, 0, 0))],
        out_specs=pl.BlockSpec(parts.shape[1:], lambda i: (0, 0)),
        out_shape=jax.ShapeDtypeStruct(parts.shape[1:], F32),
        compiler_params=_params(("arbitrary",)),
    )(parts)


def _adam_update(w, g, m, v):
    nm = ADAM_B1 * m + (1.0 - ADAM_B1) * g
    nv = ADAM_B2 * v + (1.0 - ADAM_B2) * (g * g)
    m_hat = nm * (1.0 / (1.0 - ADAM_B1 ** ADAM_STEP))
    v_hat = nv * (1.0 / (1.0 - ADAM_B2 ** ADAM_STEP))
    return -ADAM_LR * (m_hat / (jnp.sqrt(v_hat) + ADAM_EPS) + ADAM_WD * w), nm, nv


def _adamw(name, w, g, m, v):
    def body(w_ref, g_ref, m_ref, v_ref, d_ref, nm_ref, nv_ref):
        d_ref[...], nm_ref[...], nv_ref[...] = _adam_update(w_ref[...], g_ref[...], m_ref[...], v_ref[...])

    spec = pl.BlockSpec(w.shape, lambda i: (0, 0))
    out = jax.ShapeDtypeStruct(w.shape, F32)
    return pl.pallas_call(
        body, name=name, grid=(1,),
        in_specs=[spec] * 4, out_specs=[spec] * 3, out_shape=[out] * 3,
        compiler_params=_params(("arbitrary",)),
    )(w, g, m, v)


def _adamw_halves(name, ws, mines, theirs, ms, vs, core):
    n = len(ws)
    cols = ws[0].shape[1]
    assert all(w.shape[1] == cols for w in ws)
    hc = cols // 2
    tc = LANES if n > 1 else min(256, hc)
    nt = hc // tc

    def body(core_ref, *refs):
        ins, outs = refs[:5 * n], refs[5 * n:]
        for a in range(n):
            w_ref, mine_ref, theirs_ref, m_ref, v_ref = [ins[j * n + a] for j in range(5)]
            g_ref, d_ref, nm_ref, nv_ref = outs[4 * a:4 * a + 4]
            gv = jnp.where(pl.program_id(0) == core_ref[0], mine_ref[...], theirs_ref[...])
            g_ref[...] = gv
            d_ref[...], nm_ref[...], nv_ref[...] = _adam_update(w_ref[...], gv, m_ref[...], v_ref[...])

    whole = lambda w: pl.BlockSpec((w.shape[0], tc), lambda h, i, core_ref: (0, h * nt + i))
    mine_spec = lambda w: pl.BlockSpec((w.shape[0], tc), lambda h, i, core_ref: (0, jnp.where(h == core_ref[0], i, 0)))
    theirs_spec = lambda w: pl.BlockSpec((w.shape[0], tc), lambda h, i, core_ref: (0, jnp.where(h == core_ref[0], 0, i)))
    outs = pl.pallas_call(
        body, name=name,
        grid_spec=pltpu.PrefetchScalarGridSpec(
            num_scalar_prefetch=1, grid=(2, nt),
            in_specs=[whole(w) for w in ws] + [mine_spec(w) for w in ws] + [theirs_spec(w) for w in ws]
            + [whole(w) for w in ws] * 2,
            out_specs=[whole(w) for w in ws for _ in range(4)]),
        out_shape=[jax.ShapeDtypeStruct(w.shape, F32) for w in ws for _ in range(4)],
        compiler_params=_params(("arbitrary", "arbitrary")),
    )(core, *ws, *mines, *theirs, *ms, *vs)
    return [outs[4 * a:4 * a + 4] for a in range(n)]


WEIGHTS = ("ffn1_norm", "ffn1_gate", "ffn1_up", "ffn1_down", "mix_norm", "w_in", "b_forget", "conv_w",
           "w_o_attn", "w_o_conv", "w_out", "ffn2_norm", "ffn2_gate", "ffn2_up", "ffn2_down", "final_norm")
VEC_ROWS = 8


def _pack_small(t, conv_rows):
    conv = t["conv_w"]
    parts = [t[n].reshape(VEC_ROWS, LANES) for n in NORMS]
    parts.append(jnp.pad(conv, ((0, conv_rows - conv.shape[0]), (0, 0))))
    parts.append(jnp.pad(t["b_forget"].reshape(1, N_HEADS), ((0, 7), (0, LANES - N_HEADS))))
    return jnp.concatenate(parts, axis=0)


def _unpack_small(p, conv_rows):
    out = {n: p[VEC_ROWS * i:VEC_ROWS * (i + 1)].reshape(-1) for i, n in enumerate(NORMS)}
    base = VEC_ROWS * len(NORMS)
    out["conv_w"] = p[base:base + 3]
    out["b_forget"] = p[base + conv_rows, :N_HEADS]
    return out


def _travel(name, a):
    return a.T if name in TRANSPOSED else a


GATHER_FIRST = ("ffn1_gate", "ffn1_up")
GATHER_RIDES = {"ffn1_up": ("ffn1_down", "w_in"), "ffn1_down": ("w_o_attn", "w_o_conv", "w_out"),
                "attn_fwd": ("ffn2_gate", "ffn2_up", "ffn2_down")}
SIBLING_RIDES = {"ffn2": "mix_out_bwd", "out": None, "w_in": "mix_proj_bwd_dx", "ffn1": None}
CHIP_RIDES = {"ffn2": "attn_bwd", "out": "attn_bwd", "w_in": "ffn1_bwd_dw", "ffn1": None}
SMALL_RIDE = "ffn1_bwd_dw"


class _MeshPlan(_LocalPlan):
    def __init__(self, wts, core):
        self.small, self.core = wts, core
        self.shards = {n: wts[n].astype(BF16) for n in BIG}
        self.chip_part, self.from_chips, self.rides = {}, {}, {}
        conv_shard = jnp.pad(wts["conv_w"], ((0, 8 - wts["conv_w"].shape[0]), (0, 0)))
        own = [self.shards[n] for n in GATHER_FIRST] + [conv_shard]
        got = _run_comm("gather_first", _gather_comm(own[:-1], conv_shard))
        self.stacks = dict(zip(GATHER_FIRST + ("conv_w",), _fill_own(got, own)))
        for kernel_name, names in GATHER_RIDES.items():
            mine = [self.shards[n] for n in names]
            self._ride(kernel_name, _gather_comm(mine),
                       lambda got, names=names, mine=mine: self.stacks.update(zip(names, _fill_own(got, mine))))

    def _ride(self, kernel_name, comm, then):
        self.rides.setdefault(kernel_name, []).append((comm, then))

    def rider(self, kernel_name):
        comms = [comm for comm, _ in self.rides.get(kernel_name, [])]
        return _merge_comms(comms) if comms else None

    def arrived(self, kernel_name, results):
        for comm, then in self.rides.pop(kernel_name, []):
            then(results[:len(comm.out_shape)])
            results = results[len(comm.out_shape):]

    def reduce(self, group, grads):
        names = tuple(grads)
        mine = [grads[n] for n in names]

        def with_sibling(from_sibling):
            parts = _add_halves("add_halves_" + group, mine, list(from_sibling), self.core)
            self.chip_part.update(zip(names, parts))
            if CHIP_RIDES[group] is None:
                self.last = (names, _exchange_start("exchange_start_" + group, parts))
            else:
                self._ride(CHIP_RIDES[group], _chip_exchange_comm(parts),
                           lambda got: self.from_chips.update(zip(names, got)))

        if SIBLING_RIDES[group] is None:
            with_sibling(_run_comm("sibling_exchange_" + group, _sibling_exchange_comm(mine)))
        else:
            self._ride(SIBLING_RIDES[group], _sibling_exchange_comm(mine), with_sibling)

    def reduce_small(self, gs, loss):
        conv_all = _shard_cols(gs["conv_w"]).reshape(N_CHIPS * 8, LANES)
        part = _pack_small({**{n: gs[n] for n in NORMS}, "conv_w": conv_all, "b_forget": gs["b_forget"][0, :N_HEADS]},
                           N_CHIPS * 8)
        part = jnp.concatenate([part, jnp.broadcast_to(loss, (8, LANES))], axis=0)
        me = 4 * lax.axis_index("x") + 2 * lax.axis_index("y") + lax.axis_index("c")

        def landed(got):
            self.small_parts = lax.dynamic_update_index_in_dim(got[0], part, me, 0)

        self._ride(SMALL_RIDE, _small_gather_comm(part), landed)


def kernel(x, ffn1_norm, ffn1_gate, ffn1_up, ffn1_down, mix_norm, w_in, b_forget, conv_w, w_o_attn, w_o_conv, w_out, ffn2_norm, ffn2_gate, ffn2_up, ffn2_down, final_norm, loss_target, m_ffn1_norm, m_ffn1_gate, m_ffn1_up, m_ffn1_down, m_mix_norm, m_w_in, m_b_forget, m_conv_w, m_w_o_attn, m_w_o_conv, m_w_out, m_ffn2_norm, m_ffn2_gate, m_ffn2_up, m_ffn2_down, m_final_norm, v_ffn1_norm, v_ffn1_gate, v_ffn1_up, v_ffn1_down, v_mix_norm, v_w_in, v_b_forget, v_conv_w, v_w_o_attn, v_w_o_conv, v_w_out, v_ffn2_norm, v_ffn2_gate, v_ffn2_up, v_ffn2_down, v_final_norm):
    given = dict(locals())
    wts = {n: _travel(n, given[n]) for n in WEIGHTS}
    mom = {n: _travel(n, given["m_" + n]) for n in WEIGHTS}
    var = {n: _travel(n, given["v_" + n]) for n in WEIGHTS}
    B, S, D = x.shape
    chip = 2 * lax.axis_index("x") + lax.axis_index("y")
    chip1 = chip.astype(jnp.int32).reshape(1)
    core = lax.axis_index("c").astype(jnp.int32).reshape(1)

    plan = _MeshPlan(wts, core)
    loss, grad_x, gs = _local_step(x.reshape(B * S, D), loss_target.reshape(B * S, D), plan, B, S)

    last_names, (send_sems, recv_sems, parts_thru, lands, token) = plan.last
    delta, new_m, new_v, grads = {}, {}, {}, {}

    def finish(tag, names):
        by_cols = {}
        for n in names:
            by_cols.setdefault(wts[n].shape[1], []).append(n)
        mine = {}
        for cols, ns in by_cols.items():
            mine.update(zip(ns, _sum_chips("sum_chips_%s_%d" % (tag, cols), [plan.chip_part[n] for n in ns],
                                           [plan.from_chips[n] for n in ns], chip1, token)))
        theirs = dict(zip(names, _share_halves("share_halves_" + tag, [mine[n] for n in names])))
        raw = []
        for cols, ns in by_cols.items():
            outs = _adamw_halves("adamw_%s_%d" % (tag, cols), [wts[n] for n in ns], [mine[n] for n in ns],
                                 [theirs[n] for n in ns], [mom[n] for n in ns], [var[n] for n in ns], core)
            for n, per in zip(ns, outs):
                raw.append(per[-1])
                grads[n], delta[n], new_m[n], new_v[n] = [_travel(n, o) for o in per]
        return raw

    small_sum = _sum_devices(plan.small_parts)
    base = VEC_ROWS * len(NORMS)
    loss_row = small_sum.shape[0] - 8
    small_grads = _unpack_small(small_sum, N_CHIPS * 8)
    small_grads["conv_w"] = lax.dynamic_slice_in_dim(small_sum[base:base + N_CHIPS * 8], chip * 8, 8, axis=0)[:3]
    packs = [_pack_small(t, 8) for t in (wts, small_grads, mom, var)]
    small_out = _adamw("adamw_small", *packs)

    done = finish("early", [n for n in BIG if n not in last_names])
    parts_back, got = _exchange_wait("exchange_wait", send_sems, recv_sems, parts_thru, lands, done + list(small_out))
    plan.chip_part.update(zip(last_names, parts_back))
    plan.from_chips.update(zip(last_names, got))
    finish("last", last_names)
    grads.update(small_grads)
    for out, p in zip((delta, new_m, new_v), small_out):
        out.update(_unpack_small(p, 8))

    return (small_sum[loss_row, 0], grad_x.reshape(B, S, D), *[grads[n] for n in WEIGHTS], *[delta[n] for n in WEIGHTS],
            *[new_m[n] for n in WEIGHTS], *[new_v[n] for n in WEIGHTS])
```

```python
import functools
import math

import jax
import jax.numpy as jnp
from jax import lax
from jax.experimental import pallas as pl
from jax.experimental.pallas import tpu as pltpu

F32 = jnp.float32
BF16 = jnp.bfloat16
MESH = pl.DeviceIdType.MESH

N_CHIPS = 4
N_DEV = 8
N_HEADS = 8
HEAD_DIM = 64
HEAD_PAIRS = N_HEADS // 2
ATTN_W = N_HEADS * HEAD_DIM
CONV_W = 512
RMS_EPS = 1e-6
FFN_RES = 0.5
LANES = 128
VMEM_LIMIT = 56 * 1024 * 1024
ROW_BLOCK = 256

ADAM_LR = 0.001
ADAM_B1 = 0.9
ADAM_B2 = 0.999
ADAM_EPS = 1e-08
ADAM_WD = 0.01
ADAM_STEP = 10

PROJ_W = 3 * ATTN_W + 3 * CONV_W + 2 * 1024
COL_CB, COL_CC, COL_CX = 3 * ATTN_W, 3 * ATTN_W + CONV_W, 3 * ATTN_W + 2 * CONV_W
COL_GATES = 3 * ATTN_W + 3 * CONV_W
N_FORGET_COL = 3 * ATTN_W


def _params(sem=None, vmem=VMEM_LIMIT):
    return pltpu.CompilerParams(dimension_semantics=sem, vmem_limit_bytes=vmem)


def _dot(a, b):
    return lax.dot_general(a, b, (((1,), (0,)), ((), ())), preferred_element_type=F32)


def _dot_nt(a, b):
    return lax.dot_general(a, b, (((1,), (1,)), ((), ())), preferred_element_type=F32)


def _dot_tn(a, b):
    return lax.dot_general(a, b, (((0,), (0,)), ((), ())), preferred_element_type=F32)


def _sigmoid(x):
    return 1.0 / (1.0 + jnp.exp(-x))


def _rms(xv):
    inv = lax.rsqrt(jnp.mean(xv * xv, axis=-1, keepdims=True) + RMS_EPS)
    return xv * inv, inv


class _Comm:
    def __init__(self, inputs, out_shape, scratch, start, finish):
        self.inputs, self.out_shape, self.scratch = list(inputs), list(out_shape), list(scratch)
        self.start, self.finish = start, finish


def _pallas(body, name, grid, in_specs, out_specs, out_shape, scratch, args, comm=None):
    sem = ("arbitrary",) * len(grid)
    if comm is None:
        outs = pl.pallas_call(body, name=name, grid=grid, in_specs=in_specs, out_specs=out_specs,
                              out_shape=out_shape, scratch_shapes=scratch, compiler_params=_params(sem))(*args)
        return list(outs), []
    n_in, n_out, n_scr = len(in_specs), len(out_specs), len(scratch)
    ci, co = len(comm.inputs), len(comm.out_shape)

    def riding(*refs):
        ins, refs = refs[:n_in], refs[n_in:]
        cins, refs = refs[:ci], refs[ci:]
        outs, refs = refs[:n_out], refs[n_out:]
        couts, refs = refs[:co], refs[co:]
        scr, sems = refs[:n_scr], refs[n_scr:]
        ids = [pl.program_id(d) for d in range(len(grid))]
        first = functools.reduce(lambda a, b: a & b, [i == 0 for i in ids])
        last = functools.reduce(lambda a, b: a & b, [i == g - 1 for i, g in zip(ids, grid)])

        @pl.when(first)
        def _():
            comm.start(cins, couts, sems)

        body(*ins, *outs, *scr)

        @pl.when(last)
        def _():
            comm.finish(cins, couts, sems)

    any_spec = pl.BlockSpec(memory_space=pl.ANY)
    outs = pl.pallas_call(
        riding, name=name, grid=grid,
        in_specs=list(in_specs) + [any_spec] * ci, out_specs=list(out_specs) + [any_spec] * co,
        out_shape=list(out_shape) + comm.out_shape, scratch_shapes=list(scratch) + comm.scratch,
        compiler_params=_params(sem))(*args, *comm.inputs)
    return list(outs[:n_out]), list(outs[n_out:])


def _rms_bwd(dn, xhat, inv, g):
    dxhat = dn * g
    dx = inv * (dxhat - xhat * jnp.mean(dxhat * xhat, axis=-1, keepdims=True))
    return dx, jnp.sum(dn * xhat, axis=0, keepdims=True)


def _ffn_fwd(name, x, g, wgt, wut, wd, tm, comm=None):
    T, D = x.shape
    K, Fs, _ = wgt.shape

    def body(x_ref, g_ref, wg_ref, wu_ref, wd_ref, out_ref, hg_ref, hu_ref, n_ref, acc_scr):
        k = pl.program_id(1)

        @pl.when(k == 0)
        def _():
            xhat, _ = _rms(x_ref[...])
            n_ref[...] = (xhat * g_ref[...]).astype(BF16)
            acc_scr[...] = jnp.zeros_like(acc_scr)

        n = n_ref[...]
        hg = _dot_nt(n, wg_ref[...])
        hu = _dot_nt(n, wu_ref[...])
        hg_ref[...] = hg.astype(BF16)
        hu_ref[...] = hu.astype(BF16)
        act = (hg * _sigmoid(hg) * hu).astype(BF16)
        acc_scr[...] += _dot(act, wd_ref[...])

        @pl.when(k == K - 1)
        def _():
            out_ref[...] = x_ref[...] + FFN_RES * acc_scr[...]

    w_spec = pl.BlockSpec((None, Fs, D), lambda i, k: (k, 0, 0))
    act_spec = pl.BlockSpec((None, tm, Fs), lambda i, k: (k, i, 0))
    return _pallas(
        body, name, (T // tm, K),
        [pl.BlockSpec((tm, D), lambda i, k: (i, 0)), pl.BlockSpec((1, D), lambda i, k: (0, 0)),
         w_spec, w_spec, w_spec],
        [pl.BlockSpec((tm, D), lambda i, k: (i, 0)), act_spec, act_spec, pl.BlockSpec((tm, D), lambda i, k: (i, 0))],
        [jax.ShapeDtypeStruct((T, D), F32), jax.ShapeDtypeStruct((K, T, Fs), BF16),
         jax.ShapeDtypeStruct((K, T, Fs), BF16), jax.ShapeDtypeStruct((T, D), BF16)],
        [pltpu.VMEM((tm, D), F32)],
        (x, g, wgt, wut, wd), comm)


def _ffn_up(name, x, g, wgt, wut, tm, comm=None):
    T, D = x.shape
    K, Fs, _ = wgt.shape

    def body(x_ref, g_ref, wg_ref, wu_ref, hg_ref, hu_ref, n_ref):
        @pl.when(pl.program_id(1) == 0)
        def _():
            xhat, _ = _rms(x_ref[...])
            n_ref[...] = (xhat * g_ref[...]).astype(BF16)

        n = n_ref[...]
        hg_ref[...] = _dot_nt(n, wg_ref[...]).astype(BF16)
        hu_ref[...] = _dot_nt(n, wu_ref[...]).astype(BF16)

    w_spec = pl.BlockSpec((None, Fs, D), lambda i, k: (k, 0, 0))
    act_spec = pl.BlockSpec((None, tm, Fs), lambda i, k: (k, i, 0))
    return _pallas(
        body, name, (T // tm, K),
        [pl.BlockSpec((tm, D), lambda i, k: (i, 0)), pl.BlockSpec((1, D), lambda i, k: (0, 0)), w_spec, w_spec],
        [act_spec, act_spec, pl.BlockSpec((tm, D), lambda i, k: (i, 0))],
        [jax.ShapeDtypeStruct((K, T, Fs), BF16), jax.ShapeDtypeStruct((K, T, Fs), BF16),
         jax.ShapeDtypeStruct((T, D), BF16)],
        [],
        (x, g, wgt, wut), comm)


def _ffn_down(name, x, hg, hu, wd, tm, comm=None):
    T, D = x.shape
    K, Fs, _ = wd.shape

    def body(x_ref, hg_ref, hu_ref, wd_ref, out_ref, acc_scr):
        k = pl.program_id(1)

        @pl.when(k == 0)
        def _():
            acc_scr[...] = jnp.zeros_like(acc_scr)

        hgv = hg_ref[...].astype(F32)
        act = (hgv * _sigmoid(hgv) * hu_ref[...].astype(F32)).astype(BF16)
        acc_scr[...] += _dot(act, wd_ref[...])

        @pl.when(k == K - 1)
        def _():
            out_ref[...] = x_ref[...] + FFN_RES * acc_scr[...]

    act_spec = pl.BlockSpec((None, tm, Fs), lambda i, k: (k, i, 0))
    row = pl.BlockSpec((tm, D), lambda i, k: (i, 0))
    return _pallas(
        body, name, (T // tm, K),
        [row, act_spec, act_spec, pl.BlockSpec((None, Fs, D), lambda i, k: (k, 0, 0))],
        [row], [jax.ShapeDtypeStruct((T, D), F32)], [pltpu.VMEM((tm, D), F32)],
        (x, hg, hu, wd), comm)


def _ffn_bwd_dx(name, dout, x, g, hg, hu, wgt, wut, wd, tm, comm=None):
    T, D = x.shape
    K, Fs, _ = wgt.shape

    def body(dout_ref, x_ref, g_ref, hg_ref, hu_ref, wg_ref, wu_ref, wd_ref,
             dx_ref, dhg_ref, dhu_ref, dg_ref, df_ref, dn_scr):
        i, k = pl.program_id(0), pl.program_id(1)

        @pl.when(k == 0)
        def _():
            df_ref[...] = (FFN_RES * dout_ref[...]).astype(BF16)
            dn_scr[...] = jnp.zeros_like(dn_scr)

        @pl.when((k == 0) & (i == 0))
        def _():
            dg_ref[...] = jnp.zeros_like(dg_ref)

        for r0 in range(0, tm, ROW_BLOCK):
            rows = slice(r0, r0 + ROW_BLOCK)
            dact = _dot_nt(df_ref[rows, :], wd_ref[...])
            hgv = hg_ref[rows, :].astype(F32)
            huv = hu_ref[rows, :].astype(F32)
            s = _sigmoid(hgv)
            dhu = (dact * (hgv * s)).astype(BF16)
            dhg = (dact * huv * (s * (1.0 + hgv * (1.0 - s)))).astype(BF16)
            dhg_ref[rows, :] = dhg
            dhu_ref[rows, :] = dhu
            dn_scr[rows, :] += _dot(dhg, wg_ref[...]) + _dot(dhu, wu_ref[...])

        @pl.when(k == K - 1)
        def _():
            xhat, inv = _rms(x_ref[...])
            dx, dg = _rms_bwd(dn_scr[...], xhat, inv, g_ref[...])
            dx_ref[...] = dout_ref[...] + dx
            dg_ref[...] += dg

    w_spec = pl.BlockSpec((None, Fs, D), lambda i, k: (k, 0, 0))
    act_spec = pl.BlockSpec((None, tm, Fs), lambda i, k: (k, i, 0))
    row = pl.BlockSpec((tm, D), lambda i, k: (i, 0))
    row_once = pl.BlockSpec((tm, D), lambda i, k: (i, 0), pipeline_mode=pl.Buffered(1))
    vec = pl.BlockSpec((1, D), lambda i, k: (0, 0))
    return _pallas(
        body, name, (T // tm, K),
        [row, row_once, vec, act_spec, act_spec, w_spec, w_spec, w_spec],
        [row_once, act_spec, act_spec, vec, row],
        [jax.ShapeDtypeStruct((T, D), F32), jax.ShapeDtypeStruct((K, T, Fs), BF16),
         jax.ShapeDtypeStruct((K, T, Fs), BF16), jax.ShapeDtypeStruct((1, D), F32),
         jax.ShapeDtypeStruct((T, D), BF16)],
        [pltpu.VMEM((tm, D), F32)],
        (dout, x, g, hg, hu, wgt, wut, wd), comm)


def _ffn_bwd_dw(name, n, df, hg, hu, dhg, dhu, tk, comm=None):
    T, D = n.shape
    K, _, Fs = hg.shape
    nt = T // tk

    def body(n_ref, df_ref, hg_ref, hu_ref, dhg_ref, dhu_ref, dwg_ref, dwu_ref, dwd_ref, accg, accu, accd):
        t = pl.program_id(1)

        @pl.when(t == 0)
        def _():
            accg[...] = jnp.zeros_like(accg)
            accu[...] = jnp.zeros_like(accu)
            accd[...] = jnp.zeros_like(accd)

        nv = n_ref[...]
        hgv = hg_ref[...].astype(F32)
        act = (hgv * _sigmoid(hgv) * hu_ref[...].astype(F32)).astype(BF16)
        accg[...] += _dot_tn(dhg_ref[...], nv)
        accu[...] += _dot_tn(dhu_ref[...], nv)
        accd[...] += _dot_tn(act, df_ref[...])

        @pl.when(t == nt - 1)
        def _():
            dwg_ref[...] = accg[...].astype(BF16)
            dwu_ref[...] = accu[...].astype(BF16)
            dwd_ref[...] = accd[...].astype(BF16)

    act_spec = pl.BlockSpec((None, tk, Fs), lambda k, t: (k, t, 0))
    w_spec = pl.BlockSpec((None, Fs, D), lambda k, t: (k, 0, 0))
    row = pl.BlockSpec((tk, D), lambda k, t: (t, 0))
    return _pallas(
        body, name, (K, nt),
        [row, row, act_spec, act_spec, act_spec, act_spec],
        [w_spec, w_spec, w_spec],
        [jax.ShapeDtypeStruct((K, Fs, D), BF16)] * 3,
        [pltpu.VMEM((Fs, D), F32)] * 3,
        (n, df, hg, hu, dhg, dhu), comm)


def _mix_proj_fwd(x, g, wproj_t, wf_t, tm, tn):
    T, D = x.shape
    N = wproj_t.shape[0]

    def body(x_ref, g_ref, w_ref, wf_ref, h_ref, proj_ref, flog_ref, h_scr):
        @pl.when(pl.program_id(1) == 0)
        def _():
            xhat, _ = _rms(x_ref[...])
            h = (xhat * g_ref[...]).astype(BF16)
            h_scr[...] = h
            h_ref[...] = h
            flog_ref[...] = _dot_nt(h, wf_ref[...])

        proj_ref[...] = _dot_nt(h_scr[...], w_ref[...]).astype(BF16)

    return pl.pallas_call(
        body, name="mix_proj_fwd", grid=(T // tm, N // tn),
        in_specs=[pl.BlockSpec((tm, D), lambda i, n: (i, 0)),
                  pl.BlockSpec((1, D), lambda i, n: (0, 0)),
                  pl.BlockSpec((tn, D), lambda i, n: (n, 0)),
                  pl.BlockSpec((LANES, D), lambda i, n: (0, 0))],
        out_specs=[pl.BlockSpec((tm, D), lambda i, n: (i, 0)),
                   pl.BlockSpec((tm, tn), lambda i, n: (i, n)),
                   pl.BlockSpec((tm, LANES), lambda i, n: (i, 0))],
        out_shape=[jax.ShapeDtypeStruct((T, D), BF16),
                   jax.ShapeDtypeStruct((T, N), BF16),
                   jax.ShapeDtypeStruct((T, LANES), F32)],
        scratch_shapes=[pltpu.VMEM((tm, D), BF16)],
        compiler_params=_params(("arbitrary", "arbitrary")),
    )(x, g, wproj_t, wf_t)


def _log_sigmoid(z):
    return -(jnp.maximum(-z, 0.0) + jnp.log(1.0 + jnp.exp(-jnp.abs(z))))


def _tri(n, lower):
    r = lax.broadcasted_iota(jnp.int32, (n, n), 0)
    c = lax.broadcasted_iota(jnp.int32, (n, n), 1)
    return jnp.where((r >= c) if lower else (r <= c), 1.0, 0.0).astype(F32)


def _dot_f32(a, b):
    return lax.dot_general(a, b, (((1,), (0,)), ((), ())), preferred_element_type=F32,
                           precision=lax.Precision.HIGHEST)


def _fgate_fwd(flog, bias, B, S, ch):
    def body(flog_ref, b_ref, cum_ref):
        tri = _tri(ch, True)
        carry = jnp.zeros((1, LANES), F32)
        for c0 in range(0, S, ch):
            lf = _log_sigmoid(flog_ref[c0:c0 + ch, :] + b_ref[...])
            cs = _dot_f32(tri, lf) + carry
            cum_ref[c0:c0 + ch, :] = cs
            carry = cs[ch - 1:ch, :]

    return pl.pallas_call(
        body, name="fgate_fwd", grid=(B,),
        in_specs=[pl.BlockSpec((S, LANES), lambda b: (b, 0)),
                  pl.BlockSpec((1, LANES), lambda b: (0, 0))],
        out_specs=pl.BlockSpec((S, LANES), lambda b: (b, 0)),
        out_shape=jax.ShapeDtypeStruct((B * S, LANES), F32),
        compiler_params=_params(("arbitrary",)),
    )(flog, bias)


def _fgate_bwd(dcum, flog, bias, B, S, ch):
    def body(dcum_ref, flog_ref, b_ref, dflog_ref, db_ref):
        @pl.when(pl.program_id(0) == 0)
        def _():
            db_ref[...] = jnp.zeros_like(db_ref)

        tri = _tri(ch, False)
        carry = jnp.zeros((1, LANES), F32)
        db = jnp.zeros((1, LANES), F32)
        for c0 in range(S - ch, -1, -ch):
            dlf = _dot_f32(tri, dcum_ref[c0:c0 + ch, :]) + carry
            carry = dlf[0:1, :]
            z = flog_ref[c0:c0 + ch, :] + b_ref[...]
            dz = dlf * _sigmoid(-z)
            dflog_ref[c0:c0 + ch, :] = dz
            db = db + jnp.sum(dz, axis=0, keepdims=True)
        db_ref[...] += db

    return pl.pallas_call(
        body, name="fgate_bwd", grid=(B,),
        in_specs=[pl.BlockSpec((S, LANES), lambda b: (b, 0)),
                  pl.BlockSpec((S, LANES), lambda b: (b, 0)),
                  pl.BlockSpec((1, LANES), lambda b: (0, 0))],
        out_specs=[pl.BlockSpec((S, LANES), lambda b: (b, 0)),
                   pl.BlockSpec((1, LANES), lambda b: (0, 0))],
        out_shape=[jax.ShapeDtypeStruct((B * S, LANES), F32),
                   jax.ShapeDtypeStruct((1, LANES), F32)],
        compiler_params=_params(("arbitrary",)),
    )(dcum, flog, bias)


def _pick_lane(tile, h):
    lane = lax.broadcasted_iota(jnp.int32, tile.shape, 1)
    return jnp.sum(jnp.where(lane == h, tile, 0.0), axis=1, keepdims=True)


def _put_lane(col, h, width=LANES):
    lane = lax.broadcasted_iota(jnp.int32, (col.shape[0], width), 1)
    return jnp.where(lane == h, col, 0.0)


def _pick_row(tile, h):
    row = lax.broadcasted_iota(jnp.int32, tile.shape, 0)
    return jnp.sum(jnp.where(row == h, tile, 0.0), axis=0, keepdims=True)


def _put_row(vec, h):
    row = lax.broadcasted_iota(jnp.int32, (8, vec.shape[1]), 0)
    return jnp.where(row == h, vec, 0.0)


def _causal(tq):
    r = lax.broadcasted_iota(jnp.int32, (tq, tq), 0)
    c = lax.broadcasted_iota(jnp.int32, (tq, tq), 1)
    return r >= c


def _head_halves(t):
    lo = lax.broadcasted_iota(jnp.int32, t.shape, 1) < HEAD_DIM
    zero = jnp.zeros_like(t)
    return jnp.where(lo, t, zero), jnp.where(lo, zero, t)


NEG = -1e30
ATTN_SCALE = 1.0 / math.sqrt(HEAD_DIM)


def _scaled(q):
    return (q.astype(F32) * ATTN_SCALE).astype(q.dtype)


def _attn_fwd(proj, cum, cum_t, B, S, tq, comm=None):
    nq = S // tq

    def body(q_ref, k_ref, v_ref, cum_ref, cumt_ref, o_ref, lse_ref):
        qi, hp = pl.program_id(1), pl.program_id(2)
        qm = _head_halves(_scaled(q_ref[...]))
        cumv = cum_ref[...]
        cq = [_pick_lane(cumv, 2 * hp + e) for e in range(2)]

        def tile(j, carry, masked):
            off = pl.multiple_of(j * tq, tq)
            kj = k_ref[pl.ds(off, tq), :]
            vj = v_ref[pl.ds(off, tq), :]
            ct = cumt_ref[j]
            new = []
            for e in range(2):
                m, l, acc = carry[e]
                s = _dot_nt(qm[e], kj) - _pick_row(ct, 2 * hp + e)
                if masked:
                    s = jnp.where(_causal(tq), s, NEG)
                m_new = jnp.maximum(m, jnp.max(s, axis=1, keepdims=True))
                p = jnp.exp(s - m_new)
                alpha = jnp.exp(m - m_new)
                l = alpha * l + jnp.sum(p, axis=1, keepdims=True)
                acc = alpha * acc + _dot(p.astype(BF16), vj)
                new.append((m_new, l, acc))
            return tuple(new)

        one = (jnp.full((tq, 1), NEG, F32), jnp.zeros((tq, 1), F32), jnp.zeros((tq, LANES), F32))
        carry = lax.fori_loop(0, qi, lambda j, c: tile(j, c, False), (one, one))
        (ma, la, acca), (mb, lb, accb) = tile(qi, carry, True)
        lo = lax.broadcasted_iota(jnp.int32, (tq, LANES), 1) < HEAD_DIM
        o_ref[...] = jnp.where(lo, acca / la, accb / lb).astype(BF16)

        @pl.when(hp == 0)
        def _():
            lse_ref[...] = jnp.zeros_like(lse_ref)

        lse_ref[...] += (_put_lane(ma + jnp.log(la) + cq[0], 2 * hp) + _put_lane(mb + jnp.log(lb) + cq[1], 2 * hp + 1))

    kv = lambda first: pl.BlockSpec((S, LANES), lambda b, i, hp: (b, first + hp))
    return _pallas(
        body, "attn_fwd", (B, nq, HEAD_PAIRS),
        [pl.BlockSpec((tq, LANES), lambda b, i, hp: (b * nq + i, hp)),
         kv(ATTN_W // LANES), kv(2 * ATTN_W // LANES),
         pl.BlockSpec((tq, LANES), lambda b, i, hp: (b * nq + i, 0)),
         pl.BlockSpec((None, nq, 8, tq), lambda b, i, hp: (b, 0, 0, 0))],
        [pl.BlockSpec((tq, LANES), lambda b, i, hp: (b * nq + i, hp)),
         pl.BlockSpec((tq, LANES), lambda b, i, hp: (b * nq + i, 0))],
        [jax.ShapeDtypeStruct((B * S, ATTN_W), BF16), jax.ShapeDtypeStruct((B * S, LANES), F32)],
        [], (proj, proj, proj, cum, cum_t), comm)


def _attn_bwd(proj, o, do, lse, cum, cum_t, B, S, tq, comm=None):
    nq = S // tq

    def body(q_ref, k_ref, v_ref, o_ref, do_ref, lse_ref, cum_ref, cumt_ref,
             dq_ref, dk_ref, dv_ref, dcq_ref, dck_ref, dq_scr):
        hp, kj = pl.program_id(1), pl.program_id(2)

        @pl.when(kj == 0)
        def _():
            dq_scr[...] = jnp.zeros_like(dq_scr)

        @pl.when((kj == 0) & (hp == 0))
        def _():
            dcq_ref[...] = jnp.zeros_like(dcq_ref)
            dck_ref[...] = jnp.zeros_like(dck_ref)

        kv = k_ref[...]
        vv = v_ref[...]
        km = _head_halves(kv)
        ct = cumt_ref[...]
        ck = [_pick_row(ct, 2 * hp + e) for e in range(2)]

        def tile(i, carry, masked):
            dk, dv, dcol = carry
            off = pl.multiple_of(i * tq, tq)
            qi = q_ref[pl.ds(off, tq), :]
            ov = o_ref[pl.ds(off, tq), :].astype(F32)
            qm = _head_halves(_scaled(qi))
            dom = _head_halves(do_ref[pl.ds(off, tq), :])
            cumv = cum_ref[pl.ds(off, tq), :]
            lsev = lse_ref[pl.ds(off, tq), :]
            dcq = jnp.zeros((tq, LANES), F32)
            dq = jnp.zeros((tq, LANES), F32)
            dcol_new = []
            for e in range(2):
                delta = jnp.sum(dom[e].astype(F32) * ov, axis=1, keepdims=True)
                row_term = _pick_lane(cumv, 2 * hp + e) - _pick_lane(lsev, 2 * hp + e)
                p = jnp.exp(_dot_nt(qm[e], kv) + row_term - ck[e])
                if masked:
                    p = jnp.where(_causal(tq), p, 0.0)
                dv = dv + _dot_tn(dom[e], p.astype(BF16))
                ds = p * (_dot_nt(dom[e], vv) - delta)
                dcol_new.append(dcol[e] + jnp.sum(ds, axis=0, keepdims=True))
                dcq = dcq + _put_lane(jnp.sum(ds, axis=1, keepdims=True), 2 * hp + e)
                dsb = ds.astype(BF16)
                dk = dk + _dot_tn(qm[e], dsb)
                dq = dq + _dot(dsb, km[e]) * ATTN_SCALE
            dq_scr[pl.ds(off, tq), :] += dq
            dcq_ref[pl.ds(off, tq), :] += dcq
            return dk, dv, tuple(dcol_new)

        zero_row = jnp.zeros((1, tq), F32)
        init = (jnp.zeros((LANES, tq), F32), jnp.zeros((LANES, tq), F32), (zero_row, zero_row))
        carry = tile(kj, init, True)
        dk, dv, dcol = lax.fori_loop(kj + 1, nq, lambda i, c: tile(i, c, False), carry)
        dk_ref[...] = dk.T.astype(BF16)
        dv_ref[...] = dv.T.astype(BF16)
        dck_ref[kj] += -(_put_row(dcol[0], 2 * hp) + _put_row(dcol[1], 2 * hp + 1))

        @pl.when(kj == nq - 1)
        def _():
            dq_ref[...] = dq_scr[...].astype(BF16)

    seq = lambda first: pl.BlockSpec((S, LANES), lambda b, hp, j: (b, first + hp))
    tile_in = lambda first: pl.BlockSpec((tq, LANES), lambda b, hp, j: (b * nq + j, first + hp))
    lanes0 = pl.BlockSpec((S, LANES), lambda b, hp, j: (b, 0))
    out = jax.ShapeDtypeStruct((B * S, ATTN_W), BF16)
    return _pallas(
        body, "attn_bwd", (B, HEAD_PAIRS, nq),
        [seq(0), tile_in(ATTN_W // LANES), tile_in(2 * ATTN_W // LANES), seq(0), seq(0), lanes0, lanes0,
         pl.BlockSpec((None, None, 8, tq), lambda b, hp, j: (b, j, 0, 0))],
        [seq(0), tile_in(0), tile_in(0), lanes0,
         pl.BlockSpec((None, nq, 8, tq), lambda b, hp, j: (b, 0, 0, 0))],
        [out, out, out, jax.ShapeDtypeStruct((B * S, LANES), F32), jax.ShapeDtypeStruct((B, nq, 8, tq), F32)],
        [pltpu.VMEM((S, LANES), F32)],
        (proj, proj, proj, o, do, lse, cum, cum_t), comm)


def _shift_down(u, n):
    row = lax.broadcasted_iota(jnp.int32, u.shape, 0)
    return jnp.where(row >= n, pltpu.roll(u, n, 0), 0.0)


def _shift_up(u, n):
    rows = u.shape[0]
    row = lax.broadcasted_iota(jnp.int32, u.shape, 0)
    return jnp.where(row < rows - n, pltpu.roll(u, rows - n, 0), 0.0)


def _conv_specs(S):
    cb = pl.BlockSpec((S, LANES), lambda g, b: (b, COL_CB // LANES + g))
    cc = pl.BlockSpec((S, LANES), lambda g, b: (b, COL_CC // LANES + g))
    cx = pl.BlockSpec((S, LANES), lambda g, b: (b, COL_CX // LANES + g))
    w = pl.BlockSpec((8, LANES), lambda g, b: (0, g))
    return cb, cc, cx, w


def _conv_fwd(proj, conv_w, B, S):
    def body(cb_ref, cc_ref, cx_ref, w_ref, y_ref):
        u = cc_ref[...].astype(F32) * cx_ref[...].astype(F32)
        w = w_ref[...]
        conv = w[0:1, :] * _shift_down(u, 2) + w[1:2, :] * _shift_down(u, 1) + w[2:3, :] * u
        y_ref[...] = (cb_ref[...].astype(F32) * conv).astype(BF16)

    cb, cc, cx, w = _conv_specs(S)
    return pl.pallas_call(
        body, name="conv_fwd", grid=(CONV_W // LANES, B),
        in_specs=[cb, cc, cx, w],
        out_specs=pl.BlockSpec((S, LANES), lambda g, b: (b, g)),
        out_shape=jax.ShapeDtypeStruct((B * S, CONV_W), BF16),
        compiler_params=_params(("arbitrary", "arbitrary")),
    )(proj, proj, proj, conv_w)


def _conv_bwd(dy, proj, conv_w, B, S):
    def body(dy_ref, cb_ref, cc_ref, cx_ref, w_ref, dcb_ref, dcc_ref, dcx_ref, dw_ref):
        @pl.when(pl.program_id(1) == 0)
        def _():
            dw_ref[...] = jnp.zeros_like(dw_ref)

        ccv = cc_ref[...].astype(F32)
        cxv = cx_ref[...].astype(F32)
        u = ccv * cxv
        u1 = _shift_down(u, 1)
        u2 = _shift_down(u, 2)
        w = w_ref[...]
        conv = w[0:1, :] * u2 + w[1:2, :] * u1 + w[2:3, :] * u
        dyv = dy_ref[...].astype(F32)
        dcb_ref[...] = (dyv * conv).astype(BF16)
        dconv = dyv * cb_ref[...].astype(F32)
        du = w[2:3, :] * dconv + w[1:2, :] * _shift_up(dconv, 1) + w[0:1, :] * _shift_up(dconv, 2)
        dcc_ref[...] = (du * cxv).astype(BF16)
        dcx_ref[...] = (du * ccv).astype(BF16)
        row = lax.broadcasted_iota(jnp.int32, (8, LANES), 0)
        dw = jnp.where(row == 0, jnp.sum(dconv * u2, axis=0, keepdims=True),
                       jnp.where(row == 1, jnp.sum(dconv * u1, axis=0, keepdims=True),
                                 jnp.where(row == 2, jnp.sum(dconv * u, axis=0, keepdims=True), 0.0)))
        dw_ref[...] += dw

    cb, cc, cx, w = _conv_specs(S)
    out = pl.BlockSpec((S, LANES), lambda g, b: (b, g))
    return pl.pallas_call(
        body, name="conv_bwd", grid=(CONV_W // LANES, B),
        in_specs=[out, cb, cc, cx, w],
        out_specs=[out, out, out, w],
        out_shape=[jax.ShapeDtypeStruct((B * S, CONV_W), BF16)] * 3 + [jax.ShapeDtypeStruct((8, CONV_W), F32)],
        compiler_params=_params(("arbitrary", "arbitrary")),
    )(dy, proj, proj, proj, conv_w)


def _gate_specs(tm, D):
    ga = pl.BlockSpec((tm, D), lambda i: (i, COL_GATES // D))
    gc = pl.BlockSpec((tm, D), lambda i: (i, COL_GATES // D + 1))
    return ga, gc


def _mix_out_fwd(x, o, yc, proj, woa, woc, wout, tm):
    T, D = x.shape

    def body(x_ref, o_ref, yc_ref, ga_ref, gc_ref, woa_ref, woc_ref, wout_ref, out_ref):
        ya = _dot(o_ref[...], woa_ref[...])
        yp = _dot(yc_ref[...], woc_ref[...])
        merged = _sigmoid(ga_ref[...].astype(F32)) * ya + _sigmoid(gc_ref[...].astype(F32)) * yp
        out_ref[...] = x_ref[...] + _dot(merged.astype(BF16), wout_ref[...])

    ga, gc = _gate_specs(tm, D)
    row = lambda w: pl.BlockSpec((tm, w), lambda i: (i, 0))
    whole = lambda a: pl.BlockSpec(a.shape, lambda i: (0, 0))
    return pl.pallas_call(
        body, name="mix_out_fwd", grid=(T // tm,),
        in_specs=[row(D), row(ATTN_W), row(CONV_W), ga, gc, whole(woa), whole(woc), whole(wout)],
        out_specs=row(D),
        out_shape=jax.ShapeDtypeStruct((T, D), F32),
        compiler_params=_params(("arbitrary",)),
    )(x, o, yc, proj, proj, woa, woc, wout)


def _mix_out_bwd(dx, o, yc, proj, woa, woc, wout, tm, comm=None):
    T, D = dx.shape
    nt = T // tm

    def body(dx_ref, o_ref, yc_ref, ga_ref, gc_ref, woa_ref, woc_ref, wout_ref,
             do_ref, dyc_ref, dg_ref, dwoa_ref, dwoc_ref, dwout_ref, acca, accc, acco):
        t = pl.program_id(0)

        @pl.when(t == 0)
        def _():
            acca[...] = jnp.zeros_like(acca)
            accc[...] = jnp.zeros_like(accc)
            acco[...] = jnp.zeros_like(acco)

        dxb = dx_ref[...].astype(BF16)
        ov, ycv = o_ref[...], yc_ref[...]
        ya = _dot(ov, woa_ref[...])
        yp = _dot(ycv, woc_ref[...])
        sa = _sigmoid(ga_ref[...].astype(F32))
        sc = _sigmoid(gc_ref[...].astype(F32))
        merged = (sa * ya + sc * yp).astype(BF16)
        dm = _dot_nt(dxb, wout_ref[...])
        dya = (dm * sa).astype(BF16)
        dyp = (dm * sc).astype(BF16)
        dg_ref[:, :D] = (dm * ya * sa * (1.0 - sa)).astype(BF16)
        dg_ref[:, D:] = (dm * yp * sc * (1.0 - sc)).astype(BF16)
        do_ref[...] = _dot_nt(dya, woa_ref[...]).astype(BF16)
        dyc_ref[...] = _dot_nt(dyp, woc_ref[...]).astype(BF16)
        acca[...] += _dot_tn(ov, dya)
        accc[...] += _dot_tn(ycv, dyp)
        acco[...] += _dot_tn(merged, dxb)

        @pl.when(t == nt - 1)
        def _():
            dwoa_ref[...] = acca[...].astype(BF16)
            dwoc_ref[...] = accc[...].astype(BF16)
            dwout_ref[...] = acco[...].astype(BF16)

    ga, gc = _gate_specs(tm, D)
    row = lambda w: pl.BlockSpec((tm, w), lambda i: (i, 0))
    whole = lambda a: pl.BlockSpec(a.shape, lambda i: (0, 0))
    return _pallas(
        body, "mix_out_bwd", (nt,),
        [row(D), row(ATTN_W), row(CONV_W), ga, gc, whole(woa), whole(woc), whole(wout)],
        [row(ATTN_W), row(CONV_W), row(2 * D), whole(woa), whole(woc), whole(wout)],
        [jax.ShapeDtypeStruct((T, ATTN_W), BF16), jax.ShapeDtypeStruct((T, CONV_W), BF16),
         jax.ShapeDtypeStruct((T, 2 * D), BF16),
         jax.ShapeDtypeStruct(woa.shape, BF16), jax.ShapeDtypeStruct(woc.shape, BF16),
         jax.ShapeDtypeStruct(wout.shape, BF16)],
        [pltpu.VMEM(woa.shape, F32), pltpu.VMEM(woc.shape, F32), pltpu.VMEM(wout.shape, F32)],
        (dx, o, yc, proj, proj, woa, woc, wout), comm)


def _proj_pieces(dq, dk, dv, dcb, dcc, dcx, dgates, dflog):
    D = dgates.shape[1] // 2
    return [(dq, ATTN_W, 0), (dk, ATTN_W, 0), (dv, ATTN_W, 0), (dcb, CONV_W, 0), (dcc, CONV_W, 0), (dcx, CONV_W, 0),
            (dgates, D, 0), (dgates, D, 1), (dflog, LANES, 0)]


def _mix_proj_bwd_dx(dres, x, g, pieces, wproj_t, wf_t, tm, comm=None):
    T, D = x.shape
    n = len(pieces)
    w_blocks = [(ATTN_W, 0), (ATTN_W, 1), (ATTN_W, 2), (CONV_W, 3), (CONV_W, 4), (CONV_W, 5),
                (D, COL_GATES // D), (D, COL_GATES // D + 1)]

    def body(*refs):
        dres_ref, x_ref, g_ref = refs[:3]
        p_refs, w_refs = refs[3:3 + n], refs[3 + n:3 + 2 * n]
        dx_ref, dg_ref = refs[3 + 2 * n:]

        @pl.when(pl.program_id(0) == 0)
        def _():
            dg_ref[...] = jnp.zeros_like(dg_ref)

        dh = _dot(p_refs[0][...].astype(BF16), w_refs[0][...])
        for p_ref, w_ref in zip(p_refs[1:], w_refs[1:]):
            dh = dh + _dot(p_ref[...].astype(BF16), w_ref[...])
        xhat, inv = _rms(x_ref[...])
        dx, dg = _rms_bwd(dh, xhat, inv, g_ref[...])
        dx_ref[...] = dres_ref[...] + dx
        dg_ref[...] += dg

    row = pl.BlockSpec((tm, D), lambda i: (i, 0))
    vec = pl.BlockSpec((1, D), lambda i: (0, 0))
    p_specs = [pl.BlockSpec((tm, w), lambda i, cb=cb: (i, cb)) for _, w, cb in pieces]
    w_specs = [pl.BlockSpec((r, D), lambda i, rb=rb: (rb, 0)) for r, rb in w_blocks]
    w_specs.append(pl.BlockSpec((LANES, D), lambda i: (0, 0)))
    return _pallas(
        body, "mix_proj_bwd_dx", (T // tm,),
        [row, row, vec] + p_specs + w_specs, [row, vec],
        [jax.ShapeDtypeStruct((T, D), F32), jax.ShapeDtypeStruct((1, D), F32)], [],
        (dres, x, g, *[p for p, _, _ in pieces], *([wproj_t] * len(w_blocks)), wf_t), comm)


def _matmuls_tn(name, pieces, b, tk):
    T, N = b.shape
    nt = T // tk
    n = len(pieces)

    def body(*refs):
        a_refs, b_ref, out_refs, accs = refs[:n], refs[n], refs[n + 1:2 * n + 1], refs[2 * n + 1:]
        t = pl.program_id(0)

        @pl.when(t == 0)
        def _():
            for acc in accs:
                acc[...] = jnp.zeros_like(acc)

        bv = b_ref[...]
        for a_ref, acc in zip(a_refs, accs):
            acc[...] += _dot_tn(a_ref[...].astype(BF16), bv)

        @pl.when(t == nt - 1)
        def _():
            for out_ref, acc in zip(out_refs, accs):
                out_ref[...] = acc[...].astype(BF16)

    return pl.pallas_call(
        body, name=name, grid=(nt,),
        in_specs=[pl.BlockSpec((tk, w), lambda t, cb=cb: (t, cb)) for _, w, cb in pieces]
        + [pl.BlockSpec((tk, N), lambda t: (t, 0))],
        out_specs=[pl.BlockSpec((w, N), lambda t: (0, 0)) for _, w, _ in pieces],
        out_shape=[jax.ShapeDtypeStruct((w, N), BF16) for _, w, _ in pieces],
        scratch_shapes=[pltpu.VMEM((w, N), F32) for _, w, _ in pieces],
        compiler_params=_params(("arbitrary",)),
    )(*[a for a, _, _ in pieces], b)


def _final_loss(x, target, g, tm):
    T, D = x.shape

    def body(x_ref, t_ref, g_ref, dx_ref, loss_ref, dg_ref):
        @pl.when(pl.program_id(0) == 0)
        def _():
            loss_ref[...] = jnp.zeros_like(loss_ref)
            dg_ref[...] = jnp.zeros_like(dg_ref)

        xhat, inv = _rms(x_ref[...])
        err = xhat * g_ref[...] - t_ref[...]
        loss_ref[...] += 0.5 * jnp.sum(jnp.sum(err * err, axis=1, keepdims=True), axis=0, keepdims=True) / D
        dx, dg = _rms_bwd(err * (1.0 / D), xhat, inv, g_ref[...])
        dx_ref[...] = dx
        dg_ref[...] += dg

    row = pl.BlockSpec((tm, D), lambda i: (i, 0))
    return pl.pallas_call(
        body, name="final_loss", grid=(T // tm,),
        in_specs=[row, row, pl.BlockSpec((1, D), lambda i: (0, 0))],
        out_specs=[row, pl.BlockSpec((1, LANES), lambda i: (0, 0)), pl.BlockSpec((1, D), lambda i: (0, 0))],
        out_shape=[jax.ShapeDtypeStruct((T, D), F32), jax.ShapeDtypeStruct((1, LANES), F32),
                   jax.ShapeDtypeStruct((1, D), F32)],
        compiler_params=_params(("arbitrary",)),
    )(x, target, g)


class _LocalPlan:
    def __init__(self, stacks, small):
        self.stacks, self.small, self.grads = stacks, small, {}

    def weights(self, group):
        return _LAYOUTS[group](self.stacks, self.small)

    def rider(self, kernel_name):
        return None

    def arrived(self, kernel_name, results):
        pass

    def reduce(self, group, grads):
        self.grads.update(grads)

    def reduce_small(self, small_grads, loss):
        pass


def _local_step(x, target, plan, B, S):
    T, D = x.shape
    tm = min(512, T)
    tm_fwd = min(1024, T)
    tq = min(512, S)
    nq = S // tq
    ch = min(256, S)

    def riding(kernel_name, build):
        results, brought = build(plan.rider(kernel_name))
        plan.arrived(kernel_name, brought)
        return results

    w1 = plan.weights("ffn1_in")
    hg1, hu1, n1 = riding("ffn1_up", lambda comm: _ffn_up(
        "ffn1_up", x, w1["ffn1_norm"], w1["ffn1_gate"], w1["ffn1_up"], tm_fwd, comm))
    w1 = plan.weights("ffn1")
    x1, = riding("ffn1_down", lambda comm: _ffn_down("ffn1_down", x, hg1, hu1, w1["ffn1_down"], tm_fwd, comm))
    wm = plan.weights("mix")
    h, proj, flog = _mix_proj_fwd(x1, wm["mix_norm"], wm["w_proj"], wm["w_f"], tm_fwd, 1280)
    cum = _fgate_fwd(flog, wm["b_forget"], B, S, ch)
    cum_t = jnp.transpose(cum[:, :N_HEADS].reshape(B, nq, tq, N_HEADS), (0, 1, 3, 2))
    o, lse = riding("attn_fwd", lambda comm: _attn_fwd(proj, cum, cum_t, B, S, tq, comm))
    yc = _conv_fwd(proj, wm["conv_w"], B, S)
    x2 = _mix_out_fwd(x1, o, yc, proj, wm["w_o_attn"], wm["w_o_conv"], wm["w_out"], tm)
    w2 = plan.weights("ffn2")
    x3, hg2, hu2, n2 = _ffn_fwd("ffn2_fwd", x2, w2["ffn2_norm"], w2["ffn2_gate"], w2["ffn2_up"], w2["ffn2_down"], tm_fwd)[0]
    dx3, loss, d_final_norm = _final_loss(x3, target, w2["final_norm"], tm)

    g = {"final_norm": d_final_norm}
    dx2, dhg2, dhu2, g["ffn2_norm"], df2 = _ffn_bwd_dx("ffn2_bwd_dx", dx3, x2, w2["ffn2_norm"], hg2, hu2,
                                                  w2["ffn2_gate"], w2["ffn2_up"], w2["ffn2_down"], tm_fwd)[0]
    plan.reduce("ffn2", dict(zip(("ffn2_gate", "ffn2_up", "ffn2_down"),
                                 _ffn_bwd_dw("ffn2_bwd_dw", n2, df2, hg2, hu2, dhg2, dhu2, tm)[0])))
    do, dyc, dgates, dwoa, dwoc, dwout = riding("mix_out_bwd", lambda comm: _mix_out_bwd(
        dx2, o, yc, proj, wm["w_o_attn"], wm["w_o_conv"], wm["w_out"], tm, comm))
    plan.reduce("out", dict(w_o_attn=_shard_cols(dwoa), w_o_conv=_shard_cols(dwoc), w_out=dwout.reshape(N_CHIPS, -1, D)))
    dq, dk, dv, dcq, dck = riding("attn_bwd", lambda comm: _attn_bwd(proj, o, do, lse, cum, cum_t, B, S, tq, comm))
    dcum = dcq + jnp.pad(jnp.transpose(dck, (0, 1, 3, 2)).reshape(T, N_HEADS), ((0, 0), (0, LANES - N_HEADS)))
    dflog, g["b_forget"] = _fgate_bwd(dcum, flog, wm["b_forget"], B, S, ch)
    dcb, dcc, dcx, g["conv_w"] = _conv_bwd(dyc, proj, wm["conv_w"], B, S)
    pieces = _proj_pieces(dq, dk, dv, dcb, dcc, dcx, dgates, dflog)
    dwq, dwk, dwv, dwcb, dwcc, dwcx = _matmuls_tn("mix_dw_a", pieces[:6], h, tm)
    dwga, dwgc, dwf = _matmuls_tn("mix_dw_b", pieces[6:], h, tm)
    dwin_t = jnp.concatenate([dwq, dwk, dwv, dwf[:N_HEADS], dwcb, dwcc, dwcx, dwga, dwgc], axis=0)
    plan.reduce("w_in", {"w_in": dwin_t.reshape(N_CHIPS, -1, D)})
    dx1, g["mix_norm"] = riding("mix_proj_bwd_dx", lambda comm: _mix_proj_bwd_dx(
        dx2, x1, wm["mix_norm"], pieces, wm["w_proj"], wm["w_f"], min(256, T), comm))
    grad_x, dhg1, dhu1, g["ffn1_norm"], df1 = _ffn_bwd_dx(
        "ffn1_bwd_dx", dx1, x, w1["ffn1_norm"], hg1, hu1, w1["ffn1_gate"], w1["ffn1_up"], w1["ffn1_down"], tm_fwd)[0]
    plan.reduce_small(g, loss)
    plan.reduce("ffn1", dict(zip(("ffn1_gate", "ffn1_up", "ffn1_down"), riding("ffn1_bwd_dw", lambda comm: _ffn_bwd_dw(
        "ffn1_bwd_dw", n1, df1, hg1, hu1, dhg1, dhu1, tm, comm)))))
    return loss, grad_x, g


TRANSPOSED = ("ffn1_gate", "ffn1_up", "ffn2_gate", "ffn2_up", "w_in")
NORMS = ("ffn1_norm", "mix_norm", "ffn2_norm", "final_norm")


def _unshard_cols(a):
    return jnp.transpose(a, (1, 0, 2)).reshape(a.shape[1], N_CHIPS * a.shape[2])


def _shard_cols(a):
    return jnp.transpose(a.reshape(a.shape[0], N_CHIPS, a.shape[1] // N_CHIPS), (1, 0, 2))


def _layout_ffn(which):
    def layout(st, small):
        w = {n: st[n] for n in (which + "_gate", which + "_up", which + "_down")}
        w[which + "_norm"] = small[which + "_norm"].reshape(1, -1)
        if which == "ffn2":
            w["final_norm"] = small["final_norm"].reshape(1, -1)
        return w
    return layout


def _layout_mix(st, small):
    win_t = st["w_in"].reshape(-1, st["w_in"].shape[2])
    return {
        "w_proj": jnp.concatenate([win_t[:N_FORGET_COL], win_t[N_FORGET_COL + N_HEADS:]], axis=0),
        "w_f": jnp.pad(win_t[N_FORGET_COL:N_FORGET_COL + N_HEADS], ((0, LANES - N_HEADS), (0, 0))),
        "w_o_attn": _unshard_cols(st["w_o_attn"]),
        "w_o_conv": _unshard_cols(st["w_o_conv"]),
        "w_out": st["w_out"].reshape(-1, st["w_out"].shape[2]),
        "conv_w": _unshard_cols(st["conv_w"]),
        "mix_norm": small["mix_norm"].reshape(1, -1),
        "b_forget": jnp.pad(small["b_forget"].reshape(1, -1), ((0, 0), (0, LANES - N_HEADS))),
    }


def _layout_ffn1_in(st, small):
    return {"ffn1_gate": st["ffn1_gate"], "ffn1_up": st["ffn1_up"], "ffn1_norm": small["ffn1_norm"].reshape(1, -1)}


_LAYOUTS = {"ffn1_in": _layout_ffn1_in, "ffn1": _layout_ffn("ffn1"), "mix": _layout_mix, "ffn2": _layout_ffn("ffn2")}


ANY = pl.BlockSpec(memory_space=pl.ANY)
BIG = ("ffn1_gate", "ffn1_up", "ffn1_down", "w_in", "w_o_attn", "w_o_conv", "w_out",
       "ffn2_gate", "ffn2_up", "ffn2_down")


def _place():
    x, y, c = lax.axis_index("x"), lax.axis_index("y"), lax.axis_index("c")
    others = [(1 - x, y), (x, 1 - y), (1 - x, 1 - y)]
    return x, y, c, others


def _col_halves(cols, c):
    hc = cols // 2
    return pl.ds(pl.multiple_of(c * hc, LANES), hc), pl.ds(pl.multiple_of((1 - c) * hc, LANES), hc)


def _gather_comm(shards, conv_shard=None):
    n = len(shards)
    inputs = list(shards) + ([] if conv_shard is None else [conv_shard])

    def copies(ins, outs, sems):
        send_sems, recv_sems, pass_send, pass_recv = sems[:4]
        x, y, c, others = _place()

        def chip_copy(a, j, chip):
            mine, _ = _col_halves(ins[a].shape[1], c)
            return pltpu.make_async_remote_copy(
                src_ref=ins[a].at[:, mine], dst_ref=outs[a].at[chip, :, mine],
                send_sem=send_sems.at[3 * a + j], recv_sem=recv_sems.at[3 * a + j],
                device_id=(*others[j], c), device_id_type=MESH)

        def pass_copy(a, j, chip, half):
            return pltpu.make_async_remote_copy(
                src_ref=outs[a].at[chip, :, half], dst_ref=outs[a].at[chip, :, half],
                send_sem=pass_send.at[3 * a + j], recv_sem=pass_recv.at[3 * a + j],
                device_id=(x, y, 1 - c), device_id_type=MESH)

        def conv_copy(j, chip):
            return pltpu.make_async_remote_copy(
                src_ref=ins[n], dst_ref=outs[n].at[chip],
                send_sem=sems[4].at[j], recv_sem=sems[5].at[j],
                device_id=(*others[j], c), device_id_type=MESH)

        me = 2 * x + y
        sends = [chip_copy(a, j, me) for a in range(n) for j in range(3)]
        if conv_shard is not None:
            sends += [conv_copy(j, me) for j in range(3)]
        return c, others, sends, chip_copy, pass_copy, conv_copy

    def start(ins, outs, sems):
        for cp in copies(ins, outs, sems)[2]:
            cp.start()

    def finish(ins, outs, sems):
        c, others, sends, chip_copy, pass_copy, conv_copy = copies(ins, outs, sems)
        passed = []
        for a in range(n):
            mine, _ = _col_halves(ins[a].shape[1], c)
            for j, (ox, oy) in enumerate(others):
                chip_copy(a, j, 2 * ox + oy).wait_recv()
                passed.append(pass_copy(a, j, 2 * ox + oy, mine))
                passed[-1].start()
        for a in range(n):
            _, theirs = _col_halves(ins[a].shape[1], c)
            for j, (ox, oy) in enumerate(others):
                pass_copy(a, j, 2 * ox + oy, theirs).wait_recv()
        if conv_shard is not None:
            for j, (ox, oy) in enumerate(others):
                conv_copy(j, 2 * ox + oy).wait_recv()
        for cp in sends + passed:
            cp.wait_send()

    scratch = [pltpu.SemaphoreType.DMA((3 * n,))] * 4
    if conv_shard is not None:
        scratch += [pltpu.SemaphoreType.DMA((3,))] * 2
    return _Comm(inputs, [jax.ShapeDtypeStruct((N_CHIPS,) + s.shape, s.dtype) for s in inputs], scratch, start, finish)


def _fill_own(stacks, shards):
    chip = 2 * lax.axis_index("x") + lax.axis_index("y")
    return [lax.dynamic_update_index_in_dim(st, s, chip, 0) for st, s in zip(stacks, shards)]


def _run_comm(name, comm):
    ci, co = len(comm.inputs), len(comm.out_shape)

    def body(*refs):
        comm.start(refs[:ci], refs[ci:ci + co], refs[ci + co:])
        comm.finish(refs[:ci], refs[ci:ci + co], refs[ci + co:])

    return pl.pallas_call(body, name=name, in_specs=[ANY] * ci, out_specs=[ANY] * co, out_shape=comm.out_shape,
                          scratch_shapes=comm.scratch)(*comm.inputs)


def _sibling_exchange_comm(grads):
    n = len(grads)

    def copies(ins, outs, sems):
        x, y, c, _ = _place()
        return [pltpu.make_async_remote_copy(
            src_ref=ins[a].at[:, :, _col_halves(ins[a].shape[2], c)[1]], dst_ref=outs[a],
            send_sem=sems[0].at[a], recv_sem=sems[1].at[a],
            device_id=(x, y, 1 - c), device_id_type=MESH) for a in range(n)]

    def start(ins, outs, sems):
        for cp in copies(ins, outs, sems):
            cp.start()

    def finish(ins, outs, sems):
        for cp in copies(ins, outs, sems):
            cp.wait()

    half = lambda s: jax.ShapeDtypeStruct((s.shape[0], s.shape[1], s.shape[2] // 2), s.dtype)
    return _Comm(grads, [half(s) for s in grads], [pltpu.SemaphoreType.DMA((n,))] * 2, start, finish)


def _merge_comms(comms):
    def split(refs, count):
        out, at = [], 0
        for cm in comms:
            out.append(refs[at:at + count(cm)])
            at += count(cm)
        return out

    def parts(ins, outs, sems):
        return zip(comms, split(ins, lambda cm: len(cm.inputs)), split(outs, lambda cm: len(cm.out_shape)),
                   split(sems, lambda cm: len(cm.scratch)))

    def start(ins, outs, sems):
        for cm, i, o, s in parts(ins, outs, sems):
            cm.start(i, o, s)

    def finish(ins, outs, sems):
        for cm, i, o, s in parts(ins, outs, sems):
            cm.finish(i, o, s)

    return _Comm(sum([cm.inputs for cm in comms], []), sum([cm.out_shape for cm in comms], []),
                 sum([cm.scratch for cm in comms], []), start, finish)


def _add_halves(name, grads, recvs, core):
    n = len(grads)

    def body(core_ref, *refs):
        for g_ref, r_ref, out_ref in zip(refs[:n], refs[n:2 * n], refs[2 * n:]):
            out_ref[...] = (g_ref[...].astype(F32) + r_ref[...].astype(F32)).astype(BF16)

    half = lambda g: pl.BlockSpec((None, g.shape[1], g.shape[2] // 2), lambda k, core_ref: (k, 0, 0))
    mine = lambda g: pl.BlockSpec((None, g.shape[1], g.shape[2] // 2), lambda k, core_ref: (k, 0, core_ref[0]))
    return pl.pallas_call(
        body, name=name,
        grid_spec=pltpu.PrefetchScalarGridSpec(
            num_scalar_prefetch=1, grid=(N_CHIPS,),
            in_specs=[mine(g) for g in grads] + [half(g) for g in grads],
            out_specs=[half(g) for g in grads]),
        out_shape=[jax.ShapeDtypeStruct(r.shape, BF16) for r in recvs],
        compiler_params=_params(("arbitrary",)),
    )(core, *grads, *recvs)


def _chip_exchange_comm(parts):
    n = len(parts)

    def copies(ins, outs, sems):
        x, y, c, others = _place()
        return [pltpu.make_async_remote_copy(
            src_ref=ins[a].at[2 * ox + oy], dst_ref=outs[a].at[j],
            send_sem=sems[0].at[3 * a + j], recv_sem=sems[1].at[3 * a + j],
            device_id=(ox, oy, c), device_id_type=MESH) for a in range(n) for j, (ox, oy) in enumerate(others)]

    def start(ins, outs, sems):
        for cp in copies(ins, outs, sems):
            cp.start()

    def finish(ins, outs, sems):
        for cp in copies(ins, outs, sems):
            cp.wait()

    return _Comm(parts, [jax.ShapeDtypeStruct((3,) + s.shape[1:], s.dtype) for s in parts],
                 [pltpu.SemaphoreType.DMA((3 * n,))] * 2, start, finish)


HBM = pl.BlockSpec(memory_space=pltpu.HBM)
SEM = pl.BlockSpec(memory_space=pltpu.SEMAPHORE)


def _split_exchange_copies(parts, lands, send_sems, recv_sems):
    x, y, c, others = _place()
    return [pltpu.make_async_remote_copy(
        src_ref=parts[a].at[2 * ox + oy], dst_ref=lands[a].at[j],
        send_sem=send_sems.at[3 * a + j], recv_sem=recv_sems.at[3 * a + j],
        device_id=(ox, oy, c), device_id_type=MESH) for a in range(len(parts)) for j, (ox, oy) in enumerate(others)]


def _exchange_start(name, parts):
    n = len(parts)

    def body(*refs):
        ins, lands = refs[:n], refs[n:2 * n]
        send_sems, recv_sems, token = refs[2 * n], refs[2 * n + 1], refs[-1]
        for cp in _split_exchange_copies(ins, lands, send_sems, recv_sems):
            cp.start()
        token[...] = jnp.zeros_like(token)

    land_shape = [(3,) + p.shape[1:] for p in parts]
    outs = pl.pallas_call(
        body, name=name,
        out_shape=[pltpu.SemaphoreType.DMA((3 * n,)), pltpu.SemaphoreType.DMA((3 * n,))]
        + [pltpu.HBM(p.shape, p.dtype) for p in parts] + [pltpu.HBM(s, p.dtype) for s, p in zip(land_shape, parts)]
        + [jax.ShapeDtypeStruct((8, LANES), F32)],
        in_specs=[HBM] * (2 * n), out_specs=[SEM, SEM] + [HBM] * (2 * n) + [pl.BlockSpec(memory_space=pltpu.VMEM)],
        input_output_aliases={i: 2 + i for i in range(2 * n)},
        compiler_params=pltpu.CompilerParams(has_side_effects=pltpu.SideEffectType.DATAFLOW_SIDE_EFFECTING),
    )(*[pltpu.with_memory_space_constraint(p, pltpu.HBM) for p in parts],
      *[pltpu.with_memory_space_constraint(lax.empty(s, p.dtype), pltpu.HBM) for s, p in zip(land_shape, parts)])
    return outs[0], outs[1], list(outs[2:2 + n]), list(outs[2 + n:2 + 2 * n]), outs[-1]


def _exchange_wait(name, send_sems, recv_sems, parts, lands, after):
    n = len(parts)

    def body(*refs):
        ins, zones = refs[:n], refs[n:2 * n]
        for cp in _split_exchange_copies(ins, zones, refs[2 * n], refs[2 * n + 1]):
            cp.wait_send()
            cp.wait_recv()

    outs = pl.pallas_call(
        body, name=name,
        out_shape=[pltpu.HBM(p.shape, p.dtype) for p in parts] + [pltpu.HBM(z.shape, z.dtype) for z in lands],
        in_specs=[HBM] * (2 * n) + [SEM, SEM] + [ANY] * len(after), out_specs=[HBM] * (2 * n),
        input_output_aliases={i: i for i in range(2 * n)},
        compiler_params=pltpu.CompilerParams(has_side_effects=pltpu.SideEffectType.DATAFLOW_SIDE_EFFECTING),
    )(*parts, *lands, send_sems, recv_sems, *after)
    return list(outs[:n]), list(outs[n:])


def _sum_chips(name, owns, recvs, chip, after):
    n = len(owns)
    hc = owns[0].shape[2]
    assert all(o.shape[2] == hc for o in owns)

    def body(chip_ref, *refs):
        for own_ref, recv_ref, out_ref in zip(refs[:n], refs[n:2 * n], refs[2 * n + 1:]):
            acc = own_ref[...].astype(F32)
            for j in range(3):
                acc = acc + recv_ref[j].astype(F32)
            out_ref[...] = acc

    return pl.pallas_call(
        body, name=name,
        grid_spec=pltpu.PrefetchScalarGridSpec(
            num_scalar_prefetch=1, grid=(hc // LANES,),
            in_specs=[pl.BlockSpec((None, o.shape[1], LANES), lambda i, chip_ref: (chip_ref[0], 0, i)) for o in owns]
            + [pl.BlockSpec((3, o.shape[1], LANES), lambda i, chip_ref: (0, 0, i)) for o in owns]
            + [pl.BlockSpec((8, LANES), lambda i, chip_ref: (0, 0))],
            out_specs=[pl.BlockSpec((o.shape[1], LANES), lambda i, chip_ref: (0, i)) for o in owns]),
        out_shape=[jax.ShapeDtypeStruct((o.shape[1], hc), F32) for o in owns],
        compiler_params=_params(("arbitrary",)),
    )(chip, *owns, *recvs, after)


def _share_halves(name, halves):
    n = len(halves)

    def body(*refs):
        srcs, dsts = refs[:n], refs[n:2 * n]
        send_sems, recv_sems = refs[2 * n:]
        x, y, c, _ = _place()
        copies = [pltpu.make_async_remote_copy(
            src_ref=srcs[a], dst_ref=dsts[a], send_sem=send_sems.at[a], recv_sem=recv_sems.at[a],
            device_id=(x, y, 1 - c), device_id_type=MESH) for a in range(n)]
        for cp in copies:
            cp.start()
        for cp in copies:
            cp.wait()

    return pl.pallas_call(
        body, name=name,
        in_specs=[ANY] * n, out_specs=[ANY] * n,
        out_shape=[jax.ShapeDtypeStruct(s.shape, s.dtype) for s in halves],
        scratch_shapes=[pltpu.SemaphoreType.DMA((n,)), pltpu.SemaphoreType.DMA((n,))],
    )(*halves)


def _small_gather_comm(part):
    def copies(ins, outs, sems):
        x, y, c, _ = _place()
        me = 4 * x + 2 * y + c
        both = []
        for d in range(1, N_DEV):
            px, py, pc = (1 - x if d & 4 else x, 1 - y if d & 2 else y, 1 - c if d & 1 else c)
            send = pltpu.make_async_remote_copy(
                src_ref=ins[0], dst_ref=outs[0].at[me], send_sem=sems[0].at[d - 1], recv_sem=sems[1].at[d - 1],
                device_id=(px, py, pc), device_id_type=MESH)
            recv = pltpu.make_async_remote_copy(
                src_ref=ins[0], dst_ref=outs[0].at[4 * px + 2 * py + pc], send_sem=sems[0].at[d - 1],
                recv_sem=sems[1].at[d - 1], device_id=(px, py, pc), device_id_type=MESH)
            both.append((send, recv))
        return both

    def start(ins, outs, sems):
        for send, _ in copies(ins, outs, sems):
            send.start()

    def finish(ins, outs, sems):
        for send, recv in copies(ins, outs, sems):
            recv.wait_recv()
            send.wait_send()

    return _Comm([part], [jax.ShapeDtypeStruct((N_DEV,) + part.shape, F32)],
                 [pltpu.SemaphoreType.DMA((N_DEV - 1,))] * 2, start, finish)


def _sum_devices(parts):
    def body(p_ref, out_ref):
        acc = p_ref[0]
        for k in range(1, N_DEV):
            acc = acc + p_ref[k]
        out_ref[...] = acc

    return pl.pallas_call(
        body, name="sum_devices", grid=(1,),
        in_specs=[pl.BlockSpec(parts.shape, lambda i: (0, 0, 0))],
        out_specs=pl.BlockSpec(parts.shape[1:], lambda i: (0, 0)),
        out_shape=jax.ShapeDtypeStruct(parts.shape[1:], F32),
        compiler_params=_params(("arbitrary",)),
    )(parts)


def _adam_update(w, g, m, v):
    nm = ADAM_B1 * m + (1.0 - ADAM_B1) * g
    nv = ADAM_B2 * v + (1.0 - ADAM_B2) * (g * g)
    m_hat = nm * (1.0 / (1.0 - ADAM_B1 ** ADAM_STEP))
    v_hat = nv * (1.0 / (1.0 - ADAM_B2 ** ADAM_STEP))
    return -ADAM_LR * (m_hat / (jnp.sqrt(v_hat) + ADAM_EPS) + ADAM_WD * w), nm, nv


def _adamw(name, w, g, m, v):
    def body(w_ref, g_ref, m_ref, v_ref, d_ref, nm_ref, nv_ref):
        d_ref[...], nm_ref[...], nv_ref[...] = _adam_update(w_ref[...], g_ref[...], m_ref[...], v_ref[...])

    spec = pl.BlockSpec(w.shape, lambda i: (0, 0))
    out = jax.ShapeDtypeStruct(w.shape, F32)
    return pl.pallas_call(
        body, name=name, grid=(1,),
        in_specs=[spec] * 4, out_specs=[spec] * 3, out_shape=[out] * 3,
        compiler_params=_params(("arbitrary",)),
    )(w, g, m, v)


def _adamw_halves(name, ws, mines, theirs, ms, vs, core):
    n = len(ws)
    cols = ws[0].shape[1]
    assert all(w.shape[1] == cols for w in ws)
    hc = cols // 2
    tc = LANES if n > 1 else min(256, hc)
    nt = hc // tc

    def body(core_ref, *refs):
        ins, outs = refs[:5 * n], refs[5 * n:]
        for a in range(n):
            w_ref, mine_ref, theirs_ref, m_ref, v_ref = [ins[j * n + a] for j in range(5)]
            g_ref, d_ref, nm_ref, nv_ref = outs[4 * a:4 * a + 4]
            gv = jnp.where(pl.program_id(0) == core_ref[0], mine_ref[...], theirs_ref[...])
            g_ref[...] = gv
            d_ref[...], nm_ref[...], nv_ref[...] = _adam_update(w_ref[...], gv, m_ref[...], v_ref[...])

    whole = lambda w: pl.BlockSpec((w.shape[0], tc), lambda h, i, core_ref: (0, h * nt + i))
    mine_spec = lambda w: pl.BlockSpec((w.shape[0], tc), lambda h, i, core_ref: (0, jnp.where(h == core_ref[0], i, 0)))
    theirs_spec = lambda w: pl.BlockSpec((w.shape[0], tc), lambda h, i, core_ref: (0, jnp.where(h == core_ref[0], 0, i)))
    outs = pl.pallas_call(
        body, name=name,
        grid_spec=pltpu.PrefetchScalarGridSpec(
            num_scalar_prefetch=1, grid=(2, nt),
            in_specs=[whole(w) for w in ws] + [mine_spec(w) for w in ws] + [theirs_spec(w) for w in ws]
            + [whole(w) for w in ws] * 2,
            out_specs=[whole(w) for w in ws for _ in range(4)]),
        out_shape=[jax.ShapeDtypeStruct(w.shape, F32) for w in ws for _ in range(4)],
        compiler_params=_params(("arbitrary", "arbitrary")),
    )(core, *ws, *mines, *theirs, *ms, *vs)
    return [outs[4 * a:4 * a + 4] for a in range(n)]


WEIGHTS = ("ffn1_norm", "ffn1_gate", "ffn1_up", "ffn1_down", "mix_norm", "w_in", "b_forget", "conv_w",
           "w_o_attn", "w_o_conv", "w_out", "ffn2_norm", "ffn2_gate", "ffn2_up", "ffn2_down", "final_norm")
VEC_ROWS = 8


def _pack_small(t, conv_rows):
    conv = t["conv_w"]
    parts = [t[n].reshape(VEC_ROWS, LANES) for n in NORMS]
    parts.append(jnp.pad(conv, ((0, conv_rows - conv.shape[0]), (0, 0))))
    parts.append(jnp.pad(t["b_forget"].reshape(1, N_HEADS), ((0, 7), (0, LANES - N_HEADS))))
    return jnp.concatenate(parts, axis=0)


def _unpack_small(p, conv_rows):
    out = {n: p[VEC_ROWS * i:VEC_ROWS * (i + 1)].reshape(-1) for i, n in enumerate(NORMS)}
    base = VEC_ROWS * len(NORMS)
    out["conv_w"] = p[base:base + 3]
    out["b_forget"] = p[base + conv_rows, :N_HEADS]
    return out


def _travel(name, a):
    return a.T if name in TRANSPOSED else a


GATHER_FIRST = ("ffn1_gate", "ffn1_up")
GATHER_RIDES = {"ffn1_up": ("ffn1_down", "w_in"), "ffn1_down": ("w_o_attn", "w_o_conv", "w_out"),
                "attn_fwd": ("ffn2_gate", "ffn2_up", "ffn2_down")}
SIBLING_RIDES = {"ffn2": "mix_out_bwd", "out": None, "w_in": "mix_proj_bwd_dx", "ffn1": None}
CHIP_RIDES = {"ffn2": "attn_bwd", "out": "attn_bwd", "w_in": "ffn1_bwd_dw", "ffn1": None}
SMALL_RIDE = "ffn1_bwd_dw"


class _MeshPlan(_LocalPlan):
    def __init__(self, wts, core):
        self.small, self.core = wts, core
        self.shards = {n: wts[n].astype(BF16) for n in BIG}
        self.chip_part, self.from_chips, self.rides = {}, {}, {}
        conv_shard = jnp.pad(wts["conv_w"], ((0, 8 - wts["conv_w"].shape[0]), (0, 0)))
        own = [self.shards[n] for n in GATHER_FIRST] + [conv_shard]
        got = _run_comm("gather_first", _gather_comm(own[:-1], conv_shard))
        self.stacks = dict(zip(GATHER_FIRST + ("conv_w",), _fill_own(got, own)))
        for kernel_name, names in GATHER_RIDES.items():
            mine = [self.shards[n] for n in names]
            self._ride(kernel_name, _gather_comm(mine),
                       lambda got, names=names, mine=mine: self.stacks.update(zip(names, _fill_own(got, mine))))

    def _ride(self, kernel_name, comm, then):
        self.rides.setdefault(kernel_name, []).append((comm, then))

    def rider(self, kernel_name):
        comms = [comm for comm, _ in self.rides.get(kernel_name, [])]
        return _merge_comms(comms) if comms else None

    def arrived(self, kernel_name, results):
        for comm, then in self.rides.pop(kernel_name, []):
            then(results[:len(comm.out_shape)])
            results = results[len(comm.out_shape):]

    def reduce(self, group, grads):
        names = tuple(grads)
        mine = [grads[n] for n in names]

        def with_sibling(from_sibling):
            parts = _add_halves("add_halves_" + group, mine, list(from_sibling), self.core)
            self.chip_part.update(zip(names, parts))
            if CHIP_RIDES[group] is None:
                self.last = (names, _exchange_start("exchange_start_" + group, parts))
            else:
                self._ride(CHIP_RIDES[group], _chip_exchange_comm(parts),
                           lambda got: self.from_chips.update(zip(names, got)))

        if SIBLING_RIDES[group] is None:
            with_sibling(_run_comm("sibling_exchange_" + group, _sibling_exchange_comm(mine)))
        else:
            self._ride(SIBLING_RIDES[group], _sibling_exchange_comm(mine), with_sibling)

    def reduce_small(self, gs, loss):
        conv_all = _shard_cols(gs["conv_w"]).reshape(N_CHIPS * 8, LANES)
        part = _pack_small({**{n: gs[n] for n in NORMS}, "conv_w": conv_all, "b_forget": gs["b_forget"][0, :N_HEADS]},
                           N_CHIPS * 8)
        part = jnp.concatenate([part, jnp.broadcast_to(loss, (8, LANES))], axis=0)
        me = 4 * lax.axis_index("x") + 2 * lax.axis_index("y") + lax.axis_index("c")

        def landed(got):
            self.small_parts = lax.dynamic_update_index_in_dim(got[0], part, me, 0)

        self._ride(SMALL_RIDE, _small_gather_comm(part), landed)


def kernel(x, ffn1_norm, ffn1_gate, ffn1_up, ffn1_down, mix_norm, w_in, b_forget, conv_w, w_o_attn, w_o_conv, w_out, ffn2_norm, ffn2_gate, ffn2_up, ffn2_down, final_norm, loss_target, m_ffn1_norm, m_ffn1_gate, m_ffn1_up, m_ffn1_down, m_mix_norm, m_w_in, m_b_forget, m_conv_w, m_w_o_attn, m_w_o_conv, m_w_out, m_ffn2_norm, m_ffn2_gate, m_ffn2_up, m_ffn2_down, m_final_norm, v_ffn1_norm, v_ffn1_gate, v_ffn1_up, v_ffn1_down, v_mix_norm, v_w_in, v_b_forget, v_conv_w, v_w_o_attn, v_w_o_conv, v_w_out, v_ffn2_norm, v_ffn2_gate, v_ffn2_up, v_ffn2_down, v_final_norm):
    given = dict(locals())
    wts = {n: _travel(n, given[n]) for n in WEIGHTS}
    mom = {n: _travel(n, given["m_" + n]) for n in WEIGHTS}
    var = {n: _travel(n, given["v_" + n]) for n in WEIGHTS}
    B, S, D = x.shape
    chip = 2 * lax.axis_index("x") + lax.axis_index("y")
    chip1 = chip.astype(jnp.int32).reshape(1)
    core = lax.axis_index("c").astype(jnp.int32).reshape(1)

    plan = _MeshPlan(wts, core)
    loss, grad_x, gs = _local_step(x.reshape(B * S, D), loss_target.reshape(B * S, D), plan, B, S)

    last_names, (send_sems, recv_sems, parts_thru, lands, token) = plan.last
    delta, new_m, new_v, grads = {}, {}, {}, {}

    def finish(tag, names):
        by_cols = {}
        for n in names:
            by_cols.setdefault(wts[n].shape[1], []).append(n)
        mine = {}
        for cols, ns in by_cols.items():
            mine.update(zip(ns, _sum_chips("sum_chips_%s_%d" % (tag, cols), [plan.chip_part[n] for n in ns],
                                           [plan.from_chips[n] for n in ns], chip1, token)))
        theirs = dict(zip(names, _share_halves("share_halves_" + tag, [mine[n] for n in names])))
        raw = []
        for cols, ns in by_cols.items():
            outs = _adamw_halves("adamw_%s_%d" % (tag, cols), [wts[n] for n in ns], [mine[n] for n in ns],
                                 [theirs[n] for n in ns], [mom[n] for n in ns], [var[n] for n in ns], core)
            for n, per in zip(ns, outs):
                raw.append(per[-1])
                grads[n], delta[n], new_m[n], new_v[n] = [_travel(n, o) for o in per]
        return raw

    small_sum = _sum_devices(plan.small_parts)
    base = VEC_ROWS * len(NORMS)
    loss_row = small_sum.shape[0] - 8
    small_grads = _unpack_small(small_sum, N_CHIPS * 8)
    small_grads["conv_w"] = lax.dynamic_slice_in_dim(small_sum[base:base + N_CHIPS * 8], chip * 8, 8, axis=0)[:3]
    packs = [_pack_small(t, 8) for t in (wts, small_grads, mom, var)]
    small_out = _adamw("adamw_small", *packs)

    done = finish("early", [n for n in BIG if n not in last_names])
    parts_back, got = _exchange_wait("exchange_wait", send_sems, recv_sems, parts_thru, lands, done + list(small_out))
    plan.chip_part.update(zip(last_names, parts_back))
    plan.from_chips.update(zip(last_names, got))
    finish("last", last_names)
    grads.update(small_grads)
    for out, p in zip((delta, new_m, new_v), small_out):
        out.update(_unpack_small(p, 8))

    return (small_sum[loss_row, 0], grad_x.reshape(B, S, D), *[grads[n] for n in WEIGHTS], *[delta[n] for n in WEIGHTS],
            *[new_m[n] for n in WEIGHTS], *[new_v[n] for n in WEIGHTS])
```

```python
import functools
import math

import jax
import jax.numpy as jnp
from jax import lax
from jax.experimental import pallas as pl
from jax.experimental.pallas import tpu as pltpu

F32 = jnp.float32
BF16 = jnp.bfloat16
MESH = pl.DeviceIdType.MESH

N_CHIPS = 4
N_DEV = 8
N_HEADS = 8
HEAD_DIM = 64
HEAD_PAIRS = N_HEADS // 2
ATTN_W = N_HEADS * HEAD_DIM
CONV_W = 512
RMS_EPS = 1e-6
FFN_RES = 0.5
LANES = 128
VMEM_LIMIT = 56 * 1024 * 1024
ROW_BLOCK = 256

ADAM_LR = 0.001
ADAM_B1 = 0.9
ADAM_B2 = 0.999
ADAM_EPS = 1e-08
ADAM_WD = 0.01
ADAM_STEP = 10

PROJ_W = 3 * ATTN_W + 3 * CONV_W + 2 * 1024
COL_CB, COL_CC, COL_CX = 3 * ATTN_W, 3 * ATTN_W + CONV_W, 3 * ATTN_W + 2 * CONV_W
COL_GATES = 3 * ATTN_W + 3 * CONV_W
N_FORGET_COL = 3 * ATTN_W


def _params(sem=None, vmem=VMEM_LIMIT):
    return pltpu.CompilerParams(dimension_semantics=sem, vmem_limit_bytes=vmem)


def _dot(a, b):
    return lax.dot_general(a, b, (((1,), (0,)), ((), ())), preferred_element_type=F32)


def _dot_nt(a, b):
    return lax.dot_general(a, b, (((1,), (1,)), ((), ())), preferred_element_type=F32)


def _dot_tn(a, b):
    return lax.dot_general(a, b, (((0,), (0,)), ((), ())), preferred_element_type=F32)


def _sigmoid(x):
    return 1.0 / (1.0 + jnp.exp(-x))


def _rms(xv):
    inv = lax.rsqrt(jnp.mean(xv * xv, axis=-1, keepdims=True) + RMS_EPS)
    return xv * inv, inv


class _Comm:
    def __init__(self, inputs, out_shape, scratch, start, finish):
        self.inputs, self.out_shape, self.scratch = list(inputs), list(out_shape), list(scratch)
        self.start, self.finish = start, finish


def _pallas(body, name, grid, in_specs, out_specs, out_shape, scratch, args, comm=None):
    sem = ("arbitrary",) * len(grid)
    if comm is None:
        outs = pl.pallas_call(body, name=name, grid=grid, in_specs=in_specs, out_specs=out_specs,
                              out_shape=out_shape, scratch_shapes=scratch, compiler_params=_params(sem))(*args)
        return list(outs), []
    n_in, n_out, n_scr = len(in_specs), len(out_specs), len(scratch)
    ci, co = len(comm.inputs), len(comm.out_shape)

    def riding(*refs):
        ins, refs = refs[:n_in], refs[n_in:]
        cins, refs = refs[:ci], refs[ci:]
        outs, refs = refs[:n_out], refs[n_out:]
        couts, refs = refs[:co], refs[co:]
        scr, sems = refs[:n_scr], refs[n_scr:]
        ids = [pl.program_id(d) for d in range(len(grid))]
        first = functools.reduce(lambda a, b: a & b, [i == 0 for i in ids])
        last = functools.reduce(lambda a, b: a & b, [i == g - 1 for i, g in zip(ids, grid)])

        @pl.when(first)
        def _():
            comm.start(cins, couts, sems)

        body(*ins, *outs, *scr)

        @pl.when(last)
        def _():
            comm.finish(cins, couts, sems)

    any_spec = pl.BlockSpec(memory_space=pl.ANY)
    outs = pl.pallas_call(
        riding, name=name, grid=grid,
        in_specs=list(in_specs) + [any_spec] * ci, out_specs=list(out_specs) + [any_spec] * co,
        out_shape=list(out_shape) + comm.out_shape, scratch_shapes=list(scratch) + comm.scratch,
        compiler_params=_params(sem))(*args, *comm.inputs)
    return list(outs[:n_out]), list(outs[n_out:])


def _rms_bwd(dn, xhat, inv, g):
    dxhat = dn * g
    dx = inv * (dxhat - xhat * jnp.mean(dxhat * xhat, axis=-1, keepdims=True))
    return dx, jnp.sum(dn * xhat, axis=0, keepdims=True)


def _ffn_fwd(name, x, g, wgt, wut, wd, tm, comm=None):
    T, D = x.shape
    K, Fs, _ = wgt.shape

    def body(x_ref, g_ref, wg_ref, wu_ref, wd_ref, out_ref, hg_ref, hu_ref, n_ref, acc_scr):
        k = pl.program_id(1)

        @pl.when(k == 0)
        def _():
            xhat, _ = _rms(x_ref[...])
            n_ref[...] = (xhat * g_ref[...]).astype(BF16)
            acc_scr[...] = jnp.zeros_like(acc_scr)

        n = n_ref[...]
        hg = _dot_nt(n, wg_ref[...])
        hu = _dot_nt(n, wu_ref[...])
        hg_ref[...] = hg.astype(BF16)
        hu_ref[...] = hu.astype(BF16)
        act = (hg * _sigmoid(hg) * hu).astype(BF16)
        acc_scr[...] += _dot(act, wd_ref[...])

        @pl.when(k == K - 1)
        def _():
            out_ref[...] = x_ref[...] + FFN_RES * acc_scr[...]

    w_spec = pl.BlockSpec((None, Fs, D), lambda i, k: (k, 0, 0))
    act_spec = pl.BlockSpec((None, tm, Fs), lambda i, k: (k, i, 0))
    return _pallas(
        body, name, (T // tm, K),
        [pl.BlockSpec((tm, D), lambda i, k: (i, 0)), pl.BlockSpec((1, D), lambda i, k: (0, 0)),
         w_spec, w_spec, w_spec],
        [pl.BlockSpec((tm, D), lambda i, k: (i, 0)), act_spec, act_spec, pl.BlockSpec((tm, D), lambda i, k: (i, 0))],
        [jax.ShapeDtypeStruct((T, D), F32), jax.ShapeDtypeStruct((K, T, Fs), BF16),
         jax.ShapeDtypeStruct((K, T, Fs), BF16), jax.ShapeDtypeStruct((T, D), BF16)],
        [pltpu.VMEM((tm, D), F32)],
        (x, g, wgt, wut, wd), comm)


def _ffn_up(name, x, g, wgt, wut, tm, comm=None):
    T, D = x.shape
    K, Fs, _ = wgt.shape

    def body(x_ref, g_ref, wg_ref, wu_ref, hg_ref, hu_ref, n_ref):
        @pl.when(pl.program_id(1) == 0)
        def _():
            xhat, _ = _rms(x_ref[...])
            n_ref[...] = (xhat * g_ref[...]).astype(BF16)

        n = n_ref[...]
        hg_ref[...] = _dot_nt(n, wg_ref[...]).astype(BF16)
        hu_ref[...] = _dot_nt(n, wu_ref[...]).astype(BF16)

    w_spec = pl.BlockSpec((None, Fs, D), lambda i, k: (k, 0, 0))
    act_spec = pl.BlockSpec((None, tm, Fs), lambda i, k: (k, i, 0))
    return _pallas(
        body, name, (T // tm, K),
        [pl.BlockSpec((tm, D), lambda i, k: (i, 0)), pl.BlockSpec((1, D), lambda i, k: (0, 0)), w_spec, w_spec],
        [act_spec, act_spec, pl.BlockSpec((tm, D), lambda i, k: (i, 0))],
        [jax.ShapeDtypeStruct((K, T, Fs), BF16), jax.ShapeDtypeStruct((K, T, Fs), BF16),
         jax.ShapeDtypeStruct((T, D), BF16)],
        [],
        (x, g, wgt, wut), comm)


def _ffn_up_gather(name, x, g, wg_own, wu_own, order, tm, comm=None):
    T, D = x.shape
    Fs = wg_own.shape[0]
    nt = T // tm
    ci, co = (len(comm.inputs), len(comm.out_shape)) if comm is not None else (0, 0)

    def body(order_ref, x_ref, g_ref, wgo_ref, wuo_ref, *rest):
        cins, rest = rest[:ci], rest[ci:]
        (hg_ref, hu_ref, n_ref, sg_ref, su_ref), rest = rest[:5], rest[5:]
        couts, rest = rest[:co], rest[co:]
        (n_all, wbuf, send_sems, recv_sems, pass_send, pass_recv, load_sems), csems = rest[:7], rest[7:]
        k, i = pl.program_id(0), pl.program_id(1)
        x_pos, y_pos, c, others = _place()
        me = 2 * x_pos + y_pos
        owns, stacks = (wgo_ref, wuo_ref), (sg_ref, su_ref)
        mine, theirs = _col_halves(D, c)

        def chip_copy(a, j, chip):
            return pltpu.make_async_remote_copy(
                src_ref=owns[a].at[:, mine], dst_ref=stacks[a].at[chip, :, mine],
                send_sem=send_sems.at[3 * a + j], recv_sem=recv_sems.at[3 * a + j],
                device_id=(*others[j], c), device_id_type=MESH)

        def pass_copy(a, j, chip, half):
            return pltpu.make_async_remote_copy(
                src_ref=stacks[a].at[chip, :, half], dst_ref=stacks[a].at[chip, :, half],
                send_sem=pass_send.at[3 * a + j], recv_sem=pass_recv.at[3 * a + j],
                device_id=(x_pos, y_pos, 1 - c), device_id_type=MESH)

        @pl.when((k == 0) & (i == 0))
        def _():
            for a in range(2):
                for j in range(3):
                    chip_copy(a, j, me).start()
            if comm is not None:
                comm.start(cins, couts, csems)

        for j, (ox, oy) in enumerate(others):
            @pl.when((k == j + 1) & (i == 0))
            def _(j=j, chip=2 * ox + oy):
                for a in range(2):
                    chip_copy(a, j, chip).wait_recv()
                for a in range(2):
                    pass_copy(a, j, chip, mine).start()
                for a in range(2):
                    pass_copy(a, j, chip, theirs).wait_recv()
                loads = [pltpu.make_async_copy(stacks[a].at[chip], wbuf.at[a], load_sems.at[a]) for a in range(2)]
                for cp in loads:
                    cp.start()
                for cp in loads:
                    cp.wait()

        rows = pl.ds(pl.multiple_of(i * tm, tm), tm)

        @pl.when(k == 0)
        def _():
            xhat, _ = _rms(x_ref[...])
            n = (xhat * g_ref[...]).astype(BF16)
            n_ref[...] = n
            n_all[rows, :] = n
            hg_ref[...] = _dot_nt(n, wgo_ref[...]).astype(BF16)
            hu_ref[...] = _dot_nt(n, wuo_ref[...]).astype(BF16)

        @pl.when(k > 0)
        def _():
            n = n_all[rows, :]
            hg_ref[...] = _dot_nt(n, wbuf[0]).astype(BF16)
            hu_ref[...] = _dot_nt(n, wbuf[1]).astype(BF16)

        @pl.when((k == N_CHIPS - 1) & (i == nt - 1))
        def _():
            for a in range(2):
                for j, (ox, oy) in enumerate(others):
                    chip_copy(a, j, me).wait_send()
                    pass_copy(a, j, 2 * ox + oy, mine).wait_send()
            if comm is not None:
                comm.finish(cins, couts, csems)

    any_spec = pl.BlockSpec(memory_space=pl.ANY)
    first_pass = lambda k, i, order_ref: (jnp.where(k == 0, i, nt - 1), 0)
    whole = pl.BlockSpec((Fs, D), lambda k, i, order_ref: (0, 0))
    act_spec = pl.BlockSpec((None, tm, Fs), lambda k, i, order_ref: (order_ref[k], i, 0))
    stack = jax.ShapeDtypeStruct((N_CHIPS, Fs, D), BF16)
    outs = pl.pallas_call(
        body, name=name,
        grid_spec=pltpu.PrefetchScalarGridSpec(
            num_scalar_prefetch=1, grid=(N_CHIPS, nt),
            in_specs=[pl.BlockSpec((tm, D), first_pass), pl.BlockSpec((1, D), lambda k, i, order_ref: (0, 0)),
                      whole, whole] + [any_spec] * ci,
            out_specs=[act_spec, act_spec, pl.BlockSpec((tm, D), first_pass), any_spec, any_spec] + [any_spec] * co,
            scratch_shapes=[pltpu.VMEM((T, D), BF16), pltpu.VMEM((2, Fs, D), BF16)]
            + [pltpu.SemaphoreType.DMA((6,))] * 4 + [pltpu.SemaphoreType.DMA((2,))]
            + (comm.scratch if comm is not None else [])),
        out_shape=[jax.ShapeDtypeStruct((N_CHIPS, T, Fs), BF16), jax.ShapeDtypeStruct((N_CHIPS, T, Fs), BF16),
                   jax.ShapeDtypeStruct((T, D), BF16), stack, stack] + (comm.out_shape if comm is not None else []),
        compiler_params=_params(("arbitrary", "arbitrary")),
    )(order, x, g, wg_own, wu_own, *(comm.inputs if comm is not None else []))
    return list(outs[:5]), list(outs[5:])


def _ffn_down(name, x, hg, hu, wd, tm, comm=None):
    T, D = x.shape
    K, Fs, _ = wd.shape

    def body(x_ref, hg_ref, hu_ref, wd_ref, out_ref, acc_scr):
        k = pl.program_id(1)

        @pl.when(k == 0)
        def _():
            acc_scr[...] = jnp.zeros_like(acc_scr)

        hgv = hg_ref[...].astype(F32)
        act = (hgv * _sigmoid(hgv) * hu_ref[...].astype(F32)).astype(BF16)
        acc_scr[...] += _dot(act, wd_ref[...])

        @pl.when(k == K - 1)
        def _():
            out_ref[...] = x_ref[...] + FFN_RES * acc_scr[...]

    act_spec = pl.BlockSpec((None, tm, Fs), lambda i, k: (k, i, 0))
    row = pl.BlockSpec((tm, D), lambda i, k: (i, 0))
    return _pallas(
        body, name, (T // tm, K),
        [row, act_spec, act_spec, pl.BlockSpec((None, Fs, D), lambda i, k: (k, 0, 0))],
        [row], [jax.ShapeDtypeStruct((T, D), F32)], [pltpu.VMEM((tm, D), F32)],
        (x, hg, hu, wd), comm)


def _ffn_bwd_dx(name, dout, x, g, hg, hu, wgt, wut, wd, tm, comm=None):
    T, D = x.shape
    K, Fs, _ = wgt.shape

    def body(dout_ref, x_ref, g_ref, hg_ref, hu_ref, wg_ref, wu_ref, wd_ref,
             dx_ref, dhg_ref, dhu_ref, dg_ref, df_ref, dn_scr):
        i, k = pl.program_id(0), pl.program_id(1)

        @pl.when(k == 0)
        def _():
            df_ref[...] = (FFN_RES * dout_ref[...]).astype(BF16)
            dn_scr[...] = jnp.zeros_like(dn_scr)

        @pl.when((k == 0) & (i == 0))
        def _():
            dg_ref[...] = jnp.zeros_like(dg_ref)

        for r0 in range(0, tm, ROW_BLOCK):
            rows = slice(r0, r0 + ROW_BLOCK)
            dact = _dot_nt(df_ref[rows, :], wd_ref[...])
            hgv = hg_ref[rows, :].astype(F32)
            huv = hu_ref[rows, :].astype(F32)
            s = _sigmoid(hgv)
            dhu = (dact * (hgv * s)).astype(BF16)
            dhg = (dact * huv * (s * (1.0 + hgv * (1.0 - s)))).astype(BF16)
            dhg_ref[rows, :] = dhg
            dhu_ref[rows, :] = dhu
            dn_scr[rows, :] += _dot(dhg, wg_ref[...]) + _dot(dhu, wu_ref[...])

        @pl.when(k == K - 1)
        def _():
            xhat, inv = _rms(x_ref[...])
            dx, dg = _rms_bwd(dn_scr[...], xhat, inv, g_ref[...])
            dx_ref[...] = dout_ref[...] + dx
            dg_ref[...] += dg

    w_spec = pl.BlockSpec((None, Fs, D), lambda i, k: (k, 0, 0))
    act_spec = pl.BlockSpec((None, tm, Fs), lambda i, k: (k, i, 0))
    row = pl.BlockSpec((tm, D), lambda i, k: (i, 0))
    row_once = pl.BlockSpec((tm, D), lambda i, k: (i, 0), pipeline_mode=pl.Buffered(1))
    vec = pl.BlockSpec((1, D), lambda i, k: (0, 0))
    return _pallas(
        body, name, (T // tm, K),
        [row, row_once, vec, act_spec, act_spec, w_spec, w_spec, w_spec],
        [row_once, act_spec, act_spec, vec, row],
        [jax.ShapeDtypeStruct((T, D), F32), jax.ShapeDtypeStruct((K, T, Fs), BF16),
         jax.ShapeDtypeStruct((K, T, Fs), BF16), jax.ShapeDtypeStruct((1, D), F32),
         jax.ShapeDtypeStruct((T, D), BF16)],
        [pltpu.VMEM((tm, D), F32)],
        (dout, x, g, hg, hu, wgt, wut, wd), comm)


def _ffn_bwd_dw(name, n, df, hg, hu, dhg, dhu, tk, comm=None):
    T, D = n.shape
    K, _, Fs = hg.shape
    nt = T // tk

    def body(n_ref, df_ref, hg_ref, hu_ref, dhg_ref, dhu_ref, dwg_ref, dwu_ref, dwd_ref, accg, accu, accd):
        t = pl.program_id(1)

        @pl.when(t == 0)
        def _():
            accg[...] = jnp.zeros_like(accg)
            accu[...] = jnp.zeros_like(accu)
            accd[...] = jnp.zeros_like(accd)

        nv = n_ref[...]
        hgv = hg_ref[...].astype(F32)
        act = (hgv * _sigmoid(hgv) * hu_ref[...].astype(F32)).astype(BF16)
        accg[...] += _dot_tn(dhg_ref[...], nv)
        accu[...] += _dot_tn(dhu_ref[...], nv)
        accd[...] += _dot_tn(act, df_ref[...])

        @pl.when(t == nt - 1)
        def _():
            dwg_ref[...] = accg[...].astype(BF16)
            dwu_ref[...] = accu[...].astype(BF16)
            dwd_ref[...] = accd[...].astype(BF16)

    act_spec = pl.BlockSpec((None, tk, Fs), lambda k, t: (k, t, 0))
    w_spec = pl.BlockSpec((None, Fs, D), lambda k, t: (k, 0, 0))
    row = pl.BlockSpec((tk, D), lambda k, t: (t, 0))
    return _pallas(
        body, name, (K, nt),
        [row, row, act_spec, act_spec, act_spec, act_spec],
        [w_spec, w_spec, w_spec],
        [jax.ShapeDtypeStruct((K, Fs, D), BF16)] * 3,
        [pltpu.VMEM((Fs, D), F32)] * 3,
        (n, df, hg, hu, dhg, dhu), comm)


def _mix_proj_fwd(x, g, wproj_t, wf_t, tm, tn):
    T, D = x.shape
    N = wproj_t.shape[0]

    def body(x_ref, g_ref, w_ref, wf_ref, h_ref, proj_ref, flog_ref, h_scr):
        @pl.when(pl.program_id(1) == 0)
        def _():
            xhat, _ = _rms(x_ref[...])
            h = (xhat * g_ref[...]).astype(BF16)
            h_scr[...] = h
            h_ref[...] = h
            flog_ref[...] = _dot_nt(h, wf_ref[...])

        proj_ref[...] = _dot_nt(h_scr[...], w_ref[...]).astype(BF16)

    return pl.pallas_call(
        body, name="mix_proj_fwd", grid=(T // tm, N // tn),
        in_specs=[pl.BlockSpec((tm, D), lambda i, n: (i, 0)),
                  pl.BlockSpec((1, D), lambda i, n: (0, 0)),
                  pl.BlockSpec((tn, D), lambda i, n: (n, 0)),
                  pl.BlockSpec((LANES, D), lambda i, n: (0, 0))],
        out_specs=[pl.BlockSpec((tm, D), lambda i, n: (i, 0)),
                   pl.BlockSpec((tm, tn), lambda i, n: (i, n)),
                   pl.BlockSpec((tm, LANES), lambda i, n: (i, 0))],
        out_shape=[jax.ShapeDtypeStruct((T, D), BF16),
                   jax.ShapeDtypeStruct((T, N), BF16),
                   jax.ShapeDtypeStruct((T, LANES), F32)],
        scratch_shapes=[pltpu.VMEM((tm, D), BF16)],
        compiler_params=_params(("arbitrary", "arbitrary")),
    )(x, g, wproj_t, wf_t)


def _log_sigmoid(z):
    return -(jnp.maximum(-z, 0.0) + jnp.log(1.0 + jnp.exp(-jnp.abs(z))))


def _tri(n, lower):
    r = lax.broadcasted_iota(jnp.int32, (n, n), 0)
    c = lax.broadcasted_iota(jnp.int32, (n, n), 1)
    return jnp.where((r >= c) if lower else (r <= c), 1.0, 0.0).astype(F32)


def _dot_f32(a, b):
    return lax.dot_general(a, b, (((1,), (0,)), ((), ())), preferred_element_type=F32,
                           precision=lax.Precision.HIGHEST)


def _fgate_fwd(flog, bias, B, S, ch):
    def body(flog_ref, b_ref, cum_ref):
        tri = _tri(ch, True)
        carry = jnp.zeros((1, LANES), F32)
        for c0 in range(0, S, ch):
            lf = _log_sigmoid(flog_ref[c0:c0 + ch, :] + b_ref[...])
            cs = _dot_f32(tri, lf) + carry
            cum_ref[c0:c0 + ch, :] = cs
            carry = cs[ch - 1:ch, :]

    return pl.pallas_call(
        body, name="fgate_fwd", grid=(B,),
        in_specs=[pl.BlockSpec((S, LANES), lambda b: (b, 0)),
                  pl.BlockSpec((1, LANES), lambda b: (0, 0))],
        out_specs=pl.BlockSpec((S, LANES), lambda b: (b, 0)),
        out_shape=jax.ShapeDtypeStruct((B * S, LANES), F32),
        compiler_params=_params(("arbitrary",)),
    )(flog, bias)


def _fgate_bwd(dcum, flog, bias, B, S, ch):
    def body(dcum_ref, flog_ref, b_ref, dflog_ref, db_ref):
        @pl.when(pl.program_id(0) == 0)
        def _():
            db_ref[...] = jnp.zeros_like(db_ref)

        tri = _tri(ch, False)
        carry = jnp.zeros((1, LANES), F32)
        db = jnp.zeros((1, LANES), F32)
        for c0 in range(S - ch, -1, -ch):
            dlf = _dot_f32(tri, dcum_ref[c0:c0 + ch, :]) + carry
            carry = dlf[0:1, :]
            z = flog_ref[c0:c0 + ch, :] + b_ref[...]
            dz = dlf * _sigmoid(-z)
            dflog_ref[c0:c0 + ch, :] = dz
            db = db + jnp.sum(dz, axis=0, keepdims=True)
        db_ref[...] += db

    return pl.pallas_call(
        body, name="fgate_bwd", grid=(B,),
        in_specs=[pl.BlockSpec((S, LANES), lambda b: (b, 0)),
                  pl.BlockSpec((S, LANES), lambda b: (b, 0)),
                  pl.BlockSpec((1, LANES), lambda b: (0, 0))],
        out_specs=[pl.BlockSpec((S, LANES), lambda b: (b, 0)),
                   pl.BlockSpec((1, LANES), lambda b: (0, 0))],
        out_shape=[jax.ShapeDtypeStruct((B * S, LANES), F32),
                   jax.ShapeDtypeStruct((1, LANES), F32)],
        compiler_params=_params(("arbitrary",)),
    )(dcum, flog, bias)


def _pick_lane(tile, h):
    lane = lax.broadcasted_iota(jnp.int32, tile.shape, 1)
    return jnp.sum(jnp.where(lane == h, tile, 0.0), axis=1, keepdims=True)


def _put_lane(col, h, width=LANES):
    lane = lax.broadcasted_iota(jnp.int32, (col.shape[0], width), 1)
    return jnp.where(lane == h, col, 0.0)


def _pick_row(tile, h):
    row = lax.broadcasted_iota(jnp.int32, tile.shape, 0)
    return jnp.sum(jnp.where(row == h, tile, 0.0), axis=0, keepdims=True)


def _put_row(vec, h):
    row = lax.broadcasted_iota(jnp.int32, (8, vec.shape[1]), 0)
    return jnp.where(row == h, vec, 0.0)


def _causal(tq):
    r = lax.broadcasted_iota(jnp.int32, (tq, tq), 0)
    c = lax.broadcasted_iota(jnp.int32, (tq, tq), 1)
    return r >= c


def _head_halves(t):
    lo = lax.broadcasted_iota(jnp.int32, t.shape, 1) < HEAD_DIM
    zero = jnp.zeros_like(t)
    return jnp.where(lo, t, zero), jnp.where(lo, zero, t)


NEG = -1e30
ATTN_SCALE = 1.0 / math.sqrt(HEAD_DIM)


def _scaled(q):
    return (q.astype(F32) * ATTN_SCALE).astype(q.dtype)


def _attn_fwd(proj, cum, cum_t, B, S, tq, comm=None):
    nq = S // tq

    def body(q_ref, k_ref, v_ref, cum_ref, cumt_ref, o_ref, lse_ref):
        qi, hp = pl.program_id(1), pl.program_id(2)
        qm = _head_halves(_scaled(q_ref[...]))
        cumv = cum_ref[...]
        cq = [_pick_lane(cumv, 2 * hp + e) for e in range(2)]

        def tile(j, carry, masked):
            off = pl.multiple_of(j * tq, tq)
            kj = k_ref[pl.ds(off, tq), :]
            vj = v_ref[pl.ds(off, tq), :]
            ct = cumt_ref[j]
            new = []
            for e in range(2):
                m, l, acc = carry[e]
                s = _dot_nt(qm[e], kj) - _pick_row(ct, 2 * hp + e)
                if masked:
                    s = jnp.where(_causal(tq), s, NEG)
                m_new = jnp.maximum(m, jnp.max(s, axis=1, keepdims=True))
                p = jnp.exp(s - m_new)
                alpha = jnp.exp(m - m_new)
                l = alpha * l + jnp.sum(p, axis=1, keepdims=True)
                acc = alpha * acc + _dot(p.astype(BF16), vj)
                new.append((m_new, l, acc))
            return tuple(new)

        one = (jnp.full((tq, 1), NEG, F32), jnp.zeros((tq, 1), F32), jnp.zeros((tq, LANES), F32))
        carry = lax.fori_loop(0, qi, lambda j, c: tile(j, c, False), (one, one))
        (ma, la, acca), (mb, lb, accb) = tile(qi, carry, True)
        lo = lax.broadcasted_iota(jnp.int32, (tq, LANES), 1) < HEAD_DIM
        o_ref[...] = jnp.where(lo, acca / la, accb / lb).astype(BF16)

        @pl.when(hp == 0)
        def _():
            lse_ref[...] = jnp.zeros_like(lse_ref)

        lse_ref[...] += (_put_lane(ma + jnp.log(la) + cq[0], 2 * hp) + _put_lane(mb + jnp.log(lb) + cq[1], 2 * hp + 1))

    kv = lambda first: pl.BlockSpec((S, LANES), lambda b, i, hp: (b, first + hp))
    return _pallas(
        body, "attn_fwd", (B, nq, HEAD_PAIRS),
        [pl.BlockSpec((tq, LANES), lambda b, i, hp: (b * nq + i, hp)),
         kv(ATTN_W // LANES), kv(2 * ATTN_W // LANES),
         pl.BlockSpec((tq, LANES), lambda b, i, hp: (b * nq + i, 0)),
         pl.BlockSpec((None, nq, 8, tq), lambda b, i, hp: (b, 0, 0, 0))],
        [pl.BlockSpec((tq, LANES), lambda b, i, hp: (b * nq + i, hp)),
         pl.BlockSpec((tq, LANES), lambda b, i, hp: (b * nq + i, 0))],
        [jax.ShapeDtypeStruct((B * S, ATTN_W), BF16), jax.ShapeDtypeStruct((B * S, LANES), F32)],
        [], (proj, proj, proj, cum, cum_t), comm)


def _attn_bwd(proj, o, do, lse, cum, cum_t, B, S, tq, comm=None):
    nq = S // tq

    def body(q_ref, k_ref, v_ref, o_ref, do_ref, lse_ref, cum_ref, cumt_ref,
             dq_ref, dk_ref, dv_ref, dcq_ref, dck_ref, dq_scr):
        hp, kj = pl.program_id(1), pl.program_id(2)

        @pl.when(kj == 0)
        def _():
            dq_scr[...] = jnp.zeros_like(dq_scr)

        @pl.when((kj == 0) & (hp == 0))
        def _():
            dcq_ref[...] = jnp.zeros_like(dcq_ref)
            dck_ref[...] = jnp.zeros_like(dck_ref)

        kv = k_ref[...]
        vv = v_ref[...]
        km = _head_halves(kv)
        ct = cumt_ref[...]
        ck = [_pick_row(ct, 2 * hp + e) for e in range(2)]

        def tile(i, carry, masked):
            dk, dv, dcol = carry
            off = pl.multiple_of(i * tq, tq)
            qi = q_ref[pl.ds(off, tq), :]
            ov = o_ref[pl.ds(off, tq), :].astype(F32)
            qm = _head_halves(_scaled(qi))
            dom = _head_halves(do_ref[pl.ds(off, tq), :])
            cumv = cum_ref[pl.ds(off, tq), :]
            lsev = lse_ref[pl.ds(off, tq), :]
            dcq = jnp.zeros((tq, LANES), F32)
            dq = jnp.zeros((tq, LANES), F32)
            dcol_new = []
            for e in range(2):
                delta = jnp.sum(dom[e].astype(F32) * ov, axis=1, keepdims=True)
                row_term = _pick_lane(cumv, 2 * hp + e) - _pick_lane(lsev, 2 * hp + e)
                p = jnp.exp(_dot_nt(qm[e], kv) + row_term - ck[e])
                if masked:
                    p = jnp.where(_causal(tq), p, 0.0)
                dv = dv + _dot_tn(dom[e], p.astype(BF16))
                ds = p * (_dot_nt(dom[e], vv) - delta)
                dcol_new.append(dcol[e] + jnp.sum(ds, axis=0, keepdims=True))
                dcq = dcq + _put_lane(jnp.sum(ds, axis=1, keepdims=True), 2 * hp + e)
                dsb = ds.astype(BF16)
                dk = dk + _dot_tn(qm[e], dsb)
                dq = dq + _dot(dsb, km[e]) * ATTN_SCALE
            dq_scr[pl.ds(off, tq), :] += dq
            dcq_ref[pl.ds(off, tq), :] += dcq
            return dk, dv, tuple(dcol_new)

        zero_row = jnp.zeros((1, tq), F32)
        init = (jnp.zeros((LANES, tq), F32), jnp.zeros((LANES, tq), F32), (zero_row, zero_row))
        carry = tile(kj, init, True)
        dk, dv, dcol = lax.fori_loop(kj + 1, nq, lambda i, c: tile(i, c, False), carry)
        dk_ref[...] = dk.T.astype(BF16)
        dv_ref[...] = dv.T.astype(BF16)
        dck_ref[kj] += -(_put_row(dcol[0], 2 * hp) + _put_row(dcol[1], 2 * hp + 1))

        @pl.when(kj == nq - 1)
        def _():
            dq_ref[...] = dq_scr[...].astype(BF16)

    seq = lambda first: pl.BlockSpec((S, LANES), lambda b, hp, j: (b, first + hp))
    tile_in = lambda first: pl.BlockSpec((tq, LANES), lambda b, hp, j: (b * nq + j, first + hp))
    lanes0 = pl.BlockSpec((S, LANES), lambda b, hp, j: (b, 0))
    out = jax.ShapeDtypeStruct((B * S, ATTN_W), BF16)
    return _pallas(
        body, "attn_bwd", (B, HEAD_PAIRS, nq),
        [seq(0), tile_in(ATTN_W // LANES), tile_in(2 * ATTN_W // LANES), seq(0), seq(0), lanes0, lanes0,
         pl.BlockSpec((None, None, 8, tq), lambda b, hp, j: (b, j, 0, 0))],
        [seq(0), tile_in(0), tile_in(0), lanes0,
         pl.BlockSpec((None, nq, 8, tq), lambda b, hp, j: (b, 0, 0, 0))],
        [out, out, out, jax.ShapeDtypeStruct((B * S, LANES), F32), jax.ShapeDtypeStruct((B, nq, 8, tq), F32)],
        [pltpu.VMEM((S, LANES), F32)],
        (proj, proj, proj, o, do, lse, cum, cum_t), comm)


def _shift_down(u, n):
    row = lax.broadcasted_iota(jnp.int32, u.shape, 0)
    return jnp.where(row >= n, pltpu.roll(u, n, 0), 0.0)


def _shift_up(u, n):
    rows = u.shape[0]
    row = lax.broadcasted_iota(jnp.int32, u.shape, 0)
    return jnp.where(row < rows - n, pltpu.roll(u, rows - n, 0), 0.0)


def _conv_specs(S):
    cb = pl.BlockSpec((S, LANES), lambda g, b: (b, COL_CB // LANES + g))
    cc = pl.BlockSpec((S, LANES), lambda g, b: (b, COL_CC // LANES + g))
    cx = pl.BlockSpec((S, LANES), lambda g, b: (b, COL_CX // LANES + g))
    w = pl.BlockSpec((8, LANES), lambda g, b: (0, g))
    return cb, cc, cx, w


def _conv_fwd(proj, conv_w, B, S):
    def body(cb_ref, cc_ref, cx_ref, w_ref, y_ref):
        u = cc_ref[...].astype(F32) * cx_ref[...].astype(F32)
        w = w_ref[...]
        conv = w[0:1, :] * _shift_down(u, 2) + w[1:2, :] * _shift_down(u, 1) + w[2:3, :] * u
        y_ref[...] = (cb_ref[...].astype(F32) * conv).astype(BF16)

    cb, cc, cx, w = _conv_specs(S)
    return pl.pallas_call(
        body, name="conv_fwd", grid=(CONV_W // LANES, B),
        in_specs=[cb, cc, cx, w],
        out_specs=pl.BlockSpec((S, LANES), lambda g, b: (b, g)),
        out_shape=jax.ShapeDtypeStruct((B * S, CONV_W), BF16),
        compiler_params=_params(("arbitrary", "arbitrary")),
    )(proj, proj, proj, conv_w)


def _conv_bwd(dy, proj, conv_w, B, S):
    def body(dy_ref, cb_ref, cc_ref, cx_ref, w_ref, dcb_ref, dcc_ref, dcx_ref, dw_ref):
        @pl.when(pl.program_id(1) == 0)
        def _():
            dw_ref[...] = jnp.zeros_like(dw_ref)

        ccv = cc_ref[...].astype(F32)
        cxv = cx_ref[...].astype(F32)
        u = ccv * cxv
        u1 = _shift_down(u, 1)
        u2 = _shift_down(u, 2)
        w = w_ref[...]
        conv = w[0:1, :] * u2 + w[1:2, :] * u1 + w[2:3, :] * u
        dyv = dy_ref[...].astype(F32)
        dcb_ref[...] = (dyv * conv).astype(BF16)
        dconv = dyv * cb_ref[...].astype(F32)
        du = w[2:3, :] * dconv + w[1:2, :] * _shift_up(dconv, 1) + w[0:1, :] * _shift_up(dconv, 2)
        dcc_ref[...] = (du * cxv).astype(BF16)
        dcx_ref[...] = (du * ccv).astype(BF16)
        row = lax.broadcasted_iota(jnp.int32, (8, LANES), 0)
        dw = jnp.where(row == 0, jnp.sum(dconv * u2, axis=0, keepdims=True),
                       jnp.where(row == 1, jnp.sum(dconv * u1, axis=0, keepdims=True),
                                 jnp.where(row == 2, jnp.sum(dconv * u, axis=0, keepdims=True), 0.0)))
        dw_ref[...] += dw

    cb, cc, cx, w = _conv_specs(S)
    out = pl.BlockSpec((S, LANES), lambda g, b: (b, g))
    return pl.pallas_call(
        body, name="conv_bwd", grid=(CONV_W // LANES, B),
        in_specs=[out, cb, cc, cx, w],
        out_specs=[out, out, out, w],
        out_shape=[jax.ShapeDtypeStruct((B * S, CONV_W), BF16)] * 3 + [jax.ShapeDtypeStruct((8, CONV_W), F32)],
        compiler_params=_params(("arbitrary", "arbitrary")),
    )(dy, proj, proj, proj, conv_w)


def _gate_specs(tm, D):
    ga = pl.BlockSpec((tm, D), lambda i: (i, COL_GATES // D))
    gc = pl.BlockSpec((tm, D), lambda i: (i, COL_GATES // D + 1))
    return ga, gc


def _mix_out_fwd(x, o, yc, proj, woa, woc, wout, tm):
    T, D = x.shape

    def body(x_ref, o_ref, yc_ref, ga_ref, gc_ref, woa_ref, woc_ref, wout_ref, out_ref):
        ya = _dot(o_ref[...], woa_ref[...])
        yp = _dot(yc_ref[...], woc_ref[...])
        merged = _sigmoid(ga_ref[...].astype(F32)) * ya + _sigmoid(gc_ref[...].astype(F32)) * yp
        out_ref[...] = x_ref[...] + _dot(merged.astype(BF16), wout_ref[...])

    ga, gc = _gate_specs(tm, D)
    row = lambda w: pl.BlockSpec((tm, w), lambda i: (i, 0))
    whole = lambda a: pl.BlockSpec(a.shape, lambda i: (0, 0))
    return pl.pallas_call(
        body, name="mix_out_fwd", grid=(T // tm,),
        in_specs=[row(D), row(ATTN_W), row(CONV_W), ga, gc, whole(woa), whole(woc), whole(wout)],
        out_specs=row(D),
        out_shape=jax.ShapeDtypeStruct((T, D), F32),
        compiler_params=_params(("arbitrary",)),
    )(x, o, yc, proj, proj, woa, woc, wout)


def _mix_out_bwd(dx, o, yc, proj, woa, woc, wout, tm, comm=None):
    T, D = dx.shape
    nt = T // tm

    def body(dx_ref, o_ref, yc_ref, ga_ref, gc_ref, woa_ref, woc_ref, wout_ref,
             do_ref, dyc_ref, dg_ref, dwoa_ref, dwoc_ref, dwout_ref, acca, accc, acco):
        t = pl.program_id(0)

        @pl.when(t == 0)
        def _():
            acca[...] = jnp.zeros_like(acca)
            accc[...] = jnp.zeros_like(accc)
            acco[...] = jnp.zeros_like(acco)

        dxb = dx_ref[...].astype(BF16)
        ov, ycv = o_ref[...], yc_ref[...]
        ya = _dot(ov, woa_ref[...])
        yp = _dot(ycv, woc_ref[...])
        sa = _sigmoid(ga_ref[...].astype(F32))
        sc = _sigmoid(gc_ref[...].astype(F32))
        merged = (sa * ya + sc * yp).astype(BF16)
        dm = _dot_nt(dxb, wout_ref[...])
        dya = (dm * sa).astype(BF16)
        dyp = (dm * sc).astype(BF16)
        dg_ref[:, :D] = (dm * ya * sa * (1.0 - sa)).astype(BF16)
        dg_ref[:, D:] = (dm * yp * sc * (1.0 - sc)).astype(BF16)
        do_ref[...] = _dot_nt(dya, woa_ref[...]).astype(BF16)
        dyc_ref[...] = _dot_nt(dyp, woc_ref[...]).astype(BF16)
        acca[...] += _dot_tn(ov, dya)
        accc[...] += _dot_tn(ycv, dyp)
        acco[...] += _dot_tn(merged, dxb)

        @pl.when(t == nt - 1)
        def _():
            dwoa_ref[...] = acca[...].astype(BF16)
            dwoc_ref[...] = accc[...].astype(BF16)
            dwout_ref[...] = acco[...].astype(BF16)

    ga, gc = _gate_specs(tm, D)
    row = lambda w: pl.BlockSpec((tm, w), lambda i: (i, 0))
    whole = lambda a: pl.BlockSpec(a.shape, lambda i: (0, 0))
    return _pallas(
        body, "mix_out_bwd", (nt,),
        [row(D), row(ATTN_W), row(CONV_W), ga, gc, whole(woa), whole(woc), whole(wout)],
        [row(ATTN_W), row(CONV_W), row(2 * D), whole(woa), whole(woc), whole(wout)],
        [jax.ShapeDtypeStruct((T, ATTN_W), BF16), jax.ShapeDtypeStruct((T, CONV_W), BF16),
         jax.ShapeDtypeStruct((T, 2 * D), BF16),
         jax.ShapeDtypeStruct(woa.shape, BF16), jax.ShapeDtypeStruct(woc.shape, BF16),
         jax.ShapeDtypeStruct(wout.shape, BF16)],
        [pltpu.VMEM(woa.shape, F32), pltpu.VMEM(woc.shape, F32), pltpu.VMEM(wout.shape, F32)],
        (dx, o, yc, proj, proj, woa, woc, wout), comm)


def _proj_pieces(dq, dk, dv, dcb, dcc, dcx, dgates, dflog):
    D = dgates.shape[1] // 2
    return [(dq, ATTN_W, 0), (dk, ATTN_W, 0), (dv, ATTN_W, 0), (dcb, CONV_W, 0), (dcc, CONV_W, 0), (dcx, CONV_W, 0),
            (dgates, D, 0), (dgates, D, 1), (dflog, LANES, 0)]


def _mix_proj_bwd_dx(dres, x, g, pieces, wproj_t, wf_t, tm, comm=None):
    T, D = x.shape
    n = len(pieces)
    w_blocks = [(ATTN_W, 0), (ATTN_W, 1), (ATTN_W, 2), (CONV_W, 3), (CONV_W, 4), (CONV_W, 5),
                (D, COL_GATES // D), (D, COL_GATES // D + 1)]

    def body(*refs):
        dres_ref, x_ref, g_ref = refs[:3]
        p_refs, w_refs = refs[3:3 + n], refs[3 + n:3 + 2 * n]
        dx_ref, dg_ref = refs[3 + 2 * n:]

        @pl.when(pl.program_id(0) == 0)
        def _():
            dg_ref[...] = jnp.zeros_like(dg_ref)

        dh = _dot(p_refs[0][...].astype(BF16), w_refs[0][...])
        for p_ref, w_ref in zip(p_refs[1:], w_refs[1:]):
            dh = dh + _dot(p_ref[...].astype(BF16), w_ref[...])
        xhat, inv = _rms(x_ref[...])
        dx, dg = _rms_bwd(dh, xhat, inv, g_ref[...])
        dx_ref[...] = dres_ref[...] + dx
        dg_ref[...] += dg

    row = pl.BlockSpec((tm, D), lambda i: (i, 0))
    vec = pl.BlockSpec((1, D), lambda i: (0, 0))
    p_specs = [pl.BlockSpec((tm, w), lambda i, cb=cb: (i, cb)) for _, w, cb in pieces]
    w_specs = [pl.BlockSpec((r, D), lambda i, rb=rb: (rb, 0)) for r, rb in w_blocks]
    w_specs.append(pl.BlockSpec((LANES, D), lambda i: (0, 0)))
    return _pallas(
        body, "mix_proj_bwd_dx", (T // tm,),
        [row, row, vec] + p_specs + w_specs, [row, vec],
        [jax.ShapeDtypeStruct((T, D), F32), jax.ShapeDtypeStruct((1, D), F32)], [],
        (dres, x, g, *[p for p, _, _ in pieces], *([wproj_t] * len(w_blocks)), wf_t), comm)


def _matmuls_tn(name, pieces, b, tk):
    T, N = b.shape
    nt = T // tk
    n = len(pieces)

    def body(*refs):
        a_refs, b_ref, out_refs, accs = refs[:n], refs[n], refs[n + 1:2 * n + 1], refs[2 * n + 1:]
        t = pl.program_id(0)

        @pl.when(t == 0)
        def _():
            for acc in accs:
                acc[...] = jnp.zeros_like(acc)

        bv = b_ref[...]
        for a_ref, acc in zip(a_refs, accs):
            acc[...] += _dot_tn(a_ref[...].astype(BF16), bv)

        @pl.when(t == nt - 1)
        def _():
            for out_ref, acc in zip(out_refs, accs):
                out_ref[...] = acc[...].astype(BF16)

    return pl.pallas_call(
        body, name=name, grid=(nt,),
        in_specs=[pl.BlockSpec((tk, w), lambda t, cb=cb: (t, cb)) for _, w, cb in pieces]
        + [pl.BlockSpec((tk, N), lambda t: (t, 0))],
        out_specs=[pl.BlockSpec((w, N), lambda t: (0, 0)) for _, w, _ in pieces],
        out_shape=[jax.ShapeDtypeStruct((w, N), BF16) for _, w, _ in pieces],
        scratch_shapes=[pltpu.VMEM((w, N), F32) for _, w, _ in pieces],
        compiler_params=_params(("arbitrary",)),
    )(*[a for a, _, _ in pieces], b)


def _final_loss(x, target, g, tm):
    T, D = x.shape

    def body(x_ref, t_ref, g_ref, dx_ref, loss_ref, dg_ref):
        @pl.when(pl.program_id(0) == 0)
        def _():
            loss_ref[...] = jnp.zeros_like(loss_ref)
            dg_ref[...] = jnp.zeros_like(dg_ref)

        xhat, inv = _rms(x_ref[...])
        err = xhat * g_ref[...] - t_ref[...]
        loss_ref[...] += 0.5 * jnp.sum(jnp.sum(err * err, axis=1, keepdims=True), axis=0, keepdims=True) / D
        dx, dg = _rms_bwd(err * (1.0 / D), xhat, inv, g_ref[...])
        dx_ref[...] = dx
        dg_ref[...] += dg

    row = pl.BlockSpec((tm, D), lambda i: (i, 0))
    return pl.pallas_call(
        body, name="final_loss", grid=(T // tm,),
        in_specs=[row, row, pl.BlockSpec((1, D), lambda i: (0, 0))],
        out_specs=[row, pl.BlockSpec((1, LANES), lambda i: (0, 0)), pl.BlockSpec((1, D), lambda i: (0, 0))],
        out_shape=[jax.ShapeDtypeStruct((T, D), F32), jax.ShapeDtypeStruct((1, LANES), F32),
                   jax.ShapeDtypeStruct((1, D), F32)],
        compiler_params=_params(("arbitrary",)),
    )(x, target, g)


class _LocalPlan:
    def __init__(self, stacks, small):
        self.stacks, self.small, self.grads = stacks, small, {}

    def weights(self, group):
        return _LAYOUTS[group](self.stacks, self.small)

    def rider(self, kernel_name):
        return None

    def arrived(self, kernel_name, results):
        pass

    def reduce(self, group, grads):
        self.grads.update(grads)

    def reduce_small(self, small_grads, loss):
        pass

    def ffn1_up(self, x, tm):
        w = self.weights("ffn1_in")
        return _ffn_up("ffn1_up", x, w["ffn1_norm"], w["ffn1_gate"], w["ffn1_up"], tm)[0]


def _local_step(x, target, plan, B, S):
    T, D = x.shape
    tm = min(512, T)
    tm_fwd = min(1024, T)
    tq = min(512, S)
    nq = S // tq
    ch = min(256, S)

    def riding(kernel_name, build):
        results, brought = build(plan.rider(kernel_name))
        plan.arrived(kernel_name, brought)
        return results

    hg1, hu1, n1 = plan.ffn1_up(x, tm_fwd)
    w1 = plan.weights("ffn1")
    x1, = riding("ffn1_down", lambda comm: _ffn_down("ffn1_down", x, hg1, hu1, w1["ffn1_down"], tm_fwd, comm))
    wm = plan.weights("mix")
    h, proj, flog = _mix_proj_fwd(x1, wm["mix_norm"], wm["w_proj"], wm["w_f"], tm_fwd, 1280)
    cum = _fgate_fwd(flog, wm["b_forget"], B, S, ch)
    cum_t = jnp.transpose(cum[:, :N_HEADS].reshape(B, nq, tq, N_HEADS), (0, 1, 3, 2))
    o, lse = riding("attn_fwd", lambda comm: _attn_fwd(proj, cum, cum_t, B, S, tq, comm))
    yc = _conv_fwd(proj, wm["conv_w"], B, S)
    x2 = _mix_out_fwd(x1, o, yc, proj, wm["w_o_attn"], wm["w_o_conv"], wm["w_out"], tm)
    w2 = plan.weights("ffn2")
    x3, hg2, hu2, n2 = _ffn_fwd("ffn2_fwd", x2, w2["ffn2_norm"], w2["ffn2_gate"], w2["ffn2_up"], w2["ffn2_down"], tm_fwd)[0]
    dx3, loss, d_final_norm = _final_loss(x3, target, w2["final_norm"], tm)

    g = {"final_norm": d_final_norm}
    dx2, dhg2, dhu2, g["ffn2_norm"], df2 = _ffn_bwd_dx("ffn2_bwd_dx", dx3, x2, w2["ffn2_norm"], hg2, hu2,
                                                  w2["ffn2_gate"], w2["ffn2_up"], w2["ffn2_down"], tm_fwd)[0]
    plan.reduce("ffn2", dict(zip(("ffn2_gate", "ffn2_up", "ffn2_down"),
                                 _ffn_bwd_dw("ffn2_bwd_dw", n2, df2, hg2, hu2, dhg2, dhu2, tm)[0])))
    do, dyc, dgates, dwoa, dwoc, dwout = riding("mix_out_bwd", lambda comm: _mix_out_bwd(
        dx2, o, yc, proj, wm["w_o_attn"], wm["w_o_conv"], wm["w_out"], tm, comm))
    plan.reduce("out", dict(w_o_attn=_shard_cols(dwoa), w_o_conv=_shard_cols(dwoc), w_out=dwout.reshape(N_CHIPS, -1, D)))
    dq, dk, dv, dcq, dck = riding("attn_bwd", lambda comm: _attn_bwd(proj, o, do, lse, cum, cum_t, B, S, tq, comm))
    dcum = dcq + jnp.pad(jnp.transpose(dck, (0, 1, 3, 2)).reshape(T, N_HEADS), ((0, 0), (0, LANES - N_HEADS)))
    dflog, g["b_forget"] = _fgate_bwd(dcum, flog, wm["b_forget"], B, S, ch)
    dcb, dcc, dcx, g["conv_w"] = _conv_bwd(dyc, proj, wm["conv_w"], B, S)
    pieces = _proj_pieces(dq, dk, dv, dcb, dcc, dcx, dgates, dflog)
    dwq, dwk, dwv, dwcb, dwcc, dwcx = _matmuls_tn("mix_dw_a", pieces[:6], h, tm)
    dwga, dwgc, dwf = _matmuls_tn("mix_dw_b", pieces[6:], h, tm)
    dwin_t = jnp.concatenate([dwq, dwk, dwv, dwf[:N_HEADS], dwcb, dwcc, dwcx, dwga, dwgc], axis=0)
    plan.reduce("w_in", {"w_in": dwin_t.reshape(N_CHIPS, -1, D)})
    dx1, g["mix_norm"] = riding("mix_proj_bwd_dx", lambda comm: _mix_proj_bwd_dx(
        dx2, x1, wm["mix_norm"], pieces, wm["w_proj"], wm["w_f"], min(256, T), comm))
    grad_x, dhg1, dhu1, g["ffn1_norm"], df1 = _ffn_bwd_dx(
        "ffn1_bwd_dx", dx1, x, w1["ffn1_norm"], hg1, hu1, w1["ffn1_gate"], w1["ffn1_up"], w1["ffn1_down"], tm_fwd)[0]
    plan.reduce_small(g, loss)
    plan.reduce("ffn1", dict(zip(("ffn1_gate", "ffn1_up", "ffn1_down"), riding("ffn1_bwd_dw", lambda comm: _ffn_bwd_dw(
        "ffn1_bwd_dw", n1, df1, hg1, hu1, dhg1, dhu1, tm, comm)))))
    return loss, grad_x, g


TRANSPOSED = ("ffn1_gate", "ffn1_up", "ffn2_gate", "ffn2_up", "w_in")
NORMS = ("ffn1_norm", "mix_norm", "ffn2_norm", "final_norm")


def _unshard_cols(a):
    return jnp.transpose(a, (1, 0, 2)).reshape(a.shape[1], N_CHIPS * a.shape[2])


def _shard_cols(a):
    return jnp.transpose(a.reshape(a.shape[0], N_CHIPS, a.shape[1] // N_CHIPS), (1, 0, 2))


def _layout_ffn(which):
    def layout(st, small):
        w = {n: st[n] for n in (which + "_gate", which + "_up", which + "_down")}
        w[which + "_norm"] = small[which + "_norm"].reshape(1, -1)
        if which == "ffn2":
            w["final_norm"] = small["final_norm"].reshape(1, -1)
        return w
    return layout


def _layout_mix(st, small):
    win_t = st["w_in"].reshape(-1, st["w_in"].shape[2])
    return {
        "w_proj": jnp.concatenate([win_t[:N_FORGET_COL], win_t[N_FORGET_COL + N_HEADS:]], axis=0),
        "w_f": jnp.pad(win_t[N_FORGET_COL:N_FORGET_COL + N_HEADS], ((0, LANES - N_HEADS), (0, 0))),
        "w_o_attn": _unshard_cols(st["w_o_attn"]),
        "w_o_conv": _unshard_cols(st["w_o_conv"]),
        "w_out": st["w_out"].reshape(-1, st["w_out"].shape[2]),
        "conv_w": _unshard_cols(st["conv_w"]),
        "mix_norm": small["mix_norm"].reshape(1, -1),
        "b_forget": jnp.pad(small["b_forget"].reshape(1, -1), ((0, 0), (0, LANES - N_HEADS))),
    }


def _layout_ffn1_in(st, small):
    return {"ffn1_gate": st["ffn1_gate"], "ffn1_up": st["ffn1_up"], "ffn1_norm": small["ffn1_norm"].reshape(1, -1)}


_LAYOUTS = {"ffn1_in": _layout_ffn1_in, "ffn1": _layout_ffn("ffn1"), "mix": _layout_mix, "ffn2": _layout_ffn("ffn2")}


ANY = pl.BlockSpec(memory_space=pl.ANY)
BIG = ("ffn1_gate", "ffn1_up", "ffn1_down", "w_in", "w_o_attn", "w_o_conv", "w_out",
       "ffn2_gate", "ffn2_up", "ffn2_down")


def _place():
    x, y, c = lax.axis_index("x"), lax.axis_index("y"), lax.axis_index("c")
    others = [(1 - x, y), (x, 1 - y), (1 - x, 1 - y)]
    return x, y, c, others


def _col_halves(cols, c):
    hc = cols // 2
    return pl.ds(pl.multiple_of(c * hc, LANES), hc), pl.ds(pl.multiple_of((1 - c) * hc, LANES), hc)


def _gather_comm(shards, conv_shard=None):
    n = len(shards)
    inputs = list(shards) + ([] if conv_shard is None else [conv_shard])

    def copies(ins, outs, sems):
        send_sems, recv_sems, pass_send, pass_recv = sems[:4]
        x, y, c, others = _place()

        def chip_copy(a, j, chip):
            mine, _ = _col_halves(ins[a].shape[1], c)
            return pltpu.make_async_remote_copy(
                src_ref=ins[a].at[:, mine], dst_ref=outs[a].at[chip, :, mine],
                send_sem=send_sems.at[3 * a + j], recv_sem=recv_sems.at[3 * a + j],
                device_id=(*others[j], c), device_id_type=MESH)

        def pass_copy(a, j, chip, half):
            return pltpu.make_async_remote_copy(
                src_ref=outs[a].at[chip, :, half], dst_ref=outs[a].at[chip, :, half],
                send_sem=pass_send.at[3 * a + j], recv_sem=pass_recv.at[3 * a + j],
                device_id=(x, y, 1 - c), device_id_type=MESH)

        def conv_copy(j, chip):
            return pltpu.make_async_remote_copy(
                src_ref=ins[n], dst_ref=outs[n].at[chip],
                send_sem=sems[4].at[j], recv_sem=sems[5].at[j],
                device_id=(*others[j], c), device_id_type=MESH)

        me = 2 * x + y
        sends = [chip_copy(a, j, me) for a in range(n) for j in range(3)]
        if conv_shard is not None:
            sends += [conv_copy(j, me) for j in range(3)]
        return c, others, sends, chip_copy, pass_copy, conv_copy

    def start(ins, outs, sems):
        for cp in copies(ins, outs, sems)[2]:
            cp.start()

    def finish(ins, outs, sems):
        c, others, sends, chip_copy, pass_copy, conv_copy = copies(ins, outs, sems)
        passed = []
        for a in range(n):
            mine, _ = _col_halves(ins[a].shape[1], c)
            for j, (ox, oy) in enumerate(others):
                chip_copy(a, j, 2 * ox + oy).wait_recv()
                passed.append(pass_copy(a, j, 2 * ox + oy, mine))
                passed[-1].start()
        for a in range(n):
            _, theirs = _col_halves(ins[a].shape[1], c)
            for j, (ox, oy) in enumerate(others):
                pass_copy(a, j, 2 * ox + oy, theirs).wait_recv()
        if conv_shard is not None:
            for j, (ox, oy) in enumerate(others):
                conv_copy(j, 2 * ox + oy).wait_recv()
        for cp in sends + passed:
            cp.wait_send()

    scratch = [pltpu.SemaphoreType.DMA((3 * n,))] * 4
    if conv_shard is not None:
        scratch += [pltpu.SemaphoreType.DMA((3,))] * 2
    return _Comm(inputs, [jax.ShapeDtypeStruct((N_CHIPS,) + s.shape, s.dtype) for s in inputs], scratch, start, finish)


def _fill_own(stacks, shards):
    chip = 2 * lax.axis_index("x") + lax.axis_index("y")
    return [lax.dynamic_update_index_in_dim(st, s, chip, 0) for st, s in zip(stacks, shards)]


def _run_comm(name, comm):
    ci, co = len(comm.inputs), len(comm.out_shape)

    def body(*refs):
        comm.start(refs[:ci], refs[ci:ci + co], refs[ci + co:])
        comm.finish(refs[:ci], refs[ci:ci + co], refs[ci + co:])

    return pl.pallas_call(body, name=name, in_specs=[ANY] * ci, out_specs=[ANY] * co, out_shape=comm.out_shape,
                          scratch_shapes=comm.scratch)(*comm.inputs)


def _sibling_exchange_comm(grads):
    n = len(grads)

    def copies(ins, outs, sems):
        x, y, c, _ = _place()
        return [pltpu.make_async_remote_copy(
            src_ref=ins[a].at[:, :, _col_halves(ins[a].shape[2], c)[1]], dst_ref=outs[a],
            send_sem=sems[0].at[a], recv_sem=sems[1].at[a],
            device_id=(x, y, 1 - c), device_id_type=MESH) for a in range(n)]

    def start(ins, outs, sems):
        for cp in copies(ins, outs, sems):
            cp.start()

    def finish(ins, outs, sems):
        for cp in copies(ins, outs, sems):
            cp.wait()

    half = lambda s: jax.ShapeDtypeStruct((s.shape[0], s.shape[1], s.shape[2] // 2), s.dtype)
    return _Comm(grads, [half(s) for s in grads], [pltpu.SemaphoreType.DMA((n,))] * 2, start, finish)


def _merge_comms(comms):
    def split(refs, count):
        out, at = [], 0
        for cm in comms:
            out.append(refs[at:at + count(cm)])
            at += count(cm)
        return out

    def parts(ins, outs, sems):
        return zip(comms, split(ins, lambda cm: len(cm.inputs)), split(outs, lambda cm: len(cm.out_shape)),
                   split(sems, lambda cm: len(cm.scratch)))

    def start(ins, outs, sems):
        for cm, i, o, s in parts(ins, outs, sems):
            cm.start(i, o, s)

    def finish(ins, outs, sems):
        for cm, i, o, s in parts(ins, outs, sems):
            cm.finish(i, o, s)

    return _Comm(sum([cm.inputs for cm in comms], []), sum([cm.out_shape for cm in comms], []),
                 sum([cm.scratch for cm in comms], []), start, finish)


def _add_halves(name, grads, recvs, core):
    n = len(grads)

    def body(core_ref, *refs):
        for g_ref, r_ref, out_ref in zip(refs[:n], refs[n:2 * n], refs[2 * n:]):
            out_ref[...] = (g_ref[...].astype(F32) + r_ref[...].astype(F32)).astype(BF16)

    half = lambda g: pl.BlockSpec((None, g.shape[1], g.shape[2] // 2), lambda k, core_ref: (k, 0, 0))
    mine = lambda g: pl.BlockSpec((None, g.shape[1], g.shape[2] // 2), lambda k, core_ref: (k, 0, core_ref[0]))
    return pl.pallas_call(
        body, name=name,
        grid_spec=pltpu.PrefetchScalarGridSpec(
            num_scalar_prefetch=1, grid=(N_CHIPS,),
            in_specs=[mine(g) for g in grads] + [half(g) for g in grads],
            out_specs=[half(g) for g in grads]),
        out_shape=[jax.ShapeDtypeStruct(r.shape, BF16) for r in recvs],
        compiler_params=_params(("arbitrary",)),
    )(core, *grads, *recvs)


def _chip_exchange_comm(parts):
    n = len(parts)

    def copies(ins, outs, sems):
        x, y, c, others = _place()
        return [pltpu.make_async_remote_copy(
            src_ref=ins[a].at[2 * ox + oy], dst_ref=outs[a].at[j],
            send_sem=sems[0].at[3 * a + j], recv_sem=sems[1].at[3 * a + j],
            device_id=(ox, oy, c), device_id_type=MESH) for a in range(n) for j, (ox, oy) in enumerate(others)]

    def start(ins, outs, sems):
        for cp in copies(ins, outs, sems):
            cp.start()

    def finish(ins, outs, sems):
        for cp in copies(ins, outs, sems):
            cp.wait()

    return _Comm(parts, [jax.ShapeDtypeStruct((3,) + s.shape[1:], s.dtype) for s in parts],
                 [pltpu.SemaphoreType.DMA((3 * n,))] * 2, start, finish)


HBM = pl.BlockSpec(memory_space=pltpu.HBM)
SEM = pl.BlockSpec(memory_space=pltpu.SEMAPHORE)


def _split_exchange_copies(parts, lands, send_sems, recv_sems):
    x, y, c, others = _place()
    return [pltpu.make_async_remote_copy(
        src_ref=parts[a].at[2 * ox + oy], dst_ref=lands[a].at[j],
        send_sem=send_sems.at[3 * a + j], recv_sem=recv_sems.at[3 * a + j],
        device_id=(ox, oy, c), device_id_type=MESH) for a in range(len(parts)) for j, (ox, oy) in enumerate(others)]


def _exchange_start(name, parts):
    n = len(parts)

    def body(*refs):
        ins, lands = refs[:n], refs[n:2 * n]
        send_sems, recv_sems, token = refs[2 * n], refs[2 * n + 1], refs[-1]
        for cp in _split_exchange_copies(ins, lands, send_sems, recv_sems):
            cp.start()
        token[...] = jnp.zeros_like(token)

    land_shape = [(3,) + p.shape[1:] for p in parts]
    outs = pl.pallas_call(
        body, name=name,
        out_shape=[pltpu.SemaphoreType.DMA((3 * n,)), pltpu.SemaphoreType.DMA((3 * n,))]
        + [pltpu.HBM(p.shape, p.dtype) for p in parts] + [pltpu.HBM(s, p.dtype) for s, p in zip(land_shape, parts)]
        + [jax.ShapeDtypeStruct((8, LANES), F32)],
        in_specs=[HBM] * (2 * n), out_specs=[SEM, SEM] + [HBM] * (2 * n) + [pl.BlockSpec(memory_space=pltpu.VMEM)],
        input_output_aliases={i: 2 + i for i in range(2 * n)},
        compiler_params=pltpu.CompilerParams(has_side_effects=pltpu.SideEffectType.DATAFLOW_SIDE_EFFECTING),
    )(*[pltpu.with_memory_space_constraint(p, pltpu.HBM) for p in parts],
      *[pltpu.with_memory_space_constraint(lax.empty(s, p.dtype), pltpu.HBM) for s, p in zip(land_shape, parts)])
    return outs[0], outs[1], list(outs[2:2 + n]), list(outs[2 + n:2 + 2 * n]), outs[-1]


def _exchange_wait(name, send_sems, recv_sems, parts, lands, after):
    n = len(parts)

    def body(*refs):
        ins, zones = refs[:n], refs[n:2 * n]
        for cp in _split_exchange_copies(ins, zones, refs[2 * n], refs[2 * n + 1]):
            cp.wait_send()
            cp.wait_recv()

    outs = pl.pallas_call(
        body, name=name,
        out_shape=[pltpu.HBM(p.shape, p.dtype) for p in parts] + [pltpu.HBM(z.shape, z.dtype) for z in lands],
        in_specs=[HBM] * (2 * n) + [SEM, SEM] + [ANY] * len(after), out_specs=[HBM] * (2 * n),
        input_output_aliases={i: i for i in range(2 * n)},
        compiler_params=pltpu.CompilerParams(has_side_effects=pltpu.SideEffectType.DATAFLOW_SIDE_EFFECTING),
    )(*parts, *lands, send_sems, recv_sems, *after)
    return list(outs[:n]), list(outs[n:])


def _sum_chips(name, owns, recvs, chip, after):
    n = len(owns)
    hc = owns[0].shape[2]
    assert all(o.shape[2] == hc for o in owns)

    def body(chip_ref, *refs):
        for own_ref, recv_ref, out_ref in zip(refs[:n], refs[n:2 * n], refs[2 * n + 1:]):
            acc = own_ref[...].astype(F32)
            for j in range(3):
                acc = acc + recv_ref[j].astype(F32)
            out_ref[...] = acc

    return pl.pallas_call(
        body, name=name,
        grid_spec=pltpu.PrefetchScalarGridSpec(
            num_scalar_prefetch=1, grid=(hc // LANES,),
            in_specs=[pl.BlockSpec((None, o.shape[1], LANES), lambda i, chip_ref: (chip_ref[0], 0, i)) for o in owns]
            + [pl.BlockSpec((3, o.shape[1], LANES), lambda i, chip_ref: (0, 0, i)) for o in owns]
            + [pl.BlockSpec((8, LANES), lambda i, chip_ref: (0, 0))],
            out_specs=[pl.BlockSpec((o.shape[1], LANES), lambda i, chip_ref: (0, i)) for o in owns]),
        out_shape=[jax.ShapeDtypeStruct((o.shape[1], hc), F32) for o in owns],
        compiler_params=_params(("arbitrary",)),
    )(chip, *owns, *recvs, after)


def _share_halves(name, halves):
    n = len(halves)

    def body(*refs):
        srcs, dsts = refs[:n], refs[n:2 * n]
        send_sems, recv_sems = refs[2 * n:]
        x, y, c, _ = _place()
        copies = [pltpu.make_async_remote_copy(
            src_ref=srcs[a], dst_ref=dsts[a], send_sem=send_sems.at[a], recv_sem=recv_sems.at[a],
            device_id=(x, y, 1 - c), device_id_type=MESH) for a in range(n)]
        for cp in copies:
            cp.start()
        for cp in copies:
            cp.wait()

    return pl.pallas_call(
        body, name=name,
        in_specs=[ANY] * n, out_specs=[ANY] * n,
        out_shape=[jax.ShapeDtypeStruct(s.shape, s.dtype) for s in halves],
        scratch_shapes=[pltpu.SemaphoreType.DMA((n,)), pltpu.SemaphoreType.DMA((n,))],
    )(*halves)


def _small_gather_comm(part):
    def copies(ins, outs, sems):
        x, y, c, _ = _place()
        me = 4 * x + 2 * y + c
        both = []
        for d in range(1, N_DEV):
            px, py, pc = (1 - x if d & 4 else x, 1 - y if d & 2 else y, 1 - c if d & 1 else c)
            send = pltpu.make_async_remote_copy(
                src_ref=ins[0], dst_ref=outs[0].at[me], send_sem=sems[0].at[d - 1], recv_sem=sems[1].at[d - 1],
                device_id=(px, py, pc), device_id_type=MESH)
            recv = pltpu.make_async_remote_copy(
                src_ref=ins[0], dst_ref=outs[0].at[4 * px + 2 * py + pc], send_sem=sems[0].at[d - 1],
                recv_sem=sems[1].at[d - 1], device_id=(px, py, pc), device_id_type=MESH)
            both.append((send, recv))
        return both

    def start(ins, outs, sems):
        for send, _ in copies(ins, outs, sems):
            send.start()

    def finish(ins, outs, sems):
        for send, recv in copies(ins, outs, sems):
            recv.wait_recv()
            send.wait_send()

    return _Comm([part], [jax.ShapeDtypeStruct((N_DEV,) + part.shape, F32)],
                 [pltpu.SemaphoreType.DMA((N_DEV - 1,))] * 2, start, finish)


def _sum_devices(parts):
    def body(p_ref, out_ref):
        acc = p_ref[0]
        for k in range(1, N_DEV):
            acc = acc + p_ref[k]
        out_ref[...] = acc

    return pl.pallas_call(
        body, name="sum_devices", grid=(1,),
        in_specs=[pl.BlockSpec(parts.shape, lambda i: (0, 0, 0))],
        out_specs=pl.BlockSpec(parts.shape[1:], lambda i: (0, 0)),
        out_shape=jax.ShapeDtypeStruct(parts.shape[1:], F32),
        compiler_params=_params(("arbitrary",)),
    )(parts)


def _adam_update(w, g, m, v):
    nm = ADAM_B1 * m + (1.0 - ADAM_B1) * g
    nv = ADAM_B2 * v + (1.0 - ADAM_B2) * (g * g)
    m_hat = nm * (1.0 / (1.0 - ADAM_B1 ** ADAM_STEP))
    v_hat = nv * (1.0 / (1.0 - ADAM_B2 ** ADAM_STEP))
    return -ADAM_LR * (m_hat / (jnp.sqrt(v_hat) + ADAM_EPS) + ADAM_WD * w), nm, nv


def _adamw(name, w, g, m, v):
    def body(w_ref, g_ref, m_ref, v_ref, d_ref, nm_ref, nv_ref):
        d_ref[...], nm_ref[...], nv_ref[...] = _adam_update(w_ref[...], g_ref[...], m_ref[...], v_ref[...])

    spec = pl.BlockSpec(w.shape, lambda i: (0, 0))
    out = jax.ShapeDtypeStruct(w.shape, F32)
    return pl.pallas_call(
        body, name=name, grid=(1,),
        in_specs=[spec] * 4, out_specs=[spec] * 3, out_shape=[out] * 3,
        compiler_params=_params(("arbitrary",)),
    )(w, g, m, v)


def _adamw_halves(name, ws, mines, theirs, ms, vs, core):
    n = len(ws)
    cols = ws[0].shape[1]
    assert all(w.shape[1] == cols for w in ws)
    hc = cols // 2
    tc = LANES if n > 1 else min(256, hc)
    nt = hc // tc

    def body(core_ref, *refs):
        ins, outs = refs[:5 * n], refs[5 * n:]
        for a in range(n):
            w_ref, mine_ref, theirs_ref, m_ref, v_ref = [ins[j * n + a] for j in range(5)]
            g_ref, d_ref, nm_ref, nv_ref = outs[4 * a:4 * a + 4]
            gv = jnp.where(pl.program_id(0) == core_ref[0], mine_ref[...], theirs_ref[...])
            g_ref[...] = gv
            d_ref[...], nm_ref[...], nv_ref[...] = _adam_update(w_ref[...], gv, m_ref[...], v_ref[...])

    whole = lambda w: pl.BlockSpec((w.shape[0], tc), lambda h, i, core_ref: (0, h * nt + i))
    mine_spec = lambda w: pl.BlockSpec((w.shape[0], tc), lambda h, i, core_ref: (0, jnp.where(h == core_ref[0], i, 0)))
    theirs_spec = lambda w: pl.BlockSpec((w.shape[0], tc), lambda h, i, core_ref: (0, jnp.where(h == core_ref[0], 0, i)))
    outs = pl.pallas_call(
        body, name=name,
        grid_spec=pltpu.PrefetchScalarGridSpec(
            num_scalar_prefetch=1, grid=(2, nt),
            in_specs=[whole(w) for w in ws] + [mine_spec(w) for w in ws] + [theirs_spec(w) for w in ws]
            + [whole(w) for w in ws] * 2,
            out_specs=[whole(w) for w in ws for _ in range(4)]),
        out_shape=[jax.ShapeDtypeStruct(w.shape, F32) for w in ws for _ in range(4)],
        compiler_params=_params(("arbitrary", "arbitrary")),
    )(core, *ws, *mines, *theirs, *ms, *vs)
    return [outs[4 * a:4 * a + 4] for a in range(n)]


WEIGHTS = ("ffn1_norm", "ffn1_gate", "ffn1_up", "ffn1_down", "mix_norm", "w_in", "b_forget", "conv_w",
           "w_o_attn", "w_o_conv", "w_out", "ffn2_norm", "ffn2_gate", "ffn2_up", "ffn2_down", "final_norm")
VEC_ROWS = 8


def _pack_small(t, conv_rows):
    conv = t["conv_w"]
    parts = [t[n].reshape(VEC_ROWS, LANES) for n in NORMS]
    parts.append(jnp.pad(conv, ((0, conv_rows - conv.shape[0]), (0, 0))))
    parts.append(jnp.pad(t["b_forget"].reshape(1, N_HEADS), ((0, 7), (0, LANES - N_HEADS))))
    return jnp.concatenate(parts, axis=0)


def _unpack_small(p, conv_rows):
    out = {n: p[VEC_ROWS * i:VEC_ROWS * (i + 1)].reshape(-1) for i, n in enumerate(NORMS)}
    base = VEC_ROWS * len(NORMS)
    out["conv_w"] = p[base:base + 3]
    out["b_forget"] = p[base + conv_rows, :N_HEADS]
    return out


def _travel(name, a):
    return a.T if name in TRANSPOSED else a


GATHER_FIRST = ("ffn1_gate", "ffn1_up")
GATHER_RIDES = {"ffn1_up": ("ffn1_down", "w_in"), "ffn1_down": ("w_o_attn", "w_o_conv", "w_out"),
                "attn_fwd": ("ffn2_gate", "ffn2_up", "ffn2_down")}
SIBLING_RIDES = {"ffn2": "mix_out_bwd", "out": None, "w_in": "mix_proj_bwd_dx", "ffn1": None}
CHIP_RIDES = {"ffn2": "attn_bwd", "out": "attn_bwd", "w_in": "ffn1_bwd_dw", "ffn1": None}
SMALL_RIDE = "ffn1_bwd_dw"


class _MeshPlan(_LocalPlan):
    def __init__(self, wts, core):
        self.small, self.core = wts, core
        self.shards = {n: wts[n].astype(BF16) for n in BIG}
        self.chip_part, self.from_chips, self.rides = {}, {}, {}
        self.stacks = {}
        conv_shard = jnp.pad(wts["conv_w"], ((0, 8 - wts["conv_w"].shape[0]), (0, 0)))
        for kernel_name, names in GATHER_RIDES.items():
            mine = [self.shards[n] for n in names]
            conv = conv_shard if kernel_name == "ffn1_up" else None
            names = names + (("conv_w",) if conv is not None else ())
            mine = mine + ([conv] if conv is not None else [])
            self._ride(kernel_name, _gather_comm(mine[:len(mine) - (conv is not None)], conv),
                       lambda got, names=names, mine=mine: self.stacks.update(zip(names, _fill_own(got, mine))))

    def ffn1_up(self, x, tm):
        px, py = lax.axis_index("x"), lax.axis_index("y")
        order = jnp.stack([2 * px + py, 2 * (1 - px) + py, 2 * px + (1 - py), 2 * (1 - px) + (1 - py)]).astype(jnp.int32)
        own = [self.shards[n] for n in GATHER_FIRST]
        (hg, hu, n, sg, su), brought = _ffn_up_gather("ffn1_up", x, self.small["ffn1_norm"].reshape(1, -1), *own, order,
                                                     tm, self.rider("ffn1_up"))
        self.stacks.update(zip(GATHER_FIRST, _fill_own([sg, su], own)))
        self.arrived("ffn1_up", brought)
        return hg, hu, n

    def _ride(self, kernel_name, comm, then):
        self.rides.setdefault(kernel_name, []).append((comm, then))

    def rider(self, kernel_name):
        comms = [comm for comm, _ in self.rides.get(kernel_name, [])]
        return _merge_comms(comms) if comms else None

    def arrived(self, kernel_name, results):
        for comm, then in self.rides.pop(kernel_name, []):
            then(results[:len(comm.out_shape)])
            results = results[len(comm.out_shape):]

    def reduce(self, group, grads):
        names = tuple(grads)
        mine = [grads[n] for n in names]

        def with_sibling(from_sibling):
            parts = _add_halves("add_halves_" + group, mine, list(from_sibling), self.core)
            self.chip_part.update(zip(names, parts))
            if CHIP_RIDES[group] is None:
                self.last = (names, _exchange_start("exchange_start_" + group, parts))
            else:
                self._ride(CHIP_RIDES[group], _chip_exchange_comm(parts),
                           lambda got: self.from_chips.update(zip(names, got)))

        if SIBLING_RIDES[group] is None:
            with_sibling(_run_comm("sibling_exchange_" + group, _sibling_exchange_comm(mine)))
        else:
            self._ride(SIBLING_RIDES[group], _sibling_exchange_comm(mine), with_sibling)

    def reduce_small(self, gs, loss):
        conv_all = _shard_cols(gs["conv_w"]).reshape(N_CHIPS * 8, LANES)
        part = _pack_small({**{n: gs[n] for n in NORMS}, "conv_w": conv_all, "b_forget": gs["b_forget"][0, :N_HEADS]},
                           N_CHIPS * 8)
        part = jnp.concatenate([part, jnp.broadcast_to(loss, (8, LANES))], axis=0)
        me = 4 * lax.axis_index("x") + 2 * lax.axis_index("y") + lax.axis_index("c")

        def landed(got):
            self.small_parts = lax.dynamic_update_index_in_dim(got[0], part, me, 0)

        self._ride(SMALL_RIDE, _small_gather_comm(part), landed)


def kernel(x, ffn1_norm, ffn1_gate, ffn1_up, ffn1_down, mix_norm, w_in, b_forget, conv_w, w_o_attn, w_o_conv, w_out, ffn2_norm, ffn2_gate, ffn2_up, ffn2_down, final_norm, loss_target, m_ffn1_norm, m_ffn1_gate, m_ffn1_up, m_ffn1_down, m_mix_norm, m_w_in, m_b_forget, m_conv_w, m_w_o_attn, m_w_o_conv, m_w_out, m_ffn2_norm, m_ffn2_gate, m_ffn2_up, m_ffn2_down, m_final_norm, v_ffn1_norm, v_ffn1_gate, v_ffn1_up, v_ffn1_down, v_mix_norm, v_w_in, v_b_forget, v_conv_w, v_w_o_attn, v_w_o_conv, v_w_out, v_ffn2_norm, v_ffn2_gate, v_ffn2_up, v_ffn2_down, v_final_norm):
    given = dict(locals())
    wts = {n: _travel(n, given[n]) for n in WEIGHTS}
    mom = {n: _travel(n, given["m_" + n]) for n in WEIGHTS}
    var = {n: _travel(n, given["v_" + n]) for n in WEIGHTS}
    B, S, D = x.shape
    chip = 2 * lax.axis_index("x") + lax.axis_index("y")
    chip1 = chip.astype(jnp.int32).reshape(1)
    core = lax.axis_index("c").astype(jnp.int32).reshape(1)

    plan = _MeshPlan(wts, core)
    loss, grad_x, gs = _local_step(x.reshape(B * S, D), loss_target.reshape(B * S, D), plan, B, S)

    last_names, (send_sems, recv_sems, parts_thru, lands, token) = plan.last
    delta, new_m, new_v, grads = {}, {}, {}, {}

    def finish(tag, names):
        by_cols = {}
        for n in names:
            by_cols.setdefault(wts[n].shape[1], []).append(n)
        mine = {}
        for cols, ns in by_cols.items():
            mine.update(zip(ns, _sum_chips("sum_chips_%s_%d" % (tag, cols), [plan.chip_part[n] for n in ns],
                                           [plan.from_chips[n] for n in ns], chip1, token)))
        theirs = dict(zip(names, _share_halves("share_halves_" + tag, [mine[n] for n in names])))
        raw = []
        for cols, ns in by_cols.items():
            outs = _adamw_halves("adamw_%s_%d" % (tag, cols), [wts[n] for n in ns], [mine[n] for n in ns],
                                 [theirs[n] for n in ns], [mom[n] for n in ns], [var[n] for n in ns], core)
            for n, per in zip(ns, outs):
                raw.append(per[-1])
                grads[n], delta[n], new_m[n], new_v[n] = [_travel(n, o) for o in per]
        return raw

    small_sum = _sum_devices(plan.small_parts)
    base = VEC_ROWS * len(NORMS)
    loss_row = small_sum.shape[0] - 8
    small_grads = _unpack_small(small_sum, N_CHIPS * 8)
    small_grads["conv_w"] = lax.dynamic_slice_in_dim(small_sum[base:base + N_CHIPS * 8], chip * 8, 8, axis=0)[:3]
    packs = [_pack_small(t, 8) for t in (wts, small_grads, mom, var)]
    small_out = _adamw("adamw_small", *packs)

    done = finish("early", [n for n in BIG if n not in last_names])
    parts_back, got = _exchange_wait("exchange_wait", send_sems, recv_sems, parts_thru, lands, done + list(small_out))
    plan.chip_part.update(zip(last_names, parts_back))
    plan.from_chips.update(zip(last_names, got))
    finish("last", last_names)
    grads.update(small_grads)
    for out, p in zip((delta, new_m, new_v), small_out):
        out.update(_unpack_small(p, 8))

    return (small_sum[loss_row, 0], grad_x.reshape(B, S, D), *[grads[n] for n in WEIGHTS], *[delta[n] for n in WEIGHTS],
            *[new_m[n] for n in WEIGHTS], *[new_v[n] for n in WEIGHTS])
```

```python
import functools
import math

import jax
import jax.numpy as jnp
from jax import lax
from jax.experimental import pallas as pl
from jax.experimental.pallas import tpu as pltpu

F32 = jnp.float32
BF16 = jnp.bfloat16
MESH = pl.DeviceIdType.MESH

N_CHIPS = 4
N_DEV = 8
N_HEADS = 8
HEAD_DIM = 64
HEAD_PAIRS = N_HEADS // 2
ATTN_W = N_HEADS * HEAD_DIM
CONV_W = 512
RMS_EPS = 1e-6
FFN_RES = 0.5
LANES = 128
VMEM_LIMIT = 56 * 1024 * 1024
ROW_BLOCK = 256

ADAM_LR = 0.001
ADAM_B1 = 0.9
ADAM_B2 = 0.999
ADAM_EPS = 1e-08
ADAM_WD = 0.01
ADAM_STEP = 10

PROJ_W = 3 * ATTN_W + 3 * CONV_W + 2 * 1024
COL_CB, COL_CC, COL_CX = 3 * ATTN_W, 3 * ATTN_W + CONV_W, 3 * ATTN_W + 2 * CONV_W
COL_GATES = 3 * ATTN_W + 3 * CONV_W
N_FORGET_COL = 3 * ATTN_W


def _params(sem=None, vmem=VMEM_LIMIT):
    return pltpu.CompilerParams(dimension_semantics=sem, vmem_limit_bytes=vmem)


def _dot(a, b):
    return lax.dot_general(a, b, (((1,), (0,)), ((), ())), preferred_element_type=F32)


def _dot_nt(a, b):
    return lax.dot_general(a, b, (((1,), (1,)), ((), ())), preferred_element_type=F32)


def _dot_tn(a, b):
    return lax.dot_general(a, b, (((0,), (0,)), ((), ())), preferred_element_type=F32)


def _sigmoid(x):
    return 1.0 / (1.0 + jnp.exp(-x))


def _rms(xv):
    inv = lax.rsqrt(jnp.mean(xv * xv, axis=-1, keepdims=True) + RMS_EPS)
    return xv * inv, inv


class _Comm:
    def __init__(self, inputs, out_shape, scratch, start, finish):
        self.inputs, self.out_shape, self.scratch = list(inputs), list(out_shape), list(scratch)
        self.start, self.finish = start, finish


def _pallas(body, name, grid, in_specs, out_specs, out_shape, scratch, args, comm=None):
    sem = ("arbitrary",) * len(grid)
    if comm is None:
        outs = pl.pallas_call(body, name=name, grid=grid, in_specs=in_specs, out_specs=out_specs,
                              out_shape=out_shape, scratch_shapes=scratch, compiler_params=_params(sem))(*args)
        return list(outs), []
    n_in, n_out, n_scr = len(in_specs), len(out_specs), len(scratch)
    ci, co = len(comm.inputs), len(comm.out_shape)

    def riding(*refs):
        ins, refs = refs[:n_in], refs[n_in:]
        cins, refs = refs[:ci], refs[ci:]
        outs, refs = refs[:n_out], refs[n_out:]
        couts, refs = refs[:co], refs[co:]
        scr, sems = refs[:n_scr], refs[n_scr:]
        ids = [pl.program_id(d) for d in range(len(grid))]
        first = functools.reduce(lambda a, b: a & b, [i == 0 for i in ids])
        last = functools.reduce(lambda a, b: a & b, [i == g - 1 for i, g in zip(ids, grid)])

        @pl.when(first)
        def _():
            comm.start(cins, couts, sems)

        body(*ins, *outs, *scr)

        @pl.when(last)
        def _():
            comm.finish(cins, couts, sems)

    any_spec = pl.BlockSpec(memory_space=pl.ANY)
    outs = pl.pallas_call(
        riding, name=name, grid=grid,
        in_specs=list(in_specs) + [any_spec] * ci, out_specs=list(out_specs) + [any_spec] * co,
        out_shape=list(out_shape) + comm.out_shape, scratch_shapes=list(scratch) + comm.scratch,
        compiler_params=_params(sem))(*args, *comm.inputs)
    return list(outs[:n_out]), list(outs[n_out:])


def _rms_bwd(dn, xhat, inv, g):
    dxhat = dn * g
    dx = inv * (dxhat - xhat * jnp.mean(dxhat * xhat, axis=-1, keepdims=True))
    return dx, jnp.sum(dn * xhat, axis=0, keepdims=True)


def _ffn_fwd(name, x, g, wgt, wut, wd, tm, comm=None):
    T, D = x.shape
    K, Fs, _ = wgt.shape

    def body(x_ref, g_ref, wg_ref, wu_ref, wd_ref, out_ref, hg_ref, hu_ref, n_ref, acc_scr):
        k = pl.program_id(1)

        @pl.when(k == 0)
        def _():
            xhat, _ = _rms(x_ref[...])
            n_ref[...] = (xhat * g_ref[...]).astype(BF16)
            acc_scr[...] = jnp.zeros_like(acc_scr)

        n = n_ref[...]
        hg = _dot_nt(n, wg_ref[...])
        hu = _dot_nt(n, wu_ref[...])
        hg_ref[...] = hg.astype(BF16)
        hu_ref[...] = hu.astype(BF16)
        act = (hg * _sigmoid(hg) * hu).astype(BF16)
        acc_scr[...] += _dot(act, wd_ref[...])

        @pl.when(k == K - 1)
        def _():
            out_ref[...] = x_ref[...] + FFN_RES * acc_scr[...]

    w_spec = pl.BlockSpec((None, Fs, D), lambda i, k: (k, 0, 0))
    act_spec = pl.BlockSpec((None, tm, Fs), lambda i, k: (k, i, 0))
    return _pallas(
        body, name, (T // tm, K),
        [pl.BlockSpec((tm, D), lambda i, k: (i, 0)), pl.BlockSpec((1, D), lambda i, k: (0, 0)),
         w_spec, w_spec, w_spec],
        [pl.BlockSpec((tm, D), lambda i, k: (i, 0)), act_spec, act_spec, pl.BlockSpec((tm, D), lambda i, k: (i, 0))],
        [jax.ShapeDtypeStruct((T, D), F32), jax.ShapeDtypeStruct((K, T, Fs), BF16),
         jax.ShapeDtypeStruct((K, T, Fs), BF16), jax.ShapeDtypeStruct((T, D), BF16)],
        [pltpu.VMEM((tm, D), F32)],
        (x, g, wgt, wut, wd), comm)


def _ffn_up(name, x, g, wgt, wut, tm, comm=None):
    T, D = x.shape
    K, Fs, _ = wgt.shape

    def body(x_ref, g_ref, wg_ref, wu_ref, hg_ref, hu_ref, n_ref):
        @pl.when(pl.program_id(1) == 0)
        def _():
            xhat, _ = _rms(x_ref[...])
            n_ref[...] = (xhat * g_ref[...]).astype(BF16)

        n = n_ref[...]
        hg_ref[...] = _dot_nt(n, wg_ref[...]).astype(BF16)
        hu_ref[...] = _dot_nt(n, wu_ref[...]).astype(BF16)

    w_spec = pl.BlockSpec((None, Fs, D), lambda i, k: (k, 0, 0))
    act_spec = pl.BlockSpec((None, tm, Fs), lambda i, k: (k, i, 0))
    return _pallas(
        body, name, (T // tm, K),
        [pl.BlockSpec((tm, D), lambda i, k: (i, 0)), pl.BlockSpec((1, D), lambda i, k: (0, 0)), w_spec, w_spec],
        [act_spec, act_spec, pl.BlockSpec((tm, D), lambda i, k: (i, 0))],
        [jax.ShapeDtypeStruct((K, T, Fs), BF16), jax.ShapeDtypeStruct((K, T, Fs), BF16),
         jax.ShapeDtypeStruct((T, D), BF16)],
        [],
        (x, g, wgt, wut), comm)


def _ffn_up_gather(name, x, g, wg_own, wu_own, order, tm, comm=None):
    T, D = x.shape
    Fs = wg_own.shape[0]
    nt = T // tm
    ci, co = (len(comm.inputs), len(comm.out_shape)) if comm is not None else (0, 0)

    def body(order_ref, x_ref, g_ref, wgo_ref, wuo_ref, *rest):
        cins, rest = rest[:ci], rest[ci:]
        (hg_ref, hu_ref, n_ref, sg_ref, su_ref), rest = rest[:5], rest[5:]
        couts, rest = rest[:co], rest[co:]
        (n_all, wbuf, send_sems, recv_sems, pass_send, pass_recv, load_sems), csems = rest[:7], rest[7:]
        k, i = pl.program_id(0), pl.program_id(1)
        x_pos, y_pos, c, others = _place()
        me = 2 * x_pos + y_pos
        owns, stacks = (wgo_ref, wuo_ref), (sg_ref, su_ref)
        mine, theirs = _col_halves(D, c)

        def chip_copy(a, j, chip):
            return pltpu.make_async_remote_copy(
                src_ref=owns[a].at[:, mine], dst_ref=stacks[a].at[chip, :, mine],
                send_sem=send_sems.at[3 * a + j], recv_sem=recv_sems.at[3 * a + j],
                device_id=(*others[j], c), device_id_type=MESH)

        def pass_copy(a, j, chip, half):
            return pltpu.make_async_remote_copy(
                src_ref=stacks[a].at[chip, :, half], dst_ref=stacks[a].at[chip, :, half],
                send_sem=pass_send.at[3 * a + j], recv_sem=pass_recv.at[3 * a + j],
                device_id=(x_pos, y_pos, 1 - c), device_id_type=MESH)

        @pl.when((k == 0) & (i == 0))
        def _():
            for a in range(2):
                for j in range(3):
                    chip_copy(a, j, me).start()
            if comm is not None:
                comm.start(cins, couts, csems)

        for j, (ox, oy) in enumerate(others):
            @pl.when((k == j + 1) & (i == 0))
            def _(j=j, chip=2 * ox + oy):
                for a in range(2):
                    chip_copy(a, j, chip).wait_recv()
                for a in range(2):
                    pass_copy(a, j, chip, mine).start()
                for a in range(2):
                    pass_copy(a, j, chip, theirs).wait_recv()
                loads = [pltpu.make_async_copy(stacks[a].at[chip], wbuf.at[a], load_sems.at[a]) for a in range(2)]
                for cp in loads:
                    cp.start()
                for cp in loads:
                    cp.wait()

        rows = pl.ds(pl.multiple_of(i * tm, tm), tm)

        @pl.when(k == 0)
        def _():
            xhat, _ = _rms(x_ref[...])
            n = (xhat * g_ref[...]).astype(BF16)
            n_ref[...] = n
            n_all[rows, :] = n
            hg_ref[...] = _dot_nt(n, wgo_ref[...]).astype(BF16)
            hu_ref[...] = _dot_nt(n, wuo_ref[...]).astype(BF16)

        @pl.when(k > 0)
        def _():
            n = n_all[rows, :]
            hg_ref[...] = _dot_nt(n, wbuf[0]).astype(BF16)
            hu_ref[...] = _dot_nt(n, wbuf[1]).astype(BF16)

        @pl.when((k == N_CHIPS - 1) & (i == nt - 1))
        def _():
            for a in range(2):
                for j, (ox, oy) in enumerate(others):
                    chip_copy(a, j, me).wait_send()
                    pass_copy(a, j, 2 * ox + oy, mine).wait_send()
            if comm is not None:
                comm.finish(cins, couts, csems)

    any_spec = pl.BlockSpec(memory_space=pl.ANY)
    first_pass = lambda k, i, order_ref: (jnp.where(k == 0, i, nt - 1), 0)
    whole = pl.BlockSpec((Fs, D), lambda k, i, order_ref: (0, 0))
    act_spec = pl.BlockSpec((None, tm, Fs), lambda k, i, order_ref: (order_ref[k], i, 0))
    stack = jax.ShapeDtypeStruct((N_CHIPS, Fs, D), BF16)
    outs = pl.pallas_call(
        body, name=name,
        grid_spec=pltpu.PrefetchScalarGridSpec(
            num_scalar_prefetch=1, grid=(N_CHIPS, nt),
            in_specs=[pl.BlockSpec((tm, D), first_pass), pl.BlockSpec((1, D), lambda k, i, order_ref: (0, 0)),
                      whole, whole] + [any_spec] * ci,
            out_specs=[act_spec, act_spec, pl.BlockSpec((tm, D), first_pass), any_spec, any_spec] + [any_spec] * co,
            scratch_shapes=[pltpu.VMEM((T, D), BF16), pltpu.VMEM((2, Fs, D), BF16)]
            + [pltpu.SemaphoreType.DMA((6,))] * 4 + [pltpu.SemaphoreType.DMA((2,))]
            + (comm.scratch if comm is not None else [])),
        out_shape=[jax.ShapeDtypeStruct((N_CHIPS, T, Fs), BF16), jax.ShapeDtypeStruct((N_CHIPS, T, Fs), BF16),
                   jax.ShapeDtypeStruct((T, D), BF16), stack, stack] + (comm.out_shape if comm is not None else []),
        compiler_params=_params(("arbitrary", "arbitrary")),
    )(order, x, g, wg_own, wu_own, *(comm.inputs if comm is not None else []))
    return list(outs[:5]), list(outs[5:])


def _ffn_down(name, x, hg, hu, wd, tm, comm=None):
    T, D = x.shape
    K, Fs, _ = wd.shape

    def body(x_ref, hg_ref, hu_ref, wd_ref, out_ref, acc_scr):
        k = pl.program_id(1)

        @pl.when(k == 0)
        def _():
            acc_scr[...] = jnp.zeros_like(acc_scr)

        hgv = hg_ref[...].astype(F32)
        act = (hgv * _sigmoid(hgv) * hu_ref[...].astype(F32)).astype(BF16)
        acc_scr[...] += _dot(act, wd_ref[...])

        @pl.when(k == K - 1)
        def _():
            out_ref[...] = x_ref[...] + FFN_RES * acc_scr[...]

    act_spec = pl.BlockSpec((None, tm, Fs), lambda i, k: (k, i, 0))
    row = pl.BlockSpec((tm, D), lambda i, k: (i, 0))
    return _pallas(
        body, name, (T // tm, K),
        [row, act_spec, act_spec, pl.BlockSpec((None, Fs, D), lambda i, k: (k, 0, 0))],
        [row], [jax.ShapeDtypeStruct((T, D), F32)], [pltpu.VMEM((tm, D), F32)],
        (x, hg, hu, wd), comm)


def _ffn_bwd_dx(name, dout, x, g, hg, hu, wgt, wut, wd, tm, comm=None):
    T, D = x.shape
    K, Fs, _ = wgt.shape

    def body(dout_ref, x_ref, g_ref, hg_ref, hu_ref, wg_ref, wu_ref, wd_ref,
             dx_ref, dhg_ref, dhu_ref, dg_ref, df_ref, dn_scr):
        i, k = pl.program_id(0), pl.program_id(1)

        @pl.when(k == 0)
        def _():
            df_ref[...] = (FFN_RES * dout_ref[...]).astype(BF16)
            dn_scr[...] = jnp.zeros_like(dn_scr)

        @pl.when((k == 0) & (i == 0))
        def _():
            dg_ref[...] = jnp.zeros_like(dg_ref)

        for r0 in range(0, tm, ROW_BLOCK):
            rows = slice(r0, r0 + ROW_BLOCK)
            dact = _dot_nt(df_ref[rows, :], wd_ref[...])
            hgv = hg_ref[rows, :].astype(F32)
            huv = hu_ref[rows, :].astype(F32)
            s = _sigmoid(hgv)
            dhu = (dact * (hgv * s)).astype(BF16)
            dhg = (dact * huv * (s * (1.0 + hgv * (1.0 - s)))).astype(BF16)
            dhg_ref[rows, :] = dhg
            dhu_ref[rows, :] = dhu
            dn_scr[rows, :] += _dot(dhg, wg_ref[...]) + _dot(dhu, wu_ref[...])

        @pl.when(k == K - 1)
        def _():
            xhat, inv = _rms(x_ref[...])
            dx, dg = _rms_bwd(dn_scr[...], xhat, inv, g_ref[...])
            dx_ref[...] = dout_ref[...] + dx
            dg_ref[...] += dg

    w_spec = pl.BlockSpec((None, Fs, D), lambda i, k: (k, 0, 0))
    act_spec = pl.BlockSpec((None, tm, Fs), lambda i, k: (k, i, 0))
    row = pl.BlockSpec((tm, D), lambda i, k: (i, 0))
    row_once = pl.BlockSpec((tm, D), lambda i, k: (i, 0), pipeline_mode=pl.Buffered(1))
    vec = pl.BlockSpec((1, D), lambda i, k: (0, 0))
    return _pallas(
        body, name, (T // tm, K),
        [row, row_once, vec, act_spec, act_spec, w_spec, w_spec, w_spec],
        [row_once, act_spec, act_spec, vec, row],
        [jax.ShapeDtypeStruct((T, D), F32), jax.ShapeDtypeStruct((K, T, Fs), BF16),
         jax.ShapeDtypeStruct((K, T, Fs), BF16), jax.ShapeDtypeStruct((1, D), F32),
         jax.ShapeDtypeStruct((T, D), BF16)],
        [pltpu.VMEM((tm, D), F32)],
        (dout, x, g, hg, hu, wgt, wut, wd), comm)


def _ffn_bwd_dw(name, n, df, hg, hu, dhg, dhu, tk, comm=None):
    T, D = n.shape
    K, _, Fs = hg.shape
    nt = T // tk

    def body(n_ref, df_ref, hg_ref, hu_ref, dhg_ref, dhu_ref, dwg_ref, dwu_ref, dwd_ref, accg, accu, accd):
        t = pl.program_id(1)

        @pl.when(t == 0)
        def _():
            accg[...] = jnp.zeros_like(accg)
            accu[...] = jnp.zeros_like(accu)
            accd[...] = jnp.zeros_like(accd)

        nv = n_ref[...]
        hgv = hg_ref[...].astype(F32)
        act = (hgv * _sigmoid(hgv) * hu_ref[...].astype(F32)).astype(BF16)
        accg[...] += _dot_tn(dhg_ref[...], nv)
        accu[...] += _dot_tn(dhu_ref[...], nv)
        accd[...] += _dot_tn(act, df_ref[...])

        @pl.when(t == nt - 1)
        def _():
            dwg_ref[...] = accg[...].astype(BF16)
            dwu_ref[...] = accu[...].astype(BF16)
            dwd_ref[...] = accd[...].astype(BF16)

    act_spec = pl.BlockSpec((None, tk, Fs), lambda k, t: (k, t, 0))
    w_spec = pl.BlockSpec((None, Fs, D), lambda k, t: (k, 0, 0))
    row = pl.BlockSpec((tk, D), lambda k, t: (t, 0))
    return _pallas(
        body, name, (K, nt),
        [row, row, act_spec, act_spec, act_spec, act_spec],
        [w_spec, w_spec, w_spec],
        [jax.ShapeDtypeStruct((K, Fs, D), BF16)] * 3,
        [pltpu.VMEM((Fs, D), F32)] * 3,
        (n, df, hg, hu, dhg, dhu), comm)


def _mix_proj_fwd(x, g, wproj_t, wf_t, tm, tn, comm=None):
    T, D = x.shape
    N = wproj_t.shape[0]

    def body(x_ref, g_ref, w_ref, wf_ref, h_ref, proj_ref, flog_ref, h_scr):
        @pl.when(pl.program_id(1) == 0)
        def _():
            xhat, _ = _rms(x_ref[...])
            h = (xhat * g_ref[...]).astype(BF16)
            h_scr[...] = h
            h_ref[...] = h
            flog_ref[...] = _dot_nt(h, wf_ref[...])

        proj_ref[...] = _dot_nt(h_scr[...], w_ref[...]).astype(BF16)

    return _pallas(
        body, "mix_proj_fwd", (T // tm, N // tn),
        [pl.BlockSpec((tm, D), lambda i, n: (i, 0)), pl.BlockSpec((1, D), lambda i, n: (0, 0)),
         pl.BlockSpec((tn, D), lambda i, n: (n, 0)), pl.BlockSpec((LANES, D), lambda i, n: (0, 0))],
        [pl.BlockSpec((tm, D), lambda i, n: (i, 0)), pl.BlockSpec((tm, tn), lambda i, n: (i, n)),
         pl.BlockSpec((tm, LANES), lambda i, n: (i, 0))],
        [jax.ShapeDtypeStruct((T, D), BF16), jax.ShapeDtypeStruct((T, N), BF16),
         jax.ShapeDtypeStruct((T, LANES), F32)],
        [pltpu.VMEM((tm, D), BF16)],
        (x, g, wproj_t, wf_t), comm)


def _log_sigmoid(z):
    return -(jnp.maximum(-z, 0.0) + jnp.log(1.0 + jnp.exp(-jnp.abs(z))))


def _tri(n, lower):
    r = lax.broadcasted_iota(jnp.int32, (n, n), 0)
    c = lax.broadcasted_iota(jnp.int32, (n, n), 1)
    return jnp.where((r >= c) if lower else (r <= c), 1.0, 0.0).astype(F32)


def _dot_f32(a, b):
    return lax.dot_general(a, b, (((1,), (0,)), ((), ())), preferred_element_type=F32,
                           precision=lax.Precision.HIGHEST)


def _fgate_fwd(flog, bias, B, S, ch):
    def body(flog_ref, b_ref, cum_ref):
        tri = _tri(ch, True)
        carry = jnp.zeros((1, LANES), F32)
        for c0 in range(0, S, ch):
            lf = _log_sigmoid(flog_ref[c0:c0 + ch, :] + b_ref[...])
            cs = _dot_f32(tri, lf) + carry
            cum_ref[c0:c0 + ch, :] = cs
            carry = cs[ch - 1:ch, :]

    return pl.pallas_call(
        body, name="fgate_fwd", grid=(B,),
        in_specs=[pl.BlockSpec((S, LANES), lambda b: (b, 0)),
                  pl.BlockSpec((1, LANES), lambda b: (0, 0))],
        out_specs=pl.BlockSpec((S, LANES), lambda b: (b, 0)),
        out_shape=jax.ShapeDtypeStruct((B * S, LANES), F32),
        compiler_params=_params(("arbitrary",)),
    )(flog, bias)


def _fgate_bwd(dcum, flog, bias, B, S, ch):
    def body(dcum_ref, flog_ref, b_ref, dflog_ref, db_ref):
        @pl.when(pl.program_id(0) == 0)
        def _():
            db_ref[...] = jnp.zeros_like(db_ref)

        tri = _tri(ch, False)
        carry = jnp.zeros((1, LANES), F32)
        db = jnp.zeros((1, LANES), F32)
        for c0 in range(S - ch, -1, -ch):
            dlf = _dot_f32(tri, dcum_ref[c0:c0 + ch, :]) + carry
            carry = dlf[0:1, :]
            z = flog_ref[c0:c0 + ch, :] + b_ref[...]
            dz = dlf * _sigmoid(-z)
            dflog_ref[c0:c0 + ch, :] = dz
            db = db + jnp.sum(dz, axis=0, keepdims=True)
        db_ref[...] += db

    return pl.pallas_call(
        body, name="fgate_bwd", grid=(B,),
        in_specs=[pl.BlockSpec((S, LANES), lambda b: (b, 0)),
                  pl.BlockSpec((S, LANES), lambda b: (b, 0)),
                  pl.BlockSpec((1, LANES), lambda b: (0, 0))],
        out_specs=[pl.BlockSpec((S, LANES), lambda b: (b, 0)),
                   pl.BlockSpec((1, LANES), lambda b: (0, 0))],
        out_shape=[jax.ShapeDtypeStruct((B * S, LANES), F32),
                   jax.ShapeDtypeStruct((1, LANES), F32)],
        compiler_params=_params(("arbitrary",)),
    )(dcum, flog, bias)


def _pick_lane(tile, h):
    lane = lax.broadcasted_iota(jnp.int32, tile.shape, 1)
    return jnp.sum(jnp.where(lane == h, tile, 0.0), axis=1, keepdims=True)


def _put_lane(col, h, width=LANES):
    lane = lax.broadcasted_iota(jnp.int32, (col.shape[0], width), 1)
    return jnp.where(lane == h, col, 0.0)


def _pick_row(tile, h):
    row = lax.broadcasted_iota(jnp.int32, tile.shape, 0)
    return jnp.sum(jnp.where(row == h, tile, 0.0), axis=0, keepdims=True)


def _put_row(vec, h):
    row = lax.broadcasted_iota(jnp.int32, (8, vec.shape[1]), 0)
    return jnp.where(row == h, vec, 0.0)


def _causal(tq):
    r = lax.broadcasted_iota(jnp.int32, (tq, tq), 0)
    c = lax.broadcasted_iota(jnp.int32, (tq, tq), 1)
    return r >= c


def _head_halves(t):
    lo = lax.broadcasted_iota(jnp.int32, t.shape, 1) < HEAD_DIM
    zero = jnp.zeros_like(t)
    return jnp.where(lo, t, zero), jnp.where(lo, zero, t)


NEG = -1e30
ATTN_SCALE = 1.0 / math.sqrt(HEAD_DIM)


def _scaled(q):
    return (q.astype(F32) * ATTN_SCALE).astype(q.dtype)


def _attn_fwd(proj, cum, cum_t, B, S, tq, comm=None):
    nq = S // tq

    def body(q_ref, k_ref, v_ref, cum_ref, cumt_ref, o_ref, lse_ref):
        qi, hp = pl.program_id(1), pl.program_id(2)
        qm = _head_halves(_scaled(q_ref[...]))
        cumv = cum_ref[...]
        cq = [_pick_lane(cumv, 2 * hp + e) for e in range(2)]

        def tile(j, carry, masked):
            off = pl.multiple_of(j * tq, tq)
            kj = k_ref[pl.ds(off, tq), :]
            vj = v_ref[pl.ds(off, tq), :]
            ct = cumt_ref[j]
            new = []
            for e in range(2):
                m, l, acc = carry[e]
                s = _dot_nt(qm[e], kj) - _pick_row(ct, 2 * hp + e)
                if masked:
                    s = jnp.where(_causal(tq), s, NEG)
                m_new = jnp.maximum(m, jnp.max(s, axis=1, keepdims=True))
                p = jnp.exp(s - m_new)
                alpha = jnp.exp(m - m_new)
                l = alpha * l + jnp.sum(p, axis=1, keepdims=True)
                acc = alpha * acc + _dot(p.astype(BF16), vj)
                new.append((m_new, l, acc))
            return tuple(new)

        one = (jnp.full((tq, 1), NEG, F32), jnp.zeros((tq, 1), F32), jnp.zeros((tq, LANES), F32))
        carry = lax.fori_loop(0, qi, lambda j, c: tile(j, c, False), (one, one))
        (ma, la, acca), (mb, lb, accb) = tile(qi, carry, True)
        lo = lax.broadcasted_iota(jnp.int32, (tq, LANES), 1) < HEAD_DIM
        o_ref[...] = jnp.where(lo, acca / la, accb / lb).astype(BF16)

        @pl.when(hp == 0)
        def _():
            lse_ref[...] = jnp.zeros_like(lse_ref)

        lse_ref[...] += (_put_lane(ma + jnp.log(la) + cq[0], 2 * hp) + _put_lane(mb + jnp.log(lb) + cq[1], 2 * hp + 1))

    kv = lambda first: pl.BlockSpec((S, LANES), lambda b, i, hp: (b, first + hp))
    return _pallas(
        body, "attn_fwd", (B, nq, HEAD_PAIRS),
        [pl.BlockSpec((tq, LANES), lambda b, i, hp: (b * nq + i, hp)),
         kv(ATTN_W // LANES), kv(2 * ATTN_W // LANES),
         pl.BlockSpec((tq, LANES), lambda b, i, hp: (b * nq + i, 0)),
         pl.BlockSpec((None, nq, 8, tq), lambda b, i, hp: (b, 0, 0, 0))],
        [pl.BlockSpec((tq, LANES), lambda b, i, hp: (b * nq + i, hp)),
         pl.BlockSpec((tq, LANES), lambda b, i, hp: (b * nq + i, 0))],
        [jax.ShapeDtypeStruct((B * S, ATTN_W), BF16), jax.ShapeDtypeStruct((B * S, LANES), F32)],
        [], (proj, proj, proj, cum, cum_t), comm)


def _attn_bwd(proj, o, do, lse, cum, cum_t, B, S, tq, comm=None):
    nq = S // tq

    def body(q_ref, k_ref, v_ref, o_ref, do_ref, lse_ref, cum_ref, cumt_ref,
             dq_ref, dk_ref, dv_ref, dcq_ref, dck_ref, dq_scr):
        hp, kj = pl.program_id(1), pl.program_id(2)

        @pl.when(kj == 0)
        def _():
            dq_scr[...] = jnp.zeros_like(dq_scr)

        @pl.when((kj == 0) & (hp == 0))
        def _():
            dcq_ref[...] = jnp.zeros_like(dcq_ref)
            dck_ref[...] = jnp.zeros_like(dck_ref)

        kv = k_ref[...]
        vv = v_ref[...]
        km = _head_halves(kv)
        ct = cumt_ref[...]
        ck = [_pick_row(ct, 2 * hp + e) for e in range(2)]

        def tile(i, carry, masked):
            dk, dv, dcol = carry
            off = pl.multiple_of(i * tq, tq)
            qi = q_ref[pl.ds(off, tq), :]
            ov = o_ref[pl.ds(off, tq), :].astype(F32)
            qm = _head_halves(_scaled(qi))
            dom = _head_halves(do_ref[pl.ds(off, tq), :])
            cumv = cum_ref[pl.ds(off, tq), :]
            lsev = lse_ref[pl.ds(off, tq), :]
            dcq = jnp.zeros((tq, LANES), F32)
            dq = jnp.zeros((tq, LANES), F32)
            dcol_new = []
            for e in range(2):
                delta = jnp.sum(dom[e].astype(F32) * ov, axis=1, keepdims=True)
                row_term = _pick_lane(cumv, 2 * hp + e) - _pick_lane(lsev, 2 * hp + e)
                p = jnp.exp(_dot_nt(qm[e], kv) + row_term - ck[e])
                if masked:
                    p = jnp.where(_causal(tq), p, 0.0)
                dv = dv + _dot_tn(dom[e], p.astype(BF16))
                ds = p * (_dot_nt(dom[e], vv) - delta)
                dcol_new.append(dcol[e] + jnp.sum(ds, axis=0, keepdims=True))
                dcq = dcq + _put_lane(jnp.sum(ds, axis=1, keepdims=True), 2 * hp + e)
                dsb = ds.astype(BF16)
                dk = dk + _dot_tn(qm[e], dsb)
                dq = dq + _dot(dsb, km[e]) * ATTN_SCALE
            dq_scr[pl.ds(off, tq), :] += dq
            dcq_ref[pl.ds(off, tq), :] += dcq
            return dk, dv, tuple(dcol_new)

        zero_row = jnp.zeros((1, tq), F32)
        init = (jnp.zeros((LANES, tq), F32), jnp.zeros((LANES, tq), F32), (zero_row, zero_row))
        carry = tile(kj, init, True)
        dk, dv, dcol = lax.fori_loop(kj + 1, nq, lambda i, c: tile(i, c, False), carry)
        dk_ref[...] = dk.T.astype(BF16)
        dv_ref[...] = dv.T.astype(BF16)
        dck_ref[kj] += -(_put_row(dcol[0], 2 * hp) + _put_row(dcol[1], 2 * hp + 1))

        @pl.when(kj == nq - 1)
        def _():
            dq_ref[...] = dq_scr[...].astype(BF16)

    seq = lambda first: pl.BlockSpec((S, LANES), lambda b, hp, j: (b, first + hp))
    tile_in = lambda first: pl.BlockSpec((tq, LANES), lambda b, hp, j: (b * nq + j, first + hp))
    lanes0 = pl.BlockSpec((S, LANES), lambda b, hp, j: (b, 0))
    out = jax.ShapeDtypeStruct((B * S, ATTN_W), BF16)
    return _pallas(
        body, "attn_bwd", (B, HEAD_PAIRS, nq),
        [seq(0), tile_in(ATTN_W // LANES), tile_in(2 * ATTN_W // LANES), seq(0), seq(0), lanes0, lanes0,
         pl.BlockSpec((None, None, 8, tq), lambda b, hp, j: (b, j, 0, 0))],
        [seq(0), tile_in(0), tile_in(0), lanes0,
         pl.BlockSpec((None, nq, 8, tq), lambda b, hp, j: (b, 0, 0, 0))],
        [out, out, out, jax.ShapeDtypeStruct((B * S, LANES), F32), jax.ShapeDtypeStruct((B, nq, 8, tq), F32)],
        [pltpu.VMEM((S, LANES), F32)],
        (proj, proj, proj, o, do, lse, cum, cum_t), comm)


def _shift_down(u, n):
    row = lax.broadcasted_iota(jnp.int32, u.shape, 0)
    return jnp.where(row >= n, pltpu.roll(u, n, 0), 0.0)


def _shift_up(u, n):
    rows = u.shape[0]
    row = lax.broadcasted_iota(jnp.int32, u.shape, 0)
    return jnp.where(row < rows - n, pltpu.roll(u, rows - n, 0), 0.0)


def _conv_specs(S):
    cb = pl.BlockSpec((S, LANES), lambda g, b: (b, COL_CB // LANES + g))
    cc = pl.BlockSpec((S, LANES), lambda g, b: (b, COL_CC // LANES + g))
    cx = pl.BlockSpec((S, LANES), lambda g, b: (b, COL_CX // LANES + g))
    w = pl.BlockSpec((8, LANES), lambda g, b: (0, g))
    return cb, cc, cx, w


def _conv_fwd(proj, conv_w, B, S):
    def body(cb_ref, cc_ref, cx_ref, w_ref, y_ref):
        u = cc_ref[...].astype(F32) * cx_ref[...].astype(F32)
        w = w_ref[...]
        conv = w[0:1, :] * _shift_down(u, 2) + w[1:2, :] * _shift_down(u, 1) + w[2:3, :] * u
        y_ref[...] = (cb_ref[...].astype(F32) * conv).astype(BF16)

    cb, cc, cx, w = _conv_specs(S)
    return pl.pallas_call(
        body, name="conv_fwd", grid=(CONV_W // LANES, B),
        in_specs=[cb, cc, cx, w],
        out_specs=pl.BlockSpec((S, LANES), lambda g, b: (b, g)),
        out_shape=jax.ShapeDtypeStruct((B * S, CONV_W), BF16),
        compiler_params=_params(("arbitrary", "arbitrary")),
    )(proj, proj, proj, conv_w)


def _conv_bwd(dy, proj, conv_w, B, S):
    def body(dy_ref, cb_ref, cc_ref, cx_ref, w_ref, dcb_ref, dcc_ref, dcx_ref, dw_ref):
        @pl.when(pl.program_id(1) == 0)
        def _():
            dw_ref[...] = jnp.zeros_like(dw_ref)

        ccv = cc_ref[...].astype(F32)
        cxv = cx_ref[...].astype(F32)
        u = ccv * cxv
        u1 = _shift_down(u, 1)
        u2 = _shift_down(u, 2)
        w = w_ref[...]
        conv = w[0:1, :] * u2 + w[1:2, :] * u1 + w[2:3, :] * u
        dyv = dy_ref[...].astype(F32)
        dcb_ref[...] = (dyv * conv).astype(BF16)
        dconv = dyv * cb_ref[...].astype(F32)
        du = w[2:3, :] * dconv + w[1:2, :] * _shift_up(dconv, 1) + w[0:1, :] * _shift_up(dconv, 2)
        dcc_ref[...] = (du * cxv).astype(BF16)
        dcx_ref[...] = (du * ccv).astype(BF16)
        row = lax.broadcasted_iota(jnp.int32, (8, LANES), 0)
        dw = jnp.where(row == 0, jnp.sum(dconv * u2, axis=0, keepdims=True),
                       jnp.where(row == 1, jnp.sum(dconv * u1, axis=0, keepdims=True),
                                 jnp.where(row == 2, jnp.sum(dconv * u, axis=0, keepdims=True), 0.0)))
        dw_ref[...] += dw

    cb, cc, cx, w = _conv_specs(S)
    out = pl.BlockSpec((S, LANES), lambda g, b: (b, g))
    return pl.pallas_call(
        body, name="conv_bwd", grid=(CONV_W // LANES, B),
        in_specs=[out, cb, cc, cx, w],
        out_specs=[out, out, out, w],
        out_shape=[jax.ShapeDtypeStruct((B * S, CONV_W), BF16)] * 3 + [jax.ShapeDtypeStruct((8, CONV_W), F32)],
        compiler_params=_params(("arbitrary", "arbitrary")),
    )(dy, proj, proj, proj, conv_w)


def _gate_specs(tm, D):
    ga = pl.BlockSpec((tm, D), lambda i: (i, COL_GATES // D))
    gc = pl.BlockSpec((tm, D), lambda i: (i, COL_GATES // D + 1))
    return ga, gc


def _mix_out_fwd(x, o, yc, proj, woa, woc, wout, tm):
    T, D = x.shape

    def body(x_ref, o_ref, yc_ref, ga_ref, gc_ref, woa_ref, woc_ref, wout_ref, out_ref):
        ya = _dot(o_ref[...], woa_ref[...])
        yp = _dot(yc_ref[...], woc_ref[...])
        merged = _sigmoid(ga_ref[...].astype(F32)) * ya + _sigmoid(gc_ref[...].astype(F32)) * yp
        out_ref[...] = x_ref[...] + _dot(merged.astype(BF16), wout_ref[...])

    ga, gc = _gate_specs(tm, D)
    row = lambda w: pl.BlockSpec((tm, w), lambda i: (i, 0))
    whole = lambda a: pl.BlockSpec(a.shape, lambda i: (0, 0))
    return pl.pallas_call(
        body, name="mix_out_fwd", grid=(T // tm,),
        in_specs=[row(D), row(ATTN_W), row(CONV_W), ga, gc, whole(woa), whole(woc), whole(wout)],
        out_specs=row(D),
        out_shape=jax.ShapeDtypeStruct((T, D), F32),
        compiler_params=_params(("arbitrary",)),
    )(x, o, yc, proj, proj, woa, woc, wout)


def _mix_out_bwd(dx, o, yc, proj, woa, woc, wout, tm, comm=None):
    T, D = dx.shape
    nt = T // tm

    def body(dx_ref, o_ref, yc_ref, ga_ref, gc_ref, woa_ref, woc_ref, wout_ref,
             do_ref, dyc_ref, dg_ref, dwoa_ref, dwoc_ref, dwout_ref, acca, accc, acco):
        t = pl.program_id(0)

        @pl.when(t == 0)
        def _():
            acca[...] = jnp.zeros_like(acca)
            accc[...] = jnp.zeros_like(accc)
            acco[...] = jnp.zeros_like(acco)

        dxb = dx_ref[...].astype(BF16)
        ov, ycv = o_ref[...], yc_ref[...]
        ya = _dot(ov, woa_ref[...])
        yp = _dot(ycv, woc_ref[...])
        sa = _sigmoid(ga_ref[...].astype(F32))
        sc = _sigmoid(gc_ref[...].astype(F32))
        merged = (sa * ya + sc * yp).astype(BF16)
        dm = _dot_nt(dxb, wout_ref[...])
        dya = (dm * sa).astype(BF16)
        dyp = (dm * sc).astype(BF16)
        dg_ref[:, :D] = (dm * ya * sa * (1.0 - sa)).astype(BF16)
        dg_ref[:, D:] = (dm * yp * sc * (1.0 - sc)).astype(BF16)
        do_ref[...] = _dot_nt(dya, woa_ref[...]).astype(BF16)
        dyc_ref[...] = _dot_nt(dyp, woc_ref[...]).astype(BF16)
        acca[...] += _dot_tn(ov, dya)
        accc[...] += _dot_tn(ycv, dyp)
        acco[...] += _dot_tn(merged, dxb)

        @pl.when(t == nt - 1)
        def _():
            dwoa_ref[...] = acca[...].astype(BF16)
            dwoc_ref[...] = accc[...].astype(BF16)
            dwout_ref[...] = acco[...].astype(BF16)

    ga, gc = _gate_specs(tm, D)
    row = lambda w: pl.BlockSpec((tm, w), lambda i: (i, 0))
    whole = lambda a: pl.BlockSpec(a.shape, lambda i: (0, 0))
    return _pallas(
        body, "mix_out_bwd", (nt,),
        [row(D), row(ATTN_W), row(CONV_W), ga, gc, whole(woa), whole(woc), whole(wout)],
        [row(ATTN_W), row(CONV_W), row(2 * D), whole(woa), whole(woc), whole(wout)],
        [jax.ShapeDtypeStruct((T, ATTN_W), BF16), jax.ShapeDtypeStruct((T, CONV_W), BF16),
         jax.ShapeDtypeStruct((T, 2 * D), BF16),
         jax.ShapeDtypeStruct(woa.shape, BF16), jax.ShapeDtypeStruct(woc.shape, BF16),
         jax.ShapeDtypeStruct(wout.shape, BF16)],
        [pltpu.VMEM(woa.shape, F32), pltpu.VMEM(woc.shape, F32), pltpu.VMEM(wout.shape, F32)],
        (dx, o, yc, proj, proj, woa, woc, wout), comm)


def _proj_pieces(dq, dk, dv, dcb, dcc, dcx, dgates, dflog):
    D = dgates.shape[1] // 2
    return [(dq, ATTN_W, 0), (dk, ATTN_W, 0), (dv, ATTN_W, 0), (dcb, CONV_W, 0), (dcc, CONV_W, 0), (dcx, CONV_W, 0),
            (dgates, D, 0), (dgates, D, 1), (dflog, LANES, 0)]


def _mix_proj_bwd_dx(dres, x, g, pieces, wproj_t, wf_t, tm, comm=None):
    T, D = x.shape
    n = len(pieces)
    w_blocks = [(ATTN_W, 0), (ATTN_W, 1), (ATTN_W, 2), (CONV_W, 3), (CONV_W, 4), (CONV_W, 5),
                (D, COL_GATES // D), (D, COL_GATES // D + 1)]

    def body(*refs):
        dres_ref, x_ref, g_ref = refs[:3]
        p_refs, w_refs = refs[3:3 + n], refs[3 + n:3 + 2 * n]
        dx_ref, dg_ref = refs[3 + 2 * n:]

        @pl.when(pl.program_id(0) == 0)
        def _():
            dg_ref[...] = jnp.zeros_like(dg_ref)

        dh = _dot(p_refs[0][...].astype(BF16), w_refs[0][...])
        for p_ref, w_ref in zip(p_refs[1:], w_refs[1:]):
            dh = dh + _dot(p_ref[...].astype(BF16), w_ref[...])
        xhat, inv = _rms(x_ref[...])
        dx, dg = _rms_bwd(dh, xhat, inv, g_ref[...])
        dx_ref[...] = dres_ref[...] + dx
        dg_ref[...] += dg

    row = pl.BlockSpec((tm, D), lambda i: (i, 0))
    vec = pl.BlockSpec((1, D), lambda i: (0, 0))
    p_specs = [pl.BlockSpec((tm, w), lambda i, cb=cb: (i, cb)) for _, w, cb in pieces]
    w_specs = [pl.BlockSpec((r, D), lambda i, rb=rb: (rb, 0)) for r, rb in w_blocks]
    w_specs.append(pl.BlockSpec((LANES, D), lambda i: (0, 0)))
    return _pallas(
        body, "mix_proj_bwd_dx", (T // tm,),
        [row, row, vec] + p_specs + w_specs, [row, vec],
        [jax.ShapeDtypeStruct((T, D), F32), jax.ShapeDtypeStruct((1, D), F32)], [],
        (dres, x, g, *[p for p, _, _ in pieces], *([wproj_t] * len(w_blocks)), wf_t), comm)


def _matmuls_tn(name, pieces, b, tk):
    T, N = b.shape
    nt = T // tk
    n = len(pieces)

    def body(*refs):
        a_refs, b_ref, out_refs, accs = refs[:n], refs[n], refs[n + 1:2 * n + 1], refs[2 * n + 1:]
        t = pl.program_id(0)

        @pl.when(t == 0)
        def _():
            for acc in accs:
                acc[...] = jnp.zeros_like(acc)

        bv = b_ref[...]
        for a_ref, acc in zip(a_refs, accs):
            acc[...] += _dot_tn(a_ref[...].astype(BF16), bv)

        @pl.when(t == nt - 1)
        def _():
            for out_ref, acc in zip(out_refs, accs):
                out_ref[...] = acc[...].astype(BF16)

    return pl.pallas_call(
        body, name=name, grid=(nt,),
        in_specs=[pl.BlockSpec((tk, w), lambda t, cb=cb: (t, cb)) for _, w, cb in pieces]
        + [pl.BlockSpec((tk, N), lambda t: (t, 0))],
        out_specs=[pl.BlockSpec((w, N), lambda t: (0, 0)) for _, w, _ in pieces],
        out_shape=[jax.ShapeDtypeStruct((w, N), BF16) for _, w, _ in pieces],
        scratch_shapes=[pltpu.VMEM((w, N), F32) for _, w, _ in pieces],
        compiler_params=_params(("arbitrary",)),
    )(*[a for a, _, _ in pieces], b)


def _final_loss(x, target, g, tm):
    T, D = x.shape

    def body(x_ref, t_ref, g_ref, dx_ref, loss_ref, dg_ref):
        @pl.when(pl.program_id(0) == 0)
        def _():
            loss_ref[...] = jnp.zeros_like(loss_ref)
            dg_ref[...] = jnp.zeros_like(dg_ref)

        xhat, inv = _rms(x_ref[...])
        err = xhat * g_ref[...] - t_ref[...]
        loss_ref[...] += 0.5 * jnp.sum(jnp.sum(err * err, axis=1, keepdims=True), axis=0, keepdims=True) / D
        dx, dg = _rms_bwd(err * (1.0 / D), xhat, inv, g_ref[...])
        dx_ref[...] = dx
        dg_ref[...] += dg

    row = pl.BlockSpec((tm, D), lambda i: (i, 0))
    return pl.pallas_call(
        body, name="final_loss", grid=(T // tm,),
        in_specs=[row, row, pl.BlockSpec((1, D), lambda i: (0, 0))],
        out_specs=[row, pl.BlockSpec((1, LANES), lambda i: (0, 0)), pl.BlockSpec((1, D), lambda i: (0, 0))],
        out_shape=[jax.ShapeDtypeStruct((T, D), F32), jax.ShapeDtypeStruct((1, LANES), F32),
                   jax.ShapeDtypeStruct((1, D), F32)],
        compiler_params=_params(("arbitrary",)),
    )(x, target, g)


class _LocalPlan:
    def __init__(self, stacks, small):
        self.stacks, self.small, self.grads = stacks, small, {}

    def weights(self, group):
        return _LAYOUTS[group](self.stacks, self.small)

    def rider(self, kernel_name):
        return None

    def arrived(self, kernel_name, results):
        pass

    def reduce(self, group, grads):
        self.grads.update(grads)

    def reduce_small(self, small_grads, loss):
        pass

    def ffn1_up(self, x, tm):
        w = self.weights("ffn1_in")
        return _ffn_up("ffn1_up", x, w["ffn1_norm"], w["ffn1_gate"], w["ffn1_up"], tm)[0]


def _local_step(x, target, plan, B, S):
    T, D = x.shape
    tm = min(512, T)
    tm_fwd = min(1024, T)
    tq = min(512, S)
    nq = S // tq
    ch = min(256, S)

    def riding(kernel_name, build):
        results, brought = build(plan.rider(kernel_name))
        plan.arrived(kernel_name, brought)
        return results

    hg1, hu1, n1 = plan.ffn1_up(x, tm_fwd)
    w1 = plan.weights("ffn1")
    x1, = riding("ffn1_down", lambda comm: _ffn_down("ffn1_down", x, hg1, hu1, w1["ffn1_down"], tm_fwd, comm))
    wm = plan.weights("mix_in")
    h, proj, flog = riding("mix_proj_fwd", lambda comm: _mix_proj_fwd(
        x1, wm["mix_norm"], wm["w_proj"], wm["w_f"], tm_fwd, PROJ_W // 4, comm))
    wm.update(plan.weights("mix_out"))
    cum = _fgate_fwd(flog, wm["b_forget"], B, S, ch)
    cum_t = jnp.transpose(cum[:, :N_HEADS].reshape(B, nq, tq, N_HEADS), (0, 1, 3, 2))
    o, lse = riding("attn_fwd", lambda comm: _attn_fwd(proj, cum, cum_t, B, S, tq, comm))
    yc = _conv_fwd(proj, wm["conv_w"], B, S)
    x2 = _mix_out_fwd(x1, o, yc, proj, wm["w_o_attn"], wm["w_o_conv"], wm["w_out"], tm)
    w2 = plan.weights("ffn2")
    x3, hg2, hu2, n2 = _ffn_fwd("ffn2_fwd", x2, w2["ffn2_norm"], w2["ffn2_gate"], w2["ffn2_up"], w2["ffn2_down"], tm_fwd)[0]
    dx3, loss, d_final_norm = _final_loss(x3, target, w2["final_norm"], tm)

    g = {"final_norm": d_final_norm}
    dx2, dhg2, dhu2, g["ffn2_norm"], df2 = _ffn_bwd_dx("ffn2_bwd_dx", dx3, x2, w2["ffn2_norm"], hg2, hu2,
                                                  w2["ffn2_gate"], w2["ffn2_up"], w2["ffn2_down"], tm_fwd)[0]
    plan.reduce("ffn2", dict(zip(("ffn2_gate", "ffn2_up", "ffn2_down"),
                                 _ffn_bwd_dw("ffn2_bwd_dw", n2, df2, hg2, hu2, dhg2, dhu2, tm)[0])))
    do, dyc, dgates, dwoa, dwoc, dwout = riding("mix_out_bwd", lambda comm: _mix_out_bwd(
        dx2, o, yc, proj, wm["w_o_attn"], wm["w_o_conv"], wm["w_out"], tm, comm))
    plan.reduce("out", dict(w_o_attn=_shard_cols(dwoa), w_o_conv=_shard_cols(dwoc), w_out=dwout.reshape(N_CHIPS, -1, D)))
    dq, dk, dv, dcq, dck = riding("attn_bwd", lambda comm: _attn_bwd(proj, o, do, lse, cum, cum_t, B, S, tq, comm))
    dcum = dcq + jnp.pad(jnp.transpose(dck, (0, 1, 3, 2)).reshape(T, N_HEADS), ((0, 0), (0, LANES - N_HEADS)))
    dflog, g["b_forget"] = _fgate_bwd(dcum, flog, wm["b_forget"], B, S, ch)
    dcb, dcc, dcx, g["conv_w"] = _conv_bwd(dyc, proj, wm["conv_w"], B, S)
    pieces = _proj_pieces(dq, dk, dv, dcb, dcc, dcx, dgates, dflog)
    dwq, dwk, dwv, dwcb, dwcc, dwcx = _matmuls_tn("mix_dw_a", pieces[:6], h, tm)
    dwga, dwgc, dwf = _matmuls_tn("mix_dw_b", pieces[6:], h, tm)
    dwin_t = jnp.concatenate([dwq, dwk, dwv, dwf[:N_HEADS], dwcb, dwcc, dwcx, dwga, dwgc], axis=0)
    plan.reduce("w_in", {"w_in": dwin_t.reshape(N_CHIPS, -1, D)})
    dx1, g["mix_norm"] = riding("mix_proj_bwd_dx", lambda comm: _mix_proj_bwd_dx(
        dx2, x1, wm["mix_norm"], pieces, wm["w_proj"], wm["w_f"], min(256, T), comm))
    grad_x, dhg1, dhu1, g["ffn1_norm"], df1 = _ffn_bwd_dx(
        "ffn1_bwd_dx", dx1, x, w1["ffn1_norm"], hg1, hu1, w1["ffn1_gate"], w1["ffn1_up"], w1["ffn1_down"], tm_fwd)[0]
    plan.reduce_small(g, loss)
    plan.reduce("ffn1", dict(zip(("ffn1_gate", "ffn1_up", "ffn1_down"), riding("ffn1_bwd_dw", lambda comm: _ffn_bwd_dw(
        "ffn1_bwd_dw", n1, df1, hg1, hu1, dhg1, dhu1, tm, comm)))))
    return loss, grad_x, g


TRANSPOSED = ("ffn1_gate", "ffn1_up", "ffn2_gate", "ffn2_up", "w_in")
NORMS = ("ffn1_norm", "mix_norm", "ffn2_norm", "final_norm")


def _unshard_cols(a):
    return jnp.transpose(a, (1, 0, 2)).reshape(a.shape[1], N_CHIPS * a.shape[2])


def _shard_cols(a):
    return jnp.transpose(a.reshape(a.shape[0], N_CHIPS, a.shape[1] // N_CHIPS), (1, 0, 2))


def _layout_ffn(which):
    def layout(st, small):
        w = {n: st[n] for n in (which + "_gate", which + "_up", which + "_down")}
        w[which + "_norm"] = small[which + "_norm"].reshape(1, -1)
        if which == "ffn2":
            w["final_norm"] = small["final_norm"].reshape(1, -1)
        return w
    return layout


def _layout_mix_in(st, small):
    win_t = st["w_in"].reshape(-1, st["w_in"].shape[2])
    return {
        "w_proj": jnp.concatenate([win_t[:N_FORGET_COL], win_t[N_FORGET_COL + N_HEADS:]], axis=0),
        "w_f": jnp.pad(win_t[N_FORGET_COL:N_FORGET_COL + N_HEADS], ((0, LANES - N_HEADS), (0, 0))),
        "conv_w": _unshard_cols(st["conv_w"]),
        "mix_norm": small["mix_norm"].reshape(1, -1),
        "b_forget": jnp.pad(small["b_forget"].reshape(1, -1), ((0, 0), (0, LANES - N_HEADS))),
    }


def _layout_mix_out(st, small):
    return {"w_o_attn": _unshard_cols(st["w_o_attn"]), "w_o_conv": _unshard_cols(st["w_o_conv"]),
            "w_out": st["w_out"].reshape(-1, st["w_out"].shape[2])}


def _layout_ffn1_in(st, small):
    return {"ffn1_gate": st["ffn1_gate"], "ffn1_up": st["ffn1_up"], "ffn1_norm": small["ffn1_norm"].reshape(1, -1)}


_LAYOUTS = {"ffn1_in": _layout_ffn1_in, "ffn1": _layout_ffn("ffn1"), "mix_in": _layout_mix_in, "mix_out": _layout_mix_out,
            "ffn2": _layout_ffn("ffn2")}


ANY = pl.BlockSpec(memory_space=pl.ANY)
BIG = ("ffn1_gate", "ffn1_up", "ffn1_down", "w_in", "w_o_attn", "w_o_conv", "w_out",
       "ffn2_gate", "ffn2_up", "ffn2_down")


def _place():
    x, y, c = lax.axis_index("x"), lax.axis_index("y"), lax.axis_index("c")
    others = [(1 - x, y), (x, 1 - y), (1 - x, 1 - y)]
    return x, y, c, others


def _col_halves(cols, c):
    hc = cols // 2
    return pl.ds(pl.multiple_of(c * hc, LANES), hc), pl.ds(pl.multiple_of((1 - c) * hc, LANES), hc)


def _gather_comm(shards, conv_shard=None):
    n = len(shards)
    inputs = list(shards) + ([] if conv_shard is None else [conv_shard])

    def copies(ins, outs, sems):
        send_sems, recv_sems, pass_send, pass_recv = sems[:4]
        x, y, c, others = _place()

        def chip_copy(a, j, chip):
            mine, _ = _col_halves(ins[a].shape[1], c)
            return pltpu.make_async_remote_copy(
                src_ref=ins[a].at[:, mine], dst_ref=outs[a].at[chip, :, mine],
                send_sem=send_sems.at[3 * a + j], recv_sem=recv_sems.at[3 * a + j],
                device_id=(*others[j], c), device_id_type=MESH)

        def pass_copy(a, j, chip, half):
            return pltpu.make_async_remote_copy(
                src_ref=outs[a].at[chip, :, half], dst_ref=outs[a].at[chip, :, half],
                send_sem=pass_send.at[3 * a + j], recv_sem=pass_recv.at[3 * a + j],
                device_id=(x, y, 1 - c), device_id_type=MESH)

        def conv_copy(j, chip):
            return pltpu.make_async_remote_copy(
                src_ref=ins[n], dst_ref=outs[n].at[chip],
                send_sem=sems[4].at[j], recv_sem=sems[5].at[j],
                device_id=(*others[j], c), device_id_type=MESH)

        me = 2 * x + y
        sends = [chip_copy(a, j, me) for a in range(n) for j in range(3)]
        if conv_shard is not None:
            sends += [conv_copy(j, me) for j in range(3)]
        return c, others, sends, chip_copy, pass_copy, conv_copy

    def start(ins, outs, sems):
        for cp in copies(ins, outs, sems)[2]:
            cp.start()

    def finish(ins, outs, sems):
        c, others, sends, chip_copy, pass_copy, conv_copy = copies(ins, outs, sems)
        passed = []
        for a in range(n):
            mine, _ = _col_halves(ins[a].shape[1], c)
            for j, (ox, oy) in enumerate(others):
                chip_copy(a, j, 2 * ox + oy).wait_recv()
                passed.append(pass_copy(a, j, 2 * ox + oy, mine))
                passed[-1].start()
        for a in range(n):
            _, theirs = _col_halves(ins[a].shape[1], c)
            for j, (ox, oy) in enumerate(others):
                pass_copy(a, j, 2 * ox + oy, theirs).wait_recv()
        if conv_shard is not None:
            for j, (ox, oy) in enumerate(others):
                conv_copy(j, 2 * ox + oy).wait_recv()
        for cp in sends + passed:
            cp.wait_send()

    scratch = [pltpu.SemaphoreType.DMA((3 * n,))] * 4
    if conv_shard is not None:
        scratch += [pltpu.SemaphoreType.DMA((3,))] * 2
    return _Comm(inputs, [jax.ShapeDtypeStruct((N_CHIPS,) + s.shape, s.dtype) for s in inputs], scratch, start, finish)


def _fill_own(stacks, shards):
    chip = 2 * lax.axis_index("x") + lax.axis_index("y")
    return [lax.dynamic_update_index_in_dim(st, s, chip, 0) for st, s in zip(stacks, shards)]


def _run_comm(name, comm):
    ci, co = len(comm.inputs), len(comm.out_shape)

    def body(*refs):
        comm.start(refs[:ci], refs[ci:ci + co], refs[ci + co:])
        comm.finish(refs[:ci], refs[ci:ci + co], refs[ci + co:])

    return pl.pallas_call(body, name=name, in_specs=[ANY] * ci, out_specs=[ANY] * co, out_shape=comm.out_shape,
                          scratch_shapes=comm.scratch)(*comm.inputs)


def _sibling_exchange_comm(grads):
    n = len(grads)

    def copies(ins, outs, sems):
        x, y, c, _ = _place()
        return [pltpu.make_async_remote_copy(
            src_ref=ins[a].at[:, :, _col_halves(ins[a].shape[2], c)[1]], dst_ref=outs[a],
            send_sem=sems[0].at[a], recv_sem=sems[1].at[a],
            device_id=(x, y, 1 - c), device_id_type=MESH) for a in range(n)]

    def start(ins, outs, sems):
        for cp in copies(ins, outs, sems):
            cp.start()

    def finish(ins, outs, sems):
        for cp in copies(ins, outs, sems):
            cp.wait()

    half = lambda s: jax.ShapeDtypeStruct((s.shape[0], s.shape[1], s.shape[2] // 2), s.dtype)
    return _Comm(grads, [half(s) for s in grads], [pltpu.SemaphoreType.DMA((n,))] * 2, start, finish)


def _merge_comms(comms):
    def split(refs, count):
        out, at = [], 0
        for cm in comms:
            out.append(refs[at:at + count(cm)])
            at += count(cm)
        return out

    def parts(ins, outs, sems):
        return zip(comms, split(ins, lambda cm: len(cm.inputs)), split(outs, lambda cm: len(cm.out_shape)),
                   split(sems, lambda cm: len(cm.scratch)))

    def start(ins, outs, sems):
        for cm, i, o, s in parts(ins, outs, sems):
            cm.start(i, o, s)

    def finish(ins, outs, sems):
        for cm, i, o, s in parts(ins, outs, sems):
            cm.finish(i, o, s)

    return _Comm(sum([cm.inputs for cm in comms], []), sum([cm.out_shape for cm in comms], []),
                 sum([cm.scratch for cm in comms], []), start, finish)


def _add_halves(name, grads, recvs, core):
    n = len(grads)

    def body(core_ref, *refs):
        for g_ref, r_ref, out_ref in zip(refs[:n], refs[n:2 * n], refs[2 * n:]):
            out_ref[...] = (g_ref[...].astype(F32) + r_ref[...].astype(F32)).astype(BF16)

    half = lambda g: pl.BlockSpec((None, g.shape[1], g.shape[2] // 2), lambda k, core_ref: (k, 0, 0))
    mine = lambda g: pl.BlockSpec((None, g.shape[1], g.shape[2] // 2), lambda k, core_ref: (k, 0, core_ref[0]))
    return pl.pallas_call(
        body, name=name,
        grid_spec=pltpu.PrefetchScalarGridSpec(
            num_scalar_prefetch=1, grid=(N_CHIPS,),
            in_specs=[mine(g) for g in grads] + [half(g) for g in grads],
            out_specs=[half(g) for g in grads]),
        out_shape=[jax.ShapeDtypeStruct(r.shape, BF16) for r in recvs],
        compiler_params=_params(("arbitrary",)),
    )(core, *grads, *recvs)


def _chip_exchange_comm(parts):
    n = len(parts)

    def copies(ins, outs, sems):
        x, y, c, others = _place()
        return [pltpu.make_async_remote_copy(
            src_ref=ins[a].at[2 * ox + oy], dst_ref=outs[a].at[j],
            send_sem=sems[0].at[3 * a + j], recv_sem=sems[1].at[3 * a + j],
            device_id=(ox, oy, c), device_id_type=MESH) for a in range(n) for j, (ox, oy) in enumerate(others)]

    def start(ins, outs, sems):
        for cp in copies(ins, outs, sems):
            cp.start()

    def finish(ins, outs, sems):
        for cp in copies(ins, outs, sems):
            cp.wait()

    return _Comm(parts, [jax.ShapeDtypeStruct((3,) + s.shape[1:], s.dtype) for s in parts],
                 [pltpu.SemaphoreType.DMA((3 * n,))] * 2, start, finish)


HBM = pl.BlockSpec(memory_space=pltpu.HBM)
SEM = pl.BlockSpec(memory_space=pltpu.SEMAPHORE)


def _split_exchange_copies(parts, lands, send_sems, recv_sems):
    x, y, c, others = _place()
    return [pltpu.make_async_remote_copy(
        src_ref=parts[a].at[2 * ox + oy], dst_ref=lands[a].at[j],
        send_sem=send_sems.at[3 * a + j], recv_sem=recv_sems.at[3 * a + j],
        device_id=(ox, oy, c), device_id_type=MESH) for a in range(len(parts)) for j, (ox, oy) in enumerate(others)]


def _exchange_start(name, parts):
    n = len(parts)

    def body(*refs):
        ins, lands = refs[:n], refs[n:2 * n]
        send_sems, recv_sems, token = refs[2 * n], refs[2 * n + 1], refs[-1]
        for cp in _split_exchange_copies(ins, lands, send_sems, recv_sems):
            cp.start()
        token[...] = jnp.zeros_like(token)

    land_shape = [(3,) + p.shape[1:] for p in parts]
    outs = pl.pallas_call(
        body, name=name,
        out_shape=[pltpu.SemaphoreType.DMA((3 * n,)), pltpu.SemaphoreType.DMA((3 * n,))]
        + [pltpu.HBM(p.shape, p.dtype) for p in parts] + [pltpu.HBM(s, p.dtype) for s, p in zip(land_shape, parts)]
        + [jax.ShapeDtypeStruct((8, LANES), F32)],
        in_specs=[HBM] * (2 * n), out_specs=[SEM, SEM] + [HBM] * (2 * n) + [pl.BlockSpec(memory_space=pltpu.VMEM)],
        input_output_aliases={i: 2 + i for i in range(2 * n)},
        compiler_params=pltpu.CompilerParams(has_side_effects=pltpu.SideEffectType.DATAFLOW_SIDE_EFFECTING),
    )(*[pltpu.with_memory_space_constraint(p, pltpu.HBM) for p in parts],
      *[pltpu.with_memory_space_constraint(lax.empty(s, p.dtype), pltpu.HBM) for s, p in zip(land_shape, parts)])
    return outs[0], outs[1], list(outs[2:2 + n]), list(outs[2 + n:2 + 2 * n]), outs[-1]


def _exchange_wait(name, send_sems, recv_sems, parts, lands, after):
    n = len(parts)

    def body(*refs):
        ins, zones = refs[:n], refs[n:2 * n]
        for cp in _split_exchange_copies(ins, zones, refs[2 * n], refs[2 * n + 1]):
            cp.wait_send()
            cp.wait_recv()

    outs = pl.pallas_call(
        body, name=name,
        out_shape=[pltpu.HBM(p.shape, p.dtype) for p in parts] + [pltpu.HBM(z.shape, z.dtype) for z in lands],
        in_specs=[HBM] * (2 * n) + [SEM, SEM] + [ANY] * len(after), out_specs=[HBM] * (2 * n),
        input_output_aliases={i: i for i in range(2 * n)},
        compiler_params=pltpu.CompilerParams(has_side_effects=pltpu.SideEffectType.DATAFLOW_SIDE_EFFECTING),
    )(*parts, *lands, send_sems, recv_sems, *after)
    return list(outs[:n]), list(outs[n:])


def _sum_chips(name, owns, recvs, chip, after):
    n = len(owns)
    hc = owns[0].shape[2]
    assert all(o.shape[2] == hc for o in owns)

    def body(chip_ref, *refs):
        for own_ref, recv_ref, out_ref in zip(refs[:n], refs[n:2 * n], refs[2 * n + 1:]):
            acc = own_ref[...].astype(F32)
            for j in range(3):
                acc = acc + recv_ref[j].astype(F32)
            out_ref[...] = acc

    return pl.pallas_call(
        body, name=name,
        grid_spec=pltpu.PrefetchScalarGridSpec(
            num_scalar_prefetch=1, grid=(hc // LANES,),
            in_specs=[pl.BlockSpec((None, o.shape[1], LANES), lambda i, chip_ref: (chip_ref[0], 0, i)) for o in owns]
            + [pl.BlockSpec((3, o.shape[1], LANES), lambda i, chip_ref: (0, 0, i)) for o in owns]
            + [pl.BlockSpec((8, LANES), lambda i, chip_ref: (0, 0))],
            out_specs=[pl.BlockSpec((o.shape[1], LANES), lambda i, chip_ref: (0, i)) for o in owns]),
        out_shape=[jax.ShapeDtypeStruct((o.shape[1], hc), F32) for o in owns],
        compiler_params=_params(("arbitrary",)),
    )(chip, *owns, *recvs, after)


def _share_halves(name, halves):
    n = len(halves)

    def body(*refs):
        srcs, dsts = refs[:n], refs[n:2 * n]
        send_sems, recv_sems = refs[2 * n:]
        x, y, c, _ = _place()
        copies = [pltpu.make_async_remote_copy(
            src_ref=srcs[a], dst_ref=dsts[a], send_sem=send_sems.at[a], recv_sem=recv_sems.at[a],
            device_id=(x, y, 1 - c), device_id_type=MESH) for a in range(n)]
        for cp in copies:
            cp.start()
        for cp in copies:
            cp.wait()

    return pl.pallas_call(
        body, name=name,
        in_specs=[ANY] * n, out_specs=[ANY] * n,
        out_shape=[jax.ShapeDtypeStruct(s.shape, s.dtype) for s in halves],
        scratch_shapes=[pltpu.SemaphoreType.DMA((n,)), pltpu.SemaphoreType.DMA((n,))],
    )(*halves)


def _small_gather_comm(part):
    def copies(ins, outs, sems):
        x, y, c, _ = _place()
        me = 4 * x + 2 * y + c
        both = []
        for d in range(1, N_DEV):
            px, py, pc = (1 - x if d & 4 else x, 1 - y if d & 2 else y, 1 - c if d & 1 else c)
            send = pltpu.make_async_remote_copy(
                src_ref=ins[0], dst_ref=outs[0].at[me], send_sem=sems[0].at[d - 1], recv_sem=sems[1].at[d - 1],
                device_id=(px, py, pc), device_id_type=MESH)
            recv = pltpu.make_async_remote_copy(
                src_ref=ins[0], dst_ref=outs[0].at[4 * px + 2 * py + pc], send_sem=sems[0].at[d - 1],
                recv_sem=sems[1].at[d - 1], device_id=(px, py, pc), device_id_type=MESH)
            both.append((send, recv))
        return both

    def start(ins, outs, sems):
        for send, _ in copies(ins, outs, sems):
            send.start()

    def finish(ins, outs, sems):
        for send, recv in copies(ins, outs, sems):
            recv.wait_recv()
            send.wait_send()

    return _Comm([part], [jax.ShapeDtypeStruct((N_DEV,) + part.shape, F32)],
                 [pltpu.SemaphoreType.DMA((N_DEV - 1,))] * 2, start, finish)


def _sum_devices(parts):
    def body(p_ref, out_ref):
        acc = p_ref[0]
        for k in range(1, N_DEV):
            acc = acc + p_ref[k]
        out_ref[...] = acc

    return pl.pallas_call(
        body, name="sum_devices", grid=(1,),
        in_specs=[pl.BlockSpec(parts.shape, lambda i: (0, 0, 0))],
        out_specs=pl.BlockSpec(parts.shape[1:], lambda i: (0, 0)),
        out_shape=jax.ShapeDtypeStruct(parts.shape[1:], F32),
        compiler_params=_params(("arbitrary",)),
    )(parts)


def _adam_update(w, g, m, v):
    nm = ADAM_B1 * m + (1.0 - ADAM_B1) * g
    nv = ADAM_B2 * v + (1.0 - ADAM_B2) * (g * g)
    m_hat = nm * (1.0 / (1.0 - ADAM_B1 ** ADAM_STEP))
    v_hat = nv * (1.0 / (1.0 - ADAM_B2 ** ADAM_STEP))
    return -ADAM_LR * (m_hat / (jnp.sqrt(v_hat) + ADAM_EPS) + ADAM_WD * w), nm, nv


def _adamw(name, w, g, m, v):
    def body(w_ref, g_ref, m_ref, v_ref, d_ref, nm_ref, nv_ref):
        d_ref[...], nm_ref[...], nv_ref[...] = _adam_update(w_ref[...], g_ref[...], m_ref[...], v_ref[...])

    spec = pl.BlockSpec(w.shape, lambda i: (0, 0))
    out = jax.ShapeDtypeStruct(w.shape, F32)
    return pl.pallas_call(
        body, name=name, grid=(1,),
        in_specs=[spec] * 4, out_specs=[spec] * 3, out_shape=[out] * 3,
        compiler_params=_params(("arbitrary",)),
    )(w, g, m, v)


def _adamw_halves(name, ws, mines, theirs, ms, vs, core):
    n = len(ws)
    cols = ws[0].shape[1]
    assert all(w.shape[1] == cols for w in ws)
    hc = cols // 2
    tc = LANES if n > 1 else min(256, hc)
    nt = hc // tc

    def body(core_ref, *refs):
        ins, outs = refs[:5 * n], refs[5 * n:]
        for a in range(n):
            w_ref, mine_ref, theirs_ref, m_ref, v_ref = [ins[j * n + a] for j in range(5)]
            g_ref, d_ref, nm_ref, nv_ref = outs[4 * a:4 * a + 4]
            gv = jnp.where(pl.program_id(0) == core_ref[0], mine_ref[...], theirs_ref[...])
            g_ref[...] = gv
            d_ref[...], nm_ref[...], nv_ref[...] = _adam_update(w_ref[...], gv, m_ref[...], v_ref[...])

    whole = lambda w: pl.BlockSpec((w.shape[0], tc), lambda h, i, core_ref: (0, h * nt + i))
    mine_spec = lambda w: pl.BlockSpec((w.shape[0], tc), lambda h, i, core_ref: (0, jnp.where(h == core_ref[0], i, 0)))
    theirs_spec = lambda w: pl.BlockSpec((w.shape[0], tc), lambda h, i, core_ref: (0, jnp.where(h == core_ref[0], 0, i)))
    outs = pl.pallas_call(
        body, name=name,
        grid_spec=pltpu.PrefetchScalarGridSpec(
            num_scalar_prefetch=1, grid=(2, nt),
            in_specs=[whole(w) for w in ws] + [mine_spec(w) for w in ws] + [theirs_spec(w) for w in ws]
            + [whole(w) for w in ws] * 2,
            out_specs=[whole(w) for w in ws for _ in range(4)]),
        out_shape=[jax.ShapeDtypeStruct(w.shape, F32) for w in ws for _ in range(4)],
        compiler_params=_params(("arbitrary", "arbitrary")),
    )(core, *ws, *mines, *theirs, *ms, *vs)
    return [outs[4 * a:4 * a + 4] for a in range(n)]


WEIGHTS = ("ffn1_norm", "ffn1_gate", "ffn1_up", "ffn1_down", "mix_norm", "w_in", "b_forget", "conv_w",
           "w_o_attn", "w_o_conv", "w_out", "ffn2_norm", "ffn2_gate", "ffn2_up", "ffn2_down", "final_norm")
VEC_ROWS = 8


def _pack_small(t, conv_rows):
    conv = t["conv_w"]
    parts = [t[n].reshape(VEC_ROWS, LANES) for n in NORMS]
    parts.append(jnp.pad(conv, ((0, conv_rows - conv.shape[0]), (0, 0))))
    parts.append(jnp.pad(t["b_forget"].reshape(1, N_HEADS), ((0, 7), (0, LANES - N_HEADS))))
    return jnp.concatenate(parts, axis=0)


def _unpack_small(p, conv_rows):
    out = {n: p[VEC_ROWS * i:VEC_ROWS * (i + 1)].reshape(-1) for i, n in enumerate(NORMS)}
    base = VEC_ROWS * len(NORMS)
    out["conv_w"] = p[base:base + 3]
    out["b_forget"] = p[base + conv_rows, :N_HEADS]
    return out


def _travel(name, a):
    return a.T if name in TRANSPOSED else a


GATHER_FIRST = ("ffn1_gate", "ffn1_up")
GATHER_RIDES = {"ffn1_up": ("ffn1_down",), "ffn1_down": ("w_in",), "mix_proj_fwd": ("w_o_attn", "w_o_conv", "w_out"),
                "attn_fwd": ("ffn2_gate", "ffn2_up", "ffn2_down")}
SIBLING_RIDES = {"ffn2": "mix_out_bwd", "out": None, "w_in": "mix_proj_bwd_dx", "ffn1": None}
CHIP_RIDES = {"ffn2": "attn_bwd", "out": "attn_bwd", "w_in": "ffn1_bwd_dw", "ffn1": None}
SMALL_RIDE = "ffn1_bwd_dw"


class _MeshPlan(_LocalPlan):
    def __init__(self, wts, core):
        self.small, self.core = wts, core
        self.shards = {n: wts[n].astype(BF16) for n in BIG}
        self.chip_part, self.from_chips, self.rides = {}, {}, {}
        self.stacks = {}
        conv_shard = jnp.pad(wts["conv_w"], ((0, 8 - wts["conv_w"].shape[0]), (0, 0)))
        for kernel_name, names in GATHER_RIDES.items():
            mine = [self.shards[n] for n in names]
            conv = conv_shard if kernel_name == "ffn1_up" else None
            names = names + (("conv_w",) if conv is not None else ())
            mine = mine + ([conv] if conv is not None else [])
            self._ride(kernel_name, _gather_comm(mine[:len(mine) - (conv is not None)], conv),
                       lambda got, names=names, mine=mine: self.stacks.update(zip(names, _fill_own(got, mine))))

    def ffn1_up(self, x, tm):
        px, py = lax.axis_index("x"), lax.axis_index("y")
        order = jnp.stack([2 * px + py, 2 * (1 - px) + py, 2 * px + (1 - py), 2 * (1 - px) + (1 - py)]).astype(jnp.int32)
        own = [self.shards[n] for n in GATHER_FIRST]
        (hg, hu, n, sg, su), brought = _ffn_up_gather("ffn1_up", x, self.small["ffn1_norm"].reshape(1, -1), *own, order,
                                                     tm, self.rider("ffn1_up"))
        self.stacks.update(zip(GATHER_FIRST, _fill_own([sg, su], own)))
        self.arrived("ffn1_up", brought)
        return hg, hu, n

    def _ride(self, kernel_name, comm, then):
        self.rides.setdefault(kernel_name, []).append((comm, then))

    def rider(self, kernel_name):
        comms = [comm for comm, _ in self.rides.get(kernel_name, [])]
        return _merge_comms(comms) if comms else None

    def arrived(self, kernel_name, results):
        for comm, then in self.rides.pop(kernel_name, []):
            then(results[:len(comm.out_shape)])
            results = results[len(comm.out_shape):]

    def reduce(self, group, grads):
        names = tuple(grads)
        mine = [grads[n] for n in names]

        def with_sibling(from_sibling):
            parts = _add_halves("add_halves_" + group, mine, list(from_sibling), self.core)
            self.chip_part.update(zip(names, parts))
            if CHIP_RIDES[group] is None:
                self.last = (names, _exchange_start("exchange_start_" + group, parts))
            else:
                self._ride(CHIP_RIDES[group], _chip_exchange_comm(parts),
                           lambda got: self.from_chips.update(zip(names, got)))

        if SIBLING_RIDES[group] is None:
            with_sibling(_run_comm("sibling_exchange_" + group, _sibling_exchange_comm(mine)))
        else:
            self._ride(SIBLING_RIDES[group], _sibling_exchange_comm(mine), with_sibling)

    def reduce_small(self, gs, loss):
        conv_all = _shard_cols(gs["conv_w"]).reshape(N_CHIPS * 8, LANES)
        part = _pack_small({**{n: gs[n] for n in NORMS}, "conv_w": conv_all, "b_forget": gs["b_forget"][0, :N_HEADS]},
                           N_CHIPS * 8)
        part = jnp.concatenate([part, jnp.broadcast_to(loss, (8, LANES))], axis=0)
        me = 4 * lax.axis_index("x") + 2 * lax.axis_index("y") + lax.axis_index("c")

        def landed(got):
            self.small_parts = lax.dynamic_update_index_in_dim(got[0], part, me, 0)

        self._ride(SMALL_RIDE, _small_gather_comm(part), landed)


def kernel(x, ffn1_norm, ffn1_gate, ffn1_up, ffn1_down, mix_norm, w_in, b_forget, conv_w, w_o_attn, w_o_conv, w_out, ffn2_norm, ffn2_gate, ffn2_up, ffn2_down, final_norm, loss_target, m_ffn1_norm, m_ffn1_gate, m_ffn1_up, m_ffn1_down, m_mix_norm, m_w_in, m_b_forget, m_conv_w, m_w_o_attn, m_w_o_conv, m_w_out, m_ffn2_norm, m_ffn2_gate, m_ffn2_up, m_ffn2_down, m_final_norm, v_ffn1_norm, v_ffn1_gate, v_ffn1_up, v_ffn1_down, v_mix_norm, v_w_in, v_b_forget, v_conv_w, v_w_o_attn, v_w_o_conv, v_w_out, v_ffn2_norm, v_ffn2_gate, v_ffn2_up, v_ffn2_down, v_final_norm):
    given = dict(locals())
    wts = {n: _travel(n, given[n]) for n in WEIGHTS}
    mom = {n: _travel(n, given["m_" + n]) for n in WEIGHTS}
    var = {n: _travel(n, given["v_" + n]) for n in WEIGHTS}
    B, S, D = x.shape
    chip = 2 * lax.axis_index("x") + lax.axis_index("y")
    chip1 = chip.astype(jnp.int32).reshape(1)
    core = lax.axis_index("c").astype(jnp.int32).reshape(1)

    plan = _MeshPlan(wts, core)
    loss, grad_x, gs = _local_step(x.reshape(B * S, D), loss_target.reshape(B * S, D), plan, B, S)

    last_names, (send_sems, recv_sems, parts_thru, lands, token) = plan.last
    delta, new_m, new_v, grads = {}, {}, {}, {}

    def finish(tag, names):
        by_cols = {}
        for n in names:
            by_cols.setdefault(wts[n].shape[1], []).append(n)
        mine = {}
        for cols, ns in by_cols.items():
            mine.update(zip(ns, _sum_chips("sum_chips_%s_%d" % (tag, cols), [plan.chip_part[n] for n in ns],
                                           [plan.from_chips[n] for n in ns], chip1, token)))
        theirs = dict(zip(names, _share_halves("share_halves_" + tag, [mine[n] for n in names])))
        raw = []
        for cols, ns in by_cols.items():
            outs = _adamw_halves("adamw_%s_%d" % (tag, cols), [wts[n] for n in ns], [mine[n] for n in ns],
                                 [theirs[n] for n in ns], [mom[n] for n in ns], [var[n] for n in ns], core)
            for n, per in zip(ns, outs):
                raw.append(per[-1])
                grads[n], delta[n], new_m[n], new_v[n] = [_travel(n, o) for o in per]
        return raw

    small_sum = _sum_devices(plan.small_parts)
    base = VEC_ROWS * len(NORMS)
    loss_row = small_sum.shape[0] - 8
    small_grads = _unpack_small(small_sum, N_CHIPS * 8)
    small_grads["conv_w"] = lax.dynamic_slice_in_dim(small_sum[base:base + N_CHIPS * 8], chip * 8, 8, axis=0)[:3]
    packs = [_pack_small(t, 8) for t in (wts, small_grads, mom, var)]
    small_out = _adamw("adamw_small", *packs)

    done = finish("early", [n for n in BIG if n not in last_names])
    parts_back, got = _exchange_wait("exchange_wait", send_sems, recv_sems, parts_thru, lands, done + list(small_out))
    plan.chip_part.update(zip(last_names, parts_back))
    plan.from_chips.update(zip(last_names, got))
    finish("last", last_names)
    grads.update(small_grads)
    for out, p in zip((delta, new_m, new_v), small_out):
        out.update(_unpack_small(p, 8))

    return (small_sum[loss_row, 0], grad_x.reshape(B, S, D), *[grads[n] for n in WEIGHTS], *[delta[n] for n in WEIGHTS],
            *[new_m[n] for n in WEIGHTS], *[new_v[n] for n in WEIGHTS])
```

```python
import functools
import math

import jax
import jax.numpy as jnp
from jax import lax
from jax.experimental import pallas as pl
from jax.experimental.pallas import tpu as pltpu

F32 = jnp.float32
BF16 = jnp.bfloat16
MESH = pl.DeviceIdType.MESH

N_CHIPS = 4
N_DEV = 8
N_HEADS = 8
HEAD_DIM = 64
HEAD_PAIRS = N_HEADS // 2
ATTN_W = N_HEADS * HEAD_DIM
CONV_W = 512
RMS_EPS = 1e-6
FFN_RES = 0.5
LANES = 128
VMEM_LIMIT = 56 * 1024 * 1024
ROW_BLOCK = 256

ADAM_LR = 0.001
ADAM_B1 = 0.9
ADAM_B2 = 0.999
ADAM_EPS = 1e-08
ADAM_WD = 0.01
ADAM_STEP = 10

PROJ_W = 3 * ATTN_W + 3 * CONV_W + 2 * 1024
COL_CB, COL_CC, COL_CX = 3 * ATTN_W, 3 * ATTN_W + CONV_W, 3 * ATTN_W + 2 * CONV_W
COL_GATES = 3 * ATTN_W + 3 * CONV_W
N_FORGET_COL = 3 * ATTN_W


def _params(sem=None, vmem=VMEM_LIMIT):
    return pltpu.CompilerParams(dimension_semantics=sem, vmem_limit_bytes=vmem)


def _dot(a, b):
    return lax.dot_general(a, b, (((1,), (0,)), ((), ())), preferred_element_type=F32)


def _dot_nt(a, b):
    return lax.dot_general(a, b, (((1,), (1,)), ((), ())), preferred_element_type=F32)


def _dot_tn(a, b):
    return lax.dot_general(a, b, (((0,), (0,)), ((), ())), preferred_element_type=F32)


def _sigmoid(x):
    return 1.0 / (1.0 + jnp.exp(-x))


def _rms(xv):
    inv = lax.rsqrt(jnp.mean(xv * xv, axis=-1, keepdims=True) + RMS_EPS)
    return xv * inv, inv


class _Comm:
    def __init__(self, inputs, out_shape, scratch, start, finish):
        self.inputs, self.out_shape, self.scratch = list(inputs), list(out_shape), list(scratch)
        self.start, self.finish = start, finish


def _pallas(body, name, grid, in_specs, out_specs, out_shape, scratch, args, comm=None):
    sem = ("arbitrary",) * len(grid)
    if comm is None:
        outs = pl.pallas_call(body, name=name, grid=grid, in_specs=in_specs, out_specs=out_specs,
                              out_shape=out_shape, scratch_shapes=scratch, compiler_params=_params(sem))(*args)
        return list(outs), []
    n_in, n_out, n_scr = len(in_specs), len(out_specs), len(scratch)
    ci, co = len(comm.inputs), len(comm.out_shape)

    def riding(*refs):
        ins, refs = refs[:n_in], refs[n_in:]
        cins, refs = refs[:ci], refs[ci:]
        outs, refs = refs[:n_out], refs[n_out:]
        couts, refs = refs[:co], refs[co:]
        scr, sems = refs[:n_scr], refs[n_scr:]
        ids = [pl.program_id(d) for d in range(len(grid))]
        first = functools.reduce(lambda a, b: a & b, [i == 0 for i in ids])
        last = functools.reduce(lambda a, b: a & b, [i == g - 1 for i, g in zip(ids, grid)])

        @pl.when(first)
        def _():
            comm.start(cins, couts, sems)

        body(*ins, *outs, *scr)

        @pl.when(last)
        def _():
            comm.finish(cins, couts, sems)

    any_spec = pl.BlockSpec(memory_space=pl.ANY)
    outs = pl.pallas_call(
        riding, name=name, grid=grid,
        in_specs=list(in_specs) + [any_spec] * ci, out_specs=list(out_specs) + [any_spec] * co,
        out_shape=list(out_shape) + comm.out_shape, scratch_shapes=list(scratch) + comm.scratch,
        compiler_params=_params(sem))(*args, *comm.inputs)
    return list(outs[:n_out]), list(outs[n_out:])


def _rms_bwd(dn, xhat, inv, g):
    dxhat = dn * g
    dx = inv * (dxhat - xhat * jnp.mean(dxhat * xhat, axis=-1, keepdims=True))
    return dx, jnp.sum(dn * xhat, axis=0, keepdims=True)


def _ffn_fwd(name, x, g, wgt, wut, wd, tm, comm=None):
    T, D = x.shape
    K, Fs, _ = wgt.shape

    def body(x_ref, g_ref, wg_ref, wu_ref, wd_ref, out_ref, hg_ref, hu_ref, n_ref, acc_scr):
        k = pl.program_id(1)

        @pl.when(k == 0)
        def _():
            xhat, _ = _rms(x_ref[...])
            n_ref[...] = (xhat * g_ref[...]).astype(BF16)
            acc_scr[...] = jnp.zeros_like(acc_scr)

        n = n_ref[...]
        hg = _dot_nt(n, wg_ref[...])
        hu = _dot_nt(n, wu_ref[...])
        hg_ref[...] = hg.astype(BF16)
        hu_ref[...] = hu.astype(BF16)
        act = (hg * _sigmoid(hg) * hu).astype(BF16)
        acc_scr[...] += _dot(act, wd_ref[...])

        @pl.when(k == K - 1)
        def _():
            out_ref[...] = x_ref[...] + FFN_RES * acc_scr[...]

    w_spec = pl.BlockSpec((None, Fs, D), lambda i, k: (k, 0, 0))
    act_spec = pl.BlockSpec((None, tm, Fs), lambda i, k: (k, i, 0))
    return _pallas(
        body, name, (T // tm, K),
        [pl.BlockSpec((tm, D), lambda i, k: (i, 0)), pl.BlockSpec((1, D), lambda i, k: (0, 0)),
         w_spec, w_spec, w_spec],
        [pl.BlockSpec((tm, D), lambda i, k: (i, 0)), act_spec, act_spec, pl.BlockSpec((tm, D), lambda i, k: (i, 0))],
        [jax.ShapeDtypeStruct((T, D), F32), jax.ShapeDtypeStruct((K, T, Fs), BF16),
         jax.ShapeDtypeStruct((K, T, Fs), BF16), jax.ShapeDtypeStruct((T, D), BF16)],
        [pltpu.VMEM((tm, D), F32)],
        (x, g, wgt, wut, wd), comm)


def _ffn_up_gather(name, x, g, wg_own, wu_own, order, tm, comm=None):
    T, D = x.shape
    Fs = wg_own.shape[0]
    nt = T // tm
    ci, co = (len(comm.inputs), len(comm.out_shape)) if comm is not None else (0, 0)

    def body(order_ref, x_ref, g_ref, wgo_ref, wuo_ref, *rest):
        cins, rest = rest[:ci], rest[ci:]
        (hg_ref, hu_ref, n_ref, sg_ref, su_ref), rest = rest[:5], rest[5:]
        couts, rest = rest[:co], rest[co:]
        (n_all, wbuf, send_sems, recv_sems, pass_send, pass_recv, load_sems), csems = rest[:7], rest[7:]
        k, i = pl.program_id(0), pl.program_id(1)
        x_pos, y_pos, c, others = _place()
        me = 2 * x_pos + y_pos
        owns, stacks = (wgo_ref, wuo_ref), (sg_ref, su_ref)
        mine, theirs = _col_halves(D, c)

        def chip_copy(a, j, chip):
            return pltpu.make_async_remote_copy(
                src_ref=owns[a].at[:, mine], dst_ref=stacks[a].at[chip, :, mine],
                send_sem=send_sems.at[3 * a + j], recv_sem=recv_sems.at[3 * a + j],
                device_id=(*others[j], c), device_id_type=MESH)

        def pass_copy(a, j, chip, half):
            return pltpu.make_async_remote_copy(
                src_ref=stacks[a].at[chip, :, half], dst_ref=stacks[a].at[chip, :, half],
                send_sem=pass_send.at[3 * a + j], recv_sem=pass_recv.at[3 * a + j],
                device_id=(x_pos, y_pos, 1 - c), device_id_type=MESH)

        @pl.when((k == 0) & (i == 0))
        def _():
            for a in range(2):
                for j in range(3):
                    chip_copy(a, j, me).start()
            if comm is not None:
                comm.start(cins, couts, csems)

        def bring(j):
            ox, oy = others[j]
            chip = 2 * ox + oy
            for a in range(2):
                chip_copy(a, j, chip).wait_recv()
            for a in range(2):
                pass_copy(a, j, chip, mine).start()
            for a in range(2):
                pass_copy(a, j, chip, theirs).wait_recv()
            loads = [pltpu.make_async_copy(stacks[a].at[chip], wbuf.at[j % 2, a], load_sems.at[2 * (j % 2) + a])
                     for a in range(2)]
            for cp in loads:
                cp.start()
            for cp in loads:
                cp.wait()

        @pl.when((k == 1) & (i == 0))
        def _():
            bring(0)
            bring(1)

        @pl.when((k == 2) & (i == nt - 1))
        def _():
            bring(2)

        rows = pl.ds(pl.multiple_of(i * tm, tm), tm)

        @pl.when(k == 0)
        def _():
            xhat, _ = _rms(x_ref[...])
            n = (xhat * g_ref[...]).astype(BF16)
            n_ref[...] = n
            n_all[rows, :] = n
            hg_ref[...] = _dot_nt(n, wgo_ref[...]).astype(BF16)
            hu_ref[...] = _dot_nt(n, wuo_ref[...]).astype(BF16)

        @pl.when(k > 0)
        def _():
            n = n_all[rows, :]
            slot = (k - 1) % 2
            hg_ref[...] = _dot_nt(n, wbuf[slot, 0]).astype(BF16)
            hu_ref[...] = _dot_nt(n, wbuf[slot, 1]).astype(BF16)

        @pl.when((k == N_CHIPS - 1) & (i == nt - 1))
        def _():
            for a in range(2):
                for j, (ox, oy) in enumerate(others):
                    chip_copy(a, j, me).wait_send()
                    pass_copy(a, j, 2 * ox + oy, mine).wait_send()
            if comm is not None:
                comm.finish(cins, couts, csems)

    any_spec = pl.BlockSpec(memory_space=pl.ANY)
    first_pass = lambda k, i, order_ref: (jnp.where(k == 0, i, nt - 1), 0)
    whole = pl.BlockSpec((Fs, D), lambda k, i, order_ref: (0, 0))
    act_spec = pl.BlockSpec((None, tm, Fs), lambda k, i, order_ref: (order_ref[k], i, 0))
    stack = jax.ShapeDtypeStruct((N_CHIPS, Fs, D), BF16)
    outs = pl.pallas_call(
        body, name=name,
        grid_spec=pltpu.PrefetchScalarGridSpec(
            num_scalar_prefetch=1, grid=(N_CHIPS, nt),
            in_specs=[pl.BlockSpec((tm, D), first_pass), pl.BlockSpec((1, D), lambda k, i, order_ref: (0, 0)),
                      whole, whole] + [any_spec] * ci,
            out_specs=[act_spec, act_spec, pl.BlockSpec((tm, D), first_pass), any_spec, any_spec] + [any_spec] * co,
            scratch_shapes=[pltpu.VMEM((T, D), BF16), pltpu.VMEM((2, 2, Fs, D), BF16)]
            + [pltpu.SemaphoreType.DMA((6,))] * 4 + [pltpu.SemaphoreType.DMA((4,))]
            + (comm.scratch if comm is not None else [])),
        out_shape=[jax.ShapeDtypeStruct((N_CHIPS, T, Fs), BF16), jax.ShapeDtypeStruct((N_CHIPS, T, Fs), BF16),
                   jax.ShapeDtypeStruct((T, D), BF16), stack, stack] + (comm.out_shape if comm is not None else []),
        compiler_params=_params(("arbitrary", "arbitrary")),
    )(order, x, g, wg_own, wu_own, *(comm.inputs if comm is not None else []))
    return list(outs[:5]), list(outs[5:])


def _ffn_down(name, x, hg, hu, wd, tm, comm=None):
    T, D = x.shape
    K, Fs, _ = wd.shape

    def body(x_ref, hg_ref, hu_ref, wd_ref, out_ref, acc_scr):
        k = pl.program_id(1)

        @pl.when(k == 0)
        def _():
            acc_scr[...] = jnp.zeros_like(acc_scr)

        hgv = hg_ref[...].astype(F32)
        act = (hgv * _sigmoid(hgv) * hu_ref[...].astype(F32)).astype(BF16)
        acc_scr[...] += _dot(act, wd_ref[...])

        @pl.when(k == K - 1)
        def _():
            out_ref[...] = x_ref[...] + FFN_RES * acc_scr[...]

    act_spec = pl.BlockSpec((None, tm, Fs), lambda i, k: (k, i, 0))
    row = pl.BlockSpec((tm, D), lambda i, k: (i, 0))
    return _pallas(
        body, name, (T // tm, K),
        [row, act_spec, act_spec, pl.BlockSpec((None, Fs, D), lambda i, k: (k, 0, 0))],
        [row], [jax.ShapeDtypeStruct((T, D), F32)], [pltpu.VMEM((tm, D), F32)],
        (x, hg, hu, wd), comm)


def _ffn_bwd_dx(name, dout, x, g, hg, hu, wgt, wut, wd, tm, comm=None):
    T, D = x.shape
    K, Fs, _ = wgt.shape

    def body(dout_ref, x_ref, g_ref, hg_ref, hu_ref, wg_ref, wu_ref, wd_ref,
             dx_ref, dhg_ref, dhu_ref, dg_ref, df_ref, dn_scr):
        i, k = pl.program_id(0), pl.program_id(1)

        @pl.when(k == 0)
        def _():
            df_ref[...] = (FFN_RES * dout_ref[...]).astype(BF16)
            dn_scr[...] = jnp.zeros_like(dn_scr)

        @pl.when((k == 0) & (i == 0))
        def _():
            dg_ref[...] = jnp.zeros_like(dg_ref)

        for r0 in range(0, tm, ROW_BLOCK):
            rows = slice(r0, r0 + ROW_BLOCK)
            dact = _dot_nt(df_ref[rows, :], wd_ref[...])
            hgv = hg_ref[rows, :].astype(F32)
            huv = hu_ref[rows, :].astype(F32)
            s = _sigmoid(hgv)
            dhu = (dact * (hgv * s)).astype(BF16)
            dhg = (dact * huv * (s * (1.0 + hgv * (1.0 - s)))).astype(BF16)
            dhg_ref[rows, :] = dhg
            dhu_ref[rows, :] = dhu
            dn_scr[rows, :] += _dot(dhg, wg_ref[...]) + _dot(dhu, wu_ref[...])

        @pl.when(k == K - 1)
        def _():
            xhat, inv = _rms(x_ref[...])
            dx, dg = _rms_bwd(dn_scr[...], xhat, inv, g_ref[...])
            dx_ref[...] = dout_ref[...] + dx
            dg_ref[...] += dg

    w_spec = pl.BlockSpec((None, Fs, D), lambda i, k: (k, 0, 0))
    act_spec = pl.BlockSpec((None, tm, Fs), lambda i, k: (k, i, 0))
    row = pl.BlockSpec((tm, D), lambda i, k: (i, 0))
    row_once = pl.BlockSpec((tm, D), lambda i, k: (i, 0), pipeline_mode=pl.Buffered(1))
    vec = pl.BlockSpec((1, D), lambda i, k: (0, 0))
    return _pallas(
        body, name, (T // tm, K),
        [row, row_once, vec, act_spec, act_spec, w_spec, w_spec, w_spec],
        [row_once, act_spec, act_spec, vec, row],
        [jax.ShapeDtypeStruct((T, D), F32), jax.ShapeDtypeStruct((K, T, Fs), BF16),
         jax.ShapeDtypeStruct((K, T, Fs), BF16), jax.ShapeDtypeStruct((1, D), F32),
         jax.ShapeDtypeStruct((T, D), BF16)],
        [pltpu.VMEM((tm, D), F32)],
        (dout, x, g, hg, hu, wgt, wut, wd), comm)


def _ffn_bwd_dw(name, n, df, hg, hu, dhg, dhu, tk, comm=None):
    T, D = n.shape
    K, _, Fs = hg.shape
    nt = T // tk

    def body(n_ref, df_ref, hg_ref, hu_ref, dhg_ref, dhu_ref, dwg_ref, dwu_ref, dwd_ref, accg, accu, accd):
        t = pl.program_id(1)

        @pl.when(t == 0)
        def _():
            accg[...] = jnp.zeros_like(accg)
            accu[...] = jnp.zeros_like(accu)
            accd[...] = jnp.zeros_like(accd)

        nv = n_ref[...]
        hgv = hg_ref[...].astype(F32)
        act = (hgv * _sigmoid(hgv) * hu_ref[...].astype(F32)).astype(BF16)
        accg[...] += _dot_tn(dhg_ref[...], nv)
        accu[...] += _dot_tn(dhu_ref[...], nv)
        accd[...] += _dot_tn(act, df_ref[...])

        @pl.when(t == nt - 1)
        def _():
            dwg_ref[...] = accg[...].astype(BF16)
            dwu_ref[...] = accu[...].astype(BF16)
            dwd_ref[...] = accd[...].astype(BF16)

    act_spec = pl.BlockSpec((None, tk, Fs), lambda k, t: (k, t, 0))
    w_spec = pl.BlockSpec((None, Fs, D), lambda k, t: (k, 0, 0))
    row = pl.BlockSpec((tk, D), lambda k, t: (t, 0))
    return _pallas(
        body, name, (K, nt),
        [row, row, act_spec, act_spec, act_spec, act_spec],
        [w_spec, w_spec, w_spec],
        [jax.ShapeDtypeStruct((K, Fs, D), BF16)] * 3,
        [pltpu.VMEM((Fs, D), F32)] * 3,
        (n, df, hg, hu, dhg, dhu), comm)


def _mix_proj_fwd(x, g, wproj_t, wf_t, tm, tn, comm=None):
    T, D = x.shape
    N = wproj_t.shape[0]

    def body(x_ref, g_ref, w_ref, wf_ref, h_ref, proj_ref, flog_ref, h_scr):
        @pl.when(pl.program_id(1) == 0)
        def _():
            xhat, _ = _rms(x_ref[...])
            h = (xhat * g_ref[...]).astype(BF16)
            h_scr[...] = h
            h_ref[...] = h
            flog_ref[...] = _dot_nt(h, wf_ref[...])

        proj_ref[...] = _dot_nt(h_scr[...], w_ref[...]).astype(BF16)

    return _pallas(
        body, "mix_proj_fwd", (T // tm, N // tn),
        [pl.BlockSpec((tm, D), lambda i, n: (i, 0)), pl.BlockSpec((1, D), lambda i, n: (0, 0)),
         pl.BlockSpec((tn, D), lambda i, n: (n, 0)), pl.BlockSpec((LANES, D), lambda i, n: (0, 0))],
        [pl.BlockSpec((tm, D), lambda i, n: (i, 0)), pl.BlockSpec((tm, tn), lambda i, n: (i, n)),
         pl.BlockSpec((tm, LANES), lambda i, n: (i, 0))],
        [jax.ShapeDtypeStruct((T, D), BF16), jax.ShapeDtypeStruct((T, N), BF16),
         jax.ShapeDtypeStruct((T, LANES), F32)],
        [pltpu.VMEM((tm, D), BF16)],
        (x, g, wproj_t, wf_t), comm)


def _log_sigmoid(z):
    return -(jnp.maximum(-z, 0.0) + jnp.log(1.0 + jnp.exp(-jnp.abs(z))))


def _tri(n, lower):
    r = lax.broadcasted_iota(jnp.int32, (n, n), 0)
    c = lax.broadcasted_iota(jnp.int32, (n, n), 1)
    return jnp.where((r >= c) if lower else (r <= c), 1.0, 0.0).astype(F32)


def _dot_f32(a, b):
    return lax.dot_general(a, b, (((1,), (0,)), ((), ())), preferred_element_type=F32,
                           precision=lax.Precision.HIGHEST)


def _fgate_fwd(flog, bias, B, S, ch):
    def body(flog_ref, b_ref, cum_ref):
        tri = _tri(ch, True)
        carry = jnp.zeros((1, LANES), F32)
        for c0 in range(0, S, ch):
            lf = _log_sigmoid(flog_ref[c0:c0 + ch, :] + b_ref[...])
            cs = _dot_f32(tri, lf) + carry
            cum_ref[c0:c0 + ch, :] = cs
            carry = cs[ch - 1:ch, :]

    return pl.pallas_call(
        body, name="fgate_fwd", grid=(B,),
        in_specs=[pl.BlockSpec((S, LANES), lambda b: (b, 0)),
                  pl.BlockSpec((1, LANES), lambda b: (0, 0))],
        out_specs=pl.BlockSpec((S, LANES), lambda b: (b, 0)),
        out_shape=jax.ShapeDtypeStruct((B * S, LANES), F32),
        compiler_params=_params(("arbitrary",)),
    )(flog, bias)


def _fgate_bwd(dcum, flog, bias, B, S, ch):
    def body(dcum_ref, flog_ref, b_ref, dflog_ref, db_ref):
        @pl.when(pl.program_id(0) == 0)
        def _():
            db_ref[...] = jnp.zeros_like(db_ref)

        tri = _tri(ch, False)
        carry = jnp.zeros((1, LANES), F32)
        db = jnp.zeros((1, LANES), F32)
        for c0 in range(S - ch, -1, -ch):
            dlf = _dot_f32(tri, dcum_ref[c0:c0 + ch, :]) + carry
            carry = dlf[0:1, :]
            z = flog_ref[c0:c0 + ch, :] + b_ref[...]
            dz = dlf * _sigmoid(-z)
            dflog_ref[c0:c0 + ch, :] = dz
            db = db + jnp.sum(dz, axis=0, keepdims=True)
        db_ref[...] += db

    return pl.pallas_call(
        body, name="fgate_bwd", grid=(B,),
        in_specs=[pl.BlockSpec((S, LANES), lambda b: (b, 0)),
                  pl.BlockSpec((S, LANES), lambda b: (b, 0)),
                  pl.BlockSpec((1, LANES), lambda b: (0, 0))],
        out_specs=[pl.BlockSpec((S, LANES), lambda b: (b, 0)),
                   pl.BlockSpec((1, LANES), lambda b: (0, 0))],
        out_shape=[jax.ShapeDtypeStruct((B * S, LANES), F32),
                   jax.ShapeDtypeStruct((1, LANES), F32)],
        compiler_params=_params(("arbitrary",)),
    )(dcum, flog, bias)


def _pick_lane(tile, h):
    lane = lax.broadcasted_iota(jnp.int32, tile.shape, 1)
    return jnp.sum(jnp.where(lane == h, tile, 0.0), axis=1, keepdims=True)


def _put_lane(col, h, width=LANES):
    lane = lax.broadcasted_iota(jnp.int32, (col.shape[0], width), 1)
    return jnp.where(lane == h, col, 0.0)


def _pick_row(tile, h):
    row = lax.broadcasted_iota(jnp.int32, tile.shape, 0)
    return jnp.sum(jnp.where(row == h, tile, 0.0), axis=0, keepdims=True)


def _put_row(vec, h):
    row = lax.broadcasted_iota(jnp.int32, (8, vec.shape[1]), 0)
    return jnp.where(row == h, vec, 0.0)


def _causal(tq):
    r = lax.broadcasted_iota(jnp.int32, (tq, tq), 0)
    c = lax.broadcasted_iota(jnp.int32, (tq, tq), 1)
    return r >= c


def _head_halves(t):
    lo = lax.broadcasted_iota(jnp.int32, t.shape, 1) < HEAD_DIM
    zero = jnp.zeros_like(t)
    return jnp.where(lo, t, zero), jnp.where(lo, zero, t)


NEG = -1e30
ATTN_SCALE = 1.0 / math.sqrt(HEAD_DIM)


def _scaled(q):
    return (q.astype(F32) * ATTN_SCALE).astype(q.dtype)


def _attn_fwd(proj, cum, cum_t, B, S, tq, comm=None):
    nq = S // tq

    def body(q_ref, k_ref, v_ref, cum_ref, cumt_ref, o_ref, lse_ref):
        qi, hp = pl.program_id(1), pl.program_id(2)
        qm = _head_halves(_scaled(q_ref[...]))
        cumv = cum_ref[...]
        cq = [_pick_lane(cumv, 2 * hp + e) for e in range(2)]

        def tile(j, carry, masked):
            off = pl.multiple_of(j * tq, tq)
            kj = k_ref[pl.ds(off, tq), :]
            vj = v_ref[pl.ds(off, tq), :]
            ct = cumt_ref[j]
            new = []
            for e in range(2):
                m, l, acc = carry[e]
                s = _dot_nt(qm[e], kj) - _pick_row(ct, 2 * hp + e)
                if masked:
                    s = jnp.where(_causal(tq), s, NEG)
                m_new = jnp.maximum(m, jnp.max(s, axis=1, keepdims=True))
                p = jnp.exp(s - m_new)
                alpha = jnp.exp(m - m_new)
                l = alpha * l + jnp.sum(p, axis=1, keepdims=True)
                acc = alpha * acc + _dot(p.astype(BF16), vj)
                new.append((m_new, l, acc))
            return tuple(new)

        one = (jnp.full((tq, 1), NEG, F32), jnp.zeros((tq, 1), F32), jnp.zeros((tq, LANES), F32))
        carry = lax.fori_loop(0, qi, lambda j, c: tile(j, c, False), (one, one))
        (ma, la, acca), (mb, lb, accb) = tile(qi, carry, True)
        lo = lax.broadcasted_iota(jnp.int32, (tq, LANES), 1) < HEAD_DIM
        o_ref[...] = jnp.where(lo, acca / la, accb / lb).astype(BF16)

        @pl.when(hp == 0)
        def _():
            lse_ref[...] = jnp.zeros_like(lse_ref)

        lse_ref[...] += (_put_lane(ma + jnp.log(la) + cq[0], 2 * hp) + _put_lane(mb + jnp.log(lb) + cq[1], 2 * hp + 1))

    kv = lambda first: pl.BlockSpec((S, LANES), lambda b, i, hp: (b, first + hp))
    return _pallas(
        body, "attn_fwd", (B, nq, HEAD_PAIRS),
        [pl.BlockSpec((tq, LANES), lambda b, i, hp: (b * nq + i, hp)),
         kv(ATTN_W // LANES), kv(2 * ATTN_W // LANES),
         pl.BlockSpec((tq, LANES), lambda b, i, hp: (b * nq + i, 0)),
         pl.BlockSpec((None, nq, 8, tq), lambda b, i, hp: (b, 0, 0, 0))],
        [pl.BlockSpec((tq, LANES), lambda b, i, hp: (b * nq + i, hp)),
         pl.BlockSpec((tq, LANES), lambda b, i, hp: (b * nq + i, 0))],
        [jax.ShapeDtypeStruct((B * S, ATTN_W), BF16), jax.ShapeDtypeStruct((B * S, LANES), F32)],
        [], (proj, proj, proj, cum, cum_t), comm)


def _attn_bwd(proj, o, do, lse, cum, cum_t, B, S, tq, comm=None):
    nq = S // tq

    def body(q_ref, k_ref, v_ref, o_ref, do_ref, lse_ref, cum_ref, cumt_ref,
             dq_ref, dk_ref, dv_ref, dcq_ref, dck_ref, dq_scr):
        hp, kj = pl.program_id(1), pl.program_id(2)

        @pl.when(kj == 0)
        def _():
            dq_scr[...] = jnp.zeros_like(dq_scr)

        @pl.when((kj == 0) & (hp == 0))
        def _():
            dcq_ref[...] = jnp.zeros_like(dcq_ref)
            dck_ref[...] = jnp.zeros_like(dck_ref)

        kv = k_ref[...]
        vv = v_ref[...]
        km = _head_halves(kv)
        ct = cumt_ref[...]
        ck = [_pick_row(ct, 2 * hp + e) for e in range(2)]

        def tile(i, carry, masked):
            dk, dv, dcol = carry
            off = pl.multiple_of(i * tq, tq)
            qi = q_ref[pl.ds(off, tq), :]
            ov = o_ref[pl.ds(off, tq), :].astype(F32)
            qm = _head_halves(_scaled(qi))
            dom = _head_halves(do_ref[pl.ds(off, tq), :])
            cumv = cum_ref[pl.ds(off, tq), :]
            lsev = lse_ref[pl.ds(off, tq), :]
            dcq = jnp.zeros((tq, LANES), F32)
            dq = jnp.zeros((tq, LANES), F32)
            dcol_new = []
            for e in range(2):
                delta = jnp.sum(dom[e].astype(F32) * ov, axis=1, keepdims=True)
                row_term = _pick_lane(cumv, 2 * hp + e) - _pick_lane(lsev, 2 * hp + e)
                p = jnp.exp(_dot_nt(qm[e], kv) + row_term - ck[e])
                if masked:
                    p = jnp.where(_causal(tq), p, 0.0)
                dv = dv + _dot_tn(dom[e], p.astype(BF16))
                ds = p * (_dot_nt(dom[e], vv) - delta)
                dcol_new.append(dcol[e] + jnp.sum(ds, axis=0, keepdims=True))
                dcq = dcq + _put_lane(jnp.sum(ds, axis=1, keepdims=True), 2 * hp + e)
                dsb = ds.astype(BF16)
                dk = dk + _dot_tn(qm[e], dsb)
                dq = dq + _dot(dsb, km[e]) * ATTN_SCALE
            dq_scr[pl.ds(off, tq), :] += dq
            dcq_ref[pl.ds(off, tq), :] += dcq
            return dk, dv, tuple(dcol_new)

        zero_row = jnp.zeros((1, tq), F32)
        init = (jnp.zeros((LANES, tq), F32), jnp.zeros((LANES, tq), F32), (zero_row, zero_row))
        carry = tile(kj, init, True)
        dk, dv, dcol = lax.fori_loop(kj + 1, nq, lambda i, c: tile(i, c, False), carry)
        dk_ref[...] = dk.T.astype(BF16)
        dv_ref[...] = dv.T.astype(BF16)
        dck_ref[kj] += -(_put_row(dcol[0], 2 * hp) + _put_row(dcol[1], 2 * hp + 1))

        @pl.when(kj == nq - 1)
        def _():
            dq_ref[...] = dq_scr[...].astype(BF16)

    seq = lambda first: pl.BlockSpec((S, LANES), lambda b, hp, j: (b, first + hp))
    tile_in = lambda first: pl.BlockSpec((tq, LANES), lambda b, hp, j: (b * nq + j, first + hp))
    lanes0 = pl.BlockSpec((S, LANES), lambda b, hp, j: (b, 0))
    out = jax.ShapeDtypeStruct((B * S, ATTN_W), BF16)
    return _pallas(
        body, "attn_bwd", (B, HEAD_PAIRS, nq),
        [seq(0), tile_in(ATTN_W // LANES), tile_in(2 * ATTN_W // LANES), seq(0), seq(0), lanes0, lanes0,
         pl.BlockSpec((None, None, 8, tq), lambda b, hp, j: (b, j, 0, 0))],
        [seq(0), tile_in(0), tile_in(0), lanes0,
         pl.BlockSpec((None, nq, 8, tq), lambda b, hp, j: (b, 0, 0, 0))],
        [out, out, out, jax.ShapeDtypeStruct((B * S, LANES), F32), jax.ShapeDtypeStruct((B, nq, 8, tq), F32)],
        [pltpu.VMEM((S, LANES), F32)],
        (proj, proj, proj, o, do, lse, cum, cum_t), comm)


def _shift_down(u, n):
    row = lax.broadcasted_iota(jnp.int32, u.shape, 0)
    return jnp.where(row >= n, pltpu.roll(u, n, 0), 0.0)


def _shift_up(u, n):
    rows = u.shape[0]
    row = lax.broadcasted_iota(jnp.int32, u.shape, 0)
    return jnp.where(row < rows - n, pltpu.roll(u, rows - n, 0), 0.0)


def _conv_specs(S):
    cb = pl.BlockSpec((S, LANES), lambda g, b: (b, COL_CB // LANES + g))
    cc = pl.BlockSpec((S, LANES), lambda g, b: (b, COL_CC // LANES + g))
    cx = pl.BlockSpec((S, LANES), lambda g, b: (b, COL_CX // LANES + g))
    w = pl.BlockSpec((8, LANES), lambda g, b: (0, g))
    return cb, cc, cx, w


def _conv_fwd(proj, conv_w, B, S):
    def body(cb_ref, cc_ref, cx_ref, w_ref, y_ref):
        u = cc_ref[...].astype(F32) * cx_ref[...].astype(F32)
        w = w_ref[...]
        conv = w[0:1, :] * _shift_down(u, 2) + w[1:2, :] * _shift_down(u, 1) + w[2:3, :] * u
        y_ref[...] = (cb_ref[...].astype(F32) * conv).astype(BF16)

    cb, cc, cx, w = _conv_specs(S)
    return pl.pallas_call(
        body, name="conv_fwd", grid=(CONV_W // LANES, B),
        in_specs=[cb, cc, cx, w],
        out_specs=pl.BlockSpec((S, LANES), lambda g, b: (b, g)),
        out_shape=jax.ShapeDtypeStruct((B * S, CONV_W), BF16),
        compiler_params=_params(("arbitrary", "arbitrary")),
    )(proj, proj, proj, conv_w)


def _conv_bwd(dy, proj, conv_w, B, S):
    def body(dy_ref, cb_ref, cc_ref, cx_ref, w_ref, dcb_ref, dcc_ref, dcx_ref, dw_ref):
        @pl.when(pl.program_id(1) == 0)
        def _():
            dw_ref[...] = jnp.zeros_like(dw_ref)

        ccv = cc_ref[...].astype(F32)
        cxv = cx_ref[...].astype(F32)
        u = ccv * cxv
        u1 = _shift_down(u, 1)
        u2 = _shift_down(u, 2)
        w = w_ref[...]
        conv = w[0:1, :] * u2 + w[1:2, :] * u1 + w[2:3, :] * u
        dyv = dy_ref[...].astype(F32)
        dcb_ref[...] = (dyv * conv).astype(BF16)
        dconv = dyv * cb_ref[...].astype(F32)
        du = w[2:3, :] * dconv + w[1:2, :] * _shift_up(dconv, 1) + w[0:1, :] * _shift_up(dconv, 2)
        dcc_ref[...] = (du * cxv).astype(BF16)
        dcx_ref[...] = (du * ccv).astype(BF16)
        row = lax.broadcasted_iota(jnp.int32, (8, LANES), 0)
        dw = jnp.where(row == 0, jnp.sum(dconv * u2, axis=0, keepdims=True),
                       jnp.where(row == 1, jnp.sum(dconv * u1, axis=0, keepdims=True),
                                 jnp.where(row == 2, jnp.sum(dconv * u, axis=0, keepdims=True), 0.0)))
        dw_ref[...] += dw

    cb, cc, cx, w = _conv_specs(S)
    out = pl.BlockSpec((S, LANES), lambda g, b: (b, g))
    return pl.pallas_call(
        body, name="conv_bwd", grid=(CONV_W // LANES, B),
        in_specs=[out, cb, cc, cx, w],
        out_specs=[out, out, out, w],
        out_shape=[jax.ShapeDtypeStruct((B * S, CONV_W), BF16)] * 3 + [jax.ShapeDtypeStruct((8, CONV_W), F32)],
        compiler_params=_params(("arbitrary", "arbitrary")),
    )(dy, proj, proj, proj, conv_w)


def _gate_specs(tm, D):
    ga = pl.BlockSpec((tm, D), lambda i: (i, COL_GATES // D))
    gc = pl.BlockSpec((tm, D), lambda i: (i, COL_GATES // D + 1))
    return ga, gc


def _mix_out_fwd(x, o, yc, proj, woa, woc, wout, tm):
    T, D = x.shape

    def body(x_ref, o_ref, yc_ref, ga_ref, gc_ref, woa_ref, woc_ref, wout_ref, out_ref):
        ya = _dot(o_ref[...], woa_ref[...])
        yp = _dot(yc_ref[...], woc_ref[...])
        merged = _sigmoid(ga_ref[...].astype(F32)) * ya + _sigmoid(gc_ref[...].astype(F32)) * yp
        out_ref[...] = x_ref[...] + _dot(merged.astype(BF16), wout_ref[...])

    ga, gc = _gate_specs(tm, D)
    row = lambda w: pl.BlockSpec((tm, w), lambda i: (i, 0))
    whole = lambda a: pl.BlockSpec(a.shape, lambda i: (0, 0))
    return pl.pallas_call(
        body, name="mix_out_fwd", grid=(T // tm,),
        in_specs=[row(D), row(ATTN_W), row(CONV_W), ga, gc, whole(woa), whole(woc), whole(wout)],
        out_specs=row(D),
        out_shape=jax.ShapeDtypeStruct((T, D), F32),
        compiler_params=_params(("arbitrary",)),
    )(x, o, yc, proj, proj, woa, woc, wout)


def _mix_out_bwd(dx, o, yc, proj, woa, woc, wout, tm, comm=None):
    T, D = dx.shape
    nt = T // tm

    def body(dx_ref, o_ref, yc_ref, ga_ref, gc_ref, woa_ref, woc_ref, wout_ref,
             do_ref, dyc_ref, dg_ref, dwoa_ref, dwoc_ref, dwout_ref, acca, accc, acco):
        t = pl.program_id(0)

        @pl.when(t == 0)
        def _():
            acca[...] = jnp.zeros_like(acca)
            accc[...] = jnp.zeros_like(accc)
            acco[...] = jnp.zeros_like(acco)

        dxb = dx_ref[...].astype(BF16)
        ov, ycv = o_ref[...], yc_ref[...]
        ya = _dot(ov, woa_ref[...])
        yp = _dot(ycv, woc_ref[...])
        sa = _sigmoid(ga_ref[...].astype(F32))
        sc = _sigmoid(gc_ref[...].astype(F32))
        merged = (sa * ya + sc * yp).astype(BF16)
        dm = _dot_nt(dxb, wout_ref[...])
        dya = (dm * sa).astype(BF16)
        dyp = (dm * sc).astype(BF16)
        dg_ref[:, :D] = (dm * ya * sa * (1.0 - sa)).astype(BF16)
        dg_ref[:, D:] = (dm * yp * sc * (1.0 - sc)).astype(BF16)
        do_ref[...] = _dot_nt(dya, woa_ref[...]).astype(BF16)
        dyc_ref[...] = _dot_nt(dyp, woc_ref[...]).astype(BF16)
        acca[...] += _dot_tn(ov, dya)
        accc[...] += _dot_tn(ycv, dyp)
        acco[...] += _dot_tn(merged, dxb)

        @pl.when(t == nt - 1)
        def _():
            dwoa_ref[...] = acca[...].astype(BF16)
            dwoc_ref[...] = accc[...].astype(BF16)
            dwout_ref[...] = acco[...].astype(BF16)

    ga, gc = _gate_specs(tm, D)
    row = lambda w: pl.BlockSpec((tm, w), lambda i: (i, 0))
    whole = lambda a: pl.BlockSpec(a.shape, lambda i: (0, 0))
    return _pallas(
        body, "mix_out_bwd", (nt,),
        [row(D), row(ATTN_W), row(CONV_W), ga, gc, whole(woa), whole(woc), whole(wout)],
        [row(ATTN_W), row(CONV_W), row(2 * D), whole(woa), whole(woc), whole(wout)],
        [jax.ShapeDtypeStruct((T, ATTN_W), BF16), jax.ShapeDtypeStruct((T, CONV_W), BF16),
         jax.ShapeDtypeStruct((T, 2 * D), BF16),
         jax.ShapeDtypeStruct(woa.shape, BF16), jax.ShapeDtypeStruct(woc.shape, BF16),
         jax.ShapeDtypeStruct(wout.shape, BF16)],
        [pltpu.VMEM(woa.shape, F32), pltpu.VMEM(woc.shape, F32), pltpu.VMEM(wout.shape, F32)],
        (dx, o, yc, proj, proj, woa, woc, wout), comm)


def _proj_pieces(dq, dk, dv, dcb, dcc, dcx, dgates, dflog):
    D = dgates.shape[1] // 2
    return [(dq, ATTN_W, 0), (dk, ATTN_W, 0), (dv, ATTN_W, 0), (dcb, CONV_W, 0), (dcc, CONV_W, 0), (dcx, CONV_W, 0),
            (dgates, D, 0), (dgates, D, 1), (dflog, LANES, 0)]


def _mix_proj_bwd_dx(dres, x, g, pieces, wproj_t, wf_t, tm, comm=None):
    T, D = x.shape
    n = len(pieces)
    w_blocks = [(ATTN_W, 0), (ATTN_W, 1), (ATTN_W, 2), (CONV_W, 3), (CONV_W, 4), (CONV_W, 5),
                (D, COL_GATES // D), (D, COL_GATES // D + 1)]

    def body(*refs):
        dres_ref, x_ref, g_ref = refs[:3]
        p_refs, w_refs = refs[3:3 + n], refs[3 + n:3 + 2 * n]
        dx_ref, dg_ref = refs[3 + 2 * n:]

        @pl.when(pl.program_id(0) == 0)
        def _():
            dg_ref[...] = jnp.zeros_like(dg_ref)

        dh = _dot(p_refs[0][...].astype(BF16), w_refs[0][...])
        for p_ref, w_ref in zip(p_refs[1:], w_refs[1:]):
            dh = dh + _dot(p_ref[...].astype(BF16), w_ref[...])
        xhat, inv = _rms(x_ref[...])
        dx, dg = _rms_bwd(dh, xhat, inv, g_ref[...])
        dx_ref[...] = dres_ref[...] + dx
        dg_ref[...] += dg

    row = pl.BlockSpec((tm, D), lambda i: (i, 0))
    vec = pl.BlockSpec((1, D), lambda i: (0, 0))
    p_specs = [pl.BlockSpec((tm, w), lambda i, cb=cb: (i, cb)) for _, w, cb in pieces]
    w_specs = [pl.BlockSpec((r, D), lambda i, rb=rb: (rb, 0)) for r, rb in w_blocks]
    w_specs.append(pl.BlockSpec((LANES, D), lambda i: (0, 0)))
    return _pallas(
        body, "mix_proj_bwd_dx", (T // tm,),
        [row, row, vec] + p_specs + w_specs, [row, vec],
        [jax.ShapeDtypeStruct((T, D), F32), jax.ShapeDtypeStruct((1, D), F32)], [],
        (dres, x, g, *[p for p, _, _ in pieces], *([wproj_t] * len(w_blocks)), wf_t), comm)


def _matmuls_tn(name, pieces, b, tk):
    T, N = b.shape
    nt = T // tk
    n = len(pieces)

    def body(*refs):
        a_refs, b_ref, out_refs, accs = refs[:n], refs[n], refs[n + 1:2 * n + 1], refs[2 * n + 1:]
        t = pl.program_id(0)

        @pl.when(t == 0)
        def _():
            for acc in accs:
                acc[...] = jnp.zeros_like(acc)

        bv = b_ref[...]
        for a_ref, acc in zip(a_refs, accs):
            acc[...] += _dot_tn(a_ref[...].astype(BF16), bv)

        @pl.when(t == nt - 1)
        def _():
            for out_ref, acc in zip(out_refs, accs):
                out_ref[...] = acc[...].astype(BF16)

    return pl.pallas_call(
        body, name=name, grid=(nt,),
        in_specs=[pl.BlockSpec((tk, w), lambda t, cb=cb: (t, cb)) for _, w, cb in pieces]
        + [pl.BlockSpec((tk, N), lambda t: (t, 0))],
        out_specs=[pl.BlockSpec((w, N), lambda t: (0, 0)) for _, w, _ in pieces],
        out_shape=[jax.ShapeDtypeStruct((w, N), BF16) for _, w, _ in pieces],
        scratch_shapes=[pltpu.VMEM((w, N), F32) for _, w, _ in pieces],
        compiler_params=_params(("arbitrary",)),
    )(*[a for a, _, _ in pieces], b)


def _final_loss(x, target, g, tm):
    T, D = x.shape

    def body(x_ref, t_ref, g_ref, dx_ref, loss_ref, dg_ref):
        @pl.when(pl.program_id(0) == 0)
        def _():
            loss_ref[...] = jnp.zeros_like(loss_ref)
            dg_ref[...] = jnp.zeros_like(dg_ref)

        xhat, inv = _rms(x_ref[...])
        err = xhat * g_ref[...] - t_ref[...]
        loss_ref[...] += 0.5 * jnp.sum(jnp.sum(err * err, axis=1, keepdims=True), axis=0, keepdims=True) / D
        dx, dg = _rms_bwd(err * (1.0 / D), xhat, inv, g_ref[...])
        dx_ref[...] = dx
        dg_ref[...] += dg

    row = pl.BlockSpec((tm, D), lambda i: (i, 0))
    return pl.pallas_call(
        body, name="final_loss", grid=(T // tm,),
        in_specs=[row, row, pl.BlockSpec((1, D), lambda i: (0, 0))],
        out_specs=[row, pl.BlockSpec((1, LANES), lambda i: (0, 0)), pl.BlockSpec((1, D), lambda i: (0, 0))],
        out_shape=[jax.ShapeDtypeStruct((T, D), F32), jax.ShapeDtypeStruct((1, LANES), F32),
                   jax.ShapeDtypeStruct((1, D), F32)],
        compiler_params=_params(("arbitrary",)),
    )(x, target, g)


TOKEN_TILE = 512
TOKEN_TILE_WIDE = 1024
ATTN_TILE = 512
SCAN_CHUNK = 256
PROJ_DX_TILE = 256


def _local_step(x, target, plan, B, S):
    T, D = x.shape
    tm = min(TOKEN_TILE, T)
    tm_fwd = min(TOKEN_TILE_WIDE, T)
    tq = min(ATTN_TILE, S)
    nq = S // tq
    ch = min(SCAN_CHUNK, S)

    def riding(kernel_name, build):
        results, brought = build(plan.rider(kernel_name))
        plan.arrived(kernel_name, brought)
        return results

    hg1, hu1, n1 = plan.ffn1_up(x, tm_fwd)
    w1 = plan.weights("ffn1")
    x1, = riding("ffn1_down", lambda comm: _ffn_down("ffn1_down", x, hg1, hu1, w1["ffn1_down"], tm_fwd, comm))
    wm = plan.weights("mix_in")
    h, proj, flog = riding("mix_proj_fwd", lambda comm: _mix_proj_fwd(
        x1, wm["mix_norm"], wm["w_proj"], wm["w_f"], tm_fwd, PROJ_W // 4, comm))
    wm.update(plan.weights("mix_out"))
    cum = _fgate_fwd(flog, wm["b_forget"], B, S, ch)
    cum_t = jnp.transpose(cum[:, :N_HEADS].reshape(B, nq, tq, N_HEADS), (0, 1, 3, 2))
    o, lse = riding("attn_fwd", lambda comm: _attn_fwd(proj, cum, cum_t, B, S, tq, comm))
    yc = _conv_fwd(proj, wm["conv_w"], B, S)
    x2 = _mix_out_fwd(x1, o, yc, proj, wm["w_o_attn"], wm["w_o_conv"], wm["w_out"], tm)
    w2 = plan.weights("ffn2")
    x3, hg2, hu2, n2 = _ffn_fwd("ffn2_fwd", x2, w2["ffn2_norm"], w2["ffn2_gate"], w2["ffn2_up"], w2["ffn2_down"], tm_fwd)[0]
    dx3, loss, d_final_norm = _final_loss(x3, target, w2["final_norm"], tm)

    g = {"final_norm": d_final_norm}
    dx2, dhg2, dhu2, g["ffn2_norm"], df2 = _ffn_bwd_dx("ffn2_bwd_dx", dx3, x2, w2["ffn2_norm"], hg2, hu2,
                                                  w2["ffn2_gate"], w2["ffn2_up"], w2["ffn2_down"], tm_fwd)[0]
    plan.reduce("ffn2", dict(zip(("ffn2_gate", "ffn2_up", "ffn2_down"),
                                 _ffn_bwd_dw("ffn2_bwd_dw", n2, df2, hg2, hu2, dhg2, dhu2, tm)[0])))
    do, dyc, dgates, dwoa, dwoc, dwout = riding("mix_out_bwd", lambda comm: _mix_out_bwd(
        dx2, o, yc, proj, wm["w_o_attn"], wm["w_o_conv"], wm["w_out"], tm, comm))
    plan.reduce("out", dict(w_o_attn=_shard_cols(dwoa), w_o_conv=_shard_cols(dwoc), w_out=dwout.reshape(N_CHIPS, -1, D)))
    dq, dk, dv, dcq, dck = riding("attn_bwd", lambda comm: _attn_bwd(proj, o, do, lse, cum, cum_t, B, S, tq, comm))
    dcum = dcq + jnp.pad(jnp.transpose(dck, (0, 1, 3, 2)).reshape(T, N_HEADS), ((0, 0), (0, LANES - N_HEADS)))
    dflog, g["b_forget"] = _fgate_bwd(dcum, flog, wm["b_forget"], B, S, ch)
    dcb, dcc, dcx, g["conv_w"] = _conv_bwd(dyc, proj, wm["conv_w"], B, S)
    pieces = _proj_pieces(dq, dk, dv, dcb, dcc, dcx, dgates, dflog)
    dwq, dwk, dwv, dwcb, dwcc, dwcx = _matmuls_tn("mix_dw_a", pieces[:6], h, tm)
    dwga, dwgc, dwf = _matmuls_tn("mix_dw_b", pieces[6:], h, tm)
    dwin_t = jnp.concatenate([dwq, dwk, dwv, dwf[:N_HEADS], dwcb, dwcc, dwcx, dwga, dwgc], axis=0)
    plan.reduce("w_in", {"w_in": dwin_t.reshape(N_CHIPS, -1, D)})
    dx1, g["mix_norm"] = riding("mix_proj_bwd_dx", lambda comm: _mix_proj_bwd_dx(
        dx2, x1, wm["mix_norm"], pieces, wm["w_proj"], wm["w_f"], min(PROJ_DX_TILE, T), comm))
    grad_x, dhg1, dhu1, g["ffn1_norm"], df1 = _ffn_bwd_dx(
        "ffn1_bwd_dx", dx1, x, w1["ffn1_norm"], hg1, hu1, w1["ffn1_gate"], w1["ffn1_up"], w1["ffn1_down"], tm_fwd)[0]
    plan.reduce_small(g, loss)
    plan.reduce("ffn1", dict(zip(("ffn1_gate", "ffn1_up", "ffn1_down"), riding("ffn1_bwd_dw", lambda comm: _ffn_bwd_dw(
        "ffn1_bwd_dw", n1, df1, hg1, hu1, dhg1, dhu1, tm, comm)))))
    return loss, grad_x, g


TRANSPOSED = ("ffn1_gate", "ffn1_up", "ffn2_gate", "ffn2_up", "w_in")
NORMS = ("ffn1_norm", "mix_norm", "ffn2_norm", "final_norm")


def _unshard_cols(a):
    return jnp.transpose(a, (1, 0, 2)).reshape(a.shape[1], N_CHIPS * a.shape[2])


def _shard_cols(a):
    return jnp.transpose(a.reshape(a.shape[0], N_CHIPS, a.shape[1] // N_CHIPS), (1, 0, 2))


def _layout_ffn(which):
    def layout(st, small):
        w = {n: st[n] for n in (which + "_gate", which + "_up", which + "_down")}
        w[which + "_norm"] = small[which + "_norm"].reshape(1, -1)
        if which == "ffn2":
            w["final_norm"] = small["final_norm"].reshape(1, -1)
        return w
    return layout


def _layout_mix_in(st, small):
    win_t = st["w_in"].reshape(-1, st["w_in"].shape[2])
    return {
        "w_proj": jnp.concatenate([win_t[:N_FORGET_COL], win_t[N_FORGET_COL + N_HEADS:]], axis=0),
        "w_f": jnp.pad(win_t[N_FORGET_COL:N_FORGET_COL + N_HEADS], ((0, LANES - N_HEADS), (0, 0))),
        "conv_w": _unshard_cols(st["conv_w"]),
        "mix_norm": small["mix_norm"].reshape(1, -1),
        "b_forget": jnp.pad(small["b_forget"].reshape(1, -1), ((0, 0), (0, LANES - N_HEADS))),
    }


def _layout_mix_out(st, small):
    return {"w_o_attn": _unshard_cols(st["w_o_attn"]), "w_o_conv": _unshard_cols(st["w_o_conv"]),
            "w_out": st["w_out"].reshape(-1, st["w_out"].shape[2])}


_LAYOUTS = {"ffn1": _layout_ffn("ffn1"), "mix_in": _layout_mix_in, "mix_out": _layout_mix_out, "ffn2": _layout_ffn("ffn2")}


ANY = pl.BlockSpec(memory_space=pl.ANY)
BIG = ("ffn1_gate", "ffn1_up", "ffn1_down", "w_in", "w_o_attn", "w_o_conv", "w_out",
       "ffn2_gate", "ffn2_up", "ffn2_down")


def _place():
    x, y, c = lax.axis_index("x"), lax.axis_index("y"), lax.axis_index("c")
    others = [(1 - x, y), (x, 1 - y), (1 - x, 1 - y)]
    return x, y, c, others


def _col_halves(cols, c):
    hc = cols // 2
    return pl.ds(pl.multiple_of(c * hc, LANES), hc), pl.ds(pl.multiple_of((1 - c) * hc, LANES), hc)


def _gather_comm(shards, conv_shard=None):
    n = len(shards)
    inputs = list(shards) + ([] if conv_shard is None else [conv_shard])

    def copies(ins, outs, sems):
        send_sems, recv_sems, pass_send, pass_recv = sems[:4]
        x, y, c, others = _place()

        def chip_copy(a, j, chip):
            mine, _ = _col_halves(ins[a].shape[1], c)
            return pltpu.make_async_remote_copy(
                src_ref=ins[a].at[:, mine], dst_ref=outs[a].at[chip, :, mine],
                send_sem=send_sems.at[3 * a + j], recv_sem=recv_sems.at[3 * a + j],
                device_id=(*others[j], c), device_id_type=MESH)

        def pass_copy(a, j, chip, half):
            return pltpu.make_async_remote_copy(
                src_ref=outs[a].at[chip, :, half], dst_ref=outs[a].at[chip, :, half],
                send_sem=pass_send.at[3 * a + j], recv_sem=pass_recv.at[3 * a + j],
                device_id=(x, y, 1 - c), device_id_type=MESH)

        def conv_copy(j, chip):
            return pltpu.make_async_remote_copy(
                src_ref=ins[n], dst_ref=outs[n].at[chip],
                send_sem=sems[4].at[j], recv_sem=sems[5].at[j],
                device_id=(*others[j], c), device_id_type=MESH)

        me = 2 * x + y
        sends = [chip_copy(a, j, me) for a in range(n) for j in range(3)]
        if conv_shard is not None:
            sends += [conv_copy(j, me) for j in range(3)]
        return c, others, sends, chip_copy, pass_copy, conv_copy

    def start(ins, outs, sems):
        for cp in copies(ins, outs, sems)[2]:
            cp.start()

    def finish(ins, outs, sems):
        c, others, sends, chip_copy, pass_copy, conv_copy = copies(ins, outs, sems)
        passed = []
        for a in range(n):
            mine, _ = _col_halves(ins[a].shape[1], c)
            for j, (ox, oy) in enumerate(others):
                chip_copy(a, j, 2 * ox + oy).wait_recv()
                passed.append(pass_copy(a, j, 2 * ox + oy, mine))
                passed[-1].start()
        for a in range(n):
            _, theirs = _col_halves(ins[a].shape[1], c)
            for j, (ox, oy) in enumerate(others):
                pass_copy(a, j, 2 * ox + oy, theirs).wait_recv()
        if conv_shard is not None:
            for j, (ox, oy) in enumerate(others):
                conv_copy(j, 2 * ox + oy).wait_recv()
        for cp in sends + passed:
            cp.wait_send()

    scratch = [pltpu.SemaphoreType.DMA((3 * n,))] * 4
    if conv_shard is not None:
        scratch += [pltpu.SemaphoreType.DMA((3,))] * 2
    return _Comm(inputs, [jax.ShapeDtypeStruct((N_CHIPS,) + s.shape, s.dtype) for s in inputs], scratch, start, finish)


def _fill_own(stacks, shards):
    chip = 2 * lax.axis_index("x") + lax.axis_index("y")
    return [lax.dynamic_update_index_in_dim(st, s, chip, 0) for st, s in zip(stacks, shards)]


def _run_comm(name, comm):
    ci, co = len(comm.inputs), len(comm.out_shape)

    def body(*refs):
        comm.start(refs[:ci], refs[ci:ci + co], refs[ci + co:])
        comm.finish(refs[:ci], refs[ci:ci + co], refs[ci + co:])

    return pl.pallas_call(body, name=name, in_specs=[ANY] * ci, out_specs=[ANY] * co, out_shape=comm.out_shape,
                          scratch_shapes=comm.scratch)(*comm.inputs)


def _sibling_exchange_comm(grads):
    n = len(grads)

    def copies(ins, outs, sems):
        x, y, c, _ = _place()
        return [pltpu.make_async_remote_copy(
            src_ref=ins[a].at[:, :, _col_halves(ins[a].shape[2], c)[1]], dst_ref=outs[a],
            send_sem=sems[0].at[a], recv_sem=sems[1].at[a],
            device_id=(x, y, 1 - c), device_id_type=MESH) for a in range(n)]

    def start(ins, outs, sems):
        for cp in copies(ins, outs, sems):
            cp.start()

    def finish(ins, outs, sems):
        for cp in copies(ins, outs, sems):
            cp.wait()

    half = lambda s: jax.ShapeDtypeStruct((s.shape[0], s.shape[1], s.shape[2] // 2), s.dtype)
    return _Comm(grads, [half(s) for s in grads], [pltpu.SemaphoreType.DMA((n,))] * 2, start, finish)


def _merge_comms(comms):
    def split(refs, count):
        out, at = [], 0
        for cm in comms:
            out.append(refs[at:at + count(cm)])
            at += count(cm)
        return out

    def parts(ins, outs, sems):
        return zip(comms, split(ins, lambda cm: len(cm.inputs)), split(outs, lambda cm: len(cm.out_shape)),
                   split(sems, lambda cm: len(cm.scratch)))

    def start(ins, outs, sems):
        for cm, i, o, s in parts(ins, outs, sems):
            cm.start(i, o, s)

    def finish(ins, outs, sems):
        for cm, i, o, s in parts(ins, outs, sems):
            cm.finish(i, o, s)

    return _Comm(sum([cm.inputs for cm in comms], []), sum([cm.out_shape for cm in comms], []),
                 sum([cm.scratch for cm in comms], []), start, finish)


def _add_halves(name, grads, recvs, core):
    n = len(grads)

    def body(core_ref, *refs):
        for g_ref, r_ref, out_ref in zip(refs[:n], refs[n:2 * n], refs[2 * n:]):
            out_ref[...] = (g_ref[...].astype(F32) + r_ref[...].astype(F32)).astype(BF16)

    half = lambda g: pl.BlockSpec((None, g.shape[1], g.shape[2] // 2), lambda k, core_ref: (k, 0, 0))
    mine = lambda g: pl.BlockSpec((None, g.shape[1], g.shape[2] // 2), lambda k, core_ref: (k, 0, core_ref[0]))
    return pl.pallas_call(
        body, name=name,
        grid_spec=pltpu.PrefetchScalarGridSpec(
            num_scalar_prefetch=1, grid=(N_CHIPS,),
            in_specs=[mine(g) for g in grads] + [half(g) for g in grads],
            out_specs=[half(g) for g in grads]),
        out_shape=[jax.ShapeDtypeStruct(r.shape, BF16) for r in recvs],
        compiler_params=_params(("arbitrary",)),
    )(core, *grads, *recvs)


def _chip_exchange_comm(parts):
    n = len(parts)

    def copies(ins, outs, sems):
        x, y, c, others = _place()
        return [pltpu.make_async_remote_copy(
            src_ref=ins[a].at[2 * ox + oy], dst_ref=outs[a].at[j],
            send_sem=sems[0].at[3 * a + j], recv_sem=sems[1].at[3 * a + j],
            device_id=(ox, oy, c), device_id_type=MESH) for a in range(n) for j, (ox, oy) in enumerate(others)]

    def start(ins, outs, sems):
        for cp in copies(ins, outs, sems):
            cp.start()

    def finish(ins, outs, sems):
        for cp in copies(ins, outs, sems):
            cp.wait()

    return _Comm(parts, [jax.ShapeDtypeStruct((3,) + s.shape[1:], s.dtype) for s in parts],
                 [pltpu.SemaphoreType.DMA((3 * n,))] * 2, start, finish)


HBM = pl.BlockSpec(memory_space=pltpu.HBM)
SEM = pl.BlockSpec(memory_space=pltpu.SEMAPHORE)


def _split_exchange_copies(parts, lands, send_sems, recv_sems):
    x, y, c, others = _place()
    return [pltpu.make_async_remote_copy(
        src_ref=parts[a].at[2 * ox + oy], dst_ref=lands[a].at[j],
        send_sem=send_sems.at[3 * a + j], recv_sem=recv_sems.at[3 * a + j],
        device_id=(ox, oy, c), device_id_type=MESH) for a in range(len(parts)) for j, (ox, oy) in enumerate(others)]


def _exchange_start(name, parts):
    n = len(parts)

    def body(*refs):
        ins, lands = refs[:n], refs[n:2 * n]
        send_sems, recv_sems, token = refs[2 * n], refs[2 * n + 1], refs[-1]
        for cp in _split_exchange_copies(ins, lands, send_sems, recv_sems):
            cp.start()
        token[...] = jnp.zeros_like(token)

    land_shape = [(3,) + p.shape[1:] for p in parts]
    outs = pl.pallas_call(
        body, name=name,
        out_shape=[pltpu.SemaphoreType.DMA((3 * n,)), pltpu.SemaphoreType.DMA((3 * n,))]
        + [pltpu.HBM(p.shape, p.dtype) for p in parts] + [pltpu.HBM(s, p.dtype) for s, p in zip(land_shape, parts)]
        + [jax.ShapeDtypeStruct((8, LANES), F32)],
        in_specs=[HBM] * (2 * n), out_specs=[SEM, SEM] + [HBM] * (2 * n) + [pl.BlockSpec(memory_space=pltpu.VMEM)],
        input_output_aliases={i: 2 + i for i in range(2 * n)},
        compiler_params=pltpu.CompilerParams(has_side_effects=pltpu.SideEffectType.DATAFLOW_SIDE_EFFECTING),
    )(*[pltpu.with_memory_space_constraint(p, pltpu.HBM) for p in parts],
      *[pltpu.with_memory_space_constraint(lax.empty(s, p.dtype), pltpu.HBM) for s, p in zip(land_shape, parts)])
    return outs[0], outs[1], list(outs[2:2 + n]), list(outs[2 + n:2 + 2 * n]), outs[-1]


def _exchange_wait(name, send_sems, recv_sems, parts, lands, after):
    n = len(parts)

    def body(*refs):
        ins, zones = refs[:n], refs[n:2 * n]
        for cp in _split_exchange_copies(ins, zones, refs[2 * n], refs[2 * n + 1]):
            cp.wait_send()
            cp.wait_recv()

    outs = pl.pallas_call(
        body, name=name,
        out_shape=[pltpu.HBM(p.shape, p.dtype) for p in parts] + [pltpu.HBM(z.shape, z.dtype) for z in lands],
        in_specs=[HBM] * (2 * n) + [SEM, SEM] + [ANY] * len(after), out_specs=[HBM] * (2 * n),
        input_output_aliases={i: i for i in range(2 * n)},
        compiler_params=pltpu.CompilerParams(has_side_effects=pltpu.SideEffectType.DATAFLOW_SIDE_EFFECTING),
    )(*parts, *lands, send_sems, recv_sems, *after)
    return list(outs[:n]), list(outs[n:])


def _sum_chips(name, owns, recvs, chip, after):
    n = len(owns)
    hc = owns[0].shape[2]
    assert all(o.shape[2] == hc for o in owns)

    def body(chip_ref, *refs):
        for own_ref, recv_ref, out_ref in zip(refs[:n], refs[n:2 * n], refs[2 * n + 1:]):
            acc = own_ref[...].astype(F32)
            for j in range(3):
                acc = acc + recv_ref[j].astype(F32)
            out_ref[...] = acc

    return pl.pallas_call(
        body, name=name,
        grid_spec=pltpu.PrefetchScalarGridSpec(
            num_scalar_prefetch=1, grid=(hc // LANES,),
            in_specs=[pl.BlockSpec((None, o.shape[1], LANES), lambda i, chip_ref: (chip_ref[0], 0, i)) for o in owns]
            + [pl.BlockSpec((3, o.shape[1], LANES), lambda i, chip_ref: (0, 0, i)) for o in owns]
            + [pl.BlockSpec((8, LANES), lambda i, chip_ref: (0, 0))],
            out_specs=[pl.BlockSpec((o.shape[1], LANES), lambda i, chip_ref: (0, i)) for o in owns]),
        out_shape=[jax.ShapeDtypeStruct((o.shape[1], hc), F32) for o in owns],
        compiler_params=_params(("arbitrary",)),
    )(chip, *owns, *recvs, after)


def _share_halves(name, halves):
    n = len(halves)

    def body(*refs):
        srcs, dsts = refs[:n], refs[n:2 * n]
        send_sems, recv_sems = refs[2 * n:]
        x, y, c, _ = _place()
        copies = [pltpu.make_async_remote_copy(
            src_ref=srcs[a], dst_ref=dsts[a], send_sem=send_sems.at[a], recv_sem=recv_sems.at[a],
            device_id=(x, y, 1 - c), device_id_type=MESH) for a in range(n)]
        for cp in copies:
            cp.start()
        for cp in copies:
            cp.wait()

    return pl.pallas_call(
        body, name=name,
        in_specs=[ANY] * n, out_specs=[ANY] * n,
        out_shape=[jax.ShapeDtypeStruct(s.shape, s.dtype) for s in halves],
        scratch_shapes=[pltpu.SemaphoreType.DMA((n,)), pltpu.SemaphoreType.DMA((n,))],
    )(*halves)


def _small_gather_comm(part):
    def copies(ins, outs, sems):
        x, y, c, _ = _place()
        me = 4 * x + 2 * y + c
        both = []
        for d in range(1, N_DEV):
            px, py, pc = (1 - x if d & 4 else x, 1 - y if d & 2 else y, 1 - c if d & 1 else c)
            send = pltpu.make_async_remote_copy(
                src_ref=ins[0], dst_ref=outs[0].at[me], send_sem=sems[0].at[d - 1], recv_sem=sems[1].at[d - 1],
                device_id=(px, py, pc), device_id_type=MESH)
            recv = pltpu.make_async_remote_copy(
                src_ref=ins[0], dst_ref=outs[0].at[4 * px + 2 * py + pc], send_sem=sems[0].at[d - 1],
                recv_sem=sems[1].at[d - 1], device_id=(px, py, pc), device_id_type=MESH)
            both.append((send, recv))
        return both

    def start(ins, outs, sems):
        for send, _ in copies(ins, outs, sems):
            send.start()

    def finish(ins, outs, sems):
        for send, recv in copies(ins, outs, sems):
            recv.wait_recv()
            send.wait_send()

    return _Comm([part], [jax.ShapeDtypeStruct((N_DEV,) + part.shape, F32)],
                 [pltpu.SemaphoreType.DMA((N_DEV - 1,))] * 2, start, finish)


def _sum_devices(parts):
    def body(p_ref, out_ref):
        acc = p_ref[0]
        for k in range(1, N_DEV):
            acc = acc + p_ref[k]
        out_ref[...] = acc

    return pl.pallas_call(
        body, name="sum_devices", grid=(1,),
        in_specs=[pl.BlockSpec(parts.shape, lambda i: (0, 0, 0))],
        out_specs=pl.BlockSpec(parts.shape[1:], lambda i: (0, 0)),
        out_shape=jax.ShapeDtypeStruct(parts.shape[1:], F32),
        compiler_params=_params(("arbitrary",)),
    )(parts)


def _adam_update(w, g, m, v):
    nm = ADAM_B1 * m + (1.0 - ADAM_B1) * g
    nv = ADAM_B2 * v + (1.0 - ADAM_B2) * (g * g)
    m_hat = nm * (1.0 / (1.0 - ADAM_B1 ** ADAM_STEP))
    v_hat = nv * (1.0 / (1.0 - ADAM_B2 ** ADAM_STEP))
    return -ADAM_LR * (m_hat / (jnp.sqrt(v_hat) + ADAM_EPS) + ADAM_WD * w), nm, nv


def _adamw(name, w, g, m, v):
    def body(w_ref, g_ref, m_ref, v_ref, d_ref, nm_ref, nv_ref):
        d_ref[...], nm_ref[...], nv_ref[...] = _adam_update(w_ref[...], g_ref[...], m_ref[...], v_ref[...])

    spec = pl.BlockSpec(w.shape, lambda i: (0, 0))
    out = jax.ShapeDtypeStruct(w.shape, F32)
    return pl.pallas_call(
        body, name=name, grid=(1,),
        in_specs=[spec] * 4, out_specs=[spec] * 3, out_shape=[out] * 3,
        compiler_params=_params(("arbitrary",)),
    )(w, g, m, v)


def _adamw_halves(name, ws, mines, theirs, ms, vs, core):
    n = len(ws)
    cols = ws[0].shape[1]
    assert all(w.shape[1] == cols for w in ws)
    hc = cols // 2
    tc = LANES if n > 1 else min(256, hc)
    nt = hc // tc

    def body(core_ref, *refs):
        ins, outs = refs[:5 * n], refs[5 * n:]
        for a in range(n):
            w_ref, mine_ref, theirs_ref, m_ref, v_ref = [ins[j * n + a] for j in range(5)]
            g_ref, d_ref, nm_ref, nv_ref = outs[4 * a:4 * a + 4]
            gv = jnp.where(pl.program_id(0) == core_ref[0], mine_ref[...], theirs_ref[...])
            g_ref[...] = gv
            d_ref[...], nm_ref[...], nv_ref[...] = _adam_update(w_ref[...], gv, m_ref[...], v_ref[...])

    whole = lambda w: pl.BlockSpec((w.shape[0], tc), lambda h, i, core_ref: (0, h * nt + i))
    mine_spec = lambda w: pl.BlockSpec((w.shape[0], tc), lambda h, i, core_ref: (0, jnp.where(h == core_ref[0], i, 0)))
    theirs_spec = lambda w: pl.BlockSpec((w.shape[0], tc), lambda h, i, core_ref: (0, jnp.where(h == core_ref[0], 0, i)))
    outs = pl.pallas_call(
        body, name=name,
        grid_spec=pltpu.PrefetchScalarGridSpec(
            num_scalar_prefetch=1, grid=(2, nt),
            in_specs=[whole(w) for w in ws] + [mine_spec(w) for w in ws] + [theirs_spec(w) for w in ws]
            + [whole(w) for w in ws] * 2,
            out_specs=[whole(w) for w in ws for _ in range(4)]),
        out_shape=[jax.ShapeDtypeStruct(w.shape, F32) for w in ws for _ in range(4)],
        compiler_params=_params(("arbitrary", "arbitrary")),
    )(core, *ws, *mines, *theirs, *ms, *vs)
    return [outs[4 * a:4 * a + 4] for a in range(n)]


WEIGHTS = ("ffn1_norm", "ffn1_gate", "ffn1_up", "ffn1_down", "mix_norm", "w_in", "b_forget", "conv_w",
           "w_o_attn", "w_o_conv", "w_out", "ffn2_norm", "ffn2_gate", "ffn2_up", "ffn2_down", "final_norm")
VEC_ROWS = 8


def _pack_small(t, conv_rows):
    conv = t["conv_w"]
    parts = [t[n].reshape(VEC_ROWS, LANES) for n in NORMS]
    parts.append(jnp.pad(conv, ((0, conv_rows - conv.shape[0]), (0, 0))))
    parts.append(jnp.pad(t["b_forget"].reshape(1, N_HEADS), ((0, 7), (0, LANES - N_HEADS))))
    return jnp.concatenate(parts, axis=0)


def _unpack_small(p, conv_rows):
    out = {n: p[VEC_ROWS * i:VEC_ROWS * (i + 1)].reshape(-1) for i, n in enumerate(NORMS)}
    base = VEC_ROWS * len(NORMS)
    out["conv_w"] = p[base:base + 3]
    out["b_forget"] = p[base + conv_rows, :N_HEADS]
    return out


def _travel(name, a):
    return a.T if name in TRANSPOSED else a


GATHER_FIRST = ("ffn1_gate", "ffn1_up")
GATHER_RIDES = {"ffn1_up": ("ffn1_down",), "ffn1_down": ("w_in",), "mix_proj_fwd": ("w_o_attn", "w_o_conv", "w_out"),
                "attn_fwd": ("ffn2_gate", "ffn2_up", "ffn2_down")}
SIBLING_RIDES = {"ffn2": "mix_out_bwd", "out": None, "w_in": "mix_proj_bwd_dx", "ffn1": None}
CHIP_RIDES = {"ffn2": "attn_bwd", "out": "attn_bwd", "w_in": "ffn1_bwd_dw", "ffn1": None}
SMALL_RIDE = "ffn1_bwd_dw"


class _MeshPlan:
    def __init__(self, wts, core):
        self.small, self.core = wts, core
        self.shards = {n: wts[n].astype(BF16) for n in BIG}
        self.chip_part, self.from_chips, self.rides = {}, {}, {}
        self.stacks = {}
        conv_shard = jnp.pad(wts["conv_w"], ((0, 8 - wts["conv_w"].shape[0]), (0, 0)))
        for kernel_name, names in GATHER_RIDES.items():
            mine = [self.shards[n] for n in names]
            conv = conv_shard if kernel_name == "ffn1_up" else None
            names = names + (("conv_w",) if conv is not None else ())
            mine = mine + ([conv] if conv is not None else [])
            self._ride(kernel_name, _gather_comm(mine[:len(mine) - (conv is not None)], conv),
                       lambda got, names=names, mine=mine: self.stacks.update(zip(names, _fill_own(got, mine))))

    def weights(self, group):
        return _LAYOUTS[group](self.stacks, self.small)

    def ffn1_up(self, x, tm):
        px, py = lax.axis_index("x"), lax.axis_index("y")
        order = jnp.stack([2 * px + py, 2 * (1 - px) + py, 2 * px + (1 - py), 2 * (1 - px) + (1 - py)]).astype(jnp.int32)
        own = [self.shards[n] for n in GATHER_FIRST]
        (hg, hu, n, sg, su), brought = _ffn_up_gather("ffn1_up", x, self.small["ffn1_norm"].reshape(1, -1), *own, order,
                                                     tm, self.rider("ffn1_up"))
        self.stacks.update(zip(GATHER_FIRST, _fill_own([sg, su], own)))
        self.arrived("ffn1_up", brought)
        return hg, hu, n

    def _ride(self, kernel_name, comm, then):
        self.rides.setdefault(kernel_name, []).append((comm, then))

    def rider(self, kernel_name):
        comms = [comm for comm, _ in self.rides.get(kernel_name, [])]
        return _merge_comms(comms) if comms else None

    def arrived(self, kernel_name, results):
        for comm, then in self.rides.pop(kernel_name, []):
            then(results[:len(comm.out_shape)])
            results = results[len(comm.out_shape):]

    def reduce(self, group, grads):
        names = tuple(grads)
        mine = [grads[n] for n in names]

        def with_sibling(from_sibling):
            parts = _add_halves("add_halves_" + group, mine, list(from_sibling), self.core)
            self.chip_part.update(zip(names, parts))
            if CHIP_RIDES[group] is None:
                self.last = (names, _exchange_start("exchange_start_" + group, parts))
            else:
                self._ride(CHIP_RIDES[group], _chip_exchange_comm(parts),
                           lambda got: self.from_chips.update(zip(names, got)))

        if SIBLING_RIDES[group] is None:
            with_sibling(_run_comm("sibling_exchange_" + group, _sibling_exchange_comm(mine)))
        else:
            self._ride(SIBLING_RIDES[group], _sibling_exchange_comm(mine), with_sibling)

    def reduce_small(self, gs, loss):
        conv_all = _shard_cols(gs["conv_w"]).reshape(N_CHIPS * 8, LANES)
        part = _pack_small({**{n: gs[n] for n in NORMS}, "conv_w": conv_all, "b_forget": gs["b_forget"][0, :N_HEADS]},
                           N_CHIPS * 8)
        part = jnp.concatenate([part, jnp.broadcast_to(loss, (8, LANES))], axis=0)
        me = 4 * lax.axis_index("x") + 2 * lax.axis_index("y") + lax.axis_index("c")

        def landed(got):
            self.small_parts = lax.dynamic_update_index_in_dim(got[0], part, me, 0)

        self._ride(SMALL_RIDE, _small_gather_comm(part), landed)


def kernel(x, ffn1_norm, ffn1_gate, ffn1_up, ffn1_down, mix_norm, w_in, b_forget, conv_w, w_o_attn, w_o_conv, w_out, ffn2_norm, ffn2_gate, ffn2_up, ffn2_down, final_norm, loss_target, m_ffn1_norm, m_ffn1_gate, m_ffn1_up, m_ffn1_down, m_mix_norm, m_w_in, m_b_forget, m_conv_w, m_w_o_attn, m_w_o_conv, m_w_out, m_ffn2_norm, m_ffn2_gate, m_ffn2_up, m_ffn2_down, m_final_norm, v_ffn1_norm, v_ffn1_gate, v_ffn1_up, v_ffn1_down, v_mix_norm, v_w_in, v_b_forget, v_conv_w, v_w_o_attn, v_w_o_conv, v_w_out, v_ffn2_norm, v_ffn2_gate, v_ffn2_up, v_ffn2_down, v_final_norm):
    given = dict(locals())
    wts = {n: _travel(n, given[n]) for n in WEIGHTS}
    mom = {n: _travel(n, given["m_" + n]) for n in WEIGHTS}
    var = {n: _travel(n, given["v_" + n]) for n in WEIGHTS}
    B, S, D = x.shape
    chip = 2 * lax.axis_index("x") + lax.axis_index("y")
    chip1 = chip.astype(jnp.int32).reshape(1)
    core = lax.axis_index("c").astype(jnp.int32).reshape(1)

    plan = _MeshPlan(wts, core)
    loss, grad_x, gs = _local_step(x.reshape(B * S, D), loss_target.reshape(B * S, D), plan, B, S)

    last_names, (send_sems, recv_sems, parts_thru, lands, token) = plan.last
    delta, new_m, new_v, grads = {}, {}, {}, {}

    def finish(tag, names):
        by_cols = {}
        for n in names:
            by_cols.setdefault(wts[n].shape[1], []).append(n)
        mine = {}
        for cols, ns in by_cols.items():
            mine.update(zip(ns, _sum_chips("sum_chips_%s_%d" % (tag, cols), [plan.chip_part[n] for n in ns],
                                           [plan.from_chips[n] for n in ns], chip1, token)))
        theirs = dict(zip(names, _share_halves("share_halves_" + tag, [mine[n] for n in names])))
        raw = []
        for cols, ns in by_cols.items():
            outs = _adamw_halves("adamw_%s_%d" % (tag, cols), [wts[n] for n in ns], [mine[n] for n in ns],
                                 [theirs[n] for n in ns], [mom[n] for n in ns], [var[n] for n in ns], core)
            for n, per in zip(ns, outs):
                raw.append(per[-1])
                grads[n], delta[n], new_m[n], new_v[n] = [_travel(n, o) for o in per]
        return raw

    small_sum = _sum_devices(plan.small_parts)
    base = VEC_ROWS * len(NORMS)
    loss_row = small_sum.shape[0] - 8
    small_grads = _unpack_small(small_sum, N_CHIPS * 8)
    small_grads["conv_w"] = lax.dynamic_slice_in_dim(small_sum[base:base + N_CHIPS * 8], chip * 8, 8, axis=0)[:3]
    packs = [_pack_small(t, 8) for t in (wts, small_grads, mom, var)]
    small_out = _adamw("adamw_small", *packs)

    done = finish("early", [n for n in BIG if n not in last_names])
    parts_back, got = _exchange_wait("exchange_wait", send_sems, recv_sems, parts_thru, lands, done + list(small_out))
    plan.chip_part.update(zip(last_names, parts_back))
    plan.from_chips.update(zip(last_names, got))
    finish("last", last_names)
    grads.update(small_grads)
    for out, p in zip((delta, new_m, new_v), small_out):
        out.update(_unpack_small(p, 8))

    return (small_sum[loss_row, 0], grad_x.reshape(B, S, D), *[grads[n] for n in WEIGHTS], *[delta[n] for n in WEIGHTS],
            *[new_m[n] for n in WEIGHTS], *[new_v[n] for n in WEIGHTS])
```

```python
import functools
import math

import jax
import jax.numpy as jnp
from jax import lax
from jax.experimental import pallas as pl
from jax.experimental.pallas import tpu as pltpu

F32 = jnp.float32
BF16 = jnp.bfloat16
MESH = pl.DeviceIdType.MESH

N_CHIPS = 4
N_DEV = 8
N_HEADS = 8
HEAD_DIM = 64
HEAD_PAIRS = N_HEADS // 2
ATTN_W = N_HEADS * HEAD_DIM
CONV_W = 512
RMS_EPS = 1e-6
FFN_RES = 0.5
LANES = 128
VMEM_LIMIT = 56 * 1024 * 1024
ROW_BLOCK = 256

ADAM_LR = 0.001
ADAM_B1 = 0.9
ADAM_B2 = 0.999
ADAM_EPS = 1e-08
ADAM_WD = 0.01
ADAM_STEP = 10

PROJ_W = 3 * ATTN_W + 3 * CONV_W + 2 * 1024
COL_CB, COL_CC, COL_CX = 3 * ATTN_W, 3 * ATTN_W + CONV_W, 3 * ATTN_W + 2 * CONV_W
COL_GATES = 3 * ATTN_W + 3 * CONV_W
N_FORGET_COL = 3 * ATTN_W


def _params(sem=None, vmem=VMEM_LIMIT):
    return pltpu.CompilerParams(dimension_semantics=sem, vmem_limit_bytes=vmem)


def _dot(a, b):
    return lax.dot_general(a, b, (((1,), (0,)), ((), ())), preferred_element_type=F32)


def _dot_nt(a, b):
    return lax.dot_general(a, b, (((1,), (1,)), ((), ())), preferred_element_type=F32)


def _dot_tn(a, b):
    return lax.dot_general(a, b, (((0,), (0,)), ((), ())), preferred_element_type=F32)


def _sigmoid(x):
    return 1.0 / (1.0 + jnp.exp(-x))


def _rms(xv):
    inv = lax.rsqrt(jnp.mean(xv * xv, axis=-1, keepdims=True) + RMS_EPS)
    return xv * inv, inv


class _Comm:
    def __init__(self, inputs, out_shape, scratch, start, finish):
        self.inputs, self.out_shape, self.scratch = list(inputs), list(out_shape), list(scratch)
        self.start, self.finish = start, finish


def _pallas(body, name, grid, in_specs, out_specs, out_shape, scratch, args, comm=None):
    sem = ("arbitrary",) * len(grid)
    if comm is None:
        outs = pl.pallas_call(body, name=name, grid=grid, in_specs=in_specs, out_specs=out_specs,
                              out_shape=out_shape, scratch_shapes=scratch, compiler_params=_params(sem))(*args)
        return list(outs), []
    n_in, n_out, n_scr = len(in_specs), len(out_specs), len(scratch)
    ci, co = len(comm.inputs), len(comm.out_shape)

    def riding(*refs):
        ins, refs = refs[:n_in], refs[n_in:]
        cins, refs = refs[:ci], refs[ci:]
        outs, refs = refs[:n_out], refs[n_out:]
        couts, refs = refs[:co], refs[co:]
        scr, sems = refs[:n_scr], refs[n_scr:]
        ids = [pl.program_id(d) for d in range(len(grid))]
        first = functools.reduce(lambda a, b: a & b, [i == 0 for i in ids])
        last = functools.reduce(lambda a, b: a & b, [i == g - 1 for i, g in zip(ids, grid)])

        @pl.when(first)
        def _():
            comm.start(cins, couts, sems)

        body(*ins, *outs, *scr)

        @pl.when(last)
        def _():
            comm.finish(cins, couts, sems)

    any_spec = pl.BlockSpec(memory_space=pl.ANY)
    outs = pl.pallas_call(
        riding, name=name, grid=grid,
        in_specs=list(in_specs) + [any_spec] * ci, out_specs=list(out_specs) + [any_spec] * co,
        out_shape=list(out_shape) + comm.out_shape, scratch_shapes=list(scratch) + comm.scratch,
        compiler_params=_params(sem))(*args, *comm.inputs)
    return list(outs[:n_out]), list(outs[n_out:])


def _rms_bwd(dn, xhat, inv, g):
    dxhat = dn * g
    dx = inv * (dxhat - xhat * jnp.mean(dxhat * xhat, axis=-1, keepdims=True))
    return dx, jnp.sum(dn * xhat, axis=0, keepdims=True)


def _ffn_fwd_loss(name, x, g, wgt, wut, wd, target, gf, tm):
    T, D = x.shape
    K, Fs, _ = wgt.shape

    def body(x_ref, g_ref, wg_ref, wu_ref, wd_ref, t_ref, gf_ref,
             dx_ref, hg_ref, hu_ref, n_ref, loss_ref, dgf_ref, acc_scr):
        i, k = pl.program_id(0), pl.program_id(1)

        @pl.when(k == 0)
        def _():
            xhat, _ = _rms(x_ref[...])
            n_ref[...] = (xhat * g_ref[...]).astype(BF16)
            acc_scr[...] = jnp.zeros_like(acc_scr)

        @pl.when((k == 0) & (i == 0))
        def _():
            loss_ref[...] = jnp.zeros_like(loss_ref)
            dgf_ref[...] = jnp.zeros_like(dgf_ref)

        n = n_ref[...]
        hg = _dot_nt(n, wg_ref[...])
        hu = _dot_nt(n, wu_ref[...])
        hg_ref[...] = hg.astype(BF16)
        hu_ref[...] = hu.astype(BF16)
        act = (hg * _sigmoid(hg) * hu).astype(BF16)
        acc_scr[...] += _dot(act, wd_ref[...])

        @pl.when(k == K - 1)
        def _():
            gfv = gf_ref[...]
            for r0 in range(0, tm, ROW_BLOCK):
                rows = slice(r0, r0 + ROW_BLOCK)
                xhat, inv = _rms(x_ref[rows, :] + FFN_RES * acc_scr[rows, :])
                err = xhat * gfv - t_ref[rows, :]
                loss_ref[...] += 0.5 * jnp.sum(jnp.sum(err * err, axis=1, keepdims=True), axis=0, keepdims=True) / D
                dx, dg = _rms_bwd(err * (1.0 / D), xhat, inv, gfv)
                dx_ref[rows, :] = dx
                dgf_ref[...] += dg

    w_spec = pl.BlockSpec((None, Fs, D), lambda i, k: (k, 0, 0))
    act_spec = pl.BlockSpec((None, tm, Fs), lambda i, k: (k, i, 0))
    row = pl.BlockSpec((tm, D), lambda i, k: (i, 0))
    vec = pl.BlockSpec((1, D), lambda i, k: (0, 0))
    return _pallas(
        body, name, (T // tm, K),
        [row, vec, w_spec, w_spec, w_spec, row, vec],
        [row, act_spec, act_spec, row, pl.BlockSpec((1, LANES), lambda i, k: (0, 0)), vec],
        [jax.ShapeDtypeStruct((T, D), F32), jax.ShapeDtypeStruct((K, T, Fs), BF16),
         jax.ShapeDtypeStruct((K, T, Fs), BF16), jax.ShapeDtypeStruct((T, D), BF16),
         jax.ShapeDtypeStruct((1, LANES), F32), jax.ShapeDtypeStruct((1, D), F32)],
        [pltpu.VMEM((tm, D), F32)],
        (x, g, wgt, wut, wd, target, gf))[0]


def _ffn_up_gather(name, x, g, wg_own, wu_own, order, tm, comm=None):
    T, D = x.shape
    Fs = wg_own.shape[0]
    nt = T // tm
    ci, co = (len(comm.inputs), len(comm.out_shape)) if comm is not None else (0, 0)

    def body(order_ref, x_ref, g_ref, wgo_ref, wuo_ref, *rest):
        cins, rest = rest[:ci], rest[ci:]
        (hg_ref, hu_ref, n_ref, sg_ref, su_ref), rest = rest[:5], rest[5:]
        couts, rest = rest[:co], rest[co:]
        (n_all, wbuf, send_sems, recv_sems, pass_send, pass_recv, load_sems), csems = rest[:7], rest[7:]
        k, i = pl.program_id(0), pl.program_id(1)
        x_pos, y_pos, c, others = _place()
        me = 2 * x_pos + y_pos
        owns, stacks = (wgo_ref, wuo_ref), (sg_ref, su_ref)
        mine, theirs = _col_halves(D, c)

        def chip_copy(a, j, chip):
            return pltpu.make_async_remote_copy(
                src_ref=owns[a].at[:, mine], dst_ref=stacks[a].at[chip, :, mine],
                send_sem=send_sems.at[3 * a + j], recv_sem=recv_sems.at[3 * a + j],
                device_id=(*others[j], c), device_id_type=MESH)

        def pass_copy(a, j, chip, half):
            return pltpu.make_async_remote_copy(
                src_ref=stacks[a].at[chip, :, half], dst_ref=stacks[a].at[chip, :, half],
                send_sem=pass_send.at[3 * a + j], recv_sem=pass_recv.at[3 * a + j],
                device_id=(x_pos, y_pos, 1 - c), device_id_type=MESH)

        @pl.when((k == 0) & (i == 0))
        def _():
            for a in range(2):
                for j in range(3):
                    chip_copy(a, j, me).start()
            if comm is not None:
                comm.start(cins, couts, csems)

        def bring(j):
            ox, oy = others[j]
            chip = 2 * ox + oy
            for a in range(2):
                chip_copy(a, j, chip).wait_recv()
            for a in range(2):
                pass_copy(a, j, chip, mine).start()
            for a in range(2):
                pass_copy(a, j, chip, theirs).wait_recv()
            loads = [pltpu.make_async_copy(stacks[a].at[chip], wbuf.at[j % 2, a], load_sems.at[2 * (j % 2) + a])
                     for a in range(2)]
            for cp in loads:
                cp.start()
            for cp in loads:
                cp.wait()

        @pl.when((k == 1) & (i == 0))
        def _():
            bring(0)
            bring(1)

        @pl.when((k == 2) & (i == nt - 1))
        def _():
            bring(2)

        rows = pl.ds(pl.multiple_of(i * tm, tm), tm)

        @pl.when(k == 0)
        def _():
            xhat, _ = _rms(x_ref[...])
            n = (xhat * g_ref[...]).astype(BF16)
            n_ref[...] = n
            n_all[rows, :] = n
            hg_ref[...] = _dot_nt(n, wgo_ref[...]).astype(BF16)
            hu_ref[...] = _dot_nt(n, wuo_ref[...]).astype(BF16)

        @pl.when(k > 0)
        def _():
            n = n_all[rows, :]
            slot = (k - 1) % 2
            hg_ref[...] = _dot_nt(n, wbuf[slot, 0]).astype(BF16)
            hu_ref[...] = _dot_nt(n, wbuf[slot, 1]).astype(BF16)

        @pl.when((k == N_CHIPS - 1) & (i == nt - 1))
        def _():
            for a in range(2):
                for j, (ox, oy) in enumerate(others):
                    chip_copy(a, j, me).wait_send()
                    pass_copy(a, j, 2 * ox + oy, mine).wait_send()
            if comm is not None:
                comm.finish(cins, couts, csems)

    any_spec = pl.BlockSpec(memory_space=pl.ANY)
    first_pass = lambda k, i, order_ref: (jnp.where(k == 0, i, nt - 1), 0)
    whole = pl.BlockSpec((Fs, D), lambda k, i, order_ref: (0, 0))
    act_spec = pl.BlockSpec((None, tm, Fs), lambda k, i, order_ref: (order_ref[k], i, 0))
    stack = jax.ShapeDtypeStruct((N_CHIPS, Fs, D), BF16)
    outs = pl.pallas_call(
        body, name=name,
        grid_spec=pltpu.PrefetchScalarGridSpec(
            num_scalar_prefetch=1, grid=(N_CHIPS, nt),
            in_specs=[pl.BlockSpec((tm, D), first_pass), pl.BlockSpec((1, D), lambda k, i, order_ref: (0, 0)),
                      whole, whole] + [any_spec] * ci,
            out_specs=[act_spec, act_spec, pl.BlockSpec((tm, D), first_pass), any_spec, any_spec] + [any_spec] * co,
            scratch_shapes=[pltpu.VMEM((T, D), BF16), pltpu.VMEM((2, 2, Fs, D), BF16)]
            + [pltpu.SemaphoreType.DMA((6,))] * 4 + [pltpu.SemaphoreType.DMA((4,))]
            + (comm.scratch if comm is not None else [])),
        out_shape=[jax.ShapeDtypeStruct((N_CHIPS, T, Fs), BF16), jax.ShapeDtypeStruct((N_CHIPS, T, Fs), BF16),
                   jax.ShapeDtypeStruct((T, D), BF16), stack, stack] + (comm.out_shape if comm is not None else []),
        compiler_params=_params(("arbitrary", "arbitrary")),
    )(order, x, g, wg_own, wu_own, *(comm.inputs if comm is not None else []))
    return list(outs[:5]), list(outs[5:])


def _ffn_down(name, x, hg, hu, wd, tm, comm=None):
    T, D = x.shape
    K, Fs, _ = wd.shape

    def body(x_ref, hg_ref, hu_ref, wd_ref, out_ref, acc_scr):
        k = pl.program_id(1)

        @pl.when(k == 0)
        def _():
            acc_scr[...] = jnp.zeros_like(acc_scr)

        hgv = hg_ref[...].astype(F32)
        act = (hgv * _sigmoid(hgv) * hu_ref[...].astype(F32)).astype(BF16)
        acc_scr[...] += _dot(act, wd_ref[...])

        @pl.when(k == K - 1)
        def _():
            out_ref[...] = x_ref[...] + FFN_RES * acc_scr[...]

    act_spec = pl.BlockSpec((None, tm, Fs), lambda i, k: (k, i, 0))
    row = pl.BlockSpec((tm, D), lambda i, k: (i, 0))
    return _pallas(
        body, name, (T // tm, K),
        [row, act_spec, act_spec, pl.BlockSpec((None, Fs, D), lambda i, k: (k, 0, 0))],
        [row], [jax.ShapeDtypeStruct((T, D), F32)], [pltpu.VMEM((tm, D), F32)],
        (x, hg, hu, wd), comm)


def _ffn_bwd_dx(name, dout, x, g, hg, hu, wgt, wut, wd, tm, comm=None):
    T, D = x.shape
    K, Fs, _ = wgt.shape

    def body(dout_ref, x_ref, g_ref, hg_ref, hu_ref, wg_ref, wu_ref, wd_ref,
             dx_ref, dhg_ref, dhu_ref, dg_ref, df_ref, dn_scr):
        i, k = pl.program_id(0), pl.program_id(1)

        @pl.when(k == 0)
        def _():
            df_ref[...] = (FFN_RES * dout_ref[...]).astype(BF16)
            dn_scr[...] = jnp.zeros_like(dn_scr)

        @pl.when((k == 0) & (i == 0))
        def _():
            dg_ref[...] = jnp.zeros_like(dg_ref)

        for r0 in range(0, tm, ROW_BLOCK):
            rows = slice(r0, r0 + ROW_BLOCK)
            dact = _dot_nt(df_ref[rows, :], wd_ref[...])
            hgv = hg_ref[rows, :].astype(F32)
            huv = hu_ref[rows, :].astype(F32)
            s = _sigmoid(hgv)
            dhu = (dact * (hgv * s)).astype(BF16)
            dhg = (dact * huv * (s * (1.0 + hgv * (1.0 - s)))).astype(BF16)
            dhg_ref[rows, :] = dhg
            dhu_ref[rows, :] = dhu
            dn_scr[rows, :] += _dot(dhg, wg_ref[...]) + _dot(dhu, wu_ref[...])

        @pl.when(k == K - 1)
        def _():
            xhat, inv = _rms(x_ref[...])
            dx, dg = _rms_bwd(dn_scr[...], xhat, inv, g_ref[...])
            dx_ref[...] = dout_ref[...] + dx
            dg_ref[...] += dg

    w_spec = pl.BlockSpec((None, Fs, D), lambda i, k: (k, 0, 0))
    act_spec = pl.BlockSpec((None, tm, Fs), lambda i, k: (k, i, 0))
    row = pl.BlockSpec((tm, D), lambda i, k: (i, 0))
    row_once = pl.BlockSpec((tm, D), lambda i, k: (i, 0), pipeline_mode=pl.Buffered(1))
    vec = pl.BlockSpec((1, D), lambda i, k: (0, 0))
    return _pallas(
        body, name, (T // tm, K),
        [row, row_once, vec, act_spec, act_spec, w_spec, w_spec, w_spec],
        [row_once, act_spec, act_spec, vec, row],
        [jax.ShapeDtypeStruct((T, D), F32), jax.ShapeDtypeStruct((K, T, Fs), BF16),
         jax.ShapeDtypeStruct((K, T, Fs), BF16), jax.ShapeDtypeStruct((1, D), F32),
         jax.ShapeDtypeStruct((T, D), BF16)],
        [pltpu.VMEM((tm, D), F32)],
        (dout, x, g, hg, hu, wgt, wut, wd), comm)


def _ffn_bwd_dw(name, n, df, hg, hu, dhg, dhu, tk, comm=None):
    T, D = n.shape
    K, _, Fs = hg.shape
    nt = T // tk

    def body(n_ref, df_ref, hg_ref, hu_ref, dhg_ref, dhu_ref, dwg_ref, dwu_ref, dwd_ref, accg, accu, accd):
        t = pl.program_id(1)

        @pl.when(t == 0)
        def _():
            accg[...] = jnp.zeros_like(accg)
            accu[...] = jnp.zeros_like(accu)
            accd[...] = jnp.zeros_like(accd)

        nv = n_ref[...]
        hgv = hg_ref[...].astype(F32)
        act = (hgv * _sigmoid(hgv) * hu_ref[...].astype(F32)).astype(BF16)
        accg[...] += _dot_tn(dhg_ref[...], nv)
        accu[...] += _dot_tn(dhu_ref[...], nv)
        accd[...] += _dot_tn(act, df_ref[...])

        @pl.when(t == nt - 1)
        def _():
            dwg_ref[...] = accg[...].astype(BF16)
            dwu_ref[...] = accu[...].astype(BF16)
            dwd_ref[...] = accd[...].astype(BF16)

    act_spec = pl.BlockSpec((None, tk, Fs), lambda k, t: (k, t, 0))
    w_spec = pl.BlockSpec((None, Fs, D), lambda k, t: (k, 0, 0))
    row = pl.BlockSpec((tk, D), lambda k, t: (t, 0))
    return _pallas(
        body, name, (K, nt),
        [row, row, act_spec, act_spec, act_spec, act_spec],
        [w_spec, w_spec, w_spec],
        [jax.ShapeDtypeStruct((K, Fs, D), BF16)] * 3,
        [pltpu.VMEM((Fs, D), F32)] * 3,
        (n, df, hg, hu, dhg, dhu), comm)


def _mix_proj_fwd(x, g, wproj_t, wf_t, tm, tn, comm=None):
    T, D = x.shape
    N = wproj_t.shape[0]

    def body(x_ref, g_ref, w_ref, wf_ref, h_ref, proj_ref, flog_ref, h_scr):
        @pl.when(pl.program_id(1) == 0)
        def _():
            xhat, _ = _rms(x_ref[...])
            h = (xhat * g_ref[...]).astype(BF16)
            h_scr[...] = h
            h_ref[...] = h
            flog_ref[...] = _dot_nt(h, wf_ref[...])

        proj_ref[...] = _dot_nt(h_scr[...], w_ref[...]).astype(BF16)

    return _pallas(
        body, "mix_proj_fwd", (T // tm, N // tn),
        [pl.BlockSpec((tm, D), lambda i, n: (i, 0)), pl.BlockSpec((1, D), lambda i, n: (0, 0)),
         pl.BlockSpec((tn, D), lambda i, n: (n, 0)), pl.BlockSpec((LANES, D), lambda i, n: (0, 0))],
        [pl.BlockSpec((tm, D), lambda i, n: (i, 0)), pl.BlockSpec((tm, tn), lambda i, n: (i, n)),
         pl.BlockSpec((tm, LANES), lambda i, n: (i, 0))],
        [jax.ShapeDtypeStruct((T, D), BF16), jax.ShapeDtypeStruct((T, N), BF16),
         jax.ShapeDtypeStruct((T, LANES), F32)],
        [pltpu.VMEM((tm, D), BF16)],
        (x, g, wproj_t, wf_t), comm)


def _log_sigmoid(z):
    return -(jnp.maximum(-z, 0.0) + jnp.log(1.0 + jnp.exp(-jnp.abs(z))))


def _tri(n, lower):
    r = lax.broadcasted_iota(jnp.int32, (n, n), 0)
    c = lax.broadcasted_iota(jnp.int32, (n, n), 1)
    return jnp.where((r >= c) if lower else (r <= c), 1.0, 0.0).astype(F32)


def _dot_f32(a, b):
    return lax.dot_general(a, b, (((1,), (0,)), ((), ())), preferred_element_type=F32,
                           precision=lax.Precision.HIGHEST)


def _fgate_fwd(flog, bias, B, S, ch):
    def body(flog_ref, b_ref, cum_ref):
        tri = _tri(ch, True)
        carry = jnp.zeros((1, LANES), F32)
        for c0 in range(0, S, ch):
            lf = _log_sigmoid(flog_ref[c0:c0 + ch, :] + b_ref[...])
            cs = _dot_f32(tri, lf) + carry
            cum_ref[c0:c0 + ch, :] = cs
            carry = cs[ch - 1:ch, :]

    return pl.pallas_call(
        body, name="fgate_fwd", grid=(B,),
        in_specs=[pl.BlockSpec((S, LANES), lambda b: (b, 0)),
                  pl.BlockSpec((1, LANES), lambda b: (0, 0))],
        out_specs=pl.BlockSpec((S, LANES), lambda b: (b, 0)),
        out_shape=jax.ShapeDtypeStruct((B * S, LANES), F32),
        compiler_params=_params(("arbitrary",)),
    )(flog, bias)


def _fgate_bwd(dcum, flog, bias, B, S, ch):
    def body(dcum_ref, flog_ref, b_ref, dflog_ref, db_ref):
        @pl.when(pl.program_id(0) == 0)
        def _():
            db_ref[...] = jnp.zeros_like(db_ref)

        tri = _tri(ch, False)
        carry = jnp.zeros((1, LANES), F32)
        db = jnp.zeros((1, LANES), F32)
        for c0 in range(S - ch, -1, -ch):
            dlf = _dot_f32(tri, dcum_ref[c0:c0 + ch, :]) + carry
            carry = dlf[0:1, :]
            z = flog_ref[c0:c0 + ch, :] + b_ref[...]
            dz = dlf * _sigmoid(-z)
            dflog_ref[c0:c0 + ch, :] = dz
            db = db + jnp.sum(dz, axis=0, keepdims=True)
        db_ref[...] += db

    return pl.pallas_call(
        body, name="fgate_bwd", grid=(B,),
        in_specs=[pl.BlockSpec((S, LANES), lambda b: (b, 0)),
                  pl.BlockSpec((S, LANES), lambda b: (b, 0)),
                  pl.BlockSpec((1, LANES), lambda b: (0, 0))],
        out_specs=[pl.BlockSpec((S, LANES), lambda b: (b, 0)),
                   pl.BlockSpec((1, LANES), lambda b: (0, 0))],
        out_shape=[jax.ShapeDtypeStruct((B * S, LANES), F32),
                   jax.ShapeDtypeStruct((1, LANES), F32)],
        compiler_params=_params(("arbitrary",)),
    )(dcum, flog, bias)


def _pick_lane(tile, h):
    lane = lax.broadcasted_iota(jnp.int32, tile.shape, 1)
    return jnp.sum(jnp.where(lane == h, tile, 0.0), axis=1, keepdims=True)


def _put_lane(col, h, width=LANES):
    lane = lax.broadcasted_iota(jnp.int32, (col.shape[0], width), 1)
    return jnp.where(lane == h, col, 0.0)


def _pick_row(tile, h):
    row = lax.broadcasted_iota(jnp.int32, tile.shape, 0)
    return jnp.sum(jnp.where(row == h, tile, 0.0), axis=0, keepdims=True)


def _put_row(vec, h):
    row = lax.broadcasted_iota(jnp.int32, (8, vec.shape[1]), 0)
    return jnp.where(row == h, vec, 0.0)


def _causal(tq):
    r = lax.broadcasted_iota(jnp.int32, (tq, tq), 0)
    c = lax.broadcasted_iota(jnp.int32, (tq, tq), 1)
    return r >= c


def _head_halves(t):
    lo = lax.broadcasted_iota(jnp.int32, t.shape, 1) < HEAD_DIM
    zero = jnp.zeros_like(t)
    return jnp.where(lo, t, zero), jnp.where(lo, zero, t)


NEG = -1e30
ATTN_SCALE = 1.0 / math.sqrt(HEAD_DIM)


def _scaled(q):
    return (q.astype(F32) * ATTN_SCALE).astype(q.dtype)


def _attn_fwd(proj, cum, cum_t, B, S, tq, comm=None):
    nq = S // tq

    def body(q_ref, k_ref, v_ref, cum_ref, cumt_ref, o_ref, lse_ref):
        qi, hp = pl.program_id(1), pl.program_id(2)
        qm = _head_halves(_scaled(q_ref[...]))
        cumv = cum_ref[...]
        cq = [_pick_lane(cumv, 2 * hp + e) for e in range(2)]

        def tile(j, carry, masked):
            off = pl.multiple_of(j * tq, tq)
            kj = k_ref[pl.ds(off, tq), :]
            vj = v_ref[pl.ds(off, tq), :]
            ct = cumt_ref[j]
            new = []
            for e in range(2):
                m, l, acc = carry[e]
                s = _dot_nt(qm[e], kj) - _pick_row(ct, 2 * hp + e)
                if masked:
                    s = jnp.where(_causal(tq), s, NEG)
                m_new = jnp.maximum(m, jnp.max(s, axis=1, keepdims=True))
                p = jnp.exp(s - m_new)
                alpha = jnp.exp(m - m_new)
                l = alpha * l + jnp.sum(p, axis=1, keepdims=True)
                acc = alpha * acc + _dot(p.astype(BF16), vj)
                new.append((m_new, l, acc))
            return tuple(new)

        one = (jnp.full((tq, 1), NEG, F32), jnp.zeros((tq, 1), F32), jnp.zeros((tq, LANES), F32))
        carry = lax.fori_loop(0, qi, lambda j, c: tile(j, c, False), (one, one))
        (ma, la, acca), (mb, lb, accb) = tile(qi, carry, True)
        lo = lax.broadcasted_iota(jnp.int32, (tq, LANES), 1) < HEAD_DIM
        o_ref[...] = jnp.where(lo, acca / la, accb / lb).astype(BF16)

        @pl.when(hp == 0)
        def _():
            lse_ref[...] = jnp.zeros_like(lse_ref)

        lse_ref[...] += (_put_lane(ma + jnp.log(la) + cq[0], 2 * hp) + _put_lane(mb + jnp.log(lb) + cq[1], 2 * hp + 1))

    kv = lambda first: pl.BlockSpec((S, LANES), lambda b, i, hp: (b, first + hp))
    return _pallas(
        body, "attn_fwd", (B, nq, HEAD_PAIRS),
        [pl.BlockSpec((tq, LANES), lambda b, i, hp: (b * nq + i, hp)),
         kv(ATTN_W // LANES), kv(2 * ATTN_W // LANES),
         pl.BlockSpec((tq, LANES), lambda b, i, hp: (b * nq + i, 0)),
         pl.BlockSpec((None, nq, 8, tq), lambda b, i, hp: (b, 0, 0, 0))],
        [pl.BlockSpec((tq, LANES), lambda b, i, hp: (b * nq + i, hp)),
         pl.BlockSpec((tq, LANES), lambda b, i, hp: (b * nq + i, 0))],
        [jax.ShapeDtypeStruct((B * S, ATTN_W), BF16), jax.ShapeDtypeStruct((B * S, LANES), F32)],
        [], (proj, proj, proj, cum, cum_t), comm)


def _attn_bwd(proj, o, do, lse, cum, cum_t, B, S, tq, comm=None):
    nq = S // tq

    def body(q_ref, k_ref, v_ref, o_ref, do_ref, lse_ref, cum_ref, cumt_ref,
             dq_ref, dk_ref, dv_ref, dcq_ref, dck_ref, dq_scr):
        hp, kj = pl.program_id(1), pl.program_id(2)

        @pl.when(kj == 0)
        def _():
            dq_scr[...] = jnp.zeros_like(dq_scr)

        @pl.when((kj == 0) & (hp == 0))
        def _():
            dcq_ref[...] = jnp.zeros_like(dcq_ref)
            dck_ref[...] = jnp.zeros_like(dck_ref)

        kv = k_ref[...]
        vv = v_ref[...]
        km = _head_halves(kv)
        ct = cumt_ref[...]
        ck = [_pick_row(ct, 2 * hp + e) for e in range(2)]

        def tile(i, carry, masked):
            dk, dv, dcol = carry
            off = pl.multiple_of(i * tq, tq)
            qi = q_ref[pl.ds(off, tq), :]
            ov = o_ref[pl.ds(off, tq), :].astype(F32)
            qm = _head_halves(_scaled(qi))
            dom = _head_halves(do_ref[pl.ds(off, tq), :])
            cumv = cum_ref[pl.ds(off, tq), :]
            lsev = lse_ref[pl.ds(off, tq), :]
            dcq = jnp.zeros((tq, LANES), F32)
            dq = jnp.zeros((tq, LANES), F32)
            dcol_new = []
            for e in range(2):
                delta = jnp.sum(dom[e].astype(F32) * ov, axis=1, keepdims=True)
                row_term = _pick_lane(cumv, 2 * hp + e) - _pick_lane(lsev, 2 * hp + e)
                p = jnp.exp(_dot_nt(qm[e], kv) + row_term - ck[e])
                if masked:
                    p = jnp.where(_causal(tq), p, 0.0)
                dv = dv + _dot_tn(dom[e], p.astype(BF16))
                ds = p * (_dot_nt(dom[e], vv) - delta)
                dcol_new.append(dcol[e] + jnp.sum(ds, axis=0, keepdims=True))
                dcq = dcq + _put_lane(jnp.sum(ds, axis=1, keepdims=True), 2 * hp + e)
                dsb = ds.astype(BF16)
                dk = dk + _dot_tn(qm[e], dsb)
                dq = dq + _dot(dsb, km[e]) * ATTN_SCALE
            dq_scr[pl.ds(off, tq), :] += dq
            dcq_ref[pl.ds(off, tq), :] += dcq
            return dk, dv, tuple(dcol_new)

        zero_row = jnp.zeros((1, tq), F32)
        init = (jnp.zeros((LANES, tq), F32), jnp.zeros((LANES, tq), F32), (zero_row, zero_row))
        carry = tile(kj, init, True)
        dk, dv, dcol = lax.fori_loop(kj + 1, nq, lambda i, c: tile(i, c, False), carry)
        dk_ref[...] = dk.T.astype(BF16)
        dv_ref[...] = dv.T.astype(BF16)
        dck_ref[kj] += -(_put_row(dcol[0], 2 * hp) + _put_row(dcol[1], 2 * hp + 1))

        @pl.when(kj == nq - 1)
        def _():
            dq_ref[...] = dq_scr[...].astype(BF16)

    seq = lambda first: pl.BlockSpec((S, LANES), lambda b, hp, j: (b, first + hp))
    tile_in = lambda first: pl.BlockSpec((tq, LANES), lambda b, hp, j: (b * nq + j, first + hp))
    lanes0 = pl.BlockSpec((S, LANES), lambda b, hp, j: (b, 0))
    out = jax.ShapeDtypeStruct((B * S, ATTN_W), BF16)
    return _pallas(
        body, "attn_bwd", (B, HEAD_PAIRS, nq),
        [seq(0), tile_in(ATTN_W // LANES), tile_in(2 * ATTN_W // LANES), seq(0), seq(0), lanes0, lanes0,
         pl.BlockSpec((None, None, 8, tq), lambda b, hp, j: (b, j, 0, 0))],
        [seq(0), tile_in(0), tile_in(0), lanes0,
         pl.BlockSpec((None, nq, 8, tq), lambda b, hp, j: (b, 0, 0, 0))],
        [out, out, out, jax.ShapeDtypeStruct((B * S, LANES), F32), jax.ShapeDtypeStruct((B, nq, 8, tq), F32)],
        [pltpu.VMEM((S, LANES), F32)],
        (proj, proj, proj, o, do, lse, cum, cum_t), comm)


def _shift_down(u, n):
    row = lax.broadcasted_iota(jnp.int32, u.shape, 0)
    return jnp.where(row >= n, pltpu.roll(u, n, 0), 0.0)


def _shift_up(u, n):
    rows = u.shape[0]
    row = lax.broadcasted_iota(jnp.int32, u.shape, 0)
    return jnp.where(row < rows - n, pltpu.roll(u, rows - n, 0), 0.0)


def _conv_specs(S):
    cb = pl.BlockSpec((S, LANES), lambda g, b: (b, COL_CB // LANES + g))
    cc = pl.BlockSpec((S, LANES), lambda g, b: (b, COL_CC // LANES + g))
    cx = pl.BlockSpec((S, LANES), lambda g, b: (b, COL_CX // LANES + g))
    w = pl.BlockSpec((8, LANES), lambda g, b: (0, g))
    return cb, cc, cx, w


def _conv_fwd(proj, conv_w, B, S):
    def body(cb_ref, cc_ref, cx_ref, w_ref, y_ref):
        u = cc_ref[...].astype(F32) * cx_ref[...].astype(F32)
        w = w_ref[...]
        conv = w[0:1, :] * _shift_down(u, 2) + w[1:2, :] * _shift_down(u, 1) + w[2:3, :] * u
        y_ref[...] = (cb_ref[...].astype(F32) * conv).astype(BF16)

    cb, cc, cx, w = _conv_specs(S)
    return pl.pallas_call(
        body, name="conv_fwd", grid=(CONV_W // LANES, B),
        in_specs=[cb, cc, cx, w],
        out_specs=pl.BlockSpec((S, LANES), lambda g, b: (b, g)),
        out_shape=jax.ShapeDtypeStruct((B * S, CONV_W), BF16),
        compiler_params=_params(("arbitrary", "arbitrary")),
    )(proj, proj, proj, conv_w)


def _conv_bwd(dy, proj, conv_w, B, S):
    def body(dy_ref, cb_ref, cc_ref, cx_ref, w_ref, dcb_ref, dcc_ref, dcx_ref, dw_ref):
        @pl.when(pl.program_id(1) == 0)
        def _():
            dw_ref[...] = jnp.zeros_like(dw_ref)

        ccv = cc_ref[...].astype(F32)
        cxv = cx_ref[...].astype(F32)
        u = ccv * cxv
        u1 = _shift_down(u, 1)
        u2 = _shift_down(u, 2)
        w = w_ref[...]
        conv = w[0:1, :] * u2 + w[1:2, :] * u1 + w[2:3, :] * u
        dyv = dy_ref[...].astype(F32)
        dcb_ref[...] = (dyv * conv).astype(BF16)
        dconv = dyv * cb_ref[...].astype(F32)
        du = w[2:3, :] * dconv + w[1:2, :] * _shift_up(dconv, 1) + w[0:1, :] * _shift_up(dconv, 2)
        dcc_ref[...] = (du * cxv).astype(BF16)
        dcx_ref[...] = (du * ccv).astype(BF16)
        row = lax.broadcasted_iota(jnp.int32, (8, LANES), 0)
        dw = jnp.where(row == 0, jnp.sum(dconv * u2, axis=0, keepdims=True),
                       jnp.where(row == 1, jnp.sum(dconv * u1, axis=0, keepdims=True),
                                 jnp.where(row == 2, jnp.sum(dconv * u, axis=0, keepdims=True), 0.0)))
        dw_ref[...] += dw

    cb, cc, cx, w = _conv_specs(S)
    out = pl.BlockSpec((S, LANES), lambda g, b: (b, g))
    return pl.pallas_call(
        body, name="conv_bwd", grid=(CONV_W // LANES, B),
        in_specs=[out, cb, cc, cx, w],
        out_specs=[out, out, out, w],
        out_shape=[jax.ShapeDtypeStruct((B * S, CONV_W), BF16)] * 3 + [jax.ShapeDtypeStruct((8, CONV_W), F32)],
        compiler_params=_params(("arbitrary", "arbitrary")),
    )(dy, proj, proj, proj, conv_w)


def _gate_specs(tm, D):
    ga = pl.BlockSpec((tm, D), lambda i: (i, COL_GATES // D))
    gc = pl.BlockSpec((tm, D), lambda i: (i, COL_GATES // D + 1))
    return ga, gc


def _mix_out_fwd(x, o, yc, proj, woa, woc, wout, tm):
    T, D = x.shape

    def body(x_ref, o_ref, yc_ref, ga_ref, gc_ref, woa_ref, woc_ref, wout_ref, out_ref):
        ya = _dot(o_ref[...], woa_ref[...])
        yp = _dot(yc_ref[...], woc_ref[...])
        merged = _sigmoid(ga_ref[...].astype(F32)) * ya + _sigmoid(gc_ref[...].astype(F32)) * yp
        out_ref[...] = x_ref[...] + _dot(merged.astype(BF16), wout_ref[...])

    ga, gc = _gate_specs(tm, D)
    row = lambda w: pl.BlockSpec((tm, w), lambda i: (i, 0))
    whole = lambda a: pl.BlockSpec(a.shape, lambda i: (0, 0))
    return pl.pallas_call(
        body, name="mix_out_fwd", grid=(T // tm,),
        in_specs=[row(D), row(ATTN_W), row(CONV_W), ga, gc, whole(woa), whole(woc), whole(wout)],
        out_specs=row(D),
        out_shape=jax.ShapeDtypeStruct((T, D), F32),
        compiler_params=_params(("arbitrary",)),
    )(x, o, yc, proj, proj, woa, woc, wout)


def _mix_out_bwd(dx, o, yc, proj, woa, woc, wout, tm, comm=None):
    T, D = dx.shape
    nt = T // tm

    def body(dx_ref, o_ref, yc_ref, ga_ref, gc_ref, woa_ref, woc_ref, wout_ref,
             do_ref, dyc_ref, dg_ref, dwoa_ref, dwoc_ref, dwout_ref, acca, accc, acco):
        t = pl.program_id(0)

        @pl.when(t == 0)
        def _():
            acca[...] = jnp.zeros_like(acca)
            accc[...] = jnp.zeros_like(accc)
            acco[...] = jnp.zeros_like(acco)

        dxb = dx_ref[...].astype(BF16)
        ov, ycv = o_ref[...], yc_ref[...]
        ya = _dot(ov, woa_ref[...])
        yp = _dot(ycv, woc_ref[...])
        sa = _sigmoid(ga_ref[...].astype(F32))
        sc = _sigmoid(gc_ref[...].astype(F32))
        merged = (sa * ya + sc * yp).astype(BF16)
        dm = _dot_nt(dxb, wout_ref[...])
        dya = (dm * sa).astype(BF16)
        dyp = (dm * sc).astype(BF16)
        dg_ref[:, :D] = (dm * ya * sa * (1.0 - sa)).astype(BF16)
        dg_ref[:, D:] = (dm * yp * sc * (1.0 - sc)).astype(BF16)
        do_ref[...] = _dot_nt(dya, woa_ref[...]).astype(BF16)
        dyc_ref[...] = _dot_nt(dyp, woc_ref[...]).astype(BF16)
        acca[...] += _dot_tn(ov, dya)
        accc[...] += _dot_tn(ycv, dyp)
        acco[...] += _dot_tn(merged, dxb)

        @pl.when(t == nt - 1)
        def _():
            dwoa_ref[...] = acca[...].astype(BF16)
            dwoc_ref[...] = accc[...].astype(BF16)
            dwout_ref[...] = acco[...].astype(BF16)

    ga, gc = _gate_specs(tm, D)
    row = lambda w: pl.BlockSpec((tm, w), lambda i: (i, 0))
    whole = lambda a: pl.BlockSpec(a.shape, lambda i: (0, 0))
    return _pallas(
        body, "mix_out_bwd", (nt,),
        [row(D), row(ATTN_W), row(CONV_W), ga, gc, whole(woa), whole(woc), whole(wout)],
        [row(ATTN_W), row(CONV_W), row(2 * D), whole(woa), whole(woc), whole(wout)],
        [jax.ShapeDtypeStruct((T, ATTN_W), BF16), jax.ShapeDtypeStruct((T, CONV_W), BF16),
         jax.ShapeDtypeStruct((T, 2 * D), BF16),
         jax.ShapeDtypeStruct(woa.shape, BF16), jax.ShapeDtypeStruct(woc.shape, BF16),
         jax.ShapeDtypeStruct(wout.shape, BF16)],
        [pltpu.VMEM(woa.shape, F32), pltpu.VMEM(woc.shape, F32), pltpu.VMEM(wout.shape, F32)],
        (dx, o, yc, proj, proj, woa, woc, wout), comm)


def _proj_pieces(dq, dk, dv, dcb, dcc, dcx, dgates, dflog):
    D = dgates.shape[1] // 2
    return [(dq, ATTN_W, 0), (dk, ATTN_W, 0), (dv, ATTN_W, 0), (dcb, CONV_W, 0), (dcc, CONV_W, 0), (dcx, CONV_W, 0),
            (dgates, D, 0), (dgates, D, 1), (dflog, LANES, 0)]


def _mix_proj_bwd_dx(dres, x, g, pieces, wproj_t, wf_t, tm, comm=None):
    T, D = x.shape
    n = len(pieces)
    w_blocks = [(ATTN_W, 0), (ATTN_W, 1), (ATTN_W, 2), (CONV_W, 3), (CONV_W, 4), (CONV_W, 5),
                (D, COL_GATES // D), (D, COL_GATES // D + 1)]

    def body(*refs):
        dres_ref, x_ref, g_ref = refs[:3]
        p_refs, w_refs = refs[3:3 + n], refs[3 + n:3 + 2 * n]
        dx_ref, dg_ref = refs[3 + 2 * n:]

        @pl.when(pl.program_id(0) == 0)
        def _():
            dg_ref[...] = jnp.zeros_like(dg_ref)

        dh = _dot(p_refs[0][...].astype(BF16), w_refs[0][...])
        for p_ref, w_ref in zip(p_refs[1:], w_refs[1:]):
            dh = dh + _dot(p_ref[...].astype(BF16), w_ref[...])
        xhat, inv = _rms(x_ref[...])
        dx, dg = _rms_bwd(dh, xhat, inv, g_ref[...])
        dx_ref[...] = dres_ref[...] + dx
        dg_ref[...] += dg

    row = pl.BlockSpec((tm, D), lambda i: (i, 0))
    vec = pl.BlockSpec((1, D), lambda i: (0, 0))
    p_specs = [pl.BlockSpec((tm, w), lambda i, cb=cb: (i, cb)) for _, w, cb in pieces]
    w_specs = [pl.BlockSpec((r, D), lambda i, rb=rb: (rb, 0)) for r, rb in w_blocks]
    w_specs.append(pl.BlockSpec((LANES, D), lambda i: (0, 0)))
    return _pallas(
        body, "mix_proj_bwd_dx", (T // tm,),
        [row, row, vec] + p_specs + w_specs, [row, vec],
        [jax.ShapeDtypeStruct((T, D), F32), jax.ShapeDtypeStruct((1, D), F32)], [],
        (dres, x, g, *[p for p, _, _ in pieces], *([wproj_t] * len(w_blocks)), wf_t), comm)


def _matmuls_tn(name, pieces, b, tk):
    T, N = b.shape
    nt = T // tk
    n = len(pieces)

    def body(*refs):
        a_refs, b_ref, out_refs, accs = refs[:n], refs[n], refs[n + 1:2 * n + 1], refs[2 * n + 1:]
        t = pl.program_id(0)

        @pl.when(t == 0)
        def _():
            for acc in accs:
                acc[...] = jnp.zeros_like(acc)

        bv = b_ref[...]
        for a_ref, acc in zip(a_refs, accs):
            acc[...] += _dot_tn(a_ref[...].astype(BF16), bv)

        @pl.when(t == nt - 1)
        def _():
            for out_ref, acc in zip(out_refs, accs):
                out_ref[...] = acc[...].astype(BF16)

    return pl.pallas_call(
        body, name=name, grid=(nt,),
        in_specs=[pl.BlockSpec((tk, w), lambda t, cb=cb: (t, cb)) for _, w, cb in pieces]
        + [pl.BlockSpec((tk, N), lambda t: (t, 0))],
        out_specs=[pl.BlockSpec((w, N), lambda t: (0, 0)) for _, w, _ in pieces],
        out_shape=[jax.ShapeDtypeStruct((w, N), BF16) for _, w, _ in pieces],
        scratch_shapes=[pltpu.VMEM((w, N), F32) for _, w, _ in pieces],
        compiler_params=_params(("arbitrary",)),
    )(*[a for a, _, _ in pieces], b)


TOKEN_TILE = 512
TOKEN_TILE_WIDE = 1024
ATTN_TILE = 512
SCAN_CHUNK = 256
PROJ_DX_TILE = 256


def _local_step(x, target, plan, B, S):
    T, D = x.shape
    tm = min(TOKEN_TILE, T)
    tm_fwd = min(TOKEN_TILE_WIDE, T)
    tq = min(ATTN_TILE, S)
    nq = S // tq
    ch = min(SCAN_CHUNK, S)

    def riding(kernel_name, build):
        results, brought = build(plan.rider(kernel_name))
        plan.arrived(kernel_name, brought)
        return results

    hg1, hu1, n1 = plan.ffn1_up(x, tm_fwd)
    w1 = plan.weights("ffn1")
    x1, = riding("ffn1_down", lambda comm: _ffn_down("ffn1_down", x, hg1, hu1, w1["ffn1_down"], tm_fwd, comm))
    wm = plan.weights("mix_in")
    h, proj, flog = riding("mix_proj_fwd", lambda comm: _mix_proj_fwd(
        x1, wm["mix_norm"], wm["w_proj"], wm["w_f"], tm_fwd, PROJ_W // 4, comm))
    wm.update(plan.weights("mix_out"))
    cum = _fgate_fwd(flog, wm["b_forget"], B, S, ch)
    cum_t = jnp.transpose(cum[:, :N_HEADS].reshape(B, nq, tq, N_HEADS), (0, 1, 3, 2))
    o, lse = riding("attn_fwd", lambda comm: _attn_fwd(proj, cum, cum_t, B, S, tq, comm))
    yc = _conv_fwd(proj, wm["conv_w"], B, S)
    x2 = _mix_out_fwd(x1, o, yc, proj, wm["w_o_attn"], wm["w_o_conv"], wm["w_out"], tm)
    w2 = plan.weights("ffn2")
    dx3, hg2, hu2, n2, loss, d_final_norm = _ffn_fwd_loss(
        "ffn2_fwd_loss", x2, w2["ffn2_norm"], w2["ffn2_gate"], w2["ffn2_up"], w2["ffn2_down"], target, w2["final_norm"],
        tm_fwd)

    g = {"final_norm": d_final_norm}
    dx2, dhg2, dhu2, g["ffn2_norm"], df2 = _ffn_bwd_dx("ffn2_bwd_dx", dx3, x2, w2["ffn2_norm"], hg2, hu2,
                                                  w2["ffn2_gate"], w2["ffn2_up"], w2["ffn2_down"], tm_fwd)[0]
    plan.reduce("ffn2", dict(zip(("ffn2_gate", "ffn2_up", "ffn2_down"),
                                 _ffn_bwd_dw("ffn2_bwd_dw", n2, df2, hg2, hu2, dhg2, dhu2, tm)[0])))
    do, dyc, dgates, dwoa, dwoc, dwout = riding("mix_out_bwd", lambda comm: _mix_out_bwd(
        dx2, o, yc, proj, wm["w_o_attn"], wm["w_o_conv"], wm["w_out"], tm, comm))
    plan.reduce("out", dict(w_o_attn=_shard_cols(dwoa), w_o_conv=_shard_cols(dwoc), w_out=dwout.reshape(N_CHIPS, -1, D)))
    dq, dk, dv, dcq, dck = riding("attn_bwd", lambda comm: _attn_bwd(proj, o, do, lse, cum, cum_t, B, S, tq, comm))
    dcum = dcq + jnp.pad(jnp.transpose(dck, (0, 1, 3, 2)).reshape(T, N_HEADS), ((0, 0), (0, LANES - N_HEADS)))
    dflog, g["b_forget"] = _fgate_bwd(dcum, flog, wm["b_forget"], B, S, ch)
    dcb, dcc, dcx, g["conv_w"] = _conv_bwd(dyc, proj, wm["conv_w"], B, S)
    pieces = _proj_pieces(dq, dk, dv, dcb, dcc, dcx, dgates, dflog)
    dwq, dwk, dwv, dwcb, dwcc, dwcx = _matmuls_tn("mix_dw_a", pieces[:6], h, tm)
    dwga, dwgc, dwf = _matmuls_tn("mix_dw_b", pieces[6:], h, tm)
    dwin_t = jnp.concatenate([dwq, dwk, dwv, dwf[:N_HEADS], dwcb, dwcc, dwcx, dwga, dwgc], axis=0)
    plan.reduce("w_in", {"w_in": dwin_t.reshape(N_CHIPS, -1, D)})
    dx1, g["mix_norm"] = riding("mix_proj_bwd_dx", lambda comm: _mix_proj_bwd_dx(
        dx2, x1, wm["mix_norm"], pieces, wm["w_proj"], wm["w_f"], min(PROJ_DX_TILE, T), comm))
    grad_x, dhg1, dhu1, g["ffn1_norm"], df1 = _ffn_bwd_dx(
        "ffn1_bwd_dx", dx1, x, w1["ffn1_norm"], hg1, hu1, w1["ffn1_gate"], w1["ffn1_up"], w1["ffn1_down"], tm_fwd)[0]
    plan.reduce_small(g, loss)
    plan.reduce("ffn1", dict(zip(("ffn1_gate", "ffn1_up", "ffn1_down"), riding("ffn1_bwd_dw", lambda comm: _ffn_bwd_dw(
        "ffn1_bwd_dw", n1, df1, hg1, hu1, dhg1, dhu1, tm, comm)))))
    return loss, grad_x, g


TRANSPOSED = ("ffn1_gate", "ffn1_up", "ffn2_gate", "ffn2_up", "w_in")
NORMS = ("ffn1_norm", "mix_norm", "ffn2_norm", "final_norm")


def _unshard_cols(a):
    return jnp.transpose(a, (1, 0, 2)).reshape(a.shape[1], N_CHIPS * a.shape[2])


def _shard_cols(a):
    return jnp.transpose(a.reshape(a.shape[0], N_CHIPS, a.shape[1] // N_CHIPS), (1, 0, 2))


def _layout_ffn(which):
    def layout(st, small):
        w = {n: st[n] for n in (which + "_gate", which + "_up", which + "_down")}
        w[which + "_norm"] = small[which + "_norm"].reshape(1, -1)
        if which == "ffn2":
            w["final_norm"] = small["final_norm"].reshape(1, -1)
        return w
    return layout


def _layout_mix_in(st, small):
    win_t = st["w_in"].reshape(-1, st["w_in"].shape[2])
    return {
        "w_proj": jnp.concatenate([win_t[:N_FORGET_COL], win_t[N_FORGET_COL + N_HEADS:]], axis=0),
        "w_f": jnp.pad(win_t[N_FORGET_COL:N_FORGET_COL + N_HEADS], ((0, LANES - N_HEADS), (0, 0))),
        "conv_w": _unshard_cols(st["conv_w"]),
        "mix_norm": small["mix_norm"].reshape(1, -1),
        "b_forget": jnp.pad(small["b_forget"].reshape(1, -1), ((0, 0), (0, LANES - N_HEADS))),
    }


def _layout_mix_out(st, small):
    return {"w_o_attn": _unshard_cols(st["w_o_attn"]), "w_o_conv": _unshard_cols(st["w_o_conv"]),
            "w_out": st["w_out"].reshape(-1, st["w_out"].shape[2])}


_LAYOUTS = {"ffn1": _layout_ffn("ffn1"), "mix_in": _layout_mix_in, "mix_out": _layout_mix_out, "ffn2": _layout_ffn("ffn2")}


ANY = pl.BlockSpec(memory_space=pl.ANY)
BIG = ("ffn1_gate", "ffn1_up", "ffn1_down", "w_in", "w_o_attn", "w_o_conv", "w_out",
       "ffn2_gate", "ffn2_up", "ffn2_down")


def _place():
    x, y, c = lax.axis_index("x"), lax.axis_index("y"), lax.axis_index("c")
    others = [(1 - x, y), (x, 1 - y), (1 - x, 1 - y)]
    return x, y, c, others


def _col_halves(cols, c):
    hc = cols // 2
    return pl.ds(pl.multiple_of(c * hc, LANES), hc), pl.ds(pl.multiple_of((1 - c) * hc, LANES), hc)


def _gather_comm(shards, conv_shard=None):
    n = len(shards)
    inputs = list(shards) + ([] if conv_shard is None else [conv_shard])

    def copies(ins, outs, sems):
        send_sems, recv_sems, pass_send, pass_recv = sems[:4]
        x, y, c, others = _place()

        def chip_copy(a, j, chip):
            mine, _ = _col_halves(ins[a].shape[1], c)
            return pltpu.make_async_remote_copy(
                src_ref=ins[a].at[:, mine], dst_ref=outs[a].at[chip, :, mine],
                send_sem=send_sems.at[3 * a + j], recv_sem=recv_sems.at[3 * a + j],
                device_id=(*others[j], c), device_id_type=MESH)

        def pass_copy(a, j, chip, half):
            return pltpu.make_async_remote_copy(
                src_ref=outs[a].at[chip, :, half], dst_ref=outs[a].at[chip, :, half],
                send_sem=pass_send.at[3 * a + j], recv_sem=pass_recv.at[3 * a + j],
                device_id=(x, y, 1 - c), device_id_type=MESH)

        def conv_copy(j, chip):
            return pltpu.make_async_remote_copy(
                src_ref=ins[n], dst_ref=outs[n].at[chip],
                send_sem=sems[4].at[j], recv_sem=sems[5].at[j],
                device_id=(*others[j], c), device_id_type=MESH)

        me = 2 * x + y
        sends = [chip_copy(a, j, me) for a in range(n) for j in range(3)]
        if conv_shard is not None:
            sends += [conv_copy(j, me) for j in range(3)]
        return c, others, sends, chip_copy, pass_copy, conv_copy

    def start(ins, outs, sems):
        for cp in copies(ins, outs, sems)[2]:
            cp.start()

    def finish(ins, outs, sems):
        c, others, sends, chip_copy, pass_copy, conv_copy = copies(ins, outs, sems)
        passed = []
        for a in range(n):
            mine, _ = _col_halves(ins[a].shape[1], c)
            for j, (ox, oy) in enumerate(others):
                chip_copy(a, j, 2 * ox + oy).wait_recv()
                passed.append(pass_copy(a, j, 2 * ox + oy, mine))
                passed[-1].start()
        for a in range(n):
            _, theirs = _col_halves(ins[a].shape[1], c)
            for j, (ox, oy) in enumerate(others):
                pass_copy(a, j, 2 * ox + oy, theirs).wait_recv()
        if conv_shard is not None:
            for j, (ox, oy) in enumerate(others):
                conv_copy(j, 2 * ox + oy).wait_recv()
        for cp in sends + passed:
            cp.wait_send()

    scratch = [pltpu.SemaphoreType.DMA((3 * n,))] * 4
    if conv_shard is not None:
        scratch += [pltpu.SemaphoreType.DMA((3,))] * 2
    return _Comm(inputs, [jax.ShapeDtypeStruct((N_CHIPS,) + s.shape, s.dtype) for s in inputs], scratch, start, finish)


def _fill_own(stacks, shards):
    chip = 2 * lax.axis_index("x") + lax.axis_index("y")
    return [lax.dynamic_update_index_in_dim(st, s, chip, 0) for st, s in zip(stacks, shards)]


def _run_comm(name, comm):
    ci, co = len(comm.inputs), len(comm.out_shape)

    def body(*refs):
        comm.start(refs[:ci], refs[ci:ci + co], refs[ci + co:])
        comm.finish(refs[:ci], refs[ci:ci + co], refs[ci + co:])

    return pl.pallas_call(body, name=name, in_specs=[ANY] * ci, out_specs=[ANY] * co, out_shape=comm.out_shape,
                          scratch_shapes=comm.scratch)(*comm.inputs)


def _sibling_exchange_comm(grads):
    n = len(grads)

    def copies(ins, outs, sems):
        x, y, c, _ = _place()
        return [pltpu.make_async_remote_copy(
            src_ref=ins[a].at[:, :, _col_halves(ins[a].shape[2], c)[1]], dst_ref=outs[a],
            send_sem=sems[0].at[a], recv_sem=sems[1].at[a],
            device_id=(x, y, 1 - c), device_id_type=MESH) for a in range(n)]

    def start(ins, outs, sems):
        for cp in copies(ins, outs, sems):
            cp.start()

    def finish(ins, outs, sems):
        for cp in copies(ins, outs, sems):
            cp.wait()

    half = lambda s: jax.ShapeDtypeStruct((s.shape[0], s.shape[1], s.shape[2] // 2), s.dtype)
    return _Comm(grads, [half(s) for s in grads], [pltpu.SemaphoreType.DMA((n,))] * 2, start, finish)


def _merge_comms(comms):
    def split(refs, count):
        out, at = [], 0
        for cm in comms:
            out.append(refs[at:at + count(cm)])
            at += count(cm)
        return out

    def parts(ins, outs, sems):
        return zip(comms, split(ins, lambda cm: len(cm.inputs)), split(outs, lambda cm: len(cm.out_shape)),
                   split(sems, lambda cm: len(cm.scratch)))

    def start(ins, outs, sems):
        for cm, i, o, s in parts(ins, outs, sems):
            cm.start(i, o, s)

    def finish(ins, outs, sems):
        for cm, i, o, s in parts(ins, outs, sems):
            cm.finish(i, o, s)

    return _Comm(sum([cm.inputs for cm in comms], []), sum([cm.out_shape for cm in comms], []),
                 sum([cm.scratch for cm in comms], []), start, finish)


def _add_halves(name, grads, recvs, core):
    n = len(grads)

    def body(core_ref, *refs):
        for g_ref, r_ref, out_ref in zip(refs[:n], refs[n:2 * n], refs[2 * n:]):
            out_ref[...] = (g_ref[...].astype(F32) + r_ref[...].astype(F32)).astype(BF16)

    half = lambda g: pl.BlockSpec((None, g.shape[1], g.shape[2] // 2), lambda k, core_ref: (k, 0, 0))
    mine = lambda g: pl.BlockSpec((None, g.shape[1], g.shape[2] // 2), lambda k, core_ref: (k, 0, core_ref[0]))
    return pl.pallas_call(
        body, name=name,
        grid_spec=pltpu.PrefetchScalarGridSpec(
            num_scalar_prefetch=1, grid=(N_CHIPS,),
            in_specs=[mine(g) for g in grads] + [half(g) for g in grads],
            out_specs=[half(g) for g in grads]),
        out_shape=[jax.ShapeDtypeStruct(r.shape, BF16) for r in recvs],
        compiler_params=_params(("arbitrary",)),
    )(core, *grads, *recvs)


def _chip_exchange_comm(parts):
    n = len(parts)

    def copies(ins, outs, sems):
        x, y, c, others = _place()
        return [pltpu.make_async_remote_copy(
            src_ref=ins[a].at[2 * ox + oy], dst_ref=outs[a].at[j],
            send_sem=sems[0].at[3 * a + j], recv_sem=sems[1].at[3 * a + j],
            device_id=(ox, oy, c), device_id_type=MESH) for a in range(n) for j, (ox, oy) in enumerate(others)]

    def start(ins, outs, sems):
        for cp in copies(ins, outs, sems):
            cp.start()

    def finish(ins, outs, sems):
        for cp in copies(ins, outs, sems):
            cp.wait()

    return _Comm(parts, [jax.ShapeDtypeStruct((3,) + s.shape[1:], s.dtype) for s in parts],
                 [pltpu.SemaphoreType.DMA((3 * n,))] * 2, start, finish)


HBM = pl.BlockSpec(memory_space=pltpu.HBM)
SEM = pl.BlockSpec(memory_space=pltpu.SEMAPHORE)


def _split_exchange_copies(parts, lands, send_sems, recv_sems):
    x, y, c, others = _place()
    return [pltpu.make_async_remote_copy(
        src_ref=parts[a].at[2 * ox + oy], dst_ref=lands[a].at[j],
        send_sem=send_sems.at[3 * a + j], recv_sem=recv_sems.at[3 * a + j],
        device_id=(ox, oy, c), device_id_type=MESH) for a in range(len(parts)) for j, (ox, oy) in enumerate(others)]


def _exchange_start(name, parts):
    n = len(parts)

    def body(*refs):
        ins, lands = refs[:n], refs[n:2 * n]
        send_sems, recv_sems, token = refs[2 * n], refs[2 * n + 1], refs[-1]
        for cp in _split_exchange_copies(ins, lands, send_sems, recv_sems):
            cp.start()
        token[...] = jnp.zeros_like(token)

    land_shape = [(3,) + p.shape[1:] for p in parts]
    outs = pl.pallas_call(
        body, name=name,
        out_shape=[pltpu.SemaphoreType.DMA((3 * n,)), pltpu.SemaphoreType.DMA((3 * n,))]
        + [pltpu.HBM(p.shape, p.dtype) for p in parts] + [pltpu.HBM(s, p.dtype) for s, p in zip(land_shape, parts)]
        + [jax.ShapeDtypeStruct((8, LANES), F32)],
        in_specs=[HBM] * (2 * n), out_specs=[SEM, SEM] + [HBM] * (2 * n) + [pl.BlockSpec(memory_space=pltpu.VMEM)],
        input_output_aliases={i: 2 + i for i in range(2 * n)},
        compiler_params=pltpu.CompilerParams(has_side_effects=pltpu.SideEffectType.DATAFLOW_SIDE_EFFECTING),
    )(*[pltpu.with_memory_space_constraint(p, pltpu.HBM) for p in parts],
      *[pltpu.with_memory_space_constraint(lax.empty(s, p.dtype), pltpu.HBM) for s, p in zip(land_shape, parts)])
    return outs[0], outs[1], list(outs[2:2 + n]), list(outs[2 + n:2 + 2 * n]), outs[-1]


def _exchange_wait(name, send_sems, recv_sems, parts, lands, after):
    n = len(parts)

    def body(*refs):
        ins, zones = refs[:n], refs[n:2 * n]
        for cp in _split_exchange_copies(ins, zones, refs[2 * n], refs[2 * n + 1]):
            cp.wait_send()
            cp.wait_recv()

    outs = pl.pallas_call(
        body, name=name,
        out_shape=[pltpu.HBM(p.shape, p.dtype) for p in parts] + [pltpu.HBM(z.shape, z.dtype) for z in lands],
        in_specs=[HBM] * (2 * n) + [SEM, SEM] + [ANY] * len(after), out_specs=[HBM] * (2 * n),
        input_output_aliases={i: i for i in range(2 * n)},
        compiler_params=pltpu.CompilerParams(has_side_effects=pltpu.SideEffectType.DATAFLOW_SIDE_EFFECTING),
    )(*parts, *lands, send_sems, recv_sems, *after)
    return list(outs[:n]), list(outs[n:])


def _sum_chips(name, owns, recvs, chip, after):
    n = len(owns)
    hc = owns[0].shape[2]
    assert all(o.shape[2] == hc for o in owns)

    def body(chip_ref, *refs):
        for own_ref, recv_ref, out_ref in zip(refs[:n], refs[n:2 * n], refs[2 * n + 1:]):
            acc = own_ref[...].astype(F32)
            for j in range(3):
                acc = acc + recv_ref[j].astype(F32)
            out_ref[...] = acc

    return pl.pallas_call(
        body, name=name,
        grid_spec=pltpu.PrefetchScalarGridSpec(
            num_scalar_prefetch=1, grid=(hc // LANES,),
            in_specs=[pl.BlockSpec((None, o.shape[1], LANES), lambda i, chip_ref: (chip_ref[0], 0, i)) for o in owns]
            + [pl.BlockSpec((3, o.shape[1], LANES), lambda i, chip_ref: (0, 0, i)) for o in owns]
            + [pl.BlockSpec((8, LANES), lambda i, chip_ref: (0, 0))],
            out_specs=[pl.BlockSpec((o.shape[1], LANES), lambda i, chip_ref: (0, i)) for o in owns]),
        out_shape=[jax.ShapeDtypeStruct((o.shape[1], hc), F32) for o in owns],
        compiler_params=_params(("arbitrary",)),
    )(chip, *owns, *recvs, after)


def _share_halves(name, halves):
    n = len(halves)

    def body(*refs):
        srcs, dsts = refs[:n], refs[n:2 * n]
        send_sems, recv_sems = refs[2 * n:]
        x, y, c, _ = _place()
        copies = [pltpu.make_async_remote_copy(
            src_ref=srcs[a], dst_ref=dsts[a], send_sem=send_sems.at[a], recv_sem=recv_sems.at[a],
            device_id=(x, y, 1 - c), device_id_type=MESH) for a in range(n)]
        for cp in copies:
            cp.start()
        for cp in copies:
            cp.wait()

    return pl.pallas_call(
        body, name=name,
        in_specs=[ANY] * n, out_specs=[ANY] * n,
        out_shape=[jax.ShapeDtypeStruct(s.shape, s.dtype) for s in halves],
        scratch_shapes=[pltpu.SemaphoreType.DMA((n,)), pltpu.SemaphoreType.DMA((n,))],
    )(*halves)


def _small_gather_comm(part):
    def copies(ins, outs, sems):
        x, y, c, _ = _place()
        me = 4 * x + 2 * y + c
        both = []
        for d in range(1, N_DEV):
            px, py, pc = (1 - x if d & 4 else x, 1 - y if d & 2 else y, 1 - c if d & 1 else c)
            send = pltpu.make_async_remote_copy(
                src_ref=ins[0], dst_ref=outs[0].at[me], send_sem=sems[0].at[d - 1], recv_sem=sems[1].at[d - 1],
                device_id=(px, py, pc), device_id_type=MESH)
            recv = pltpu.make_async_remote_copy(
                src_ref=ins[0], dst_ref=outs[0].at[4 * px + 2 * py + pc], send_sem=sems[0].at[d - 1],
                recv_sem=sems[1].at[d - 1], device_id=(px, py, pc), device_id_type=MESH)
            both.append((send, recv))
        return both

    def start(ins, outs, sems):
        for send, _ in copies(ins, outs, sems):
            send.start()

    def finish(ins, outs, sems):
        for send, recv in copies(ins, outs, sems):
            recv.wait_recv()
            send.wait_send()

    return _Comm([part], [jax.ShapeDtypeStruct((N_DEV,) + part.shape, F32)],
                 [pltpu.SemaphoreType.DMA((N_DEV - 1,))] * 2, start, finish)


def _sum_devices(parts):
    def body(p_ref, out_ref):
        acc = p_ref[0]
        for k in range(1, N_DEV):
            acc = acc + p_ref[k]
        out_ref[...] = acc

    return pl.pallas_call(
        body, name="sum_devices", grid=(1,),
        in_specs=[pl.BlockSpec(parts.shape, lambda i: (0, 0, 0))],
        out_specs=pl.BlockSpec(parts.shape[1:], lambda i: (0, 0)),
        out_shape=jax.ShapeDtypeStruct(parts.shape[1:], F32),
        compiler_params=_params(("arbitrary",)),
    )(parts)


def _adam_update(w, g, m, v):
    nm = ADAM_B1 * m + (1.0 - ADAM_B1) * g
    nv = ADAM_B2 * v + (1.0 - ADAM_B2) * (g * g)
    m_hat = nm * (1.0 / (1.0 - ADAM_B1 ** ADAM_STEP))
    v_hat = nv * (1.0 / (1.0 - ADAM_B2 ** ADAM_STEP))
    return -ADAM_LR * (m_hat / (jnp.sqrt(v_hat) + ADAM_EPS) + ADAM_WD * w), nm, nv


def _adamw(name, w, g, m, v):
    def body(w_ref, g_ref, m_ref, v_ref, d_ref, nm_ref, nv_ref):
        d_ref[...], nm_ref[...], nv_ref[...] = _adam_update(w_ref[...], g_ref[...], m_ref[...], v_ref[...])

    spec = pl.BlockSpec(w.shape, lambda i: (0, 0))
    out = jax.ShapeDtypeStruct(w.shape, F32)
    return pl.pallas_call(
        body, name=name, grid=(1,),
        in_specs=[spec] * 4, out_specs=[spec] * 3, out_shape=[out] * 3,
        compiler_params=_params(("arbitrary",)),
    )(w, g, m, v)


def _adamw_halves(name, ws, mines, theirs, ms, vs, core):
    n = len(ws)
    cols = ws[0].shape[1]
    assert all(w.shape[1] == cols for w in ws)
    hc = cols // 2
    tc = LANES if n > 1 else min(256, hc)
    nt = hc // tc

    def body(core_ref, *refs):
        ins, outs = refs[:5 * n], refs[5 * n:]
        for a in range(n):
            w_ref, mine_ref, theirs_ref, m_ref, v_ref = [ins[j * n + a] for j in range(5)]
            g_ref, d_ref, nm_ref, nv_ref = outs[4 * a:4 * a + 4]
            gv = jnp.where(pl.program_id(0) == core_ref[0], mine_ref[...], theirs_ref[...])
            g_ref[...] = gv
            d_ref[...], nm_ref[...], nv_ref[...] = _adam_update(w_ref[...], gv, m_ref[...], v_ref[...])

    whole = lambda w: pl.BlockSpec((w.shape[0], tc), lambda h, i, core_ref: (0, h * nt + i))
    mine_spec = lambda w: pl.BlockSpec((w.shape[0], tc), lambda h, i, core_ref: (0, jnp.where(h == core_ref[0], i, 0)))
    theirs_spec = lambda w: pl.BlockSpec((w.shape[0], tc), lambda h, i, core_ref: (0, jnp.where(h == core_ref[0], 0, i)))
    outs = pl.pallas_call(
        body, name=name,
        grid_spec=pltpu.PrefetchScalarGridSpec(
            num_scalar_prefetch=1, grid=(2, nt),
            in_specs=[whole(w) for w in ws] + [mine_spec(w) for w in ws] + [theirs_spec(w) for w in ws]
            + [whole(w) for w in ws] * 2,
            out_specs=[whole(w) for w in ws for _ in range(4)]),
        out_shape=[jax.ShapeDtypeStruct(w.shape, F32) for w in ws for _ in range(4)],
        compiler_params=_params(("arbitrary", "arbitrary")),
    )(core, *ws, *mines, *theirs, *ms, *vs)
    return [outs[4 * a:4 * a + 4] for a in range(n)]


WEIGHTS = ("ffn1_norm", "ffn1_gate", "ffn1_up", "ffn1_down", "mix_norm", "w_in", "b_forget", "conv_w",
           "w_o_attn", "w_o_conv", "w_out", "ffn2_norm", "ffn2_gate", "ffn2_up", "ffn2_down", "final_norm")
VEC_ROWS = 8


def _pack_small(t, conv_rows):
    conv = t["conv_w"]
    parts = [t[n].reshape(VEC_ROWS, LANES) for n in NORMS]
    parts.append(jnp.pad(conv, ((0, conv_rows - conv.shape[0]), (0, 0))))
    parts.append(jnp.pad(t["b_forget"].reshape(1, N_HEADS), ((0, 7), (0, LANES - N_HEADS))))
    return jnp.concatenate(parts, axis=0)


def _unpack_small(p, conv_rows):
    out = {n: p[VEC_ROWS * i:VEC_ROWS * (i + 1)].reshape(-1) for i, n in enumerate(NORMS)}
    base = VEC_ROWS * len(NORMS)
    out["conv_w"] = p[base:base + 3]
    out["b_forget"] = p[base + conv_rows, :N_HEADS]
    return out


def _travel(name, a):
    return a.T if name in TRANSPOSED else a


GATHER_FIRST = ("ffn1_gate", "ffn1_up")
GATHER_RIDES = {"ffn1_up": ("ffn1_down",), "ffn1_down": ("w_in",), "mix_proj_fwd": ("w_o_attn", "w_o_conv", "w_out"),
                "attn_fwd": ("ffn2_gate", "ffn2_up", "ffn2_down")}
SIBLING_RIDES = {"ffn2": "mix_out_bwd", "out": None, "w_in": "mix_proj_bwd_dx", "ffn1": None}
CHIP_RIDES = {"ffn2": "attn_bwd", "out": "attn_bwd", "w_in": "ffn1_bwd_dw", "ffn1": None}
SMALL_RIDE = "ffn1_bwd_dw"


class _MeshPlan:
    def __init__(self, wts, core):
        self.small, self.core = wts, core
        self.shards = {n: wts[n].astype(BF16) for n in BIG}
        self.chip_part, self.from_chips, self.rides = {}, {}, {}
        self.stacks = {}
        conv_shard = jnp.pad(wts["conv_w"], ((0, 8 - wts["conv_w"].shape[0]), (0, 0)))
        for kernel_name, names in GATHER_RIDES.items():
            mine = [self.shards[n] for n in names]
            conv = conv_shard if kernel_name == "ffn1_up" else None
            names = names + (("conv_w",) if conv is not None else ())
            mine = mine + ([conv] if conv is not None else [])
            self._ride(kernel_name, _gather_comm(mine[:len(mine) - (conv is not None)], conv),
                       lambda got, names=names, mine=mine: self.stacks.update(zip(names, _fill_own(got, mine))))

    def weights(self, group):
        return _LAYOUTS[group](self.stacks, self.small)

    def ffn1_up(self, x, tm):
        px, py = lax.axis_index("x"), lax.axis_index("y")
        order = jnp.stack([2 * px + py, 2 * (1 - px) + py, 2 * px + (1 - py), 2 * (1 - px) + (1 - py)]).astype(jnp.int32)
        own = [self.shards[n] for n in GATHER_FIRST]
        (hg, hu, n, sg, su), brought = _ffn_up_gather("ffn1_up", x, self.small["ffn1_norm"].reshape(1, -1), *own, order,
                                                     tm, self.rider("ffn1_up"))
        self.stacks.update(zip(GATHER_FIRST, _fill_own([sg, su], own)))
        self.arrived("ffn1_up", brought)
        return hg, hu, n

    def _ride(self, kernel_name, comm, then):
        self.rides.setdefault(kernel_name, []).append((comm, then))

    def rider(self, kernel_name):
        comms = [comm for comm, _ in self.rides.get(kernel_name, [])]
        return _merge_comms(comms) if comms else None

    def arrived(self, kernel_name, results):
        for comm, then in self.rides.pop(kernel_name, []):
            then(results[:len(comm.out_shape)])
            results = results[len(comm.out_shape):]

    def reduce(self, group, grads):
        names = tuple(grads)
        mine = [grads[n] for n in names]

        def with_sibling(from_sibling):
            parts = _add_halves("add_halves_" + group, mine, list(from_sibling), self.core)
            self.chip_part.update(zip(names, parts))
            if CHIP_RIDES[group] is None:
                self.last = (names, _exchange_start("exchange_start_" + group, parts))
            else:
                self._ride(CHIP_RIDES[group], _chip_exchange_comm(parts),
                           lambda got: self.from_chips.update(zip(names, got)))

        if SIBLING_RIDES[group] is None:
            with_sibling(_run_comm("sibling_exchange_" + group, _sibling_exchange_comm(mine)))
        else:
            self._ride(SIBLING_RIDES[group], _sibling_exchange_comm(mine), with_sibling)

    def reduce_small(self, gs, loss):
        conv_all = _shard_cols(gs["conv_w"]).reshape(N_CHIPS * 8, LANES)
        part = _pack_small({**{n: gs[n] for n in NORMS}, "conv_w": conv_all, "b_forget": gs["b_forget"][0, :N_HEADS]},
                           N_CHIPS * 8)
        part = jnp.concatenate([part, jnp.broadcast_to(loss, (8, LANES))], axis=0)
        me = 4 * lax.axis_index("x") + 2 * lax.axis_index("y") + lax.axis_index("c")

        def landed(got):
            self.small_parts = lax.dynamic_update_index_in_dim(got[0], part, me, 0)

        self._ride(SMALL_RIDE, _small_gather_comm(part), landed)


def kernel(x, ffn1_norm, ffn1_gate, ffn1_up, ffn1_down, mix_norm, w_in, b_forget, conv_w, w_o_attn, w_o_conv, w_out, ffn2_norm, ffn2_gate, ffn2_up, ffn2_down, final_norm, loss_target, m_ffn1_norm, m_ffn1_gate, m_ffn1_up, m_ffn1_down, m_mix_norm, m_w_in, m_b_forget, m_conv_w, m_w_o_attn, m_w_o_conv, m_w_out, m_ffn2_norm, m_ffn2_gate, m_ffn2_up, m_ffn2_down, m_final_norm, v_ffn1_norm, v_ffn1_gate, v_ffn1_up, v_ffn1_down, v_mix_norm, v_w_in, v_b_forget, v_conv_w, v_w_o_attn, v_w_o_conv, v_w_out, v_ffn2_norm, v_ffn2_gate, v_ffn2_up, v_ffn2_down, v_final_norm):
    given = dict(locals())
    wts = {n: _travel(n, given[n]) for n in WEIGHTS}
    mom = {n: _travel(n, given["m_" + n]) for n in WEIGHTS}
    var = {n: _travel(n, given["v_" + n]) for n in WEIGHTS}
    B, S, D = x.shape
    chip = 2 * lax.axis_index("x") + lax.axis_index("y")
    chip1 = chip.astype(jnp.int32).reshape(1)
    core = lax.axis_index("c").astype(jnp.int32).reshape(1)

    plan = _MeshPlan(wts, core)
    loss, grad_x, gs = _local_step(x.reshape(B * S, D), loss_target.reshape(B * S, D), plan, B, S)

    last_names, (send_sems, recv_sems, parts_thru, lands, token) = plan.last
    delta, new_m, new_v, grads = {}, {}, {}, {}

    def finish(tag, names):
        by_cols = {}
        for n in names:
            by_cols.setdefault(wts[n].shape[1], []).append(n)
        mine = {}
        for cols, ns in by_cols.items():
            mine.update(zip(ns, _sum_chips("sum_chips_%s_%d" % (tag, cols), [plan.chip_part[n] for n in ns],
                                           [plan.from_chips[n] for n in ns], chip1, token)))
        theirs = dict(zip(names, _share_halves("share_halves_" + tag, [mine[n] for n in names])))
        raw = []
        for cols, ns in by_cols.items():
            outs = _adamw_halves("adamw_%s_%d" % (tag, cols), [wts[n] for n in ns], [mine[n] for n in ns],
                                 [theirs[n] for n in ns], [mom[n] for n in ns], [var[n] for n in ns], core)
            for n, per in zip(ns, outs):
                raw.append(per[-1])
                grads[n], delta[n], new_m[n], new_v[n] = [_travel(n, o) for o in per]
        return raw

    small_sum = _sum_devices(plan.small_parts)
    base = VEC_ROWS * len(NORMS)
    loss_row = small_sum.shape[0] - 8
    small_grads = _unpack_small(small_sum, N_CHIPS * 8)
    small_grads["conv_w"] = lax.dynamic_slice_in_dim(small_sum[base:base + N_CHIPS * 8], chip * 8, 8, axis=0)[:3]
    packs = [_pack_small(t, 8) for t in (wts, small_grads, mom, var)]
    small_out = _adamw("adamw_small", *packs)

    done = finish("early", [n for n in BIG if n not in last_names])
    parts_back, got = _exchange_wait("exchange_wait", send_sems, recv_sems, parts_thru, lands, done + list(small_out))
    plan.chip_part.update(zip(last_names, parts_back))
    plan.from_chips.update(zip(last_names, got))
    finish("last", last_names)
    grads.update(small_grads)
    for out, p in zip((delta, new_m, new_v), small_out):
        out.update(_unpack_small(p, 8))

    return (small_sum[loss_row, 0], grad_x.reshape(B, S, D), *[grads[n] for n in WEIGHTS], *[delta[n] for n in WEIGHTS],
            *[new_m[n] for n in WEIGHTS], *[new_v[n] for n in WEIGHTS])
```

```python
import functools
import math

import jax
import jax.numpy as jnp
from jax import lax
from jax.experimental import pallas as pl
from jax.experimental.pallas import tpu as pltpu

F32 = jnp.float32
BF16 = jnp.bfloat16
MESH = pl.DeviceIdType.MESH

N_CHIPS = 4
N_DEV = 8
N_HEADS = 8
HEAD_DIM = 64
HEAD_PAIRS = N_HEADS // 2
ATTN_W = N_HEADS * HEAD_DIM
CONV_W = 512
RMS_EPS = 1e-6
FFN_RES = 0.5
LANES = 128
VMEM_LIMIT = 56 * 1024 * 1024
ROW_BLOCK = 256

ADAM_LR = 0.001
ADAM_B1 = 0.9
ADAM_B2 = 0.999
ADAM_EPS = 1e-08
ADAM_WD = 0.01
ADAM_STEP = 10

PROJ_W = 3 * ATTN_W + 3 * CONV_W + 2 * 1024
COL_CB, COL_CC, COL_CX = 3 * ATTN_W, 3 * ATTN_W + CONV_W, 3 * ATTN_W + 2 * CONV_W
COL_GATES = 3 * ATTN_W + 3 * CONV_W
N_FORGET_COL = 3 * ATTN_W


def _params(sem=None, vmem=VMEM_LIMIT):
    return pltpu.CompilerParams(dimension_semantics=sem, vmem_limit_bytes=vmem)


def _dot(a, b):
    return lax.dot_general(a, b, (((1,), (0,)), ((), ())), preferred_element_type=F32)


def _dot_nt(a, b):
    return lax.dot_general(a, b, (((1,), (1,)), ((), ())), preferred_element_type=F32)


def _dot_tn(a, b):
    return lax.dot_general(a, b, (((0,), (0,)), ((), ())), preferred_element_type=F32)


def _sigmoid(x):
    return 1.0 / (1.0 + jnp.exp(-x))


def _rms(xv):
    inv = lax.rsqrt(jnp.mean(xv * xv, axis=-1, keepdims=True) + RMS_EPS)
    return xv * inv, inv


class _Comm:
    def __init__(self, inputs, out_shape, scratch, start, finish):
        self.inputs, self.out_shape, self.scratch = list(inputs), list(out_shape), list(scratch)
        self.start, self.finish = start, finish


def _pallas(body, name, grid, in_specs, out_specs, out_shape, scratch, args, comm=None):
    sem = ("arbitrary",) * len(grid)
    if comm is None:
        outs = pl.pallas_call(body, name=name, grid=grid, in_specs=in_specs, out_specs=out_specs,
                              out_shape=out_shape, scratch_shapes=scratch, compiler_params=_params(sem))(*args)
        return list(outs), []
    n_in, n_out, n_scr = len(in_specs), len(out_specs), len(scratch)
    ci, co = len(comm.inputs), len(comm.out_shape)

    def riding(*refs):
        ins, refs = refs[:n_in], refs[n_in:]
        cins, refs = refs[:ci], refs[ci:]
        outs, refs = refs[:n_out], refs[n_out:]
        couts, refs = refs[:co], refs[co:]
        scr, sems = refs[:n_scr], refs[n_scr:]
        ids = [pl.program_id(d) for d in range(len(grid))]
        first = functools.reduce(lambda a, b: a & b, [i == 0 for i in ids])
        last = functools.reduce(lambda a, b: a & b, [i == g - 1 for i, g in zip(ids, grid)])

        @pl.when(first)
        def _():
            comm.start(cins, couts, sems)

        body(*ins, *outs, *scr)

        @pl.when(last)
        def _():
            comm.finish(cins, couts, sems)

    any_spec = pl.BlockSpec(memory_space=pl.ANY)
    outs = pl.pallas_call(
        riding, name=name, grid=grid,
        in_specs=list(in_specs) + [any_spec] * ci, out_specs=list(out_specs) + [any_spec] * co,
        out_shape=list(out_shape) + comm.out_shape, scratch_shapes=list(scratch) + comm.scratch,
        compiler_params=_params(sem))(*args, *comm.inputs)
    return list(outs[:n_out]), list(outs[n_out:])


def _rms_bwd(dn, xhat, inv, g):
    dxhat = dn * g
    dx = inv * (dxhat - xhat * jnp.mean(dxhat * xhat, axis=-1, keepdims=True))
    return dx, jnp.sum(dn * xhat, axis=0, keepdims=True)


def _ffn_fwd_loss(name, x, g, wgt, wut, wd, target, gf, tm):
    T, D = x.shape
    K, Fs, _ = wgt.shape

    def body(x_ref, g_ref, wg_ref, wu_ref, wd_ref, t_ref, gf_ref,
             dx_ref, hg_ref, hu_ref, n_ref, loss_ref, dgf_ref, acc_scr):
        i, k = pl.program_id(0), pl.program_id(1)

        @pl.when(k == 0)
        def _():
            xhat, _ = _rms(x_ref[...])
            n_ref[...] = (xhat * g_ref[...]).astype(BF16)
            acc_scr[...] = jnp.zeros_like(acc_scr)

        @pl.when((k == 0) & (i == 0))
        def _():
            loss_ref[...] = jnp.zeros_like(loss_ref)
            dgf_ref[...] = jnp.zeros_like(dgf_ref)

        n = n_ref[...]
        hg = _dot_nt(n, wg_ref[...])
        hu = _dot_nt(n, wu_ref[...])
        hg_ref[...] = hg.astype(BF16)
        hu_ref[...] = hu.astype(BF16)
        act = (hg * _sigmoid(hg) * hu).astype(BF16)
        acc_scr[...] += _dot(act, wd_ref[...])

        @pl.when(k == K - 1)
        def _():
            gfv = gf_ref[...]
            for r0 in range(0, tm, ROW_BLOCK):
                rows = slice(r0, r0 + ROW_BLOCK)
                xhat, inv = _rms(x_ref[rows, :] + FFN_RES * acc_scr[rows, :])
                err = xhat * gfv - t_ref[rows, :]
                loss_ref[...] += 0.5 * jnp.sum(jnp.sum(err * err, axis=1, keepdims=True), axis=0, keepdims=True) / D
                dx, dg = _rms_bwd(err * (1.0 / D), xhat, inv, gfv)
                dx_ref[rows, :] = dx
                dgf_ref[...] += dg

    w_spec = pl.BlockSpec((None, Fs, D), lambda i, k: (k, 0, 0))
    act_spec = pl.BlockSpec((None, tm, Fs), lambda i, k: (k, i, 0))
    row = pl.BlockSpec((tm, D), lambda i, k: (i, 0))
    vec = pl.BlockSpec((1, D), lambda i, k: (0, 0))
    return _pallas(
        body, name, (T // tm, K),
        [row, vec, w_spec, w_spec, w_spec, row, vec],
        [row, act_spec, act_spec, row, pl.BlockSpec((1, LANES), lambda i, k: (0, 0)), vec],
        [jax.ShapeDtypeStruct((T, D), F32), jax.ShapeDtypeStruct((K, T, Fs), BF16),
         jax.ShapeDtypeStruct((K, T, Fs), BF16), jax.ShapeDtypeStruct((T, D), BF16),
         jax.ShapeDtypeStruct((1, LANES), F32), jax.ShapeDtypeStruct((1, D), F32)],
        [pltpu.VMEM((tm, D), F32)],
        (x, g, wgt, wut, wd, target, gf))[0]


def _ffn_up_gather(name, x, g, wg_own, wu_own, order, tm, comm=None):
    T, D = x.shape
    Fs = wg_own.shape[0]
    nt = T // tm
    ci, co = (len(comm.inputs), len(comm.out_shape)) if comm is not None else (0, 0)

    def body(order_ref, x_ref, g_ref, wgo_ref, wuo_ref, *rest):
        cins, rest = rest[:ci], rest[ci:]
        (hg_ref, hu_ref, n_ref, sg_ref, su_ref), rest = rest[:5], rest[5:]
        couts, rest = rest[:co], rest[co:]
        (n_all, wbuf, send_sems, recv_sems, pass_send, pass_recv, load_sems), csems = rest[:7], rest[7:]
        k, i = pl.program_id(0), pl.program_id(1)
        x_pos, y_pos, c, others = _place()
        me = 2 * x_pos + y_pos
        owns, stacks = (wgo_ref, wuo_ref), (sg_ref, su_ref)
        mine, theirs = _col_halves(D, c)

        def chip_copy(a, j, chip):
            return pltpu.make_async_remote_copy(
                src_ref=owns[a].at[:, mine], dst_ref=stacks[a].at[chip, :, mine],
                send_sem=send_sems.at[3 * a + j], recv_sem=recv_sems.at[3 * a + j],
                device_id=(*others[j], c), device_id_type=MESH)

        def pass_copy(a, j, chip, half):
            return pltpu.make_async_remote_copy(
                src_ref=stacks[a].at[chip, :, half], dst_ref=stacks[a].at[chip, :, half],
                send_sem=pass_send.at[3 * a + j], recv_sem=pass_recv.at[3 * a + j],
                device_id=(x_pos, y_pos, 1 - c), device_id_type=MESH)

        @pl.when((k == 0) & (i == 0))
        def _():
            for a in range(2):
                for j in range(3):
                    chip_copy(a, j, me).start()
            if comm is not None:
                comm.start(cins, couts, csems)

        def bring(j):
            ox, oy = others[j]
            chip = 2 * ox + oy
            for a in range(2):
                chip_copy(a, j, chip).wait_recv()
            for a in range(2):
                pass_copy(a, j, chip, mine).start()
            for a in range(2):
                pass_copy(a, j, chip, theirs).wait_recv()
            loads = [pltpu.make_async_copy(stacks[a].at[chip], wbuf.at[j % 2, a], load_sems.at[2 * (j % 2) + a])
                     for a in range(2)]
            for cp in loads:
                cp.start()
            for cp in loads:
                cp.wait()

        @pl.when((k == 1) & (i == 0))
        def _():
            bring(0)
            bring(1)

        @pl.when((k == 2) & (i == nt - 1))
        def _():
            bring(2)

        rows = pl.ds(pl.multiple_of(i * tm, tm), tm)

        @pl.when(k == 0)
        def _():
            xhat, _ = _rms(x_ref[...])
            n = (xhat * g_ref[...]).astype(BF16)
            n_ref[...] = n
            n_all[rows, :] = n
            hg_ref[...] = _dot_nt(n, wgo_ref[...]).astype(BF16)
            hu_ref[...] = _dot_nt(n, wuo_ref[...]).astype(BF16)

        @pl.when(k > 0)
        def _():
            n = n_all[rows, :]
            slot = (k - 1) % 2
            hg_ref[...] = _dot_nt(n, wbuf[slot, 0]).astype(BF16)
            hu_ref[...] = _dot_nt(n, wbuf[slot, 1]).astype(BF16)

        @pl.when((k == N_CHIPS - 1) & (i == nt - 1))
        def _():
            for a in range(2):
                for j, (ox, oy) in enumerate(others):
                    chip_copy(a, j, me).wait_send()
                    pass_copy(a, j, 2 * ox + oy, mine).wait_send()
            if comm is not None:
                comm.finish(cins, couts, csems)

    any_spec = pl.BlockSpec(memory_space=pl.ANY)
    first_pass = lambda k, i, order_ref: (jnp.where(k == 0, i, nt - 1), 0)
    whole = pl.BlockSpec((Fs, D), lambda k, i, order_ref: (0, 0))
    act_spec = pl.BlockSpec((None, tm, Fs), lambda k, i, order_ref: (order_ref[k], i, 0))
    stack = jax.ShapeDtypeStruct((N_CHIPS, Fs, D), BF16)
    outs = pl.pallas_call(
        body, name=name,
        grid_spec=pltpu.PrefetchScalarGridSpec(
            num_scalar_prefetch=1, grid=(N_CHIPS, nt),
            in_specs=[pl.BlockSpec((tm, D), first_pass), pl.BlockSpec((1, D), lambda k, i, order_ref: (0, 0)),
                      whole, whole] + [any_spec] * ci,
            out_specs=[act_spec, act_spec, pl.BlockSpec((tm, D), first_pass), any_spec, any_spec] + [any_spec] * co,
            scratch_shapes=[pltpu.VMEM((T, D), BF16), pltpu.VMEM((2, 2, Fs, D), BF16)]
            + [pltpu.SemaphoreType.DMA((6,))] * 4 + [pltpu.SemaphoreType.DMA((4,))]
            + (comm.scratch if comm is not None else [])),
        out_shape=[jax.ShapeDtypeStruct((N_CHIPS, T, Fs), BF16), jax.ShapeDtypeStruct((N_CHIPS, T, Fs), BF16),
                   jax.ShapeDtypeStruct((T, D), BF16), stack, stack] + (comm.out_shape if comm is not None else []),
        compiler_params=_params(("arbitrary", "arbitrary")),
    )(order, x, g, wg_own, wu_own, *(comm.inputs if comm is not None else []))
    return list(outs[:5]), list(outs[5:])


def _ffn_down(name, x, hg, hu, wd, tm, comm=None):
    T, D = x.shape
    K, Fs, _ = wd.shape

    def body(x_ref, hg_ref, hu_ref, wd_ref, out_ref, acc_scr):
        k = pl.program_id(1)

        @pl.when(k == 0)
        def _():
            acc_scr[...] = jnp.zeros_like(acc_scr)

        hgv = hg_ref[...].astype(F32)
        act = (hgv * _sigmoid(hgv) * hu_ref[...].astype(F32)).astype(BF16)
        acc_scr[...] += _dot(act, wd_ref[...])

        @pl.when(k == K - 1)
        def _():
            out_ref[...] = x_ref[...] + FFN_RES * acc_scr[...]

    act_spec = pl.BlockSpec((None, tm, Fs), lambda i, k: (k, i, 0))
    row = pl.BlockSpec((tm, D), lambda i, k: (i, 0))
    return _pallas(
        body, name, (T // tm, K),
        [row, act_spec, act_spec, pl.BlockSpec((None, Fs, D), lambda i, k: (k, 0, 0))],
        [row], [jax.ShapeDtypeStruct((T, D), F32)], [pltpu.VMEM((tm, D), F32)],
        (x, hg, hu, wd), comm)


def _ffn_bwd_dx(name, dout, x, g, hg, hu, wgt, wut, wd, tm, comm=None):
    T, D = x.shape
    K, Fs, _ = wgt.shape

    def body(dout_ref, x_ref, g_ref, hg_ref, hu_ref, wg_ref, wu_ref, wd_ref,
             dx_ref, dhg_ref, dhu_ref, dg_ref, df_ref, dn_scr):
        i, k = pl.program_id(0), pl.program_id(1)

        @pl.when(k == 0)
        def _():
            df_ref[...] = (FFN_RES * dout_ref[...]).astype(BF16)
            dn_scr[...] = jnp.zeros_like(dn_scr)

        @pl.when((k == 0) & (i == 0))
        def _():
            dg_ref[...] = jnp.zeros_like(dg_ref)

        for r0 in range(0, tm, ROW_BLOCK):
            rows = slice(r0, r0 + ROW_BLOCK)
            dact = _dot_nt(df_ref[rows, :], wd_ref[...])
            hgv = hg_ref[rows, :].astype(F32)
            huv = hu_ref[rows, :].astype(F32)
            s = _sigmoid(hgv)
            dhu = (dact * (hgv * s)).astype(BF16)
            dhg = (dact * huv * (s * (1.0 + hgv * (1.0 - s)))).astype(BF16)
            dhg_ref[rows, :] = dhg
            dhu_ref[rows, :] = dhu
            dn_scr[rows, :] += _dot(dhg, wg_ref[...]) + _dot(dhu, wu_ref[...])

        @pl.when(k == K - 1)
        def _():
            xhat, inv = _rms(x_ref[...])
            dx, dg = _rms_bwd(dn_scr[...], xhat, inv, g_ref[...])
            dx_ref[...] = dout_ref[...] + dx
            dg_ref[...] += dg

    w_spec = pl.BlockSpec((None, Fs, D), lambda i, k: (k, 0, 0))
    act_spec = pl.BlockSpec((None, tm, Fs), lambda i, k: (k, i, 0))
    row = pl.BlockSpec((tm, D), lambda i, k: (i, 0))
    row_once = pl.BlockSpec((tm, D), lambda i, k: (i, 0), pipeline_mode=pl.Buffered(1))
    vec = pl.BlockSpec((1, D), lambda i, k: (0, 0))
    return _pallas(
        body, name, (T // tm, K),
        [row, row_once, vec, act_spec, act_spec, w_spec, w_spec, w_spec],
        [row_once, act_spec, act_spec, vec, row],
        [jax.ShapeDtypeStruct((T, D), F32), jax.ShapeDtypeStruct((K, T, Fs), BF16),
         jax.ShapeDtypeStruct((K, T, Fs), BF16), jax.ShapeDtypeStruct((1, D), F32),
         jax.ShapeDtypeStruct((T, D), BF16)],
        [pltpu.VMEM((tm, D), F32)],
        (dout, x, g, hg, hu, wgt, wut, wd), comm)


def _ffn_bwd_dw(name, n, df, hg, hu, dhg, dhu, tk, comm=None):
    T, D = n.shape
    K, _, Fs = hg.shape
    nt = T // tk

    def body(n_ref, df_ref, hg_ref, hu_ref, dhg_ref, dhu_ref, dwg_ref, dwu_ref, dwd_ref, accg, accu, accd):
        t = pl.program_id(1)

        @pl.when(t == 0)
        def _():
            accg[...] = jnp.zeros_like(accg)
            accu[...] = jnp.zeros_like(accu)
            accd[...] = jnp.zeros_like(accd)

        nv = n_ref[...]
        hgv = hg_ref[...].astype(F32)
        act = (hgv * _sigmoid(hgv) * hu_ref[...].astype(F32)).astype(BF16)
        accg[...] += _dot_tn(dhg_ref[...], nv)
        accu[...] += _dot_tn(dhu_ref[...], nv)
        accd[...] += _dot_tn(act, df_ref[...])

        @pl.when(t == nt - 1)
        def _():
            dwg_ref[...] = accg[...].astype(BF16)
            dwu_ref[...] = accu[...].astype(BF16)
            dwd_ref[...] = accd[...].astype(BF16)

    act_spec = pl.BlockSpec((None, tk, Fs), lambda k, t: (k, t, 0))
    w_spec = pl.BlockSpec((None, Fs, D), lambda k, t: (k, 0, 0))
    row = pl.BlockSpec((tk, D), lambda k, t: (t, 0))
    return _pallas(
        body, name, (K, nt),
        [row, row, act_spec, act_spec, act_spec, act_spec],
        [w_spec, w_spec, w_spec],
        [jax.ShapeDtypeStruct((K, Fs, D), BF16)] * 3,
        [pltpu.VMEM((Fs, D), F32)] * 3,
        (n, df, hg, hu, dhg, dhu), comm)


def _ffn_bwd_dw_down(name, dout, hg, hu, tk, comm=None):
    T, D = dout.shape
    K, _, Fs = hg.shape
    nt = T // tk

    def body(dout_ref, hg_ref, hu_ref, dwd_ref, accd):
        t = pl.program_id(1)

        @pl.when(t == 0)
        def _():
            accd[...] = jnp.zeros_like(accd)

        hgv = hg_ref[...].astype(F32)
        act = (hgv * _sigmoid(hgv) * hu_ref[...].astype(F32)).astype(BF16)
        accd[...] += _dot_tn(act, (FFN_RES * dout_ref[...]).astype(BF16))

        @pl.when(t == nt - 1)
        def _():
            dwd_ref[...] = accd[...].astype(BF16)

    act_spec = pl.BlockSpec((None, tk, Fs), lambda k, t: (k, t, 0))
    return _pallas(
        body, name, (K, nt),
        [pl.BlockSpec((tk, D), lambda k, t: (t, 0)), act_spec, act_spec],
        [pl.BlockSpec((None, Fs, D), lambda k, t: (k, 0, 0))],
        [jax.ShapeDtypeStruct((K, Fs, D), BF16)], [pltpu.VMEM((Fs, D), F32)],
        (dout, hg, hu), comm)


def _ffn_bwd_dw_in(name, n, dhg, dhu, tk, comm=None):
    T, D = n.shape
    K, _, Fs = dhg.shape
    nt = T // tk

    def body(n_ref, dhg_ref, dhu_ref, dwg_ref, dwu_ref, accg, accu):
        t = pl.program_id(1)

        @pl.when(t == 0)
        def _():
            accg[...] = jnp.zeros_like(accg)
            accu[...] = jnp.zeros_like(accu)

        nv = n_ref[...]
        accg[...] += _dot_tn(dhg_ref[...], nv)
        accu[...] += _dot_tn(dhu_ref[...], nv)

        @pl.when(t == nt - 1)
        def _():
            dwg_ref[...] = accg[...].astype(BF16)
            dwu_ref[...] = accu[...].astype(BF16)

    act_spec = pl.BlockSpec((None, tk, Fs), lambda k, t: (k, t, 0))
    w_spec = pl.BlockSpec((None, Fs, D), lambda k, t: (k, 0, 0))
    return _pallas(
        body, name, (K, nt),
        [pl.BlockSpec((tk, D), lambda k, t: (t, 0)), act_spec, act_spec],
        [w_spec, w_spec],
        [jax.ShapeDtypeStruct((K, Fs, D), BF16)] * 2, [pltpu.VMEM((Fs, D), F32)] * 2,
        (n, dhg, dhu), comm)


def _mix_proj_fwd(x, g, wproj_t, wf_t, tm, tn, comm=None):
    T, D = x.shape
    N = wproj_t.shape[0]

    def body(x_ref, g_ref, w_ref, wf_ref, h_ref, proj_ref, flog_ref, h_scr):
        @pl.when(pl.program_id(1) == 0)
        def _():
            xhat, _ = _rms(x_ref[...])
            h = (xhat * g_ref[...]).astype(BF16)
            h_scr[...] = h
            h_ref[...] = h
            flog_ref[...] = _dot_nt(h, wf_ref[...])

        proj_ref[...] = _dot_nt(h_scr[...], w_ref[...]).astype(BF16)

    return _pallas(
        body, "mix_proj_fwd", (T // tm, N // tn),
        [pl.BlockSpec((tm, D), lambda i, n: (i, 0)), pl.BlockSpec((1, D), lambda i, n: (0, 0)),
         pl.BlockSpec((tn, D), lambda i, n: (n, 0)), pl.BlockSpec((LANES, D), lambda i, n: (0, 0))],
        [pl.BlockSpec((tm, D), lambda i, n: (i, 0)), pl.BlockSpec((tm, tn), lambda i, n: (i, n)),
         pl.BlockSpec((tm, LANES), lambda i, n: (i, 0))],
        [jax.ShapeDtypeStruct((T, D), BF16), jax.ShapeDtypeStruct((T, N), BF16),
         jax.ShapeDtypeStruct((T, LANES), F32)],
        [pltpu.VMEM((tm, D), BF16)],
        (x, g, wproj_t, wf_t), comm)


def _log_sigmoid(z):
    return -(jnp.maximum(-z, 0.0) + jnp.log(1.0 + jnp.exp(-jnp.abs(z))))


def _tri(n, lower):
    r = lax.broadcasted_iota(jnp.int32, (n, n), 0)
    c = lax.broadcasted_iota(jnp.int32, (n, n), 1)
    return jnp.where((r >= c) if lower else (r <= c), 1.0, 0.0).astype(F32)


def _dot_f32(a, b):
    return lax.dot_general(a, b, (((1,), (0,)), ((), ())), preferred_element_type=F32,
                           precision=lax.Precision.HIGHEST)


def _fgate_fwd(flog, bias, B, S, ch):
    def body(flog_ref, b_ref, cum_ref):
        tri = _tri(ch, True)
        carry = jnp.zeros((1, LANES), F32)
        for c0 in range(0, S, ch):
            lf = _log_sigmoid(flog_ref[c0:c0 + ch, :] + b_ref[...])
            cs = _dot_f32(tri, lf) + carry
            cum_ref[c0:c0 + ch, :] = cs
            carry = cs[ch - 1:ch, :]

    return pl.pallas_call(
        body, name="fgate_fwd", grid=(B,),
        in_specs=[pl.BlockSpec((S, LANES), lambda b: (b, 0)),
                  pl.BlockSpec((1, LANES), lambda b: (0, 0))],
        out_specs=pl.BlockSpec((S, LANES), lambda b: (b, 0)),
        out_shape=jax.ShapeDtypeStruct((B * S, LANES), F32),
        compiler_params=_params(("arbitrary",)),
    )(flog, bias)


def _fgate_bwd(dcum, flog, bias, B, S, ch):
    def body(dcum_ref, flog_ref, b_ref, dflog_ref, db_ref):
        @pl.when(pl.program_id(0) == 0)
        def _():
            db_ref[...] = jnp.zeros_like(db_ref)

        tri = _tri(ch, False)
        carry = jnp.zeros((1, LANES), F32)
        db = jnp.zeros((1, LANES), F32)
        for c0 in range(S - ch, -1, -ch):
            dlf = _dot_f32(tri, dcum_ref[c0:c0 + ch, :]) + carry
            carry = dlf[0:1, :]
            z = flog_ref[c0:c0 + ch, :] + b_ref[...]
            dz = dlf * _sigmoid(-z)
            dflog_ref[c0:c0 + ch, :] = dz
            db = db + jnp.sum(dz, axis=0, keepdims=True)
        db_ref[...] += db

    return pl.pallas_call(
        body, name="fgate_bwd", grid=(B,),
        in_specs=[pl.BlockSpec((S, LANES), lambda b: (b, 0)),
                  pl.BlockSpec((S, LANES), lambda b: (b, 0)),
                  pl.BlockSpec((1, LANES), lambda b: (0, 0))],
        out_specs=[pl.BlockSpec((S, LANES), lambda b: (b, 0)),
                   pl.BlockSpec((1, LANES), lambda b: (0, 0))],
        out_shape=[jax.ShapeDtypeStruct((B * S, LANES), F32),
                   jax.ShapeDtypeStruct((1, LANES), F32)],
        compiler_params=_params(("arbitrary",)),
    )(dcum, flog, bias)


def _pick_lane(tile, h):
    lane = lax.broadcasted_iota(jnp.int32, tile.shape, 1)
    return jnp.sum(jnp.where(lane == h, tile, 0.0), axis=1, keepdims=True)


def _put_lane(col, h, width=LANES):
    lane = lax.broadcasted_iota(jnp.int32, (col.shape[0], width), 1)
    return jnp.where(lane == h, col, 0.0)


def _pick_row(tile, h):
    row = lax.broadcasted_iota(jnp.int32, tile.shape, 0)
    return jnp.sum(jnp.where(row == h, tile, 0.0), axis=0, keepdims=True)


def _put_row(vec, h):
    row = lax.broadcasted_iota(jnp.int32, (8, vec.shape[1]), 0)
    return jnp.where(row == h, vec, 0.0)


def _causal(tq):
    r = lax.broadcasted_iota(jnp.int32, (tq, tq), 0)
    c = lax.broadcasted_iota(jnp.int32, (tq, tq), 1)
    return r >= c


def _head_halves(t):
    lo = lax.broadcasted_iota(jnp.int32, t.shape, 1) < HEAD_DIM
    zero = jnp.zeros_like(t)
    return jnp.where(lo, t, zero), jnp.where(lo, zero, t)


NEG = -1e30
ATTN_SCALE = 1.0 / math.sqrt(HEAD_DIM)


def _scaled(q):
    return (q.astype(F32) * ATTN_SCALE).astype(q.dtype)


def _attn_fwd(proj, cum, cum_t, B, S, tq, comm=None):
    nq = S // tq

    def body(q_ref, k_ref, v_ref, cum_ref, cumt_ref, o_ref, lse_ref):
        qi, hp = pl.program_id(1), pl.program_id(2)
        qm = _head_halves(_scaled(q_ref[...]))
        cumv = cum_ref[...]
        cq = [_pick_lane(cumv, 2 * hp + e) for e in range(2)]

        def tile(j, carry, masked):
            off = pl.multiple_of(j * tq, tq)
            kj = k_ref[pl.ds(off, tq), :]
            vj = v_ref[pl.ds(off, tq), :]
            ct = cumt_ref[j]
            new = []
            for e in range(2):
                m, l, acc = carry[e]
                s = _dot_nt(qm[e], kj) - _pick_row(ct, 2 * hp + e)
                if masked:
                    s = jnp.where(_causal(tq), s, NEG)
                m_new = jnp.maximum(m, jnp.max(s, axis=1, keepdims=True))
                p = jnp.exp(s - m_new)
                alpha = jnp.exp(m - m_new)
                l = alpha * l + jnp.sum(p, axis=1, keepdims=True)
                acc = alpha * acc + _dot(p.astype(BF16), vj)
                new.append((m_new, l, acc))
            return tuple(new)

        one = (jnp.full((tq, 1), NEG, F32), jnp.zeros((tq, 1), F32), jnp.zeros((tq, LANES), F32))
        carry = lax.fori_loop(0, qi, lambda j, c: tile(j, c, False), (one, one))
        (ma, la, acca), (mb, lb, accb) = tile(qi, carry, True)
        lo = lax.broadcasted_iota(jnp.int32, (tq, LANES), 1) < HEAD_DIM
        o_ref[...] = jnp.where(lo, acca / la, accb / lb).astype(BF16)

        @pl.when(hp == 0)
        def _():
            lse_ref[...] = jnp.zeros_like(lse_ref)

        lse_ref[...] += (_put_lane(ma + jnp.log(la) + cq[0], 2 * hp) + _put_lane(mb + jnp.log(lb) + cq[1], 2 * hp + 1))

    kv = lambda first: pl.BlockSpec((S, LANES), lambda b, i, hp: (b, first + hp))
    return _pallas(
        body, "attn_fwd", (B, nq, HEAD_PAIRS),
        [pl.BlockSpec((tq, LANES), lambda b, i, hp: (b * nq + i, hp)),
         kv(ATTN_W // LANES), kv(2 * ATTN_W // LANES),
         pl.BlockSpec((tq, LANES), lambda b, i, hp: (b * nq + i, 0)),
         pl.BlockSpec((None, nq, 8, tq), lambda b, i, hp: (b, 0, 0, 0))],
        [pl.BlockSpec((tq, LANES), lambda b, i, hp: (b * nq + i, hp)),
         pl.BlockSpec((tq, LANES), lambda b, i, hp: (b * nq + i, 0))],
        [jax.ShapeDtypeStruct((B * S, ATTN_W), BF16), jax.ShapeDtypeStruct((B * S, LANES), F32)],
        [], (proj, proj, proj, cum, cum_t), comm)


def _attn_bwd(proj, o, do, lse, cum, cum_t, B, S, tq, comm=None):
    nq = S // tq

    def body(q_ref, k_ref, v_ref, o_ref, do_ref, lse_ref, cum_ref, cumt_ref,
             dq_ref, dk_ref, dv_ref, dcq_ref, dck_ref, dq_scr):
        hp, kj = pl.program_id(1), pl.program_id(2)

        @pl.when(kj == 0)
        def _():
            dq_scr[...] = jnp.zeros_like(dq_scr)

        @pl.when((kj == 0) & (hp == 0))
        def _():
            dcq_ref[...] = jnp.zeros_like(dcq_ref)
            dck_ref[...] = jnp.zeros_like(dck_ref)

        kv = k_ref[...]
        vv = v_ref[...]
        km = _head_halves(kv)
        ct = cumt_ref[...]
        ck = [_pick_row(ct, 2 * hp + e) for e in range(2)]

        def tile(i, carry, masked):
            dk, dv, dcol = carry
            off = pl.multiple_of(i * tq, tq)
            qi = q_ref[pl.ds(off, tq), :]
            ov = o_ref[pl.ds(off, tq), :].astype(F32)
            qm = _head_halves(_scaled(qi))
            dom = _head_halves(do_ref[pl.ds(off, tq), :])
            cumv = cum_ref[pl.ds(off, tq), :]
            lsev = lse_ref[pl.ds(off, tq), :]
            dcq = jnp.zeros((tq, LANES), F32)
            dq = jnp.zeros((tq, LANES), F32)
            dcol_new = []
            for e in range(2):
                delta = jnp.sum(dom[e].astype(F32) * ov, axis=1, keepdims=True)
                row_term = _pick_lane(cumv, 2 * hp + e) - _pick_lane(lsev, 2 * hp + e)
                p = jnp.exp(_dot_nt(qm[e], kv) + row_term - ck[e])
                if masked:
                    p = jnp.where(_causal(tq), p, 0.0)
                dv = dv + _dot_tn(dom[e], p.astype(BF16))
                ds = p * (_dot_nt(dom[e], vv) - delta)
                dcol_new.append(dcol[e] + jnp.sum(ds, axis=0, keepdims=True))
                dcq = dcq + _put_lane(jnp.sum(ds, axis=1, keepdims=True), 2 * hp + e)
                dsb = ds.astype(BF16)
                dk = dk + _dot_tn(qm[e], dsb)
                dq = dq + _dot(dsb, km[e]) * ATTN_SCALE
            dq_scr[pl.ds(off, tq), :] += dq
            dcq_ref[pl.ds(off, tq), :] += dcq
            return dk, dv, tuple(dcol_new)

        zero_row = jnp.zeros((1, tq), F32)
        init = (jnp.zeros((LANES, tq), F32), jnp.zeros((LANES, tq), F32), (zero_row, zero_row))
        carry = tile(kj, init, True)
        dk, dv, dcol = lax.fori_loop(kj + 1, nq, lambda i, c: tile(i, c, False), carry)
        dk_ref[...] = dk.T.astype(BF16)
        dv_ref[...] = dv.T.astype(BF16)
        dck_ref[kj] += -(_put_row(dcol[0], 2 * hp) + _put_row(dcol[1], 2 * hp + 1))

        @pl.when(kj == nq - 1)
        def _():
            dq_ref[...] = dq_scr[...].astype(BF16)

    seq = lambda first: pl.BlockSpec((S, LANES), lambda b, hp, j: (b, first + hp))
    tile_in = lambda first: pl.BlockSpec((tq, LANES), lambda b, hp, j: (b * nq + j, first + hp))
    lanes0 = pl.BlockSpec((S, LANES), lambda b, hp, j: (b, 0))
    out = jax.ShapeDtypeStruct((B * S, ATTN_W), BF16)
    return _pallas(
        body, "attn_bwd", (B, HEAD_PAIRS, nq),
        [seq(0), tile_in(ATTN_W // LANES), tile_in(2 * ATTN_W // LANES), seq(0), seq(0), lanes0, lanes0,
         pl.BlockSpec((None, None, 8, tq), lambda b, hp, j: (b, j, 0, 0))],
        [seq(0), tile_in(0), tile_in(0), lanes0,
         pl.BlockSpec((None, nq, 8, tq), lambda b, hp, j: (b, 0, 0, 0))],
        [out, out, out, jax.ShapeDtypeStruct((B * S, LANES), F32), jax.ShapeDtypeStruct((B, nq, 8, tq), F32)],
        [pltpu.VMEM((S, LANES), F32)],
        (proj, proj, proj, o, do, lse, cum, cum_t), comm)


def _shift_down(u, n):
    row = lax.broadcasted_iota(jnp.int32, u.shape, 0)
    return jnp.where(row >= n, pltpu.roll(u, n, 0), 0.0)


def _shift_up(u, n):
    rows = u.shape[0]
    row = lax.broadcasted_iota(jnp.int32, u.shape, 0)
    return jnp.where(row < rows - n, pltpu.roll(u, rows - n, 0), 0.0)


def _conv_specs(S):
    cb = pl.BlockSpec((S, LANES), lambda g, b: (b, COL_CB // LANES + g))
    cc = pl.BlockSpec((S, LANES), lambda g, b: (b, COL_CC // LANES + g))
    cx = pl.BlockSpec((S, LANES), lambda g, b: (b, COL_CX // LANES + g))
    w = pl.BlockSpec((8, LANES), lambda g, b: (0, g))
    return cb, cc, cx, w


def _conv_fwd(proj, conv_w, B, S):
    def body(cb_ref, cc_ref, cx_ref, w_ref, y_ref):
        u = cc_ref[...].astype(F32) * cx_ref[...].astype(F32)
        w = w_ref[...]
        conv = w[0:1, :] * _shift_down(u, 2) + w[1:2, :] * _shift_down(u, 1) + w[2:3, :] * u
        y_ref[...] = (cb_ref[...].astype(F32) * conv).astype(BF16)

    cb, cc, cx, w = _conv_specs(S)
    return pl.pallas_call(
        body, name="conv_fwd", grid=(CONV_W // LANES, B),
        in_specs=[cb, cc, cx, w],
        out_specs=pl.BlockSpec((S, LANES), lambda g, b: (b, g)),
        out_shape=jax.ShapeDtypeStruct((B * S, CONV_W), BF16),
        compiler_params=_params(("arbitrary", "arbitrary")),
    )(proj, proj, proj, conv_w)


def _conv_bwd(dy, proj, conv_w, B, S):
    def body(dy_ref, cb_ref, cc_ref, cx_ref, w_ref, dcb_ref, dcc_ref, dcx_ref, dw_ref):
        @pl.when(pl.program_id(1) == 0)
        def _():
            dw_ref[...] = jnp.zeros_like(dw_ref)

        ccv = cc_ref[...].astype(F32)
        cxv = cx_ref[...].astype(F32)
        u = ccv * cxv
        u1 = _shift_down(u, 1)
        u2 = _shift_down(u, 2)
        w = w_ref[...]
        conv = w[0:1, :] * u2 + w[1:2, :] * u1 + w[2:3, :] * u
        dyv = dy_ref[...].astype(F32)
        dcb_ref[...] = (dyv * conv).astype(BF16)
        dconv = dyv * cb_ref[...].astype(F32)
        du = w[2:3, :] * dconv + w[1:2, :] * _shift_up(dconv, 1) + w[0:1, :] * _shift_up(dconv, 2)
        dcc_ref[...] = (du * cxv).astype(BF16)
        dcx_ref[...] = (du * ccv).astype(BF16)
        row = lax.broadcasted_iota(jnp.int32, (8, LANES), 0)
        dw = jnp.where(row == 0, jnp.sum(dconv * u2, axis=0, keepdims=True),
                       jnp.where(row == 1, jnp.sum(dconv * u1, axis=0, keepdims=True),
                                 jnp.where(row == 2, jnp.sum(dconv * u, axis=0, keepdims=True), 0.0)))
        dw_ref[...] += dw

    cb, cc, cx, w = _conv_specs(S)
    out = pl.BlockSpec((S, LANES), lambda g, b: (b, g))
    return pl.pallas_call(
        body, name="conv_bwd", grid=(CONV_W // LANES, B),
        in_specs=[out, cb, cc, cx, w],
        out_specs=[out, out, out, w],
        out_shape=[jax.ShapeDtypeStruct((B * S, CONV_W), BF16)] * 3 + [jax.ShapeDtypeStruct((8, CONV_W), F32)],
        compiler_params=_params(("arbitrary", "arbitrary")),
    )(dy, proj, proj, proj, conv_w)


def _gate_specs(tm, D):
    ga = pl.BlockSpec((tm, D), lambda i: (i, COL_GATES // D))
    gc = pl.BlockSpec((tm, D), lambda i: (i, COL_GATES // D + 1))
    return ga, gc


def _mix_out_fwd(x, o, yc, proj, woa, woc, wout, tm):
    T, D = x.shape

    def body(x_ref, o_ref, yc_ref, ga_ref, gc_ref, woa_ref, woc_ref, wout_ref, out_ref):
        ya = _dot(o_ref[...], woa_ref[...])
        yp = _dot(yc_ref[...], woc_ref[...])
        merged = _sigmoid(ga_ref[...].astype(F32)) * ya + _sigmoid(gc_ref[...].astype(F32)) * yp
        out_ref[...] = x_ref[...] + _dot(merged.astype(BF16), wout_ref[...])

    ga, gc = _gate_specs(tm, D)
    row = lambda w: pl.BlockSpec((tm, w), lambda i: (i, 0))
    whole = lambda a: pl.BlockSpec(a.shape, lambda i: (0, 0))
    return pl.pallas_call(
        body, name="mix_out_fwd", grid=(T // tm,),
        in_specs=[row(D), row(ATTN_W), row(CONV_W), ga, gc, whole(woa), whole(woc), whole(wout)],
        out_specs=row(D),
        out_shape=jax.ShapeDtypeStruct((T, D), F32),
        compiler_params=_params(("arbitrary",)),
    )(x, o, yc, proj, proj, woa, woc, wout)


def _mix_out_bwd(dx, o, yc, proj, woa, woc, wout, tm, comm=None):
    T, D = dx.shape
    nt = T // tm

    def body(dx_ref, o_ref, yc_ref, ga_ref, gc_ref, woa_ref, woc_ref, wout_ref,
             do_ref, dyc_ref, dg_ref, dwoa_ref, dwoc_ref, dwout_ref, acca, accc, acco):
        t = pl.program_id(0)

        @pl.when(t == 0)
        def _():
            acca[...] = jnp.zeros_like(acca)
            accc[...] = jnp.zeros_like(accc)
            acco[...] = jnp.zeros_like(acco)

        dxb = dx_ref[...].astype(BF16)
        ov, ycv = o_ref[...], yc_ref[...]
        ya = _dot(ov, woa_ref[...])
        yp = _dot(ycv, woc_ref[...])
        sa = _sigmoid(ga_ref[...].astype(F32))
        sc = _sigmoid(gc_ref[...].astype(F32))
        merged = (sa * ya + sc * yp).astype(BF16)
        dm = _dot_nt(dxb, wout_ref[...])
        dya = (dm * sa).astype(BF16)
        dyp = (dm * sc).astype(BF16)
        dg_ref[:, :D] = (dm * ya * sa * (1.0 - sa)).astype(BF16)
        dg_ref[:, D:] = (dm * yp * sc * (1.0 - sc)).astype(BF16)
        do_ref[...] = _dot_nt(dya, woa_ref[...]).astype(BF16)
        dyc_ref[...] = _dot_nt(dyp, woc_ref[...]).astype(BF16)
        acca[...] += _dot_tn(ov, dya)
        accc[...] += _dot_tn(ycv, dyp)
        acco[...] += _dot_tn(merged, dxb)

        @pl.when(t == nt - 1)
        def _():
            dwoa_ref[...] = acca[...].astype(BF16)
            dwoc_ref[...] = accc[...].astype(BF16)
            dwout_ref[...] = acco[...].astype(BF16)

    ga, gc = _gate_specs(tm, D)
    row = lambda w: pl.BlockSpec((tm, w), lambda i: (i, 0))
    whole = lambda a: pl.BlockSpec(a.shape, lambda i: (0, 0))
    return _pallas(
        body, "mix_out_bwd", (nt,),
        [row(D), row(ATTN_W), row(CONV_W), ga, gc, whole(woa), whole(woc), whole(wout)],
        [row(ATTN_W), row(CONV_W), row(2 * D), whole(woa), whole(woc), whole(wout)],
        [jax.ShapeDtypeStruct((T, ATTN_W), BF16), jax.ShapeDtypeStruct((T, CONV_W), BF16),
         jax.ShapeDtypeStruct((T, 2 * D), BF16),
         jax.ShapeDtypeStruct(woa.shape, BF16), jax.ShapeDtypeStruct(woc.shape, BF16),
         jax.ShapeDtypeStruct(wout.shape, BF16)],
        [pltpu.VMEM(woa.shape, F32), pltpu.VMEM(woc.shape, F32), pltpu.VMEM(wout.shape, F32)],
        (dx, o, yc, proj, proj, woa, woc, wout), comm)


def _proj_pieces(dq, dk, dv, dcb, dcc, dcx, dgates, dflog):
    D = dgates.shape[1] // 2
    return [(dq, ATTN_W, 0), (dk, ATTN_W, 0), (dv, ATTN_W, 0), (dcb, CONV_W, 0), (dcc, CONV_W, 0), (dcx, CONV_W, 0),
            (dgates, D, 0), (dgates, D, 1), (dflog, LANES, 0)]


def _mix_proj_bwd_dx(dres, x, g, pieces, wproj_t, wf_t, tm, comm=None):
    T, D = x.shape
    n = len(pieces)
    w_blocks = [(ATTN_W, 0), (ATTN_W, 1), (ATTN_W, 2), (CONV_W, 3), (CONV_W, 4), (CONV_W, 5),
                (D, COL_GATES // D), (D, COL_GATES // D + 1)]

    def body(*refs):
        dres_ref, x_ref, g_ref = refs[:3]
        p_refs, w_refs = refs[3:3 + n], refs[3 + n:3 + 2 * n]
        dx_ref, dg_ref = refs[3 + 2 * n:]

        @pl.when(pl.program_id(0) == 0)
        def _():
            dg_ref[...] = jnp.zeros_like(dg_ref)

        dh = _dot(p_refs[0][...].astype(BF16), w_refs[0][...])
        for p_ref, w_ref in zip(p_refs[1:], w_refs[1:]):
            dh = dh + _dot(p_ref[...].astype(BF16), w_ref[...])
        xhat, inv = _rms(x_ref[...])
        dx, dg = _rms_bwd(dh, xhat, inv, g_ref[...])
        dx_ref[...] = dres_ref[...] + dx
        dg_ref[...] += dg

    row = pl.BlockSpec((tm, D), lambda i: (i, 0))
    vec = pl.BlockSpec((1, D), lambda i: (0, 0))
    p_specs = [pl.BlockSpec((tm, w), lambda i, cb=cb: (i, cb)) for _, w, cb in pieces]
    w_specs = [pl.BlockSpec((r, D), lambda i, rb=rb: (rb, 0)) for r, rb in w_blocks]
    w_specs.append(pl.BlockSpec((LANES, D), lambda i: (0, 0)))
    return _pallas(
        body, "mix_proj_bwd_dx", (T // tm,),
        [row, row, vec] + p_specs + w_specs, [row, vec],
        [jax.ShapeDtypeStruct((T, D), F32), jax.ShapeDtypeStruct((1, D), F32)], [],
        (dres, x, g, *[p for p, _, _ in pieces], *([wproj_t] * len(w_blocks)), wf_t), comm)


def _matmuls_tn(name, pieces, b, tk):
    T, N = b.shape
    nt = T // tk
    n = len(pieces)

    def body(*refs):
        a_refs, b_ref, out_refs, accs = refs[:n], refs[n], refs[n + 1:2 * n + 1], refs[2 * n + 1:]
        t = pl.program_id(0)

        @pl.when(t == 0)
        def _():
            for acc in accs:
                acc[...] = jnp.zeros_like(acc)

        bv = b_ref[...]
        for a_ref, acc in zip(a_refs, accs):
            acc[...] += _dot_tn(a_ref[...].astype(BF16), bv)

        @pl.when(t == nt - 1)
        def _():
            for out_ref, acc in zip(out_refs, accs):
                out_ref[...] = acc[...].astype(BF16)

    return pl.pallas_call(
        body, name=name, grid=(nt,),
        in_specs=[pl.BlockSpec((tk, w), lambda t, cb=cb: (t, cb)) for _, w, cb in pieces]
        + [pl.BlockSpec((tk, N), lambda t: (t, 0))],
        out_specs=[pl.BlockSpec((w, N), lambda t: (0, 0)) for _, w, _ in pieces],
        out_shape=[jax.ShapeDtypeStruct((w, N), BF16) for _, w, _ in pieces],
        scratch_shapes=[pltpu.VMEM((w, N), F32) for _, w, _ in pieces],
        compiler_params=_params(("arbitrary",)),
    )(*[a for a, _, _ in pieces], b)


TOKEN_TILE = 512
TOKEN_TILE_WIDE = 1024
ATTN_TILE = 512
SCAN_CHUNK = 256
PROJ_DX_TILE = 256


def _local_step(x, target, plan, B, S):
    T, D = x.shape
    tm = min(TOKEN_TILE, T)
    tm_fwd = min(TOKEN_TILE_WIDE, T)
    tq = min(ATTN_TILE, S)
    nq = S // tq
    ch = min(SCAN_CHUNK, S)

    def riding(kernel_name, build):
        results, brought = build(plan.rider(kernel_name))
        plan.arrived(kernel_name, brought)
        return results

    hg1, hu1, n1 = plan.ffn1_up(x, tm_fwd)
    w1 = plan.weights("ffn1")
    x1, = riding("ffn1_down", lambda comm: _ffn_down("ffn1_down", x, hg1, hu1, w1["ffn1_down"], tm_fwd, comm))
    wm = plan.weights("mix_in")
    h, proj, flog = riding("mix_proj_fwd", lambda comm: _mix_proj_fwd(
        x1, wm["mix_norm"], wm["w_proj"], wm["w_f"], tm_fwd, PROJ_W // 4, comm))
    wm.update(plan.weights("mix_out"))
    cum = _fgate_fwd(flog, wm["b_forget"], B, S, ch)
    cum_t = jnp.transpose(cum[:, :N_HEADS].reshape(B, nq, tq, N_HEADS), (0, 1, 3, 2))
    o, lse = riding("attn_fwd", lambda comm: _attn_fwd(proj, cum, cum_t, B, S, tq, comm))
    yc = _conv_fwd(proj, wm["conv_w"], B, S)
    x2 = _mix_out_fwd(x1, o, yc, proj, wm["w_o_attn"], wm["w_o_conv"], wm["w_out"], tm)
    w2 = plan.weights("ffn2")
    dx3, hg2, hu2, n2, loss, d_final_norm = _ffn_fwd_loss(
        "ffn2_fwd_loss", x2, w2["ffn2_norm"], w2["ffn2_gate"], w2["ffn2_up"], w2["ffn2_down"], target, w2["final_norm"],
        tm_fwd)

    g = {"final_norm": d_final_norm}
    dx2, dhg2, dhu2, g["ffn2_norm"], df2 = _ffn_bwd_dx("ffn2_bwd_dx", dx3, x2, w2["ffn2_norm"], hg2, hu2,
                                                  w2["ffn2_gate"], w2["ffn2_up"], w2["ffn2_down"], tm_fwd)[0]
    plan.reduce("ffn2", dict(zip(("ffn2_gate", "ffn2_up", "ffn2_down"),
                                 _ffn_bwd_dw("ffn2_bwd_dw", n2, df2, hg2, hu2, dhg2, dhu2, tm)[0])))
    do, dyc, dgates, dwoa, dwoc, dwout = riding("mix_out_bwd", lambda comm: _mix_out_bwd(
        dx2, o, yc, proj, wm["w_o_attn"], wm["w_o_conv"], wm["w_out"], tm, comm))
    plan.reduce("out", dict(w_o_attn=_shard_cols(dwoa), w_o_conv=_shard_cols(dwoc), w_out=dwout.reshape(N_CHIPS, -1, D)))
    dq, dk, dv, dcq, dck = riding("attn_bwd", lambda comm: _attn_bwd(proj, o, do, lse, cum, cum_t, B, S, tq, comm))
    dcum = dcq + jnp.pad(jnp.transpose(dck, (0, 1, 3, 2)).reshape(T, N_HEADS), ((0, 0), (0, LANES - N_HEADS)))
    dflog, g["b_forget"] = _fgate_bwd(dcum, flog, wm["b_forget"], B, S, ch)
    dcb, dcc, dcx, g["conv_w"] = _conv_bwd(dyc, proj, wm["conv_w"], B, S)
    pieces = _proj_pieces(dq, dk, dv, dcb, dcc, dcx, dgates, dflog)
    dwq, dwk, dwv, dwcb, dwcc, dwcx = _matmuls_tn("mix_dw_a", pieces[:6], h, tm)
    dwga, dwgc, dwf = _matmuls_tn("mix_dw_b", pieces[6:], h, tm)
    dwin_t = jnp.concatenate([dwq, dwk, dwv, dwf[:N_HEADS], dwcb, dwcc, dwcx, dwga, dwgc], axis=0)
    plan.reduce("w_in", {"w_in": dwin_t.reshape(N_CHIPS, -1, D)})
    dx1, g["mix_norm"] = riding("mix_proj_bwd_dx", lambda comm: _mix_proj_bwd_dx(
        dx2, x1, wm["mix_norm"], pieces, wm["w_proj"], wm["w_f"], min(PROJ_DX_TILE, T), comm))
    plan.reduce("ffn1_down", {"ffn1_down": _ffn_bwd_dw_down("ffn1_bwd_dw_down", dx1, hg1, hu1, tm)[0][0]})
    grad_x, dhg1, dhu1, g["ffn1_norm"], _ = _ffn_bwd_dx(
        "ffn1_bwd_dx", dx1, x, w1["ffn1_norm"], hg1, hu1, w1["ffn1_gate"], w1["ffn1_up"], w1["ffn1_down"], tm_fwd)[0]
    plan.reduce_small(g, loss)
    plan.reduce("ffn1_in", dict(zip(("ffn1_gate", "ffn1_up"), riding("ffn1_bwd_dw_in", lambda comm: _ffn_bwd_dw_in(
        "ffn1_bwd_dw_in", n1, dhg1, dhu1, tm, comm)))))
    return loss, grad_x, g


TRANSPOSED = ("ffn1_gate", "ffn1_up", "ffn2_gate", "ffn2_up", "w_in")
NORMS = ("ffn1_norm", "mix_norm", "ffn2_norm", "final_norm")


def _unshard_cols(a):
    return jnp.transpose(a, (1, 0, 2)).reshape(a.shape[1], N_CHIPS * a.shape[2])


def _shard_cols(a):
    return jnp.transpose(a.reshape(a.shape[0], N_CHIPS, a.shape[1] // N_CHIPS), (1, 0, 2))


def _layout_ffn(which):
    def layout(st, small):
        w = {n: st[n] for n in (which + "_gate", which + "_up", which + "_down")}
        w[which + "_norm"] = small[which + "_norm"].reshape(1, -1)
        if which == "ffn2":
            w["final_norm"] = small["final_norm"].reshape(1, -1)
        return w
    return layout


def _layout_mix_in(st, small):
    win_t = st["w_in"].reshape(-1, st["w_in"].shape[2])
    return {
        "w_proj": jnp.concatenate([win_t[:N_FORGET_COL], win_t[N_FORGET_COL + N_HEADS:]], axis=0),
        "w_f": jnp.pad(win_t[N_FORGET_COL:N_FORGET_COL + N_HEADS], ((0, LANES - N_HEADS), (0, 0))),
        "conv_w": _unshard_cols(st["conv_w"]),
        "mix_norm": small["mix_norm"].reshape(1, -1),
        "b_forget": jnp.pad(small["b_forget"].reshape(1, -1), ((0, 0), (0, LANES - N_HEADS))),
    }


def _layout_mix_out(st, small):
    return {"w_o_attn": _unshard_cols(st["w_o_attn"]), "w_o_conv": _unshard_cols(st["w_o_conv"]),
            "w_out": st["w_out"].reshape(-1, st["w_out"].shape[2])}


_LAYOUTS = {"ffn1": _layout_ffn("ffn1"), "mix_in": _layout_mix_in, "mix_out": _layout_mix_out, "ffn2": _layout_ffn("ffn2")}


ANY = pl.BlockSpec(memory_space=pl.ANY)
BIG = ("ffn1_gate", "ffn1_up", "ffn1_down", "w_in", "w_o_attn", "w_o_conv", "w_out",
       "ffn2_gate", "ffn2_up", "ffn2_down")


def _place():
    x, y, c = lax.axis_index("x"), lax.axis_index("y"), lax.axis_index("c")
    others = [(1 - x, y), (x, 1 - y), (1 - x, 1 - y)]
    return x, y, c, others


def _col_halves(cols, c):
    hc = cols // 2
    return pl.ds(pl.multiple_of(c * hc, LANES), hc), pl.ds(pl.multiple_of((1 - c) * hc, LANES), hc)


def _gather_comm(shards, conv_shard=None):
    n = len(shards)
    inputs = list(shards) + ([] if conv_shard is None else [conv_shard])

    def copies(ins, outs, sems):
        send_sems, recv_sems, pass_send, pass_recv = sems[:4]
        x, y, c, others = _place()

        def chip_copy(a, j, chip):
            mine, _ = _col_halves(ins[a].shape[1], c)
            return pltpu.make_async_remote_copy(
                src_ref=ins[a].at[:, mine], dst_ref=outs[a].at[chip, :, mine],
                send_sem=send_sems.at[3 * a + j], recv_sem=recv_sems.at[3 * a + j],
                device_id=(*others[j], c), device_id_type=MESH)

        def pass_copy(a, j, chip, half):
            return pltpu.make_async_remote_copy(
                src_ref=outs[a].at[chip, :, half], dst_ref=outs[a].at[chip, :, half],
                send_sem=pass_send.at[3 * a + j], recv_sem=pass_recv.at[3 * a + j],
                device_id=(x, y, 1 - c), device_id_type=MESH)

        def conv_copy(j, chip):
            return pltpu.make_async_remote_copy(
                src_ref=ins[n], dst_ref=outs[n].at[chip],
                send_sem=sems[4].at[j], recv_sem=sems[5].at[j],
                device_id=(*others[j], c), device_id_type=MESH)

        me = 2 * x + y
        sends = [chip_copy(a, j, me) for a in range(n) for j in range(3)]
        if conv_shard is not None:
            sends += [conv_copy(j, me) for j in range(3)]
        return c, others, sends, chip_copy, pass_copy, conv_copy

    def start(ins, outs, sems):
        for cp in copies(ins, outs, sems)[2]:
            cp.start()

    def finish(ins, outs, sems):
        c, others, sends, chip_copy, pass_copy, conv_copy = copies(ins, outs, sems)
        passed = []
        for a in range(n):
            mine, _ = _col_halves(ins[a].shape[1], c)
            for j, (ox, oy) in enumerate(others):
                chip_copy(a, j, 2 * ox + oy).wait_recv()
                passed.append(pass_copy(a, j, 2 * ox + oy, mine))
                passed[-1].start()
        for a in range(n):
            _, theirs = _col_halves(ins[a].shape[1], c)
            for j, (ox, oy) in enumerate(others):
                pass_copy(a, j, 2 * ox + oy, theirs).wait_recv()
        if conv_shard is not None:
            for j, (ox, oy) in enumerate(others):
                conv_copy(j, 2 * ox + oy).wait_recv()
        for cp in sends + passed:
            cp.wait_send()

    scratch = [pltpu.SemaphoreType.DMA((3 * n,))] * 4
    if conv_shard is not None:
        scratch += [pltpu.SemaphoreType.DMA((3,))] * 2
    return _Comm(inputs, [jax.ShapeDtypeStruct((N_CHIPS,) + s.shape, s.dtype) for s in inputs], scratch, start, finish)


def _fill_own(stacks, shards):
    chip = 2 * lax.axis_index("x") + lax.axis_index("y")
    return [lax.dynamic_update_index_in_dim(st, s, chip, 0) for st, s in zip(stacks, shards)]


def _run_comm(name, comm):
    ci, co = len(comm.inputs), len(comm.out_shape)

    def body(*refs):
        comm.start(refs[:ci], refs[ci:ci + co], refs[ci + co:])
        comm.finish(refs[:ci], refs[ci:ci + co], refs[ci + co:])

    return pl.pallas_call(body, name=name, in_specs=[ANY] * ci, out_specs=[ANY] * co, out_shape=comm.out_shape,
                          scratch_shapes=comm.scratch)(*comm.inputs)


def _sibling_exchange_comm(grads):
    n = len(grads)

    def copies(ins, outs, sems):
        x, y, c, _ = _place()
        return [pltpu.make_async_remote_copy(
            src_ref=ins[a].at[:, :, _col_halves(ins[a].shape[2], c)[1]], dst_ref=outs[a],
            send_sem=sems[0].at[a], recv_sem=sems[1].at[a],
            device_id=(x, y, 1 - c), device_id_type=MESH) for a in range(n)]

    def start(ins, outs, sems):
        for cp in copies(ins, outs, sems):
            cp.start()

    def finish(ins, outs, sems):
        for cp in copies(ins, outs, sems):
            cp.wait()

    half = lambda s: jax.ShapeDtypeStruct((s.shape[0], s.shape[1], s.shape[2] // 2), s.dtype)
    return _Comm(grads, [half(s) for s in grads], [pltpu.SemaphoreType.DMA((n,))] * 2, start, finish)


def _merge_comms(comms):
    def split(refs, count):
        out, at = [], 0
        for cm in comms:
            out.append(refs[at:at + count(cm)])
            at += count(cm)
        return out

    def parts(ins, outs, sems):
        return zip(comms, split(ins, lambda cm: len(cm.inputs)), split(outs, lambda cm: len(cm.out_shape)),
                   split(sems, lambda cm: len(cm.scratch)))

    def start(ins, outs, sems):
        for cm, i, o, s in parts(ins, outs, sems):
            cm.start(i, o, s)

    def finish(ins, outs, sems):
        for cm, i, o, s in parts(ins, outs, sems):
            cm.finish(i, o, s)

    return _Comm(sum([cm.inputs for cm in comms], []), sum([cm.out_shape for cm in comms], []),
                 sum([cm.scratch for cm in comms], []), start, finish)


def _add_halves(name, grads, recvs, core):
    n = len(grads)

    def body(core_ref, *refs):
        for g_ref, r_ref, out_ref in zip(refs[:n], refs[n:2 * n], refs[2 * n:]):
            out_ref[...] = (g_ref[...].astype(F32) + r_ref[...].astype(F32)).astype(BF16)

    half = lambda g: pl.BlockSpec((None, g.shape[1], g.shape[2] // 2), lambda k, core_ref: (k, 0, 0))
    mine = lambda g: pl.BlockSpec((None, g.shape[1], g.shape[2] // 2), lambda k, core_ref: (k, 0, core_ref[0]))
    return pl.pallas_call(
        body, name=name,
        grid_spec=pltpu.PrefetchScalarGridSpec(
            num_scalar_prefetch=1, grid=(N_CHIPS,),
            in_specs=[mine(g) for g in grads] + [half(g) for g in grads],
            out_specs=[half(g) for g in grads]),
        out_shape=[jax.ShapeDtypeStruct(r.shape, BF16) for r in recvs],
        compiler_params=_params(("arbitrary",)),
    )(core, *grads, *recvs)


def _chip_exchange_comm(parts):
    n = len(parts)

    def copies(ins, outs, sems):
        x, y, c, others = _place()
        return [pltpu.make_async_remote_copy(
            src_ref=ins[a].at[2 * ox + oy], dst_ref=outs[a].at[j],
            send_sem=sems[0].at[3 * a + j], recv_sem=sems[1].at[3 * a + j],
            device_id=(ox, oy, c), device_id_type=MESH) for a in range(n) for j, (ox, oy) in enumerate(others)]

    def start(ins, outs, sems):
        for cp in copies(ins, outs, sems):
            cp.start()

    def finish(ins, outs, sems):
        for cp in copies(ins, outs, sems):
            cp.wait()

    return _Comm(parts, [jax.ShapeDtypeStruct((3,) + s.shape[1:], s.dtype) for s in parts],
                 [pltpu.SemaphoreType.DMA((3 * n,))] * 2, start, finish)


HBM = pl.BlockSpec(memory_space=pltpu.HBM)
SEM = pl.BlockSpec(memory_space=pltpu.SEMAPHORE)


def _split_exchange_copies(parts, lands, send_sems, recv_sems):
    x, y, c, others = _place()
    return [pltpu.make_async_remote_copy(
        src_ref=parts[a].at[2 * ox + oy], dst_ref=lands[a].at[j],
        send_sem=send_sems.at[3 * a + j], recv_sem=recv_sems.at[3 * a + j],
        device_id=(ox, oy, c), device_id_type=MESH) for a in range(len(parts)) for j, (ox, oy) in enumerate(others)]


def _exchange_start(name, parts):
    n = len(parts)

    def body(*refs):
        ins, lands = refs[:n], refs[n:2 * n]
        send_sems, recv_sems, token = refs[2 * n], refs[2 * n + 1], refs[-1]
        for cp in _split_exchange_copies(ins, lands, send_sems, recv_sems):
            cp.start()
        token[...] = jnp.zeros_like(token)

    land_shape = [(3,) + p.shape[1:] for p in parts]
    outs = pl.pallas_call(
        body, name=name,
        out_shape=[pltpu.SemaphoreType.DMA((3 * n,)), pltpu.SemaphoreType.DMA((3 * n,))]
        + [pltpu.HBM(p.shape, p.dtype) for p in parts] + [pltpu.HBM(s, p.dtype) for s, p in zip(land_shape, parts)]
        + [jax.ShapeDtypeStruct((8, LANES), F32)],
        in_specs=[HBM] * (2 * n), out_specs=[SEM, SEM] + [HBM] * (2 * n) + [pl.BlockSpec(memory_space=pltpu.VMEM)],
        input_output_aliases={i: 2 + i for i in range(2 * n)},
        compiler_params=pltpu.CompilerParams(has_side_effects=pltpu.SideEffectType.DATAFLOW_SIDE_EFFECTING),
    )(*[pltpu.with_memory_space_constraint(p, pltpu.HBM) for p in parts],
      *[pltpu.with_memory_space_constraint(lax.empty(s, p.dtype), pltpu.HBM) for s, p in zip(land_shape, parts)])
    return outs[0], outs[1], list(outs[2:2 + n]), list(outs[2 + n:2 + 2 * n]), outs[-1]


def _exchange_wait(name, send_sems, recv_sems, parts, lands, after):
    n = len(parts)

    def body(*refs):
        ins, zones = refs[:n], refs[n:2 * n]
        for cp in _split_exchange_copies(ins, zones, refs[2 * n], refs[2 * n + 1]):
            cp.wait_send()
            cp.wait_recv()

    outs = pl.pallas_call(
        body, name=name,
        out_shape=[pltpu.HBM(p.shape, p.dtype) for p in parts] + [pltpu.HBM(z.shape, z.dtype) for z in lands],
        in_specs=[HBM] * (2 * n) + [SEM, SEM] + [ANY] * len(after), out_specs=[HBM] * (2 * n),
        input_output_aliases={i: i for i in range(2 * n)},
        compiler_params=pltpu.CompilerParams(has_side_effects=pltpu.SideEffectType.DATAFLOW_SIDE_EFFECTING),
    )(*parts, *lands, send_sems, recv_sems, *after)
    return list(outs[:n]), list(outs[n:])


def _sum_chips(name, owns, recvs, chip, after):
    n = len(owns)
    hc = owns[0].shape[2]
    assert all(o.shape[2] == hc for o in owns)

    def body(chip_ref, *refs):
        for own_ref, recv_ref, out_ref in zip(refs[:n], refs[n:2 * n], refs[2 * n + 1:]):
            acc = own_ref[...].astype(F32)
            for j in range(3):
                acc = acc + recv_ref[j].astype(F32)
            out_ref[...] = acc

    return pl.pallas_call(
        body, name=name,
        grid_spec=pltpu.PrefetchScalarGridSpec(
            num_scalar_prefetch=1, grid=(hc // LANES,),
            in_specs=[pl.BlockSpec((None, o.shape[1], LANES), lambda i, chip_ref: (chip_ref[0], 0, i)) for o in owns]
            + [pl.BlockSpec((3, o.shape[1], LANES), lambda i, chip_ref: (0, 0, i)) for o in owns]
            + [pl.BlockSpec((8, LANES), lambda i, chip_ref: (0, 0))],
            out_specs=[pl.BlockSpec((o.shape[1], LANES), lambda i, chip_ref: (0, i)) for o in owns]),
        out_shape=[jax.ShapeDtypeStruct((o.shape[1], hc), F32) for o in owns],
        compiler_params=_params(("arbitrary",)),
    )(chip, *owns, *recvs, after)


def _share_halves(name, halves):
    n = len(halves)

    def body(*refs):
        srcs, dsts = refs[:n], refs[n:2 * n]
        send_sems, recv_sems = refs[2 * n:]
        x, y, c, _ = _place()
        copies = [pltpu.make_async_remote_copy(
            src_ref=srcs[a], dst_ref=dsts[a], send_sem=send_sems.at[a], recv_sem=recv_sems.at[a],
            device_id=(x, y, 1 - c), device_id_type=MESH) for a in range(n)]
        for cp in copies:
            cp.start()
        for cp in copies:
            cp.wait()

    return pl.pallas_call(
        body, name=name,
        in_specs=[ANY] * n, out_specs=[ANY] * n,
        out_shape=[jax.ShapeDtypeStruct(s.shape, s.dtype) for s in halves],
        scratch_shapes=[pltpu.SemaphoreType.DMA((n,)), pltpu.SemaphoreType.DMA((n,))],
    )(*halves)


def _small_gather_comm(part):
    def copies(ins, outs, sems):
        x, y, c, _ = _place()
        me = 4 * x + 2 * y + c
        both = []
        for d in range(1, N_DEV):
            px, py, pc = (1 - x if d & 4 else x, 1 - y if d & 2 else y, 1 - c if d & 1 else c)
            send = pltpu.make_async_remote_copy(
                src_ref=ins[0], dst_ref=outs[0].at[me], send_sem=sems[0].at[d - 1], recv_sem=sems[1].at[d - 1],
                device_id=(px, py, pc), device_id_type=MESH)
            recv = pltpu.make_async_remote_copy(
                src_ref=ins[0], dst_ref=outs[0].at[4 * px + 2 * py + pc], send_sem=sems[0].at[d - 1],
                recv_sem=sems[1].at[d - 1], device_id=(px, py, pc), device_id_type=MESH)
            both.append((send, recv))
        return both

    def start(ins, outs, sems):
        for send, _ in copies(ins, outs, sems):
            send.start()

    def finish(ins, outs, sems):
        for send, recv in copies(ins, outs, sems):
            recv.wait_recv()
            send.wait_send()

    return _Comm([part], [jax.ShapeDtypeStruct((N_DEV,) + part.shape, F32)],
                 [pltpu.SemaphoreType.DMA((N_DEV - 1,))] * 2, start, finish)


def _sum_devices(parts):
    def body(p_ref, out_ref):
        acc = p_ref[0]
        for k in range(1, N_DEV):
            acc = acc + p_ref[k]
        out_ref[...] = acc

    return pl.pallas_call(
        body, name="sum_devices", grid=(1,),
        in_specs=[pl.BlockSpec(parts.shape, lambda i: (0, 0, 0))],
        out_specs=pl.BlockSpec(parts.shape[1:], lambda i: (0, 0)),
        out_shape=jax.ShapeDtypeStruct(parts.shape[1:], F32),
        compiler_params=_params(("arbitrary",)),
    )(parts)


def _adam_update(w, g, m, v):
    nm = ADAM_B1 * m + (1.0 - ADAM_B1) * g
    nv = ADAM_B2 * v + (1.0 - ADAM_B2) * (g * g)
    m_hat = nm * (1.0 / (1.0 - ADAM_B1 ** ADAM_STEP))
    v_hat = nv * (1.0 / (1.0 - ADAM_B2 ** ADAM_STEP))
    return -ADAM_LR * (m_hat / (jnp.sqrt(v_hat) + ADAM_EPS) + ADAM_WD * w), nm, nv


def _adamw(name, w, g, m, v):
    def body(w_ref, g_ref, m_ref, v_ref, d_ref, nm_ref, nv_ref):
        d_ref[...], nm_ref[...], nv_ref[...] = _adam_update(w_ref[...], g_ref[...], m_ref[...], v_ref[...])

    spec = pl.BlockSpec(w.shape, lambda i: (0, 0))
    out = jax.ShapeDtypeStruct(w.shape, F32)
    return pl.pallas_call(
        body, name=name, grid=(1,),
        in_specs=[spec] * 4, out_specs=[spec] * 3, out_shape=[out] * 3,
        compiler_params=_params(("arbitrary",)),
    )(w, g, m, v)


def _adamw_halves(name, ws, mines, theirs, ms, vs, core):
    n = len(ws)
    cols = ws[0].shape[1]
    assert all(w.shape[1] == cols for w in ws)
    hc = cols // 2
    tc = LANES if n > 1 else min(256, hc)
    nt = hc // tc

    def body(core_ref, *refs):
        ins, outs = refs[:5 * n], refs[5 * n:]
        for a in range(n):
            w_ref, mine_ref, theirs_ref, m_ref, v_ref = [ins[j * n + a] for j in range(5)]
            g_ref, d_ref, nm_ref, nv_ref = outs[4 * a:4 * a + 4]
            gv = jnp.where(pl.program_id(0) == core_ref[0], mine_ref[...], theirs_ref[...])
            g_ref[...] = gv
            d_ref[...], nm_ref[...], nv_ref[...] = _adam_update(w_ref[...], gv, m_ref[...], v_ref[...])

    whole = lambda w: pl.BlockSpec((w.shape[0], tc), lambda h, i, core_ref: (0, h * nt + i))
    mine_spec = lambda w: pl.BlockSpec((w.shape[0], tc), lambda h, i, core_ref: (0, jnp.where(h == core_ref[0], i, 0)))
    theirs_spec = lambda w: pl.BlockSpec((w.shape[0], tc), lambda h, i, core_ref: (0, jnp.where(h == core_ref[0], 0, i)))
    outs = pl.pallas_call(
        body, name=name,
        grid_spec=pltpu.PrefetchScalarGridSpec(
            num_scalar_prefetch=1, grid=(2, nt),
            in_specs=[whole(w) for w in ws] + [mine_spec(w) for w in ws] + [theirs_spec(w) for w in ws]
            + [whole(w) for w in ws] * 2,
            out_specs=[whole(w) for w in ws for _ in range(4)]),
        out_shape=[jax.ShapeDtypeStruct(w.shape, F32) for w in ws for _ in range(4)],
        compiler_params=_params(("arbitrary", "arbitrary")),
    )(core, *ws, *mines, *theirs, *ms, *vs)
    return [outs[4 * a:4 * a + 4] for a in range(n)]


WEIGHTS = ("ffn1_norm", "ffn1_gate", "ffn1_up", "ffn1_down", "mix_norm", "w_in", "b_forget", "conv_w",
           "w_o_attn", "w_o_conv", "w_out", "ffn2_norm", "ffn2_gate", "ffn2_up", "ffn2_down", "final_norm")
VEC_ROWS = 8


def _pack_small(t, conv_rows):
    conv = t["conv_w"]
    parts = [t[n].reshape(VEC_ROWS, LANES) for n in NORMS]
    parts.append(jnp.pad(conv, ((0, conv_rows - conv.shape[0]), (0, 0))))
    parts.append(jnp.pad(t["b_forget"].reshape(1, N_HEADS), ((0, 7), (0, LANES - N_HEADS))))
    return jnp.concatenate(parts, axis=0)


def _unpack_small(p, conv_rows):
    out = {n: p[VEC_ROWS * i:VEC_ROWS * (i + 1)].reshape(-1) for i, n in enumerate(NORMS)}
    base = VEC_ROWS * len(NORMS)
    out["conv_w"] = p[base:base + 3]
    out["b_forget"] = p[base + conv_rows, :N_HEADS]
    return out


def _travel(name, a):
    return a.T if name in TRANSPOSED else a


GATHER_FIRST = ("ffn1_gate", "ffn1_up")
GATHER_RIDES = {"ffn1_up": ("ffn1_down",), "ffn1_down": ("w_in",), "mix_proj_fwd": ("w_o_attn", "w_o_conv", "w_out"),
                "attn_fwd": ("ffn2_gate", "ffn2_up", "ffn2_down")}
SIBLING_RIDES = {"ffn2": "mix_out_bwd", "out": None, "w_in": "mix_proj_bwd_dx", "ffn1_down": None, "ffn1_in": None}
CHIP_RIDES = {"ffn2": "attn_bwd", "out": "attn_bwd", "w_in": "ffn1_bwd_dw_in", "ffn1_down": None, "ffn1_in": None}
SMALL_RIDE = "ffn1_bwd_dw_in"


class _MeshPlan:
    def __init__(self, wts, core):
        self.small, self.core = wts, core
        self.shards = {n: wts[n].astype(BF16) for n in BIG}
        self.chip_part, self.from_chips, self.rides, self.started = {}, {}, {}, []
        self.stacks = {}
        conv_shard = jnp.pad(wts["conv_w"], ((0, 8 - wts["conv_w"].shape[0]), (0, 0)))
        for kernel_name, names in GATHER_RIDES.items():
            mine = [self.shards[n] for n in names]
            conv = conv_shard if kernel_name == "ffn1_up" else None
            names = names + (("conv_w",) if conv is not None else ())
            mine = mine + ([conv] if conv is not None else [])
            self._ride(kernel_name, _gather_comm(mine[:len(mine) - (conv is not None)], conv),
                       lambda got, names=names, mine=mine: self.stacks.update(zip(names, _fill_own(got, mine))))

    def weights(self, group):
        return _LAYOUTS[group](self.stacks, self.small)

    def ffn1_up(self, x, tm):
        px, py = lax.axis_index("x"), lax.axis_index("y")
        order = jnp.stack([2 * px + py, 2 * (1 - px) + py, 2 * px + (1 - py), 2 * (1 - px) + (1 - py)]).astype(jnp.int32)
        own = [self.shards[n] for n in GATHER_FIRST]
        (hg, hu, n, sg, su), brought = _ffn_up_gather("ffn1_up", x, self.small["ffn1_norm"].reshape(1, -1), *own, order,
                                                     tm, self.rider("ffn1_up"))
        self.stacks.update(zip(GATHER_FIRST, _fill_own([sg, su], own)))
        self.arrived("ffn1_up", brought)
        return hg, hu, n

    def _ride(self, kernel_name, comm, then):
        self.rides.setdefault(kernel_name, []).append((comm, then))

    def rider(self, kernel_name):
        comms = [comm for comm, _ in self.rides.get(kernel_name, [])]
        return _merge_comms(comms) if comms else None

    def arrived(self, kernel_name, results):
        for comm, then in self.rides.pop(kernel_name, []):
            then(results[:len(comm.out_shape)])
            results = results[len(comm.out_shape):]

    def reduce(self, group, grads):
        names = tuple(grads)
        mine = [grads[n] for n in names]

        def with_sibling(from_sibling):
            parts = _add_halves("add_halves_" + group, mine, list(from_sibling), self.core)
            self.chip_part.update(zip(names, parts))
            if CHIP_RIDES[group] is None:
                self.started.append((names, _exchange_start("exchange_start_" + group, parts)))
            else:
                self._ride(CHIP_RIDES[group], _chip_exchange_comm(parts),
                           lambda got: self.from_chips.update(zip(names, got)))

        if SIBLING_RIDES[group] is None:
            with_sibling(_run_comm("sibling_exchange_" + group, _sibling_exchange_comm(mine)))
        else:
            self._ride(SIBLING_RIDES[group], _sibling_exchange_comm(mine), with_sibling)

    def reduce_small(self, gs, loss):
        conv_all = _shard_cols(gs["conv_w"]).reshape(N_CHIPS * 8, LANES)
        part = _pack_small({**{n: gs[n] for n in NORMS}, "conv_w": conv_all, "b_forget": gs["b_forget"][0, :N_HEADS]},
                           N_CHIPS * 8)
        part = jnp.concatenate([part, jnp.broadcast_to(loss, (8, LANES))], axis=0)
        me = 4 * lax.axis_index("x") + 2 * lax.axis_index("y") + lax.axis_index("c")

        def landed(got):
            self.small_parts = lax.dynamic_update_index_in_dim(got[0], part, me, 0)

        self._ride(SMALL_RIDE, _small_gather_comm(part), landed)


def kernel(x, ffn1_norm, ffn1_gate, ffn1_up, ffn1_down, mix_norm, w_in, b_forget, conv_w, w_o_attn, w_o_conv, w_out, ffn2_norm, ffn2_gate, ffn2_up, ffn2_down, final_norm, loss_target, m_ffn1_norm, m_ffn1_gate, m_ffn1_up, m_ffn1_down, m_mix_norm, m_w_in, m_b_forget, m_conv_w, m_w_o_attn, m_w_o_conv, m_w_out, m_ffn2_norm, m_ffn2_gate, m_ffn2_up, m_ffn2_down, m_final_norm, v_ffn1_norm, v_ffn1_gate, v_ffn1_up, v_ffn1_down, v_mix_norm, v_w_in, v_b_forget, v_conv_w, v_w_o_attn, v_w_o_conv, v_w_out, v_ffn2_norm, v_ffn2_gate, v_ffn2_up, v_ffn2_down, v_final_norm):
    given = dict(locals())
    wts = {n: _travel(n, given[n]) for n in WEIGHTS}
    mom = {n: _travel(n, given["m_" + n]) for n in WEIGHTS}
    var = {n: _travel(n, given["v_" + n]) for n in WEIGHTS}
    B, S, D = x.shape
    chip = 2 * lax.axis_index("x") + lax.axis_index("y")
    chip1 = chip.astype(jnp.int32).reshape(1)
    core = lax.axis_index("c").astype(jnp.int32).reshape(1)

    plan = _MeshPlan(wts, core)
    loss, grad_x, gs = _local_step(x.reshape(B * S, D), loss_target.reshape(B * S, D), plan, B, S)

    (first_names, first), (last_names, (send_sems, recv_sems, parts_thru, lands, token)) = plan.started
    delta, new_m, new_v, grads = {}, {}, {}, {}

    def finish(tag, names):
        by_cols = {}
        for n in names:
            by_cols.setdefault(wts[n].shape[1], []).append(n)
        mine = {}
        for cols, ns in by_cols.items():
            mine.update(zip(ns, _sum_chips("sum_chips_%s_%d" % (tag, cols), [plan.chip_part[n] for n in ns],
                                           [plan.from_chips[n] for n in ns], chip1, token)))
        theirs = dict(zip(names, _share_halves("share_halves_" + tag, [mine[n] for n in names])))
        raw = []
        for cols, ns in by_cols.items():
            outs = _adamw_halves("adamw_%s_%d" % (tag, cols), [wts[n] for n in ns], [mine[n] for n in ns],
                                 [theirs[n] for n in ns], [mom[n] for n in ns], [var[n] for n in ns], core)
            for n, per in zip(ns, outs):
                raw.append(per[-1])
                grads[n], delta[n], new_m[n], new_v[n] = [_travel(n, o) for o in per]
        return raw

    small_sum = _sum_devices(plan.small_parts)
    base = VEC_ROWS * len(NORMS)
    loss_row = small_sum.shape[0] - 8
    small_grads = _unpack_small(small_sum, N_CHIPS * 8)
    small_grads["conv_w"] = lax.dynamic_slice_in_dim(small_sum[base:base + N_CHIPS * 8], chip * 8, 8, axis=0)[:3]
    packs = [_pack_small(t, 8) for t in (wts, small_grads, mom, var)]
    small_out = _adamw("adamw_small", *packs)

    parts_back, got = _exchange_wait("exchange_wait_first", *first[:4], [token])
    plan.chip_part.update(zip(first_names, parts_back))
    plan.from_chips.update(zip(first_names, got))
    done = finish("early", [n for n in BIG if n not in last_names])
    parts_back, got = _exchange_wait("exchange_wait_last", send_sems, recv_sems, parts_thru, lands, done + list(small_out))
    plan.chip_part.update(zip(last_names, parts_back))
    plan.from_chips.update(zip(last_names, got))
    finish("last", last_names)
    grads.update(small_grads)
    for out, p in zip((delta, new_m, new_v), small_out):
        out.update(_unpack_small(p, 8))

    return (small_sum[loss_row, 0], grad_x.reshape(B, S, D), *[grads[n] for n in WEIGHTS], *[delta[n] for n in WEIGHTS],
            *[new_m[n] for n in WEIGHTS], *[new_v[n] for n in WEIGHTS])
```

```python
import functools
import math

import jax
import jax.numpy as jnp
from jax import lax
from jax.experimental import pallas as pl
from jax.experimental.pallas import tpu as pltpu

F32 = jnp.float32
BF16 = jnp.bfloat16
MESH = pl.DeviceIdType.MESH

N_CHIPS = 4
N_DEV = 8
N_HEADS = 8
HEAD_DIM = 64
HEAD_PAIRS = N_HEADS // 2
ATTN_W = N_HEADS * HEAD_DIM
CONV_W = 512
RMS_EPS = 1e-6
FFN_RES = 0.5
LANES = 128
VMEM_LIMIT = 56 * 1024 * 1024
ROW_BLOCK = 256

ADAM_LR = 0.001
ADAM_B1 = 0.9
ADAM_B2 = 0.999
ADAM_EPS = 1e-08
ADAM_WD = 0.01
ADAM_STEP = 10

PROJ_W = 3 * ATTN_W + 3 * CONV_W + 2 * 1024
COL_CB, COL_CC, COL_CX = 3 * ATTN_W, 3 * ATTN_W + CONV_W, 3 * ATTN_W + 2 * CONV_W
COL_GATES = 3 * ATTN_W + 3 * CONV_W
N_FORGET_COL = 3 * ATTN_W


def _params(sem=None, vmem=VMEM_LIMIT):
    return pltpu.CompilerParams(dimension_semantics=sem, vmem_limit_bytes=vmem)


def _dot(a, b):
    return lax.dot_general(a, b, (((1,), (0,)), ((), ())), preferred_element_type=F32)


def _dot_nt(a, b):
    return lax.dot_general(a, b, (((1,), (1,)), ((), ())), preferred_element_type=F32)


def _dot_tn(a, b):
    return lax.dot_general(a, b, (((0,), (0,)), ((), ())), preferred_element_type=F32)


def _sigmoid(x):
    return 1.0 / (1.0 + jnp.exp(-x))


def _rms(xv):
    inv = lax.rsqrt(jnp.mean(xv * xv, axis=-1, keepdims=True) + RMS_EPS)
    return xv * inv, inv


class _Comm:
    def __init__(self, inputs, out_shape, scratch, start, finish):
        self.inputs, self.out_shape, self.scratch = list(inputs), list(out_shape), list(scratch)
        self.start, self.finish = start, finish


def _pallas(body, name, grid, in_specs, out_specs, out_shape, scratch, args, comm=None):
    sem = ("arbitrary",) * len(grid)
    if comm is None:
        outs = pl.pallas_call(body, name=name, grid=grid, in_specs=in_specs, out_specs=out_specs,
                              out_shape=out_shape, scratch_shapes=scratch, compiler_params=_params(sem))(*args)
        return list(outs), []
    n_in, n_out, n_scr = len(in_specs), len(out_specs), len(scratch)
    ci, co = len(comm.inputs), len(comm.out_shape)

    def riding(*refs):
        ins, refs = refs[:n_in], refs[n_in:]
        cins, refs = refs[:ci], refs[ci:]
        outs, refs = refs[:n_out], refs[n_out:]
        couts, refs = refs[:co], refs[co:]
        scr, sems = refs[:n_scr], refs[n_scr:]
        ids = [pl.program_id(d) for d in range(len(grid))]
        first = functools.reduce(lambda a, b: a & b, [i == 0 for i in ids])
        last = functools.reduce(lambda a, b: a & b, [i == g - 1 for i, g in zip(ids, grid)])

        @pl.when(first)
        def _():
            comm.start(cins, couts, sems)

        body(*ins, *outs, *scr)

        @pl.when(last)
        def _():
            comm.finish(cins, couts, sems)

    any_spec = pl.BlockSpec(memory_space=pl.ANY)
    outs = pl.pallas_call(
        riding, name=name, grid=grid,
        in_specs=list(in_specs) + [any_spec] * ci, out_specs=list(out_specs) + [any_spec] * co,
        out_shape=list(out_shape) + comm.out_shape, scratch_shapes=list(scratch) + comm.scratch,
        compiler_params=_params(sem))(*args, *comm.inputs)
    return list(outs[:n_out]), list(outs[n_out:])


def _rms_bwd(dn, xhat, inv, g):
    dxhat = dn * g
    dx = inv * (dxhat - xhat * jnp.mean(dxhat * xhat, axis=-1, keepdims=True))
    return dx, jnp.sum(dn * xhat, axis=0, keepdims=True)


def _ffn_fwd_loss(name, x, g, wgt, wut, wd, target, gf, tm):
    T, D = x.shape
    K, Fs, _ = wgt.shape

    def body(x_ref, g_ref, wg_ref, wu_ref, wd_ref, t_ref, gf_ref,
             dx_ref, hg_ref, hu_ref, n_ref, loss_ref, dgf_ref, acc_scr):
        i, k = pl.program_id(0), pl.program_id(1)

        @pl.when(k == 0)
        def _():
            xhat, _ = _rms(x_ref[...])
            n_ref[...] = (xhat * g_ref[...]).astype(BF16)
            acc_scr[...] = jnp.zeros_like(acc_scr)

        @pl.when((k == 0) & (i == 0))
        def _():
            loss_ref[...] = jnp.zeros_like(loss_ref)
            dgf_ref[...] = jnp.zeros_like(dgf_ref)

        n = n_ref[...]
        hg = _dot_nt(n, wg_ref[...])
        hu = _dot_nt(n, wu_ref[...])
        hg_ref[...] = hg.astype(BF16)
        hu_ref[...] = hu.astype(BF16)
        act = (hg * _sigmoid(hg) * hu).astype(BF16)
        acc_scr[...] += _dot(act, wd_ref[...])

        @pl.when(k == K - 1)
        def _():
            gfv = gf_ref[...]
            for r0 in range(0, tm, ROW_BLOCK):
                rows = slice(r0, r0 + ROW_BLOCK)
                xhat, inv = _rms(x_ref[rows, :] + FFN_RES * acc_scr[rows, :])
                err = xhat * gfv - t_ref[rows, :]
                loss_ref[...] += 0.5 * jnp.sum(jnp.sum(err * err, axis=1, keepdims=True), axis=0, keepdims=True) / D
                dx, dg = _rms_bwd(err * (1.0 / D), xhat, inv, gfv)
                dx_ref[rows, :] = dx
                dgf_ref[...] += dg

    w_spec = pl.BlockSpec((None, Fs, D), lambda i, k: (k, 0, 0))
    act_spec = pl.BlockSpec((None, tm, Fs), lambda i, k: (k, i, 0))
    row = pl.BlockSpec((tm, D), lambda i, k: (i, 0))
    vec = pl.BlockSpec((1, D), lambda i, k: (0, 0))
    return _pallas(
        body, name, (T // tm, K),
        [row, vec, w_spec, w_spec, w_spec, row, vec],
        [row, act_spec, act_spec, row, pl.BlockSpec((1, LANES), lambda i, k: (0, 0)), vec],
        [jax.ShapeDtypeStruct((T, D), F32), jax.ShapeDtypeStruct((K, T, Fs), BF16),
         jax.ShapeDtypeStruct((K, T, Fs), BF16), jax.ShapeDtypeStruct((T, D), BF16),
         jax.ShapeDtypeStruct((1, LANES), F32), jax.ShapeDtypeStruct((1, D), F32)],
        [pltpu.VMEM((tm, D), F32)],
        (x, g, wgt, wut, wd, target, gf))[0]


def _ffn_up_gather(name, x, g, wg_own, wu_own, order, tm, comm=None):
    T, D = x.shape
    Fs = wg_own.shape[0]
    nt = T // tm
    ci, co = (len(comm.inputs), len(comm.out_shape)) if comm is not None else (0, 0)

    def body(order_ref, x_ref, g_ref, wgo_ref, wuo_ref, *rest):
        cins, rest = rest[:ci], rest[ci:]
        (hg_ref, hu_ref, n_ref, sg_ref, su_ref), rest = rest[:5], rest[5:]
        couts, rest = rest[:co], rest[co:]
        (n_all, wbuf, send_sems, recv_sems, pass_send, pass_recv, load_sems), csems = rest[:7], rest[7:]
        k, i = pl.program_id(0), pl.program_id(1)
        x_pos, y_pos, c, others = _place()
        me = 2 * x_pos + y_pos
        owns, stacks = (wgo_ref, wuo_ref), (sg_ref, su_ref)
        mine, theirs = _col_halves(D, c)

        def chip_copy(a, j, chip):
            return pltpu.make_async_remote_copy(
                src_ref=owns[a].at[:, mine], dst_ref=stacks[a].at[chip, :, mine],
                send_sem=send_sems.at[3 * a + j], recv_sem=recv_sems.at[3 * a + j],
                device_id=(*others[j], c), device_id_type=MESH)

        def pass_copy(a, j, chip, half):
            return pltpu.make_async_remote_copy(
                src_ref=stacks[a].at[chip, :, half], dst_ref=stacks[a].at[chip, :, half],
                send_sem=pass_send.at[3 * a + j], recv_sem=pass_recv.at[3 * a + j],
                device_id=(x_pos, y_pos, 1 - c), device_id_type=MESH)

        @pl.when((k == 0) & (i == 0))
        def _():
            for a in range(2):
                for j in range(3):
                    chip_copy(a, j, me).start()
            if comm is not None:
                comm.start(cins, couts, csems)

        def bring(j):
            ox, oy = others[j]
            chip = 2 * ox + oy
            for a in range(2):
                chip_copy(a, j, chip).wait_recv()
            for a in range(2):
                pass_copy(a, j, chip, mine).start()
            for a in range(2):
                pass_copy(a, j, chip, theirs).wait_recv()
            loads = [pltpu.make_async_copy(stacks[a].at[chip], wbuf.at[j % 2, a], load_sems.at[2 * (j % 2) + a])
                     for a in range(2)]
            for cp in loads:
                cp.start()
            for cp in loads:
                cp.wait()

        @pl.when((k == 1) & (i == 0))
        def _():
            bring(0)
            bring(1)

        @pl.when((k == 2) & (i == nt - 1))
        def _():
            bring(2)

        rows = pl.ds(pl.multiple_of(i * tm, tm), tm)

        @pl.when(k == 0)
        def _():
            xhat, _ = _rms(x_ref[...])
            n = (xhat * g_ref[...]).astype(BF16)
            n_ref[...] = n
            n_all[rows, :] = n
            hg_ref[...] = _dot_nt(n, wgo_ref[...]).astype(BF16)
            hu_ref[...] = _dot_nt(n, wuo_ref[...]).astype(BF16)

        @pl.when(k > 0)
        def _():
            n = n_all[rows, :]
            slot = (k - 1) % 2
            hg_ref[...] = _dot_nt(n, wbuf[slot, 0]).astype(BF16)
            hu_ref[...] = _dot_nt(n, wbuf[slot, 1]).astype(BF16)

        @pl.when((k == N_CHIPS - 1) & (i == nt - 1))
        def _():
            for a in range(2):
                for j, (ox, oy) in enumerate(others):
                    chip_copy(a, j, me).wait_send()
                    pass_copy(a, j, 2 * ox + oy, mine).wait_send()
            if comm is not None:
                comm.finish(cins, couts, csems)

    any_spec = pl.BlockSpec(memory_space=pl.ANY)
    first_pass = lambda k, i, order_ref: (jnp.where(k == 0, i, nt - 1), 0)
    whole = pl.BlockSpec((Fs, D), lambda k, i, order_ref: (0, 0))
    act_spec = pl.BlockSpec((None, tm, Fs), lambda k, i, order_ref: (order_ref[k], i, 0))
    stack = jax.ShapeDtypeStruct((N_CHIPS, Fs, D), BF16)
    outs = pl.pallas_call(
        body, name=name,
        grid_spec=pltpu.PrefetchScalarGridSpec(
            num_scalar_prefetch=1, grid=(N_CHIPS, nt),
            in_specs=[pl.BlockSpec((tm, D), first_pass), pl.BlockSpec((1, D), lambda k, i, order_ref: (0, 0)),
                      whole, whole] + [any_spec] * ci,
            out_specs=[act_spec, act_spec, pl.BlockSpec((tm, D), first_pass), any_spec, any_spec] + [any_spec] * co,
            scratch_shapes=[pltpu.VMEM((T, D), BF16), pltpu.VMEM((2, 2, Fs, D), BF16)]
            + [pltpu.SemaphoreType.DMA((6,))] * 4 + [pltpu.SemaphoreType.DMA((4,))]
            + (comm.scratch if comm is not None else [])),
        out_shape=[jax.ShapeDtypeStruct((N_CHIPS, T, Fs), BF16), jax.ShapeDtypeStruct((N_CHIPS, T, Fs), BF16),
                   jax.ShapeDtypeStruct((T, D), BF16), stack, stack] + (comm.out_shape if comm is not None else []),
        compiler_params=_params(("arbitrary", "arbitrary")),
    )(order, x, g, wg_own, wu_own, *(comm.inputs if comm is not None else []))
    return list(outs[:5]), list(outs[5:])


def _ffn_down(name, x, hg, hu, wd, tm, comm=None):
    T, D = x.shape
    K, Fs, _ = wd.shape

    def body(x_ref, hg_ref, hu_ref, wd_ref, out_ref, acc_scr):
        k = pl.program_id(1)

        @pl.when(k == 0)
        def _():
            acc_scr[...] = jnp.zeros_like(acc_scr)

        hgv = hg_ref[...].astype(F32)
        act = (hgv * _sigmoid(hgv) * hu_ref[...].astype(F32)).astype(BF16)
        acc_scr[...] += _dot(act, wd_ref[...])

        @pl.when(k == K - 1)
        def _():
            out_ref[...] = x_ref[...] + FFN_RES * acc_scr[...]

    act_spec = pl.BlockSpec((None, tm, Fs), lambda i, k: (k, i, 0))
    row = pl.BlockSpec((tm, D), lambda i, k: (i, 0))
    return _pallas(
        body, name, (T // tm, K),
        [row, act_spec, act_spec, pl.BlockSpec((None, Fs, D), lambda i, k: (k, 0, 0))],
        [row], [jax.ShapeDtypeStruct((T, D), F32)], [pltpu.VMEM((tm, D), F32)],
        (x, hg, hu, wd), comm)


def _ffn_bwd_dx(name, dout, x, g, hg, hu, wgt, wut, wd, tm, comm=None):
    T, D = x.shape
    K, Fs, _ = wgt.shape

    def body(dout_ref, x_ref, g_ref, hg_ref, hu_ref, wg_ref, wu_ref, wd_ref,
             dx_ref, dhg_ref, dhu_ref, dg_ref, df_ref, dn_scr):
        i, k = pl.program_id(0), pl.program_id(1)

        @pl.when(k == 0)
        def _():
            df_ref[...] = (FFN_RES * dout_ref[...]).astype(BF16)
            dn_scr[...] = jnp.zeros_like(dn_scr)

        @pl.when((k == 0) & (i == 0))
        def _():
            dg_ref[...] = jnp.zeros_like(dg_ref)

        for r0 in range(0, tm, ROW_BLOCK):
            rows = slice(r0, r0 + ROW_BLOCK)
            dact = _dot_nt(df_ref[rows, :], wd_ref[...])
            hgv = hg_ref[rows, :].astype(F32)
            huv = hu_ref[rows, :].astype(F32)
            s = _sigmoid(hgv)
            dhu = (dact * (hgv * s)).astype(BF16)
            dhg = (dact * huv * (s * (1.0 + hgv * (1.0 - s)))).astype(BF16)
            dhg_ref[rows, :] = dhg
            dhu_ref[rows, :] = dhu
            dn_scr[rows, :] += _dot(dhg, wg_ref[...]) + _dot(dhu, wu_ref[...])

        @pl.when(k == K - 1)
        def _():
            xhat, inv = _rms(x_ref[...])
            dx, dg = _rms_bwd(dn_scr[...], xhat, inv, g_ref[...])
            dx_ref[...] = dout_ref[...] + dx
            dg_ref[...] += dg

    w_spec = pl.BlockSpec((None, Fs, D), lambda i, k: (k, 0, 0))
    act_spec = pl.BlockSpec((None, tm, Fs), lambda i, k: (k, i, 0))
    row = pl.BlockSpec((tm, D), lambda i, k: (i, 0))
    row_once = pl.BlockSpec((tm, D), lambda i, k: (i, 0), pipeline_mode=pl.Buffered(1))
    vec = pl.BlockSpec((1, D), lambda i, k: (0, 0))
    return _pallas(
        body, name, (T // tm, K),
        [row, row_once, vec, act_spec, act_spec, w_spec, w_spec, w_spec],
        [row_once, act_spec, act_spec, vec, row],
        [jax.ShapeDtypeStruct((T, D), F32), jax.ShapeDtypeStruct((K, T, Fs), BF16),
         jax.ShapeDtypeStruct((K, T, Fs), BF16), jax.ShapeDtypeStruct((1, D), F32),
         jax.ShapeDtypeStruct((T, D), BF16)],
        [pltpu.VMEM((tm, D), F32)],
        (dout, x, g, hg, hu, wgt, wut, wd), comm)


def _ffn_bwd_dw(name, n, df, hg, hu, dhg, dhu, tk, comm=None):
    T, D = n.shape
    K, _, Fs = hg.shape
    nt = T // tk

    def body(n_ref, df_ref, hg_ref, hu_ref, dhg_ref, dhu_ref, dwg_ref, dwu_ref, dwd_ref, accg, accu, accd):
        t = pl.program_id(1)

        @pl.when(t == 0)
        def _():
            accg[...] = jnp.zeros_like(accg)
            accu[...] = jnp.zeros_like(accu)
            accd[...] = jnp.zeros_like(accd)

        nv = n_ref[...]
        hgv = hg_ref[...].astype(F32)
        act = (hgv * _sigmoid(hgv) * hu_ref[...].astype(F32)).astype(BF16)
        accg[...] += _dot_tn(dhg_ref[...], nv)
        accu[...] += _dot_tn(dhu_ref[...], nv)
        accd[...] += _dot_tn(act, df_ref[...])

        @pl.when(t == nt - 1)
        def _():
            dwg_ref[...] = accg[...].astype(BF16)
            dwu_ref[...] = accu[...].astype(BF16)
            dwd_ref[...] = accd[...].astype(BF16)

    act_spec = pl.BlockSpec((None, tk, Fs), lambda k, t: (k, t, 0))
    w_spec = pl.BlockSpec((None, Fs, D), lambda k, t: (k, 0, 0))
    row = pl.BlockSpec((tk, D), lambda k, t: (t, 0))
    return _pallas(
        body, name, (K, nt),
        [row, row, act_spec, act_spec, act_spec, act_spec],
        [w_spec, w_spec, w_spec],
        [jax.ShapeDtypeStruct((K, Fs, D), BF16)] * 3,
        [pltpu.VMEM((Fs, D), F32)] * 3,
        (n, df, hg, hu, dhg, dhu), comm)


def _mix_proj_fwd(x, g, wproj_t, wf_t, tm, tn, comm=None):
    T, D = x.shape
    N = wproj_t.shape[0]

    def body(x_ref, g_ref, w_ref, wf_ref, h_ref, proj_ref, flog_ref, h_scr):
        @pl.when(pl.program_id(1) == 0)
        def _():
            xhat, _ = _rms(x_ref[...])
            h = (xhat * g_ref[...]).astype(BF16)
            h_scr[...] = h
            h_ref[...] = h
            flog_ref[...] = _dot_nt(h, wf_ref[...])

        proj_ref[...] = _dot_nt(h_scr[...], w_ref[...]).astype(BF16)

    return _pallas(
        body, "mix_proj_fwd", (T // tm, N // tn),
        [pl.BlockSpec((tm, D), lambda i, n: (i, 0)), pl.BlockSpec((1, D), lambda i, n: (0, 0)),
         pl.BlockSpec((tn, D), lambda i, n: (n, 0)), pl.BlockSpec((LANES, D), lambda i, n: (0, 0))],
        [pl.BlockSpec((tm, D), lambda i, n: (i, 0)), pl.BlockSpec((tm, tn), lambda i, n: (i, n)),
         pl.BlockSpec((tm, LANES), lambda i, n: (i, 0))],
        [jax.ShapeDtypeStruct((T, D), BF16), jax.ShapeDtypeStruct((T, N), BF16),
         jax.ShapeDtypeStruct((T, LANES), F32)],
        [pltpu.VMEM((tm, D), BF16)],
        (x, g, wproj_t, wf_t), comm)


def _log_sigmoid(z):
    return -(jnp.maximum(-z, 0.0) + jnp.log(1.0 + jnp.exp(-jnp.abs(z))))


def _tri(n, lower):
    r = lax.broadcasted_iota(jnp.int32, (n, n), 0)
    c = lax.broadcasted_iota(jnp.int32, (n, n), 1)
    return jnp.where((r >= c) if lower else (r <= c), 1.0, 0.0).astype(F32)


def _dot_f32(a, b):
    return lax.dot_general(a, b, (((1,), (0,)), ((), ())), preferred_element_type=F32,
                           precision=lax.Precision.HIGHEST)


def _fgate_fwd(flog, bias, B, S, ch):
    def body(flog_ref, b_ref, cum_ref):
        tri = _tri(ch, True)
        carry = jnp.zeros((1, LANES), F32)
        for c0 in range(0, S, ch):
            lf = _log_sigmoid(flog_ref[c0:c0 + ch, :] + b_ref[...])
            cs = _dot_f32(tri, lf) + carry
            cum_ref[c0:c0 + ch, :] = cs
            carry = cs[ch - 1:ch, :]

    return pl.pallas_call(
        body, name="fgate_fwd", grid=(B,),
        in_specs=[pl.BlockSpec((S, LANES), lambda b: (b, 0)),
                  pl.BlockSpec((1, LANES), lambda b: (0, 0))],
        out_specs=pl.BlockSpec((S, LANES), lambda b: (b, 0)),
        out_shape=jax.ShapeDtypeStruct((B * S, LANES), F32),
        compiler_params=_params(("arbitrary",)),
    )(flog, bias)


def _fgate_bwd(dcum, flog, bias, B, S, ch):
    def body(dcum_ref, flog_ref, b_ref, dflog_ref, db_ref):
        @pl.when(pl.program_id(0) == 0)
        def _():
            db_ref[...] = jnp.zeros_like(db_ref)

        tri = _tri(ch, False)
        carry = jnp.zeros((1, LANES), F32)
        db = jnp.zeros((1, LANES), F32)
        for c0 in range(S - ch, -1, -ch):
            dlf = _dot_f32(tri, dcum_ref[c0:c0 + ch, :]) + carry
            carry = dlf[0:1, :]
            z = flog_ref[c0:c0 + ch, :] + b_ref[...]
            dz = dlf * _sigmoid(-z)
            dflog_ref[c0:c0 + ch, :] = dz
            db = db + jnp.sum(dz, axis=0, keepdims=True)
        db_ref[...] += db

    return pl.pallas_call(
        body, name="fgate_bwd", grid=(B,),
        in_specs=[pl.BlockSpec((S, LANES), lambda b: (b, 0)),
                  pl.BlockSpec((S, LANES), lambda b: (b, 0)),
                  pl.BlockSpec((1, LANES), lambda b: (0, 0))],
        out_specs=[pl.BlockSpec((S, LANES), lambda b: (b, 0)),
                   pl.BlockSpec((1, LANES), lambda b: (0, 0))],
        out_shape=[jax.ShapeDtypeStruct((B * S, LANES), F32),
                   jax.ShapeDtypeStruct((1, LANES), F32)],
        compiler_params=_params(("arbitrary",)),
    )(dcum, flog, bias)


def _pick_lane(tile, h):
    lane = lax.broadcasted_iota(jnp.int32, tile.shape, 1)
    return jnp.sum(jnp.where(lane == h, tile, 0.0), axis=1, keepdims=True)


def _put_lane(col, h, width=LANES):
    lane = lax.broadcasted_iota(jnp.int32, (col.shape[0], width), 1)
    return jnp.where(lane == h, col, 0.0)


def _pick_row(tile, h):
    row = lax.broadcasted_iota(jnp.int32, tile.shape, 0)
    return jnp.sum(jnp.where(row == h, tile, 0.0), axis=0, keepdims=True)


def _put_row(vec, h):
    row = lax.broadcasted_iota(jnp.int32, (8, vec.shape[1]), 0)
    return jnp.where(row == h, vec, 0.0)


def _causal(tq):
    r = lax.broadcasted_iota(jnp.int32, (tq, tq), 0)
    c = lax.broadcasted_iota(jnp.int32, (tq, tq), 1)
    return r >= c


def _head_halves(t):
    lo = lax.broadcasted_iota(jnp.int32, t.shape, 1) < HEAD_DIM
    zero = jnp.zeros_like(t)
    return jnp.where(lo, t, zero), jnp.where(lo, zero, t)


NEG = -1e30
ATTN_SCALE = 1.0 / math.sqrt(HEAD_DIM)


def _scaled(q):
    return (q.astype(F32) * ATTN_SCALE).astype(q.dtype)


def _attn_fwd(proj, cum, cum_t, B, S, tq, comm=None):
    nq = S // tq

    def body(q_ref, k_ref, v_ref, cum_ref, cumt_ref, o_ref, lse_ref):
        qi, hp = pl.program_id(1), pl.program_id(2)
        qm = _head_halves(_scaled(q_ref[...]))
        cumv = cum_ref[...]
        cq = [_pick_lane(cumv, 2 * hp + e) for e in range(2)]

        def tile(j, carry, masked):
            off = pl.multiple_of(j * tq, tq)
            kj = k_ref[pl.ds(off, tq), :]
            vj = v_ref[pl.ds(off, tq), :]
            ct = cumt_ref[j]
            new = []
            for e in range(2):
                m, l, acc = carry[e]
                s = _dot_nt(qm[e], kj) - _pick_row(ct, 2 * hp + e)
                if masked:
                    s = jnp.where(_causal(tq), s, NEG)
                m_new = jnp.maximum(m, jnp.max(s, axis=1, keepdims=True))
                p = jnp.exp(s - m_new)
                alpha = jnp.exp(m - m_new)
                l = alpha * l + jnp.sum(p, axis=1, keepdims=True)
                acc = alpha * acc + _dot(p.astype(BF16), vj)
                new.append((m_new, l, acc))
            return tuple(new)

        one = (jnp.full((tq, 1), NEG, F32), jnp.zeros((tq, 1), F32), jnp.zeros((tq, LANES), F32))
        carry = lax.fori_loop(0, qi, lambda j, c: tile(j, c, False), (one, one))
        (ma, la, acca), (mb, lb, accb) = tile(qi, carry, True)
        lo = lax.broadcasted_iota(jnp.int32, (tq, LANES), 1) < HEAD_DIM
        o_ref[...] = jnp.where(lo, acca / la, accb / lb).astype(BF16)

        @pl.when(hp == 0)
        def _():
            lse_ref[...] = jnp.zeros_like(lse_ref)

        lse_ref[...] += (_put_lane(ma + jnp.log(la) + cq[0], 2 * hp) + _put_lane(mb + jnp.log(lb) + cq[1], 2 * hp + 1))

    kv = lambda first: pl.BlockSpec((S, LANES), lambda b, i, hp: (b, first + hp))
    return _pallas(
        body, "attn_fwd", (B, nq, HEAD_PAIRS),
        [pl.BlockSpec((tq, LANES), lambda b, i, hp: (b * nq + i, hp)),
         kv(ATTN_W // LANES), kv(2 * ATTN_W // LANES),
         pl.BlockSpec((tq, LANES), lambda b, i, hp: (b * nq + i, 0)),
         pl.BlockSpec((None, nq, 8, tq), lambda b, i, hp: (b, 0, 0, 0))],
        [pl.BlockSpec((tq, LANES), lambda b, i, hp: (b * nq + i, hp)),
         pl.BlockSpec((tq, LANES), lambda b, i, hp: (b * nq + i, 0))],
        [jax.ShapeDtypeStruct((B * S, ATTN_W), BF16), jax.ShapeDtypeStruct((B * S, LANES), F32)],
        [], (proj, proj, proj, cum, cum_t), comm)


def _attn_bwd(proj, o, do, lse, cum, cum_t, B, S, tq, comm=None):
    nq = S // tq

    def body(q_ref, k_ref, v_ref, o_ref, do_ref, lse_ref, cum_ref, cumt_ref,
             dq_ref, dk_ref, dv_ref, dcq_ref, dck_ref, dq_scr):
        hp, kj = pl.program_id(1), pl.program_id(2)

        @pl.when(kj == 0)
        def _():
            dq_scr[...] = jnp.zeros_like(dq_scr)

        @pl.when((kj == 0) & (hp == 0))
        def _():
            dcq_ref[...] = jnp.zeros_like(dcq_ref)
            dck_ref[...] = jnp.zeros_like(dck_ref)

        kv = k_ref[...]
        vv = v_ref[...]
        km = _head_halves(kv)
        ct = cumt_ref[...]
        ck = [_pick_row(ct, 2 * hp + e) for e in range(2)]

        def tile(i, carry, masked):
            dk, dv, dcol = carry
            off = pl.multiple_of(i * tq, tq)
            qi = q_ref[pl.ds(off, tq), :]
            ov = o_ref[pl.ds(off, tq), :].astype(F32)
            qm = _head_halves(_scaled(qi))
            dom = _head_halves(do_ref[pl.ds(off, tq), :])
            cumv = cum_ref[pl.ds(off, tq), :]
            lsev = lse_ref[pl.ds(off, tq), :]
            dcq = jnp.zeros((tq, LANES), F32)
            dq = jnp.zeros((tq, LANES), F32)
            dcol_new = []
            for e in range(2):
                delta = jnp.sum(dom[e].astype(F32) * ov, axis=1, keepdims=True)
                row_term = _pick_lane(cumv, 2 * hp + e) - _pick_lane(lsev, 2 * hp + e)
                p = jnp.exp(_dot_nt(qm[e], kv) + row_term - ck[e])
                if masked:
                    p = jnp.where(_causal(tq), p, 0.0)
                dv = dv + _dot_tn(dom[e], p.astype(BF16))
                ds = p * (_dot_nt(dom[e], vv) - delta)
                dcol_new.append(dcol[e] + jnp.sum(ds, axis=0, keepdims=True))
                dcq = dcq + _put_lane(jnp.sum(ds, axis=1, keepdims=True), 2 * hp + e)
                dsb = ds.astype(BF16)
                dk = dk + _dot_tn(qm[e], dsb)
                dq = dq + _dot(dsb, km[e]) * ATTN_SCALE
            dq_scr[pl.ds(off, tq), :] += dq
            dcq_ref[pl.ds(off, tq), :] += dcq
            return dk, dv, tuple(dcol_new)

        zero_row = jnp.zeros((1, tq), F32)
        init = (jnp.zeros((LANES, tq), F32), jnp.zeros((LANES, tq), F32), (zero_row, zero_row))
        carry = tile(kj, init, True)
        dk, dv, dcol = lax.fori_loop(kj + 1, nq, lambda i, c: tile(i, c, False), carry)
        dk_ref[...] = dk.T.astype(BF16)
        dv_ref[...] = dv.T.astype(BF16)
        dck_ref[kj] += -(_put_row(dcol[0], 2 * hp) + _put_row(dcol[1], 2 * hp + 1))

        @pl.when(kj == nq - 1)
        def _():
            dq_ref[...] = dq_scr[...].astype(BF16)

    seq = lambda first: pl.BlockSpec((S, LANES), lambda b, hp, j: (b, first + hp))
    tile_in = lambda first: pl.BlockSpec((tq, LANES), lambda b, hp, j: (b * nq + j, first + hp))
    lanes0 = pl.BlockSpec((S, LANES), lambda b, hp, j: (b, 0))
    out = jax.ShapeDtypeStruct((B * S, ATTN_W), BF16)
    return _pallas(
        body, "attn_bwd", (B, HEAD_PAIRS, nq),
        [seq(0), tile_in(ATTN_W // LANES), tile_in(2 * ATTN_W // LANES), seq(0), seq(0), lanes0, lanes0,
         pl.BlockSpec((None, None, 8, tq), lambda b, hp, j: (b, j, 0, 0))],
        [seq(0), tile_in(0), tile_in(0), lanes0,
         pl.BlockSpec((None, nq, 8, tq), lambda b, hp, j: (b, 0, 0, 0))],
        [out, out, out, jax.ShapeDtypeStruct((B * S, LANES), F32), jax.ShapeDtypeStruct((B, nq, 8, tq), F32)],
        [pltpu.VMEM((S, LANES), F32)],
        (proj, proj, proj, o, do, lse, cum, cum_t), comm)


def _shift_down(u, n):
    row = lax.broadcasted_iota(jnp.int32, u.shape, 0)
    return jnp.where(row >= n, pltpu.roll(u, n, 0), 0.0)


def _shift_up(u, n):
    rows = u.shape[0]
    row = lax.broadcasted_iota(jnp.int32, u.shape, 0)
    return jnp.where(row < rows - n, pltpu.roll(u, rows - n, 0), 0.0)


def _conv_specs(S):
    cb = pl.BlockSpec((S, LANES), lambda g, b: (b, COL_CB // LANES + g))
    cc = pl.BlockSpec((S, LANES), lambda g, b: (b, COL_CC // LANES + g))
    cx = pl.BlockSpec((S, LANES), lambda g, b: (b, COL_CX // LANES + g))
    w = pl.BlockSpec((8, LANES), lambda g, b: (0, g))
    return cb, cc, cx, w


def _conv_fwd(proj, conv_w, B, S):
    def body(cb_ref, cc_ref, cx_ref, w_ref, y_ref):
        u = cc_ref[...].astype(F32) * cx_ref[...].astype(F32)
        w = w_ref[...]
        conv = w[0:1, :] * _shift_down(u, 2) + w[1:2, :] * _shift_down(u, 1) + w[2:3, :] * u
        y_ref[...] = (cb_ref[...].astype(F32) * conv).astype(BF16)

    cb, cc, cx, w = _conv_specs(S)
    return pl.pallas_call(
        body, name="conv_fwd", grid=(CONV_W // LANES, B),
        in_specs=[cb, cc, cx, w],
        out_specs=pl.BlockSpec((S, LANES), lambda g, b: (b, g)),
        out_shape=jax.ShapeDtypeStruct((B * S, CONV_W), BF16),
        compiler_params=_params(("arbitrary", "arbitrary")),
    )(proj, proj, proj, conv_w)


def _conv_bwd(dy, proj, conv_w, B, S):
    def body(dy_ref, cb_ref, cc_ref, cx_ref, w_ref, dcb_ref, dcc_ref, dcx_ref, dw_ref):
        @pl.when(pl.program_id(1) == 0)
        def _():
            dw_ref[...] = jnp.zeros_like(dw_ref)

        ccv = cc_ref[...].astype(F32)
        cxv = cx_ref[...].astype(F32)
        u = ccv * cxv
        u1 = _shift_down(u, 1)
        u2 = _shift_down(u, 2)
        w = w_ref[...]
        conv = w[0:1, :] * u2 + w[1:2, :] * u1 + w[2:3, :] * u
        dyv = dy_ref[...].astype(F32)
        dcb_ref[...] = (dyv * conv).astype(BF16)
        dconv = dyv * cb_ref[...].astype(F32)
        du = w[2:3, :] * dconv + w[1:2, :] * _shift_up(dconv, 1) + w[0:1, :] * _shift_up(dconv, 2)
        dcc_ref[...] = (du * cxv).astype(BF16)
        dcx_ref[...] = (du * ccv).astype(BF16)
        row = lax.broadcasted_iota(jnp.int32, (8, LANES), 0)
        dw = jnp.where(row == 0, jnp.sum(dconv * u2, axis=0, keepdims=True),
                       jnp.where(row == 1, jnp.sum(dconv * u1, axis=0, keepdims=True),
                                 jnp.where(row == 2, jnp.sum(dconv * u, axis=0, keepdims=True), 0.0)))
        dw_ref[...] += dw

    cb, cc, cx, w = _conv_specs(S)
    out = pl.BlockSpec((S, LANES), lambda g, b: (b, g))
    return pl.pallas_call(
        body, name="conv_bwd", grid=(CONV_W // LANES, B),
        in_specs=[out, cb, cc, cx, w],
        out_specs=[out, out, out, w],
        out_shape=[jax.ShapeDtypeStruct((B * S, CONV_W), BF16)] * 3 + [jax.ShapeDtypeStruct((8, CONV_W), F32)],
        compiler_params=_params(("arbitrary", "arbitrary")),
    )(dy, proj, proj, proj, conv_w)


def _gate_specs(tm, D):
    ga = pl.BlockSpec((tm, D), lambda i: (i, COL_GATES // D))
    gc = pl.BlockSpec((tm, D), lambda i: (i, COL_GATES // D + 1))
    return ga, gc


def _mix_out_fwd(x, o, yc, proj, woa, woc, wout, tm):
    T, D = x.shape

    def body(x_ref, o_ref, yc_ref, ga_ref, gc_ref, woa_ref, woc_ref, wout_ref, out_ref):
        ya = _dot(o_ref[...], woa_ref[...])
        yp = _dot(yc_ref[...], woc_ref[...])
        merged = _sigmoid(ga_ref[...].astype(F32)) * ya + _sigmoid(gc_ref[...].astype(F32)) * yp
        out_ref[...] = x_ref[...] + _dot(merged.astype(BF16), wout_ref[...])

    ga, gc = _gate_specs(tm, D)
    row = lambda w: pl.BlockSpec((tm, w), lambda i: (i, 0))
    whole = lambda a: pl.BlockSpec(a.shape, lambda i: (0, 0))
    return pl.pallas_call(
        body, name="mix_out_fwd", grid=(T // tm,),
        in_specs=[row(D), row(ATTN_W), row(CONV_W), ga, gc, whole(woa), whole(woc), whole(wout)],
        out_specs=row(D),
        out_shape=jax.ShapeDtypeStruct((T, D), F32),
        compiler_params=_params(("arbitrary",)),
    )(x, o, yc, proj, proj, woa, woc, wout)


def _mix_out_bwd(dx, o, yc, proj, woa, woc, wout, tm, comm=None):
    T, D = dx.shape
    nt = T // tm

    def body(dx_ref, o_ref, yc_ref, ga_ref, gc_ref, woa_ref, woc_ref, wout_ref,
             do_ref, dyc_ref, dg_ref, dwoa_ref, dwoc_ref, dwout_ref, acca, accc, acco):
        t = pl.program_id(0)

        @pl.when(t == 0)
        def _():
            acca[...] = jnp.zeros_like(acca)
            accc[...] = jnp.zeros_like(accc)
            acco[...] = jnp.zeros_like(acco)

        dxb = dx_ref[...].astype(BF16)
        ov, ycv = o_ref[...], yc_ref[...]
        ya = _dot(ov, woa_ref[...])
        yp = _dot(ycv, woc_ref[...])
        sa = _sigmoid(ga_ref[...].astype(F32))
        sc = _sigmoid(gc_ref[...].astype(F32))
        merged = (sa * ya + sc * yp).astype(BF16)
        dm = _dot_nt(dxb, wout_ref[...])
        dya = (dm * sa).astype(BF16)
        dyp = (dm * sc).astype(BF16)
        dg_ref[:, :D] = (dm * ya * sa * (1.0 - sa)).astype(BF16)
        dg_ref[:, D:] = (dm * yp * sc * (1.0 - sc)).astype(BF16)
        do_ref[...] = _dot_nt(dya, woa_ref[...]).astype(BF16)
        dyc_ref[...] = _dot_nt(dyp, woc_ref[...]).astype(BF16)
        acca[...] += _dot_tn(ov, dya)
        accc[...] += _dot_tn(ycv, dyp)
        acco[...] += _dot_tn(merged, dxb)

        @pl.when(t == nt - 1)
        def _():
            dwoa_ref[...] = acca[...].astype(BF16)
            dwoc_ref[...] = accc[...].astype(BF16)
            dwout_ref[...] = acco[...].astype(BF16)

    ga, gc = _gate_specs(tm, D)
    row = lambda w: pl.BlockSpec((tm, w), lambda i: (i, 0))
    whole = lambda a: pl.BlockSpec(a.shape, lambda i: (0, 0))
    return _pallas(
        body, "mix_out_bwd", (nt,),
        [row(D), row(ATTN_W), row(CONV_W), ga, gc, whole(woa), whole(woc), whole(wout)],
        [row(ATTN_W), row(CONV_W), row(2 * D), whole(woa), whole(woc), whole(wout)],
        [jax.ShapeDtypeStruct((T, ATTN_W), BF16), jax.ShapeDtypeStruct((T, CONV_W), BF16),
         jax.ShapeDtypeStruct((T, 2 * D), BF16),
         jax.ShapeDtypeStruct(woa.shape, BF16), jax.ShapeDtypeStruct(woc.shape, BF16),
         jax.ShapeDtypeStruct(wout.shape, BF16)],
        [pltpu.VMEM(woa.shape, F32), pltpu.VMEM(woc.shape, F32), pltpu.VMEM(wout.shape, F32)],
        (dx, o, yc, proj, proj, woa, woc, wout), comm)


def _proj_pieces(dq, dk, dv, dcb, dcc, dcx, dgates, dflog):
    D = dgates.shape[1] // 2
    return [(dq, ATTN_W, 0), (dk, ATTN_W, 0), (dv, ATTN_W, 0), (dcb, CONV_W, 0), (dcc, CONV_W, 0), (dcx, CONV_W, 0),
            (dgates, D, 0), (dgates, D, 1), (dflog, LANES, 0)]


def _mix_proj_bwd_dx(dres, x, g, pieces, wproj_t, wf_t, tm, comm=None):
    T, D = x.shape
    n = len(pieces)
    w_blocks = [(ATTN_W, 0), (ATTN_W, 1), (ATTN_W, 2), (CONV_W, 3), (CONV_W, 4), (CONV_W, 5),
                (D, COL_GATES // D), (D, COL_GATES // D + 1)]

    def body(*refs):
        dres_ref, x_ref, g_ref = refs[:3]
        p_refs, w_refs = refs[3:3 + n], refs[3 + n:3 + 2 * n]
        dx_ref, dg_ref = refs[3 + 2 * n:]

        @pl.when(pl.program_id(0) == 0)
        def _():
            dg_ref[...] = jnp.zeros_like(dg_ref)

        dh = _dot(p_refs[0][...].astype(BF16), w_refs[0][...])
        for p_ref, w_ref in zip(p_refs[1:], w_refs[1:]):
            dh = dh + _dot(p_ref[...].astype(BF16), w_ref[...])
        xhat, inv = _rms(x_ref[...])
        dx, dg = _rms_bwd(dh, xhat, inv, g_ref[...])
        dx_ref[...] = dres_ref[...] + dx
        dg_ref[...] += dg

    row = pl.BlockSpec((tm, D), lambda i: (i, 0))
    vec = pl.BlockSpec((1, D), lambda i: (0, 0))
    p_specs = [pl.BlockSpec((tm, w), lambda i, cb=cb: (i, cb)) for _, w, cb in pieces]
    w_specs = [pl.BlockSpec((r, D), lambda i, rb=rb: (rb, 0)) for r, rb in w_blocks]
    w_specs.append(pl.BlockSpec((LANES, D), lambda i: (0, 0)))
    return _pallas(
        body, "mix_proj_bwd_dx", (T // tm,),
        [row, row, vec] + p_specs + w_specs, [row, vec],
        [jax.ShapeDtypeStruct((T, D), F32), jax.ShapeDtypeStruct((1, D), F32)], [],
        (dres, x, g, *[p for p, _, _ in pieces], *([wproj_t] * len(w_blocks)), wf_t), comm)


def _matmuls_tn(name, pieces, b, tk):
    T, N = b.shape
    nt = T // tk
    n = len(pieces)

    def body(*refs):
        a_refs, b_ref, out_refs, accs = refs[:n], refs[n], refs[n + 1:2 * n + 1], refs[2 * n + 1:]
        t = pl.program_id(0)

        @pl.when(t == 0)
        def _():
            for acc in accs:
                acc[...] = jnp.zeros_like(acc)

        bv = b_ref[...]
        for a_ref, acc in zip(a_refs, accs):
            acc[...] += _dot_tn(a_ref[...].astype(BF16), bv)

        @pl.when(t == nt - 1)
        def _():
            for out_ref, acc in zip(out_refs, accs):
                out_ref[...] = acc[...].astype(BF16)

    return pl.pallas_call(
        body, name=name, grid=(nt,),
        in_specs=[pl.BlockSpec((tk, w), lambda t, cb=cb: (t, cb)) for _, w, cb in pieces]
        + [pl.BlockSpec((tk, N), lambda t: (t, 0))],
        out_specs=[pl.BlockSpec((w, N), lambda t: (0, 0)) for _, w, _ in pieces],
        out_shape=[jax.ShapeDtypeStruct((w, N), BF16) for _, w, _ in pieces],
        scratch_shapes=[pltpu.VMEM((w, N), F32) for _, w, _ in pieces],
        compiler_params=_params(("arbitrary",)),
    )(*[a for a, _, _ in pieces], b)


TOKEN_TILE = 512
TOKEN_TILE_WIDE = 1024
ATTN_TILE = 512
SCAN_CHUNK = 256
PROJ_DX_TILE = 256


def _local_step(x, target, plan, B, S):
    T, D = x.shape
    tm = min(TOKEN_TILE, T)
    tm_fwd = min(TOKEN_TILE_WIDE, T)
    tq = min(ATTN_TILE, S)
    nq = S // tq
    ch = min(SCAN_CHUNK, S)

    def riding(kernel_name, build):
        results, brought = build(plan.rider(kernel_name))
        plan.arrived(kernel_name, brought)
        return results

    hg1, hu1, n1 = plan.ffn1_up(x, tm_fwd)
    w1 = plan.weights("ffn1")
    x1, = riding("ffn1_down", lambda comm: _ffn_down("ffn1_down", x, hg1, hu1, w1["ffn1_down"], tm_fwd, comm))
    wm = plan.weights("mix_in")
    h, proj, flog = riding("mix_proj_fwd", lambda comm: _mix_proj_fwd(
        x1, wm["mix_norm"], wm["w_proj"], wm["w_f"], tm_fwd, PROJ_W // 4, comm))
    wm.update(plan.weights("mix_out"))
    cum = _fgate_fwd(flog, wm["b_forget"], B, S, ch)
    cum_t = jnp.transpose(cum[:, :N_HEADS].reshape(B, nq, tq, N_HEADS), (0, 1, 3, 2))
    o, lse = riding("attn_fwd", lambda comm: _attn_fwd(proj, cum, cum_t, B, S, tq, comm))
    yc = _conv_fwd(proj, wm["conv_w"], B, S)
    x2 = _mix_out_fwd(x1, o, yc, proj, wm["w_o_attn"], wm["w_o_conv"], wm["w_out"], tm)
    w2 = plan.weights("ffn2")
    dx3, hg2, hu2, n2, loss, d_final_norm = _ffn_fwd_loss(
        "ffn2_fwd_loss", x2, w2["ffn2_norm"], w2["ffn2_gate"], w2["ffn2_up"], w2["ffn2_down"], target, w2["final_norm"],
        tm_fwd)

    g = {"final_norm": d_final_norm}
    dx2, dhg2, dhu2, g["ffn2_norm"], df2 = _ffn_bwd_dx("ffn2_bwd_dx", dx3, x2, w2["ffn2_norm"], hg2, hu2,
                                                  w2["ffn2_gate"], w2["ffn2_up"], w2["ffn2_down"], tm_fwd)[0]
    plan.reduce("ffn2", dict(zip(("ffn2_gate", "ffn2_up", "ffn2_down"),
                                 _ffn_bwd_dw("ffn2_bwd_dw", n2, df2, hg2, hu2, dhg2, dhu2, tm_fwd)[0])))
    do, dyc, dgates, dwoa, dwoc, dwout = riding("mix_out_bwd", lambda comm: _mix_out_bwd(
        dx2, o, yc, proj, wm["w_o_attn"], wm["w_o_conv"], wm["w_out"], tm, comm))
    plan.reduce("out", dict(w_o_attn=_shard_cols(dwoa), w_o_conv=_shard_cols(dwoc), w_out=dwout.reshape(N_CHIPS, -1, D)))
    dq, dk, dv, dcq, dck = riding("attn_bwd", lambda comm: _attn_bwd(proj, o, do, lse, cum, cum_t, B, S, tq, comm))
    dcum = dcq + jnp.pad(jnp.transpose(dck, (0, 1, 3, 2)).reshape(T, N_HEADS), ((0, 0), (0, LANES - N_HEADS)))
    dflog, g["b_forget"] = _fgate_bwd(dcum, flog, wm["b_forget"], B, S, ch)
    dcb, dcc, dcx, g["conv_w"] = _conv_bwd(dyc, proj, wm["conv_w"], B, S)
    pieces = _proj_pieces(dq, dk, dv, dcb, dcc, dcx, dgates, dflog)
    dwq, dwk, dwv, dwcb, dwcc, dwcx = _matmuls_tn("mix_dw_a", pieces[:6], h, tm)
    dwga, dwgc, dwf = _matmuls_tn("mix_dw_b", pieces[6:], h, tm)
    dwin_t = jnp.concatenate([dwq, dwk, dwv, dwf[:N_HEADS], dwcb, dwcc, dwcx, dwga, dwgc], axis=0)
    plan.reduce("w_in", {"w_in": dwin_t.reshape(N_CHIPS, -1, D)})
    dx1, g["mix_norm"] = riding("mix_proj_bwd_dx", lambda comm: _mix_proj_bwd_dx(
        dx2, x1, wm["mix_norm"], pieces, wm["w_proj"], wm["w_f"], min(PROJ_DX_TILE, T), comm))
    grad_x, dhg1, dhu1, g["ffn1_norm"], df1 = _ffn_bwd_dx(
        "ffn1_bwd_dx", dx1, x, w1["ffn1_norm"], hg1, hu1, w1["ffn1_gate"], w1["ffn1_up"], w1["ffn1_down"], tm_fwd)[0]
    plan.reduce_small(g, loss)
    plan.reduce("ffn1", dict(zip(("ffn1_gate", "ffn1_up", "ffn1_down"), riding("ffn1_bwd_dw", lambda comm: _ffn_bwd_dw(
        "ffn1_bwd_dw", n1, df1, hg1, hu1, dhg1, dhu1, tm_fwd, comm)))))
    return loss, grad_x, g


TRANSPOSED = ("ffn1_gate", "ffn1_up", "ffn2_gate", "ffn2_up", "w_in")
NORMS = ("ffn1_norm", "mix_norm", "ffn2_norm", "final_norm")


def _unshard_cols(a):
    return jnp.transpose(a, (1, 0, 2)).reshape(a.shape[1], N_CHIPS * a.shape[2])


def _shard_cols(a):
    return jnp.transpose(a.reshape(a.shape[0], N_CHIPS, a.shape[1] // N_CHIPS), (1, 0, 2))


def _layout_ffn(which):
    def layout(st, small):
        w = {n: st[n] for n in (which + "_gate", which + "_up", which + "_down")}
        w[which + "_norm"] = small[which + "_norm"].reshape(1, -1)
        if which == "ffn2":
            w["final_norm"] = small["final_norm"].reshape(1, -1)
        return w
    return layout


def _layout_mix_in(st, small):
    win_t = st["w_in"].reshape(-1, st["w_in"].shape[2])
    return {
        "w_proj": jnp.concatenate([win_t[:N_FORGET_COL], win_t[N_FORGET_COL + N_HEADS:]], axis=0),
        "w_f": jnp.pad(win_t[N_FORGET_COL:N_FORGET_COL + N_HEADS], ((0, LANES - N_HEADS), (0, 0))),
        "conv_w": _unshard_cols(st["conv_w"]),
        "mix_norm": small["mix_norm"].reshape(1, -1),
        "b_forget": jnp.pad(small["b_forget"].reshape(1, -1), ((0, 0), (0, LANES - N_HEADS))),
    }


def _layout_mix_out(st, small):
    return {"w_o_attn": _unshard_cols(st["w_o_attn"]), "w_o_conv": _unshard_cols(st["w_o_conv"]),
            "w_out": st["w_out"].reshape(-1, st["w_out"].shape[2])}


_LAYOUTS = {"ffn1": _layout_ffn("ffn1"), "mix_in": _layout_mix_in, "mix_out": _layout_mix_out, "ffn2": _layout_ffn("ffn2")}


ANY = pl.BlockSpec(memory_space=pl.ANY)
BIG = ("ffn1_gate", "ffn1_up", "ffn1_down", "w_in", "w_o_attn", "w_o_conv", "w_out",
       "ffn2_gate", "ffn2_up", "ffn2_down")


def _place():
    x, y, c = lax.axis_index("x"), lax.axis_index("y"), lax.axis_index("c")
    others = [(1 - x, y), (x, 1 - y), (1 - x, 1 - y)]
    return x, y, c, others


def _col_halves(cols, c):
    hc = cols // 2
    return pl.ds(pl.multiple_of(c * hc, LANES), hc), pl.ds(pl.multiple_of((1 - c) * hc, LANES), hc)


def _gather_comm(shards, conv_shard=None):
    n = len(shards)
    inputs = list(shards) + ([] if conv_shard is None else [conv_shard])

    def copies(ins, outs, sems):
        send_sems, recv_sems, pass_send, pass_recv = sems[:4]
        x, y, c, others = _place()

        def chip_copy(a, j, chip):
            mine, _ = _col_halves(ins[a].shape[1], c)
            return pltpu.make_async_remote_copy(
                src_ref=ins[a].at[:, mine], dst_ref=outs[a].at[chip, :, mine],
                send_sem=send_sems.at[3 * a + j], recv_sem=recv_sems.at[3 * a + j],
                device_id=(*others[j], c), device_id_type=MESH)

        def pass_copy(a, j, chip, half):
            return pltpu.make_async_remote_copy(
                src_ref=outs[a].at[chip, :, half], dst_ref=outs[a].at[chip, :, half],
                send_sem=pass_send.at[3 * a + j], recv_sem=pass_recv.at[3 * a + j],
                device_id=(x, y, 1 - c), device_id_type=MESH)

        def conv_copy(j, chip):
            return pltpu.make_async_remote_copy(
                src_ref=ins[n], dst_ref=outs[n].at[chip],
                send_sem=sems[4].at[j], recv_sem=sems[5].at[j],
                device_id=(*others[j], c), device_id_type=MESH)

        me = 2 * x + y
        sends = [chip_copy(a, j, me) for a in range(n) for j in range(3)]
        if conv_shard is not None:
            sends += [conv_copy(j, me) for j in range(3)]
        return c, others, sends, chip_copy, pass_copy, conv_copy

    def start(ins, outs, sems):
        for cp in copies(ins, outs, sems)[2]:
            cp.start()

    def finish(ins, outs, sems):
        c, others, sends, chip_copy, pass_copy, conv_copy = copies(ins, outs, sems)
        passed = []
        for a in range(n):
            mine, _ = _col_halves(ins[a].shape[1], c)
            for j, (ox, oy) in enumerate(others):
                chip_copy(a, j, 2 * ox + oy).wait_recv()
                passed.append(pass_copy(a, j, 2 * ox + oy, mine))
                passed[-1].start()
        for a in range(n):
            _, theirs = _col_halves(ins[a].shape[1], c)
            for j, (ox, oy) in enumerate(others):
                pass_copy(a, j, 2 * ox + oy, theirs).wait_recv()
        if conv_shard is not None:
            for j, (ox, oy) in enumerate(others):
                conv_copy(j, 2 * ox + oy).wait_recv()
        for cp in sends + passed:
            cp.wait_send()

    scratch = [pltpu.SemaphoreType.DMA((3 * n,))] * 4
    if conv_shard is not None:
        scratch += [pltpu.SemaphoreType.DMA((3,))] * 2
    return _Comm(inputs, [jax.ShapeDtypeStruct((N_CHIPS,) + s.shape, s.dtype) for s in inputs], scratch, start, finish)


def _fill_own(stacks, shards):
    chip = 2 * lax.axis_index("x") + lax.axis_index("y")
    return [lax.dynamic_update_index_in_dim(st, s, chip, 0) for st, s in zip(stacks, shards)]


def _run_comm(name, comm):
    ci, co = len(comm.inputs), len(comm.out_shape)

    def body(*refs):
        comm.start(refs[:ci], refs[ci:ci + co], refs[ci + co:])
        comm.finish(refs[:ci], refs[ci:ci + co], refs[ci + co:])

    return pl.pallas_call(body, name=name, in_specs=[ANY] * ci, out_specs=[ANY] * co, out_shape=comm.out_shape,
                          scratch_shapes=comm.scratch)(*comm.inputs)


def _sibling_exchange_comm(grads):
    n = len(grads)

    def copies(ins, outs, sems):
        x, y, c, _ = _place()
        return [pltpu.make_async_remote_copy(
            src_ref=ins[a].at[:, :, _col_halves(ins[a].shape[2], c)[1]], dst_ref=outs[a],
            send_sem=sems[0].at[a], recv_sem=sems[1].at[a],
            device_id=(x, y, 1 - c), device_id_type=MESH) for a in range(n)]

    def start(ins, outs, sems):
        for cp in copies(ins, outs, sems):
            cp.start()

    def finish(ins, outs, sems):
        for cp in copies(ins, outs, sems):
            cp.wait()

    half = lambda s: jax.ShapeDtypeStruct((s.shape[0], s.shape[1], s.shape[2] // 2), s.dtype)
    return _Comm(grads, [half(s) for s in grads], [pltpu.SemaphoreType.DMA((n,))] * 2, start, finish)


def _merge_comms(comms):
    def split(refs, count):
        out, at = [], 0
        for cm in comms:
            out.append(refs[at:at + count(cm)])
            at += count(cm)
        return out

    def parts(ins, outs, sems):
        return zip(comms, split(ins, lambda cm: len(cm.inputs)), split(outs, lambda cm: len(cm.out_shape)),
                   split(sems, lambda cm: len(cm.scratch)))

    def start(ins, outs, sems):
        for cm, i, o, s in parts(ins, outs, sems):
            cm.start(i, o, s)

    def finish(ins, outs, sems):
        for cm, i, o, s in parts(ins, outs, sems):
            cm.finish(i, o, s)

    return _Comm(sum([cm.inputs for cm in comms], []), sum([cm.out_shape for cm in comms], []),
                 sum([cm.scratch for cm in comms], []), start, finish)


def _add_halves(name, grads, recvs, core):
    n = len(grads)

    def body(core_ref, *refs):
        for g_ref, r_ref, out_ref in zip(refs[:n], refs[n:2 * n], refs[2 * n:]):
            out_ref[...] = (g_ref[...].astype(F32) + r_ref[...].astype(F32)).astype(BF16)

    half = lambda g: pl.BlockSpec((None, g.shape[1], g.shape[2] // 2), lambda k, core_ref: (k, 0, 0))
    mine = lambda g: pl.BlockSpec((None, g.shape[1], g.shape[2] // 2), lambda k, core_ref: (k, 0, core_ref[0]))
    return pl.pallas_call(
        body, name=name,
        grid_spec=pltpu.PrefetchScalarGridSpec(
            num_scalar_prefetch=1, grid=(N_CHIPS,),
            in_specs=[mine(g) for g in grads] + [half(g) for g in grads],
            out_specs=[half(g) for g in grads]),
        out_shape=[jax.ShapeDtypeStruct(r.shape, BF16) for r in recvs],
        compiler_params=_params(("arbitrary",)),
    )(core, *grads, *recvs)


def _chip_exchange_comm(parts):
    n = len(parts)

    def copies(ins, outs, sems):
        x, y, c, others = _place()
        return [pltpu.make_async_remote_copy(
            src_ref=ins[a].at[2 * ox + oy], dst_ref=outs[a].at[j],
            send_sem=sems[0].at[3 * a + j], recv_sem=sems[1].at[3 * a + j],
            device_id=(ox, oy, c), device_id_type=MESH) for a in range(n) for j, (ox, oy) in enumerate(others)]

    def start(ins, outs, sems):
        for cp in copies(ins, outs, sems):
            cp.start()

    def finish(ins, outs, sems):
        for cp in copies(ins, outs, sems):
            cp.wait()

    return _Comm(parts, [jax.ShapeDtypeStruct((3,) + s.shape[1:], s.dtype) for s in parts],
                 [pltpu.SemaphoreType.DMA((3 * n,))] * 2, start, finish)


HBM = pl.BlockSpec(memory_space=pltpu.HBM)
SEM = pl.BlockSpec(memory_space=pltpu.SEMAPHORE)


def _split_exchange_copies(parts, lands, send_sems, recv_sems):
    x, y, c, others = _place()
    return [pltpu.make_async_remote_copy(
        src_ref=parts[a].at[2 * ox + oy], dst_ref=lands[a].at[j],
        send_sem=send_sems.at[3 * a + j], recv_sem=recv_sems.at[3 * a + j],
        device_id=(ox, oy, c), device_id_type=MESH) for a in range(len(parts)) for j, (ox, oy) in enumerate(others)]


def _exchange_start(name, parts):
    n = len(parts)

    def body(*refs):
        ins, lands = refs[:n], refs[n:2 * n]
        send_sems, recv_sems, token = refs[2 * n], refs[2 * n + 1], refs[-1]
        for cp in _split_exchange_copies(ins, lands, send_sems, recv_sems):
            cp.start()
        token[...] = jnp.zeros_like(token)

    land_shape = [(3,) + p.shape[1:] for p in parts]
    outs = pl.pallas_call(
        body, name=name,
        out_shape=[pltpu.SemaphoreType.DMA((3 * n,)), pltpu.SemaphoreType.DMA((3 * n,))]
        + [pltpu.HBM(p.shape, p.dtype) for p in parts] + [pltpu.HBM(s, p.dtype) for s, p in zip(land_shape, parts)]
        + [jax.ShapeDtypeStruct((8, LANES), F32)],
        in_specs=[HBM] * (2 * n), out_specs=[SEM, SEM] + [HBM] * (2 * n) + [pl.BlockSpec(memory_space=pltpu.VMEM)],
        input_output_aliases={i: 2 + i for i in range(2 * n)},
        compiler_params=pltpu.CompilerParams(has_side_effects=pltpu.SideEffectType.DATAFLOW_SIDE_EFFECTING),
    )(*[pltpu.with_memory_space_constraint(p, pltpu.HBM) for p in parts],
      *[pltpu.with_memory_space_constraint(lax.empty(s, p.dtype), pltpu.HBM) for s, p in zip(land_shape, parts)])
    return outs[0], outs[1], list(outs[2:2 + n]), list(outs[2 + n:2 + 2 * n]), outs[-1]


def _exchange_wait(name, send_sems, recv_sems, parts, lands, after):
    n = len(parts)

    def body(*refs):
        ins, zones = refs[:n], refs[n:2 * n]
        for cp in _split_exchange_copies(ins, zones, refs[2 * n], refs[2 * n + 1]):
            cp.wait_send()
            cp.wait_recv()

    outs = pl.pallas_call(
        body, name=name,
        out_shape=[pltpu.HBM(p.shape, p.dtype) for p in parts] + [pltpu.HBM(z.shape, z.dtype) for z in lands],
        in_specs=[HBM] * (2 * n) + [SEM, SEM] + [ANY] * len(after), out_specs=[HBM] * (2 * n),
        input_output_aliases={i: i for i in range(2 * n)},
        compiler_params=pltpu.CompilerParams(has_side_effects=pltpu.SideEffectType.DATAFLOW_SIDE_EFFECTING),
    )(*parts, *lands, send_sems, recv_sems, *after)
    return list(outs[:n]), list(outs[n:])


def _sum_chips(name, owns, recvs, chip, after):
    n = len(owns)
    hc = owns[0].shape[2]
    assert all(o.shape[2] == hc for o in owns)

    def body(chip_ref, *refs):
        for own_ref, recv_ref, out_ref in zip(refs[:n], refs[n:2 * n], refs[2 * n + 1:]):
            acc = own_ref[...].astype(F32)
            for j in range(3):
                acc = acc + recv_ref[j].astype(F32)
            out_ref[...] = acc

    return pl.pallas_call(
        body, name=name,
        grid_spec=pltpu.PrefetchScalarGridSpec(
            num_scalar_prefetch=1, grid=(hc // LANES,),
            in_specs=[pl.BlockSpec((None, o.shape[1], LANES), lambda i, chip_ref: (chip_ref[0], 0, i)) for o in owns]
            + [pl.BlockSpec((3, o.shape[1], LANES), lambda i, chip_ref: (0, 0, i)) for o in owns]
            + [pl.BlockSpec((8, LANES), lambda i, chip_ref: (0, 0))],
            out_specs=[pl.BlockSpec((o.shape[1], LANES), lambda i, chip_ref: (0, i)) for o in owns]),
        out_shape=[jax.ShapeDtypeStruct((o.shape[1], hc), F32) for o in owns],
        compiler_params=_params(("arbitrary",)),
    )(chip, *owns, *recvs, after)


def _share_halves(name, halves):
    n = len(halves)

    def body(*refs):
        srcs, dsts = refs[:n], refs[n:2 * n]
        send_sems, recv_sems = refs[2 * n:]
        x, y, c, _ = _place()
        copies = [pltpu.make_async_remote_copy(
            src_ref=srcs[a], dst_ref=dsts[a], send_sem=send_sems.at[a], recv_sem=recv_sems.at[a],
            device_id=(x, y, 1 - c), device_id_type=MESH) for a in range(n)]
        for cp in copies:
            cp.start()
        for cp in copies:
            cp.wait()

    return pl.pallas_call(
        body, name=name,
        in_specs=[ANY] * n, out_specs=[ANY] * n,
        out_shape=[jax.ShapeDtypeStruct(s.shape, s.dtype) for s in halves],
        scratch_shapes=[pltpu.SemaphoreType.DMA((n,)), pltpu.SemaphoreType.DMA((n,))],
    )(*halves)


def _small_gather_comm(part):
    def copies(ins, outs, sems):
        x, y, c, _ = _place()
        me = 4 * x + 2 * y + c
        both = []
        for d in range(1, N_DEV):
            px, py, pc = (1 - x if d & 4 else x, 1 - y if d & 2 else y, 1 - c if d & 1 else c)
            send = pltpu.make_async_remote_copy(
                src_ref=ins[0], dst_ref=outs[0].at[me], send_sem=sems[0].at[d - 1], recv_sem=sems[1].at[d - 1],
                device_id=(px, py, pc), device_id_type=MESH)
            recv = pltpu.make_async_remote_copy(
                src_ref=ins[0], dst_ref=outs[0].at[4 * px + 2 * py + pc], send_sem=sems[0].at[d - 1],
                recv_sem=sems[1].at[d - 1], device_id=(px, py, pc), device_id_type=MESH)
            both.append((send, recv))
        return both

    def start(ins, outs, sems):
        for send, _ in copies(ins, outs, sems):
            send.start()

    def finish(ins, outs, sems):
        for send, recv in copies(ins, outs, sems):
            recv.wait_recv()
            send.wait_send()

    return _Comm([part], [jax.ShapeDtypeStruct((N_DEV,) + part.shape, F32)],
                 [pltpu.SemaphoreType.DMA((N_DEV - 1,))] * 2, start, finish)


def _sum_devices(parts):
    def body(p_ref, out_ref):
        acc = p_ref[0]
        for k in range(1, N_DEV):
            acc = acc + p_ref[k]
        out_ref[...] = acc

    return pl.pallas_call(
        body, name="sum_devices", grid=(1,),
        in_specs=[pl.BlockSpec(parts.shape, lambda i: (0, 0, 0))],
        out_specs=pl.BlockSpec(parts.shape[1:], lambda i: (0, 0)),
        out_shape=jax.ShapeDtypeStruct(parts.shape[1:], F32),
        compiler_params=_params(("arbitrary",)),
    )(parts)


def _adam_update(w, g, m, v):
    nm = ADAM_B1 * m + (1.0 - ADAM_B1) * g
    nv = ADAM_B2 * v + (1.0 - ADAM_B2) * (g * g)
    m_hat = nm * (1.0 / (1.0 - ADAM_B1 ** ADAM_STEP))
    v_hat = nv * (1.0 / (1.0 - ADAM_B2 ** ADAM_STEP))
    return -ADAM_LR * (m_hat / (jnp.sqrt(v_hat) + ADAM_EPS) + ADAM_WD * w), nm, nv


def _adamw(name, w, g, m, v):
    def body(w_ref, g_ref, m_ref, v_ref, d_ref, nm_ref, nv_ref):
        d_ref[...], nm_ref[...], nv_ref[...] = _adam_update(w_ref[...], g_ref[...], m_ref[...], v_ref[...])

    spec = pl.BlockSpec(w.shape, lambda i: (0, 0))
    out = jax.ShapeDtypeStruct(w.shape, F32)
    return pl.pallas_call(
        body, name=name, grid=(1,),
        in_specs=[spec] * 4, out_specs=[spec] * 3, out_shape=[out] * 3,
        compiler_params=_params(("arbitrary",)),
    )(w, g, m, v)


def _adamw_halves(name, ws, mines, theirs, ms, vs, core):
    n = len(ws)
    cols = ws[0].shape[1]
    assert all(w.shape[1] == cols for w in ws)
    hc = cols // 2
    tc = LANES if n > 1 else min(256, hc)
    nt = hc // tc

    def body(core_ref, *refs):
        ins, outs = refs[:5 * n], refs[5 * n:]
        for a in range(n):
            w_ref, mine_ref, theirs_ref, m_ref, v_ref = [ins[j * n + a] for j in range(5)]
            g_ref, d_ref, nm_ref, nv_ref = outs[4 * a:4 * a + 4]
            gv = jnp.where(pl.program_id(0) == core_ref[0], mine_ref[...], theirs_ref[...])
            g_ref[...] = gv
            d_ref[...], nm_ref[...], nv_ref[...] = _adam_update(w_ref[...], gv, m_ref[...], v_ref[...])

    whole = lambda w: pl.BlockSpec((w.shape[0], tc), lambda h, i, core_ref: (0, h * nt + i))
    mine_spec = lambda w: pl.BlockSpec((w.shape[0], tc), lambda h, i, core_ref: (0, jnp.where(h == core_ref[0], i, 0)))
    theirs_spec = lambda w: pl.BlockSpec((w.shape[0], tc), lambda h, i, core_ref: (0, jnp.where(h == core_ref[0], 0, i)))
    outs = pl.pallas_call(
        body, name=name,
        grid_spec=pltpu.PrefetchScalarGridSpec(
            num_scalar_prefetch=1, grid=(2, nt),
            in_specs=[whole(w) for w in ws] + [mine_spec(w) for w in ws] + [theirs_spec(w) for w in ws]
            + [whole(w) for w in ws] * 2,
            out_specs=[whole(w) for w in ws for _ in range(4)]),
        out_shape=[jax.ShapeDtypeStruct(w.shape, F32) for w in ws for _ in range(4)],
        compiler_params=_params(("arbitrary", "arbitrary")),
    )(core, *ws, *mines, *theirs, *ms, *vs)
    return [outs[4 * a:4 * a + 4] for a in range(n)]


WEIGHTS = ("ffn1_norm", "ffn1_gate", "ffn1_up", "ffn1_down", "mix_norm", "w_in", "b_forget", "conv_w",
           "w_o_attn", "w_o_conv", "w_out", "ffn2_norm", "ffn2_gate", "ffn2_up", "ffn2_down", "final_norm")
VEC_ROWS = 8


def _pack_small(t, conv_rows):
    conv = t["conv_w"]
    parts = [t[n].reshape(VEC_ROWS, LANES) for n in NORMS]
    parts.append(jnp.pad(conv, ((0, conv_rows - conv.shape[0]), (0, 0))))
    parts.append(jnp.pad(t["b_forget"].reshape(1, N_HEADS), ((0, 7), (0, LANES - N_HEADS))))
    return jnp.concatenate(parts, axis=0)


def _unpack_small(p, conv_rows):
    out = {n: p[VEC_ROWS * i:VEC_ROWS * (i + 1)].reshape(-1) for i, n in enumerate(NORMS)}
    base = VEC_ROWS * len(NORMS)
    out["conv_w"] = p[base:base + 3]
    out["b_forget"] = p[base + conv_rows, :N_HEADS]
    return out


def _travel(name, a):
    return a.T if name in TRANSPOSED else a


GATHER_FIRST = ("ffn1_gate", "ffn1_up")
GATHER_RIDES = {"ffn1_up": ("ffn1_down",), "ffn1_down": ("w_in",), "mix_proj_fwd": ("w_o_attn", "w_o_conv", "w_out"),
                "attn_fwd": ("ffn2_gate", "ffn2_up", "ffn2_down")}
SIBLING_RIDES = {"ffn2": "mix_out_bwd", "out": None, "w_in": "mix_proj_bwd_dx", "ffn1": None}
CHIP_RIDES = {"ffn2": "attn_bwd", "out": "attn_bwd", "w_in": "ffn1_bwd_dw", "ffn1": None}
SMALL_RIDE = "ffn1_bwd_dw"


class _MeshPlan:
    def __init__(self, wts, core):
        self.small, self.core = wts, core
        self.shards = {n: wts[n].astype(BF16) for n in BIG}
        self.chip_part, self.from_chips, self.rides = {}, {}, {}
        self.stacks = {}
        conv_shard = jnp.pad(wts["conv_w"], ((0, 8 - wts["conv_w"].shape[0]), (0, 0)))
        for kernel_name, names in GATHER_RIDES.items():
            mine = [self.shards[n] for n in names]
            conv = conv_shard if kernel_name == "ffn1_up" else None
            names = names + (("conv_w",) if conv is not None else ())
            mine = mine + ([conv] if conv is not None else [])
            self._ride(kernel_name, _gather_comm(mine[:len(mine) - (conv is not None)], conv),
                       lambda got, names=names, mine=mine: self.stacks.update(zip(names, _fill_own(got, mine))))

    def weights(self, group):
        return _LAYOUTS[group](self.stacks, self.small)

    def ffn1_up(self, x, tm):
        px, py = lax.axis_index("x"), lax.axis_index("y")
        order = jnp.stack([2 * px + py, 2 * (1 - px) + py, 2 * px + (1 - py), 2 * (1 - px) + (1 - py)]).astype(jnp.int32)
        own = [self.shards[n] for n in GATHER_FIRST]
        (hg, hu, n, sg, su), brought = _ffn_up_gather("ffn1_up", x, self.small["ffn1_norm"].reshape(1, -1), *own, order,
                                                     tm, self.rider("ffn1_up"))
        self.stacks.update(zip(GATHER_FIRST, _fill_own([sg, su], own)))
        self.arrived("ffn1_up", brought)
        return hg, hu, n

    def _ride(self, kernel_name, comm, then):
        self.rides.setdefault(kernel_name, []).append((comm, then))

    def rider(self, kernel_name):
        comms = [comm for comm, _ in self.rides.get(kernel_name, [])]
        return _merge_comms(comms) if comms else None

    def arrived(self, kernel_name, results):
        for comm, then in self.rides.pop(kernel_name, []):
            then(results[:len(comm.out_shape)])
            results = results[len(comm.out_shape):]

    def reduce(self, group, grads):
        names = tuple(grads)
        mine = [grads[n] for n in names]

        def with_sibling(from_sibling):
            parts = _add_halves("add_halves_" + group, mine, list(from_sibling), self.core)
            self.chip_part.update(zip(names, parts))
            if CHIP_RIDES[group] is None:
                self.last = (names, _exchange_start("exchange_start_" + group, parts))
            else:
                self._ride(CHIP_RIDES[group], _chip_exchange_comm(parts),
                           lambda got: self.from_chips.update(zip(names, got)))

        if SIBLING_RIDES[group] is None:
            with_sibling(_run_comm("sibling_exchange_" + group, _sibling_exchange_comm(mine)))
        else:
            self._ride(SIBLING_RIDES[group], _sibling_exchange_comm(mine), with_sibling)

    def reduce_small(self, gs, loss):
        conv_all = _shard_cols(gs["conv_w"]).reshape(N_CHIPS * 8, LANES)
        part = _pack_small({**{n: gs[n] for n in NORMS}, "conv_w": conv_all, "b_forget": gs["b_forget"][0, :N_HEADS]},
                           N_CHIPS * 8)
        part = jnp.concatenate([part, jnp.broadcast_to(loss, (8, LANES))], axis=0)
        me = 4 * lax.axis_index("x") + 2 * lax.axis_index("y") + lax.axis_index("c")

        def landed(got):
            self.small_parts = lax.dynamic_update_index_in_dim(got[0], part, me, 0)

        self._ride(SMALL_RIDE, _small_gather_comm(part), landed)


def kernel(x, ffn1_norm, ffn1_gate, ffn1_up, ffn1_down, mix_norm, w_in, b_forget, conv_w, w_o_attn, w_o_conv, w_out, ffn2_norm, ffn2_gate, ffn2_up, ffn2_down, final_norm, loss_target, m_ffn1_norm, m_ffn1_gate, m_ffn1_up, m_ffn1_down, m_mix_norm, m_w_in, m_b_forget, m_conv_w, m_w_o_attn, m_w_o_conv, m_w_out, m_ffn2_norm, m_ffn2_gate, m_ffn2_up, m_ffn2_down, m_final_norm, v_ffn1_norm, v_ffn1_gate, v_ffn1_up, v_ffn1_down, v_mix_norm, v_w_in, v_b_forget, v_conv_w, v_w_o_attn, v_w_o_conv, v_w_out, v_ffn2_norm, v_ffn2_gate, v_ffn2_up, v_ffn2_down, v_final_norm):
    given = dict(locals())
    wts = {n: _travel(n, given[n]) for n in WEIGHTS}
    mom = {n: _travel(n, given["m_" + n]) for n in WEIGHTS}
    var = {n: _travel(n, given["v_" + n]) for n in WEIGHTS}
    B, S, D = x.shape
    chip = 2 * lax.axis_index("x") + lax.axis_index("y")
    chip1 = chip.astype(jnp.int32).reshape(1)
    core = lax.axis_index("c").astype(jnp.int32).reshape(1)

    plan = _MeshPlan(wts, core)
    loss, grad_x, gs = _local_step(x.reshape(B * S, D), loss_target.reshape(B * S, D), plan, B, S)

    last_names, (send_sems, recv_sems, parts_thru, lands, token) = plan.last
    delta, new_m, new_v, grads = {}, {}, {}, {}

    def finish(tag, names):
        by_cols = {}
        for n in names:
            by_cols.setdefault(wts[n].shape[1], []).append(n)
        mine = {}
        for cols, ns in by_cols.items():
            mine.update(zip(ns, _sum_chips("sum_chips_%s_%d" % (tag, cols), [plan.chip_part[n] for n in ns],
                                           [plan.from_chips[n] for n in ns], chip1, token)))
        theirs = dict(zip(names, _share_halves("share_halves_" + tag, [mine[n] for n in names])))
        raw = []
        for cols, ns in by_cols.items():
            outs = _adamw_halves("adamw_%s_%d" % (tag, cols), [wts[n] for n in ns], [mine[n] for n in ns],
                                 [theirs[n] for n in ns], [mom[n] for n in ns], [var[n] for n in ns], core)
            for n, per in zip(ns, outs):
                raw.append(per[-1])
                grads[n], delta[n], new_m[n], new_v[n] = [_travel(n, o) for o in per]
        return raw

    small_sum = _sum_devices(plan.small_parts)
    base = VEC_ROWS * len(NORMS)
    loss_row = small_sum.shape[0] - 8
    small_grads = _unpack_small(small_sum, N_CHIPS * 8)
    small_grads["conv_w"] = lax.dynamic_slice_in_dim(small_sum[base:base + N_CHIPS * 8], chip * 8, 8, axis=0)[:3]
    packs = [_pack_small(t, 8) for t in (wts, small_grads, mom, var)]
    small_out = _adamw("adamw_small", *packs)

    done = finish("early", [n for n in BIG if n not in last_names])
    parts_back, got = _exchange_wait("exchange_wait", send_sems, recv_sems, parts_thru, lands, done + list(small_out))
    plan.chip_part.update(zip(last_names, parts_back))
    plan.from_chips.update(zip(last_names, got))
    finish("last", last_names)
    grads.update(small_grads)
    for out, p in zip((delta, new_m, new_v), small_out):
        out.update(_unpack_small(p, 8))

    return (small_sum[loss_row, 0], grad_x.reshape(B, S, D), *[grads[n] for n in WEIGHTS], *[delta[n] for n in WEIGHTS],
            *[new_m[n] for n in WEIGHTS], *[new_v[n] for n in WEIGHTS])
```

```python
import functools
import math

import jax
import jax.numpy as jnp
from jax import lax
from jax.experimental import pallas as pl
from jax.experimental.pallas import tpu as pltpu

F32 = jnp.float32
BF16 = jnp.bfloat16
MESH = pl.DeviceIdType.MESH

N_CHIPS = 4
N_DEV = 8
N_HEADS = 8
HEAD_DIM = 64
HEAD_PAIRS = N_HEADS // 2
ATTN_W = N_HEADS * HEAD_DIM
CONV_W = 512
RMS_EPS = 1e-6
FFN_RES = 0.5
LANES = 128
VMEM_LIMIT = 56 * 1024 * 1024
ROW_BLOCK = 256

ADAM_LR = 0.001
ADAM_B1 = 0.9
ADAM_B2 = 0.999
ADAM_EPS = 1e-08
ADAM_WD = 0.01
ADAM_STEP = 10

PROJ_W = 3 * ATTN_W + 3 * CONV_W + 2 * 1024
COL_CB, COL_CC, COL_CX = 3 * ATTN_W, 3 * ATTN_W + CONV_W, 3 * ATTN_W + 2 * CONV_W
COL_GATES = 3 * ATTN_W + 3 * CONV_W
N_FORGET_COL = 3 * ATTN_W


def _params(sem=None, vmem=VMEM_LIMIT):
    return pltpu.CompilerParams(dimension_semantics=sem, vmem_limit_bytes=vmem)


def _dot(a, b):
    return lax.dot_general(a, b, (((1,), (0,)), ((), ())), preferred_element_type=F32)


def _dot_nt(a, b):
    return lax.dot_general(a, b, (((1,), (1,)), ((), ())), preferred_element_type=F32)


def _dot_tn(a, b):
    return lax.dot_general(a, b, (((0,), (0,)), ((), ())), preferred_element_type=F32)


def _sigmoid(x):
    return 1.0 / (1.0 + jnp.exp(-x))


def _rms(xv):
    inv = lax.rsqrt(jnp.mean(xv * xv, axis=-1, keepdims=True) + RMS_EPS)
    return xv * inv, inv


class _Comm:
    def __init__(self, inputs, out_shape, scratch, start, finish):
        self.inputs, self.out_shape, self.scratch = list(inputs), list(out_shape), list(scratch)
        self.start, self.finish = start, finish


def _pallas(body, name, grid, in_specs, out_specs, out_shape, scratch, args, comm=None):
    sem = ("arbitrary",) * len(grid)
    if comm is None:
        outs = pl.pallas_call(body, name=name, grid=grid, in_specs=in_specs, out_specs=out_specs,
                              out_shape=out_shape, scratch_shapes=scratch, compiler_params=_params(sem))(*args)
        return list(outs), []
    n_in, n_out, n_scr = len(in_specs), len(out_specs), len(scratch)
    ci, co = len(comm.inputs), len(comm.out_shape)

    def riding(*refs):
        ins, refs = refs[:n_in], refs[n_in:]
        cins, refs = refs[:ci], refs[ci:]
        outs, refs = refs[:n_out], refs[n_out:]
        couts, refs = refs[:co], refs[co:]
        scr, sems = refs[:n_scr], refs[n_scr:]
        ids = [pl.program_id(d) for d in range(len(grid))]
        first = functools.reduce(lambda a, b: a & b, [i == 0 for i in ids])
        last = functools.reduce(lambda a, b: a & b, [i == g - 1 for i, g in zip(ids, grid)])

        @pl.when(first)
        def _():
            comm.start(cins, couts, sems)

        body(*ins, *outs, *scr)

        @pl.when(last)
        def _():
            comm.finish(cins, couts, sems)

    any_spec = pl.BlockSpec(memory_space=pl.ANY)
    outs = pl.pallas_call(
        riding, name=name, grid=grid,
        in_specs=list(in_specs) + [any_spec] * ci, out_specs=list(out_specs) + [any_spec] * co,
        out_shape=list(out_shape) + comm.out_shape, scratch_shapes=list(scratch) + comm.scratch,
        compiler_params=_params(sem))(*args, *comm.inputs)
    return list(outs[:n_out]), list(outs[n_out:])


def _rms_bwd(dn, xhat, inv, g):
    dxhat = dn * g
    dx = inv * (dxhat - xhat * jnp.mean(dxhat * xhat, axis=-1, keepdims=True))
    return dx, jnp.sum(dn * xhat, axis=0, keepdims=True)


def _ffn_fwd_loss(name, x, g, wgt, wut, wd, target, gf, tm):
    T, D = x.shape
    K, Fs, _ = wgt.shape

    def body(x_ref, g_ref, wg_ref, wu_ref, wd_ref, t_ref, gf_ref,
             dx_ref, hg_ref, hu_ref, n_ref, loss_ref, dgf_ref, acc_scr):
        i, k = pl.program_id(0), pl.program_id(1)

        @pl.when(k == 0)
        def _():
            xhat, _ = _rms(x_ref[...])
            n_ref[...] = (xhat * g_ref[...]).astype(BF16)
            acc_scr[...] = jnp.zeros_like(acc_scr)

        @pl.when((k == 0) & (i == 0))
        def _():
            loss_ref[...] = jnp.zeros_like(loss_ref)
            dgf_ref[...] = jnp.zeros_like(dgf_ref)

        n = n_ref[...]
        hg = _dot_nt(n, wg_ref[...])
        hu = _dot_nt(n, wu_ref[...])
        hg_ref[...] = hg.astype(BF16)
        hu_ref[...] = hu.astype(BF16)
        act = (hg * _sigmoid(hg) * hu).astype(BF16)
        acc_scr[...] += _dot(act, wd_ref[...])

        @pl.when(k == K - 1)
        def _():
            gfv = gf_ref[...]
            for r0 in range(0, tm, ROW_BLOCK):
                rows = slice(r0, r0 + ROW_BLOCK)
                xhat, inv = _rms(x_ref[rows, :] + FFN_RES * acc_scr[rows, :])
                err = xhat * gfv - t_ref[rows, :]
                loss_ref[...] += 0.5 * jnp.sum(jnp.sum(err * err, axis=1, keepdims=True), axis=0, keepdims=True) / D
                dx, dg = _rms_bwd(err * (1.0 / D), xhat, inv, gfv)
                dx_ref[rows, :] = dx
                dgf_ref[...] += dg

    w_spec = pl.BlockSpec((None, Fs, D), lambda i, k: (k, 0, 0))
    act_spec = pl.BlockSpec((None, tm, Fs), lambda i, k: (k, i, 0))
    row = pl.BlockSpec((tm, D), lambda i, k: (i, 0))
    vec = pl.BlockSpec((1, D), lambda i, k: (0, 0))
    return _pallas(
        body, name, (T // tm, K),
        [row, vec, w_spec, w_spec, w_spec, row, vec],
        [row, act_spec, act_spec, row, pl.BlockSpec((1, LANES), lambda i, k: (0, 0)), vec],
        [jax.ShapeDtypeStruct((T, D), F32), jax.ShapeDtypeStruct((K, T, Fs), BF16),
         jax.ShapeDtypeStruct((K, T, Fs), BF16), jax.ShapeDtypeStruct((T, D), BF16),
         jax.ShapeDtypeStruct((1, LANES), F32), jax.ShapeDtypeStruct((1, D), F32)],
        [pltpu.VMEM((tm, D), F32)],
        (x, g, wgt, wut, wd, target, gf))[0]


def _ffn_up_gather(name, x, g, wg_own, wu_own, order, tm, comm=None):
    T, D = x.shape
    Fs = wg_own.shape[0]
    nt = T // tm
    ci, co = (len(comm.inputs), len(comm.out_shape)) if comm is not None else (0, 0)

    def body(order_ref, x_ref, g_ref, wgo_ref, wuo_ref, *rest):
        cins, rest = rest[:ci], rest[ci:]
        (hg_ref, hu_ref, n_ref, sg_ref, su_ref), rest = rest[:5], rest[5:]
        couts, rest = rest[:co], rest[co:]
        (n_all, wbuf, send_sems, recv_sems, pass_send, pass_recv, load_sems), csems = rest[:7], rest[7:]
        k, i = pl.program_id(0), pl.program_id(1)
        x_pos, y_pos, c, others = _place()
        me = 2 * x_pos + y_pos
        owns, stacks = (wgo_ref, wuo_ref), (sg_ref, su_ref)
        mine, theirs = _col_halves(D, c)

        def chip_copy(a, j, chip):
            return pltpu.make_async_remote_copy(
                src_ref=owns[a].at[:, mine], dst_ref=stacks[a].at[chip, :, mine],
                send_sem=send_sems.at[3 * a + j], recv_sem=recv_sems.at[3 * a + j],
                device_id=(*others[j], c), device_id_type=MESH)

        def pass_copy(a, j, chip, half):
            return pltpu.make_async_remote_copy(
                src_ref=stacks[a].at[chip, :, half], dst_ref=stacks[a].at[chip, :, half],
                send_sem=pass_send.at[3 * a + j], recv_sem=pass_recv.at[3 * a + j],
                device_id=(x_pos, y_pos, 1 - c), device_id_type=MESH)

        @pl.when((k == 0) & (i == 0))
        def _():
            for a in range(2):
                for j in range(3):
                    chip_copy(a, j, me).start()
            if comm is not None:
                comm.start(cins, couts, csems)

        def bring(j):
            ox, oy = others[j]
            chip = 2 * ox + oy
            for a in range(2):
                chip_copy(a, j, chip).wait_recv()
            for a in range(2):
                pass_copy(a, j, chip, mine).start()
            for a in range(2):
                pass_copy(a, j, chip, theirs).wait_recv()
            loads = [pltpu.make_async_copy(stacks[a].at[chip], wbuf.at[j % 2, a], load_sems.at[2 * (j % 2) + a])
                     for a in range(2)]
            for cp in loads:
                cp.start()
            for cp in loads:
                cp.wait()

        @pl.when((k == 1) & (i == 0))
        def _():
            bring(0)
            bring(1)

        @pl.when((k == 2) & (i == nt - 1))
        def _():
            bring(2)

        rows = pl.ds(pl.multiple_of(i * tm, tm), tm)

        @pl.when(k == 0)
        def _():
            xhat, _ = _rms(x_ref[...])
            n = (xhat * g_ref[...]).astype(BF16)
            n_ref[...] = n
            n_all[rows, :] = n
            hg_ref[...] = _dot_nt(n, wgo_ref[...]).astype(BF16)
            hu_ref[...] = _dot_nt(n, wuo_ref[...]).astype(BF16)

        @pl.when(k > 0)
        def _():
            n = n_all[rows, :]
            slot = (k - 1) % 2
            hg_ref[...] = _dot_nt(n, wbuf[slot, 0]).astype(BF16)
            hu_ref[...] = _dot_nt(n, wbuf[slot, 1]).astype(BF16)

        @pl.when((k == N_CHIPS - 1) & (i == nt - 1))
        def _():
            for a in range(2):
                for j, (ox, oy) in enumerate(others):
                    chip_copy(a, j, me).wait_send()
                    pass_copy(a, j, 2 * ox + oy, mine).wait_send()
            if comm is not None:
                comm.finish(cins, couts, csems)

    any_spec = pl.BlockSpec(memory_space=pl.ANY)
    first_pass = lambda k, i, order_ref: (jnp.where(k == 0, i, nt - 1), 0)
    whole = pl.BlockSpec((Fs, D), lambda k, i, order_ref: (0, 0))
    act_spec = pl.BlockSpec((None, tm, Fs), lambda k, i, order_ref: (order_ref[k], i, 0))
    stack = jax.ShapeDtypeStruct((N_CHIPS, Fs, D), BF16)
    outs = pl.pallas_call(
        body, name=name,
        grid_spec=pltpu.PrefetchScalarGridSpec(
            num_scalar_prefetch=1, grid=(N_CHIPS, nt),
            in_specs=[pl.BlockSpec((tm, D), first_pass), pl.BlockSpec((1, D), lambda k, i, order_ref: (0, 0)),
                      whole, whole] + [any_spec] * ci,
            out_specs=[act_spec, act_spec, pl.BlockSpec((tm, D), first_pass), any_spec, any_spec] + [any_spec] * co,
            scratch_shapes=[pltpu.VMEM((T, D), BF16), pltpu.VMEM((2, 2, Fs, D), BF16)]
            + [pltpu.SemaphoreType.DMA((6,))] * 4 + [pltpu.SemaphoreType.DMA((4,))]
            + (comm.scratch if comm is not None else [])),
        out_shape=[jax.ShapeDtypeStruct((N_CHIPS, T, Fs), BF16), jax.ShapeDtypeStruct((N_CHIPS, T, Fs), BF16),
                   jax.ShapeDtypeStruct((T, D), BF16), stack, stack] + (comm.out_shape if comm is not None else []),
        compiler_params=_params(("arbitrary", "arbitrary")),
    )(order, x, g, wg_own, wu_own, *(comm.inputs if comm is not None else []))
    return list(outs[:5]), list(outs[5:])


def _ffn_down(name, x, hg, hu, wd, tm, comm=None):
    T, D = x.shape
    K, Fs, _ = wd.shape

    def body(x_ref, hg_ref, hu_ref, wd_ref, out_ref, acc_scr):
        k = pl.program_id(1)

        @pl.when(k == 0)
        def _():
            acc_scr[...] = jnp.zeros_like(acc_scr)

        hgv = hg_ref[...].astype(F32)
        act = (hgv * _sigmoid(hgv) * hu_ref[...].astype(F32)).astype(BF16)
        acc_scr[...] += _dot(act, wd_ref[...])

        @pl.when(k == K - 1)
        def _():
            out_ref[...] = x_ref[...] + FFN_RES * acc_scr[...]

    act_spec = pl.BlockSpec((None, tm, Fs), lambda i, k: (k, i, 0))
    row = pl.BlockSpec((tm, D), lambda i, k: (i, 0))
    return _pallas(
        body, name, (T // tm, K),
        [row, act_spec, act_spec, pl.BlockSpec((None, Fs, D), lambda i, k: (k, 0, 0))],
        [row], [jax.ShapeDtypeStruct((T, D), F32)], [pltpu.VMEM((tm, D), F32)],
        (x, hg, hu, wd), comm)


def _ffn_bwd_dx(name, dout, x, g, hg, hu, wgt, wut, wd, tm, comm=None):
    T, D = x.shape
    K, Fs, _ = wgt.shape

    def body(dout_ref, x_ref, g_ref, hg_ref, hu_ref, wg_ref, wu_ref, wd_ref,
             dx_ref, dhg_ref, dhu_ref, dg_ref, df_ref, dn_scr):
        i, k = pl.program_id(0), pl.program_id(1)

        @pl.when(k == 0)
        def _():
            df_ref[...] = (FFN_RES * dout_ref[...]).astype(BF16)
            dn_scr[...] = jnp.zeros_like(dn_scr)

        @pl.when((k == 0) & (i == 0))
        def _():
            dg_ref[...] = jnp.zeros_like(dg_ref)

        for r0 in range(0, tm, ROW_BLOCK):
            rows = slice(r0, r0 + ROW_BLOCK)
            dact = _dot_nt(df_ref[rows, :], wd_ref[...])
            hgv = hg_ref[rows, :].astype(F32)
            huv = hu_ref[rows, :].astype(F32)
            s = _sigmoid(hgv)
            dhu = (dact * (hgv * s)).astype(BF16)
            dhg = (dact * huv * (s * (1.0 + hgv * (1.0 - s)))).astype(BF16)
            dhg_ref[rows, :] = dhg
            dhu_ref[rows, :] = dhu
            dn_scr[rows, :] += _dot(dhg, wg_ref[...]) + _dot(dhu, wu_ref[...])

        @pl.when(k == K - 1)
        def _():
            xhat, inv = _rms(x_ref[...])
            dx, dg = _rms_bwd(dn_scr[...], xhat, inv, g_ref[...])
            dx_ref[...] = dout_ref[...] + dx
            dg_ref[...] += dg

    w_spec = pl.BlockSpec((None, Fs, D), lambda i, k: (k, 0, 0))
    act_spec = pl.BlockSpec((None, tm, Fs), lambda i, k: (k, i, 0))
    row = pl.BlockSpec((tm, D), lambda i, k: (i, 0))
    row_once = pl.BlockSpec((tm, D), lambda i, k: (i, 0), pipeline_mode=pl.Buffered(1))
    vec = pl.BlockSpec((1, D), lambda i, k: (0, 0))
    return _pallas(
        body, name, (T // tm, K),
        [row, row_once, vec, act_spec, act_spec, w_spec, w_spec, w_spec],
        [row_once, act_spec, act_spec, vec, row],
        [jax.ShapeDtypeStruct((T, D), F32), jax.ShapeDtypeStruct((K, T, Fs), BF16),
         jax.ShapeDtypeStruct((K, T, Fs), BF16), jax.ShapeDtypeStruct((1, D), F32),
         jax.ShapeDtypeStruct((T, D), BF16)],
        [pltpu.VMEM((tm, D), F32)],
        (dout, x, g, hg, hu, wgt, wut, wd), comm)


def _ffn_bwd_dw(name, n, df, hg, hu, dhg, dhu, tk, comm=None):
    T, D = n.shape
    K, _, Fs = hg.shape
    nt = T // tk

    def body(n_ref, df_ref, hg_ref, hu_ref, dhg_ref, dhu_ref, dwg_ref, dwu_ref, dwd_ref, accg, accu, accd):
        t = pl.program_id(1)

        @pl.when(t == 0)
        def _():
            accg[...] = jnp.zeros_like(accg)
            accu[...] = jnp.zeros_like(accu)
            accd[...] = jnp.zeros_like(accd)

        nv = n_ref[...]
        hgv = hg_ref[...].astype(F32)
        act = (hgv * _sigmoid(hgv) * hu_ref[...].astype(F32)).astype(BF16)
        accg[...] += _dot_tn(dhg_ref[...], nv)
        accu[...] += _dot_tn(dhu_ref[...], nv)
        accd[...] += _dot_tn(act, df_ref[...])

        @pl.when(t == nt - 1)
        def _():
            dwg_ref[...] = accg[...].astype(BF16)
            dwu_ref[...] = accu[...].astype(BF16)
            dwd_ref[...] = accd[...].astype(BF16)

    act_spec = pl.BlockSpec((None, tk, Fs), lambda k, t: (k, t, 0))
    w_spec = pl.BlockSpec((None, Fs, D), lambda k, t: (k, 0, 0))
    row = pl.BlockSpec((tk, D), lambda k, t: (t, 0))
    return _pallas(
        body, name, (K, nt),
        [row, row, act_spec, act_spec, act_spec, act_spec],
        [w_spec, w_spec, w_spec],
        [jax.ShapeDtypeStruct((K, Fs, D), BF16)] * 3,
        [pltpu.VMEM((Fs, D), F32)] * 3,
        (n, df, hg, hu, dhg, dhu), comm)


def _mix_proj_fwd(x, g, wproj_t, wf_t, tm, tn, comm=None):
    T, D = x.shape
    N = wproj_t.shape[0]

    def body(x_ref, g_ref, w_ref, wf_ref, h_ref, proj_ref, flog_ref, h_scr):
        @pl.when(pl.program_id(1) == 0)
        def _():
            xhat, _ = _rms(x_ref[...])
            h = (xhat * g_ref[...]).astype(BF16)
            h_scr[...] = h
            h_ref[...] = h
            flog_ref[...] = _dot_nt(h, wf_ref[...])

        proj_ref[...] = _dot_nt(h_scr[...], w_ref[...]).astype(BF16)

    return _pallas(
        body, "mix_proj_fwd", (T // tm, N // tn),
        [pl.BlockSpec((tm, D), lambda i, n: (i, 0)), pl.BlockSpec((1, D), lambda i, n: (0, 0)),
         pl.BlockSpec((tn, D), lambda i, n: (n, 0)), pl.BlockSpec((LANES, D), lambda i, n: (0, 0))],
        [pl.BlockSpec((tm, D), lambda i, n: (i, 0)), pl.BlockSpec((tm, tn), lambda i, n: (i, n)),
         pl.BlockSpec((tm, LANES), lambda i, n: (i, 0))],
        [jax.ShapeDtypeStruct((T, D), BF16), jax.ShapeDtypeStruct((T, N), BF16),
         jax.ShapeDtypeStruct((T, LANES), F32)],
        [pltpu.VMEM((tm, D), BF16)],
        (x, g, wproj_t, wf_t), comm)


def _log_sigmoid(z):
    return -(jnp.maximum(-z, 0.0) + jnp.log(1.0 + jnp.exp(-jnp.abs(z))))


def _tri(n, lower):
    r = lax.broadcasted_iota(jnp.int32, (n, n), 0)
    c = lax.broadcasted_iota(jnp.int32, (n, n), 1)
    return jnp.where((r >= c) if lower else (r <= c), 1.0, 0.0).astype(F32)


def _dot_f32(a, b):
    return lax.dot_general(a, b, (((1,), (0,)), ((), ())), preferred_element_type=F32,
                           precision=lax.Precision.HIGHEST)


def _fgate_fwd(flog, bias, B, S, ch):
    def body(flog_ref, b_ref, cum_ref):
        tri = _tri(ch, True)
        carry = jnp.zeros((1, LANES), F32)
        for c0 in range(0, S, ch):
            lf = _log_sigmoid(flog_ref[c0:c0 + ch, :] + b_ref[...])
            cs = _dot_f32(tri, lf) + carry
            cum_ref[c0:c0 + ch, :] = cs
            carry = cs[ch - 1:ch, :]

    return pl.pallas_call(
        body, name="fgate_fwd", grid=(B,),
        in_specs=[pl.BlockSpec((S, LANES), lambda b: (b, 0)),
                  pl.BlockSpec((1, LANES), lambda b: (0, 0))],
        out_specs=pl.BlockSpec((S, LANES), lambda b: (b, 0)),
        out_shape=jax.ShapeDtypeStruct((B * S, LANES), F32),
        compiler_params=_params(("arbitrary",)),
    )(flog, bias)


def _fgate_bwd(dcum, flog, bias, B, S, ch):
    def body(dcum_ref, flog_ref, b_ref, dflog_ref, db_ref):
        @pl.when(pl.program_id(0) == 0)
        def _():
            db_ref[...] = jnp.zeros_like(db_ref)

        tri = _tri(ch, False)
        carry = jnp.zeros((1, LANES), F32)
        db = jnp.zeros((1, LANES), F32)
        for c0 in range(S - ch, -1, -ch):
            dlf = _dot_f32(tri, dcum_ref[c0:c0 + ch, :]) + carry
            carry = dlf[0:1, :]
            z = flog_ref[c0:c0 + ch, :] + b_ref[...]
            dz = dlf * _sigmoid(-z)
            dflog_ref[c0:c0 + ch, :] = dz
            db = db + jnp.sum(dz, axis=0, keepdims=True)
        db_ref[...] += db

    return pl.pallas_call(
        body, name="fgate_bwd", grid=(B,),
        in_specs=[pl.BlockSpec((S, LANES), lambda b: (b, 0)),
                  pl.BlockSpec((S, LANES), lambda b: (b, 0)),
                  pl.BlockSpec((1, LANES), lambda b: (0, 0))],
        out_specs=[pl.BlockSpec((S, LANES), lambda b: (b, 0)),
                   pl.BlockSpec((1, LANES), lambda b: (0, 0))],
        out_shape=[jax.ShapeDtypeStruct((B * S, LANES), F32),
                   jax.ShapeDtypeStruct((1, LANES), F32)],
        compiler_params=_params(("arbitrary",)),
    )(dcum, flog, bias)


def _pick_lane(tile, h):
    lane = lax.broadcasted_iota(jnp.int32, tile.shape, 1)
    return jnp.sum(jnp.where(lane == h, tile, 0.0), axis=1, keepdims=True)


def _put_lane(col, h, width=LANES):
    lane = lax.broadcasted_iota(jnp.int32, (col.shape[0], width), 1)
    return jnp.where(lane == h, col, 0.0)


def _pick_row(tile, h):
    row = lax.broadcasted_iota(jnp.int32, tile.shape, 0)
    return jnp.sum(jnp.where(row == h, tile, 0.0), axis=0, keepdims=True)


def _put_row(vec, h):
    row = lax.broadcasted_iota(jnp.int32, (8, vec.shape[1]), 0)
    return jnp.where(row == h, vec, 0.0)


def _causal(tq):
    r = lax.broadcasted_iota(jnp.int32, (tq, tq), 0)
    c = lax.broadcasted_iota(jnp.int32, (tq, tq), 1)
    return r >= c


def _head_halves(t):
    lo = lax.broadcasted_iota(jnp.int32, t.shape, 1) < HEAD_DIM
    zero = jnp.zeros_like(t)
    return jnp.where(lo, t, zero), jnp.where(lo, zero, t)


NEG = -1e30
ATTN_SCALE = 1.0 / math.sqrt(HEAD_DIM)


def _scaled(q):
    return (q.astype(F32) * ATTN_SCALE).astype(q.dtype)


def _attn_fwd(proj, cum, cum_t, B, S, tq, comm=None):
    nq = S // tq

    def body(q_ref, k_ref, v_ref, cum_ref, cumt_ref, o_ref, lse_ref):
        qi, hp = pl.program_id(1), pl.program_id(2)
        qm = _head_halves(_scaled(q_ref[...]))
        cumv = cum_ref[...]
        cq = [_pick_lane(cumv, 2 * hp + e) for e in range(2)]

        def tile(j, carry, masked):
            off = pl.multiple_of(j * tq, tq)
            kj = k_ref[pl.ds(off, tq), :]
            vj = v_ref[pl.ds(off, tq), :]
            ct = cumt_ref[j]
            new = []
            for e in range(2):
                m, l, acc = carry[e]
                s = _dot_nt(qm[e], kj) - _pick_row(ct, 2 * hp + e)
                if masked:
                    s = jnp.where(_causal(tq), s, NEG)
                m_new = jnp.maximum(m, jnp.max(s, axis=1, keepdims=True))
                p = jnp.exp(s - m_new)
                alpha = jnp.exp(m - m_new)
                l = alpha * l + jnp.sum(p, axis=1, keepdims=True)
                acc = alpha * acc + _dot(p.astype(BF16), vj)
                new.append((m_new, l, acc))
            return tuple(new)

        one = (jnp.full((tq, 1), NEG, F32), jnp.zeros((tq, 1), F32), jnp.zeros((tq, LANES), F32))
        carry = lax.fori_loop(0, qi, lambda j, c: tile(j, c, False), (one, one))
        (ma, la, acca), (mb, lb, accb) = tile(qi, carry, True)
        lo = lax.broadcasted_iota(jnp.int32, (tq, LANES), 1) < HEAD_DIM
        o_ref[...] = jnp.where(lo, acca / la, accb / lb).astype(BF16)

        @pl.when(hp == 0)
        def _():
            lse_ref[...] = jnp.zeros_like(lse_ref)

        lse_ref[...] += (_put_lane(ma + jnp.log(la) + cq[0], 2 * hp) + _put_lane(mb + jnp.log(lb) + cq[1], 2 * hp + 1))

    kv = lambda first: pl.BlockSpec((S, LANES), lambda b, i, hp: (b, first + hp))
    return _pallas(
        body, "attn_fwd", (B, nq, HEAD_PAIRS),
        [pl.BlockSpec((tq, LANES), lambda b, i, hp: (b * nq + i, hp)),
         kv(ATTN_W // LANES), kv(2 * ATTN_W // LANES),
         pl.BlockSpec((tq, LANES), lambda b, i, hp: (b * nq + i, 0)),
         pl.BlockSpec((None, nq, 8, tq), lambda b, i, hp: (b, 0, 0, 0))],
        [pl.BlockSpec((tq, LANES), lambda b, i, hp: (b * nq + i, hp)),
         pl.BlockSpec((tq, LANES), lambda b, i, hp: (b * nq + i, 0))],
        [jax.ShapeDtypeStruct((B * S, ATTN_W), BF16), jax.ShapeDtypeStruct((B * S, LANES), F32)],
        [], (proj, proj, proj, cum, cum_t), comm)


def _attn_bwd(proj, o, do, lse, cum, cum_t, B, S, tq, comm=None):
    nq = S // tq

    def body(q_ref, k_ref, v_ref, o_ref, do_ref, lse_ref, cum_ref, cumt_ref,
             dq_ref, dk_ref, dv_ref, dcq_ref, dck_ref, dq_scr):
        hp, kj = pl.program_id(1), pl.program_id(2)

        @pl.when(kj == 0)
        def _():
            dq_scr[...] = jnp.zeros_like(dq_scr)

        @pl.when((kj == 0) & (hp == 0))
        def _():
            dcq_ref[...] = jnp.zeros_like(dcq_ref)
            dck_ref[...] = jnp.zeros_like(dck_ref)

        kv = k_ref[...]
        vv = v_ref[...]
        km = _head_halves(kv)
        ct = cumt_ref[...]
        ck = [_pick_row(ct, 2 * hp + e) for e in range(2)]

        def tile(i, carry, masked):
            dk, dv, dcol = carry
            off = pl.multiple_of(i * tq, tq)
            qi = q_ref[pl.ds(off, tq), :]
            ov = o_ref[pl.ds(off, tq), :].astype(F32)
            qm = _head_halves(_scaled(qi))
            dom = _head_halves(do_ref[pl.ds(off, tq), :])
            cumv = cum_ref[pl.ds(off, tq), :]
            lsev = lse_ref[pl.ds(off, tq), :]
            dcq = jnp.zeros((tq, LANES), F32)
            dq = jnp.zeros((tq, LANES), F32)
            dcol_new = []
            for e in range(2):
                delta = jnp.sum(dom[e].astype(F32) * ov, axis=1, keepdims=True)
                row_term = _pick_lane(cumv, 2 * hp + e) - _pick_lane(lsev, 2 * hp + e)
                p = jnp.exp(_dot_nt(qm[e], kv) + row_term - ck[e])
                if masked:
                    p = jnp.where(_causal(tq), p, 0.0)
                dv = dv + _dot_tn(dom[e], p.astype(BF16))
                ds = p * (_dot_nt(dom[e], vv) - delta)
                dcol_new.append(dcol[e] + jnp.sum(ds, axis=0, keepdims=True))
                dcq = dcq + _put_lane(jnp.sum(ds, axis=1, keepdims=True), 2 * hp + e)
                dsb = ds.astype(BF16)
                dk = dk + _dot_tn(qm[e], dsb)
                dq = dq + _dot(dsb, km[e]) * ATTN_SCALE
            dq_scr[pl.ds(off, tq), :] += dq
            dcq_ref[pl.ds(off, tq), :] += dcq
            return dk, dv, tuple(dcol_new)

        zero_row = jnp.zeros((1, tq), F32)
        init = (jnp.zeros((LANES, tq), F32), jnp.zeros((LANES, tq), F32), (zero_row, zero_row))
        carry = tile(kj, init, True)
        dk, dv, dcol = lax.fori_loop(kj + 1, nq, lambda i, c: tile(i, c, False), carry)
        dk_ref[...] = dk.T.astype(BF16)
        dv_ref[...] = dv.T.astype(BF16)
        dck_ref[kj] += -(_put_row(dcol[0], 2 * hp) + _put_row(dcol[1], 2 * hp + 1))

        @pl.when(kj == nq - 1)
        def _():
            dq_ref[...] = dq_scr[...].astype(BF16)

    seq = lambda first: pl.BlockSpec((S, LANES), lambda b, hp, j: (b, first + hp))
    tile_in = lambda first: pl.BlockSpec((tq, LANES), lambda b, hp, j: (b * nq + j, first + hp))
    lanes0 = pl.BlockSpec((S, LANES), lambda b, hp, j: (b, 0))
    out = jax.ShapeDtypeStruct((B * S, ATTN_W), BF16)
    return _pallas(
        body, "attn_bwd", (B, HEAD_PAIRS, nq),
        [seq(0), tile_in(ATTN_W // LANES), tile_in(2 * ATTN_W // LANES), seq(0), seq(0), lanes0, lanes0,
         pl.BlockSpec((None, None, 8, tq), lambda b, hp, j: (b, j, 0, 0))],
        [seq(0), tile_in(0), tile_in(0), lanes0,
         pl.BlockSpec((None, nq, 8, tq), lambda b, hp, j: (b, 0, 0, 0))],
        [out, out, out, jax.ShapeDtypeStruct((B * S, LANES), F32), jax.ShapeDtypeStruct((B, nq, 8, tq), F32)],
        [pltpu.VMEM((S, LANES), F32)],
        (proj, proj, proj, o, do, lse, cum, cum_t), comm)


def _shift_down(u, n):
    row = lax.broadcasted_iota(jnp.int32, u.shape, 0)
    return jnp.where(row >= n, pltpu.roll(u, n, 0), 0.0)


def _shift_up(u, n):
    rows = u.shape[0]
    row = lax.broadcasted_iota(jnp.int32, u.shape, 0)
    return jnp.where(row < rows - n, pltpu.roll(u, rows - n, 0), 0.0)


def _conv_specs(S):
    cb = pl.BlockSpec((S, LANES), lambda g, b: (b, COL_CB // LANES + g))
    cc = pl.BlockSpec((S, LANES), lambda g, b: (b, COL_CC // LANES + g))
    cx = pl.BlockSpec((S, LANES), lambda g, b: (b, COL_CX // LANES + g))
    w = pl.BlockSpec((8, LANES), lambda g, b: (0, g))
    return cb, cc, cx, w


def _conv_fwd(proj, conv_w, B, S):
    def body(cb_ref, cc_ref, cx_ref, w_ref, y_ref):
        u = cc_ref[...].astype(F32) * cx_ref[...].astype(F32)
        w = w_ref[...]
        conv = w[0:1, :] * _shift_down(u, 2) + w[1:2, :] * _shift_down(u, 1) + w[2:3, :] * u
        y_ref[...] = (cb_ref[...].astype(F32) * conv).astype(BF16)

    cb, cc, cx, w = _conv_specs(S)
    return pl.pallas_call(
        body, name="conv_fwd", grid=(CONV_W // LANES, B),
        in_specs=[cb, cc, cx, w],
        out_specs=pl.BlockSpec((S, LANES), lambda g, b: (b, g)),
        out_shape=jax.ShapeDtypeStruct((B * S, CONV_W), BF16),
        compiler_params=_params(("arbitrary", "arbitrary")),
    )(proj, proj, proj, conv_w)


def _conv_bwd(dy, proj, conv_w, B, S):
    def body(dy_ref, cb_ref, cc_ref, cx_ref, w_ref, dcb_ref, dcc_ref, dcx_ref, dw_ref):
        @pl.when(pl.program_id(1) == 0)
        def _():
            dw_ref[...] = jnp.zeros_like(dw_ref)

        ccv = cc_ref[...].astype(F32)
        cxv = cx_ref[...].astype(F32)
        u = ccv * cxv
        u1 = _shift_down(u, 1)
        u2 = _shift_down(u, 2)
        w = w_ref[...]
        conv = w[0:1, :] * u2 + w[1:2, :] * u1 + w[2:3, :] * u
        dyv = dy_ref[...].astype(F32)
        dcb_ref[...] = (dyv * conv).astype(BF16)
        dconv = dyv * cb_ref[...].astype(F32)
        du = w[2:3, :] * dconv + w[1:2, :] * _shift_up(dconv, 1) + w[0:1, :] * _shift_up(dconv, 2)
        dcc_ref[...] = (du * cxv).astype(BF16)
        dcx_ref[...] = (du * ccv).astype(BF16)
        row = lax.broadcasted_iota(jnp.int32, (8, LANES), 0)
        dw = jnp.where(row == 0, jnp.sum(dconv * u2, axis=0, keepdims=True),
                       jnp.where(row == 1, jnp.sum(dconv * u1, axis=0, keepdims=True),
                                 jnp.where(row == 2, jnp.sum(dconv * u, axis=0, keepdims=True), 0.0)))
        dw_ref[...] += dw

    cb, cc, cx, w = _conv_specs(S)
    out = pl.BlockSpec((S, LANES), lambda g, b: (b, g))
    return pl.pallas_call(
        body, name="conv_bwd", grid=(CONV_W // LANES, B),
        in_specs=[out, cb, cc, cx, w],
        out_specs=[out, out, out, w],
        out_shape=[jax.ShapeDtypeStruct((B * S, CONV_W), BF16)] * 3 + [jax.ShapeDtypeStruct((8, CONV_W), F32)],
        compiler_params=_params(("arbitrary", "arbitrary")),
    )(dy, proj, proj, proj, conv_w)


def _gate_specs(tm, D):
    ga = pl.BlockSpec((tm, D), lambda i: (i, COL_GATES // D))
    gc = pl.BlockSpec((tm, D), lambda i: (i, COL_GATES // D + 1))
    return ga, gc


def _mix_out_fwd(x, o, yc, proj, woa, woc, wout, tm):
    T, D = x.shape

    def body(x_ref, o_ref, yc_ref, ga_ref, gc_ref, woa_ref, woc_ref, wout_ref, out_ref):
        ya = _dot(o_ref[...], woa_ref[...])
        yp = _dot(yc_ref[...], woc_ref[...])
        merged = _sigmoid(ga_ref[...].astype(F32)) * ya + _sigmoid(gc_ref[...].astype(F32)) * yp
        out_ref[...] = x_ref[...] + _dot(merged.astype(BF16), wout_ref[...])

    ga, gc = _gate_specs(tm, D)
    row = lambda w: pl.BlockSpec((tm, w), lambda i: (i, 0))
    whole = lambda a: pl.BlockSpec(a.shape, lambda i: (0, 0))
    return pl.pallas_call(
        body, name="mix_out_fwd", grid=(T // tm,),
        in_specs=[row(D), row(ATTN_W), row(CONV_W), ga, gc, whole(woa), whole(woc), whole(wout)],
        out_specs=row(D),
        out_shape=jax.ShapeDtypeStruct((T, D), F32),
        compiler_params=_params(("arbitrary",)),
    )(x, o, yc, proj, proj, woa, woc, wout)


def _mix_out_bwd(dx, o, yc, proj, woa, woc, wout, tm, comm=None):
    T, D = dx.shape
    nt = T // tm

    def body(dx_ref, o_ref, yc_ref, ga_ref, gc_ref, woa_ref, woc_ref, wout_ref,
             do_ref, dyc_ref, dg_ref, dwoa_ref, dwoc_ref, dwout_ref, acca, accc, acco):
        t = pl.program_id(0)

        @pl.when(t == 0)
        def _():
            acca[...] = jnp.zeros_like(acca)
            accc[...] = jnp.zeros_like(accc)
            acco[...] = jnp.zeros_like(acco)

        dxb = dx_ref[...].astype(BF16)
        ov, ycv = o_ref[...], yc_ref[...]
        ya = _dot(ov, woa_ref[...])
        yp = _dot(ycv, woc_ref[...])
        sa = _sigmoid(ga_ref[...].astype(F32))
        sc = _sigmoid(gc_ref[...].astype(F32))
        merged = (sa * ya + sc * yp).astype(BF16)
        dm = _dot_nt(dxb, wout_ref[...])
        dya = (dm * sa).astype(BF16)
        dyp = (dm * sc).astype(BF16)
        dg_ref[:, :D] = (dm * ya * sa * (1.0 - sa)).astype(BF16)
        dg_ref[:, D:] = (dm * yp * sc * (1.0 - sc)).astype(BF16)
        do_ref[...] = _dot_nt(dya, woa_ref[...]).astype(BF16)
        dyc_ref[...] = _dot_nt(dyp, woc_ref[...]).astype(BF16)
        acca[...] += _dot_tn(ov, dya)
        accc[...] += _dot_tn(ycv, dyp)
        acco[...] += _dot_tn(merged, dxb)

        @pl.when(t == nt - 1)
        def _():
            dwoa_ref[...] = acca[...].astype(BF16)
            dwoc_ref[...] = accc[...].astype(BF16)
            dwout_ref[...] = acco[...].astype(BF16)

    ga, gc = _gate_specs(tm, D)
    row = lambda w: pl.BlockSpec((tm, w), lambda i: (i, 0))
    whole = lambda a: pl.BlockSpec(a.shape, lambda i: (0, 0))
    return _pallas(
        body, "mix_out_bwd", (nt,),
        [row(D), row(ATTN_W), row(CONV_W), ga, gc, whole(woa), whole(woc), whole(wout)],
        [row(ATTN_W), row(CONV_W), row(2 * D), whole(woa), whole(woc), whole(wout)],
        [jax.ShapeDtypeStruct((T, ATTN_W), BF16), jax.ShapeDtypeStruct((T, CONV_W), BF16),
         jax.ShapeDtypeStruct((T, 2 * D), BF16),
         jax.ShapeDtypeStruct(woa.shape, BF16), jax.ShapeDtypeStruct(woc.shape, BF16),
         jax.ShapeDtypeStruct(wout.shape, BF16)],
        [pltpu.VMEM(woa.shape, F32), pltpu.VMEM(woc.shape, F32), pltpu.VMEM(wout.shape, F32)],
        (dx, o, yc, proj, proj, woa, woc, wout), comm)


def _proj_pieces(dq, dk, dv, dcb, dcc, dcx, dgates, dflog):
    D = dgates.shape[1] // 2
    return [(dq, ATTN_W, 0), (dk, ATTN_W, 0), (dv, ATTN_W, 0), (dcb, CONV_W, 0), (dcc, CONV_W, 0), (dcx, CONV_W, 0),
            (dgates, D, 0), (dgates, D, 1), (dflog, LANES, 0)]


def _mix_proj_bwd_dx(dres, x, g, pieces, wproj_t, wf_t, tm, comm=None):
    T, D = x.shape
    n = len(pieces)
    w_blocks = [(ATTN_W, 0), (ATTN_W, 1), (ATTN_W, 2), (CONV_W, 3), (CONV_W, 4), (CONV_W, 5),
                (D, COL_GATES // D), (D, COL_GATES // D + 1)]

    def body(*refs):
        dres_ref, x_ref, g_ref = refs[:3]
        p_refs, w_refs = refs[3:3 + n], refs[3 + n:3 + 2 * n]
        dx_ref, dg_ref = refs[3 + 2 * n:]

        @pl.when(pl.program_id(0) == 0)
        def _():
            dg_ref[...] = jnp.zeros_like(dg_ref)

        dh = _dot(p_refs[0][...].astype(BF16), w_refs[0][...])
        for p_ref, w_ref in zip(p_refs[1:], w_refs[1:]):
            dh = dh + _dot(p_ref[...].astype(BF16), w_ref[...])
        xhat, inv = _rms(x_ref[...])
        dx, dg = _rms_bwd(dh, xhat, inv, g_ref[...])
        dx_ref[...] = dres_ref[...] + dx
        dg_ref[...] += dg

    row = pl.BlockSpec((tm, D), lambda i: (i, 0))
    vec = pl.BlockSpec((1, D), lambda i: (0, 0))
    p_specs = [pl.BlockSpec((tm, w), lambda i, cb=cb: (i, cb)) for _, w, cb in pieces]
    w_specs = [pl.BlockSpec((r, D), lambda i, rb=rb: (rb, 0)) for r, rb in w_blocks]
    w_specs.append(pl.BlockSpec((LANES, D), lambda i: (0, 0)))
    return _pallas(
        body, "mix_proj_bwd_dx", (T // tm,),
        [row, row, vec] + p_specs + w_specs, [row, vec],
        [jax.ShapeDtypeStruct((T, D), F32), jax.ShapeDtypeStruct((1, D), F32)], [],
        (dres, x, g, *[p for p, _, _ in pieces], *([wproj_t] * len(w_blocks)), wf_t), comm)


def _matmuls_tn(name, pieces, b, tk):
    T, N = b.shape
    nt = T // tk
    n = len(pieces)

    def body(*refs):
        a_refs, b_ref, out_refs, accs = refs[:n], refs[n], refs[n + 1:2 * n + 1], refs[2 * n + 1:]
        t = pl.program_id(0)

        @pl.when(t == 0)
        def _():
            for acc in accs:
                acc[...] = jnp.zeros_like(acc)

        bv = b_ref[...]
        for a_ref, acc in zip(a_refs, accs):
            acc[...] += _dot_tn(a_ref[...].astype(BF16), bv)

        @pl.when(t == nt - 1)
        def _():
            for out_ref, acc in zip(out_refs, accs):
                out_ref[...] = acc[...].astype(BF16)

    return pl.pallas_call(
        body, name=name, grid=(nt,),
        in_specs=[pl.BlockSpec((tk, w), lambda t, cb=cb: (t, cb)) for _, w, cb in pieces]
        + [pl.BlockSpec((tk, N), lambda t: (t, 0))],
        out_specs=[pl.BlockSpec((w, N), lambda t: (0, 0)) for _, w, _ in pieces],
        out_shape=[jax.ShapeDtypeStruct((w, N), BF16) for _, w, _ in pieces],
        scratch_shapes=[pltpu.VMEM((w, N), F32) for _, w, _ in pieces],
        compiler_params=_params(("arbitrary",)),
    )(*[a for a, _, _ in pieces], b)


TOKEN_TILE = 512
TOKEN_TILE_WIDE = 1024
ATTN_TILE = 512
SCAN_CHUNK = 256
PROJ_DX_TILE = 512


def _local_step(x, target, plan, B, S):
    T, D = x.shape
    tm = min(TOKEN_TILE, T)
    tm_fwd = min(TOKEN_TILE_WIDE, T)
    tq = min(ATTN_TILE, S)
    nq = S // tq
    ch = min(SCAN_CHUNK, S)

    def riding(kernel_name, build):
        results, brought = build(plan.rider(kernel_name))
        plan.arrived(kernel_name, brought)
        return results

    hg1, hu1, n1 = plan.ffn1_up(x, tm_fwd)
    w1 = plan.weights("ffn1")
    x1, = riding("ffn1_down", lambda comm: _ffn_down("ffn1_down", x, hg1, hu1, w1["ffn1_down"], tm_fwd, comm))
    wm = plan.weights("mix_in")
    h, proj, flog = riding("mix_proj_fwd", lambda comm: _mix_proj_fwd(
        x1, wm["mix_norm"], wm["w_proj"], wm["w_f"], tm_fwd, PROJ_W // 4, comm))
    wm.update(plan.weights("mix_out"))
    cum = _fgate_fwd(flog, wm["b_forget"], B, S, ch)
    cum_t = jnp.transpose(cum[:, :N_HEADS].reshape(B, nq, tq, N_HEADS), (0, 1, 3, 2))
    o, lse = riding("attn_fwd", lambda comm: _attn_fwd(proj, cum, cum_t, B, S, tq, comm))
    yc = _conv_fwd(proj, wm["conv_w"], B, S)
    x2 = _mix_out_fwd(x1, o, yc, proj, wm["w_o_attn"], wm["w_o_conv"], wm["w_out"], tm_fwd)
    w2 = plan.weights("ffn2")
    dx3, hg2, hu2, n2, loss, d_final_norm = _ffn_fwd_loss(
        "ffn2_fwd_loss", x2, w2["ffn2_norm"], w2["ffn2_gate"], w2["ffn2_up"], w2["ffn2_down"], target, w2["final_norm"],
        tm_fwd)

    g = {"final_norm": d_final_norm}
    dx2, dhg2, dhu2, g["ffn2_norm"], df2 = _ffn_bwd_dx("ffn2_bwd_dx", dx3, x2, w2["ffn2_norm"], hg2, hu2,
                                                  w2["ffn2_gate"], w2["ffn2_up"], w2["ffn2_down"], tm_fwd)[0]
    plan.reduce("ffn2", dict(zip(("ffn2_gate", "ffn2_up", "ffn2_down"),
                                 _ffn_bwd_dw("ffn2_bwd_dw", n2, df2, hg2, hu2, dhg2, dhu2, tm_fwd)[0])))
    do, dyc, dgates, dwoa, dwoc, dwout = riding("mix_out_bwd", lambda comm: _mix_out_bwd(
        dx2, o, yc, proj, wm["w_o_attn"], wm["w_o_conv"], wm["w_out"], tm, comm))
    plan.reduce("out", dict(w_o_attn=_shard_cols(dwoa), w_o_conv=_shard_cols(dwoc), w_out=dwout.reshape(N_CHIPS, -1, D)))
    dq, dk, dv, dcq, dck = riding("attn_bwd", lambda comm: _attn_bwd(proj, o, do, lse, cum, cum_t, B, S, tq, comm))
    dcum = dcq + jnp.pad(jnp.transpose(dck, (0, 1, 3, 2)).reshape(T, N_HEADS), ((0, 0), (0, LANES - N_HEADS)))
    dflog, g["b_forget"] = _fgate_bwd(dcum, flog, wm["b_forget"], B, S, ch)
    dcb, dcc, dcx, g["conv_w"] = _conv_bwd(dyc, proj, wm["conv_w"], B, S)
    pieces = _proj_pieces(dq, dk, dv, dcb, dcc, dcx, dgates, dflog)
    dwq, dwk, dwv, dwcb, dwcc, dwcx = _matmuls_tn("mix_dw_a", pieces[:6], h, tm_fwd)
    dwga, dwgc, dwf = _matmuls_tn("mix_dw_b", pieces[6:], h, tm_fwd)
    dwin_t = jnp.concatenate([dwq, dwk, dwv, dwf[:N_HEADS], dwcb, dwcc, dwcx, dwga, dwgc], axis=0)
    plan.reduce("w_in", {"w_in": dwin_t.reshape(N_CHIPS, -1, D)})
    dx1, g["mix_norm"] = riding("mix_proj_bwd_dx", lambda comm: _mix_proj_bwd_dx(
        dx2, x1, wm["mix_norm"], pieces, wm["w_proj"], wm["w_f"], min(PROJ_DX_TILE, T), comm))
    grad_x, dhg1, dhu1, g["ffn1_norm"], df1 = _ffn_bwd_dx(
        "ffn1_bwd_dx", dx1, x, w1["ffn1_norm"], hg1, hu1, w1["ffn1_gate"], w1["ffn1_up"], w1["ffn1_down"], tm_fwd)[0]
    plan.reduce_small(g, loss)
    plan.reduce("ffn1", dict(zip(("ffn1_gate", "ffn1_up", "ffn1_down"), riding("ffn1_bwd_dw", lambda comm: _ffn_bwd_dw(
        "ffn1_bwd_dw", n1, df1, hg1, hu1, dhg1, dhu1, tm_fwd, comm)))))
    return loss, grad_x, g


TRANSPOSED = ("ffn1_gate", "ffn1_up", "ffn2_gate", "ffn2_up", "w_in")
NORMS = ("ffn1_norm", "mix_norm", "ffn2_norm", "final_norm")


def _unshard_cols(a):
    return jnp.transpose(a, (1, 0, 2)).reshape(a.shape[1], N_CHIPS * a.shape[2])


def _shard_cols(a):
    return jnp.transpose(a.reshape(a.shape[0], N_CHIPS, a.shape[1] // N_CHIPS), (1, 0, 2))


def _layout_ffn(which):
    def layout(st, small):
        w = {n: st[n] for n in (which + "_gate", which + "_up", which + "_down")}
        w[which + "_norm"] = small[which + "_norm"].reshape(1, -1)
        if which == "ffn2":
            w["final_norm"] = small["final_norm"].reshape(1, -1)
        return w
    return layout


def _layout_mix_in(st, small):
    win_t = st["w_in"].reshape(-1, st["w_in"].shape[2])
    return {
        "w_proj": jnp.concatenate([win_t[:N_FORGET_COL], win_t[N_FORGET_COL + N_HEADS:]], axis=0),
        "w_f": jnp.pad(win_t[N_FORGET_COL:N_FORGET_COL + N_HEADS], ((0, LANES - N_HEADS), (0, 0))),
        "conv_w": _unshard_cols(st["conv_w"]),
        "mix_norm": small["mix_norm"].reshape(1, -1),
        "b_forget": jnp.pad(small["b_forget"].reshape(1, -1), ((0, 0), (0, LANES - N_HEADS))),
    }


def _layout_mix_out(st, small):
    return {"w_o_attn": _unshard_cols(st["w_o_attn"]), "w_o_conv": _unshard_cols(st["w_o_conv"]),
            "w_out": st["w_out"].reshape(-1, st["w_out"].shape[2])}


_LAYOUTS = {"ffn1": _layout_ffn("ffn1"), "mix_in": _layout_mix_in, "mix_out": _layout_mix_out, "ffn2": _layout_ffn("ffn2")}


ANY = pl.BlockSpec(memory_space=pl.ANY)
BIG = ("ffn1_gate", "ffn1_up", "ffn1_down", "w_in", "w_o_attn", "w_o_conv", "w_out",
       "ffn2_gate", "ffn2_up", "ffn2_down")


def _place():
    x, y, c = lax.axis_index("x"), lax.axis_index("y"), lax.axis_index("c")
    others = [(1 - x, y), (x, 1 - y), (1 - x, 1 - y)]
    return x, y, c, others


def _col_halves(cols, c):
    hc = cols // 2
    return pl.ds(pl.multiple_of(c * hc, LANES), hc), pl.ds(pl.multiple_of((1 - c) * hc, LANES), hc)


def _gather_comm(shards, conv_shard=None):
    n = len(shards)
    inputs = list(shards) + ([] if conv_shard is None else [conv_shard])

    def copies(ins, outs, sems):
        send_sems, recv_sems, pass_send, pass_recv = sems[:4]
        x, y, c, others = _place()

        def chip_copy(a, j, chip):
            mine, _ = _col_halves(ins[a].shape[1], c)
            return pltpu.make_async_remote_copy(
                src_ref=ins[a].at[:, mine], dst_ref=outs[a].at[chip, :, mine],
                send_sem=send_sems.at[3 * a + j], recv_sem=recv_sems.at[3 * a + j],
                device_id=(*others[j], c), device_id_type=MESH)

        def pass_copy(a, j, chip, half):
            return pltpu.make_async_remote_copy(
                src_ref=outs[a].at[chip, :, half], dst_ref=outs[a].at[chip, :, half],
                send_sem=pass_send.at[3 * a + j], recv_sem=pass_recv.at[3 * a + j],
                device_id=(x, y, 1 - c), device_id_type=MESH)

        def conv_copy(j, chip):
            return pltpu.make_async_remote_copy(
                src_ref=ins[n], dst_ref=outs[n].at[chip],
                send_sem=sems[4].at[j], recv_sem=sems[5].at[j],
                device_id=(*others[j], c), device_id_type=MESH)

        me = 2 * x + y
        sends = [chip_copy(a, j, me) for a in range(n) for j in range(3)]
        if conv_shard is not None:
            sends += [conv_copy(j, me) for j in range(3)]
        return c, others, sends, chip_copy, pass_copy, conv_copy

    def start(ins, outs, sems):
        for cp in copies(ins, outs, sems)[2]:
            cp.start()

    def finish(ins, outs, sems):
        c, others, sends, chip_copy, pass_copy, conv_copy = copies(ins, outs, sems)
        passed = []
        for a in range(n):
            mine, _ = _col_halves(ins[a].shape[1], c)
            for j, (ox, oy) in enumerate(others):
                chip_copy(a, j, 2 * ox + oy).wait_recv()
                passed.append(pass_copy(a, j, 2 * ox + oy, mine))
                passed[-1].start()
        for a in range(n):
            _, theirs = _col_halves(ins[a].shape[1], c)
            for j, (ox, oy) in enumerate(others):
                pass_copy(a, j, 2 * ox + oy, theirs).wait_recv()
        if conv_shard is not None:
            for j, (ox, oy) in enumerate(others):
                conv_copy(j, 2 * ox + oy).wait_recv()
        for cp in sends + passed:
            cp.wait_send()

    scratch = [pltpu.SemaphoreType.DMA((3 * n,))] * 4
    if conv_shard is not None:
        scratch += [pltpu.SemaphoreType.DMA((3,))] * 2
    return _Comm(inputs, [jax.ShapeDtypeStruct((N_CHIPS,) + s.shape, s.dtype) for s in inputs], scratch, start, finish)


def _fill_own(stacks, shards):
    chip = 2 * lax.axis_index("x") + lax.axis_index("y")
    return [lax.dynamic_update_index_in_dim(st, s, chip, 0) for st, s in zip(stacks, shards)]


def _run_comm(name, comm):
    ci, co = len(comm.inputs), len(comm.out_shape)

    def body(*refs):
        comm.start(refs[:ci], refs[ci:ci + co], refs[ci + co:])
        comm.finish(refs[:ci], refs[ci:ci + co], refs[ci + co:])

    return pl.pallas_call(body, name=name, in_specs=[ANY] * ci, out_specs=[ANY] * co, out_shape=comm.out_shape,
                          scratch_shapes=comm.scratch)(*comm.inputs)


def _sibling_exchange_comm(grads):
    n = len(grads)

    def copies(ins, outs, sems):
        x, y, c, _ = _place()
        return [pltpu.make_async_remote_copy(
            src_ref=ins[a].at[:, :, _col_halves(ins[a].shape[2], c)[1]], dst_ref=outs[a],
            send_sem=sems[0].at[a], recv_sem=sems[1].at[a],
            device_id=(x, y, 1 - c), device_id_type=MESH) for a in range(n)]

    def start(ins, outs, sems):
        for cp in copies(ins, outs, sems):
            cp.start()

    def finish(ins, outs, sems):
        for cp in copies(ins, outs, sems):
            cp.wait()

    half = lambda s: jax.ShapeDtypeStruct((s.shape[0], s.shape[1], s.shape[2] // 2), s.dtype)
    return _Comm(grads, [half(s) for s in grads], [pltpu.SemaphoreType.DMA((n,))] * 2, start, finish)


def _merge_comms(comms):
    def split(refs, count):
        out, at = [], 0
        for cm in comms:
            out.append(refs[at:at + count(cm)])
            at += count(cm)
        return out

    def parts(ins, outs, sems):
        return zip(comms, split(ins, lambda cm: len(cm.inputs)), split(outs, lambda cm: len(cm.out_shape)),
                   split(sems, lambda cm: len(cm.scratch)))

    def start(ins, outs, sems):
        for cm, i, o, s in parts(ins, outs, sems):
            cm.start(i, o, s)

    def finish(ins, outs, sems):
        for cm, i, o, s in parts(ins, outs, sems):
            cm.finish(i, o, s)

    return _Comm(sum([cm.inputs for cm in comms], []), sum([cm.out_shape for cm in comms], []),
                 sum([cm.scratch for cm in comms], []), start, finish)


def _add_halves(name, grads, recvs, core):
    n = len(grads)

    def body(core_ref, *refs):
        for g_ref, r_ref, out_ref in zip(refs[:n], refs[n:2 * n], refs[2 * n:]):
            out_ref[...] = (g_ref[...].astype(F32) + r_ref[...].astype(F32)).astype(BF16)

    half = lambda g: pl.BlockSpec((None, g.shape[1], g.shape[2] // 2), lambda k, core_ref: (k, 0, 0))
    mine = lambda g: pl.BlockSpec((None, g.shape[1], g.shape[2] // 2), lambda k, core_ref: (k, 0, core_ref[0]))
    return pl.pallas_call(
        body, name=name,
        grid_spec=pltpu.PrefetchScalarGridSpec(
            num_scalar_prefetch=1, grid=(N_CHIPS,),
            in_specs=[mine(g) for g in grads] + [half(g) for g in grads],
            out_specs=[half(g) for g in grads]),
        out_shape=[jax.ShapeDtypeStruct(r.shape, BF16) for r in recvs],
        compiler_params=_params(("arbitrary",)),
    )(core, *grads, *recvs)


def _chip_exchange_comm(parts):
    n = len(parts)

    def copies(ins, outs, sems):
        x, y, c, others = _place()
        return [pltpu.make_async_remote_copy(
            src_ref=ins[a].at[2 * ox + oy], dst_ref=outs[a].at[j],
            send_sem=sems[0].at[3 * a + j], recv_sem=sems[1].at[3 * a + j],
            device_id=(ox, oy, c), device_id_type=MESH) for a in range(n) for j, (ox, oy) in enumerate(others)]

    def start(ins, outs, sems):
        for cp in copies(ins, outs, sems):
            cp.start()

    def finish(ins, outs, sems):
        for cp in copies(ins, outs, sems):
            cp.wait()

    return _Comm(parts, [jax.ShapeDtypeStruct((3,) + s.shape[1:], s.dtype) for s in parts],
                 [pltpu.SemaphoreType.DMA((3 * n,))] * 2, start, finish)


HBM = pl.BlockSpec(memory_space=pltpu.HBM)
SEM = pl.BlockSpec(memory_space=pltpu.SEMAPHORE)


def _split_exchange_copies(parts, lands, send_sems, recv_sems):
    x, y, c, others = _place()
    return [pltpu.make_async_remote_copy(
        src_ref=parts[a].at[2 * ox + oy], dst_ref=lands[a].at[j],
        send_sem=send_sems.at[3 * a + j], recv_sem=recv_sems.at[3 * a + j],
        device_id=(ox, oy, c), device_id_type=MESH) for a in range(len(parts)) for j, (ox, oy) in enumerate(others)]


def _exchange_start(name, parts):
    n = len(parts)

    def body(*refs):
        ins, lands = refs[:n], refs[n:2 * n]
        send_sems, recv_sems, token = refs[2 * n], refs[2 * n + 1], refs[-1]
        for cp in _split_exchange_copies(ins, lands, send_sems, recv_sems):
            cp.start()
        token[...] = jnp.zeros_like(token)

    land_shape = [(3,) + p.shape[1:] for p in parts]
    outs = pl.pallas_call(
        body, name=name,
        out_shape=[pltpu.SemaphoreType.DMA((3 * n,)), pltpu.SemaphoreType.DMA((3 * n,))]
        + [pltpu.HBM(p.shape, p.dtype) for p in parts] + [pltpu.HBM(s, p.dtype) for s, p in zip(land_shape, parts)]
        + [jax.ShapeDtypeStruct((8, LANES), F32)],
        in_specs=[HBM] * (2 * n), out_specs=[SEM, SEM] + [HBM] * (2 * n) + [pl.BlockSpec(memory_space=pltpu.VMEM)],
        input_output_aliases={i: 2 + i for i in range(2 * n)},
        compiler_params=pltpu.CompilerParams(has_side_effects=pltpu.SideEffectType.DATAFLOW_SIDE_EFFECTING),
    )(*[pltpu.with_memory_space_constraint(p, pltpu.HBM) for p in parts],
      *[pltpu.with_memory_space_constraint(lax.empty(s, p.dtype), pltpu.HBM) for s, p in zip(land_shape, parts)])
    return outs[0], outs[1], list(outs[2:2 + n]), list(outs[2 + n:2 + 2 * n]), outs[-1]


def _exchange_wait(name, send_sems, recv_sems, parts, lands, after):
    n = len(parts)

    def body(*refs):
        ins, zones = refs[:n], refs[n:2 * n]
        for cp in _split_exchange_copies(ins, zones, refs[2 * n], refs[2 * n + 1]):
            cp.wait_send()
            cp.wait_recv()

    outs = pl.pallas_call(
        body, name=name,
        out_shape=[pltpu.HBM(p.shape, p.dtype) for p in parts] + [pltpu.HBM(z.shape, z.dtype) for z in lands],
        in_specs=[HBM] * (2 * n) + [SEM, SEM] + [ANY] * len(after), out_specs=[HBM] * (2 * n),
        input_output_aliases={i: i for i in range(2 * n)},
        compiler_params=pltpu.CompilerParams(has_side_effects=pltpu.SideEffectType.DATAFLOW_SIDE_EFFECTING),
    )(*parts, *lands, send_sems, recv_sems, *after)
    return list(outs[:n]), list(outs[n:])


def _sum_chips(name, owns, recvs, chip, after):
    n = len(owns)
    hc = owns[0].shape[2]
    assert all(o.shape[2] == hc for o in owns)

    def body(chip_ref, *refs):
        for own_ref, recv_ref, out_ref in zip(refs[:n], refs[n:2 * n], refs[2 * n + 1:]):
            acc = own_ref[...].astype(F32)
            for j in range(3):
                acc = acc + recv_ref[j].astype(F32)
            out_ref[...] = acc

    return pl.pallas_call(
        body, name=name,
        grid_spec=pltpu.PrefetchScalarGridSpec(
            num_scalar_prefetch=1, grid=(hc // LANES,),
            in_specs=[pl.BlockSpec((None, o.shape[1], LANES), lambda i, chip_ref: (chip_ref[0], 0, i)) for o in owns]
            + [pl.BlockSpec((3, o.shape[1], LANES), lambda i, chip_ref: (0, 0, i)) for o in owns]
            + [pl.BlockSpec((8, LANES), lambda i, chip_ref: (0, 0))],
            out_specs=[pl.BlockSpec((o.shape[1], LANES), lambda i, chip_ref: (0, i)) for o in owns]),
        out_shape=[jax.ShapeDtypeStruct((o.shape[1], hc), F32) for o in owns],
        compiler_params=_params(("arbitrary",)),
    )(chip, *owns, *recvs, after)


def _share_halves(name, halves):
    n = len(halves)

    def body(*refs):
        srcs, dsts = refs[:n], refs[n:2 * n]
        send_sems, recv_sems = refs[2 * n:]
        x, y, c, _ = _place()
        copies = [pltpu.make_async_remote_copy(
            src_ref=srcs[a], dst_ref=dsts[a], send_sem=send_sems.at[a], recv_sem=recv_sems.at[a],
            device_id=(x, y, 1 - c), device_id_type=MESH) for a in range(n)]
        for cp in copies:
            cp.start()
        for cp in copies:
            cp.wait()

    return pl.pallas_call(
        body, name=name,
        in_specs=[ANY] * n, out_specs=[ANY] * n,
        out_shape=[jax.ShapeDtypeStruct(s.shape, s.dtype) for s in halves],
        scratch_shapes=[pltpu.SemaphoreType.DMA((n,)), pltpu.SemaphoreType.DMA((n,))],
    )(*halves)


def _small_gather_comm(part):
    def copies(ins, outs, sems):
        x, y, c, _ = _place()
        me = 4 * x + 2 * y + c
        both = []
        for d in range(1, N_DEV):
            px, py, pc = (1 - x if d & 4 else x, 1 - y if d & 2 else y, 1 - c if d & 1 else c)
            send = pltpu.make_async_remote_copy(
                src_ref=ins[0], dst_ref=outs[0].at[me], send_sem=sems[0].at[d - 1], recv_sem=sems[1].at[d - 1],
                device_id=(px, py, pc), device_id_type=MESH)
            recv = pltpu.make_async_remote_copy(
                src_ref=ins[0], dst_ref=outs[0].at[4 * px + 2 * py + pc], send_sem=sems[0].at[d - 1],
                recv_sem=sems[1].at[d - 1], device_id=(px, py, pc), device_id_type=MESH)
            both.append((send, recv))
        return both

    def start(ins, outs, sems):
        for send, _ in copies(ins, outs, sems):
            send.start()

    def finish(ins, outs, sems):
        for send, recv in copies(ins, outs, sems):
            recv.wait_recv()
            send.wait_send()

    return _Comm([part], [jax.ShapeDtypeStruct((N_DEV,) + part.shape, F32)],
                 [pltpu.SemaphoreType.DMA((N_DEV - 1,))] * 2, start, finish)


def _sum_devices(parts):
    def body(p_ref, out_ref):
        acc = p_ref[0]
        for k in range(1, N_DEV):
            acc = acc + p_ref[k]
        out_ref[...] = acc

    return pl.pallas_call(
        body, name="sum_devices", grid=(1,),
        in_specs=[pl.BlockSpec(parts.shape, lambda i: (0, 0, 0))],
        out_specs=pl.BlockSpec(parts.shape[1:], lambda i: (0, 0)),
        out_shape=jax.ShapeDtypeStruct(parts.shape[1:], F32),
        compiler_params=_params(("arbitrary",)),
    )(parts)


def _adam_update(w, g, m, v):
    nm = ADAM_B1 * m + (1.0 - ADAM_B1) * g
    nv = ADAM_B2 * v + (1.0 - ADAM_B2) * (g * g)
    m_hat = nm * (1.0 / (1.0 - ADAM_B1 ** ADAM_STEP))
    v_hat = nv * (1.0 / (1.0 - ADAM_B2 ** ADAM_STEP))
    return -ADAM_LR * (m_hat / (jnp.sqrt(v_hat) + ADAM_EPS) + ADAM_WD * w), nm, nv


def _adamw(name, w, g, m, v):
    def body(w_ref, g_ref, m_ref, v_ref, d_ref, nm_ref, nv_ref):
        d_ref[...], nm_ref[...], nv_ref[...] = _adam_update(w_ref[...], g_ref[...], m_ref[...], v_ref[...])

    spec = pl.BlockSpec(w.shape, lambda i: (0, 0))
    out = jax.ShapeDtypeStruct(w.shape, F32)
    return pl.pallas_call(
        body, name=name, grid=(1,),
        in_specs=[spec] * 4, out_specs=[spec] * 3, out_shape=[out] * 3,
        compiler_params=_params(("arbitrary",)),
    )(w, g, m, v)


def _adamw_halves(name, ws, mines, theirs, ms, vs, core):
    n = len(ws)
    cols = ws[0].shape[1]
    assert all(w.shape[1] == cols for w in ws)
    hc = cols // 2
    tc = LANES if n > 1 else min(256, hc)
    nt = hc // tc

    def body(core_ref, *refs):
        ins, outs = refs[:5 * n], refs[5 * n:]
        for a in range(n):
            w_ref, mine_ref, theirs_ref, m_ref, v_ref = [ins[j * n + a] for j in range(5)]
            g_ref, d_ref, nm_ref, nv_ref = outs[4 * a:4 * a + 4]
            gv = jnp.where(pl.program_id(0) == core_ref[0], mine_ref[...], theirs_ref[...])
            g_ref[...] = gv
            d_ref[...], nm_ref[...], nv_ref[...] = _adam_update(w_ref[...], gv, m_ref[...], v_ref[...])

    whole = lambda w: pl.BlockSpec((w.shape[0], tc), lambda h, i, core_ref: (0, h * nt + i))
    mine_spec = lambda w: pl.BlockSpec((w.shape[0], tc), lambda h, i, core_ref: (0, jnp.where(h == core_ref[0], i, 0)))
    theirs_spec = lambda w: pl.BlockSpec((w.shape[0], tc), lambda h, i, core_ref: (0, jnp.where(h == core_ref[0], 0, i)))
    outs = pl.pallas_call(
        body, name=name,
        grid_spec=pltpu.PrefetchScalarGridSpec(
            num_scalar_prefetch=1, grid=(2, nt),
            in_specs=[whole(w) for w in ws] + [mine_spec(w) for w in ws] + [theirs_spec(w) for w in ws]
            + [whole(w) for w in ws] * 2,
            out_specs=[whole(w) for w in ws for _ in range(4)]),
        out_shape=[jax.ShapeDtypeStruct(w.shape, F32) for w in ws for _ in range(4)],
        compiler_params=_params(("arbitrary", "arbitrary")),
    )(core, *ws, *mines, *theirs, *ms, *vs)
    return [outs[4 * a:4 * a + 4] for a in range(n)]


WEIGHTS = ("ffn1_norm", "ffn1_gate", "ffn1_up", "ffn1_down", "mix_norm", "w_in", "b_forget", "conv_w",
           "w_o_attn", "w_o_conv", "w_out", "ffn2_norm", "ffn2_gate", "ffn2_up", "ffn2_down", "final_norm")
VEC_ROWS = 8


def _pack_small(t, conv_rows):
    conv = t["conv_w"]
    parts = [t[n].reshape(VEC_ROWS, LANES) for n in NORMS]
    parts.append(jnp.pad(conv, ((0, conv_rows - conv.shape[0]), (0, 0))))
    parts.append(jnp.pad(t["b_forget"].reshape(1, N_HEADS), ((0, 7), (0, LANES - N_HEADS))))
    return jnp.concatenate(parts, axis=0)


def _unpack_small(p, conv_rows):
    out = {n: p[VEC_ROWS * i:VEC_ROWS * (i + 1)].reshape(-1) for i, n in enumerate(NORMS)}
    base = VEC_ROWS * len(NORMS)
    out["conv_w"] = p[base:base + 3]
    out["b_forget"] = p[base + conv_rows, :N_HEADS]
    return out


def _travel(name, a):
    return a.T if name in TRANSPOSED else a


GATHER_FIRST = ("ffn1_gate", "ffn1_up")
GATHER_RIDES = {"ffn1_up": ("ffn1_down",), "ffn1_down": ("w_in",), "mix_proj_fwd": ("w_o_attn", "w_o_conv", "w_out"),
                "attn_fwd": ("ffn2_gate", "ffn2_up", "ffn2_down")}
SIBLING_RIDES = {"ffn2": "mix_out_bwd", "out": None, "w_in": "mix_proj_bwd_dx", "ffn1": None}
CHIP_RIDES = {"ffn2": "attn_bwd", "out": "attn_bwd", "w_in": "ffn1_bwd_dw", "ffn1": None}
SMALL_RIDE = "ffn1_bwd_dw"


class _MeshPlan:
    def __init__(self, wts, core):
        self.small, self.core = wts, core
        self.shards = {n: wts[n].astype(BF16) for n in BIG}
        self.chip_part, self.from_chips, self.rides = {}, {}, {}
        self.stacks = {}
        conv_shard = jnp.pad(wts["conv_w"], ((0, 8 - wts["conv_w"].shape[0]), (0, 0)))
        for kernel_name, names in GATHER_RIDES.items():
            mine = [self.shards[n] for n in names]
            conv = conv_shard if kernel_name == "ffn1_up" else None
            names = names + (("conv_w",) if conv is not None else ())
            mine = mine + ([conv] if conv is not None else [])
            self._ride(kernel_name, _gather_comm(mine[:len(mine) - (conv is not None)], conv),
                       lambda got, names=names, mine=mine: self.stacks.update(zip(names, _fill_own(got, mine))))

    def weights(self, group):
        return _LAYOUTS[group](self.stacks, self.small)

    def ffn1_up(self, x, tm):
        px, py = lax.axis_index("x"), lax.axis_index("y")
        order = jnp.stack([2 * px + py, 2 * (1 - px) + py, 2 * px + (1 - py), 2 * (1 - px) + (1 - py)]).astype(jnp.int32)
        own = [self.shards[n] for n in GATHER_FIRST]
        (hg, hu, n, sg, su), brought = _ffn_up_gather("ffn1_up", x, self.small["ffn1_norm"].reshape(1, -1), *own, order,
                                                     tm, self.rider("ffn1_up"))
        self.stacks.update(zip(GATHER_FIRST, _fill_own([sg, su], own)))
        self.arrived("ffn1_up", brought)
        return hg, hu, n

    def _ride(self, kernel_name, comm, then):
        self.rides.setdefault(kernel_name, []).append((comm, then))

    def rider(self, kernel_name):
        comms = [comm for comm, _ in self.rides.get(kernel_name, [])]
        return _merge_comms(comms) if comms else None

    def arrived(self, kernel_name, results):
        for comm, then in self.rides.pop(kernel_name, []):
            then(results[:len(comm.out_shape)])
            results = results[len(comm.out_shape):]

    def reduce(self, group, grads):
        names = tuple(grads)
        mine = [grads[n] for n in names]

        def with_sibling(from_sibling):
            parts = _add_halves("add_halves_" + group, mine, list(from_sibling), self.core)
            self.chip_part.update(zip(names, parts))
            if CHIP_RIDES[group] is None:
                self.last = (names, _exchange_start("exchange_start_" + group, parts))
            else:
                self._ride(CHIP_RIDES[group], _chip_exchange_comm(parts),
                           lambda got: self.from_chips.update(zip(names, got)))

        if SIBLING_RIDES[group] is None:
            with_sibling(_run_comm("sibling_exchange_" + group, _sibling_exchange_comm(mine)))
        else:
            self._ride(SIBLING_RIDES[group], _sibling_exchange_comm(mine), with_sibling)

    def reduce_small(self, gs, loss):
        conv_all = _shard_cols(gs["conv_w"]).reshape(N_CHIPS * 8, LANES)
        part = _pack_small({**{n: gs[n] for n in NORMS}, "conv_w": conv_all, "b_forget": gs["b_forget"][0, :N_HEADS]},
                           N_CHIPS * 8)
        part = jnp.concatenate([part, jnp.broadcast_to(loss, (8, LANES))], axis=0)
        me = 4 * lax.axis_index("x") + 2 * lax.axis_index("y") + lax.axis_index("c")

        def landed(got):
            self.small_parts = lax.dynamic_update_index_in_dim(got[0], part, me, 0)

        self._ride(SMALL_RIDE, _small_gather_comm(part), landed)


def kernel(x, ffn1_norm, ffn1_gate, ffn1_up, ffn1_down, mix_norm, w_in, b_forget, conv_w, w_o_attn, w_o_conv, w_out, ffn2_norm, ffn2_gate, ffn2_up, ffn2_down, final_norm, loss_target, m_ffn1_norm, m_ffn1_gate, m_ffn1_up, m_ffn1_down, m_mix_norm, m_w_in, m_b_forget, m_conv_w, m_w_o_attn, m_w_o_conv, m_w_out, m_ffn2_norm, m_ffn2_gate, m_ffn2_up, m_ffn2_down, m_final_norm, v_ffn1_norm, v_ffn1_gate, v_ffn1_up, v_ffn1_down, v_mix_norm, v_w_in, v_b_forget, v_conv_w, v_w_o_attn, v_w_o_conv, v_w_out, v_ffn2_norm, v_ffn2_gate, v_ffn2_up, v_ffn2_down, v_final_norm):
    given = dict(locals())
    wts = {n: _travel(n, given[n]) for n in WEIGHTS}
    mom = {n: _travel(n, given["m_" + n]) for n in WEIGHTS}
    var = {n: _travel(n, given["v_" + n]) for n in WEIGHTS}
    B, S, D = x.shape
    chip = 2 * lax.axis_index("x") + lax.axis_index("y")
    chip1 = chip.astype(jnp.int32).reshape(1)
    core = lax.axis_index("c").astype(jnp.int32).reshape(1)

    plan = _MeshPlan(wts, core)
    loss, grad_x, gs = _local_step(x.reshape(B * S, D), loss_target.reshape(B * S, D), plan, B, S)

    last_names, (send_sems, recv_sems, parts_thru, lands, token) = plan.last
    delta, new_m, new_v, grads = {}, {}, {}, {}

    def finish(tag, names):
        by_cols = {}
        for n in names:
            by_cols.setdefault(wts[n].shape[1], []).append(n)
        mine = {}
        for cols, ns in by_cols.items():
            mine.update(zip(ns, _sum_chips("sum_chips_%s_%d" % (tag, cols), [plan.chip_part[n] for n in ns],
                                           [plan.from_chips[n] for n in ns], chip1, token)))
        theirs = dict(zip(names, _share_halves("share_halves_" + tag, [mine[n] for n in names])))
        raw = []
        for cols, ns in by_cols.items():
            outs = _adamw_halves("adamw_%s_%d" % (tag, cols), [wts[n] for n in ns], [mine[n] for n in ns],
                                 [theirs[n] for n in ns], [mom[n] for n in ns], [var[n] for n in ns], core)
            for n, per in zip(ns, outs):
                raw.append(per[-1])
                grads[n], delta[n], new_m[n], new_v[n] = [_travel(n, o) for o in per]
        return raw

    small_sum = _sum_devices(plan.small_parts)
    base = VEC_ROWS * len(NORMS)
    loss_row = small_sum.shape[0] - 8
    small_grads = _unpack_small(small_sum, N_CHIPS * 8)
    small_grads["conv_w"] = lax.dynamic_slice_in_dim(small_sum[base:base + N_CHIPS * 8], chip * 8, 8, axis=0)[:3]
    packs = [_pack_small(t, 8) for t in (wts, small_grads, mom, var)]
    small_out = _adamw("adamw_small", *packs)

    done = finish("early", [n for n in BIG if n not in last_names])
    parts_back, got = _exchange_wait("exchange_wait", send_sems, recv_sems, parts_thru, lands, done + list(small_out))
    plan.chip_part.update(zip(last_names, parts_back))
    plan.from_chips.update(zip(last_names, got))
    finish("last", last_names)
    grads.update(small_grads)
    for out, p in zip((delta, new_m, new_v), small_out):
        out.update(_unpack_small(p, 8))

    return (small_sum[loss_row, 0], grad_x.reshape(B, S, D), *[grads[n] for n in WEIGHTS], *[delta[n] for n in WEIGHTS],
            *[new_m[n] for n in WEIGHTS], *[new_v[n] for n in WEIGHTS])
```

```python
import functools
import math

import jax
import jax.numpy as jnp
from jax import lax
from jax.experimental import pallas as pl
from jax.experimental.pallas import tpu as pltpu

F32 = jnp.float32
BF16 = jnp.bfloat16
MESH = pl.DeviceIdType.MESH

N_CHIPS = 4
N_DEV = 8
N_HEADS = 8
HEAD_DIM = 64
HEAD_PAIRS = N_HEADS // 2
ATTN_W = N_HEADS * HEAD_DIM
CONV_W = 512
RMS_EPS = 1e-6
FFN_RES = 0.5
LANES = 128
VMEM_LIMIT = 56 * 1024 * 1024
ROW_BLOCK = 256

ADAM_LR = 0.001
ADAM_B1 = 0.9
ADAM_B2 = 0.999
ADAM_EPS = 1e-08
ADAM_WD = 0.01
ADAM_STEP = 10

PROJ_W = 3 * ATTN_W + 3 * CONV_W + 2 * 1024
COL_CB, COL_CC, COL_CX = 3 * ATTN_W, 3 * ATTN_W + CONV_W, 3 * ATTN_W + 2 * CONV_W
COL_GATES = 3 * ATTN_W + 3 * CONV_W
N_FORGET_COL = 3 * ATTN_W


def _params(sem=None, vmem=VMEM_LIMIT):
    return pltpu.CompilerParams(dimension_semantics=sem, vmem_limit_bytes=vmem)


def _dot(a, b):
    return lax.dot_general(a, b, (((1,), (0,)), ((), ())), preferred_element_type=F32)


def _dot_nt(a, b):
    return lax.dot_general(a, b, (((1,), (1,)), ((), ())), preferred_element_type=F32)


def _dot_tn(a, b):
    return lax.dot_general(a, b, (((0,), (0,)), ((), ())), preferred_element_type=F32)


def _sigmoid(x):
    return 1.0 / (1.0 + jnp.exp(-x))


def _rms(xv):
    inv = lax.rsqrt(jnp.mean(xv * xv, axis=-1, keepdims=True) + RMS_EPS)
    return xv * inv, inv


class _Comm:
    def __init__(self, inputs, out_shape, scratch, start, finish):
        self.inputs, self.out_shape, self.scratch = list(inputs), list(out_shape), list(scratch)
        self.start, self.finish = start, finish


def _pallas(body, name, grid, in_specs, out_specs, out_shape, scratch, args, comm=None):
    sem = ("arbitrary",) * len(grid)
    if comm is None:
        outs = pl.pallas_call(body, name=name, grid=grid, in_specs=in_specs, out_specs=out_specs,
                              out_shape=out_shape, scratch_shapes=scratch, compiler_params=_params(sem))(*args)
        return list(outs), []
    n_in, n_out, n_scr = len(in_specs), len(out_specs), len(scratch)
    ci, co = len(comm.inputs), len(comm.out_shape)

    def riding(*refs):
        ins, refs = refs[:n_in], refs[n_in:]
        cins, refs = refs[:ci], refs[ci:]
        outs, refs = refs[:n_out], refs[n_out:]
        couts, refs = refs[:co], refs[co:]
        scr, sems = refs[:n_scr], refs[n_scr:]
        ids = [pl.program_id(d) for d in range(len(grid))]
        first = functools.reduce(lambda a, b: a & b, [i == 0 for i in ids])
        last = functools.reduce(lambda a, b: a & b, [i == g - 1 for i, g in zip(ids, grid)])

        @pl.when(first)
        def _():
            comm.start(cins, couts, sems)

        body(*ins, *outs, *scr)

        @pl.when(last)
        def _():
            comm.finish(cins, couts, sems)

    any_spec = pl.BlockSpec(memory_space=pl.ANY)
    outs = pl.pallas_call(
        riding, name=name, grid=grid,
        in_specs=list(in_specs) + [any_spec] * ci, out_specs=list(out_specs) + [any_spec] * co,
        out_shape=list(out_shape) + comm.out_shape, scratch_shapes=list(scratch) + comm.scratch,
        compiler_params=_params(sem))(*args, *comm.inputs)
    return list(outs[:n_out]), list(outs[n_out:])


def _rms_bwd(dn, xhat, inv, g):
    dxhat = dn * g
    dx = inv * (dxhat - xhat * jnp.mean(dxhat * xhat, axis=-1, keepdims=True))
    return dx, jnp.sum(dn * xhat, axis=0, keepdims=True)


def _ffn_fwd_loss(name, x, g, wgt, wut, wd, target, gf, tm):
    T, D = x.shape
    K, Fs, _ = wgt.shape

    def body(x_ref, g_ref, wg_ref, wu_ref, wd_ref, t_ref, gf_ref,
             dx_ref, hg_ref, hu_ref, n_ref, loss_ref, dgf_ref, acc_scr):
        i, k = pl.program_id(0), pl.program_id(1)

        @pl.when(k == 0)
        def _():
            xhat, _ = _rms(x_ref[...])
            n_ref[...] = (xhat * g_ref[...]).astype(BF16)
            acc_scr[...] = jnp.zeros_like(acc_scr)

        @pl.when((k == 0) & (i == 0))
        def _():
            loss_ref[...] = jnp.zeros_like(loss_ref)
            dgf_ref[...] = jnp.zeros_like(dgf_ref)

        n = n_ref[...]
        hg = _dot_nt(n, wg_ref[...])
        hu = _dot_nt(n, wu_ref[...])
        hg_ref[...] = hg.astype(BF16)
        hu_ref[...] = hu.astype(BF16)
        act = (hg * _sigmoid(hg) * hu).astype(BF16)
        acc_scr[...] += _dot(act, wd_ref[...])

        @pl.when(k == K - 1)
        def _():
            gfv = gf_ref[...]
            for r0 in range(0, tm, ROW_BLOCK):
                rows = slice(r0, r0 + ROW_BLOCK)
                xhat, inv = _rms(x_ref[rows, :] + FFN_RES * acc_scr[rows, :])
                err = xhat * gfv - t_ref[rows, :]
                loss_ref[...] += 0.5 * jnp.sum(jnp.sum(err * err, axis=1, keepdims=True), axis=0, keepdims=True) / D
                dx, dg = _rms_bwd(err * (1.0 / D), xhat, inv, gfv)
                dx_ref[rows, :] = dx
                dgf_ref[...] += dg

    w_spec = pl.BlockSpec((None, Fs, D), lambda i, k: (k, 0, 0))
    act_spec = pl.BlockSpec((None, tm, Fs), lambda i, k: (k, i, 0))
    row = pl.BlockSpec((tm, D), lambda i, k: (i, 0))
    vec = pl.BlockSpec((1, D), lambda i, k: (0, 0))
    return _pallas(
        body, name, (T // tm, K),
        [row, vec, w_spec, w_spec, w_spec, row, vec],
        [row, act_spec, act_spec, row, pl.BlockSpec((1, LANES), lambda i, k: (0, 0)), vec],
        [jax.ShapeDtypeStruct((T, D), F32), jax.ShapeDtypeStruct((K, T, Fs), BF16),
         jax.ShapeDtypeStruct((K, T, Fs), BF16), jax.ShapeDtypeStruct((T, D), BF16),
         jax.ShapeDtypeStruct((1, LANES), F32), jax.ShapeDtypeStruct((1, D), F32)],
        [pltpu.VMEM((tm, D), F32)],
        (x, g, wgt, wut, wd, target, gf))[0]


def _ffn_up_gather(name, x, g, wg_own, wu_own, order, tm, comm=None):
    T, D = x.shape
    Fs = wg_own.shape[0]
    nt = T // tm
    ci, co = (len(comm.inputs), len(comm.out_shape)) if comm is not None else (0, 0)

    def body(order_ref, x_ref, g_ref, wgo_ref, wuo_ref, *rest):
        cins, rest = rest[:ci], rest[ci:]
        (hg_ref, hu_ref, n_ref, sg_ref, su_ref), rest = rest[:5], rest[5:]
        couts, rest = rest[:co], rest[co:]
        (n_all, wbuf, send_sems, recv_sems, pass_send, pass_recv, load_sems), csems = rest[:7], rest[7:]
        k, i = pl.program_id(0), pl.program_id(1)
        x_pos, y_pos, c, others = _place()
        me = 2 * x_pos + y_pos
        owns, stacks = (wgo_ref, wuo_ref), (sg_ref, su_ref)
        mine, theirs = _col_halves(D, c)

        def chip_copy(a, j, chip):
            return pltpu.make_async_remote_copy(
                src_ref=owns[a].at[:, mine], dst_ref=stacks[a].at[chip, :, mine],
                send_sem=send_sems.at[3 * a + j], recv_sem=recv_sems.at[3 * a + j],
                device_id=(*others[j], c), device_id_type=MESH)

        def pass_copy(a, j, chip, half):
            return pltpu.make_async_remote_copy(
                src_ref=stacks[a].at[chip, :, half], dst_ref=stacks[a].at[chip, :, half],
                send_sem=pass_send.at[3 * a + j], recv_sem=pass_recv.at[3 * a + j],
                device_id=(x_pos, y_pos, 1 - c), device_id_type=MESH)

        @pl.when((k == 0) & (i == 0))
        def _():
            for a in range(2):
                for j in range(3):
                    chip_copy(a, j, me).start()
            if comm is not None:
                comm.start(cins, couts, csems)

        def bring(j):
            ox, oy = others[j]
            chip = 2 * ox + oy
            for a in range(2):
                chip_copy(a, j, chip).wait_recv()
            for a in range(2):
                pass_copy(a, j, chip, mine).start()
            for a in range(2):
                pass_copy(a, j, chip, theirs).wait_recv()
            loads = [pltpu.make_async_copy(stacks[a].at[chip], wbuf.at[j % 2, a], load_sems.at[2 * (j % 2) + a])
                     for a in range(2)]
            for cp in loads:
                cp.start()
            for cp in loads:
                cp.wait()

        @pl.when((k == 1) & (i == 0))
        def _():
            bring(0)
            bring(1)

        @pl.when((k == 2) & (i == nt - 1))
        def _():
            bring(2)

        rows = pl.ds(pl.multiple_of(i * tm, tm), tm)

        @pl.when(k == 0)
        def _():
            xhat, _ = _rms(x_ref[...])
            n = (xhat * g_ref[...]).astype(BF16)
            n_ref[...] = n
            n_all[rows, :] = n
            hg_ref[...] = _dot_nt(n, wgo_ref[...]).astype(BF16)
            hu_ref[...] = _dot_nt(n, wuo_ref[...]).astype(BF16)

        @pl.when(k > 0)
        def _():
            n = n_all[rows, :]
            slot = (k - 1) % 2
            hg_ref[...] = _dot_nt(n, wbuf[slot, 0]).astype(BF16)
            hu_ref[...] = _dot_nt(n, wbuf[slot, 1]).astype(BF16)

        @pl.when((k == N_CHIPS - 1) & (i == nt - 1))
        def _():
            for a in range(2):
                for j, (ox, oy) in enumerate(others):
                    chip_copy(a, j, me).wait_send()
                    pass_copy(a, j, 2 * ox + oy, mine).wait_send()
            if comm is not None:
                comm.finish(cins, couts, csems)

    any_spec = pl.BlockSpec(memory_space=pl.ANY)
    first_pass = lambda k, i, order_ref: (jnp.where(k == 0, i, nt - 1), 0)
    whole = pl.BlockSpec((Fs, D), lambda k, i, order_ref: (0, 0))
    act_spec = pl.BlockSpec((None, tm, Fs), lambda k, i, order_ref: (order_ref[k], i, 0))
    stack = jax.ShapeDtypeStruct((N_CHIPS, Fs, D), BF16)
    outs = pl.pallas_call(
        body, name=name,
        grid_spec=pltpu.PrefetchScalarGridSpec(
            num_scalar_prefetch=1, grid=(N_CHIPS, nt),
            in_specs=[pl.BlockSpec((tm, D), first_pass), pl.BlockSpec((1, D), lambda k, i, order_ref: (0, 0)),
                      whole, whole] + [any_spec] * ci,
            out_specs=[act_spec, act_spec, pl.BlockSpec((tm, D), first_pass), any_spec, any_spec] + [any_spec] * co,
            scratch_shapes=[pltpu.VMEM((T, D), BF16), pltpu.VMEM((2, 2, Fs, D), BF16)]
            + [pltpu.SemaphoreType.DMA((6,))] * 4 + [pltpu.SemaphoreType.DMA((4,))]
            + (comm.scratch if comm is not None else [])),
        out_shape=[jax.ShapeDtypeStruct((N_CHIPS, T, Fs), BF16), jax.ShapeDtypeStruct((N_CHIPS, T, Fs), BF16),
                   jax.ShapeDtypeStruct((T, D), BF16), stack, stack] + (comm.out_shape if comm is not None else []),
        compiler_params=_params(("arbitrary", "arbitrary")),
    )(order, x, g, wg_own, wu_own, *(comm.inputs if comm is not None else []))
    return list(outs[:5]), list(outs[5:])


def _ffn_down(name, x, hg, hu, wd, tm, comm=None):
    T, D = x.shape
    K, Fs, _ = wd.shape

    def body(x_ref, hg_ref, hu_ref, wd_ref, out_ref, acc_scr):
        k = pl.program_id(1)

        @pl.when(k == 0)
        def _():
            acc_scr[...] = jnp.zeros_like(acc_scr)

        hgv = hg_ref[...].astype(F32)
        act = (hgv * _sigmoid(hgv) * hu_ref[...].astype(F32)).astype(BF16)
        acc_scr[...] += _dot(act, wd_ref[...])

        @pl.when(k == K - 1)
        def _():
            out_ref[...] = x_ref[...] + FFN_RES * acc_scr[...]

    act_spec = pl.BlockSpec((None, tm, Fs), lambda i, k: (k, i, 0))
    row = pl.BlockSpec((tm, D), lambda i, k: (i, 0))
    return _pallas(
        body, name, (T // tm, K),
        [row, act_spec, act_spec, pl.BlockSpec((None, Fs, D), lambda i, k: (k, 0, 0))],
        [row], [jax.ShapeDtypeStruct((T, D), F32)], [pltpu.VMEM((tm, D), F32)],
        (x, hg, hu, wd), comm)


def _ffn_bwd_dx(name, dout, x, g, hg, hu, wgt, wut, wd, tm, comm=None):
    T, D = x.shape
    K, Fs, _ = wgt.shape

    def body(dout_ref, x_ref, g_ref, hg_ref, hu_ref, wg_ref, wu_ref, wd_ref,
             dx_ref, dhg_ref, dhu_ref, dg_ref, df_ref, dn_scr):
        i, k = pl.program_id(0), pl.program_id(1)

        @pl.when(k == 0)
        def _():
            df_ref[...] = (FFN_RES * dout_ref[...]).astype(BF16)
            dn_scr[...] = jnp.zeros_like(dn_scr)

        @pl.when((k == 0) & (i == 0))
        def _():
            dg_ref[...] = jnp.zeros_like(dg_ref)

        for r0 in range(0, tm, ROW_BLOCK):
            rows = slice(r0, r0 + ROW_BLOCK)
            dact = _dot_nt(df_ref[rows, :], wd_ref[...])
            hgv = hg_ref[rows, :].astype(F32)
            huv = hu_ref[rows, :].astype(F32)
            s = _sigmoid(hgv)
            dhu = (dact * (hgv * s)).astype(BF16)
            dhg = (dact * huv * (s * (1.0 + hgv * (1.0 - s)))).astype(BF16)
            dhg_ref[rows, :] = dhg
            dhu_ref[rows, :] = dhu
            dn_scr[rows, :] += _dot(dhg, wg_ref[...]) + _dot(dhu, wu_ref[...])

        @pl.when(k == K - 1)
        def _():
            xhat, inv = _rms(x_ref[...])
            dx, dg = _rms_bwd(dn_scr[...], xhat, inv, g_ref[...])
            dx_ref[...] = dout_ref[...] + dx
            dg_ref[...] += dg

    w_spec = pl.BlockSpec((None, Fs, D), lambda i, k: (k, 0, 0))
    act_spec = pl.BlockSpec((None, tm, Fs), lambda i, k: (k, i, 0))
    row = pl.BlockSpec((tm, D), lambda i, k: (i, 0))
    row_once = pl.BlockSpec((tm, D), lambda i, k: (i, 0), pipeline_mode=pl.Buffered(1))
    vec = pl.BlockSpec((1, D), lambda i, k: (0, 0))
    return _pallas(
        body, name, (T // tm, K),
        [row, row_once, vec, act_spec, act_spec, w_spec, w_spec, w_spec],
        [row_once, act_spec, act_spec, vec, row],
        [jax.ShapeDtypeStruct((T, D), F32), jax.ShapeDtypeStruct((K, T, Fs), BF16),
         jax.ShapeDtypeStruct((K, T, Fs), BF16), jax.ShapeDtypeStruct((1, D), F32),
         jax.ShapeDtypeStruct((T, D), BF16)],
        [pltpu.VMEM((tm, D), F32)],
        (dout, x, g, hg, hu, wgt, wut, wd), comm)


def _ffn_bwd_dw(name, n, df, hg, hu, dhg, dhu, tk, comm=None):
    T, D = n.shape
    K, _, Fs = hg.shape
    nt = T // tk

    def body(n_ref, df_ref, hg_ref, hu_ref, dhg_ref, dhu_ref, dwg_ref, dwu_ref, dwd_ref, accg, accu, accd):
        t = pl.program_id(1)

        @pl.when(t == 0)
        def _():
            accg[...] = jnp.zeros_like(accg)
            accu[...] = jnp.zeros_like(accu)
            accd[...] = jnp.zeros_like(accd)

        nv = n_ref[...]
        hgv = hg_ref[...].astype(F32)
        act = (hgv * _sigmoid(hgv) * hu_ref[...].astype(F32)).astype(BF16)
        accg[...] += _dot_tn(dhg_ref[...], nv)
        accu[...] += _dot_tn(dhu_ref[...], nv)
        accd[...] += _dot_tn(act, df_ref[...])

        @pl.when(t == nt - 1)
        def _():
            dwg_ref[...] = accg[...].astype(BF16)
            dwu_ref[...] = accu[...].astype(BF16)
            dwd_ref[...] = accd[...].astype(BF16)

    act_spec = pl.BlockSpec((None, tk, Fs), lambda k, t: (k, t, 0))
    w_spec = pl.BlockSpec((None, Fs, D), lambda k, t: (k, 0, 0))
    row = pl.BlockSpec((tk, D), lambda k, t: (t, 0))
    return _pallas(
        body, name, (K, nt),
        [row, row, act_spec, act_spec, act_spec, act_spec],
        [w_spec, w_spec, w_spec],
        [jax.ShapeDtypeStruct((K, Fs, D), BF16)] * 3,
        [pltpu.VMEM((Fs, D), F32)] * 3,
        (n, df, hg, hu, dhg, dhu), comm)


def _mix_proj_fwd(x, g, wproj_t, wf_t, tm, tn, comm=None):
    T, D = x.shape
    N = wproj_t.shape[0]

    def body(x_ref, g_ref, w_ref, wf_ref, h_ref, proj_ref, flog_ref, h_scr):
        @pl.when(pl.program_id(1) == 0)
        def _():
            xhat, _ = _rms(x_ref[...])
            h = (xhat * g_ref[...]).astype(BF16)
            h_scr[...] = h
            h_ref[...] = h
            flog_ref[...] = _dot_nt(h, wf_ref[...])

        proj_ref[...] = _dot_nt(h_scr[...], w_ref[...]).astype(BF16)

    return _pallas(
        body, "mix_proj_fwd", (T // tm, N // tn),
        [pl.BlockSpec((tm, D), lambda i, n: (i, 0)), pl.BlockSpec((1, D), lambda i, n: (0, 0)),
         pl.BlockSpec((tn, D), lambda i, n: (n, 0)), pl.BlockSpec((LANES, D), lambda i, n: (0, 0))],
        [pl.BlockSpec((tm, D), lambda i, n: (i, 0)), pl.BlockSpec((tm, tn), lambda i, n: (i, n)),
         pl.BlockSpec((tm, LANES), lambda i, n: (i, 0))],
        [jax.ShapeDtypeStruct((T, D), BF16), jax.ShapeDtypeStruct((T, N), BF16),
         jax.ShapeDtypeStruct((T, LANES), F32)],
        [pltpu.VMEM((tm, D), BF16)],
        (x, g, wproj_t, wf_t), comm)


def _log_sigmoid(z):
    return -(jnp.maximum(-z, 0.0) + jnp.log(1.0 + jnp.exp(-jnp.abs(z))))


def _tri(n, lower):
    r = lax.broadcasted_iota(jnp.int32, (n, n), 0)
    c = lax.broadcasted_iota(jnp.int32, (n, n), 1)
    return jnp.where((r >= c) if lower else (r <= c), 1.0, 0.0).astype(F32)


def _dot_f32(a, b):
    return lax.dot_general(a, b, (((1,), (0,)), ((), ())), preferred_element_type=F32,
                           precision=lax.Precision.HIGHEST)


def _fgate_fwd(flog, bias, B, S, ch):
    def body(flog_ref, b_ref, cum_ref):
        tri = _tri(ch, True)
        carry = jnp.zeros((1, LANES), F32)
        for c0 in range(0, S, ch):
            lf = _log_sigmoid(flog_ref[c0:c0 + ch, :] + b_ref[...])
            cs = _dot_f32(tri, lf) + carry
            cum_ref[c0:c0 + ch, :] = cs
            carry = cs[ch - 1:ch, :]

    return pl.pallas_call(
        body, name="fgate_fwd", grid=(B,),
        in_specs=[pl.BlockSpec((S, LANES), lambda b: (b, 0)),
                  pl.BlockSpec((1, LANES), lambda b: (0, 0))],
        out_specs=pl.BlockSpec((S, LANES), lambda b: (b, 0)),
        out_shape=jax.ShapeDtypeStruct((B * S, LANES), F32),
        compiler_params=_params(("arbitrary",)),
    )(flog, bias)


def _fgate_bwd(dcum, flog, bias, B, S, ch):
    def body(dcum_ref, flog_ref, b_ref, dflog_ref, db_ref):
        @pl.when(pl.program_id(0) == 0)
        def _():
            db_ref[...] = jnp.zeros_like(db_ref)

        tri = _tri(ch, False)
        carry = jnp.zeros((1, LANES), F32)
        db = jnp.zeros((1, LANES), F32)
        for c0 in range(S - ch, -1, -ch):
            dlf = _dot_f32(tri, dcum_ref[c0:c0 + ch, :]) + carry
            carry = dlf[0:1, :]
            z = flog_ref[c0:c0 + ch, :] + b_ref[...]
            dz = dlf * _sigmoid(-z)
            dflog_ref[c0:c0 + ch, :] = dz
            db = db + jnp.sum(dz, axis=0, keepdims=True)
        db_ref[...] += db

    return pl.pallas_call(
        body, name="fgate_bwd", grid=(B,),
        in_specs=[pl.BlockSpec((S, LANES), lambda b: (b, 0)),
                  pl.BlockSpec((S, LANES), lambda b: (b, 0)),
                  pl.BlockSpec((1, LANES), lambda b: (0, 0))],
        out_specs=[pl.BlockSpec((S, LANES), lambda b: (b, 0)),
                   pl.BlockSpec((1, LANES), lambda b: (0, 0))],
        out_shape=[jax.ShapeDtypeStruct((B * S, LANES), F32),
                   jax.ShapeDtypeStruct((1, LANES), F32)],
        compiler_params=_params(("arbitrary",)),
    )(dcum, flog, bias)


def _pick_lane(tile, h):
    lane = lax.broadcasted_iota(jnp.int32, tile.shape, 1)
    return jnp.sum(jnp.where(lane == h, tile, 0.0), axis=1, keepdims=True)


def _put_lane(col, h, width=LANES):
    lane = lax.broadcasted_iota(jnp.int32, (col.shape[0], width), 1)
    return jnp.where(lane == h, col, 0.0)


def _pick_row(tile, h):
    row = lax.broadcasted_iota(jnp.int32, tile.shape, 0)
    return jnp.sum(jnp.where(row == h, tile, 0.0), axis=0, keepdims=True)


def _put_row(vec, h):
    row = lax.broadcasted_iota(jnp.int32, (8, vec.shape[1]), 0)
    return jnp.where(row == h, vec, 0.0)


def _causal(tq):
    r = lax.broadcasted_iota(jnp.int32, (tq, tq), 0)
    c = lax.broadcasted_iota(jnp.int32, (tq, tq), 1)
    return r >= c


def _head_halves(t):
    lo = lax.broadcasted_iota(jnp.int32, t.shape, 1) < HEAD_DIM
    zero = jnp.zeros_like(t)
    return jnp.where(lo, t, zero), jnp.where(lo, zero, t)


NEG = -1e30
ATTN_SCALE = 1.0 / math.sqrt(HEAD_DIM)


def _scaled(q):
    return (q.astype(F32) * ATTN_SCALE).astype(q.dtype)


def _attn_fwd(proj, cum, cum_t, B, S, tq, comm=None):
    nq = S // tq

    def body(q_ref, k_ref, v_ref, cum_ref, cumt_ref, o_ref, lse_ref):
        qi, hp = pl.program_id(1), pl.program_id(2)
        qm = _head_halves(_scaled(q_ref[...]))
        first_head = lax.broadcasted_iota(jnp.int32, (LANES, tq), 0) < HEAD_DIM
        r = lax.broadcasted_iota(jnp.int32, (tq, tq), 0)
        c = lax.broadcasted_iota(jnp.int32, (tq, tq), 1)

        def tile(j, carry, masked):
            (ma, la), (mb, lb), acc = carry
            off = pl.multiple_of(j * tq, tq)
            kj = k_ref[pl.ds(off, tq), :]
            vm = _head_halves(v_ref[pl.ds(off, tq), :])
            cumk = cum_ref[pl.ds(off, tq), :]
            new, alphas, pv = [], [], jnp.zeros((LANES, tq), F32)
            for e, (m, l) in enumerate(((ma, la), (mb, lb))):
                s = _dot_nt(kj, qm[e]) - _pick_lane(cumk, 2 * hp + e)
                if masked:
                    s = jnp.where(r <= c, s, NEG)
                m_new = jnp.maximum(m, jnp.max(s, axis=0, keepdims=True))
                p = jnp.exp(s - m_new)
                alpha = jnp.exp(m - m_new)
                new.append((m_new, alpha * l + jnp.sum(p, axis=0, keepdims=True)))
                alphas.append(alpha)
                pv = pv + _dot_tn(vm[e], p.astype(BF16))
            acc = jnp.where(first_head, alphas[0], alphas[1]) * acc + pv
            return new[0], new[1], acc

        one = (jnp.full((1, tq), NEG, F32), jnp.zeros((1, tq), F32))
        carry = lax.fori_loop(0, qi, lambda j, cr: tile(j, cr, False), (one, one, jnp.zeros((LANES, tq), F32)))
        (ma, la), (mb, lb), acc = tile(qi, carry, True)
        o_ref[...] = (acc / jnp.where(first_head, la, lb)).T.astype(BF16)

        @pl.when(hp == 0)
        def _():
            lse_ref[...] = jnp.zeros_like(lse_ref)

        ct = cumt_ref[...]
        lse_ref[...] += (_put_row(ma + jnp.log(la) + _pick_row(ct, 2 * hp), 2 * hp)
                         + _put_row(mb + jnp.log(lb) + _pick_row(ct, 2 * hp + 1), 2 * hp + 1))

    kv = lambda first: pl.BlockSpec((S, LANES), lambda b, i, hp: (b, first + hp))
    row_block = pl.BlockSpec((None, None, 8, tq), lambda b, i, hp: (b, i, 0, 0))
    return _pallas(
        body, "attn_fwd", (B, nq, HEAD_PAIRS),
        [pl.BlockSpec((tq, LANES), lambda b, i, hp: (b * nq + i, hp)),
         kv(ATTN_W // LANES), kv(2 * ATTN_W // LANES),
         pl.BlockSpec((S, LANES), lambda b, i, hp: (b, 0)), row_block],
        [pl.BlockSpec((tq, LANES), lambda b, i, hp: (b * nq + i, hp)), row_block],
        [jax.ShapeDtypeStruct((B * S, ATTN_W), BF16), jax.ShapeDtypeStruct((B, nq, 8, tq), F32)],
        [], (proj, proj, proj, cum, cum_t), comm)


def _attn_bwd(proj, o, do, lse, cum, cum_t, B, S, tq, comm=None):
    nq = S // tq

    def body(q_ref, k_ref, v_ref, o_ref, do_ref, lse_ref, cum_ref, cumt_ref,
             dq_ref, dk_ref, dv_ref, dcq_ref, dck_ref, dq_scr):
        hp, kj = pl.program_id(1), pl.program_id(2)

        @pl.when(kj == 0)
        def _():
            dq_scr[...] = jnp.zeros_like(dq_scr)

        @pl.when((kj == 0) & (hp == 0))
        def _():
            dcq_ref[...] = jnp.zeros_like(dcq_ref)
            dck_ref[...] = jnp.zeros_like(dck_ref)

        kv = k_ref[...]
        vv = v_ref[...]
        km = _head_halves(kv)
        ct = cumt_ref[...]
        ck = [_pick_row(ct, 2 * hp + e) for e in range(2)]

        def tile(i, carry, masked):
            dk, dv, dcol = carry
            off = pl.multiple_of(i * tq, tq)
            qi = q_ref[pl.ds(off, tq), :]
            ov = o_ref[pl.ds(off, tq), :].astype(F32)
            qm = _head_halves(_scaled(qi))
            dom = _head_halves(do_ref[pl.ds(off, tq), :])
            cumv = cum_ref[pl.ds(off, tq), :]
            lsev = lse_ref[pl.ds(off, tq), :]
            dcq = jnp.zeros((tq, LANES), F32)
            dq = jnp.zeros((tq, LANES), F32)
            dcol_new = []
            for e in range(2):
                delta = jnp.sum(dom[e].astype(F32) * ov, axis=1, keepdims=True)
                row_term = _pick_lane(cumv, 2 * hp + e) - _pick_lane(lsev, 2 * hp + e)
                p = jnp.exp(_dot_nt(qm[e], kv) + row_term - ck[e])
                if masked:
                    p = jnp.where(_causal(tq), p, 0.0)
                dv = dv + _dot_tn(dom[e], p.astype(BF16))
                ds = p * (_dot_nt(dom[e], vv) - delta)
                dcol_new.append(dcol[e] + jnp.sum(ds, axis=0, keepdims=True))
                dcq = dcq + _put_lane(jnp.sum(ds, axis=1, keepdims=True), 2 * hp + e)
                dsb = ds.astype(BF16)
                dk = dk + _dot_tn(qm[e], dsb)
                dq = dq + _dot(dsb, km[e]) * ATTN_SCALE
            dq_scr[pl.ds(off, tq), :] += dq
            dcq_ref[pl.ds(off, tq), :] += dcq
            return dk, dv, tuple(dcol_new)

        zero_row = jnp.zeros((1, tq), F32)
        init = (jnp.zeros((LANES, tq), F32), jnp.zeros((LANES, tq), F32), (zero_row, zero_row))
        carry = tile(kj, init, True)
        dk, dv, dcol = lax.fori_loop(kj + 1, nq, lambda i, c: tile(i, c, False), carry)
        dk_ref[...] = dk.T.astype(BF16)
        dv_ref[...] = dv.T.astype(BF16)
        dck_ref[kj] += -(_put_row(dcol[0], 2 * hp) + _put_row(dcol[1], 2 * hp + 1))

        @pl.when(kj == nq - 1)
        def _():
            dq_ref[...] = dq_scr[...].astype(BF16)

    seq = lambda first: pl.BlockSpec((S, LANES), lambda b, hp, j: (b, first + hp))
    tile_in = lambda first: pl.BlockSpec((tq, LANES), lambda b, hp, j: (b * nq + j, first + hp))
    lanes0 = pl.BlockSpec((S, LANES), lambda b, hp, j: (b, 0))
    out = jax.ShapeDtypeStruct((B * S, ATTN_W), BF16)
    return _pallas(
        body, "attn_bwd", (B, HEAD_PAIRS, nq),
        [seq(0), tile_in(ATTN_W // LANES), tile_in(2 * ATTN_W // LANES), seq(0), seq(0), lanes0, lanes0,
         pl.BlockSpec((None, None, 8, tq), lambda b, hp, j: (b, j, 0, 0))],
        [seq(0), tile_in(0), tile_in(0), lanes0,
         pl.BlockSpec((None, nq, 8, tq), lambda b, hp, j: (b, 0, 0, 0))],
        [out, out, out, jax.ShapeDtypeStruct((B * S, LANES), F32), jax.ShapeDtypeStruct((B, nq, 8, tq), F32)],
        [pltpu.VMEM((S, LANES), F32)],
        (proj, proj, proj, o, do, lse, cum, cum_t), comm)


def _shift_down(u, n):
    row = lax.broadcasted_iota(jnp.int32, u.shape, 0)
    return jnp.where(row >= n, pltpu.roll(u, n, 0), 0.0)


def _shift_up(u, n):
    rows = u.shape[0]
    row = lax.broadcasted_iota(jnp.int32, u.shape, 0)
    return jnp.where(row < rows - n, pltpu.roll(u, rows - n, 0), 0.0)


def _conv_specs(S):
    cb = pl.BlockSpec((S, LANES), lambda g, b: (b, COL_CB // LANES + g))
    cc = pl.BlockSpec((S, LANES), lambda g, b: (b, COL_CC // LANES + g))
    cx = pl.BlockSpec((S, LANES), lambda g, b: (b, COL_CX // LANES + g))
    w = pl.BlockSpec((8, LANES), lambda g, b: (0, g))
    return cb, cc, cx, w


def _conv_fwd(proj, conv_w, B, S):
    def body(cb_ref, cc_ref, cx_ref, w_ref, y_ref):
        u = cc_ref[...].astype(F32) * cx_ref[...].astype(F32)
        w = w_ref[...]
        conv = w[0:1, :] * _shift_down(u, 2) + w[1:2, :] * _shift_down(u, 1) + w[2:3, :] * u
        y_ref[...] = (cb_ref[...].astype(F32) * conv).astype(BF16)

    cb, cc, cx, w = _conv_specs(S)
    return pl.pallas_call(
        body, name="conv_fwd", grid=(CONV_W // LANES, B),
        in_specs=[cb, cc, cx, w],
        out_specs=pl.BlockSpec((S, LANES), lambda g, b: (b, g)),
        out_shape=jax.ShapeDtypeStruct((B * S, CONV_W), BF16),
        compiler_params=_params(("arbitrary", "arbitrary")),
    )(proj, proj, proj, conv_w)


def _conv_bwd(dy, proj, conv_w, B, S):
    def body(dy_ref, cb_ref, cc_ref, cx_ref, w_ref, dcb_ref, dcc_ref, dcx_ref, dw_ref):
        @pl.when(pl.program_id(1) == 0)
        def _():
            dw_ref[...] = jnp.zeros_like(dw_ref)

        ccv = cc_ref[...].astype(F32)
        cxv = cx_ref[...].astype(F32)
        u = ccv * cxv
        u1 = _shift_down(u, 1)
        u2 = _shift_down(u, 2)
        w = w_ref[...]
        conv = w[0:1, :] * u2 + w[1:2, :] * u1 + w[2:3, :] * u
        dyv = dy_ref[...].astype(F32)
        dcb_ref[...] = (dyv * conv).astype(BF16)
        dconv = dyv * cb_ref[...].astype(F32)
        du = w[2:3, :] * dconv + w[1:2, :] * _shift_up(dconv, 1) + w[0:1, :] * _shift_up(dconv, 2)
        dcc_ref[...] = (du * cxv).astype(BF16)
        dcx_ref[...] = (du * ccv).astype(BF16)
        row = lax.broadcasted_iota(jnp.int32, (8, LANES), 0)
        dw = jnp.where(row == 0, jnp.sum(dconv * u2, axis=0, keepdims=True),
                       jnp.where(row == 1, jnp.sum(dconv * u1, axis=0, keepdims=True),
                                 jnp.where(row == 2, jnp.sum(dconv * u, axis=0, keepdims=True), 0.0)))
        dw_ref[...] += dw

    cb, cc, cx, w = _conv_specs(S)
    out = pl.BlockSpec((S, LANES), lambda g, b: (b, g))
    return pl.pallas_call(
        body, name="conv_bwd", grid=(CONV_W // LANES, B),
        in_specs=[out, cb, cc, cx, w],
        out_specs=[out, out, out, w],
        out_shape=[jax.ShapeDtypeStruct((B * S, CONV_W), BF16)] * 3 + [jax.ShapeDtypeStruct((8, CONV_W), F32)],
        compiler_params=_params(("arbitrary", "arbitrary")),
    )(dy, proj, proj, proj, conv_w)


def _gate_specs(tm, D):
    ga = pl.BlockSpec((tm, D), lambda i: (i, COL_GATES // D))
    gc = pl.BlockSpec((tm, D), lambda i: (i, COL_GATES // D + 1))
    return ga, gc


def _mix_out_fwd(x, o, yc, proj, woa, woc, wout, tm):
    T, D = x.shape

    def body(x_ref, o_ref, yc_ref, ga_ref, gc_ref, woa_ref, woc_ref, wout_ref, out_ref):
        ya = _dot(o_ref[...], woa_ref[...])
        yp = _dot(yc_ref[...], woc_ref[...])
        merged = _sigmoid(ga_ref[...].astype(F32)) * ya + _sigmoid(gc_ref[...].astype(F32)) * yp
        out_ref[...] = x_ref[...] + _dot(merged.astype(BF16), wout_ref[...])

    ga, gc = _gate_specs(tm, D)
    row = lambda w: pl.BlockSpec((tm, w), lambda i: (i, 0))
    whole = lambda a: pl.BlockSpec(a.shape, lambda i: (0, 0))
    return pl.pallas_call(
        body, name="mix_out_fwd", grid=(T // tm,),
        in_specs=[row(D), row(ATTN_W), row(CONV_W), ga, gc, whole(woa), whole(woc), whole(wout)],
        out_specs=row(D),
        out_shape=jax.ShapeDtypeStruct((T, D), F32),
        compiler_params=_params(("arbitrary",)),
    )(x, o, yc, proj, proj, woa, woc, wout)


def _mix_out_bwd(dx, o, yc, proj, woa, woc, wout, tm, comm=None):
    T, D = dx.shape
    nt = T // tm

    def body(dx_ref, o_ref, yc_ref, ga_ref, gc_ref, woa_ref, woc_ref, wout_ref,
             do_ref, dyc_ref, dg_ref, dwoa_ref, dwoc_ref, dwout_ref, acca, accc, acco):
        t = pl.program_id(0)

        @pl.when(t == 0)
        def _():
            acca[...] = jnp.zeros_like(acca)
            accc[...] = jnp.zeros_like(accc)
            acco[...] = jnp.zeros_like(acco)

        dxb = dx_ref[...].astype(BF16)
        ov, ycv = o_ref[...], yc_ref[...]
        ya = _dot(ov, woa_ref[...])
        yp = _dot(ycv, woc_ref[...])
        sa = _sigmoid(ga_ref[...].astype(F32))
        sc = _sigmoid(gc_ref[...].astype(F32))
        merged = (sa * ya + sc * yp).astype(BF16)
        dm = _dot_nt(dxb, wout_ref[...])
        dya = (dm * sa).astype(BF16)
        dyp = (dm * sc).astype(BF16)
        dg_ref[:, :D] = (dm * ya * sa * (1.0 - sa)).astype(BF16)
        dg_ref[:, D:] = (dm * yp * sc * (1.0 - sc)).astype(BF16)
        do_ref[...] = _dot_nt(dya, woa_ref[...]).astype(BF16)
        dyc_ref[...] = _dot_nt(dyp, woc_ref[...]).astype(BF16)
        acca[...] += _dot_tn(ov, dya)
        accc[...] += _dot_tn(ycv, dyp)
        acco[...] += _dot_tn(merged, dxb)

        @pl.when(t == nt - 1)
        def _():
            dwoa_ref[...] = acca[...].astype(BF16)
            dwoc_ref[...] = accc[...].astype(BF16)
            dwout_ref[...] = acco[...].astype(BF16)

    ga, gc = _gate_specs(tm, D)
    row = lambda w: pl.BlockSpec((tm, w), lambda i: (i, 0))
    whole = lambda a: pl.BlockSpec(a.shape, lambda i: (0, 0))
    return _pallas(
        body, "mix_out_bwd", (nt,),
        [row(D), row(ATTN_W), row(CONV_W), ga, gc, whole(woa), whole(woc), whole(wout)],
        [row(ATTN_W), row(CONV_W), row(2 * D), whole(woa), whole(woc), whole(wout)],
        [jax.ShapeDtypeStruct((T, ATTN_W), BF16), jax.ShapeDtypeStruct((T, CONV_W), BF16),
         jax.ShapeDtypeStruct((T, 2 * D), BF16),
         jax.ShapeDtypeStruct(woa.shape, BF16), jax.ShapeDtypeStruct(woc.shape, BF16),
         jax.ShapeDtypeStruct(wout.shape, BF16)],
        [pltpu.VMEM(woa.shape, F32), pltpu.VMEM(woc.shape, F32), pltpu.VMEM(wout.shape, F32)],
        (dx, o, yc, proj, proj, woa, woc, wout), comm)


def _proj_pieces(dq, dk, dv, dcb, dcc, dcx, dgates, dflog):
    D = dgates.shape[1] // 2
    return [(dq, ATTN_W, 0), (dk, ATTN_W, 0), (dv, ATTN_W, 0), (dcb, CONV_W, 0), (dcc, CONV_W, 0), (dcx, CONV_W, 0),
            (dgates, D, 0), (dgates, D, 1), (dflog, LANES, 0)]


def _mix_proj_bwd_dx(dres, x, g, pieces, wproj_t, wf_t, tm, comm=None):
    T, D = x.shape
    n = len(pieces)
    w_blocks = [(ATTN_W, 0), (ATTN_W, 1), (ATTN_W, 2), (CONV_W, 3), (CONV_W, 4), (CONV_W, 5),
                (D, COL_GATES // D), (D, COL_GATES // D + 1)]

    def body(*refs):
        dres_ref, x_ref, g_ref = refs[:3]
        p_refs, w_refs = refs[3:3 + n], refs[3 + n:3 + 2 * n]
        dx_ref, dg_ref = refs[3 + 2 * n:]

        @pl.when(pl.program_id(0) == 0)
        def _():
            dg_ref[...] = jnp.zeros_like(dg_ref)

        dh = _dot(p_refs[0][...].astype(BF16), w_refs[0][...])
        for p_ref, w_ref in zip(p_refs[1:], w_refs[1:]):
            dh = dh + _dot(p_ref[...].astype(BF16), w_ref[...])
        xhat, inv = _rms(x_ref[...])
        dx, dg = _rms_bwd(dh, xhat, inv, g_ref[...])
        dx_ref[...] = dres_ref[...] + dx
        dg_ref[...] += dg

    row = pl.BlockSpec((tm, D), lambda i: (i, 0))
    vec = pl.BlockSpec((1, D), lambda i: (0, 0))
    p_specs = [pl.BlockSpec((tm, w), lambda i, cb=cb: (i, cb)) for _, w, cb in pieces]
    w_specs = [pl.BlockSpec((r, D), lambda i, rb=rb: (rb, 0)) for r, rb in w_blocks]
    w_specs.append(pl.BlockSpec((LANES, D), lambda i: (0, 0)))
    return _pallas(
        body, "mix_proj_bwd_dx", (T // tm,),
        [row, row, vec] + p_specs + w_specs, [row, vec],
        [jax.ShapeDtypeStruct((T, D), F32), jax.ShapeDtypeStruct((1, D), F32)], [],
        (dres, x, g, *[p for p, _, _ in pieces], *([wproj_t] * len(w_blocks)), wf_t), comm)


def _matmuls_tn(name, pieces, b, tk):
    T, N = b.shape
    nt = T // tk
    n = len(pieces)

    def body(*refs):
        a_refs, b_ref, out_refs, accs = refs[:n], refs[n], refs[n + 1:2 * n + 1], refs[2 * n + 1:]
        t = pl.program_id(0)

        @pl.when(t == 0)
        def _():
            for acc in accs:
                acc[...] = jnp.zeros_like(acc)

        bv = b_ref[...]
        for a_ref, acc in zip(a_refs, accs):
            acc[...] += _dot_tn(a_ref[...].astype(BF16), bv)

        @pl.when(t == nt - 1)
        def _():
            for out_ref, acc in zip(out_refs, accs):
                out_ref[...] = acc[...].astype(BF16)

    return pl.pallas_call(
        body, name=name, grid=(nt,),
        in_specs=[pl.BlockSpec((tk, w), lambda t, cb=cb: (t, cb)) for _, w, cb in pieces]
        + [pl.BlockSpec((tk, N), lambda t: (t, 0))],
        out_specs=[pl.BlockSpec((w, N), lambda t: (0, 0)) for _, w, _ in pieces],
        out_shape=[jax.ShapeDtypeStruct((w, N), BF16) for _, w, _ in pieces],
        scratch_shapes=[pltpu.VMEM((w, N), F32) for _, w, _ in pieces],
        compiler_params=_params(("arbitrary",)),
    )(*[a for a, _, _ in pieces], b)


TOKEN_TILE = 512
TOKEN_TILE_WIDE = 1024
ATTN_TILE = 512
SCAN_CHUNK = 256
PROJ_DX_TILE = 256


def _local_step(x, target, plan, B, S):
    T, D = x.shape
    tm = min(TOKEN_TILE, T)
    tm_fwd = min(TOKEN_TILE_WIDE, T)
    tq = min(ATTN_TILE, S)
    nq = S // tq
    ch = min(SCAN_CHUNK, S)

    def riding(kernel_name, build):
        results, brought = build(plan.rider(kernel_name))
        plan.arrived(kernel_name, brought)
        return results

    hg1, hu1, n1 = plan.ffn1_up(x, tm_fwd)
    w1 = plan.weights("ffn1")
    x1, = riding("ffn1_down", lambda comm: _ffn_down("ffn1_down", x, hg1, hu1, w1["ffn1_down"], tm_fwd, comm))
    wm = plan.weights("mix_in")
    h, proj, flog = riding("mix_proj_fwd", lambda comm: _mix_proj_fwd(
        x1, wm["mix_norm"], wm["w_proj"], wm["w_f"], tm_fwd, PROJ_W // 4, comm))
    wm.update(plan.weights("mix_out"))
    cum = _fgate_fwd(flog, wm["b_forget"], B, S, ch)
    cum_t = jnp.transpose(cum[:, :N_HEADS].reshape(B, nq, tq, N_HEADS), (0, 1, 3, 2))
    o, lse_t = riding("attn_fwd", lambda comm: _attn_fwd(proj, cum, cum_t, B, S, tq, comm))
    lse = jnp.pad(jnp.transpose(lse_t, (0, 1, 3, 2)).reshape(T, N_HEADS), ((0, 0), (0, LANES - N_HEADS)))
    yc = _conv_fwd(proj, wm["conv_w"], B, S)
    x2 = _mix_out_fwd(x1, o, yc, proj, wm["w_o_attn"], wm["w_o_conv"], wm["w_out"], tm)
    w2 = plan.weights("ffn2")
    dx3, hg2, hu2, n2, loss, d_final_norm = _ffn_fwd_loss(
        "ffn2_fwd_loss", x2, w2["ffn2_norm"], w2["ffn2_gate"], w2["ffn2_up"], w2["ffn2_down"], target, w2["final_norm"],
        tm_fwd)

    g = {"final_norm": d_final_norm}
    dx2, dhg2, dhu2, g["ffn2_norm"], df2 = _ffn_bwd_dx("ffn2_bwd_dx", dx3, x2, w2["ffn2_norm"], hg2, hu2,
                                                  w2["ffn2_gate"], w2["ffn2_up"], w2["ffn2_down"], tm_fwd)[0]
    plan.reduce("ffn2", dict(zip(("ffn2_gate", "ffn2_up", "ffn2_down"),
                                 _ffn_bwd_dw("ffn2_bwd_dw", n2, df2, hg2, hu2, dhg2, dhu2, tm_fwd)[0])))
    do, dyc, dgates, dwoa, dwoc, dwout = riding("mix_out_bwd", lambda comm: _mix_out_bwd(
        dx2, o, yc, proj, wm["w_o_attn"], wm["w_o_conv"], wm["w_out"], tm, comm))
    plan.reduce("out", dict(w_o_attn=_shard_cols(dwoa), w_o_conv=_shard_cols(dwoc), w_out=dwout.reshape(N_CHIPS, -1, D)))
    dq, dk, dv, dcq, dck = riding("attn_bwd", lambda comm: _attn_bwd(proj, o, do, lse, cum, cum_t, B, S, tq, comm))
    dcum = dcq + jnp.pad(jnp.transpose(dck, (0, 1, 3, 2)).reshape(T, N_HEADS), ((0, 0), (0, LANES - N_HEADS)))
    dflog, g["b_forget"] = _fgate_bwd(dcum, flog, wm["b_forget"], B, S, ch)
    dcb, dcc, dcx, g["conv_w"] = _conv_bwd(dyc, proj, wm["conv_w"], B, S)
    pieces = _proj_pieces(dq, dk, dv, dcb, dcc, dcx, dgates, dflog)
    dwq, dwk, dwv, dwcb, dwcc, dwcx = _matmuls_tn("mix_dw_a", pieces[:6], h, tm)
    dwga, dwgc, dwf = _matmuls_tn("mix_dw_b", pieces[6:], h, tm)
    dwin_t = jnp.concatenate([dwq, dwk, dwv, dwf[:N_HEADS], dwcb, dwcc, dwcx, dwga, dwgc], axis=0)
    plan.reduce("w_in", {"w_in": dwin_t.reshape(N_CHIPS, -1, D)})
    dx1, g["mix_norm"] = riding("mix_proj_bwd_dx", lambda comm: _mix_proj_bwd_dx(
        dx2, x1, wm["mix_norm"], pieces, wm["w_proj"], wm["w_f"], min(PROJ_DX_TILE, T), comm))
    grad_x, dhg1, dhu1, g["ffn1_norm"], df1 = _ffn_bwd_dx(
        "ffn1_bwd_dx", dx1, x, w1["ffn1_norm"], hg1, hu1, w1["ffn1_gate"], w1["ffn1_up"], w1["ffn1_down"], tm_fwd)[0]
    plan.reduce_small(g, loss)
    plan.reduce("ffn1", dict(zip(("ffn1_gate", "ffn1_up", "ffn1_down"), riding("ffn1_bwd_dw", lambda comm: _ffn_bwd_dw(
        "ffn1_bwd_dw", n1, df1, hg1, hu1, dhg1, dhu1, tm_fwd, comm)))))
    return loss, grad_x, g


TRANSPOSED = ("ffn1_gate", "ffn1_up", "ffn2_gate", "ffn2_up", "w_in")
NORMS = ("ffn1_norm", "mix_norm", "ffn2_norm", "final_norm")


def _unshard_cols(a):
    return jnp.transpose(a, (1, 0, 2)).reshape(a.shape[1], N_CHIPS * a.shape[2])


def _shard_cols(a):
    return jnp.transpose(a.reshape(a.shape[0], N_CHIPS, a.shape[1] // N_CHIPS), (1, 0, 2))


def _layout_ffn(which):
    def layout(st, small):
        w = {n: st[n] for n in (which + "_gate", which + "_up", which + "_down")}
        w[which + "_norm"] = small[which + "_norm"].reshape(1, -1)
        if which == "ffn2":
            w["final_norm"] = small["final_norm"].reshape(1, -1)
        return w
    return layout


def _layout_mix_in(st, small):
    win_t = st["w_in"].reshape(-1, st["w_in"].shape[2])
    return {
        "w_proj": jnp.concatenate([win_t[:N_FORGET_COL], win_t[N_FORGET_COL + N_HEADS:]], axis=0),
        "w_f": jnp.pad(win_t[N_FORGET_COL:N_FORGET_COL + N_HEADS], ((0, LANES - N_HEADS), (0, 0))),
        "conv_w": _unshard_cols(st["conv_w"]),
        "mix_norm": small["mix_norm"].reshape(1, -1),
        "b_forget": jnp.pad(small["b_forget"].reshape(1, -1), ((0, 0), (0, LANES - N_HEADS))),
    }


def _layout_mix_out(st, small):
    return {"w_o_attn": _unshard_cols(st["w_o_attn"]), "w_o_conv": _unshard_cols(st["w_o_conv"]),
            "w_out": st["w_out"].reshape(-1, st["w_out"].shape[2])}


_LAYOUTS = {"ffn1": _layout_ffn("ffn1"), "mix_in": _layout_mix_in, "mix_out": _layout_mix_out, "ffn2": _layout_ffn("ffn2")}


ANY = pl.BlockSpec(memory_space=pl.ANY)
BIG = ("ffn1_gate", "ffn1_up", "ffn1_down", "w_in", "w_o_attn", "w_o_conv", "w_out",
       "ffn2_gate", "ffn2_up", "ffn2_down")


def _place():
    x, y, c = lax.axis_index("x"), lax.axis_index("y"), lax.axis_index("c")
    others = [(1 - x, y), (x, 1 - y), (1 - x, 1 - y)]
    return x, y, c, others


def _col_halves(cols, c):
    hc = cols // 2
    return pl.ds(pl.multiple_of(c * hc, LANES), hc), pl.ds(pl.multiple_of((1 - c) * hc, LANES), hc)


def _gather_comm(shards, conv_shard=None):
    n = len(shards)
    inputs = list(shards) + ([] if conv_shard is None else [conv_shard])

    def copies(ins, outs, sems):
        send_sems, recv_sems, pass_send, pass_recv = sems[:4]
        x, y, c, others = _place()

        def chip_copy(a, j, chip):
            mine, _ = _col_halves(ins[a].shape[1], c)
            return pltpu.make_async_remote_copy(
                src_ref=ins[a].at[:, mine], dst_ref=outs[a].at[chip, :, mine],
                send_sem=send_sems.at[3 * a + j], recv_sem=recv_sems.at[3 * a + j],
                device_id=(*others[j], c), device_id_type=MESH)

        def pass_copy(a, j, chip, half):
            return pltpu.make_async_remote_copy(
                src_ref=outs[a].at[chip, :, half], dst_ref=outs[a].at[chip, :, half],
                send_sem=pass_send.at[3 * a + j], recv_sem=pass_recv.at[3 * a + j],
                device_id=(x, y, 1 - c), device_id_type=MESH)

        def conv_copy(j, chip):
            return pltpu.make_async_remote_copy(
                src_ref=ins[n], dst_ref=outs[n].at[chip],
                send_sem=sems[4].at[j], recv_sem=sems[5].at[j],
                device_id=(*others[j], c), device_id_type=MESH)

        me = 2 * x + y
        sends = [chip_copy(a, j, me) for a in range(n) for j in range(3)]
        if conv_shard is not None:
            sends += [conv_copy(j, me) for j in range(3)]
        return c, others, sends, chip_copy, pass_copy, conv_copy

    def start(ins, outs, sems):
        for cp in copies(ins, outs, sems)[2]:
            cp.start()

    def finish(ins, outs, sems):
        c, others, sends, chip_copy, pass_copy, conv_copy = copies(ins, outs, sems)
        passed = []
        for a in range(n):
            mine, _ = _col_halves(ins[a].shape[1], c)
            for j, (ox, oy) in enumerate(others):
                chip_copy(a, j, 2 * ox + oy).wait_recv()
                passed.append(pass_copy(a, j, 2 * ox + oy, mine))
                passed[-1].start()
        for a in range(n):
            _, theirs = _col_halves(ins[a].shape[1], c)
            for j, (ox, oy) in enumerate(others):
                pass_copy(a, j, 2 * ox + oy, theirs).wait_recv()
        if conv_shard is not None:
            for j, (ox, oy) in enumerate(others):
                conv_copy(j, 2 * ox + oy).wait_recv()
        for cp in sends + passed:
            cp.wait_send()

    scratch = [pltpu.SemaphoreType.DMA((3 * n,))] * 4
    if conv_shard is not None:
        scratch += [pltpu.SemaphoreType.DMA((3,))] * 2
    return _Comm(inputs, [jax.ShapeDtypeStruct((N_CHIPS,) + s.shape, s.dtype) for s in inputs], scratch, start, finish)


def _fill_own(stacks, shards):
    chip = 2 * lax.axis_index("x") + lax.axis_index("y")
    return [lax.dynamic_update_index_in_dim(st, s, chip, 0) for st, s in zip(stacks, shards)]


def _run_comm(name, comm):
    ci, co = len(comm.inputs), len(comm.out_shape)

    def body(*refs):
        comm.start(refs[:ci], refs[ci:ci + co], refs[ci + co:])
        comm.finish(refs[:ci], refs[ci:ci + co], refs[ci + co:])

    return pl.pallas_call(body, name=name, in_specs=[ANY] * ci, out_specs=[ANY] * co, out_shape=comm.out_shape,
                          scratch_shapes=comm.scratch)(*comm.inputs)


def _sibling_exchange_comm(grads):
    n = len(grads)

    def copies(ins, outs, sems):
        x, y, c, _ = _place()
        return [pltpu.make_async_remote_copy(
            src_ref=ins[a].at[:, :, _col_halves(ins[a].shape[2], c)[1]], dst_ref=outs[a],
            send_sem=sems[0].at[a], recv_sem=sems[1].at[a],
            device_id=(x, y, 1 - c), device_id_type=MESH) for a in range(n)]

    def start(ins, outs, sems):
        for cp in copies(ins, outs, sems):
            cp.start()

    def finish(ins, outs, sems):
        for cp in copies(ins, outs, sems):
            cp.wait()

    half = lambda s: jax.ShapeDtypeStruct((s.shape[0], s.shape[1], s.shape[2] // 2), s.dtype)
    return _Comm(grads, [half(s) for s in grads], [pltpu.SemaphoreType.DMA((n,))] * 2, start, finish)


def _merge_comms(comms):
    def split(refs, count):
        out, at = [], 0
        for cm in comms:
            out.append(refs[at:at + count(cm)])
            at += count(cm)
        return out

    def parts(ins, outs, sems):
        return zip(comms, split(ins, lambda cm: len(cm.inputs)), split(outs, lambda cm: len(cm.out_shape)),
                   split(sems, lambda cm: len(cm.scratch)))

    def start(ins, outs, sems):
        for cm, i, o, s in parts(ins, outs, sems):
            cm.start(i, o, s)

    def finish(ins, outs, sems):
        for cm, i, o, s in parts(ins, outs, sems):
            cm.finish(i, o, s)

    return _Comm(sum([cm.inputs for cm in comms], []), sum([cm.out_shape for cm in comms], []),
                 sum([cm.scratch for cm in comms], []), start, finish)


def _add_halves(name, grads, recvs, core):
    n = len(grads)

    def body(core_ref, *refs):
        for g_ref, r_ref, out_ref in zip(refs[:n], refs[n:2 * n], refs[2 * n:]):
            out_ref[...] = (g_ref[...].astype(F32) + r_ref[...].astype(F32)).astype(BF16)

    half = lambda g: pl.BlockSpec((None, g.shape[1], g.shape[2] // 2), lambda k, core_ref: (k, 0, 0))
    mine = lambda g: pl.BlockSpec((None, g.shape[1], g.shape[2] // 2), lambda k, core_ref: (k, 0, core_ref[0]))
    return pl.pallas_call(
        body, name=name,
        grid_spec=pltpu.PrefetchScalarGridSpec(
            num_scalar_prefetch=1, grid=(N_CHIPS,),
            in_specs=[mine(g) for g in grads] + [half(g) for g in grads],
            out_specs=[half(g) for g in grads]),
        out_shape=[jax.ShapeDtypeStruct(r.shape, BF16) for r in recvs],
        compiler_params=_params(("arbitrary",)),
    )(core, *grads, *recvs)


def _chip_exchange_comm(parts):
    n = len(parts)

    def copies(ins, outs, sems):
        x, y, c, others = _place()
        return [pltpu.make_async_remote_copy(
            src_ref=ins[a].at[2 * ox + oy], dst_ref=outs[a].at[j],
            send_sem=sems[0].at[3 * a + j], recv_sem=sems[1].at[3 * a + j],
            device_id=(ox, oy, c), device_id_type=MESH) for a in range(n) for j, (ox, oy) in enumerate(others)]

    def start(ins, outs, sems):
        for cp in copies(ins, outs, sems):
            cp.start()

    def finish(ins, outs, sems):
        for cp in copies(ins, outs, sems):
            cp.wait()

    return _Comm(parts, [jax.ShapeDtypeStruct((3,) + s.shape[1:], s.dtype) for s in parts],
                 [pltpu.SemaphoreType.DMA((3 * n,))] * 2, start, finish)


HBM = pl.BlockSpec(memory_space=pltpu.HBM)
SEM = pl.BlockSpec(memory_space=pltpu.SEMAPHORE)


def _split_exchange_copies(parts, lands, send_sems, recv_sems):
    x, y, c, others = _place()
    return [pltpu.make_async_remote_copy(
        src_ref=parts[a].at[2 * ox + oy], dst_ref=lands[a].at[j],
        send_sem=send_sems.at[3 * a + j], recv_sem=recv_sems.at[3 * a + j],
        device_id=(ox, oy, c), device_id_type=MESH) for a in range(len(parts)) for j, (ox, oy) in enumerate(others)]


def _exchange_start(name, parts):
    n = len(parts)

    def body(*refs):
        ins, lands = refs[:n], refs[n:2 * n]
        send_sems, recv_sems, token = refs[2 * n], refs[2 * n + 1], refs[-1]
        for cp in _split_exchange_copies(ins, lands, send_sems, recv_sems):
            cp.start()
        token[...] = jnp.zeros_like(token)

    land_shape = [(3,) + p.shape[1:] for p in parts]
    outs = pl.pallas_call(
        body, name=name,
        out_shape=[pltpu.SemaphoreType.DMA((3 * n,)), pltpu.SemaphoreType.DMA((3 * n,))]
        + [pltpu.HBM(p.shape, p.dtype) for p in parts] + [pltpu.HBM(s, p.dtype) for s, p in zip(land_shape, parts)]
        + [jax.ShapeDtypeStruct((8, LANES), F32)],
        in_specs=[HBM] * (2 * n), out_specs=[SEM, SEM] + [HBM] * (2 * n) + [pl.BlockSpec(memory_space=pltpu.VMEM)],
        input_output_aliases={i: 2 + i for i in range(2 * n)},
        compiler_params=pltpu.CompilerParams(has_side_effects=pltpu.SideEffectType.DATAFLOW_SIDE_EFFECTING),
    )(*[pltpu.with_memory_space_constraint(p, pltpu.HBM) for p in parts],
      *[pltpu.with_memory_space_constraint(lax.empty(s, p.dtype), pltpu.HBM) for s, p in zip(land_shape, parts)])
    return outs[0], outs[1], list(outs[2:2 + n]), list(outs[2 + n:2 + 2 * n]), outs[-1]


def _exchange_wait(name, send_sems, recv_sems, parts, lands, after):
    n = len(parts)

    def body(*refs):
        ins, zones = refs[:n], refs[n:2 * n]
        for cp in _split_exchange_copies(ins, zones, refs[2 * n], refs[2 * n + 1]):
            cp.wait_send()
            cp.wait_recv()

    outs = pl.pallas_call(
        body, name=name,
        out_shape=[pltpu.HBM(p.shape, p.dtype) for p in parts] + [pltpu.HBM(z.shape, z.dtype) for z in lands],
        in_specs=[HBM] * (2 * n) + [SEM, SEM] + [ANY] * len(after), out_specs=[HBM] * (2 * n),
        input_output_aliases={i: i for i in range(2 * n)},
        compiler_params=pltpu.CompilerParams(has_side_effects=pltpu.SideEffectType.DATAFLOW_SIDE_EFFECTING),
    )(*parts, *lands, send_sems, recv_sems, *after)
    return list(outs[:n]), list(outs[n:])


def _sum_chips(name, owns, recvs, chip, after):
    n = len(owns)
    hc = owns[0].shape[2]
    assert all(o.shape[2] == hc for o in owns)

    def body(chip_ref, *refs):
        for own_ref, recv_ref, out_ref in zip(refs[:n], refs[n:2 * n], refs[2 * n + 1:]):
            acc = own_ref[...].astype(F32)
            for j in range(3):
                acc = acc + recv_ref[j].astype(F32)
            out_ref[...] = acc

    return pl.pallas_call(
        body, name=name,
        grid_spec=pltpu.PrefetchScalarGridSpec(
            num_scalar_prefetch=1, grid=(hc // LANES,),
            in_specs=[pl.BlockSpec((None, o.shape[1], LANES), lambda i, chip_ref: (chip_ref[0], 0, i)) for o in owns]
            + [pl.BlockSpec((3, o.shape[1], LANES), lambda i, chip_ref: (0, 0, i)) for o in owns]
            + [pl.BlockSpec((8, LANES), lambda i, chip_ref: (0, 0))],
            out_specs=[pl.BlockSpec((o.shape[1], LANES), lambda i, chip_ref: (0, i)) for o in owns]),
        out_shape=[jax.ShapeDtypeStruct((o.shape[1], hc), F32) for o in owns],
        compiler_params=_params(("arbitrary",)),
    )(chip, *owns, *recvs, after)


def _share_halves(name, halves):
    n = len(halves)

    def body(*refs):
        srcs, dsts = refs[:n], refs[n:2 * n]
        send_sems, recv_sems = refs[2 * n:]
        x, y, c, _ = _place()
        copies = [pltpu.make_async_remote_copy(
            src_ref=srcs[a], dst_ref=dsts[a], send_sem=send_sems.at[a], recv_sem=recv_sems.at[a],
            device_id=(x, y, 1 - c), device_id_type=MESH) for a in range(n)]
        for cp in copies:
            cp.start()
        for cp in copies:
            cp.wait()

    return pl.pallas_call(
        body, name=name,
        in_specs=[ANY] * n, out_specs=[ANY] * n,
        out_shape=[jax.ShapeDtypeStruct(s.shape, s.dtype) for s in halves],
        scratch_shapes=[pltpu.SemaphoreType.DMA((n,)), pltpu.SemaphoreType.DMA((n,))],
    )(*halves)


def _small_gather_comm(part):
    def copies(ins, outs, sems):
        x, y, c, _ = _place()
        me = 4 * x + 2 * y + c
        both = []
        for d in range(1, N_DEV):
            px, py, pc = (1 - x if d & 4 else x, 1 - y if d & 2 else y, 1 - c if d & 1 else c)
            send = pltpu.make_async_remote_copy(
                src_ref=ins[0], dst_ref=outs[0].at[me], send_sem=sems[0].at[d - 1], recv_sem=sems[1].at[d - 1],
                device_id=(px, py, pc), device_id_type=MESH)
            recv = pltpu.make_async_remote_copy(
                src_ref=ins[0], dst_ref=outs[0].at[4 * px + 2 * py + pc], send_sem=sems[0].at[d - 1],
                recv_sem=sems[1].at[d - 1], device_id=(px, py, pc), device_id_type=MESH)
            both.append((send, recv))
        return both

    def start(ins, outs, sems):
        for send, _ in copies(ins, outs, sems):
            send.start()

    def finish(ins, outs, sems):
        for send, recv in copies(ins, outs, sems):
            recv.wait_recv()
            send.wait_send()

    return _Comm([part], [jax.ShapeDtypeStruct((N_DEV,) + part.shape, F32)],
                 [pltpu.SemaphoreType.DMA((N_DEV - 1,))] * 2, start, finish)


def _sum_devices(parts):
    def body(p_ref, out_ref):
        acc = p_ref[0]
        for k in range(1, N_DEV):
            acc = acc + p_ref[k]
        out_ref[...] = acc

    return pl.pallas_call(
        body, name="sum_devices", grid=(1,),
        in_specs=[pl.BlockSpec(parts.shape, lambda i: (0, 0, 0))],
        out_specs=pl.BlockSpec(parts.shape[1:], lambda i: (0, 0)),
        out_shape=jax.ShapeDtypeStruct(parts.shape[1:], F32),
        compiler_params=_params(("arbitrary",)),
    )(parts)


def _adam_update(w, g, m, v):
    nm = ADAM_B1 * m + (1.0 - ADAM_B1) * g
    nv = ADAM_B2 * v + (1.0 - ADAM_B2) * (g * g)
    m_hat = nm * (1.0 / (1.0 - ADAM_B1 ** ADAM_STEP))
    v_hat = nv * (1.0 / (1.0 - ADAM_B2 ** ADAM_STEP))
    return -ADAM_LR * (m_hat / (jnp.sqrt(v_hat) + ADAM_EPS) + ADAM_WD * w), nm, nv


def _adamw(name, w, g, m, v):
    def body(w_ref, g_ref, m_ref, v_ref, d_ref, nm_ref, nv_ref):
        d_ref[...], nm_ref[...], nv_ref[...] = _adam_update(w_ref[...], g_ref[...], m_ref[...], v_ref[...])

    spec = pl.BlockSpec(w.shape, lambda i: (0, 0))
    out = jax.ShapeDtypeStruct(w.shape, F32)
    return pl.pallas_call(
        body, name=name, grid=(1,),
        in_specs=[spec] * 4, out_specs=[spec] * 3, out_shape=[out] * 3,
        compiler_params=_params(("arbitrary",)),
    )(w, g, m, v)


def _adamw_halves(name, ws, mines, theirs, ms, vs, core):
    n = len(ws)
    cols = ws[0].shape[1]
    assert all(w.shape[1] == cols for w in ws)
    hc = cols // 2
    tc = LANES if n > 1 else min(256, hc)
    nt = hc // tc

    def body(core_ref, *refs):
        ins, outs = refs[:5 * n], refs[5 * n:]
        for a in range(n):
            w_ref, mine_ref, theirs_ref, m_ref, v_ref = [ins[j * n + a] for j in range(5)]
            g_ref, d_ref, nm_ref, nv_ref = outs[4 * a:4 * a + 4]
            gv = jnp.where(pl.program_id(0) == core_ref[0], mine_ref[...], theirs_ref[...])
            g_ref[...] = gv
            d_ref[...], nm_ref[...], nv_ref[...] = _adam_update(w_ref[...], gv, m_ref[...], v_ref[...])

    whole = lambda w: pl.BlockSpec((w.shape[0], tc), lambda h, i, core_ref: (0, h * nt + i))
    mine_spec = lambda w: pl.BlockSpec((w.shape[0], tc), lambda h, i, core_ref: (0, jnp.where(h == core_ref[0], i, 0)))
    theirs_spec = lambda w: pl.BlockSpec((w.shape[0], tc), lambda h, i, core_ref: (0, jnp.where(h == core_ref[0], 0, i)))
    outs = pl.pallas_call(
        body, name=name,
        grid_spec=pltpu.PrefetchScalarGridSpec(
            num_scalar_prefetch=1, grid=(2, nt),
            in_specs=[whole(w) for w in ws] + [mine_spec(w) for w in ws] + [theirs_spec(w) for w in ws]
            + [whole(w) for w in ws] * 2,
            out_specs=[whole(w) for w in ws for _ in range(4)]),
        out_shape=[jax.ShapeDtypeStruct(w.shape, F32) for w in ws for _ in range(4)],
        compiler_params=_params(("arbitrary", "arbitrary")),
    )(core, *ws, *mines, *theirs, *ms, *vs)
    return [outs[4 * a:4 * a + 4] for a in range(n)]


WEIGHTS = ("ffn1_norm", "ffn1_gate", "ffn1_up", "ffn1_down", "mix_norm", "w_in", "b_forget", "conv_w",
           "w_o_attn", "w_o_conv", "w_out", "ffn2_norm", "ffn2_gate", "ffn2_up", "ffn2_down", "final_norm")
VEC_ROWS = 8


def _pack_small(t, conv_rows):
    conv = t["conv_w"]
    parts = [t[n].reshape(VEC_ROWS, LANES) for n in NORMS]
    parts.append(jnp.pad(conv, ((0, conv_rows - conv.shape[0]), (0, 0))))
    parts.append(jnp.pad(t["b_forget"].reshape(1, N_HEADS), ((0, 7), (0, LANES - N_HEADS))))
    return jnp.concatenate(parts, axis=0)


def _unpack_small(p, conv_rows):
    out = {n: p[VEC_ROWS * i:VEC_ROWS * (i + 1)].reshape(-1) for i, n in enumerate(NORMS)}
    base = VEC_ROWS * len(NORMS)
    out["conv_w"] = p[base:base + 3]
    out["b_forget"] = p[base + conv_rows, :N_HEADS]
    return out


def _travel(name, a):
    return a.T if name in TRANSPOSED else a


GATHER_FIRST = ("ffn1_gate", "ffn1_up")
GATHER_RIDES = {"ffn1_up": ("ffn1_down",), "ffn1_down": ("w_in",), "mix_proj_fwd": ("w_o_attn", "w_o_conv", "w_out"),
                "attn_fwd": ("ffn2_gate", "ffn2_up", "ffn2_down")}
SIBLING_RIDES = {"ffn2": "mix_out_bwd", "out": None, "w_in": "mix_proj_bwd_dx", "ffn1": None}
CHIP_RIDES = {"ffn2": "attn_bwd", "out": "attn_bwd", "w_in": "ffn1_bwd_dw", "ffn1": None}
SMALL_RIDE = "ffn1_bwd_dw"


class _MeshPlan:
    def __init__(self, wts, core):
        self.small, self.core = wts, core
        self.shards = {n: wts[n].astype(BF16) for n in BIG}
        self.chip_part, self.from_chips, self.rides = {}, {}, {}
        self.stacks = {}
        conv_shard = jnp.pad(wts["conv_w"], ((0, 8 - wts["conv_w"].shape[0]), (0, 0)))
        for kernel_name, names in GATHER_RIDES.items():
            mine = [self.shards[n] for n in names]
            conv = conv_shard if kernel_name == "ffn1_up" else None
            names = names + (("conv_w",) if conv is not None else ())
            mine = mine + ([conv] if conv is not None else [])
            self._ride(kernel_name, _gather_comm(mine[:len(mine) - (conv is not None)], conv),
                       lambda got, names=names, mine=mine: self.stacks.update(zip(names, _fill_own(got, mine))))

    def weights(self, group):
        return _LAYOUTS[group](self.stacks, self.small)

    def ffn1_up(self, x, tm):
        px, py = lax.axis_index("x"), lax.axis_index("y")
        order = jnp.stack([2 * px + py, 2 * (1 - px) + py, 2 * px + (1 - py), 2 * (1 - px) + (1 - py)]).astype(jnp.int32)
        own = [self.shards[n] for n in GATHER_FIRST]
        (hg, hu, n, sg, su), brought = _ffn_up_gather("ffn1_up", x, self.small["ffn1_norm"].reshape(1, -1), *own, order,
                                                     tm, self.rider("ffn1_up"))
        self.stacks.update(zip(GATHER_FIRST, _fill_own([sg, su], own)))
        self.arrived("ffn1_up", brought)
        return hg, hu, n

    def _ride(self, kernel_name, comm, then):
        self.rides.setdefault(kernel_name, []).append((comm, then))

    def rider(self, kernel_name):
        comms = [comm for comm, _ in self.rides.get(kernel_name, [])]
        return _merge_comms(comms) if comms else None

    def arrived(self, kernel_name, results):
        for comm, then in self.rides.pop(kernel_name, []):
            then(results[:len(comm.out_shape)])
            results = results[len(comm.out_shape):]

    def reduce(self, group, grads):
        names = tuple(grads)
        mine = [grads[n] for n in names]

        def with_sibling(from_sibling):
            parts = _add_halves("add_halves_" + group, mine, list(from_sibling), self.core)
            self.chip_part.update(zip(names, parts))
            if CHIP_RIDES[group] is None:
                self.last = (names, _exchange_start("exchange_start_" + group, parts))
            else:
                self._ride(CHIP_RIDES[group], _chip_exchange_comm(parts),
                           lambda got: self.from_chips.update(zip(names, got)))

        if SIBLING_RIDES[group] is None:
            with_sibling(_run_comm("sibling_exchange_" + group, _sibling_exchange_comm(mine)))
        else:
            self._ride(SIBLING_RIDES[group], _sibling_exchange_comm(mine), with_sibling)

    def reduce_small(self, gs, loss):
        conv_all = _shard_cols(gs["conv_w"]).reshape(N_CHIPS * 8, LANES)
        part = _pack_small({**{n: gs[n] for n in NORMS}, "conv_w": conv_all, "b_forget": gs["b_forget"][0, :N_HEADS]},
                           N_CHIPS * 8)
        part = jnp.concatenate([part, jnp.broadcast_to(loss, (8, LANES))], axis=0)
        me = 4 * lax.axis_index("x") + 2 * lax.axis_index("y") + lax.axis_index("c")

        def landed(got):
            self.small_parts = lax.dynamic_update_index_in_dim(got[0], part, me, 0)

        self._ride(SMALL_RIDE, _small_gather_comm(part), landed)


def kernel(x, ffn1_norm, ffn1_gate, ffn1_up, ffn1_down, mix_norm, w_in, b_forget, conv_w, w_o_attn, w_o_conv, w_out, ffn2_norm, ffn2_gate, ffn2_up, ffn2_down, final_norm, loss_target, m_ffn1_norm, m_ffn1_gate, m_ffn1_up, m_ffn1_down, m_mix_norm, m_w_in, m_b_forget, m_conv_w, m_w_o_attn, m_w_o_conv, m_w_out, m_ffn2_norm, m_ffn2_gate, m_ffn2_up, m_ffn2_down, m_final_norm, v_ffn1_norm, v_ffn1_gate, v_ffn1_up, v_ffn1_down, v_mix_norm, v_w_in, v_b_forget, v_conv_w, v_w_o_attn, v_w_o_conv, v_w_out, v_ffn2_norm, v_ffn2_gate, v_ffn2_up, v_ffn2_down, v_final_norm):
    given = dict(locals())
    wts = {n: _travel(n, given[n]) for n in WEIGHTS}
    mom = {n: _travel(n, given["m_" + n]) for n in WEIGHTS}
    var = {n: _travel(n, given["v_" + n]) for n in WEIGHTS}
    B, S, D = x.shape
    chip = 2 * lax.axis_index("x") + lax.axis_index("y")
    chip1 = chip.astype(jnp.int32).reshape(1)
    core = lax.axis_index("c").astype(jnp.int32).reshape(1)

    plan = _MeshPlan(wts, core)
    loss, grad_x, gs = _local_step(x.reshape(B * S, D), loss_target.reshape(B * S, D), plan, B, S)

    last_names, (send_sems, recv_sems, parts_thru, lands, token) = plan.last
    delta, new_m, new_v, grads = {}, {}, {}, {}

    def finish(tag, names):
        by_cols = {}
        for n in names:
            by_cols.setdefault(wts[n].shape[1], []).append(n)
        mine = {}
        for cols, ns in by_cols.items():
            mine.update(zip(ns, _sum_chips("sum_chips_%s_%d" % (tag, cols), [plan.chip_part[n] for n in ns],
                                           [plan.from_chips[n] for n in ns], chip1, token)))
        theirs = dict(zip(names, _share_halves("share_halves_" + tag, [mine[n] for n in names])))
        raw = []
        for cols, ns in by_cols.items():
            outs = _adamw_halves("adamw_%s_%d" % (tag, cols), [wts[n] for n in ns], [mine[n] for n in ns],
                                 [theirs[n] for n in ns], [mom[n] for n in ns], [var[n] for n in ns], core)
            for n, per in zip(ns, outs):
                raw.append(per[-1])
                grads[n], delta[n], new_m[n], new_v[n] = [_travel(n, o) for o in per]
        return raw

    small_sum = _sum_devices(plan.small_parts)
    base = VEC_ROWS * len(NORMS)
    loss_row = small_sum.shape[0] - 8
    small_grads = _unpack_small(small_sum, N_CHIPS * 8)
    small_grads["conv_w"] = lax.dynamic_slice_in_dim(small_sum[base:base + N_CHIPS * 8], chip * 8, 8, axis=0)[:3]
    packs = [_pack_small(t, 8) for t in (wts, small_grads, mom, var)]
    small_out = _adamw("adamw_small", *packs)

    done = finish("early", [n for n in BIG if n not in last_names])
    parts_back, got = _exchange_wait("exchange_wait", send_sems, recv_sems, parts_thru, lands, done + list(small_out))
    plan.chip_part.update(zip(last_names, parts_back))
    plan.from_chips.update(zip(last_names, got))
    finish("last", last_names)
    grads.update(small_grads)
    for out, p in zip((delta, new_m, new_v), small_out):
        out.update(_unpack_small(p, 8))

    return (small_sum[loss_row, 0], grad_x.reshape(B, S, D), *[grads[n] for n in WEIGHTS], *[delta[n] for n in WEIGHTS],
            *[new_m[n] for n in WEIGHTS], *[new_v[n] for n in WEIGHTS])
```

```python
import functools
import math

import jax
import jax.numpy as jnp
from jax import lax
from jax.experimental import pallas as pl
from jax.experimental.pallas import tpu as pltpu

F32 = jnp.float32
BF16 = jnp.bfloat16
MESH = pl.DeviceIdType.MESH

N_CHIPS = 4
N_DEV = 8
N_HEADS = 8
HEAD_DIM = 64
HEAD_PAIRS = N_HEADS // 2
ATTN_W = N_HEADS * HEAD_DIM
CONV_W = 512
RMS_EPS = 1e-6
FFN_RES = 0.5
LANES = 128
VMEM_LIMIT = 56 * 1024 * 1024
ROW_BLOCK = 256

ADAM_LR = 0.001
ADAM_B1 = 0.9
ADAM_B2 = 0.999
ADAM_EPS = 1e-08
ADAM_WD = 0.01
ADAM_STEP = 10

PROJ_W = 3 * ATTN_W + 3 * CONV_W + 2 * 1024
COL_CB, COL_CC, COL_CX = 3 * ATTN_W, 3 * ATTN_W + CONV_W, 3 * ATTN_W + 2 * CONV_W
COL_GATES = 3 * ATTN_W + 3 * CONV_W
N_FORGET_COL = 3 * ATTN_W


def _params(sem=None, vmem=VMEM_LIMIT):
    return pltpu.CompilerParams(dimension_semantics=sem, vmem_limit_bytes=vmem)


def _dot(a, b):
    return lax.dot_general(a, b, (((1,), (0,)), ((), ())), preferred_element_type=F32)


def _dot_nt(a, b):
    return lax.dot_general(a, b, (((1,), (1,)), ((), ())), preferred_element_type=F32)


def _dot_tn(a, b):
    return lax.dot_general(a, b, (((0,), (0,)), ((), ())), preferred_element_type=F32)


def _sigmoid(x):
    return 1.0 / (1.0 + jnp.exp(-x))


def _rms(xv):
    inv = lax.rsqrt(jnp.mean(xv * xv, axis=-1, keepdims=True) + RMS_EPS)
    return xv * inv, inv


class _Comm:
    def __init__(self, inputs, out_shape, scratch, start, finish):
        self.inputs, self.out_shape, self.scratch = list(inputs), list(out_shape), list(scratch)
        self.start, self.finish = start, finish


def _pallas(body, name, grid, in_specs, out_specs, out_shape, scratch, args, comm=None):
    sem = ("arbitrary",) * len(grid)
    if comm is None:
        outs = pl.pallas_call(body, name=name, grid=grid, in_specs=in_specs, out_specs=out_specs,
                              out_shape=out_shape, scratch_shapes=scratch, compiler_params=_params(sem))(*args)
        return list(outs), []
    n_in, n_out, n_scr = len(in_specs), len(out_specs), len(scratch)
    ci, co = len(comm.inputs), len(comm.out_shape)

    def riding(*refs):
        ins, refs = refs[:n_in], refs[n_in:]
        cins, refs = refs[:ci], refs[ci:]
        outs, refs = refs[:n_out], refs[n_out:]
        couts, refs = refs[:co], refs[co:]
        scr, sems = refs[:n_scr], refs[n_scr:]
        ids = [pl.program_id(d) for d in range(len(grid))]
        first = functools.reduce(lambda a, b: a & b, [i == 0 for i in ids])
        last = functools.reduce(lambda a, b: a & b, [i == g - 1 for i, g in zip(ids, grid)])

        @pl.when(first)
        def _():
            comm.start(cins, couts, sems)

        body(*ins, *outs, *scr)

        @pl.when(last)
        def _():
            comm.finish(cins, couts, sems)

    any_spec = pl.BlockSpec(memory_space=pl.ANY)
    outs = pl.pallas_call(
        riding, name=name, grid=grid,
        in_specs=list(in_specs) + [any_spec] * ci, out_specs=list(out_specs) + [any_spec] * co,
        out_shape=list(out_shape) + comm.out_shape, scratch_shapes=list(scratch) + comm.scratch,
        compiler_params=_params(sem))(*args, *comm.inputs)
    return list(outs[:n_out]), list(outs[n_out:])


def _rms_bwd(dn, xhat, inv, g):
    dxhat = dn * g
    dx = inv * (dxhat - xhat * jnp.mean(dxhat * xhat, axis=-1, keepdims=True))
    return dx, jnp.sum(dn * xhat, axis=0, keepdims=True)


def _ffn_fwd_loss(name, x, g, wgt, wut, wd, target, gf, tm):
    T, D = x.shape
    K, Fs, _ = wgt.shape

    def body(x_ref, g_ref, wg_ref, wu_ref, wd_ref, t_ref, gf_ref,
             dx_ref, hg_ref, hu_ref, n_ref, loss_ref, dgf_ref, acc_scr):
        i, k = pl.program_id(0), pl.program_id(1)

        @pl.when(k == 0)
        def _():
            xhat, _ = _rms(x_ref[...])
            n_ref[...] = (xhat * g_ref[...]).astype(BF16)
            acc_scr[...] = jnp.zeros_like(acc_scr)

        @pl.when((k == 0) & (i == 0))
        def _():
            loss_ref[...] = jnp.zeros_like(loss_ref)
            dgf_ref[...] = jnp.zeros_like(dgf_ref)

        n = n_ref[...]
        hg = _dot_nt(n, wg_ref[...])
        hu = _dot_nt(n, wu_ref[...])
        hg_ref[...] = hg.astype(BF16)
        hu_ref[...] = hu.astype(BF16)
        act = (hg * _sigmoid(hg) * hu).astype(BF16)
        acc_scr[...] += _dot(act, wd_ref[...])

        @pl.when(k == K - 1)
        def _():
            gfv = gf_ref[...]
            for r0 in range(0, tm, ROW_BLOCK):
                rows = slice(r0, r0 + ROW_BLOCK)
                xhat, inv = _rms(x_ref[rows, :] + FFN_RES * acc_scr[rows, :])
                err = xhat * gfv - t_ref[rows, :]
                loss_ref[...] += 0.5 * jnp.sum(jnp.sum(err * err, axis=1, keepdims=True), axis=0, keepdims=True) / D
                dx, dg = _rms_bwd(err * (1.0 / D), xhat, inv, gfv)
                dx_ref[rows, :] = dx
                dgf_ref[...] += dg

    w_spec = pl.BlockSpec((None, Fs, D), lambda i, k: (k, 0, 0))
    act_spec = pl.BlockSpec((None, tm, Fs), lambda i, k: (k, i, 0))
    row = pl.BlockSpec((tm, D), lambda i, k: (i, 0))
    vec = pl.BlockSpec((1, D), lambda i, k: (0, 0))
    return _pallas(
        body, name, (T // tm, K),
        [row, vec, w_spec, w_spec, w_spec, row, vec],
        [row, act_spec, act_spec, row, pl.BlockSpec((1, LANES), lambda i, k: (0, 0)), vec],
        [jax.ShapeDtypeStruct((T, D), F32), jax.ShapeDtypeStruct((K, T, Fs), BF16),
         jax.ShapeDtypeStruct((K, T, Fs), BF16), jax.ShapeDtypeStruct((T, D), BF16),
         jax.ShapeDtypeStruct((1, LANES), F32), jax.ShapeDtypeStruct((1, D), F32)],
        [pltpu.VMEM((tm, D), F32)],
        (x, g, wgt, wut, wd, target, gf))[0]


def _ffn_up_gather(name, x, g, wg_own, wu_own, order, tm, comm=None):
    T, D = x.shape
    Fs = wg_own.shape[0]
    nt = T // tm
    ci, co = (len(comm.inputs), len(comm.out_shape)) if comm is not None else (0, 0)

    def body(order_ref, x_ref, g_ref, wgo_ref, wuo_ref, *rest):
        cins, rest = rest[:ci], rest[ci:]
        (hg_ref, hu_ref, n_ref, sg_ref, su_ref), rest = rest[:5], rest[5:]
        couts, rest = rest[:co], rest[co:]
        (n_all, wbuf, send_sems, recv_sems, pass_send, pass_recv, load_sems), csems = rest[:7], rest[7:]
        k, i = pl.program_id(0), pl.program_id(1)
        x_pos, y_pos, c, others = _place()
        me = 2 * x_pos + y_pos
        owns, stacks = (wgo_ref, wuo_ref), (sg_ref, su_ref)
        mine, theirs = _col_halves(D, c)

        def chip_copy(a, j, chip):
            return pltpu.make_async_remote_copy(
                src_ref=owns[a].at[:, mine], dst_ref=stacks[a].at[chip, :, mine],
                send_sem=send_sems.at[3 * a + j], recv_sem=recv_sems.at[3 * a + j],
                device_id=(*others[j], c), device_id_type=MESH)

        def pass_copy(a, j, chip, half):
            return pltpu.make_async_remote_copy(
                src_ref=stacks[a].at[chip, :, half], dst_ref=stacks[a].at[chip, :, half],
                send_sem=pass_send.at[3 * a + j], recv_sem=pass_recv.at[3 * a + j],
                device_id=(x_pos, y_pos, 1 - c), device_id_type=MESH)

        @pl.when((k == 0) & (i == 0))
        def _():
            for a in range(2):
                for j in range(3):
                    chip_copy(a, j, me).start()
            if comm is not None:
                comm.start(cins, couts, csems)

        def bring(j):
            ox, oy = others[j]
            chip = 2 * ox + oy
            for a in range(2):
                chip_copy(a, j, chip).wait_recv()
            for a in range(2):
                pass_copy(a, j, chip, mine).start()
            for a in range(2):
                pass_copy(a, j, chip, theirs).wait_recv()
            loads = [pltpu.make_async_copy(stacks[a].at[chip], wbuf.at[j % 2, a], load_sems.at[2 * (j % 2) + a])
                     for a in range(2)]
            for cp in loads:
                cp.start()
            for cp in loads:
                cp.wait()

        @pl.when((k == 1) & (i == 0))
        def _():
            bring(0)
            bring(1)

        @pl.when((k == 2) & (i == nt - 1))
        def _():
            bring(2)

        rows = pl.ds(pl.multiple_of(i * tm, tm), tm)

        @pl.when(k == 0)
        def _():
            xhat, _ = _rms(x_ref[...])
            n = (xhat * g_ref[...]).astype(BF16)
            n_ref[...] = n
            n_all[rows, :] = n
            hg_ref[...] = _dot_nt(n, wgo_ref[...]).astype(BF16)
            hu_ref[...] = _dot_nt(n, wuo_ref[...]).astype(BF16)

        @pl.when(k > 0)
        def _():
            n = n_all[rows, :]
            slot = (k - 1) % 2
            hg_ref[...] = _dot_nt(n, wbuf[slot, 0]).astype(BF16)
            hu_ref[...] = _dot_nt(n, wbuf[slot, 1]).astype(BF16)

        @pl.when((k == N_CHIPS - 1) & (i == nt - 1))
        def _():
            for a in range(2):
                for j, (ox, oy) in enumerate(others):
                    chip_copy(a, j, me).wait_send()
                    pass_copy(a, j, 2 * ox + oy, mine).wait_send()
            if comm is not None:
                comm.finish(cins, couts, csems)

    any_spec = pl.BlockSpec(memory_space=pl.ANY)
    first_pass = lambda k, i, order_ref: (jnp.where(k == 0, i, nt - 1), 0)
    whole = pl.BlockSpec((Fs, D), lambda k, i, order_ref: (0, 0))
    act_spec = pl.BlockSpec((None, tm, Fs), lambda k, i, order_ref: (order_ref[k], i, 0))
    stack = jax.ShapeDtypeStruct((N_CHIPS, Fs, D), BF16)
    outs = pl.pallas_call(
        body, name=name,
        grid_spec=pltpu.PrefetchScalarGridSpec(
            num_scalar_prefetch=1, grid=(N_CHIPS, nt),
            in_specs=[pl.BlockSpec((tm, D), first_pass), pl.BlockSpec((1, D), lambda k, i, order_ref: (0, 0)),
                      whole, whole] + [any_spec] * ci,
            out_specs=[act_spec, act_spec, pl.BlockSpec((tm, D), first_pass), any_spec, any_spec] + [any_spec] * co,
            scratch_shapes=[pltpu.VMEM((T, D), BF16), pltpu.VMEM((2, 2, Fs, D), BF16)]
            + [pltpu.SemaphoreType.DMA((6,))] * 4 + [pltpu.SemaphoreType.DMA((4,))]
            + (comm.scratch if comm is not None else [])),
        out_shape=[jax.ShapeDtypeStruct((N_CHIPS, T, Fs), BF16), jax.ShapeDtypeStruct((N_CHIPS, T, Fs), BF16),
                   jax.ShapeDtypeStruct((T, D), BF16), stack, stack] + (comm.out_shape if comm is not None else []),
        compiler_params=_params(("arbitrary", "arbitrary")),
    )(order, x, g, wg_own, wu_own, *(comm.inputs if comm is not None else []))
    return list(outs[:5]), list(outs[5:])


def _ffn_down(name, x, hg, hu, wd, tm, comm=None):
    T, D = x.shape
    K, Fs, _ = wd.shape

    def body(x_ref, hg_ref, hu_ref, wd_ref, out_ref, acc_scr):
        k = pl.program_id(1)

        @pl.when(k == 0)
        def _():
            acc_scr[...] = jnp.zeros_like(acc_scr)

        hgv = hg_ref[...].astype(F32)
        act = (hgv * _sigmoid(hgv) * hu_ref[...].astype(F32)).astype(BF16)
        acc_scr[...] += _dot(act, wd_ref[...])

        @pl.when(k == K - 1)
        def _():
            out_ref[...] = x_ref[...] + FFN_RES * acc_scr[...]

    act_spec = pl.BlockSpec((None, tm, Fs), lambda i, k: (k, i, 0))
    row = pl.BlockSpec((tm, D), lambda i, k: (i, 0))
    return _pallas(
        body, name, (T // tm, K),
        [row, act_spec, act_spec, pl.BlockSpec((None, Fs, D), lambda i, k: (k, 0, 0))],
        [row], [jax.ShapeDtypeStruct((T, D), F32)], [pltpu.VMEM((tm, D), F32)],
        (x, hg, hu, wd), comm)


def _ffn_bwd_dx(name, dout, x, g, hg, hu, wgt, wut, wd, tm, comm=None):
    T, D = x.shape
    K, Fs, _ = wgt.shape

    def body(dout_ref, x_ref, g_ref, hg_ref, hu_ref, wg_ref, wu_ref, wd_ref,
             dx_ref, dhg_ref, dhu_ref, dg_ref, df_ref, dn_scr):
        i, k = pl.program_id(0), pl.program_id(1)

        @pl.when(k == 0)
        def _():
            df_ref[...] = (FFN_RES * dout_ref[...]).astype(BF16)
            dn_scr[...] = jnp.zeros_like(dn_scr)

        @pl.when((k == 0) & (i == 0))
        def _():
            dg_ref[...] = jnp.zeros_like(dg_ref)

        for r0 in range(0, tm, ROW_BLOCK):
            rows = slice(r0, r0 + ROW_BLOCK)
            dact = _dot_nt(df_ref[rows, :], wd_ref[...])
            hgv = hg_ref[rows, :].astype(F32)
            huv = hu_ref[rows, :].astype(F32)
            s = _sigmoid(hgv)
            dhu = (dact * (hgv * s)).astype(BF16)
            dhg = (dact * huv * (s * (1.0 + hgv * (1.0 - s)))).astype(BF16)
            dhg_ref[rows, :] = dhg
            dhu_ref[rows, :] = dhu
            dn_scr[rows, :] += _dot(dhg, wg_ref[...]) + _dot(dhu, wu_ref[...])

        @pl.when(k == K - 1)
        def _():
            xhat, inv = _rms(x_ref[...])
            dx, dg = _rms_bwd(dn_scr[...], xhat, inv, g_ref[...])
            dx_ref[...] = dout_ref[...] + dx
            dg_ref[...] += dg

    w_spec = pl.BlockSpec((None, Fs, D), lambda i, k: (k, 0, 0))
    act_spec = pl.BlockSpec((None, tm, Fs), lambda i, k: (k, i, 0))
    row = pl.BlockSpec((tm, D), lambda i, k: (i, 0))
    row_once = pl.BlockSpec((tm, D), lambda i, k: (i, 0), pipeline_mode=pl.Buffered(1))
    vec = pl.BlockSpec((1, D), lambda i, k: (0, 0))
    return _pallas(
        body, name, (T // tm, K),
        [row, row_once, vec, act_spec, act_spec, w_spec, w_spec, w_spec],
        [row_once, act_spec, act_spec, vec, row],
        [jax.ShapeDtypeStruct((T, D), F32), jax.ShapeDtypeStruct((K, T, Fs), BF16),
         jax.ShapeDtypeStruct((K, T, Fs), BF16), jax.ShapeDtypeStruct((1, D), F32),
         jax.ShapeDtypeStruct((T, D), BF16)],
        [pltpu.VMEM((tm, D), F32)],
        (dout, x, g, hg, hu, wgt, wut, wd), comm)


def _ffn_bwd_dw(name, n, df, hg, hu, dhg, dhu, tk, comm=None):
    T, D = n.shape
    K, _, Fs = hg.shape
    nt = T // tk

    def body(n_ref, df_ref, hg_ref, hu_ref, dhg_ref, dhu_ref, dwg_ref, dwu_ref, dwd_ref, accg, accu, accd):
        t = pl.program_id(1)

        @pl.when(t == 0)
        def _():
            accg[...] = jnp.zeros_like(accg)
            accu[...] = jnp.zeros_like(accu)
            accd[...] = jnp.zeros_like(accd)

        nv = n_ref[...]
        hgv = hg_ref[...].astype(F32)
        act = (hgv * _sigmoid(hgv) * hu_ref[...].astype(F32)).astype(BF16)
        accg[...] += _dot_tn(dhg_ref[...], nv)
        accu[...] += _dot_tn(dhu_ref[...], nv)
        accd[...] += _dot_tn(act, df_ref[...])

        @pl.when(t == nt - 1)
        def _():
            dwg_ref[...] = accg[...].astype(BF16)
            dwu_ref[...] = accu[...].astype(BF16)
            dwd_ref[...] = accd[...].astype(BF16)

    act_spec = pl.BlockSpec((None, tk, Fs), lambda k, t: (k, t, 0))
    w_spec = pl.BlockSpec((None, Fs, D), lambda k, t: (k, 0, 0))
    row = pl.BlockSpec((tk, D), lambda k, t: (t, 0))
    return _pallas(
        body, name, (K, nt),
        [row, row, act_spec, act_spec, act_spec, act_spec],
        [w_spec, w_spec, w_spec],
        [jax.ShapeDtypeStruct((K, Fs, D), BF16)] * 3,
        [pltpu.VMEM((Fs, D), F32)] * 3,
        (n, df, hg, hu, dhg, dhu), comm)


def _ffn_bwd_dw_reduced(name, n, df, hg, hu, dhg, dhu, tk, comm=None):
    T, D = n.shape
    K, _, Fs = hg.shape
    nt = T // tk
    hc = D // 2
    assert nt >= 2, "a pass's swap is finished at the second step of the next pass"

    def body(n_ref, df_ref, hg_ref, hu_ref, dhg_ref, dhu_ref, pg_ref, pu_ref, pd_ref,
             accg, accu, accd, stage, own, land, send_sems, recv_sems):
        k, t = pl.program_id(0), pl.program_id(1)
        x, y, c, _ = _place()
        _, theirs = _col_halves(D, c)
        accs, outs = (accg, accu, accd), (pg_ref, pu_ref, pd_ref)

        def swap(j, a):
            return pltpu.make_async_remote_copy(
                src_ref=stage.at[a, :, theirs], dst_ref=land.at[j, a],
                send_sem=send_sems.at[3 * j + a], recv_sem=recv_sems.at[3 * j + a],
                device_id=(x, y, 1 - c), device_id_type=MESH)

        def finish(j):
            for a in range(3):
                swap(j, a).wait_recv()
                outs[a][j] = (own[a] + land[j, a].astype(F32)).astype(BF16)

        @pl.when(t == 0)
        def _():
            for acc in accs:
                acc[...] = jnp.zeros_like(acc)

        nv = n_ref[...]
        hgv = hg_ref[...].astype(F32)
        act = (hgv * _sigmoid(hgv) * hu_ref[...].astype(F32)).astype(BF16)
        accg[...] += _dot_tn(dhg_ref[...], nv)
        accu[...] += _dot_tn(dhu_ref[...], nv)
        accd[...] += _dot_tn(act, df_ref[...])

        for j in range(K - 1):
            @pl.when((k == j + 1) & (t == 1))
            def _(j=j):
                finish(j)

        @pl.when(t == nt - 1)
        def _():
            for j in range(K - 1):
                @pl.when(k == j + 1)
                def _(j=j):
                    for a in range(3):
                        swap(j, a).wait_send()
            for a in range(3):
                stage[a] = accs[a][...].astype(BF16)

                @pl.when(c == 0)
                def _(a=a):
                    own[a] = accs[a][:, :hc]

                @pl.when(c == 1)
                def _(a=a):
                    own[a] = accs[a][:, hc:]
            for j in range(K):
                @pl.when(k == j)
                def _(j=j):
                    for a in range(3):
                        swap(j, a).start()

        @pl.when((k == K - 1) & (t == nt - 1))
        def _():
            finish(K - 1)
            for a in range(3):
                swap(K - 1, a).wait_send()

    act_spec = pl.BlockSpec((None, tk, Fs), lambda k, t: (k, t, 0))
    row = pl.BlockSpec((tk, D), lambda k, t: (t, 0))
    resident = pl.BlockSpec(memory_space=pltpu.VMEM)
    part = jax.ShapeDtypeStruct((K, Fs, hc), BF16)
    return _pallas(
        body, name, (K, nt),
        [row, row, act_spec, act_spec, act_spec, act_spec],
        [resident, resident, resident], [part, part, part],
        [pltpu.VMEM((Fs, D), F32)] * 3
        + [pltpu.VMEM((3, Fs, D), BF16), pltpu.VMEM((3, Fs, hc), F32), pltpu.VMEM((K, 3, Fs, hc), BF16),
           pltpu.SemaphoreType.DMA((3 * K,)), pltpu.SemaphoreType.DMA((3 * K,))],
        (n, df, hg, hu, dhg, dhu), comm)


def _mix_proj_fwd(x, g, wproj_t, wf_t, tm, tn, comm=None):
    T, D = x.shape
    N = wproj_t.shape[0]

    def body(x_ref, g_ref, w_ref, wf_ref, h_ref, proj_ref, flog_ref, h_scr):
        @pl.when(pl.program_id(1) == 0)
        def _():
            xhat, _ = _rms(x_ref[...])
            h = (xhat * g_ref[...]).astype(BF16)
            h_scr[...] = h
            h_ref[...] = h
            flog_ref[...] = _dot_nt(h, wf_ref[...])

        proj_ref[...] = _dot_nt(h_scr[...], w_ref[...]).astype(BF16)

    return _pallas(
        body, "mix_proj_fwd", (T // tm, N // tn),
        [pl.BlockSpec((tm, D), lambda i, n: (i, 0)), pl.BlockSpec((1, D), lambda i, n: (0, 0)),
         pl.BlockSpec((tn, D), lambda i, n: (n, 0)), pl.BlockSpec((LANES, D), lambda i, n: (0, 0))],
        [pl.BlockSpec((tm, D), lambda i, n: (i, 0)), pl.BlockSpec((tm, tn), lambda i, n: (i, n)),
         pl.BlockSpec((tm, LANES), lambda i, n: (i, 0))],
        [jax.ShapeDtypeStruct((T, D), BF16), jax.ShapeDtypeStruct((T, N), BF16),
         jax.ShapeDtypeStruct((T, LANES), F32)],
        [pltpu.VMEM((tm, D), BF16)],
        (x, g, wproj_t, wf_t), comm)


def _log_sigmoid(z):
    return -(jnp.maximum(-z, 0.0) + jnp.log(1.0 + jnp.exp(-jnp.abs(z))))


def _tri(n, lower):
    r = lax.broadcasted_iota(jnp.int32, (n, n), 0)
    c = lax.broadcasted_iota(jnp.int32, (n, n), 1)
    return jnp.where((r >= c) if lower else (r <= c), 1.0, 0.0).astype(F32)


def _dot_f32(a, b):
    return lax.dot_general(a, b, (((1,), (0,)), ((), ())), preferred_element_type=F32,
                           precision=lax.Precision.HIGHEST)


def _fgate_fwd(flog, bias, B, S, ch):
    def body(flog_ref, b_ref, cum_ref):
        tri = _tri(ch, True)
        carry = jnp.zeros((1, LANES), F32)
        for c0 in range(0, S, ch):
            lf = _log_sigmoid(flog_ref[c0:c0 + ch, :] + b_ref[...])
            cs = _dot_f32(tri, lf) + carry
            cum_ref[c0:c0 + ch, :] = cs
            carry = cs[ch - 1:ch, :]

    return pl.pallas_call(
        body, name="fgate_fwd", grid=(B,),
        in_specs=[pl.BlockSpec((S, LANES), lambda b: (b, 0)),
                  pl.BlockSpec((1, LANES), lambda b: (0, 0))],
        out_specs=pl.BlockSpec((S, LANES), lambda b: (b, 0)),
        out_shape=jax.ShapeDtypeStruct((B * S, LANES), F32),
        compiler_params=_params(("arbitrary",)),
    )(flog, bias)


def _fgate_bwd(dcum, flog, bias, B, S, ch):
    def body(dcum_ref, flog_ref, b_ref, dflog_ref, db_ref):
        @pl.when(pl.program_id(0) == 0)
        def _():
            db_ref[...] = jnp.zeros_like(db_ref)

        tri = _tri(ch, False)
        carry = jnp.zeros((1, LANES), F32)
        db = jnp.zeros((1, LANES), F32)
        for c0 in range(S - ch, -1, -ch):
            dlf = _dot_f32(tri, dcum_ref[c0:c0 + ch, :]) + carry
            carry = dlf[0:1, :]
            z = flog_ref[c0:c0 + ch, :] + b_ref[...]
            dz = dlf * _sigmoid(-z)
            dflog_ref[c0:c0 + ch, :] = dz
            db = db + jnp.sum(dz, axis=0, keepdims=True)
        db_ref[...] += db

    return pl.pallas_call(
        body, name="fgate_bwd", grid=(B,),
        in_specs=[pl.BlockSpec((S, LANES), lambda b: (b, 0)),
                  pl.BlockSpec((S, LANES), lambda b: (b, 0)),
                  pl.BlockSpec((1, LANES), lambda b: (0, 0))],
        out_specs=[pl.BlockSpec((S, LANES), lambda b: (b, 0)),
                   pl.BlockSpec((1, LANES), lambda b: (0, 0))],
        out_shape=[jax.ShapeDtypeStruct((B * S, LANES), F32),
                   jax.ShapeDtypeStruct((1, LANES), F32)],
        compiler_params=_params(("arbitrary",)),
    )(dcum, flog, bias)


def _pick_lane(tile, h):
    lane = lax.broadcasted_iota(jnp.int32, tile.shape, 1)
    return jnp.sum(jnp.where(lane == h, tile, 0.0), axis=1, keepdims=True)


def _put_lane(col, h, width=LANES):
    lane = lax.broadcasted_iota(jnp.int32, (col.shape[0], width), 1)
    return jnp.where(lane == h, col, 0.0)


def _pick_row(tile, h):
    row = lax.broadcasted_iota(jnp.int32, tile.shape, 0)
    return jnp.sum(jnp.where(row == h, tile, 0.0), axis=0, keepdims=True)


def _put_row(vec, h):
    row = lax.broadcasted_iota(jnp.int32, (8, vec.shape[1]), 0)
    return jnp.where(row == h, vec, 0.0)


def _causal(tq):
    r = lax.broadcasted_iota(jnp.int32, (tq, tq), 0)
    c = lax.broadcasted_iota(jnp.int32, (tq, tq), 1)
    return r >= c


def _head_halves(t):
    lo = lax.broadcasted_iota(jnp.int32, t.shape, 1) < HEAD_DIM
    zero = jnp.zeros_like(t)
    return jnp.where(lo, t, zero), jnp.where(lo, zero, t)


NEG = -1e30
ATTN_SCALE = 1.0 / math.sqrt(HEAD_DIM)


def _scaled(q):
    return (q.astype(F32) * ATTN_SCALE).astype(q.dtype)


def _attn_fwd(proj, cum, cum_t, B, S, tq, comm=None):
    nq = S // tq

    def body(q_ref, k_ref, v_ref, cum_ref, cumt_ref, o_ref, lse_ref):
        qi, hp = pl.program_id(1), pl.program_id(2)
        qm = _head_halves(_scaled(q_ref[...]))
        first_head = lax.broadcasted_iota(jnp.int32, (LANES, tq), 0) < HEAD_DIM
        r = lax.broadcasted_iota(jnp.int32, (tq, tq), 0)
        c = lax.broadcasted_iota(jnp.int32, (tq, tq), 1)

        def tile(j, carry, masked):
            (ma, la), (mb, lb), acc = carry
            off = pl.multiple_of(j * tq, tq)
            kj = k_ref[pl.ds(off, tq), :]
            vm = _head_halves(v_ref[pl.ds(off, tq), :])
            cumk = cum_ref[pl.ds(off, tq), :]
            new, alphas, pv = [], [], jnp.zeros((LANES, tq), F32)
            for e, (m, l) in enumerate(((ma, la), (mb, lb))):
                s = _dot_nt(kj, qm[e]) - _pick_lane(cumk, 2 * hp + e)
                if masked:
                    s = jnp.where(r <= c, s, NEG)
                m_new = jnp.maximum(m, jnp.max(s, axis=0, keepdims=True))
                p = jnp.exp(s - m_new)
                alpha = jnp.exp(m - m_new)
                new.append((m_new, alpha * l + jnp.sum(p, axis=0, keepdims=True)))
                alphas.append(alpha)
                pv = pv + _dot_tn(vm[e], p.astype(BF16))
            acc = jnp.where(first_head, alphas[0], alphas[1]) * acc + pv
            return new[0], new[1], acc

        one = (jnp.full((1, tq), NEG, F32), jnp.zeros((1, tq), F32))
        carry = lax.fori_loop(0, qi, lambda j, cr: tile(j, cr, False), (one, one, jnp.zeros((LANES, tq), F32)))
        (ma, la), (mb, lb), acc = tile(qi, carry, True)
        o_ref[...] = (acc / jnp.where(first_head, la, lb)).T.astype(BF16)

        @pl.when(hp == 0)
        def _():
            lse_ref[...] = jnp.zeros_like(lse_ref)

        ct = cumt_ref[...]
        lse_ref[...] += (_put_row(ma + jnp.log(la) + _pick_row(ct, 2 * hp), 2 * hp)
                         + _put_row(mb + jnp.log(lb) + _pick_row(ct, 2 * hp + 1), 2 * hp + 1))

    kv = lambda first: pl.BlockSpec((S, LANES), lambda b, i, hp: (b, first + hp))
    row_block = pl.BlockSpec((None, None, 8, tq), lambda b, i, hp: (b, i, 0, 0))
    return _pallas(
        body, "attn_fwd", (B, nq, HEAD_PAIRS),
        [pl.BlockSpec((tq, LANES), lambda b, i, hp: (b * nq + i, hp)),
         kv(ATTN_W // LANES), kv(2 * ATTN_W // LANES),
         pl.BlockSpec((S, LANES), lambda b, i, hp: (b, 0)), row_block],
        [pl.BlockSpec((tq, LANES), lambda b, i, hp: (b * nq + i, hp)), row_block],
        [jax.ShapeDtypeStruct((B * S, ATTN_W), BF16), jax.ShapeDtypeStruct((B, nq, 8, tq), F32)],
        [], (proj, proj, proj, cum, cum_t), comm)


def _attn_bwd(proj, o, do, lse, cum, cum_t, B, S, tq, comm=None):
    nq = S // tq

    def body(q_ref, k_ref, v_ref, o_ref, do_ref, lse_ref, cum_ref, cumt_ref,
             dq_ref, dk_ref, dv_ref, dcq_ref, dck_ref, dq_scr):
        hp, kj = pl.program_id(1), pl.program_id(2)

        @pl.when(kj == 0)
        def _():
            dq_scr[...] = jnp.zeros_like(dq_scr)

        @pl.when((kj == 0) & (hp == 0))
        def _():
            dcq_ref[...] = jnp.zeros_like(dcq_ref)
            dck_ref[...] = jnp.zeros_like(dck_ref)

        kv = k_ref[...]
        vv = v_ref[...]
        km = _head_halves(kv)
        ct = cumt_ref[...]
        ck = [_pick_row(ct, 2 * hp + e) for e in range(2)]

        def tile(i, carry, masked):
            dk, dv, dcol = carry
            off = pl.multiple_of(i * tq, tq)
            qi = q_ref[pl.ds(off, tq), :]
            ov = o_ref[pl.ds(off, tq), :].astype(F32)
            qm = _head_halves(_scaled(qi))
            dom = _head_halves(do_ref[pl.ds(off, tq), :])
            cumv = cum_ref[pl.ds(off, tq), :]
            lsev = lse_ref[pl.ds(off, tq), :]
            dcq = jnp.zeros((tq, LANES), F32)
            dq = jnp.zeros((tq, LANES), F32)
            dcol_new = []
            for e in range(2):
                delta = jnp.sum(dom[e].astype(F32) * ov, axis=1, keepdims=True)
                row_term = _pick_lane(cumv, 2 * hp + e) - _pick_lane(lsev, 2 * hp + e)
                p = jnp.exp(_dot_nt(qm[e], kv) + row_term - ck[e])
                if masked:
                    p = jnp.where(_causal(tq), p, 0.0)
                dv = dv + _dot_tn(dom[e], p.astype(BF16))
                ds = p * (_dot_nt(dom[e], vv) - delta)
                dcol_new.append(dcol[e] + jnp.sum(ds, axis=0, keepdims=True))
                dcq = dcq + _put_lane(jnp.sum(ds, axis=1, keepdims=True), 2 * hp + e)
                dsb = ds.astype(BF16)
                dk = dk + _dot_tn(qm[e], dsb)
                dq = dq + _dot(dsb, km[e]) * ATTN_SCALE
            dq_scr[pl.ds(off, tq), :] += dq
            dcq_ref[pl.ds(off, tq), :] += dcq
            return dk, dv, tuple(dcol_new)

        zero_row = jnp.zeros((1, tq), F32)
        init = (jnp.zeros((LANES, tq), F32), jnp.zeros((LANES, tq), F32), (zero_row, zero_row))
        carry = tile(kj, init, True)
        dk, dv, dcol = lax.fori_loop(kj + 1, nq, lambda i, c: tile(i, c, False), carry)
        dk_ref[...] = dk.T.astype(BF16)
        dv_ref[...] = dv.T.astype(BF16)
        dck_ref[kj] += -(_put_row(dcol[0], 2 * hp) + _put_row(dcol[1], 2 * hp + 1))

        @pl.when(kj == nq - 1)
        def _():
            dq_ref[...] = dq_scr[...].astype(BF16)

    seq = lambda first: pl.BlockSpec((S, LANES), lambda b, hp, j: (b, first + hp))
    tile_in = lambda first: pl.BlockSpec((tq, LANES), lambda b, hp, j: (b * nq + j, first + hp))
    lanes0 = pl.BlockSpec((S, LANES), lambda b, hp, j: (b, 0))
    out = jax.ShapeDtypeStruct((B * S, ATTN_W), BF16)
    return _pallas(
        body, "attn_bwd", (B, HEAD_PAIRS, nq),
        [seq(0), tile_in(ATTN_W // LANES), tile_in(2 * ATTN_W // LANES), seq(0), seq(0), lanes0, lanes0,
         pl.BlockSpec((None, None, 8, tq), lambda b, hp, j: (b, j, 0, 0))],
        [seq(0), tile_in(0), tile_in(0), lanes0,
         pl.BlockSpec((None, nq, 8, tq), lambda b, hp, j: (b, 0, 0, 0))],
        [out, out, out, jax.ShapeDtypeStruct((B * S, LANES), F32), jax.ShapeDtypeStruct((B, nq, 8, tq), F32)],
        [pltpu.VMEM((S, LANES), F32)],
        (proj, proj, proj, o, do, lse, cum, cum_t), comm)


def _shift_down(u, n):
    row = lax.broadcasted_iota(jnp.int32, u.shape, 0)
    return jnp.where(row >= n, pltpu.roll(u, n, 0), 0.0)


def _shift_up(u, n):
    rows = u.shape[0]
    row = lax.broadcasted_iota(jnp.int32, u.shape, 0)
    return jnp.where(row < rows - n, pltpu.roll(u, rows - n, 0), 0.0)


def _conv_specs(S):
    cb = pl.BlockSpec((S, LANES), lambda g, b: (b, COL_CB // LANES + g))
    cc = pl.BlockSpec((S, LANES), lambda g, b: (b, COL_CC // LANES + g))
    cx = pl.BlockSpec((S, LANES), lambda g, b: (b, COL_CX // LANES + g))
    w = pl.BlockSpec((8, LANES), lambda g, b: (0, g))
    return cb, cc, cx, w


def _conv_fwd(proj, conv_w, B, S):
    def body(cb_ref, cc_ref, cx_ref, w_ref, y_ref):
        u = cc_ref[...].astype(F32) * cx_ref[...].astype(F32)
        w = w_ref[...]
        conv = w[0:1, :] * _shift_down(u, 2) + w[1:2, :] * _shift_down(u, 1) + w[2:3, :] * u
        y_ref[...] = (cb_ref[...].astype(F32) * conv).astype(BF16)

    cb, cc, cx, w = _conv_specs(S)
    return pl.pallas_call(
        body, name="conv_fwd", grid=(CONV_W // LANES, B),
        in_specs=[cb, cc, cx, w],
        out_specs=pl.BlockSpec((S, LANES), lambda g, b: (b, g)),
        out_shape=jax.ShapeDtypeStruct((B * S, CONV_W), BF16),
        compiler_params=_params(("arbitrary", "arbitrary")),
    )(proj, proj, proj, conv_w)


def _conv_bwd(dy, proj, conv_w, B, S):
    def body(dy_ref, cb_ref, cc_ref, cx_ref, w_ref, dcb_ref, dcc_ref, dcx_ref, dw_ref):
        @pl.when(pl.program_id(1) == 0)
        def _():
            dw_ref[...] = jnp.zeros_like(dw_ref)

        ccv = cc_ref[...].astype(F32)
        cxv = cx_ref[...].astype(F32)
        u = ccv * cxv
        u1 = _shift_down(u, 1)
        u2 = _shift_down(u, 2)
        w = w_ref[...]
        conv = w[0:1, :] * u2 + w[1:2, :] * u1 + w[2:3, :] * u
        dyv = dy_ref[...].astype(F32)
        dcb_ref[...] = (dyv * conv).astype(BF16)
        dconv = dyv * cb_ref[...].astype(F32)
        du = w[2:3, :] * dconv + w[1:2, :] * _shift_up(dconv, 1) + w[0:1, :] * _shift_up(dconv, 2)
        dcc_ref[...] = (du * cxv).astype(BF16)
        dcx_ref[...] = (du * ccv).astype(BF16)
        row = lax.broadcasted_iota(jnp.int32, (8, LANES), 0)
        dw = jnp.where(row == 0, jnp.sum(dconv * u2, axis=0, keepdims=True),
                       jnp.where(row == 1, jnp.sum(dconv * u1, axis=0, keepdims=True),
                                 jnp.where(row == 2, jnp.sum(dconv * u, axis=0, keepdims=True), 0.0)))
        dw_ref[...] += dw

    cb, cc, cx, w = _conv_specs(S)
    out = pl.BlockSpec((S, LANES), lambda g, b: (b, g))
    return pl.pallas_call(
        body, name="conv_bwd", grid=(CONV_W // LANES, B),
        in_specs=[out, cb, cc, cx, w],
        out_specs=[out, out, out, w],
        out_shape=[jax.ShapeDtypeStruct((B * S, CONV_W), BF16)] * 3 + [jax.ShapeDtypeStruct((8, CONV_W), F32)],
        compiler_params=_params(("arbitrary", "arbitrary")),
    )(dy, proj, proj, proj, conv_w)


def _gate_specs(tm, D):
    ga = pl.BlockSpec((tm, D), lambda i: (i, COL_GATES // D))
    gc = pl.BlockSpec((tm, D), lambda i: (i, COL_GATES // D + 1))
    return ga, gc


def _mix_out_fwd(x, o, yc, proj, woa, woc, wout, tm):
    T, D = x.shape

    def body(x_ref, o_ref, yc_ref, ga_ref, gc_ref, woa_ref, woc_ref, wout_ref, out_ref):
        ya = _dot(o_ref[...], woa_ref[...])
        yp = _dot(yc_ref[...], woc_ref[...])
        merged = _sigmoid(ga_ref[...].astype(F32)) * ya + _sigmoid(gc_ref[...].astype(F32)) * yp
        out_ref[...] = x_ref[...] + _dot(merged.astype(BF16), wout_ref[...])

    ga, gc = _gate_specs(tm, D)
    row = lambda w: pl.BlockSpec((tm, w), lambda i: (i, 0))
    whole = lambda a: pl.BlockSpec(a.shape, lambda i: (0, 0))
    return pl.pallas_call(
        body, name="mix_out_fwd", grid=(T // tm,),
        in_specs=[row(D), row(ATTN_W), row(CONV_W), ga, gc, whole(woa), whole(woc), whole(wout)],
        out_specs=row(D),
        out_shape=jax.ShapeDtypeStruct((T, D), F32),
        compiler_params=_params(("arbitrary",)),
    )(x, o, yc, proj, proj, woa, woc, wout)


def _mix_out_bwd(dx, o, yc, proj, woa, woc, wout, tm, comm=None):
    T, D = dx.shape
    nt = T // tm

    def body(dx_ref, o_ref, yc_ref, ga_ref, gc_ref, woa_ref, woc_ref, wout_ref,
             do_ref, dyc_ref, dg_ref, dwoa_ref, dwoc_ref, dwout_ref, acca, accc, acco):
        t = pl.program_id(0)

        @pl.when(t == 0)
        def _():
            acca[...] = jnp.zeros_like(acca)
            accc[...] = jnp.zeros_like(accc)
            acco[...] = jnp.zeros_like(acco)

        dxb = dx_ref[...].astype(BF16)
        ov, ycv = o_ref[...], yc_ref[...]
        ya = _dot(ov, woa_ref[...])
        yp = _dot(ycv, woc_ref[...])
        sa = _sigmoid(ga_ref[...].astype(F32))
        sc = _sigmoid(gc_ref[...].astype(F32))
        merged = (sa * ya + sc * yp).astype(BF16)
        dm = _dot_nt(dxb, wout_ref[...])
        dya = (dm * sa).astype(BF16)
        dyp = (dm * sc).astype(BF16)
        dg_ref[:, :D] = (dm * ya * sa * (1.0 - sa)).astype(BF16)
        dg_ref[:, D:] = (dm * yp * sc * (1.0 - sc)).astype(BF16)
        do_ref[...] = _dot_nt(dya, woa_ref[...]).astype(BF16)
        dyc_ref[...] = _dot_nt(dyp, woc_ref[...]).astype(BF16)
        acca[...] += _dot_tn(ov, dya)
        accc[...] += _dot_tn(ycv, dyp)
        acco[...] += _dot_tn(merged, dxb)

        @pl.when(t == nt - 1)
        def _():
            dwoa_ref[...] = acca[...].astype(BF16)
            dwoc_ref[...] = accc[...].astype(BF16)
            dwout_ref[...] = acco[...].astype(BF16)

    ga, gc = _gate_specs(tm, D)
    row = lambda w: pl.BlockSpec((tm, w), lambda i: (i, 0))
    whole = lambda a: pl.BlockSpec(a.shape, lambda i: (0, 0))
    return _pallas(
        body, "mix_out_bwd", (nt,),
        [row(D), row(ATTN_W), row(CONV_W), ga, gc, whole(woa), whole(woc), whole(wout)],
        [row(ATTN_W), row(CONV_W), row(2 * D), whole(woa), whole(woc), whole(wout)],
        [jax.ShapeDtypeStruct((T, ATTN_W), BF16), jax.ShapeDtypeStruct((T, CONV_W), BF16),
         jax.ShapeDtypeStruct((T, 2 * D), BF16),
         jax.ShapeDtypeStruct(woa.shape, BF16), jax.ShapeDtypeStruct(woc.shape, BF16),
         jax.ShapeDtypeStruct(wout.shape, BF16)],
        [pltpu.VMEM(woa.shape, F32), pltpu.VMEM(woc.shape, F32), pltpu.VMEM(wout.shape, F32)],
        (dx, o, yc, proj, proj, woa, woc, wout), comm)


def _proj_pieces(dq, dk, dv, dcb, dcc, dcx, dgates, dflog):
    D = dgates.shape[1] // 2
    return [(dq, ATTN_W, 0), (dk, ATTN_W, 0), (dv, ATTN_W, 0), (dcb, CONV_W, 0), (dcc, CONV_W, 0), (dcx, CONV_W, 0),
            (dgates, D, 0), (dgates, D, 1), (dflog, LANES, 0)]


def _mix_proj_bwd_dx(dres, x, g, pieces, wproj_t, wf_t, tm, comm=None):
    T, D = x.shape
    n = len(pieces)
    w_blocks = [(ATTN_W, 0), (ATTN_W, 1), (ATTN_W, 2), (CONV_W, 3), (CONV_W, 4), (CONV_W, 5),
                (D, COL_GATES // D), (D, COL_GATES // D + 1)]

    def body(*refs):
        dres_ref, x_ref, g_ref = refs[:3]
        p_refs, w_refs = refs[3:3 + n], refs[3 + n:3 + 2 * n]
        dx_ref, dg_ref = refs[3 + 2 * n:]

        @pl.when(pl.program_id(0) == 0)
        def _():
            dg_ref[...] = jnp.zeros_like(dg_ref)

        dh = _dot(p_refs[0][...].astype(BF16), w_refs[0][...])
        for p_ref, w_ref in zip(p_refs[1:], w_refs[1:]):
            dh = dh + _dot(p_ref[...].astype(BF16), w_ref[...])
        xhat, inv = _rms(x_ref[...])
        dx, dg = _rms_bwd(dh, xhat, inv, g_ref[...])
        dx_ref[...] = dres_ref[...] + dx
        dg_ref[...] += dg

    row = pl.BlockSpec((tm, D), lambda i: (i, 0))
    vec = pl.BlockSpec((1, D), lambda i: (0, 0))
    p_specs = [pl.BlockSpec((tm, w), lambda i, cb=cb: (i, cb)) for _, w, cb in pieces]
    w_specs = [pl.BlockSpec((r, D), lambda i, rb=rb: (rb, 0)) for r, rb in w_blocks]
    w_specs.append(pl.BlockSpec((LANES, D), lambda i: (0, 0)))
    return _pallas(
        body, "mix_proj_bwd_dx", (T // tm,),
        [row, row, vec] + p_specs + w_specs, [row, vec],
        [jax.ShapeDtypeStruct((T, D), F32), jax.ShapeDtypeStruct((1, D), F32)], [],
        (dres, x, g, *[p for p, _, _ in pieces], *([wproj_t] * len(w_blocks)), wf_t), comm)


def _matmuls_tn(name, pieces, b, tk):
    T, N = b.shape
    nt = T // tk
    n = len(pieces)

    def body(*refs):
        a_refs, b_ref, out_refs, accs = refs[:n], refs[n], refs[n + 1:2 * n + 1], refs[2 * n + 1:]
        t = pl.program_id(0)

        @pl.when(t == 0)
        def _():
            for acc in accs:
                acc[...] = jnp.zeros_like(acc)

        bv = b_ref[...]
        for a_ref, acc in zip(a_refs, accs):
            acc[...] += _dot_tn(a_ref[...].astype(BF16), bv)

        @pl.when(t == nt - 1)
        def _():
            for out_ref, acc in zip(out_refs, accs):
                out_ref[...] = acc[...].astype(BF16)

    return pl.pallas_call(
        body, name=name, grid=(nt,),
        in_specs=[pl.BlockSpec((tk, w), lambda t, cb=cb: (t, cb)) for _, w, cb in pieces]
        + [pl.BlockSpec((tk, N), lambda t: (t, 0))],
        out_specs=[pl.BlockSpec((w, N), lambda t: (0, 0)) for _, w, _ in pieces],
        out_shape=[jax.ShapeDtypeStruct((w, N), BF16) for _, w, _ in pieces],
        scratch_shapes=[pltpu.VMEM((w, N), F32) for _, w, _ in pieces],
        compiler_params=_params(("arbitrary",)),
    )(*[a for a, _, _ in pieces], b)


TOKEN_TILE = 512
TOKEN_TILE_WIDE = 1024
ATTN_TILE = 512
SCAN_CHUNK = 256
PROJ_DX_TILE = 256


def _local_step(x, target, plan, B, S):
    T, D = x.shape
    tm = min(TOKEN_TILE, T)
    tm_fwd = min(TOKEN_TILE_WIDE, T)
    tq = min(ATTN_TILE, S)
    nq = S // tq
    ch = min(SCAN_CHUNK, S)

    def riding(kernel_name, build):
        results, brought = build(plan.rider(kernel_name))
        plan.arrived(kernel_name, brought)
        return results

    hg1, hu1, n1 = plan.ffn1_up(x, tm_fwd)
    w1 = plan.weights("ffn1")
    x1, = riding("ffn1_down", lambda comm: _ffn_down("ffn1_down", x, hg1, hu1, w1["ffn1_down"], tm_fwd, comm))
    wm = plan.weights("mix_in")
    h, proj, flog = riding("mix_proj_fwd", lambda comm: _mix_proj_fwd(
        x1, wm["mix_norm"], wm["w_proj"], wm["w_f"], tm_fwd, PROJ_W // 4, comm))
    wm.update(plan.weights("mix_out"))
    cum = _fgate_fwd(flog, wm["b_forget"], B, S, ch)
    cum_t = jnp.transpose(cum[:, :N_HEADS].reshape(B, nq, tq, N_HEADS), (0, 1, 3, 2))
    o, lse_t = riding("attn_fwd", lambda comm: _attn_fwd(proj, cum, cum_t, B, S, tq, comm))
    lse = jnp.pad(jnp.transpose(lse_t, (0, 1, 3, 2)).reshape(T, N_HEADS), ((0, 0), (0, LANES - N_HEADS)))
    yc = _conv_fwd(proj, wm["conv_w"], B, S)
    x2 = _mix_out_fwd(x1, o, yc, proj, wm["w_o_attn"], wm["w_o_conv"], wm["w_out"], tm)
    w2 = plan.weights("ffn2")
    dx3, hg2, hu2, n2, loss, d_final_norm = _ffn_fwd_loss(
        "ffn2_fwd_loss", x2, w2["ffn2_norm"], w2["ffn2_gate"], w2["ffn2_up"], w2["ffn2_down"], target, w2["final_norm"],
        tm_fwd)

    g = {"final_norm": d_final_norm}
    dx2, dhg2, dhu2, g["ffn2_norm"], df2 = _ffn_bwd_dx("ffn2_bwd_dx", dx3, x2, w2["ffn2_norm"], hg2, hu2,
                                                  w2["ffn2_gate"], w2["ffn2_up"], w2["ffn2_down"], tm_fwd)[0]
    plan.reduce("ffn2", dict(zip(("ffn2_gate", "ffn2_up", "ffn2_down"),
                                 _ffn_bwd_dw("ffn2_bwd_dw", n2, df2, hg2, hu2, dhg2, dhu2, tm_fwd)[0])))
    do, dyc, dgates, dwoa, dwoc, dwout = riding("mix_out_bwd", lambda comm: _mix_out_bwd(
        dx2, o, yc, proj, wm["w_o_attn"], wm["w_o_conv"], wm["w_out"], tm, comm))
    plan.reduce("out", dict(w_o_attn=_shard_cols(dwoa), w_o_conv=_shard_cols(dwoc), w_out=dwout.reshape(N_CHIPS, -1, D)))
    dq, dk, dv, dcq, dck = riding("attn_bwd", lambda comm: _attn_bwd(proj, o, do, lse, cum, cum_t, B, S, tq, comm))
    dcum = dcq + jnp.pad(jnp.transpose(dck, (0, 1, 3, 2)).reshape(T, N_HEADS), ((0, 0), (0, LANES - N_HEADS)))
    dflog, g["b_forget"] = _fgate_bwd(dcum, flog, wm["b_forget"], B, S, ch)
    dcb, dcc, dcx, g["conv_w"] = _conv_bwd(dyc, proj, wm["conv_w"], B, S)
    pieces = _proj_pieces(dq, dk, dv, dcb, dcc, dcx, dgates, dflog)
    dwq, dwk, dwv, dwcb, dwcc, dwcx = _matmuls_tn("mix_dw_a", pieces[:6], h, tm)
    dwga, dwgc, dwf = _matmuls_tn("mix_dw_b", pieces[6:], h, tm)
    dwin_t = jnp.concatenate([dwq, dwk, dwv, dwf[:N_HEADS], dwcb, dwcc, dwcx, dwga, dwgc], axis=0)
    plan.reduce("w_in", {"w_in": dwin_t.reshape(N_CHIPS, -1, D)})
    dx1, g["mix_norm"] = riding("mix_proj_bwd_dx", lambda comm: _mix_proj_bwd_dx(
        dx2, x1, wm["mix_norm"], pieces, wm["w_proj"], wm["w_f"], min(PROJ_DX_TILE, T), comm))
    grad_x, dhg1, dhu1, g["ffn1_norm"], df1 = _ffn_bwd_dx(
        "ffn1_bwd_dx", dx1, x, w1["ffn1_norm"], hg1, hu1, w1["ffn1_gate"], w1["ffn1_up"], w1["ffn1_down"], tm_fwd)[0]
    plan.reduce_small(g, loss)
    plan.reduce_parts("ffn1", dict(zip(("ffn1_gate", "ffn1_up", "ffn1_down"), riding(
        "ffn1_bwd_dw", lambda comm: _ffn_bwd_dw_reduced("ffn1_bwd_dw", n1, df1, hg1, hu1, dhg1, dhu1, tm, comm)))))
    return loss, grad_x, g


TRANSPOSED = ("ffn1_gate", "ffn1_up", "ffn2_gate", "ffn2_up", "w_in")
NORMS = ("ffn1_norm", "mix_norm", "ffn2_norm", "final_norm")


def _unshard_cols(a):
    return jnp.transpose(a, (1, 0, 2)).reshape(a.shape[1], N_CHIPS * a.shape[2])


def _shard_cols(a):
    return jnp.transpose(a.reshape(a.shape[0], N_CHIPS, a.shape[1] // N_CHIPS), (1, 0, 2))


def _layout_ffn(which):
    def layout(st, small):
        w = {n: st[n] for n in (which + "_gate", which + "_up", which + "_down")}
        w[which + "_norm"] = small[which + "_norm"].reshape(1, -1)
        if which == "ffn2":
            w["final_norm"] = small["final_norm"].reshape(1, -1)
        return w
    return layout


def _layout_mix_in(st, small):
    win_t = st["w_in"].reshape(-1, st["w_in"].shape[2])
    return {
        "w_proj": jnp.concatenate([win_t[:N_FORGET_COL], win_t[N_FORGET_COL + N_HEADS:]], axis=0),
        "w_f": jnp.pad(win_t[N_FORGET_COL:N_FORGET_COL + N_HEADS], ((0, LANES - N_HEADS), (0, 0))),
        "conv_w": _unshard_cols(st["conv_w"]),
        "mix_norm": small["mix_norm"].reshape(1, -1),
        "b_forget": jnp.pad(small["b_forget"].reshape(1, -1), ((0, 0), (0, LANES - N_HEADS))),
    }


def _layout_mix_out(st, small):
    return {"w_o_attn": _unshard_cols(st["w_o_attn"]), "w_o_conv": _unshard_cols(st["w_o_conv"]),
            "w_out": st["w_out"].reshape(-1, st["w_out"].shape[2])}


_LAYOUTS = {"ffn1": _layout_ffn("ffn1"), "mix_in": _layout_mix_in, "mix_out": _layout_mix_out, "ffn2": _layout_ffn("ffn2")}


ANY = pl.BlockSpec(memory_space=pl.ANY)
BIG = ("ffn1_gate", "ffn1_up", "ffn1_down", "w_in", "w_o_attn", "w_o_conv", "w_out",
       "ffn2_gate", "ffn2_up", "ffn2_down")


def _place():
    x, y, c = lax.axis_index("x"), lax.axis_index("y"), lax.axis_index("c")
    others = [(1 - x, y), (x, 1 - y), (1 - x, 1 - y)]
    return x, y, c, others


def _col_halves(cols, c):
    hc = cols // 2
    return pl.ds(pl.multiple_of(c * hc, LANES), hc), pl.ds(pl.multiple_of((1 - c) * hc, LANES), hc)


def _gather_comm(shards, conv_shard=None):
    n = len(shards)
    inputs = list(shards) + ([] if conv_shard is None else [conv_shard])

    def copies(ins, outs, sems):
        send_sems, recv_sems, pass_send, pass_recv = sems[:4]
        x, y, c, others = _place()

        def chip_copy(a, j, chip):
            mine, _ = _col_halves(ins[a].shape[1], c)
            return pltpu.make_async_remote_copy(
                src_ref=ins[a].at[:, mine], dst_ref=outs[a].at[chip, :, mine],
                send_sem=send_sems.at[3 * a + j], recv_sem=recv_sems.at[3 * a + j],
                device_id=(*others[j], c), device_id_type=MESH)

        def pass_copy(a, j, chip, half):
            return pltpu.make_async_remote_copy(
                src_ref=outs[a].at[chip, :, half], dst_ref=outs[a].at[chip, :, half],
                send_sem=pass_send.at[3 * a + j], recv_sem=pass_recv.at[3 * a + j],
                device_id=(x, y, 1 - c), device_id_type=MESH)

        def conv_copy(j, chip):
            return pltpu.make_async_remote_copy(
                src_ref=ins[n], dst_ref=outs[n].at[chip],
                send_sem=sems[4].at[j], recv_sem=sems[5].at[j],
                device_id=(*others[j], c), device_id_type=MESH)

        me = 2 * x + y
        sends = [chip_copy(a, j, me) for a in range(n) for j in range(3)]
        if conv_shard is not None:
            sends += [conv_copy(j, me) for j in range(3)]
        return c, others, sends, chip_copy, pass_copy, conv_copy

    def start(ins, outs, sems):
        for cp in copies(ins, outs, sems)[2]:
            cp.start()

    def finish(ins, outs, sems):
        c, others, sends, chip_copy, pass_copy, conv_copy = copies(ins, outs, sems)
        passed = []
        for a in range(n):
            mine, _ = _col_halves(ins[a].shape[1], c)
            for j, (ox, oy) in enumerate(others):
                chip_copy(a, j, 2 * ox + oy).wait_recv()
                passed.append(pass_copy(a, j, 2 * ox + oy, mine))
                passed[-1].start()
        for a in range(n):
            _, theirs = _col_halves(ins[a].shape[1], c)
            for j, (ox, oy) in enumerate(others):
                pass_copy(a, j, 2 * ox + oy, theirs).wait_recv()
        if conv_shard is not None:
            for j, (ox, oy) in enumerate(others):
                conv_copy(j, 2 * ox + oy).wait_recv()
        for cp in sends + passed:
            cp.wait_send()

    scratch = [pltpu.SemaphoreType.DMA((3 * n,))] * 4
    if conv_shard is not None:
        scratch += [pltpu.SemaphoreType.DMA((3,))] * 2
    return _Comm(inputs, [jax.ShapeDtypeStruct((N_CHIPS,) + s.shape, s.dtype) for s in inputs], scratch, start, finish)


def _fill_own(stacks, shards):
    chip = 2 * lax.axis_index("x") + lax.axis_index("y")
    return [lax.dynamic_update_index_in_dim(st, s, chip, 0) for st, s in zip(stacks, shards)]


def _run_comm(name, comm):
    ci, co = len(comm.inputs), len(comm.out_shape)

    def body(*refs):
        comm.start(refs[:ci], refs[ci:ci + co], refs[ci + co:])
        comm.finish(refs[:ci], refs[ci:ci + co], refs[ci + co:])

    return pl.pallas_call(body, name=name, in_specs=[ANY] * ci, out_specs=[ANY] * co, out_shape=comm.out_shape,
                          scratch_shapes=comm.scratch)(*comm.inputs)


def _sibling_exchange_comm(grads):
    n = len(grads)

    def copies(ins, outs, sems):
        x, y, c, _ = _place()
        return [pltpu.make_async_remote_copy(
            src_ref=ins[a].at[:, :, _col_halves(ins[a].shape[2], c)[1]], dst_ref=outs[a],
            send_sem=sems[0].at[a], recv_sem=sems[1].at[a],
            device_id=(x, y, 1 - c), device_id_type=MESH) for a in range(n)]

    def start(ins, outs, sems):
        for cp in copies(ins, outs, sems):
            cp.start()

    def finish(ins, outs, sems):
        for cp in copies(ins, outs, sems):
            cp.wait()

    half = lambda s: jax.ShapeDtypeStruct((s.shape[0], s.shape[1], s.shape[2] // 2), s.dtype)
    return _Comm(grads, [half(s) for s in grads], [pltpu.SemaphoreType.DMA((n,))] * 2, start, finish)


def _merge_comms(comms):
    def split(refs, count):
        out, at = [], 0
        for cm in comms:
            out.append(refs[at:at + count(cm)])
            at += count(cm)
        return out

    def parts(ins, outs, sems):
        return zip(comms, split(ins, lambda cm: len(cm.inputs)), split(outs, lambda cm: len(cm.out_shape)),
                   split(sems, lambda cm: len(cm.scratch)))

    def start(ins, outs, sems):
        for cm, i, o, s in parts(ins, outs, sems):
            cm.start(i, o, s)

    def finish(ins, outs, sems):
        for cm, i, o, s in parts(ins, outs, sems):
            cm.finish(i, o, s)

    return _Comm(sum([cm.inputs for cm in comms], []), sum([cm.out_shape for cm in comms], []),
                 sum([cm.scratch for cm in comms], []), start, finish)


def _add_halves(name, grads, recvs, core):
    n = len(grads)

    def body(core_ref, *refs):
        for g_ref, r_ref, out_ref in zip(refs[:n], refs[n:2 * n], refs[2 * n:]):
            out_ref[...] = (g_ref[...].astype(F32) + r_ref[...].astype(F32)).astype(BF16)

    half = lambda g: pl.BlockSpec((None, g.shape[1], g.shape[2] // 2), lambda k, core_ref: (k, 0, 0))
    mine = lambda g: pl.BlockSpec((None, g.shape[1], g.shape[2] // 2), lambda k, core_ref: (k, 0, core_ref[0]))
    return pl.pallas_call(
        body, name=name,
        grid_spec=pltpu.PrefetchScalarGridSpec(
            num_scalar_prefetch=1, grid=(N_CHIPS,),
            in_specs=[mine(g) for g in grads] + [half(g) for g in grads],
            out_specs=[half(g) for g in grads]),
        out_shape=[jax.ShapeDtypeStruct(r.shape, BF16) for r in recvs],
        compiler_params=_params(("arbitrary",)),
    )(core, *grads, *recvs)


def _chip_exchange_comm(parts):
    n = len(parts)

    def copies(ins, outs, sems):
        x, y, c, others = _place()
        return [pltpu.make_async_remote_copy(
            src_ref=ins[a].at[2 * ox + oy], dst_ref=outs[a].at[j],
            send_sem=sems[0].at[3 * a + j], recv_sem=sems[1].at[3 * a + j],
            device_id=(ox, oy, c), device_id_type=MESH) for a in range(n) for j, (ox, oy) in enumerate(others)]

    def start(ins, outs, sems):
        for cp in copies(ins, outs, sems):
            cp.start()

    def finish(ins, outs, sems):
        for cp in copies(ins, outs, sems):
            cp.wait()

    return _Comm(parts, [jax.ShapeDtypeStruct((3,) + s.shape[1:], s.dtype) for s in parts],
                 [pltpu.SemaphoreType.DMA((3 * n,))] * 2, start, finish)


HBM = pl.BlockSpec(memory_space=pltpu.HBM)
SEM = pl.BlockSpec(memory_space=pltpu.SEMAPHORE)


def _split_exchange_copies(parts, lands, send_sems, recv_sems):
    x, y, c, others = _place()
    return [pltpu.make_async_remote_copy(
        src_ref=parts[a].at[2 * ox + oy], dst_ref=lands[a].at[j],
        send_sem=send_sems.at[3 * a + j], recv_sem=recv_sems.at[3 * a + j],
        device_id=(ox, oy, c), device_id_type=MESH) for a in range(len(parts)) for j, (ox, oy) in enumerate(others)]


def _exchange_start(name, parts):
    n = len(parts)

    def body(*refs):
        ins, lands = refs[:n], refs[n:2 * n]
        send_sems, recv_sems, token = refs[2 * n], refs[2 * n + 1], refs[-1]
        for cp in _split_exchange_copies(ins, lands, send_sems, recv_sems):
            cp.start()
        token[...] = jnp.zeros_like(token)

    land_shape = [(3,) + p.shape[1:] for p in parts]
    outs = pl.pallas_call(
        body, name=name,
        out_shape=[pltpu.SemaphoreType.DMA((3 * n,)), pltpu.SemaphoreType.DMA((3 * n,))]
        + [pltpu.HBM(p.shape, p.dtype) for p in parts] + [pltpu.HBM(s, p.dtype) for s, p in zip(land_shape, parts)]
        + [jax.ShapeDtypeStruct((8, LANES), F32)],
        in_specs=[HBM] * (2 * n), out_specs=[SEM, SEM] + [HBM] * (2 * n) + [pl.BlockSpec(memory_space=pltpu.VMEM)],
        input_output_aliases={i: 2 + i for i in range(2 * n)},
        compiler_params=pltpu.CompilerParams(has_side_effects=pltpu.SideEffectType.DATAFLOW_SIDE_EFFECTING),
    )(*[pltpu.with_memory_space_constraint(p, pltpu.HBM) for p in parts],
      *[pltpu.with_memory_space_constraint(lax.empty(s, p.dtype), pltpu.HBM) for s, p in zip(land_shape, parts)])
    return outs[0], outs[1], list(outs[2:2 + n]), list(outs[2 + n:2 + 2 * n]), outs[-1]


def _exchange_wait(name, send_sems, recv_sems, parts, lands, after):
    n = len(parts)

    def body(*refs):
        ins, zones = refs[:n], refs[n:2 * n]
        for cp in _split_exchange_copies(ins, zones, refs[2 * n], refs[2 * n + 1]):
            cp.wait_send()
            cp.wait_recv()

    outs = pl.pallas_call(
        body, name=name,
        out_shape=[pltpu.HBM(p.shape, p.dtype) for p in parts] + [pltpu.HBM(z.shape, z.dtype) for z in lands],
        in_specs=[HBM] * (2 * n) + [SEM, SEM] + [ANY] * len(after), out_specs=[HBM] * (2 * n),
        input_output_aliases={i: i for i in range(2 * n)},
        compiler_params=pltpu.CompilerParams(has_side_effects=pltpu.SideEffectType.DATAFLOW_SIDE_EFFECTING),
    )(*parts, *lands, send_sems, recv_sems, *after)
    return list(outs[:n]), list(outs[n:])


def _sum_chips(name, owns, recvs, chip, after):
    n = len(owns)
    hc = owns[0].shape[2]
    assert all(o.shape[2] == hc for o in owns)

    def body(chip_ref, *refs):
        for own_ref, recv_ref, out_ref in zip(refs[:n], refs[n:2 * n], refs[2 * n + 1:]):
            acc = own_ref[...].astype(F32)
            for j in range(3):
                acc = acc + recv_ref[j].astype(F32)
            out_ref[...] = acc

    return pl.pallas_call(
        body, name=name,
        grid_spec=pltpu.PrefetchScalarGridSpec(
            num_scalar_prefetch=1, grid=(hc // LANES,),
            in_specs=[pl.BlockSpec((None, o.shape[1], LANES), lambda i, chip_ref: (chip_ref[0], 0, i)) for o in owns]
            + [pl.BlockSpec((3, o.shape[1], LANES), lambda i, chip_ref: (0, 0, i)) for o in owns]
            + [pl.BlockSpec((8, LANES), lambda i, chip_ref: (0, 0))],
            out_specs=[pl.BlockSpec((o.shape[1], LANES), lambda i, chip_ref: (0, i)) for o in owns]),
        out_shape=[jax.ShapeDtypeStruct((o.shape[1], hc), F32) for o in owns],
        compiler_params=_params(("arbitrary",)),
    )(chip, *owns, *recvs, after)


def _share_halves(name, halves):
    n = len(halves)

    def body(*refs):
        srcs, dsts = refs[:n], refs[n:2 * n]
        send_sems, recv_sems = refs[2 * n:]
        x, y, c, _ = _place()
        copies = [pltpu.make_async_remote_copy(
            src_ref=srcs[a], dst_ref=dsts[a], send_sem=send_sems.at[a], recv_sem=recv_sems.at[a],
            device_id=(x, y, 1 - c), device_id_type=MESH) for a in range(n)]
        for cp in copies:
            cp.start()
        for cp in copies:
            cp.wait()

    return pl.pallas_call(
        body, name=name,
        in_specs=[ANY] * n, out_specs=[ANY] * n,
        out_shape=[jax.ShapeDtypeStruct(s.shape, s.dtype) for s in halves],
        scratch_shapes=[pltpu.SemaphoreType.DMA((n,)), pltpu.SemaphoreType.DMA((n,))],
    )(*halves)


def _small_gather_comm(part):
    def copies(ins, outs, sems):
        x, y, c, _ = _place()
        me = 4 * x + 2 * y + c
        both = []
        for d in range(1, N_DEV):
            px, py, pc = (1 - x if d & 4 else x, 1 - y if d & 2 else y, 1 - c if d & 1 else c)
            send = pltpu.make_async_remote_copy(
                src_ref=ins[0], dst_ref=outs[0].at[me], send_sem=sems[0].at[d - 1], recv_sem=sems[1].at[d - 1],
                device_id=(px, py, pc), device_id_type=MESH)
            recv = pltpu.make_async_remote_copy(
                src_ref=ins[0], dst_ref=outs[0].at[4 * px + 2 * py + pc], send_sem=sems[0].at[d - 1],
                recv_sem=sems[1].at[d - 1], device_id=(px, py, pc), device_id_type=MESH)
            both.append((send, recv))
        return both

    def start(ins, outs, sems):
        for send, _ in copies(ins, outs, sems):
            send.start()

    def finish(ins, outs, sems):
        for send, recv in copies(ins, outs, sems):
            recv.wait_recv()
            send.wait_send()

    return _Comm([part], [jax.ShapeDtypeStruct((N_DEV,) + part.shape, F32)],
                 [pltpu.SemaphoreType.DMA((N_DEV - 1,))] * 2, start, finish)


def _sum_devices(parts):
    def body(p_ref, out_ref):
        acc = p_ref[0]
        for k in range(1, N_DEV):
            acc = acc + p_ref[k]
        out_ref[...] = acc

    return pl.pallas_call(
        body, name="sum_devices", grid=(1,),
        in_specs=[pl.BlockSpec(parts.shape, lambda i: (0, 0, 0))],
        out_specs=pl.BlockSpec(parts.shape[1:], lambda i: (0, 0)),
        out_shape=jax.ShapeDtypeStruct(parts.shape[1:], F32),
        compiler_params=_params(("arbitrary",)),
    )(parts)


def _adam_update(w, g, m, v):
    nm = ADAM_B1 * m + (1.0 - ADAM_B1) * g
    nv = ADAM_B2 * v + (1.0 - ADAM_B2) * (g * g)
    m_hat = nm * (1.0 / (1.0 - ADAM_B1 ** ADAM_STEP))
    v_hat = nv * (1.0 / (1.0 - ADAM_B2 ** ADAM_STEP))
    return -ADAM_LR * (m_hat / (jnp.sqrt(v_hat) + ADAM_EPS) + ADAM_WD * w), nm, nv


def _adamw(name, w, g, m, v):
    def body(w_ref, g_ref, m_ref, v_ref, d_ref, nm_ref, nv_ref):
        d_ref[...], nm_ref[...], nv_ref[...] = _adam_update(w_ref[...], g_ref[...], m_ref[...], v_ref[...])

    spec = pl.BlockSpec(w.shape, lambda i: (0, 0))
    out = jax.ShapeDtypeStruct(w.shape, F32)
    return pl.pallas_call(
        body, name=name, grid=(1,),
        in_specs=[spec] * 4, out_specs=[spec] * 3, out_shape=[out] * 3,
        compiler_params=_params(("arbitrary",)),
    )(w, g, m, v)


def _adamw_halves(name, ws, mines, theirs, ms, vs, core):
    n = len(ws)
    cols = ws[0].shape[1]
    assert all(w.shape[1] == cols for w in ws)
    hc = cols // 2
    tc = LANES if n > 1 else min(256, hc)
    nt = hc // tc

    def body(core_ref, *refs):
        ins, outs = refs[:5 * n], refs[5 * n:]
        for a in range(n):
            w_ref, mine_ref, theirs_ref, m_ref, v_ref = [ins[j * n + a] for j in range(5)]
            g_ref, d_ref, nm_ref, nv_ref = outs[4 * a:4 * a + 4]
            gv = jnp.where(pl.program_id(0) == core_ref[0], mine_ref[...], theirs_ref[...])
            g_ref[...] = gv
            d_ref[...], nm_ref[...], nv_ref[...] = _adam_update(w_ref[...], gv, m_ref[...], v_ref[...])

    whole = lambda w: pl.BlockSpec((w.shape[0], tc), lambda h, i, core_ref: (0, h * nt + i))
    mine_spec = lambda w: pl.BlockSpec((w.shape[0], tc), lambda h, i, core_ref: (0, jnp.where(h == core_ref[0], i, 0)))
    theirs_spec = lambda w: pl.BlockSpec((w.shape[0], tc), lambda h, i, core_ref: (0, jnp.where(h == core_ref[0], 0, i)))
    outs = pl.pallas_call(
        body, name=name,
        grid_spec=pltpu.PrefetchScalarGridSpec(
            num_scalar_prefetch=1, grid=(2, nt),
            in_specs=[whole(w) for w in ws] + [mine_spec(w) for w in ws] + [theirs_spec(w) for w in ws]
            + [whole(w) for w in ws] * 2,
            out_specs=[whole(w) for w in ws for _ in range(4)]),
        out_shape=[jax.ShapeDtypeStruct(w.shape, F32) for w in ws for _ in range(4)],
        compiler_params=_params(("arbitrary", "arbitrary")),
    )(core, *ws, *mines, *theirs, *ms, *vs)
    return [outs[4 * a:4 * a + 4] for a in range(n)]


WEIGHTS = ("ffn1_norm", "ffn1_gate", "ffn1_up", "ffn1_down", "mix_norm", "w_in", "b_forget", "conv_w",
           "w_o_attn", "w_o_conv", "w_out", "ffn2_norm", "ffn2_gate", "ffn2_up", "ffn2_down", "final_norm")
VEC_ROWS = 8


def _pack_small(t, conv_rows):
    conv = t["conv_w"]
    parts = [t[n].reshape(VEC_ROWS, LANES) for n in NORMS]
    parts.append(jnp.pad(conv, ((0, conv_rows - conv.shape[0]), (0, 0))))
    parts.append(jnp.pad(t["b_forget"].reshape(1, N_HEADS), ((0, 7), (0, LANES - N_HEADS))))
    return jnp.concatenate(parts, axis=0)


def _unpack_small(p, conv_rows):
    out = {n: p[VEC_ROWS * i:VEC_ROWS * (i + 1)].reshape(-1) for i, n in enumerate(NORMS)}
    base = VEC_ROWS * len(NORMS)
    out["conv_w"] = p[base:base + 3]
    out["b_forget"] = p[base + conv_rows, :N_HEADS]
    return out


def _travel(name, a):
    return a.T if name in TRANSPOSED else a


GATHER_FIRST = ("ffn1_gate", "ffn1_up")
GATHER_RIDES = {"ffn1_up": ("ffn1_down",), "ffn1_down": ("w_in",), "mix_proj_fwd": ("w_o_attn", "w_o_conv", "w_out"),
                "attn_fwd": ("ffn2_gate", "ffn2_up", "ffn2_down")}
SIBLING_RIDES = {"ffn2": "mix_out_bwd", "out": None, "w_in": "mix_proj_bwd_dx", "ffn1": None}
CHIP_RIDES = {"ffn2": "attn_bwd", "out": "attn_bwd", "w_in": "ffn1_bwd_dw", "ffn1": None}
SMALL_RIDE = "ffn1_bwd_dw"


class _MeshPlan:
    def __init__(self, wts, core):
        self.small, self.core = wts, core
        self.shards = {n: wts[n].astype(BF16) for n in BIG}
        self.chip_part, self.from_chips, self.rides = {}, {}, {}
        self.stacks = {}
        conv_shard = jnp.pad(wts["conv_w"], ((0, 8 - wts["conv_w"].shape[0]), (0, 0)))
        for kernel_name, names in GATHER_RIDES.items():
            mine = [self.shards[n] for n in names]
            conv = conv_shard if kernel_name == "ffn1_up" else None
            names = names + (("conv_w",) if conv is not None else ())
            mine = mine + ([conv] if conv is not None else [])
            self._ride(kernel_name, _gather_comm(mine[:len(mine) - (conv is not None)], conv),
                       lambda got, names=names, mine=mine: self.stacks.update(zip(names, _fill_own(got, mine))))

    def weights(self, group):
        return _LAYOUTS[group](self.stacks, self.small)

    def ffn1_up(self, x, tm):
        px, py = lax.axis_index("x"), lax.axis_index("y")
        order = jnp.stack([2 * px + py, 2 * (1 - px) + py, 2 * px + (1 - py), 2 * (1 - px) + (1 - py)]).astype(jnp.int32)
        own = [self.shards[n] for n in GATHER_FIRST]
        (hg, hu, n, sg, su), brought = _ffn_up_gather("ffn1_up", x, self.small["ffn1_norm"].reshape(1, -1), *own, order,
                                                     tm, self.rider("ffn1_up"))
        self.stacks.update(zip(GATHER_FIRST, _fill_own([sg, su], own)))
        self.arrived("ffn1_up", brought)
        return hg, hu, n

    def _ride(self, kernel_name, comm, then):
        self.rides.setdefault(kernel_name, []).append((comm, then))

    def rider(self, kernel_name):
        comms = [comm for comm, _ in self.rides.get(kernel_name, [])]
        return _merge_comms(comms) if comms else None

    def arrived(self, kernel_name, results):
        for comm, then in self.rides.pop(kernel_name, []):
            then(results[:len(comm.out_shape)])
            results = results[len(comm.out_shape):]

    def reduce(self, group, grads):
        names = tuple(grads)
        mine = [grads[n] for n in names]

        def with_sibling(from_sibling):
            parts = _add_halves("add_halves_" + group, mine, list(from_sibling), self.core)
            self.chip_part.update(zip(names, parts))
            if CHIP_RIDES[group] is None:
                self.last = (names, _exchange_start("exchange_start_" + group, parts))
            else:
                self._ride(CHIP_RIDES[group], _chip_exchange_comm(parts),
                           lambda got: self.from_chips.update(zip(names, got)))

        if SIBLING_RIDES[group] is None:
            with_sibling(_run_comm("sibling_exchange_" + group, _sibling_exchange_comm(mine)))
        else:
            self._ride(SIBLING_RIDES[group], _sibling_exchange_comm(mine), with_sibling)

    def reduce_parts(self, group, parts):
        self.chip_part.update(parts)
        self.last = (tuple(parts), _exchange_start("exchange_start_" + group, list(parts.values())))

    def reduce_small(self, gs, loss):
        conv_all = _shard_cols(gs["conv_w"]).reshape(N_CHIPS * 8, LANES)
        part = _pack_small({**{n: gs[n] for n in NORMS}, "conv_w": conv_all, "b_forget": gs["b_forget"][0, :N_HEADS]},
                           N_CHIPS * 8)
        part = jnp.concatenate([part, jnp.broadcast_to(loss, (8, LANES))], axis=0)
        me = 4 * lax.axis_index("x") + 2 * lax.axis_index("y") + lax.axis_index("c")

        def landed(got):
            self.small_parts = lax.dynamic_update_index_in_dim(got[0], part, me, 0)

        self._ride(SMALL_RIDE, _small_gather_comm(part), landed)


def kernel(x, ffn1_norm, ffn1_gate, ffn1_up, ffn1_down, mix_norm, w_in, b_forget, conv_w, w_o_attn, w_o_conv, w_out, ffn2_norm, ffn2_gate, ffn2_up, ffn2_down, final_norm, loss_target, m_ffn1_norm, m_ffn1_gate, m_ffn1_up, m_ffn1_down, m_mix_norm, m_w_in, m_b_forget, m_conv_w, m_w_o_attn, m_w_o_conv, m_w_out, m_ffn2_norm, m_ffn2_gate, m_ffn2_up, m_ffn2_down, m_final_norm, v_ffn1_norm, v_ffn1_gate, v_ffn1_up, v_ffn1_down, v_mix_norm, v_w_in, v_b_forget, v_conv_w, v_w_o_attn, v_w_o_conv, v_w_out, v_ffn2_norm, v_ffn2_gate, v_ffn2_up, v_ffn2_down, v_final_norm):
    given = dict(locals())
    wts = {n: _travel(n, given[n]) for n in WEIGHTS}
    mom = {n: _travel(n, given["m_" + n]) for n in WEIGHTS}
    var = {n: _travel(n, given["v_" + n]) for n in WEIGHTS}
    B, S, D = x.shape
    chip = 2 * lax.axis_index("x") + lax.axis_index("y")
    chip1 = chip.astype(jnp.int32).reshape(1)
    core = lax.axis_index("c").astype(jnp.int32).reshape(1)

    plan = _MeshPlan(wts, core)
    loss, grad_x, gs = _local_step(x.reshape(B * S, D), loss_target.reshape(B * S, D), plan, B, S)

    last_names, (send_sems, recv_sems, parts_thru, lands, token) = plan.last
    delta, new_m, new_v, grads = {}, {}, {}, {}

    def finish(tag, names):
        by_cols = {}
        for n in names:
            by_cols.setdefault(wts[n].shape[1], []).append(n)
        mine = {}
        for cols, ns in by_cols.items():
            mine.update(zip(ns, _sum_chips("sum_chips_%s_%d" % (tag, cols), [plan.chip_part[n] for n in ns],
                                           [plan.from_chips[n] for n in ns], chip1, token)))
        theirs = dict(zip(names, _share_halves("share_halves_" + tag, [mine[n] for n in names])))
        raw = []
        for cols, ns in by_cols.items():
            outs = _adamw_halves("adamw_%s_%d" % (tag, cols), [wts[n] for n in ns], [mine[n] for n in ns],
                                 [theirs[n] for n in ns], [mom[n] for n in ns], [var[n] for n in ns], core)
            for n, per in zip(ns, outs):
                raw.append(per[-1])
                grads[n], delta[n], new_m[n], new_v[n] = [_travel(n, o) for o in per]
        return raw

    small_sum = _sum_devices(plan.small_parts)
    base = VEC_ROWS * len(NORMS)
    loss_row = small_sum.shape[0] - 8
    small_grads = _unpack_small(small_sum, N_CHIPS * 8)
    small_grads["conv_w"] = lax.dynamic_slice_in_dim(small_sum[base:base + N_CHIPS * 8], chip * 8, 8, axis=0)[:3]
    packs = [_pack_small(t, 8) for t in (wts, small_grads, mom, var)]
    small_out = _adamw("adamw_small", *packs)

    done = finish("early", [n for n in BIG if n not in last_names])
    parts_back, got = _exchange_wait("exchange_wait", send_sems, recv_sems, parts_thru, lands, done + list(small_out))
    plan.chip_part.update(zip(last_names, parts_back))
    plan.from_chips.update(zip(last_names, got))
    finish("last", last_names)
    grads.update(small_grads)
    for out, p in zip((delta, new_m, new_v), small_out):
        out.update(_unpack_small(p, 8))

    return (small_sum[loss_row, 0], grad_x.reshape(B, S, D), *[grads[n] for n in WEIGHTS], *[delta[n] for n in WEIGHTS],
            *[new_m[n] for n in WEIGHTS], *[new_v[n] for n in WEIGHTS])
```

```python
import functools
import math

import jax
import jax.numpy as jnp
from jax import lax
from jax.experimental import pallas as pl
from jax.experimental.pallas import tpu as pltpu

F32 = jnp.float32
BF16 = jnp.bfloat16
MESH = pl.DeviceIdType.MESH

N_CHIPS = 4
N_DEV = 8
N_HEADS = 8
HEAD_DIM = 64
HEAD_PAIRS = N_HEADS // 2
ATTN_W = N_HEADS * HEAD_DIM
CONV_W = 512
RMS_EPS = 1e-6
FFN_RES = 0.5
LANES = 128
VMEM_LIMIT = 56 * 1024 * 1024
ROW_BLOCK = 256

ADAM_LR = 0.001
ADAM_B1 = 0.9
ADAM_B2 = 0.999
ADAM_EPS = 1e-08
ADAM_WD = 0.01
ADAM_STEP = 10

PROJ_W = 3 * ATTN_W + 3 * CONV_W + 2 * 1024
COL_CB, COL_CC, COL_CX = 3 * ATTN_W, 3 * ATTN_W + CONV_W, 3 * ATTN_W + 2 * CONV_W
COL_GATES = 3 * ATTN_W + 3 * CONV_W
N_FORGET_COL = 3 * ATTN_W


def _params(sem=None, vmem=VMEM_LIMIT):
    return pltpu.CompilerParams(dimension_semantics=sem, vmem_limit_bytes=vmem)


def _dot(a, b):
    return lax.dot_general(a, b, (((1,), (0,)), ((), ())), preferred_element_type=F32)


def _dot_nt(a, b):
    return lax.dot_general(a, b, (((1,), (1,)), ((), ())), preferred_element_type=F32)


def _dot_tn(a, b):
    return lax.dot_general(a, b, (((0,), (0,)), ((), ())), preferred_element_type=F32)


def _sigmoid(x):
    return 1.0 / (1.0 + jnp.exp(-x))


def _rms(xv):
    inv = lax.rsqrt(jnp.mean(xv * xv, axis=-1, keepdims=True) + RMS_EPS)
    return xv * inv, inv


class _Comm:
    def __init__(self, inputs, out_shape, scratch, start, finish):
        self.inputs, self.out_shape, self.scratch = list(inputs), list(out_shape), list(scratch)
        self.start, self.finish = start, finish


def _pallas(body, name, grid, in_specs, out_specs, out_shape, scratch, args, comm=None):
    sem = ("arbitrary",) * len(grid)
    if comm is None:
        outs = pl.pallas_call(body, name=name, grid=grid, in_specs=in_specs, out_specs=out_specs,
                              out_shape=out_shape, scratch_shapes=scratch, compiler_params=_params(sem))(*args)
        return list(outs), []
    n_in, n_out, n_scr = len(in_specs), len(out_specs), len(scratch)
    ci, co = len(comm.inputs), len(comm.out_shape)

    def riding(*refs):
        ins, refs = refs[:n_in], refs[n_in:]
        cins, refs = refs[:ci], refs[ci:]
        outs, refs = refs[:n_out], refs[n_out:]
        couts, refs = refs[:co], refs[co:]
        scr, sems = refs[:n_scr], refs[n_scr:]
        ids = [pl.program_id(d) for d in range(len(grid))]
        first = functools.reduce(lambda a, b: a & b, [i == 0 for i in ids])
        last = functools.reduce(lambda a, b: a & b, [i == g - 1 for i, g in zip(ids, grid)])

        @pl.when(first)
        def _():
            comm.start(cins, couts, sems)

        body(*ins, *outs, *scr)

        @pl.when(last)
        def _():
            comm.finish(cins, couts, sems)

    any_spec = pl.BlockSpec(memory_space=pl.ANY)
    outs = pl.pallas_call(
        riding, name=name, grid=grid,
        in_specs=list(in_specs) + [any_spec] * ci, out_specs=list(out_specs) + [any_spec] * co,
        out_shape=list(out_shape) + comm.out_shape, scratch_shapes=list(scratch) + comm.scratch,
        compiler_params=_params(sem))(*args, *comm.inputs)
    return list(outs[:n_out]), list(outs[n_out:])


def _rms_bwd(dn, xhat, inv, g):
    dxhat = dn * g
    dx = inv * (dxhat - xhat * jnp.mean(dxhat * xhat, axis=-1, keepdims=True))
    return dx, jnp.sum(dn * xhat, axis=0, keepdims=True)


def _ffn_fwd_loss(name, x, g, wgt, wut, wd, target, gf, tm):
    T, D = x.shape
    K, Fs, _ = wgt.shape

    def body(x_ref, g_ref, wg_ref, wu_ref, wd_ref, t_ref, gf_ref,
             dx_ref, hg_ref, hu_ref, n_ref, loss_ref, dgf_ref, acc_scr):
        i, k = pl.program_id(0), pl.program_id(1)

        @pl.when(k == 0)
        def _():
            xhat, _ = _rms(x_ref[...])
            n_ref[...] = (xhat * g_ref[...]).astype(BF16)
            acc_scr[...] = jnp.zeros_like(acc_scr)

        @pl.when((k == 0) & (i == 0))
        def _():
            loss_ref[...] = jnp.zeros_like(loss_ref)
            dgf_ref[...] = jnp.zeros_like(dgf_ref)

        n = n_ref[...]
        hg = _dot_nt(n, wg_ref[...])
        hu = _dot_nt(n, wu_ref[...])
        hg_ref[...] = hg.astype(BF16)
        hu_ref[...] = hu.astype(BF16)
        act = (hg * _sigmoid(hg) * hu).astype(BF16)
        acc_scr[...] += _dot(act, wd_ref[...])

        @pl.when(k == K - 1)
        def _():
            gfv = gf_ref[...]
            for r0 in range(0, tm, ROW_BLOCK):
                rows = slice(r0, r0 + ROW_BLOCK)
                xhat, inv = _rms(x_ref[rows, :] + FFN_RES * acc_scr[rows, :])
                err = xhat * gfv - t_ref[rows, :]
                loss_ref[...] += 0.5 * jnp.sum(jnp.sum(err * err, axis=1, keepdims=True), axis=0, keepdims=True) / D
                dx, dg = _rms_bwd(err * (1.0 / D), xhat, inv, gfv)
                dx_ref[rows, :] = dx
                dgf_ref[...] += dg

    w_spec = pl.BlockSpec((None, Fs, D), lambda i, k: (k, 0, 0))
    act_spec = pl.BlockSpec((None, tm, Fs), lambda i, k: (k, i, 0))
    row = pl.BlockSpec((tm, D), lambda i, k: (i, 0))
    vec = pl.BlockSpec((1, D), lambda i, k: (0, 0))
    return _pallas(
        body, name, (T // tm, K),
        [row, vec, w_spec, w_spec, w_spec, row, vec],
        [row, act_spec, act_spec, row, pl.BlockSpec((1, LANES), lambda i, k: (0, 0)), vec],
        [jax.ShapeDtypeStruct((T, D), F32), jax.ShapeDtypeStruct((K, T, Fs), BF16),
         jax.ShapeDtypeStruct((K, T, Fs), BF16), jax.ShapeDtypeStruct((T, D), BF16),
         jax.ShapeDtypeStruct((1, LANES), F32), jax.ShapeDtypeStruct((1, D), F32)],
        [pltpu.VMEM((tm, D), F32)],
        (x, g, wgt, wut, wd, target, gf))[0]


def _ffn_up_gather(name, x, g, wg_own, wu_own, order, tm, comm=None):
    T, D = x.shape
    Fs = wg_own.shape[0]
    nt = T // tm
    ci, co = (len(comm.inputs), len(comm.out_shape)) if comm is not None else (0, 0)

    def body(order_ref, x_ref, g_ref, wgo_ref, wuo_ref, *rest):
        cins, rest = rest[:ci], rest[ci:]
        (hg_ref, hu_ref, n_ref, sg_ref, su_ref), rest = rest[:5], rest[5:]
        couts, rest = rest[:co], rest[co:]
        (n_all, wbuf, send_sems, recv_sems, pass_send, pass_recv, load_sems), csems = rest[:7], rest[7:]
        k, i = pl.program_id(0), pl.program_id(1)
        x_pos, y_pos, c, others = _place()
        me = 2 * x_pos + y_pos
        owns, stacks = (wgo_ref, wuo_ref), (sg_ref, su_ref)
        mine, theirs = _col_halves(D, c)

        def chip_copy(a, j, chip):
            return pltpu.make_async_remote_copy(
                src_ref=owns[a].at[:, mine], dst_ref=stacks[a].at[chip, :, mine],
                send_sem=send_sems.at[3 * a + j], recv_sem=recv_sems.at[3 * a + j],
                device_id=(*others[j], c), device_id_type=MESH)

        def pass_copy(a, j, chip, half):
            return pltpu.make_async_remote_copy(
                src_ref=stacks[a].at[chip, :, half], dst_ref=stacks[a].at[chip, :, half],
                send_sem=pass_send.at[3 * a + j], recv_sem=pass_recv.at[3 * a + j],
                device_id=(x_pos, y_pos, 1 - c), device_id_type=MESH)

        @pl.when((k == 0) & (i == 0))
        def _():
            for a in range(2):
                for j in range(3):
                    chip_copy(a, j, me).start()
            if comm is not None:
                comm.start(cins, couts, csems)

        def bring(j):
            ox, oy = others[j]
            chip = 2 * ox + oy
            for a in range(2):
                chip_copy(a, j, chip).wait_recv()
            for a in range(2):
                pass_copy(a, j, chip, mine).start()
            for a in range(2):
                pass_copy(a, j, chip, theirs).wait_recv()
            loads = [pltpu.make_async_copy(stacks[a].at[chip], wbuf.at[j % 2, a], load_sems.at[2 * (j % 2) + a])
                     for a in range(2)]
            for cp in loads:
                cp.start()
            for cp in loads:
                cp.wait()

        @pl.when((k == 1) & (i == 0))
        def _():
            bring(0)
            bring(1)

        @pl.when((k == 2) & (i == nt - 1))
        def _():
            bring(2)

        rows = pl.ds(pl.multiple_of(i * tm, tm), tm)

        @pl.when(k == 0)
        def _():
            xhat, _ = _rms(x_ref[...])
            n = (xhat * g_ref[...]).astype(BF16)
            n_ref[...] = n
            n_all[rows, :] = n
            hg_ref[...] = _dot_nt(n, wgo_ref[...]).astype(BF16)
            hu_ref[...] = _dot_nt(n, wuo_ref[...]).astype(BF16)

        @pl.when(k > 0)
        def _():
            n = n_all[rows, :]
            slot = (k - 1) % 2
            hg_ref[...] = _dot_nt(n, wbuf[slot, 0]).astype(BF16)
            hu_ref[...] = _dot_nt(n, wbuf[slot, 1]).astype(BF16)

        @pl.when((k == N_CHIPS - 1) & (i == nt - 1))
        def _():
            for a in range(2):
                for j, (ox, oy) in enumerate(others):
                    chip_copy(a, j, me).wait_send()
                    pass_copy(a, j, 2 * ox + oy, mine).wait_send()
            if comm is not None:
                comm.finish(cins, couts, csems)

    any_spec = pl.BlockSpec(memory_space=pl.ANY)
    first_pass = lambda k, i, order_ref: (jnp.where(k == 0, i, nt - 1), 0)
    whole = pl.BlockSpec((Fs, D), lambda k, i, order_ref: (0, 0))
    act_spec = pl.BlockSpec((None, tm, Fs), lambda k, i, order_ref: (order_ref[k], i, 0))
    stack = jax.ShapeDtypeStruct((N_CHIPS, Fs, D), BF16)
    outs = pl.pallas_call(
        body, name=name,
        grid_spec=pltpu.PrefetchScalarGridSpec(
            num_scalar_prefetch=1, grid=(N_CHIPS, nt),
            in_specs=[pl.BlockSpec((tm, D), first_pass), pl.BlockSpec((1, D), lambda k, i, order_ref: (0, 0)),
                      whole, whole] + [any_spec] * ci,
            out_specs=[act_spec, act_spec, pl.BlockSpec((tm, D), first_pass), any_spec, any_spec] + [any_spec] * co,
            scratch_shapes=[pltpu.VMEM((T, D), BF16), pltpu.VMEM((2, 2, Fs, D), BF16)]
            + [pltpu.SemaphoreType.DMA((6,))] * 4 + [pltpu.SemaphoreType.DMA((4,))]
            + (comm.scratch if comm is not None else [])),
        out_shape=[jax.ShapeDtypeStruct((N_CHIPS, T, Fs), BF16), jax.ShapeDtypeStruct((N_CHIPS, T, Fs), BF16),
                   jax.ShapeDtypeStruct((T, D), BF16), stack, stack] + (comm.out_shape if comm is not None else []),
        compiler_params=_params(("arbitrary", "arbitrary")),
    )(order, x, g, wg_own, wu_own, *(comm.inputs if comm is not None else []))
    return list(outs[:5]), list(outs[5:])


def _ffn_down(name, x, hg, hu, wd, tm, comm=None):
    T, D = x.shape
    K, Fs, _ = wd.shape

    def body(x_ref, hg_ref, hu_ref, wd_ref, out_ref, acc_scr):
        k = pl.program_id(1)

        @pl.when(k == 0)
        def _():
            acc_scr[...] = jnp.zeros_like(acc_scr)

        hgv = hg_ref[...].astype(F32)
        act = (hgv * _sigmoid(hgv) * hu_ref[...].astype(F32)).astype(BF16)
        acc_scr[...] += _dot(act, wd_ref[...])

        @pl.when(k == K - 1)
        def _():
            out_ref[...] = x_ref[...] + FFN_RES * acc_scr[...]

    act_spec = pl.BlockSpec((None, tm, Fs), lambda i, k: (k, i, 0))
    row = pl.BlockSpec((tm, D), lambda i, k: (i, 0))
    return _pallas(
        body, name, (T // tm, K),
        [row, act_spec, act_spec, pl.BlockSpec((None, Fs, D), lambda i, k: (k, 0, 0))],
        [row], [jax.ShapeDtypeStruct((T, D), F32)], [pltpu.VMEM((tm, D), F32)],
        (x, hg, hu, wd), comm)


def _ffn_bwd_dx(name, dout, x, g, hg, hu, wgt, wut, wd, tm, comm=None):
    T, D = x.shape
    K, Fs, _ = wgt.shape

    def body(dout_ref, x_ref, g_ref, hg_ref, hu_ref, wg_ref, wu_ref, wd_ref,
             dx_ref, dhg_ref, dhu_ref, dg_ref, df_ref, dn_scr):
        i, k = pl.program_id(0), pl.program_id(1)

        @pl.when(k == 0)
        def _():
            df_ref[...] = (FFN_RES * dout_ref[...]).astype(BF16)
            dn_scr[...] = jnp.zeros_like(dn_scr)

        @pl.when((k == 0) & (i == 0))
        def _():
            dg_ref[...] = jnp.zeros_like(dg_ref)

        for r0 in range(0, tm, ROW_BLOCK):
            rows = slice(r0, r0 + ROW_BLOCK)
            dact = _dot_nt(df_ref[rows, :], wd_ref[...])
            hgv = hg_ref[rows, :].astype(F32)
            huv = hu_ref[rows, :].astype(F32)
            s = _sigmoid(hgv)
            dhu = (dact * (hgv * s)).astype(BF16)
            dhg = (dact * huv * (s * (1.0 + hgv * (1.0 - s)))).astype(BF16)
            dhg_ref[rows, :] = dhg
            dhu_ref[rows, :] = dhu
            dn_scr[rows, :] += _dot(dhg, wg_ref[...]) + _dot(dhu, wu_ref[...])

        @pl.when(k == K - 1)
        def _():
            xhat, inv = _rms(x_ref[...])
            dx, dg = _rms_bwd(dn_scr[...], xhat, inv, g_ref[...])
            dx_ref[...] = dout_ref[...] + dx
            dg_ref[...] += dg

    w_spec = pl.BlockSpec((None, Fs, D), lambda i, k: (k, 0, 0))
    act_spec = pl.BlockSpec((None, tm, Fs), lambda i, k: (k, i, 0))
    row = pl.BlockSpec((tm, D), lambda i, k: (i, 0))
    row_once = pl.BlockSpec((tm, D), lambda i, k: (i, 0), pipeline_mode=pl.Buffered(1))
    vec = pl.BlockSpec((1, D), lambda i, k: (0, 0))
    return _pallas(
        body, name, (T // tm, K),
        [row, row_once, vec, act_spec, act_spec, w_spec, w_spec, w_spec],
        [row_once, act_spec, act_spec, vec, row],
        [jax.ShapeDtypeStruct((T, D), F32), jax.ShapeDtypeStruct((K, T, Fs), BF16),
         jax.ShapeDtypeStruct((K, T, Fs), BF16), jax.ShapeDtypeStruct((1, D), F32),
         jax.ShapeDtypeStruct((T, D), BF16)],
        [pltpu.VMEM((tm, D), F32)],
        (dout, x, g, hg, hu, wgt, wut, wd), comm)


def _ffn_bwd_dw(name, n, df, hg, hu, dhg, dhu, tk, comm=None):
    T, D = n.shape
    K, _, Fs = hg.shape
    nt = T // tk

    def body(n_ref, df_ref, hg_ref, hu_ref, dhg_ref, dhu_ref, dwg_ref, dwu_ref, dwd_ref, accg, accu, accd):
        t = pl.program_id(1)

        @pl.when(t == 0)
        def _():
            accg[...] = jnp.zeros_like(accg)
            accu[...] = jnp.zeros_like(accu)
            accd[...] = jnp.zeros_like(accd)

        nv = n_ref[...]
        hgv = hg_ref[...].astype(F32)
        act = (hgv * _sigmoid(hgv) * hu_ref[...].astype(F32)).astype(BF16)
        accg[...] += _dot_tn(dhg_ref[...], nv)
        accu[...] += _dot_tn(dhu_ref[...], nv)
        accd[...] += _dot_tn(act, df_ref[...])

        @pl.when(t == nt - 1)
        def _():
            dwg_ref[...] = accg[...].astype(BF16)
            dwu_ref[...] = accu[...].astype(BF16)
            dwd_ref[...] = accd[...].astype(BF16)

    act_spec = pl.BlockSpec((None, tk, Fs), lambda k, t: (k, t, 0))
    w_spec = pl.BlockSpec((None, Fs, D), lambda k, t: (k, 0, 0))
    row = pl.BlockSpec((tk, D), lambda k, t: (t, 0))
    return _pallas(
        body, name, (K, nt),
        [row, row, act_spec, act_spec, act_spec, act_spec],
        [w_spec, w_spec, w_spec],
        [jax.ShapeDtypeStruct((K, Fs, D), BF16)] * 3,
        [pltpu.VMEM((Fs, D), F32)] * 3,
        (n, df, hg, hu, dhg, dhu), comm)


def _ffn_bwd_dw_reduced(name, n, df, hg, hu, dhg, dhu, tk, comm=None):
    T, D = n.shape
    K, _, Fs = hg.shape
    nt = T // tk
    hc = D // 2
    assert nt >= 2, "a pass's swap is finished at the second step of the next pass"

    def body(n_ref, df_ref, hg_ref, hu_ref, dhg_ref, dhu_ref, pg_ref, pu_ref, pd_ref,
             accg, accu, accd, stage, own, land, send_sems, recv_sems):
        k, t = pl.program_id(0), pl.program_id(1)
        x, y, c, _ = _place()
        accs, outs = (accg, accu, accd), (pg_ref, pu_ref, pd_ref)

        def swap(j, a):
            return pltpu.make_async_remote_copy(
                src_ref=stage.at[a], dst_ref=land.at[j % 2, a],
                send_sem=send_sems.at[3 * j + a], recv_sem=recv_sems.at[3 * j + a],
                device_id=(x, y, 1 - c), device_id_type=MESH)

        def finish(j):
            for a in range(3):
                swap(j, a).wait_recv()
                outs[a][j] = (own[a] + land[j % 2, a].astype(F32)).astype(BF16)

        @pl.when(t == 0)
        def _():
            for acc in accs:
                acc[...] = jnp.zeros_like(acc)

        nv = n_ref[...]
        hgv = hg_ref[...].astype(F32)
        act = (hgv * _sigmoid(hgv) * hu_ref[...].astype(F32)).astype(BF16)
        accg[...] += _dot_tn(dhg_ref[...], nv)
        accu[...] += _dot_tn(dhu_ref[...], nv)
        accd[...] += _dot_tn(act, df_ref[...])

        for j in range(K - 1):
            @pl.when((k == j + 1) & (t == 1))
            def _(j=j):
                finish(j)

        @pl.when(t == nt - 1)
        def _():
            for j in range(K - 1):
                @pl.when(k == j + 1)
                def _(j=j):
                    for a in range(3):
                        swap(j, a).wait_send()
            for a in range(3):
                @pl.when(c == 0)
                def _(a=a):
                    own[a] = accs[a][:, :hc]
                    stage[a] = accs[a][:, hc:].astype(BF16)

                @pl.when(c == 1)
                def _(a=a):
                    own[a] = accs[a][:, hc:]
                    stage[a] = accs[a][:, :hc].astype(BF16)
            for j in range(K):
                @pl.when(k == j)
                def _(j=j):
                    for a in range(3):
                        swap(j, a).start()

        @pl.when((k == K - 1) & (t == nt - 1))
        def _():
            finish(K - 1)
            for a in range(3):
                swap(K - 1, a).wait_send()

    act_spec = pl.BlockSpec((None, tk, Fs), lambda k, t: (k, t, 0))
    row = pl.BlockSpec((tk, D), lambda k, t: (t, 0))
    resident = pl.BlockSpec(memory_space=pltpu.VMEM)
    part = jax.ShapeDtypeStruct((K, Fs, hc), BF16)
    return _pallas(
        body, name, (K, nt),
        [row, row, act_spec, act_spec, act_spec, act_spec],
        [resident, resident, resident], [part, part, part],
        [pltpu.VMEM((Fs, D), F32)] * 3
        + [pltpu.VMEM((3, Fs, hc), BF16), pltpu.VMEM((3, Fs, hc), F32), pltpu.VMEM((2, 3, Fs, hc), BF16),
           pltpu.SemaphoreType.DMA((3 * K,)), pltpu.SemaphoreType.DMA((3 * K,))],
        (n, df, hg, hu, dhg, dhu), comm)


def _mix_proj_fwd(x, g, wproj_t, wf_t, tm, tn, comm=None):
    T, D = x.shape
    N = wproj_t.shape[0]

    def body(x_ref, g_ref, w_ref, wf_ref, h_ref, proj_ref, flog_ref, h_scr):
        @pl.when(pl.program_id(1) == 0)
        def _():
            xhat, _ = _rms(x_ref[...])
            h = (xhat * g_ref[...]).astype(BF16)
            h_scr[...] = h
            h_ref[...] = h
            flog_ref[...] = _dot_nt(h, wf_ref[...])

        proj_ref[...] = _dot_nt(h_scr[...], w_ref[...]).astype(BF16)

    return _pallas(
        body, "mix_proj_fwd", (T // tm, N // tn),
        [pl.BlockSpec((tm, D), lambda i, n: (i, 0)), pl.BlockSpec((1, D), lambda i, n: (0, 0)),
         pl.BlockSpec((tn, D), lambda i, n: (n, 0)), pl.BlockSpec((LANES, D), lambda i, n: (0, 0))],
        [pl.BlockSpec((tm, D), lambda i, n: (i, 0)), pl.BlockSpec((tm, tn), lambda i, n: (i, n)),
         pl.BlockSpec((tm, LANES), lambda i, n: (i, 0))],
        [jax.ShapeDtypeStruct((T, D), BF16), jax.ShapeDtypeStruct((T, N), BF16),
         jax.ShapeDtypeStruct((T, LANES), F32)],
        [pltpu.VMEM((tm, D), BF16)],
        (x, g, wproj_t, wf_t), comm)


def _log_sigmoid(z):
    return -(jnp.maximum(-z, 0.0) + jnp.log(1.0 + jnp.exp(-jnp.abs(z))))


def _tri(n, lower):
    r = lax.broadcasted_iota(jnp.int32, (n, n), 0)
    c = lax.broadcasted_iota(jnp.int32, (n, n), 1)
    return jnp.where((r >= c) if lower else (r <= c), 1.0, 0.0).astype(F32)


def _dot_f32(a, b):
    return lax.dot_general(a, b, (((1,), (0,)), ((), ())), preferred_element_type=F32,
                           precision=lax.Precision.HIGHEST)


def _fgate_fwd(flog, bias, B, S, ch):
    def body(flog_ref, b_ref, cum_ref):
        tri = _tri(ch, True)
        carry = jnp.zeros((1, LANES), F32)
        for c0 in range(0, S, ch):
            lf = _log_sigmoid(flog_ref[c0:c0 + ch, :] + b_ref[...])
            cs = _dot_f32(tri, lf) + carry
            cum_ref[c0:c0 + ch, :] = cs
            carry = cs[ch - 1:ch, :]

    return pl.pallas_call(
        body, name="fgate_fwd", grid=(B,),
        in_specs=[pl.BlockSpec((S, LANES), lambda b: (b, 0)),
                  pl.BlockSpec((1, LANES), lambda b: (0, 0))],
        out_specs=pl.BlockSpec((S, LANES), lambda b: (b, 0)),
        out_shape=jax.ShapeDtypeStruct((B * S, LANES), F32),
        compiler_params=_params(("arbitrary",)),
    )(flog, bias)


def _fgate_bwd(dcum, flog, bias, B, S, ch):
    def body(dcum_ref, flog_ref, b_ref, dflog_ref, db_ref):
        @pl.when(pl.program_id(0) == 0)
        def _():
            db_ref[...] = jnp.zeros_like(db_ref)

        tri = _tri(ch, False)
        carry = jnp.zeros((1, LANES), F32)
        db = jnp.zeros((1, LANES), F32)
        for c0 in range(S - ch, -1, -ch):
            dlf = _dot_f32(tri, dcum_ref[c0:c0 + ch, :]) + carry
            carry = dlf[0:1, :]
            z = flog_ref[c0:c0 + ch, :] + b_ref[...]
            dz = dlf * _sigmoid(-z)
            dflog_ref[c0:c0 + ch, :] = dz
            db = db + jnp.sum(dz, axis=0, keepdims=True)
        db_ref[...] += db

    return pl.pallas_call(
        body, name="fgate_bwd", grid=(B,),
        in_specs=[pl.BlockSpec((S, LANES), lambda b: (b, 0)),
                  pl.BlockSpec((S, LANES), lambda b: (b, 0)),
                  pl.BlockSpec((1, LANES), lambda b: (0, 0))],
        out_specs=[pl.BlockSpec((S, LANES), lambda b: (b, 0)),
                   pl.BlockSpec((1, LANES), lambda b: (0, 0))],
        out_shape=[jax.ShapeDtypeStruct((B * S, LANES), F32),
                   jax.ShapeDtypeStruct((1, LANES), F32)],
        compiler_params=_params(("arbitrary",)),
    )(dcum, flog, bias)


def _pick_lane(tile, h):
    lane = lax.broadcasted_iota(jnp.int32, tile.shape, 1)
    return jnp.sum(jnp.where(lane == h, tile, 0.0), axis=1, keepdims=True)


def _put_lane(col, h, width=LANES):
    lane = lax.broadcasted_iota(jnp.int32, (col.shape[0], width), 1)
    return jnp.where(lane == h, col, 0.0)


def _pick_row(tile, h):
    row = lax.broadcasted_iota(jnp.int32, tile.shape, 0)
    return jnp.sum(jnp.where(row == h, tile, 0.0), axis=0, keepdims=True)


def _put_row(vec, h):
    row = lax.broadcasted_iota(jnp.int32, (8, vec.shape[1]), 0)
    return jnp.where(row == h, vec, 0.0)


def _causal(tq):
    r = lax.broadcasted_iota(jnp.int32, (tq, tq), 0)
    c = lax.broadcasted_iota(jnp.int32, (tq, tq), 1)
    return r >= c


def _head_halves(t):
    lo = lax.broadcasted_iota(jnp.int32, t.shape, 1) < HEAD_DIM
    zero = jnp.zeros_like(t)
    return jnp.where(lo, t, zero), jnp.where(lo, zero, t)


NEG = -1e30
ATTN_SCALE = 1.0 / math.sqrt(HEAD_DIM)


def _scaled(q):
    return (q.astype(F32) * ATTN_SCALE).astype(q.dtype)


def _attn_fwd(proj, cum, cum_t, B, S, tq, comm=None):
    nq = S // tq

    def body(q_ref, k_ref, v_ref, cum_ref, cumt_ref, o_ref, lse_ref):
        qi, hp = pl.program_id(1), pl.program_id(2)
        qm = _head_halves(_scaled(q_ref[...]))
        first_head = lax.broadcasted_iota(jnp.int32, (LANES, tq), 0) < HEAD_DIM
        r = lax.broadcasted_iota(jnp.int32, (tq, tq), 0)
        c = lax.broadcasted_iota(jnp.int32, (tq, tq), 1)

        def tile(j, carry, masked):
            (ma, la), (mb, lb), acc = carry
            off = pl.multiple_of(j * tq, tq)
            kj = k_ref[pl.ds(off, tq), :]
            vm = _head_halves(v_ref[pl.ds(off, tq), :])
            cumk = cum_ref[pl.ds(off, tq), :]
            new, alphas, pv = [], [], jnp.zeros((LANES, tq), F32)
            for e, (m, l) in enumerate(((ma, la), (mb, lb))):
                s = _dot_nt(kj, qm[e]) - _pick_lane(cumk, 2 * hp + e)
                if masked:
                    s = jnp.where(r <= c, s, NEG)
                m_new = jnp.maximum(m, jnp.max(s, axis=0, keepdims=True))
                p = jnp.exp(s - m_new)
                alpha = jnp.exp(m - m_new)
                new.append((m_new, alpha * l + jnp.sum(p, axis=0, keepdims=True)))
                alphas.append(alpha)
                pv = pv + _dot_tn(vm[e], p.astype(BF16))
            acc = jnp.where(first_head, alphas[0], alphas[1]) * acc + pv
            return new[0], new[1], acc

        one = (jnp.full((1, tq), NEG, F32), jnp.zeros((1, tq), F32))
        carry = lax.fori_loop(0, qi, lambda j, cr: tile(j, cr, False), (one, one, jnp.zeros((LANES, tq), F32)))
        (ma, la), (mb, lb), acc = tile(qi, carry, True)
        o_ref[...] = (acc / jnp.where(first_head, la, lb)).T.astype(BF16)

        @pl.when(hp == 0)
        def _():
            lse_ref[...] = jnp.zeros_like(lse_ref)

        ct = cumt_ref[...]
        lse_ref[...] += (_put_row(ma + jnp.log(la) + _pick_row(ct, 2 * hp), 2 * hp)
                         + _put_row(mb + jnp.log(lb) + _pick_row(ct, 2 * hp + 1), 2 * hp + 1))

    kv = lambda first: pl.BlockSpec((S, LANES), lambda b, i, hp: (b, first + hp))
    row_block = pl.BlockSpec((None, None, 8, tq), lambda b, i, hp: (b, i, 0, 0))
    return _pallas(
        body, "attn_fwd", (B, nq, HEAD_PAIRS),
        [pl.BlockSpec((tq, LANES), lambda b, i, hp: (b * nq + i, hp)),
         kv(ATTN_W // LANES), kv(2 * ATTN_W // LANES),
         pl.BlockSpec((S, LANES), lambda b, i, hp: (b, 0)), row_block],
        [pl.BlockSpec((tq, LANES), lambda b, i, hp: (b * nq + i, hp)), row_block],
        [jax.ShapeDtypeStruct((B * S, ATTN_W), BF16), jax.ShapeDtypeStruct((B, nq, 8, tq), F32)],
        [], (proj, proj, proj, cum, cum_t), comm)


def _attn_bwd(proj, o, do, lse, cum, cum_t, B, S, tq, comm=None):
    nq = S // tq

    def body(q_ref, k_ref, v_ref, o_ref, do_ref, lse_ref, cum_ref, cumt_ref,
             dq_ref, dk_ref, dv_ref, dcq_ref, dck_ref, dq_scr):
        hp, kj = pl.program_id(1), pl.program_id(2)

        @pl.when(kj == 0)
        def _():
            dq_scr[...] = jnp.zeros_like(dq_scr)

        @pl.when((kj == 0) & (hp == 0))
        def _():
            dcq_ref[...] = jnp.zeros_like(dcq_ref)
            dck_ref[...] = jnp.zeros_like(dck_ref)

        kv = k_ref[...]
        vv = v_ref[...]
        km = _head_halves(kv)
        ct = cumt_ref[...]
        ck = [_pick_row(ct, 2 * hp + e) for e in range(2)]

        def tile(i, carry, masked):
            dk, dv, dcol = carry
            off = pl.multiple_of(i * tq, tq)
            qi = q_ref[pl.ds(off, tq), :]
            ov = o_ref[pl.ds(off, tq), :].astype(F32)
            qm = _head_halves(_scaled(qi))
            dom = _head_halves(do_ref[pl.ds(off, tq), :])
            cumv = cum_ref[pl.ds(off, tq), :]
            lsev = lse_ref[pl.ds(off, tq), :]
            dcq = jnp.zeros((tq, LANES), F32)
            dq = jnp.zeros((tq, LANES), F32)
            dcol_new = []
            for e in range(2):
                delta = jnp.sum(dom[e].astype(F32) * ov, axis=1, keepdims=True)
                row_term = _pick_lane(cumv, 2 * hp + e) - _pick_lane(lsev, 2 * hp + e)
                p = jnp.exp(_dot_nt(qm[e], kv) + row_term - ck[e])
                if masked:
                    p = jnp.where(_causal(tq), p, 0.0)
                dv = dv + _dot_tn(dom[e], p.astype(BF16))
                ds = p * (_dot_nt(dom[e], vv) - delta)
                dcol_new.append(dcol[e] + jnp.sum(ds, axis=0, keepdims=True))
                dcq = dcq + _put_lane(jnp.sum(ds, axis=1, keepdims=True), 2 * hp + e)
                dsb = ds.astype(BF16)
                dk = dk + _dot_tn(qm[e], dsb)
                dq = dq + _dot(dsb, km[e]) * ATTN_SCALE
            dq_scr[pl.ds(off, tq), :] += dq
            dcq_ref[pl.ds(off, tq), :] += dcq
            return dk, dv, tuple(dcol_new)

        zero_row = jnp.zeros((1, tq), F32)
        init = (jnp.zeros((LANES, tq), F32), jnp.zeros((LANES, tq), F32), (zero_row, zero_row))
        carry = tile(kj, init, True)
        dk, dv, dcol = lax.fori_loop(kj + 1, nq, lambda i, c: tile(i, c, False), carry)
        dk_ref[...] = dk.T.astype(BF16)
        dv_ref[...] = dv.T.astype(BF16)
        dck_ref[kj] += -(_put_row(dcol[0], 2 * hp) + _put_row(dcol[1], 2 * hp + 1))

        @pl.when(kj == nq - 1)
        def _():
            dq_ref[...] = dq_scr[...].astype(BF16)

    seq = lambda first: pl.BlockSpec((S, LANES), lambda b, hp, j: (b, first + hp))
    tile_in = lambda first: pl.BlockSpec((tq, LANES), lambda b, hp, j: (b * nq + j, first + hp))
    lanes0 = pl.BlockSpec((S, LANES), lambda b, hp, j: (b, 0))
    out = jax.ShapeDtypeStruct((B * S, ATTN_W), BF16)
    return _pallas(
        body, "attn_bwd", (B, HEAD_PAIRS, nq),
        [seq(0), tile_in(ATTN_W // LANES), tile_in(2 * ATTN_W // LANES), seq(0), seq(0), lanes0, lanes0,
         pl.BlockSpec((None, None, 8, tq), lambda b, hp, j: (b, j, 0, 0))],
        [seq(0), tile_in(0), tile_in(0), lanes0,
         pl.BlockSpec((None, nq, 8, tq), lambda b, hp, j: (b, 0, 0, 0))],
        [out, out, out, jax.ShapeDtypeStruct((B * S, LANES), F32), jax.ShapeDtypeStruct((B, nq, 8, tq), F32)],
        [pltpu.VMEM((S, LANES), F32)],
        (proj, proj, proj, o, do, lse, cum, cum_t), comm)


def _shift_down(u, n):
    row = lax.broadcasted_iota(jnp.int32, u.shape, 0)
    return jnp.where(row >= n, pltpu.roll(u, n, 0), 0.0)


def _shift_up(u, n):
    rows = u.shape[0]
    row = lax.broadcasted_iota(jnp.int32, u.shape, 0)
    return jnp.where(row < rows - n, pltpu.roll(u, rows - n, 0), 0.0)


def _conv_specs(S):
    cb = pl.BlockSpec((S, LANES), lambda g, b: (b, COL_CB // LANES + g))
    cc = pl.BlockSpec((S, LANES), lambda g, b: (b, COL_CC // LANES + g))
    cx = pl.BlockSpec((S, LANES), lambda g, b: (b, COL_CX // LANES + g))
    w = pl.BlockSpec((8, LANES), lambda g, b: (0, g))
    return cb, cc, cx, w


def _conv_fwd(proj, conv_w, B, S):
    def body(cb_ref, cc_ref, cx_ref, w_ref, y_ref):
        u = cc_ref[...].astype(F32) * cx_ref[...].astype(F32)
        w = w_ref[...]
        conv = w[0:1, :] * _shift_down(u, 2) + w[1:2, :] * _shift_down(u, 1) + w[2:3, :] * u
        y_ref[...] = (cb_ref[...].astype(F32) * conv).astype(BF16)

    cb, cc, cx, w = _conv_specs(S)
    return pl.pallas_call(
        body, name="conv_fwd", grid=(CONV_W // LANES, B),
        in_specs=[cb, cc, cx, w],
        out_specs=pl.BlockSpec((S, LANES), lambda g, b: (b, g)),
        out_shape=jax.ShapeDtypeStruct((B * S, CONV_W), BF16),
        compiler_params=_params(("arbitrary", "arbitrary")),
    )(proj, proj, proj, conv_w)


def _conv_bwd(dy, proj, conv_w, B, S):
    def body(dy_ref, cb_ref, cc_ref, cx_ref, w_ref, dcb_ref, dcc_ref, dcx_ref, dw_ref):
        @pl.when(pl.program_id(1) == 0)
        def _():
            dw_ref[...] = jnp.zeros_like(dw_ref)

        ccv = cc_ref[...].astype(F32)
        cxv = cx_ref[...].astype(F32)
        u = ccv * cxv
        u1 = _shift_down(u, 1)
        u2 = _shift_down(u, 2)
        w = w_ref[...]
        conv = w[0:1, :] * u2 + w[1:2, :] * u1 + w[2:3, :] * u
        dyv = dy_ref[...].astype(F32)
        dcb_ref[...] = (dyv * conv).astype(BF16)
        dconv = dyv * cb_ref[...].astype(F32)
        du = w[2:3, :] * dconv + w[1:2, :] * _shift_up(dconv, 1) + w[0:1, :] * _shift_up(dconv, 2)
        dcc_ref[...] = (du * cxv).astype(BF16)
        dcx_ref[...] = (du * ccv).astype(BF16)
        row = lax.broadcasted_iota(jnp.int32, (8, LANES), 0)
        dw = jnp.where(row == 0, jnp.sum(dconv * u2, axis=0, keepdims=True),
                       jnp.where(row == 1, jnp.sum(dconv * u1, axis=0, keepdims=True),
                                 jnp.where(row == 2, jnp.sum(dconv * u, axis=0, keepdims=True), 0.0)))
        dw_ref[...] += dw

    cb, cc, cx, w = _conv_specs(S)
    out = pl.BlockSpec((S, LANES), lambda g, b: (b, g))
    return pl.pallas_call(
        body, name="conv_bwd", grid=(CONV_W // LANES, B),
        in_specs=[out, cb, cc, cx, w],
        out_specs=[out, out, out, w],
        out_shape=[jax.ShapeDtypeStruct((B * S, CONV_W), BF16)] * 3 + [jax.ShapeDtypeStruct((8, CONV_W), F32)],
        compiler_params=_params(("arbitrary", "arbitrary")),
    )(dy, proj, proj, proj, conv_w)


def _gate_specs(tm, D):
    ga = pl.BlockSpec((tm, D), lambda i: (i, COL_GATES // D))
    gc = pl.BlockSpec((tm, D), lambda i: (i, COL_GATES // D + 1))
    return ga, gc


def _mix_out_fwd(x, o, yc, proj, woa, woc, wout, tm):
    T, D = x.shape

    def body(x_ref, o_ref, yc_ref, ga_ref, gc_ref, woa_ref, woc_ref, wout_ref, out_ref):
        ya = _dot(o_ref[...], woa_ref[...])
        yp = _dot(yc_ref[...], woc_ref[...])
        merged = _sigmoid(ga_ref[...].astype(F32)) * ya + _sigmoid(gc_ref[...].astype(F32)) * yp
        out_ref[...] = x_ref[...] + _dot(merged.astype(BF16), wout_ref[...])

    ga, gc = _gate_specs(tm, D)
    row = lambda w: pl.BlockSpec((tm, w), lambda i: (i, 0))
    whole = lambda a: pl.BlockSpec(a.shape, lambda i: (0, 0))
    return pl.pallas_call(
        body, name="mix_out_fwd", grid=(T // tm,),
        in_specs=[row(D), row(ATTN_W), row(CONV_W), ga, gc, whole(woa), whole(woc), whole(wout)],
        out_specs=row(D),
        out_shape=jax.ShapeDtypeStruct((T, D), F32),
        compiler_params=_params(("arbitrary",)),
    )(x, o, yc, proj, proj, woa, woc, wout)


def _mix_out_bwd(dx, o, yc, proj, woa, woc, wout, tm, comm=None):
    T, D = dx.shape
    nt = T // tm

    def body(dx_ref, o_ref, yc_ref, ga_ref, gc_ref, woa_ref, woc_ref, wout_ref,
             do_ref, dyc_ref, dg_ref, dwoa_ref, dwoc_ref, dwout_ref, acca, accc, acco):
        t = pl.program_id(0)

        @pl.when(t == 0)
        def _():
            acca[...] = jnp.zeros_like(acca)
            accc[...] = jnp.zeros_like(accc)
            acco[...] = jnp.zeros_like(acco)

        dxb = dx_ref[...].astype(BF16)
        ov, ycv = o_ref[...], yc_ref[...]
        ya = _dot(ov, woa_ref[...])
        yp = _dot(ycv, woc_ref[...])
        sa = _sigmoid(ga_ref[...].astype(F32))
        sc = _sigmoid(gc_ref[...].astype(F32))
        merged = (sa * ya + sc * yp).astype(BF16)
        dm = _dot_nt(dxb, wout_ref[...])
        dya = (dm * sa).astype(BF16)
        dyp = (dm * sc).astype(BF16)
        dg_ref[:, :D] = (dm * ya * sa * (1.0 - sa)).astype(BF16)
        dg_ref[:, D:] = (dm * yp * sc * (1.0 - sc)).astype(BF16)
        do_ref[...] = _dot_nt(dya, woa_ref[...]).astype(BF16)
        dyc_ref[...] = _dot_nt(dyp, woc_ref[...]).astype(BF16)
        acca[...] += _dot_tn(ov, dya)
        accc[...] += _dot_tn(ycv, dyp)
        acco[...] += _dot_tn(merged, dxb)

        @pl.when(t == nt - 1)
        def _():
            dwoa_ref[...] = acca[...].astype(BF16)
            dwoc_ref[...] = accc[...].astype(BF16)
            dwout_ref[...] = acco[...].astype(BF16)

    ga, gc = _gate_specs(tm, D)
    row = lambda w: pl.BlockSpec((tm, w), lambda i: (i, 0))
    whole = lambda a: pl.BlockSpec(a.shape, lambda i: (0, 0))
    return _pallas(
        body, "mix_out_bwd", (nt,),
        [row(D), row(ATTN_W), row(CONV_W), ga, gc, whole(woa), whole(woc), whole(wout)],
        [row(ATTN_W), row(CONV_W), row(2 * D), whole(woa), whole(woc), whole(wout)],
        [jax.ShapeDtypeStruct((T, ATTN_W), BF16), jax.ShapeDtypeStruct((T, CONV_W), BF16),
         jax.ShapeDtypeStruct((T, 2 * D), BF16),
         jax.ShapeDtypeStruct(woa.shape, BF16), jax.ShapeDtypeStruct(woc.shape, BF16),
         jax.ShapeDtypeStruct(wout.shape, BF16)],
        [pltpu.VMEM(woa.shape, F32), pltpu.VMEM(woc.shape, F32), pltpu.VMEM(wout.shape, F32)],
        (dx, o, yc, proj, proj, woa, woc, wout), comm)


def _proj_pieces(dq, dk, dv, dcb, dcc, dcx, dgates, dflog):
    D = dgates.shape[1] // 2
    return [(dq, ATTN_W, 0), (dk, ATTN_W, 0), (dv, ATTN_W, 0), (dcb, CONV_W, 0), (dcc, CONV_W, 0), (dcx, CONV_W, 0),
            (dgates, D, 0), (dgates, D, 1), (dflog, LANES, 0)]


def _mix_proj_bwd_dx(dres, x, g, pieces, wproj_t, wf_t, tm, comm=None):
    T, D = x.shape
    n = len(pieces)
    w_blocks = [(ATTN_W, 0), (ATTN_W, 1), (ATTN_W, 2), (CONV_W, 3), (CONV_W, 4), (CONV_W, 5),
                (D, COL_GATES // D), (D, COL_GATES // D + 1)]

    def body(*refs):
        dres_ref, x_ref, g_ref = refs[:3]
        p_refs, w_refs = refs[3:3 + n], refs[3 + n:3 + 2 * n]
        dx_ref, dg_ref = refs[3 + 2 * n:]

        @pl.when(pl.program_id(0) == 0)
        def _():
            dg_ref[...] = jnp.zeros_like(dg_ref)

        dh = _dot(p_refs[0][...].astype(BF16), w_refs[0][...])
        for p_ref, w_ref in zip(p_refs[1:], w_refs[1:]):
            dh = dh + _dot(p_ref[...].astype(BF16), w_ref[...])
        xhat, inv = _rms(x_ref[...])
        dx, dg = _rms_bwd(dh, xhat, inv, g_ref[...])
        dx_ref[...] = dres_ref[...] + dx
        dg_ref[...] += dg

    row = pl.BlockSpec((tm, D), lambda i: (i, 0))
    vec = pl.BlockSpec((1, D), lambda i: (0, 0))
    p_specs = [pl.BlockSpec((tm, w), lambda i, cb=cb: (i, cb)) for _, w, cb in pieces]
    w_specs = [pl.BlockSpec((r, D), lambda i, rb=rb: (rb, 0)) for r, rb in w_blocks]
    w_specs.append(pl.BlockSpec((LANES, D), lambda i: (0, 0)))
    return _pallas(
        body, "mix_proj_bwd_dx", (T // tm,),
        [row, row, vec] + p_specs + w_specs, [row, vec],
        [jax.ShapeDtypeStruct((T, D), F32), jax.ShapeDtypeStruct((1, D), F32)], [],
        (dres, x, g, *[p for p, _, _ in pieces], *([wproj_t] * len(w_blocks)), wf_t), comm)


def _matmuls_tn(name, pieces, b, tk):
    T, N = b.shape
    nt = T // tk
    n = len(pieces)

    def body(*refs):
        a_refs, b_ref, out_refs, accs = refs[:n], refs[n], refs[n + 1:2 * n + 1], refs[2 * n + 1:]
        t = pl.program_id(0)

        @pl.when(t == 0)
        def _():
            for acc in accs:
                acc[...] = jnp.zeros_like(acc)

        bv = b_ref[...]
        for a_ref, acc in zip(a_refs, accs):
            acc[...] += _dot_tn(a_ref[...].astype(BF16), bv)

        @pl.when(t == nt - 1)
        def _():
            for out_ref, acc in zip(out_refs, accs):
                out_ref[...] = acc[...].astype(BF16)

    return pl.pallas_call(
        body, name=name, grid=(nt,),
        in_specs=[pl.BlockSpec((tk, w), lambda t, cb=cb: (t, cb)) for _, w, cb in pieces]
        + [pl.BlockSpec((tk, N), lambda t: (t, 0))],
        out_specs=[pl.BlockSpec((w, N), lambda t: (0, 0)) for _, w, _ in pieces],
        out_shape=[jax.ShapeDtypeStruct((w, N), BF16) for _, w, _ in pieces],
        scratch_shapes=[pltpu.VMEM((w, N), F32) for _, w, _ in pieces],
        compiler_params=_params(("arbitrary",)),
    )(*[a for a, _, _ in pieces], b)


TOKEN_TILE = 512
TOKEN_TILE_WIDE = 1024
ATTN_TILE = 512
SCAN_CHUNK = 256
PROJ_DX_TILE = 256


def _local_step(x, target, plan, B, S):
    T, D = x.shape
    tm = min(TOKEN_TILE, T)
    tm_fwd = min(TOKEN_TILE_WIDE, T)
    tq = min(ATTN_TILE, S)
    nq = S // tq
    ch = min(SCAN_CHUNK, S)

    def riding(kernel_name, build):
        results, brought = build(plan.rider(kernel_name))
        plan.arrived(kernel_name, brought)
        return results

    hg1, hu1, n1 = plan.ffn1_up(x, tm_fwd)
    w1 = plan.weights("ffn1")
    x1, = riding("ffn1_down", lambda comm: _ffn_down("ffn1_down", x, hg1, hu1, w1["ffn1_down"], tm_fwd, comm))
    wm = plan.weights("mix_in")
    h, proj, flog = riding("mix_proj_fwd", lambda comm: _mix_proj_fwd(
        x1, wm["mix_norm"], wm["w_proj"], wm["w_f"], tm_fwd, PROJ_W // 4, comm))
    wm.update(plan.weights("mix_out"))
    cum = _fgate_fwd(flog, wm["b_forget"], B, S, ch)
    cum_t = jnp.transpose(cum[:, :N_HEADS].reshape(B, nq, tq, N_HEADS), (0, 1, 3, 2))
    o, lse_t = riding("attn_fwd", lambda comm: _attn_fwd(proj, cum, cum_t, B, S, tq, comm))
    lse = jnp.pad(jnp.transpose(lse_t, (0, 1, 3, 2)).reshape(T, N_HEADS), ((0, 0), (0, LANES - N_HEADS)))
    yc = _conv_fwd(proj, wm["conv_w"], B, S)
    x2 = _mix_out_fwd(x1, o, yc, proj, wm["w_o_attn"], wm["w_o_conv"], wm["w_out"], tm)
    w2 = plan.weights("ffn2")
    dx3, hg2, hu2, n2, loss, d_final_norm = _ffn_fwd_loss(
        "ffn2_fwd_loss", x2, w2["ffn2_norm"], w2["ffn2_gate"], w2["ffn2_up"], w2["ffn2_down"], target, w2["final_norm"],
        tm_fwd)

    g = {"final_norm": d_final_norm}
    dx2, dhg2, dhu2, g["ffn2_norm"], df2 = _ffn_bwd_dx("ffn2_bwd_dx", dx3, x2, w2["ffn2_norm"], hg2, hu2,
                                                  w2["ffn2_gate"], w2["ffn2_up"], w2["ffn2_down"], tm_fwd)[0]
    plan.reduce("ffn2", dict(zip(("ffn2_gate", "ffn2_up", "ffn2_down"),
                                 _ffn_bwd_dw("ffn2_bwd_dw", n2, df2, hg2, hu2, dhg2, dhu2, tm_fwd)[0])))
    do, dyc, dgates, dwoa, dwoc, dwout = riding("mix_out_bwd", lambda comm: _mix_out_bwd(
        dx2, o, yc, proj, wm["w_o_attn"], wm["w_o_conv"], wm["w_out"], tm, comm))
    plan.reduce("out", dict(w_o_attn=_shard_cols(dwoa), w_o_conv=_shard_cols(dwoc), w_out=dwout.reshape(N_CHIPS, -1, D)))
    dq, dk, dv, dcq, dck = riding("attn_bwd", lambda comm: _attn_bwd(proj, o, do, lse, cum, cum_t, B, S, tq, comm))
    dcum = dcq + jnp.pad(jnp.transpose(dck, (0, 1, 3, 2)).reshape(T, N_HEADS), ((0, 0), (0, LANES - N_HEADS)))
    dflog, g["b_forget"] = _fgate_bwd(dcum, flog, wm["b_forget"], B, S, ch)
    dcb, dcc, dcx, g["conv_w"] = _conv_bwd(dyc, proj, wm["conv_w"], B, S)
    pieces = _proj_pieces(dq, dk, dv, dcb, dcc, dcx, dgates, dflog)
    dwq, dwk, dwv, dwcb, dwcc, dwcx = _matmuls_tn("mix_dw_a", pieces[:6], h, tm)
    dwga, dwgc, dwf = _matmuls_tn("mix_dw_b", pieces[6:], h, tm)
    dwin_t = jnp.concatenate([dwq, dwk, dwv, dwf[:N_HEADS], dwcb, dwcc, dwcx, dwga, dwgc], axis=0)
    plan.reduce("w_in", {"w_in": dwin_t.reshape(N_CHIPS, -1, D)})
    dx1, g["mix_norm"] = riding("mix_proj_bwd_dx", lambda comm: _mix_proj_bwd_dx(
        dx2, x1, wm["mix_norm"], pieces, wm["w_proj"], wm["w_f"], min(PROJ_DX_TILE, T), comm))
    grad_x, dhg1, dhu1, g["ffn1_norm"], df1 = _ffn_bwd_dx(
        "ffn1_bwd_dx", dx1, x, w1["ffn1_norm"], hg1, hu1, w1["ffn1_gate"], w1["ffn1_up"], w1["ffn1_down"], tm_fwd)[0]
    plan.reduce_small(g, loss)
    plan.reduce_parts("ffn1", dict(zip(("ffn1_gate", "ffn1_up", "ffn1_down"), riding(
        "ffn1_bwd_dw", lambda comm: _ffn_bwd_dw_reduced("ffn1_bwd_dw", n1, df1, hg1, hu1, dhg1, dhu1, tm_fwd, comm)))))
    return loss, grad_x, g


TRANSPOSED = ("ffn1_gate", "ffn1_up", "ffn2_gate", "ffn2_up", "w_in")
NORMS = ("ffn1_norm", "mix_norm", "ffn2_norm", "final_norm")


def _unshard_cols(a):
    return jnp.transpose(a, (1, 0, 2)).reshape(a.shape[1], N_CHIPS * a.shape[2])


def _shard_cols(a):
    return jnp.transpose(a.reshape(a.shape[0], N_CHIPS, a.shape[1] // N_CHIPS), (1, 0, 2))


def _layout_ffn(which):
    def layout(st, small):
        w = {n: st[n] for n in (which + "_gate", which + "_up", which + "_down")}
        w[which + "_norm"] = small[which + "_norm"].reshape(1, -1)
        if which == "ffn2":
            w["final_norm"] = small["final_norm"].reshape(1, -1)
        return w
    return layout


def _layout_mix_in(st, small):
    win_t = st["w_in"].reshape(-1, st["w_in"].shape[2])
    return {
        "w_proj": jnp.concatenate([win_t[:N_FORGET_COL], win_t[N_FORGET_COL + N_HEADS:]], axis=0),
        "w_f": jnp.pad(win_t[N_FORGET_COL:N_FORGET_COL + N_HEADS], ((0, LANES - N_HEADS), (0, 0))),
        "conv_w": _unshard_cols(st["conv_w"]),
        "mix_norm": small["mix_norm"].reshape(1, -1),
        "b_forget": jnp.pad(small["b_forget"].reshape(1, -1), ((0, 0), (0, LANES - N_HEADS))),
    }


def _layout_mix_out(st, small):
    return {"w_o_attn": _unshard_cols(st["w_o_attn"]), "w_o_conv": _unshard_cols(st["w_o_conv"]),
            "w_out": st["w_out"].reshape(-1, st["w_out"].shape[2])}


_LAYOUTS = {"ffn1": _layout_ffn("ffn1"), "mix_in": _layout_mix_in, "mix_out": _layout_mix_out, "ffn2": _layout_ffn("ffn2")}


ANY = pl.BlockSpec(memory_space=pl.ANY)
BIG = ("ffn1_gate", "ffn1_up", "ffn1_down", "w_in", "w_o_attn", "w_o_conv", "w_out",
       "ffn2_gate", "ffn2_up", "ffn2_down")


def _place():
    x, y, c = lax.axis_index("x"), lax.axis_index("y"), lax.axis_index("c")
    others = [(1 - x, y), (x, 1 - y), (1 - x, 1 - y)]
    return x, y, c, others


def _col_halves(cols, c):
    hc = cols // 2
    return pl.ds(pl.multiple_of(c * hc, LANES), hc), pl.ds(pl.multiple_of((1 - c) * hc, LANES), hc)


def _gather_comm(shards, conv_shard=None):
    n = len(shards)
    inputs = list(shards) + ([] if conv_shard is None else [conv_shard])

    def copies(ins, outs, sems):
        send_sems, recv_sems, pass_send, pass_recv = sems[:4]
        x, y, c, others = _place()

        def chip_copy(a, j, chip):
            mine, _ = _col_halves(ins[a].shape[1], c)
            return pltpu.make_async_remote_copy(
                src_ref=ins[a].at[:, mine], dst_ref=outs[a].at[chip, :, mine],
                send_sem=send_sems.at[3 * a + j], recv_sem=recv_sems.at[3 * a + j],
                device_id=(*others[j], c), device_id_type=MESH)

        def pass_copy(a, j, chip, half):
            return pltpu.make_async_remote_copy(
                src_ref=outs[a].at[chip, :, half], dst_ref=outs[a].at[chip, :, half],
                send_sem=pass_send.at[3 * a + j], recv_sem=pass_recv.at[3 * a + j],
                device_id=(x, y, 1 - c), device_id_type=MESH)

        def conv_copy(j, chip):
            return pltpu.make_async_remote_copy(
                src_ref=ins[n], dst_ref=outs[n].at[chip],
                send_sem=sems[4].at[j], recv_sem=sems[5].at[j],
                device_id=(*others[j], c), device_id_type=MESH)

        me = 2 * x + y
        sends = [chip_copy(a, j, me) for a in range(n) for j in range(3)]
        if conv_shard is not None:
            sends += [conv_copy(j, me) for j in range(3)]
        return c, others, sends, chip_copy, pass_copy, conv_copy

    def start(ins, outs, sems):
        for cp in copies(ins, outs, sems)[2]:
            cp.start()

    def finish(ins, outs, sems):
        c, others, sends, chip_copy, pass_copy, conv_copy = copies(ins, outs, sems)
        passed = []
        for a in range(n):
            mine, _ = _col_halves(ins[a].shape[1], c)
            for j, (ox, oy) in enumerate(others):
                chip_copy(a, j, 2 * ox + oy).wait_recv()
                passed.append(pass_copy(a, j, 2 * ox + oy, mine))
                passed[-1].start()
        for a in range(n):
            _, theirs = _col_halves(ins[a].shape[1], c)
            for j, (ox, oy) in enumerate(others):
                pass_copy(a, j, 2 * ox + oy, theirs).wait_recv()
        if conv_shard is not None:
            for j, (ox, oy) in enumerate(others):
                conv_copy(j, 2 * ox + oy).wait_recv()
        for cp in sends + passed:
            cp.wait_send()

    scratch = [pltpu.SemaphoreType.DMA((3 * n,))] * 4
    if conv_shard is not None:
        scratch += [pltpu.SemaphoreType.DMA((3,))] * 2
    return _Comm(inputs, [jax.ShapeDtypeStruct((N_CHIPS,) + s.shape, s.dtype) for s in inputs], scratch, start, finish)


def _fill_own(stacks, shards):
    chip = 2 * lax.axis_index("x") + lax.axis_index("y")
    return [lax.dynamic_update_index_in_dim(st, s, chip, 0) for st, s in zip(stacks, shards)]


def _run_comm(name, comm):
    ci, co = len(comm.inputs), len(comm.out_shape)

    def body(*refs):
        comm.start(refs[:ci], refs[ci:ci + co], refs[ci + co:])
        comm.finish(refs[:ci], refs[ci:ci + co], refs[ci + co:])

    return pl.pallas_call(body, name=name, in_specs=[ANY] * ci, out_specs=[ANY] * co, out_shape=comm.out_shape,
                          scratch_shapes=comm.scratch)(*comm.inputs)


def _sibling_exchange_comm(grads):
    n = len(grads)

    def copies(ins, outs, sems):
        x, y, c, _ = _place()
        return [pltpu.make_async_remote_copy(
            src_ref=ins[a].at[:, :, _col_halves(ins[a].shape[2], c)[1]], dst_ref=outs[a],
            send_sem=sems[0].at[a], recv_sem=sems[1].at[a],
            device_id=(x, y, 1 - c), device_id_type=MESH) for a in range(n)]

    def start(ins, outs, sems):
        for cp in copies(ins, outs, sems):
            cp.start()

    def finish(ins, outs, sems):
        for cp in copies(ins, outs, sems):
            cp.wait()

    half = lambda s: jax.ShapeDtypeStruct((s.shape[0], s.shape[1], s.shape[2] // 2), s.dtype)
    return _Comm(grads, [half(s) for s in grads], [pltpu.SemaphoreType.DMA((n,))] * 2, start, finish)


def _merge_comms(comms):
    def split(refs, count):
        out, at = [], 0
        for cm in comms:
            out.append(refs[at:at + count(cm)])
            at += count(cm)
        return out

    def parts(ins, outs, sems):
        return zip(comms, split(ins, lambda cm: len(cm.inputs)), split(outs, lambda cm: len(cm.out_shape)),
                   split(sems, lambda cm: len(cm.scratch)))

    def start(ins, outs, sems):
        for cm, i, o, s in parts(ins, outs, sems):
            cm.start(i, o, s)

    def finish(ins, outs, sems):
        for cm, i, o, s in parts(ins, outs, sems):
            cm.finish(i, o, s)

    return _Comm(sum([cm.inputs for cm in comms], []), sum([cm.out_shape for cm in comms], []),
                 sum([cm.scratch for cm in comms], []), start, finish)


def _add_halves(name, grads, recvs, core):
    n = len(grads)

    def body(core_ref, *refs):
        for g_ref, r_ref, out_ref in zip(refs[:n], refs[n:2 * n], refs[2 * n:]):
            out_ref[...] = (g_ref[...].astype(F32) + r_ref[...].astype(F32)).astype(BF16)

    half = lambda g: pl.BlockSpec((None, g.shape[1], g.shape[2] // 2), lambda k, core_ref: (k, 0, 0))
    mine = lambda g: pl.BlockSpec((None, g.shape[1], g.shape[2] // 2), lambda k, core_ref: (k, 0, core_ref[0]))
    return pl.pallas_call(
        body, name=name,
        grid_spec=pltpu.PrefetchScalarGridSpec(
            num_scalar_prefetch=1, grid=(N_CHIPS,),
            in_specs=[mine(g) for g in grads] + [half(g) for g in grads],
            out_specs=[half(g) for g in grads]),
        out_shape=[jax.ShapeDtypeStruct(r.shape, BF16) for r in recvs],
        compiler_params=_params(("arbitrary",)),
    )(core, *grads, *recvs)


def _chip_exchange_comm(parts):
    n = len(parts)

    def copies(ins, outs, sems):
        x, y, c, others = _place()
        return [pltpu.make_async_remote_copy(
            src_ref=ins[a].at[2 * ox + oy], dst_ref=outs[a].at[j],
            send_sem=sems[0].at[3 * a + j], recv_sem=sems[1].at[3 * a + j],
            device_id=(ox, oy, c), device_id_type=MESH) for a in range(n) for j, (ox, oy) in enumerate(others)]

    def start(ins, outs, sems):
        for cp in copies(ins, outs, sems):
            cp.start()

    def finish(ins, outs, sems):
        for cp in copies(ins, outs, sems):
            cp.wait()

    return _Comm(parts, [jax.ShapeDtypeStruct((3,) + s.shape[1:], s.dtype) for s in parts],
                 [pltpu.SemaphoreType.DMA((3 * n,))] * 2, start, finish)


HBM = pl.BlockSpec(memory_space=pltpu.HBM)
SEM = pl.BlockSpec(memory_space=pltpu.SEMAPHORE)


def _split_exchange_copies(parts, lands, send_sems, recv_sems):
    x, y, c, others = _place()
    return [pltpu.make_async_remote_copy(
        src_ref=parts[a].at[2 * ox + oy], dst_ref=lands[a].at[j],
        send_sem=send_sems.at[3 * a + j], recv_sem=recv_sems.at[3 * a + j],
        device_id=(ox, oy, c), device_id_type=MESH) for a in range(len(parts)) for j, (ox, oy) in enumerate(others)]


def _exchange_start(name, parts):
    n = len(parts)

    def body(*refs):
        ins, lands = refs[:n], refs[n:2 * n]
        send_sems, recv_sems, token = refs[2 * n], refs[2 * n + 1], refs[-1]
        for cp in _split_exchange_copies(ins, lands, send_sems, recv_sems):
            cp.start()
        token[...] = jnp.zeros_like(token)

    land_shape = [(3,) + p.shape[1:] for p in parts]
    outs = pl.pallas_call(
        body, name=name,
        out_shape=[pltpu.SemaphoreType.DMA((3 * n,)), pltpu.SemaphoreType.DMA((3 * n,))]
        + [pltpu.HBM(p.shape, p.dtype) for p in parts] + [pltpu.HBM(s, p.dtype) for s, p in zip(land_shape, parts)]
        + [jax.ShapeDtypeStruct((8, LANES), F32)],
        in_specs=[HBM] * (2 * n), out_specs=[SEM, SEM] + [HBM] * (2 * n) + [pl.BlockSpec(memory_space=pltpu.VMEM)],
        input_output_aliases={i: 2 + i for i in range(2 * n)},
        compiler_params=pltpu.CompilerParams(has_side_effects=pltpu.SideEffectType.DATAFLOW_SIDE_EFFECTING),
    )(*[pltpu.with_memory_space_constraint(p, pltpu.HBM) for p in parts],
      *[pltpu.with_memory_space_constraint(lax.empty(s, p.dtype), pltpu.HBM) for s, p in zip(land_shape, parts)])
    return outs[0], outs[1], list(outs[2:2 + n]), list(outs[2 + n:2 + 2 * n]), outs[-1]


def _exchange_wait(name, send_sems, recv_sems, parts, lands, after):
    n = len(parts)

    def body(*refs):
        ins, zones = refs[:n], refs[n:2 * n]
        for cp in _split_exchange_copies(ins, zones, refs[2 * n], refs[2 * n + 1]):
            cp.wait_send()
            cp.wait_recv()

    outs = pl.pallas_call(
        body, name=name,
        out_shape=[pltpu.HBM(p.shape, p.dtype) for p in parts] + [pltpu.HBM(z.shape, z.dtype) for z in lands],
        in_specs=[HBM] * (2 * n) + [SEM, SEM] + [ANY] * len(after), out_specs=[HBM] * (2 * n),
        input_output_aliases={i: i for i in range(2 * n)},
        compiler_params=pltpu.CompilerParams(has_side_effects=pltpu.SideEffectType.DATAFLOW_SIDE_EFFECTING),
    )(*parts, *lands, send_sems, recv_sems, *after)
    return list(outs[:n]), list(outs[n:])


def _sum_chips(name, owns, recvs, chip, after):
    n = len(owns)
    hc = owns[0].shape[2]
    assert all(o.shape[2] == hc for o in owns)

    def body(chip_ref, *refs):
        for own_ref, recv_ref, out_ref in zip(refs[:n], refs[n:2 * n], refs[2 * n + 1:]):
            acc = own_ref[...].astype(F32)
            for j in range(3):
                acc = acc + recv_ref[j].astype(F32)
            out_ref[...] = acc

    return pl.pallas_call(
        body, name=name,
        grid_spec=pltpu.PrefetchScalarGridSpec(
            num_scalar_prefetch=1, grid=(hc // LANES,),
            in_specs=[pl.BlockSpec((None, o.shape[1], LANES), lambda i, chip_ref: (chip_ref[0], 0, i)) for o in owns]
            + [pl.BlockSpec((3, o.shape[1], LANES), lambda i, chip_ref: (0, 0, i)) for o in owns]
            + [pl.BlockSpec((8, LANES), lambda i, chip_ref: (0, 0))],
            out_specs=[pl.BlockSpec((o.shape[1], LANES), lambda i, chip_ref: (0, i)) for o in owns]),
        out_shape=[jax.ShapeDtypeStruct((o.shape[1], hc), F32) for o in owns],
        compiler_params=_params(("arbitrary",)),
    )(chip, *owns, *recvs, after)


def _share_halves(name, halves):
    n = len(halves)

    def body(*refs):
        srcs, dsts = refs[:n], refs[n:2 * n]
        send_sems, recv_sems = refs[2 * n:]
        x, y, c, _ = _place()
        copies = [pltpu.make_async_remote_copy(
            src_ref=srcs[a], dst_ref=dsts[a], send_sem=send_sems.at[a], recv_sem=recv_sems.at[a],
            device_id=(x, y, 1 - c), device_id_type=MESH) for a in range(n)]
        for cp in copies:
            cp.start()
        for cp in copies:
            cp.wait()

    return pl.pallas_call(
        body, name=name,
        in_specs=[ANY] * n, out_specs=[ANY] * n,
        out_shape=[jax.ShapeDtypeStruct(s.shape, s.dtype) for s in halves],
        scratch_shapes=[pltpu.SemaphoreType.DMA((n,)), pltpu.SemaphoreType.DMA((n,))],
    )(*halves)


def _small_gather_comm(part):
    def copies(ins, outs, sems):
        x, y, c, _ = _place()
        me = 4 * x + 2 * y + c
        both = []
        for d in range(1, N_DEV):
            px, py, pc = (1 - x if d & 4 else x, 1 - y if d & 2 else y, 1 - c if d & 1 else c)
            send = pltpu.make_async_remote_copy(
                src_ref=ins[0], dst_ref=outs[0].at[me], send_sem=sems[0].at[d - 1], recv_sem=sems[1].at[d - 1],
                device_id=(px, py, pc), device_id_type=MESH)
            recv = pltpu.make_async_remote_copy(
                src_ref=ins[0], dst_ref=outs[0].at[4 * px + 2 * py + pc], send_sem=sems[0].at[d - 1],
                recv_sem=sems[1].at[d - 1], device_id=(px, py, pc), device_id_type=MESH)
            both.append((send, recv))
        return both

    def start(ins, outs, sems):
        for send, _ in copies(ins, outs, sems):
            send.start()

    def finish(ins, outs, sems):
        for send, recv in copies(ins, outs, sems):
            recv.wait_recv()
            send.wait_send()

    return _Comm([part], [jax.ShapeDtypeStruct((N_DEV,) + part.shape, F32)],
                 [pltpu.SemaphoreType.DMA((N_DEV - 1,))] * 2, start, finish)


def _sum_devices(parts):
    def body(p_ref, out_ref):
        acc = p_ref[0]
        for k in range(1, N_DEV):
            acc = acc + p_ref[k]
        out_ref[...] = acc

    return pl.pallas_call(
        body, name="sum_devices", grid=(1,),
        in_specs=[pl.BlockSpec(parts.shape, lambda i: (0, 0, 0))],
        out_specs=pl.BlockSpec(parts.shape[1:], lambda i: (0, 0)),
        out_shape=jax.ShapeDtypeStruct(parts.shape[1:], F32),
        compiler_params=_params(("arbitrary",)),
    )(parts)


def _adam_update(w, g, m, v):
    nm = ADAM_B1 * m + (1.0 - ADAM_B1) * g
    nv = ADAM_B2 * v + (1.0 - ADAM_B2) * (g * g)
    m_hat = nm * (1.0 / (1.0 - ADAM_B1 ** ADAM_STEP))
    v_hat = nv * (1.0 / (1.0 - ADAM_B2 ** ADAM_STEP))
    return -ADAM_LR * (m_hat / (jnp.sqrt(v_hat) + ADAM_EPS) + ADAM_WD * w), nm, nv


def _adamw(name, w, g, m, v):
    def body(w_ref, g_ref, m_ref, v_ref, d_ref, nm_ref, nv_ref):
        d_ref[...], nm_ref[...], nv_ref[...] = _adam_update(w_ref[...], g_ref[...], m_ref[...], v_ref[...])

    spec = pl.BlockSpec(w.shape, lambda i: (0, 0))
    out = jax.ShapeDtypeStruct(w.shape, F32)
    return pl.pallas_call(
        body, name=name, grid=(1,),
        in_specs=[spec] * 4, out_specs=[spec] * 3, out_shape=[out] * 3,
        compiler_params=_params(("arbitrary",)),
    )(w, g, m, v)


def _adamw_halves(name, ws, mines, theirs, ms, vs, core):
    n = len(ws)
    cols = ws[0].shape[1]
    assert all(w.shape[1] == cols for w in ws)
    hc = cols // 2
    tc = LANES if n > 1 else min(256, hc)
    nt = hc // tc

    def body(core_ref, *refs):
        ins, outs = refs[:5 * n], refs[5 * n:]
        for a in range(n):
            w_ref, mine_ref, theirs_ref, m_ref, v_ref = [ins[j * n + a] for j in range(5)]
            g_ref, d_ref, nm_ref, nv_ref = outs[4 * a:4 * a + 4]
            gv = jnp.where(pl.program_id(0) == core_ref[0], mine_ref[...], theirs_ref[...])
            g_ref[...] = gv
            d_ref[...], nm_ref[...], nv_ref[...] = _adam_update(w_ref[...], gv, m_ref[...], v_ref[...])

    whole = lambda w: pl.BlockSpec((w.shape[0], tc), lambda h, i, core_ref: (0, h * nt + i))
    mine_spec = lambda w: pl.BlockSpec((w.shape[0], tc), lambda h, i, core_ref: (0, jnp.where(h == core_ref[0], i, 0)))
    theirs_spec = lambda w: pl.BlockSpec((w.shape[0], tc), lambda h, i, core_ref: (0, jnp.where(h == core_ref[0], 0, i)))
    outs = pl.pallas_call(
        body, name=name,
        grid_spec=pltpu.PrefetchScalarGridSpec(
            num_scalar_prefetch=1, grid=(2, nt),
            in_specs=[whole(w) for w in ws] + [mine_spec(w) for w in ws] + [theirs_spec(w) for w in ws]
            + [whole(w) for w in ws] * 2,
            out_specs=[whole(w) for w in ws for _ in range(4)]),
        out_shape=[jax.ShapeDtypeStruct(w.shape, F32) for w in ws for _ in range(4)],
        compiler_params=_params(("arbitrary", "arbitrary")),
    )(core, *ws, *mines, *theirs, *ms, *vs)
    return [outs[4 * a:4 * a + 4] for a in range(n)]


WEIGHTS = ("ffn1_norm", "ffn1_gate", "ffn1_up", "ffn1_down", "mix_norm", "w_in", "b_forget", "conv_w",
           "w_o_attn", "w_o_conv", "w_out", "ffn2_norm", "ffn2_gate", "ffn2_up", "ffn2_down", "final_norm")
VEC_ROWS = 8


def _pack_small(t, conv_rows):
    conv = t["conv_w"]
    parts = [t[n].reshape(VEC_ROWS, LANES) for n in NORMS]
    parts.append(jnp.pad(conv, ((0, conv_rows - conv.shape[0]), (0, 0))))
    parts.append(jnp.pad(t["b_forget"].reshape(1, N_HEADS), ((0, 7), (0, LANES - N_HEADS))))
    return jnp.concatenate(parts, axis=0)


def _unpack_small(p, conv_rows):
    out = {n: p[VEC_ROWS * i:VEC_ROWS * (i + 1)].reshape(-1) for i, n in enumerate(NORMS)}
    base = VEC_ROWS * len(NORMS)
    out["conv_w"] = p[base:base + 3]
    out["b_forget"] = p[base + conv_rows, :N_HEADS]
    return out


def _travel(name, a):
    return a.T if name in TRANSPOSED else a


GATHER_FIRST = ("ffn1_gate", "ffn1_up")
GATHER_RIDES = {"ffn1_up": ("ffn1_down",), "ffn1_down": ("w_in",), "mix_proj_fwd": ("w_o_attn", "w_o_conv", "w_out"),
                "attn_fwd": ("ffn2_gate", "ffn2_up", "ffn2_down")}
SIBLING_RIDES = {"ffn2": "mix_out_bwd", "out": None, "w_in": "mix_proj_bwd_dx", "ffn1": None}
CHIP_RIDES = {"ffn2": "attn_bwd", "out": "attn_bwd", "w_in": "ffn1_bwd_dw", "ffn1": None}
SMALL_RIDE = "ffn1_bwd_dw"


class _MeshPlan:
    def __init__(self, wts, core):
        self.small, self.core = wts, core
        self.shards = {n: wts[n].astype(BF16) for n in BIG}
        self.chip_part, self.from_chips, self.rides = {}, {}, {}
        self.stacks = {}
        conv_shard = jnp.pad(wts["conv_w"], ((0, 8 - wts["conv_w"].shape[0]), (0, 0)))
        for kernel_name, names in GATHER_RIDES.items():
            mine = [self.shards[n] for n in names]
            conv = conv_shard if kernel_name == "ffn1_up" else None
            names = names + (("conv_w",) if conv is not None else ())
            mine = mine + ([conv] if conv is not None else [])
            self._ride(kernel_name, _gather_comm(mine[:len(mine) - (conv is not None)], conv),
                       lambda got, names=names, mine=mine: self.stacks.update(zip(names, _fill_own(got, mine))))

    def weights(self, group):
        return _LAYOUTS[group](self.stacks, self.small)

    def ffn1_up(self, x, tm):
        px, py = lax.axis_index("x"), lax.axis_index("y")
        order = jnp.stack([2 * px + py, 2 * (1 - px) + py, 2 * px + (1 - py), 2 * (1 - px) + (1 - py)]).astype(jnp.int32)
        own = [self.shards[n] for n in GATHER_FIRST]
        (hg, hu, n, sg, su), brought = _ffn_up_gather("ffn1_up", x, self.small["ffn1_norm"].reshape(1, -1), *own, order,
                                                     tm, self.rider("ffn1_up"))
        self.stacks.update(zip(GATHER_FIRST, _fill_own([sg, su], own)))
        self.arrived("ffn1_up", brought)
        return hg, hu, n

    def _ride(self, kernel_name, comm, then):
        self.rides.setdefault(kernel_name, []).append((comm, then))

    def rider(self, kernel_name):
        comms = [comm for comm, _ in self.rides.get(kernel_name, [])]
        return _merge_comms(comms) if comms else None

    def arrived(self, kernel_name, results):
        for comm, then in self.rides.pop(kernel_name, []):
            then(results[:len(comm.out_shape)])
            results = results[len(comm.out_shape):]

    def reduce(self, group, grads):
        names = tuple(grads)
        mine = [grads[n] for n in names]

        def with_sibling(from_sibling):
            parts = _add_halves("add_halves_" + group, mine, list(from_sibling), self.core)
            self.chip_part.update(zip(names, parts))
            if CHIP_RIDES[group] is None:
                self.last = (names, _exchange_start("exchange_start_" + group, parts))
            else:
                self._ride(CHIP_RIDES[group], _chip_exchange_comm(parts),
                           lambda got: self.from_chips.update(zip(names, got)))

        if SIBLING_RIDES[group] is None:
            with_sibling(_run_comm("sibling_exchange_" + group, _sibling_exchange_comm(mine)))
        else:
            self._ride(SIBLING_RIDES[group], _sibling_exchange_comm(mine), with_sibling)

    def reduce_parts(self, group, parts):
        self.chip_part.update(parts)
        self.last = (tuple(parts), _exchange_start("exchange_start_" + group, list(parts.values())))

    def reduce_small(self, gs, loss):
        conv_all = _shard_cols(gs["conv_w"]).reshape(N_CHIPS * 8, LANES)
        part = _pack_small({**{n: gs[n] for n in NORMS}, "conv_w": conv_all, "b_forget": gs["b_forget"][0, :N_HEADS]},
                           N_CHIPS * 8)
        part = jnp.concatenate([part, jnp.broadcast_to(loss, (8, LANES))], axis=0)
        me = 4 * lax.axis_index("x") + 2 * lax.axis_index("y") + lax.axis_index("c")

        def landed(got):
            self.small_parts = lax.dynamic_update_index_in_dim(got[0], part, me, 0)

        self._ride(SMALL_RIDE, _small_gather_comm(part), landed)


def kernel(x, ffn1_norm, ffn1_gate, ffn1_up, ffn1_down, mix_norm, w_in, b_forget, conv_w, w_o_attn, w_o_conv, w_out, ffn2_norm, ffn2_gate, ffn2_up, ffn2_down, final_norm, loss_target, m_ffn1_norm, m_ffn1_gate, m_ffn1_up, m_ffn1_down, m_mix_norm, m_w_in, m_b_forget, m_conv_w, m_w_o_attn, m_w_o_conv, m_w_out, m_ffn2_norm, m_ffn2_gate, m_ffn2_up, m_ffn2_down, m_final_norm, v_ffn1_norm, v_ffn1_gate, v_ffn1_up, v_ffn1_down, v_mix_norm, v_w_in, v_b_forget, v_conv_w, v_w_o_attn, v_w_o_conv, v_w_out, v_ffn2_norm, v_ffn2_gate, v_ffn2_up, v_ffn2_down, v_final_norm):
    given = dict(locals())
    wts = {n: _travel(n, given[n]) for n in WEIGHTS}
    mom = {n: _travel(n, given["m_" + n]) for n in WEIGHTS}
    var = {n: _travel(n, given["v_" + n]) for n in WEIGHTS}
    B, S, D = x.shape
    chip = 2 * lax.axis_index("x") + lax.axis_index("y")
    chip1 = chip.astype(jnp.int32).reshape(1)
    core = lax.axis_index("c").astype(jnp.int32).reshape(1)

    plan = _MeshPlan(wts, core)
    loss, grad_x, gs = _local_step(x.reshape(B * S, D), loss_target.reshape(B * S, D), plan, B, S)

    last_names, (send_sems, recv_sems, parts_thru, lands, token) = plan.last
    delta, new_m, new_v, grads = {}, {}, {}, {}

    def finish(tag, names):
        by_cols = {}
        for n in names:
            by_cols.setdefault(wts[n].shape[1], []).append(n)
        mine = {}
        for cols, ns in by_cols.items():
            mine.update(zip(ns, _sum_chips("sum_chips_%s_%d" % (tag, cols), [plan.chip_part[n] for n in ns],
                                           [plan.from_chips[n] for n in ns], chip1, token)))
        theirs = dict(zip(names, _share_halves("share_halves_" + tag, [mine[n] for n in names])))
        raw = []
        for cols, ns in by_cols.items():
            outs = _adamw_halves("adamw_%s_%d" % (tag, cols), [wts[n] for n in ns], [mine[n] for n in ns],
                                 [theirs[n] for n in ns], [mom[n] for n in ns], [var[n] for n in ns], core)
            for n, per in zip(ns, outs):
                raw.append(per[-1])
                grads[n], delta[n], new_m[n], new_v[n] = [_travel(n, o) for o in per]
        return raw

    small_sum = _sum_devices(plan.small_parts)
    base = VEC_ROWS * len(NORMS)
    loss_row = small_sum.shape[0] - 8
    small_grads = _unpack_small(small_sum, N_CHIPS * 8)
    small_grads["conv_w"] = lax.dynamic_slice_in_dim(small_sum[base:base + N_CHIPS * 8], chip * 8, 8, axis=0)[:3]
    packs = [_pack_small(t, 8) for t in (wts, small_grads, mom, var)]
    small_out = _adamw("adamw_small", *packs)

    done = finish("early", [n for n in BIG if n not in last_names])
    parts_back, got = _exchange_wait("exchange_wait", send_sems, recv_sems, parts_thru, lands, done + list(small_out))
    plan.chip_part.update(zip(last_names, parts_back))
    plan.from_chips.update(zip(last_names, got))
    finish("last", last_names)
    grads.update(small_grads)
    for out, p in zip((delta, new_m, new_v), small_out):
        out.update(_unpack_small(p, 8))

    return (small_sum[loss_row, 0], grad_x.reshape(B, S, D), *[grads[n] for n in WEIGHTS], *[delta[n] for n in WEIGHTS],
            *[new_m[n] for n in WEIGHTS], *[new_v[n] for n in WEIGHTS])
```

```python
import functools
import math

import jax
import jax.numpy as jnp
from jax import lax
from jax.experimental import pallas as pl
from jax.experimental.pallas import tpu as pltpu

F32 = jnp.float32
BF16 = jnp.bfloat16
MESH = pl.DeviceIdType.MESH

N_CHIPS = 4
N_DEV = 8
N_HEADS = 8
HEAD_DIM = 64
HEAD_PAIRS = N_HEADS // 2
ATTN_W = N_HEADS * HEAD_DIM
CONV_W = 512
RMS_EPS = 1e-6
FFN_RES = 0.5
LANES = 128
VMEM_LIMIT = 56 * 1024 * 1024
ROW_BLOCK = 256

ADAM_LR = 0.001
ADAM_B1 = 0.9
ADAM_B2 = 0.999
ADAM_EPS = 1e-08
ADAM_WD = 0.01
ADAM_STEP = 10

PROJ_W = 3 * ATTN_W + 3 * CONV_W + 2 * 1024
COL_CB, COL_CC, COL_CX = 3 * ATTN_W, 3 * ATTN_W + CONV_W, 3 * ATTN_W + 2 * CONV_W
COL_GATES = 3 * ATTN_W + 3 * CONV_W
N_FORGET_COL = 3 * ATTN_W


def _params(sem=None, vmem=VMEM_LIMIT):
    return pltpu.CompilerParams(dimension_semantics=sem, vmem_limit_bytes=vmem)


def _dot(a, b):
    return lax.dot_general(a, b, (((1,), (0,)), ((), ())), preferred_element_type=F32)


def _dot_nt(a, b):
    return lax.dot_general(a, b, (((1,), (1,)), ((), ())), preferred_element_type=F32)


def _dot_tn(a, b):
    return lax.dot_general(a, b, (((0,), (0,)), ((), ())), preferred_element_type=F32)


def _sigmoid(x):
    return 1.0 / (1.0 + jnp.exp(-x))


def _rms(xv):
    inv = lax.rsqrt(jnp.mean(xv * xv, axis=-1, keepdims=True) + RMS_EPS)
    return xv * inv, inv


class _Comm:
    def __init__(self, inputs, out_shape, scratch, start, finish):
        self.inputs, self.out_shape, self.scratch = list(inputs), list(out_shape), list(scratch)
        self.start, self.finish = start, finish


def _pallas(body, name, grid, in_specs, out_specs, out_shape, scratch, args, comm=None):
    sem = ("arbitrary",) * len(grid)
    if comm is None:
        outs = pl.pallas_call(body, name=name, grid=grid, in_specs=in_specs, out_specs=out_specs,
                              out_shape=out_shape, scratch_shapes=scratch, compiler_params=_params(sem))(*args)
        return list(outs), []
    n_in, n_out, n_scr = len(in_specs), len(out_specs), len(scratch)
    ci, co = len(comm.inputs), len(comm.out_shape)

    def riding(*refs):
        ins, refs = refs[:n_in], refs[n_in:]
        cins, refs = refs[:ci], refs[ci:]
        outs, refs = refs[:n_out], refs[n_out:]
        couts, refs = refs[:co], refs[co:]
        scr, sems = refs[:n_scr], refs[n_scr:]
        ids = [pl.program_id(d) for d in range(len(grid))]
        first = functools.reduce(lambda a, b: a & b, [i == 0 for i in ids])
        last = functools.reduce(lambda a, b: a & b, [i == g - 1 for i, g in zip(ids, grid)])

        @pl.when(first)
        def _():
            comm.start(cins, couts, sems)

        body(*ins, *outs, *scr)

        @pl.when(last)
        def _():
            comm.finish(cins, couts, sems)

    any_spec = pl.BlockSpec(memory_space=pl.ANY)
    outs = pl.pallas_call(
        riding, name=name, grid=grid,
        in_specs=list(in_specs) + [any_spec] * ci, out_specs=list(out_specs) + [any_spec] * co,
        out_shape=list(out_shape) + comm.out_shape, scratch_shapes=list(scratch) + comm.scratch,
        compiler_params=_params(sem))(*args, *comm.inputs)
    return list(outs[:n_out]), list(outs[n_out:])


def _rms_bwd(dn, xhat, inv, g):
    dxhat = dn * g
    dx = inv * (dxhat - xhat * jnp.mean(dxhat * xhat, axis=-1, keepdims=True))
    return dx, jnp.sum(dn * xhat, axis=0, keepdims=True)


def _ffn_fwd_loss(name, x, g, wgt, wut, wd, target, gf, tm):
    T, D = x.shape
    K, Fs, _ = wgt.shape

    def body(x_ref, g_ref, wg_ref, wu_ref, wd_ref, t_ref, gf_ref,
             dx_ref, hg_ref, hu_ref, n_ref, loss_ref, dgf_ref, acc_scr):
        i, k = pl.program_id(0), pl.program_id(1)

        @pl.when(k == 0)
        def _():
            xhat, _ = _rms(x_ref[...])
            n_ref[...] = (xhat * g_ref[...]).astype(BF16)
            acc_scr[...] = jnp.zeros_like(acc_scr)

        @pl.when((k == 0) & (i == 0))
        def _():
            loss_ref[...] = jnp.zeros_like(loss_ref)
            dgf_ref[...] = jnp.zeros_like(dgf_ref)

        n = n_ref[...]
        hg = _dot_nt(n, wg_ref[...])
        hu = _dot_nt(n, wu_ref[...])
        hg_ref[...] = hg.astype(BF16)
        hu_ref[...] = hu.astype(BF16)
        act = (hg * _sigmoid(hg) * hu).astype(BF16)
        acc_scr[...] += _dot(act, wd_ref[...])

        @pl.when(k == K - 1)
        def _():
            gfv = gf_ref[...]
            for r0 in range(0, tm, ROW_BLOCK):
                rows = slice(r0, r0 + ROW_BLOCK)
                xhat, inv = _rms(x_ref[rows, :] + FFN_RES * acc_scr[rows, :])
                err = xhat * gfv - t_ref[rows, :]
                loss_ref[...] += 0.5 * jnp.sum(jnp.sum(err * err, axis=1, keepdims=True), axis=0, keepdims=True) / D
                dx, dg = _rms_bwd(err * (1.0 / D), xhat, inv, gfv)
                dx_ref[rows, :] = dx
                dgf_ref[...] += dg

    w_spec = pl.BlockSpec((None, Fs, D), lambda i, k: (k, 0, 0))
    act_spec = pl.BlockSpec((None, tm, Fs), lambda i, k: (k, i, 0))
    row = pl.BlockSpec((tm, D), lambda i, k: (i, 0))
    vec = pl.BlockSpec((1, D), lambda i, k: (0, 0))
    return _pallas(
        body, name, (T // tm, K),
        [row, vec, w_spec, w_spec, w_spec, row, vec],
        [row, act_spec, act_spec, row, pl.BlockSpec((1, LANES), lambda i, k: (0, 0)), vec],
        [jax.ShapeDtypeStruct((T, D), F32), jax.ShapeDtypeStruct((K, T, Fs), BF16),
         jax.ShapeDtypeStruct((K, T, Fs), BF16), jax.ShapeDtypeStruct((T, D), BF16),
         jax.ShapeDtypeStruct((1, LANES), F32), jax.ShapeDtypeStruct((1, D), F32)],
        [pltpu.VMEM((tm, D), F32)],
        (x, g, wgt, wut, wd, target, gf))[0]


def _ffn_up_gather(name, x, g, wg_own, wu_own, order, tm, comm=None):
    T, D = x.shape
    Fs = wg_own.shape[0]
    nt = T // tm
    ci, co = (len(comm.inputs), len(comm.out_shape)) if comm is not None else (0, 0)

    def body(order_ref, x_ref, g_ref, wgo_ref, wuo_ref, *rest):
        cins, rest = rest[:ci], rest[ci:]
        (hg_ref, hu_ref, n_ref, sg_ref, su_ref), rest = rest[:5], rest[5:]
        couts, rest = rest[:co], rest[co:]
        (n_all, wbuf, send_sems, recv_sems, pass_send, pass_recv, load_sems), csems = rest[:7], rest[7:]
        k, i = pl.program_id(0), pl.program_id(1)
        x_pos, y_pos, c, others = _place()
        me = 2 * x_pos + y_pos
        owns, stacks = (wgo_ref, wuo_ref), (sg_ref, su_ref)
        mine, theirs = _col_halves(D, c)

        def chip_copy(a, j, chip):
            return pltpu.make_async_remote_copy(
                src_ref=owns[a].at[:, mine], dst_ref=stacks[a].at[chip, :, mine],
                send_sem=send_sems.at[3 * a + j], recv_sem=recv_sems.at[3 * a + j],
                device_id=(*others[j], c), device_id_type=MESH)

        def pass_copy(a, j, chip, half):
            return pltpu.make_async_remote_copy(
                src_ref=stacks[a].at[chip, :, half], dst_ref=stacks[a].at[chip, :, half],
                send_sem=pass_send.at[3 * a + j], recv_sem=pass_recv.at[3 * a + j],
                device_id=(x_pos, y_pos, 1 - c), device_id_type=MESH)

        @pl.when((k == 0) & (i == 0))
        def _():
            for a in range(2):
                for j in range(3):
                    chip_copy(a, j, me).start()
            if comm is not None:
                comm.start(cins, couts, csems)

        def bring(j):
            ox, oy = others[j]
            chip = 2 * ox + oy
            for a in range(2):
                chip_copy(a, j, chip).wait_recv()
            for a in range(2):
                pass_copy(a, j, chip, mine).start()
            for a in range(2):
                pass_copy(a, j, chip, theirs).wait_recv()
            loads = [pltpu.make_async_copy(stacks[a].at[chip], wbuf.at[j % 2, a], load_sems.at[2 * (j % 2) + a])
                     for a in range(2)]
            for cp in loads:
                cp.start()
            for cp in loads:
                cp.wait()

        @pl.when((k == 1) & (i == 0))
        def _():
            bring(0)
            bring(1)

        @pl.when((k == 2) & (i == nt - 1))
        def _():
            bring(2)

        rows = pl.ds(pl.multiple_of(i * tm, tm), tm)

        @pl.when(k == 0)
        def _():
            xhat, _ = _rms(x_ref[...])
            n = (xhat * g_ref[...]).astype(BF16)
            n_ref[...] = n
            n_all[rows, :] = n
            hg_ref[...] = _dot_nt(n, wgo_ref[...]).astype(BF16)
            hu_ref[...] = _dot_nt(n, wuo_ref[...]).astype(BF16)

        @pl.when(k > 0)
        def _():
            n = n_all[rows, :]
            slot = (k - 1) % 2
            hg_ref[...] = _dot_nt(n, wbuf[slot, 0]).astype(BF16)
            hu_ref[...] = _dot_nt(n, wbuf[slot, 1]).astype(BF16)

        @pl.when((k == N_CHIPS - 1) & (i == nt - 1))
        def _():
            for a in range(2):
                for j, (ox, oy) in enumerate(others):
                    chip_copy(a, j, me).wait_send()
                    pass_copy(a, j, 2 * ox + oy, mine).wait_send()
            if comm is not None:
                comm.finish(cins, couts, csems)

    any_spec = pl.BlockSpec(memory_space=pl.ANY)
    first_pass = lambda k, i, order_ref: (jnp.where(k == 0, i, nt - 1), 0)
    whole = pl.BlockSpec((Fs, D), lambda k, i, order_ref: (0, 0))
    act_spec = pl.BlockSpec((None, tm, Fs), lambda k, i, order_ref: (order_ref[k], i, 0))
    stack = jax.ShapeDtypeStruct((N_CHIPS, Fs, D), BF16)
    outs = pl.pallas_call(
        body, name=name,
        grid_spec=pltpu.PrefetchScalarGridSpec(
            num_scalar_prefetch=1, grid=(N_CHIPS, nt),
            in_specs=[pl.BlockSpec((tm, D), first_pass), pl.BlockSpec((1, D), lambda k, i, order_ref: (0, 0)),
                      whole, whole] + [any_spec] * ci,
            out_specs=[act_spec, act_spec, pl.BlockSpec((tm, D), first_pass), any_spec, any_spec] + [any_spec] * co,
            scratch_shapes=[pltpu.VMEM((T, D), BF16), pltpu.VMEM((2, 2, Fs, D), BF16)]
            + [pltpu.SemaphoreType.DMA((6,))] * 4 + [pltpu.SemaphoreType.DMA((4,))]
            + (comm.scratch if comm is not None else [])),
        out_shape=[jax.ShapeDtypeStruct((N_CHIPS, T, Fs), BF16), jax.ShapeDtypeStruct((N_CHIPS, T, Fs), BF16),
                   jax.ShapeDtypeStruct((T, D), BF16), stack, stack] + (comm.out_shape if comm is not None else []),
        compiler_params=_params(("arbitrary", "arbitrary")),
    )(order, x, g, wg_own, wu_own, *(comm.inputs if comm is not None else []))
    return list(outs[:5]), list(outs[5:])


def _ffn_down(name, x, hg, hu, wd, tm, comm=None):
    T, D = x.shape
    K, Fs, _ = wd.shape

    def body(x_ref, hg_ref, hu_ref, wd_ref, out_ref, acc_scr):
        k = pl.program_id(1)

        @pl.when(k == 0)
        def _():
            acc_scr[...] = jnp.zeros_like(acc_scr)

        hgv = hg_ref[...].astype(F32)
        act = (hgv * _sigmoid(hgv) * hu_ref[...].astype(F32)).astype(BF16)
        acc_scr[...] += _dot(act, wd_ref[...])

        @pl.when(k == K - 1)
        def _():
            out_ref[...] = x_ref[...] + FFN_RES * acc_scr[...]

    act_spec = pl.BlockSpec((None, tm, Fs), lambda i, k: (k, i, 0))
    row = pl.BlockSpec((tm, D), lambda i, k: (i, 0))
    return _pallas(
        body, name, (T // tm, K),
        [row, act_spec, act_spec, pl.BlockSpec((None, Fs, D), lambda i, k: (k, 0, 0))],
        [row], [jax.ShapeDtypeStruct((T, D), F32)], [pltpu.VMEM((tm, D), F32)],
        (x, hg, hu, wd), comm)


def _ffn_bwd_dx(name, dout, x, g, hg, hu, wgt, wut, wd, tm, comm=None):
    T, D = x.shape
    K, Fs, _ = wgt.shape

    def body(dout_ref, x_ref, g_ref, hg_ref, hu_ref, wg_ref, wu_ref, wd_ref,
             dx_ref, dhg_ref, dhu_ref, dg_ref, df_ref, dn_scr):
        i, k = pl.program_id(0), pl.program_id(1)

        @pl.when(k == 0)
        def _():
            df_ref[...] = (FFN_RES * dout_ref[...]).astype(BF16)
            dn_scr[...] = jnp.zeros_like(dn_scr)

        @pl.when((k == 0) & (i == 0))
        def _():
            dg_ref[...] = jnp.zeros_like(dg_ref)

        for r0 in range(0, tm, ROW_BLOCK):
            rows = slice(r0, r0 + ROW_BLOCK)
            dact = _dot_nt(df_ref[rows, :], wd_ref[...])
            hgv = hg_ref[rows, :].astype(F32)
            huv = hu_ref[rows, :].astype(F32)
            s = _sigmoid(hgv)
            dhu = (dact * (hgv * s)).astype(BF16)
            dhg = (dact * huv * (s * (1.0 + hgv * (1.0 - s)))).astype(BF16)
            dhg_ref[rows, :] = dhg
            dhu_ref[rows, :] = dhu
            dn_scr[rows, :] += _dot(dhg, wg_ref[...]) + _dot(dhu, wu_ref[...])

        @pl.when(k == K - 1)
        def _():
            xhat, inv = _rms(x_ref[...])
            dx, dg = _rms_bwd(dn_scr[...], xhat, inv, g_ref[...])
            dx_ref[...] = dout_ref[...] + dx
            dg_ref[...] += dg

    w_spec = pl.BlockSpec((None, Fs, D), lambda i, k: (k, 0, 0))
    act_spec = pl.BlockSpec((None, tm, Fs), lambda i, k: (k, i, 0))
    row = pl.BlockSpec((tm, D), lambda i, k: (i, 0))
    row_once = pl.BlockSpec((tm, D), lambda i, k: (i, 0), pipeline_mode=pl.Buffered(1))
    vec = pl.BlockSpec((1, D), lambda i, k: (0, 0))
    return _pallas(
        body, name, (T // tm, K),
        [row, row_once, vec, act_spec, act_spec, w_spec, w_spec, w_spec],
        [row_once, act_spec, act_spec, vec, row],
        [jax.ShapeDtypeStruct((T, D), F32), jax.ShapeDtypeStruct((K, T, Fs), BF16),
         jax.ShapeDtypeStruct((K, T, Fs), BF16), jax.ShapeDtypeStruct((1, D), F32),
         jax.ShapeDtypeStruct((T, D), BF16)],
        [pltpu.VMEM((tm, D), F32)],
        (dout, x, g, hg, hu, wgt, wut, wd), comm)


def _ffn_bwd_dw(name, n, df, hg, hu, dhg, dhu, tk, comm=None):
    T, D = n.shape
    K, _, Fs = hg.shape
    nt = T // tk

    def body(n_ref, df_ref, hg_ref, hu_ref, dhg_ref, dhu_ref, dwg_ref, dwu_ref, dwd_ref, accg, accu, accd):
        t = pl.program_id(1)

        @pl.when(t == 0)
        def _():
            accg[...] = jnp.zeros_like(accg)
            accu[...] = jnp.zeros_like(accu)
            accd[...] = jnp.zeros_like(accd)

        nv = n_ref[...]
        hgv = hg_ref[...].astype(F32)
        act = (hgv * _sigmoid(hgv) * hu_ref[...].astype(F32)).astype(BF16)
        accg[...] += _dot_tn(dhg_ref[...], nv)
        accu[...] += _dot_tn(dhu_ref[...], nv)
        accd[...] += _dot_tn(act, df_ref[...])

        @pl.when(t == nt - 1)
        def _():
            dwg_ref[...] = accg[...].astype(BF16)
            dwu_ref[...] = accu[...].astype(BF16)
            dwd_ref[...] = accd[...].astype(BF16)

    act_spec = pl.BlockSpec((None, tk, Fs), lambda k, t: (k, t, 0))
    w_spec = pl.BlockSpec((None, Fs, D), lambda k, t: (k, 0, 0))
    row = pl.BlockSpec((tk, D), lambda k, t: (t, 0))
    return _pallas(
        body, name, (K, nt),
        [row, row, act_spec, act_spec, act_spec, act_spec],
        [w_spec, w_spec, w_spec],
        [jax.ShapeDtypeStruct((K, Fs, D), BF16)] * 3,
        [pltpu.VMEM((Fs, D), F32)] * 3,
        (n, df, hg, hu, dhg, dhu), comm)


def _ffn_bwd_dw_reduced(name, n, df, hg, hu, dhg, dhu, tk, comm=None):
    T, D = n.shape
    K, _, Fs = hg.shape
    nt = T // tk
    hc = D // 2
    assert nt >= 2, "a pass's swap is finished at the second step of the next pass"

    def body(n_ref, df_ref, hg_ref, hu_ref, dhg_ref, dhu_ref, pg_ref, pu_ref, pd_ref,
             accg, accu, accd, stage, own, land, send_sems, recv_sems):
        k, t = pl.program_id(0), pl.program_id(1)
        x, y, c, _ = _place()
        accs, outs = (accg, accu, accd), (pg_ref, pu_ref, pd_ref)

        def swap(j, a):
            return pltpu.make_async_remote_copy(
                src_ref=stage.at[a], dst_ref=land.at[j % 2, a],
                send_sem=send_sems.at[3 * j + a], recv_sem=recv_sems.at[3 * j + a],
                device_id=(x, y, 1 - c), device_id_type=MESH)

        def finish(j):
            for a in range(3):
                swap(j, a).wait_recv()
                outs[a][j] = (own[a] + land[j % 2, a].astype(F32)).astype(BF16)

        @pl.when(t == 0)
        def _():
            for acc in accs:
                acc[...] = jnp.zeros_like(acc)

        nv = n_ref[...]
        hgv = hg_ref[...].astype(F32)
        act = (hgv * _sigmoid(hgv) * hu_ref[...].astype(F32)).astype(BF16)
        accg[...] += _dot_tn(dhg_ref[...], nv)
        accu[...] += _dot_tn(dhu_ref[...], nv)
        accd[...] += _dot_tn(act, df_ref[...])

        for j in range(K - 1):
            @pl.when((k == j + 1) & (t == 1))
            def _(j=j):
                finish(j)

        @pl.when(t == nt - 1)
        def _():
            for j in range(K - 1):
                @pl.when(k == j + 1)
                def _(j=j):
                    for a in range(3):
                        swap(j, a).wait_send()
            for a in range(3):
                @pl.when(c == 0)
                def _(a=a):
                    own[a] = accs[a][:, :hc]
                    stage[a] = accs[a][:, hc:].astype(BF16)

                @pl.when(c == 1)
                def _(a=a):
                    own[a] = accs[a][:, hc:]
                    stage[a] = accs[a][:, :hc].astype(BF16)
            for j in range(K):
                @pl.when(k == j)
                def _(j=j):
                    for a in range(3):
                        swap(j, a).start()

        @pl.when((k == K - 1) & (t == nt - 1))
        def _():
            finish(K - 1)
            for a in range(3):
                swap(K - 1, a).wait_send()

    act_spec = pl.BlockSpec((None, tk, Fs), lambda k, t: (k, t, 0))
    row = pl.BlockSpec((tk, D), lambda k, t: (t, 0))
    resident = pl.BlockSpec(memory_space=pltpu.VMEM)
    part = jax.ShapeDtypeStruct((K, Fs, hc), BF16)
    return _pallas(
        body, name, (K, nt),
        [row, row, act_spec, act_spec, act_spec, act_spec],
        [resident, resident, resident], [part, part, part],
        [pltpu.VMEM((Fs, D), F32)] * 3
        + [pltpu.VMEM((3, Fs, hc), BF16), pltpu.VMEM((3, Fs, hc), F32), pltpu.VMEM((2, 3, Fs, hc), BF16),
           pltpu.SemaphoreType.DMA((3 * K,)), pltpu.SemaphoreType.DMA((3 * K,))],
        (n, df, hg, hu, dhg, dhu), comm)


def _mix_proj_fwd(x, g, wproj_t, wf_t, tm, tn, comm=None):
    T, D = x.shape
    N = wproj_t.shape[0]

    def body(x_ref, g_ref, w_ref, wf_ref, h_ref, proj_ref, flog_ref, h_scr):
        @pl.when(pl.program_id(1) == 0)
        def _():
            xhat, _ = _rms(x_ref[...])
            h = (xhat * g_ref[...]).astype(BF16)
            h_scr[...] = h
            h_ref[...] = h
            flog_ref[...] = _dot_nt(h, wf_ref[...])

        proj_ref[...] = _dot_nt(h_scr[...], w_ref[...]).astype(BF16)

    return _pallas(
        body, "mix_proj_fwd", (T // tm, N // tn),
        [pl.BlockSpec((tm, D), lambda i, n: (i, 0)), pl.BlockSpec((1, D), lambda i, n: (0, 0)),
         pl.BlockSpec((tn, D), lambda i, n: (n, 0)), pl.BlockSpec((LANES, D), lambda i, n: (0, 0))],
        [pl.BlockSpec((tm, D), lambda i, n: (i, 0)), pl.BlockSpec((tm, tn), lambda i, n: (i, n)),
         pl.BlockSpec((tm, LANES), lambda i, n: (i, 0))],
        [jax.ShapeDtypeStruct((T, D), BF16), jax.ShapeDtypeStruct((T, N), BF16),
         jax.ShapeDtypeStruct((T, LANES), F32)],
        [pltpu.VMEM((tm, D), BF16)],
        (x, g, wproj_t, wf_t), comm)


def _log_sigmoid(z):
    return -(jnp.maximum(-z, 0.0) + jnp.log(1.0 + jnp.exp(-jnp.abs(z))))


def _tri(n, lower):
    r = lax.broadcasted_iota(jnp.int32, (n, n), 0)
    c = lax.broadcasted_iota(jnp.int32, (n, n), 1)
    return jnp.where((r >= c) if lower else (r <= c), 1.0, 0.0).astype(F32)


def _dot_f32(a, b):
    return lax.dot_general(a, b, (((1,), (0,)), ((), ())), preferred_element_type=F32,
                           precision=lax.Precision.HIGHEST)


def _fgate_fwd(flog, bias, B, S, ch):
    def body(flog_ref, b_ref, cum_ref):
        tri = _tri(ch, True)
        carry = jnp.zeros((1, LANES), F32)
        for c0 in range(0, S, ch):
            lf = _log_sigmoid(flog_ref[c0:c0 + ch, :] + b_ref[...])
            cs = _dot_f32(tri, lf) + carry
            cum_ref[c0:c0 + ch, :] = cs
            carry = cs[ch - 1:ch, :]

    return pl.pallas_call(
        body, name="fgate_fwd", grid=(B,),
        in_specs=[pl.BlockSpec((S, LANES), lambda b: (b, 0)),
                  pl.BlockSpec((1, LANES), lambda b: (0, 0))],
        out_specs=pl.BlockSpec((S, LANES), lambda b: (b, 0)),
        out_shape=jax.ShapeDtypeStruct((B * S, LANES), F32),
        compiler_params=_params(("arbitrary",)),
    )(flog, bias)


def _fgate_bwd(dcum, flog, bias, B, S, ch):
    def body(dcum_ref, flog_ref, b_ref, dflog_ref, db_ref):
        @pl.when(pl.program_id(0) == 0)
        def _():
            db_ref[...] = jnp.zeros_like(db_ref)

        tri = _tri(ch, False)
        carry = jnp.zeros((1, LANES), F32)
        db = jnp.zeros((1, LANES), F32)
        for c0 in range(S - ch, -1, -ch):
            dlf = _dot_f32(tri, dcum_ref[c0:c0 + ch, :]) + carry
            carry = dlf[0:1, :]
            z = flog_ref[c0:c0 + ch, :] + b_ref[...]
            dz = dlf * _sigmoid(-z)
            dflog_ref[c0:c0 + ch, :] = dz
            db = db + jnp.sum(dz, axis=0, keepdims=True)
        db_ref[...] += db

    return pl.pallas_call(
        body, name="fgate_bwd", grid=(B,),
        in_specs=[pl.BlockSpec((S, LANES), lambda b: (b, 0)),
                  pl.BlockSpec((S, LANES), lambda b: (b, 0)),
                  pl.BlockSpec((1, LANES), lambda b: (0, 0))],
        out_specs=[pl.BlockSpec((S, LANES), lambda b: (b, 0)),
                   pl.BlockSpec((1, LANES), lambda b: (0, 0))],
        out_shape=[jax.ShapeDtypeStruct((B * S, LANES), F32),
                   jax.ShapeDtypeStruct((1, LANES), F32)],
        compiler_params=_params(("arbitrary",)),
    )(dcum, flog, bias)


def _pick_lane(tile, h):
    lane = lax.broadcasted_iota(jnp.int32, tile.shape, 1)
    return jnp.sum(jnp.where(lane == h, tile, 0.0), axis=1, keepdims=True)


def _put_lane(col, h, width=LANES):
    lane = lax.broadcasted_iota(jnp.int32, (col.shape[0], width), 1)
    return jnp.where(lane == h, col, 0.0)


def _pick_row(tile, h):
    row = lax.broadcasted_iota(jnp.int32, tile.shape, 0)
    return jnp.sum(jnp.where(row == h, tile, 0.0), axis=0, keepdims=True)


def _put_row(vec, h):
    row = lax.broadcasted_iota(jnp.int32, (8, vec.shape[1]), 0)
    return jnp.where(row == h, vec, 0.0)


def _causal(tq):
    r = lax.broadcasted_iota(jnp.int32, (tq, tq), 0)
    c = lax.broadcasted_iota(jnp.int32, (tq, tq), 1)
    return r >= c


def _head_halves(t):
    lo = lax.broadcasted_iota(jnp.int32, t.shape, 1) < HEAD_DIM
    zero = jnp.zeros_like(t)
    return jnp.where(lo, t, zero), jnp.where(lo, zero, t)


NEG = -1e30
ATTN_SCALE = 1.0 / math.sqrt(HEAD_DIM)


def _scaled(q):
    return (q.astype(F32) * ATTN_SCALE).astype(q.dtype)


def _attn_fwd(proj, cum, cum_t, B, S, tq, comm=None):
    nq = S // tq

    def body(q_ref, k_ref, v_ref, cum_ref, cumt_ref, o_ref, lse_ref):
        qi, hp = pl.program_id(1), pl.program_id(2)
        qm = _head_halves(_scaled(q_ref[...]))
        first_head = lax.broadcasted_iota(jnp.int32, (LANES, tq), 0) < HEAD_DIM
        r = lax.broadcasted_iota(jnp.int32, (tq, tq), 0)
        c = lax.broadcasted_iota(jnp.int32, (tq, tq), 1)

        def tile(j, carry, masked):
            (ma, la), (mb, lb), acc = carry
            off = pl.multiple_of(j * tq, tq)
            kj = k_ref[pl.ds(off, tq), :]
            vm = _head_halves(v_ref[pl.ds(off, tq), :])
            cumk = cum_ref[pl.ds(off, tq), :]
            new, alphas, pv = [], [], jnp.zeros((LANES, tq), F32)
            for e, (m, l) in enumerate(((ma, la), (mb, lb))):
                s = _dot_nt(kj, qm[e]) - _pick_lane(cumk, 2 * hp + e)
                if masked:
                    s = jnp.where(r <= c, s, NEG)
                m_new = jnp.maximum(m, jnp.max(s, axis=0, keepdims=True))
                p = jnp.exp(s - m_new)
                alpha = jnp.exp(m - m_new)
                new.append((m_new, alpha * l + jnp.sum(p, axis=0, keepdims=True)))
                alphas.append(alpha)
                pv = pv + _dot_tn(vm[e], p.astype(BF16))
            acc = jnp.where(first_head, alphas[0], alphas[1]) * acc + pv
            return new[0], new[1], acc

        one = (jnp.full((1, tq), NEG, F32), jnp.zeros((1, tq), F32))
        carry = lax.fori_loop(0, qi, lambda j, cr: tile(j, cr, False), (one, one, jnp.zeros((LANES, tq), F32)))
        (ma, la), (mb, lb), acc = tile(qi, carry, True)
        o_ref[...] = (acc / jnp.where(first_head, la, lb)).T.astype(BF16)

        @pl.when(hp == 0)
        def _():
            lse_ref[...] = jnp.zeros_like(lse_ref)

        ct = cumt_ref[...]
        lse_ref[...] += (_put_row(ma + jnp.log(la) + _pick_row(ct, 2 * hp), 2 * hp)
                         + _put_row(mb + jnp.log(lb) + _pick_row(ct, 2 * hp + 1), 2 * hp + 1))

    kv = lambda first: pl.BlockSpec((S, LANES), lambda b, i, hp: (b, first + hp))
    row_block = pl.BlockSpec((None, None, 8, tq), lambda b, i, hp: (b, i, 0, 0))
    return _pallas(
        body, "attn_fwd", (B, nq, HEAD_PAIRS),
        [pl.BlockSpec((tq, LANES), lambda b, i, hp: (b * nq + i, hp)),
         kv(ATTN_W // LANES), kv(2 * ATTN_W // LANES),
         pl.BlockSpec((S, LANES), lambda b, i, hp: (b, 0)), row_block],
        [pl.BlockSpec((tq, LANES), lambda b, i, hp: (b * nq + i, hp)), row_block],
        [jax.ShapeDtypeStruct((B * S, ATTN_W), BF16), jax.ShapeDtypeStruct((B, nq, 8, tq), F32)],
        [], (proj, proj, proj, cum, cum_t), comm)


def _attn_bwd(proj, o, do, lse, cum, cum_t, B, S, tq, comm=None):
    nq = S // tq

    def body(q_ref, k_ref, v_ref, o_ref, do_ref, lse_ref, cum_ref, cumt_ref,
             dq_ref, dk_ref, dv_ref, dcq_ref, dck_ref, dq_scr):
        hp, kj = pl.program_id(1), pl.program_id(2)

        @pl.when(kj == 0)
        def _():
            dq_scr[...] = jnp.zeros_like(dq_scr)

        @pl.when((kj == 0) & (hp == 0))
        def _():
            dcq_ref[...] = jnp.zeros_like(dcq_ref)
            dck_ref[...] = jnp.zeros_like(dck_ref)

        kv = k_ref[...]
        vv = v_ref[...]
        km = _head_halves(kv)
        ct = cumt_ref[...]
        ck = [_pick_row(ct, 2 * hp + e) for e in range(2)]

        def tile(i, carry, masked):
            dk, dv, dcol = carry
            off = pl.multiple_of(i * tq, tq)
            qi = q_ref[pl.ds(off, tq), :]
            ov = o_ref[pl.ds(off, tq), :].astype(F32)
            qm = _head_halves(_scaled(qi))
            dom = _head_halves(do_ref[pl.ds(off, tq), :])
            cumv = cum_ref[pl.ds(off, tq), :]
            lsev = lse_ref[pl.ds(off, tq), :]
            dcq = jnp.zeros((tq, LANES), F32)
            dq = jnp.zeros((tq, LANES), F32)
            dcol_new = []
            for e in range(2):
                delta = jnp.sum(dom[e].astype(F32) * ov, axis=1, keepdims=True)
                row_term = _pick_lane(cumv, 2 * hp + e) - _pick_lane(lsev, 2 * hp + e)
                p = jnp.exp(_dot_nt(qm[e], kv) + row_term - ck[e])
                if masked:
                    p = jnp.where(_causal(tq), p, 0.0)
                dv = dv + _dot_tn(dom[e], p.astype(BF16))
                ds = p * (_dot_nt(dom[e], vv) - delta)
                dcol_new.append(dcol[e] + jnp.sum(ds, axis=0, keepdims=True))
                dcq = dcq + _put_lane(jnp.sum(ds, axis=1, keepdims=True), 2 * hp + e)
                dsb = ds.astype(BF16)
                dk = dk + _dot_tn(qm[e], dsb)
                dq = dq + _dot(dsb, km[e]) * ATTN_SCALE
            dq_scr[pl.ds(off, tq), :] += dq
            dcq_ref[pl.ds(off, tq), :] += dcq
            return dk, dv, tuple(dcol_new)

        zero_row = jnp.zeros((1, tq), F32)
        init = (jnp.zeros((LANES, tq), F32), jnp.zeros((LANES, tq), F32), (zero_row, zero_row))
        carry = tile(kj, init, True)
        dk, dv, dcol = lax.fori_loop(kj + 1, nq, lambda i, c: tile(i, c, False), carry)
        dk_ref[...] = dk.T.astype(BF16)
        dv_ref[...] = dv.T.astype(BF16)
        dck_ref[kj] += -(_put_row(dcol[0], 2 * hp) + _put_row(dcol[1], 2 * hp + 1))

        @pl.when(kj == nq - 1)
        def _():
            dq_ref[...] = dq_scr[...].astype(BF16)

    seq = lambda first: pl.BlockSpec((S, LANES), lambda b, hp, j: (b, first + hp))
    tile_in = lambda first: pl.BlockSpec((tq, LANES), lambda b, hp, j: (b * nq + j, first + hp))
    lanes0 = pl.BlockSpec((S, LANES), lambda b, hp, j: (b, 0))
    out = jax.ShapeDtypeStruct((B * S, ATTN_W), BF16)
    return _pallas(
        body, "attn_bwd", (B, HEAD_PAIRS, nq),
        [seq(0), tile_in(ATTN_W // LANES), tile_in(2 * ATTN_W // LANES), seq(0), seq(0), lanes0, lanes0,
         pl.BlockSpec((None, None, 8, tq), lambda b, hp, j: (b, j, 0, 0))],
        [seq(0), tile_in(0), tile_in(0), lanes0,
         pl.BlockSpec((None, nq, 8, tq), lambda b, hp, j: (b, 0, 0, 0))],
        [out, out, out, jax.ShapeDtypeStruct((B * S, LANES), F32), jax.ShapeDtypeStruct((B, nq, 8, tq), F32)],
        [pltpu.VMEM((S, LANES), F32)],
        (proj, proj, proj, o, do, lse, cum, cum_t), comm)


def _shift_down(u, n):
    row = lax.broadcasted_iota(jnp.int32, u.shape, 0)
    return jnp.where(row >= n, pltpu.roll(u, n, 0), 0.0)


def _shift_up(u, n):
    rows = u.shape[0]
    row = lax.broadcasted_iota(jnp.int32, u.shape, 0)
    return jnp.where(row < rows - n, pltpu.roll(u, rows - n, 0), 0.0)


def _conv_specs(S):
    cb = pl.BlockSpec((S, LANES), lambda g, b: (b, COL_CB // LANES + g))
    cc = pl.BlockSpec((S, LANES), lambda g, b: (b, COL_CC // LANES + g))
    cx = pl.BlockSpec((S, LANES), lambda g, b: (b, COL_CX // LANES + g))
    w = pl.BlockSpec((8, LANES), lambda g, b: (0, g))
    return cb, cc, cx, w


def _conv_fwd(proj, conv_w, B, S):
    def body(cb_ref, cc_ref, cx_ref, w_ref, y_ref):
        u = cc_ref[...].astype(F32) * cx_ref[...].astype(F32)
        w = w_ref[...]
        conv = w[0:1, :] * _shift_down(u, 2) + w[1:2, :] * _shift_down(u, 1) + w[2:3, :] * u
        y_ref[...] = (cb_ref[...].astype(F32) * conv).astype(BF16)

    cb, cc, cx, w = _conv_specs(S)
    return pl.pallas_call(
        body, name="conv_fwd", grid=(CONV_W // LANES, B),
        in_specs=[cb, cc, cx, w],
        out_specs=pl.BlockSpec((S, LANES), lambda g, b: (b, g)),
        out_shape=jax.ShapeDtypeStruct((B * S, CONV_W), BF16),
        compiler_params=_params(("arbitrary", "arbitrary")),
    )(proj, proj, proj, conv_w)


def _conv_bwd(dy, proj, conv_w, B, S):
    def body(dy_ref, cb_ref, cc_ref, cx_ref, w_ref, dcb_ref, dcc_ref, dcx_ref, dw_ref):
        @pl.when(pl.program_id(1) == 0)
        def _():
            dw_ref[...] = jnp.zeros_like(dw_ref)

        ccv = cc_ref[...].astype(F32)
        cxv = cx_ref[...].astype(F32)
        u = ccv * cxv
        u1 = _shift_down(u, 1)
        u2 = _shift_down(u, 2)
        w = w_ref[...]
        conv = w[0:1, :] * u2 + w[1:2, :] * u1 + w[2:3, :] * u
        dyv = dy_ref[...].astype(F32)
        dcb_ref[...] = (dyv * conv).astype(BF16)
        dconv = dyv * cb_ref[...].astype(F32)
        du = w[2:3, :] * dconv + w[1:2, :] * _shift_up(dconv, 1) + w[0:1, :] * _shift_up(dconv, 2)
        dcc_ref[...] = (du * cxv).astype(BF16)
        dcx_ref[...] = (du * ccv).astype(BF16)
        row = lax.broadcasted_iota(jnp.int32, (8, LANES), 0)
        dw = jnp.where(row == 0, jnp.sum(dconv * u2, axis=0, keepdims=True),
                       jnp.where(row == 1, jnp.sum(dconv * u1, axis=0, keepdims=True),
                                 jnp.where(row == 2, jnp.sum(dconv * u, axis=0, keepdims=True), 0.0)))
        dw_ref[...] += dw

    cb, cc, cx, w = _conv_specs(S)
    out = pl.BlockSpec((S, LANES), lambda g, b: (b, g))
    return pl.pallas_call(
        body, name="conv_bwd", grid=(CONV_W // LANES, B),
        in_specs=[out, cb, cc, cx, w],
        out_specs=[out, out, out, w],
        out_shape=[jax.ShapeDtypeStruct((B * S, CONV_W), BF16)] * 3 + [jax.ShapeDtypeStruct((8, CONV_W), F32)],
        compiler_params=_params(("arbitrary", "arbitrary")),
    )(dy, proj, proj, proj, conv_w)


def _gate_specs(tm, D):
    ga = pl.BlockSpec((tm, D), lambda i: (i, COL_GATES // D))
    gc = pl.BlockSpec((tm, D), lambda i: (i, COL_GATES // D + 1))
    return ga, gc


def _mix_out_fwd(x, o, yc, proj, woa, woc, wout, tm):
    T, D = x.shape

    def body(x_ref, o_ref, yc_ref, ga_ref, gc_ref, woa_ref, woc_ref, wout_ref, out_ref):
        ya = _dot(o_ref[...], woa_ref[...])
        yp = _dot(yc_ref[...], woc_ref[...])
        merged = _sigmoid(ga_ref[...].astype(F32)) * ya + _sigmoid(gc_ref[...].astype(F32)) * yp
        out_ref[...] = x_ref[...] + _dot(merged.astype(BF16), wout_ref[...])

    ga, gc = _gate_specs(tm, D)
    row = lambda w: pl.BlockSpec((tm, w), lambda i: (i, 0))
    whole = lambda a: pl.BlockSpec(a.shape, lambda i: (0, 0))
    return pl.pallas_call(
        body, name="mix_out_fwd", grid=(T // tm,),
        in_specs=[row(D), row(ATTN_W), row(CONV_W), ga, gc, whole(woa), whole(woc), whole(wout)],
        out_specs=row(D),
        out_shape=jax.ShapeDtypeStruct((T, D), F32),
        compiler_params=_params(("arbitrary",)),
    )(x, o, yc, proj, proj, woa, woc, wout)


def _mix_out_bwd(dx, o, yc, proj, woa, woc, wout, tm, comm=None):
    T, D = dx.shape
    nt = T // tm

    def body(dx_ref, o_ref, yc_ref, ga_ref, gc_ref, woa_ref, woc_ref, wout_ref,
             do_ref, dyc_ref, dg_ref, dwoa_ref, dwoc_ref, dwout_ref, acca, accc, acco):
        t = pl.program_id(0)

        @pl.when(t == 0)
        def _():
            acca[...] = jnp.zeros_like(acca)
            accc[...] = jnp.zeros_like(accc)
            acco[...] = jnp.zeros_like(acco)

        dxb = dx_ref[...].astype(BF16)
        ov, ycv = o_ref[...], yc_ref[...]
        ya = _dot(ov, woa_ref[...])
        yp = _dot(ycv, woc_ref[...])
        sa = _sigmoid(ga_ref[...].astype(F32))
        sc = _sigmoid(gc_ref[...].astype(F32))
        merged = (sa * ya + sc * yp).astype(BF16)
        dm = _dot_nt(dxb, wout_ref[...])
        dya = (dm * sa).astype(BF16)
        dyp = (dm * sc).astype(BF16)
        dg_ref[:, :D] = (dm * ya * sa * (1.0 - sa)).astype(BF16)
        dg_ref[:, D:] = (dm * yp * sc * (1.0 - sc)).astype(BF16)
        do_ref[...] = _dot_nt(dya, woa_ref[...]).astype(BF16)
        dyc_ref[...] = _dot_nt(dyp, woc_ref[...]).astype(BF16)
        acca[...] += _dot_tn(ov, dya)
        accc[...] += _dot_tn(ycv, dyp)
        acco[...] += _dot_tn(merged, dxb)

        @pl.when(t == nt - 1)
        def _():
            dwoa_ref[...] = acca[...].astype(BF16)
            dwoc_ref[...] = accc[...].astype(BF16)
            dwout_ref[...] = acco[...].astype(BF16)

    ga, gc = _gate_specs(tm, D)
    row = lambda w: pl.BlockSpec((tm, w), lambda i: (i, 0))
    whole = lambda a: pl.BlockSpec(a.shape, lambda i: (0, 0))
    return _pallas(
        body, "mix_out_bwd", (nt,),
        [row(D), row(ATTN_W), row(CONV_W), ga, gc, whole(woa), whole(woc), whole(wout)],
        [row(ATTN_W), row(CONV_W), row(2 * D), whole(woa), whole(woc), whole(wout)],
        [jax.ShapeDtypeStruct((T, ATTN_W), BF16), jax.ShapeDtypeStruct((T, CONV_W), BF16),
         jax.ShapeDtypeStruct((T, 2 * D), BF16),
         jax.ShapeDtypeStruct(woa.shape, BF16), jax.ShapeDtypeStruct(woc.shape, BF16),
         jax.ShapeDtypeStruct(wout.shape, BF16)],
        [pltpu.VMEM(woa.shape, F32), pltpu.VMEM(woc.shape, F32), pltpu.VMEM(wout.shape, F32)],
        (dx, o, yc, proj, proj, woa, woc, wout), comm)


def _proj_pieces(dq, dk, dv, dcb, dcc, dcx, dgates, dflog):
    D = dgates.shape[1] // 2
    return [(dq, ATTN_W, 0), (dk, ATTN_W, 0), (dv, ATTN_W, 0), (dcb, CONV_W, 0), (dcc, CONV_W, 0), (dcx, CONV_W, 0),
            (dgates, D, 0), (dgates, D, 1), (dflog, LANES, 0)]


def _mix_proj_bwd_dx(dres, x, g, pieces, wproj_t, wf_t, tm, comm=None):
    T, D = x.shape
    n = len(pieces)
    w_blocks = [(ATTN_W, 0), (ATTN_W, 1), (ATTN_W, 2), (CONV_W, 3), (CONV_W, 4), (CONV_W, 5),
                (D, COL_GATES // D), (D, COL_GATES // D + 1)]

    def body(*refs):
        dres_ref, x_ref, g_ref = refs[:3]
        p_refs, w_refs = refs[3:3 + n], refs[3 + n:3 + 2 * n]
        dx_ref, dg_ref = refs[3 + 2 * n:]

        @pl.when(pl.program_id(0) == 0)
        def _():
            dg_ref[...] = jnp.zeros_like(dg_ref)

        dh = _dot(p_refs[0][...].astype(BF16), w_refs[0][...])
        for p_ref, w_ref in zip(p_refs[1:], w_refs[1:]):
            dh = dh + _dot(p_ref[...].astype(BF16), w_ref[...])
        xhat, inv = _rms(x_ref[...])
        dx, dg = _rms_bwd(dh, xhat, inv, g_ref[...])
        dx_ref[...] = dres_ref[...] + dx
        dg_ref[...] += dg

    row = pl.BlockSpec((tm, D), lambda i: (i, 0))
    vec = pl.BlockSpec((1, D), lambda i: (0, 0))
    p_specs = [pl.BlockSpec((tm, w), lambda i, cb=cb: (i, cb)) for _, w, cb in pieces]
    w_specs = [pl.BlockSpec((r, D), lambda i, rb=rb: (rb, 0)) for r, rb in w_blocks]
    w_specs.append(pl.BlockSpec((LANES, D), lambda i: (0, 0)))
    return _pallas(
        body, "mix_proj_bwd_dx", (T // tm,),
        [row, row, vec] + p_specs + w_specs, [row, vec],
        [jax.ShapeDtypeStruct((T, D), F32), jax.ShapeDtypeStruct((1, D), F32)], [],
        (dres, x, g, *[p for p, _, _ in pieces], *([wproj_t] * len(w_blocks)), wf_t), comm)


def _matmuls_tn(name, pieces, b, tk):
    T, N = b.shape
    nt = T // tk
    n = len(pieces)

    def body(*refs):
        a_refs, b_ref, out_refs, accs = refs[:n], refs[n], refs[n + 1:2 * n + 1], refs[2 * n + 1:]
        t = pl.program_id(0)

        @pl.when(t == 0)
        def _():
            for acc in accs:
                acc[...] = jnp.zeros_like(acc)

        bv = b_ref[...]
        for a_ref, acc in zip(a_refs, accs):
            acc[...] += _dot_tn(a_ref[...].astype(BF16), bv)

        @pl.when(t == nt - 1)
        def _():
            for out_ref, acc in zip(out_refs, accs):
                out_ref[...] = acc[...].astype(BF16)

    return pl.pallas_call(
        body, name=name, grid=(nt,),
        in_specs=[pl.BlockSpec((tk, w), lambda t, cb=cb: (t, cb)) for _, w, cb in pieces]
        + [pl.BlockSpec((tk, N), lambda t: (t, 0))],
        out_specs=[pl.BlockSpec((w, N), lambda t: (0, 0)) for _, w, _ in pieces],
        out_shape=[jax.ShapeDtypeStruct((w, N), BF16) for _, w, _ in pieces],
        scratch_shapes=[pltpu.VMEM((w, N), F32) for _, w, _ in pieces],
        compiler_params=_params(("arbitrary",)),
    )(*[a for a, _, _ in pieces], b)


TOKEN_TILE = 512
TOKEN_TILE_WIDE = 1024
ATTN_TILE = 512
SCAN_CHUNK = 256
PROJ_DX_TILE = 256


def _local_step(x, target, plan, B, S):
    T, D = x.shape
    tm = min(TOKEN_TILE, T)
    tm_fwd = min(TOKEN_TILE_WIDE, T)
    tq = min(ATTN_TILE, S)
    nq = S // tq
    ch = min(SCAN_CHUNK, S)

    def riding(kernel_name, build):
        results, brought = build(plan.rider(kernel_name))
        plan.arrived(kernel_name, brought)
        return results

    hg1, hu1, n1 = plan.ffn1_up(x, tm_fwd)
    w1 = plan.weights("ffn1")
    x1, = riding("ffn1_down", lambda comm: _ffn_down("ffn1_down", x, hg1, hu1, w1["ffn1_down"], tm_fwd, comm))
    wm = plan.weights("mix_in")
    h, proj, flog = riding("mix_proj_fwd", lambda comm: _mix_proj_fwd(
        x1, wm["mix_norm"], wm["w_proj"], wm["w_f"], tm_fwd, PROJ_W // 4, comm))
    wm.update(plan.weights("mix_out"))
    cum = _fgate_fwd(flog, wm["b_forget"], B, S, ch)
    cum_t = jnp.transpose(cum[:, :N_HEADS].reshape(B, nq, tq, N_HEADS), (0, 1, 3, 2))
    o, lse_t = riding("attn_fwd", lambda comm: _attn_fwd(proj, cum, cum_t, B, S, tq, comm))
    lse = jnp.pad(jnp.transpose(lse_t, (0, 1, 3, 2)).reshape(T, N_HEADS), ((0, 0), (0, LANES - N_HEADS)))
    yc = _conv_fwd(proj, wm["conv_w"], B, S)
    x2 = _mix_out_fwd(x1, o, yc, proj, wm["w_o_attn"], wm["w_o_conv"], wm["w_out"], tm)
    w2 = plan.weights("ffn2")
    dx3, hg2, hu2, n2, loss, d_final_norm = _ffn_fwd_loss(
        "ffn2_fwd_loss", x2, w2["ffn2_norm"], w2["ffn2_gate"], w2["ffn2_up"], w2["ffn2_down"], target, w2["final_norm"],
        tm_fwd)

    g = {"final_norm": d_final_norm}
    dx2, dhg2, dhu2, g["ffn2_norm"], df2 = _ffn_bwd_dx("ffn2_bwd_dx", dx3, x2, w2["ffn2_norm"], hg2, hu2,
                                                  w2["ffn2_gate"], w2["ffn2_up"], w2["ffn2_down"], tm_fwd)[0]
    plan.reduce_parts("ffn2", dict(zip(("ffn2_gate", "ffn2_up", "ffn2_down"),
                                       _ffn_bwd_dw_reduced("ffn2_bwd_dw", n2, df2, hg2, hu2, dhg2, dhu2, tm_fwd)[0])))
    do, dyc, dgates, dwoa, dwoc, dwout = riding("mix_out_bwd", lambda comm: _mix_out_bwd(
        dx2, o, yc, proj, wm["w_o_attn"], wm["w_o_conv"], wm["w_out"], tm, comm))
    plan.reduce("out", dict(w_o_attn=_shard_cols(dwoa), w_o_conv=_shard_cols(dwoc), w_out=dwout.reshape(N_CHIPS, -1, D)))
    dq, dk, dv, dcq, dck = riding("attn_bwd", lambda comm: _attn_bwd(proj, o, do, lse, cum, cum_t, B, S, tq, comm))
    dcum = dcq + jnp.pad(jnp.transpose(dck, (0, 1, 3, 2)).reshape(T, N_HEADS), ((0, 0), (0, LANES - N_HEADS)))
    dflog, g["b_forget"] = _fgate_bwd(dcum, flog, wm["b_forget"], B, S, ch)
    dcb, dcc, dcx, g["conv_w"] = _conv_bwd(dyc, proj, wm["conv_w"], B, S)
    pieces = _proj_pieces(dq, dk, dv, dcb, dcc, dcx, dgates, dflog)
    dwq, dwk, dwv, dwcb, dwcc, dwcx = _matmuls_tn("mix_dw_a", pieces[:6], h, tm)
    dwga, dwgc, dwf = _matmuls_tn("mix_dw_b", pieces[6:], h, tm)
    dwin_t = jnp.concatenate([dwq, dwk, dwv, dwf[:N_HEADS], dwcb, dwcc, dwcx, dwga, dwgc], axis=0)
    plan.reduce("w_in", {"w_in": dwin_t.reshape(N_CHIPS, -1, D)})
    dx1, g["mix_norm"] = riding("mix_proj_bwd_dx", lambda comm: _mix_proj_bwd_dx(
        dx2, x1, wm["mix_norm"], pieces, wm["w_proj"], wm["w_f"], min(PROJ_DX_TILE, T), comm))
    grad_x, dhg1, dhu1, g["ffn1_norm"], df1 = _ffn_bwd_dx(
        "ffn1_bwd_dx", dx1, x, w1["ffn1_norm"], hg1, hu1, w1["ffn1_gate"], w1["ffn1_up"], w1["ffn1_down"], tm_fwd)[0]
    plan.reduce_small(g, loss)
    plan.reduce_parts("ffn1", dict(zip(("ffn1_gate", "ffn1_up", "ffn1_down"), riding(
        "ffn1_bwd_dw", lambda comm: _ffn_bwd_dw_reduced("ffn1_bwd_dw", n1, df1, hg1, hu1, dhg1, dhu1, tm_fwd, comm)))))
    return loss, grad_x, g


TRANSPOSED = ("ffn1_gate", "ffn1_up", "ffn2_gate", "ffn2_up", "w_in")
NORMS = ("ffn1_norm", "mix_norm", "ffn2_norm", "final_norm")


def _unshard_cols(a):
    return jnp.transpose(a, (1, 0, 2)).reshape(a.shape[1], N_CHIPS * a.shape[2])


def _shard_cols(a):
    return jnp.transpose(a.reshape(a.shape[0], N_CHIPS, a.shape[1] // N_CHIPS), (1, 0, 2))


def _layout_ffn(which):
    def layout(st, small):
        w = {n: st[n] for n in (which + "_gate", which + "_up", which + "_down")}
        w[which + "_norm"] = small[which + "_norm"].reshape(1, -1)
        if which == "ffn2":
            w["final_norm"] = small["final_norm"].reshape(1, -1)
        return w
    return layout


def _layout_mix_in(st, small):
    win_t = st["w_in"].reshape(-1, st["w_in"].shape[2])
    return {
        "w_proj": jnp.concatenate([win_t[:N_FORGET_COL], win_t[N_FORGET_COL + N_HEADS:]], axis=0),
        "w_f": jnp.pad(win_t[N_FORGET_COL:N_FORGET_COL + N_HEADS], ((0, LANES - N_HEADS), (0, 0))),
        "conv_w": _unshard_cols(st["conv_w"]),
        "mix_norm": small["mix_norm"].reshape(1, -1),
        "b_forget": jnp.pad(small["b_forget"].reshape(1, -1), ((0, 0), (0, LANES - N_HEADS))),
    }


def _layout_mix_out(st, small):
    return {"w_o_attn": _unshard_cols(st["w_o_attn"]), "w_o_conv": _unshard_cols(st["w_o_conv"]),
            "w_out": st["w_out"].reshape(-1, st["w_out"].shape[2])}


_LAYOUTS = {"ffn1": _layout_ffn("ffn1"), "mix_in": _layout_mix_in, "mix_out": _layout_mix_out, "ffn2": _layout_ffn("ffn2")}


ANY = pl.BlockSpec(memory_space=pl.ANY)
BIG = ("ffn1_gate", "ffn1_up", "ffn1_down", "w_in", "w_o_attn", "w_o_conv", "w_out",
       "ffn2_gate", "ffn2_up", "ffn2_down")


def _place():
    x, y, c = lax.axis_index("x"), lax.axis_index("y"), lax.axis_index("c")
    others = [(1 - x, y), (x, 1 - y), (1 - x, 1 - y)]
    return x, y, c, others


def _col_halves(cols, c):
    hc = cols // 2
    return pl.ds(pl.multiple_of(c * hc, LANES), hc), pl.ds(pl.multiple_of((1 - c) * hc, LANES), hc)


def _gather_comm(shards, conv_shard=None):
    n = len(shards)
    inputs = list(shards) + ([] if conv_shard is None else [conv_shard])

    def copies(ins, outs, sems):
        send_sems, recv_sems, pass_send, pass_recv = sems[:4]
        x, y, c, others = _place()

        def chip_copy(a, j, chip):
            mine, _ = _col_halves(ins[a].shape[1], c)
            return pltpu.make_async_remote_copy(
                src_ref=ins[a].at[:, mine], dst_ref=outs[a].at[chip, :, mine],
                send_sem=send_sems.at[3 * a + j], recv_sem=recv_sems.at[3 * a + j],
                device_id=(*others[j], c), device_id_type=MESH)

        def pass_copy(a, j, chip, half):
            return pltpu.make_async_remote_copy(
                src_ref=outs[a].at[chip, :, half], dst_ref=outs[a].at[chip, :, half],
                send_sem=pass_send.at[3 * a + j], recv_sem=pass_recv.at[3 * a + j],
                device_id=(x, y, 1 - c), device_id_type=MESH)

        def conv_copy(j, chip):
            return pltpu.make_async_remote_copy(
                src_ref=ins[n], dst_ref=outs[n].at[chip],
                send_sem=sems[4].at[j], recv_sem=sems[5].at[j],
                device_id=(*others[j], c), device_id_type=MESH)

        me = 2 * x + y
        sends = [chip_copy(a, j, me) for a in range(n) for j in range(3)]
        if conv_shard is not None:
            sends += [conv_copy(j, me) for j in range(3)]
        return c, others, sends, chip_copy, pass_copy, conv_copy

    def start(ins, outs, sems):
        for cp in copies(ins, outs, sems)[2]:
            cp.start()

    def finish(ins, outs, sems):
        c, others, sends, chip_copy, pass_copy, conv_copy = copies(ins, outs, sems)
        passed = []
        for a in range(n):
            mine, _ = _col_halves(ins[a].shape[1], c)
            for j, (ox, oy) in enumerate(others):
                chip_copy(a, j, 2 * ox + oy).wait_recv()
                passed.append(pass_copy(a, j, 2 * ox + oy, mine))
                passed[-1].start()
        for a in range(n):
            _, theirs = _col_halves(ins[a].shape[1], c)
            for j, (ox, oy) in enumerate(others):
                pass_copy(a, j, 2 * ox + oy, theirs).wait_recv()
        if conv_shard is not None:
            for j, (ox, oy) in enumerate(others):
                conv_copy(j, 2 * ox + oy).wait_recv()
        for cp in sends + passed:
            cp.wait_send()

    scratch = [pltpu.SemaphoreType.DMA((3 * n,))] * 4
    if conv_shard is not None:
        scratch += [pltpu.SemaphoreType.DMA((3,))] * 2
    return _Comm(inputs, [jax.ShapeDtypeStruct((N_CHIPS,) + s.shape, s.dtype) for s in inputs], scratch, start, finish)


def _fill_own(stacks, shards):
    chip = 2 * lax.axis_index("x") + lax.axis_index("y")
    return [lax.dynamic_update_index_in_dim(st, s, chip, 0) for st, s in zip(stacks, shards)]


def _run_comm(name, comm):
    ci, co = len(comm.inputs), len(comm.out_shape)

    def body(*refs):
        comm.start(refs[:ci], refs[ci:ci + co], refs[ci + co:])
        comm.finish(refs[:ci], refs[ci:ci + co], refs[ci + co:])

    return pl.pallas_call(body, name=name, in_specs=[ANY] * ci, out_specs=[ANY] * co, out_shape=comm.out_shape,
                          scratch_shapes=comm.scratch)(*comm.inputs)


def _sibling_exchange_comm(grads):
    n = len(grads)

    def copies(ins, outs, sems):
        x, y, c, _ = _place()
        return [pltpu.make_async_remote_copy(
            src_ref=ins[a].at[:, :, _col_halves(ins[a].shape[2], c)[1]], dst_ref=outs[a],
            send_sem=sems[0].at[a], recv_sem=sems[1].at[a],
            device_id=(x, y, 1 - c), device_id_type=MESH) for a in range(n)]

    def start(ins, outs, sems):
        for cp in copies(ins, outs, sems):
            cp.start()

    def finish(ins, outs, sems):
        for cp in copies(ins, outs, sems):
            cp.wait()

    half = lambda s: jax.ShapeDtypeStruct((s.shape[0], s.shape[1], s.shape[2] // 2), s.dtype)
    return _Comm(grads, [half(s) for s in grads], [pltpu.SemaphoreType.DMA((n,))] * 2, start, finish)


def _merge_comms(comms):
    def split(refs, count):
        out, at = [], 0
        for cm in comms:
            out.append(refs[at:at + count(cm)])
            at += count(cm)
        return out

    def parts(ins, outs, sems):
        return zip(comms, split(ins, lambda cm: len(cm.inputs)), split(outs, lambda cm: len(cm.out_shape)),
                   split(sems, lambda cm: len(cm.scratch)))

    def start(ins, outs, sems):
        for cm, i, o, s in parts(ins, outs, sems):
            cm.start(i, o, s)

    def finish(ins, outs, sems):
        for cm, i, o, s in parts(ins, outs, sems):
            cm.finish(i, o, s)

    return _Comm(sum([cm.inputs for cm in comms], []), sum([cm.out_shape for cm in comms], []),
                 sum([cm.scratch for cm in comms], []), start, finish)


def _add_halves(name, grads, recvs, core):
    n = len(grads)

    def body(core_ref, *refs):
        for g_ref, r_ref, out_ref in zip(refs[:n], refs[n:2 * n], refs[2 * n:]):
            out_ref[...] = (g_ref[...].astype(F32) + r_ref[...].astype(F32)).astype(BF16)

    half = lambda g: pl.BlockSpec((None, g.shape[1], g.shape[2] // 2), lambda k, core_ref: (k, 0, 0))
    mine = lambda g: pl.BlockSpec((None, g.shape[1], g.shape[2] // 2), lambda k, core_ref: (k, 0, core_ref[0]))
    return pl.pallas_call(
        body, name=name,
        grid_spec=pltpu.PrefetchScalarGridSpec(
            num_scalar_prefetch=1, grid=(N_CHIPS,),
            in_specs=[mine(g) for g in grads] + [half(g) for g in grads],
            out_specs=[half(g) for g in grads]),
        out_shape=[jax.ShapeDtypeStruct(r.shape, BF16) for r in recvs],
        compiler_params=_params(("arbitrary",)),
    )(core, *grads, *recvs)


def _chip_exchange_comm(parts):
    n = len(parts)

    def copies(ins, outs, sems):
        x, y, c, others = _place()
        return [pltpu.make_async_remote_copy(
            src_ref=ins[a].at[2 * ox + oy], dst_ref=outs[a].at[j],
            send_sem=sems[0].at[3 * a + j], recv_sem=sems[1].at[3 * a + j],
            device_id=(ox, oy, c), device_id_type=MESH) for a in range(n) for j, (ox, oy) in enumerate(others)]

    def start(ins, outs, sems):
        for cp in copies(ins, outs, sems):
            cp.start()

    def finish(ins, outs, sems):
        for cp in copies(ins, outs, sems):
            cp.wait()

    return _Comm(parts, [jax.ShapeDtypeStruct((3,) + s.shape[1:], s.dtype) for s in parts],
                 [pltpu.SemaphoreType.DMA((3 * n,))] * 2, start, finish)


HBM = pl.BlockSpec(memory_space=pltpu.HBM)
SEM = pl.BlockSpec(memory_space=pltpu.SEMAPHORE)


def _split_exchange_copies(parts, lands, send_sems, recv_sems):
    x, y, c, others = _place()
    return [pltpu.make_async_remote_copy(
        src_ref=parts[a].at[2 * ox + oy], dst_ref=lands[a].at[j],
        send_sem=send_sems.at[3 * a + j], recv_sem=recv_sems.at[3 * a + j],
        device_id=(ox, oy, c), device_id_type=MESH) for a in range(len(parts)) for j, (ox, oy) in enumerate(others)]


def _exchange_start(name, parts):
    n = len(parts)

    def body(*refs):
        ins, lands = refs[:n], refs[n:2 * n]
        send_sems, recv_sems, token = refs[2 * n], refs[2 * n + 1], refs[-1]
        for cp in _split_exchange_copies(ins, lands, send_sems, recv_sems):
            cp.start()
        token[...] = jnp.zeros_like(token)

    land_shape = [(3,) + p.shape[1:] for p in parts]
    outs = pl.pallas_call(
        body, name=name,
        out_shape=[pltpu.SemaphoreType.DMA((3 * n,)), pltpu.SemaphoreType.DMA((3 * n,))]
        + [pltpu.HBM(p.shape, p.dtype) for p in parts] + [pltpu.HBM(s, p.dtype) for s, p in zip(land_shape, parts)]
        + [jax.ShapeDtypeStruct((8, LANES), F32)],
        in_specs=[HBM] * (2 * n), out_specs=[SEM, SEM] + [HBM] * (2 * n) + [pl.BlockSpec(memory_space=pltpu.VMEM)],
        input_output_aliases={i: 2 + i for i in range(2 * n)},
        compiler_params=pltpu.CompilerParams(has_side_effects=pltpu.SideEffectType.DATAFLOW_SIDE_EFFECTING),
    )(*[pltpu.with_memory_space_constraint(p, pltpu.HBM) for p in parts],
      *[pltpu.with_memory_space_constraint(lax.empty(s, p.dtype), pltpu.HBM) for s, p in zip(land_shape, parts)])
    return outs[0], outs[1], list(outs[2:2 + n]), list(outs[2 + n:2 + 2 * n]), outs[-1]


def _exchange_wait(name, send_sems, recv_sems, parts, lands, after):
    n = len(parts)

    def body(*refs):
        ins, zones = refs[:n], refs[n:2 * n]
        for cp in _split_exchange_copies(ins, zones, refs[2 * n], refs[2 * n + 1]):
            cp.wait_send()
            cp.wait_recv()

    outs = pl.pallas_call(
        body, name=name,
        out_shape=[pltpu.HBM(p.shape, p.dtype) for p in parts] + [pltpu.HBM(z.shape, z.dtype) for z in lands],
        in_specs=[HBM] * (2 * n) + [SEM, SEM] + [ANY] * len(after), out_specs=[HBM] * (2 * n),
        input_output_aliases={i: i for i in range(2 * n)},
        compiler_params=pltpu.CompilerParams(has_side_effects=pltpu.SideEffectType.DATAFLOW_SIDE_EFFECTING),
    )(*parts, *lands, send_sems, recv_sems, *after)
    return list(outs[:n]), list(outs[n:])


def _sum_chips(name, owns, recvs, chip, after):
    n = len(owns)
    hc = owns[0].shape[2]
    assert all(o.shape[2] == hc for o in owns)

    def body(chip_ref, *refs):
        for own_ref, recv_ref, out_ref in zip(refs[:n], refs[n:2 * n], refs[2 * n + 1:]):
            acc = own_ref[...].astype(F32)
            for j in range(3):
                acc = acc + recv_ref[j].astype(F32)
            out_ref[...] = acc

    return pl.pallas_call(
        body, name=name,
        grid_spec=pltpu.PrefetchScalarGridSpec(
            num_scalar_prefetch=1, grid=(hc // LANES,),
            in_specs=[pl.BlockSpec((None, o.shape[1], LANES), lambda i, chip_ref: (chip_ref[0], 0, i)) for o in owns]
            + [pl.BlockSpec((3, o.shape[1], LANES), lambda i, chip_ref: (0, 0, i)) for o in owns]
            + [pl.BlockSpec((8, LANES), lambda i, chip_ref: (0, 0))],
            out_specs=[pl.BlockSpec((o.shape[1], LANES), lambda i, chip_ref: (0, i)) for o in owns]),
        out_shape=[jax.ShapeDtypeStruct((o.shape[1], hc), F32) for o in owns],
        compiler_params=_params(("arbitrary",)),
    )(chip, *owns, *recvs, after)


def _share_halves(name, halves):
    n = len(halves)

    def body(*refs):
        srcs, dsts = refs[:n], refs[n:2 * n]
        send_sems, recv_sems = refs[2 * n:]
        x, y, c, _ = _place()
        copies = [pltpu.make_async_remote_copy(
            src_ref=srcs[a], dst_ref=dsts[a], send_sem=send_sems.at[a], recv_sem=recv_sems.at[a],
            device_id=(x, y, 1 - c), device_id_type=MESH) for a in range(n)]
        for cp in copies:
            cp.start()
        for cp in copies:
            cp.wait()

    return pl.pallas_call(
        body, name=name,
        in_specs=[ANY] * n, out_specs=[ANY] * n,
        out_shape=[jax.ShapeDtypeStruct(s.shape, s.dtype) for s in halves],
        scratch_shapes=[pltpu.SemaphoreType.DMA((n,)), pltpu.SemaphoreType.DMA((n,))],
    )(*halves)


def _small_gather_comm(part):
    def copies(ins, outs, sems):
        x, y, c, _ = _place()
        me = 4 * x + 2 * y + c
        both = []
        for d in range(1, N_DEV):
            px, py, pc = (1 - x if d & 4 else x, 1 - y if d & 2 else y, 1 - c if d & 1 else c)
            send = pltpu.make_async_remote_copy(
                src_ref=ins[0], dst_ref=outs[0].at[me], send_sem=sems[0].at[d - 1], recv_sem=sems[1].at[d - 1],
                device_id=(px, py, pc), device_id_type=MESH)
            recv = pltpu.make_async_remote_copy(
                src_ref=ins[0], dst_ref=outs[0].at[4 * px + 2 * py + pc], send_sem=sems[0].at[d - 1],
                recv_sem=sems[1].at[d - 1], device_id=(px, py, pc), device_id_type=MESH)
            both.append((send, recv))
        return both

    def start(ins, outs, sems):
        for send, _ in copies(ins, outs, sems):
            send.start()

    def finish(ins, outs, sems):
        for send, recv in copies(ins, outs, sems):
            recv.wait_recv()
            send.wait_send()

    return _Comm([part], [jax.ShapeDtypeStruct((N_DEV,) + part.shape, F32)],
                 [pltpu.SemaphoreType.DMA((N_DEV - 1,))] * 2, start, finish)


def _sum_devices(parts):
    def body(p_ref, out_ref):
        acc = p_ref[0]
        for k in range(1, N_DEV):
            acc = acc + p_ref[k]
        out_ref[...] = acc

    return pl.pallas_call(
        body, name="sum_devices", grid=(1,),
        in_specs=[pl.BlockSpec(parts.shape, lambda i: (0, 0, 0))],
        out_specs=pl.BlockSpec(parts.shape[1:], lambda i: (0, 0)),
        out_shape=jax.ShapeDtypeStruct(parts.shape[1:], F32),
        compiler_params=_params(("arbitrary",)),
    )(parts)


def _adam_update(w, g, m, v):
    nm = ADAM_B1 * m + (1.0 - ADAM_B1) * g
    nv = ADAM_B2 * v + (1.0 - ADAM_B2) * (g * g)
    m_hat = nm * (1.0 / (1.0 - ADAM_B1 ** ADAM_STEP))
    v_hat = nv * (1.0 / (1.0 - ADAM_B2 ** ADAM_STEP))
    return -ADAM_LR * (m_hat / (jnp.sqrt(v_hat) + ADAM_EPS) + ADAM_WD * w), nm, nv


def _adamw(name, w, g, m, v):
    def body(w_ref, g_ref, m_ref, v_ref, d_ref, nm_ref, nv_ref):
        d_ref[...], nm_ref[...], nv_ref[...] = _adam_update(w_ref[...], g_ref[...], m_ref[...], v_ref[...])

    spec = pl.BlockSpec(w.shape, lambda i: (0, 0))
    out = jax.ShapeDtypeStruct(w.shape, F32)
    return pl.pallas_call(
        body, name=name, grid=(1,),
        in_specs=[spec] * 4, out_specs=[spec] * 3, out_shape=[out] * 3,
        compiler_params=_params(("arbitrary",)),
    )(w, g, m, v)


def _adamw_halves(name, ws, mines, theirs, ms, vs, core):
    n = len(ws)
    cols = ws[0].shape[1]
    assert all(w.shape[1] == cols for w in ws)
    hc = cols // 2
    tc = LANES if n > 1 else min(256, hc)
    nt = hc // tc

    def body(core_ref, *refs):
        ins, outs = refs[:5 * n], refs[5 * n:]
        for a in range(n):
            w_ref, mine_ref, theirs_ref, m_ref, v_ref = [ins[j * n + a] for j in range(5)]
            g_ref, d_ref, nm_ref, nv_ref = outs[4 * a:4 * a + 4]
            gv = jnp.where(pl.program_id(0) == core_ref[0], mine_ref[...], theirs_ref[...])
            g_ref[...] = gv
            d_ref[...], nm_ref[...], nv_ref[...] = _adam_update(w_ref[...], gv, m_ref[...], v_ref[...])

    whole = lambda w: pl.BlockSpec((w.shape[0], tc), lambda h, i, core_ref: (0, h * nt + i))
    mine_spec = lambda w: pl.BlockSpec((w.shape[0], tc), lambda h, i, core_ref: (0, jnp.where(h == core_ref[0], i, 0)))
    theirs_spec = lambda w: pl.BlockSpec((w.shape[0], tc), lambda h, i, core_ref: (0, jnp.where(h == core_ref[0], 0, i)))
    outs = pl.pallas_call(
        body, name=name,
        grid_spec=pltpu.PrefetchScalarGridSpec(
            num_scalar_prefetch=1, grid=(2, nt),
            in_specs=[whole(w) for w in ws] + [mine_spec(w) for w in ws] + [theirs_spec(w) for w in ws]
            + [whole(w) for w in ws] * 2,
            out_specs=[whole(w) for w in ws for _ in range(4)]),
        out_shape=[jax.ShapeDtypeStruct(w.shape, F32) for w in ws for _ in range(4)],
        compiler_params=_params(("arbitrary", "arbitrary")),
    )(core, *ws, *mines, *theirs, *ms, *vs)
    return [outs[4 * a:4 * a + 4] for a in range(n)]


WEIGHTS = ("ffn1_norm", "ffn1_gate", "ffn1_up", "ffn1_down", "mix_norm", "w_in", "b_forget", "conv_w",
           "w_o_attn", "w_o_conv", "w_out", "ffn2_norm", "ffn2_gate", "ffn2_up", "ffn2_down", "final_norm")
VEC_ROWS = 8


def _pack_small(t, conv_rows):
    conv = t["conv_w"]
    parts = [t[n].reshape(VEC_ROWS, LANES) for n in NORMS]
    parts.append(jnp.pad(conv, ((0, conv_rows - conv.shape[0]), (0, 0))))
    parts.append(jnp.pad(t["b_forget"].reshape(1, N_HEADS), ((0, 7), (0, LANES - N_HEADS))))
    return jnp.concatenate(parts, axis=0)


def _unpack_small(p, conv_rows):
    out = {n: p[VEC_ROWS * i:VEC_ROWS * (i + 1)].reshape(-1) for i, n in enumerate(NORMS)}
    base = VEC_ROWS * len(NORMS)
    out["conv_w"] = p[base:base + 3]
    out["b_forget"] = p[base + conv_rows, :N_HEADS]
    return out


def _travel(name, a):
    return a.T if name in TRANSPOSED else a


GATHER_FIRST = ("ffn1_gate", "ffn1_up")
GATHER_RIDES = {"ffn1_up": ("ffn1_down",), "ffn1_down": ("w_in",), "mix_proj_fwd": ("w_o_attn", "w_o_conv", "w_out"),
                "attn_fwd": ("ffn2_gate", "ffn2_up", "ffn2_down")}
SIBLING_RIDES = {"ffn2": "mix_out_bwd", "out": None, "w_in": "mix_proj_bwd_dx", "ffn1": None}
CHIP_RIDES = {"ffn2": "attn_bwd", "out": "attn_bwd", "w_in": "ffn1_bwd_dw", "ffn1": None}
SMALL_RIDE = "ffn1_bwd_dw"


class _MeshPlan:
    def __init__(self, wts, core):
        self.small, self.core = wts, core
        self.shards = {n: wts[n].astype(BF16) for n in BIG}
        self.chip_part, self.from_chips, self.rides = {}, {}, {}
        self.stacks = {}
        conv_shard = jnp.pad(wts["conv_w"], ((0, 8 - wts["conv_w"].shape[0]), (0, 0)))
        for kernel_name, names in GATHER_RIDES.items():
            mine = [self.shards[n] for n in names]
            conv = conv_shard if kernel_name == "ffn1_up" else None
            names = names + (("conv_w",) if conv is not None else ())
            mine = mine + ([conv] if conv is not None else [])
            self._ride(kernel_name, _gather_comm(mine[:len(mine) - (conv is not None)], conv),
                       lambda got, names=names, mine=mine: self.stacks.update(zip(names, _fill_own(got, mine))))

    def weights(self, group):
        return _LAYOUTS[group](self.stacks, self.small)

    def ffn1_up(self, x, tm):
        px, py = lax.axis_index("x"), lax.axis_index("y")
        order = jnp.stack([2 * px + py, 2 * (1 - px) + py, 2 * px + (1 - py), 2 * (1 - px) + (1 - py)]).astype(jnp.int32)
        own = [self.shards[n] for n in GATHER_FIRST]
        (hg, hu, n, sg, su), brought = _ffn_up_gather("ffn1_up", x, self.small["ffn1_norm"].reshape(1, -1), *own, order,
                                                     tm, self.rider("ffn1_up"))
        self.stacks.update(zip(GATHER_FIRST, _fill_own([sg, su], own)))
        self.arrived("ffn1_up", brought)
        return hg, hu, n

    def _ride(self, kernel_name, comm, then):
        self.rides.setdefault(kernel_name, []).append((comm, then))

    def rider(self, kernel_name):
        comms = [comm for comm, _ in self.rides.get(kernel_name, [])]
        return _merge_comms(comms) if comms else None

    def arrived(self, kernel_name, results):
        for comm, then in self.rides.pop(kernel_name, []):
            then(results[:len(comm.out_shape)])
            results = results[len(comm.out_shape):]

    def reduce(self, group, grads):
        names = tuple(grads)
        mine = [grads[n] for n in names]

        def with_sibling(from_sibling):
            parts = _add_halves("add_halves_" + group, mine, list(from_sibling), self.core)
            self.chip_part.update(zip(names, parts))
            if CHIP_RIDES[group] is None:
                self.last = (names, _exchange_start("exchange_start_" + group, parts))
            else:
                self._ride(CHIP_RIDES[group], _chip_exchange_comm(parts),
                           lambda got: self.from_chips.update(zip(names, got)))

        if SIBLING_RIDES[group] is None:
            with_sibling(_run_comm("sibling_exchange_" + group, _sibling_exchange_comm(mine)))
        else:
            self._ride(SIBLING_RIDES[group], _sibling_exchange_comm(mine), with_sibling)

    def reduce_parts(self, group, parts):
        self.chip_part.update(parts)
        names = tuple(parts)
        if CHIP_RIDES[group] is None:
            self.last = (names, _exchange_start("exchange_start_" + group, list(parts.values())))
        else:
            self._ride(CHIP_RIDES[group], _chip_exchange_comm(list(parts.values())),
                       lambda got: self.from_chips.update(zip(names, got)))

    def reduce_small(self, gs, loss):
        conv_all = _shard_cols(gs["conv_w"]).reshape(N_CHIPS * 8, LANES)
        part = _pack_small({**{n: gs[n] for n in NORMS}, "conv_w": conv_all, "b_forget": gs["b_forget"][0, :N_HEADS]},
                           N_CHIPS * 8)
        part = jnp.concatenate([part, jnp.broadcast_to(loss, (8, LANES))], axis=0)
        me = 4 * lax.axis_index("x") + 2 * lax.axis_index("y") + lax.axis_index("c")

        def landed(got):
            self.small_parts = lax.dynamic_update_index_in_dim(got[0], part, me, 0)

        self._ride(SMALL_RIDE, _small_gather_comm(part), landed)


def kernel(x, ffn1_norm, ffn1_gate, ffn1_up, ffn1_down, mix_norm, w_in, b_forget, conv_w, w_o_attn, w_o_conv, w_out, ffn2_norm, ffn2_gate, ffn2_up, ffn2_down, final_norm, loss_target, m_ffn1_norm, m_ffn1_gate, m_ffn1_up, m_ffn1_down, m_mix_norm, m_w_in, m_b_forget, m_conv_w, m_w_o_attn, m_w_o_conv, m_w_out, m_ffn2_norm, m_ffn2_gate, m_ffn2_up, m_ffn2_down, m_final_norm, v_ffn1_norm, v_ffn1_gate, v_ffn1_up, v_ffn1_down, v_mix_norm, v_w_in, v_b_forget, v_conv_w, v_w_o_attn, v_w_o_conv, v_w_out, v_ffn2_norm, v_ffn2_gate, v_ffn2_up, v_ffn2_down, v_final_norm):
    given = dict(locals())
    wts = {n: _travel(n, given[n]) for n in WEIGHTS}
    mom = {n: _travel(n, given["m_" + n]) for n in WEIGHTS}
    var = {n: _travel(n, given["v_" + n]) for n in WEIGHTS}
    B, S, D = x.shape
    chip = 2 * lax.axis_index("x") + lax.axis_index("y")
    chip1 = chip.astype(jnp.int32).reshape(1)
    core = lax.axis_index("c").astype(jnp.int32).reshape(1)

    plan = _MeshPlan(wts, core)
    loss, grad_x, gs = _local_step(x.reshape(B * S, D), loss_target.reshape(B * S, D), plan, B, S)

    last_names, (send_sems, recv_sems, parts_thru, lands, token) = plan.last
    delta, new_m, new_v, grads = {}, {}, {}, {}

    def finish(tag, names):
        by_cols = {}
        for n in names:
            by_cols.setdefault(wts[n].shape[1], []).append(n)
        mine = {}
        for cols, ns in by_cols.items():
            mine.update(zip(ns, _sum_chips("sum_chips_%s_%d" % (tag, cols), [plan.chip_part[n] for n in ns],
                                           [plan.from_chips[n] for n in ns], chip1, token)))
        theirs = dict(zip(names, _share_halves("share_halves_" + tag, [mine[n] for n in names])))
        raw = []
        for cols, ns in by_cols.items():
            outs = _adamw_halves("adamw_%s_%d" % (tag, cols), [wts[n] for n in ns], [mine[n] for n in ns],
                                 [theirs[n] for n in ns], [mom[n] for n in ns], [var[n] for n in ns], core)
            for n, per in zip(ns, outs):
                raw.append(per[-1])
                grads[n], delta[n], new_m[n], new_v[n] = [_travel(n, o) for o in per]
        return raw

    small_sum = _sum_devices(plan.small_parts)
    base = VEC_ROWS * len(NORMS)
    loss_row = small_sum.shape[0] - 8
    small_grads = _unpack_small(small_sum, N_CHIPS * 8)
    small_grads["conv_w"] = lax.dynamic_slice_in_dim(small_sum[base:base + N_CHIPS * 8], chip * 8, 8, axis=0)[:3]
    packs = [_pack_small(t, 8) for t in (wts, small_grads, mom, var)]
    small_out = _adamw("adamw_small", *packs)

    done = finish("early", [n for n in BIG if n not in last_names])
    parts_back, got = _exchange_wait("exchange_wait", send_sems, recv_sems, parts_thru, lands, done + list(small_out))
    plan.chip_part.update(zip(last_names, parts_back))
    plan.from_chips.update(zip(last_names, got))
    finish("last", last_names)
    grads.update(small_grads)
    for out, p in zip((delta, new_m, new_v), small_out):
        out.update(_unpack_small(p, 8))

    return (small_sum[loss_row, 0], grad_x.reshape(B, S, D), *[grads[n] for n in WEIGHTS], *[delta[n] for n in WEIGHTS],
            *[new_m[n] for n in WEIGHTS], *[new_v[n] for n in WEIGHTS])
```

```python
import functools
import math

import jax
import jax.numpy as jnp
from jax import lax
from jax.experimental import pallas as pl
from jax.experimental.pallas import tpu as pltpu

F32 = jnp.float32
BF16 = jnp.bfloat16
MESH = pl.DeviceIdType.MESH

N_CHIPS = 4
N_DEV = 8
N_HEADS = 8
HEAD_DIM = 64
HEAD_PAIRS = N_HEADS // 2
ATTN_W = N_HEADS * HEAD_DIM
CONV_W = 512
RMS_EPS = 1e-6
FFN_RES = 0.5
LANES = 128
VMEM_LIMIT = 56 * 1024 * 1024
ROW_BLOCK = 256

ADAM_LR = 0.001
ADAM_B1 = 0.9
ADAM_B2 = 0.999
ADAM_EPS = 1e-08
ADAM_WD = 0.01
ADAM_STEP = 10

PROJ_W = 3 * ATTN_W + 3 * CONV_W + 2 * 1024
COL_CB, COL_CC, COL_CX = 3 * ATTN_W, 3 * ATTN_W + CONV_W, 3 * ATTN_W + 2 * CONV_W
COL_GATES = 3 * ATTN_W + 3 * CONV_W
N_FORGET_COL = 3 * ATTN_W


def _params(sem=None, vmem=VMEM_LIMIT):
    return pltpu.CompilerParams(dimension_semantics=sem, vmem_limit_bytes=vmem)


def _dot(a, b):
    return lax.dot_general(a, b, (((1,), (0,)), ((), ())), preferred_element_type=F32)


def _dot_nt(a, b):
    return lax.dot_general(a, b, (((1,), (1,)), ((), ())), preferred_element_type=F32)


def _dot_tn(a, b):
    return lax.dot_general(a, b, (((0,), (0,)), ((), ())), preferred_element_type=F32)


def _sigmoid(x):
    return 1.0 / (1.0 + jnp.exp(-x))


def _rms(xv):
    inv = lax.rsqrt(jnp.mean(xv * xv, axis=-1, keepdims=True) + RMS_EPS)
    return xv * inv, inv


class _Comm:
    def __init__(self, inputs, out_shape, scratch, start, finish):
        self.inputs, self.out_shape, self.scratch = list(inputs), list(out_shape), list(scratch)
        self.start, self.finish = start, finish


def _pallas(body, name, grid, in_specs, out_specs, out_shape, scratch, args, comm=None):
    sem = ("arbitrary",) * len(grid)
    if comm is None:
        outs = pl.pallas_call(body, name=name, grid=grid, in_specs=in_specs, out_specs=out_specs,
                              out_shape=out_shape, scratch_shapes=scratch, compiler_params=_params(sem))(*args)
        return list(outs), []
    n_in, n_out, n_scr = len(in_specs), len(out_specs), len(scratch)
    ci, co = len(comm.inputs), len(comm.out_shape)

    def riding(*refs):
        ins, refs = refs[:n_in], refs[n_in:]
        cins, refs = refs[:ci], refs[ci:]
        outs, refs = refs[:n_out], refs[n_out:]
        couts, refs = refs[:co], refs[co:]
        scr, sems = refs[:n_scr], refs[n_scr:]
        ids = [pl.program_id(d) for d in range(len(grid))]
        first = functools.reduce(lambda a, b: a & b, [i == 0 for i in ids])
        last = functools.reduce(lambda a, b: a & b, [i == g - 1 for i, g in zip(ids, grid)])

        @pl.when(first)
        def _():
            comm.start(cins, couts, sems)

        body(*ins, *outs, *scr)

        @pl.when(last)
        def _():
            comm.finish(cins, couts, sems)

    any_spec = pl.BlockSpec(memory_space=pl.ANY)
    outs = pl.pallas_call(
        riding, name=name, grid=grid,
        in_specs=list(in_specs) + [any_spec] * ci, out_specs=list(out_specs) + [any_spec] * co,
        out_shape=list(out_shape) + comm.out_shape, scratch_shapes=list(scratch) + comm.scratch,
        compiler_params=_params(sem))(*args, *comm.inputs)
    return list(outs[:n_out]), list(outs[n_out:])


def _rms_bwd(dn, xhat, inv, g):
    dxhat = dn * g
    dx = inv * (dxhat - xhat * jnp.mean(dxhat * xhat, axis=-1, keepdims=True))
    return dx, jnp.sum(dn * xhat, axis=0, keepdims=True)


def _ffn_fwd_loss(name, x, g, wgt, wut, wd, target, gf, tm):
    T, D = x.shape
    K, Fs, _ = wgt.shape

    def body(x_ref, g_ref, wg_ref, wu_ref, wd_ref, t_ref, gf_ref,
             dx_ref, hg_ref, hu_ref, n_ref, loss_ref, dgf_ref, acc_scr):
        i, k = pl.program_id(0), pl.program_id(1)

        @pl.when(k == 0)
        def _():
            xhat, _ = _rms(x_ref[...])
            n_ref[...] = (xhat * g_ref[...]).astype(BF16)
            acc_scr[...] = jnp.zeros_like(acc_scr)

        @pl.when((k == 0) & (i == 0))
        def _():
            loss_ref[...] = jnp.zeros_like(loss_ref)
            dgf_ref[...] = jnp.zeros_like(dgf_ref)

        n = n_ref[...]
        hg = _dot_nt(n, wg_ref[...])
        hu = _dot_nt(n, wu_ref[...])
        hg_ref[...] = hg.astype(BF16)
        hu_ref[...] = hu.astype(BF16)
        act = (hg * _sigmoid(hg) * hu).astype(BF16)
        acc_scr[...] += _dot(act, wd_ref[...])

        @pl.when(k == K - 1)
        def _():
            gfv = gf_ref[...]
            for r0 in range(0, tm, ROW_BLOCK):
                rows = slice(r0, r0 + ROW_BLOCK)
                xhat, inv = _rms(x_ref[rows, :] + FFN_RES * acc_scr[rows, :])
                err = xhat * gfv - t_ref[rows, :]
                loss_ref[...] += 0.5 * jnp.sum(jnp.sum(err * err, axis=1, keepdims=True), axis=0, keepdims=True) / D
                dx, dg = _rms_bwd(err * (1.0 / D), xhat, inv, gfv)
                dx_ref[rows, :] = dx
                dgf_ref[...] += dg

    w_spec = pl.BlockSpec((None, Fs, D), lambda i, k: (k, 0, 0))
    act_spec = pl.BlockSpec((None, tm, Fs), lambda i, k: (k, i, 0))
    row = pl.BlockSpec((tm, D), lambda i, k: (i, 0))
    vec = pl.BlockSpec((1, D), lambda i, k: (0, 0))
    return _pallas(
        body, name, (T // tm, K),
        [row, vec, w_spec, w_spec, w_spec, row, vec],
        [row, act_spec, act_spec, row, pl.BlockSpec((1, LANES), lambda i, k: (0, 0)), vec],
        [jax.ShapeDtypeStruct((T, D), F32), jax.ShapeDtypeStruct((K, T, Fs), BF16),
         jax.ShapeDtypeStruct((K, T, Fs), BF16), jax.ShapeDtypeStruct((T, D), BF16),
         jax.ShapeDtypeStruct((1, LANES), F32), jax.ShapeDtypeStruct((1, D), F32)],
        [pltpu.VMEM((tm, D), F32)],
        (x, g, wgt, wut, wd, target, gf))[0]


def _ffn_up_gather(name, x, g, wg_own, wu_own, order, tm, comm=None):
    T, D = x.shape
    Fs = wg_own.shape[0]
    nt = T // tm
    ci, co = (len(comm.inputs), len(comm.out_shape)) if comm is not None else (0, 0)

    def body(order_ref, x_ref, g_ref, wgo_ref, wuo_ref, *rest):
        cins, rest = rest[:ci], rest[ci:]
        (hg_ref, hu_ref, n_ref, sg_ref, su_ref), rest = rest[:5], rest[5:]
        couts, rest = rest[:co], rest[co:]
        (n_all, wbuf, send_sems, recv_sems, pass_send, pass_recv, load_sems), csems = rest[:7], rest[7:]
        k, i = pl.program_id(0), pl.program_id(1)
        x_pos, y_pos, c, others = _place()
        me = 2 * x_pos + y_pos
        owns, stacks = (wgo_ref, wuo_ref), (sg_ref, su_ref)
        mine, theirs = _col_halves(D, c)

        def chip_copy(a, j, chip):
            return pltpu.make_async_remote_copy(
                src_ref=owns[a].at[:, mine], dst_ref=stacks[a].at[chip, :, mine],
                send_sem=send_sems.at[3 * a + j], recv_sem=recv_sems.at[3 * a + j],
                device_id=(*others[j], c), device_id_type=MESH)

        def pass_copy(a, j, chip, half):
            return pltpu.make_async_remote_copy(
                src_ref=stacks[a].at[chip, :, half], dst_ref=stacks[a].at[chip, :, half],
                send_sem=pass_send.at[3 * a + j], recv_sem=pass_recv.at[3 * a + j],
                device_id=(x_pos, y_pos, 1 - c), device_id_type=MESH)

        @pl.when((k == 0) & (i == 0))
        def _():
            for a in range(2):
                for j in range(3):
                    chip_copy(a, j, me).start()
            if comm is not None:
                comm.start(cins, couts, csems)

        def bring(j):
            ox, oy = others[j]
            chip = 2 * ox + oy
            for a in range(2):
                chip_copy(a, j, chip).wait_recv()
            for a in range(2):
                pass_copy(a, j, chip, mine).start()
            for a in range(2):
                pass_copy(a, j, chip, theirs).wait_recv()
            loads = [pltpu.make_async_copy(stacks[a].at[chip], wbuf.at[j % 2, a], load_sems.at[2 * (j % 2) + a])
                     for a in range(2)]
            for cp in loads:
                cp.start()
            for cp in loads:
                cp.wait()

        @pl.when((k == 1) & (i == 0))
        def _():
            bring(0)
            bring(1)

        @pl.when((k == 2) & (i == nt - 1))
        def _():
            bring(2)

        rows = pl.ds(pl.multiple_of(i * tm, tm), tm)

        @pl.when(k == 0)
        def _():
            xhat, _ = _rms(x_ref[...])
            n = (xhat * g_ref[...]).astype(BF16)
            n_ref[...] = n
            n_all[rows, :] = n
            hg_ref[...] = _dot_nt(n, wgo_ref[...]).astype(BF16)
            hu_ref[...] = _dot_nt(n, wuo_ref[...]).astype(BF16)

        @pl.when(k > 0)
        def _():
            n = n_all[rows, :]
            slot = (k - 1) % 2
            hg_ref[...] = _dot_nt(n, wbuf[slot, 0]).astype(BF16)
            hu_ref[...] = _dot_nt(n, wbuf[slot, 1]).astype(BF16)

        @pl.when((k == N_CHIPS - 1) & (i == nt - 1))
        def _():
            for a in range(2):
                for j, (ox, oy) in enumerate(others):
                    chip_copy(a, j, me).wait_send()
                    pass_copy(a, j, 2 * ox + oy, mine).wait_send()
            if comm is not None:
                comm.finish(cins, couts, csems)

    any_spec = pl.BlockSpec(memory_space=pl.ANY)
    first_pass = lambda k, i, order_ref: (jnp.where(k == 0, i, nt - 1), 0)
    whole = pl.BlockSpec((Fs, D), lambda k, i, order_ref: (0, 0))
    act_spec = pl.BlockSpec((None, tm, Fs), lambda k, i, order_ref: (order_ref[k], i, 0))
    stack = jax.ShapeDtypeStruct((N_CHIPS, Fs, D), BF16)
    outs = pl.pallas_call(
        body, name=name,
        grid_spec=pltpu.PrefetchScalarGridSpec(
            num_scalar_prefetch=1, grid=(N_CHIPS, nt),
            in_specs=[pl.BlockSpec((tm, D), first_pass), pl.BlockSpec((1, D), lambda k, i, order_ref: (0, 0)),
                      whole, whole] + [any_spec] * ci,
            out_specs=[act_spec, act_spec, pl.BlockSpec((tm, D), first_pass), any_spec, any_spec] + [any_spec] * co,
            scratch_shapes=[pltpu.VMEM((T, D), BF16), pltpu.VMEM((2, 2, Fs, D), BF16)]
            + [pltpu.SemaphoreType.DMA((6,))] * 4 + [pltpu.SemaphoreType.DMA((4,))]
            + (comm.scratch if comm is not None else [])),
        out_shape=[jax.ShapeDtypeStruct((N_CHIPS, T, Fs), BF16), jax.ShapeDtypeStruct((N_CHIPS, T, Fs), BF16),
                   jax.ShapeDtypeStruct((T, D), BF16), stack, stack] + (comm.out_shape if comm is not None else []),
        compiler_params=_params(("arbitrary", "arbitrary")),
    )(order, x, g, wg_own, wu_own, *(comm.inputs if comm is not None else []))
    return list(outs[:5]), list(outs[5:])


def _ffn_down(name, x, hg, hu, wd, tm, comm=None):
    T, D = x.shape
    K, Fs, _ = wd.shape

    def body(x_ref, hg_ref, hu_ref, wd_ref, out_ref, acc_scr):
        k = pl.program_id(1)

        @pl.when(k == 0)
        def _():
            acc_scr[...] = jnp.zeros_like(acc_scr)

        hgv = hg_ref[...].astype(F32)
        act = (hgv * _sigmoid(hgv) * hu_ref[...].astype(F32)).astype(BF16)
        acc_scr[...] += _dot(act, wd_ref[...])

        @pl.when(k == K - 1)
        def _():
            out_ref[...] = x_ref[...] + FFN_RES * acc_scr[...]

    act_spec = pl.BlockSpec((None, tm, Fs), lambda i, k: (k, i, 0))
    row = pl.BlockSpec((tm, D), lambda i, k: (i, 0))
    return _pallas(
        body, name, (T // tm, K),
        [row, act_spec, act_spec, pl.BlockSpec((None, Fs, D), lambda i, k: (k, 0, 0))],
        [row], [jax.ShapeDtypeStruct((T, D), F32)], [pltpu.VMEM((tm, D), F32)],
        (x, hg, hu, wd), comm)


def _ffn_bwd_dx(name, dout, x, g, hg, hu, wgt, wut, wd, tm, comm=None):
    T, D = x.shape
    K, Fs, _ = wgt.shape

    def body(dout_ref, x_ref, g_ref, hg_ref, hu_ref, wg_ref, wu_ref, wd_ref,
             dx_ref, dhg_ref, dhu_ref, dg_ref, df_ref, dn_scr):
        i, k = pl.program_id(0), pl.program_id(1)

        @pl.when(k == 0)
        def _():
            df_ref[...] = (FFN_RES * dout_ref[...]).astype(BF16)
            dn_scr[...] = jnp.zeros_like(dn_scr)

        @pl.when((k == 0) & (i == 0))
        def _():
            dg_ref[...] = jnp.zeros_like(dg_ref)

        for r0 in range(0, tm, ROW_BLOCK):
            rows = slice(r0, r0 + ROW_BLOCK)
            dact = _dot_nt(df_ref[rows, :], wd_ref[...])
            hgv = hg_ref[rows, :].astype(F32)
            huv = hu_ref[rows, :].astype(F32)
            s = _sigmoid(hgv)
            dhu = (dact * (hgv * s)).astype(BF16)
            dhg = (dact * huv * (s * (1.0 + hgv * (1.0 - s)))).astype(BF16)
            dhg_ref[rows, :] = dhg
            dhu_ref[rows, :] = dhu
            dn_scr[rows, :] += _dot(dhg, wg_ref[...]) + _dot(dhu, wu_ref[...])

        @pl.when(k == K - 1)
        def _():
            xhat, inv = _rms(x_ref[...])
            dx, dg = _rms_bwd(dn_scr[...], xhat, inv, g_ref[...])
            dx_ref[...] = dout_ref[...] + dx
            dg_ref[...] += dg

    w_spec = pl.BlockSpec((None, Fs, D), lambda i, k: (k, 0, 0))
    act_spec = pl.BlockSpec((None, tm, Fs), lambda i, k: (k, i, 0))
    row = pl.BlockSpec((tm, D), lambda i, k: (i, 0))
    row_once = pl.BlockSpec((tm, D), lambda i, k: (i, 0), pipeline_mode=pl.Buffered(1))
    vec = pl.BlockSpec((1, D), lambda i, k: (0, 0))
    return _pallas(
        body, name, (T // tm, K),
        [row, row_once, vec, act_spec, act_spec, w_spec, w_spec, w_spec],
        [row_once, act_spec, act_spec, vec, row],
        [jax.ShapeDtypeStruct((T, D), F32), jax.ShapeDtypeStruct((K, T, Fs), BF16),
         jax.ShapeDtypeStruct((K, T, Fs), BF16), jax.ShapeDtypeStruct((1, D), F32),
         jax.ShapeDtypeStruct((T, D), BF16)],
        [pltpu.VMEM((tm, D), F32)],
        (dout, x, g, hg, hu, wgt, wut, wd), comm)


def _ffn_bwd_dw(name, n, df, hg, hu, dhg, dhu, tk, comm=None):
    T, D = n.shape
    K, _, Fs = hg.shape
    nt = T // tk

    def body(n_ref, df_ref, hg_ref, hu_ref, dhg_ref, dhu_ref, dwg_ref, dwu_ref, dwd_ref, accg, accu, accd):
        t = pl.program_id(1)

        @pl.when(t == 0)
        def _():
            accg[...] = jnp.zeros_like(accg)
            accu[...] = jnp.zeros_like(accu)
            accd[...] = jnp.zeros_like(accd)

        nv = n_ref[...]
        hgv = hg_ref[...].astype(F32)
        act = (hgv * _sigmoid(hgv) * hu_ref[...].astype(F32)).astype(BF16)
        accg[...] += _dot_tn(dhg_ref[...], nv)
        accu[...] += _dot_tn(dhu_ref[...], nv)
        accd[...] += _dot_tn(act, df_ref[...])

        @pl.when(t == nt - 1)
        def _():
            dwg_ref[...] = accg[...].astype(BF16)
            dwu_ref[...] = accu[...].astype(BF16)
            dwd_ref[...] = accd[...].astype(BF16)

    act_spec = pl.BlockSpec((None, tk, Fs), lambda k, t: (k, t, 0))
    w_spec = pl.BlockSpec((None, Fs, D), lambda k, t: (k, 0, 0))
    row = pl.BlockSpec((tk, D), lambda k, t: (t, 0))
    return _pallas(
        body, name, (K, nt),
        [row, row, act_spec, act_spec, act_spec, act_spec],
        [w_spec, w_spec, w_spec],
        [jax.ShapeDtypeStruct((K, Fs, D), BF16)] * 3,
        [pltpu.VMEM((Fs, D), F32)] * 3,
        (n, df, hg, hu, dhg, dhu), comm)


def _ffn_bwd_dw_reduced(name, n, df, hg, hu, dhg, dhu, tk, comm=None):
    T, D = n.shape
    K, _, Fs = hg.shape
    nt = T // tk
    hc = D // 2
    assert nt >= 2, "a pass's swap is finished at the second step of the next pass"

    def body(n_ref, df_ref, hg_ref, hu_ref, dhg_ref, dhu_ref, pg_ref, pu_ref, pd_ref,
             accg, accu, accd, stage, land, send_sems, recv_sems):
        k, t = pl.program_id(0), pl.program_id(1)
        x, y, c, _ = _place()
        accs, outs = (accg, accu, accd), (pg_ref, pu_ref, pd_ref)

        def swap(j, a):
            return pltpu.make_async_remote_copy(
                src_ref=stage.at[a], dst_ref=land.at[j % 2, a],
                send_sem=send_sems.at[3 * j + a], recv_sem=recv_sems.at[3 * j + a],
                device_id=(x, y, 1 - c), device_id_type=MESH)

        def finish(j):
            for a in range(3):
                swap(j, a).wait_recv()
                outs[a][j] = (outs[a][j].astype(F32) + land[j % 2, a].astype(F32)).astype(BF16)

        @pl.when(t == 0)
        def _():
            for acc in accs:
                acc[...] = jnp.zeros_like(acc)

        nv = n_ref[...]
        hgv = hg_ref[...].astype(F32)
        act = (hgv * _sigmoid(hgv) * hu_ref[...].astype(F32)).astype(BF16)
        accg[...] += _dot_tn(dhg_ref[...], nv)
        accu[...] += _dot_tn(dhu_ref[...], nv)
        accd[...] += _dot_tn(act, df_ref[...])

        for j in range(K - 1):
            @pl.when((k == j + 1) & (t == 1))
            def _(j=j):
                finish(j)

        @pl.when(t == nt - 1)
        def _():
            for j in range(K - 1):
                @pl.when(k == j + 1)
                def _(j=j):
                    for a in range(3):
                        swap(j, a).wait_send()
            for a in range(3):
                @pl.when(c == 0)
                def _(a=a):
                    outs[a][k] = accs[a][:, :hc].astype(BF16)
                    stage[a] = accs[a][:, hc:].astype(BF16)

                @pl.when(c == 1)
                def _(a=a):
                    outs[a][k] = accs[a][:, hc:].astype(BF16)
                    stage[a] = accs[a][:, :hc].astype(BF16)
            for j in range(K):
                @pl.when(k == j)
                def _(j=j):
                    for a in range(3):
                        swap(j, a).start()

        @pl.when((k == K - 1) & (t == nt - 1))
        def _():
            finish(K - 1)
            for a in range(3):
                swap(K - 1, a).wait_send()

    act_spec = pl.BlockSpec((None, tk, Fs), lambda k, t: (k, t, 0))
    row = pl.BlockSpec((tk, D), lambda k, t: (t, 0))
    resident = pl.BlockSpec(memory_space=pltpu.VMEM)
    part = jax.ShapeDtypeStruct((K, Fs, hc), BF16)
    return _pallas(
        body, name, (K, nt),
        [row, row, act_spec, act_spec, act_spec, act_spec],
        [resident, resident, resident], [part, part, part],
        [pltpu.VMEM((Fs, D), F32)] * 3
        + [pltpu.VMEM((3, Fs, hc), BF16), pltpu.VMEM((2, 3, Fs, hc), BF16),
           pltpu.SemaphoreType.DMA((3 * K,)), pltpu.SemaphoreType.DMA((3 * K,))],
        (n, df, hg, hu, dhg, dhu), comm)


def _mix_proj_fwd(x, g, wproj_t, wf_t, tm, tn, comm=None):
    T, D = x.shape
    N = wproj_t.shape[0]

    def body(x_ref, g_ref, w_ref, wf_ref, h_ref, proj_ref, flog_ref, h_scr):
        @pl.when(pl.program_id(1) == 0)
        def _():
            xhat, _ = _rms(x_ref[...])
            h = (xhat * g_ref[...]).astype(BF16)
            h_scr[...] = h
            h_ref[...] = h
            flog_ref[...] = _dot_nt(h, wf_ref[...])

        proj_ref[...] = _dot_nt(h_scr[...], w_ref[...]).astype(BF16)

    return _pallas(
        body, "mix_proj_fwd", (T // tm, N // tn),
        [pl.BlockSpec((tm, D), lambda i, n: (i, 0)), pl.BlockSpec((1, D), lambda i, n: (0, 0)),
         pl.BlockSpec((tn, D), lambda i, n: (n, 0)), pl.BlockSpec((LANES, D), lambda i, n: (0, 0))],
        [pl.BlockSpec((tm, D), lambda i, n: (i, 0)), pl.BlockSpec((tm, tn), lambda i, n: (i, n)),
         pl.BlockSpec((tm, LANES), lambda i, n: (i, 0))],
        [jax.ShapeDtypeStruct((T, D), BF16), jax.ShapeDtypeStruct((T, N), BF16),
         jax.ShapeDtypeStruct((T, LANES), F32)],
        [pltpu.VMEM((tm, D), BF16)],
        (x, g, wproj_t, wf_t), comm)


def _log_sigmoid(z):
    return -(jnp.maximum(-z, 0.0) + jnp.log(1.0 + jnp.exp(-jnp.abs(z))))


def _tri(n, lower):
    r = lax.broadcasted_iota(jnp.int32, (n, n), 0)
    c = lax.broadcasted_iota(jnp.int32, (n, n), 1)
    return jnp.where((r >= c) if lower else (r <= c), 1.0, 0.0).astype(F32)


def _dot_f32(a, b):
    return lax.dot_general(a, b, (((1,), (0,)), ((), ())), preferred_element_type=F32,
                           precision=lax.Precision.HIGHEST)


def _fgate_fwd(flog, bias, B, S, ch):
    def body(flog_ref, b_ref, cum_ref):
        tri = _tri(ch, True)
        carry = jnp.zeros((1, LANES), F32)
        for c0 in range(0, S, ch):
            lf = _log_sigmoid(flog_ref[c0:c0 + ch, :] + b_ref[...])
            cs = _dot_f32(tri, lf) + carry
            cum_ref[c0:c0 + ch, :] = cs
            carry = cs[ch - 1:ch, :]

    return pl.pallas_call(
        body, name="fgate_fwd", grid=(B,),
        in_specs=[pl.BlockSpec((S, LANES), lambda b: (b, 0)),
                  pl.BlockSpec((1, LANES), lambda b: (0, 0))],
        out_specs=pl.BlockSpec((S, LANES), lambda b: (b, 0)),
        out_shape=jax.ShapeDtypeStruct((B * S, LANES), F32),
        compiler_params=_params(("arbitrary",)),
    )(flog, bias)


def _fgate_bwd(dcum, flog, bias, B, S, ch):
    def body(dcum_ref, flog_ref, b_ref, dflog_ref, db_ref):
        @pl.when(pl.program_id(0) == 0)
        def _():
            db_ref[...] = jnp.zeros_like(db_ref)

        tri = _tri(ch, False)
        carry = jnp.zeros((1, LANES), F32)
        db = jnp.zeros((1, LANES), F32)
        for c0 in range(S - ch, -1, -ch):
            dlf = _dot_f32(tri, dcum_ref[c0:c0 + ch, :]) + carry
            carry = dlf[0:1, :]
            z = flog_ref[c0:c0 + ch, :] + b_ref[...]
            dz = dlf * _sigmoid(-z)
            dflog_ref[c0:c0 + ch, :] = dz
            db = db + jnp.sum(dz, axis=0, keepdims=True)
        db_ref[...] += db

    return pl.pallas_call(
        body, name="fgate_bwd", grid=(B,),
        in_specs=[pl.BlockSpec((S, LANES), lambda b: (b, 0)),
                  pl.BlockSpec((S, LANES), lambda b: (b, 0)),
                  pl.BlockSpec((1, LANES), lambda b: (0, 0))],
        out_specs=[pl.BlockSpec((S, LANES), lambda b: (b, 0)),
                   pl.BlockSpec((1, LANES), lambda b: (0, 0))],
        out_shape=[jax.ShapeDtypeStruct((B * S, LANES), F32),
                   jax.ShapeDtypeStruct((1, LANES), F32)],
        compiler_params=_params(("arbitrary",)),
    )(dcum, flog, bias)


def _pick_lane(tile, h):
    lane = lax.broadcasted_iota(jnp.int32, tile.shape, 1)
    return jnp.sum(jnp.where(lane == h, tile, 0.0), axis=1, keepdims=True)


def _put_lane(col, h, width=LANES):
    lane = lax.broadcasted_iota(jnp.int32, (col.shape[0], width), 1)
    return jnp.where(lane == h, col, 0.0)


def _pick_row(tile, h):
    row = lax.broadcasted_iota(jnp.int32, tile.shape, 0)
    return jnp.sum(jnp.where(row == h, tile, 0.0), axis=0, keepdims=True)


def _put_row(vec, h):
    row = lax.broadcasted_iota(jnp.int32, (8, vec.shape[1]), 0)
    return jnp.where(row == h, vec, 0.0)


def _causal(tq):
    r = lax.broadcasted_iota(jnp.int32, (tq, tq), 0)
    c = lax.broadcasted_iota(jnp.int32, (tq, tq), 1)
    return r >= c


def _head_halves(t):
    lo = lax.broadcasted_iota(jnp.int32, t.shape, 1) < HEAD_DIM
    zero = jnp.zeros_like(t)
    return jnp.where(lo, t, zero), jnp.where(lo, zero, t)


NEG = -1e30
ATTN_SCALE = 1.0 / math.sqrt(HEAD_DIM)


def _scaled(q):
    return (q.astype(F32) * ATTN_SCALE).astype(q.dtype)


def _attn_fwd(proj, cum, cum_t, B, S, tq, comm=None):
    nq = S // tq

    def body(q_ref, k_ref, v_ref, cum_ref, cumt_ref, o_ref, lse_ref):
        qi, hp = pl.program_id(1), pl.program_id(2)
        qm = _head_halves(_scaled(q_ref[...]))
        first_head = lax.broadcasted_iota(jnp.int32, (LANES, tq), 0) < HEAD_DIM
        r = lax.broadcasted_iota(jnp.int32, (tq, tq), 0)
        c = lax.broadcasted_iota(jnp.int32, (tq, tq), 1)

        def tile(j, carry, masked):
            (ma, la), (mb, lb), acc = carry
            off = pl.multiple_of(j * tq, tq)
            kj = k_ref[pl.ds(off, tq), :]
            vm = _head_halves(v_ref[pl.ds(off, tq), :])
            cumk = cum_ref[pl.ds(off, tq), :]
            new, alphas, pv = [], [], jnp.zeros((LANES, tq), F32)
            for e, (m, l) in enumerate(((ma, la), (mb, lb))):
                s = _dot_nt(kj, qm[e]) - _pick_lane(cumk, 2 * hp + e)
                if masked:
                    s = jnp.where(r <= c, s, NEG)
                m_new = jnp.maximum(m, jnp.max(s, axis=0, keepdims=True))
                p = jnp.exp(s - m_new)
                alpha = jnp.exp(m - m_new)
                new.append((m_new, alpha * l + jnp.sum(p, axis=0, keepdims=True)))
                alphas.append(alpha)
                pv = pv + _dot_tn(vm[e], p.astype(BF16))
            acc = jnp.where(first_head, alphas[0], alphas[1]) * acc + pv
            return new[0], new[1], acc

        one = (jnp.full((1, tq), NEG, F32), jnp.zeros((1, tq), F32))
        carry = lax.fori_loop(0, qi, lambda j, cr: tile(j, cr, False), (one, one, jnp.zeros((LANES, tq), F32)))
        (ma, la), (mb, lb), acc = tile(qi, carry, True)
        o_ref[...] = (acc / jnp.where(first_head, la, lb)).T.astype(BF16)

        @pl.when(hp == 0)
        def _():
            lse_ref[...] = jnp.zeros_like(lse_ref)

        ct = cumt_ref[...]
        lse_ref[...] += (_put_row(ma + jnp.log(la) + _pick_row(ct, 2 * hp), 2 * hp)
                         + _put_row(mb + jnp.log(lb) + _pick_row(ct, 2 * hp + 1), 2 * hp + 1))

    kv = lambda first: pl.BlockSpec((S, LANES), lambda b, i, hp: (b, first + hp))
    row_block = pl.BlockSpec((None, None, 8, tq), lambda b, i, hp: (b, i, 0, 0))
    return _pallas(
        body, "attn_fwd", (B, nq, HEAD_PAIRS),
        [pl.BlockSpec((tq, LANES), lambda b, i, hp: (b * nq + i, hp)),
         kv(ATTN_W // LANES), kv(2 * ATTN_W // LANES),
         pl.BlockSpec((S, LANES), lambda b, i, hp: (b, 0)), row_block],
        [pl.BlockSpec((tq, LANES), lambda b, i, hp: (b * nq + i, hp)), row_block],
        [jax.ShapeDtypeStruct((B * S, ATTN_W), BF16), jax.ShapeDtypeStruct((B, nq, 8, tq), F32)],
        [], (proj, proj, proj, cum, cum_t), comm)


def _attn_bwd(proj, o, do, lse, cum, cum_t, B, S, tq, comm=None):
    nq = S // tq

    def body(q_ref, k_ref, v_ref, o_ref, do_ref, lse_ref, cum_ref, cumt_ref,
             dq_ref, dk_ref, dv_ref, dcq_ref, dck_ref, dq_scr):
        hp, kj = pl.program_id(1), pl.program_id(2)

        @pl.when(kj == 0)
        def _():
            dq_scr[...] = jnp.zeros_like(dq_scr)

        @pl.when((kj == 0) & (hp == 0))
        def _():
            dcq_ref[...] = jnp.zeros_like(dcq_ref)
            dck_ref[...] = jnp.zeros_like(dck_ref)

        kv = k_ref[...]
        vv = v_ref[...]
        km = _head_halves(kv)
        ct = cumt_ref[...]
        ck = [_pick_row(ct, 2 * hp + e) for e in range(2)]

        def tile(i, carry, masked):
            dk, dv, dcol = carry
            off = pl.multiple_of(i * tq, tq)
            qi = q_ref[pl.ds(off, tq), :]
            ov = o_ref[pl.ds(off, tq), :].astype(F32)
            qm = _head_halves(_scaled(qi))
            dom = _head_halves(do_ref[pl.ds(off, tq), :])
            cumv = cum_ref[pl.ds(off, tq), :]
            lsev = lse_ref[pl.ds(off, tq), :]
            dcq = jnp.zeros((tq, LANES), F32)
            dq = jnp.zeros((tq, LANES), F32)
            dcol_new = []
            for e in range(2):
                delta = jnp.sum(dom[e].astype(F32) * ov, axis=1, keepdims=True)
                row_term = _pick_lane(cumv, 2 * hp + e) - _pick_lane(lsev, 2 * hp + e)
                p = jnp.exp(_dot_nt(qm[e], kv) + row_term - ck[e])
                if masked:
                    p = jnp.where(_causal(tq), p, 0.0)
                dv = dv + _dot_tn(dom[e], p.astype(BF16))
                ds = p * (_dot_nt(dom[e], vv) - delta)
                dcol_new.append(dcol[e] + jnp.sum(ds, axis=0, keepdims=True))
                dcq = dcq + _put_lane(jnp.sum(ds, axis=1, keepdims=True), 2 * hp + e)
                dsb = ds.astype(BF16)
                dk = dk + _dot_tn(qm[e], dsb)
                dq = dq + _dot(dsb, km[e]) * ATTN_SCALE
            dq_scr[pl.ds(off, tq), :] += dq
            dcq_ref[pl.ds(off, tq), :] += dcq
            return dk, dv, tuple(dcol_new)

        zero_row = jnp.zeros((1, tq), F32)
        init = (jnp.zeros((LANES, tq), F32), jnp.zeros((LANES, tq), F32), (zero_row, zero_row))
        carry = tile(kj, init, True)
        dk, dv, dcol = lax.fori_loop(kj + 1, nq, lambda i, c: tile(i, c, False), carry)
        dk_ref[...] = dk.T.astype(BF16)
        dv_ref[...] = dv.T.astype(BF16)
        dck_ref[kj] += -(_put_row(dcol[0], 2 * hp) + _put_row(dcol[1], 2 * hp + 1))

        @pl.when(kj == nq - 1)
        def _():
            dq_ref[...] = dq_scr[...].astype(BF16)

    seq = lambda first: pl.BlockSpec((S, LANES), lambda b, hp, j: (b, first + hp))
    tile_in = lambda first: pl.BlockSpec((tq, LANES), lambda b, hp, j: (b * nq + j, first + hp))
    lanes0 = pl.BlockSpec((S, LANES), lambda b, hp, j: (b, 0))
    out = jax.ShapeDtypeStruct((B * S, ATTN_W), BF16)
    return _pallas(
        body, "attn_bwd", (B, HEAD_PAIRS, nq),
        [seq(0), tile_in(ATTN_W // LANES), tile_in(2 * ATTN_W // LANES), seq(0), seq(0), lanes0, lanes0,
         pl.BlockSpec((None, None, 8, tq), lambda b, hp, j: (b, j, 0, 0))],
        [seq(0), tile_in(0), tile_in(0), lanes0,
         pl.BlockSpec((None, nq, 8, tq), lambda b, hp, j: (b, 0, 0, 0))],
        [out, out, out, jax.ShapeDtypeStruct((B * S, LANES), F32), jax.ShapeDtypeStruct((B, nq, 8, tq), F32)],
        [pltpu.VMEM((S, LANES), F32)],
        (proj, proj, proj, o, do, lse, cum, cum_t), comm)


def _shift_down(u, n):
    row = lax.broadcasted_iota(jnp.int32, u.shape, 0)
    return jnp.where(row >= n, pltpu.roll(u, n, 0), 0.0)


def _shift_up(u, n):
    rows = u.shape[0]
    row = lax.broadcasted_iota(jnp.int32, u.shape, 0)
    return jnp.where(row < rows - n, pltpu.roll(u, rows - n, 0), 0.0)


def _conv_specs(S):
    cb = pl.BlockSpec((S, LANES), lambda g, b: (b, COL_CB // LANES + g))
    cc = pl.BlockSpec((S, LANES), lambda g, b: (b, COL_CC // LANES + g))
    cx = pl.BlockSpec((S, LANES), lambda g, b: (b, COL_CX // LANES + g))
    w = pl.BlockSpec((8, LANES), lambda g, b: (0, g))
    return cb, cc, cx, w


def _conv_fwd(proj, conv_w, B, S):
    def body(cb_ref, cc_ref, cx_ref, w_ref, y_ref):
        u = cc_ref[...].astype(F32) * cx_ref[...].astype(F32)
        w = w_ref[...]
        conv = w[0:1, :] * _shift_down(u, 2) + w[1:2, :] * _shift_down(u, 1) + w[2:3, :] * u
        y_ref[...] = (cb_ref[...].astype(F32) * conv).astype(BF16)

    cb, cc, cx, w = _conv_specs(S)
    return pl.pallas_call(
        body, name="conv_fwd", grid=(CONV_W // LANES, B),
        in_specs=[cb, cc, cx, w],
        out_specs=pl.BlockSpec((S, LANES), lambda g, b: (b, g)),
        out_shape=jax.ShapeDtypeStruct((B * S, CONV_W), BF16),
        compiler_params=_params(("arbitrary", "arbitrary")),
    )(proj, proj, proj, conv_w)


def _conv_bwd(dy, proj, conv_w, B, S):
    def body(dy_ref, cb_ref, cc_ref, cx_ref, w_ref, dcb_ref, dcc_ref, dcx_ref, dw_ref):
        @pl.when(pl.program_id(1) == 0)
        def _():
            dw_ref[...] = jnp.zeros_like(dw_ref)

        ccv = cc_ref[...].astype(F32)
        cxv = cx_ref[...].astype(F32)
        u = ccv * cxv
        u1 = _shift_down(u, 1)
        u2 = _shift_down(u, 2)
        w = w_ref[...]
        conv = w[0:1, :] * u2 + w[1:2, :] * u1 + w[2:3, :] * u
        dyv = dy_ref[...].astype(F32)
        dcb_ref[...] = (dyv * conv).astype(BF16)
        dconv = dyv * cb_ref[...].astype(F32)
        du = w[2:3, :] * dconv + w[1:2, :] * _shift_up(dconv, 1) + w[0:1, :] * _shift_up(dconv, 2)
        dcc_ref[...] = (du * cxv).astype(BF16)
        dcx_ref[...] = (du * ccv).astype(BF16)
        row = lax.broadcasted_iota(jnp.int32, (8, LANES), 0)
        dw = jnp.where(row == 0, jnp.sum(dconv * u2, axis=0, keepdims=True),
                       jnp.where(row == 1, jnp.sum(dconv * u1, axis=0, keepdims=True),
                                 jnp.where(row == 2, jnp.sum(dconv * u, axis=0, keepdims=True), 0.0)))
        dw_ref[...] += dw

    cb, cc, cx, w = _conv_specs(S)
    out = pl.BlockSpec((S, LANES), lambda g, b: (b, g))
    return pl.pallas_call(
        body, name="conv_bwd", grid=(CONV_W // LANES, B),
        in_specs=[out, cb, cc, cx, w],
        out_specs=[out, out, out, w],
        out_shape=[jax.ShapeDtypeStruct((B * S, CONV_W), BF16)] * 3 + [jax.ShapeDtypeStruct((8, CONV_W), F32)],
        compiler_params=_params(("arbitrary", "arbitrary")),
    )(dy, proj, proj, proj, conv_w)


def _gate_specs(tm, D):
    ga = pl.BlockSpec((tm, D), lambda i: (i, COL_GATES // D))
    gc = pl.BlockSpec((tm, D), lambda i: (i, COL_GATES // D + 1))
    return ga, gc


def _mix_out_fwd(x, o, yc, proj, woa, woc, wout, tm):
    T, D = x.shape

    def body(x_ref, o_ref, yc_ref, ga_ref, gc_ref, woa_ref, woc_ref, wout_ref, out_ref):
        ya = _dot(o_ref[...], woa_ref[...])
        yp = _dot(yc_ref[...], woc_ref[...])
        merged = _sigmoid(ga_ref[...].astype(F32)) * ya + _sigmoid(gc_ref[...].astype(F32)) * yp
        out_ref[...] = x_ref[...] + _dot(merged.astype(BF16), wout_ref[...])

    ga, gc = _gate_specs(tm, D)
    row = lambda w: pl.BlockSpec((tm, w), lambda i: (i, 0))
    whole = lambda a: pl.BlockSpec(a.shape, lambda i: (0, 0))
    return pl.pallas_call(
        body, name="mix_out_fwd", grid=(T // tm,),
        in_specs=[row(D), row(ATTN_W), row(CONV_W), ga, gc, whole(woa), whole(woc), whole(wout)],
        out_specs=row(D),
        out_shape=jax.ShapeDtypeStruct((T, D), F32),
        compiler_params=_params(("arbitrary",)),
    )(x, o, yc, proj, proj, woa, woc, wout)


def _mix_out_bwd(dx, o, yc, proj, woa, woc, wout, tm, comm=None):
    T, D = dx.shape
    nt = T // tm

    def body(dx_ref, o_ref, yc_ref, ga_ref, gc_ref, woa_ref, woc_ref, wout_ref,
             do_ref, dyc_ref, dg_ref, dwoa_ref, dwoc_ref, dwout_ref, acca, accc, acco):
        t = pl.program_id(0)

        @pl.when(t == 0)
        def _():
            acca[...] = jnp.zeros_like(acca)
            accc[...] = jnp.zeros_like(accc)
            acco[...] = jnp.zeros_like(acco)

        dxb = dx_ref[...].astype(BF16)
        ov, ycv = o_ref[...], yc_ref[...]
        ya = _dot(ov, woa_ref[...])
        yp = _dot(ycv, woc_ref[...])
        sa = _sigmoid(ga_ref[...].astype(F32))
        sc = _sigmoid(gc_ref[...].astype(F32))
        merged = (sa * ya + sc * yp).astype(BF16)
        dm = _dot_nt(dxb, wout_ref[...])
        dya = (dm * sa).astype(BF16)
        dyp = (dm * sc).astype(BF16)
        dg_ref[:, :D] = (dm * ya * sa * (1.0 - sa)).astype(BF16)
        dg_ref[:, D:] = (dm * yp * sc * (1.0 - sc)).astype(BF16)
        do_ref[...] = _dot_nt(dya, woa_ref[...]).astype(BF16)
        dyc_ref[...] = _dot_nt(dyp, woc_ref[...]).astype(BF16)
        acca[...] += _dot_tn(ov, dya)
        accc[...] += _dot_tn(ycv, dyp)
        acco[...] += _dot_tn(merged, dxb)

        @pl.when(t == nt - 1)
        def _():
            dwoa_ref[...] = acca[...].astype(BF16)
            dwoc_ref[...] = accc[...].astype(BF16)
            dwout_ref[...] = acco[...].astype(BF16)

    ga, gc = _gate_specs(tm, D)
    row = lambda w: pl.BlockSpec((tm, w), lambda i: (i, 0))
    whole = lambda a: pl.BlockSpec(a.shape, lambda i: (0, 0))
    return _pallas(
        body, "mix_out_bwd", (nt,),
        [row(D), row(ATTN_W), row(CONV_W), ga, gc, whole(woa), whole(woc), whole(wout)],
        [row(ATTN_W), row(CONV_W), row(2 * D), whole(woa), whole(woc), whole(wout)],
        [jax.ShapeDtypeStruct((T, ATTN_W), BF16), jax.ShapeDtypeStruct((T, CONV_W), BF16),
         jax.ShapeDtypeStruct((T, 2 * D), BF16),
         jax.ShapeDtypeStruct(woa.shape, BF16), jax.ShapeDtypeStruct(woc.shape, BF16),
         jax.ShapeDtypeStruct(wout.shape, BF16)],
        [pltpu.VMEM(woa.shape, F32), pltpu.VMEM(woc.shape, F32), pltpu.VMEM(wout.shape, F32)],
        (dx, o, yc, proj, proj, woa, woc, wout), comm)


def _proj_pieces(dq, dk, dv, dcb, dcc, dcx, dgates, dflog):
    D = dgates.shape[1] // 2
    return [(dq, ATTN_W, 0), (dk, ATTN_W, 0), (dv, ATTN_W, 0), (dcb, CONV_W, 0), (dcc, CONV_W, 0), (dcx, CONV_W, 0),
            (dgates, D, 0), (dgates, D, 1), (dflog, LANES, 0)]


def _mix_proj_bwd_dx(dres, x, g, pieces, wproj_t, wf_t, tm, comm=None):
    T, D = x.shape
    n = len(pieces)
    w_blocks = [(ATTN_W, 0), (ATTN_W, 1), (ATTN_W, 2), (CONV_W, 3), (CONV_W, 4), (CONV_W, 5),
                (D, COL_GATES // D), (D, COL_GATES // D + 1)]

    def body(*refs):
        dres_ref, x_ref, g_ref = refs[:3]
        p_refs, w_refs = refs[3:3 + n], refs[3 + n:3 + 2 * n]
        dx_ref, dg_ref = refs[3 + 2 * n:]

        @pl.when(pl.program_id(0) == 0)
        def _():
            dg_ref[...] = jnp.zeros_like(dg_ref)

        dh = _dot(p_refs[0][...].astype(BF16), w_refs[0][...])
        for p_ref, w_ref in zip(p_refs[1:], w_refs[1:]):
            dh = dh + _dot(p_ref[...].astype(BF16), w_ref[...])
        xhat, inv = _rms(x_ref[...])
        dx, dg = _rms_bwd(dh, xhat, inv, g_ref[...])
        dx_ref[...] = dres_ref[...] + dx
        dg_ref[...] += dg

    row = pl.BlockSpec((tm, D), lambda i: (i, 0))
    vec = pl.BlockSpec((1, D), lambda i: (0, 0))
    p_specs = [pl.BlockSpec((tm, w), lambda i, cb=cb: (i, cb)) for _, w, cb in pieces]
    w_specs = [pl.BlockSpec((r, D), lambda i, rb=rb: (rb, 0)) for r, rb in w_blocks]
    w_specs.append(pl.BlockSpec((LANES, D), lambda i: (0, 0)))
    return _pallas(
        body, "mix_proj_bwd_dx", (T // tm,),
        [row, row, vec] + p_specs + w_specs, [row, vec],
        [jax.ShapeDtypeStruct((T, D), F32), jax.ShapeDtypeStruct((1, D), F32)], [],
        (dres, x, g, *[p for p, _, _ in pieces], *([wproj_t] * len(w_blocks)), wf_t), comm)


def _matmuls_tn(name, pieces, b, tk):
    T, N = b.shape
    nt = T // tk
    n = len(pieces)

    def body(*refs):
        a_refs, b_ref, out_refs, accs = refs[:n], refs[n], refs[n + 1:2 * n + 1], refs[2 * n + 1:]
        t = pl.program_id(0)

        @pl.when(t == 0)
        def _():
            for acc in accs:
                acc[...] = jnp.zeros_like(acc)

        bv = b_ref[...]
        for a_ref, acc in zip(a_refs, accs):
            acc[...] += _dot_tn(a_ref[...].astype(BF16), bv)

        @pl.when(t == nt - 1)
        def _():
            for out_ref, acc in zip(out_refs, accs):
                out_ref[...] = acc[...].astype(BF16)

    return pl.pallas_call(
        body, name=name, grid=(nt,),
        in_specs=[pl.BlockSpec((tk, w), lambda t, cb=cb: (t, cb)) for _, w, cb in pieces]
        + [pl.BlockSpec((tk, N), lambda t: (t, 0))],
        out_specs=[pl.BlockSpec((w, N), lambda t: (0, 0)) for _, w, _ in pieces],
        out_shape=[jax.ShapeDtypeStruct((w, N), BF16) for _, w, _ in pieces],
        scratch_shapes=[pltpu.VMEM((w, N), F32) for _, w, _ in pieces],
        compiler_params=_params(("arbitrary",)),
    )(*[a for a, _, _ in pieces], b)


TOKEN_TILE = 512
TOKEN_TILE_WIDE = 1024
ATTN_TILE = 512
SCAN_CHUNK = 256
PROJ_DX_TILE = 256


def _local_step(x, target, plan, B, S):
    T, D = x.shape
    tm = min(TOKEN_TILE, T)
    tm_fwd = min(TOKEN_TILE_WIDE, T)
    tq = min(ATTN_TILE, S)
    nq = S // tq
    ch = min(SCAN_CHUNK, S)

    def riding(kernel_name, build):
        results, brought = build(plan.rider(kernel_name))
        plan.arrived(kernel_name, brought)
        return results

    hg1, hu1, n1 = plan.ffn1_up(x, tm_fwd)
    w1 = plan.weights("ffn1")
    x1, = riding("ffn1_down", lambda comm: _ffn_down("ffn1_down", x, hg1, hu1, w1["ffn1_down"], tm_fwd, comm))
    wm = plan.weights("mix_in")
    h, proj, flog = riding("mix_proj_fwd", lambda comm: _mix_proj_fwd(
        x1, wm["mix_norm"], wm["w_proj"], wm["w_f"], tm_fwd, PROJ_W // 4, comm))
    wm.update(plan.weights("mix_out"))
    cum = _fgate_fwd(flog, wm["b_forget"], B, S, ch)
    cum_t = jnp.transpose(cum[:, :N_HEADS].reshape(B, nq, tq, N_HEADS), (0, 1, 3, 2))
    o, lse_t = riding("attn_fwd", lambda comm: _attn_fwd(proj, cum, cum_t, B, S, tq, comm))
    lse = jnp.pad(jnp.transpose(lse_t, (0, 1, 3, 2)).reshape(T, N_HEADS), ((0, 0), (0, LANES - N_HEADS)))
    yc = _conv_fwd(proj, wm["conv_w"], B, S)
    x2 = _mix_out_fwd(x1, o, yc, proj, wm["w_o_attn"], wm["w_o_conv"], wm["w_out"], tm)
    w2 = plan.weights("ffn2")
    dx3, hg2, hu2, n2, loss, d_final_norm = _ffn_fwd_loss(
        "ffn2_fwd_loss", x2, w2["ffn2_norm"], w2["ffn2_gate"], w2["ffn2_up"], w2["ffn2_down"], target, w2["final_norm"],
        tm_fwd)

    g = {"final_norm": d_final_norm}
    dx2, dhg2, dhu2, g["ffn2_norm"], df2 = _ffn_bwd_dx("ffn2_bwd_dx", dx3, x2, w2["ffn2_norm"], hg2, hu2,
                                                  w2["ffn2_gate"], w2["ffn2_up"], w2["ffn2_down"], tm_fwd)[0]
    plan.reduce_parts("ffn2", dict(zip(("ffn2_gate", "ffn2_up", "ffn2_down"),
                                       _ffn_bwd_dw_reduced("ffn2_bwd_dw", n2, df2, hg2, hu2, dhg2, dhu2, tm_fwd)[0])))
    do, dyc, dgates, dwoa, dwoc, dwout = riding("mix_out_bwd", lambda comm: _mix_out_bwd(
        dx2, o, yc, proj, wm["w_o_attn"], wm["w_o_conv"], wm["w_out"], tm, comm))
    plan.reduce("out", dict(w_o_attn=_shard_cols(dwoa), w_o_conv=_shard_cols(dwoc), w_out=dwout.reshape(N_CHIPS, -1, D)))
    dq, dk, dv, dcq, dck = riding("attn_bwd", lambda comm: _attn_bwd(proj, o, do, lse, cum, cum_t, B, S, tq, comm))
    dcum = dcq + jnp.pad(jnp.transpose(dck, (0, 1, 3, 2)).reshape(T, N_HEADS), ((0, 0), (0, LANES - N_HEADS)))
    dflog, g["b_forget"] = _fgate_bwd(dcum, flog, wm["b_forget"], B, S, ch)
    dcb, dcc, dcx, g["conv_w"] = _conv_bwd(dyc, proj, wm["conv_w"], B, S)
    pieces = _proj_pieces(dq, dk, dv, dcb, dcc, dcx, dgates, dflog)
    dwq, dwk, dwv, dwcb, dwcc, dwcx = _matmuls_tn("mix_dw_a", pieces[:6], h, tm)
    dwga, dwgc, dwf = _matmuls_tn("mix_dw_b", pieces[6:], h, tm)
    dwin_t = jnp.concatenate([dwq, dwk, dwv, dwf[:N_HEADS], dwcb, dwcc, dwcx, dwga, dwgc], axis=0)
    plan.reduce("w_in", {"w_in": dwin_t.reshape(N_CHIPS, -1, D)})
    dx1, g["mix_norm"] = riding("mix_proj_bwd_dx", lambda comm: _mix_proj_bwd_dx(
        dx2, x1, wm["mix_norm"], pieces, wm["w_proj"], wm["w_f"], min(PROJ_DX_TILE, T), comm))
    grad_x, dhg1, dhu1, g["ffn1_norm"], df1 = _ffn_bwd_dx(
        "ffn1_bwd_dx", dx1, x, w1["ffn1_norm"], hg1, hu1, w1["ffn1_gate"], w1["ffn1_up"], w1["ffn1_down"], tm_fwd)[0]
    plan.reduce_small(g, loss)
    plan.reduce_parts("ffn1", dict(zip(("ffn1_gate", "ffn1_up", "ffn1_down"), riding(
        "ffn1_bwd_dw", lambda comm: _ffn_bwd_dw_reduced("ffn1_bwd_dw", n1, df1, hg1, hu1, dhg1, dhu1, tm_fwd, comm)))))
    return loss, grad_x, g


TRANSPOSED = ("ffn1_gate", "ffn1_up", "ffn2_gate", "ffn2_up", "w_in")
NORMS = ("ffn1_norm", "mix_norm", "ffn2_norm", "final_norm")


def _unshard_cols(a):
    return jnp.transpose(a, (1, 0, 2)).reshape(a.shape[1], N_CHIPS * a.shape[2])


def _shard_cols(a):
    return jnp.transpose(a.reshape(a.shape[0], N_CHIPS, a.shape[1] // N_CHIPS), (1, 0, 2))


def _layout_ffn(which):
    def layout(st, small):
        w = {n: st[n] for n in (which + "_gate", which + "_up", which + "_down")}
        w[which + "_norm"] = small[which + "_norm"].reshape(1, -1)
        if which == "ffn2":
            w["final_norm"] = small["final_norm"].reshape(1, -1)
        return w
    return layout


def _layout_mix_in(st, small):
    win_t = st["w_in"].reshape(-1, st["w_in"].shape[2])
    return {
        "w_proj": jnp.concatenate([win_t[:N_FORGET_COL], win_t[N_FORGET_COL + N_HEADS:]], axis=0),
        "w_f": jnp.pad(win_t[N_FORGET_COL:N_FORGET_COL + N_HEADS], ((0, LANES - N_HEADS), (0, 0))),
        "conv_w": _unshard_cols(st["conv_w"]),
        "mix_norm": small["mix_norm"].reshape(1, -1),
        "b_forget": jnp.pad(small["b_forget"].reshape(1, -1), ((0, 0), (0, LANES - N_HEADS))),
    }


def _layout_mix_out(st, small):
    return {"w_o_attn": _unshard_cols(st["w_o_attn"]), "w_o_conv": _unshard_cols(st["w_o_conv"]),
            "w_out": st["w_out"].reshape(-1, st["w_out"].shape[2])}


_LAYOUTS = {"ffn1": _layout_ffn("ffn1"), "mix_in": _layout_mix_in, "mix_out": _layout_mix_out, "ffn2": _layout_ffn("ffn2")}


ANY = pl.BlockSpec(memory_space=pl.ANY)
BIG = ("ffn1_gate", "ffn1_up", "ffn1_down", "w_in", "w_o_attn", "w_o_conv", "w_out",
       "ffn2_gate", "ffn2_up", "ffn2_down")


def _place():
    x, y, c = lax.axis_index("x"), lax.axis_index("y"), lax.axis_index("c")
    others = [(1 - x, y), (x, 1 - y), (1 - x, 1 - y)]
    return x, y, c, others


def _col_halves(cols, c):
    hc = cols // 2
    return pl.ds(pl.multiple_of(c * hc, LANES), hc), pl.ds(pl.multiple_of((1 - c) * hc, LANES), hc)


def _gather_comm(shards, conv_shard=None):
    n = len(shards)
    inputs = list(shards) + ([] if conv_shard is None else [conv_shard])

    def copies(ins, outs, sems):
        send_sems, recv_sems, pass_send, pass_recv = sems[:4]
        x, y, c, others = _place()

        def chip_copy(a, j, chip):
            mine, _ = _col_halves(ins[a].shape[1], c)
            return pltpu.make_async_remote_copy(
                src_ref=ins[a].at[:, mine], dst_ref=outs[a].at[chip, :, mine],
                send_sem=send_sems.at[3 * a + j], recv_sem=recv_sems.at[3 * a + j],
                device_id=(*others[j], c), device_id_type=MESH)

        def pass_copy(a, j, chip, half):
            return pltpu.make_async_remote_copy(
                src_ref=outs[a].at[chip, :, half], dst_ref=outs[a].at[chip, :, half],
                send_sem=pass_send.at[3 * a + j], recv_sem=pass_recv.at[3 * a + j],
                device_id=(x, y, 1 - c), device_id_type=MESH)

        def conv_copy(j, chip):
            return pltpu.make_async_remote_copy(
                src_ref=ins[n], dst_ref=outs[n].at[chip],
                send_sem=sems[4].at[j], recv_sem=sems[5].at[j],
                device_id=(*others[j], c), device_id_type=MESH)

        me = 2 * x + y
        sends = [chip_copy(a, j, me) for a in range(n) for j in range(3)]
        if conv_shard is not None:
            sends += [conv_copy(j, me) for j in range(3)]
        return c, others, sends, chip_copy, pass_copy, conv_copy

    def start(ins, outs, sems):
        for cp in copies(ins, outs, sems)[2]:
            cp.start()

    def finish(ins, outs, sems):
        c, others, sends, chip_copy, pass_copy, conv_copy = copies(ins, outs, sems)
        passed = []
        for a in range(n):
            mine, _ = _col_halves(ins[a].shape[1], c)
            for j, (ox, oy) in enumerate(others):
                chip_copy(a, j, 2 * ox + oy).wait_recv()
                passed.append(pass_copy(a, j, 2 * ox + oy, mine))
                passed[-1].start()
        for a in range(n):
            _, theirs = _col_halves(ins[a].shape[1], c)
            for j, (ox, oy) in enumerate(others):
                pass_copy(a, j, 2 * ox + oy, theirs).wait_recv()
        if conv_shard is not None:
            for j, (ox, oy) in enumerate(others):
                conv_copy(j, 2 * ox + oy).wait_recv()
        for cp in sends + passed:
            cp.wait_send()

    scratch = [pltpu.SemaphoreType.DMA((3 * n,))] * 4
    if conv_shard is not None:
        scratch += [pltpu.SemaphoreType.DMA((3,))] * 2
    return _Comm(inputs, [jax.ShapeDtypeStruct((N_CHIPS,) + s.shape, s.dtype) for s in inputs], scratch, start, finish)


def _fill_own(stacks, shards):
    chip = 2 * lax.axis_index("x") + lax.axis_index("y")
    return [lax.dynamic_update_index_in_dim(st, s, chip, 0) for st, s in zip(stacks, shards)]


def _run_comm(name, comm):
    ci, co = len(comm.inputs), len(comm.out_shape)

    def body(*refs):
        comm.start(refs[:ci], refs[ci:ci + co], refs[ci + co:])
        comm.finish(refs[:ci], refs[ci:ci + co], refs[ci + co:])

    return pl.pallas_call(body, name=name, in_specs=[ANY] * ci, out_specs=[ANY] * co, out_shape=comm.out_shape,
                          scratch_shapes=comm.scratch)(*comm.inputs)


def _sibling_exchange_comm(grads):
    n = len(grads)

    def copies(ins, outs, sems):
        x, y, c, _ = _place()
        return [pltpu.make_async_remote_copy(
            src_ref=ins[a].at[:, :, _col_halves(ins[a].shape[2], c)[1]], dst_ref=outs[a],
            send_sem=sems[0].at[a], recv_sem=sems[1].at[a],
            device_id=(x, y, 1 - c), device_id_type=MESH) for a in range(n)]

    def start(ins, outs, sems):
        for cp in copies(ins, outs, sems):
            cp.start()

    def finish(ins, outs, sems):
        for cp in copies(ins, outs, sems):
            cp.wait()

    half = lambda s: jax.ShapeDtypeStruct((s.shape[0], s.shape[1], s.shape[2] // 2), s.dtype)
    return _Comm(grads, [half(s) for s in grads], [pltpu.SemaphoreType.DMA((n,))] * 2, start, finish)


def _merge_comms(comms):
    def split(refs, count):
        out, at = [], 0
        for cm in comms:
            out.append(refs[at:at + count(cm)])
            at += count(cm)
        return out

    def parts(ins, outs, sems):
        return zip(comms, split(ins, lambda cm: len(cm.inputs)), split(outs, lambda cm: len(cm.out_shape)),
                   split(sems, lambda cm: len(cm.scratch)))

    def start(ins, outs, sems):
        for cm, i, o, s in parts(ins, outs, sems):
            cm.start(i, o, s)

    def finish(ins, outs, sems):
        for cm, i, o, s in parts(ins, outs, sems):
            cm.finish(i, o, s)

    return _Comm(sum([cm.inputs for cm in comms], []), sum([cm.out_shape for cm in comms], []),
                 sum([cm.scratch for cm in comms], []), start, finish)


def _add_halves(name, grads, recvs, core):
    n = len(grads)

    def body(core_ref, *refs):
        for g_ref, r_ref, out_ref in zip(refs[:n], refs[n:2 * n], refs[2 * n:]):
            out_ref[...] = (g_ref[...].astype(F32) + r_ref[...].astype(F32)).astype(BF16)

    half = lambda g: pl.BlockSpec((None, g.shape[1], g.shape[2] // 2), lambda k, core_ref: (k, 0, 0))
    mine = lambda g: pl.BlockSpec((None, g.shape[1], g.shape[2] // 2), lambda k, core_ref: (k, 0, core_ref[0]))
    return pl.pallas_call(
        body, name=name,
        grid_spec=pltpu.PrefetchScalarGridSpec(
            num_scalar_prefetch=1, grid=(N_CHIPS,),
            in_specs=[mine(g) for g in grads] + [half(g) for g in grads],
            out_specs=[half(g) for g in grads]),
        out_shape=[jax.ShapeDtypeStruct(r.shape, BF16) for r in recvs],
        compiler_params=_params(("arbitrary",)),
    )(core, *grads, *recvs)


def _chip_exchange_comm(parts):
    n = len(parts)

    def copies(ins, outs, sems):
        x, y, c, others = _place()
        return [pltpu.make_async_remote_copy(
            src_ref=ins[a].at[2 * ox + oy], dst_ref=outs[a].at[j],
            send_sem=sems[0].at[3 * a + j], recv_sem=sems[1].at[3 * a + j],
            device_id=(ox, oy, c), device_id_type=MESH) for a in range(n) for j, (ox, oy) in enumerate(others)]

    def start(ins, outs, sems):
        for cp in copies(ins, outs, sems):
            cp.start()

    def finish(ins, outs, sems):
        for cp in copies(ins, outs, sems):
            cp.wait()

    return _Comm(parts, [jax.ShapeDtypeStruct((3,) + s.shape[1:], s.dtype) for s in parts],
                 [pltpu.SemaphoreType.DMA((3 * n,))] * 2, start, finish)


HBM = pl.BlockSpec(memory_space=pltpu.HBM)
SEM = pl.BlockSpec(memory_space=pltpu.SEMAPHORE)


def _split_exchange_copies(parts, lands, send_sems, recv_sems):
    x, y, c, others = _place()
    return [pltpu.make_async_remote_copy(
        src_ref=parts[a].at[2 * ox + oy], dst_ref=lands[a].at[j],
        send_sem=send_sems.at[3 * a + j], recv_sem=recv_sems.at[3 * a + j],
        device_id=(ox, oy, c), device_id_type=MESH) for a in range(len(parts)) for j, (ox, oy) in enumerate(others)]


def _exchange_start(name, parts):
    n = len(parts)

    def body(*refs):
        ins, lands = refs[:n], refs[n:2 * n]
        send_sems, recv_sems, token = refs[2 * n], refs[2 * n + 1], refs[-1]
        for cp in _split_exchange_copies(ins, lands, send_sems, recv_sems):
            cp.start()
        token[...] = jnp.zeros_like(token)

    land_shape = [(3,) + p.shape[1:] for p in parts]
    outs = pl.pallas_call(
        body, name=name,
        out_shape=[pltpu.SemaphoreType.DMA((3 * n,)), pltpu.SemaphoreType.DMA((3 * n,))]
        + [pltpu.HBM(p.shape, p.dtype) for p in parts] + [pltpu.HBM(s, p.dtype) for s, p in zip(land_shape, parts)]
        + [jax.ShapeDtypeStruct((8, LANES), F32)],
        in_specs=[HBM] * (2 * n), out_specs=[SEM, SEM] + [HBM] * (2 * n) + [pl.BlockSpec(memory_space=pltpu.VMEM)],
        input_output_aliases={i: 2 + i for i in range(2 * n)},
        compiler_params=pltpu.CompilerParams(has_side_effects=pltpu.SideEffectType.DATAFLOW_SIDE_EFFECTING),
    )(*[pltpu.with_memory_space_constraint(p, pltpu.HBM) for p in parts],
      *[pltpu.with_memory_space_constraint(lax.empty(s, p.dtype), pltpu.HBM) for s, p in zip(land_shape, parts)])
    return outs[0], outs[1], list(outs[2:2 + n]), list(outs[2 + n:2 + 2 * n]), outs[-1]


def _exchange_wait(name, send_sems, recv_sems, parts, lands, after):
    n = len(parts)

    def body(*refs):
        ins, zones = refs[:n], refs[n:2 * n]
        for cp in _split_exchange_copies(ins, zones, refs[2 * n], refs[2 * n + 1]):
            cp.wait_send()
            cp.wait_recv()

    outs = pl.pallas_call(
        body, name=name,
        out_shape=[pltpu.HBM(p.shape, p.dtype) for p in parts] + [pltpu.HBM(z.shape, z.dtype) for z in lands],
        in_specs=[HBM] * (2 * n) + [SEM, SEM] + [ANY] * len(after), out_specs=[HBM] * (2 * n),
        input_output_aliases={i: i for i in range(2 * n)},
        compiler_params=pltpu.CompilerParams(has_side_effects=pltpu.SideEffectType.DATAFLOW_SIDE_EFFECTING),
    )(*parts, *lands, send_sems, recv_sems, *after)
    return list(outs[:n]), list(outs[n:])


def _sum_chips(name, owns, recvs, chip, after):
    n = len(owns)
    hc = owns[0].shape[2]
    assert all(o.shape[2] == hc for o in owns)

    def body(chip_ref, *refs):
        for own_ref, recv_ref, out_ref in zip(refs[:n], refs[n:2 * n], refs[2 * n + 1:]):
            acc = own_ref[...].astype(F32)
            for j in range(3):
                acc = acc + recv_ref[j].astype(F32)
            out_ref[...] = acc

    return pl.pallas_call(
        body, name=name,
        grid_spec=pltpu.PrefetchScalarGridSpec(
            num_scalar_prefetch=1, grid=(hc // LANES,),
            in_specs=[pl.BlockSpec((None, o.shape[1], LANES), lambda i, chip_ref: (chip_ref[0], 0, i)) for o in owns]
            + [pl.BlockSpec((3, o.shape[1], LANES), lambda i, chip_ref: (0, 0, i)) for o in owns]
            + [pl.BlockSpec((8, LANES), lambda i, chip_ref: (0, 0))],
            out_specs=[pl.BlockSpec((o.shape[1], LANES), lambda i, chip_ref: (0, i)) for o in owns]),
        out_shape=[jax.ShapeDtypeStruct((o.shape[1], hc), F32) for o in owns],
        compiler_params=_params(("arbitrary",)),
    )(chip, *owns, *recvs, after)


def _share_halves(name, halves):
    n = len(halves)

    def body(*refs):
        srcs, dsts = refs[:n], refs[n:2 * n]
        send_sems, recv_sems = refs[2 * n:]
        x, y, c, _ = _place()
        copies = [pltpu.make_async_remote_copy(
            src_ref=srcs[a], dst_ref=dsts[a], send_sem=send_sems.at[a], recv_sem=recv_sems.at[a],
            device_id=(x, y, 1 - c), device_id_type=MESH) for a in range(n)]
        for cp in copies:
            cp.start()
        for cp in copies:
            cp.wait()

    return pl.pallas_call(
        body, name=name,
        in_specs=[ANY] * n, out_specs=[ANY] * n,
        out_shape=[jax.ShapeDtypeStruct(s.shape, s.dtype) for s in halves],
        scratch_shapes=[pltpu.SemaphoreType.DMA((n,)), pltpu.SemaphoreType.DMA((n,))],
    )(*halves)


def _small_gather_comm(part):
    def copies(ins, outs, sems):
        x, y, c, _ = _place()
        me = 4 * x + 2 * y + c
        both = []
        for d in range(1, N_DEV):
            px, py, pc = (1 - x if d & 4 else x, 1 - y if d & 2 else y, 1 - c if d & 1 else c)
            send = pltpu.make_async_remote_copy(
                src_ref=ins[0], dst_ref=outs[0].at[me], send_sem=sems[0].at[d - 1], recv_sem=sems[1].at[d - 1],
                device_id=(px, py, pc), device_id_type=MESH)
            recv = pltpu.make_async_remote_copy(
                src_ref=ins[0], dst_ref=outs[0].at[4 * px + 2 * py + pc], send_sem=sems[0].at[d - 1],
                recv_sem=sems[1].at[d - 1], device_id=(px, py, pc), device_id_type=MESH)
            both.append((send, recv))
        return both

    def start(ins, outs, sems):
        for send, _ in copies(ins, outs, sems):
            send.start()

    def finish(ins, outs, sems):
        for send, recv in copies(ins, outs, sems):
            recv.wait_recv()
            send.wait_send()

    return _Comm([part], [jax.ShapeDtypeStruct((N_DEV,) + part.shape, F32)],
                 [pltpu.SemaphoreType.DMA((N_DEV - 1,))] * 2, start, finish)


def _sum_devices(parts):
    def body(p_ref, out_ref):
        acc = p_ref[0]
        for k in range(1, N_DEV):
            acc = acc + p_ref[k]
        out_ref[...] = acc

    return pl.pallas_call(
        body, name="sum_devices", grid=(1,),
        in_specs=[pl.BlockSpec(parts.shape, lambda i: (0, 0, 0))],
        out_specs=pl.BlockSpec(parts.shape[1:], lambda i: (0, 0)),
        out_shape=jax.ShapeDtypeStruct(parts.shape[1:], F32),
        compiler_params=_params(("arbitrary",)),
    )(parts)


def _adam_update(w, g, m, v):
    nm = ADAM_B1 * m + (1.0 - ADAM_B1) * g
    nv = ADAM_B2 * v + (1.0 - ADAM_B2) * (g * g)
    m_hat = nm * (1.0 / (1.0 - ADAM_B1 ** ADAM_STEP))
    v_hat = nv * (1.0 / (1.0 - ADAM_B2 ** ADAM_STEP))
    return -ADAM_LR * (m_hat / (jnp.sqrt(v_hat) + ADAM_EPS) + ADAM_WD * w), nm, nv


def _adamw(name, w, g, m, v):
    def body(w_ref, g_ref, m_ref, v_ref, d_ref, nm_ref, nv_ref):
        d_ref[...], nm_ref[...], nv_ref[...] = _adam_update(w_ref[...], g_ref[...], m_ref[...], v_ref[...])

    spec = pl.BlockSpec(w.shape, lambda i: (0, 0))
    out = jax.ShapeDtypeStruct(w.shape, F32)
    return pl.pallas_call(
        body, name=name, grid=(1,),
        in_specs=[spec] * 4, out_specs=[spec] * 3, out_shape=[out] * 3,
        compiler_params=_params(("arbitrary",)),
    )(w, g, m, v)


def _adamw_halves(name, ws, mines, theirs, ms, vs, core):
    n = len(ws)
    cols = ws[0].shape[1]
    assert all(w.shape[1] == cols for w in ws)
    hc = cols // 2
    tc = LANES if n > 1 else min(256, hc)
    nt = hc // tc

    def body(core_ref, *refs):
        ins, outs = refs[:5 * n], refs[5 * n:]
        for a in range(n):
            w_ref, mine_ref, theirs_ref, m_ref, v_ref = [ins[j * n + a] for j in range(5)]
            g_ref, d_ref, nm_ref, nv_ref = outs[4 * a:4 * a + 4]
            gv = jnp.where(pl.program_id(0) == core_ref[0], mine_ref[...], theirs_ref[...])
            g_ref[...] = gv
            d_ref[...], nm_ref[...], nv_ref[...] = _adam_update(w_ref[...], gv, m_ref[...], v_ref[...])

    whole = lambda w: pl.BlockSpec((w.shape[0], tc), lambda h, i, core_ref: (0, h * nt + i))
    mine_spec = lambda w: pl.BlockSpec((w.shape[0], tc), lambda h, i, core_ref: (0, jnp.where(h == core_ref[0], i, 0)))
    theirs_spec = lambda w: pl.BlockSpec((w.shape[0], tc), lambda h, i, core_ref: (0, jnp.where(h == core_ref[0], 0, i)))
    outs = pl.pallas_call(
        body, name=name,
        grid_spec=pltpu.PrefetchScalarGridSpec(
            num_scalar_prefetch=1, grid=(2, nt),
            in_specs=[whole(w) for w in ws] + [mine_spec(w) for w in ws] + [theirs_spec(w) for w in ws]
            + [whole(w) for w in ws] * 2,
            out_specs=[whole(w) for w in ws for _ in range(4)]),
        out_shape=[jax.ShapeDtypeStruct(w.shape, F32) for w in ws for _ in range(4)],
        compiler_params=_params(("arbitrary", "arbitrary")),
    )(core, *ws, *mines, *theirs, *ms, *vs)
    return [outs[4 * a:4 * a + 4] for a in range(n)]


WEIGHTS = ("ffn1_norm", "ffn1_gate", "ffn1_up", "ffn1_down", "mix_norm", "w_in", "b_forget", "conv_w",
           "w_o_attn", "w_o_conv", "w_out", "ffn2_norm", "ffn2_gate", "ffn2_up", "ffn2_down", "final_norm")
VEC_ROWS = 8


def _pack_small(t, conv_rows):
    conv = t["conv_w"]
    parts = [t[n].reshape(VEC_ROWS, LANES) for n in NORMS]
    parts.append(jnp.pad(conv, ((0, conv_rows - conv.shape[0]), (0, 0))))
    parts.append(jnp.pad(t["b_forget"].reshape(1, N_HEADS), ((0, 7), (0, LANES - N_HEADS))))
    return jnp.concatenate(parts, axis=0)


def _unpack_small(p, conv_rows):
    out = {n: p[VEC_ROWS * i:VEC_ROWS * (i + 1)].reshape(-1) for i, n in enumerate(NORMS)}
    base = VEC_ROWS * len(NORMS)
    out["conv_w"] = p[base:base + 3]
    out["b_forget"] = p[base + conv_rows, :N_HEADS]
    return out


def _travel(name, a):
    return a.T if name in TRANSPOSED else a


GATHER_FIRST = ("ffn1_gate", "ffn1_up")
GATHER_RIDES = {"ffn1_up": ("ffn1_down",), "ffn1_down": ("w_in",), "mix_proj_fwd": ("w_o_attn", "w_o_conv", "w_out"),
                "attn_fwd": ("ffn2_gate", "ffn2_up", "ffn2_down")}
SIBLING_RIDES = {"ffn2": "mix_out_bwd", "out": None, "w_in": "mix_proj_bwd_dx", "ffn1": None}
CHIP_RIDES = {"ffn2": "attn_bwd", "out": "attn_bwd", "w_in": "ffn1_bwd_dw", "ffn1": None}
SMALL_RIDE = "ffn1_bwd_dw"


class _MeshPlan:
    def __init__(self, wts, core):
        self.small, self.core = wts, core
        self.shards = {n: wts[n].astype(BF16) for n in BIG}
        self.chip_part, self.from_chips, self.rides = {}, {}, {}
        self.stacks = {}
        conv_shard = jnp.pad(wts["conv_w"], ((0, 8 - wts["conv_w"].shape[0]), (0, 0)))
        for kernel_name, names in GATHER_RIDES.items():
            mine = [self.shards[n] for n in names]
            conv = conv_shard if kernel_name == "ffn1_up" else None
            names = names + (("conv_w",) if conv is not None else ())
            mine = mine + ([conv] if conv is not None else [])
            self._ride(kernel_name, _gather_comm(mine[:len(mine) - (conv is not None)], conv),
                       lambda got, names=names, mine=mine: self.stacks.update(zip(names, _fill_own(got, mine))))

    def weights(self, group):
        return _LAYOUTS[group](self.stacks, self.small)

    def ffn1_up(self, x, tm):
        px, py = lax.axis_index("x"), lax.axis_index("y")
        order = jnp.stack([2 * px + py, 2 * (1 - px) + py, 2 * px + (1 - py), 2 * (1 - px) + (1 - py)]).astype(jnp.int32)
        own = [self.shards[n] for n in GATHER_FIRST]
        (hg, hu, n, sg, su), brought = _ffn_up_gather("ffn1_up", x, self.small["ffn1_norm"].reshape(1, -1), *own, order,
                                                     tm, self.rider("ffn1_up"))
        self.stacks.update(zip(GATHER_FIRST, _fill_own([sg, su], own)))
        self.arrived("ffn1_up", brought)
        return hg, hu, n

    def _ride(self, kernel_name, comm, then):
        self.rides.setdefault(kernel_name, []).append((comm, then))

    def rider(self, kernel_name):
        comms = [comm for comm, _ in self.rides.get(kernel_name, [])]
        return _merge_comms(comms) if comms else None

    def arrived(self, kernel_name, results):
        for comm, then in self.rides.pop(kernel_name, []):
            then(results[:len(comm.out_shape)])
            results = results[len(comm.out_shape):]

    def reduce(self, group, grads):
        names = tuple(grads)
        mine = [grads[n] for n in names]

        def with_sibling(from_sibling):
            parts = _add_halves("add_halves_" + group, mine, list(from_sibling), self.core)
            self.chip_part.update(zip(names, parts))
            if CHIP_RIDES[group] is None:
                self.last = (names, _exchange_start("exchange_start_" + group, parts))
            else:
                self._ride(CHIP_RIDES[group], _chip_exchange_comm(parts),
                           lambda got: self.from_chips.update(zip(names, got)))

        if SIBLING_RIDES[group] is None:
            with_sibling(_run_comm("sibling_exchange_" + group, _sibling_exchange_comm(mine)))
        else:
            self._ride(SIBLING_RIDES[group], _sibling_exchange_comm(mine), with_sibling)

    def reduce_parts(self, group, parts):
        self.chip_part.update(parts)
        names = tuple(parts)
        if CHIP_RIDES[group] is None:
            self.last = (names, _exchange_start("exchange_start_" + group, list(parts.values())))
        else:
            self._ride(CHIP_RIDES[group], _chip_exchange_comm(list(parts.values())),
                       lambda got: self.from_chips.update(zip(names, got)))

    def reduce_small(self, gs, loss):
        conv_all = _shard_cols(gs["conv_w"]).reshape(N_CHIPS * 8, LANES)
        part = _pack_small({**{n: gs[n] for n in NORMS}, "conv_w": conv_all, "b_forget": gs["b_forget"][0, :N_HEADS]},
                           N_CHIPS * 8)
        part = jnp.concatenate([part, jnp.broadcast_to(loss, (8, LANES))], axis=0)
        me = 4 * lax.axis_index("x") + 2 * lax.axis_index("y") + lax.axis_index("c")

        def landed(got):
            self.small_parts = lax.dynamic_update_index_in_dim(got[0], part, me, 0)

        self._ride(SMALL_RIDE, _small_gather_comm(part), landed)


def kernel(x, ffn1_norm, ffn1_gate, ffn1_up, ffn1_down, mix_norm, w_in, b_forget, conv_w, w_o_attn, w_o_conv, w_out, ffn2_norm, ffn2_gate, ffn2_up, ffn2_down, final_norm, loss_target, m_ffn1_norm, m_ffn1_gate, m_ffn1_up, m_ffn1_down, m_mix_norm, m_w_in, m_b_forget, m_conv_w, m_w_o_attn, m_w_o_conv, m_w_out, m_ffn2_norm, m_ffn2_gate, m_ffn2_up, m_ffn2_down, m_final_norm, v_ffn1_norm, v_ffn1_gate, v_ffn1_up, v_ffn1_down, v_mix_norm, v_w_in, v_b_forget, v_conv_w, v_w_o_attn, v_w_o_conv, v_w_out, v_ffn2_norm, v_ffn2_gate, v_ffn2_up, v_ffn2_down, v_final_norm):
    given = dict(locals())
    wts = {n: _travel(n, given[n]) for n in WEIGHTS}
    mom = {n: _travel(n, given["m_" + n]) for n in WEIGHTS}
    var = {n: _travel(n, given["v_" + n]) for n in WEIGHTS}
    B, S, D = x.shape
    chip = 2 * lax.axis_index("x") + lax.axis_index("y")
    chip1 = chip.astype(jnp.int32).reshape(1)
    core = lax.axis_index("c").astype(jnp.int32).reshape(1)

    plan = _MeshPlan(wts, core)
    loss, grad_x, gs = _local_step(x.reshape(B * S, D), loss_target.reshape(B * S, D), plan, B, S)

    last_names, (send_sems, recv_sems, parts_thru, lands, token) = plan.last
    delta, new_m, new_v, grads = {}, {}, {}, {}

    def finish(tag, names):
        by_cols = {}
        for n in names:
            by_cols.setdefault(wts[n].shape[1], []).append(n)
        mine = {}
        for cols, ns in by_cols.items():
            mine.update(zip(ns, _sum_chips("sum_chips_%s_%d" % (tag, cols), [plan.chip_part[n] for n in ns],
                                           [plan.from_chips[n] for n in ns], chip1, token)))
        theirs = dict(zip(names, _share_halves("share_halves_" + tag, [mine[n] for n in names])))
        raw = []
        for cols, ns in by_cols.items():
            outs = _adamw_halves("adamw_%s_%d" % (tag, cols), [wts[n] for n in ns], [mine[n] for n in ns],
                                 [theirs[n] for n in ns], [mom[n] for n in ns], [var[n] for n in ns], core)
            for n, per in zip(ns, outs):
                raw.append(per[-1])
                grads[n], delta[n], new_m[n], new_v[n] = [_travel(n, o) for o in per]
        return raw

    small_sum = _sum_devices(plan.small_parts)
    base = VEC_ROWS * len(NORMS)
    loss_row = small_sum.shape[0] - 8
    small_grads = _unpack_small(small_sum, N_CHIPS * 8)
    small_grads["conv_w"] = lax.dynamic_slice_in_dim(small_sum[base:base + N_CHIPS * 8], chip * 8, 8, axis=0)[:3]
    packs = [_pack_small(t, 8) for t in (wts, small_grads, mom, var)]
    small_out = _adamw("adamw_small", *packs)

    done = finish("early", [n for n in BIG if n not in last_names])
    parts_back, got = _exchange_wait("exchange_wait", send_sems, recv_sems, parts_thru, lands, done + list(small_out))
    plan.chip_part.update(zip(last_names, parts_back))
    plan.from_chips.update(zip(last_names, got))
    finish("last", last_names)
    grads.update(small_grads)
    for out, p in zip((delta, new_m, new_v), small_out):
        out.update(_unpack_small(p, 8))

    return (small_sum[loss_row, 0], grad_x.reshape(B, S, D), *[grads[n] for n in WEIGHTS], *[delta[n] for n in WEIGHTS],
            *[new_m[n] for n in WEIGHTS], *[new_v[n] for n in WEIGHTS])
```
